```python
import jax, jax.numpy as jnp
from jax import lax
import numpy as np

D_MODEL = 1024
BATCH = 8
SEQ = 4096
DEPTH = 2

N_MIXERS = 2
N_ATTN_LAYERS = (DEPTH + 1) // 2
N_HGRN_LAYERS = DEPTH // 2
ATTN_HEAD_DIM = 64
ATTN_HEADS = D_MODEL // ATTN_HEAD_DIM
DILATED_PATTERNS = ((128, 1), (512, 4), (2048, 16))
N_GROUPS = len(DILATED_PATTERNS)
ROPE_THETA = 10000.0
HGRN_EXPAND = 128
HGRN_HEADS = D_MODEL // HGRN_EXPAND
HGRN_DK = HGRN_EXPAND
HGRN_DV = D_MODEL // HGRN_HEADS
HGRN_CHUNK = 64
D_FF = 4 * D_MODEL
LN_EPS = 1e-5
RMS_EPS = 1e-6
DEEPNORM_ALPHA = (2 * DEPTH) ** 0.25
DEEPNORM_BETA = (8 * DEPTH) ** -0.25

kernel_name = 'hybrid_dilated_attn_hgrn2_deepnorm'

F32 = jnp.float32


def layer_norm(x, g, b):
    xf = x.astype(F32)
    mu = jnp.mean(xf, axis=-1, keepdims=True)
    var = jnp.mean(jnp.square(xf - mu), axis=-1, keepdims=True)
    return ((xf - mu) * lax.rsqrt(var + LN_EPS) * g.astype(F32) + b.astype(F32)).astype(x.dtype)


def rotary(x, pos):
    e = x.shape[-1]
    half = e // 2
    inv = ROPE_THETA ** (-jnp.arange(half, dtype=F32) * (2.0 / e))
    ang = pos.astype(F32)[:, None] * inv[None, :]
    cos = jnp.cos(ang)[None, :, None, :]
    sin = jnp.sin(ang)[None, :, None, :]
    xf = x.astype(F32)
    x1, x2 = xf[..., :half], xf[..., half:]
    return jnp.concatenate([x1 * cos - x2 * sin, x2 * cos + x1 * sin], axis=-1).astype(x.dtype)


def dilated_window_attention(q, k, v, window, dilation):
    B, S, H, E = q.shape
    blk = window // dilation
    span = dilation * blk
    s_pad = -(-S // span) * span
    n = s_pad // dilation
    nb = n // blk
    pad = ((0, 0), (0, s_pad - S), (0, 0), (0, 0))

    def to_blocks(t):
        t = jnp.pad(t, pad).reshape(B, n, dilation, H, E).transpose(0, 2, 1, 3, 4)
        return t.reshape(B * dilation, nb, blk, H, E)

    def with_prev(t):
        prev = jnp.pad(t, ((0, 0), (1, 0), (0, 0), (0, 0), (0, 0)))[:, :-1]
        return jnp.concatenate([prev, t], axis=2)

    qb, kb, vb = to_blocks(q), to_blocks(k), to_blocks(v)
    kk, vv = with_prev(kb), with_prev(vb)
    s = jnp.einsum('znqhe,znkhe->znhqk', qb, kk).astype(F32) * (E ** -0.5)
    qi = jnp.arange(blk)[:, None]
    kj = jnp.arange(2 * blk)[None, :]
    dist = qi + blk - kj
    in_band = (dist >= 0) & (dist <= blk)
    kabs = jnp.arange(nb)[:, None, None] * blk + kj[None] - blk
    valid = in_band[None] & (kabs >= 0)
    s = jnp.where(valid[None, :, None], s, -jnp.inf)
    m = jnp.max(s, axis=-1, keepdims=True)
    p = jnp.exp(s - m)
    l = jnp.sum(p, axis=-1, keepdims=True)
    o = jnp.einsum('znhqk,znkhe->znqhe', (p / l).astype(v.dtype), vv)
    lse = (m + jnp.log(l))[..., 0].transpose(0, 1, 3, 2)

    def from_blocks(t):
        t = t.reshape(B, dilation, n, *t.shape[3:]).swapaxes(1, 2)
        return t.reshape(B, s_pad, *t.shape[3:])[:, :S]

    return from_blocks(o), from_blocks(lse)


def dilated_attention_mixer(x, w_in, w_out):
    B, S, _ = x.shape
    proj = (x @ w_in).reshape(B, S, N_GROUPS, 3, ATTN_HEADS, ATTN_HEAD_DIM)
    pos = jnp.arange(S)
    outs, lses = [], []
    for g, (window, dilation) in enumerate(DILATED_PATTERNS):
        q = rotary(proj[:, :, g, 0], pos)
        k = rotary(proj[:, :, g, 1], pos)
        o, lse = dilated_window_attention(q, k, proj[:, :, g, 2], window, dilation)
        outs.append(o)
        lses.append(lse)
    wts = jax.nn.softmax(jnp.stack(lses, axis=0), axis=0)
    o = jnp.einsum('gbsh,gbshe->bshe', wts, jnp.stack(outs, axis=0).astype(F32))
    return o.reshape(B, S, ATTN_HEADS * ATTN_HEAD_DIM).astype(x.dtype) @ w_out


def forget_lower_bounds(lb_logits):
    c = jnp.cumsum(jax.nn.softmax(lb_logits.astype(F32), axis=0), axis=0)
    return c - c[0]


def hgrn2_mixer(x, w_in, w_out, norm_g, lb):
    B, S, _ = x.shape
    H, K, V, C = HGRN_HEADS, HGRN_DK, HGRN_DV, HGRN_CHUNK
    nc = S // C
    q_raw, f_raw, i_raw = jnp.split(x @ w_in, [H * K, 2 * H * K], axis=-1)
    z = f_raw.astype(F32)
    log_f = jnp.logaddexp(jnp.log(lb), jnp.log1p(-lb) + jax.nn.log_sigmoid(z))
    key = (1.0 - lb) * jax.nn.sigmoid(-z)
    q = jax.nn.silu(q_raw.astype(F32))
    v = i_raw.astype(F32)

    def chunks(t, d):
        return t.reshape(B, nc, C, H, d).transpose(0, 3, 1, 2, 4)

    q, key, log_f, v = chunks(q, K), chunks(key, K), chunks(log_f, K), chunks(v, V)
    b = jnp.cumsum(log_f, axis=3)
    q_dec = q * jnp.exp(b)
    k_dec = key * jnp.exp(-b)
    causal = jnp.tril(jnp.ones((C, C), dtype=bool))
    a = jnp.where(causal, jnp.einsum('bhcid,bhcjd->bhcij', q_dec, k_dec), 0.0)
    o_intra = jnp.einsum('bhcij,bhcje->bhcie', a, v)
    b_last = b[:, :, :, -1:, :]
    kv = jnp.einsum('bhcjd,bhcje->bhcde', key * jnp.exp(b_last - b), v)
    chunk_decay = jnp.exp(b_last[:, :, :, 0, :])

    def step(state, inp):
        dec, kv_c = inp
        return dec[..., None] * state + kv_c, state

    s0 = jnp.zeros((B, H, K, V), F32)
    _, states = lax.scan(step, s0, (jnp.moveaxis(chunk_decay, 2, 0), jnp.moveaxis(kv, 2, 0)))
    o_inter = jnp.einsum('bhcid,cbhde->bhcie', q_dec, states)
    o = (o_intra + o_inter).transpose(0, 2, 3, 1, 4).reshape(B, S, H, V)
    o = o * lax.rsqrt(jnp.mean(o * o, axis=-1, keepdims=True) + RMS_EPS) * norm_g.astype(F32).reshape(H, V)
    return o.reshape(B, S, H * V).astype(x.dtype) @ w_out


def squared_relu_mlp(x, w_up, w_down):
    return jnp.square(jax.nn.relu(x @ w_up)) @ w_down


def _fwd_setup_inputs(seed: int = 0) -> dict:
    key = jax.random.key(seed)
    ks = jax.random.split(key, 13)
    d_attn_in = N_GROUPS * 3 * ATTN_HEADS * ATTN_HEAD_DIM
    d_attn_out = ATTN_HEADS * ATTN_HEAD_DIM
    d_hgrn_in = 2 * HGRN_HEADS * HGRN_DK + HGRN_HEADS * HGRN_DV
    d_hgrn_out = HGRN_HEADS * HGRN_DV
    nrm = lambda k, shape, scale: jax.random.normal(k, shape, F32) * scale
    return {
        'x': nrm(ks[0], (BATCH, SEQ, D_MODEL), 1.0),
        'attn_w_in': nrm(ks[1], (N_ATTN_LAYERS, D_MODEL, d_attn_in), D_MODEL ** -0.5),
        'attn_w_out': nrm(ks[2], (N_ATTN_LAYERS, d_attn_out, D_MODEL), d_attn_out ** -0.5 * DEEPNORM_BETA),
        'hgrn_w_in': nrm(ks[3], (N_HGRN_LAYERS, D_MODEL, d_hgrn_in), D_MODEL ** -0.5),
        'hgrn_w_out': nrm(ks[4], (N_HGRN_LAYERS, d_hgrn_out, D_MODEL), d_hgrn_out ** -0.5 * DEEPNORM_BETA),
        'hgrn_norm_g': 1.0 + nrm(ks[5], (N_HGRN_LAYERS, d_hgrn_out), 0.02),
        'lb_logits': nrm(ks[6], (DEPTH, HGRN_HEADS * HGRN_DK), 0.1),
        'ln_mix_g': 1.0 + nrm(ks[7], (DEPTH, D_MODEL), 0.02),
        'ln_mix_b': nrm(ks[8], (DEPTH, D_MODEL), 0.02),
        'ln_ffn_g': 1.0 + nrm(ks[9], (DEPTH, D_MODEL), 0.02),
        'ln_ffn_b': nrm(ks[10], (DEPTH, D_MODEL), 0.02),
        'ffn_w_up': nrm(ks[11], (DEPTH, D_MODEL, D_FF), D_MODEL ** -0.5),
        'ffn_w_down': nrm(ks[12], (DEPTH, D_FF, D_MODEL), D_FF ** -0.5 * DEEPNORM_BETA),
    }


def _fwd_reference(x, attn_w_in, attn_w_out, hgrn_w_in, hgrn_w_out, hgrn_norm_g, lb_logits,
              ln_mix_g, ln_mix_b, ln_ffn_g, ln_ffn_b, ffn_w_up, ffn_w_down):
    lbs = forget_lower_bounds(lb_logits)
    for i in range(DEPTH):
        j = i // N_MIXERS
        if i % N_MIXERS == 0:
            y = dilated_attention_mixer(x, attn_w_in[j], attn_w_out[j])
        else:
            y = hgrn2_mixer(x, hgrn_w_in[j], hgrn_w_out[j], hgrn_norm_g[j], lbs[i])
        x = layer_norm(DEEPNORM_ALPHA * x + y, ln_mix_g[i], ln_mix_b[i])
        y = squared_relu_mlp(x, ffn_w_up[i], ffn_w_down[i])
        x = layer_norm(DEEPNORM_ALPHA * x + y, ln_ffn_g[i], ln_ffn_b[i])
    return x


import jax as _jax
import jax.numpy as _jnp

TWIN_FORMAT = 'train_step'
FWD_PARAMS = ['x', 'attn_w_in', 'attn_w_out', 'hgrn_w_in', 'hgrn_w_out', 'hgrn_norm_g', 'lb_logits', 'ln_mix_g', 'ln_mix_b', 'ln_ffn_g', 'ln_ffn_b', 'ffn_w_up', 'ffn_w_down']
TWIN_WEIGHTS = ['attn_w_in', 'attn_w_out', 'hgrn_w_in', 'hgrn_w_out', 'hgrn_norm_g', 'lb_logits', 'ln_mix_g', 'ln_mix_b', 'ln_ffn_g', 'ln_ffn_b', 'ffn_w_up', 'ffn_w_down']
TWIN_DIFF_INPUT = 'x'
TWIN_INPUTS = ['x', 'attn_w_in', 'attn_w_out', 'hgrn_w_in', 'hgrn_w_out', 'hgrn_norm_g', 'lb_logits', 'ln_mix_g', 'ln_mix_b', 'ln_ffn_g', 'ln_ffn_b', 'ffn_w_up', 'ffn_w_down', 'loss_target', 'm_attn_w_in', 'm_attn_w_out', 'm_hgrn_w_in', 'm_hgrn_w_out', 'm_hgrn_norm_g', 'm_lb_logits', 'm_ln_mix_g', 'm_ln_mix_b', 'm_ln_ffn_g', 'm_ln_ffn_b', 'm_ffn_w_up', 'm_ffn_w_down', 'v_attn_w_in', 'v_attn_w_out', 'v_hgrn_w_in', 'v_hgrn_w_out', 'v_hgrn_norm_g', 'v_lb_logits', 'v_ln_mix_g', 'v_ln_mix_b', 'v_ln_ffn_g', 'v_ln_ffn_b', 'v_ffn_w_up', 'v_ffn_w_down']
TWIN_OUTPUTS = ['loss', 'grad_x', 'grad_attn_w_in', 'grad_attn_w_out', 'grad_hgrn_w_in', 'grad_hgrn_w_out', 'grad_hgrn_norm_g', 'grad_lb_logits', 'grad_ln_mix_g', 'grad_ln_mix_b', 'grad_ln_ffn_g', 'grad_ln_ffn_b', 'grad_ffn_w_up', 'grad_ffn_w_down', 'delta_attn_w_in', 'delta_attn_w_out', 'delta_hgrn_w_in', 'delta_hgrn_w_out', 'delta_hgrn_norm_g', 'delta_lb_logits', 'delta_ln_mix_g', 'delta_ln_mix_b', 'delta_ln_ffn_g', 'delta_ln_ffn_b', 'delta_ffn_w_up', 'delta_ffn_w_down', 'new_m_attn_w_in', 'new_m_attn_w_out', 'new_m_hgrn_w_in', 'new_m_hgrn_w_out', 'new_m_hgrn_norm_g', 'new_m_lb_logits', 'new_m_ln_mix_g', 'new_m_ln_mix_b', 'new_m_ln_ffn_g', 'new_m_ln_ffn_b', 'new_m_ffn_w_up', 'new_m_ffn_w_down', 'new_v_attn_w_in', 'new_v_attn_w_out', 'new_v_hgrn_w_in', 'new_v_hgrn_w_out', 'new_v_hgrn_norm_g', 'new_v_lb_logits', 'new_v_ln_mix_g', 'new_v_ln_mix_b', 'new_v_ln_ffn_g', 'new_v_ln_ffn_b', 'new_v_ffn_w_up', 'new_v_ffn_w_down']
TWIN_LEAF_KINDS = {'loss': 'loss', 'grad_x': 'grad_x', 'grad_attn_w_in': 'grad_w', 'grad_attn_w_out': 'grad_w', 'grad_hgrn_w_in': 'grad_w', 'grad_hgrn_w_out': 'grad_w', 'grad_hgrn_norm_g': 'grad_w', 'grad_lb_logits': 'grad_w', 'grad_ln_mix_g': 'grad_w', 'grad_ln_mix_b': 'grad_w', 'grad_ln_ffn_g': 'grad_w', 'grad_ln_ffn_b': 'grad_w', 'grad_ffn_w_up': 'grad_w', 'grad_ffn_w_down': 'grad_w', 'delta_attn_w_in': 'delta_w', 'delta_attn_w_out': 'delta_w', 'delta_hgrn_w_in': 'delta_w', 'delta_hgrn_w_out': 'delta_w', 'delta_hgrn_norm_g': 'delta_w', 'delta_lb_logits': 'delta_w', 'delta_ln_mix_g': 'delta_w', 'delta_ln_mix_b': 'delta_w', 'delta_ln_ffn_g': 'delta_w', 'delta_ln_ffn_b': 'delta_w', 'delta_ffn_w_up': 'delta_w', 'delta_ffn_w_down': 'delta_w', 'new_m_attn_w_in': 'new_m', 'new_m_attn_w_out': 'new_m', 'new_m_hgrn_w_in': 'new_m', 'new_m_hgrn_w_out': 'new_m', 'new_m_hgrn_norm_g': 'new_m', 'new_m_lb_logits': 'new_m', 'new_m_ln_mix_g': 'new_m', 'new_m_ln_mix_b': 'new_m', 'new_m_ln_ffn_g': 'new_m', 'new_m_ln_ffn_b': 'new_m', 'new_m_ffn_w_up': 'new_m', 'new_m_ffn_w_down': 'new_m', 'new_v_attn_w_in': 'new_v', 'new_v_attn_w_out': 'new_v', 'new_v_hgrn_w_in': 'new_v', 'new_v_hgrn_w_out': 'new_v', 'new_v_hgrn_norm_g': 'new_v', 'new_v_lb_logits': 'new_v', 'new_v_ln_mix_g': 'new_v', 'new_v_ln_mix_b': 'new_v', 'new_v_ln_ffn_g': 'new_v', 'new_v_ln_ffn_b': 'new_v', 'new_v_ffn_w_up': 'new_v', 'new_v_ffn_w_down': 'new_v'}


def _forward(args):
    return _fwd_reference(*[args[k] for k in FWD_PARAMS])


def _output_shape():
    def fwd():
        inp = _fwd_setup_inputs(0)
        return _fwd_reference(*[inp[k] for k in FWD_PARAMS])
    out = _jax.eval_shape(fwd)
    return out.shape, out.dtype

N_MICROBATCH = 1
ADAM_LR = 0.001
ADAM_B1 = 0.9
ADAM_B2 = 0.999
ADAM_EPS = 1e-08
ADAM_WD = 0.01
ADAM_STEP = 10
PER_EXAMPLE_BATCH_AXIS = {'x': 0, 'loss_target': 0}
SHARED_INPUTS = []
_WEIGHT_DTYPES = {'attn_w_in': _jnp.float32, 'attn_w_out': _jnp.float32, 'hgrn_w_in': _jnp.float32, 'hgrn_w_out': _jnp.float32, 'hgrn_norm_g': _jnp.float32, 'lb_logits': _jnp.float32, 'ln_mix_g': _jnp.float32, 'ln_mix_b': _jnp.float32, 'ln_ffn_g': _jnp.float32, 'ln_ffn_b': _jnp.float32, 'ffn_w_up': _jnp.float32, 'ffn_w_down': _jnp.float32}
MOMENT_SCALE = {'attn_w_in': 6.762771e-03, 'attn_w_out': 2.628513e-02, 'hgrn_w_in': 5.308214e-02, 'hgrn_w_out': 1.825000e-01, 'hgrn_norm_g': 8.858374e-02, 'lb_logits': 5.614045e-03, 'ln_mix_g': 9.226381e-01, 'ln_mix_b': 6.875652e-01, 'ln_ffn_g': 2.274759e+01, 'ln_ffn_b': 5.865132e+00, 'ffn_w_up': 4.244252e-02, 'ffn_w_down': 2.445103e-01}


def _to_microbatches(a, axis):
    t = _jnp.moveaxis(a, axis, 0)
    t = t.reshape((N_MICROBATCH, t.shape[0] // N_MICROBATCH) + t.shape[1:])
    return _jnp.moveaxis(t, 1, axis + 1)


def setup_inputs(seed: int = 0) -> dict:
    inp = _fwd_setup_inputs(seed)
    key = _jax.random.fold_in(_jax.random.key(seed), 7919)
    shape, _ = _output_shape()
    out = dict(inp)
    out["loss_target"] = _jax.random.normal(_jax.random.fold_in(key, 0), shape, _jnp.float32)
    for i, name in enumerate(TWIN_WEIGHTS):
        w = inp[name].astype(_jnp.float32)
        if MOMENT_SCALE is None:
            s = _jnp.sqrt(_jnp.mean(_jnp.square(w)) + 1e-30)
        else:
            s = MOMENT_SCALE[name]
        km, kv = _jax.random.split(_jax.random.fold_in(key, i + 1))
        out[name] = w
        out["m_" + name] = s * _jax.random.normal(km, w.shape, _jnp.float32)
        out["v_" + name] = (s * s) * _jax.random.uniform(kv, w.shape, _jnp.float32, 0.5, 1.5)
    if N_MICROBATCH > 1:
        for name, axis in PER_EXAMPLE_BATCH_AXIS.items():
            out[name] = _to_microbatches(out[name], axis)
    return {'x': out['x'], 'attn_w_in': out['attn_w_in'], 'attn_w_out': out['attn_w_out'], 'hgrn_w_in': out['hgrn_w_in'], 'hgrn_w_out': out['hgrn_w_out'], 'hgrn_norm_g': out['hgrn_norm_g'], 'lb_logits': out['lb_logits'], 'ln_mix_g': out['ln_mix_g'], 'ln_mix_b': out['ln_mix_b'], 'ln_ffn_g': out['ln_ffn_g'], 'ln_ffn_b': out['ln_ffn_b'], 'ffn_w_up': out['ffn_w_up'], 'ffn_w_down': out['ffn_w_down'], 'loss_target': out['loss_target'], 'm_attn_w_in': out['m_attn_w_in'], 'm_attn_w_out': out['m_attn_w_out'], 'm_hgrn_w_in': out['m_hgrn_w_in'], 'm_hgrn_w_out': out['m_hgrn_w_out'], 'm_hgrn_norm_g': out['m_hgrn_norm_g'], 'm_lb_logits': out['m_lb_logits'], 'm_ln_mix_g': out['m_ln_mix_g'], 'm_ln_mix_b': out['m_ln_mix_b'], 'm_ln_ffn_g': out['m_ln_ffn_g'], 'm_ln_ffn_b': out['m_ln_ffn_b'], 'm_ffn_w_up': out['m_ffn_w_up'], 'm_ffn_w_down': out['m_ffn_w_down'], 'v_attn_w_in': out['v_attn_w_in'], 'v_attn_w_out': out['v_attn_w_out'], 'v_hgrn_w_in': out['v_hgrn_w_in'], 'v_hgrn_w_out': out['v_hgrn_w_out'], 'v_hgrn_norm_g': out['v_hgrn_norm_g'], 'v_lb_logits': out['v_lb_logits'], 'v_ln_mix_g': out['v_ln_mix_g'], 'v_ln_mix_b': out['v_ln_mix_b'], 'v_ln_ffn_g': out['v_ln_ffn_g'], 'v_ln_ffn_b': out['v_ln_ffn_b'], 'v_ffn_w_up': out['v_ffn_w_up'], 'v_ffn_w_down': out['v_ffn_w_down']}


def _loss(weights, diff, rest, loss_target):
    with _jax.named_scope("forward"):
        args = {**rest, TWIN_DIFF_INPUT: diff, **{k: w.astype(_WEIGHT_DTYPES[k]) for k, w in weights.items()}}
        y = _forward(args)
    with _jax.named_scope("loss_head"):
        err = _jnp.square(y.astype(_jnp.float32) - loss_target)
        return 0.5 * _jnp.sum(_jnp.mean(err, axis=-1)) if err.ndim else 0.5 * err


def _adamw(w, g, m, v):
    m = ADAM_B1 * m + (1.0 - ADAM_B1) * g
    v = ADAM_B2 * v + (1.0 - ADAM_B2) * _jnp.square(g)
    m_hat = m / (1.0 - ADAM_B1 ** ADAM_STEP)
    v_hat = v / (1.0 - ADAM_B2 ** ADAM_STEP)
    delta = -ADAM_LR * (m_hat / (_jnp.sqrt(v_hat) + ADAM_EPS) + ADAM_WD * w)
    return delta, m, v


def reference(x, attn_w_in, attn_w_out, hgrn_w_in, hgrn_w_out, hgrn_norm_g, lb_logits, ln_mix_g, ln_mix_b, ln_ffn_g, ln_ffn_b, ffn_w_up, ffn_w_down, loss_target, m_attn_w_in, m_attn_w_out, m_hgrn_w_in, m_hgrn_w_out, m_hgrn_norm_g, m_lb_logits, m_ln_mix_g, m_ln_mix_b, m_ln_ffn_g, m_ln_ffn_b, m_ffn_w_up, m_ffn_w_down, v_attn_w_in, v_attn_w_out, v_hgrn_w_in, v_hgrn_w_out, v_hgrn_norm_g, v_lb_logits, v_ln_mix_g, v_ln_mix_b, v_ln_ffn_g, v_ln_ffn_b, v_ffn_w_up, v_ffn_w_down):
    given = dict(x=x, attn_w_in=attn_w_in, attn_w_out=attn_w_out, hgrn_w_in=hgrn_w_in, hgrn_w_out=hgrn_w_out, hgrn_norm_g=hgrn_norm_g, lb_logits=lb_logits, ln_mix_g=ln_mix_g, ln_mix_b=ln_mix_b, ln_ffn_g=ln_ffn_g, ln_ffn_b=ln_ffn_b, ffn_w_up=ffn_w_up, ffn_w_down=ffn_w_down, loss_target=loss_target, m_attn_w_in=m_attn_w_in, m_attn_w_out=m_attn_w_out, m_hgrn_w_in=m_hgrn_w_in, m_hgrn_w_out=m_hgrn_w_out, m_hgrn_norm_g=m_hgrn_norm_g, m_lb_logits=m_lb_logits, m_ln_mix_g=m_ln_mix_g, m_ln_mix_b=m_ln_mix_b, m_ln_ffn_g=m_ln_ffn_g, m_ln_ffn_b=m_ln_ffn_b, m_ffn_w_up=m_ffn_w_up, m_ffn_w_down=m_ffn_w_down, v_attn_w_in=v_attn_w_in, v_attn_w_out=v_attn_w_out, v_hgrn_w_in=v_hgrn_w_in, v_hgrn_w_out=v_hgrn_w_out, v_hgrn_norm_g=v_hgrn_norm_g, v_lb_logits=v_lb_logits, v_ln_mix_g=v_ln_mix_g, v_ln_mix_b=v_ln_mix_b, v_ln_ffn_g=v_ln_ffn_g, v_ln_ffn_b=v_ln_ffn_b, v_ffn_w_up=v_ffn_w_up, v_ffn_w_down=v_ffn_w_down)
    weights = {n: given[n] for n in TWIN_WEIGHTS}
    shared = {n: given[n] for n in SHARED_INPUTS}
    per_example = {n: given[n] for n in ['x']}
    grad_fn = _jax.value_and_grad(_loss, argnums=(0, 1))

    def one_microbatch(ex, loss_target):
        ex = dict(ex)
        diff = ex.pop(TWIN_DIFF_INPUT)
        return grad_fn(weights, diff, {**shared, **ex}, loss_target)

    if N_MICROBATCH == 1:
        loss, (grad_w, grad_x) = one_microbatch(per_example, given["loss_target"])
    else:
        def body(carry, xs):
            loss_sum, grad_sum = carry
            l_k, (gw_k, gx_k) = one_microbatch(xs[0], xs[1])
            with _jax.named_scope("update"):
                return (loss_sum + l_k, _jax.tree.map(_jnp.add, grad_sum, gw_k)), gx_k

        init = (_jnp.zeros((), _jnp.float32), _jax.tree.map(_jnp.zeros_like, weights))
        (loss, grad_w), grad_x = _jax.lax.scan(body, init, (per_example, given["loss_target"]))
    with _jax.named_scope("update"):
        delta_w, new_m, new_v = {}, {}, {}
        for n in TWIN_WEIGHTS:
            delta_w[n], new_m[n], new_v[n] = _adamw(weights[n], grad_w[n], given["m_" + n], given["v_" + n])
    return (loss, grad_x, *[grad_w[n] for n in TWIN_WEIGHTS], *[delta_w[n] for n in TWIN_WEIGHTS],
            *[new_m[n] for n in TWIN_WEIGHTS], *[new_v[n] for n in TWIN_WEIGHTS])
```

```python
import functools
import math

import jax
import jax.numpy as jnp
from jax import lax
from jax.experimental import pallas as pl
from jax.experimental.pallas import tpu as pltpu

F32 = jnp.float32
BF16 = jnp.bfloat16

N_DEV = 8
LANES = 128
D_MODEL = 1024
ATTN_HEAD_DIM = 64
ATTN_HEADS = 16
ATTN_BLK = 128
DILATIONS = (1, 4, 16)
ROPE_THETA = 10000.0
HGRN_HEADS = 8
HGRN_CHUNK = 64
D_FF = 4096
LN_EPS = 1e-5
RMS_EPS = 1e-6
DEPTH = 2
ALPHA = (2 * DEPTH) ** 0.25
ADAM_LR, ADAM_B1, ADAM_B2, ADAM_EPS, ADAM_WD, ADAM_STEP = 0.001, 0.9, 0.999, 1e-08, 0.01, 10
VMEM_LIMIT = 48 * 1024 * 1024

_NT = (((1,), (1,)), ((), ()))
_TN = (((0,), (0,)), ((), ()))


def _dot(a, b):
    return jnp.dot(a, b, preferred_element_type=F32)


def _dot_nt(a, b):
    return lax.dot_general(a, b, _NT, preferred_element_type=F32)


def _dot_tn(a, b):
    return lax.dot_general(a, b, _TN, preferred_element_type=F32)


def _split3(x):
    p1 = x.astype(BF16)
    r1 = x - p1.astype(F32)
    p2 = r1.astype(BF16)
    p3 = (r1 - p2.astype(F32)).astype(BF16)
    return p1, p2, p3


def _exact_dot(sel, x):
    p1, p2, p3 = _split3(x)
    return _dot(sel, p1) + _dot(sel, p2) + _dot(sel, p3)


def _exact_dot_r(x, sel):
    p1, p2, p3 = _split3(x)
    return _dot(p1, sel) + _dot(p2, sel) + _dot(p3, sel)


def _params(sem=None):
    return pltpu.CompilerParams(dimension_semantics=sem, vmem_limit_bytes=VMEM_LIMIT)


def _mm(a, b, mode, *, name, m, n, k, tm=512, tn=512, tk=512, out_dtypes=(F32,), epi=None,
        tile_extras=(), rowtile_extras=(), a_perm=1, out_perm=1, a_split=1, b_split=1, out_split=1,
        b_col_off=0, b_k_off=0, out_col_off=0, out_cols=None, alias=None, epi_wants_j=False):
    tm, tn, tk = min(tm, m), min(tn, n), min(tk, k)
    assert m % tm == 0 and n % tn == 0 and k % tk == 0, (name, m, n, k, tm, tn, tk)
    gm, gn, gk = m // tm, n // tn, k // tk

    def perm_idx(t, rows_per_class_tiles):
        return t // rows_per_class_tiles, t % rows_per_class_tiles

    if mode in ("nn", "nt"):
        if a_perm > 1:
            assert a_split == 1
            per = (m // a_perm) // tm
            assert per >= 1 and (m // a_perm) % tm == 0
            a_spec = pl.BlockSpec((tm, tk), lambda i, j, kk: (i % per, (i // per) * gk + kk))
        elif a_split > 1:
            kc = (k // a_split) // tk
            a_spec = pl.BlockSpec((None, tm, tk), lambda i, j, kk: (kk // kc, i, kk % kc))
        else:
            a_spec = pl.BlockSpec((tm, tk), lambda i, j, kk: (i, kk))
    else:
        if a_perm > 1:
            per = (k // a_perm) // tk
            assert per >= 1 and (k // a_perm) % tk == 0
            a_spec = pl.BlockSpec((tk, tm), lambda i, j, kk: (kk % per, (kk // per) * gm + i))
        else:
            a_spec = pl.BlockSpec((tk, tm), lambda i, j, kk: (kk, i))
    if mode == "nn" or mode == "tn":
        if b_split > 1:
            nc = (n // b_split) // tn
            b_spec = pl.BlockSpec((None, tk, tn), lambda i, j, kk: (j // nc, kk, j % nc))
        else:
            b_spec = pl.BlockSpec((tk, tn), lambda i, j, kk: (kk + b_k_off, j + b_col_off))
    else:
        b_spec = pl.BlockSpec((tn, tk), lambda i, j, kk: (j + b_col_off, kk + b_k_off))
    if out_perm > 1:
        per_o = (m // out_perm) // tm
        o_spec = pl.BlockSpec((tm, tn), lambda i, j, kk: (i % per_o, (i // per_o) * gn + j))
        o_shape = (m // out_perm, out_perm * n)
    elif out_split > 1:
        nco = (n // out_split) // tn
        o_spec = pl.BlockSpec((None, tm, tn), lambda i, j, kk: (j // nco, i, j % nco))
        o_shape = (out_split, m, n // out_split)
    else:
        o_spec = pl.BlockSpec((tm, tn), lambda i, j, kk: (i, j + out_col_off))
        o_shape = (m, out_cols if out_cols is not None else n)
    n_ex = len(tile_extras) + len(rowtile_extras)
    n_out = len(out_dtypes)
    if epi is None:
        def epi(acc):
            return (acc,)

    if mode == "nn":
        dot = _dot
    elif mode == "nt":
        dot = _dot_nt
    else:
        dot = _dot_tn

    def body(*refs):
        a_ref, b_ref = refs[0], refs[1]
        ex = refs[2:2 + n_ex]
        outs = refs[2 + n_ex + (1 if alias is not None else 0):][:n_out]
        acc = refs[-1]
        kk = pl.program_id(2)
        jj = pl.program_id(1)

        @pl.when(kk == 0)
        def _():
            acc[...] = jnp.zeros_like(acc)

        acc[...] += dot(a_ref[...].astype(BF16), b_ref[...].astype(BF16))

        @pl.when(kk == gk - 1)
        def _():
            lead = (jj,) if epi_wants_j else ()
            res = epi(*lead, acc[...], *[e[...] for e in ex])
            for o, r in zip(outs, res):
                o[...] = r.astype(o.dtype)

    rt_spec = pl.BlockSpec((tm, tn), lambda i, j, kk: (i, 0))
    in_specs = [a_spec, b_spec] + [o_spec] * len(tile_extras) + [rt_spec] * len(rowtile_extras)
    args = [a, b] + list(tile_extras) + list(rowtile_extras)
    io_alias = {}
    if alias is not None:
        in_specs.append(pl.BlockSpec(memory_space=pl.ANY))
        args.append(alias)
        io_alias = {len(args) - 1: 0}
    out = pl.pallas_call(
        body, name=name, grid=(gm, gn, gk), in_specs=in_specs,
        out_specs=[o_spec] * n_out,
        out_shape=[jax.ShapeDtypeStruct(o_shape, dt) for dt in out_dtypes],
        scratch_shapes=[pltpu.VMEM((tm, tn), F32)],
        input_output_aliases=io_alias,
        compiler_params=_params(("parallel", "parallel", "arbitrary")),
    )(*args)
    return out[0] if n_out == 1 else out


def _rope_tables(seq, dil, width):
    pos = jnp.arange(seq, dtype=jnp.int32).reshape(seq // dil, dil).T.reshape(seq)
    half = ATTN_HEAD_DIM // 2
    inv = ROPE_THETA ** (-jnp.arange(half, dtype=F32) * (2.0 / ATTN_HEAD_DIM))
    ang = pos.astype(F32)[:, None] * inv[None, :]
    cos, sin = jnp.cos(ang), jnp.sin(ang)
    zero = jnp.zeros_like(sin)
    reps = width // ATTN_HEAD_DIM
    c = jnp.tile(jnp.concatenate([cos, cos], axis=1), (1, reps))
    s_lo = jnp.tile(jnp.concatenate([-sin, zero], axis=1), (1, reps))
    s_hi = jnp.tile(jnp.concatenate([zero, sin], axis=1), (1, reps))
    return c, s_lo, s_hi


def _rotate(x, c, s_lo, s_hi, sign=1.0):
    w = x.shape[-1]
    half = ATTN_HEAD_DIM // 2
    up = pltpu.roll(x, w - half, 1)
    dn = pltpu.roll(x, half, 1)
    return x * c + sign * (up * s_lo + dn * s_hi)


ROW_TILE = 256


def _ln_fwd(x, y, g, b, *, name):
    s, d = x.shape

    def body(x_ref, y_ref, g_ref, b_ref, o_ref, xh_ref, r_ref):
        u = ALPHA * x_ref[...] + y_ref[...]
        mu = jnp.mean(u, axis=-1, keepdims=True)
        uc = u - mu
        var = jnp.mean(uc * uc, axis=-1, keepdims=True)
        rstd = lax.rsqrt(var + LN_EPS)
        xh = uc * rstd
        o_ref[...] = xh * g_ref[...] + b_ref[...]
        xh_ref[...] = xh
        r_ref[...] = rstd

    row = pl.BlockSpec((ROW_TILE, d), lambda i: (i, 0))
    vec = pl.BlockSpec((1, d), lambda i: (0, 0))
    return pl.pallas_call(
        body, name=name, grid=(s // ROW_TILE,), in_specs=[row, row, vec, vec],
        out_specs=[row, row, pl.BlockSpec((ROW_TILE, 1), lambda i: (i, 0))],
        out_shape=[jax.ShapeDtypeStruct((s, d), F32), jax.ShapeDtypeStruct((s, d), F32),
                   jax.ShapeDtypeStruct((s, 1), F32)],
        compiler_params=_params(("parallel",)),
    )(x, y, g, b)


def _ln_bwd(dy, xhat, rstd, g, *, name):
    s, d = dy.shape

    def body(dy_ref, xh_ref, r_ref, g_ref, du_ref, dg_ref, db_ref):
        @pl.when(pl.program_id(0) == 0)
        def _():
            dg_ref[...] = jnp.zeros_like(dg_ref)
            db_ref[...] = jnp.zeros_like(db_ref)

        dyv = dy_ref[...]
        xh = xh_ref[...]
        dxh = dyv * g_ref[...]
        m1 = jnp.mean(dxh, axis=-1, keepdims=True)
        m2 = jnp.mean(dxh * xh, axis=-1, keepdims=True)
        du_ref[...] = r_ref[...] * (dxh - m1 - xh * m2)
        dg_ref[...] += jnp.sum(dyv * xh, axis=0, keepdims=True)
        db_ref[...] += jnp.sum(dyv, axis=0, keepdims=True)

    row = pl.BlockSpec((ROW_TILE, d), lambda i: (i, 0))
    vec = pl.BlockSpec((1, d), lambda i: (0, 0))
    return pl.pallas_call(
        body, name=name, grid=(s // ROW_TILE,),
        in_specs=[row, row, pl.BlockSpec((ROW_TILE, 1), lambda i: (i, 0)), vec],
        out_specs=[row, vec, vec],
        out_shape=[jax.ShapeDtypeStruct((s, d), F32), jax.ShapeDtypeStruct((1, d), F32),
                   jax.ShapeDtypeStruct((1, d), F32)],
        compiler_params=_params(("arbitrary",)),
    )(dy, xhat, rstd, g)


def _loss_head(y, target):
    s, d = y.shape

    def body(y_ref, t_ref, dy_ref, sq_ref):
        @pl.when(pl.program_id(0) == 0)
        def _():
            sq_ref[...] = jnp.zeros_like(sq_ref)

        e = y_ref[...] - t_ref[...]
        dy_ref[...] = e * (1.0 / d)
        sq_ref[...] += jnp.sum(e * e, axis=0, keepdims=True)

    row = pl.BlockSpec((ROW_TILE, d), lambda i: (i, 0))
    vec = pl.BlockSpec((1, d), lambda i: (0, 0))
    return pl.pallas_call(
        body, name="loss_head", grid=(s // ROW_TILE,), in_specs=[row, row], out_specs=[row, vec],
        out_shape=[jax.ShapeDtypeStruct((s, d), F32), jax.ShapeDtypeStruct((1, d), F32)],
        compiler_params=_params(("arbitrary",)),
    )(y, target)


def _head_expand_matrix():
    h = lax.broadcasted_iota(jnp.int32, (LANES, D_MODEL), 0)
    l = lax.broadcasted_iota(jnp.int32, (LANES, D_MODEL), 1)
    return (l // ATTN_HEAD_DIM == h).astype(BF16)


def _attn_fwd(qkv, dil, *, name):
    _, s, d = qkv.shape
    nq = s // ATTN_BLK
    per = nq // dil
    scale = ATTN_HEAD_DIM ** -0.5
    tiles = d // LANES

    def body(q_ref, kc_ref, kp_ref, vc_ref, vp_ref, o_ref, lse_ref):
        qb = pl.program_id(0)
        first = (qb % per) == 0
        qi = lax.broadcasted_iota(jnp.int32, (ATTN_BLK, 2 * ATTN_BLK), 0)
        kj = lax.broadcasted_iota(jnp.int32, (ATTN_BLK, 2 * ATTN_BLK), 1)
        dist = qi + ATTN_BLK - kj
        valid = (dist >= 0) & (dist <= ATTN_BLK) & ((kj >= ATTN_BLK) | jnp.logical_not(first))
        lane = lax.broadcasted_iota(jnp.int32, (ATTN_BLK, LANES), 1)
        lse_tile = jnp.zeros((ATTN_BLK, LANES), F32)
        for t in range(tiles):
            cols = pl.ds(t * LANES, LANES)
            q = q_ref[:, cols]
            k2 = jnp.concatenate([kp_ref[:, cols], kc_ref[:, cols]], axis=0)
            v2 = jnp.concatenate([vp_ref[:, cols], vc_ref[:, cols]], axis=0)
            o_tile = jnp.zeros((ATTN_BLK, LANES), F32)
            for hh in range(2):
                in_head = (lane // ATTN_HEAD_DIM) == hh
                qh = jnp.where(in_head, q, jnp.zeros_like(q))
                sc = _dot_nt(qh, k2) * scale
                sc = jnp.where(valid, sc, -jnp.inf)
                mx = jnp.max(sc, axis=-1, keepdims=True)
                p = jnp.exp(sc - mx)
                l = jnp.sum(p, axis=-1, keepdims=True)
                oh = _dot(p.astype(BF16), v2) / l
                o_tile = jnp.where(in_head, oh, o_tile)
                lse_tile = jnp.where(lane == 2 * t + hh, mx + jnp.log(l), lse_tile)
            o_ref[:, cols] = o_tile
        lse_ref[...] = lse_tile

    def blk(piece, prev):
        if prev:
            return pl.BlockSpec((None, ATTN_BLK, d), lambda i: (piece, jnp.maximum(i - 1, 0), 0))
        return pl.BlockSpec((None, ATTN_BLK, d), lambda i: (piece, i, 0))

    return pl.pallas_call(
        body, name=name, grid=(nq,),
        in_specs=[blk(0, False), blk(1, False), blk(1, True), blk(2, False), blk(2, True)],
        out_specs=[pl.BlockSpec((ATTN_BLK, d), lambda i: (i % per, i // per)),
                   pl.BlockSpec((ATTN_BLK, LANES), lambda i: (i % per, i // per))],
        out_shape=[jax.ShapeDtypeStruct((s // dil, dil * d), F32),
                   jax.ShapeDtypeStruct((s // dil, dil * LANES), F32)],
        compiler_params=_params(("parallel",)),
    )(qkv, qkv, qkv, qkv, qkv)


def _attn_combine(os_, lses):
    s, d = os_[0].shape
    sel = _head_expand_matrix()

    def body(o0, o1, o2, l0, l1, l2, sel_ref, of_ref, ob_ref, lt_ref):
        la, lb_, lc = l0[...], l1[...], l2[...]
        mx = jnp.maximum(jnp.maximum(la, lb_), lc)
        ea, eb, ec = jnp.exp(la - mx), jnp.exp(lb_ - mx), jnp.exp(lc - mx)
        z = ea + eb + ec
        lt_ref[...] = mx + jnp.log(z)
        acc = jnp.zeros((ROW_TILE, d), F32)
        for e, o in ((ea, o0), (eb, o1), (ec, o2)):
            acc += _exact_dot_r(e / z, sel_ref[...]) * o[...]
        of_ref[...] = acc
        ob_ref[...] = acc.astype(BF16)

    row = pl.BlockSpec((ROW_TILE, d), lambda i: (i, 0))
    st = pl.BlockSpec((ROW_TILE, LANES), lambda i: (i, 0))
    return pl.pallas_call(
        body, name="attn_combine", grid=(s // ROW_TILE,),
        in_specs=[row, row, row, st, st, st, pl.BlockSpec((LANES, d), lambda i: (0, 0))],
        out_specs=[row, row, st],
        out_shape=[jax.ShapeDtypeStruct((s, d), F32), jax.ShapeDtypeStruct((s, d), BF16),
                   jax.ShapeDtypeStruct((s, LANES), F32)],
        compiler_params=_params(("parallel",)),
    )(*os_, *lses, sel)


def _attn_delta(do, o):
    s, d = do.shape
    sel_t = _head_expand_matrix().T

    def body(do_ref, o_ref, sel_ref, out_ref):
        out_ref[...] = _exact_dot_r(do_ref[...] * o_ref[...], sel_ref[...])

    row = pl.BlockSpec((ROW_TILE, d), lambda i: (i, 0))
    return pl.pallas_call(
        body, name="attn_delta", grid=(s // ROW_TILE,),
        in_specs=[row, row, pl.BlockSpec((d, LANES), lambda i: (0, 0))],
        out_specs=pl.BlockSpec((ROW_TILE, LANES), lambda i: (i, 0)),
        out_shape=jax.ShapeDtypeStruct((s, LANES), F32),
        compiler_params=_params(("parallel",)),
    )(do, o, sel_t)


def _attn_bwd(qkv, do, lse, delta, tables, dil, *, name):
    _, s, d = qkv.shape
    nq = s // ATTN_BLK
    per = nq // dil
    scale = ATTN_HEAD_DIM ** -0.5
    tiles = d // LANES
    do_v = do.reshape(s // dil, dil * d)
    lse_v = lse.reshape(s // dil, dil * LANES)
    delta_v = delta.reshape(s // dil, dil * LANES)

    def body(qd_ref, qo_ref, k_ref, v_ref, dod_ref, doo_ref, lsd_ref, lso_ref, dld_ref, dlo_ref,
             c_ref, slo_ref, shi_ref, out_ref, carry):
        kb = pl.program_id(0)

        @pl.when(kb == 0)
        def _():
            carry[...] = jnp.zeros_like(carry)

        has_next = (kb % per) != (per - 1)
        qi = lax.broadcasted_iota(jnp.int32, (ATTN_BLK, ATTN_BLK), 0)
        kj = lax.broadcasted_iota(jnp.int32, (ATTN_BLK, ATTN_BLK), 1)
        valid_d = kj <= qi
        valid_o = (kj >= qi) & has_next
        lane = lax.broadcasted_iota(jnp.int32, (ATTN_BLK, LANES), 1)
        c, slo, shi = c_ref[...], slo_ref[...], shi_ref[...]
        for t in range(tiles):
            cols = pl.ds(t * LANES, LANES)
            k = k_ref[:, cols]
            v = v_ref[:, cols]
            dk = jnp.zeros((ATTN_BLK, LANES), F32)
            dv = jnp.zeros((ATTN_BLK, LANES), F32)
            dq_d = carry[:, cols]
            dq_o = jnp.zeros((ATTN_BLK, LANES), F32)
            for hh in range(2):
                head = 2 * t + hh
                in_head = (lane // ATTN_HEAD_DIM) == hh
                kh = jnp.where(in_head, k, jnp.zeros_like(k))
                for which in range(2):
                    q_ref, do_ref, ls_ref, dl_ref, valid = (
                        (qd_ref, dod_ref, lsd_ref, dld_ref, valid_d),
                        (qo_ref, doo_ref, lso_ref, dlo_ref, valid_o))[which]
                    q = q_ref[:, cols]
                    qh = jnp.where(in_head, q, jnp.zeros_like(q))
                    dob = do_ref[:, cols].astype(BF16)
                    doh = jnp.where(in_head, dob, jnp.zeros_like(dob))
                    sc = _dot_nt(qh, k) * scale
                    p = jnp.where(valid, jnp.exp(sc - ls_ref[:, head:head + 1]), 0.0)
                    dp = _dot_nt(doh, v)
                    ds = (p * (dp - dl_ref[:, head:head + 1]) * scale).astype(BF16)
                    dv += _dot_tn(p.astype(BF16), doh)
                    dk += _dot_tn(ds, qh)
                    dq = _dot(ds, kh)
                    if which == 0:
                        dq_d += dq
                    else:
                        dq_o += dq
            carry[:, cols] = dq_o
            out_ref[0, :, cols] = _rotate(dq_d, c, slo, shi, -1.0).astype(BF16)
            out_ref[1, :, cols] = _rotate(dk, c, slo, shi, -1.0).astype(BF16)
            out_ref[2, :, cols] = dv.astype(BF16)

    def nxt(i):
        return jnp.minimum(i + 1, nq - 1)

    def cls(piece, shift):
        if shift:
            return pl.BlockSpec((None, ATTN_BLK, d), lambda i: (piece, nxt(i), 0))
        return pl.BlockSpec((None, ATTN_BLK, d), lambda i: (piece, i, 0))

    def pos(width, shift):
        if shift:
            return pl.BlockSpec((ATTN_BLK, width), lambda i: (nxt(i) % per, nxt(i) // per))
        return pl.BlockSpec((ATTN_BLK, width), lambda i: (i % per, i // per))

    tab = pl.BlockSpec((ATTN_BLK, LANES), lambda i: (i, 0))
    return pl.pallas_call(
        body, name=name, grid=(nq,),
        in_specs=[cls(0, False), cls(0, True), cls(1, False), cls(2, False),
                  pos(d, False), pos(d, True), pos(LANES, False), pos(LANES, True), pos(LANES, False), pos(LANES, True),
                  tab, tab, tab],
        out_specs=pl.BlockSpec((3, ATTN_BLK, d), lambda i: (0, i, 0)),
        out_shape=jax.ShapeDtypeStruct((3, s, d), BF16),
        scratch_shapes=[pltpu.VMEM((ATTN_BLK, d), F32)],
        compiler_params=_params(("arbitrary",)),
    )(qkv, qkv, qkv, qkv, do_v, do_v, lse_v, lse_v, delta_v, delta_v, *tables)


HGRN_ROWS = 512


def _tri(lower):
    i = lax.broadcasted_iota(jnp.int32, (HGRN_CHUNK, HGRN_CHUNK), 0)
    j = lax.broadcasted_iota(jnp.int32, (HGRN_CHUNK, HGRN_CHUNK), 1)
    return (j <= i) if lower else (j >= i)


def _hgrn_gates(qr, z, lb):
    sq = jax.nn.sigmoid(qr)
    q = qr * sq
    sg = jax.nn.sigmoid(z)
    f = lb + (1.0 - lb) * sg
    key = (1.0 - lb) * jax.nn.sigmoid(-z)
    return sq, q, sg, f, key


def _hgrn_fwd(p2, lb, norm_g):
    _, s, d = p2.shape
    nblk = s // HGRN_ROWS
    cps = HGRN_ROWS // HGRN_CHUNK

    def body(p_ref, lb_ref, g_ref, o_ref, on_ref, st_ref, state):
        @pl.when(pl.program_id(1) == 0)
        def _():
            state[...] = jnp.zeros_like(state)

        lbv = lb_ref[...]
        gv = g_ref[...]
        ltri = _tri(True)
        lsel = ltri.astype(BF16)
        for cidx in range(cps):
            rows = pl.ds(cidx * HGRN_CHUNK, HGRN_CHUNK)
            _, q, _, f, key = _hgrn_gates(p_ref[0, rows, :], p_ref[1, rows, :], lbv)
            v = p_ref[2, rows, :].astype(BF16)
            b = _exact_dot(lsel, jnp.log(f))
            b_last = b[HGRN_CHUNK - 1:HGRN_CHUNK, :]
            qd = (q * jnp.exp(b)).astype(BF16)
            kd = (key * jnp.exp(-b)).astype(BF16)
            k2 = (key * jnp.exp(b_last - b)).astype(BF16)
            st = state[...]
            st_ref[0, cidx] = st
            a = jnp.where(ltri, _dot_nt(qd, kd), 0.0)
            o = _dot(a.astype(BF16), v) + _dot_nt(qd, st.astype(BF16))
            state[...] = st * jnp.exp(b_last) + _dot_tn(v, k2)
            o_ref[rows, :] = o
            r = lax.rsqrt(jnp.mean(o * o, axis=-1, keepdims=True) + RMS_EPS)
            on_ref[rows, :] = (o * r * gv).astype(BF16)

    vec = pl.BlockSpec((1, LANES), lambda h, c: (0, h))
    col = pl.BlockSpec((HGRN_ROWS, LANES), lambda h, c: (c, h))
    return pl.pallas_call(
        body, name="hgrn_fwd", grid=(HGRN_HEADS, nblk),
        in_specs=[pl.BlockSpec((3, HGRN_ROWS, LANES), lambda h, c: (0, c, h)), vec, vec],
        out_specs=[col, col, pl.BlockSpec((1, cps, LANES, LANES), lambda h, c: (h, c, 0, 0))],
        out_shape=[jax.ShapeDtypeStruct((s, d), F32), jax.ShapeDtypeStruct((s, d), BF16),
                   jax.ShapeDtypeStruct((HGRN_HEADS, s // HGRN_CHUNK, LANES, LANES), F32)],
        scratch_shapes=[pltpu.VMEM((LANES, LANES), F32)],
        compiler_params=_params(("parallel", "arbitrary")),
    )(p2, lb, norm_g)


def _hgrn_bwd(p2, lb, norm_g, o_raw, states, dyn):
    _, s, d = p2.shape
    nblk = s // HGRN_ROWS
    cps = HGRN_ROWS // HGRN_CHUNK

    def body(p_ref, lb_ref, g_ref, o_ref, st_ref, dy_ref, dp_ref, dg_ref, dlb_ref, dstate):
        @pl.when(pl.program_id(1) == 0)
        def _():
            dstate[...] = jnp.zeros_like(dstate)
            dg_ref[...] = jnp.zeros_like(dg_ref)
            dlb_ref[...] = jnp.zeros_like(dlb_ref)

        lbv = lb_ref[...]
        gv = g_ref[...]
        ltri = _tri(True)
        lsel = ltri.astype(BF16)
        usel = _tri(False).astype(BF16)
        last_row = lax.broadcasted_iota(jnp.int32, (HGRN_CHUNK, LANES), 0) == HGRN_CHUNK - 1
        for cidx in reversed(range(cps)):
            rows = pl.ds(cidx * HGRN_CHUNK, HGRN_CHUNK)
            qr, z = p_ref[0, rows, :], p_ref[1, rows, :]
            sq, q, sg, f, key = _hgrn_gates(qr, z, lbv)
            v = p_ref[2, rows, :].astype(BF16)
            b = _exact_dot(lsel, jnp.log(f))
            b_last = b[HGRN_CHUNK - 1:HGRN_CHUNK, :]
            eb, enb, e2 = jnp.exp(b), jnp.exp(-b), jnp.exp(b_last - b)
            dec = jnp.exp(b_last)
            qd_f, kd_f, k2_f = q * eb, key * enb, key * e2
            qd, kd, k2 = qd_f.astype(BF16), kd_f.astype(BF16), k2_f.astype(BF16)
            a = jnp.where(ltri, _dot_nt(qd, kd), 0.0).astype(BF16)
            st = st_ref[0, cidx]
            dst = dstate[...]
            o = o_ref[rows, :]
            dyv = dy_ref[rows, :]
            r = lax.rsqrt(jnp.mean(o * o, axis=-1, keepdims=True) + RMS_EPS)
            oh = o * r
            doh = dyv * gv
            do_f = r * (doh - oh * jnp.mean(doh * oh, axis=-1, keepdims=True))
            dg_ref[...] += jnp.sum(dyv * oh, axis=0, keepdims=True)
            do = do_f.astype(BF16)
            dstb = dst.astype(BF16)
            da = jnp.where(ltri, _dot_nt(do, v), 0.0).astype(BF16)
            dv = _dot_tn(a, do) + _dot_nt(k2, dstb)
            dqd = _dot(da, kd) + _dot(do, st.astype(BF16))
            dkd = _dot_tn(da, qd)
            dk2 = _dot(v, dstb)
            ddec = jnp.sum(dst * st, axis=0, keepdims=True)
            dstate[...] = dst * dec + _dot_tn(do, qd)
            dq = dqd * eb
            dkey = dkd * enb + dk2 * e2
            db = dqd * qd_f - dkd * kd_f - dk2 * k2_f
            db_last = jnp.sum(dk2 * k2_f, axis=0, keepdims=True) + ddec * dec
            db = db + jnp.where(last_row, db_last, 0.0)
            dlf = _exact_dot(usel, db)
            one_m_s = 1.0 - sg
            dz = (dlf / f - dkey) * ((1.0 - lbv) * sg * one_m_s)
            dlb_ref[...] += jnp.sum(dlf * one_m_s / f - dkey * jax.nn.sigmoid(-z), axis=0, keepdims=True)
            dp_ref[0, rows, :] = (dq * (sq * (1.0 + qr * (1.0 - sq)))).astype(BF16)
            dp_ref[1, rows, :] = dz.astype(BF16)
            dp_ref[2, rows, :] = dv.astype(BF16)

    def rev(c):
        return nblk - 1 - c

    vec = pl.BlockSpec((1, LANES), lambda h, c: (0, h))
    col = pl.BlockSpec((HGRN_ROWS, LANES), lambda h, c: (rev(c), h))
    p3 = pl.BlockSpec((3, HGRN_ROWS, LANES), lambda h, c: (0, rev(c), h))
    return pl.pallas_call(
        body, name="hgrn_bwd", grid=(HGRN_HEADS, nblk),
        in_specs=[p3, vec, vec, col, pl.BlockSpec((1, cps, LANES, LANES), lambda h, c: (h, rev(c), 0, 0)), col],
        out_specs=[p3, vec, vec],
        out_shape=[jax.ShapeDtypeStruct((3, s, d), BF16), jax.ShapeDtypeStruct((1, d), F32),
                   jax.ShapeDtypeStruct((1, d), F32)],
        scratch_shapes=[pltpu.VMEM((LANES, LANES), F32)],
        compiler_params=_params(("parallel", "arbitrary")),
    )(p2, lb, norm_g, o_raw, states, dyn)


def _relu2_epi(acc):
    r = jnp.maximum(acc, 0.0)
    return acc, r * r


def _dact_epi(acc, h):
    return (acc * (2.0 * jnp.maximum(h, 0.0)),)


def _local_step(x, target, w, small):
    s, d = x.shape
    lb, norm_g = small["lb"], w["hgrn_norm_g"]
    gq = 3 * d

    def res_epi(scale):
        def epi(acc, r):
            return (acc + scale * r,)
        return epi

    def rope_epi(j, acc, c, slo, shi):
        return (jnp.where(j >= 4, acc, _rotate(acc, c, slo, shi)),)

    def ffn_fwd(xin, w_up, w_down, tag):
        h, a = _mm(xin, w_up, "nn", name=f"ffn_up_{tag}", m=s, n=D_FF, k=d, tk=d, out_dtypes=(F32, BF16), epi=_relu2_epi)
        y = _mm(a, w_down, "nn", name=f"ffn_down_{tag}", m=s, n=d, k=D_FF)
        return h, a, y

    def ffn_bwd(du, xin, h, a, w_up, w_down, tag):
        dh = _mm(du, w_down, "nt", name=f"ffn_dact_{tag}", m=s, n=D_FF, k=d, tk=d, out_dtypes=(BF16,), epi=_dact_epi,
                 tile_extras=(h,))
        g_down = _mm(a, du, "tn", name=f"ffn_gdown_{tag}", m=D_FF, n=d, k=s, out_dtypes=(BF16,))
        g_up = _mm(xin, dh, "tn", name=f"ffn_gup_{tag}", m=d, n=D_FF, k=s, out_dtypes=(BF16,))
        dx = _mm(dh, w_up, "nt", name=f"ffn_dx_{tag}", m=s, n=d, k=D_FF, epi=res_epi(ALPHA), tile_extras=(du,))
        return dx, g_up, g_down

    tabs, qkvs, o_parts, lse_parts = [], [], [], []
    for g, dil in enumerate(DILATIONS):
        tabs.append(_rope_tables(s, dil, LANES))
        qkv = _mm(x.reshape(s // dil, dil * d), w["attn_w_in"], "nn", name=f"attn_in_{g}", m=s, n=gq, k=d,
                  tm=min(256, s // dil), tn=512, tk=d, a_perm=dil, b_col_off=g * (gq // 512), out_split=3, out_dtypes=(BF16,),
                  epi=rope_epi, epi_wants_j=True, rowtile_extras=_rope_tables(s, dil, 512))
        qkvs.append(qkv)
        o_g, lse_g = _attn_fwd(qkv, dil, name=f"attn_fwd_{g}")
        o_parts.append(o_g.reshape(s, d))
        lse_parts.append(lse_g.reshape(s, LANES))
    o_f, o_b, lse_t = _attn_combine(o_parts, lse_parts)
    y = _mm(o_b, w["attn_w_out"], "nn", name="attn_out", m=s, n=d, k=d, tk=d)
    x1, xh1, r1 = _ln_fwd(x, y, small["ln_mix_g"][0:1], small["ln_mix_b"][0:1], name="ln_mix_0")
    h0, a0, y = ffn_fwd(x1, w["ffn_w_up"][0], w["ffn_w_down"][0], 0)
    x2, xh2, r2 = _ln_fwd(x1, y, small["ln_ffn_g"][0:1], small["ln_ffn_b"][0:1], name="ln_ffn_0")
    p2 = _mm(x2, w["hgrn_w_in"], "nn", name="hgrn_in", m=s, n=3 * d, k=d, tk=d, out_split=3)
    o_raw, o_n, states = _hgrn_fwd(p2, lb, norm_g)
    y = _mm(o_n, w["hgrn_w_out"], "nn", name="hgrn_out", m=s, n=d, k=d, tk=d)
    x3, xh3, r3 = _ln_fwd(x2, y, small["ln_mix_g"][1:2], small["ln_mix_b"][1:2], name="ln_mix_1")
    h1, a1, y = ffn_fwd(x3, w["ffn_w_up"][1], w["ffn_w_down"][1], 1)
    x4, xh4, r4 = _ln_fwd(x3, y, small["ln_ffn_g"][1:2], small["ln_ffn_b"][1:2], name="ln_ffn_1")
    dy, sq = _loss_head(x4, target)
    du, dg_ffn1, db_ffn1 = _ln_bwd(dy, xh4, r4, small["ln_ffn_g"][1:2], name="ln_ffn_1_bwd")
    dx3, g_up1, g_down1 = ffn_bwd(du, x3, h1, a1, w["ffn_w_up"][1], w["ffn_w_down"][1], 1)
    du, dg_mix1, db_mix1 = _ln_bwd(dx3, xh3, r3, small["ln_mix_g"][1:2], name="ln_mix_1_bwd")
    dyn = _mm(du, w["hgrn_w_out"], "nt", name="hgrn_dout", m=s, n=d, k=d, tk=d)
    g_hout = _mm(o_n, du, "tn", name="hgrn_gout", m=d, n=d, k=s, out_dtypes=(BF16,))
    dp2, d_norm_g, d_lb = _hgrn_bwd(p2, lb, norm_g, o_raw, states, dyn)
    g_hin = _mm(x2, dp2, "tn", name="hgrn_gin", m=d, n=3 * d, k=s, b_split=3, out_dtypes=(BF16,))
    dx2 = _mm(dp2, w["hgrn_w_in"], "nt", name="hgrn_dx", m=s, n=d, k=3 * d, a_split=3, epi=res_epi(ALPHA),
              tile_extras=(du,))
    du, dg_ffn0, db_ffn0 = _ln_bwd(dx2, xh2, r2, small["ln_ffn_g"][0:1], name="ln_ffn_0_bwd")
    dx1, g_up0, g_down0 = ffn_bwd(du, x1, h0, a0, w["ffn_w_up"][0], w["ffn_w_down"][0], 0)
    du, dg_mix0, db_mix0 = _ln_bwd(dx1, xh1, r1, small["ln_mix_g"][0:1], name="ln_mix_0_bwd")
    do = _mm(du, w["attn_w_out"], "nt", name="attn_dout", m=s, n=d, k=d, tk=d)
    g_aout = _mm(o_b, du, "tn", name="attn_gout", m=d, n=d, k=s, out_dtypes=(BF16,))
    delta = _attn_delta(do, o_f)
    gx, g_ain = du, None
    for g, dil in enumerate(DILATIONS):
        dqkv = _attn_bwd(qkvs[g], do, lse_t, delta, tabs[g], dil, name=f"attn_bwd_{g}")
        g_ain = _mm(x.reshape(s // dil, dil * d), dqkv, "tn", name=f"attn_gin_{g}", m=d, n=gq, k=s, tk=min(512, s // dil),
                    a_perm=dil, b_split=3, out_dtypes=(BF16,), out_col_off=g * (gq // 512), out_cols=3 * gq, alias=g_ain)
        gx = _mm(dqkv, w["attn_w_in"], "nt", name=f"attn_dx_{g}", m=s, n=d, k=gq, tm=min(256, s // dil), a_split=3,
                 b_k_off=g * (gq // 512), out_perm=dil, epi=res_epi(ALPHA if g == 0 else 1.0),
                 tile_extras=(gx.reshape(s // dil, dil * d),)).reshape(s, d)
    sm1 = jax.nn.softmax(small["lb_logits"], axis=0)
    d_l1 = d_lb * (sm1[0:1] * sm1[1:2])
    zeros = jnp.zeros((SMALL_ROWS - 11, d), F32)
    small_grads = jnp.concatenate([dg_mix0, dg_mix1, db_mix0, db_mix1, dg_ffn0, dg_ffn1, db_ffn0, db_ffn1,
                                   -d_l1, d_l1, d_norm_g, zeros], axis=0)
    grads = {
        "attn_w_in": g_ain, "attn_w_out": g_aout, "hgrn_w_in": g_hin, "hgrn_w_out": g_hout,
        "ffn_w_up": jnp.stack([g_up0, g_up1]), "ffn_w_down": jnp.stack([g_down0, g_down1]),
    }
    return jnp.sum(sq), gx, grads, small_grads


SMALL_ROWS = 16


def _mesh_place():
    x, y, c = lax.axis_index("x"), lax.axis_index("y"), lax.axis_index("c")
    return x, y, c, 4 * x + 2 * y + c


def _peer(x, y, c, k):
    px = 1 - x if (k >> 2) & 1 else x
    py = 1 - y if (k >> 1) & 1 else y
    pc = 1 - c if k & 1 else c
    return (px, py, pc), 4 * px + 2 * py + pc


def _window(ref, axis, size, idx):
    if axis is None:
        return ref
    sl = [slice(None)] * len(ref.shape)
    sl[axis] = pl.ds(idx * size, size)
    return ref.at[tuple(sl)]


def _all_gather(shards, axes):
    n = len(shards)
    full_shapes = []
    for sh, ax in zip(shards, axes):
        shape = list(sh.shape)
        shape[ax] *= N_DEV
        full_shapes.append(jax.ShapeDtypeStruct(tuple(shape), sh.dtype))

    def body(*refs):
        srcs, outs = refs[:n], refs[n:2 * n]
        send_sems, recv_sems, local_sems = refs[2 * n:]
        x, y, c, me = _mesh_place()
        local = []
        for i in range(n):
            cp = pltpu.make_async_copy(srcs[i], _window(outs[i], axes[i], srcs[i].shape[axes[i]], me), local_sems.at[i])
            cp.start()
            local.append(cp)
        sends = []
        for k in range(1, N_DEV):
            peer, _ = _peer(x, y, c, k)
            for i in range(n):
                cp = pltpu.make_async_remote_copy(
                    src_ref=srcs[i], dst_ref=_window(outs[i], axes[i], srcs[i].shape[axes[i]], me),
                    send_sem=send_sems.at[i, k - 1], recv_sem=recv_sems.at[i, k - 1],
                    device_id=peer, device_id_type=pl.DeviceIdType.MESH)
                cp.start()
                sends.append(cp)
        for k in range(1, N_DEV):
            peer, pidx = _peer(x, y, c, k)
            for i in range(n):
                pltpu.make_async_remote_copy(
                    src_ref=srcs[i], dst_ref=_window(outs[i], axes[i], srcs[i].shape[axes[i]], pidx),
                    send_sem=send_sems.at[i, k - 1], recv_sem=recv_sems.at[i, k - 1],
                    device_id=peer, device_id_type=pl.DeviceIdType.MESH).wait_recv()
        for cp in sends:
            cp.wait_send()
        for cp in local:
            cp.wait()

    hbm = pl.BlockSpec(memory_space=pl.ANY)
    return pl.pallas_call(
        body, name="all_gather_weights", in_specs=[hbm] * n, out_specs=[hbm] * n, out_shape=full_shapes,
        scratch_shapes=[pltpu.SemaphoreType.DMA((n, N_DEV - 1)), pltpu.SemaphoreType.DMA((n, N_DEV - 1)),
                        pltpu.SemaphoreType.DMA((n,))],
    )(*shards)


def _reduce_scatter(fulls, axes):
    n = len(fulls)
    sizes, out_shapes = [], []
    for f, ax in zip(fulls, axes):
        shape = list(f.shape)
        if ax is not None:
            shape[ax] //= N_DEV
            sizes.append(shape[ax])
        else:
            sizes.append(None)
        out_shapes.append(jax.ShapeDtypeStruct((N_DEV, *shape), f.dtype))

    def body(*refs):
        srcs, outs = refs[:n], refs[n:2 * n]
        send_sems, recv_sems, local_sems = refs[2 * n:]
        x, y, c, me = _mesh_place()
        local = []
        for i in range(n):
            cp = pltpu.make_async_copy(_window(srcs[i], axes[i], sizes[i], me), outs[i].at[me], local_sems.at[i])
            cp.start()
            local.append(cp)
        sends = []
        for k in range(1, N_DEV):
            peer, pidx = _peer(x, y, c, k)
            for i in range(n):
                cp = pltpu.make_async_remote_copy(
                    src_ref=_window(srcs[i], axes[i], sizes[i], pidx), dst_ref=outs[i].at[me],
                    send_sem=send_sems.at[i, k - 1], recv_sem=recv_sems.at[i, k - 1],
                    device_id=peer, device_id_type=pl.DeviceIdType.MESH)
                cp.start()
                sends.append(cp)
        for k in range(1, N_DEV):
            peer, pidx = _peer(x, y, c, k)
            for i in range(n):
                pltpu.make_async_remote_copy(
                    src_ref=_window(srcs[i], axes[i], sizes[i], me), dst_ref=outs[i].at[pidx],
                    send_sem=send_sems.at[i, k - 1], recv_sem=recv_sems.at[i, k - 1],
                    device_id=peer, device_id_type=pl.DeviceIdType.MESH).wait_recv()
        for cp in sends:
            cp.wait_send()
        for cp in local:
            cp.wait()

    hbm = pl.BlockSpec(memory_space=pl.ANY)
    return pl.pallas_call(
        body, name="reduce_scatter_grads", in_specs=[hbm] * n, out_specs=[hbm] * n, out_shape=out_shapes,
        scratch_shapes=[pltpu.SemaphoreType.DMA((n, N_DEV - 1)), pltpu.SemaphoreType.DMA((n, N_DEV - 1)),
                        pltpu.SemaphoreType.DMA((n,))],
    )(*fulls)


def _cast_bf16(a, *, name):
    r, c = a.shape
    tr = min(r, 512)

    def body(a_ref, o_ref):
        o_ref[...] = a_ref[...].astype(BF16)

    spec = pl.BlockSpec((tr, c), lambda i: (i, 0))
    return pl.pallas_call(body, name=name, grid=(r // tr,), in_specs=[spec], out_specs=spec,
                          out_shape=jax.ShapeDtypeStruct((r, c), BF16), compiler_params=_params(("parallel",)))(a)


def _adamw(slabs, w, m, v, *, name):
    r, c = w.shape
    tr = min(r, 256)

    def body(s_ref, w_ref, m_ref, v_ref, g_ref, d_ref, mo_ref, vo_ref):
        g = s_ref[0].astype(F32)
        for i in range(1, N_DEV):
            g = g + s_ref[i].astype(F32)
        m2 = ADAM_B1 * m_ref[...] + (1.0 - ADAM_B1) * g
        v2 = ADAM_B2 * v_ref[...] + (1.0 - ADAM_B2) * (g * g)
        m_hat = m2 / (1.0 - ADAM_B1 ** ADAM_STEP)
        v_hat = v2 / (1.0 - ADAM_B2 ** ADAM_STEP)
        g_ref[...] = g
        d_ref[...] = -ADAM_LR * (m_hat / (jnp.sqrt(v_hat) + ADAM_EPS) + ADAM_WD * w_ref[...])
        mo_ref[...] = m2
        vo_ref[...] = v2

    spec = pl.BlockSpec((tr, c), lambda i: (i, 0))
    return pl.pallas_call(
        body, name=name, grid=(r // tr,),
        in_specs=[pl.BlockSpec((N_DEV, tr, c), lambda i: (0, i, 0)), spec, spec, spec],
        out_specs=[spec] * 4, out_shape=[jax.ShapeDtypeStruct((r, c), F32)] * 4,
        compiler_params=_params(("parallel",)),
    )(slabs, w, m, v)


BIG = ("attn_w_in", "attn_w_out", "hgrn_w_in", "hgrn_w_out", "ffn_w_up", "ffn_w_down")
SHARD_AXIS = {"attn_w_in": 1, "attn_w_out": 0, "hgrn_w_in": 1, "hgrn_w_out": 0, "ffn_w_up": 2, "ffn_w_down": 1}
SMALL = ("lb_logits", "ln_mix_g", "ln_mix_b", "ln_ffn_g", "ln_ffn_b")
SMALL_ROW = {"ln_mix_g": 0, "ln_mix_b": 2, "ln_ffn_g": 4, "ln_ffn_b": 6, "lb_logits": 8}
NORM_G_ROW = 10


def kernel(x, attn_w_in, attn_w_out, hgrn_w_in, hgrn_w_out, hgrn_norm_g, lb_logits, ln_mix_g, ln_mix_b, ln_ffn_g, ln_ffn_b, ffn_w_up, ffn_w_down, loss_target, m_attn_w_in, m_attn_w_out, m_hgrn_w_in, m_hgrn_w_out, m_hgrn_norm_g, m_lb_logits, m_ln_mix_g, m_ln_mix_b, m_ln_ffn_g, m_ln_ffn_b, m_ffn_w_up, m_ffn_w_down, v_attn_w_in, v_attn_w_out, v_hgrn_w_in, v_hgrn_w_out, v_hgrn_norm_g, v_lb_logits, v_ln_mix_g, v_ln_mix_b, v_ln_ffn_g, v_ln_ffn_b, v_ffn_w_up, v_ffn_w_down):
    wts = dict(attn_w_in=attn_w_in, attn_w_out=attn_w_out, hgrn_w_in=hgrn_w_in, hgrn_w_out=hgrn_w_out,
               hgrn_norm_g=hgrn_norm_g, lb_logits=lb_logits, ln_mix_g=ln_mix_g, ln_mix_b=ln_mix_b, ln_ffn_g=ln_ffn_g,
               ln_ffn_b=ln_ffn_b, ffn_w_up=ffn_w_up, ffn_w_down=ffn_w_down)
    mom = dict(attn_w_in=m_attn_w_in, attn_w_out=m_attn_w_out, hgrn_w_in=m_hgrn_w_in, hgrn_w_out=m_hgrn_w_out,
               hgrn_norm_g=m_hgrn_norm_g, lb_logits=m_lb_logits, ln_mix_g=m_ln_mix_g, ln_mix_b=m_ln_mix_b,
               ln_ffn_g=m_ln_ffn_g, ln_ffn_b=m_ln_ffn_b, ffn_w_up=m_ffn_w_up, ffn_w_down=m_ffn_w_down)
    vel = dict(attn_w_in=v_attn_w_in, attn_w_out=v_attn_w_out, hgrn_w_in=v_hgrn_w_in, hgrn_w_out=v_hgrn_w_out,
               hgrn_norm_g=v_hgrn_norm_g, lb_logits=v_lb_logits, ln_mix_g=v_ln_mix_g, ln_mix_b=v_ln_mix_b,
               ln_ffn_g=v_ln_ffn_g, ln_ffn_b=v_ln_ffn_b, ffn_w_up=v_ffn_w_up, ffn_w_down=v_ffn_w_down)
    me = 4 * lax.axis_index("x") + 2 * lax.axis_index("y") + lax.axis_index("c")

    def flat(a):
        return a.reshape(-1, a.shape[-1])

    def unstacked(name):
        return wts[name][0] if wts[name].shape[0] == 1 else wts[name]

    shards = []
    for name in BIG:
        sh = unstacked(name)
        shards.append(_cast_bf16(flat(sh), name=f"cast_{name}").reshape(sh.shape))
    shards.append(hgrn_norm_g)
    full = _all_gather(shards, [SHARD_AXIS[nm] for nm in BIG] + [1])
    w_full = dict(zip(BIG + ("hgrn_norm_g",), full))
    sm = jax.nn.softmax(lb_logits, axis=0)
    csum = jnp.cumsum(sm, axis=0)
    small = dict(lb=(csum - csum[0:1])[1:2], lb_logits=lb_logits, ln_mix_g=ln_mix_g, ln_mix_b=ln_mix_b,
                 ln_ffn_g=ln_ffn_g, ln_ffn_b=ln_ffn_b)
    sq_sum, grad_x, grads, small_grads = _local_step(x[0], loss_target[0], w_full, small)
    loss = lax.psum(0.5 * sq_sum / x.shape[-1], ("x", "y", "c"))
    recv = _reduce_scatter([grads[nm] for nm in BIG] + [small_grads], [SHARD_AXIS[nm] for nm in BIG] + [None])
    out = {}
    for name, slabs in zip(BIG, recv[:-1]):
        shape = wts[name].shape
        res = _adamw(slabs.reshape(N_DEV, -1, shape[-1]), flat(wts[name]), flat(mom[name]), flat(vel[name]),
                     name=f"adamw_{name}")
        out[name] = [r.reshape(shape) for r in res]
    small_slabs = recv[-1]

    def stack_small(src):
        rows = [None] * SMALL_ROWS
        for name in SMALL:
            rows[SMALL_ROW[name]], rows[SMALL_ROW[name] + 1] = src[name][0:1], src[name][1:2]
        zero = jnp.zeros((1, x.shape[-1]), F32)
        return jnp.concatenate([zero if r is None else r for r in rows], axis=0)

    res = _adamw(small_slabs, stack_small(wts), stack_small(mom), stack_small(vel), name="adamw_small")
    for name in SMALL:
        out[name] = [r[SMALL_ROW[name]:SMALL_ROW[name] + 2] for r in res]
    ng = hgrn_norm_g.shape[-1]
    ng_slabs = lax.dynamic_slice(small_slabs, (0, NORM_G_ROW, me * ng), (N_DEV, 1, ng))
    out["hgrn_norm_g"] = _adamw(ng_slabs, hgrn_norm_g, m_hgrn_norm_g, v_hgrn_norm_g, name="adamw_norm_g")
    order = ("attn_w_in", "attn_w_out", "hgrn_w_in", "hgrn_w_out", "hgrn_norm_g", "lb_logits", "ln_mix_g", "ln_mix_b",
             "ln_ffn_g", "ln_ffn_b", "ffn_w_up", "ffn_w_down")
    return (loss, grad_x[None], *[out[nm][0] for nm in order], *[out[nm][1] for nm in order],
            *[out[nm][2] for nm in order], *[out[nm][3] for nm in order])
```

```python
import jax
import jax.numpy as jnp
from jax import lax
from jax.experimental import pallas as pl
from jax.experimental.pallas import tpu as pltpu

F32 = jnp.float32
BF16 = jnp.bfloat16

N_DEV = 8
LANES = 128
D_MODEL = 1024
ATTN_HEAD_DIM = 64
ATTN_HEADS = 16
ATTN_BLK = 128
DILATIONS = (1, 4, 16)
ROPE_THETA = 10000.0
HGRN_HEADS = 8
HGRN_CHUNK = 64
D_FF = 4096
LN_EPS = 1e-5
RMS_EPS = 1e-6
DEPTH = 2
ALPHA = (2 * DEPTH) ** 0.25
ADAM_LR, ADAM_B1, ADAM_B2, ADAM_EPS, ADAM_WD, ADAM_STEP = 0.001, 0.9, 0.999, 1e-08, 0.01, 10
VMEM_LIMIT = 48 * 1024 * 1024

_NT = (((1,), (1,)), ((), ()))
_TN = (((0,), (0,)), ((), ()))


def _dot(a, b):
    return jnp.dot(a, b, preferred_element_type=F32)


def _dot_nt(a, b):
    return lax.dot_general(a, b, _NT, preferred_element_type=F32)


def _dot_tn(a, b):
    return lax.dot_general(a, b, _TN, preferred_element_type=F32)


def _split3(x):
    p1 = x.astype(BF16)
    r1 = x - p1.astype(F32)
    p2 = r1.astype(BF16)
    p3 = (r1 - p2.astype(F32)).astype(BF16)
    return p1, p2, p3


def _exact_dot(sel, x):
    p1, p2, p3 = _split3(x)
    return _dot(sel, p1) + _dot(sel, p2) + _dot(sel, p3)


def _exact_dot_r(x, sel):
    p1, p2, p3 = _split3(x)
    return _dot(p1, sel) + _dot(p2, sel) + _dot(p3, sel)


def _params(sem=None):
    return pltpu.CompilerParams(dimension_semantics=sem, vmem_limit_bytes=VMEM_LIMIT)


def _mm(a, b, mode, *, name, m, n, k, tm=1024, tn=1024, tk=1024, out_dtypes=(F32,), epi=None, a_pre=None,
        tile_extras=(), row_extras=(), a_split=1, b_split=1, out_split=1,
        b_col_off=0, b_k_off=0, out_col_off=0, out_cols=None, alias=None, epi_wants_j=False):
    tm, tn, tk = min(tm, m), min(tn, n), min(tk, k)
    assert m % tm == 0 and n % tn == 0 and k % tk == 0, (name, m, n, k, tm, tn, tk)
    gm, gn, gk = m // tm, n // tn, k // tk
    if mode in ("nn", "nt"):
        if a_split > 1:
            kc = (k // a_split) // tk
            a_spec = pl.BlockSpec((None, tm, tk), lambda i, j, kk: (kk // kc, i, kk % kc))
        else:
            a_spec = pl.BlockSpec((tm, tk), lambda i, j, kk: (i, kk))
    else:
        a_spec = pl.BlockSpec((tk, tm), lambda i, j, kk: (kk, i))
    if mode in ("nn", "tn"):
        if b_split > 1:
            nc = (n // b_split) // tn
            b_spec = pl.BlockSpec((None, tk, tn), lambda i, j, kk: (j // nc, kk, j % nc))
        else:
            b_spec = pl.BlockSpec((tk, tn), lambda i, j, kk: (kk + b_k_off, j + b_col_off))
    else:
        b_spec = pl.BlockSpec((tn, tk), lambda i, j, kk: (j + b_col_off, kk + b_k_off))
    if out_split > 1:
        nco = (n // out_split) // tn
        o_spec = pl.BlockSpec((None, tm, tn), lambda i, j, kk: (j // nco, i, j % nco))
        o_shape = (out_split, m, n // out_split)
    else:
        o_spec = pl.BlockSpec((tm, tn), lambda i, j, kk: (i, j + out_col_off))
        o_shape = (m, out_cols if out_cols is not None else n)
    n_ex = len(tile_extras) + len(row_extras)
    n_out = len(out_dtypes)
    if epi is None:
        def epi(acc):
            return (acc,)
    dot = {"nn": _dot, "nt": _dot_nt, "tn": _dot_tn}[mode]

    def body(*refs):
        a_ref, b_ref = refs[0], refs[1]
        ex = refs[2:2 + n_ex]
        outs = refs[2 + n_ex + (1 if alias is not None else 0):][:n_out]
        jj = pl.program_id(1)

        def product():
            av = a_ref[...]
            if a_pre is not None:
                av = a_pre(av)
            return dot(av.astype(BF16), b_ref[...].astype(BF16))

        def finish(total):
            lead = (jj,) if epi_wants_j else ()
            res = epi(*lead, total, *[e[...] for e in ex])
            for o, r in zip(outs, res):
                o[...] = r.astype(o.dtype)

        if gk == 1:
            finish(product())
        else:
            acc = refs[-1]
            kk = pl.program_id(2)

            @pl.when(kk == 0)
            def _():
                acc[...] = product()

            @pl.when(kk > 0)
            def _():
                acc[...] += product()

            @pl.when(kk == gk - 1)
            def _():
                finish(acc[...])

    in_specs = [a_spec, b_spec] + [o_spec] * len(tile_extras)
    in_specs += [pl.BlockSpec((tm, r.shape[1]), lambda i, j, kk: (i, 0)) for r in row_extras]
    args = [a, b] + list(tile_extras) + list(row_extras)
    io_alias = {}
    if alias is not None:
        in_specs.append(pl.BlockSpec(memory_space=pl.ANY))
        args.append(alias)
        io_alias = {len(args) - 1: 0}
    out = pl.pallas_call(
        body, name=name, grid=(gm, gn, gk), in_specs=in_specs,
        out_specs=[o_spec] * n_out,
        out_shape=[jax.ShapeDtypeStruct(o_shape, dt) for dt in out_dtypes],
        scratch_shapes=[pltpu.VMEM((tm, tn), F32)] if gk > 1 else [],
        input_output_aliases=io_alias,
        compiler_params=_params(("parallel", "parallel", "arbitrary")),
    )(*args)
    return out[0] if n_out == 1 else out


def _rope_tables(seq, dil):
    pos = jnp.arange(seq, dtype=jnp.int32).reshape(seq // dil, dil).T.reshape(seq)
    half = ATTN_HEAD_DIM // 2
    inv = ROPE_THETA ** (-jnp.arange(half, dtype=F32) * (2.0 / ATTN_HEAD_DIM))
    ang = pos.astype(F32)[:, None] * inv[None, :]
    cos, sin = jnp.cos(ang), jnp.sin(ang)
    reps = LANES // ATTN_HEAD_DIM
    return (jnp.tile(jnp.concatenate([cos, cos], axis=1), (1, reps)),
            jnp.tile(jnp.concatenate([-sin, sin], axis=1), (1, reps)))


def _rotate(x, c, ss, sign=1.0):
    w = x.shape[-1]
    half = ATTN_HEAD_DIM // 2
    lane = lax.broadcasted_iota(jnp.int32, x.shape, 1)
    first = (lane % ATTN_HEAD_DIM) < half
    partner = jnp.where(first, pltpu.roll(x, w - half, 1), pltpu.roll(x, half, 1))
    reps = w // LANES
    if reps > 1:
        c = jnp.concatenate([c] * reps, axis=1)
        ss = jnp.concatenate([ss] * reps, axis=1)
    return x * c + sign * (partner * ss)


ROW_TILE = 512


def _ln_fwd(x, y, g, b, *, name):
    s, d = x.shape

    def body(x_ref, y_ref, g_ref, b_ref, o_ref, ob_ref, xh_ref, r_ref):
        u = ALPHA * x_ref[...] + y_ref[...]
        mu = jnp.mean(u, axis=-1, keepdims=True)
        uc = u - mu
        var = jnp.mean(uc * uc, axis=-1, keepdims=True)
        rstd = lax.rsqrt(var + LN_EPS)
        xh = uc * rstd
        out = xh * g_ref[...] + b_ref[...]
        o_ref[...] = out
        ob_ref[...] = out.astype(BF16)
        xh_ref[...] = xh
        r_ref[...] = rstd

    row = pl.BlockSpec((ROW_TILE, d), lambda i: (i, 0))
    vec = pl.BlockSpec((1, d), lambda i: (0, 0))
    return pl.pallas_call(
        body, name=name, grid=(s // ROW_TILE,), in_specs=[row, row, vec, vec],
        out_specs=[row, row, row, pl.BlockSpec((ROW_TILE, 1), lambda i: (i, 0))],
        out_shape=[jax.ShapeDtypeStruct((s, d), F32), jax.ShapeDtypeStruct((s, d), BF16),
                   jax.ShapeDtypeStruct((s, d), F32), jax.ShapeDtypeStruct((s, 1), F32)],
        compiler_params=_params(("parallel",)),
    )(x, y, g, b)


def _ln_bwd(dy, xhat, rstd, g, *, name):
    s, d = dy.shape

    def body(dy_ref, xh_ref, r_ref, g_ref, du_ref, dub_ref, dg_ref, db_ref):
        @pl.when(pl.program_id(0) == 0)
        def _():
            dg_ref[...] = jnp.zeros_like(dg_ref)
            db_ref[...] = jnp.zeros_like(db_ref)

        dyv = dy_ref[...]
        xh = xh_ref[...]
        dxh = dyv * g_ref[...]
        m1 = jnp.mean(dxh, axis=-1, keepdims=True)
        m2 = jnp.mean(dxh * xh, axis=-1, keepdims=True)
        du = r_ref[...] * (dxh - m1 - xh * m2)
        du_ref[...] = du
        dub_ref[...] = du.astype(BF16)
        dg_ref[...] += jnp.sum(dyv * xh, axis=0, keepdims=True)
        db_ref[...] += jnp.sum(dyv, axis=0, keepdims=True)

    row = pl.BlockSpec((ROW_TILE, d), lambda i: (i, 0))
    vec = pl.BlockSpec((1, d), lambda i: (0, 0))
    return pl.pallas_call(
        body, name=name, grid=(s // ROW_TILE,),
        in_specs=[row, row, pl.BlockSpec((ROW_TILE, 1), lambda i: (i, 0)), vec],
        out_specs=[row, row, vec, vec],
        out_shape=[jax.ShapeDtypeStruct((s, d), F32), jax.ShapeDtypeStruct((s, d), BF16),
                   jax.ShapeDtypeStruct((1, d), F32), jax.ShapeDtypeStruct((1, d), F32)],
        compiler_params=_params(("arbitrary",)),
    )(dy, xhat, rstd, g)


def _loss_head(y, target):
    s, d = y.shape

    def body(y_ref, t_ref, dy_ref, sq_ref):
        @pl.when(pl.program_id(0) == 0)
        def _():
            sq_ref[...] = jnp.zeros_like(sq_ref)

        e = y_ref[...] - t_ref[...]
        dy_ref[...] = e * (1.0 / d)
        sq_ref[...] += jnp.sum(e * e, axis=0, keepdims=True)

    row = pl.BlockSpec((ROW_TILE, d), lambda i: (i, 0))
    vec = pl.BlockSpec((1, d), lambda i: (0, 0))
    return pl.pallas_call(
        body, name="loss_head", grid=(s // ROW_TILE,), in_specs=[row, row], out_specs=[row, vec],
        out_shape=[jax.ShapeDtypeStruct((s, d), F32), jax.ShapeDtypeStruct((1, d), F32)],
        compiler_params=_params(("arbitrary",)),
    )(y, target)


POS_BLK = 2048


def _class_rows(r, dil):
    return pl.ds(r, POS_BLK // dil, stride=dil) if dil > 1 else pl.ds(0, POS_BLK)


def _class_view(a, dil):
    s, w = a.shape
    return a.reshape(dil, s // dil, w)


def _class_spec(dil, all_tiles=True):
    if all_tiles:
        return pl.BlockSpec((dil, POS_BLK // dil, LANES), lambda i, t: (0, i, t))
    return pl.BlockSpec((dil, POS_BLK // dil, LANES), lambda i, t: (0, i, 0))


def _pos_spec(all_tiles=True):
    if all_tiles:
        return pl.BlockSpec((POS_BLK, LANES), lambda i, t: (i, t))
    return pl.BlockSpec((POS_BLK, LANES), lambda i, t: (i, 0))


def _prep_x(x):
    s, d = x.shape

    def body(x_ref, *outs):
        for dil, o_ref in zip(DILATIONS, outs):
            for r in range(dil):
                o_ref[r] = x_ref[_class_rows(r, dil), :].astype(BF16)

    outs = pl.pallas_call(
        body, name="prep_x", grid=(s // POS_BLK, d // LANES),
        in_specs=[_pos_spec()], out_specs=[_class_spec(dil) for dil in DILATIONS],
        out_shape=[jax.ShapeDtypeStruct((dil, s // dil, d), BF16) for dil in DILATIONS],
        compiler_params=_params(("parallel", "parallel")),
    )(x)
    return [o.reshape(s, d) for o in outs]


def _head_expand_matrix():
    h = lax.broadcasted_iota(jnp.int32, (LANES, D_MODEL), 0)
    l = lax.broadcasted_iota(jnp.int32, (LANES, D_MODEL), 1)
    return (l // ATTN_HEAD_DIM == h).astype(BF16)


def _attn_fwd(qkv, dil, *, name):
    _, s, d = qkv.shape
    nq = s // ATTN_BLK
    per = nq // dil
    scale = ATTN_HEAD_DIM ** -0.5
    tiles = d // LANES

    def body(q_ref, kc_ref, kp_ref, vc_ref, vp_ref, o_ref, lse_ref):
        qb = pl.program_id(0)
        first = (qb % per) == 0
        qi = lax.broadcasted_iota(jnp.int32, (ATTN_BLK, 2 * ATTN_BLK), 0)
        kj = lax.broadcasted_iota(jnp.int32, (ATTN_BLK, 2 * ATTN_BLK), 1)
        dist = qi + ATTN_BLK - kj
        valid = (dist >= 0) & (dist <= ATTN_BLK) & ((kj >= ATTN_BLK) | jnp.logical_not(first))
        lane = lax.broadcasted_iota(jnp.int32, (ATTN_BLK, LANES), 1)
        lse_tile = jnp.zeros((ATTN_BLK, LANES), F32)
        for t in range(tiles):
            cols = pl.ds(t * LANES, LANES)
            q = q_ref[:, cols]
            k2 = jnp.concatenate([kp_ref[:, cols], kc_ref[:, cols]], axis=0)
            v2 = jnp.concatenate([vp_ref[:, cols], vc_ref[:, cols]], axis=0)
            o_tile = jnp.zeros((ATTN_BLK, LANES), F32)
            for hh in range(2):
                in_head = (lane // ATTN_HEAD_DIM) == hh
                qh = jnp.where(in_head, q, jnp.zeros_like(q))
                sc = _dot_nt(qh, k2) * scale
                sc = jnp.where(valid, sc, -jnp.inf)
                mx = jnp.max(sc, axis=-1, keepdims=True)
                p = jnp.exp(sc - mx)
                l = jnp.sum(p, axis=-1, keepdims=True)
                oh = _dot(p.astype(BF16), v2) / l
                o_tile = jnp.where(in_head, oh, o_tile)
                lse_tile = jnp.where(lane == 2 * t + hh, mx + jnp.log(l), lse_tile)
            o_ref[:, cols] = o_tile
        lse_ref[...] = lse_tile

    def blk(piece, prev):
        if prev:
            return pl.BlockSpec((None, ATTN_BLK, d), lambda i: (piece, jnp.maximum(i - 1, 0), 0))
        return pl.BlockSpec((None, ATTN_BLK, d), lambda i: (piece, i, 0))

    return pl.pallas_call(
        body, name=name, grid=(nq,),
        in_specs=[blk(0, False), blk(1, False), blk(1, True), blk(2, False), blk(2, True)],
        out_specs=[pl.BlockSpec((ATTN_BLK, d), lambda i: (i, 0)), pl.BlockSpec((ATTN_BLK, LANES), lambda i: (i, 0))],
        out_shape=[jax.ShapeDtypeStruct((s, d), F32), jax.ShapeDtypeStruct((s, LANES), F32)],
        compiler_params=_params(("parallel",)),
    )(qkv, qkv, qkv, qkv, qkv)


def _attn_combine(os_, lses):
    s, d = os_[0].shape
    sel = _head_expand_matrix()

    def body(o0, o1, o2, l0, l1, l2, sel_ref, of_ref, ob_ref, lt_ref, o_pos, l_pos):
        for g, (dil, o_ref, l_ref) in enumerate(zip(DILATIONS, (o0, o1, o2), (l0, l1, l2))):
            for r in range(dil):
                o_pos[g, _class_rows(r, dil), :] = o_ref[r]
                l_pos[g, _class_rows(r, dil), :] = l_ref[r]
        la, lb_, lc = l_pos[0], l_pos[1], l_pos[2]
        mx = jnp.maximum(jnp.maximum(la, lb_), lc)
        es = (jnp.exp(la - mx), jnp.exp(lb_ - mx), jnp.exp(lc - mx))
        z = es[0] + es[1] + es[2]
        lt_ref[...] = mx + jnp.log(z)
        acc = jnp.zeros((POS_BLK, LANES), F32)
        for g in range(3):
            acc += _exact_dot_r(es[g] / z, sel_ref[...]) * o_pos[g]
        of_ref[...] = acc
        ob_ref[...] = acc.astype(BF16)

    return pl.pallas_call(
        body, name="attn_combine", grid=(s // POS_BLK, d // LANES),
        in_specs=[_class_spec(dil) for dil in DILATIONS] + [_class_spec(dil, False) for dil in DILATIONS]
        + [pl.BlockSpec((LANES, LANES), lambda i, t: (0, t))],
        out_specs=[_pos_spec(), _pos_spec(), _pos_spec(False)],
        out_shape=[jax.ShapeDtypeStruct((s, d), F32), jax.ShapeDtypeStruct((s, d), BF16),
                   jax.ShapeDtypeStruct((s, LANES), F32)],
        scratch_shapes=[pltpu.VMEM((3, POS_BLK, LANES), F32), pltpu.VMEM((3, POS_BLK, LANES), F32)],
        compiler_params=_params(("parallel", "arbitrary")),
    )(*[_class_view(o, dil) for o, dil in zip(os_, DILATIONS)],
      *[_class_view(l, dil) for l, dil in zip(lses, DILATIONS)], sel)


def _attn_bwd_prep(do, o, lse):
    s, d = do.shape
    tiles = d // LANES
    sel_t = _head_expand_matrix().T

    def body(do_ref, o_ref, l_ref, sel_ref, *refs):
        outs, delta = refs[:9], refs[9]
        t = pl.program_id(1)
        part = _exact_dot_r(do_ref[...] * o_ref[...], sel_ref[...])

        @pl.when(t == 0)
        def _():
            delta[...] = part

        @pl.when(t > 0)
        def _():
            delta[...] += part

        for g, dil in enumerate(DILATIONS):
            for r in range(dil):
                outs[g][r] = do_ref[_class_rows(r, dil), :].astype(BF16)

        @pl.when(t == tiles - 1)
        def _():
            for g, dil in enumerate(DILATIONS):
                for r in range(dil):
                    outs[3 + g][r] = l_ref[_class_rows(r, dil), :]
                    outs[6 + g][r] = delta[_class_rows(r, dil), :]

    outs = pl.pallas_call(
        body, name="attn_bwd_prep", grid=(s // POS_BLK, tiles),
        in_specs=[_pos_spec(), _pos_spec(), _pos_spec(False), pl.BlockSpec((LANES, LANES), lambda i, t: (t, 0))],
        out_specs=[_class_spec(dil) for dil in DILATIONS] + [_class_spec(dil, False) for dil in DILATIONS] * 2,
        out_shape=[jax.ShapeDtypeStruct((dil, s // dil, d), BF16) for dil in DILATIONS]
        + [jax.ShapeDtypeStruct((dil, s // dil, LANES), F32) for dil in DILATIONS] * 2,
        scratch_shapes=[pltpu.VMEM((POS_BLK, LANES), F32)],
        compiler_params=_params(("parallel", "arbitrary")),
    )(do, o, lse, sel_t)
    flat = [a.reshape(s, a.shape[-1]) for a in outs]
    return flat[0:3], flat[3:6], flat[6:9]


def _attn_bwd(qkv, do, lse, delta, tables, dil, *, name):
    _, s, d = qkv.shape
    nq = s // ATTN_BLK
    per = nq // dil
    scale = ATTN_HEAD_DIM ** -0.5
    tiles = d // LANES

    def body(qd_ref, qo_ref, k_ref, v_ref, dod_ref, doo_ref, lsd_ref, lso_ref, dld_ref, dlo_ref,
             c_ref, ss_ref, out_ref, carry):
        kb = pl.program_id(0)

        @pl.when(kb == 0)
        def _():
            carry[...] = jnp.zeros_like(carry)

        has_next = (kb % per) != (per - 1)
        qi = lax.broadcasted_iota(jnp.int32, (ATTN_BLK, ATTN_BLK), 0)
        kj = lax.broadcasted_iota(jnp.int32, (ATTN_BLK, ATTN_BLK), 1)
        valid_d = kj <= qi
        valid_o = (kj >= qi) & has_next
        lane = lax.broadcasted_iota(jnp.int32, (ATTN_BLK, LANES), 1)
        c, ss = c_ref[...], ss_ref[...]
        for t in range(tiles):
            cols = pl.ds(t * LANES, LANES)
            k = k_ref[:, cols]
            v = v_ref[:, cols]
            dk = jnp.zeros((ATTN_BLK, LANES), F32)
            dv = jnp.zeros((ATTN_BLK, LANES), F32)
            dq_d = carry[:, cols]
            dq_o = jnp.zeros((ATTN_BLK, LANES), F32)
            for hh in range(2):
                head = 2 * t + hh
                in_head = (lane // ATTN_HEAD_DIM) == hh
                kh = jnp.where(in_head, k, jnp.zeros_like(k))
                for which in range(2):
                    q_ref, do_ref, ls_ref, dl_ref, valid = (
                        (qd_ref, dod_ref, lsd_ref, dld_ref, valid_d),
                        (qo_ref, doo_ref, lso_ref, dlo_ref, valid_o))[which]
                    q = q_ref[:, cols]
                    qh = jnp.where(in_head, q, jnp.zeros_like(q))
                    dob = do_ref[:, cols]
                    doh = jnp.where(in_head, dob, jnp.zeros_like(dob))
                    sc = _dot_nt(qh, k) * scale
                    p = jnp.where(valid, jnp.exp(sc - ls_ref[:, head:head + 1]), 0.0)
                    dp = _dot_nt(doh, v)
                    ds = (p * (dp - dl_ref[:, head:head + 1]) * scale).astype(BF16)
                    dv += _dot_tn(p.astype(BF16), doh)
                    dk += _dot_tn(ds, qh)
                    dq = _dot(ds, kh)
                    if which == 0:
                        dq_d += dq
                    else:
                        dq_o += dq
            carry[:, cols] = dq_o
            out_ref[0, :, cols] = _rotate(dq_d, c, ss, -1.0).astype(BF16)
            out_ref[1, :, cols] = _rotate(dk, c, ss, -1.0).astype(BF16)
            out_ref[2, :, cols] = dv.astype(BF16)

    def nxt(i):
        return jnp.minimum(i + 1, nq - 1)

    def piece(p, shift):
        if shift:
            return pl.BlockSpec((None, ATTN_BLK, d), lambda i: (p, nxt(i), 0))
        return pl.BlockSpec((None, ATTN_BLK, d), lambda i: (p, i, 0))

    def rows(width, shift):
        if shift:
            return pl.BlockSpec((ATTN_BLK, width), lambda i: (nxt(i), 0))
        return pl.BlockSpec((ATTN_BLK, width), lambda i: (i, 0))

    return pl.pallas_call(
        body, name=name, grid=(nq,),
        in_specs=[piece(0, False), piece(0, True), piece(1, False), piece(2, False),
                  rows(d, False), rows(d, True), rows(LANES, False), rows(LANES, True), rows(LANES, False),
                  rows(LANES, True), rows(LANES, False), rows(LANES, False)],
        out_specs=pl.BlockSpec((3, ATTN_BLK, d), lambda i: (0, i, 0)),
        out_shape=jax.ShapeDtypeStruct((3, s, d), BF16),
        scratch_shapes=[pltpu.VMEM((ATTN_BLK, d), F32)],
        compiler_params=_params(("arbitrary",)),
    )(qkv, qkv, qkv, qkv, do, do, lse, lse, delta, delta, *tables)


def _dx_combine(du, parts):
    s, d = du.shape

    def body(du_ref, p0, p1, p2, out_ref):
        out_ref[...] = ALPHA * du_ref[...] + p0[0]
        for dil, p_ref in zip(DILATIONS[1:], (p1, p2)):
            for r in range(dil):
                out_ref[_class_rows(r, dil), :] += p_ref[r]

    return pl.pallas_call(
        body, name="dx_combine", grid=(s // POS_BLK, d // LANES),
        in_specs=[_pos_spec()] + [_class_spec(dil) for dil in DILATIONS], out_specs=_pos_spec(),
        out_shape=jax.ShapeDtypeStruct((s, d), F32),
        compiler_params=_params(("parallel", "parallel")),
    )(du, *[_class_view(p, dil) for p, dil in zip(parts, DILATIONS)])


HGRN_ROWS = 512


def _tri(lower):
    i = lax.broadcasted_iota(jnp.int32, (HGRN_CHUNK, HGRN_CHUNK), 0)
    j = lax.broadcasted_iota(jnp.int32, (HGRN_CHUNK, HGRN_CHUNK), 1)
    return (j <= i) if lower else (j >= i)


def _hgrn_gates(qr, z, lb):
    sq = jax.nn.sigmoid(qr)
    q = qr * sq
    sg = jax.nn.sigmoid(z)
    f = lb + (1.0 - lb) * sg
    key = (1.0 - lb) * jax.nn.sigmoid(-z)
    return sq, q, sg, f, key


def _hgrn_fwd(p2, lb, norm_g):
    _, s, d = p2.shape
    nblk = s // HGRN_ROWS
    cps = HGRN_ROWS // HGRN_CHUNK

    def body(p_ref, lb_ref, g_ref, o_ref, on_ref, st_ref, state):
        @pl.when(pl.program_id(1) == 0)
        def _():
            state[...] = jnp.zeros_like(state)

        lbv = lb_ref[...]
        gv = g_ref[...]
        ltri = _tri(True)
        lsel = ltri.astype(BF16)
        for cidx in range(cps):
            rows = pl.ds(cidx * HGRN_CHUNK, HGRN_CHUNK)
            _, q, _, f, key = _hgrn_gates(p_ref[0, rows, :], p_ref[1, rows, :], lbv)
            v = p_ref[2, rows, :].astype(BF16)
            b = _exact_dot(lsel, jnp.log(f))
            b_last = b[HGRN_CHUNK - 1:HGRN_CHUNK, :]
            qd = (q * jnp.exp(b)).astype(BF16)
            kd = (key * jnp.exp(-b)).astype(BF16)
            k2 = (key * jnp.exp(b_last - b)).astype(BF16)
            st = state[...]
            st_ref[0, cidx] = st
            a = jnp.where(ltri, _dot_nt(qd, kd), 0.0)
            o = _dot(a.astype(BF16), v) + _dot_nt(qd, st.astype(BF16))
            state[...] = st * jnp.exp(b_last) + _dot_tn(v, k2)
            o_ref[rows, :] = o
            r = lax.rsqrt(jnp.mean(o * o, axis=-1, keepdims=True) + RMS_EPS)
            on_ref[rows, :] = (o * r * gv).astype(BF16)

    vec = pl.BlockSpec((1, LANES), lambda h, c: (0, h))
    col = pl.BlockSpec((HGRN_ROWS, LANES), lambda h, c: (c, h))
    return pl.pallas_call(
        body, name="hgrn_fwd", grid=(HGRN_HEADS, nblk),
        in_specs=[pl.BlockSpec((3, HGRN_ROWS, LANES), lambda h, c: (0, c, h)), vec, vec],
        out_specs=[col, col, pl.BlockSpec((1, cps, LANES, LANES), lambda h, c: (h, c, 0, 0))],
        out_shape=[jax.ShapeDtypeStruct((s, d), F32), jax.ShapeDtypeStruct((s, d), BF16),
                   jax.ShapeDtypeStruct((HGRN_HEADS, s // HGRN_CHUNK, LANES, LANES), F32)],
        scratch_shapes=[pltpu.VMEM((LANES, LANES), F32)],
        compiler_params=_params(("parallel", "arbitrary")),
    )(p2, lb, norm_g)


def _hgrn_bwd(p2, lb, norm_g, o_raw, states, dyn):
    _, s, d = p2.shape
    nblk = s // HGRN_ROWS
    cps = HGRN_ROWS // HGRN_CHUNK

    def body(p_ref, lb_ref, g_ref, o_ref, st_ref, dy_ref, dp_ref, dg_ref, dlb_ref, dstate):
        @pl.when(pl.program_id(1) == 0)
        def _():
            dstate[...] = jnp.zeros_like(dstate)
            dg_ref[...] = jnp.zeros_like(dg_ref)
            dlb_ref[...] = jnp.zeros_like(dlb_ref)

        lbv = lb_ref[...]
        gv = g_ref[...]
        ltri = _tri(True)
        lsel = ltri.astype(BF16)
        usel = _tri(False).astype(BF16)
        last_row = lax.broadcasted_iota(jnp.int32, (HGRN_CHUNK, LANES), 0) == HGRN_CHUNK - 1
        for cidx in reversed(range(cps)):
            rows = pl.ds(cidx * HGRN_CHUNK, HGRN_CHUNK)
            qr, z = p_ref[0, rows, :], p_ref[1, rows, :]
            sq, q, sg, f, key = _hgrn_gates(qr, z, lbv)
            v = p_ref[2, rows, :].astype(BF16)
            b = _exact_dot(lsel, jnp.log(f))
            b_last = b[HGRN_CHUNK - 1:HGRN_CHUNK, :]
            eb, enb, e2 = jnp.exp(b), jnp.exp(-b), jnp.exp(b_last - b)
            dec = jnp.exp(b_last)
            qd_f, kd_f, k2_f = q * eb, key * enb, key * e2
            qd, kd, k2 = qd_f.astype(BF16), kd_f.astype(BF16), k2_f.astype(BF16)
            a = jnp.where(ltri, _dot_nt(qd, kd), 0.0).astype(BF16)
            st = st_ref[0, cidx]
            dst = dstate[...]
            o = o_ref[rows, :]
            dyv = dy_ref[rows, :]
            r = lax.rsqrt(jnp.mean(o * o, axis=-1, keepdims=True) + RMS_EPS)
            oh = o * r
            doh = dyv * gv
            do_f = r * (doh - oh * jnp.mean(doh * oh, axis=-1, keepdims=True))
            dg_ref[...] += jnp.sum(dyv * oh, axis=0, keepdims=True)
            do = do_f.astype(BF16)
            dstb = dst.astype(BF16)
            da = jnp.where(ltri, _dot_nt(do, v), 0.0).astype(BF16)
            dv = _dot_tn(a, do) + _dot_nt(k2, dstb)
            dqd = _dot(da, kd) + _dot(do, st.astype(BF16))
            dkd = _dot_tn(da, qd)
            dk2 = _dot(v, dstb)
            ddec = jnp.sum(dst * st, axis=0, keepdims=True)
            dstate[...] = dst * dec + _dot_tn(do, qd)
            dq = dqd * eb
            dkey = dkd * enb + dk2 * e2
            db = dqd * qd_f - dkd * kd_f - dk2 * k2_f
            db_last = jnp.sum(dk2 * k2_f, axis=0, keepdims=True) + ddec * dec
            db = db + jnp.where(last_row, db_last, 0.0)
            dlf = _exact_dot(usel, db)
            one_m_s = 1.0 - sg
            dz = (dlf / f - dkey) * ((1.0 - lbv) * sg * one_m_s)
            dlb_ref[...] += jnp.sum(dlf * one_m_s / f - dkey * jax.nn.sigmoid(-z), axis=0, keepdims=True)
            dp_ref[0, rows, :] = (dq * (sq * (1.0 + qr * (1.0 - sq)))).astype(BF16)
            dp_ref[1, rows, :] = dz.astype(BF16)
            dp_ref[2, rows, :] = dv.astype(BF16)

    def rev(c):
        return nblk - 1 - c

    vec = pl.BlockSpec((1, LANES), lambda h, c: (0, h))
    col = pl.BlockSpec((HGRN_ROWS, LANES), lambda h, c: (rev(c), h))
    p3 = pl.BlockSpec((3, HGRN_ROWS, LANES), lambda h, c: (0, rev(c), h))
    return pl.pallas_call(
        body, name="hgrn_bwd", grid=(HGRN_HEADS, nblk),
        in_specs=[p3, vec, vec, col, pl.BlockSpec((1, cps, LANES, LANES), lambda h, c: (h, rev(c), 0, 0)), col],
        out_specs=[p3, vec, vec],
        out_shape=[jax.ShapeDtypeStruct((3, s, d), BF16), jax.ShapeDtypeStruct((1, d), F32),
                   jax.ShapeDtypeStruct((1, d), F32)],
        scratch_shapes=[pltpu.VMEM((LANES, LANES), F32)],
        compiler_params=_params(("parallel", "arbitrary")),
    )(p2, lb, norm_g, o_raw, states, dyn)


SMALL_ROWS = 16


def _relu2(h):
    r = jnp.maximum(h.astype(F32), 0.0)
    return r * r


def _dact_epi(acc, h):
    return (acc * (2.0 * jnp.maximum(h.astype(F32), 0.0)),)


def _res_epi(scale):
    def epi(acc, r):
        return (acc + scale * r,)
    return epi


def _rope_epi(j, acc, c, ss):
    return (jnp.where(j >= 2, acc, _rotate(acc, c, ss)),)


def _local_step(x, target, w, small):
    s, d = x.shape
    lb, norm_g = small["lb"], w["hgrn_norm_g"]
    gq = 3 * d
    gt = gq // 1024

    def ffn_fwd(xb, w_up, w_down, tag):
        h = _mm(xb, w_up, "nn", name=f"ffn_up_{tag}", m=s, n=D_FF, k=d, out_dtypes=(BF16,))
        y = _mm(h, w_down, "nn", name=f"ffn_down_{tag}", m=s, n=d, k=D_FF, a_pre=_relu2)
        return h, y

    def ffn_bwd(du, dub, xb, h, w_up, w_down, tag):
        dh = _mm(dub, w_down, "nt", name=f"ffn_dact_{tag}", m=s, n=D_FF, k=d, out_dtypes=(BF16,), epi=_dact_epi,
                 tile_extras=(h,))
        g_down = _mm(h, dub, "tn", name=f"ffn_gdown_{tag}", m=D_FF, n=d, k=s, a_pre=_relu2, out_dtypes=(BF16,))
        g_up = _mm(xb, dh, "tn", name=f"ffn_gup_{tag}", m=d, n=D_FF, k=s, out_dtypes=(BF16,))
        dx = _mm(dh, w_up, "nt", name=f"ffn_dx_{tag}", m=s, n=d, k=D_FF, epi=_res_epi(ALPHA), tile_extras=(du,))
        return dx, g_up, g_down

    xs = _prep_x(x)
    tabs, qkvs, o_parts, lse_parts = [], [], [], []
    for g, dil in enumerate(DILATIONS):
        tabs.append(_rope_tables(s, dil))
        qkv = _mm(xs[g], w["attn_w_in"], "nn", name=f"attn_in_{g}", m=s, n=gq, k=d, b_col_off=g * gt, out_split=3,
                  out_dtypes=(BF16,), epi=_rope_epi, epi_wants_j=True, row_extras=tabs[g])
        qkvs.append(qkv)
        o_g, lse_g = _attn_fwd(qkv, dil, name=f"attn_fwd_{g}")
        o_parts.append(o_g)
        lse_parts.append(lse_g)
    o_f, o_b, lse_t = _attn_combine(o_parts, lse_parts)
    y = _mm(o_b, w["attn_w_out"], "nn", name="attn_out", m=s, n=d, k=d)
    x1, x1b, xh1, r1 = _ln_fwd(x, y, small["ln_mix_g"][0:1], small["ln_mix_b"][0:1], name="ln_mix_0")
    h0, y = ffn_fwd(x1b, w["ffn_w_up"][0], w["ffn_w_down"][0], 0)
    x2, x2b, xh2, r2 = _ln_fwd(x1, y, small["ln_ffn_g"][0:1], small["ln_ffn_b"][0:1], name="ln_ffn_0")
    p2 = _mm(x2b, w["hgrn_w_in"], "nn", name="hgrn_in", m=s, n=3 * d, k=d, out_split=3)
    o_raw, o_n, states = _hgrn_fwd(p2, lb, norm_g)
    y = _mm(o_n, w["hgrn_w_out"], "nn", name="hgrn_out", m=s, n=d, k=d)
    x3, x3b, xh3, r3 = _ln_fwd(x2, y, small["ln_mix_g"][1:2], small["ln_mix_b"][1:2], name="ln_mix_1")
    h1, y = ffn_fwd(x3b, w["ffn_w_up"][1], w["ffn_w_down"][1], 1)
    x4, _, xh4, r4 = _ln_fwd(x3, y, small["ln_ffn_g"][1:2], small["ln_ffn_b"][1:2], name="ln_ffn_1")
    dy, sq = _loss_head(x4, target)
    du, dub, dg_ffn1, db_ffn1 = _ln_bwd(dy, xh4, r4, small["ln_ffn_g"][1:2], name="ln_ffn_1_bwd")
    dx3, g_up1, g_down1 = ffn_bwd(du, dub, x3b, h1, w["ffn_w_up"][1], w["ffn_w_down"][1], 1)
    du, dub, dg_mix1, db_mix1 = _ln_bwd(dx3, xh3, r3, small["ln_mix_g"][1:2], name="ln_mix_1_bwd")
    dyn = _mm(dub, w["hgrn_w_out"], "nt", name="hgrn_dout", m=s, n=d, k=d)
    g_hout = _mm(o_n, dub, "tn", name="hgrn_gout", m=d, n=d, k=s, out_dtypes=(BF16,))
    dp2, d_norm_g, d_lb = _hgrn_bwd(p2, lb, norm_g, o_raw, states, dyn)
    g_hin = _mm(x2b, dp2, "tn", name="hgrn_gin", m=d, n=3 * d, k=s, b_split=3, out_dtypes=(BF16,))
    dx2 = _mm(dp2, w["hgrn_w_in"], "nt", name="hgrn_dx", m=s, n=d, k=3 * d, a_split=3, epi=_res_epi(ALPHA),
              tile_extras=(du,))
    du, dub, dg_ffn0, db_ffn0 = _ln_bwd(dx2, xh2, r2, small["ln_ffn_g"][0:1], name="ln_ffn_0_bwd")
    dx1, g_up0, g_down0 = ffn_bwd(du, dub, x1b, h0, w["ffn_w_up"][0], w["ffn_w_down"][0], 0)
    du, dub, dg_mix0, db_mix0 = _ln_bwd(dx1, xh1, r1, small["ln_mix_g"][0:1], name="ln_mix_0_bwd")
    do = _mm(dub, w["attn_w_out"], "nt", name="attn_dout", m=s, n=d, k=d)
    g_aout = _mm(o_b, dub, "tn", name="attn_gout", m=d, n=d, k=s, out_dtypes=(BF16,))
    do_parts, ls_parts, dl_parts = _attn_bwd_prep(do, o_f, lse_t)
    g_ain, dx_parts = None, []
    for g, dil in enumerate(DILATIONS):
        dqkv = _attn_bwd(qkvs[g], do_parts[g], ls_parts[g], dl_parts[g], tabs[g], dil, name=f"attn_bwd_{g}")
        g_ain = _mm(xs[g], dqkv, "tn", name=f"attn_gin_{g}", m=d, n=gq, k=s, b_split=3, out_dtypes=(BF16,),
                    out_col_off=g * gt, out_cols=3 * gq, alias=g_ain)
        dx_parts.append(_mm(dqkv, w["attn_w_in"], "nt", name=f"attn_dx_{g}", m=s, n=d, k=gq, a_split=3,
                            b_k_off=g * gt))
    gx = _dx_combine(du, dx_parts)
    sm1 = jax.nn.softmax(small["lb_logits"], axis=0)
    d_l1 = d_lb * (sm1[0:1] * sm1[1:2])
    zeros = jnp.zeros((SMALL_ROWS - 11, d), F32)
    small_grads = jnp.concatenate([dg_mix0, dg_mix1, db_mix0, db_mix1, dg_ffn0, dg_ffn1, db_ffn0, db_ffn1,
                                   -d_l1, d_l1, d_norm_g, zeros], axis=0)
    grads = {
        "attn_w_in": g_ain, "attn_w_out": g_aout, "hgrn_w_in": g_hin, "hgrn_w_out": g_hout,
        "ffn_w_up": jnp.stack([g_up0, g_up1]), "ffn_w_down": jnp.stack([g_down0, g_down1]),
    }
    return jnp.sum(sq), gx, grads, small_grads


def _mesh_place():
    x, y, c = lax.axis_index("x"), lax.axis_index("y"), lax.axis_index("c")
    return x, y, c, 4 * x + 2 * y + c


def _peer(x, y, c, k):
    px = 1 - x if (k >> 2) & 1 else x
    py = 1 - y if (k >> 1) & 1 else y
    pc = 1 - c if k & 1 else c
    return (px, py, pc), 4 * px + 2 * py + pc


def _window(ref, axis, size, idx):
    if axis is None:
        return ref
    sl = [slice(None)] * len(ref.shape)
    sl[axis] = pl.ds(idx * size, size)
    return ref.at[tuple(sl)]


def _all_gather(shards, axes):
    n = len(shards)
    full_shapes = []
    for sh, ax in zip(shards, axes):
        shape = list(sh.shape)
        shape[ax] *= N_DEV
        full_shapes.append(jax.ShapeDtypeStruct(tuple(shape), sh.dtype))

    def body(*refs):
        srcs, outs = refs[:n], refs[n:2 * n]
        send_sems, recv_sems, local_sems = refs[2 * n:]
        x, y, c, me = _mesh_place()
        local = []
        for i in range(n):
            cp = pltpu.make_async_copy(srcs[i], _window(outs[i], axes[i], srcs[i].shape[axes[i]], me), local_sems.at[i])
            cp.start()
            local.append(cp)
        sends = []
        for k in range(1, N_DEV):
            peer, _ = _peer(x, y, c, k)
            for i in range(n):
                cp = pltpu.make_async_remote_copy(
                    src_ref=srcs[i], dst_ref=_window(outs[i], axes[i], srcs[i].shape[axes[i]], me),
                    send_sem=send_sems.at[i, k - 1], recv_sem=recv_sems.at[i, k - 1],
                    device_id=peer, device_id_type=pl.DeviceIdType.MESH)
                cp.start()
                sends.append(cp)
        for k in range(1, N_DEV):
            peer, pidx = _peer(x, y, c, k)
            for i in range(n):
                pltpu.make_async_remote_copy(
                    src_ref=srcs[i], dst_ref=_window(outs[i], axes[i], srcs[i].shape[axes[i]], pidx),
                    send_sem=send_sems.at[i, k - 1], recv_sem=recv_sems.at[i, k - 1],
                    device_id=peer, device_id_type=pl.DeviceIdType.MESH).wait_recv()
        for cp in sends:
            cp.wait_send()
        for cp in local:
            cp.wait()

    hbm = pl.BlockSpec(memory_space=pl.ANY)
    return pl.pallas_call(
        body, name="all_gather_weights", in_specs=[hbm] * n, out_specs=[hbm] * n, out_shape=full_shapes,
        scratch_shapes=[pltpu.SemaphoreType.DMA((n, N_DEV - 1)), pltpu.SemaphoreType.DMA((n, N_DEV - 1)),
                        pltpu.SemaphoreType.DMA((n,))],
    )(*shards)


def _reduce_scatter(fulls, axes):
    n = len(fulls)
    sizes, out_shapes = [], []
    for f, ax in zip(fulls, axes):
        shape = list(f.shape)
        if ax is not None:
            shape[ax] //= N_DEV
            sizes.append(shape[ax])
        else:
            sizes.append(None)
        out_shapes.append(jax.ShapeDtypeStruct((N_DEV, *shape), f.dtype))

    def body(*refs):
        srcs, outs = refs[:n], refs[n:2 * n]
        send_sems, recv_sems, local_sems = refs[2 * n:]
        x, y, c, me = _mesh_place()
        local = []
        for i in range(n):
            cp = pltpu.make_async_copy(_window(srcs[i], axes[i], sizes[i], me), outs[i].at[me], local_sems.at[i])
            cp.start()
            local.append(cp)
        sends = []
        for k in range(1, N_DEV):
            peer, pidx = _peer(x, y, c, k)
            for i in range(n):
                cp = pltpu.make_async_remote_copy(
                    src_ref=_window(srcs[i], axes[i], sizes[i], pidx), dst_ref=outs[i].at[me],
                    send_sem=send_sems.at[i, k - 1], recv_sem=recv_sems.at[i, k - 1],
                    device_id=peer, device_id_type=pl.DeviceIdType.MESH)
                cp.start()
                sends.append(cp)
        for k in range(1, N_DEV):
            peer, pidx = _peer(x, y, c, k)
            for i in range(n):
                pltpu.make_async_remote_copy(
                    src_ref=_window(srcs[i], axes[i], sizes[i], me), dst_ref=outs[i].at[pidx],
                    send_sem=send_sems.at[i, k - 1], recv_sem=recv_sems.at[i, k - 1],
                    device_id=peer, device_id_type=pl.DeviceIdType.MESH).wait_recv()
        for cp in sends:
            cp.wait_send()
        for cp in local:
            cp.wait()

    hbm = pl.BlockSpec(memory_space=pl.ANY)
    return pl.pallas_call(
        body, name="reduce_scatter_grads", in_specs=[hbm] * n, out_specs=[hbm] * n, out_shape=out_shapes,
        scratch_shapes=[pltpu.SemaphoreType.DMA((n, N_DEV - 1)), pltpu.SemaphoreType.DMA((n, N_DEV - 1)),
                        pltpu.SemaphoreType.DMA((n,))],
    )(*fulls)


def _cast_bf16(a, *, name):
    r, c = a.shape
    tr = min(r, 512)

    def body(a_ref, o_ref):
        o_ref[...] = a_ref[...].astype(BF16)

    spec = pl.BlockSpec((tr, c), lambda i: (i, 0))
    return pl.pallas_call(body, name=name, grid=(r // tr,), in_specs=[spec], out_specs=spec,
                          out_shape=jax.ShapeDtypeStruct((r, c), BF16), compiler_params=_params(("parallel",)))(a)


def _adamw(slabs, w, m, v, *, name):
    r, c = w.shape
    tr = min(r, 256)

    def body(s_ref, w_ref, m_ref, v_ref, g_ref, d_ref, mo_ref, vo_ref):
        g = s_ref[0].astype(F32)
        for i in range(1, N_DEV):
            g = g + s_ref[i].astype(F32)
        m2 = ADAM_B1 * m_ref[...] + (1.0 - ADAM_B1) * g
        v2 = ADAM_B2 * v_ref[...] + (1.0 - ADAM_B2) * (g * g)
        m_hat = m2 / (1.0 - ADAM_B1 ** ADAM_STEP)
        v_hat = v2 / (1.0 - ADAM_B2 ** ADAM_STEP)
        g_ref[...] = g
        d_ref[...] = -ADAM_LR * (m_hat / (jnp.sqrt(v_hat) + ADAM_EPS) + ADAM_WD * w_ref[...])
        mo_ref[...] = m2
        vo_ref[...] = v2

    spec = pl.BlockSpec((tr, c), lambda i: (i, 0))
    return pl.pallas_call(
        body, name=name, grid=(r // tr,),
        in_specs=[pl.BlockSpec((N_DEV, tr, c), lambda i: (0, i, 0)), spec, spec, spec],
        out_specs=[spec] * 4, out_shape=[jax.ShapeDtypeStruct((r, c), F32)] * 4,
        compiler_params=_params(("parallel",)),
    )(slabs, w, m, v)


BIG = ("attn_w_in", "attn_w_out", "hgrn_w_in", "hgrn_w_out", "ffn_w_up", "ffn_w_down")
SHARD_AXIS = {"attn_w_in": 1, "attn_w_out": 0, "hgrn_w_in": 1, "hgrn_w_out": 0, "ffn_w_up": 2, "ffn_w_down": 1}
SMALL = ("lb_logits", "ln_mix_g", "ln_mix_b", "ln_ffn_g", "ln_ffn_b")
SMALL_ROW = {"ln_mix_g": 0, "ln_mix_b": 2, "ln_ffn_g": 4, "ln_ffn_b": 6, "lb_logits": 8}
NORM_G_ROW = 10


def kernel(x, attn_w_in, attn_w_out, hgrn_w_in, hgrn_w_out, hgrn_norm_g, lb_logits, ln_mix_g, ln_mix_b, ln_ffn_g, ln_ffn_b, ffn_w_up, ffn_w_down, loss_target, m_attn_w_in, m_attn_w_out, m_hgrn_w_in, m_hgrn_w_out, m_hgrn_norm_g, m_lb_logits, m_ln_mix_g, m_ln_mix_b, m_ln_ffn_g, m_ln_ffn_b, m_ffn_w_up, m_ffn_w_down, v_attn_w_in, v_attn_w_out, v_hgrn_w_in, v_hgrn_w_out, v_hgrn_norm_g, v_lb_logits, v_ln_mix_g, v_ln_mix_b, v_ln_ffn_g, v_ln_ffn_b, v_ffn_w_up, v_ffn_w_down):
    wts = dict(attn_w_in=attn_w_in, attn_w_out=attn_w_out, hgrn_w_in=hgrn_w_in, hgrn_w_out=hgrn_w_out,
               hgrn_norm_g=hgrn_norm_g, lb_logits=lb_logits, ln_mix_g=ln_mix_g, ln_mix_b=ln_mix_b, ln_ffn_g=ln_ffn_g,
               ln_ffn_b=ln_ffn_b, ffn_w_up=ffn_w_up, ffn_w_down=ffn_w_down)
    mom = dict(attn_w_in=m_attn_w_in, attn_w_out=m_attn_w_out, hgrn_w_in=m_hgrn_w_in, hgrn_w_out=m_hgrn_w_out,
               hgrn_norm_g=m_hgrn_norm_g, lb_logits=m_lb_logits, ln_mix_g=m_ln_mix_g, ln_mix_b=m_ln_mix_b,
               ln_ffn_g=m_ln_ffn_g, ln_ffn_b=m_ln_ffn_b, ffn_w_up=m_ffn_w_up, ffn_w_down=m_ffn_w_down)
    vel = dict(attn_w_in=v_attn_w_in, attn_w_out=v_attn_w_out, hgrn_w_in=v_hgrn_w_in, hgrn_w_out=v_hgrn_w_out,
               hgrn_norm_g=v_hgrn_norm_g, lb_logits=v_lb_logits, ln_mix_g=v_ln_mix_g, ln_mix_b=v_ln_mix_b,
               ln_ffn_g=v_ln_ffn_g, ln_ffn_b=v_ln_ffn_b, ffn_w_up=v_ffn_w_up, ffn_w_down=v_ffn_w_down)
    me = 4 * lax.axis_index("x") + 2 * lax.axis_index("y") + lax.axis_index("c")

    def flat(a):
        return a.reshape(-1, a.shape[-1])

    def unstacked(name):
        return wts[name][0] if wts[name].shape[0] == 1 else wts[name]

    shards = []
    for name in BIG:
        sh = unstacked(name)
        shards.append(_cast_bf16(flat(sh), name=f"cast_{name}").reshape(sh.shape))
    shards.append(hgrn_norm_g)
    full = _all_gather(shards, [SHARD_AXIS[nm] for nm in BIG] + [1])
    w_full = dict(zip(BIG + ("hgrn_norm_g",), full))
    sm = jax.nn.softmax(lb_logits, axis=0)
    csum = jnp.cumsum(sm, axis=0)
    small = dict(lb=(csum - csum[0:1])[1:2], lb_logits=lb_logits, ln_mix_g=ln_mix_g, ln_mix_b=ln_mix_b,
                 ln_ffn_g=ln_ffn_g, ln_ffn_b=ln_ffn_b)
    sq_sum, grad_x, grads, small_grads = _local_step(x[0], loss_target[0], w_full, small)
    loss = lax.psum(0.5 * sq_sum / x.shape[-1], ("x", "y", "c"))
    recv = _reduce_scatter([grads[nm] for nm in BIG] + [small_grads], [SHARD_AXIS[nm] for nm in BIG] + [None])
    out = {}
    for name, slabs in zip(BIG, recv[:-1]):
        shape = wts[name].shape
        res = _adamw(slabs.reshape(N_DEV, -1, shape[-1]), flat(wts[name]), flat(mom[name]), flat(vel[name]),
                     name=f"adamw_{name}")
        out[name] = [r.reshape(shape) for r in res]
    small_slabs = recv[-1]

    def stack_small(src):
        rows = [None] * SMALL_ROWS
        for name in SMALL:
            rows[SMALL_ROW[name]], rows[SMALL_ROW[name] + 1] = src[name][0:1], src[name][1:2]
        zero = jnp.zeros((1, x.shape[-1]), F32)
        return jnp.concatenate([zero if r is None else r for r in rows], axis=0)

    res = _adamw(small_slabs, stack_small(wts), stack_small(mom), stack_small(vel), name="adamw_small")
    for name in SMALL:
        out[name] = [r[SMALL_ROW[name]:SMALL_ROW[name] + 2] for r in res]
    ng = hgrn_norm_g.shape[-1]
    ng_slabs = lax.dynamic_slice(small_slabs, (0, NORM_G_ROW, me * ng), (N_DEV, 1, ng))
    out["hgrn_norm_g"] = _adamw(ng_slabs, hgrn_norm_g, m_hgrn_norm_g, v_hgrn_norm_g, name="adamw_norm_g")
    order = ("attn_w_in", "attn_w_out", "hgrn_w_in", "hgrn_w_out", "hgrn_norm_g", "lb_logits", "ln_mix_g", "ln_mix_b",
             "ln_ffn_g", "ln_ffn_b", "ffn_w_up", "ffn_w_down")
    return (loss, grad_x[None], *[out[nm][0] for nm in order], *[out[nm][1] for nm in order],
            *[out[nm][2] for nm in order], *[out[nm][3] for nm in order])
```

```python
import jax
import jax.numpy as jnp
from jax import lax
from jax.experimental import pallas as pl
from jax.experimental.pallas import tpu as pltpu

F32 = jnp.float32
BF16 = jnp.bfloat16

N_DEV = 8
LANES = 128
D_MODEL = 1024
ATTN_HEAD_DIM = 64
ATTN_HEADS = 16
ATTN_BLK = 128
DILATIONS = (1, 4, 16)
ROPE_THETA = 10000.0
HGRN_HEADS = 8
HGRN_CHUNK = 64
D_FF = 4096
LN_EPS = 1e-5
RMS_EPS = 1e-6
DEPTH = 2
ALPHA = (2 * DEPTH) ** 0.25
ADAM_LR, ADAM_B1, ADAM_B2, ADAM_EPS, ADAM_WD, ADAM_STEP = 0.001, 0.9, 0.999, 1e-08, 0.01, 10
VMEM_LIMIT = 48 * 1024 * 1024

_NT = (((1,), (1,)), ((), ()))
_TN = (((0,), (0,)), ((), ()))


def _dot(a, b):
    return jnp.dot(a, b, preferred_element_type=F32)


def _dot_nt(a, b):
    return lax.dot_general(a, b, _NT, preferred_element_type=F32)


def _dot_tn(a, b):
    return lax.dot_general(a, b, _TN, preferred_element_type=F32)


def _split3(x):
    p1 = x.astype(BF16)
    r1 = x - p1.astype(F32)
    p2 = r1.astype(BF16)
    p3 = (r1 - p2.astype(F32)).astype(BF16)
    return p1, p2, p3


def _exact_dot(sel, x):
    p1, p2, p3 = _split3(x)
    return _dot(sel, p1) + _dot(sel, p2) + _dot(sel, p3)


def _exact_dot_r(x, sel):
    p1, p2, p3 = _split3(x)
    return _dot(p1, sel) + _dot(p2, sel) + _dot(p3, sel)


def _params(sem=None):
    return pltpu.CompilerParams(dimension_semantics=sem, vmem_limit_bytes=VMEM_LIMIT)


def _mm(a, b, mode, *, name, m, n, k, tm=1024, tn=1024, tk=1024, out_dtypes=(F32,), epi=None, a_pre=None,
        tile_extras=(), row_extras=(), a_split=1, b_split=1, out_split=1,
        b_col_off=0, b_k_off=0, out_col_off=0, out_cols=None, alias=None, epi_wants_j=False, deps=()):
    tm, tn, tk = min(tm, m), min(tn, n), min(tk, k)
    assert m % tm == 0 and n % tn == 0 and k % tk == 0, (name, m, n, k, tm, tn, tk)
    gm, gn, gk = m // tm, n // tn, k // tk
    if mode in ("nn", "nt"):
        if a_split > 1:
            kc = (k // a_split) // tk
            a_spec = pl.BlockSpec((None, tm, tk), lambda i, j, kk: (kk // kc, i, kk % kc))
        else:
            a_spec = pl.BlockSpec((tm, tk), lambda i, j, kk: (i, kk))
    else:
        a_spec = pl.BlockSpec((tk, tm), lambda i, j, kk: (kk, i))
    if mode in ("nn", "tn"):
        if b_split > 1:
            nc = (n // b_split) // tn
            b_spec = pl.BlockSpec((None, tk, tn), lambda i, j, kk: (j // nc, kk, j % nc))
        else:
            b_spec = pl.BlockSpec((tk, tn), lambda i, j, kk: (kk + b_k_off, j + b_col_off))
    else:
        b_spec = pl.BlockSpec((tn, tk), lambda i, j, kk: (j + b_col_off, kk + b_k_off))
    if out_split > 1:
        nco = (n // out_split) // tn
        o_spec = pl.BlockSpec((None, tm, tn), lambda i, j, kk: (j // nco, i, j % nco))
        o_shape = (out_split, m, n // out_split)
    else:
        o_spec = pl.BlockSpec((tm, tn), lambda i, j, kk: (i, j + out_col_off))
        o_shape = (m, out_cols if out_cols is not None else n)
    n_ex = len(tile_extras) + len(row_extras)
    n_out = len(out_dtypes)
    if epi is None:
        def epi(acc):
            return (acc,)
    dot = {"nn": _dot, "nt": _dot_nt, "tn": _dot_tn}[mode]

    def body(*refs):
        a_ref, b_ref = refs[0], refs[1]
        ex = refs[2:2 + n_ex]
        outs = refs[2 + n_ex + (1 if alias is not None else 0) + len(deps):][:n_out]
        jj = pl.program_id(1)

        def product():
            av = a_ref[...]
            if a_pre is not None:
                av = a_pre(av)
            return dot(av.astype(BF16), b_ref[...].astype(BF16))

        def finish(total):
            lead = (jj,) if epi_wants_j else ()
            res = epi(*lead, total, *[e[...] for e in ex])
            for o, r in zip(outs, res):
                o[...] = r.astype(o.dtype)

        if gk == 1:
            finish(product())
        else:
            acc = refs[-1]
            kk = pl.program_id(2)

            @pl.when(kk == 0)
            def _():
                acc[...] = product()

            @pl.when(kk > 0)
            def _():
                acc[...] += product()

            @pl.when(kk == gk - 1)
            def _():
                finish(acc[...])

    in_specs = [a_spec, b_spec] + [o_spec] * len(tile_extras)
    in_specs += [pl.BlockSpec((tm, r.shape[1]), lambda i, j, kk: (i, 0)) for r in row_extras]
    args = [a, b] + list(tile_extras) + list(row_extras)
    io_alias = {}
    if alias is not None:
        in_specs.append(pl.BlockSpec(memory_space=pl.ANY))
        args.append(alias)
        io_alias = {len(args) - 1: 0}
    in_specs += [pl.BlockSpec(memory_space=pl.ANY)] * len(deps)
    args += list(deps)
    out = pl.pallas_call(
        body, name=name, grid=(gm, gn, gk), in_specs=in_specs,
        out_specs=[o_spec] * n_out,
        out_shape=[jax.ShapeDtypeStruct(o_shape, dt) for dt in out_dtypes],
        scratch_shapes=[pltpu.VMEM((tm, tn), F32)] if gk > 1 else [],
        input_output_aliases=io_alias,
        compiler_params=_params(("parallel", "parallel", "arbitrary")),
    )(*args)
    return out[0] if n_out == 1 else out


def _rope_tables(seq, dil):
    pos = jnp.arange(seq, dtype=jnp.int32).reshape(seq // dil, dil).T.reshape(seq)
    half = ATTN_HEAD_DIM // 2
    inv = ROPE_THETA ** (-jnp.arange(half, dtype=F32) * (2.0 / ATTN_HEAD_DIM))
    ang = pos.astype(F32)[:, None] * inv[None, :]
    cos, sin = jnp.cos(ang), jnp.sin(ang)
    reps = LANES // ATTN_HEAD_DIM
    return (jnp.tile(jnp.concatenate([cos, cos], axis=1), (1, reps)),
            jnp.tile(jnp.concatenate([-sin, sin], axis=1), (1, reps)))


def _rotate(x, c, ss, sign=1.0):
    w = x.shape[-1]
    half = ATTN_HEAD_DIM // 2
    lane = lax.broadcasted_iota(jnp.int32, x.shape, 1)
    first = (lane % ATTN_HEAD_DIM) < half
    partner = jnp.where(first, pltpu.roll(x, w - half, 1), pltpu.roll(x, half, 1))
    reps = w // LANES
    if reps > 1:
        c = jnp.concatenate([c] * reps, axis=1)
        ss = jnp.concatenate([ss] * reps, axis=1)
    return x * c + sign * (partner * ss)


ROW_TILE = 512


def _ln_fwd(x, y, g, b, *, name):
    s, d = x.shape

    def body(x_ref, y_ref, g_ref, b_ref, o_ref, ob_ref, xh_ref, r_ref):
        u = ALPHA * x_ref[...] + y_ref[...]
        mu = jnp.mean(u, axis=-1, keepdims=True)
        uc = u - mu
        var = jnp.mean(uc * uc, axis=-1, keepdims=True)
        rstd = lax.rsqrt(var + LN_EPS)
        xh = uc * rstd
        out = xh * g_ref[...] + b_ref[...]
        o_ref[...] = out
        ob_ref[...] = out.astype(BF16)
        xh_ref[...] = xh
        r_ref[...] = rstd

    row = pl.BlockSpec((ROW_TILE, d), lambda i: (i, 0))
    vec = pl.BlockSpec((1, d), lambda i: (0, 0))
    return pl.pallas_call(
        body, name=name, grid=(s // ROW_TILE,), in_specs=[row, row, vec, vec],
        out_specs=[row, row, row, pl.BlockSpec((ROW_TILE, 1), lambda i: (i, 0))],
        out_shape=[jax.ShapeDtypeStruct((s, d), F32), jax.ShapeDtypeStruct((s, d), BF16),
                   jax.ShapeDtypeStruct((s, d), F32), jax.ShapeDtypeStruct((s, 1), F32)],
        compiler_params=_params(("parallel",)),
    )(x, y, g, b)


def _ln_bwd(dy, xhat, rstd, g, *, name):
    s, d = dy.shape

    def body(dy_ref, xh_ref, r_ref, g_ref, du_ref, dub_ref, dg_ref, db_ref):
        @pl.when(pl.program_id(0) == 0)
        def _():
            dg_ref[...] = jnp.zeros_like(dg_ref)
            db_ref[...] = jnp.zeros_like(db_ref)

        dyv = dy_ref[...]
        xh = xh_ref[...]
        dxh = dyv * g_ref[...]
        m1 = jnp.mean(dxh, axis=-1, keepdims=True)
        m2 = jnp.mean(dxh * xh, axis=-1, keepdims=True)
        du = r_ref[...] * (dxh - m1 - xh * m2)
        du_ref[...] = du
        dub_ref[...] = du.astype(BF16)
        dg_ref[...] += jnp.sum(dyv * xh, axis=0, keepdims=True)
        db_ref[...] += jnp.sum(dyv, axis=0, keepdims=True)

    row = pl.BlockSpec((ROW_TILE, d), lambda i: (i, 0))
    vec = pl.BlockSpec((1, d), lambda i: (0, 0))
    return pl.pallas_call(
        body, name=name, grid=(s // ROW_TILE,),
        in_specs=[row, row, pl.BlockSpec((ROW_TILE, 1), lambda i: (i, 0)), vec],
        out_specs=[row, row, vec, vec],
        out_shape=[jax.ShapeDtypeStruct((s, d), F32), jax.ShapeDtypeStruct((s, d), BF16),
                   jax.ShapeDtypeStruct((1, d), F32), jax.ShapeDtypeStruct((1, d), F32)],
        compiler_params=_params(("arbitrary",)),
    )(dy, xhat, rstd, g)


def _loss_head(y, target):
    s, d = y.shape

    def body(y_ref, t_ref, dy_ref, sq_ref):
        @pl.when(pl.program_id(0) == 0)
        def _():
            sq_ref[...] = jnp.zeros_like(sq_ref)

        e = y_ref[...] - t_ref[...]
        dy_ref[...] = e * (1.0 / d)
        sq_ref[...] += jnp.sum(e * e, axis=0, keepdims=True)

    row = pl.BlockSpec((ROW_TILE, d), lambda i: (i, 0))
    vec = pl.BlockSpec((1, d), lambda i: (0, 0))
    return pl.pallas_call(
        body, name="loss_head", grid=(s // ROW_TILE,), in_specs=[row, row], out_specs=[row, vec],
        out_shape=[jax.ShapeDtypeStruct((s, d), F32), jax.ShapeDtypeStruct((1, d), F32)],
        compiler_params=_params(("arbitrary",)),
    )(y, target)


POS_BLK = 2048


def _class_rows(r, dil):
    return pl.ds(r, POS_BLK // dil, stride=dil) if dil > 1 else pl.ds(0, POS_BLK)


def _class_view(a, dil):
    s, w = a.shape
    return a.reshape(dil, s // dil, w)


def _class_spec(dil, all_tiles=True):
    if all_tiles:
        return pl.BlockSpec((dil, POS_BLK // dil, LANES), lambda i, t: (0, i, t))
    return pl.BlockSpec((dil, POS_BLK // dil, LANES), lambda i, t: (0, i, 0))


def _pos_spec(all_tiles=True):
    if all_tiles:
        return pl.BlockSpec((POS_BLK, LANES), lambda i, t: (i, t))
    return pl.BlockSpec((POS_BLK, LANES), lambda i, t: (i, 0))


def _prep_x(x, deps=()):
    s, d = x.shape

    def body(x_ref, *refs):
        outs = refs[len(deps):]
        for dil, o_ref in zip(DILATIONS, outs):
            for r in range(dil):
                o_ref[r] = x_ref[_class_rows(r, dil), :].astype(BF16)

    outs = pl.pallas_call(
        body, name="prep_x", grid=(s // POS_BLK, d // LANES),
        in_specs=[_pos_spec()] + [pl.BlockSpec(memory_space=pl.ANY)] * len(deps),
        out_specs=[_class_spec(dil) for dil in DILATIONS],
        out_shape=[jax.ShapeDtypeStruct((dil, s // dil, d), BF16) for dil in DILATIONS],
        compiler_params=_params(("parallel", "parallel")),
    )(x, *deps)
    return [o.reshape(s, d) for o in outs]


def _head_expand_matrix():
    h = lax.broadcasted_iota(jnp.int32, (LANES, D_MODEL), 0)
    l = lax.broadcasted_iota(jnp.int32, (LANES, D_MODEL), 1)
    return (l // ATTN_HEAD_DIM == h).astype(BF16)


def _attn_fwd(qkv, dil, *, name):
    _, s, d = qkv.shape
    nq = s // ATTN_BLK
    per = nq // dil
    scale = ATTN_HEAD_DIM ** -0.5
    tiles = d // LANES

    def body(q_ref, kc_ref, kp_ref, vc_ref, vp_ref, o_ref, lse_ref):
        qb = pl.program_id(0)
        first = (qb % per) == 0
        qi = lax.broadcasted_iota(jnp.int32, (ATTN_BLK, 2 * ATTN_BLK), 0)
        kj = lax.broadcasted_iota(jnp.int32, (ATTN_BLK, 2 * ATTN_BLK), 1)
        dist = qi + ATTN_BLK - kj
        valid = (dist >= 0) & (dist <= ATTN_BLK) & ((kj >= ATTN_BLK) | jnp.logical_not(first))
        lane = lax.broadcasted_iota(jnp.int32, (ATTN_BLK, LANES), 1)
        lse_tile = jnp.zeros((ATTN_BLK, LANES), F32)
        for t in range(tiles):
            cols = pl.ds(t * LANES, LANES)
            q = q_ref[:, cols]
            k2 = jnp.concatenate([kp_ref[:, cols], kc_ref[:, cols]], axis=0)
            v2 = jnp.concatenate([vp_ref[:, cols], vc_ref[:, cols]], axis=0)
            o_tile = jnp.zeros((ATTN_BLK, LANES), F32)
            for hh in range(2):
                in_head = (lane // ATTN_HEAD_DIM) == hh
                qh = jnp.where(in_head, q, jnp.zeros_like(q))
                sc = _dot_nt(qh, k2) * scale
                sc = jnp.where(valid, sc, -jnp.inf)
                mx = jnp.max(sc, axis=-1, keepdims=True)
                p = jnp.exp(sc - mx)
                l = jnp.sum(p, axis=-1, keepdims=True)
                oh = _dot(p.astype(BF16), v2) / l
                o_tile = jnp.where(in_head, oh, o_tile)
                lse_tile = jnp.where(lane == 2 * t + hh, mx + jnp.log(l), lse_tile)
            o_ref[:, cols] = o_tile
        lse_ref[...] = lse_tile

    def blk(piece, prev):
        if prev:
            return pl.BlockSpec((None, ATTN_BLK, d), lambda i: (piece, jnp.maximum(i - 1, 0), 0))
        return pl.BlockSpec((None, ATTN_BLK, d), lambda i: (piece, i, 0))

    return pl.pallas_call(
        body, name=name, grid=(nq,),
        in_specs=[blk(0, False), blk(1, False), blk(1, True), blk(2, False), blk(2, True)],
        out_specs=[pl.BlockSpec((ATTN_BLK, d), lambda i: (i, 0)), pl.BlockSpec((ATTN_BLK, LANES), lambda i: (i, 0))],
        out_shape=[jax.ShapeDtypeStruct((s, d), F32), jax.ShapeDtypeStruct((s, LANES), F32)],
        compiler_params=_params(("parallel",)),
    )(qkv, qkv, qkv, qkv, qkv)


def _attn_combine(os_, lses):
    s, d = os_[0].shape
    sel = _head_expand_matrix()

    def body(o0, o1, o2, l0, l1, l2, sel_ref, of_ref, ob_ref, lt_ref, o_pos, l_pos):
        for g, (dil, o_ref, l_ref) in enumerate(zip(DILATIONS, (o0, o1, o2), (l0, l1, l2))):
            for r in range(dil):
                o_pos[g, _class_rows(r, dil), :] = o_ref[r]
                l_pos[g, _class_rows(r, dil), :] = l_ref[r]
        la, lb_, lc = l_pos[0], l_pos[1], l_pos[2]
        mx = jnp.maximum(jnp.maximum(la, lb_), lc)
        es = (jnp.exp(la - mx), jnp.exp(lb_ - mx), jnp.exp(lc - mx))
        z = es[0] + es[1] + es[2]
        lt_ref[...] = mx + jnp.log(z)
        acc = jnp.zeros((POS_BLK, LANES), F32)
        for g in range(3):
            acc += _exact_dot_r(es[g] / z, sel_ref[...]) * o_pos[g]
        of_ref[...] = acc
        ob_ref[...] = acc.astype(BF16)

    return pl.pallas_call(
        body, name="attn_combine", grid=(s // POS_BLK, d // LANES),
        in_specs=[_class_spec(dil) for dil in DILATIONS] + [_class_spec(dil, False) for dil in DILATIONS]
        + [pl.BlockSpec((LANES, LANES), lambda i, t: (0, t))],
        out_specs=[_pos_spec(), _pos_spec(), _pos_spec(False)],
        out_shape=[jax.ShapeDtypeStruct((s, d), F32), jax.ShapeDtypeStruct((s, d), BF16),
                   jax.ShapeDtypeStruct((s, LANES), F32)],
        scratch_shapes=[pltpu.VMEM((3, POS_BLK, LANES), F32), pltpu.VMEM((3, POS_BLK, LANES), F32)],
        compiler_params=_params(("parallel", "arbitrary")),
    )(*[_class_view(o, dil) for o, dil in zip(os_, DILATIONS)],
      *[_class_view(l, dil) for l, dil in zip(lses, DILATIONS)], sel)


def _attn_bwd_prep(do, o, lse):
    s, d = do.shape
    tiles = d // LANES
    sel_t = _head_expand_matrix().T

    def body(do_ref, o_ref, l_ref, sel_ref, *refs):
        outs, delta = refs[:9], refs[9]
        t = pl.program_id(1)
        part = _exact_dot_r(do_ref[...] * o_ref[...], sel_ref[...])

        @pl.when(t == 0)
        def _():
            delta[...] = part

        @pl.when(t > 0)
        def _():
            delta[...] += part

        for g, dil in enumerate(DILATIONS):
            for r in range(dil):
                outs[g][r] = do_ref[_class_rows(r, dil), :].astype(BF16)

        @pl.when(t == tiles - 1)
        def _():
            for g, dil in enumerate(DILATIONS):
                for r in range(dil):
                    outs[3 + g][r] = l_ref[_class_rows(r, dil), :]
                    outs[6 + g][r] = delta[_class_rows(r, dil), :]

    outs = pl.pallas_call(
        body, name="attn_bwd_prep", grid=(s // POS_BLK, tiles),
        in_specs=[_pos_spec(), _pos_spec(), _pos_spec(False), pl.BlockSpec((LANES, LANES), lambda i, t: (t, 0))],
        out_specs=[_class_spec(dil) for dil in DILATIONS] + [_class_spec(dil, False) for dil in DILATIONS] * 2,
        out_shape=[jax.ShapeDtypeStruct((dil, s // dil, d), BF16) for dil in DILATIONS]
        + [jax.ShapeDtypeStruct((dil, s // dil, LANES), F32) for dil in DILATIONS] * 2,
        scratch_shapes=[pltpu.VMEM((POS_BLK, LANES), F32)],
        compiler_params=_params(("parallel", "arbitrary")),
    )(do, o, lse, sel_t)
    flat = [a.reshape(s, a.shape[-1]) for a in outs]
    return flat[0:3], flat[3:6], flat[6:9]


def _attn_bwd(qkv, do, lse, delta, tables, dil, *, name):
    _, s, d = qkv.shape
    nq = s // ATTN_BLK
    per = nq // dil
    scale = ATTN_HEAD_DIM ** -0.5
    tiles = d // LANES

    def body(qd_ref, qo_ref, k_ref, v_ref, dod_ref, doo_ref, lsd_ref, lso_ref, dld_ref, dlo_ref,
             c_ref, ss_ref, out_ref, carry):
        kb = pl.program_id(0)

        @pl.when(kb == 0)
        def _():
            carry[...] = jnp.zeros_like(carry)

        has_next = (kb % per) != (per - 1)
        qi = lax.broadcasted_iota(jnp.int32, (ATTN_BLK, ATTN_BLK), 0)
        kj = lax.broadcasted_iota(jnp.int32, (ATTN_BLK, ATTN_BLK), 1)
        valid_d = kj <= qi
        valid_o = (kj >= qi) & has_next
        lane = lax.broadcasted_iota(jnp.int32, (ATTN_BLK, LANES), 1)
        c, ss = c_ref[...], ss_ref[...]
        for t in range(tiles):
            cols = pl.ds(t * LANES, LANES)
            k = k_ref[:, cols]
            v = v_ref[:, cols]
            dk = jnp.zeros((ATTN_BLK, LANES), F32)
            dv = jnp.zeros((ATTN_BLK, LANES), F32)
            dq_d = carry[:, cols]
            dq_o = jnp.zeros((ATTN_BLK, LANES), F32)
            for hh in range(2):
                head = 2 * t + hh
                in_head = (lane // ATTN_HEAD_DIM) == hh
                kh = jnp.where(in_head, k, jnp.zeros_like(k))
                for which in range(2):
                    q_ref, do_ref, ls_ref, dl_ref, valid = (
                        (qd_ref, dod_ref, lsd_ref, dld_ref, valid_d),
                        (qo_ref, doo_ref, lso_ref, dlo_ref, valid_o))[which]
                    q = q_ref[:, cols]
                    qh = jnp.where(in_head, q, jnp.zeros_like(q))
                    dob = do_ref[:, cols]
                    doh = jnp.where(in_head, dob, jnp.zeros_like(dob))
                    sc = _dot_nt(qh, k) * scale
                    p = jnp.where(valid, jnp.exp(sc - ls_ref[:, head:head + 1]), 0.0)
                    dp = _dot_nt(doh, v)
                    ds = (p * (dp - dl_ref[:, head:head + 1]) * scale).astype(BF16)
                    dv += _dot_tn(p.astype(BF16), doh)
                    dk += _dot_tn(ds, qh)
                    dq = _dot(ds, kh)
                    if which == 0:
                        dq_d += dq
                    else:
                        dq_o += dq
            carry[:, cols] = dq_o
            out_ref[0, :, cols] = _rotate(dq_d, c, ss, -1.0).astype(BF16)
            out_ref[1, :, cols] = _rotate(dk, c, ss, -1.0).astype(BF16)
            out_ref[2, :, cols] = dv.astype(BF16)

    def nxt(i):
        return jnp.minimum(i + 1, nq - 1)

    def piece(p, shift):
        if shift:
            return pl.BlockSpec((None, ATTN_BLK, d), lambda i: (p, nxt(i), 0))
        return pl.BlockSpec((None, ATTN_BLK, d), lambda i: (p, i, 0))

    def rows(width, shift):
        if shift:
            return pl.BlockSpec((ATTN_BLK, width), lambda i: (nxt(i), 0))
        return pl.BlockSpec((ATTN_BLK, width), lambda i: (i, 0))

    return pl.pallas_call(
        body, name=name, grid=(nq,),
        in_specs=[piece(0, False), piece(0, True), piece(1, False), piece(2, False),
                  rows(d, False), rows(d, True), rows(LANES, False), rows(LANES, True), rows(LANES, False),
                  rows(LANES, True), rows(LANES, False), rows(LANES, False)],
        out_specs=pl.BlockSpec((3, ATTN_BLK, d), lambda i: (0, i, 0)),
        out_shape=jax.ShapeDtypeStruct((3, s, d), BF16),
        scratch_shapes=[pltpu.VMEM((ATTN_BLK, d), F32)],
        compiler_params=_params(("arbitrary",)),
    )(qkv, qkv, qkv, qkv, do, do, lse, lse, delta, delta, *tables)


def _dx_combine(du, parts):
    s, d = du.shape

    def body(du_ref, p0, p1, p2, out_ref):
        out_ref[...] = ALPHA * du_ref[...] + p0[0]
        for dil, p_ref in zip(DILATIONS[1:], (p1, p2)):
            for r in range(dil):
                out_ref[_class_rows(r, dil), :] += p_ref[r]

    return pl.pallas_call(
        body, name="dx_combine", grid=(s // POS_BLK, d // LANES),
        in_specs=[_pos_spec()] + [_class_spec(dil) for dil in DILATIONS], out_specs=_pos_spec(),
        out_shape=jax.ShapeDtypeStruct((s, d), F32),
        compiler_params=_params(("parallel", "parallel")),
    )(du, *[_class_view(p, dil) for p, dil in zip(parts, DILATIONS)])


HGRN_ROWS = 512


def _tri(lower):
    i = lax.broadcasted_iota(jnp.int32, (HGRN_CHUNK, HGRN_CHUNK), 0)
    j = lax.broadcasted_iota(jnp.int32, (HGRN_CHUNK, HGRN_CHUNK), 1)
    return (j <= i) if lower else (j >= i)


def _hgrn_gates(qr, z, lb):
    sq = jax.nn.sigmoid(qr)
    q = qr * sq
    sg = jax.nn.sigmoid(z)
    f = lb + (1.0 - lb) * sg
    key = (1.0 - lb) * jax.nn.sigmoid(-z)
    return sq, q, sg, f, key


def _hgrn_fwd(p2, lb, norm_g):
    _, s, d = p2.shape
    nblk = s // HGRN_ROWS
    cps = HGRN_ROWS // HGRN_CHUNK

    def body(p_ref, lb_ref, g_ref, o_ref, on_ref, st_ref, state):
        @pl.when(pl.program_id(1) == 0)
        def _():
            state[...] = jnp.zeros_like(state)

        lbv = lb_ref[...]
        gv = g_ref[...]
        ltri = _tri(True)
        lsel = ltri.astype(BF16)
        for cidx in range(cps):
            rows = pl.ds(cidx * HGRN_CHUNK, HGRN_CHUNK)
            _, q, _, f, key = _hgrn_gates(p_ref[0, rows, :], p_ref[1, rows, :], lbv)
            v = p_ref[2, rows, :].astype(BF16)
            b = _exact_dot(lsel, jnp.log(f))
            b_last = b[HGRN_CHUNK - 1:HGRN_CHUNK, :]
            qd = (q * jnp.exp(b)).astype(BF16)
            kd = (key * jnp.exp(-b)).astype(BF16)
            k2 = (key * jnp.exp(b_last - b)).astype(BF16)
            st = state[...]
            st_ref[0, cidx] = st
            a = jnp.where(ltri, _dot_nt(qd, kd), 0.0)
            o = _dot(a.astype(BF16), v) + _dot_nt(qd, st.astype(BF16))
            state[...] = st * jnp.exp(b_last) + _dot_tn(v, k2)
            o_ref[rows, :] = o
            r = lax.rsqrt(jnp.mean(o * o, axis=-1, keepdims=True) + RMS_EPS)
            on_ref[rows, :] = (o * r * gv).astype(BF16)

    vec = pl.BlockSpec((1, LANES), lambda h, c: (0, h))
    col = pl.BlockSpec((HGRN_ROWS, LANES), lambda h, c: (c, h))
    return pl.pallas_call(
        body, name="hgrn_fwd", grid=(HGRN_HEADS, nblk),
        in_specs=[pl.BlockSpec((3, HGRN_ROWS, LANES), lambda h, c: (0, c, h)), vec, vec],
        out_specs=[col, col, pl.BlockSpec((1, cps, LANES, LANES), lambda h, c: (h, c, 0, 0))],
        out_shape=[jax.ShapeDtypeStruct((s, d), F32), jax.ShapeDtypeStruct((s, d), BF16),
                   jax.ShapeDtypeStruct((HGRN_HEADS, s // HGRN_CHUNK, LANES, LANES), F32)],
        scratch_shapes=[pltpu.VMEM((LANES, LANES), F32)],
        compiler_params=_params(("parallel", "arbitrary")),
    )(p2, lb, norm_g)


def _hgrn_bwd(p2, lb, norm_g, o_raw, states, dyn):
    _, s, d = p2.shape
    nblk = s // HGRN_ROWS
    cps = HGRN_ROWS // HGRN_CHUNK

    def body(p_ref, lb_ref, g_ref, o_ref, st_ref, dy_ref, dp_ref, dg_ref, dlb_ref, dstate):
        @pl.when(pl.program_id(1) == 0)
        def _():
            dstate[...] = jnp.zeros_like(dstate)
            dg_ref[...] = jnp.zeros_like(dg_ref)
            dlb_ref[...] = jnp.zeros_like(dlb_ref)

        lbv = lb_ref[...]
        gv = g_ref[...]
        ltri = _tri(True)
        lsel = ltri.astype(BF16)
        usel = _tri(False).astype(BF16)
        last_row = lax.broadcasted_iota(jnp.int32, (HGRN_CHUNK, LANES), 0) == HGRN_CHUNK - 1
        for cidx in reversed(range(cps)):
            rows = pl.ds(cidx * HGRN_CHUNK, HGRN_CHUNK)
            qr, z = p_ref[0, rows, :], p_ref[1, rows, :]
            sq, q, sg, f, key = _hgrn_gates(qr, z, lbv)
            v = p_ref[2, rows, :].astype(BF16)
            b = _exact_dot(lsel, jnp.log(f))
            b_last = b[HGRN_CHUNK - 1:HGRN_CHUNK, :]
            eb, enb, e2 = jnp.exp(b), jnp.exp(-b), jnp.exp(b_last - b)
            dec = jnp.exp(b_last)
            qd_f, kd_f, k2_f = q * eb, key * enb, key * e2
            qd, kd, k2 = qd_f.astype(BF16), kd_f.astype(BF16), k2_f.astype(BF16)
            a = jnp.where(ltri, _dot_nt(qd, kd), 0.0).astype(BF16)
            st = st_ref[0, cidx]
            dst = dstate[...]
            o = o_ref[rows, :]
            dyv = dy_ref[rows, :]
            r = lax.rsqrt(jnp.mean(o * o, axis=-1, keepdims=True) + RMS_EPS)
            oh = o * r
            doh = dyv * gv
            do_f = r * (doh - oh * jnp.mean(doh * oh, axis=-1, keepdims=True))
            dg_ref[...] += jnp.sum(dyv * oh, axis=0, keepdims=True)
            do = do_f.astype(BF16)
            dstb = dst.astype(BF16)
            da = jnp.where(ltri, _dot_nt(do, v), 0.0).astype(BF16)
            dv = _dot_tn(a, do) + _dot_nt(k2, dstb)
            dqd = _dot(da, kd) + _dot(do, st.astype(BF16))
            dkd = _dot_tn(da, qd)
            dk2 = _dot(v, dstb)
            ddec = jnp.sum(dst * st, axis=0, keepdims=True)
            dstate[...] = dst * dec + _dot_tn(do, qd)
            dq = dqd * eb
            dkey = dkd * enb + dk2 * e2
            db = dqd * qd_f - dkd * kd_f - dk2 * k2_f
            db_last = jnp.sum(dk2 * k2_f, axis=0, keepdims=True) + ddec * dec
            db = db + jnp.where(last_row, db_last, 0.0)
            dlf = _exact_dot(usel, db)
            one_m_s = 1.0 - sg
            dz = (dlf / f - dkey) * ((1.0 - lbv) * sg * one_m_s)
            dlb_ref[...] += jnp.sum(dlf * one_m_s / f - dkey * jax.nn.sigmoid(-z), axis=0, keepdims=True)
            dp_ref[0, rows, :] = (dq * (sq * (1.0 + qr * (1.0 - sq)))).astype(BF16)
            dp_ref[1, rows, :] = dz.astype(BF16)
            dp_ref[2, rows, :] = dv.astype(BF16)

    def rev(c):
        return nblk - 1 - c

    vec = pl.BlockSpec((1, LANES), lambda h, c: (0, h))
    col = pl.BlockSpec((HGRN_ROWS, LANES), lambda h, c: (rev(c), h))
    p3 = pl.BlockSpec((3, HGRN_ROWS, LANES), lambda h, c: (0, rev(c), h))
    return pl.pallas_call(
        body, name="hgrn_bwd", grid=(HGRN_HEADS, nblk),
        in_specs=[p3, vec, vec, col, pl.BlockSpec((1, cps, LANES, LANES), lambda h, c: (h, rev(c), 0, 0)), col],
        out_specs=[p3, vec, vec],
        out_shape=[jax.ShapeDtypeStruct((3, s, d), BF16), jax.ShapeDtypeStruct((1, d), F32),
                   jax.ShapeDtypeStruct((1, d), F32)],
        scratch_shapes=[pltpu.VMEM((LANES, LANES), F32)],
        compiler_params=_params(("parallel", "arbitrary")),
    )(p2, lb, norm_g, o_raw, states, dyn)


SMALL_ROWS = 16


def _relu2(h):
    r = jnp.maximum(h.astype(F32), 0.0)
    return r * r


def _dact_epi(acc, h):
    return (acc * (2.0 * jnp.maximum(h.astype(F32), 0.0)),)


def _res_epi(scale):
    def epi(acc, r):
        return (acc + scale * r,)
    return epi


def _rope_epi(j, acc, c, ss):
    return (jnp.where(j >= 2, acc, _rotate(acc, c, ss)),)


def _local_step(x, target, get_w, small, on_grads, deps=()):
    s, d = x.shape
    lb = small["lb"]
    gq = 3 * d
    gt = gq // 1024

    def ffn_fwd(xb, w_up, w_down, tag):
        h = _mm(xb, w_up, "nn", name=f"ffn_up_{tag}", m=s, n=D_FF, k=d, out_dtypes=(BF16,))
        y = _mm(h, w_down, "nn", name=f"ffn_down_{tag}", m=s, n=d, k=D_FF, a_pre=_relu2)
        return h, y

    def ffn_bwd(du, dub, xb, h, w_up, w_down, tag):
        dh = _mm(dub, w_down, "nt", name=f"ffn_dact_{tag}", m=s, n=D_FF, k=d, out_dtypes=(BF16,), epi=_dact_epi,
                 tile_extras=(h,))
        g_down = _mm(h, dub, "tn", name=f"ffn_gdown_{tag}", m=D_FF, n=d, k=s, a_pre=_relu2, out_dtypes=(BF16,))
        g_up = _mm(xb, dh, "tn", name=f"ffn_gup_{tag}", m=d, n=D_FF, k=s, out_dtypes=(BF16,))
        after = on_grads(f"ffn{tag}", {f"ffn_w_down{tag}": g_down, f"ffn_w_up{tag}": g_up})
        return _mm(dh, w_up, "nt", name=f"ffn_dx_{tag}", m=s, n=d, k=D_FF, epi=_res_epi(ALPHA), tile_extras=(du,),
                   deps=after)

    xs = _prep_x(x, deps)
    w_ain = get_w("attn_w_in", None)
    tabs, qkvs, o_parts, lse_parts = [], [], [], []
    for g, dil in enumerate(DILATIONS):
        tabs.append(_rope_tables(s, dil))
        qkv = _mm(xs[g], w_ain, "nn", name=f"attn_in_{g}", m=s, n=gq, k=d, b_col_off=g * gt, out_split=3,
                  out_dtypes=(BF16,), epi=_rope_epi, epi_wants_j=True, row_extras=tabs[g])
        qkvs.append(qkv)
        o_g, lse_g = _attn_fwd(qkv, dil, name=f"attn_fwd_{g}")
        o_parts.append(o_g)
        lse_parts.append(lse_g)
    o_f, o_b, lse_t = _attn_combine(o_parts, lse_parts)
    w_aout = get_w("attn_w_out", o_b)
    y = _mm(o_b, w_aout, "nn", name="attn_out", m=s, n=d, k=d)
    x1, x1b, xh1, r1 = _ln_fwd(x, y, small["ln_mix_g"][0:1], small["ln_mix_b"][0:1], name="ln_mix_0")
    w_up0, w_down0 = get_w("ffn_w_up0", o_b), get_w("ffn_w_down0", o_b)
    h0, y = ffn_fwd(x1b, w_up0, w_down0, 0)
    x2, x2b, xh2, r2 = _ln_fwd(x1, y, small["ln_ffn_g"][0:1], small["ln_ffn_b"][0:1], name="ln_ffn_0")
    w_hin, w_hout, norm_g = get_w("hgrn_w_in", x2b), get_w("hgrn_w_out", x2b), get_w("hgrn_norm_g", x2b)
    w_up1, w_down1 = get_w("ffn_w_up1", x2b), get_w("ffn_w_down1", x2b)
    p2 = _mm(x2b, w_hin, "nn", name="hgrn_in", m=s, n=3 * d, k=d, out_split=3)
    o_raw, o_n, states = _hgrn_fwd(p2, lb, norm_g)
    y = _mm(o_n, w_hout, "nn", name="hgrn_out", m=s, n=d, k=d)
    x3, x3b, xh3, r3 = _ln_fwd(x2, y, small["ln_mix_g"][1:2], small["ln_mix_b"][1:2], name="ln_mix_1")
    h1, y = ffn_fwd(x3b, w_up1, w_down1, 1)
    x4, _, xh4, r4 = _ln_fwd(x3, y, small["ln_ffn_g"][1:2], small["ln_ffn_b"][1:2], name="ln_ffn_1")
    dy, sq = _loss_head(x4, target)
    du, dub, dg_ffn1, db_ffn1 = _ln_bwd(dy, xh4, r4, small["ln_ffn_g"][1:2], name="ln_ffn_1_bwd")
    dx3 = ffn_bwd(du, dub, x3b, h1, w_up1, w_down1, 1)
    du, dub, dg_mix1, db_mix1 = _ln_bwd(dx3, xh3, r3, small["ln_mix_g"][1:2], name="ln_mix_1_bwd")
    dyn = _mm(dub, w_hout, "nt", name="hgrn_dout", m=s, n=d, k=d)
    g_hout = _mm(o_n, dub, "tn", name="hgrn_gout", m=d, n=d, k=s, out_dtypes=(BF16,))
    dp2, d_norm_g, d_lb = _hgrn_bwd(p2, lb, norm_g, o_raw, states, dyn)
    g_hin = _mm(x2b, dp2, "tn", name="hgrn_gin", m=d, n=3 * d, k=s, b_split=3, out_dtypes=(BF16,))
    after = on_grads("hgrn", {"hgrn_w_out": g_hout, "hgrn_w_in": g_hin})
    dx2 = _mm(dp2, w_hin, "nt", name="hgrn_dx", m=s, n=d, k=3 * d, a_split=3, epi=_res_epi(ALPHA),
              tile_extras=(du,), deps=after)
    du, dub, dg_ffn0, db_ffn0 = _ln_bwd(dx2, xh2, r2, small["ln_ffn_g"][0:1], name="ln_ffn_0_bwd")
    dx1 = ffn_bwd(du, dub, x1b, h0, w_up0, w_down0, 0)
    du, dub, dg_mix0, db_mix0 = _ln_bwd(dx1, xh1, r1, small["ln_mix_g"][0:1], name="ln_mix_0_bwd")
    g_aout = _mm(o_b, dub, "tn", name="attn_gout", m=d, n=d, k=s, out_dtypes=(BF16,))
    sm1 = jax.nn.softmax(small["lb_logits"], axis=0)
    d_l1 = d_lb * (sm1[0:1] * sm1[1:2])
    zeros = jnp.zeros((SMALL_ROWS - 11, d), F32)
    small_grads = jnp.concatenate([dg_mix0, dg_mix1, db_mix0, db_mix1, dg_ffn0, dg_ffn1, db_ffn0, db_ffn1,
                                   -d_l1, d_l1, d_norm_g, zeros], axis=0)
    after = on_grads("attn_out", {"attn_w_out": g_aout, "small": small_grads})
    do = _mm(dub, w_aout, "nt", name="attn_dout", m=s, n=d, k=d, deps=after)
    do_parts, ls_parts, dl_parts = _attn_bwd_prep(do, o_f, lse_t)
    g_ain, dqkvs = None, []
    for g, dil in enumerate(DILATIONS):
        dqkvs.append(_attn_bwd(qkvs[g], do_parts[g], ls_parts[g], dl_parts[g], tabs[g], dil, name=f"attn_bwd_{g}"))
        g_ain = _mm(xs[g], dqkvs[g], "tn", name=f"attn_gin_{g}", m=d, n=gq, k=s, b_split=3, out_dtypes=(BF16,),
                    out_col_off=g * gt, out_cols=3 * gq, alias=g_ain)
    after = on_grads("attn_in", {"attn_w_in": g_ain})
    dx_parts = [_mm(dqkvs[g], w_ain, "nt", name=f"attn_dx_{g}", m=s, n=d, k=gq, a_split=3, b_k_off=g * gt,
                    deps=after if g == 0 else ())
                for g in range(len(DILATIONS))]
    return jnp.sum(sq), _dx_combine(du, dx_parts)


def _mesh_place():
    x, y, c = lax.axis_index("x"), lax.axis_index("y"), lax.axis_index("c")
    return x, y, c, 4 * x + 2 * y + c


def _peer(x, y, c, k):
    px = 1 - x if (k >> 2) & 1 else x
    py = 1 - y if (k >> 1) & 1 else y
    pc = 1 - c if k & 1 else c
    return (px, py, pc), 4 * px + 2 * py + pc


def _window(ref, axis, size, idx):
    if axis is None:
        return ref
    sl = [slice(None)] * len(ref.shape)
    sl[axis] = pl.ds(idx * size, size)
    return ref.at[tuple(sl)]


_HBM = pl.BlockSpec(memory_space=pltpu.HBM)
_SEM = pl.BlockSpec(memory_space=pltpu.SEMAPHORE)
_EFFECT = pltpu.SideEffectType.DATAFLOW_SIDE_EFFECTING


def _xchg_ends(src_ref, land_ref, axis, gather, me, other):
    if gather:
        size = src_ref.shape[axis]
        return src_ref, _window(land_ref, axis, size, me), _window(land_ref, axis, size, other)
    size = None if axis is None else src_ref.shape[axis] // N_DEV
    return _window(src_ref, axis, size, other), land_ref.at[me], land_ref.at[other]


def _xchg_start(srcs, lands, axes, *, gather, name):
    n = len(srcs)

    def body(*refs):
        src_refs, land_refs = refs[:n], refs[n:2 * n]
        send, recv = refs[2 * n:3 * n], refs[3 * n:4 * n]
        token = refs[-1]
        x, y, c, me = _mesh_place()
        for k in range(1, N_DEV):
            peer, pidx = _peer(x, y, c, k)
            for i in range(n):
                src, dst, _ = _xchg_ends(src_refs[i], land_refs[i], axes[i], gather, me, pidx)
                pltpu.make_async_remote_copy(
                    src_ref=src, dst_ref=dst, send_sem=send[i].at[k - 1], recv_sem=recv[i].at[k - 1],
                    device_id=peer, device_id_type=pl.DeviceIdType.MESH).start()
        token[...] = jnp.zeros_like(token)

    bufs = list(srcs) + list(lands)
    outs = pl.pallas_call(
        body, name=name,
        out_shape=[pltpu.SemaphoreType.DMA((N_DEV - 1,))] * (2 * n) + [pltpu.HBM(b.shape, b.dtype) for b in bufs]
        + [jax.ShapeDtypeStruct((8, LANES), F32)],
        in_specs=[_HBM] * (2 * n),
        out_specs=[_SEM] * (2 * n) + [_HBM] * (2 * n) + [pl.BlockSpec(memory_space=pltpu.VMEM)],
        input_output_aliases={i: 2 * n + i for i in range(2 * n)},
        compiler_params=pltpu.CompilerParams(has_side_effects=_EFFECT),
    )(*[pltpu.with_memory_space_constraint(b, pltpu.HBM) for b in bufs])
    return dict(send=outs[:n], recv=outs[n:2 * n], srcs=outs[2 * n:3 * n], lands=outs[3 * n:4 * n], token=outs[-1],
                axes=list(axes), gather=gather)


def _xchg_wait(xc, items, after, *, name):
    m = len(items)
    gather = xc["gather"]
    axes = [xc["axes"][i] for i in items]

    def body(*refs):
        src_refs, land_refs = refs[:m], refs[m:2 * m]
        send, recv = refs[2 * m:3 * m], refs[3 * m:4 * m]
        x, y, c, me = _mesh_place()
        for k in range(1, N_DEV):
            peer, pidx = _peer(x, y, c, k)
            for j in range(m):
                src, dst, got = _xchg_ends(src_refs[j], land_refs[j], axes[j], gather, me, pidx)
                pltpu.make_async_remote_copy(
                    src_ref=src, dst_ref=dst, send_sem=send[j].at[k - 1], recv_sem=recv[j].at[k - 1],
                    device_id=peer, device_id_type=pl.DeviceIdType.MESH).wait_send()
                pltpu.make_async_remote_copy(
                    src_ref=src, dst_ref=got, send_sem=send[j].at[k - 1], recv_sem=recv[j].at[k - 1],
                    device_id=peer, device_id_type=pl.DeviceIdType.MESH).wait_recv()

    bufs = [xc["srcs"][i] for i in items] + [xc["lands"][i] for i in items]
    sems = [xc["send"][i] for i in items] + [xc["recv"][i] for i in items]
    outs = pl.pallas_call(
        body, name=name, out_shape=[pltpu.HBM(b.shape, b.dtype) for b in bufs],
        in_specs=[_HBM] * (2 * m) + [_SEM] * (2 * m) + [pl.BlockSpec(memory_space=pl.ANY)],
        out_specs=[_HBM] * (2 * m), input_output_aliases={j: j for j in range(2 * m)},
        compiler_params=pltpu.CompilerParams(has_side_effects=_EFFECT),
    )(*bufs, *sems, after)
    return outs[m:]


def _own_block(a, axis, me):
    if axis is None:
        return a[None]
    size = a.shape[axis] // N_DEV
    return lax.dynamic_slice_in_dim(a, me * size, size, axis)[None]


def _cast_bf16(a, *, name):
    r, c = a.shape
    tr = min(r, 512)

    def body(a_ref, o_ref):
        o_ref[...] = a_ref[...].astype(BF16)

    spec = pl.BlockSpec((tr, c), lambda i: (i, 0))
    return pl.pallas_call(body, name=name, grid=(r // tr,), in_specs=[spec], out_specs=spec,
                          out_shape=jax.ShapeDtypeStruct((r, c), BF16), compiler_params=_params(("parallel",)))(a)


def _adamw(slabs, w, m, v, *, name):
    layers, r, c = w.shape
    tr = min(r, 256)

    def body(*refs):
        s_refs = refs[:layers]
        w_ref, m_ref, v_ref, g_ref, d_ref, mo_ref, vo_ref = refs[layers:]
        for l in range(layers):
            g = s_refs[l][0].astype(F32)
            for i in range(1, N_DEV):
                g = g + s_refs[l][i].astype(F32)
            m2 = ADAM_B1 * m_ref[l] + (1.0 - ADAM_B1) * g
            v2 = ADAM_B2 * v_ref[l] + (1.0 - ADAM_B2) * (g * g)
            m_hat = m2 / (1.0 - ADAM_B1 ** ADAM_STEP)
            v_hat = v2 / (1.0 - ADAM_B2 ** ADAM_STEP)
            g_ref[l] = g
            d_ref[l] = -ADAM_LR * (m_hat / (jnp.sqrt(v_hat) + ADAM_EPS) + ADAM_WD * w_ref[l])
            mo_ref[l] = m2
            vo_ref[l] = v2

    spec = pl.BlockSpec((layers, tr, c), lambda i: (0, i, 0))
    return pl.pallas_call(
        body, name=name, grid=(r // tr,),
        in_specs=[pl.BlockSpec((N_DEV, tr, c), lambda i: (0, i, 0))] * layers + [spec, spec, spec],
        out_specs=[spec] * 4, out_shape=[jax.ShapeDtypeStruct((layers, r, c), F32)] * 4,
        compiler_params=_params(("parallel",)),
    )(*slabs, w, m, v)


GATHER_AXIS = {"attn_w_in": 1, "attn_w_out": 0, "ffn_w_up0": 1, "ffn_w_down0": 0, "hgrn_w_in": 1, "hgrn_w_out": 0,
               "hgrn_norm_g": 1, "ffn_w_up1": 1, "ffn_w_down1": 0}
GATHER_WAITS = (("attn_w_in",), ("attn_w_out", "ffn_w_up0", "ffn_w_down0"),
                ("hgrn_w_in", "hgrn_w_out", "hgrn_norm_g", "ffn_w_up1", "ffn_w_down1"))
SCATTER_AXIS = dict(GATHER_AXIS, small=None)
BIG = ("attn_w_in", "attn_w_out", "hgrn_w_in", "hgrn_w_out", "ffn_w_up", "ffn_w_down")
SMALL = ("lb_logits", "ln_mix_g", "ln_mix_b", "ln_ffn_g", "ln_ffn_b")
SMALL_ROW = {"ln_mix_g": 0, "ln_mix_b": 2, "ln_ffn_g": 4, "ln_ffn_b": 6, "lb_logits": 8}
NORM_G_ROW = 10


def kernel(x, attn_w_in, attn_w_out, hgrn_w_in, hgrn_w_out, hgrn_norm_g, lb_logits, ln_mix_g, ln_mix_b, ln_ffn_g, ln_ffn_b, ffn_w_up, ffn_w_down, loss_target, m_attn_w_in, m_attn_w_out, m_hgrn_w_in, m_hgrn_w_out, m_hgrn_norm_g, m_lb_logits, m_ln_mix_g, m_ln_mix_b, m_ln_ffn_g, m_ln_ffn_b, m_ffn_w_up, m_ffn_w_down, v_attn_w_in, v_attn_w_out, v_hgrn_w_in, v_hgrn_w_out, v_hgrn_norm_g, v_lb_logits, v_ln_mix_g, v_ln_mix_b, v_ln_ffn_g, v_ln_ffn_b, v_ffn_w_up, v_ffn_w_down):
    wts = dict(attn_w_in=attn_w_in, attn_w_out=attn_w_out, hgrn_w_in=hgrn_w_in, hgrn_w_out=hgrn_w_out,
               hgrn_norm_g=hgrn_norm_g, lb_logits=lb_logits, ln_mix_g=ln_mix_g, ln_mix_b=ln_mix_b, ln_ffn_g=ln_ffn_g,
               ln_ffn_b=ln_ffn_b, ffn_w_up=ffn_w_up, ffn_w_down=ffn_w_down)
    mom = dict(attn_w_in=m_attn_w_in, attn_w_out=m_attn_w_out, hgrn_w_in=m_hgrn_w_in, hgrn_w_out=m_hgrn_w_out,
               hgrn_norm_g=m_hgrn_norm_g, lb_logits=m_lb_logits, ln_mix_g=m_ln_mix_g, ln_mix_b=m_ln_mix_b,
               ln_ffn_g=m_ln_ffn_g, ln_ffn_b=m_ln_ffn_b, ffn_w_up=m_ffn_w_up, ffn_w_down=m_ffn_w_down)
    vel = dict(attn_w_in=v_attn_w_in, attn_w_out=v_attn_w_out, hgrn_w_in=v_hgrn_w_in, hgrn_w_out=v_hgrn_w_out,
               hgrn_norm_g=v_hgrn_norm_g, lb_logits=v_lb_logits, ln_mix_g=v_ln_mix_g, ln_mix_b=v_ln_mix_b,
               ln_ffn_g=v_ln_ffn_g, ln_ffn_b=v_ln_ffn_b, ffn_w_up=v_ffn_w_up, ffn_w_down=v_ffn_w_down)
    me = 4 * lax.axis_index("x") + 2 * lax.axis_index("y") + lax.axis_index("c")

    src = {"attn_w_in": attn_w_in[0], "attn_w_out": attn_w_out[0], "hgrn_w_in": hgrn_w_in[0], "hgrn_w_out": hgrn_w_out[0],
           "ffn_w_up0": ffn_w_up[0], "ffn_w_down0": ffn_w_down[0], "ffn_w_up1": ffn_w_up[1], "ffn_w_down1": ffn_w_down[1]}
    names = list(GATHER_AXIS)
    shards, lands = [], []
    for nm in names:
        sh = hgrn_norm_g if nm == "hgrn_norm_g" else _cast_bf16(src[nm], name=f"cast_{nm}")
        ax = GATHER_AXIS[nm]
        shape = list(sh.shape)
        shape[ax] *= N_DEV
        shards.append(sh)
        lands.append(lax.dynamic_update_slice_in_dim(lax.empty(tuple(shape), sh.dtype), sh, me * sh.shape[ax], ax))
    gathered = _xchg_start(shards, lands, [GATHER_AXIS[nm] for nm in names], gather=True, name="gather_start")
    got = {}

    def get_w(name, after):
        if name not in got:
            group = [g for g in GATHER_WAITS if name in g][0]
            res = _xchg_wait(gathered, [names.index(nm) for nm in group], gathered["token"] if after is None else after,
                             name=f"gather_wait_{GATHER_WAITS.index(group)}")
            got.update(zip(group, res))
        return got[name]

    scattered = {}

    def on_grads(tag, grads):
        gnames = list(grads)
        axes = [SCATTER_AXIS[nm] for nm in gnames]
        stacks = []
        for nm, ax in zip(gnames, axes):
            own = _own_block(grads[nm], ax, me)
            stack = lax.empty((N_DEV, *own.shape[1:]), own.dtype)
            stacks.append(lax.dynamic_update_slice_in_dim(stack, own, me, 0))
        scattered[tag] = (gnames, _xchg_start([grads[nm] for nm in gnames], stacks, axes, gather=False,
                                              name=f"scatter_start_{tag}"))
        return [scattered[tag][1]["token"]]

    sm = jax.nn.softmax(lb_logits, axis=0)
    csum = jnp.cumsum(sm, axis=0)
    small = dict(lb=(csum - csum[0:1])[1:2], lb_logits=lb_logits, ln_mix_g=ln_mix_g, ln_mix_b=ln_mix_b,
                 ln_ffn_g=ln_ffn_g, ln_ffn_b=ln_ffn_b)
    sq_sum, grad_x = _local_step(x[0], loss_target[0], get_w, small, on_grads, deps=[gathered["token"]])
    loss = lax.psum(0.5 * sq_sum / x.shape[-1], ("x", "y", "c"))
    out = {}

    def stack_small(src_):
        rows = [None] * SMALL_ROWS
        for name in SMALL:
            rows[SMALL_ROW[name]], rows[SMALL_ROW[name] + 1] = src_[name][0:1], src_[name][1:2]
        zero = jnp.zeros((1, x.shape[-1]), F32)
        return jnp.concatenate([zero if r is None else r for r in rows], axis=0)[None]

    def update(name, slabs):
        shape = wts[name].shape
        out[name] = [r.reshape(shape) for r in _adamw(slabs, wts[name], mom[name], vel[name], name=f"adamw_{name}")]
        return out[name][0]

    slabs, after = {}, grad_x
    for tag, (gnames, xc) in scattered.items():
        slabs.update(zip(gnames, _xchg_wait(xc, list(range(len(gnames))), after, name=f"scatter_wait_{tag}")))
        if tag == "ffn1":
            continue
        if tag == "ffn0":
            update("ffn_w_down", [slabs["ffn_w_down0"], slabs["ffn_w_down1"]])
            after = update("ffn_w_up", [slabs["ffn_w_up0"], slabs["ffn_w_up1"]])
        elif tag == "hgrn":
            update("hgrn_w_out", [slabs["hgrn_w_out"]])
            after = update("hgrn_w_in", [slabs["hgrn_w_in"]])
        elif tag == "attn_out":
            after = update("attn_w_out", [slabs["attn_w_out"]])
        else:
            after = update("attn_w_in", [slabs["attn_w_in"]])
    res = _adamw([slabs["small"]], stack_small(wts), stack_small(mom), stack_small(vel), name="adamw_small")
    for name in SMALL:
        out[name] = [r[0, SMALL_ROW[name]:SMALL_ROW[name] + 2] for r in res]
    ng = hgrn_norm_g.shape[-1]
    ng_slabs = lax.dynamic_slice(slabs["small"], (0, NORM_G_ROW, me * ng), (N_DEV, 1, ng))
    out["hgrn_norm_g"] = [r[0] for r in _adamw([ng_slabs], hgrn_norm_g[None], m_hgrn_norm_g[None],
                                                v_hgrn_norm_g[None], name="adamw_norm_g")]
    order =("attn_w_in", "attn_w_out", "hgrn_w_in", "hgrn_w_out", "hgrn_norm_g", "lb_logits", "ln_mix_g", "ln_mix_b",
             "ln_ffn_g", "ln_ffn_b", "ffn_w_up", "ffn_w_down")
    return (loss, grad_x[None], *[out[nm][0] for nm in order], *[out[nm][1] for nm in order],
            *[out[nm][2] for nm in order], *[out[nm][3] for nm in order])
```

```python
import jax
import jax.numpy as jnp
from jax import lax
from jax.experimental import pallas as pl
from jax.experimental.pallas import tpu as pltpu

F32 = jnp.float32
BF16 = jnp.bfloat16

N_DEV = 8
LANES = 128
D_MODEL = 1024
ATTN_HEAD_DIM = 64
ATTN_HEADS = 16
ATTN_BLK = 128
DILATIONS = (1, 4, 16)
ROPE_THETA = 10000.0
HGRN_HEADS = 8
HGRN_CHUNK = 64
D_FF = 4096
LN_EPS = 1e-5
RMS_EPS = 1e-6
DEPTH = 2
ALPHA = (2 * DEPTH) ** 0.25
ADAM_LR, ADAM_B1, ADAM_B2, ADAM_EPS, ADAM_WD, ADAM_STEP = 0.001, 0.9, 0.999, 1e-08, 0.01, 10
VMEM_LIMIT = 48 * 1024 * 1024

_NT = (((1,), (1,)), ((), ()))
_TN = (((0,), (0,)), ((), ()))


def _dot(a, b):
    return jnp.dot(a, b, preferred_element_type=F32)


def _dot_nt(a, b):
    return lax.dot_general(a, b, _NT, preferred_element_type=F32)


def _dot_tn(a, b):
    return lax.dot_general(a, b, _TN, preferred_element_type=F32)


def _split3(x):
    p1 = x.astype(BF16)
    r1 = x - p1.astype(F32)
    p2 = r1.astype(BF16)
    p3 = (r1 - p2.astype(F32)).astype(BF16)
    return p1, p2, p3


def _exact_dot(sel, x):
    p1, p2, p3 = _split3(x)
    return _dot(sel, p1) + _dot(sel, p2) + _dot(sel, p3)


def _exact_dot_r(x, sel):
    p1, p2, p3 = _split3(x)
    return _dot(p1, sel) + _dot(p2, sel) + _dot(p3, sel)


def _params(sem=None):
    return pltpu.CompilerParams(dimension_semantics=sem, vmem_limit_bytes=VMEM_LIMIT)


def _mm(a, b, mode, *, name, m, n, k, tm=1024, tn=1024, tk=1024, out_dtypes=(F32,), epi=None, a_pre=None,
        tile_extras=(), row_extras=(), a_split=1, b_split=1, out_split=1,
        b_col_off=0, b_k_off=0, out_col_off=0, out_cols=None, alias=None, epi_wants_j=False, deps=()):
    tm, tn, tk = min(tm, m), min(tn, n), min(tk, k)
    assert m % tm == 0 and n % tn == 0 and k % tk == 0, (name, m, n, k, tm, tn, tk)
    gm, gn, gk = m // tm, n // tn, k // tk
    if mode in ("nn", "nt"):
        if a_split > 1:
            kc = (k // a_split) // tk
            a_spec = pl.BlockSpec((None, tm, tk), lambda i, j, kk: (kk // kc, i, kk % kc))
        else:
            a_spec = pl.BlockSpec((tm, tk), lambda i, j, kk: (i, kk))
    else:
        a_spec = pl.BlockSpec((tk, tm), lambda i, j, kk: (kk, i))
    if mode in ("nn", "tn"):
        if b_split > 1:
            nc = (n // b_split) // tn
            b_spec = pl.BlockSpec((None, tk, tn), lambda i, j, kk: (j // nc, kk, j % nc))
        else:
            b_spec = pl.BlockSpec((tk, tn), lambda i, j, kk: (kk + b_k_off, j + b_col_off))
    else:
        b_spec = pl.BlockSpec((tn, tk), lambda i, j, kk: (j + b_col_off, kk + b_k_off))
    if out_split > 1:
        nco = (n // out_split) // tn
        o_spec = pl.BlockSpec((None, tm, tn), lambda i, j, kk: (j // nco, i, j % nco))
        o_shape = (out_split, m, n // out_split)
    else:
        o_spec = pl.BlockSpec((tm, tn), lambda i, j, kk: (i, j + out_col_off))
        o_shape = (m, out_cols if out_cols is not None else n)
    n_ex = len(tile_extras) + len(row_extras)
    n_out = len(out_dtypes)
    if epi is None:
        def epi(acc):
            return (acc,)
    dot = {"nn": _dot, "nt": _dot_nt, "tn": _dot_tn}[mode]

    def body(*refs):
        a_ref, b_ref = refs[0], refs[1]
        ex = refs[2:2 + n_ex]
        outs = refs[2 + n_ex + (1 if alias is not None else 0) + len(deps):][:n_out]
        jj = pl.program_id(1)

        def product():
            av = a_ref[...]
            if a_pre is not None:
                av = a_pre(av)
            return dot(av.astype(BF16), b_ref[...].astype(BF16))

        def finish(total):
            lead = (jj,) if epi_wants_j else ()
            res = epi(*lead, total, *[e[...] for e in ex])
            for o, r in zip(outs, res):
                o[...] = r.astype(o.dtype)

        if gk == 1:
            finish(product())
        else:
            acc = refs[-1]
            kk = pl.program_id(2)

            @pl.when(kk == 0)
            def _():
                acc[...] = product()

            @pl.when(kk > 0)
            def _():
                acc[...] += product()

            @pl.when(kk == gk - 1)
            def _():
                finish(acc[...])

    in_specs = [a_spec, b_spec] + [o_spec] * len(tile_extras)
    in_specs += [pl.BlockSpec((tm, r.shape[1]), lambda i, j, kk: (i, 0)) for r in row_extras]
    args = [a, b] + list(tile_extras) + list(row_extras)
    io_alias = {}
    if alias is not None:
        in_specs.append(pl.BlockSpec(memory_space=pl.ANY))
        args.append(alias)
        io_alias = {len(args) - 1: 0}
    in_specs += [pl.BlockSpec(memory_space=pl.ANY)] * len(deps)
    args += list(deps)
    out = pl.pallas_call(
        body, name=name, grid=(gm, gn, gk), in_specs=in_specs,
        out_specs=[o_spec] * n_out,
        out_shape=[jax.ShapeDtypeStruct(o_shape, dt) for dt in out_dtypes],
        scratch_shapes=[pltpu.VMEM((tm, tn), F32)] if gk > 1 else [],
        input_output_aliases=io_alias,
        compiler_params=_params(("parallel", "parallel", "arbitrary")),
    )(*args)
    return out[0] if n_out == 1 else out


def _rope_tables(seq, dil):
    pos = jnp.arange(seq, dtype=jnp.int32).reshape(seq // dil, dil).T.reshape(seq)
    half = ATTN_HEAD_DIM // 2
    inv = ROPE_THETA ** (-jnp.arange(half, dtype=F32) * (2.0 / ATTN_HEAD_DIM))
    ang = pos.astype(F32)[:, None] * inv[None, :]
    cos, sin = jnp.cos(ang), jnp.sin(ang)
    reps = LANES // ATTN_HEAD_DIM
    return (jnp.tile(jnp.concatenate([cos, cos], axis=1), (1, reps)),
            jnp.tile(jnp.concatenate([-sin, sin], axis=1), (1, reps)))


def _rotate(x, c, ss, sign=1.0):
    w = x.shape[-1]
    half = ATTN_HEAD_DIM // 2
    lane = lax.broadcasted_iota(jnp.int32, x.shape, 1)
    first = (lane % ATTN_HEAD_DIM) < half
    partner = jnp.where(first, pltpu.roll(x, w - half, 1), pltpu.roll(x, half, 1))
    reps = w // LANES
    if reps > 1:
        c = jnp.concatenate([c] * reps, axis=1)
        ss = jnp.concatenate([ss] * reps, axis=1)
    return x * c + sign * (partner * ss)


ROW_TILE = 512


def _ln_fwd(x, y, g, b, *, name):
    s, d = x.shape

    def body(x_ref, y_ref, g_ref, b_ref, o_ref, ob_ref, xh_ref, r_ref):
        u = ALPHA * x_ref[...] + y_ref[...]
        mu = jnp.mean(u, axis=-1, keepdims=True)
        uc = u - mu
        var = jnp.mean(uc * uc, axis=-1, keepdims=True)
        rstd = lax.rsqrt(var + LN_EPS)
        xh = uc * rstd
        out = xh * g_ref[...] + b_ref[...]
        o_ref[...] = out
        ob_ref[...] = out.astype(BF16)
        xh_ref[...] = xh
        r_ref[...] = rstd

    row = pl.BlockSpec((ROW_TILE, d), lambda i: (i, 0))
    vec = pl.BlockSpec((1, d), lambda i: (0, 0))
    return pl.pallas_call(
        body, name=name, grid=(s // ROW_TILE,), in_specs=[row, row, vec, vec],
        out_specs=[row, row, row, pl.BlockSpec((ROW_TILE, 1), lambda i: (i, 0))],
        out_shape=[jax.ShapeDtypeStruct((s, d), F32), jax.ShapeDtypeStruct((s, d), BF16),
                   jax.ShapeDtypeStruct((s, d), F32), jax.ShapeDtypeStruct((s, 1), F32)],
        compiler_params=_params(("parallel",)),
    )(x, y, g, b)


def _ln_bwd(dy, xhat, rstd, g, *, name):
    s, d = dy.shape

    def body(dy_ref, xh_ref, r_ref, g_ref, du_ref, dub_ref, dg_ref, db_ref):
        @pl.when(pl.program_id(0) == 0)
        def _():
            dg_ref[...] = jnp.zeros_like(dg_ref)
            db_ref[...] = jnp.zeros_like(db_ref)

        dyv = dy_ref[...]
        xh = xh_ref[...]
        dxh = dyv * g_ref[...]
        m1 = jnp.mean(dxh, axis=-1, keepdims=True)
        m2 = jnp.mean(dxh * xh, axis=-1, keepdims=True)
        du = r_ref[...] * (dxh - m1 - xh * m2)
        du_ref[...] = du
        dub_ref[...] = du.astype(BF16)
        dg_ref[...] += jnp.sum(dyv * xh, axis=0, keepdims=True)
        db_ref[...] += jnp.sum(dyv, axis=0, keepdims=True)

    row = pl.BlockSpec((ROW_TILE, d), lambda i: (i, 0))
    vec = pl.BlockSpec((1, d), lambda i: (0, 0))
    return pl.pallas_call(
        body, name=name, grid=(s // ROW_TILE,),
        in_specs=[row, row, pl.BlockSpec((ROW_TILE, 1), lambda i: (i, 0)), vec],
        out_specs=[row, row, vec, vec],
        out_shape=[jax.ShapeDtypeStruct((s, d), F32), jax.ShapeDtypeStruct((s, d), BF16),
                   jax.ShapeDtypeStruct((1, d), F32), jax.ShapeDtypeStruct((1, d), F32)],
        compiler_params=_params(("arbitrary",)),
    )(dy, xhat, rstd, g)


def _loss_head(y, target):
    s, d = y.shape

    def body(y_ref, t_ref, dy_ref, sq_ref):
        @pl.when(pl.program_id(0) == 0)
        def _():
            sq_ref[...] = jnp.zeros_like(sq_ref)

        e = y_ref[...] - t_ref[...]
        dy_ref[...] = e * (1.0 / d)
        sq_ref[...] += jnp.sum(e * e, axis=0, keepdims=True)

    row = pl.BlockSpec((ROW_TILE, d), lambda i: (i, 0))
    vec = pl.BlockSpec((1, d), lambda i: (0, 0))
    return pl.pallas_call(
        body, name="loss_head", grid=(s // ROW_TILE,), in_specs=[row, row], out_specs=[row, vec],
        out_shape=[jax.ShapeDtypeStruct((s, d), F32), jax.ShapeDtypeStruct((1, d), F32)],
        compiler_params=_params(("arbitrary",)),
    )(y, target)


POS_BLK = 2048


def _class_rows(r, dil):
    return pl.ds(r, POS_BLK // dil, stride=dil) if dil > 1 else pl.ds(0, POS_BLK)


def _class_view(a, dil):
    s, w = a.shape
    return a.reshape(dil, s // dil, w)


def _class_spec(dil, all_tiles=True):
    if all_tiles:
        return pl.BlockSpec((dil, POS_BLK // dil, LANES), lambda i, t: (0, i, t))
    return pl.BlockSpec((dil, POS_BLK // dil, LANES), lambda i, t: (0, i, 0))


def _pos_spec(all_tiles=True):
    if all_tiles:
        return pl.BlockSpec((POS_BLK, LANES), lambda i, t: (i, t))
    return pl.BlockSpec((POS_BLK, LANES), lambda i, t: (i, 0))


def _prep_x(x, deps=()):
    s, d = x.shape

    def body(x_ref, *refs):
        outs = refs[len(deps):]
        for dil, o_ref in zip(DILATIONS, outs):
            for r in range(dil):
                o_ref[r] = x_ref[_class_rows(r, dil), :].astype(BF16)

    outs = pl.pallas_call(
        body, name="prep_x", grid=(s // POS_BLK, d // LANES),
        in_specs=[_pos_spec()] + [pl.BlockSpec(memory_space=pl.ANY)] * len(deps),
        out_specs=[_class_spec(dil) for dil in DILATIONS],
        out_shape=[jax.ShapeDtypeStruct((dil, s // dil, d), BF16) for dil in DILATIONS],
        compiler_params=_params(("parallel", "parallel")),
    )(x, *deps)
    return [o.reshape(s, d) for o in outs]


def _head_expand_matrix():
    h = lax.broadcasted_iota(jnp.int32, (LANES, D_MODEL), 0)
    l = lax.broadcasted_iota(jnp.int32, (LANES, D_MODEL), 1)
    return (l // ATTN_HEAD_DIM == h).astype(BF16)


def _attn_fwd(qkv, dil, *, name):
    _, s, d = qkv.shape
    nq = s // ATTN_BLK
    per = nq // dil
    scale = ATTN_HEAD_DIM ** -0.5
    tiles = d // LANES

    def body(q_ref, kc_ref, kp_ref, vc_ref, vp_ref, o_ref, lse_ref):
        qb = pl.program_id(0)
        first = (qb % per) == 0
        qi = lax.broadcasted_iota(jnp.int32, (ATTN_BLK, 2 * ATTN_BLK), 0)
        kj = lax.broadcasted_iota(jnp.int32, (ATTN_BLK, 2 * ATTN_BLK), 1)
        dist = qi + ATTN_BLK - kj
        valid = (dist >= 0) & (dist <= ATTN_BLK) & ((kj >= ATTN_BLK) | jnp.logical_not(first))
        lane = lax.broadcasted_iota(jnp.int32, (ATTN_BLK, LANES), 1)
        lse_tile = jnp.zeros((ATTN_BLK, LANES), F32)
        for t in range(tiles):
            cols = pl.ds(t * LANES, LANES)
            q = q_ref[:, cols]
            k2 = jnp.concatenate([kp_ref[:, cols], kc_ref[:, cols]], axis=0)
            v2 = jnp.concatenate([vp_ref[:, cols], vc_ref[:, cols]], axis=0)
            o_tile = jnp.zeros((ATTN_BLK, LANES), F32)
            for hh in range(2):
                in_head = (lane // ATTN_HEAD_DIM) == hh
                qh = jnp.where(in_head, q, jnp.zeros_like(q))
                sc = _dot_nt(qh, k2) * scale
                sc = jnp.where(valid, sc, -jnp.inf)
                mx = jnp.max(sc, axis=-1, keepdims=True)
                p = jnp.exp(sc - mx)
                l = jnp.sum(p, axis=-1, keepdims=True)
                oh = _dot(p.astype(BF16), v2) / l
                o_tile = jnp.where(in_head, oh, o_tile)
                lse_tile = jnp.where(lane == 2 * t + hh, mx + jnp.log(l), lse_tile)
            o_ref[:, cols] = o_tile
        lse_ref[...] = lse_tile

    def blk(piece, prev):
        if prev:
            return pl.BlockSpec((None, ATTN_BLK, d), lambda i: (piece, jnp.maximum(i - 1, 0), 0))
        return pl.BlockSpec((None, ATTN_BLK, d), lambda i: (piece, i, 0))

    return pl.pallas_call(
        body, name=name, grid=(nq,),
        in_specs=[blk(0, False), blk(1, False), blk(1, True), blk(2, False), blk(2, True)],
        out_specs=[pl.BlockSpec((ATTN_BLK, d), lambda i: (i, 0)), pl.BlockSpec((ATTN_BLK, LANES), lambda i: (i, 0))],
        out_shape=[jax.ShapeDtypeStruct((s, d), F32), jax.ShapeDtypeStruct((s, LANES), F32)],
        compiler_params=_params(("parallel",)),
    )(qkv, qkv, qkv, qkv, qkv)


def _attn_combine(os_, lses):
    s, d = os_[0].shape
    sel = _head_expand_matrix()

    def body(o0, o1, o2, l0, l1, l2, sel_ref, of_ref, ob_ref, lt_ref, o_pos, l_pos):
        for g, (dil, o_ref, l_ref) in enumerate(zip(DILATIONS, (o0, o1, o2), (l0, l1, l2))):
            for r in range(dil):
                o_pos[g, _class_rows(r, dil), :] = o_ref[r]
                l_pos[g, _class_rows(r, dil), :] = l_ref[r]
        la, lb_, lc = l_pos[0], l_pos[1], l_pos[2]
        mx = jnp.maximum(jnp.maximum(la, lb_), lc)
        es = (jnp.exp(la - mx), jnp.exp(lb_ - mx), jnp.exp(lc - mx))
        z = es[0] + es[1] + es[2]
        lt_ref[...] = mx + jnp.log(z)
        acc = jnp.zeros((POS_BLK, LANES), F32)
        for g in range(3):
            acc += _exact_dot_r(es[g] / z, sel_ref[...]) * o_pos[g]
        of_ref[...] = acc
        ob_ref[...] = acc.astype(BF16)

    return pl.pallas_call(
        body, name="attn_combine", grid=(s // POS_BLK, d // LANES),
        in_specs=[_class_spec(dil) for dil in DILATIONS] + [_class_spec(dil, False) for dil in DILATIONS]
        + [pl.BlockSpec((LANES, LANES), lambda i, t: (0, t))],
        out_specs=[_pos_spec(), _pos_spec(), _pos_spec(False)],
        out_shape=[jax.ShapeDtypeStruct((s, d), F32), jax.ShapeDtypeStruct((s, d), BF16),
                   jax.ShapeDtypeStruct((s, LANES), F32)],
        scratch_shapes=[pltpu.VMEM((3, POS_BLK, LANES), F32), pltpu.VMEM((3, POS_BLK, LANES), F32)],
        compiler_params=_params(("parallel", "arbitrary")),
    )(*[_class_view(o, dil) for o, dil in zip(os_, DILATIONS)],
      *[_class_view(l, dil) for l, dil in zip(lses, DILATIONS)], sel)


def _attn_bwd_prep(do, o, lse):
    s, d = do.shape
    tiles = d // LANES
    sel_t = _head_expand_matrix().T

    def body(do_ref, o_ref, l_ref, sel_ref, *refs):
        outs, delta = refs[:9], refs[9]
        t = pl.program_id(1)
        part = _exact_dot_r(do_ref[...] * o_ref[...], sel_ref[...])

        @pl.when(t == 0)
        def _():
            delta[...] = part

        @pl.when(t > 0)
        def _():
            delta[...] += part

        for g, dil in enumerate(DILATIONS):
            for r in range(dil):
                outs[g][r] = do_ref[_class_rows(r, dil), :].astype(BF16)

        @pl.when(t == tiles - 1)
        def _():
            for g, dil in enumerate(DILATIONS):
                for r in range(dil):
                    outs[3 + g][r] = l_ref[_class_rows(r, dil), :]
                    outs[6 + g][r] = delta[_class_rows(r, dil), :]

    outs = pl.pallas_call(
        body, name="attn_bwd_prep", grid=(s // POS_BLK, tiles),
        in_specs=[_pos_spec(), _pos_spec(), _pos_spec(False), pl.BlockSpec((LANES, LANES), lambda i, t: (t, 0))],
        out_specs=[_class_spec(dil) for dil in DILATIONS] + [_class_spec(dil, False) for dil in DILATIONS] * 2,
        out_shape=[jax.ShapeDtypeStruct((dil, s // dil, d), BF16) for dil in DILATIONS]
        + [jax.ShapeDtypeStruct((dil, s // dil, LANES), F32) for dil in DILATIONS] * 2,
        scratch_shapes=[pltpu.VMEM((POS_BLK, LANES), F32)],
        compiler_params=_params(("parallel", "arbitrary")),
    )(do, o, lse, sel_t)
    flat = [a.reshape(s, a.shape[-1]) for a in outs]
    return flat[0:3], flat[3:6], flat[6:9]


def _attn_bwd(qkv, do, lse, delta, tables, dil, *, name):
    _, s, d = qkv.shape
    nq = s // ATTN_BLK
    per = nq // dil
    scale = ATTN_HEAD_DIM ** -0.5
    tiles = d // LANES

    def body(qd_ref, qo_ref, k_ref, v_ref, dod_ref, doo_ref, lsd_ref, lso_ref, dld_ref, dlo_ref,
             c_ref, ss_ref, out_ref, carry):
        kb = pl.program_id(0)

        @pl.when(kb == 0)
        def _():
            carry[...] = jnp.zeros_like(carry)

        has_next = (kb % per) != (per - 1)
        qi = lax.broadcasted_iota(jnp.int32, (ATTN_BLK, ATTN_BLK), 0)
        kj = lax.broadcasted_iota(jnp.int32, (ATTN_BLK, ATTN_BLK), 1)
        valid_d = kj <= qi
        valid_o = (kj >= qi) & has_next
        lane = lax.broadcasted_iota(jnp.int32, (ATTN_BLK, LANES), 1)
        c, ss = c_ref[...], ss_ref[...]
        for t in range(tiles):
            cols = pl.ds(t * LANES, LANES)
            k = k_ref[:, cols]
            v = v_ref[:, cols]
            dk = jnp.zeros((ATTN_BLK, LANES), F32)
            dv = jnp.zeros((ATTN_BLK, LANES), F32)
            dq_d = carry[:, cols]
            dq_o = jnp.zeros((ATTN_BLK, LANES), F32)
            for hh in range(2):
                head = 2 * t + hh
                in_head = (lane // ATTN_HEAD_DIM) == hh
                kh = jnp.where(in_head, k, jnp.zeros_like(k))
                for which in range(2):
                    q_ref, do_ref, ls_ref, dl_ref, valid = (
                        (qd_ref, dod_ref, lsd_ref, dld_ref, valid_d),
                        (qo_ref, doo_ref, lso_ref, dlo_ref, valid_o))[which]
                    q = q_ref[:, cols]
                    qh = jnp.where(in_head, q, jnp.zeros_like(q))
                    dob = do_ref[:, cols]
                    doh = jnp.where(in_head, dob, jnp.zeros_like(dob))
                    sc = _dot_nt(qh, k) * scale
                    p = jnp.where(valid, jnp.exp(sc - ls_ref[:, head:head + 1]), 0.0)
                    dp = _dot_nt(doh, v)
                    ds = (p * (dp - dl_ref[:, head:head + 1]) * scale).astype(BF16)
                    dv += _dot_tn(p.astype(BF16), doh)
                    dk += _dot_tn(ds, qh)
                    dq = _dot(ds, kh)
                    if which == 0:
                        dq_d += dq
                    else:
                        dq_o += dq
            carry[:, cols] = dq_o
            out_ref[0, :, cols] = _rotate(dq_d, c, ss, -1.0).astype(BF16)
            out_ref[1, :, cols] = _rotate(dk, c, ss, -1.0).astype(BF16)
            out_ref[2, :, cols] = dv.astype(BF16)

    def nxt(i):
        return jnp.minimum(i + 1, nq - 1)

    def piece(p, shift):
        if shift:
            return pl.BlockSpec((None, ATTN_BLK, d), lambda i: (p, nxt(i), 0))
        return pl.BlockSpec((None, ATTN_BLK, d), lambda i: (p, i, 0))

    def rows(width, shift):
        if shift:
            return pl.BlockSpec((ATTN_BLK, width), lambda i: (nxt(i), 0))
        return pl.BlockSpec((ATTN_BLK, width), lambda i: (i, 0))

    return pl.pallas_call(
        body, name=name, grid=(nq,),
        in_specs=[piece(0, False), piece(0, True), piece(1, False), piece(2, False),
                  rows(d, False), rows(d, True), rows(LANES, False), rows(LANES, True), rows(LANES, False),
                  rows(LANES, True), rows(LANES, False), rows(LANES, False)],
        out_specs=pl.BlockSpec((3, ATTN_BLK, d), lambda i: (0, i, 0)),
        out_shape=jax.ShapeDtypeStruct((3, s, d), BF16),
        scratch_shapes=[pltpu.VMEM((ATTN_BLK, d), F32)],
        compiler_params=_params(("arbitrary",)),
    )(qkv, qkv, qkv, qkv, do, do, lse, lse, delta, delta, *tables)


def _dx_combine(du, parts):
    s, d = du.shape

    def body(du_ref, p0, p1, p2, out_ref):
        out_ref[...] = ALPHA * du_ref[...] + p0[0]
        for dil, p_ref in zip(DILATIONS[1:], (p1, p2)):
            for r in range(dil):
                out_ref[_class_rows(r, dil), :] += p_ref[r]

    return pl.pallas_call(
        body, name="dx_combine", grid=(s // POS_BLK, d // LANES),
        in_specs=[_pos_spec()] + [_class_spec(dil) for dil in DILATIONS], out_specs=_pos_spec(),
        out_shape=jax.ShapeDtypeStruct((s, d), F32),
        compiler_params=_params(("parallel", "parallel")),
    )(du, *[_class_view(p, dil) for p, dil in zip(parts, DILATIONS)])


HGRN_ROWS = 512


def _tri(lower):
    i = lax.broadcasted_iota(jnp.int32, (HGRN_CHUNK, HGRN_CHUNK), 0)
    j = lax.broadcasted_iota(jnp.int32, (HGRN_CHUNK, HGRN_CHUNK), 1)
    return (j <= i) if lower else (j >= i)


def _hgrn_gates(qr, z, lb):
    sq = jax.nn.sigmoid(qr)
    q = qr * sq
    sg = jax.nn.sigmoid(z)
    f = lb + (1.0 - lb) * sg
    key = (1.0 - lb) * jax.nn.sigmoid(-z)
    return sq, q, sg, f, key


def _hgrn_fwd(p2, lb, norm_g):
    _, s, d = p2.shape
    nblk = s // HGRN_ROWS
    cps = HGRN_ROWS // HGRN_CHUNK

    def body(p_ref, lb_ref, g_ref, o_ref, on_ref, st_ref, state):
        @pl.when(pl.program_id(1) == 0)
        def _():
            state[...] = jnp.zeros_like(state)

        lbv = lb_ref[...]
        gv = g_ref[...]
        ltri = _tri(True)
        lsel = ltri.astype(BF16)
        for cidx in range(cps):
            rows = pl.ds(cidx * HGRN_CHUNK, HGRN_CHUNK)
            _, q, _, f, key = _hgrn_gates(p_ref[0, rows, :], p_ref[1, rows, :], lbv)
            v = p_ref[2, rows, :].astype(BF16)
            b = _exact_dot(lsel, jnp.log(f))
            b_last = b[HGRN_CHUNK - 1:HGRN_CHUNK, :]
            qd = (q * jnp.exp(b)).astype(BF16)
            kd = (key * jnp.exp(-b)).astype(BF16)
            k2 = (key * jnp.exp(b_last - b)).astype(BF16)
            st = state[...]
            st_ref[0, cidx] = st
            a = jnp.where(ltri, _dot_nt(qd, kd), 0.0)
            o = _dot(a.astype(BF16), v) + _dot_nt(qd, st.astype(BF16))
            state[...] = st * jnp.exp(b_last) + _dot_tn(v, k2)
            o_ref[rows, :] = o
            r = lax.rsqrt(jnp.mean(o * o, axis=-1, keepdims=True) + RMS_EPS)
            on_ref[rows, :] = (o * r * gv).astype(BF16)

    vec = pl.BlockSpec((1, LANES), lambda h, c: (0, h))
    col = pl.BlockSpec((HGRN_ROWS, LANES), lambda h, c: (c, h))
    return pl.pallas_call(
        body, name="hgrn_fwd", grid=(HGRN_HEADS, nblk),
        in_specs=[pl.BlockSpec((3, HGRN_ROWS, LANES), lambda h, c: (0, c, h)), vec, vec],
        out_specs=[col, col, pl.BlockSpec((1, cps, LANES, LANES), lambda h, c: (h, c, 0, 0))],
        out_shape=[jax.ShapeDtypeStruct((s, d), F32), jax.ShapeDtypeStruct((s, d), BF16),
                   jax.ShapeDtypeStruct((HGRN_HEADS, s // HGRN_CHUNK, LANES, LANES), F32)],
        scratch_shapes=[pltpu.VMEM((LANES, LANES), F32)],
        compiler_params=_params(("parallel", "arbitrary")),
    )(p2, lb, norm_g)


def _hgrn_bwd(p2, lb, norm_g, o_raw, states, dyn):
    _, s, d = p2.shape
    nblk = s // HGRN_ROWS
    cps = HGRN_ROWS // HGRN_CHUNK

    def body(p_ref, lb_ref, g_ref, o_ref, st_ref, dy_ref, dp_ref, dg_ref, dlb_ref, dstate):
        @pl.when(pl.program_id(1) == 0)
        def _():
            dstate[...] = jnp.zeros_like(dstate)
            dg_ref[...] = jnp.zeros_like(dg_ref)
            dlb_ref[...] = jnp.zeros_like(dlb_ref)

        lbv = lb_ref[...]
        gv = g_ref[...]
        ltri = _tri(True)
        lsel = ltri.astype(BF16)
        usel = _tri(False).astype(BF16)
        last_row = lax.broadcasted_iota(jnp.int32, (HGRN_CHUNK, LANES), 0) == HGRN_CHUNK - 1
        for cidx in reversed(range(cps)):
            rows = pl.ds(cidx * HGRN_CHUNK, HGRN_CHUNK)
            qr, z = p_ref[0, rows, :], p_ref[1, rows, :]
            sq, q, sg, f, key = _hgrn_gates(qr, z, lbv)
            v = p_ref[2, rows, :].astype(BF16)
            b = _exact_dot(lsel, jnp.log(f))
            b_last = b[HGRN_CHUNK - 1:HGRN_CHUNK, :]
            eb, enb, e2 = jnp.exp(b), jnp.exp(-b), jnp.exp(b_last - b)
            dec = jnp.exp(b_last)
            qd_f, kd_f, k2_f = q * eb, key * enb, key * e2
            qd, kd, k2 = qd_f.astype(BF16), kd_f.astype(BF16), k2_f.astype(BF16)
            a = jnp.where(ltri, _dot_nt(qd, kd), 0.0).astype(BF16)
            st = st_ref[0, cidx]
            dst = dstate[...]
            o = o_ref[rows, :]
            dyv = dy_ref[rows, :]
            r = lax.rsqrt(jnp.mean(o * o, axis=-1, keepdims=True) + RMS_EPS)
            oh = o * r
            doh = dyv * gv
            do_f = r * (doh - oh * jnp.mean(doh * oh, axis=-1, keepdims=True))
            dg_ref[...] += jnp.sum(dyv * oh, axis=0, keepdims=True)
            do = do_f.astype(BF16)
            dstb = dst.astype(BF16)
            da = jnp.where(ltri, _dot_nt(do, v), 0.0).astype(BF16)
            dv = _dot_tn(a, do) + _dot_nt(k2, dstb)
            dqd = _dot(da, kd) + _dot(do, st.astype(BF16))
            dkd = _dot_tn(da, qd)
            dk2 = _dot(v, dstb)
            ddec = jnp.sum(dst * st, axis=0, keepdims=True)
            dstate[...] = dst * dec + _dot_tn(do, qd)
            dq = dqd * eb
            dkey = dkd * enb + dk2 * e2
            db = dqd * qd_f - dkd * kd_f - dk2 * k2_f
            db_last = jnp.sum(dk2 * k2_f, axis=0, keepdims=True) + ddec * dec
            db = db + jnp.where(last_row, db_last, 0.0)
            dlf = _exact_dot(usel, db)
            one_m_s = 1.0 - sg
            dz = (dlf / f - dkey) * ((1.0 - lbv) * sg * one_m_s)
            dlb_ref[...] += jnp.sum(dlf * one_m_s / f - dkey * jax.nn.sigmoid(-z), axis=0, keepdims=True)
            dp_ref[0, rows, :] = (dq * (sq * (1.0 + qr * (1.0 - sq)))).astype(BF16)
            dp_ref[1, rows, :] = dz.astype(BF16)
            dp_ref[2, rows, :] = dv.astype(BF16)

    def rev(c):
        return nblk - 1 - c

    vec = pl.BlockSpec((1, LANES), lambda h, c: (0, h))
    col = pl.BlockSpec((HGRN_ROWS, LANES), lambda h, c: (rev(c), h))
    p3 = pl.BlockSpec((3, HGRN_ROWS, LANES), lambda h, c: (0, rev(c), h))
    return pl.pallas_call(
        body, name="hgrn_bwd", grid=(HGRN_HEADS, nblk),
        in_specs=[p3, vec, vec, col, pl.BlockSpec((1, cps, LANES, LANES), lambda h, c: (h, rev(c), 0, 0)), col],
        out_specs=[p3, vec, vec],
        out_shape=[jax.ShapeDtypeStruct((3, s, d), BF16), jax.ShapeDtypeStruct((1, d), F32),
                   jax.ShapeDtypeStruct((1, d), F32)],
        scratch_shapes=[pltpu.VMEM((LANES, LANES), F32)],
        compiler_params=_params(("parallel", "arbitrary")),
    )(p2, lb, norm_g, o_raw, states, dyn)


SMALL_ROWS = 16


def _relu2(h):
    r = jnp.maximum(h.astype(F32), 0.0)
    return r * r


def _dact_epi(acc, h):
    return (acc * (2.0 * jnp.maximum(h.astype(F32), 0.0)),)


def _res_epi(scale):
    def epi(acc, r):
        return (acc + scale * r,)
    return epi


def _rope_epi(j, acc, c, ss):
    return (jnp.where(j >= 2, acc, _rotate(acc, c, ss)),)


def _local_step(x, target, get_w, small, on_grads, deps=()):
    s, d = x.shape
    lb = small["lb"]
    gq = 3 * d
    gt = gq // 1024

    def ffn_fwd(xb, w_up, w_down, tag):
        h = _mm(xb, w_up, "nn", name=f"ffn_up_{tag}", m=s, n=D_FF, k=d, out_dtypes=(BF16,))
        y = _mm(h, w_down, "nn", name=f"ffn_down_{tag}", m=s, n=d, k=D_FF, a_pre=_relu2)
        return h, y

    def ffn_bwd(du, dub, xb, h, w_up, w_down, tag):
        dh = _mm(dub, w_down, "nt", name=f"ffn_dact_{tag}", m=s, n=D_FF, k=d, out_dtypes=(BF16,), epi=_dact_epi,
                 tile_extras=(h,))
        g_down = _mm(h, dub, "tn", name=f"ffn_gdown_{tag}", m=D_FF, n=d, k=s, a_pre=_relu2, out_dtypes=(BF16,))
        g_up = _mm(xb, dh, "tn", name=f"ffn_gup_{tag}", m=d, n=D_FF, k=s, out_dtypes=(BF16,))
        after = on_grads(f"ffn{tag}", {f"ffn_w_down{tag}": g_down, f"ffn_w_up{tag}": g_up})
        return _mm(dh, w_up, "nt", name=f"ffn_dx_{tag}", m=s, n=d, k=D_FF, epi=_res_epi(ALPHA), tile_extras=(du,),
                   deps=after)

    xs = _prep_x(x, deps)
    w_ain, after = get_w("attn_w_in", None)
    tabs, qkvs, o_parts, lse_parts = [], [], [], []
    for g, dil in enumerate(DILATIONS):
        tabs.append(_rope_tables(s, dil))
        qkv = _mm(xs[g], w_ain, "nn", name=f"attn_in_{g}", m=s, n=gq, k=d, b_col_off=g * gt, out_split=3,
                  out_dtypes=(BF16,), epi=_rope_epi, epi_wants_j=True, row_extras=tabs[g],
                  deps=after if g == 0 else ())
        qkvs.append(qkv)
        o_g, lse_g = _attn_fwd(qkv, dil, name=f"attn_fwd_{g}")
        o_parts.append(o_g)
        lse_parts.append(lse_g)
    o_f, o_b, lse_t = _attn_combine(o_parts, lse_parts)
    w_aout, after = get_w("attn_w_out", o_b)
    y = _mm(o_b, w_aout, "nn", name="attn_out", m=s, n=d, k=d, deps=after)
    x1, x1b, xh1, r1 = _ln_fwd(x, y, small["ln_mix_g"][0:1], small["ln_mix_b"][0:1], name="ln_mix_0")
    w_up0, w_down0 = get_w("ffn_w_up0", o_b)[0], get_w("ffn_w_down0", o_b)[0]
    h0, y = ffn_fwd(x1b, w_up0, w_down0, 0)
    x2, x2b, xh2, r2 = _ln_fwd(x1, y, small["ln_ffn_g"][0:1], small["ln_ffn_b"][0:1], name="ln_ffn_0")
    w_hin, w_hout, norm_g = get_w("hgrn_w_in", x2b)[0], get_w("hgrn_w_out", x2b)[0], get_w("hgrn_norm_g", x2b)[0]
    p2 = _mm(x2b, w_hin, "nn", name="hgrn_in", m=s, n=3 * d, k=d, out_split=3)
    o_raw, o_n, states = _hgrn_fwd(p2, lb, norm_g)
    w_up1, w_down1 = get_w("ffn_w_up1", o_n)[0], get_w("ffn_w_down1", o_n)[0]
    y = _mm(o_n, w_hout, "nn", name="hgrn_out", m=s, n=d, k=d)
    x3, x3b, xh3, r3 = _ln_fwd(x2, y, small["ln_mix_g"][1:2], small["ln_mix_b"][1:2], name="ln_mix_1")
    h1, y = ffn_fwd(x3b, w_up1, w_down1, 1)
    x4, _, xh4, r4 = _ln_fwd(x3, y, small["ln_ffn_g"][1:2], small["ln_ffn_b"][1:2], name="ln_ffn_1")
    dy, sq = _loss_head(x4, target)
    du, dub, dg_ffn1, db_ffn1 = _ln_bwd(dy, xh4, r4, small["ln_ffn_g"][1:2], name="ln_ffn_1_bwd")
    dx3 = ffn_bwd(du, dub, x3b, h1, w_up1, w_down1, 1)
    du, dub, dg_mix1, db_mix1 = _ln_bwd(dx3, xh3, r3, small["ln_mix_g"][1:2], name="ln_mix_1_bwd")
    dyn = _mm(dub, w_hout, "nt", name="hgrn_dout", m=s, n=d, k=d)
    g_hout = _mm(o_n, dub, "tn", name="hgrn_gout", m=d, n=d, k=s, out_dtypes=(BF16,))
    dp2, d_norm_g, d_lb = _hgrn_bwd(p2, lb, norm_g, o_raw, states, dyn)
    g_hin = _mm(x2b, dp2, "tn", name="hgrn_gin", m=d, n=3 * d, k=s, b_split=3, out_dtypes=(BF16,))
    after = on_grads("hgrn", {"hgrn_w_out": g_hout, "hgrn_w_in": g_hin})
    dx2 = _mm(dp2, w_hin, "nt", name="hgrn_dx", m=s, n=d, k=3 * d, a_split=3, epi=_res_epi(ALPHA),
              tile_extras=(du,), deps=after)
    du, dub, dg_ffn0, db_ffn0 = _ln_bwd(dx2, xh2, r2, small["ln_ffn_g"][0:1], name="ln_ffn_0_bwd")
    dx1 = ffn_bwd(du, dub, x1b, h0, w_up0, w_down0, 0)
    du, dub, dg_mix0, db_mix0 = _ln_bwd(dx1, xh1, r1, small["ln_mix_g"][0:1], name="ln_mix_0_bwd")
    g_aout = _mm(o_b, dub, "tn", name="attn_gout", m=d, n=d, k=s, out_dtypes=(BF16,))
    sm1 = jax.nn.softmax(small["lb_logits"], axis=0)
    d_l1 = d_lb * (sm1[0:1] * sm1[1:2])
    zeros = jnp.zeros((SMALL_ROWS - 11, d), F32)
    small_grads = jnp.concatenate([dg_mix0, dg_mix1, db_mix0, db_mix1, dg_ffn0, dg_ffn1, db_ffn0, db_ffn1,
                                   -d_l1, d_l1, d_norm_g, zeros], axis=0)
    after = on_grads("attn_out", {"attn_w_out": g_aout, "small": small_grads})
    do = _mm(dub, w_aout, "nt", name="attn_dout", m=s, n=d, k=d, deps=after)
    do_parts, ls_parts, dl_parts = _attn_bwd_prep(do, o_f, lse_t)
    g_ain, dqkvs = None, []
    for g, dil in enumerate(DILATIONS):
        dqkvs.append(_attn_bwd(qkvs[g], do_parts[g], ls_parts[g], dl_parts[g], tabs[g], dil, name=f"attn_bwd_{g}"))
        g_ain = _mm(xs[g], dqkvs[g], "tn", name=f"attn_gin_{g}", m=d, n=gq, k=s, b_split=3, out_dtypes=(BF16,),
                    out_col_off=g * gt, out_cols=3 * gq, alias=g_ain)
    after = on_grads("attn_in", {"attn_w_in": g_ain})
    dx_parts = [_mm(dqkvs[g], w_ain, "nt", name=f"attn_dx_{g}", m=s, n=d, k=gq, a_split=3, b_k_off=g * gt,
                    deps=after if g == 0 else ())
                for g in range(len(DILATIONS))]
    return jnp.sum(sq), _dx_combine(du, dx_parts)


def _mesh_place():
    x, y, c = lax.axis_index("x"), lax.axis_index("y"), lax.axis_index("c")
    return x, y, c, 4 * x + 2 * y + c


def _peer(x, y, c, k):
    px = 1 - x if (k >> 2) & 1 else x
    py = 1 - y if (k >> 1) & 1 else y
    pc = 1 - c if k & 1 else c
    return (px, py, pc), 4 * px + 2 * py + pc


def _window(ref, axis, size, idx):
    if axis is None:
        return ref
    sl = [slice(None)] * len(ref.shape)
    sl[axis] = pl.ds(idx * size, size)
    return ref.at[tuple(sl)]


_HBM = pl.BlockSpec(memory_space=pltpu.HBM)
_SEM = pl.BlockSpec(memory_space=pltpu.SEMAPHORE)
_EFFECT = pltpu.SideEffectType.DATAFLOW_SIDE_EFFECTING


def _xchg_ends(src_ref, land_ref, axis, gather, me, other):
    if gather:
        size = src_ref.shape[axis]
        return src_ref, _window(land_ref, axis, size, me), _window(land_ref, axis, size, other)
    size = None if axis is None else src_ref.shape[axis] // N_DEV
    return _window(src_ref, axis, size, other), land_ref.at[me], land_ref.at[other]


def _xchg_start(srcs, lands, axes, *, gather, name, deps=()):
    n = len(srcs)
    nd = len(deps)

    def body(*refs):
        src_refs, land_refs = refs[:n], refs[n:2 * n]
        send, recv = refs[2 * n + nd:3 * n + nd], refs[3 * n + nd:4 * n + nd]
        token = refs[-1]
        x, y, c, me = _mesh_place()
        for k in range(1, N_DEV):
            peer, pidx = _peer(x, y, c, k)
            for i in range(n):
                src, dst, _ = _xchg_ends(src_refs[i], land_refs[i], axes[i], gather, me, pidx)
                pltpu.make_async_remote_copy(
                    src_ref=src, dst_ref=dst, send_sem=send[i].at[k - 1], recv_sem=recv[i].at[k - 1],
                    device_id=peer, device_id_type=pl.DeviceIdType.MESH).start()
        token[...] = jnp.zeros_like(token)

    bufs = list(srcs) + list(lands)
    outs = pl.pallas_call(
        body, name=name,
        out_shape=[pltpu.SemaphoreType.DMA((N_DEV - 1,))] * (2 * n) + [pltpu.HBM(b.shape, b.dtype) for b in bufs]
        + [jax.ShapeDtypeStruct((8, LANES), F32)],
        in_specs=[_HBM] * (2 * n) + [pl.BlockSpec(memory_space=pl.ANY)] * nd,
        out_specs=[_SEM] * (2 * n) + [_HBM] * (2 * n) + [pl.BlockSpec(memory_space=pltpu.VMEM)],
        input_output_aliases={i: 2 * n + i for i in range(2 * n)},
        compiler_params=pltpu.CompilerParams(has_side_effects=_EFFECT),
    )(*[pltpu.with_memory_space_constraint(b, pltpu.HBM) for b in bufs], *deps)
    return dict(send=outs[:n], recv=outs[n:2 * n], srcs=outs[2 * n:3 * n], lands=outs[3 * n:4 * n], token=outs[-1],
                axes=list(axes), gather=gather)


def _xchg_wait(xc, items, after, *, name):
    m = len(items)
    gather = xc["gather"]
    axes = [xc["axes"][i] for i in items]

    def body(*refs):
        src_refs, land_refs = refs[:m], refs[m:2 * m]
        send, recv = refs[2 * m:3 * m], refs[3 * m:4 * m]
        x, y, c, me = _mesh_place()
        for k in range(1, N_DEV):
            peer, pidx = _peer(x, y, c, k)
            for j in range(m):
                src, dst, got = _xchg_ends(src_refs[j], land_refs[j], axes[j], gather, me, pidx)
                pltpu.make_async_remote_copy(
                    src_ref=src, dst_ref=dst, send_sem=send[j].at[k - 1], recv_sem=recv[j].at[k - 1],
                    device_id=peer, device_id_type=pl.DeviceIdType.MESH).wait_send()
                pltpu.make_async_remote_copy(
                    src_ref=src, dst_ref=got, send_sem=send[j].at[k - 1], recv_sem=recv[j].at[k - 1],
                    device_id=peer, device_id_type=pl.DeviceIdType.MESH).wait_recv()

    bufs = [xc["srcs"][i] for i in items] + [xc["lands"][i] for i in items]
    sems = [xc["send"][i] for i in items] + [xc["recv"][i] for i in items]
    outs = pl.pallas_call(
        body, name=name, out_shape=[pltpu.HBM(b.shape, b.dtype) for b in bufs],
        in_specs=[_HBM] * (2 * m) + [_SEM] * (2 * m) + [pl.BlockSpec(memory_space=pl.ANY)],
        out_specs=[_HBM] * (2 * m), input_output_aliases={j: j for j in range(2 * m)},
        compiler_params=pltpu.CompilerParams(has_side_effects=_EFFECT),
    )(*bufs, *sems, after)
    return outs[m:]


def _own_block(a, axis, me):
    if axis is None:
        return a[None]
    size = a.shape[axis] // N_DEV
    return lax.dynamic_slice_in_dim(a, me * size, size, axis)[None]


def _cast_bf16(a, *, name):
    r, c = a.shape
    tr = min(r, 512)

    def body(a_ref, o_ref):
        o_ref[...] = a_ref[...].astype(BF16)

    spec = pl.BlockSpec((tr, c), lambda i: (i, 0))
    return pl.pallas_call(body, name=name, grid=(r // tr,), in_specs=[spec], out_specs=spec,
                          out_shape=jax.ShapeDtypeStruct((r, c), BF16), compiler_params=_params(("parallel",)))(a)


def _adamw(slabs, w, m, v, *, name):
    layers, r, c = w.shape
    tr = min(r, 256)

    def body(*refs):
        s_refs = refs[:layers]
        w_ref, m_ref, v_ref, g_ref, d_ref, mo_ref, vo_ref = refs[layers:]
        for l in range(layers):
            g = s_refs[l][0].astype(F32)
            for i in range(1, N_DEV):
                g = g + s_refs[l][i].astype(F32)
            m2 = ADAM_B1 * m_ref[l] + (1.0 - ADAM_B1) * g
            v2 = ADAM_B2 * v_ref[l] + (1.0 - ADAM_B2) * (g * g)
            m_hat = m2 / (1.0 - ADAM_B1 ** ADAM_STEP)
            v_hat = v2 / (1.0 - ADAM_B2 ** ADAM_STEP)
            g_ref[l] = g
            d_ref[l] = -ADAM_LR * (m_hat / (jnp.sqrt(v_hat) + ADAM_EPS) + ADAM_WD * w_ref[l])
            mo_ref[l] = m2
            vo_ref[l] = v2

    spec = pl.BlockSpec((layers, tr, c), lambda i: (0, i, 0))
    return pl.pallas_call(
        body, name=name, grid=(r // tr,),
        in_specs=[pl.BlockSpec((N_DEV, tr, c), lambda i: (0, i, 0))] * layers + [spec, spec, spec],
        out_specs=[spec] * 4, out_shape=[jax.ShapeDtypeStruct((layers, r, c), F32)] * 4,
        compiler_params=_params(("parallel",)),
    )(*slabs, w, m, v)


GATHER_AXIS = {"attn_w_in": 1, "attn_w_out": 0, "ffn_w_up0": 1, "ffn_w_down0": 0, "hgrn_w_in": 1, "hgrn_w_out": 0,
               "hgrn_norm_g": 1, "ffn_w_up1": 1, "ffn_w_down1": 0}
GATHER_STAGES = (("attn_w_in",), ("attn_w_out", "ffn_w_up0", "ffn_w_down0", "hgrn_w_in", "hgrn_w_out", "hgrn_norm_g"),
                 ("ffn_w_up1", "ffn_w_down1"))
GATHER_WAITS = ((("attn_w_in",), 0, 1), (("attn_w_out", "ffn_w_up0", "ffn_w_down0"), 1, 2),
                (("hgrn_w_in", "hgrn_w_out", "hgrn_norm_g"), 1, None), (("ffn_w_up1", "ffn_w_down1"), 2, None))
SCATTER_AXIS = dict(GATHER_AXIS, small=None)
BIG = ("attn_w_in", "attn_w_out", "hgrn_w_in", "hgrn_w_out", "ffn_w_up", "ffn_w_down")
SMALL = ("lb_logits", "ln_mix_g", "ln_mix_b", "ln_ffn_g", "ln_ffn_b")
SMALL_ROW = {"ln_mix_g": 0, "ln_mix_b": 2, "ln_ffn_g": 4, "ln_ffn_b": 6, "lb_logits": 8}
NORM_G_ROW = 10


def kernel(x, attn_w_in, attn_w_out, hgrn_w_in, hgrn_w_out, hgrn_norm_g, lb_logits, ln_mix_g, ln_mix_b, ln_ffn_g, ln_ffn_b, ffn_w_up, ffn_w_down, loss_target, m_attn_w_in, m_attn_w_out, m_hgrn_w_in, m_hgrn_w_out, m_hgrn_norm_g, m_lb_logits, m_ln_mix_g, m_ln_mix_b, m_ln_ffn_g, m_ln_ffn_b, m_ffn_w_up, m_ffn_w_down, v_attn_w_in, v_attn_w_out, v_hgrn_w_in, v_hgrn_w_out, v_hgrn_norm_g, v_lb_logits, v_ln_mix_g, v_ln_mix_b, v_ln_ffn_g, v_ln_ffn_b, v_ffn_w_up, v_ffn_w_down):
    wts = dict(attn_w_in=attn_w_in, attn_w_out=attn_w_out, hgrn_w_in=hgrn_w_in, hgrn_w_out=hgrn_w_out,
               hgrn_norm_g=hgrn_norm_g, lb_logits=lb_logits, ln_mix_g=ln_mix_g, ln_mix_b=ln_mix_b, ln_ffn_g=ln_ffn_g,
               ln_ffn_b=ln_ffn_b, ffn_w_up=ffn_w_up, ffn_w_down=ffn_w_down)
    mom = dict(attn_w_in=m_attn_w_in, attn_w_out=m_attn_w_out, hgrn_w_in=m_hgrn_w_in, hgrn_w_out=m_hgrn_w_out,
               hgrn_norm_g=m_hgrn_norm_g, lb_logits=m_lb_logits, ln_mix_g=m_ln_mix_g, ln_mix_b=m_ln_mix_b,
               ln_ffn_g=m_ln_ffn_g, ln_ffn_b=m_ln_ffn_b, ffn_w_up=m_ffn_w_up, ffn_w_down=m_ffn_w_down)
    vel = dict(attn_w_in=v_attn_w_in, attn_w_out=v_attn_w_out, hgrn_w_in=v_hgrn_w_in, hgrn_w_out=v_hgrn_w_out,
               hgrn_norm_g=v_hgrn_norm_g, lb_logits=v_lb_logits, ln_mix_g=v_ln_mix_g, ln_mix_b=v_ln_mix_b,
               ln_ffn_g=v_ln_ffn_g, ln_ffn_b=v_ln_ffn_b, ffn_w_up=v_ffn_w_up, ffn_w_down=v_ffn_w_down)
    me = 4 * lax.axis_index("x") + 2 * lax.axis_index("y") + lax.axis_index("c")

    src = {"attn_w_in": attn_w_in[0], "attn_w_out": attn_w_out[0], "hgrn_w_in": hgrn_w_in[0], "hgrn_w_out": hgrn_w_out[0],
           "ffn_w_up0": ffn_w_up[0], "ffn_w_down0": ffn_w_down[0], "ffn_w_up1": ffn_w_up[1], "ffn_w_down1": ffn_w_down[1]}
    gathers, got = {}, {}

    def start_gather(stage, deps):
        shards, lands = [], []
        for nm in GATHER_STAGES[stage]:
            sh = hgrn_norm_g if nm == "hgrn_norm_g" else _cast_bf16(src[nm], name=f"cast_{nm}")
            ax = GATHER_AXIS[nm]
            shape = list(sh.shape)
            shape[ax] *= N_DEV
            shards.append(sh)
            lands.append(lax.dynamic_update_slice_in_dim(lax.empty(tuple(shape), sh.dtype), sh, me * sh.shape[ax], ax))
        gathers[stage] = _xchg_start(shards, lands, [GATHER_AXIS[nm] for nm in GATHER_STAGES[stage]], gather=True,
                                     name=f"gather_start_{stage}", deps=deps)
        return [gathers[stage]["token"]]

    def get_w(name, after):
        deps = []
        if name not in got:
            group, stage, then = [w for w in GATHER_WAITS if name in w[0]][0]
            xc = gathers[stage]
            res = _xchg_wait(xc, [GATHER_STAGES[stage].index(nm) for nm in group], xc["token"] if after is None else after,
                             name=f"gather_wait_{group[0]}")
            got.update(zip(group, res))
            if then is not None:
                deps = start_gather(then, [res[0]])
        return got[name], deps

    first = start_gather(0, [])

    scattered = {}

    def on_grads(tag, grads):
        gnames = list(grads)
        axes = [SCATTER_AXIS[nm] for nm in gnames]
        stacks = []
        for nm, ax in zip(gnames, axes):
            own = _own_block(grads[nm], ax, me)
            stack = lax.empty((N_DEV, *own.shape[1:]), own.dtype)
            stacks.append(lax.dynamic_update_slice_in_dim(stack, own, me, 0))
        scattered[tag] = (gnames, _xchg_start([grads[nm] for nm in gnames], stacks, axes, gather=False,
                                              name=f"scatter_start_{tag}"))
        return [scattered[tag][1]["token"]]

    sm = jax.nn.softmax(lb_logits, axis=0)
    csum = jnp.cumsum(sm, axis=0)
    small = dict(lb=(csum - csum[0:1])[1:2], lb_logits=lb_logits, ln_mix_g=ln_mix_g, ln_mix_b=ln_mix_b,
                 ln_ffn_g=ln_ffn_g, ln_ffn_b=ln_ffn_b)
    sq_sum, grad_x = _local_step(x[0], loss_target[0], get_w, small, on_grads, deps=first)
    loss = lax.psum(0.5 * sq_sum / x.shape[-1], ("x", "y", "c"))
    out = {}

    def stack_small(src_):
        rows = [None] * SMALL_ROWS
        for name in SMALL:
            rows[SMALL_ROW[name]], rows[SMALL_ROW[name] + 1] = src_[name][0:1], src_[name][1:2]
        zero = jnp.zeros((1, x.shape[-1]), F32)
        return jnp.concatenate([zero if r is None else r for r in rows], axis=0)[None]

    def update(name, slabs):
        shape = wts[name].shape
        out[name] = [r.reshape(shape) for r in _adamw(slabs, wts[name], mom[name], vel[name], name=f"adamw_{name}")]
        return out[name][0]

    slabs, after = {}, grad_x
    for tag, (gnames, xc) in scattered.items():
        slabs.update(zip(gnames, _xchg_wait(xc, list(range(len(gnames))), after, name=f"scatter_wait_{tag}")))
        if tag == "ffn1":
            continue
        if tag == "ffn0":
            update("ffn_w_down", [slabs["ffn_w_down0"], slabs["ffn_w_down1"]])
            after = update("ffn_w_up", [slabs["ffn_w_up0"], slabs["ffn_w_up1"]])
        elif tag == "hgrn":
            update("hgrn_w_out", [slabs["hgrn_w_out"]])
            after = update("hgrn_w_in", [slabs["hgrn_w_in"]])
        elif tag == "attn_out":
            after = update("attn_w_out", [slabs["attn_w_out"]])
        else:
            after = update("attn_w_in", [slabs["attn_w_in"]])
    res = _adamw([slabs["small"]], stack_small(wts), stack_small(mom), stack_small(vel), name="adamw_small")
    for name in SMALL:
        out[name] = [r[0, SMALL_ROW[name]:SMALL_ROW[name] + 2] for r in res]
    ng = hgrn_norm_g.shape[-1]
    ng_slabs = lax.dynamic_slice(slabs["small"], (0, NORM_G_ROW, me * ng), (N_DEV, 1, ng))
    out["hgrn_norm_g"] = [r[0] for r in _adamw([ng_slabs], hgrn_norm_g[None], m_hgrn_norm_g[None],
                                                v_hgrn_norm_g[None], name="adamw_norm_g")]
    order =("attn_w_in", "attn_w_out", "hgrn_w_in", "hgrn_w_out", "hgrn_norm_g", "lb_logits", "ln_mix_g", "ln_mix_b",
             "ln_ffn_g", "ln_ffn_b", "ffn_w_up", "ffn_w_down")
    return (loss, grad_x[None], *[out[nm][0] for nm in order], *[out[nm][1] for nm in order],
            *[out[nm][2] for nm in order], *[out[nm][3] for nm in order])
```

```python
import jax
import jax.numpy as jnp
from jax import lax
from jax.experimental import pallas as pl
from jax.experimental.pallas import tpu as pltpu

F32 = jnp.float32
BF16 = jnp.bfloat16

N_DEV = 8
LANES = 128
D_MODEL = 1024
ATTN_HEAD_DIM = 64
ATTN_HEADS = 16
ATTN_BLK = 128
DILATIONS = (1, 4, 16)
ROPE_THETA = 10000.0
HGRN_HEADS = 8
HGRN_CHUNK = 64
D_FF = 4096
LN_EPS = 1e-5
RMS_EPS = 1e-6
DEPTH = 2
ALPHA = (2 * DEPTH) ** 0.25
ADAM_LR, ADAM_B1, ADAM_B2, ADAM_EPS, ADAM_WD, ADAM_STEP = 0.001, 0.9, 0.999, 1e-08, 0.01, 10
VMEM_LIMIT = 48 * 1024 * 1024

_NT = (((1,), (1,)), ((), ()))
_TN = (((0,), (0,)), ((), ()))


def _dot(a, b):
    return jnp.dot(a, b, preferred_element_type=F32)


def _dot_nt(a, b):
    return lax.dot_general(a, b, _NT, preferred_element_type=F32)


def _dot_tn(a, b):
    return lax.dot_general(a, b, _TN, preferred_element_type=F32)


def _split3(x):
    p1 = x.astype(BF16)
    r1 = x - p1.astype(F32)
    p2 = r1.astype(BF16)
    p3 = (r1 - p2.astype(F32)).astype(BF16)
    return p1, p2, p3


def _exact_dot(sel, x):
    p1, p2, p3 = _split3(x)
    return _dot(sel, p1) + _dot(sel, p2) + _dot(sel, p3)


def _exact_dot_r(x, sel):
    p1, p2, p3 = _split3(x)
    return _dot(p1, sel) + _dot(p2, sel) + _dot(p3, sel)


def _params(sem=None):
    return pltpu.CompilerParams(dimension_semantics=sem, vmem_limit_bytes=VMEM_LIMIT)


def _mm(a, b, mode, *, name, m, n, k, tm=1024, tn=1024, tk=1024, out_dtypes=(F32,), epi=None, a_pre=None,
        tile_extras=(), row_extras=(), a_split=1, b_split=1, out_split=1,
        b_col_off=0, b_k_off=0, out_col_off=0, out_cols=None, alias=None, epi_wants_j=False, deps=(), t_out=False):
    tm, tn, tk = min(tm, m), min(tn, n), min(tk, k)
    assert m % tm == 0 and n % tn == 0 and k % tk == 0, (name, m, n, k, tm, tn, tk)
    gm, gn, gk = m // tm, n // tn, k // tk
    if mode in ("nn", "nt"):
        if a_split > 1:
            kc = (k // a_split) // tk
            a_spec = pl.BlockSpec((None, tm, tk), lambda i, j, kk: (kk // kc, i, kk % kc))
        else:
            a_spec = pl.BlockSpec((tm, tk), lambda i, j, kk: (i, kk))
    else:
        a_spec = pl.BlockSpec((tk, tm), lambda i, j, kk: (kk, i))
    if mode in ("nn", "tn"):
        if b_split > 1:
            nc = (n // b_split) // tn
            b_spec = pl.BlockSpec((None, tk, tn), lambda i, j, kk: (j // nc, kk, j % nc))
        else:
            b_spec = pl.BlockSpec((tk, tn), lambda i, j, kk: (kk + b_k_off, j + b_col_off))
    else:
        b_spec = pl.BlockSpec((tn, tk), lambda i, j, kk: (j + b_col_off, kk + b_k_off))
    if out_split > 1:
        nco = (n // out_split) // tn
        o_spec = pl.BlockSpec((None, tm, tn), lambda i, j, kk: (j // nco, i, j % nco))
        o_shape = (out_split, m, n // out_split)
    else:
        o_spec = pl.BlockSpec((tm, tn), lambda i, j, kk: (i, j + out_col_off))
        o_shape = (m, out_cols if out_cols is not None else n)
    n_ex = len(tile_extras) + len(row_extras)
    n_out = len(out_dtypes)
    if epi is None:
        def epi(acc):
            return (acc,)
    dot = {"nn": _dot, "nt": _dot_nt, "tn": _dot_tn}[mode]

    def body(*refs):
        a_ref, b_ref = refs[0], refs[1]
        ex = refs[2:2 + n_ex]
        outs = refs[2 + n_ex + (1 if alias is not None else 0) + len(deps):][:n_out + (1 if t_out else 0)]
        jj = pl.program_id(1)

        def product():
            av = a_ref[...]
            if a_pre is not None:
                av = a_pre(av)
            return dot(av.astype(BF16), b_ref[...].astype(BF16))

        def finish(total):
            lead = (jj,) if epi_wants_j else ()
            res = epi(*lead, total, *[e[...] for e in ex])
            for o, r in zip(outs, res):
                o[...] = r.astype(o.dtype)
            if t_out:
                outs[n_out][...] = res[0].astype(outs[n_out].dtype).T

        if gk == 1:
            finish(product())
        else:
            acc = refs[-1]
            kk = pl.program_id(2)

            @pl.when(kk == 0)
            def _():
                acc[...] = product()

            @pl.when(kk > 0)
            def _():
                acc[...] += product()

            @pl.when(kk == gk - 1)
            def _():
                finish(acc[...])

    in_specs = [a_spec, b_spec] + [o_spec] * len(tile_extras)
    in_specs += [pl.BlockSpec((tm, r.shape[1]), lambda i, j, kk: (i, 0)) for r in row_extras]
    args = [a, b] + list(tile_extras) + list(row_extras)
    io_alias = {}
    if alias is not None:
        in_specs.append(pl.BlockSpec(memory_space=pl.ANY))
        args.append(alias)
        io_alias = {len(args) - 1: 0}
    in_specs += [pl.BlockSpec(memory_space=pl.ANY)] * len(deps)
    args += list(deps)
    out_specs = [o_spec] * n_out
    out_shape = [jax.ShapeDtypeStruct(o_shape, dt) for dt in out_dtypes]
    if t_out:
        assert out_split > 1
        out_specs.append(pl.BlockSpec((None, tn, tm), lambda i, j, kk: (j // nco, j % nco, i)))
        out_shape.append(jax.ShapeDtypeStruct((out_split, n // out_split, m), out_dtypes[0]))
    out = pl.pallas_call(
        body, name=name, grid=(gm, gn, gk), in_specs=in_specs, out_specs=out_specs, out_shape=out_shape,
        scratch_shapes=[pltpu.VMEM((tm, tn), F32)] if gk > 1 else [],
        input_output_aliases=io_alias,
        compiler_params=_params(("parallel", "parallel", "arbitrary")),
    )(*args)
    return out[0] if len(out) == 1 else out


def _rope_tables(seq, dil):
    pos = jnp.arange(seq, dtype=jnp.int32).reshape(seq // dil, dil).T.reshape(seq)
    half = ATTN_HEAD_DIM // 2
    inv = ROPE_THETA ** (-jnp.arange(half, dtype=F32) * (2.0 / ATTN_HEAD_DIM))
    ang = pos.astype(F32)[:, None] * inv[None, :]
    cos, sin = jnp.cos(ang), jnp.sin(ang)
    reps = LANES // ATTN_HEAD_DIM
    return (jnp.tile(jnp.concatenate([cos, cos], axis=1), (1, reps)),
            jnp.tile(jnp.concatenate([-sin, sin], axis=1), (1, reps)))


def _rotate(x, c, ss, sign=1.0):
    w = x.shape[-1]
    half = ATTN_HEAD_DIM // 2
    lane = lax.broadcasted_iota(jnp.int32, x.shape, 1)
    first = (lane % ATTN_HEAD_DIM) < half
    partner = jnp.where(first, pltpu.roll(x, w - half, 1), pltpu.roll(x, half, 1))
    reps = w // LANES
    if reps > 1:
        c = jnp.concatenate([c] * reps, axis=1)
        ss = jnp.concatenate([ss] * reps, axis=1)
    return x * c + sign * (partner * ss)


ROW_TILE = 512


def _ln_fwd(x, y, g, b, *, name):
    s, d = x.shape

    def body(x_ref, y_ref, g_ref, b_ref, o_ref, ob_ref, xh_ref, r_ref):
        u = ALPHA * x_ref[...] + y_ref[...]
        mu = jnp.mean(u, axis=-1, keepdims=True)
        uc = u - mu
        var = jnp.mean(uc * uc, axis=-1, keepdims=True)
        rstd = lax.rsqrt(var + LN_EPS)
        xh = uc * rstd
        out = xh * g_ref[...] + b_ref[...]
        o_ref[...] = out
        ob_ref[...] = out.astype(BF16)
        xh_ref[...] = xh
        r_ref[...] = rstd

    row = pl.BlockSpec((ROW_TILE, d), lambda i: (i, 0))
    vec = pl.BlockSpec((1, d), lambda i: (0, 0))
    return pl.pallas_call(
        body, name=name, grid=(s // ROW_TILE,), in_specs=[row, row, vec, vec],
        out_specs=[row, row, row, pl.BlockSpec((ROW_TILE, 1), lambda i: (i, 0))],
        out_shape=[jax.ShapeDtypeStruct((s, d), F32), jax.ShapeDtypeStruct((s, d), BF16),
                   jax.ShapeDtypeStruct((s, d), F32), jax.ShapeDtypeStruct((s, 1), F32)],
        compiler_params=_params(("parallel",)),
    )(x, y, g, b)


def _ln_bwd(dy, xhat, rstd, g, *, name):
    s, d = dy.shape

    def body(dy_ref, xh_ref, r_ref, g_ref, du_ref, dub_ref, dg_ref, db_ref):
        @pl.when(pl.program_id(0) == 0)
        def _():
            dg_ref[...] = jnp.zeros_like(dg_ref)
            db_ref[...] = jnp.zeros_like(db_ref)

        dyv = dy_ref[...]
        xh = xh_ref[...]
        dxh = dyv * g_ref[...]
        m1 = jnp.mean(dxh, axis=-1, keepdims=True)
        m2 = jnp.mean(dxh * xh, axis=-1, keepdims=True)
        du = r_ref[...] * (dxh - m1 - xh * m2)
        du_ref[...] = du
        dub_ref[...] = du.astype(BF16)
        dg_ref[...] += jnp.sum(dyv * xh, axis=0, keepdims=True)
        db_ref[...] += jnp.sum(dyv, axis=0, keepdims=True)

    row = pl.BlockSpec((ROW_TILE, d), lambda i: (i, 0))
    vec = pl.BlockSpec((1, d), lambda i: (0, 0))
    return pl.pallas_call(
        body, name=name, grid=(s // ROW_TILE,),
        in_specs=[row, row, pl.BlockSpec((ROW_TILE, 1), lambda i: (i, 0)), vec],
        out_specs=[row, row, vec, vec],
        out_shape=[jax.ShapeDtypeStruct((s, d), F32), jax.ShapeDtypeStruct((s, d), BF16),
                   jax.ShapeDtypeStruct((1, d), F32), jax.ShapeDtypeStruct((1, d), F32)],
        compiler_params=_params(("arbitrary",)),
    )(dy, xhat, rstd, g)


def _loss_head(y, target):
    s, d = y.shape

    def body(y_ref, t_ref, dy_ref, sq_ref):
        @pl.when(pl.program_id(0) == 0)
        def _():
            sq_ref[...] = jnp.zeros_like(sq_ref)

        e = y_ref[...] - t_ref[...]
        dy_ref[...] = e * (1.0 / d)
        sq_ref[...] += jnp.sum(e * e, axis=0, keepdims=True)

    row = pl.BlockSpec((ROW_TILE, d), lambda i: (i, 0))
    vec = pl.BlockSpec((1, d), lambda i: (0, 0))
    return pl.pallas_call(
        body, name="loss_head", grid=(s // ROW_TILE,), in_specs=[row, row], out_specs=[row, vec],
        out_shape=[jax.ShapeDtypeStruct((s, d), F32), jax.ShapeDtypeStruct((1, d), F32)],
        compiler_params=_params(("arbitrary",)),
    )(y, target)


POS_BLK = 2048


def _class_rows(r, dil):
    return pl.ds(r, POS_BLK // dil, stride=dil) if dil > 1 else pl.ds(0, POS_BLK)


def _class_view(a, dil):
    s, w = a.shape
    return a.reshape(dil, s // dil, w)


def _class_spec(dil, all_tiles=True):
    if all_tiles:
        return pl.BlockSpec((dil, POS_BLK // dil, LANES), lambda i, t: (0, i, t))
    return pl.BlockSpec((dil, POS_BLK // dil, LANES), lambda i, t: (0, i, 0))


def _pos_spec(all_tiles=True):
    if all_tiles:
        return pl.BlockSpec((POS_BLK, LANES), lambda i, t: (i, t))
    return pl.BlockSpec((POS_BLK, LANES), lambda i, t: (i, 0))


def _prep_x(x, deps=()):
    s, d = x.shape

    def body(x_ref, *refs):
        outs = refs[len(deps):]
        for dil, o_ref in zip(DILATIONS, outs):
            for r in range(dil):
                o_ref[r] = x_ref[_class_rows(r, dil), :].astype(BF16)

    outs = pl.pallas_call(
        body, name="prep_x", grid=(s // POS_BLK, d // LANES),
        in_specs=[_pos_spec()] + [pl.BlockSpec(memory_space=pl.ANY)] * len(deps),
        out_specs=[_class_spec(dil) for dil in DILATIONS],
        out_shape=[jax.ShapeDtypeStruct((dil, s // dil, d), BF16) for dil in DILATIONS],
        compiler_params=_params(("parallel", "parallel")),
    )(x, *deps)
    return [o.reshape(s, d) for o in outs]


def _head_expand_matrix():
    h = lax.broadcasted_iota(jnp.int32, (LANES, D_MODEL), 0)
    l = lax.broadcasted_iota(jnp.int32, (LANES, D_MODEL), 1)
    return (l // ATTN_HEAD_DIM == h).astype(BF16)


def _attn_fwd(qkv, dil, *, name):
    _, s, d = qkv.shape
    nq = s // ATTN_BLK
    per = nq // dil
    scale = ATTN_HEAD_DIM ** -0.5
    tiles = d // LANES

    def body(q_ref, kc_ref, kp_ref, vc_ref, vp_ref, o_ref, lse_ref):
        qb = pl.program_id(0)
        first = (qb % per) == 0
        qi = lax.broadcasted_iota(jnp.int32, (ATTN_BLK, 2 * ATTN_BLK), 0)
        kj = lax.broadcasted_iota(jnp.int32, (ATTN_BLK, 2 * ATTN_BLK), 1)
        dist = qi + ATTN_BLK - kj
        valid = (dist >= 0) & (dist <= ATTN_BLK) & ((kj >= ATTN_BLK) | jnp.logical_not(first))
        lane = lax.broadcasted_iota(jnp.int32, (ATTN_BLK, LANES), 1)
        lse_tile = jnp.zeros((ATTN_BLK, LANES), F32)
        for t in range(tiles):
            cols = pl.ds(t * LANES, LANES)
            q = q_ref[:, cols]
            k2 = jnp.concatenate([kp_ref[:, cols], kc_ref[:, cols]], axis=0)
            v2 = jnp.concatenate([vp_ref[:, cols], vc_ref[:, cols]], axis=0)
            o_tile = jnp.zeros((ATTN_BLK, LANES), F32)
            for hh in range(2):
                in_head = (lane // ATTN_HEAD_DIM) == hh
                qh = jnp.where(in_head, q, jnp.zeros_like(q))
                sc = _dot_nt(qh, k2) * scale
                sc = jnp.where(valid, sc, -jnp.inf)
                mx = jnp.max(sc, axis=-1, keepdims=True)
                p = jnp.exp(sc - mx)
                l = jnp.sum(p, axis=-1, keepdims=True)
                oh = _dot(p.astype(BF16), v2) / l
                o_tile = jnp.where(in_head, oh, o_tile)
                lse_tile = jnp.where(lane == 2 * t + hh, mx + jnp.log(l), lse_tile)
            o_ref[:, cols] = o_tile
        lse_ref[...] = lse_tile

    def blk(piece, prev):
        if prev:
            return pl.BlockSpec((None, ATTN_BLK, d), lambda i: (piece, jnp.maximum(i - 1, 0), 0))
        return pl.BlockSpec((None, ATTN_BLK, d), lambda i: (piece, i, 0))

    return pl.pallas_call(
        body, name=name, grid=(nq,),
        in_specs=[blk(0, False), blk(1, False), blk(1, True), blk(2, False), blk(2, True)],
        out_specs=[pl.BlockSpec((ATTN_BLK, d), lambda i: (i, 0)), pl.BlockSpec((ATTN_BLK, LANES), lambda i: (i, 0))],
        out_shape=[jax.ShapeDtypeStruct((s, d), F32), jax.ShapeDtypeStruct((s, LANES), F32)],
        compiler_params=_params(("parallel",)),
    )(qkv, qkv, qkv, qkv, qkv)


def _attn_combine(os_, lses):
    s, d = os_[0].shape
    sel = _head_expand_matrix()

    def body(o0, o1, o2, l0, l1, l2, sel_ref, of_ref, ob_ref, lt_ref, o_pos, l_pos):
        for g, (dil, o_ref, l_ref) in enumerate(zip(DILATIONS, (o0, o1, o2), (l0, l1, l2))):
            for r in range(dil):
                o_pos[g, _class_rows(r, dil), :] = o_ref[r]
                l_pos[g, _class_rows(r, dil), :] = l_ref[r]
        la, lb_, lc = l_pos[0], l_pos[1], l_pos[2]
        mx = jnp.maximum(jnp.maximum(la, lb_), lc)
        es = (jnp.exp(la - mx), jnp.exp(lb_ - mx), jnp.exp(lc - mx))
        z = es[0] + es[1] + es[2]
        lt_ref[...] = mx + jnp.log(z)
        acc = jnp.zeros((POS_BLK, LANES), F32)
        for g in range(3):
            acc += _exact_dot_r(es[g] / z, sel_ref[...]) * o_pos[g]
        of_ref[...] = acc
        ob_ref[...] = acc.astype(BF16)

    return pl.pallas_call(
        body, name="attn_combine", grid=(s // POS_BLK, d // LANES),
        in_specs=[_class_spec(dil) for dil in DILATIONS] + [_class_spec(dil, False) for dil in DILATIONS]
        + [pl.BlockSpec((LANES, LANES), lambda i, t: (0, t))],
        out_specs=[_pos_spec(), _pos_spec(), _pos_spec(False)],
        out_shape=[jax.ShapeDtypeStruct((s, d), F32), jax.ShapeDtypeStruct((s, d), BF16),
                   jax.ShapeDtypeStruct((s, LANES), F32)],
        scratch_shapes=[pltpu.VMEM((3, POS_BLK, LANES), F32), pltpu.VMEM((3, POS_BLK, LANES), F32)],
        compiler_params=_params(("parallel", "arbitrary")),
    )(*[_class_view(o, dil) for o, dil in zip(os_, DILATIONS)],
      *[_class_view(l, dil) for l, dil in zip(lses, DILATIONS)], sel)


AUX_PER_TILE = 12


def _aux_placement():
    h = lax.broadcasted_iota(jnp.int32, (6, LANES, LANES), 1)
    l = lax.broadcasted_iota(jnp.int32, (6, LANES, LANES), 2)
    j = lax.broadcasted_iota(jnp.int32, (6, LANES, LANES), 0)
    target = AUX_PER_TILE * (h // 2) + 3 * (h % 2) + jnp.where(j < 3, j, 3 + j)
    return ((l == target) & (h < ATTN_HEADS)).astype(BF16)


def _attn_bwd_prep(do, o, lse):
    s, d = do.shape
    tiles = d // LANES
    sel_t = _head_expand_matrix().T

    def body(do_ref, o_ref, l_ref, sel_ref, place_ref, *refs):
        outs, delta, aux = refs[:9], refs[9], refs[10]
        t = pl.program_id(1)
        part = _exact_dot_r(do_ref[...] * o_ref[...], sel_ref[...])

        @pl.when(t == 0)
        def _():
            delta[...] = part

        @pl.when(t > 0)
        def _():
            delta[...] += part

        for g, dil in enumerate(DILATIONS):
            for r in range(dil):
                blk = do_ref[_class_rows(r, dil), :].astype(BF16)
                outs[g][r] = blk
                outs[3 + g][r] = blk.T

        @pl.when(t == tiles - 1)
        def _():
            pieces = _split3(l_ref[...]) + _split3(delta[...])
            acc = _dot(pieces[0], place_ref[0])
            for j in range(1, 6):
                acc += _dot(pieces[j], place_ref[j])
            aux[...] = acc
            for g, dil in enumerate(DILATIONS):
                for r in range(dil):
                    outs[6 + g][r] = aux[_class_rows(r, dil), :].astype(BF16)

    outs = pl.pallas_call(
        body, name="attn_bwd_prep", grid=(s // POS_BLK, tiles),
        in_specs=[_pos_spec(), _pos_spec(), _pos_spec(False), pl.BlockSpec((LANES, LANES), lambda i, t: (t, 0)),
                  pl.BlockSpec((6, LANES, LANES), lambda i, t: (0, 0, 0))],
        out_specs=[_class_spec(dil) for dil in DILATIONS]
        + [pl.BlockSpec((dil, LANES, POS_BLK // dil), lambda i, t: (0, t, i)) for dil in DILATIONS]
        + [_class_spec(dil, False) for dil in DILATIONS],
        out_shape=[jax.ShapeDtypeStruct((dil, s // dil, d), BF16) for dil in DILATIONS]
        + [jax.ShapeDtypeStruct((dil, d, s // dil), BF16) for dil in DILATIONS]
        + [jax.ShapeDtypeStruct((dil, s // dil, LANES), BF16) for dil in DILATIONS],
        scratch_shapes=[pltpu.VMEM((POS_BLK, LANES), F32), pltpu.VMEM((POS_BLK, LANES), F32)],
        compiler_params=_params(("parallel", "arbitrary")),
    )(do, o, lse, sel_t, _aux_placement())
    return ([a.reshape(s, d) for a in outs[0:3]], list(outs[3:6]), [a.reshape(s, LANES) for a in outs[6:9]])


def _attn_bwd(qkv, qkv_t, do, do_t, aux, tables, dil, *, name):
    _, s, d = qkv.shape
    nq = s // ATTN_BLK
    per = nq // dil
    scale = ATTN_HEAD_DIM ** -0.5
    assert scale == 0.125
    tiles = d // LANES
    half = ATTN_HEAD_DIM

    def body(qd_ref, qo_ref, k_ref, qtd_ref, qto_ref, kt_ref, vt_ref, dod_ref, doo_ref, dotd_ref, doto_ref,
             auxd_ref, auxo_ref, c_ref, ss_ref, out_ref, carry):
        kb = pl.program_id(0)

        @pl.when(kb == 0)
        def _():
            carry[...] = jnp.zeros_like(carry)

        has_next = (kb % per) != (per - 1)
        qi = lax.broadcasted_iota(jnp.int32, (ATTN_BLK, 2 * ATTN_BLK), 0)
        kj = lax.broadcasted_iota(jnp.int32, (ATTN_BLK, 2 * ATTN_BLK), 1) % ATTN_BLK
        valid_d = kj <= qi
        valid_o = (kj >= qi) & has_next
        lane = lax.broadcasted_iota(jnp.int32, (ATTN_BLK, LANES), 1)
        row = lax.broadcasted_iota(jnp.int32, (LANES, ATTN_BLK), 0)
        side = lax.broadcasted_iota(jnp.int32, (LANES, 2 * ATTN_BLK), 0)
        first = lax.broadcasted_iota(jnp.int32, (LANES, 2 * ATTN_BLK), 1) < ATTN_BLK
        c, ss = c_ref[...], ss_ref[...]
        zero = jnp.zeros((), BF16)
        for t in range(tiles):
            cols = pl.ds(t * LANES, LANES)
            base = AUX_PER_TILE * t
            hit = lambda lo: ((first & (side >= base + lo) & (side < base + lo + 3))
                              | (jnp.logical_not(first) & (side >= base + lo + 3) & (side < base + lo + 6)))
            k = k_ref[:, cols]
            kt, vt = kt_ref[cols, :], vt_ref[cols, :]
            kk = jnp.concatenate([jnp.where(lane < half, k, zero), jnp.where(lane >= half, k, zero)], axis=0)
            kk_t = jnp.concatenate([
                jnp.concatenate([jnp.where(row < half, kt, zero), jnp.where(row >= half, kt, zero)], axis=1),
                jnp.where(hit(0), -1.0 / scale, 0.0).astype(BF16)], axis=0)
            vv_t = jnp.concatenate([
                jnp.concatenate([jnp.where(row < half, vt, zero), jnp.where(row >= half, vt, zero)], axis=1),
                jnp.where(hit(6), -1.0, 0.0).astype(BF16)], axis=0)
            dk_t = jnp.zeros((LANES, 2 * ATTN_BLK), F32)
            dv_t = jnp.zeros((LANES, 2 * ATTN_BLK), F32)
            dq = [carry[:, cols], None]
            for which, (q_ref, qt_ref, do_ref, dot_ref, aux_ref, valid) in enumerate((
                    (qd_ref, qtd_ref, dod_ref, dotd_ref, auxd_ref, valid_d),
                    (qo_ref, qto_ref, doo_ref, doto_ref, auxo_ref, valid_o))):
                auxv = aux_ref[...]
                sc = _dot(jnp.concatenate([q_ref[:, cols], auxv], axis=1), kk_t)
                p = jnp.where(valid, jnp.exp(sc * scale), 0.0)
                dpd = _dot(jnp.concatenate([do_ref[:, cols], auxv], axis=1), vv_t)
                ds = (p * dpd * scale).astype(BF16)
                dv_t += _dot(dot_ref[cols, :], p.astype(BF16))
                dk_t += _dot(qt_ref[cols, :], ds)
                part = _dot(ds, kk)
                dq[which] = part if dq[which] is None else dq[which] + part
            carry[:, cols] = dq[1]
            dk = jnp.where(row < half, dk_t[:, :ATTN_BLK], dk_t[:, ATTN_BLK:]).T
            dv = jnp.where(row < half, dv_t[:, :ATTN_BLK], dv_t[:, ATTN_BLK:]).T
            out_ref[0, :, cols] = _rotate(dq[0], c, ss, -1.0).astype(BF16)
            out_ref[1, :, cols] = _rotate(dk, c, ss, -1.0).astype(BF16)
            out_ref[2, :, cols] = dv.astype(BF16)

    def nxt(i):
        return jnp.minimum(i + 1, nq - 1)

    def piece(p, shift):
        if shift:
            return pl.BlockSpec((None, ATTN_BLK, d), lambda i: (p, nxt(i), 0))
        return pl.BlockSpec((None, ATTN_BLK, d), lambda i: (p, i, 0))

    def piece_t(p, shift):
        if shift:
            return pl.BlockSpec((None, d, ATTN_BLK), lambda i: (p, 0, nxt(i)))
        return pl.BlockSpec((None, d, ATTN_BLK), lambda i: (p, 0, i))

    def rows(width, shift):
        if shift:
            return pl.BlockSpec((ATTN_BLK, width), lambda i: (nxt(i), 0))
        return pl.BlockSpec((ATTN_BLK, width), lambda i: (i, 0))

    def do_t_spec(shift):
        if shift:
            return pl.BlockSpec((None, d, ATTN_BLK), lambda i: (nxt(i) // per, 0, nxt(i) % per))
        return pl.BlockSpec((None, d, ATTN_BLK), lambda i: (i // per, 0, i % per))

    return pl.pallas_call(
        body, name=name, grid=(nq,),
        in_specs=[piece(0, False), piece(0, True), piece(1, False),
                  piece_t(0, False), piece_t(0, True), piece_t(1, False), piece_t(2, False),
                  rows(d, False), rows(d, True), do_t_spec(False), do_t_spec(True),
                  rows(LANES, False), rows(LANES, True), rows(LANES, False), rows(LANES, False)],
        out_specs=pl.BlockSpec((3, ATTN_BLK, d), lambda i: (0, i, 0)),
        out_shape=jax.ShapeDtypeStruct((3, s, d), BF16),
        scratch_shapes=[pltpu.VMEM((ATTN_BLK, d), F32)],
        compiler_params=_params(("arbitrary",)),
    )(qkv, qkv, qkv, qkv_t, qkv_t, qkv_t, qkv_t, do, do, do_t, do_t, aux, aux, *tables)


def _dx_combine(du, parts):
    s, d = du.shape

    def body(du_ref, p0, p1, p2, out_ref):
        out_ref[...] = ALPHA * du_ref[...] + p0[0]
        for dil, p_ref in zip(DILATIONS[1:], (p1, p2)):
            for r in range(dil):
                out_ref[_class_rows(r, dil), :] += p_ref[r]

    return pl.pallas_call(
        body, name="dx_combine", grid=(s // POS_BLK, d // LANES),
        in_specs=[_pos_spec()] + [_class_spec(dil) for dil in DILATIONS], out_specs=_pos_spec(),
        out_shape=jax.ShapeDtypeStruct((s, d), F32),
        compiler_params=_params(("parallel", "parallel")),
    )(du, *[_class_view(p, dil) for p, dil in zip(parts, DILATIONS)])


HGRN_ROWS = 512


def _tri(lower):
    i = lax.broadcasted_iota(jnp.int32, (HGRN_CHUNK, HGRN_CHUNK), 0)
    j = lax.broadcasted_iota(jnp.int32, (HGRN_CHUNK, HGRN_CHUNK), 1)
    return (j <= i) if lower else (j >= i)


def _hgrn_gates(qr, z, lb):
    sq = jax.nn.sigmoid(qr)
    q = qr * sq
    sg = jax.nn.sigmoid(z)
    f = lb + (1.0 - lb) * sg
    key = (1.0 - lb) * jax.nn.sigmoid(-z)
    return sq, q, sg, f, key


def _hgrn_fwd(p2, lb, norm_g):
    _, s, d = p2.shape
    nblk = s // HGRN_ROWS
    cps = HGRN_ROWS // HGRN_CHUNK

    def body(p_ref, lb_ref, g_ref, o_ref, on_ref, st_ref, state):
        @pl.when(pl.program_id(1) == 0)
        def _():
            state[...] = jnp.zeros_like(state)

        lbv = lb_ref[...]
        gv = g_ref[...]
        ltri = _tri(True)
        lsel = ltri.astype(BF16)
        for cidx in range(cps):
            rows = pl.ds(cidx * HGRN_CHUNK, HGRN_CHUNK)
            _, q, _, f, key = _hgrn_gates(p_ref[0, rows, :], p_ref[1, rows, :], lbv)
            v = p_ref[2, rows, :].astype(BF16)
            b = _exact_dot(lsel, jnp.log(f))
            b_last = b[HGRN_CHUNK - 1:HGRN_CHUNK, :]
            qd = (q * jnp.exp(b)).astype(BF16)
            kd = (key * jnp.exp(-b)).astype(BF16)
            k2 = (key * jnp.exp(b_last - b)).astype(BF16)
            st = state[...]
            st_ref[0, cidx] = st
            a = jnp.where(ltri, _dot_nt(qd, kd), 0.0)
            o = _dot(a.astype(BF16), v) + _dot_nt(qd, st.astype(BF16))
            state[...] = st * jnp.exp(b_last) + _dot_tn(v, k2)
            o_ref[rows, :] = o
            r = lax.rsqrt(jnp.mean(o * o, axis=-1, keepdims=True) + RMS_EPS)
            on_ref[rows, :] = (o * r * gv).astype(BF16)

    vec = pl.BlockSpec((1, LANES), lambda h, c: (0, h))
    col = pl.BlockSpec((HGRN_ROWS, LANES), lambda h, c: (c, h))
    return pl.pallas_call(
        body, name="hgrn_fwd", grid=(HGRN_HEADS, nblk),
        in_specs=[pl.BlockSpec((3, HGRN_ROWS, LANES), lambda h, c: (0, c, h)), vec, vec],
        out_specs=[col, col, pl.BlockSpec((1, cps, LANES, LANES), lambda h, c: (h, c, 0, 0))],
        out_shape=[jax.ShapeDtypeStruct((s, d), F32), jax.ShapeDtypeStruct((s, d), BF16),
                   jax.ShapeDtypeStruct((HGRN_HEADS, s // HGRN_CHUNK, LANES, LANES), F32)],
        scratch_shapes=[pltpu.VMEM((LANES, LANES), F32)],
        compiler_params=_params(("parallel", "arbitrary")),
    )(p2, lb, norm_g)


def _hgrn_bwd(p2, lb, norm_g, o_raw, states, dyn):
    _, s, d = p2.shape
    nblk = s // HGRN_ROWS
    cps = HGRN_ROWS // HGRN_CHUNK

    def body(p_ref, lb_ref, g_ref, o_ref, st_ref, dy_ref, dp_ref, dg_ref, dlb_ref, dstate):
        @pl.when(pl.program_id(1) == 0)
        def _():
            dstate[...] = jnp.zeros_like(dstate)
            dg_ref[...] = jnp.zeros_like(dg_ref)
            dlb_ref[...] = jnp.zeros_like(dlb_ref)

        lbv = lb_ref[...]
        gv = g_ref[...]
        ltri = _tri(True)
        lsel = ltri.astype(BF16)
        usel = _tri(False).astype(BF16)
        last_row = lax.broadcasted_iota(jnp.int32, (HGRN_CHUNK, LANES), 0) == HGRN_CHUNK - 1
        for cidx in reversed(range(cps)):
            rows = pl.ds(cidx * HGRN_CHUNK, HGRN_CHUNK)
            qr, z = p_ref[0, rows, :], p_ref[1, rows, :]
            sq, q, sg, f, key = _hgrn_gates(qr, z, lbv)
            v = p_ref[2, rows, :].astype(BF16)
            b = _exact_dot(lsel, jnp.log(f))
            b_last = b[HGRN_CHUNK - 1:HGRN_CHUNK, :]
            eb, enb, e2 = jnp.exp(b), jnp.exp(-b), jnp.exp(b_last - b)
            dec = jnp.exp(b_last)
            qd_f, kd_f, k2_f = q * eb, key * enb, key * e2
            qd, kd, k2 = qd_f.astype(BF16), kd_f.astype(BF16), k2_f.astype(BF16)
            a = jnp.where(ltri, _dot_nt(qd, kd), 0.0).astype(BF16)
            st = st_ref[0, cidx]
            dst = dstate[...]
            o = o_ref[rows, :]
            dyv = dy_ref[rows, :]
            r = lax.rsqrt(jnp.mean(o * o, axis=-1, keepdims=True) + RMS_EPS)
            oh = o * r
            doh = dyv * gv
            do_f = r * (doh - oh * jnp.mean(doh * oh, axis=-1, keepdims=True))
            dg_ref[...] += jnp.sum(dyv * oh, axis=0, keepdims=True)
            do = do_f.astype(BF16)
            dstb = dst.astype(BF16)
            da = jnp.where(ltri, _dot_nt(do, v), 0.0).astype(BF16)
            dv = _dot_tn(a, do) + _dot_nt(k2, dstb)
            dqd = _dot(da, kd) + _dot(do, st.astype(BF16))
            dkd = _dot_tn(da, qd)
            dk2 = _dot(v, dstb)
            ddec = jnp.sum(dst * st, axis=0, keepdims=True)
            dstate[...] = dst * dec + _dot_tn(do, qd)
            dq = dqd * eb
            dkey = dkd * enb + dk2 * e2
            db = dqd * qd_f - dkd * kd_f - dk2 * k2_f
            db_last = jnp.sum(dk2 * k2_f, axis=0, keepdims=True) + ddec * dec
            db = db + jnp.where(last_row, db_last, 0.0)
            dlf = _exact_dot(usel, db)
            one_m_s = 1.0 - sg
            dz = (dlf / f - dkey) * ((1.0 - lbv) * sg * one_m_s)
            dlb_ref[...] += jnp.sum(dlf * one_m_s / f - dkey * jax.nn.sigmoid(-z), axis=0, keepdims=True)
            dp_ref[0, rows, :] = (dq * (sq * (1.0 + qr * (1.0 - sq)))).astype(BF16)
            dp_ref[1, rows, :] = dz.astype(BF16)
            dp_ref[2, rows, :] = dv.astype(BF16)

    def rev(c):
        return nblk - 1 - c

    vec = pl.BlockSpec((1, LANES), lambda h, c: (0, h))
    col = pl.BlockSpec((HGRN_ROWS, LANES), lambda h, c: (rev(c), h))
    p3 = pl.BlockSpec((3, HGRN_ROWS, LANES), lambda h, c: (0, rev(c), h))
    return pl.pallas_call(
        body, name="hgrn_bwd", grid=(HGRN_HEADS, nblk),
        in_specs=[p3, vec, vec, col, pl.BlockSpec((1, cps, LANES, LANES), lambda h, c: (h, rev(c), 0, 0)), col],
        out_specs=[p3, vec, vec],
        out_shape=[jax.ShapeDtypeStruct((3, s, d), BF16), jax.ShapeDtypeStruct((1, d), F32),
                   jax.ShapeDtypeStruct((1, d), F32)],
        scratch_shapes=[pltpu.VMEM((LANES, LANES), F32)],
        compiler_params=_params(("parallel", "arbitrary")),
    )(p2, lb, norm_g, o_raw, states, dyn)


SMALL_ROWS = 16


def _relu2(h):
    r = jnp.maximum(h.astype(F32), 0.0)
    return r * r


def _dact_epi(acc, h):
    return (acc * (2.0 * jnp.maximum(h.astype(F32), 0.0)),)


def _res_epi(scale):
    def epi(acc, r):
        return (acc + scale * r,)
    return epi


def _rope_epi(j, acc, c, ss):
    return (jnp.where(j >= 2, acc, _rotate(acc, c, ss)),)


def _local_step(x, target, get_w, small, on_grads, deps=()):
    s, d = x.shape
    lb = small["lb"]
    gq = 3 * d
    gt = gq // 1024

    def ffn_fwd(xb, w_up, w_down, tag):
        h = _mm(xb, w_up, "nn", name=f"ffn_up_{tag}", m=s, n=D_FF, k=d, out_dtypes=(BF16,))
        y = _mm(h, w_down, "nn", name=f"ffn_down_{tag}", m=s, n=d, k=D_FF, a_pre=_relu2)
        return h, y

    def ffn_bwd(du, dub, xb, h, w_up, w_down, tag):
        dh = _mm(dub, w_down, "nt", name=f"ffn_dact_{tag}", m=s, n=D_FF, k=d, out_dtypes=(BF16,), epi=_dact_epi,
                 tile_extras=(h,))
        g_down = _mm(h, dub, "tn", name=f"ffn_gdown_{tag}", m=D_FF, n=d, k=s, a_pre=_relu2, out_dtypes=(BF16,))
        g_up = _mm(xb, dh, "tn", name=f"ffn_gup_{tag}", m=d, n=D_FF, k=s, out_dtypes=(BF16,))
        after = on_grads(f"ffn{tag}", {f"ffn_w_down{tag}": g_down, f"ffn_w_up{tag}": g_up})
        return _mm(dh, w_up, "nt", name=f"ffn_dx_{tag}", m=s, n=d, k=D_FF, epi=_res_epi(ALPHA), tile_extras=(du,),
                   deps=after)

    xs = _prep_x(x, deps)
    w_ain, after = get_w("attn_w_in", None)
    tabs, qkvs, qkv_ts, o_parts, lse_parts = [], [], [], [], []
    for g, dil in enumerate(DILATIONS):
        tabs.append(_rope_tables(s, dil))
        qkv, qkv_t = _mm(xs[g], w_ain, "nn", name=f"attn_in_{g}", m=s, n=gq, k=d, b_col_off=g * gt, out_split=3,
                         out_dtypes=(BF16,), epi=_rope_epi, epi_wants_j=True, row_extras=tabs[g], t_out=True,
                         deps=after if g == 0 else ())
        qkvs.append(qkv)
        qkv_ts.append(qkv_t)
        o_g, lse_g = _attn_fwd(qkv, dil, name=f"attn_fwd_{g}")
        o_parts.append(o_g)
        lse_parts.append(lse_g)
    o_f, o_b, lse_t = _attn_combine(o_parts, lse_parts)
    w_aout, after = get_w("attn_w_out", o_b)
    y = _mm(o_b, w_aout, "nn", name="attn_out", m=s, n=d, k=d, deps=after)
    x1, x1b, xh1, r1 = _ln_fwd(x, y, small["ln_mix_g"][0:1], small["ln_mix_b"][0:1], name="ln_mix_0")
    w_up0, w_down0 = get_w("ffn_w_up0", o_b)[0], get_w("ffn_w_down0", o_b)[0]
    h0, y = ffn_fwd(x1b, w_up0, w_down0, 0)
    x2, x2b, xh2, r2 = _ln_fwd(x1, y, small["ln_ffn_g"][0:1], small["ln_ffn_b"][0:1], name="ln_ffn_0")
    w_hin, w_hout, norm_g = get_w("hgrn_w_in", x2b)[0], get_w("hgrn_w_out", x2b)[0], get_w("hgrn_norm_g", x2b)[0]
    p2 = _mm(x2b, w_hin, "nn", name="hgrn_in", m=s, n=3 * d, k=d, out_split=3)
    o_raw, o_n, states = _hgrn_fwd(p2, lb, norm_g)
    w_up1, w_down1 = get_w("ffn_w_up1", o_n)[0], get_w("ffn_w_down1", o_n)[0]
    y = _mm(o_n, w_hout, "nn", name="hgrn_out", m=s, n=d, k=d)
    x3, x3b, xh3, r3 = _ln_fwd(x2, y, small["ln_mix_g"][1:2], small["ln_mix_b"][1:2], name="ln_mix_1")
    h1, y = ffn_fwd(x3b, w_up1, w_down1, 1)
    x4, _, xh4, r4 = _ln_fwd(x3, y, small["ln_ffn_g"][1:2], small["ln_ffn_b"][1:2], name="ln_ffn_1")
    dy, sq = _loss_head(x4, target)
    du, dub, dg_ffn1, db_ffn1 = _ln_bwd(dy, xh4, r4, small["ln_ffn_g"][1:2], name="ln_ffn_1_bwd")
    dx3 = ffn_bwd(du, dub, x3b, h1, w_up1, w_down1, 1)
    du, dub, dg_mix1, db_mix1 = _ln_bwd(dx3, xh3, r3, small["ln_mix_g"][1:2], name="ln_mix_1_bwd")
    dyn = _mm(dub, w_hout, "nt", name="hgrn_dout", m=s, n=d, k=d)
    g_hout = _mm(o_n, dub, "tn", name="hgrn_gout", m=d, n=d, k=s, out_dtypes=(BF16,))
    dp2, d_norm_g, d_lb = _hgrn_bwd(p2, lb, norm_g, o_raw, states, dyn)
    g_hin = _mm(x2b, dp2, "tn", name="hgrn_gin", m=d, n=3 * d, k=s, b_split=3, out_dtypes=(BF16,))
    after = on_grads("hgrn", {"hgrn_w_out": g_hout, "hgrn_w_in": g_hin})
    dx2 = _mm(dp2, w_hin, "nt", name="hgrn_dx", m=s, n=d, k=3 * d, a_split=3, epi=_res_epi(ALPHA),
              tile_extras=(du,), deps=after)
    du, dub, dg_ffn0, db_ffn0 = _ln_bwd(dx2, xh2, r2, small["ln_ffn_g"][0:1], name="ln_ffn_0_bwd")
    dx1 = ffn_bwd(du, dub, x1b, h0, w_up0, w_down0, 0)
    du, dub, dg_mix0, db_mix0 = _ln_bwd(dx1, xh1, r1, small["ln_mix_g"][0:1], name="ln_mix_0_bwd")
    g_aout = _mm(o_b, dub, "tn", name="attn_gout", m=d, n=d, k=s, out_dtypes=(BF16,))
    sm1 = jax.nn.softmax(small["lb_logits"], axis=0)
    d_l1 = d_lb * (sm1[0:1] * sm1[1:2])
    zeros = jnp.zeros((SMALL_ROWS - 11, d), F32)
    small_grads = jnp.concatenate([dg_mix0, dg_mix1, db_mix0, db_mix1, dg_ffn0, dg_ffn1, db_ffn0, db_ffn1,
                                   -d_l1, d_l1, d_norm_g, zeros], axis=0)
    after = on_grads("attn_out", {"attn_w_out": g_aout, "small": small_grads})
    do = _mm(dub, w_aout, "nt", name="attn_dout", m=s, n=d, k=d, deps=after)
    do_parts, do_ts, aux_parts = _attn_bwd_prep(do, o_f, lse_t)
    g_ain, dqkvs = None, []
    for g, dil in enumerate(DILATIONS):
        dqkvs.append(_attn_bwd(qkvs[g], qkv_ts[g], do_parts[g], do_ts[g], aux_parts[g], tabs[g], dil,
                               name=f"attn_bwd_{g}"))
        g_ain = _mm(xs[g], dqkvs[g], "tn", name=f"attn_gin_{g}", m=d, n=gq, k=s, b_split=3, out_dtypes=(BF16,),
                    out_col_off=g * gt, out_cols=3 * gq, alias=g_ain)
    after = on_grads("attn_in", {"attn_w_in": g_ain})
    dx_parts = [_mm(dqkvs[g], w_ain, "nt", name=f"attn_dx_{g}", m=s, n=d, k=gq, a_split=3, b_k_off=g * gt,
                    deps=after if g == 0 else ())
                for g in range(len(DILATIONS))]
    return jnp.sum(sq), _dx_combine(du, dx_parts)


def _mesh_place():
    x, y, c = lax.axis_index("x"), lax.axis_index("y"), lax.axis_index("c")
    return x, y, c, 4 * x + 2 * y + c


def _peer(x, y, c, k):
    px = 1 - x if (k >> 2) & 1 else x
    py = 1 - y if (k >> 1) & 1 else y
    pc = 1 - c if k & 1 else c
    return (px, py, pc), 4 * px + 2 * py + pc


def _window(ref, axis, size, idx):
    if axis is None:
        return ref
    sl = [slice(None)] * len(ref.shape)
    sl[axis] = pl.ds(idx * size, size)
    return ref.at[tuple(sl)]


_HBM = pl.BlockSpec(memory_space=pltpu.HBM)
_SEM = pl.BlockSpec(memory_space=pltpu.SEMAPHORE)
_EFFECT = pltpu.SideEffectType.DATAFLOW_SIDE_EFFECTING


def _xchg_ends(src_ref, land_ref, axis, gather, me, other):
    if gather:
        size = src_ref.shape[axis]
        return src_ref, _window(land_ref, axis, size, me), _window(land_ref, axis, size, other)
    size = None if axis is None else src_ref.shape[axis] // N_DEV
    return _window(src_ref, axis, size, other), land_ref.at[me], land_ref.at[other]


def _xchg_start(srcs, lands, axes, *, gather, name, deps=()):
    n = len(srcs)
    nd = len(deps)

    def body(*refs):
        src_refs, land_refs = refs[:n], refs[n:2 * n]
        send, recv = refs[2 * n + nd:3 * n + nd], refs[3 * n + nd:4 * n + nd]
        token = refs[-1]
        x, y, c, me = _mesh_place()
        for k in range(1, N_DEV):
            peer, pidx = _peer(x, y, c, k)
            for i in range(n):
                src, dst, _ = _xchg_ends(src_refs[i], land_refs[i], axes[i], gather, me, pidx)
                pltpu.make_async_remote_copy(
                    src_ref=src, dst_ref=dst, send_sem=send[i].at[k - 1], recv_sem=recv[i].at[k - 1],
                    device_id=peer, device_id_type=pl.DeviceIdType.MESH).start()
        token[...] = jnp.zeros_like(token)

    bufs = list(srcs) + list(lands)
    outs = pl.pallas_call(
        body, name=name,
        out_shape=[pltpu.SemaphoreType.DMA((N_DEV - 1,))] * (2 * n) + [pltpu.HBM(b.shape, b.dtype) for b in bufs]
        + [jax.ShapeDtypeStruct((8, LANES), F32)],
        in_specs=[_HBM] * (2 * n) + [pl.BlockSpec(memory_space=pl.ANY)] * nd,
        out_specs=[_SEM] * (2 * n) + [_HBM] * (2 * n) + [pl.BlockSpec(memory_space=pltpu.VMEM)],
        input_output_aliases={i: 2 * n + i for i in range(2 * n)},
        compiler_params=pltpu.CompilerParams(has_side_effects=_EFFECT),
    )(*[pltpu.with_memory_space_constraint(b, pltpu.HBM) for b in bufs], *deps)
    return dict(send=outs[:n], recv=outs[n:2 * n], srcs=outs[2 * n:3 * n], lands=outs[3 * n:4 * n], token=outs[-1],
                axes=list(axes), gather=gather)


def _xchg_wait(xc, items, after, *, name):
    m = len(items)
    gather = xc["gather"]
    axes = [xc["axes"][i] for i in items]

    def body(*refs):
        src_refs, land_refs = refs[:m], refs[m:2 * m]
        send, recv = refs[2 * m:3 * m], refs[3 * m:4 * m]
        x, y, c, me = _mesh_place()
        for k in range(1, N_DEV):
            peer, pidx = _peer(x, y, c, k)
            for j in range(m):
                src, dst, got = _xchg_ends(src_refs[j], land_refs[j], axes[j], gather, me, pidx)
                pltpu.make_async_remote_copy(
                    src_ref=src, dst_ref=dst, send_sem=send[j].at[k - 1], recv_sem=recv[j].at[k - 1],
                    device_id=peer, device_id_type=pl.DeviceIdType.MESH).wait_send()
                pltpu.make_async_remote_copy(
                    src_ref=src, dst_ref=got, send_sem=send[j].at[k - 1], recv_sem=recv[j].at[k - 1],
                    device_id=peer, device_id_type=pl.DeviceIdType.MESH).wait_recv()

    bufs = [xc["srcs"][i] for i in items] + [xc["lands"][i] for i in items]
    sems = [xc["send"][i] for i in items] + [xc["recv"][i] for i in items]
    outs = pl.pallas_call(
        body, name=name, out_shape=[pltpu.HBM(b.shape, b.dtype) for b in bufs],
        in_specs=[_HBM] * (2 * m) + [_SEM] * (2 * m) + [pl.BlockSpec(memory_space=pl.ANY)],
        out_specs=[_HBM] * (2 * m), input_output_aliases={j: j for j in range(2 * m)},
        compiler_params=pltpu.CompilerParams(has_side_effects=_EFFECT),
    )(*bufs, *sems, after)
    return outs[m:]


def _own_block(a, axis, me):
    if axis is None:
        return a[None]
    size = a.shape[axis] // N_DEV
    return lax.dynamic_slice_in_dim(a, me * size, size, axis)[None]


def _cast_bf16(a, *, name):
    r, c = a.shape
    tr = min(r, 512)

    def body(a_ref, o_ref):
        o_ref[...] = a_ref[...].astype(BF16)

    spec = pl.BlockSpec((tr, c), lambda i: (i, 0))
    return pl.pallas_call(body, name=name, grid=(r // tr,), in_specs=[spec], out_specs=spec,
                          out_shape=jax.ShapeDtypeStruct((r, c), BF16), compiler_params=_params(("parallel",)))(a)


def _adamw(slabs, w, m, v, *, name):
    layers, r, c = w.shape
    tr = min(r, 256)

    def body(*refs):
        s_refs = refs[:layers]
        w_ref, m_ref, v_ref, g_ref, d_ref, mo_ref, vo_ref = refs[layers:]
        for l in range(layers):
            g = s_refs[l][0].astype(F32)
            for i in range(1, N_DEV):
                g = g + s_refs[l][i].astype(F32)
            m2 = ADAM_B1 * m_ref[l] + (1.0 - ADAM_B1) * g
            v2 = ADAM_B2 * v_ref[l] + (1.0 - ADAM_B2) * (g * g)
            m_hat = m2 / (1.0 - ADAM_B1 ** ADAM_STEP)
            v_hat = v2 / (1.0 - ADAM_B2 ** ADAM_STEP)
            g_ref[l] = g
            d_ref[l] = -ADAM_LR * (m_hat / (jnp.sqrt(v_hat) + ADAM_EPS) + ADAM_WD * w_ref[l])
            mo_ref[l] = m2
            vo_ref[l] = v2

    spec = pl.BlockSpec((layers, tr, c), lambda i: (0, i, 0))
    return pl.pallas_call(
        body, name=name, grid=(r // tr,),
        in_specs=[pl.BlockSpec((N_DEV, tr, c), lambda i: (0, i, 0))] * layers + [spec, spec, spec],
        out_specs=[spec] * 4, out_shape=[jax.ShapeDtypeStruct((layers, r, c), F32)] * 4,
        compiler_params=_params(("parallel",)),
    )(*slabs, w, m, v)


GATHER_AXIS = {"attn_w_in": 1, "attn_w_out": 0, "ffn_w_up0": 1, "ffn_w_down0": 0, "hgrn_w_in": 1, "hgrn_w_out": 0,
               "hgrn_norm_g": 1, "ffn_w_up1": 1, "ffn_w_down1": 0}
GATHER_STAGES = (("attn_w_in",), ("attn_w_out", "ffn_w_up0", "ffn_w_down0", "hgrn_w_in", "hgrn_w_out", "hgrn_norm_g"),
                 ("ffn_w_up1", "ffn_w_down1"))
GATHER_WAITS = ((("attn_w_in",), 0, 1), (("attn_w_out", "ffn_w_up0", "ffn_w_down0"), 1, 2),
                (("hgrn_w_in", "hgrn_w_out", "hgrn_norm_g"), 1, None), (("ffn_w_up1", "ffn_w_down1"), 2, None))
SCATTER_AXIS = dict(GATHER_AXIS, small=None)
BIG = ("attn_w_in", "attn_w_out", "hgrn_w_in", "hgrn_w_out", "ffn_w_up", "ffn_w_down")
SMALL = ("lb_logits", "ln_mix_g", "ln_mix_b", "ln_ffn_g", "ln_ffn_b")
SMALL_ROW = {"ln_mix_g": 0, "ln_mix_b": 2, "ln_ffn_g": 4, "ln_ffn_b": 6, "lb_logits": 8}
NORM_G_ROW = 10


def kernel(x, attn_w_in, attn_w_out, hgrn_w_in, hgrn_w_out, hgrn_norm_g, lb_logits, ln_mix_g, ln_mix_b, ln_ffn_g, ln_ffn_b, ffn_w_up, ffn_w_down, loss_target, m_attn_w_in, m_attn_w_out, m_hgrn_w_in, m_hgrn_w_out, m_hgrn_norm_g, m_lb_logits, m_ln_mix_g, m_ln_mix_b, m_ln_ffn_g, m_ln_ffn_b, m_ffn_w_up, m_ffn_w_down, v_attn_w_in, v_attn_w_out, v_hgrn_w_in, v_hgrn_w_out, v_hgrn_norm_g, v_lb_logits, v_ln_mix_g, v_ln_mix_b, v_ln_ffn_g, v_ln_ffn_b, v_ffn_w_up, v_ffn_w_down):
    wts = dict(attn_w_in=attn_w_in, attn_w_out=attn_w_out, hgrn_w_in=hgrn_w_in, hgrn_w_out=hgrn_w_out,
               hgrn_norm_g=hgrn_norm_g, lb_logits=lb_logits, ln_mix_g=ln_mix_g, ln_mix_b=ln_mix_b, ln_ffn_g=ln_ffn_g,
               ln_ffn_b=ln_ffn_b, ffn_w_up=ffn_w_up, ffn_w_down=ffn_w_down)
    mom = dict(attn_w_in=m_attn_w_in, attn_w_out=m_attn_w_out, hgrn_w_in=m_hgrn_w_in, hgrn_w_out=m_hgrn_w_out,
               hgrn_norm_g=m_hgrn_norm_g, lb_logits=m_lb_logits, ln_mix_g=m_ln_mix_g, ln_mix_b=m_ln_mix_b,
               ln_ffn_g=m_ln_ffn_g, ln_ffn_b=m_ln_ffn_b, ffn_w_up=m_ffn_w_up, ffn_w_down=m_ffn_w_down)
    vel = dict(attn_w_in=v_attn_w_in, attn_w_out=v_attn_w_out, hgrn_w_in=v_hgrn_w_in, hgrn_w_out=v_hgrn_w_out,
               hgrn_norm_g=v_hgrn_norm_g, lb_logits=v_lb_logits, ln_mix_g=v_ln_mix_g, ln_mix_b=v_ln_mix_b,
               ln_ffn_g=v_ln_ffn_g, ln_ffn_b=v_ln_ffn_b, ffn_w_up=v_ffn_w_up, ffn_w_down=v_ffn_w_down)
    me = 4 * lax.axis_index("x") + 2 * lax.axis_index("y") + lax.axis_index("c")

    src = {"attn_w_in": attn_w_in[0], "attn_w_out": attn_w_out[0], "hgrn_w_in": hgrn_w_in[0], "hgrn_w_out": hgrn_w_out[0],
           "ffn_w_up0": ffn_w_up[0], "ffn_w_down0": ffn_w_down[0], "ffn_w_up1": ffn_w_up[1], "ffn_w_down1": ffn_w_down[1]}
    gathers, got = {}, {}

    def start_gather(stage, deps):
        shards, lands = [], []
        for nm in GATHER_STAGES[stage]:
            sh = hgrn_norm_g if nm == "hgrn_norm_g" else _cast_bf16(src[nm], name=f"cast_{nm}")
            ax = GATHER_AXIS[nm]
            shape = list(sh.shape)
            shape[ax] *= N_DEV
            shards.append(sh)
            lands.append(lax.dynamic_update_slice_in_dim(lax.empty(tuple(shape), sh.dtype), sh, me * sh.shape[ax], ax))
        gathers[stage] = _xchg_start(shards, lands, [GATHER_AXIS[nm] for nm in GATHER_STAGES[stage]], gather=True,
                                     name=f"gather_start_{stage}", deps=deps)
        return [gathers[stage]["token"]]

    def get_w(name, after):
        deps = []
        if name not in got:
            group, stage, then = [w for w in GATHER_WAITS if name in w[0]][0]
            xc = gathers[stage]
            res = _xchg_wait(xc, [GATHER_STAGES[stage].index(nm) for nm in group], xc["token"] if after is None else after,
                             name=f"gather_wait_{group[0]}")
            got.update(zip(group, res))
            if then is not None:
                deps = start_gather(then, [res[0]])
        return got[name], deps

    first = start_gather(0, [])

    scattered = {}

    def on_grads(tag, grads):
        gnames = list(grads)
        axes = [SCATTER_AXIS[nm] for nm in gnames]
        stacks = []
        for nm, ax in zip(gnames, axes):
            own = _own_block(grads[nm], ax, me)
            stack = lax.empty((N_DEV, *own.shape[1:]), own.dtype)
            stacks.append(lax.dynamic_update_slice_in_dim(stack, own, me, 0))
        scattered[tag] = (gnames, _xchg_start([grads[nm] for nm in gnames], stacks, axes, gather=False,
                                              name=f"scatter_start_{tag}"))
        return [scattered[tag][1]["token"]]

    sm = jax.nn.softmax(lb_logits, axis=0)
    csum = jnp.cumsum(sm, axis=0)
    small = dict(lb=(csum - csum[0:1])[1:2], lb_logits=lb_logits, ln_mix_g=ln_mix_g, ln_mix_b=ln_mix_b,
                 ln_ffn_g=ln_ffn_g, ln_ffn_b=ln_ffn_b)
    sq_sum, grad_x = _local_step(x[0], loss_target[0], get_w, small, on_grads, deps=first)
    loss = lax.psum(0.5 * sq_sum / x.shape[-1], ("x", "y", "c"))
    out = {}

    def stack_small(src_):
        rows = [None] * SMALL_ROWS
        for name in SMALL:
            rows[SMALL_ROW[name]], rows[SMALL_ROW[name] + 1] = src_[name][0:1], src_[name][1:2]
        zero = jnp.zeros((1, x.shape[-1]), F32)
        return jnp.concatenate([zero if r is None else r for r in rows], axis=0)[None]

    def update(name, slabs):
        shape = wts[name].shape
        out[name] = [r.reshape(shape) for r in _adamw(slabs, wts[name], mom[name], vel[name], name=f"adamw_{name}")]
        return out[name][0]

    slabs, after = {}, grad_x
    for tag, (gnames, xc) in scattered.items():
        slabs.update(zip(gnames, _xchg_wait(xc, list(range(len(gnames))), after, name=f"scatter_wait_{tag}")))
        if tag == "ffn1":
            continue
        if tag == "ffn0":
            update("ffn_w_down", [slabs["ffn_w_down0"], slabs["ffn_w_down1"]])
            after = update("ffn_w_up", [slabs["ffn_w_up0"], slabs["ffn_w_up1"]])
        elif tag == "hgrn":
            update("hgrn_w_out", [slabs["hgrn_w_out"]])
            after = update("hgrn_w_in", [slabs["hgrn_w_in"]])
        elif tag == "attn_out":
            after = update("attn_w_out", [slabs["attn_w_out"]])
        else:
            after = update("attn_w_in", [slabs["attn_w_in"]])
    res = _adamw([slabs["small"]], stack_small(wts), stack_small(mom), stack_small(vel), name="adamw_small")
    for name in SMALL:
        out[name] = [r[0, SMALL_ROW[name]:SMALL_ROW[name] + 2] for r in res]
    ng = hgrn_norm_g.shape[-1]
    ng_slabs = lax.dynamic_slice(slabs["small"], (0, NORM_G_ROW, me * ng), (N_DEV, 1, ng))
    out["hgrn_norm_g"] = [r[0] for r in _adamw([ng_slabs], hgrn_norm_g[None], m_hgrn_norm_g[None],
                                                v_hgrn_norm_g[None], name="adamw_norm_g")]
    order =("attn_w_in", "attn_w_out", "hgrn_w_in", "hgrn_w_out", "hgrn_norm_g", "lb_logits", "ln_mix_g", "ln_mix_b",
             "ln_ffn_g", "ln_ffn_b", "ffn_w_up", "ffn_w_down")
    return (loss, grad_x[None], *[out[nm][0] for nm in order], *[out[nm][1] for nm in order],
            *[out[nm][2] for nm in order], *[out[nm][3] for nm in order])
```

```python
import jax
import jax.numpy as jnp
from jax import lax
from jax.experimental import pallas as pl
from jax.experimental.pallas import tpu as pltpu

F32 = jnp.float32
BF16 = jnp.bfloat16

N_DEV = 8
LANES = 128
D_MODEL = 1024
ATTN_HEAD_DIM = 64
ATTN_HEADS = 16
ATTN_BLK = 128
DILATIONS = (1, 4, 16)
ROPE_THETA = 10000.0
HGRN_HEADS = 8
HGRN_CHUNK = 64
D_FF = 4096
LN_EPS = 1e-5
RMS_EPS = 1e-6
DEPTH = 2
ALPHA = (2 * DEPTH) ** 0.25
ADAM_LR, ADAM_B1, ADAM_B2, ADAM_EPS, ADAM_WD, ADAM_STEP = 0.001, 0.9, 0.999, 1e-08, 0.01, 10
VMEM_LIMIT = 48 * 1024 * 1024

_NT = (((1,), (1,)), ((), ()))
_TN = (((0,), (0,)), ((), ()))


def _dot(a, b):
    return jnp.dot(a, b, preferred_element_type=F32)


def _dot_nt(a, b):
    return lax.dot_general(a, b, _NT, preferred_element_type=F32)


def _dot_tn(a, b):
    return lax.dot_general(a, b, _TN, preferred_element_type=F32)


def _split3(x):
    p1 = x.astype(BF16)
    r1 = x - p1.astype(F32)
    p2 = r1.astype(BF16)
    p3 = (r1 - p2.astype(F32)).astype(BF16)
    return p1, p2, p3


def _exact_dot(sel, x):
    p1, p2, p3 = _split3(x)
    return _dot(sel, p1) + _dot(sel, p2) + _dot(sel, p3)


def _exact_dot_r(x, sel):
    p1, p2, p3 = _split3(x)
    return _dot(p1, sel) + _dot(p2, sel) + _dot(p3, sel)


def _params(sem=None):
    return pltpu.CompilerParams(dimension_semantics=sem, vmem_limit_bytes=VMEM_LIMIT)


def _mm(a, b, mode, *, name, m, n, k, tm=1024, tn=1024, tk=1024, out_dtypes=(F32,), epi=None, a_pre=None,
        tile_extras=(), row_extras=(), a_split=1, b_split=1, out_split=1,
        b_col_off=0, b_k_off=0, out_col_off=0, out_cols=None, alias=None, epi_wants_j=False, deps=(), t_out=False):
    tm, tn, tk = min(tm, m), min(tn, n), min(tk, k)
    assert m % tm == 0 and n % tn == 0 and k % tk == 0, (name, m, n, k, tm, tn, tk)
    gm, gn, gk = m // tm, n // tn, k // tk
    if mode in ("nn", "nt"):
        if a_split > 1:
            kc = (k // a_split) // tk
            a_spec = pl.BlockSpec((None, tm, tk), lambda i, j, kk: (kk // kc, i, kk % kc))
        else:
            a_spec = pl.BlockSpec((tm, tk), lambda i, j, kk: (i, kk))
    else:
        a_spec = pl.BlockSpec((tk, tm), lambda i, j, kk: (kk, i))
    if mode in ("nn", "tn"):
        if b_split > 1:
            nc = (n // b_split) // tn
            b_spec = pl.BlockSpec((None, tk, tn), lambda i, j, kk: (j // nc, kk, j % nc))
        else:
            b_spec = pl.BlockSpec((tk, tn), lambda i, j, kk: (kk + b_k_off, j + b_col_off))
    else:
        b_spec = pl.BlockSpec((tn, tk), lambda i, j, kk: (j + b_col_off, kk + b_k_off))
    if out_split > 1:
        nco = (n // out_split) // tn
        o_spec = pl.BlockSpec((None, tm, tn), lambda i, j, kk: (j // nco, i, j % nco))
        o_shape = (out_split, m, n // out_split)
    else:
        o_spec = pl.BlockSpec((tm, tn), lambda i, j, kk: (i, j + out_col_off))
        o_shape = (m, out_cols if out_cols is not None else n)
    n_ex = len(tile_extras) + len(row_extras)
    n_out = len(out_dtypes)
    if epi is None:
        def epi(acc):
            return (acc,)
    dot = {"nn": _dot, "nt": _dot_nt, "tn": _dot_tn}[mode]

    def body(*refs):
        a_ref, b_ref = refs[0], refs[1]
        ex = refs[2:2 + n_ex]
        outs = refs[2 + n_ex + (1 if alias is not None else 0) + len(deps):][:n_out + (1 if t_out else 0)]
        jj = pl.program_id(1)

        def product():
            av = a_ref[...]
            if a_pre is not None:
                av = a_pre(av)
            return dot(av.astype(BF16), b_ref[...].astype(BF16))

        def finish(total):
            lead = (jj,) if epi_wants_j else ()
            res = epi(*lead, total, *[e[...] for e in ex])
            for o, r in zip(outs, res):
                o[...] = r.astype(o.dtype)
            if t_out:
                outs[n_out][...] = res[0].astype(outs[n_out].dtype).T

        if gk == 1:
            finish(product())
        else:
            acc = refs[-1]
            kk = pl.program_id(2)

            @pl.when(kk == 0)
            def _():
                acc[...] = product()

            @pl.when(kk > 0)
            def _():
                acc[...] += product()

            @pl.when(kk == gk - 1)
            def _():
                finish(acc[...])

    in_specs = [a_spec, b_spec] + [o_spec] * len(tile_extras)
    in_specs += [pl.BlockSpec((tm, r.shape[1]), lambda i, j, kk: (i, 0)) for r in row_extras]
    args = [a, b] + list(tile_extras) + list(row_extras)
    io_alias = {}
    if alias is not None:
        in_specs.append(pl.BlockSpec(memory_space=pl.ANY))
        args.append(alias)
        io_alias = {len(args) - 1: 0}
    in_specs += [pl.BlockSpec(memory_space=pl.ANY)] * len(deps)
    args += list(deps)
    out_specs = [o_spec] * n_out
    out_shape = [jax.ShapeDtypeStruct(o_shape, dt) for dt in out_dtypes]
    if t_out:
        assert out_split > 1
        out_specs.append(pl.BlockSpec((None, tn, tm), lambda i, j, kk: (j // nco, j % nco, i)))
        out_shape.append(jax.ShapeDtypeStruct((out_split, n // out_split, m), out_dtypes[0]))
    out = pl.pallas_call(
        body, name=name, grid=(gm, gn, gk), in_specs=in_specs, out_specs=out_specs, out_shape=out_shape,
        scratch_shapes=[pltpu.VMEM((tm, tn), F32)] if gk > 1 else [],
        input_output_aliases=io_alias,
        compiler_params=_params(("parallel", "parallel", "arbitrary")),
    )(*args)
    return out[0] if len(out) == 1 else out


def _rope_tables(seq, dil):
    pos = jnp.arange(seq, dtype=jnp.int32).reshape(seq // dil, dil).T.reshape(seq)
    half = ATTN_HEAD_DIM // 2
    inv = ROPE_THETA ** (-jnp.arange(half, dtype=F32) * (2.0 / ATTN_HEAD_DIM))
    ang = pos.astype(F32)[:, None] * inv[None, :]
    cos, sin = jnp.cos(ang), jnp.sin(ang)
    reps = LANES // ATTN_HEAD_DIM
    return (jnp.tile(jnp.concatenate([cos, cos], axis=1), (1, reps)),
            jnp.tile(jnp.concatenate([-sin, sin], axis=1), (1, reps)))


def _rotate(x, c, ss, sign=1.0):
    w = x.shape[-1]
    half = ATTN_HEAD_DIM // 2
    lane = lax.broadcasted_iota(jnp.int32, x.shape, 1)
    first = (lane % ATTN_HEAD_DIM) < half
    partner = jnp.where(first, pltpu.roll(x, w - half, 1), pltpu.roll(x, half, 1))
    reps = w // LANES
    if reps > 1:
        c = jnp.concatenate([c] * reps, axis=1)
        ss = jnp.concatenate([ss] * reps, axis=1)
    return x * c + sign * (partner * ss)


ROW_TILE = 512


def _ln_fwd(x, y, g, b, *, name):
    s, d = x.shape

    def body(x_ref, y_ref, g_ref, b_ref, o_ref, ob_ref, xh_ref, r_ref):
        u = ALPHA * x_ref[...] + y_ref[...]
        mu = jnp.mean(u, axis=-1, keepdims=True)
        uc = u - mu
        var = jnp.mean(uc * uc, axis=-1, keepdims=True)
        rstd = lax.rsqrt(var + LN_EPS)
        xh = uc * rstd
        out = xh * g_ref[...] + b_ref[...]
        o_ref[...] = out
        ob_ref[...] = out.astype(BF16)
        xh_ref[...] = xh
        r_ref[...] = rstd

    row = pl.BlockSpec((ROW_TILE, d), lambda i: (i, 0))
    vec = pl.BlockSpec((1, d), lambda i: (0, 0))
    return pl.pallas_call(
        body, name=name, grid=(s // ROW_TILE,), in_specs=[row, row, vec, vec],
        out_specs=[row, row, row, pl.BlockSpec((ROW_TILE, 1), lambda i: (i, 0))],
        out_shape=[jax.ShapeDtypeStruct((s, d), F32), jax.ShapeDtypeStruct((s, d), BF16),
                   jax.ShapeDtypeStruct((s, d), F32), jax.ShapeDtypeStruct((s, 1), F32)],
        compiler_params=_params(("parallel",)),
    )(x, y, g, b)


def _ln_bwd(dy, xhat, rstd, g, *, name):
    s, d = dy.shape

    def body(dy_ref, xh_ref, r_ref, g_ref, du_ref, dub_ref, dg_ref, db_ref):
        @pl.when(pl.program_id(0) == 0)
        def _():
            dg_ref[...] = jnp.zeros_like(dg_ref)
            db_ref[...] = jnp.zeros_like(db_ref)

        dyv = dy_ref[...]
        xh = xh_ref[...]
        dxh = dyv * g_ref[...]
        m1 = jnp.mean(dxh, axis=-1, keepdims=True)
        m2 = jnp.mean(dxh * xh, axis=-1, keepdims=True)
        du = r_ref[...] * (dxh - m1 - xh * m2)
        du_ref[...] = du
        dub_ref[...] = du.astype(BF16)
        dg_ref[...] += jnp.sum(dyv * xh, axis=0, keepdims=True)
        db_ref[...] += jnp.sum(dyv, axis=0, keepdims=True)

    row = pl.BlockSpec((ROW_TILE, d), lambda i: (i, 0))
    vec = pl.BlockSpec((1, d), lambda i: (0, 0))
    return pl.pallas_call(
        body, name=name, grid=(s // ROW_TILE,),
        in_specs=[row, row, pl.BlockSpec((ROW_TILE, 1), lambda i: (i, 0)), vec],
        out_specs=[row, row, vec, vec],
        out_shape=[jax.ShapeDtypeStruct((s, d), F32), jax.ShapeDtypeStruct((s, d), BF16),
                   jax.ShapeDtypeStruct((1, d), F32), jax.ShapeDtypeStruct((1, d), F32)],
        compiler_params=_params(("arbitrary",)),
    )(dy, xhat, rstd, g)


def _loss_head(y, target):
    s, d = y.shape

    def body(y_ref, t_ref, dy_ref, sq_ref):
        @pl.when(pl.program_id(0) == 0)
        def _():
            sq_ref[...] = jnp.zeros_like(sq_ref)

        e = y_ref[...] - t_ref[...]
        dy_ref[...] = e * (1.0 / d)
        sq_ref[...] += jnp.sum(e * e, axis=0, keepdims=True)

    row = pl.BlockSpec((ROW_TILE, d), lambda i: (i, 0))
    vec = pl.BlockSpec((1, d), lambda i: (0, 0))
    return pl.pallas_call(
        body, name="loss_head", grid=(s // ROW_TILE,), in_specs=[row, row], out_specs=[row, vec],
        out_shape=[jax.ShapeDtypeStruct((s, d), F32), jax.ShapeDtypeStruct((1, d), F32)],
        compiler_params=_params(("arbitrary",)),
    )(y, target)


POS_BLK = 2048


def _class_rows(r, dil):
    return pl.ds(r, POS_BLK // dil, stride=dil) if dil > 1 else pl.ds(0, POS_BLK)


def _class_view(a, dil):
    s, w = a.shape
    return a.reshape(dil, s // dil, w)


def _class_spec(dil, all_tiles=True):
    if all_tiles:
        return pl.BlockSpec((dil, POS_BLK // dil, LANES), lambda i, t: (0, i, t))
    return pl.BlockSpec((dil, POS_BLK // dil, LANES), lambda i, t: (0, i, 0))


def _pos_spec(all_tiles=True):
    if all_tiles:
        return pl.BlockSpec((POS_BLK, LANES), lambda i, t: (i, t))
    return pl.BlockSpec((POS_BLK, LANES), lambda i, t: (i, 0))


def _prep_x(x, deps=()):
    s, d = x.shape

    def body(x_ref, *refs):
        outs = refs[len(deps):]
        for dil, o_ref in zip(DILATIONS, outs):
            for r in range(dil):
                o_ref[r] = x_ref[_class_rows(r, dil), :].astype(BF16)

    outs = pl.pallas_call(
        body, name="prep_x", grid=(s // POS_BLK, d // LANES),
        in_specs=[_pos_spec()] + [pl.BlockSpec(memory_space=pl.ANY)] * len(deps),
        out_specs=[_class_spec(dil) for dil in DILATIONS],
        out_shape=[jax.ShapeDtypeStruct((dil, s // dil, d), BF16) for dil in DILATIONS],
        compiler_params=_params(("parallel", "parallel")),
    )(x, *deps)
    return [o.reshape(s, d) for o in outs]


def _head_expand_matrix():
    h = lax.broadcasted_iota(jnp.int32, (LANES, D_MODEL), 0)
    l = lax.broadcasted_iota(jnp.int32, (LANES, D_MODEL), 1)
    return (l // ATTN_HEAD_DIM == h).astype(BF16)


def _attn_fwd(qkv, dil, *, name):
    _, s, d = qkv.shape
    nq = s // ATTN_BLK
    per = nq // dil
    scale = ATTN_HEAD_DIM ** -0.5
    tiles = d // LANES

    def body(q_ref, kc_ref, kp_ref, vc_ref, vp_ref, o_ref, lse_ref):
        qb = pl.program_id(0)
        first = (qb % per) == 0
        qi = lax.broadcasted_iota(jnp.int32, (ATTN_BLK, 2 * ATTN_BLK), 0)
        kj = lax.broadcasted_iota(jnp.int32, (ATTN_BLK, 2 * ATTN_BLK), 1)
        dist = qi + ATTN_BLK - kj
        valid = (dist >= 0) & (dist <= ATTN_BLK) & ((kj >= ATTN_BLK) | jnp.logical_not(first))
        lane = lax.broadcasted_iota(jnp.int32, (ATTN_BLK, LANES), 1)
        lse_tile = jnp.zeros((ATTN_BLK, LANES), F32)
        for t in range(tiles):
            cols = pl.ds(t * LANES, LANES)
            q = q_ref[:, cols]
            k2 = jnp.concatenate([kp_ref[:, cols], kc_ref[:, cols]], axis=0)
            v2 = jnp.concatenate([vp_ref[:, cols], vc_ref[:, cols]], axis=0)
            o_tile = jnp.zeros((ATTN_BLK, LANES), F32)
            for hh in range(2):
                in_head = (lane // ATTN_HEAD_DIM) == hh
                qh = jnp.where(in_head, q, jnp.zeros_like(q))
                sc = _dot_nt(qh, k2) * scale
                sc = jnp.where(valid, sc, -jnp.inf)
                mx = jnp.max(sc, axis=-1, keepdims=True)
                p = jnp.exp(sc - mx)
                l = jnp.sum(p, axis=-1, keepdims=True)
                oh = _dot(p.astype(BF16), v2) / l
                o_tile = jnp.where(in_head, oh, o_tile)
                lse_tile = jnp.where(lane == 2 * t + hh, mx + jnp.log(l), lse_tile)
            o_ref[:, cols] = o_tile
        lse_ref[...] = lse_tile

    def blk(piece, prev):
        if prev:
            return pl.BlockSpec((None, ATTN_BLK, d), lambda i: (piece, jnp.maximum(i - 1, 0), 0))
        return pl.BlockSpec((None, ATTN_BLK, d), lambda i: (piece, i, 0))

    return pl.pallas_call(
        body, name=name, grid=(nq,),
        in_specs=[blk(0, False), blk(1, False), blk(1, True), blk(2, False), blk(2, True)],
        out_specs=[pl.BlockSpec((ATTN_BLK, d), lambda i: (i, 0)), pl.BlockSpec((ATTN_BLK, LANES), lambda i: (i, 0))],
        out_shape=[jax.ShapeDtypeStruct((s, d), F32), jax.ShapeDtypeStruct((s, LANES), F32)],
        compiler_params=_params(("parallel",)),
    )(qkv, qkv, qkv, qkv, qkv)


def _attn_combine(os_, lses):
    s, d = os_[0].shape
    sel = _head_expand_matrix()

    def body(o0, o1, o2, l0, l1, l2, sel_ref, of_ref, ob_ref, lt_ref, o_pos, l_pos):
        for g, (dil, o_ref, l_ref) in enumerate(zip(DILATIONS, (o0, o1, o2), (l0, l1, l2))):
            for r in range(dil):
                o_pos[g, _class_rows(r, dil), :] = o_ref[r]
                l_pos[g, _class_rows(r, dil), :] = l_ref[r]
        la, lb_, lc = l_pos[0], l_pos[1], l_pos[2]
        mx = jnp.maximum(jnp.maximum(la, lb_), lc)
        es = (jnp.exp(la - mx), jnp.exp(lb_ - mx), jnp.exp(lc - mx))
        z = es[0] + es[1] + es[2]
        lt_ref[...] = mx + jnp.log(z)
        acc = jnp.zeros((POS_BLK, LANES), F32)
        for g in range(3):
            acc += _exact_dot_r(es[g] / z, sel_ref[...]) * o_pos[g]
        of_ref[...] = acc
        ob_ref[...] = acc.astype(BF16)

    return pl.pallas_call(
        body, name="attn_combine", grid=(s // POS_BLK, d // LANES),
        in_specs=[_class_spec(dil) for dil in DILATIONS] + [_class_spec(dil, False) for dil in DILATIONS]
        + [pl.BlockSpec((LANES, LANES), lambda i, t: (0, t))],
        out_specs=[_pos_spec(), _pos_spec(), _pos_spec(False)],
        out_shape=[jax.ShapeDtypeStruct((s, d), F32), jax.ShapeDtypeStruct((s, d), BF16),
                   jax.ShapeDtypeStruct((s, LANES), F32)],
        scratch_shapes=[pltpu.VMEM((3, POS_BLK, LANES), F32), pltpu.VMEM((3, POS_BLK, LANES), F32)],
        compiler_params=_params(("parallel", "arbitrary")),
    )(*[_class_view(o, dil) for o, dil in zip(os_, DILATIONS)],
      *[_class_view(l, dil) for l, dil in zip(lses, DILATIONS)], sel)


AUX_PER_TILE = 12


def _aux_placement():
    h = lax.broadcasted_iota(jnp.int32, (6, LANES, LANES), 1)
    l = lax.broadcasted_iota(jnp.int32, (6, LANES, LANES), 2)
    j = lax.broadcasted_iota(jnp.int32, (6, LANES, LANES), 0)
    target = AUX_PER_TILE * (h // 2) + 3 * (h % 2) + jnp.where(j < 3, j, 3 + j)
    return ((l == target) & (h < ATTN_HEADS)).astype(BF16)


def _attn_bwd_prep(do, o, lse):
    s, d = do.shape
    tiles = d // LANES
    sel_t = _head_expand_matrix().T

    def body(do_ref, o_ref, l_ref, sel_ref, place_ref, *refs):
        outs, delta, aux = refs[:9], refs[9], refs[10]
        t = pl.program_id(1)
        part = _exact_dot_r(do_ref[...] * o_ref[...], sel_ref[...])

        @pl.when(t == 0)
        def _():
            delta[...] = part

        @pl.when(t > 0)
        def _():
            delta[...] += part

        for g, dil in enumerate(DILATIONS):
            for r in range(dil):
                blk = do_ref[_class_rows(r, dil), :].astype(BF16)
                outs[g][r] = blk
                outs[3 + g][r] = blk.T

        @pl.when(t == tiles - 1)
        def _():
            pieces = _split3(l_ref[...]) + _split3(delta[...])
            acc = _dot(pieces[0], place_ref[0])
            for j in range(1, 6):
                acc += _dot(pieces[j], place_ref[j])
            aux[...] = acc
            for g, dil in enumerate(DILATIONS):
                for r in range(dil):
                    outs[6 + g][r] = aux[_class_rows(r, dil), :].astype(BF16)

    outs = pl.pallas_call(
        body, name="attn_bwd_prep", grid=(s // POS_BLK, tiles),
        in_specs=[_pos_spec(), _pos_spec(), _pos_spec(False), pl.BlockSpec((LANES, LANES), lambda i, t: (t, 0)),
                  pl.BlockSpec((6, LANES, LANES), lambda i, t: (0, 0, 0))],
        out_specs=[_class_spec(dil) for dil in DILATIONS]
        + [pl.BlockSpec((dil, LANES, POS_BLK // dil), lambda i, t: (0, t, i)) for dil in DILATIONS]
        + [_class_spec(dil, False) for dil in DILATIONS],
        out_shape=[jax.ShapeDtypeStruct((dil, s // dil, d), BF16) for dil in DILATIONS]
        + [jax.ShapeDtypeStruct((dil, d, s // dil), BF16) for dil in DILATIONS]
        + [jax.ShapeDtypeStruct((dil, s // dil, LANES), BF16) for dil in DILATIONS],
        scratch_shapes=[pltpu.VMEM((POS_BLK, LANES), F32), pltpu.VMEM((POS_BLK, LANES), F32)],
        compiler_params=_params(("parallel", "arbitrary")),
    )(do, o, lse, sel_t, _aux_placement())
    return ([a.reshape(s, d) for a in outs[0:3]], list(outs[3:6]), [a.reshape(s, LANES) for a in outs[6:9]])


def _attn_bwd(qkv, qkv_t, do, do_t, aux, tables, dil, *, name):
    _, s, d = qkv.shape
    nq = s // ATTN_BLK
    per = nq // dil
    scale = ATTN_HEAD_DIM ** -0.5
    assert scale == 0.125
    tiles = d // LANES
    half = ATTN_HEAD_DIM

    def body(qd_ref, qo_ref, k_ref, qtd_ref, qto_ref, kt_ref, vt_ref, dod_ref, doo_ref, dotd_ref, doto_ref,
             auxd_ref, auxo_ref, c_ref, ss_ref, out_ref, carry):
        kb = pl.program_id(0)

        @pl.when(kb == 0)
        def _():
            carry[...] = jnp.zeros_like(carry)

        has_next = (kb % per) != (per - 1)
        qi = lax.broadcasted_iota(jnp.int32, (ATTN_BLK, 2 * ATTN_BLK), 0)
        kj = lax.broadcasted_iota(jnp.int32, (ATTN_BLK, 2 * ATTN_BLK), 1) % ATTN_BLK
        valid_d = kj <= qi
        valid_o = (kj >= qi) & has_next
        lane = lax.broadcasted_iota(jnp.int32, (ATTN_BLK, LANES), 1)
        row = lax.broadcasted_iota(jnp.int32, (LANES, ATTN_BLK), 0)
        side = lax.broadcasted_iota(jnp.int32, (LANES, 2 * ATTN_BLK), 0)
        first = lax.broadcasted_iota(jnp.int32, (LANES, 2 * ATTN_BLK), 1) < ATTN_BLK
        c, ss = c_ref[...], ss_ref[...]
        zero = jnp.zeros((), BF16)
        for t in range(tiles):
            cols = pl.ds(t * LANES, LANES)
            base = AUX_PER_TILE * t
            hit = lambda lo: ((first & (side >= base + lo) & (side < base + lo + 3))
                              | (jnp.logical_not(first) & (side >= base + lo + 3) & (side < base + lo + 6)))
            k = k_ref[:, cols]
            kt, vt = kt_ref[cols, :], vt_ref[cols, :]
            kk = jnp.concatenate([jnp.where(lane < half, k, zero), jnp.where(lane >= half, k, zero)], axis=0)
            kk_t = jnp.concatenate([
                jnp.concatenate([jnp.where(row < half, kt, zero), jnp.where(row >= half, kt, zero)], axis=1),
                jnp.where(hit(0), -1.0 / scale, 0.0).astype(BF16)], axis=0)
            vv_t = jnp.concatenate([
                jnp.concatenate([jnp.where(row < half, vt, zero), jnp.where(row >= half, vt, zero)], axis=1),
                jnp.where(hit(6), -1.0, 0.0).astype(BF16)], axis=0)
            dk_t = jnp.zeros((LANES, 2 * ATTN_BLK), F32)
            dv_t = jnp.zeros((LANES, 2 * ATTN_BLK), F32)
            dq = [carry[:, cols], None]
            for which, (q_ref, qt_ref, do_ref, dot_ref, aux_ref, valid) in enumerate((
                    (qd_ref, qtd_ref, dod_ref, dotd_ref, auxd_ref, valid_d),
                    (qo_ref, qto_ref, doo_ref, doto_ref, auxo_ref, valid_o))):
                auxv = aux_ref[...]
                sc = _dot(jnp.concatenate([q_ref[:, cols], auxv], axis=1), kk_t)
                p = jnp.where(valid, jnp.exp(sc * scale), 0.0)
                dpd = _dot(jnp.concatenate([do_ref[:, cols], auxv], axis=1), vv_t)
                ds = (p * dpd * scale).astype(BF16)
                dv_t += _dot(dot_ref[cols, :], p.astype(BF16))
                dk_t += _dot(qt_ref[cols, :], ds)
                part = _dot(ds, kk)
                dq[which] = part if dq[which] is None else dq[which] + part
            carry[:, cols] = dq[1]
            dk = jnp.where(row < half, dk_t[:, :ATTN_BLK], dk_t[:, ATTN_BLK:]).T
            dv = jnp.where(row < half, dv_t[:, :ATTN_BLK], dv_t[:, ATTN_BLK:]).T
            out_ref[0, :, cols] = _rotate(dq[0], c, ss, -1.0).astype(BF16)
            out_ref[1, :, cols] = _rotate(dk, c, ss, -1.0).astype(BF16)
            out_ref[2, :, cols] = dv.astype(BF16)

    def nxt(i):
        return jnp.minimum(i + 1, nq - 1)

    def piece(p, shift):
        if shift:
            return pl.BlockSpec((None, ATTN_BLK, d), lambda i: (p, nxt(i), 0))
        return pl.BlockSpec((None, ATTN_BLK, d), lambda i: (p, i, 0))

    def piece_t(p, shift):
        if shift:
            return pl.BlockSpec((None, d, ATTN_BLK), lambda i: (p, 0, nxt(i)))
        return pl.BlockSpec((None, d, ATTN_BLK), lambda i: (p, 0, i))

    def rows(width, shift):
        if shift:
            return pl.BlockSpec((ATTN_BLK, width), lambda i: (nxt(i), 0))
        return pl.BlockSpec((ATTN_BLK, width), lambda i: (i, 0))

    def do_t_spec(shift):
        if shift:
            return pl.BlockSpec((None, d, ATTN_BLK), lambda i: (nxt(i) // per, 0, nxt(i) % per))
        return pl.BlockSpec((None, d, ATTN_BLK), lambda i: (i // per, 0, i % per))

    return pl.pallas_call(
        body, name=name, grid=(nq,),
        in_specs=[piece(0, False), piece(0, True), piece(1, False),
                  piece_t(0, False), piece_t(0, True), piece_t(1, False), piece_t(2, False),
                  rows(d, False), rows(d, True), do_t_spec(False), do_t_spec(True),
                  rows(LANES, False), rows(LANES, True), rows(LANES, False), rows(LANES, False)],
        out_specs=pl.BlockSpec((3, ATTN_BLK, d), lambda i: (0, i, 0)),
        out_shape=jax.ShapeDtypeStruct((3, s, d), BF16),
        scratch_shapes=[pltpu.VMEM((ATTN_BLK, d), F32)],
        compiler_params=_params(("arbitrary",)),
    )(qkv, qkv, qkv, qkv_t, qkv_t, qkv_t, qkv_t, do, do, do_t, do_t, aux, aux, *tables)


def _dx_combine(du, parts):
    s, d = du.shape

    def body(du_ref, p0, p1, p2, out_ref):
        out_ref[...] = ALPHA * du_ref[...] + p0[0]
        for dil, p_ref in zip(DILATIONS[1:], (p1, p2)):
            for r in range(dil):
                out_ref[_class_rows(r, dil), :] += p_ref[r]

    return pl.pallas_call(
        body, name="dx_combine", grid=(s // POS_BLK, d // LANES),
        in_specs=[_pos_spec()] + [_class_spec(dil) for dil in DILATIONS], out_specs=_pos_spec(),
        out_shape=jax.ShapeDtypeStruct((s, d), F32),
        compiler_params=_params(("parallel", "parallel")),
    )(du, *[_class_view(p, dil) for p, dil in zip(parts, DILATIONS)])


HGRN_ROWS = 512


def _tri(lower):
    i = lax.broadcasted_iota(jnp.int32, (HGRN_CHUNK, HGRN_CHUNK), 0)
    j = lax.broadcasted_iota(jnp.int32, (HGRN_CHUNK, HGRN_CHUNK), 1)
    return (j <= i) if lower else (j >= i)


def _hgrn_gates(qr, z, lb):
    sq = jax.nn.sigmoid(qr)
    q = qr * sq
    sg = jax.nn.sigmoid(z)
    f = lb + (1.0 - lb) * sg
    key = (1.0 - lb) * jax.nn.sigmoid(-z)
    return sq, q, sg, f, key


def _hgrn_fwd(p2, lb, norm_g):
    _, s, d = p2.shape
    nblk = s // HGRN_ROWS
    cps = HGRN_ROWS // HGRN_CHUNK

    def body(p_ref, lb_ref, g_ref, o_ref, on_ref, st_ref, state):
        @pl.when(pl.program_id(1) == 0)
        def _():
            state[...] = jnp.zeros_like(state)

        lbv = lb_ref[...]
        gv = g_ref[...]
        ltri = _tri(True)
        lsel = ltri.astype(BF16)
        for cidx in range(cps):
            rows = pl.ds(cidx * HGRN_CHUNK, HGRN_CHUNK)
            _, q, _, f, key = _hgrn_gates(p_ref[0, rows, :], p_ref[1, rows, :], lbv)
            v = p_ref[2, rows, :].astype(BF16)
            b = _exact_dot(lsel, jnp.log(f))
            b_last = b[HGRN_CHUNK - 1:HGRN_CHUNK, :]
            qd = (q * jnp.exp(b)).astype(BF16)
            kd = (key * jnp.exp(-b)).astype(BF16)
            k2 = (key * jnp.exp(b_last - b)).astype(BF16)
            st = state[...]
            st_ref[0, cidx] = st
            a = jnp.where(ltri, _dot_nt(qd, kd), 0.0)
            o = _dot(a.astype(BF16), v) + _dot_nt(qd, st.astype(BF16))
            state[...] = st * jnp.exp(b_last) + _dot_tn(v, k2)
            o_ref[rows, :] = o
            r = lax.rsqrt(jnp.mean(o * o, axis=-1, keepdims=True) + RMS_EPS)
            on_ref[rows, :] = (o * r * gv).astype(BF16)

    vec = pl.BlockSpec((1, LANES), lambda h, c: (0, h))
    col = pl.BlockSpec((HGRN_ROWS, LANES), lambda h, c: (c, h))
    return pl.pallas_call(
        body, name="hgrn_fwd", grid=(HGRN_HEADS, nblk),
        in_specs=[pl.BlockSpec((3, HGRN_ROWS, LANES), lambda h, c: (0, c, h)), vec, vec],
        out_specs=[col, col, pl.BlockSpec((1, cps, LANES, LANES), lambda h, c: (h, c, 0, 0))],
        out_shape=[jax.ShapeDtypeStruct((s, d), F32), jax.ShapeDtypeStruct((s, d), BF16),
                   jax.ShapeDtypeStruct((HGRN_HEADS, s // HGRN_CHUNK, LANES, LANES), F32)],
        scratch_shapes=[pltpu.VMEM((LANES, LANES), F32)],
        compiler_params=_params(("parallel", "arbitrary")),
    )(p2, lb, norm_g)


def _hgrn_bwd(p2, lb, norm_g, o_raw, states, dyn):
    _, s, d = p2.shape
    nblk = s // HGRN_ROWS
    cps = HGRN_ROWS // HGRN_CHUNK

    def body(p_ref, lb_ref, g_ref, o_ref, st_ref, dy_ref, dp_ref, dg_ref, dlb_ref, dstate):
        @pl.when(pl.program_id(1) == 0)
        def _():
            dstate[...] = jnp.zeros_like(dstate)
            dg_ref[...] = jnp.zeros_like(dg_ref)
            dlb_ref[...] = jnp.zeros_like(dlb_ref)

        lbv = lb_ref[...]
        gv = g_ref[...]
        ltri = _tri(True)
        lsel = ltri.astype(BF16)
        usel = _tri(False).astype(BF16)
        last_row = lax.broadcasted_iota(jnp.int32, (HGRN_CHUNK, LANES), 0) == HGRN_CHUNK - 1
        for cidx in reversed(range(cps)):
            rows = pl.ds(cidx * HGRN_CHUNK, HGRN_CHUNK)
            qr, z = p_ref[0, rows, :], p_ref[1, rows, :]
            sq, q, sg, f, key = _hgrn_gates(qr, z, lbv)
            v = p_ref[2, rows, :].astype(BF16)
            b = _exact_dot(lsel, jnp.log(f))
            b_last = b[HGRN_CHUNK - 1:HGRN_CHUNK, :]
            eb, enb, e2 = jnp.exp(b), jnp.exp(-b), jnp.exp(b_last - b)
            dec = jnp.exp(b_last)
            qd_f, kd_f, k2_f = q * eb, key * enb, key * e2
            qd, kd, k2 = qd_f.astype(BF16), kd_f.astype(BF16), k2_f.astype(BF16)
            a = jnp.where(ltri, _dot_nt(qd, kd), 0.0).astype(BF16)
            st = st_ref[0, cidx]
            dst = dstate[...]
            o = o_ref[rows, :]
            dyv = dy_ref[rows, :]
            r = lax.rsqrt(jnp.mean(o * o, axis=-1, keepdims=True) + RMS_EPS)
            oh = o * r
            doh = dyv * gv
            do_f = r * (doh - oh * jnp.mean(doh * oh, axis=-1, keepdims=True))
            dg_ref[...] += jnp.sum(dyv * oh, axis=0, keepdims=True)
            do = do_f.astype(BF16)
            dstb = dst.astype(BF16)
            da = jnp.where(ltri, _dot_nt(do, v), 0.0).astype(BF16)
            dv = _dot_tn(a, do) + _dot_nt(k2, dstb)
            dqd = _dot(da, kd) + _dot(do, st.astype(BF16))
            dkd = _dot_tn(da, qd)
            dk2 = _dot(v, dstb)
            ddec = jnp.sum(dst * st, axis=0, keepdims=True)
            dstate[...] = dst * dec + _dot_tn(do, qd)
            dq = dqd * eb
            dkey = dkd * enb + dk2 * e2
            db = dqd * qd_f - dkd * kd_f - dk2 * k2_f
            db_last = jnp.sum(dk2 * k2_f, axis=0, keepdims=True) + ddec * dec
            db = db + jnp.where(last_row, db_last, 0.0)
            dlf = _exact_dot(usel, db)
            one_m_s = 1.0 - sg
            dz = (dlf / f - dkey) * ((1.0 - lbv) * sg * one_m_s)
            dlb_ref[...] += jnp.sum(dlf * one_m_s / f - dkey * jax.nn.sigmoid(-z), axis=0, keepdims=True)
            dp_ref[0, rows, :] = (dq * (sq * (1.0 + qr * (1.0 - sq)))).astype(BF16)
            dp_ref[1, rows, :] = dz.astype(BF16)
            dp_ref[2, rows, :] = dv.astype(BF16)

    def rev(c):
        return nblk - 1 - c

    vec = pl.BlockSpec((1, LANES), lambda h, c: (0, h))
    col = pl.BlockSpec((HGRN_ROWS, LANES), lambda h, c: (rev(c), h))
    p3 = pl.BlockSpec((3, HGRN_ROWS, LANES), lambda h, c: (0, rev(c), h))
    return pl.pallas_call(
        body, name="hgrn_bwd", grid=(HGRN_HEADS, nblk),
        in_specs=[p3, vec, vec, col, pl.BlockSpec((1, cps, LANES, LANES), lambda h, c: (h, rev(c), 0, 0)), col],
        out_specs=[p3, vec, vec],
        out_shape=[jax.ShapeDtypeStruct((3, s, d), BF16), jax.ShapeDtypeStruct((1, d), F32),
                   jax.ShapeDtypeStruct((1, d), F32)],
        scratch_shapes=[pltpu.VMEM((LANES, LANES), F32)],
        compiler_params=_params(("parallel", "arbitrary")),
    )(p2, lb, norm_g, o_raw, states, dyn)


SMALL_ROWS = 16


def _relu2(h):
    r = jnp.maximum(h.astype(F32), 0.0)
    return r * r


def _dact_epi(acc, h):
    return (acc * (2.0 * jnp.maximum(h.astype(F32), 0.0)),)


def _res_epi(scale):
    def epi(acc, r):
        return (acc + scale * r,)
    return epi


def _rope_epi(j, acc, c, ss):
    return (jnp.where(j >= 2, acc, _rotate(acc, c, ss)),)


def _local_step(x, target, get_w, small, on_grads, deps=()):
    s, d = x.shape
    lb = small["lb"]
    gq = 3 * d
    gt = gq // 1024

    def ffn_fwd(xb, w_up, w_down, tag):
        h = _mm(xb, w_up, "nn", name=f"ffn_up_{tag}", m=s, n=D_FF, k=d, out_dtypes=(BF16,))
        y = _mm(h, w_down, "nn", name=f"ffn_down_{tag}", m=s, n=d, k=D_FF, a_pre=_relu2)
        return h, y

    def ffn_bwd(du, dub, xb, h, w_up, w_down, tag):
        dh = _mm(dub, w_down, "nt", name=f"ffn_dact_{tag}", m=s, n=D_FF, k=d, out_dtypes=(BF16,), epi=_dact_epi,
                 tile_extras=(h,))
        g_down = _mm(h, dub, "tn", name=f"ffn_gdown_{tag}", m=D_FF, n=d, k=s, a_pre=_relu2, out_dtypes=(BF16,))
        g_up = _mm(xb, dh, "tn", name=f"ffn_gup_{tag}", m=d, n=D_FF, k=s, out_dtypes=(BF16,))
        after = on_grads(f"ffn{tag}", {f"ffn_w_down{tag}": g_down, f"ffn_w_up{tag}": g_up})
        return _mm(dh, w_up, "nt", name=f"ffn_dx_{tag}", m=s, n=d, k=D_FF, epi=_res_epi(ALPHA), tile_extras=(du,),
                   deps=after)

    xs = _prep_x(x, deps)
    w_ain, after = get_w("attn_w_in", None)
    tabs, qkvs, qkv_ts, o_parts, lse_parts = [], [], [], [], []
    for g, dil in enumerate(DILATIONS):
        tabs.append(_rope_tables(s, dil))
        qkv, qkv_t = _mm(xs[g], w_ain, "nn", name=f"attn_in_{g}", m=s, n=gq, k=d, b_col_off=g * gt, out_split=3,
                         out_dtypes=(BF16,), epi=_rope_epi, epi_wants_j=True, row_extras=tabs[g], t_out=True,
                         deps=after if g == 0 else ())
        qkvs.append(qkv)
        qkv_ts.append(qkv_t)
        o_g, lse_g = _attn_fwd(qkv, dil, name=f"attn_fwd_{g}")
        o_parts.append(o_g)
        lse_parts.append(lse_g)
    o_f, o_b, lse_t = _attn_combine(o_parts, lse_parts)
    w_aout, after = get_w("attn_w_out", o_b)
    y = _mm(o_b, w_aout, "nn", name="attn_out", m=s, n=d, k=d, deps=after)
    x1, x1b, xh1, r1 = _ln_fwd(x, y, small["ln_mix_g"][0:1], small["ln_mix_b"][0:1], name="ln_mix_0")
    w_up0, w_down0 = get_w("ffn_w_up0", o_b)[0], get_w("ffn_w_down0", o_b)[0]
    h0, y = ffn_fwd(x1b, w_up0, w_down0, 0)
    x2, x2b, xh2, r2 = _ln_fwd(x1, y, small["ln_ffn_g"][0:1], small["ln_ffn_b"][0:1], name="ln_ffn_0")
    w_hin, w_hout, norm_g = get_w("hgrn_w_in", x2b)[0], get_w("hgrn_w_out", x2b)[0], get_w("hgrn_norm_g", x2b)[0]
    p2 = _mm(x2b, w_hin, "nn", name="hgrn_in", m=s, n=3 * d, k=d, out_split=3)
    o_raw, o_n, states = _hgrn_fwd(p2, lb, norm_g)
    w_up1, w_down1 = get_w("ffn_w_up1", o_n)[0], get_w("ffn_w_down1", o_n)[0]
    y = _mm(o_n, w_hout, "nn", name="hgrn_out", m=s, n=d, k=d)
    x3, x3b, xh3, r3 = _ln_fwd(x2, y, small["ln_mix_g"][1:2], small["ln_mix_b"][1:2], name="ln_mix_1")
    h1, y = ffn_fwd(x3b, w_up1, w_down1, 1)
    x4, _, xh4, r4 = _ln_fwd(x3, y, small["ln_ffn_g"][1:2], small["ln_ffn_b"][1:2], name="ln_ffn_1")
    dy, sq = _loss_head(x4, target)
    du, dub, dg_ffn1, db_ffn1 = _ln_bwd(dy, xh4, r4, small["ln_ffn_g"][1:2], name="ln_ffn_1_bwd")
    dx3 = ffn_bwd(du, dub, x3b, h1, w_up1, w_down1, 1)
    du, dub, dg_mix1, db_mix1 = _ln_bwd(dx3, xh3, r3, small["ln_mix_g"][1:2], name="ln_mix_1_bwd")
    dyn = _mm(dub, w_hout, "nt", name="hgrn_dout", m=s, n=d, k=d)
    g_hout = _mm(o_n, dub, "tn", name="hgrn_gout", m=d, n=d, k=s, out_dtypes=(BF16,))
    dp2, d_norm_g, d_lb = _hgrn_bwd(p2, lb, norm_g, o_raw, states, dyn)
    g_hin = _mm(x2b, dp2, "tn", name="hgrn_gin", m=d, n=3 * d, k=s, b_split=3, out_dtypes=(BF16,))
    after = on_grads("hgrn", {"hgrn_w_out": g_hout, "hgrn_w_in": g_hin})
    dx2 = _mm(dp2, w_hin, "nt", name="hgrn_dx", m=s, n=d, k=3 * d, a_split=3, epi=_res_epi(ALPHA),
              tile_extras=(du,), deps=after)
    du, dub, dg_ffn0, db_ffn0 = _ln_bwd(dx2, xh2, r2, small["ln_ffn_g"][0:1], name="ln_ffn_0_bwd")
    dx1 = ffn_bwd(du, dub, x1b, h0, w_up0, w_down0, 0)
    du, dub, dg_mix0, db_mix0 = _ln_bwd(dx1, xh1, r1, small["ln_mix_g"][0:1], name="ln_mix_0_bwd")
    g_aout = _mm(o_b, dub, "tn", name="attn_gout", m=d, n=d, k=s, out_dtypes=(BF16,))
    sm1 = jax.nn.softmax(small["lb_logits"], axis=0)
    d_l1 = d_lb * (sm1[0:1] * sm1[1:2])
    zeros = jnp.zeros((SMALL_ROWS - 12, d), F32)
    loss_row = jnp.broadcast_to(0.5 * jnp.sum(sq) / d, (1, d))
    small_grads = jnp.concatenate([dg_mix0, dg_mix1, db_mix0, db_mix1, dg_ffn0, dg_ffn1, db_ffn0, db_ffn1,
                                   -d_l1, d_l1, d_norm_g, loss_row, zeros], axis=0)
    after = on_grads("attn_out", {"attn_w_out": g_aout, "small": small_grads})
    do = _mm(dub, w_aout, "nt", name="attn_dout", m=s, n=d, k=d, deps=after)
    do_parts, do_ts, aux_parts = _attn_bwd_prep(do, o_f, lse_t)
    g_ain, dqkvs = None, []
    for g, dil in enumerate(DILATIONS):
        dqkvs.append(_attn_bwd(qkvs[g], qkv_ts[g], do_parts[g], do_ts[g], aux_parts[g], tabs[g], dil,
                               name=f"attn_bwd_{g}"))
        g_ain = _mm(xs[g], dqkvs[g], "tn", name=f"attn_gin_{g}", m=d, n=gq, k=s, b_split=3, out_dtypes=(BF16,),
                    out_col_off=g * gt, out_cols=3 * gq, alias=g_ain)
    after = on_grads("attn_in", {"attn_w_in": g_ain})
    dx_parts = [_mm(dqkvs[g], w_ain, "nt", name=f"attn_dx_{g}", m=s, n=d, k=gq, a_split=3, b_k_off=g * gt,
                    deps=after if g == 0 else ())
                for g in range(len(DILATIONS))]
    return _dx_combine(du, dx_parts)


def _mesh_place():
    x, y, c = lax.axis_index("x"), lax.axis_index("y"), lax.axis_index("c")
    return x, y, c, 4 * x + 2 * y + c


def _peer(x, y, c, k):
    px = 1 - x if (k >> 2) & 1 else x
    py = 1 - y if (k >> 1) & 1 else y
    pc = 1 - c if k & 1 else c
    return (px, py, pc), 4 * px + 2 * py + pc


def _window(ref, axis, size, idx):
    if axis is None:
        return ref
    sl = [slice(None)] * len(ref.shape)
    sl[axis] = pl.ds(idx * size, size)
    return ref.at[tuple(sl)]


_HBM = pl.BlockSpec(memory_space=pltpu.HBM)
_SEM = pl.BlockSpec(memory_space=pltpu.SEMAPHORE)
_EFFECT = pltpu.SideEffectType.DATAFLOW_SIDE_EFFECTING


def _xchg_ends(src_ref, land_ref, axis, gather, me, other):
    if gather:
        size = src_ref.shape[axis]
        return src_ref, _window(land_ref, axis, size, me), _window(land_ref, axis, size, other)
    size = None if axis is None else src_ref.shape[axis] // N_DEV
    return _window(src_ref, axis, size, other), land_ref.at[me], land_ref.at[other]


def _xchg_start(srcs, lands, axes, *, gather, name, deps=()):
    n = len(srcs)
    nd = len(deps)

    def body(*refs):
        src_refs, land_refs = refs[:n], refs[n:2 * n]
        send, recv = refs[2 * n + nd:3 * n + nd], refs[3 * n + nd:4 * n + nd]
        token = refs[-1]
        x, y, c, me = _mesh_place()
        for k in range(1, N_DEV):
            peer, pidx = _peer(x, y, c, k)
            for i in range(n):
                src, dst, _ = _xchg_ends(src_refs[i], land_refs[i], axes[i], gather, me, pidx)
                pltpu.make_async_remote_copy(
                    src_ref=src, dst_ref=dst, send_sem=send[i].at[k - 1], recv_sem=recv[i].at[k - 1],
                    device_id=peer, device_id_type=pl.DeviceIdType.MESH).start()
        for i in range(n):
            src, dst, _ = _xchg_ends(src_refs[i], land_refs[i], axes[i], gather, me, me)
            pltpu.make_async_copy(src, dst, send[i].at[N_DEV - 1]).start()
        token[...] = jnp.zeros_like(token)

    bufs = list(srcs) + list(lands)
    outs = pl.pallas_call(
        body, name=name,
        out_shape=[pltpu.SemaphoreType.DMA((N_DEV,))] * (2 * n) + [pltpu.HBM(b.shape, b.dtype) for b in bufs]
        + [jax.ShapeDtypeStruct((8, LANES), F32)],
        in_specs=[_HBM] * (2 * n) + [pl.BlockSpec(memory_space=pl.ANY)] * nd,
        out_specs=[_SEM] * (2 * n) + [_HBM] * (2 * n) + [pl.BlockSpec(memory_space=pltpu.VMEM)],
        input_output_aliases={i: 2 * n + i for i in range(2 * n)},
        compiler_params=pltpu.CompilerParams(has_side_effects=_EFFECT),
    )(*[pltpu.with_memory_space_constraint(b, pltpu.HBM) for b in bufs], *deps)
    return dict(send=outs[:n], recv=outs[n:2 * n], srcs=outs[2 * n:3 * n], lands=outs[3 * n:4 * n], token=outs[-1],
                axes=list(axes), gather=gather)


def _xchg_wait(xc, items, after, *, name):
    m = len(items)
    gather = xc["gather"]
    axes = [xc["axes"][i] for i in items]

    def body(*refs):
        src_refs, land_refs = refs[:m], refs[m:2 * m]
        send, recv = refs[2 * m:3 * m], refs[3 * m:4 * m]
        x, y, c, me = _mesh_place()
        for k in range(1, N_DEV):
            peer, pidx = _peer(x, y, c, k)
            for j in range(m):
                src, dst, got = _xchg_ends(src_refs[j], land_refs[j], axes[j], gather, me, pidx)
                pltpu.make_async_remote_copy(
                    src_ref=src, dst_ref=dst, send_sem=send[j].at[k - 1], recv_sem=recv[j].at[k - 1],
                    device_id=peer, device_id_type=pl.DeviceIdType.MESH).wait_send()
                pltpu.make_async_remote_copy(
                    src_ref=src, dst_ref=got, send_sem=send[j].at[k - 1], recv_sem=recv[j].at[k - 1],
                    device_id=peer, device_id_type=pl.DeviceIdType.MESH).wait_recv()
        for j in range(m):
            src, dst, _ = _xchg_ends(src_refs[j], land_refs[j], axes[j], gather, me, me)
            pltpu.make_async_copy(src, dst, send[j].at[N_DEV - 1]).wait()

    bufs = [xc["srcs"][i] for i in items] + [xc["lands"][i] for i in items]
    sems = [xc["send"][i] for i in items] + [xc["recv"][i] for i in items]
    outs = pl.pallas_call(
        body, name=name, out_shape=[pltpu.HBM(b.shape, b.dtype) for b in bufs],
        in_specs=[_HBM] * (2 * m) + [_SEM] * (2 * m) + [pl.BlockSpec(memory_space=pl.ANY)],
        out_specs=[_HBM] * (2 * m), input_output_aliases={j: j for j in range(2 * m)},
        compiler_params=pltpu.CompilerParams(has_side_effects=_EFFECT),
    )(*bufs, *sems, after)
    return outs[m:]


def _cast_bf16(a, *, name):
    r, c = a.shape
    tr = min(r, 512)

    def body(a_ref, o_ref):
        o_ref[...] = a_ref[...].astype(BF16)

    spec = pl.BlockSpec((tr, c), lambda i: (i, 0))
    return pl.pallas_call(body, name=name, grid=(r // tr,), in_specs=[spec], out_specs=spec,
                          out_shape=jax.ShapeDtypeStruct((r, c), BF16), compiler_params=_params(("parallel",)))(a)


def _adamw(slabs, w, m, v, *, name):
    layers, r, c = w.shape
    tr = min(r, 256)

    def body(*refs):
        s_refs = refs[:layers]
        w_ref, m_ref, v_ref, g_ref, d_ref, mo_ref, vo_ref = refs[layers:]
        for l in range(layers):
            g = s_refs[l][0].astype(F32)
            for i in range(1, N_DEV):
                g = g + s_refs[l][i].astype(F32)
            m2 = ADAM_B1 * m_ref[l] + (1.0 - ADAM_B1) * g
            v2 = ADAM_B2 * v_ref[l] + (1.0 - ADAM_B2) * (g * g)
            m_hat = m2 / (1.0 - ADAM_B1 ** ADAM_STEP)
            v_hat = v2 / (1.0 - ADAM_B2 ** ADAM_STEP)
            g_ref[l] = g
            d_ref[l] = -ADAM_LR * (m_hat / (jnp.sqrt(v_hat) + ADAM_EPS) + ADAM_WD * w_ref[l])
            mo_ref[l] = m2
            vo_ref[l] = v2

    spec = pl.BlockSpec((layers, tr, c), lambda i: (0, i, 0))
    return pl.pallas_call(
        body, name=name, grid=(r // tr,),
        in_specs=[pl.BlockSpec((N_DEV, tr, c), lambda i: (0, i, 0))] * layers + [spec, spec, spec],
        out_specs=[spec] * 4, out_shape=[jax.ShapeDtypeStruct((layers, r, c), F32)] * 4,
        compiler_params=_params(("parallel",)),
    )(*slabs, w, m, v)


GATHER_AXIS = {"attn_w_in": 1, "attn_w_out": 0, "ffn_w_up0": 1, "ffn_w_down0": 0, "hgrn_w_in": 1, "hgrn_w_out": 0,
               "hgrn_norm_g": 1, "ffn_w_up1": 1, "ffn_w_down1": 0}
GATHER_STAGES = (("attn_w_in",), ("attn_w_out", "ffn_w_up0", "ffn_w_down0", "hgrn_w_in", "hgrn_w_out", "hgrn_norm_g"),
                 ("ffn_w_up1", "ffn_w_down1"))
GATHER_WAITS = ((("attn_w_in",), 0, 1), (("attn_w_out", "ffn_w_up0", "ffn_w_down0"), 1, 2),
                (("hgrn_w_in", "hgrn_w_out", "hgrn_norm_g"), 1, None), (("ffn_w_up1", "ffn_w_down1"), 2, None))
SCATTER_AXIS = dict(GATHER_AXIS, small=None)
BIG = ("attn_w_in", "attn_w_out", "hgrn_w_in", "hgrn_w_out", "ffn_w_up", "ffn_w_down")
SMALL = ("lb_logits", "ln_mix_g", "ln_mix_b", "ln_ffn_g", "ln_ffn_b")
SMALL_ROW = {"ln_mix_g": 0, "ln_mix_b": 2, "ln_ffn_g": 4, "ln_ffn_b": 6, "lb_logits": 8}
NORM_G_ROW = 10
LOSS_ROW = 11


def kernel(x, attn_w_in, attn_w_out, hgrn_w_in, hgrn_w_out, hgrn_norm_g, lb_logits, ln_mix_g, ln_mix_b, ln_ffn_g, ln_ffn_b, ffn_w_up, ffn_w_down, loss_target, m_attn_w_in, m_attn_w_out, m_hgrn_w_in, m_hgrn_w_out, m_hgrn_norm_g, m_lb_logits, m_ln_mix_g, m_ln_mix_b, m_ln_ffn_g, m_ln_ffn_b, m_ffn_w_up, m_ffn_w_down, v_attn_w_in, v_attn_w_out, v_hgrn_w_in, v_hgrn_w_out, v_hgrn_norm_g, v_lb_logits, v_ln_mix_g, v_ln_mix_b, v_ln_ffn_g, v_ln_ffn_b, v_ffn_w_up, v_ffn_w_down):
    wts = dict(attn_w_in=attn_w_in, attn_w_out=attn_w_out, hgrn_w_in=hgrn_w_in, hgrn_w_out=hgrn_w_out,
               hgrn_norm_g=hgrn_norm_g, lb_logits=lb_logits, ln_mix_g=ln_mix_g, ln_mix_b=ln_mix_b, ln_ffn_g=ln_ffn_g,
               ln_ffn_b=ln_ffn_b, ffn_w_up=ffn_w_up, ffn_w_down=ffn_w_down)
    mom = dict(attn_w_in=m_attn_w_in, attn_w_out=m_attn_w_out, hgrn_w_in=m_hgrn_w_in, hgrn_w_out=m_hgrn_w_out,
               hgrn_norm_g=m_hgrn_norm_g, lb_logits=m_lb_logits, ln_mix_g=m_ln_mix_g, ln_mix_b=m_ln_mix_b,
               ln_ffn_g=m_ln_ffn_g, ln_ffn_b=m_ln_ffn_b, ffn_w_up=m_ffn_w_up, ffn_w_down=m_ffn_w_down)
    vel = dict(attn_w_in=v_attn_w_in, attn_w_out=v_attn_w_out, hgrn_w_in=v_hgrn_w_in, hgrn_w_out=v_hgrn_w_out,
               hgrn_norm_g=v_hgrn_norm_g, lb_logits=v_lb_logits, ln_mix_g=v_ln_mix_g, ln_mix_b=v_ln_mix_b,
               ln_ffn_g=v_ln_ffn_g, ln_ffn_b=v_ln_ffn_b, ffn_w_up=v_ffn_w_up, ffn_w_down=v_ffn_w_down)
    me = 4 * lax.axis_index("x") + 2 * lax.axis_index("y") + lax.axis_index("c")

    src = {"attn_w_in": attn_w_in[0], "attn_w_out": attn_w_out[0], "hgrn_w_in": hgrn_w_in[0], "hgrn_w_out": hgrn_w_out[0],
           "ffn_w_up0": ffn_w_up[0], "ffn_w_down0": ffn_w_down[0], "ffn_w_up1": ffn_w_up[1], "ffn_w_down1": ffn_w_down[1]}
    gathers, got = {}, {}

    def start_gather(stage, deps):
        shards, lands = [], []
        for nm in GATHER_STAGES[stage]:
            sh = hgrn_norm_g if nm == "hgrn_norm_g" else _cast_bf16(src[nm], name=f"cast_{nm}")
            ax = GATHER_AXIS[nm]
            shape = list(sh.shape)
            shape[ax] *= N_DEV
            shards.append(sh)
            lands.append(lax.empty(tuple(shape), sh.dtype))
        gathers[stage] = _xchg_start(shards, lands, [GATHER_AXIS[nm] for nm in GATHER_STAGES[stage]], gather=True,
                                     name=f"gather_start_{stage}", deps=deps)
        return [gathers[stage]["token"]]

    def get_w(name, after):
        deps = []
        if name not in got:
            group, stage, then = [w for w in GATHER_WAITS if name in w[0]][0]
            xc = gathers[stage]
            res = _xchg_wait(xc, [GATHER_STAGES[stage].index(nm) for nm in group], xc["token"] if after is None else after,
                             name=f"gather_wait_{group[0]}")
            got.update(zip(group, res))
            if then is not None:
                deps = start_gather(then, [res[0]])
        return got[name], deps

    first = start_gather(0, [])

    scattered = {}

    def on_grads(tag, grads):
        gnames = list(grads)
        axes = [SCATTER_AXIS[nm] for nm in gnames]
        stacks = []
        for nm, ax in zip(gnames, axes):
            shape = list(grads[nm].shape)
            if ax is not None:
                shape[ax] //= N_DEV
            stacks.append(lax.empty((N_DEV, *shape), grads[nm].dtype))
        scattered[tag] = (gnames, _xchg_start([grads[nm] for nm in gnames], stacks, axes, gather=False,
                                              name=f"scatter_start_{tag}"))
        return [scattered[tag][1]["token"]]

    sm = jax.nn.softmax(lb_logits, axis=0)
    csum = jnp.cumsum(sm, axis=0)
    small = dict(lb=(csum - csum[0:1])[1:2], lb_logits=lb_logits, ln_mix_g=ln_mix_g, ln_mix_b=ln_mix_b,
                 ln_ffn_g=ln_ffn_g, ln_ffn_b=ln_ffn_b)
    grad_x = _local_step(x[0], loss_target[0], get_w, small, on_grads, deps=first)
    out = {}

    def stack_small(src_):
        rows = [None] * SMALL_ROWS
        for name in SMALL:
            rows[SMALL_ROW[name]], rows[SMALL_ROW[name] + 1] = src_[name][0:1], src_[name][1:2]
        zero = jnp.zeros((1, x.shape[-1]), F32)
        return jnp.concatenate([zero if r is None else r for r in rows], axis=0)[None]

    def update(name, slabs):
        shape = wts[name].shape
        out[name] = [r.reshape(shape) for r in _adamw(slabs, wts[name], mom[name], vel[name], name=f"adamw_{name}")]
        return out[name][0]

    slabs, after = {}, grad_x
    for tag, (gnames, xc) in scattered.items():
        slabs.update(zip(gnames, _xchg_wait(xc, list(range(len(gnames))), after, name=f"scatter_wait_{tag}")))
        if tag == "ffn1":
            continue
        if tag == "ffn0":
            update("ffn_w_down", [slabs["ffn_w_down0"], slabs["ffn_w_down1"]])
            after = update("ffn_w_up", [slabs["ffn_w_up0"], slabs["ffn_w_up1"]])
        elif tag == "hgrn":
            update("hgrn_w_out", [slabs["hgrn_w_out"]])
            after = update("hgrn_w_in", [slabs["hgrn_w_in"]])
        elif tag == "attn_out":
            after = update("attn_w_out", [slabs["attn_w_out"]])
        else:
            after = update("attn_w_in", [slabs["attn_w_in"]])
    res = _adamw([slabs["small"]], stack_small(wts), stack_small(mom), stack_small(vel), name="adamw_small")
    for name in SMALL:
        out[name] = [r[0, SMALL_ROW[name]:SMALL_ROW[name] + 2] for r in res]
    loss = res[0][0, LOSS_ROW, 0]
    ng = hgrn_norm_g.shape[-1]
    ng_slabs = lax.dynamic_slice(slabs["small"], (0, NORM_G_ROW, me * ng), (N_DEV, 1, ng))
    out["hgrn_norm_g"] = [r[0] for r in _adamw([ng_slabs], hgrn_norm_g[None], m_hgrn_norm_g[None],
                                                v_hgrn_norm_g[None], name="adamw_norm_g")]
    order =("attn_w_in", "attn_w_out", "hgrn_w_in", "hgrn_w_out", "hgrn_norm_g", "lb_logits", "ln_mix_g", "ln_mix_b",
             "ln_ffn_g", "ln_ffn_b", "ffn_w_up", "ffn_w_down")
    return (loss, grad_x[None], *[out[nm][0] for nm in order], *[out[nm][1] for nm in order],
            *[out[nm][2] for nm in order], *[out[nm][3] for nm in order])
```

```python
import jax
import jax.numpy as jnp
from jax import lax
from jax.experimental import pallas as pl
from jax.experimental.pallas import tpu as pltpu

F32 = jnp.float32
BF16 = jnp.bfloat16

N_DEV = 8
LANES = 128
D_MODEL = 1024
ATTN_HEAD_DIM = 64
ATTN_HEADS = 16
ATTN_BLK = 128
DILATIONS = (1, 4, 16)
ROPE_THETA = 10000.0
HGRN_HEADS = 8
HGRN_CHUNK = 64
D_FF = 4096
LN_EPS = 1e-5
RMS_EPS = 1e-6
DEPTH = 2
ALPHA = (2 * DEPTH) ** 0.25
ADAM_LR, ADAM_B1, ADAM_B2, ADAM_EPS, ADAM_WD, ADAM_STEP = 0.001, 0.9, 0.999, 1e-08, 0.01, 10
VMEM_LIMIT = 48 * 1024 * 1024

_NT = (((1,), (1,)), ((), ()))
_TN = (((0,), (0,)), ((), ()))


def _dot(a, b):
    return jnp.dot(a, b, preferred_element_type=F32)


def _dot_nt(a, b):
    return lax.dot_general(a, b, _NT, preferred_element_type=F32)


def _dot_tn(a, b):
    return lax.dot_general(a, b, _TN, preferred_element_type=F32)


def _split3(x):
    p1 = x.astype(BF16)
    r1 = x - p1.astype(F32)
    p2 = r1.astype(BF16)
    p3 = (r1 - p2.astype(F32)).astype(BF16)
    return p1, p2, p3


def _exact_dot(sel, x):
    p1, p2, p3 = _split3(x)
    return _dot(sel, p1) + _dot(sel, p2) + _dot(sel, p3)


def _exact_dot_r(x, sel):
    p1, p2, p3 = _split3(x)
    return _dot(p1, sel) + _dot(p2, sel) + _dot(p3, sel)


def _params(sem=None):
    return pltpu.CompilerParams(dimension_semantics=sem, vmem_limit_bytes=VMEM_LIMIT)


def _mm(a, b, mode, *, name, m, n, k, tm=1024, tn=1024, tk=1024, out_dtypes=(F32,), epi=None, a_pre=None,
        tile_extras=(), row_extras=(), a_split=1, b_split=1, out_split=1,
        b_col_off=0, b_k_off=0, out_col_off=0, out_cols=None, alias=None, epi_wants_j=False, deps=(), t_out=False):
    tm, tn, tk = min(tm, m), min(tn, n), min(tk, k)
    assert m % tm == 0 and n % tn == 0 and k % tk == 0, (name, m, n, k, tm, tn, tk)
    gm, gn, gk = m // tm, n // tn, k // tk
    if mode in ("nn", "nt"):
        if a_split > 1:
            kc = (k // a_split) // tk
            a_spec = pl.BlockSpec((None, tm, tk), lambda i, j, kk: (kk // kc, i, kk % kc))
        else:
            a_spec = pl.BlockSpec((tm, tk), lambda i, j, kk: (i, kk))
    else:
        a_spec = pl.BlockSpec((tk, tm), lambda i, j, kk: (kk, i))
    if mode in ("nn", "tn"):
        if b_split > 1:
            nc = (n // b_split) // tn
            b_spec = pl.BlockSpec((None, tk, tn), lambda i, j, kk: (j // nc, kk, j % nc))
        else:
            b_spec = pl.BlockSpec((tk, tn), lambda i, j, kk: (kk + b_k_off, j + b_col_off))
    else:
        b_spec = pl.BlockSpec((tn, tk), lambda i, j, kk: (j + b_col_off, kk + b_k_off))
    if out_split > 1:
        nco = (n // out_split) // tn
        o_spec = pl.BlockSpec((None, tm, tn), lambda i, j, kk: (j // nco, i, j % nco))
        o_shape = (out_split, m, n // out_split)
    else:
        o_spec = pl.BlockSpec((tm, tn), lambda i, j, kk: (i, j + out_col_off))
        o_shape = (m, out_cols if out_cols is not None else n)
    n_ex = len(tile_extras) + len(row_extras)
    n_out = len(out_dtypes)
    if epi is None:
        def epi(acc):
            return (acc,)
    dot = {"nn": _dot, "nt": _dot_nt, "tn": _dot_tn}[mode]

    def body(*refs):
        a_ref, b_ref = refs[0], refs[1]
        ex = refs[2:2 + n_ex]
        outs = refs[2 + n_ex + (1 if alias is not None else 0) + len(deps):][:n_out + (1 if t_out else 0)]
        jj = pl.program_id(1)

        def product():
            av = a_ref[...]
            if a_pre is not None:
                av = a_pre(av)
            return dot(av.astype(BF16), b_ref[...].astype(BF16))

        def finish(total):
            lead = (jj,) if epi_wants_j else ()
            res = epi(*lead, total, *[e[...] for e in ex])
            for o, r in zip(outs, res):
                o[...] = r.astype(o.dtype)
            if t_out:
                outs[n_out][...] = res[0].astype(outs[n_out].dtype).T

        if gk == 1:
            finish(product())
        else:
            acc = refs[-1]
            kk = pl.program_id(2)

            @pl.when(kk == 0)
            def _():
                acc[...] = product()

            @pl.when(kk > 0)
            def _():
                acc[...] += product()

            @pl.when(kk == gk - 1)
            def _():
                finish(acc[...])

    in_specs = [a_spec, b_spec] + [o_spec] * len(tile_extras)
    in_specs += [pl.BlockSpec((tm, r.shape[1]), lambda i, j, kk: (i, 0)) for r in row_extras]
    args = [a, b] + list(tile_extras) + list(row_extras)
    io_alias = {}
    if alias is not None:
        in_specs.append(pl.BlockSpec(memory_space=pl.ANY))
        args.append(alias)
        io_alias = {len(args) - 1: 0}
    in_specs += [pl.BlockSpec(memory_space=pl.ANY)] * len(deps)
    args += list(deps)
    out_specs = [o_spec] * n_out
    out_shape = [jax.ShapeDtypeStruct(o_shape, dt) for dt in out_dtypes]
    if t_out:
        assert out_split > 1
        out_specs.append(pl.BlockSpec((None, tn, tm), lambda i, j, kk: (j // nco, j % nco, i)))
        out_shape.append(jax.ShapeDtypeStruct((out_split, n // out_split, m), out_dtypes[0]))
    out = pl.pallas_call(
        body, name=name, grid=(gm, gn, gk), in_specs=in_specs, out_specs=out_specs, out_shape=out_shape,
        scratch_shapes=[pltpu.VMEM((tm, tn), F32)] if gk > 1 else [],
        input_output_aliases=io_alias,
        compiler_params=_params(("parallel", "parallel", "arbitrary")),
    )(*args)
    return out[0] if len(out) == 1 else out


def _rope_tables(seq, dil):
    pos = jnp.arange(seq, dtype=jnp.int32).reshape(seq // dil, dil).T.reshape(seq)
    half = ATTN_HEAD_DIM // 2
    inv = ROPE_THETA ** (-jnp.arange(half, dtype=F32) * (2.0 / ATTN_HEAD_DIM))
    ang = pos.astype(F32)[:, None] * inv[None, :]
    cos, sin = jnp.cos(ang), jnp.sin(ang)
    reps = LANES // ATTN_HEAD_DIM
    return (jnp.tile(jnp.concatenate([cos, cos], axis=1), (1, reps)),
            jnp.tile(jnp.concatenate([-sin, sin], axis=1), (1, reps)))


def _rotate(x, c, ss, sign=1.0):
    w = x.shape[-1]
    half = ATTN_HEAD_DIM // 2
    lane = lax.broadcasted_iota(jnp.int32, x.shape, 1)
    first = (lane % ATTN_HEAD_DIM) < half
    partner = jnp.where(first, pltpu.roll(x, w - half, 1), pltpu.roll(x, half, 1))
    reps = w // LANES
    if reps > 1:
        c = jnp.concatenate([c] * reps, axis=1)
        ss = jnp.concatenate([ss] * reps, axis=1)
    return x * c + sign * (partner * ss)


ROW_TILE = 512


def _ln_fwd(x, y, g, b, *, name):
    s, d = x.shape

    def body(x_ref, y_ref, g_ref, b_ref, o_ref, ob_ref, xh_ref, r_ref):
        u = ALPHA * x_ref[...] + y_ref[...]
        mu = jnp.mean(u, axis=-1, keepdims=True)
        uc = u - mu
        var = jnp.mean(uc * uc, axis=-1, keepdims=True)
        rstd = lax.rsqrt(var + LN_EPS)
        xh = uc * rstd
        out = xh * g_ref[...] + b_ref[...]
        o_ref[...] = out
        ob_ref[...] = out.astype(BF16)
        xh_ref[...] = xh
        r_ref[...] = rstd

    row = pl.BlockSpec((ROW_TILE, d), lambda i: (i, 0))
    vec = pl.BlockSpec((1, d), lambda i: (0, 0))
    return pl.pallas_call(
        body, name=name, grid=(s // ROW_TILE,), in_specs=[row, row, vec, vec],
        out_specs=[row, row, row, pl.BlockSpec((ROW_TILE, 1), lambda i: (i, 0))],
        out_shape=[jax.ShapeDtypeStruct((s, d), F32), jax.ShapeDtypeStruct((s, d), BF16),
                   jax.ShapeDtypeStruct((s, d), F32), jax.ShapeDtypeStruct((s, 1), F32)],
        compiler_params=_params(("parallel",)),
    )(x, y, g, b)


def _ln_bwd(dy, xhat, rstd, g, *, name):
    s, d = dy.shape

    def body(dy_ref, xh_ref, r_ref, g_ref, du_ref, dub_ref, dg_ref, db_ref):
        @pl.when(pl.program_id(0) == 0)
        def _():
            dg_ref[...] = jnp.zeros_like(dg_ref)
            db_ref[...] = jnp.zeros_like(db_ref)

        dyv = dy_ref[...]
        xh = xh_ref[...]
        dxh = dyv * g_ref[...]
        m1 = jnp.mean(dxh, axis=-1, keepdims=True)
        m2 = jnp.mean(dxh * xh, axis=-1, keepdims=True)
        du = r_ref[...] * (dxh - m1 - xh * m2)
        du_ref[...] = du
        dub_ref[...] = du.astype(BF16)
        dg_ref[...] += jnp.sum(dyv * xh, axis=0, keepdims=True)
        db_ref[...] += jnp.sum(dyv, axis=0, keepdims=True)

    row = pl.BlockSpec((ROW_TILE, d), lambda i: (i, 0))
    vec = pl.BlockSpec((1, d), lambda i: (0, 0))
    return pl.pallas_call(
        body, name=name, grid=(s // ROW_TILE,),
        in_specs=[row, row, pl.BlockSpec((ROW_TILE, 1), lambda i: (i, 0)), vec],
        out_specs=[row, row, vec, vec],
        out_shape=[jax.ShapeDtypeStruct((s, d), F32), jax.ShapeDtypeStruct((s, d), BF16),
                   jax.ShapeDtypeStruct((1, d), F32), jax.ShapeDtypeStruct((1, d), F32)],
        compiler_params=_params(("arbitrary",)),
    )(dy, xhat, rstd, g)


def _loss_head(y, target):
    s, d = y.shape

    def body(y_ref, t_ref, dy_ref, sq_ref):
        @pl.when(pl.program_id(0) == 0)
        def _():
            sq_ref[...] = jnp.zeros_like(sq_ref)

        e = y_ref[...] - t_ref[...]
        dy_ref[...] = e * (1.0 / d)
        sq_ref[...] += jnp.sum(e * e, axis=0, keepdims=True)

    row = pl.BlockSpec((ROW_TILE, d), lambda i: (i, 0))
    vec = pl.BlockSpec((1, d), lambda i: (0, 0))
    return pl.pallas_call(
        body, name="loss_head", grid=(s // ROW_TILE,), in_specs=[row, row], out_specs=[row, vec],
        out_shape=[jax.ShapeDtypeStruct((s, d), F32), jax.ShapeDtypeStruct((1, d), F32)],
        compiler_params=_params(("arbitrary",)),
    )(y, target)


POS_BLK = 2048


def _class_rows(r, dil):
    return pl.ds(r, POS_BLK // dil, stride=dil) if dil > 1 else pl.ds(0, POS_BLK)


def _class_view(a, dil):
    s, w = a.shape
    return a.reshape(dil, s // dil, w)


def _class_spec(dil, all_tiles=True):
    if all_tiles:
        return pl.BlockSpec((dil, POS_BLK // dil, LANES), lambda i, t: (0, i, t))
    return pl.BlockSpec((dil, POS_BLK // dil, LANES), lambda i, t: (0, i, 0))


def _pos_spec(all_tiles=True):
    if all_tiles:
        return pl.BlockSpec((POS_BLK, LANES), lambda i, t: (i, t))
    return pl.BlockSpec((POS_BLK, LANES), lambda i, t: (i, 0))


def _prep_x(x, deps=()):
    s, d = x.shape

    def body(x_ref, *refs):
        outs = refs[len(deps):]
        for dil, o_ref in zip(DILATIONS, outs):
            for r in range(dil):
                o_ref[r] = x_ref[_class_rows(r, dil), :].astype(BF16)

    outs = pl.pallas_call(
        body, name="prep_x", grid=(s // POS_BLK, d // LANES),
        in_specs=[_pos_spec()] + [pl.BlockSpec(memory_space=pl.ANY)] * len(deps),
        out_specs=[_class_spec(dil) for dil in DILATIONS],
        out_shape=[jax.ShapeDtypeStruct((dil, s // dil, d), BF16) for dil in DILATIONS],
        compiler_params=_params(("parallel", "parallel")),
    )(x, *deps)
    return [o.reshape(s, d) for o in outs]


def _head_expand_matrix():
    h = lax.broadcasted_iota(jnp.int32, (LANES, D_MODEL), 0)
    l = lax.broadcasted_iota(jnp.int32, (LANES, D_MODEL), 1)
    return (l // ATTN_HEAD_DIM == h).astype(BF16)


def _attn_fwd(qkv, dil, *, name):
    _, s, d = qkv.shape
    nq = s // ATTN_BLK
    per = nq // dil
    scale = ATTN_HEAD_DIM ** -0.5
    tiles = d // LANES

    def body(q_ref, kc_ref, kp_ref, vc_ref, vp_ref, o_ref, lse_ref):
        qb = pl.program_id(0)
        first = (qb % per) == 0
        qi = lax.broadcasted_iota(jnp.int32, (ATTN_BLK, 2 * ATTN_BLK), 0)
        kj = lax.broadcasted_iota(jnp.int32, (ATTN_BLK, 2 * ATTN_BLK), 1)
        dist = qi + ATTN_BLK - kj
        valid = (dist >= 0) & (dist <= ATTN_BLK) & ((kj >= ATTN_BLK) | jnp.logical_not(first))
        lane = lax.broadcasted_iota(jnp.int32, (ATTN_BLK, LANES), 1)
        lse_tile = jnp.zeros((ATTN_BLK, LANES), F32)
        zero = jnp.zeros((), BF16)
        in_head = [(lane // ATTN_HEAD_DIM) == hh for hh in range(2)]
        for t0 in range(0, tiles, ATTN_TILE_GROUP):
            group = range(t0, t0 + ATTN_TILE_GROUP)
            heads = [(t, hh) for t in group for hh in range(2)]
            cols = {t: pl.ds(t * LANES, LANES) for t in group}
            k2 = {t: jnp.concatenate([kp_ref[:, cols[t]], kc_ref[:, cols[t]]], axis=0) for t in group}
            v2 = {t: jnp.concatenate([vp_ref[:, cols[t]], vc_ref[:, cols[t]]], axis=0) for t in group}
            sc = {(t, hh): jnp.where(valid, _dot_nt(jnp.where(in_head[hh], q_ref[:, cols[t]], zero), k2[t]) * scale,
                                     -jnp.inf) for t, hh in heads}
            mx = {i: jnp.max(sc[i], axis=-1, keepdims=True) for i in heads}
            p = {i: jnp.exp(sc[i] - mx[i]) for i in heads}
            l = {i: jnp.sum(p[i], axis=-1, keepdims=True) for i in heads}
            oh = {i: _dot(p[i].astype(BF16), v2[i[0]]) / l[i] for i in heads}
            for t in group:
                o_ref[:, cols[t]] = jnp.where(in_head[0], oh[t, 0], oh[t, 1])
                for hh in range(2):
                    lse_tile = jnp.where(lane == 2 * t + hh, mx[t, hh] + jnp.log(l[t, hh]), lse_tile)
        lse_ref[...] = lse_tile

    def blk(piece, prev):
        if prev:
            return pl.BlockSpec((None, ATTN_BLK, d), lambda i: (piece, jnp.maximum(i - 1, 0), 0))
        return pl.BlockSpec((None, ATTN_BLK, d), lambda i: (piece, i, 0))

    return pl.pallas_call(
        body, name=name, grid=(nq,),
        in_specs=[blk(0, False), blk(1, False), blk(1, True), blk(2, False), blk(2, True)],
        out_specs=[pl.BlockSpec((ATTN_BLK, d), lambda i: (i, 0)), pl.BlockSpec((ATTN_BLK, LANES), lambda i: (i, 0))],
        out_shape=[jax.ShapeDtypeStruct((s, d), F32), jax.ShapeDtypeStruct((s, LANES), F32)],
        compiler_params=_params(("parallel",)),
    )(qkv, qkv, qkv, qkv, qkv)


def _attn_combine(os_, lses):
    s, d = os_[0].shape
    sel = _head_expand_matrix()

    def body(o0, o1, o2, l0, l1, l2, sel_ref, of_ref, ob_ref, lt_ref, o_pos, l_pos):
        for g, (dil, o_ref, l_ref) in enumerate(zip(DILATIONS, (o0, o1, o2), (l0, l1, l2))):
            for r in range(dil):
                o_pos[g, _class_rows(r, dil), :] = o_ref[r]
                l_pos[g, _class_rows(r, dil), :] = l_ref[r]
        la, lb_, lc = l_pos[0], l_pos[1], l_pos[2]
        mx = jnp.maximum(jnp.maximum(la, lb_), lc)
        es = (jnp.exp(la - mx), jnp.exp(lb_ - mx), jnp.exp(lc - mx))
        z = es[0] + es[1] + es[2]
        lt_ref[...] = mx + jnp.log(z)
        acc = jnp.zeros((POS_BLK, LANES), F32)
        for g in range(3):
            acc += _exact_dot_r(es[g] / z, sel_ref[...]) * o_pos[g]
        of_ref[...] = acc
        ob_ref[...] = acc.astype(BF16)

    return pl.pallas_call(
        body, name="attn_combine", grid=(s // POS_BLK, d // LANES),
        in_specs=[_class_spec(dil) for dil in DILATIONS] + [_class_spec(dil, False) for dil in DILATIONS]
        + [pl.BlockSpec((LANES, LANES), lambda i, t: (0, t))],
        out_specs=[_pos_spec(), _pos_spec(), _pos_spec(False)],
        out_shape=[jax.ShapeDtypeStruct((s, d), F32), jax.ShapeDtypeStruct((s, d), BF16),
                   jax.ShapeDtypeStruct((s, LANES), F32)],
        scratch_shapes=[pltpu.VMEM((3, POS_BLK, LANES), F32), pltpu.VMEM((3, POS_BLK, LANES), F32)],
        compiler_params=_params(("parallel", "arbitrary")),
    )(*[_class_view(o, dil) for o, dil in zip(os_, DILATIONS)],
      *[_class_view(l, dil) for l, dil in zip(lses, DILATIONS)], sel)


ATTN_TILE_GROUP = 2
AUX_PER_TILE = 12


def _aux_placement():
    h = lax.broadcasted_iota(jnp.int32, (6, LANES, LANES), 1)
    l = lax.broadcasted_iota(jnp.int32, (6, LANES, LANES), 2)
    j = lax.broadcasted_iota(jnp.int32, (6, LANES, LANES), 0)
    target = AUX_PER_TILE * (h // 2) + 3 * (h % 2) + jnp.where(j < 3, j, 3 + j)
    return ((l == target) & (h < ATTN_HEADS)).astype(BF16)


def _attn_bwd_prep(do, o, lse):
    s, d = do.shape
    tiles = d // LANES
    sel_t = _head_expand_matrix().T

    def body(do_ref, o_ref, l_ref, sel_ref, place_ref, *refs):
        outs, delta, aux = refs[:9], refs[9], refs[10]
        t = pl.program_id(1)
        part = _exact_dot_r(do_ref[...] * o_ref[...], sel_ref[...])

        @pl.when(t == 0)
        def _():
            delta[...] = part

        @pl.when(t > 0)
        def _():
            delta[...] += part

        for g, dil in enumerate(DILATIONS):
            for r in range(dil):
                blk = do_ref[_class_rows(r, dil), :].astype(BF16)
                outs[g][r] = blk
                outs[3 + g][r] = blk.T

        @pl.when(t == tiles - 1)
        def _():
            pieces = _split3(l_ref[...]) + _split3(delta[...])
            acc = _dot(pieces[0], place_ref[0])
            for j in range(1, 6):
                acc += _dot(pieces[j], place_ref[j])
            aux[...] = acc
            for g, dil in enumerate(DILATIONS):
                for r in range(dil):
                    outs[6 + g][r] = aux[_class_rows(r, dil), :].astype(BF16)

    outs = pl.pallas_call(
        body, name="attn_bwd_prep", grid=(s // POS_BLK, tiles),
        in_specs=[_pos_spec(), _pos_spec(), _pos_spec(False), pl.BlockSpec((LANES, LANES), lambda i, t: (t, 0)),
                  pl.BlockSpec((6, LANES, LANES), lambda i, t: (0, 0, 0))],
        out_specs=[_class_spec(dil) for dil in DILATIONS]
        + [pl.BlockSpec((dil, LANES, POS_BLK // dil), lambda i, t: (0, t, i)) for dil in DILATIONS]
        + [_class_spec(dil, False) for dil in DILATIONS],
        out_shape=[jax.ShapeDtypeStruct((dil, s // dil, d), BF16) for dil in DILATIONS]
        + [jax.ShapeDtypeStruct((dil, d, s // dil), BF16) for dil in DILATIONS]
        + [jax.ShapeDtypeStruct((dil, s // dil, LANES), BF16) for dil in DILATIONS],
        scratch_shapes=[pltpu.VMEM((POS_BLK, LANES), F32), pltpu.VMEM((POS_BLK, LANES), F32)],
        compiler_params=_params(("parallel", "arbitrary")),
    )(do, o, lse, sel_t, _aux_placement())
    return ([a.reshape(s, d) for a in outs[0:3]], list(outs[3:6]), [a.reshape(s, LANES) for a in outs[6:9]])


def _attn_bwd(qkv, qkv_t, do, do_t, aux, tables, dil, *, name):
    _, s, d = qkv.shape
    nq = s // ATTN_BLK
    per = nq // dil
    scale = ATTN_HEAD_DIM ** -0.5
    assert scale == 0.125
    tiles = d // LANES
    half = ATTN_HEAD_DIM

    def body(qd_ref, qo_ref, k_ref, qtd_ref, qto_ref, kt_ref, vt_ref, dod_ref, doo_ref, dotd_ref, doto_ref,
             auxd_ref, auxo_ref, c_ref, ss_ref, out_ref, carry):
        kb = pl.program_id(0)

        @pl.when(kb == 0)
        def _():
            carry[...] = jnp.zeros_like(carry)

        has_next = (kb % per) != (per - 1)
        qi = lax.broadcasted_iota(jnp.int32, (ATTN_BLK, 2 * ATTN_BLK), 0)
        kj = lax.broadcasted_iota(jnp.int32, (ATTN_BLK, 2 * ATTN_BLK), 1) % ATTN_BLK
        valid_d = kj <= qi
        valid_o = (kj >= qi) & has_next
        lane = lax.broadcasted_iota(jnp.int32, (ATTN_BLK, LANES), 1)
        row = lax.broadcasted_iota(jnp.int32, (LANES, ATTN_BLK), 0)
        side = lax.broadcasted_iota(jnp.int32, (LANES, 2 * ATTN_BLK), 0)
        first = lax.broadcasted_iota(jnp.int32, (LANES, 2 * ATTN_BLK), 1) < ATTN_BLK
        c, ss = c_ref[...], ss_ref[...]
        zero = jnp.zeros((), BF16)
        sides = ((qd_ref, qtd_ref, dod_ref, dotd_ref, auxd_ref[...], valid_d),
                 (qo_ref, qto_ref, doo_ref, doto_ref, auxo_ref[...], valid_o))
        both = (0, 1)

        def head_halves(x, index):
            axis = 0 if index is lane else 1
            return jnp.concatenate([jnp.where(index < half, x, zero), jnp.where(index >= half, x, zero)], axis=axis)

        for t0 in range(0, tiles, ATTN_TILE_GROUP):
            group = range(t0, t0 + ATTN_TILE_GROUP)
            cols = {t: pl.ds(t * LANES, LANES) for t in group}
            kk, kk_t, vv_t = {}, {}, {}
            for t in group:
                base = AUX_PER_TILE * t
                hit = lambda lo: ((first & (side >= base + lo) & (side < base + lo + 3))
                                  | (jnp.logical_not(first) & (side >= base + lo + 3) & (side < base + lo + 6)))
                kk[t] = head_halves(k_ref[:, cols[t]], lane)
                kk_t[t] = jnp.concatenate([head_halves(kt_ref[cols[t], :], row),
                                           jnp.where(hit(0), -1.0 / scale, 0.0).astype(BF16)], axis=0)
                vv_t[t] = jnp.concatenate([head_halves(vt_ref[cols[t], :], row),
                                           jnp.where(hit(6), -1.0, 0.0).astype(BF16)], axis=0)
            sc = {(t, w): _dot(jnp.concatenate([sides[w][0][:, cols[t]], sides[w][4]], axis=1), kk_t[t])
                  for t in group for w in both}
            dpd = {(t, w): _dot(jnp.concatenate([sides[w][2][:, cols[t]], sides[w][4]], axis=1), vv_t[t])
                   for t in group for w in both}
            p = {i: jnp.where(sides[i[1]][5], jnp.exp(sc[i] * scale), 0.0) for i in sc}
            ds = {i: (p[i] * dpd[i] * scale).astype(BF16) for i in sc}
            pb = {i: p[i].astype(BF16) for i in sc}
            dv_t = {t: sum(_dot(sides[w][3][cols[t], :], pb[t, w]) for w in both) for t in group}
            dk_t = {t: sum(_dot(sides[w][1][cols[t], :], ds[t, w]) for w in both) for t in group}
            dq = {i: _dot(ds[i], kk[i[0]]) for i in sc}
            for t in group:
                dq_now = carry[:, cols[t]] + dq[t, 0]
                carry[:, cols[t]] = dq[t, 1]
                dk = jnp.where(row < half, dk_t[t][:, :ATTN_BLK], dk_t[t][:, ATTN_BLK:]).T
                dv = jnp.where(row < half, dv_t[t][:, :ATTN_BLK], dv_t[t][:, ATTN_BLK:]).T
                out_ref[0, :, cols[t]] = _rotate(dq_now, c, ss, -1.0).astype(BF16)
                out_ref[1, :, cols[t]] = _rotate(dk, c, ss, -1.0).astype(BF16)
                out_ref[2, :, cols[t]] = dv.astype(BF16)

    def nxt(i):
        return jnp.minimum(i + 1, nq - 1)

    def piece(p, shift):
        if shift:
            return pl.BlockSpec((None, ATTN_BLK, d), lambda i: (p, nxt(i), 0))
        return pl.BlockSpec((None, ATTN_BLK, d), lambda i: (p, i, 0))

    def piece_t(p, shift):
        if shift:
            return pl.BlockSpec((None, d, ATTN_BLK), lambda i: (p, 0, nxt(i)))
        return pl.BlockSpec((None, d, ATTN_BLK), lambda i: (p, 0, i))

    def rows(width, shift):
        if shift:
            return pl.BlockSpec((ATTN_BLK, width), lambda i: (nxt(i), 0))
        return pl.BlockSpec((ATTN_BLK, width), lambda i: (i, 0))

    def do_t_spec(shift):
        if shift:
            return pl.BlockSpec((None, d, ATTN_BLK), lambda i: (nxt(i) // per, 0, nxt(i) % per))
        return pl.BlockSpec((None, d, ATTN_BLK), lambda i: (i // per, 0, i % per))

    return pl.pallas_call(
        body, name=name, grid=(nq,),
        in_specs=[piece(0, False), piece(0, True), piece(1, False),
                  piece_t(0, False), piece_t(0, True), piece_t(1, False), piece_t(2, False),
                  rows(d, False), rows(d, True), do_t_spec(False), do_t_spec(True),
                  rows(LANES, False), rows(LANES, True), rows(LANES, False), rows(LANES, False)],
        out_specs=pl.BlockSpec((3, ATTN_BLK, d), lambda i: (0, i, 0)),
        out_shape=jax.ShapeDtypeStruct((3, s, d), BF16),
        scratch_shapes=[pltpu.VMEM((ATTN_BLK, d), F32)],
        compiler_params=_params(("arbitrary",)),
    )(qkv, qkv, qkv, qkv_t, qkv_t, qkv_t, qkv_t, do, do, do_t, do_t, aux, aux, *tables)


def _dx_combine(du, parts):
    s, d = du.shape

    def body(du_ref, p0, p1, p2, out_ref):
        out_ref[...] = ALPHA * du_ref[...] + p0[0]
        for dil, p_ref in zip(DILATIONS[1:], (p1, p2)):
            for r in range(dil):
                out_ref[_class_rows(r, dil), :] += p_ref[r]

    return pl.pallas_call(
        body, name="dx_combine", grid=(s // POS_BLK, d // LANES),
        in_specs=[_pos_spec()] + [_class_spec(dil) for dil in DILATIONS], out_specs=_pos_spec(),
        out_shape=jax.ShapeDtypeStruct((s, d), F32),
        compiler_params=_params(("parallel", "parallel")),
    )(du, *[_class_view(p, dil) for p, dil in zip(parts, DILATIONS)])


HGRN_ROWS = 512


def _tri(lower):
    i = lax.broadcasted_iota(jnp.int32, (HGRN_CHUNK, HGRN_CHUNK), 0)
    j = lax.broadcasted_iota(jnp.int32, (HGRN_CHUNK, HGRN_CHUNK), 1)
    return (j <= i) if lower else (j >= i)


def _hgrn_gates(qr, z, lb):
    sq = jax.nn.sigmoid(qr)
    e = jnp.exp(-jnp.abs(z))
    big = 1.0 / (1.0 + e)
    small = e * big
    sg = jnp.where(z >= 0, big, small)
    sn = jnp.where(z >= 0, small, big)
    return sq, qr * sq, sg, sn, lb + (1.0 - lb) * sg, (1.0 - lb) * sn


def _hgrn_fwd(p2, lb, norm_g):
    _, s, d = p2.shape
    nblk = s // HGRN_ROWS
    cps = HGRN_ROWS // HGRN_CHUNK

    def body(p_ref, lb_ref, g_ref, o_ref, on_ref, st_ref, state):
        @pl.when(pl.program_id(1) == 0)
        def _():
            state[...] = jnp.zeros_like(state)

        chunks = [pl.ds(c * HGRN_CHUNK, HGRN_CHUNK) for c in range(cps)]
        ltri = _tri(True)
        lsel = ltri.astype(BF16)
        _, q, _, _, f, key = _hgrn_gates(p_ref[0], p_ref[1], lb_ref[...])
        lf = jnp.log(f)
        v = [p_ref[2, rows, :].astype(BF16) for rows in chunks]
        b = [_exact_dot(lsel, lf[c * HGRN_CHUNK:(c + 1) * HGRN_CHUNK]) for c in range(cps)]
        b_last = [bc[HGRN_CHUNK - 1:HGRN_CHUNK, :] for bc in b]
        q = [q[c * HGRN_CHUNK:(c + 1) * HGRN_CHUNK] for c in range(cps)]
        key = [key[c * HGRN_CHUNK:(c + 1) * HGRN_CHUNK] for c in range(cps)]
        qd = [(q[c] * jnp.exp(b[c])).astype(BF16) for c in range(cps)]
        kd = [(key[c] * jnp.exp(-b[c])).astype(BF16) for c in range(cps)]
        k2 = [(key[c] * jnp.exp(b_last[c] - b[c])).astype(BF16) for c in range(cps)]
        a = [jnp.where(ltri, _dot_nt(qd[c], kd[c]), 0.0).astype(BF16) for c in range(cps)]
        kv = [_dot_tn(v[c], k2[c]) for c in range(cps)]
        st, sts = state[...], []
        for c in range(cps):
            sts.append(st)
            st_ref[0, c] = st
            st = st * jnp.exp(b_last[c]) + kv[c]
        state[...] = st
        gv = g_ref[...]
        for c in range(cps):
            o = _dot(a[c], v[c]) + _dot_nt(qd[c], sts[c].astype(BF16))
            o_ref[chunks[c], :] = o
            r = lax.rsqrt(jnp.mean(o * o, axis=-1, keepdims=True) + RMS_EPS)
            on_ref[chunks[c], :] = (o * r * gv).astype(BF16)

    vec = pl.BlockSpec((1, LANES), lambda h, c: (0, h))
    col = pl.BlockSpec((HGRN_ROWS, LANES), lambda h, c: (c, h))
    return pl.pallas_call(
        body, name="hgrn_fwd", grid=(HGRN_HEADS, nblk),
        in_specs=[pl.BlockSpec((3, HGRN_ROWS, LANES), lambda h, c: (0, c, h)), vec, vec],
        out_specs=[col, col, pl.BlockSpec((1, cps, LANES, LANES), lambda h, c: (h, c, 0, 0))],
        out_shape=[jax.ShapeDtypeStruct((s, d), F32), jax.ShapeDtypeStruct((s, d), BF16),
                   jax.ShapeDtypeStruct((HGRN_HEADS, s // HGRN_CHUNK, LANES, LANES), F32)],
        scratch_shapes=[pltpu.VMEM((LANES, LANES), F32)],
        compiler_params=_params(("parallel", "arbitrary")),
    )(p2, lb, norm_g)


def _hgrn_bwd(p2, lb, norm_g, o_raw, states, dyn):
    _, s, d = p2.shape
    nblk = s // HGRN_ROWS
    cps = HGRN_ROWS // HGRN_CHUNK

    def body(p_ref, lb_ref, g_ref, o_ref, st_ref, dy_ref, dp_ref, dg_ref, dlb_ref, dstate):
        @pl.when(pl.program_id(1) == 0)
        def _():
            dstate[...] = jnp.zeros_like(dstate)
            dg_ref[...] = jnp.zeros_like(dg_ref)
            dlb_ref[...] = jnp.zeros_like(dlb_ref)

        n = cps
        cut = lambda t: [t[c * HGRN_CHUNK:(c + 1) * HGRN_CHUNK] for c in range(n)]
        chunks = [pl.ds(c * HGRN_CHUNK, HGRN_CHUNK) for c in range(n)]
        lbv = lb_ref[...]
        ltri = _tri(True)
        lsel = ltri.astype(BF16)
        usel = _tri(False).astype(BF16)
        last_row = lax.broadcasted_iota(jnp.int32, (HGRN_CHUNK, LANES), 0) == HGRN_CHUNK - 1
        qr, z = p_ref[0], p_ref[1]
        sq, q, sg, sn, f, key = _hgrn_gates(qr, z, lbv)
        lf = jnp.log(f)
        v = [p_ref[2, rows, :].astype(BF16) for rows in chunks]
        o, dyv = o_ref[...], dy_ref[...]
        r = lax.rsqrt(jnp.mean(o * o, axis=-1, keepdims=True) + RMS_EPS)
        oh = o * r
        doh = dyv * g_ref[...]
        do = cut((r * (doh - oh * jnp.mean(doh * oh, axis=-1, keepdims=True))).astype(BF16))
        dg_ref[...] += jnp.sum(dyv * oh, axis=0, keepdims=True)
        b = [_exact_dot(lsel, lfc) for lfc in cut(lf)]
        b_last = [bc[HGRN_CHUNK - 1:HGRN_CHUNK, :] for bc in b]
        q, key = cut(q), cut(key)
        eb = [jnp.exp(bc) for bc in b]
        enb = [jnp.exp(-bc) for bc in b]
        e2 = [jnp.exp(b_last[c] - b[c]) for c in range(n)]
        dec = [jnp.exp(bl) for bl in b_last]
        qd_f = [q[c] * eb[c] for c in range(n)]
        kd_f = [key[c] * enb[c] for c in range(n)]
        k2_f = [key[c] * e2[c] for c in range(n)]
        qd, kd, k2 = ([t.astype(BF16) for t in ts] for ts in (qd_f, kd_f, k2_f))
        a = [jnp.where(ltri, _dot_nt(qd[c], kd[c]), 0.0).astype(BF16) for c in range(n)]
        da = [jnp.where(ltri, _dot_nt(do[c], v[c]), 0.0).astype(BF16) for c in range(n)]
        grow = [_dot_tn(do[c], qd[c]) for c in range(n)]
        dst, dsts = dstate[...], [None] * n
        for c in reversed(range(n)):
            dsts[c] = dst
            dst = dst * dec[c] + grow[c]
        dstate[...] = dst
        st = [st_ref[0, c] for c in range(n)]
        dstb = [t.astype(BF16) for t in dsts]
        dv = [_dot_tn(a[c], do[c]) + _dot_nt(k2[c], dstb[c]) for c in range(n)]
        dqd = [_dot(da[c], kd[c]) + _dot(do[c], st[c].astype(BF16)) for c in range(n)]
        dkd = [_dot_tn(da[c], qd[c]) for c in range(n)]
        dk2 = [_dot(v[c], dstb[c]) for c in range(n)]
        db = []
        for c in range(n):
            ddec = jnp.sum(dsts[c] * st[c], axis=0, keepdims=True)
            db_last = jnp.sum(dk2[c] * k2_f[c], axis=0, keepdims=True) + ddec * dec[c]
            db.append(dqd[c] * qd_f[c] - dkd[c] * kd_f[c] - dk2[c] * k2_f[c] + jnp.where(last_row, db_last, 0.0))
        dlf = [_exact_dot(usel, dbc) for dbc in db]
        f, sg, sn, sq, qr = cut(f), cut(sg), cut(sn), cut(sq), cut(qr)
        dlb_acc = jnp.zeros((1, LANES), F32)
        for c in range(n):
            dkey = dkd[c] * enb[c] + dk2[c] * e2[c]
            common = dlf[c] / f[c] - dkey
            dlb_acc += jnp.sum(common * sn[c], axis=0, keepdims=True)
            dp_ref[0, chunks[c], :] = (dqd[c] * eb[c] * (sq[c] * (1.0 + qr[c] * (1.0 - sq[c])))).astype(BF16)
            dp_ref[1, chunks[c], :] = (common * ((1.0 - lbv) * sg[c] * sn[c])).astype(BF16)
            dp_ref[2, chunks[c], :] = dv[c].astype(BF16)
        dlb_ref[...] += dlb_acc

    def rev(c):
        return nblk - 1 - c

    vec = pl.BlockSpec((1, LANES), lambda h, c: (0, h))
    col = pl.BlockSpec((HGRN_ROWS, LANES), lambda h, c: (rev(c), h))
    p3 = pl.BlockSpec((3, HGRN_ROWS, LANES), lambda h, c: (0, rev(c), h))
    return pl.pallas_call(
        body, name="hgrn_bwd", grid=(HGRN_HEADS, nblk),
        in_specs=[p3, vec, vec, col, pl.BlockSpec((1, cps, LANES, LANES), lambda h, c: (h, rev(c), 0, 0)), col],
        out_specs=[p3, vec, vec],
        out_shape=[jax.ShapeDtypeStruct((3, s, d), BF16), jax.ShapeDtypeStruct((1, d), F32),
                   jax.ShapeDtypeStruct((1, d), F32)],
        scratch_shapes=[pltpu.VMEM((LANES, LANES), F32)],
        compiler_params=_params(("parallel", "arbitrary")),
    )(p2, lb, norm_g, o_raw, states, dyn)


SMALL_ROWS = 16


def _relu2(h):
    r = jnp.maximum(h.astype(F32), 0.0)
    return r * r


def _dact_epi(acc, h):
    return (acc * (2.0 * jnp.maximum(h.astype(F32), 0.0)),)


def _res_epi(scale):
    def epi(acc, r):
        return (acc + scale * r,)
    return epi


def _rope_epi(j, acc, c, ss):
    return (jnp.where(j >= 2, acc, _rotate(acc, c, ss)),)


def _local_step(x, target, get_w, small, on_grads, deps=()):
    s, d = x.shape
    lb = small["lb"]
    gq = 3 * d
    gt = gq // 1024

    def ffn_fwd(xb, w_up, w_down, tag):
        h = _mm(xb, w_up, "nn", name=f"ffn_up_{tag}", m=s, n=D_FF, k=d, out_dtypes=(BF16,))
        y = _mm(h, w_down, "nn", name=f"ffn_down_{tag}", m=s, n=d, k=D_FF, a_pre=_relu2)
        return h, y

    def ffn_bwd(du, dub, xb, h, w_up, w_down, tag):
        dh = _mm(dub, w_down, "nt", name=f"ffn_dact_{tag}", m=s, n=D_FF, k=d, out_dtypes=(BF16,), epi=_dact_epi,
                 tile_extras=(h,))
        g_down = _mm(h, dub, "tn", name=f"ffn_gdown_{tag}", m=D_FF, n=d, k=s, a_pre=_relu2, out_dtypes=(BF16,))
        g_up = _mm(xb, dh, "tn", name=f"ffn_gup_{tag}", m=d, n=D_FF, k=s, out_dtypes=(BF16,))
        after = on_grads(f"ffn{tag}", {f"ffn_w_down{tag}": g_down, f"ffn_w_up{tag}": g_up})
        return _mm(dh, w_up, "nt", name=f"ffn_dx_{tag}", m=s, n=d, k=D_FF, epi=_res_epi(ALPHA), tile_extras=(du,),
                   deps=after)

    xs = _prep_x(x, deps)
    w_ain, after = get_w("attn_w_in", None)
    tabs, qkvs, qkv_ts, o_parts, lse_parts = [], [], [], [], []
    for g, dil in enumerate(DILATIONS):
        tabs.append(_rope_tables(s, dil))
        qkv, qkv_t = _mm(xs[g], w_ain, "nn", name=f"attn_in_{g}", m=s, n=gq, k=d, b_col_off=g * gt, out_split=3,
                         out_dtypes=(BF16,), epi=_rope_epi, epi_wants_j=True, row_extras=tabs[g], t_out=True,
                         deps=after if g == 0 else ())
        qkvs.append(qkv)
        qkv_ts.append(qkv_t)
        o_g, lse_g = _attn_fwd(qkv, dil, name=f"attn_fwd_{g}")
        o_parts.append(o_g)
        lse_parts.append(lse_g)
    o_f, o_b, lse_t = _attn_combine(o_parts, lse_parts)
    w_aout, after = get_w("attn_w_out", o_b)
    y = _mm(o_b, w_aout, "nn", name="attn_out", m=s, n=d, k=d, deps=after)
    x1, x1b, xh1, r1 = _ln_fwd(x, y, small["ln_mix_g"][0:1], small["ln_mix_b"][0:1], name="ln_mix_0")
    w_up0, w_down0 = get_w("ffn_w_up0", o_b)[0], get_w("ffn_w_down0", o_b)[0]
    h0, y = ffn_fwd(x1b, w_up0, w_down0, 0)
    x2, x2b, xh2, r2 = _ln_fwd(x1, y, small["ln_ffn_g"][0:1], small["ln_ffn_b"][0:1], name="ln_ffn_0")
    w_hin, w_hout, norm_g = get_w("hgrn_w_in", x2b)[0], get_w("hgrn_w_out", x2b)[0], get_w("hgrn_norm_g", x2b)[0]
    p2 = _mm(x2b, w_hin, "nn", name="hgrn_in", m=s, n=3 * d, k=d, out_split=3)
    o_raw, o_n, states = _hgrn_fwd(p2, lb, norm_g)
    w_up1, w_down1 = get_w("ffn_w_up1", o_n)[0], get_w("ffn_w_down1", o_n)[0]
    y = _mm(o_n, w_hout, "nn", name="hgrn_out", m=s, n=d, k=d)
    x3, x3b, xh3, r3 = _ln_fwd(x2, y, small["ln_mix_g"][1:2], small["ln_mix_b"][1:2], name="ln_mix_1")
    h1, y = ffn_fwd(x3b, w_up1, w_down1, 1)
    x4, _, xh4, r4 = _ln_fwd(x3, y, small["ln_ffn_g"][1:2], small["ln_ffn_b"][1:2], name="ln_ffn_1")
    dy, sq = _loss_head(x4, target)
    du, dub, dg_ffn1, db_ffn1 = _ln_bwd(dy, xh4, r4, small["ln_ffn_g"][1:2], name="ln_ffn_1_bwd")
    dx3 = ffn_bwd(du, dub, x3b, h1, w_up1, w_down1, 1)
    du, dub, dg_mix1, db_mix1 = _ln_bwd(dx3, xh3, r3, small["ln_mix_g"][1:2], name="ln_mix_1_bwd")
    dyn = _mm(dub, w_hout, "nt", name="hgrn_dout", m=s, n=d, k=d)
    g_hout = _mm(o_n, dub, "tn", name="hgrn_gout", m=d, n=d, k=s, out_dtypes=(BF16,))
    dp2, d_norm_g, d_lb = _hgrn_bwd(p2, lb, norm_g, o_raw, states, dyn)
    g_hin = _mm(x2b, dp2, "tn", name="hgrn_gin", m=d, n=3 * d, k=s, b_split=3, out_dtypes=(BF16,))
    after = on_grads("hgrn", {"hgrn_w_out": g_hout, "hgrn_w_in": g_hin})
    dx2 = _mm(dp2, w_hin, "nt", name="hgrn_dx", m=s, n=d, k=3 * d, a_split=3, epi=_res_epi(ALPHA),
              tile_extras=(du,), deps=after)
    du, dub, dg_ffn0, db_ffn0 = _ln_bwd(dx2, xh2, r2, small["ln_ffn_g"][0:1], name="ln_ffn_0_bwd")
    dx1 = ffn_bwd(du, dub, x1b, h0, w_up0, w_down0, 0)
    du, dub, dg_mix0, db_mix0 = _ln_bwd(dx1, xh1, r1, small["ln_mix_g"][0:1], name="ln_mix_0_bwd")
    g_aout = _mm(o_b, dub, "tn", name="attn_gout", m=d, n=d, k=s, out_dtypes=(BF16,))
    sm1 = jax.nn.softmax(small["lb_logits"], axis=0)
    d_l1 = d_lb * (sm1[0:1] * sm1[1:2])
    zeros = jnp.zeros((SMALL_ROWS - 12, d), F32)
    loss_row = jnp.broadcast_to(0.5 * jnp.sum(sq) / d, (1, d))
    small_grads = jnp.concatenate([dg_mix0, dg_mix1, db_mix0, db_mix1, dg_ffn0, dg_ffn1, db_ffn0, db_ffn1,
                                   -d_l1, d_l1, d_norm_g, loss_row, zeros], axis=0)
    after = on_grads("attn_out", {"attn_w_out": g_aout, "small": small_grads})
    do = _mm(dub, w_aout, "nt", name="attn_dout", m=s, n=d, k=d, deps=after)
    do_parts, do_ts, aux_parts = _attn_bwd_prep(do, o_f, lse_t)
    g_ain, dqkvs = None, []
    for g, dil in enumerate(DILATIONS):
        dqkvs.append(_attn_bwd(qkvs[g], qkv_ts[g], do_parts[g], do_ts[g], aux_parts[g], tabs[g], dil,
                               name=f"attn_bwd_{g}"))
        g_ain = _mm(xs[g], dqkvs[g], "tn", name=f"attn_gin_{g}", m=d, n=gq, k=s, b_split=3, out_dtypes=(BF16,),
                    out_col_off=g * gt, out_cols=3 * gq, alias=g_ain)
    after = on_grads("attn_in", {"attn_w_in": g_ain})
    dx_parts = [_mm(dqkvs[g], w_ain, "nt", name=f"attn_dx_{g}", m=s, n=d, k=gq, a_split=3, b_k_off=g * gt,
                    deps=after if g == 0 else ())
                for g in range(len(DILATIONS))]
    return _dx_combine(du, dx_parts)


def _mesh_place():
    x, y, c = lax.axis_index("x"), lax.axis_index("y"), lax.axis_index("c")
    return x, y, c, 4 * x + 2 * y + c


def _peer(x, y, c, k):
    px = 1 - x if (k >> 2) & 1 else x
    py = 1 - y if (k >> 1) & 1 else y
    pc = 1 - c if k & 1 else c
    return (px, py, pc), 4 * px + 2 * py + pc


def _window(ref, axis, size, idx):
    if axis is None:
        return ref
    sl = [slice(None)] * len(ref.shape)
    sl[axis] = pl.ds(idx * size, size)
    return ref.at[tuple(sl)]


_HBM = pl.BlockSpec(memory_space=pltpu.HBM)
_SEM = pl.BlockSpec(memory_space=pltpu.SEMAPHORE)
_EFFECT = pltpu.SideEffectType.DATAFLOW_SIDE_EFFECTING


def _xchg_ends(src_ref, land_ref, axis, gather, me, other):
    if gather:
        size = src_ref.shape[axis]
        return src_ref, _window(land_ref, axis, size, me), _window(land_ref, axis, size, other)
    size = None if axis is None else src_ref.shape[axis] // N_DEV
    return _window(src_ref, axis, size, other), land_ref.at[me], land_ref.at[other]


def _xchg_start(srcs, lands, axes, *, gather, name, deps=()):
    n = len(srcs)
    nd = len(deps)

    def body(*refs):
        src_refs, land_refs = refs[:n], refs[n:2 * n]
        send, recv = refs[2 * n + nd:3 * n + nd], refs[3 * n + nd:4 * n + nd]
        token = refs[-1]
        x, y, c, me = _mesh_place()
        for k in range(1, N_DEV):
            peer, pidx = _peer(x, y, c, k)
            for i in range(n):
                src, dst, _ = _xchg_ends(src_refs[i], land_refs[i], axes[i], gather, me, pidx)
                pltpu.make_async_remote_copy(
                    src_ref=src, dst_ref=dst, send_sem=send[i].at[k - 1], recv_sem=recv[i].at[k - 1],
                    device_id=peer, device_id_type=pl.DeviceIdType.MESH).start()
        for i in range(n):
            src, dst, _ = _xchg_ends(src_refs[i], land_refs[i], axes[i], gather, me, me)
            pltpu.make_async_copy(src, dst, send[i].at[N_DEV - 1]).start()
        token[...] = jnp.zeros_like(token)

    bufs = list(srcs) + list(lands)
    outs = pl.pallas_call(
        body, name=name,
        out_shape=[pltpu.SemaphoreType.DMA((N_DEV,))] * (2 * n) + [pltpu.HBM(b.shape, b.dtype) for b in bufs]
        + [jax.ShapeDtypeStruct((8, LANES), F32)],
        in_specs=[_HBM] * (2 * n) + [pl.BlockSpec(memory_space=pl.ANY)] * nd,
        out_specs=[_SEM] * (2 * n) + [_HBM] * (2 * n) + [pl.BlockSpec(memory_space=pltpu.VMEM)],
        input_output_aliases={i: 2 * n + i for i in range(2 * n)},
        compiler_params=pltpu.CompilerParams(has_side_effects=_EFFECT),
    )(*[pltpu.with_memory_space_constraint(b, pltpu.HBM) for b in bufs], *deps)
    return dict(send=outs[:n], recv=outs[n:2 * n], srcs=outs[2 * n:3 * n], lands=outs[3 * n:4 * n], token=outs[-1],
                axes=list(axes), gather=gather)


def _xchg_wait(xc, items, after, *, name):
    m = len(items)
    gather = xc["gather"]
    axes = [xc["axes"][i] for i in items]

    def body(*refs):
        src_refs, land_refs = refs[:m], refs[m:2 * m]
        send, recv = refs[2 * m:3 * m], refs[3 * m:4 * m]
        x, y, c, me = _mesh_place()
        for k in range(1, N_DEV):
            peer, pidx = _peer(x, y, c, k)
            for j in range(m):
                src, dst, got = _xchg_ends(src_refs[j], land_refs[j], axes[j], gather, me, pidx)
                pltpu.make_async_remote_copy(
                    src_ref=src, dst_ref=dst, send_sem=send[j].at[k - 1], recv_sem=recv[j].at[k - 1],
                    device_id=peer, device_id_type=pl.DeviceIdType.MESH).wait_send()
                pltpu.make_async_remote_copy(
                    src_ref=src, dst_ref=got, send_sem=send[j].at[k - 1], recv_sem=recv[j].at[k - 1],
                    device_id=peer, device_id_type=pl.DeviceIdType.MESH).wait_recv()
        for j in range(m):
            src, dst, _ = _xchg_ends(src_refs[j], land_refs[j], axes[j], gather, me, me)
            pltpu.make_async_copy(src, dst, send[j].at[N_DEV - 1]).wait()

    bufs = [xc["srcs"][i] for i in items] + [xc["lands"][i] for i in items]
    sems = [xc["send"][i] for i in items] + [xc["recv"][i] for i in items]
    outs = pl.pallas_call(
        body, name=name, out_shape=[pltpu.HBM(b.shape, b.dtype) for b in bufs],
        in_specs=[_HBM] * (2 * m) + [_SEM] * (2 * m) + [pl.BlockSpec(memory_space=pl.ANY)],
        out_specs=[_HBM] * (2 * m), input_output_aliases={j: j for j in range(2 * m)},
        compiler_params=pltpu.CompilerParams(has_side_effects=_EFFECT),
    )(*bufs, *sems, after)
    return outs[m:]


def _cast_bf16(a, *, name):
    r, c = a.shape
    tr = min(r, 512)

    def body(a_ref, o_ref):
        o_ref[...] = a_ref[...].astype(BF16)

    spec = pl.BlockSpec((tr, c), lambda i: (i, 0))
    return pl.pallas_call(body, name=name, grid=(r // tr,), in_specs=[spec], out_specs=spec,
                          out_shape=jax.ShapeDtypeStruct((r, c), BF16), compiler_params=_params(("parallel",)))(a)


def _adamw(slabs, w, m, v, *, name):
    layers, r, c = w.shape
    tr = min(r, 256)

    def body(*refs):
        s_refs = refs[:layers]
        w_ref, m_ref, v_ref, g_ref, d_ref, mo_ref, vo_ref = refs[layers:]
        for l in range(layers):
            g = s_refs[l][0].astype(F32)
            for i in range(1, N_DEV):
                g = g + s_refs[l][i].astype(F32)
            m2 = ADAM_B1 * m_ref[l] + (1.0 - ADAM_B1) * g
            v2 = ADAM_B2 * v_ref[l] + (1.0 - ADAM_B2) * (g * g)
            m_hat = m2 / (1.0 - ADAM_B1 ** ADAM_STEP)
            v_hat = v2 / (1.0 - ADAM_B2 ** ADAM_STEP)
            g_ref[l] = g
            d_ref[l] = -ADAM_LR * (m_hat / (jnp.sqrt(v_hat) + ADAM_EPS) + ADAM_WD * w_ref[l])
            mo_ref[l] = m2
            vo_ref[l] = v2

    spec = pl.BlockSpec((layers, tr, c), lambda i: (0, i, 0))
    return pl.pallas_call(
        body, name=name, grid=(r // tr,),
        in_specs=[pl.BlockSpec((N_DEV, tr, c), lambda i: (0, i, 0))] * layers + [spec, spec, spec],
        out_specs=[spec] * 4, out_shape=[jax.ShapeDtypeStruct((layers, r, c), F32)] * 4,
        compiler_params=_params(("parallel",)),
    )(*slabs, w, m, v)


GATHER_AXIS = {"attn_w_in": 1, "attn_w_out": 0, "ffn_w_up0": 1, "ffn_w_down0": 0, "hgrn_w_in": 1, "hgrn_w_out": 0,
               "hgrn_norm_g": 1, "ffn_w_up1": 1, "ffn_w_down1": 0}
GATHER_STAGES = (("attn_w_in",), ("attn_w_out", "ffn_w_up0", "ffn_w_down0", "hgrn_w_in", "hgrn_w_out", "hgrn_norm_g"),
                 ("ffn_w_up1", "ffn_w_down1"))
GATHER_WAITS = ((("attn_w_in",), 0, 1), (("attn_w_out", "ffn_w_up0", "ffn_w_down0"), 1, 2),
                (("hgrn_w_in", "hgrn_w_out", "hgrn_norm_g"), 1, None), (("ffn_w_up1", "ffn_w_down1"), 2, None))
SCATTER_AXIS = dict(GATHER_AXIS, small=None)
BIG = ("attn_w_in", "attn_w_out", "hgrn_w_in", "hgrn_w_out", "ffn_w_up", "ffn_w_down")
SMALL = ("lb_logits", "ln_mix_g", "ln_mix_b", "ln_ffn_g", "ln_ffn_b")
SMALL_ROW = {"ln_mix_g": 0, "ln_mix_b": 2, "ln_ffn_g": 4, "ln_ffn_b": 6, "lb_logits": 8}
NORM_G_ROW = 10
LOSS_ROW = 11


def kernel(x, attn_w_in, attn_w_out, hgrn_w_in, hgrn_w_out, hgrn_norm_g, lb_logits, ln_mix_g, ln_mix_b, ln_ffn_g, ln_ffn_b, ffn_w_up, ffn_w_down, loss_target, m_attn_w_in, m_attn_w_out, m_hgrn_w_in, m_hgrn_w_out, m_hgrn_norm_g, m_lb_logits, m_ln_mix_g, m_ln_mix_b, m_ln_ffn_g, m_ln_ffn_b, m_ffn_w_up, m_ffn_w_down, v_attn_w_in, v_attn_w_out, v_hgrn_w_in, v_hgrn_w_out, v_hgrn_norm_g, v_lb_logits, v_ln_mix_g, v_ln_mix_b, v_ln_ffn_g, v_ln_ffn_b, v_ffn_w_up, v_ffn_w_down):
    wts = dict(attn_w_in=attn_w_in, attn_w_out=attn_w_out, hgrn_w_in=hgrn_w_in, hgrn_w_out=hgrn_w_out,
               hgrn_norm_g=hgrn_norm_g, lb_logits=lb_logits, ln_mix_g=ln_mix_g, ln_mix_b=ln_mix_b, ln_ffn_g=ln_ffn_g,
               ln_ffn_b=ln_ffn_b, ffn_w_up=ffn_w_up, ffn_w_down=ffn_w_down)
    mom = dict(attn_w_in=m_attn_w_in, attn_w_out=m_attn_w_out, hgrn_w_in=m_hgrn_w_in, hgrn_w_out=m_hgrn_w_out,
               hgrn_norm_g=m_hgrn_norm_g, lb_logits=m_lb_logits, ln_mix_g=m_ln_mix_g, ln_mix_b=m_ln_mix_b,
               ln_ffn_g=m_ln_ffn_g, ln_ffn_b=m_ln_ffn_b, ffn_w_up=m_ffn_w_up, ffn_w_down=m_ffn_w_down)
    vel = dict(attn_w_in=v_attn_w_in, attn_w_out=v_attn_w_out, hgrn_w_in=v_hgrn_w_in, hgrn_w_out=v_hgrn_w_out,
               hgrn_norm_g=v_hgrn_norm_g, lb_logits=v_lb_logits, ln_mix_g=v_ln_mix_g, ln_mix_b=v_ln_mix_b,
               ln_ffn_g=v_ln_ffn_g, ln_ffn_b=v_ln_ffn_b, ffn_w_up=v_ffn_w_up, ffn_w_down=v_ffn_w_down)
    me = 4 * lax.axis_index("x") + 2 * lax.axis_index("y") + lax.axis_index("c")

    src = {"attn_w_in": attn_w_in[0], "attn_w_out": attn_w_out[0], "hgrn_w_in": hgrn_w_in[0], "hgrn_w_out": hgrn_w_out[0],
           "ffn_w_up0": ffn_w_up[0], "ffn_w_down0": ffn_w_down[0], "ffn_w_up1": ffn_w_up[1], "ffn_w_down1": ffn_w_down[1]}
    gathers, got = {}, {}

    def start_gather(stage, deps):
        shards, lands = [], []
        for nm in GATHER_STAGES[stage]:
            sh = hgrn_norm_g if nm == "hgrn_norm_g" else _cast_bf16(src[nm], name=f"cast_{nm}")
            ax = GATHER_AXIS[nm]
            shape = list(sh.shape)
            shape[ax] *= N_DEV
            shards.append(sh)
            lands.append(lax.empty(tuple(shape), sh.dtype))
        gathers[stage] = _xchg_start(shards, lands, [GATHER_AXIS[nm] for nm in GATHER_STAGES[stage]], gather=True,
                                     name=f"gather_start_{stage}", deps=deps)
        return [gathers[stage]["token"]]

    def get_w(name, after):
        deps = []
        if name not in got:
            group, stage, then = [w for w in GATHER_WAITS if name in w[0]][0]
            xc = gathers[stage]
            res = _xchg_wait(xc, [GATHER_STAGES[stage].index(nm) for nm in group], xc["token"] if after is None else after,
                             name=f"gather_wait_{group[0]}")
            got.update(zip(group, res))
            if then is not None:
                deps = start_gather(then, [res[0]])
        return got[name], deps

    first = start_gather(0, [])

    scattered = {}

    def on_grads(tag, grads):
        gnames = list(grads)
        axes = [SCATTER_AXIS[nm] for nm in gnames]
        stacks = []
        for nm, ax in zip(gnames, axes):
            shape = list(grads[nm].shape)
            if ax is not None:
                shape[ax] //= N_DEV
            stacks.append(lax.empty((N_DEV, *shape), grads[nm].dtype))
        scattered[tag] = (gnames, _xchg_start([grads[nm] for nm in gnames], stacks, axes, gather=False,
                                              name=f"scatter_start_{tag}"))
        return [scattered[tag][1]["token"]]

    sm = jax.nn.softmax(lb_logits, axis=0)
    csum = jnp.cumsum(sm, axis=0)
    small = dict(lb=(csum - csum[0:1])[1:2], lb_logits=lb_logits, ln_mix_g=ln_mix_g, ln_mix_b=ln_mix_b,
                 ln_ffn_g=ln_ffn_g, ln_ffn_b=ln_ffn_b)
    grad_x = _local_step(x[0], loss_target[0], get_w, small, on_grads, deps=first)
    out = {}

    def stack_small(src_):
        rows = [None] * SMALL_ROWS
        for name in SMALL:
            rows[SMALL_ROW[name]], rows[SMALL_ROW[name] + 1] = src_[name][0:1], src_[name][1:2]
        zero = jnp.zeros((1, x.shape[-1]), F32)
        return jnp.concatenate([zero if r is None else r for r in rows], axis=0)[None]

    def update(name, slabs):
        shape = wts[name].shape
        out[name] = [r.reshape(shape) for r in _adamw(slabs, wts[name], mom[name], vel[name], name=f"adamw_{name}")]
        return out[name][0]

    slabs, after = {}, grad_x
    for tag, (gnames, xc) in scattered.items():
        slabs.update(zip(gnames, _xchg_wait(xc, list(range(len(gnames))), after, name=f"scatter_wait_{tag}")))
        if tag == "ffn1":
            continue
        if tag == "ffn0":
            update("ffn_w_down", [slabs["ffn_w_down0"], slabs["ffn_w_down1"]])
            after = update("ffn_w_up", [slabs["ffn_w_up0"], slabs["ffn_w_up1"]])
        elif tag == "hgrn":
            update("hgrn_w_out", [slabs["hgrn_w_out"]])
            after = update("hgrn_w_in", [slabs["hgrn_w_in"]])
        elif tag == "attn_out":
            after = update("attn_w_out", [slabs["attn_w_out"]])
        else:
            after = update("attn_w_in", [slabs["attn_w_in"]])
    res = _adamw([slabs["small"]], stack_small(wts), stack_small(mom), stack_small(vel), name="adamw_small")
    for name in SMALL:
        out[name] = [r[0, SMALL_ROW[name]:SMALL_ROW[name] + 2] for r in res]
    loss = res[0][0, LOSS_ROW, 0]
    ng = hgrn_norm_g.shape[-1]
    ng_slabs = lax.dynamic_slice(slabs["small"], (0, NORM_G_ROW, me * ng), (N_DEV, 1, ng))
    out["hgrn_norm_g"] = [r[0] for r in _adamw([ng_slabs], hgrn_norm_g[None], m_hgrn_norm_g[None],
                                                v_hgrn_norm_g[None], name="adamw_norm_g")]
    order =("attn_w_in", "attn_w_out", "hgrn_w_in", "hgrn_w_out", "hgrn_norm_g", "lb_logits", "ln_mix_g", "ln_mix_b",
             "ln_ffn_g", "ln_ffn_b", "ffn_w_up", "ffn_w_down")
    return (loss, grad_x[None], *[out[nm][0] for nm in order], *[out[nm][1] for nm in order],
            *[out[nm][2] for nm in order], *[out[nm][3] for nm in order])
```

```python
import jax
import jax.numpy as jnp
from jax import lax
from jax.experimental import pallas as pl
from jax.experimental.pallas import tpu as pltpu

F32 = jnp.float32
BF16 = jnp.bfloat16

N_DEV = 8
LANES = 128
D_MODEL = 1024
ATTN_HEAD_DIM = 64
ATTN_HEADS = 16
ATTN_BLK = 128
DILATIONS = (1, 4, 16)
ROPE_THETA = 10000.0
HGRN_HEADS = 8
HGRN_CHUNK = 64
D_FF = 4096
LN_EPS = 1e-5
RMS_EPS = 1e-6
DEPTH = 2
ALPHA = (2 * DEPTH) ** 0.25
ADAM_LR, ADAM_B1, ADAM_B2, ADAM_EPS, ADAM_WD, ADAM_STEP = 0.001, 0.9, 0.999, 1e-08, 0.01, 10
VMEM_LIMIT = 48 * 1024 * 1024

_NT = (((1,), (1,)), ((), ()))
_TN = (((0,), (0,)), ((), ()))


def _dot(a, b):
    return jnp.dot(a, b, preferred_element_type=F32)


def _dot_nt(a, b):
    return lax.dot_general(a, b, _NT, preferred_element_type=F32)


def _dot_tn(a, b):
    return lax.dot_general(a, b, _TN, preferred_element_type=F32)


def _split3(x):
    p1 = x.astype(BF16)
    r1 = x - p1.astype(F32)
    p2 = r1.astype(BF16)
    p3 = (r1 - p2.astype(F32)).astype(BF16)
    return p1, p2, p3


def _exact_dot(sel, x):
    p1, p2, p3 = _split3(x)
    return _dot(sel, p1) + _dot(sel, p2) + _dot(sel, p3)


def _exact_dot_r(x, sel):
    p1, p2, p3 = _split3(x)
    return _dot(p1, sel) + _dot(p2, sel) + _dot(p3, sel)


def _params(sem=None):
    return pltpu.CompilerParams(dimension_semantics=sem, vmem_limit_bytes=VMEM_LIMIT)


def _mm(a, b, mode, *, name, m, n, k, tm=1024, tn=1024, tk=1024, out_dtypes=(F32,), epi=None, a_pre=None,
        tile_extras=(), row_extras=(), vec_extras=(), row_outs=(), vec_outs=0, a_split=1, b_split=1, out_split=1,
        b_col_off=0, b_k_off=0, out_col_off=0, out_cols=None, alias=None, epi_wants_j=False, deps=(), t_out=False):
    tm, tn, tk = min(tm, m), min(tn, n), min(tk, k)
    assert m % tm == 0 and n % tn == 0 and k % tk == 0, (name, m, n, k, tm, tn, tk)
    gm, gn, gk = m // tm, n // tn, k // tk
    if mode in ("nn", "nt"):
        if a_split > 1:
            kc = (k // a_split) // tk
            a_spec = pl.BlockSpec((None, tm, tk), lambda i, j, kk: (kk // kc, i, kk % kc))
        else:
            a_spec = pl.BlockSpec((tm, tk), lambda i, j, kk: (i, kk))
    else:
        a_spec = pl.BlockSpec((tk, tm), lambda i, j, kk: (kk, i))
    if mode in ("nn", "tn"):
        if b_split > 1:
            nc = (n // b_split) // tn
            b_spec = pl.BlockSpec((None, tk, tn), lambda i, j, kk: (j // nc, kk, j % nc))
        else:
            b_spec = pl.BlockSpec((tk, tn), lambda i, j, kk: (kk + b_k_off, j + b_col_off))
    else:
        b_spec = pl.BlockSpec((tn, tk), lambda i, j, kk: (j + b_col_off, kk + b_k_off))
    if out_split > 1:
        nco = (n // out_split) // tn
        o_spec = pl.BlockSpec((None, tm, tn), lambda i, j, kk: (j // nco, i, j % nco))
        o_shape = (out_split, m, n // out_split)
    else:
        o_spec = pl.BlockSpec((tm, tn), lambda i, j, kk: (i, j + out_col_off))
        o_shape = (m, out_cols if out_cols is not None else n)
    n_ex = len(tile_extras) + len(row_extras) + len(vec_extras)
    n_out = len(out_dtypes)
    n_plain = n_out + len(row_outs)
    assert not vec_outs or gn == 1
    if epi is None:
        def epi(acc):
            return (acc,)
    dot = {"nn": _dot, "nt": _dot_nt, "tn": _dot_tn}[mode]

    def body(*refs):
        a_ref, b_ref = refs[0], refs[1]
        ex = refs[2:2 + n_ex]
        outs = refs[2 + n_ex + (1 if alias is not None else 0) + len(deps):][:n_plain + vec_outs + (1 if t_out else 0)]
        ii = pl.program_id(0)
        jj = pl.program_id(1)

        def product():
            av = a_ref[...]
            if a_pre is not None:
                av = a_pre(av)
            return dot(av.astype(BF16), b_ref[...].astype(BF16))

        def finish(total):
            lead = (jj,) if epi_wants_j else ()
            res = epi(*lead, total, *[e[...] for e in ex])
            for o, r in zip(outs[:n_plain], res):
                o[...] = r.astype(o.dtype)
            for o, r in zip(outs[n_plain:n_plain + vec_outs], res[n_plain:]):
                @pl.when(ii == 0)
                def _(o=o, r=r):
                    o[...] = r

                @pl.when(ii > 0)
                def _(o=o, r=r):
                    o[...] += r
            if t_out:
                outs[-1][...] = res[0].astype(outs[-1].dtype).T

        if gk == 1:
            finish(product())
        else:
            acc = refs[-1]
            kk = pl.program_id(2)

            @pl.when(kk == 0)
            def _():
                acc[...] = product()

            @pl.when(kk > 0)
            def _():
                acc[...] += product()

            @pl.when(kk == gk - 1)
            def _():
                finish(acc[...])

    in_specs = [a_spec, b_spec] + [o_spec] * len(tile_extras)
    in_specs += [pl.BlockSpec((tm, r.shape[1]), lambda i, j, kk: (i, 0)) for r in row_extras]
    in_specs += [pl.BlockSpec((1, tn), lambda i, j, kk: (0, j))] * len(vec_extras)
    args = [a, b] + list(tile_extras) + list(row_extras) + list(vec_extras)
    io_alias = {}
    if alias is not None:
        in_specs.append(pl.BlockSpec(memory_space=pl.ANY))
        args.append(alias)
        io_alias = {len(args) - 1: 0}
    in_specs += [pl.BlockSpec(memory_space=pl.ANY)] * len(deps)
    args += list(deps)
    out_specs = [o_spec] * n_out
    out_shape = [jax.ShapeDtypeStruct(o_shape, dt) for dt in out_dtypes]
    for dt, w in row_outs:
        out_specs.append(pl.BlockSpec((tm, w), lambda i, j, kk: (i, 0)))
        out_shape.append(jax.ShapeDtypeStruct((m, w), dt))
    out_specs += [pl.BlockSpec((1, tn), lambda i, j, kk: (0, j))] * vec_outs
    out_shape += [jax.ShapeDtypeStruct((1, n), F32)] * vec_outs
    if t_out:
        assert out_split > 1
        out_specs.append(pl.BlockSpec((None, tn, tm), lambda i, j, kk: (j // nco, j % nco, i)))
        out_shape.append(jax.ShapeDtypeStruct((out_split, n // out_split, m), out_dtypes[0]))
    out = pl.pallas_call(
        body, name=name, grid=(gm, gn, gk), in_specs=in_specs, out_specs=out_specs, out_shape=out_shape,
        scratch_shapes=[pltpu.VMEM((tm, tn), F32)] if gk > 1 else [],
        input_output_aliases=io_alias,
        compiler_params=_params(("arbitrary" if vec_outs else "parallel", "parallel", "arbitrary")),
    )(*args)
    return out[0] if len(out) == 1 else out


def _rope_tables(seq, dil):
    pos = jnp.arange(seq, dtype=jnp.int32).reshape(seq // dil, dil).T.reshape(seq)
    half = ATTN_HEAD_DIM // 2
    inv = ROPE_THETA ** (-jnp.arange(half, dtype=F32) * (2.0 / ATTN_HEAD_DIM))
    ang = pos.astype(F32)[:, None] * inv[None, :]
    cos, sin = jnp.cos(ang), jnp.sin(ang)
    reps = LANES // ATTN_HEAD_DIM
    return (jnp.tile(jnp.concatenate([cos, cos], axis=1), (1, reps)),
            jnp.tile(jnp.concatenate([-sin, sin], axis=1), (1, reps)))


def _rotate(x, c, ss, sign=1.0):
    w = x.shape[-1]
    half = ATTN_HEAD_DIM // 2
    lane = lax.broadcasted_iota(jnp.int32, x.shape, 1)
    first = (lane % ATTN_HEAD_DIM) < half
    partner = jnp.where(first, pltpu.roll(x, w - half, 1), pltpu.roll(x, half, 1))
    reps = w // LANES
    if reps > 1:
        c = jnp.concatenate([c] * reps, axis=1)
        ss = jnp.concatenate([ss] * reps, axis=1)
    return x * c + sign * (partner * ss)


ROW_TILE = 512


def _ln_epi(acc, x, g, b):
    u = ALPHA * x + acc
    mu = jnp.mean(u, axis=-1, keepdims=True)
    uc = u - mu
    var = jnp.mean(uc * uc, axis=-1, keepdims=True)
    rstd = lax.rsqrt(var + LN_EPS)
    xh = uc * rstd
    out = xh * g + b
    return out, out, xh, rstd


def _ln_grad(dy, xh, rstd, g):
    dxh = dy * g
    m1 = jnp.mean(dxh, axis=-1, keepdims=True)
    m2 = jnp.mean(dxh * xh, axis=-1, keepdims=True)
    du = rstd * (dxh - m1 - xh * m2)
    return du, du, jnp.sum(dy * xh, axis=0, keepdims=True), jnp.sum(dy, axis=0, keepdims=True)


def _ln_grad_epi(acc, du_next, xh, rstd, g):
    return _ln_grad(acc + ALPHA * du_next, xh, rstd, g)


def _loss_ln_bwd(y, target, xhat, rstd, g):
    s, d = y.shape

    def body(y_ref, t_ref, xh_ref, r_ref, g_ref, du_ref, dub_ref, dg_ref, db_ref, sq_ref):
        @pl.when(pl.program_id(0) == 0)
        def _():
            dg_ref[...] = jnp.zeros_like(dg_ref)
            db_ref[...] = jnp.zeros_like(db_ref)
            sq_ref[...] = jnp.zeros_like(sq_ref)

        e = y_ref[...] - t_ref[...]
        du, _, dg, db = _ln_grad(e * (1.0 / d), xh_ref[...], r_ref[...], g_ref[...])
        du_ref[...] = du
        dub_ref[...] = du.astype(BF16)
        dg_ref[...] += dg
        db_ref[...] += db
        sq_ref[...] += jnp.sum(e * e, axis=0, keepdims=True)

    row = pl.BlockSpec((ROW_TILE, d), lambda i: (i, 0))
    vec = pl.BlockSpec((1, d), lambda i: (0, 0))
    return pl.pallas_call(
        body, name="loss_ln_bwd", grid=(s // ROW_TILE,),
        in_specs=[row, row, row, pl.BlockSpec((ROW_TILE, 1), lambda i: (i, 0)), vec],
        out_specs=[row, row, vec, vec, vec],
        out_shape=[jax.ShapeDtypeStruct((s, d), F32), jax.ShapeDtypeStruct((s, d), BF16)]
        + [jax.ShapeDtypeStruct((1, d), F32)] * 3,
        compiler_params=_params(("arbitrary",)),
    )(y, target, xhat, rstd, g)


POS_BLK = 2048


def _class_rows(r, dil):
    return pl.ds(r, POS_BLK // dil, stride=dil) if dil > 1 else pl.ds(0, POS_BLK)


def _class_view(a, dil):
    s, w = a.shape
    return a.reshape(dil, s // dil, w)


def _class_spec(dil, all_tiles=True):
    if all_tiles:
        return pl.BlockSpec((dil, POS_BLK // dil, LANES), lambda i, t: (0, i, t))
    return pl.BlockSpec((dil, POS_BLK // dil, LANES), lambda i, t: (0, i, 0))


def _pos_spec(all_tiles=True):
    if all_tiles:
        return pl.BlockSpec((POS_BLK, LANES), lambda i, t: (i, t))
    return pl.BlockSpec((POS_BLK, LANES), lambda i, t: (i, 0))


def _prep_x(x, deps=()):
    s, d = x.shape

    def body(x_ref, *refs):
        outs = refs[len(deps):]
        for dil, o_ref in zip(DILATIONS, outs):
            for r in range(dil):
                o_ref[r] = x_ref[_class_rows(r, dil), :].astype(BF16)

    outs = pl.pallas_call(
        body, name="prep_x", grid=(s // POS_BLK, d // LANES),
        in_specs=[_pos_spec()] + [pl.BlockSpec(memory_space=pl.ANY)] * len(deps),
        out_specs=[_class_spec(dil) for dil in DILATIONS],
        out_shape=[jax.ShapeDtypeStruct((dil, s // dil, d), BF16) for dil in DILATIONS],
        compiler_params=_params(("parallel", "parallel")),
    )(x, *deps)
    return [o.reshape(s, d) for o in outs]


def _head_expand_matrix():
    h = lax.broadcasted_iota(jnp.int32, (LANES, D_MODEL), 0)
    l = lax.broadcasted_iota(jnp.int32, (LANES, D_MODEL), 1)
    return (l // ATTN_HEAD_DIM == h).astype(BF16)


def _attn_fwd(qkv, dil, *, name):
    _, s, d = qkv.shape
    nq = s // ATTN_BLK
    per = nq // dil
    scale = ATTN_HEAD_DIM ** -0.5
    tiles = d // LANES

    def body(q_ref, kc_ref, kp_ref, vc_ref, vp_ref, o_ref, lse_ref):
        qb = pl.program_id(0)
        first = (qb % per) == 0
        qi = lax.broadcasted_iota(jnp.int32, (ATTN_BLK, 2 * ATTN_BLK), 0)
        kj = lax.broadcasted_iota(jnp.int32, (ATTN_BLK, 2 * ATTN_BLK), 1)
        dist = qi + ATTN_BLK - kj
        valid = (dist >= 0) & (dist <= ATTN_BLK) & ((kj >= ATTN_BLK) | jnp.logical_not(first))
        lane = lax.broadcasted_iota(jnp.int32, (ATTN_BLK, LANES), 1)
        lse_tile = jnp.zeros((ATTN_BLK, LANES), F32)
        zero = jnp.zeros((), BF16)
        in_head = [(lane // ATTN_HEAD_DIM) == hh for hh in range(2)]
        for t0 in range(0, tiles, ATTN_TILE_GROUP):
            group = range(t0, t0 + ATTN_TILE_GROUP)
            heads = [(t, hh) for t in group for hh in range(2)]
            cols = {t: pl.ds(t * LANES, LANES) for t in group}
            k2 = {t: jnp.concatenate([kp_ref[:, cols[t]], kc_ref[:, cols[t]]], axis=0) for t in group}
            v2 = {t: jnp.concatenate([vp_ref[:, cols[t]], vc_ref[:, cols[t]]], axis=0) for t in group}
            sc = {(t, hh): jnp.where(valid, _dot_nt(jnp.where(in_head[hh], q_ref[:, cols[t]], zero), k2[t]) * scale,
                                     -jnp.inf) for t, hh in heads}
            mx = {i: jnp.max(sc[i], axis=-1, keepdims=True) for i in heads}
            p = {i: jnp.exp(sc[i] - mx[i]) for i in heads}
            l = {i: jnp.sum(p[i], axis=-1, keepdims=True) for i in heads}
            oh = {i: _dot(p[i].astype(BF16), v2[i[0]]) / l[i] for i in heads}
            for t in group:
                o_ref[:, cols[t]] = jnp.where(in_head[0], oh[t, 0], oh[t, 1])
                for hh in range(2):
                    lse_tile = jnp.where(lane == 2 * t + hh, mx[t, hh] + jnp.log(l[t, hh]), lse_tile)
        lse_ref[...] = lse_tile

    def blk(piece, prev):
        if prev:
            return pl.BlockSpec((None, ATTN_BLK, d), lambda i: (piece, jnp.maximum(i - 1, 0), 0))
        return pl.BlockSpec((None, ATTN_BLK, d), lambda i: (piece, i, 0))

    return pl.pallas_call(
        body, name=name, grid=(nq,),
        in_specs=[blk(0, False), blk(1, False), blk(1, True), blk(2, False), blk(2, True)],
        out_specs=[pl.BlockSpec((ATTN_BLK, d), lambda i: (i, 0)), pl.BlockSpec((ATTN_BLK, LANES), lambda i: (i, 0))],
        out_shape=[jax.ShapeDtypeStruct((s, d), F32), jax.ShapeDtypeStruct((s, LANES), F32)],
        compiler_params=_params(("parallel",)),
    )(qkv, qkv, qkv, qkv, qkv)


def _attn_combine(os_, lses):
    s, d = os_[0].shape
    sel = _head_expand_matrix()

    def body(o0, o1, o2, l0, l1, l2, sel_ref, of_ref, ob_ref, lt_ref, o_pos, l_pos):
        for g, (dil, o_ref, l_ref) in enumerate(zip(DILATIONS, (o0, o1, o2), (l0, l1, l2))):
            for r in range(dil):
                o_pos[g, _class_rows(r, dil), :] = o_ref[r]
                l_pos[g, _class_rows(r, dil), :] = l_ref[r]
        la, lb_, lc = l_pos[0], l_pos[1], l_pos[2]
        mx = jnp.maximum(jnp.maximum(la, lb_), lc)
        es = (jnp.exp(la - mx), jnp.exp(lb_ - mx), jnp.exp(lc - mx))
        z = es[0] + es[1] + es[2]
        lt_ref[...] = mx + jnp.log(z)
        acc = jnp.zeros((POS_BLK, LANES), F32)
        for g in range(3):
            acc += _exact_dot_r(es[g] / z, sel_ref[...]) * o_pos[g]
        of_ref[...] = acc
        ob_ref[...] = acc.astype(BF16)

    return pl.pallas_call(
        body, name="attn_combine", grid=(s // POS_BLK, d // LANES),
        in_specs=[_class_spec(dil) for dil in DILATIONS] + [_class_spec(dil, False) for dil in DILATIONS]
        + [pl.BlockSpec((LANES, LANES), lambda i, t: (0, t))],
        out_specs=[_pos_spec(), _pos_spec(), _pos_spec(False)],
        out_shape=[jax.ShapeDtypeStruct((s, d), F32), jax.ShapeDtypeStruct((s, d), BF16),
                   jax.ShapeDtypeStruct((s, LANES), F32)],
        scratch_shapes=[pltpu.VMEM((3, POS_BLK, LANES), F32), pltpu.VMEM((3, POS_BLK, LANES), F32)],
        compiler_params=_params(("parallel", "arbitrary")),
    )(*[_class_view(o, dil) for o, dil in zip(os_, DILATIONS)],
      *[_class_view(l, dil) for l, dil in zip(lses, DILATIONS)], sel)


ATTN_TILE_GROUP = 2
AUX_PER_TILE = 12


def _aux_placement():
    h = lax.broadcasted_iota(jnp.int32, (6, LANES, LANES), 1)
    l = lax.broadcasted_iota(jnp.int32, (6, LANES, LANES), 2)
    j = lax.broadcasted_iota(jnp.int32, (6, LANES, LANES), 0)
    target = AUX_PER_TILE * (h // 2) + 3 * (h % 2) + jnp.where(j < 3, j, 3 + j)
    return ((l == target) & (h < ATTN_HEADS)).astype(BF16)


def _attn_bwd_prep(do, o, lse):
    s, d = do.shape
    tiles = d // LANES
    sel_t = _head_expand_matrix().T

    def body(do_ref, o_ref, l_ref, sel_ref, place_ref, *refs):
        outs, delta, aux = refs[:9], refs[9], refs[10]
        t = pl.program_id(1)
        part = _exact_dot_r(do_ref[...] * o_ref[...], sel_ref[...])

        @pl.when(t == 0)
        def _():
            delta[...] = part

        @pl.when(t > 0)
        def _():
            delta[...] += part

        for g, dil in enumerate(DILATIONS):
            for r in range(dil):
                blk = do_ref[_class_rows(r, dil), :].astype(BF16)
                outs[g][r] = blk
                outs[3 + g][r] = blk.T

        @pl.when(t == tiles - 1)
        def _():
            pieces = _split3(l_ref[...]) + _split3(delta[...])
            acc = _dot(pieces[0], place_ref[0])
            for j in range(1, 6):
                acc += _dot(pieces[j], place_ref[j])
            aux[...] = acc
            for g, dil in enumerate(DILATIONS):
                for r in range(dil):
                    outs[6 + g][r] = aux[_class_rows(r, dil), :].astype(BF16)

    outs = pl.pallas_call(
        body, name="attn_bwd_prep", grid=(s // POS_BLK, tiles),
        in_specs=[_pos_spec(), _pos_spec(), _pos_spec(False), pl.BlockSpec((LANES, LANES), lambda i, t: (t, 0)),
                  pl.BlockSpec((6, LANES, LANES), lambda i, t: (0, 0, 0))],
        out_specs=[_class_spec(dil) for dil in DILATIONS]
        + [pl.BlockSpec((dil, LANES, POS_BLK // dil), lambda i, t: (0, t, i)) for dil in DILATIONS]
        + [_class_spec(dil, False) for dil in DILATIONS],
        out_shape=[jax.ShapeDtypeStruct((dil, s // dil, d), BF16) for dil in DILATIONS]
        + [jax.ShapeDtypeStruct((dil, d, s // dil), BF16) for dil in DILATIONS]
        + [jax.ShapeDtypeStruct((dil, s // dil, LANES), BF16) for dil in DILATIONS],
        scratch_shapes=[pltpu.VMEM((POS_BLK, LANES), F32), pltpu.VMEM((POS_BLK, LANES), F32)],
        compiler_params=_params(("parallel", "arbitrary")),
    )(do, o, lse, sel_t, _aux_placement())
    return ([a.reshape(s, d) for a in outs[0:3]], list(outs[3:6]), [a.reshape(s, LANES) for a in outs[6:9]])


def _attn_bwd(qkv, qkv_t, do, do_t, aux, tables, dil, *, name):
    _, s, d = qkv.shape
    nq = s // ATTN_BLK
    per = nq // dil
    scale = ATTN_HEAD_DIM ** -0.5
    assert scale == 0.125
    tiles = d // LANES
    half = ATTN_HEAD_DIM

    def body(qd_ref, qo_ref, k_ref, qtd_ref, qto_ref, kt_ref, vt_ref, dod_ref, doo_ref, dotd_ref, doto_ref,
             auxd_ref, auxo_ref, c_ref, ss_ref, out_ref, carry):
        kb = pl.program_id(0)

        @pl.when(kb == 0)
        def _():
            carry[...] = jnp.zeros_like(carry)

        has_next = (kb % per) != (per - 1)
        qi = lax.broadcasted_iota(jnp.int32, (ATTN_BLK, 2 * ATTN_BLK), 0)
        kj = lax.broadcasted_iota(jnp.int32, (ATTN_BLK, 2 * ATTN_BLK), 1) % ATTN_BLK
        valid_d = kj <= qi
        valid_o = (kj >= qi) & has_next
        lane = lax.broadcasted_iota(jnp.int32, (ATTN_BLK, LANES), 1)
        row = lax.broadcasted_iota(jnp.int32, (LANES, ATTN_BLK), 0)
        side = lax.broadcasted_iota(jnp.int32, (LANES, 2 * ATTN_BLK), 0)
        first = lax.broadcasted_iota(jnp.int32, (LANES, 2 * ATTN_BLK), 1) < ATTN_BLK
        c, ss = c_ref[...], ss_ref[...]
        zero = jnp.zeros((), BF16)
        sides = ((qd_ref, qtd_ref, dod_ref, dotd_ref, auxd_ref[...], valid_d),
                 (qo_ref, qto_ref, doo_ref, doto_ref, auxo_ref[...], valid_o))
        both = (0, 1)

        def head_halves(x, index):
            axis = 0 if index is lane else 1
            return jnp.concatenate([jnp.where(index < half, x, zero), jnp.where(index >= half, x, zero)], axis=axis)

        for t0 in range(0, tiles, ATTN_TILE_GROUP):
            group = range(t0, t0 + ATTN_TILE_GROUP)
            cols = {t: pl.ds(t * LANES, LANES) for t in group}
            kk, kk_t, vv_t = {}, {}, {}
            for t in group:
                base = AUX_PER_TILE * t
                hit = lambda lo: ((first & (side >= base + lo) & (side < base + lo + 3))
                                  | (jnp.logical_not(first) & (side >= base + lo + 3) & (side < base + lo + 6)))
                kk[t] = head_halves(k_ref[:, cols[t]], lane)
                kk_t[t] = jnp.concatenate([head_halves(kt_ref[cols[t], :], row),
                                           jnp.where(hit(0), -1.0 / scale, 0.0).astype(BF16)], axis=0)
                vv_t[t] = jnp.concatenate([head_halves(vt_ref[cols[t], :], row),
                                           jnp.where(hit(6), -1.0, 0.0).astype(BF16)], axis=0)
            sc = {(t, w): _dot(jnp.concatenate([sides[w][0][:, cols[t]], sides[w][4]], axis=1), kk_t[t])
                  for t in group for w in both}
            dpd = {(t, w): _dot(jnp.concatenate([sides[w][2][:, cols[t]], sides[w][4]], axis=1), vv_t[t])
                   for t in group for w in both}
            p = {i: jnp.where(sides[i[1]][5], jnp.exp(sc[i] * scale), 0.0) for i in sc}
            ds = {i: (p[i] * dpd[i] * scale).astype(BF16) for i in sc}
            pb = {i: p[i].astype(BF16) for i in sc}
            dv_t = {t: sum(_dot(sides[w][3][cols[t], :], pb[t, w]) for w in both) for t in group}
            dk_t = {t: sum(_dot(sides[w][1][cols[t], :], ds[t, w]) for w in both) for t in group}
            dq = {i: _dot(ds[i], kk[i[0]]) for i in sc}
            for t in group:
                dq_now = carry[:, cols[t]] + dq[t, 0]
                carry[:, cols[t]] = dq[t, 1]
                dk = jnp.where(row < half, dk_t[t][:, :ATTN_BLK], dk_t[t][:, ATTN_BLK:]).T
                dv = jnp.where(row < half, dv_t[t][:, :ATTN_BLK], dv_t[t][:, ATTN_BLK:]).T
                out_ref[0, :, cols[t]] = _rotate(dq_now, c, ss, -1.0).astype(BF16)
                out_ref[1, :, cols[t]] = _rotate(dk, c, ss, -1.0).astype(BF16)
                out_ref[2, :, cols[t]] = dv.astype(BF16)

    def nxt(i):
        return jnp.minimum(i + 1, nq - 1)

    def piece(p, shift):
        if shift:
            return pl.BlockSpec((None, ATTN_BLK, d), lambda i: (p, nxt(i), 0))
        return pl.BlockSpec((None, ATTN_BLK, d), lambda i: (p, i, 0))

    def piece_t(p, shift):
        if shift:
            return pl.BlockSpec((None, d, ATTN_BLK), lambda i: (p, 0, nxt(i)))
        return pl.BlockSpec((None, d, ATTN_BLK), lambda i: (p, 0, i))

    def rows(width, shift):
        if shift:
            return pl.BlockSpec((ATTN_BLK, width), lambda i: (nxt(i), 0))
        return pl.BlockSpec((ATTN_BLK, width), lambda i: (i, 0))

    def do_t_spec(shift):
        if shift:
            return pl.BlockSpec((None, d, ATTN_BLK), lambda i: (nxt(i) // per, 0, nxt(i) % per))
        return pl.BlockSpec((None, d, ATTN_BLK), lambda i: (i // per, 0, i % per))

    return pl.pallas_call(
        body, name=name, grid=(nq,),
        in_specs=[piece(0, False), piece(0, True), piece(1, False),
                  piece_t(0, False), piece_t(0, True), piece_t(1, False), piece_t(2, False),
                  rows(d, False), rows(d, True), do_t_spec(False), do_t_spec(True),
                  rows(LANES, False), rows(LANES, True), rows(LANES, False), rows(LANES, False)],
        out_specs=pl.BlockSpec((3, ATTN_BLK, d), lambda i: (0, i, 0)),
        out_shape=jax.ShapeDtypeStruct((3, s, d), BF16),
        scratch_shapes=[pltpu.VMEM((ATTN_BLK, d), F32)],
        compiler_params=_params(("arbitrary",)),
    )(qkv, qkv, qkv, qkv_t, qkv_t, qkv_t, qkv_t, do, do, do_t, do_t, aux, aux, *tables)


def _dx_combine(du, parts):
    s, d = du.shape

    def body(du_ref, p0, p1, p2, out_ref):
        out_ref[...] = ALPHA * du_ref[...] + p0[0]
        for dil, p_ref in zip(DILATIONS[1:], (p1, p2)):
            for r in range(dil):
                out_ref[_class_rows(r, dil), :] += p_ref[r]

    return pl.pallas_call(
        body, name="dx_combine", grid=(s // POS_BLK, d // LANES),
        in_specs=[_pos_spec()] + [_class_spec(dil) for dil in DILATIONS], out_specs=_pos_spec(),
        out_shape=jax.ShapeDtypeStruct((s, d), F32),
        compiler_params=_params(("parallel", "parallel")),
    )(du, *[_class_view(p, dil) for p, dil in zip(parts, DILATIONS)])


HGRN_ROWS = 512


def _tri(lower):
    i = lax.broadcasted_iota(jnp.int32, (HGRN_CHUNK, HGRN_CHUNK), 0)
    j = lax.broadcasted_iota(jnp.int32, (HGRN_CHUNK, HGRN_CHUNK), 1)
    return (j <= i) if lower else (j >= i)


def _hgrn_gates(qr, z, lb):
    sq = jax.nn.sigmoid(qr)
    e = jnp.exp(-jnp.abs(z))
    big = 1.0 / (1.0 + e)
    small = e * big
    sg = jnp.where(z >= 0, big, small)
    sn = jnp.where(z >= 0, small, big)
    return sq, qr * sq, sg, sn, lb + (1.0 - lb) * sg, (1.0 - lb) * sn


def _hgrn_fwd(p2, lb, norm_g):
    _, s, d = p2.shape
    nblk = s // HGRN_ROWS
    cps = HGRN_ROWS // HGRN_CHUNK

    def body(p_ref, lb_ref, g_ref, o_ref, on_ref, st_ref, state):
        @pl.when(pl.program_id(1) == 0)
        def _():
            state[...] = jnp.zeros_like(state)

        chunks = [pl.ds(c * HGRN_CHUNK, HGRN_CHUNK) for c in range(cps)]
        ltri = _tri(True)
        lsel = ltri.astype(BF16)
        _, q, _, _, f, key = _hgrn_gates(p_ref[0], p_ref[1], lb_ref[...])
        lf = jnp.log(f)
        v = [p_ref[2, rows, :].astype(BF16) for rows in chunks]
        b = [_exact_dot(lsel, lf[c * HGRN_CHUNK:(c + 1) * HGRN_CHUNK]) for c in range(cps)]
        b_last = [bc[HGRN_CHUNK - 1:HGRN_CHUNK, :] for bc in b]
        q = [q[c * HGRN_CHUNK:(c + 1) * HGRN_CHUNK] for c in range(cps)]
        key = [key[c * HGRN_CHUNK:(c + 1) * HGRN_CHUNK] for c in range(cps)]
        qd = [(q[c] * jnp.exp(b[c])).astype(BF16) for c in range(cps)]
        kd = [(key[c] * jnp.exp(-b[c])).astype(BF16) for c in range(cps)]
        k2 = [(key[c] * jnp.exp(b_last[c] - b[c])).astype(BF16) for c in range(cps)]
        a = [jnp.where(ltri, _dot_nt(qd[c], kd[c]), 0.0).astype(BF16) for c in range(cps)]
        kv = [_dot_tn(v[c], k2[c]) for c in range(cps)]
        st, sts = state[...], []
        for c in range(cps):
            sts.append(st)
            st_ref[0, c] = st
            st = st * jnp.exp(b_last[c]) + kv[c]
        state[...] = st
        gv = g_ref[...]
        for c in range(cps):
            o = _dot(a[c], v[c]) + _dot_nt(qd[c], sts[c].astype(BF16))
            o_ref[chunks[c], :] = o
            r = lax.rsqrt(jnp.mean(o * o, axis=-1, keepdims=True) + RMS_EPS)
            on_ref[chunks[c], :] = (o * r * gv).astype(BF16)

    vec = pl.BlockSpec((1, LANES), lambda h, c: (0, h))
    col = pl.BlockSpec((HGRN_ROWS, LANES), lambda h, c: (c, h))
    return pl.pallas_call(
        body, name="hgrn_fwd", grid=(HGRN_HEADS, nblk),
        in_specs=[pl.BlockSpec((3, HGRN_ROWS, LANES), lambda h, c: (0, c, h)), vec, vec],
        out_specs=[col, col, pl.BlockSpec((1, cps, LANES, LANES), lambda h, c: (h, c, 0, 0))],
        out_shape=[jax.ShapeDtypeStruct((s, d), F32), jax.ShapeDtypeStruct((s, d), BF16),
                   jax.ShapeDtypeStruct((HGRN_HEADS, s // HGRN_CHUNK, LANES, LANES), F32)],
        scratch_shapes=[pltpu.VMEM((LANES, LANES), F32)],
        compiler_params=_params(("parallel", "arbitrary")),
    )(p2, lb, norm_g)


def _hgrn_bwd(p2, lb, norm_g, o_raw, states, dyn):
    _, s, d = p2.shape
    nblk = s // HGRN_ROWS
    cps = HGRN_ROWS // HGRN_CHUNK

    def body(p_ref, lb_ref, g_ref, o_ref, st_ref, dy_ref, dp_ref, dg_ref, dlb_ref, dstate):
        @pl.when(pl.program_id(1) == 0)
        def _():
            dstate[...] = jnp.zeros_like(dstate)
            dg_ref[...] = jnp.zeros_like(dg_ref)
            dlb_ref[...] = jnp.zeros_like(dlb_ref)

        n = cps
        cut = lambda t: [t[c * HGRN_CHUNK:(c + 1) * HGRN_CHUNK] for c in range(n)]
        chunks = [pl.ds(c * HGRN_CHUNK, HGRN_CHUNK) for c in range(n)]
        lbv = lb_ref[...]
        ltri = _tri(True)
        lsel = ltri.astype(BF16)
        usel = _tri(False).astype(BF16)
        last_row = lax.broadcasted_iota(jnp.int32, (HGRN_CHUNK, LANES), 0) == HGRN_CHUNK - 1
        qr, z = p_ref[0], p_ref[1]
        sq, q, sg, sn, f, key = _hgrn_gates(qr, z, lbv)
        lf = jnp.log(f)
        v = [p_ref[2, rows, :].astype(BF16) for rows in chunks]
        o, dyv = o_ref[...], dy_ref[...]
        r = lax.rsqrt(jnp.mean(o * o, axis=-1, keepdims=True) + RMS_EPS)
        oh = o * r
        doh = dyv * g_ref[...]
        do = cut((r * (doh - oh * jnp.mean(doh * oh, axis=-1, keepdims=True))).astype(BF16))
        dg_ref[...] += jnp.sum(dyv * oh, axis=0, keepdims=True)
        b = [_exact_dot(lsel, lfc) for lfc in cut(lf)]
        b_last = [bc[HGRN_CHUNK - 1:HGRN_CHUNK, :] for bc in b]
        q, key = cut(q), cut(key)
        eb = [jnp.exp(bc) for bc in b]
        enb = [jnp.exp(-bc) for bc in b]
        e2 = [jnp.exp(b_last[c] - b[c]) for c in range(n)]
        dec = [jnp.exp(bl) for bl in b_last]
        qd_f = [q[c] * eb[c] for c in range(n)]
        kd_f = [key[c] * enb[c] for c in range(n)]
        k2_f = [key[c] * e2[c] for c in range(n)]
        qd, kd, k2 = ([t.astype(BF16) for t in ts] for ts in (qd_f, kd_f, k2_f))
        a = [jnp.where(ltri, _dot_nt(qd[c], kd[c]), 0.0).astype(BF16) for c in range(n)]
        da = [jnp.where(ltri, _dot_nt(do[c], v[c]), 0.0).astype(BF16) for c in range(n)]
        grow = [_dot_tn(do[c], qd[c]) for c in range(n)]
        dst, dsts = dstate[...], [None] * n
        for c in reversed(range(n)):
            dsts[c] = dst
            dst = dst * dec[c] + grow[c]
        dstate[...] = dst
        st = [st_ref[0, c] for c in range(n)]
        dstb = [t.astype(BF16) for t in dsts]
        dv = [_dot_tn(a[c], do[c]) + _dot_nt(k2[c], dstb[c]) for c in range(n)]
        dqd = [_dot(da[c], kd[c]) + _dot(do[c], st[c].astype(BF16)) for c in range(n)]
        dkd = [_dot_tn(da[c], qd[c]) for c in range(n)]
        dk2 = [_dot(v[c], dstb[c]) for c in range(n)]
        db = []
        for c in range(n):
            ddec = jnp.sum(dsts[c] * st[c], axis=0, keepdims=True)
            db_last = jnp.sum(dk2[c] * k2_f[c], axis=0, keepdims=True) + ddec * dec[c]
            db.append(dqd[c] * qd_f[c] - dkd[c] * kd_f[c] - dk2[c] * k2_f[c] + jnp.where(last_row, db_last, 0.0))
        dlf = [_exact_dot(usel, dbc) for dbc in db]
        f, sg, sn, sq, qr = cut(f), cut(sg), cut(sn), cut(sq), cut(qr)
        dlb_acc = jnp.zeros((1, LANES), F32)
        for c in range(n):
            dkey = dkd[c] * enb[c] + dk2[c] * e2[c]
            common = dlf[c] / f[c] - dkey
            dlb_acc += jnp.sum(common * sn[c], axis=0, keepdims=True)
            dp_ref[0, chunks[c], :] = (dqd[c] * eb[c] * (sq[c] * (1.0 + qr[c] * (1.0 - sq[c])))).astype(BF16)
            dp_ref[1, chunks[c], :] = (common * ((1.0 - lbv) * sg[c] * sn[c])).astype(BF16)
            dp_ref[2, chunks[c], :] = dv[c].astype(BF16)
        dlb_ref[...] += dlb_acc

    def rev(c):
        return nblk - 1 - c

    vec = pl.BlockSpec((1, LANES), lambda h, c: (0, h))
    col = pl.BlockSpec((HGRN_ROWS, LANES), lambda h, c: (rev(c), h))
    p3 = pl.BlockSpec((3, HGRN_ROWS, LANES), lambda h, c: (0, rev(c), h))
    return pl.pallas_call(
        body, name="hgrn_bwd", grid=(HGRN_HEADS, nblk),
        in_specs=[p3, vec, vec, col, pl.BlockSpec((1, cps, LANES, LANES), lambda h, c: (h, rev(c), 0, 0)), col],
        out_specs=[p3, vec, vec],
        out_shape=[jax.ShapeDtypeStruct((3, s, d), BF16), jax.ShapeDtypeStruct((1, d), F32),
                   jax.ShapeDtypeStruct((1, d), F32)],
        scratch_shapes=[pltpu.VMEM((LANES, LANES), F32)],
        compiler_params=_params(("parallel", "arbitrary")),
    )(p2, lb, norm_g, o_raw, states, dyn)


SMALL_ROWS = 16
FUSED_TM = 512


def _relu2(h):
    r = jnp.maximum(h.astype(F32), 0.0)
    return r * r


def _dact_epi(acc, h):
    return (acc * (2.0 * jnp.maximum(h.astype(F32), 0.0)),)


def _rope_epi(j, acc, c, ss):
    return (jnp.where(j >= 2, acc, _rotate(acc, c, ss)),)


def _local_step(x, target, get_w, small, on_grads, deps=()):
    s, d = x.shape
    lb = small["lb"]
    gq = 3 * d
    gt = gq // 1024

    def ln_of(a, w, x_in, which, layer, *, name, k, a_pre=None, deps=()):
        g, b = small[f"ln_{which}_g"][layer:layer + 1], small[f"ln_{which}_b"][layer:layer + 1]
        return _mm(a, w, "nn", name=name, m=s, n=d, k=k, tm=FUSED_TM, a_pre=a_pre, out_dtypes=(F32, BF16, F32),
                   row_outs=((F32, 1),), epi=_ln_epi, tile_extras=(x_in,), vec_extras=(g, b), deps=deps)

    def ln_grad_of(a, w, du_next, xh, rstd, which, layer, *, name, k, a_split=1, deps=()):
        return _mm(a, w, "nt", name=name, m=s, n=d, k=k, tm=FUSED_TM, a_split=a_split, out_dtypes=(F32, BF16),
                   vec_outs=2, epi=_ln_grad_epi, tile_extras=(du_next, xh), row_extras=(rstd,),
                   vec_extras=(small[f"ln_{which}_g"][layer:layer + 1],), deps=deps)

    def ffn_fwd(xb, x_in, w_up, w_down, tag):
        h = _mm(xb, w_up, "nn", name=f"ffn_up_{tag}", m=s, n=D_FF, k=d, out_dtypes=(BF16,))
        return (h,) + tuple(ln_of(h, w_down, x_in, "ffn", tag, name=f"ffn_down_{tag}", k=D_FF, a_pre=_relu2))

    def ffn_bwd(du, dub, xb, h, w_up, w_down, xh, rstd, tag):
        dh = _mm(dub, w_down, "nt", name=f"ffn_dact_{tag}", m=s, n=D_FF, k=d, out_dtypes=(BF16,), epi=_dact_epi,
                 tile_extras=(h,))
        g_down = _mm(h, dub, "tn", name=f"ffn_gdown_{tag}", m=D_FF, n=d, k=s, a_pre=_relu2, out_dtypes=(BF16,))
        g_up = _mm(xb, dh, "tn", name=f"ffn_gup_{tag}", m=d, n=D_FF, k=s, out_dtypes=(BF16,))
        after = on_grads(f"ffn{tag}", {f"ffn_w_down{tag}": g_down, f"ffn_w_up{tag}": g_up})
        return ln_grad_of(dh, w_up, du, xh, rstd, "mix", tag, name=f"ffn_dx_{tag}", k=D_FF, deps=after)

    xs = _prep_x(x, deps)
    w_ain, after = get_w("attn_w_in", None)
    tabs, qkvs, qkv_ts, o_parts, lse_parts = [], [], [], [], []
    for g, dil in enumerate(DILATIONS):
        tabs.append(_rope_tables(s, dil))
        qkv, qkv_t = _mm(xs[g], w_ain, "nn", name=f"attn_in_{g}", m=s, n=gq, k=d, b_col_off=g * gt, out_split=3,
                         out_dtypes=(BF16,), epi=_rope_epi, epi_wants_j=True, row_extras=tabs[g], t_out=True,
                         deps=after if g == 0 else ())
        qkvs.append(qkv)
        qkv_ts.append(qkv_t)
        o_g, lse_g = _attn_fwd(qkv, dil, name=f"attn_fwd_{g}")
        o_parts.append(o_g)
        lse_parts.append(lse_g)
    o_f, o_b, lse_t = _attn_combine(o_parts, lse_parts)
    w_aout, after = get_w("attn_w_out", o_b)
    x1, x1b, xh1, r1 = ln_of(o_b, w_aout, x, "mix", 0, name="attn_out", k=d, deps=after)
    w_up0, w_down0 = get_w("ffn_w_up0", o_b)[0], get_w("ffn_w_down0", o_b)[0]
    h0, x2, x2b, xh2, r2 = ffn_fwd(x1b, x1, w_up0, w_down0, 0)
    w_hin, w_hout, norm_g = get_w("hgrn_w_in", x2b)[0], get_w("hgrn_w_out", x2b)[0], get_w("hgrn_norm_g", x2b)[0]
    p2 = _mm(x2b, w_hin, "nn", name="hgrn_in", m=s, n=3 * d, k=d, out_split=3)
    o_raw, o_n, states = _hgrn_fwd(p2, lb, norm_g)
    w_up1, w_down1 = get_w("ffn_w_up1", o_n)[0], get_w("ffn_w_down1", o_n)[0]
    x3, x3b, xh3, r3 = ln_of(o_n, w_hout, x2, "mix", 1, name="hgrn_out", k=d)
    h1, x4, _, xh4, r4 = ffn_fwd(x3b, x3, w_up1, w_down1, 1)
    du, dub, dg_ffn1, db_ffn1, sq = _loss_ln_bwd(x4, target, xh4, r4, small["ln_ffn_g"][1:2])
    du, dub, dg_mix1, db_mix1 = ffn_bwd(du, dub, x3b, h1, w_up1, w_down1, xh3, r3, 1)
    dyn = _mm(dub, w_hout, "nt", name="hgrn_dout", m=s, n=d, k=d)
    g_hout = _mm(o_n, dub, "tn", name="hgrn_gout", m=d, n=d, k=s, out_dtypes=(BF16,))
    dp2, d_norm_g, d_lb = _hgrn_bwd(p2, lb, norm_g, o_raw, states, dyn)
    g_hin = _mm(x2b, dp2, "tn", name="hgrn_gin", m=d, n=3 * d, k=s, b_split=3, out_dtypes=(BF16,))
    after = on_grads("hgrn", {"hgrn_w_out": g_hout, "hgrn_w_in": g_hin})
    du, dub, dg_ffn0, db_ffn0 = ln_grad_of(dp2, w_hin, du, xh2, r2, "ffn", 0, name="hgrn_dx", k=3 * d, a_split=3,
                                           deps=after)
    du, dub, dg_mix0, db_mix0 = ffn_bwd(du, dub, x1b, h0, w_up0, w_down0, xh1, r1, 0)
    g_aout = _mm(o_b, dub, "tn", name="attn_gout", m=d, n=d, k=s, out_dtypes=(BF16,))
    sm1 = jax.nn.softmax(small["lb_logits"], axis=0)
    d_l1 = d_lb * (sm1[0:1] * sm1[1:2])
    zeros = jnp.zeros((SMALL_ROWS - 12, d), F32)
    loss_row = jnp.broadcast_to(0.5 * jnp.sum(sq) / d, (1, d))
    small_grads = jnp.concatenate([dg_mix0, dg_mix1, db_mix0, db_mix1, dg_ffn0, dg_ffn1, db_ffn0, db_ffn1,
                                   -d_l1, d_l1, d_norm_g, loss_row, zeros], axis=0)
    after = on_grads("attn_out", {"attn_w_out": g_aout, "small": small_grads})
    do = _mm(dub, w_aout, "nt", name="attn_dout", m=s, n=d, k=d, deps=after)
    do_parts, do_ts, aux_parts = _attn_bwd_prep(do, o_f, lse_t)
    g_ain, dqkvs = None, []
    for g, dil in enumerate(DILATIONS):
        dqkvs.append(_attn_bwd(qkvs[g], qkv_ts[g], do_parts[g], do_ts[g], aux_parts[g], tabs[g], dil,
                               name=f"attn_bwd_{g}"))
        g_ain = _mm(xs[g], dqkvs[g], "tn", name=f"attn_gin_{g}", m=d, n=gq, k=s, b_split=3, out_dtypes=(BF16,),
                    out_col_off=g * gt, out_cols=3 * gq, alias=g_ain)
    after = on_grads("attn_in", {"attn_w_in": g_ain})
    dx_parts = [_mm(dqkvs[g], w_ain, "nt", name=f"attn_dx_{g}", m=s, n=d, k=gq, a_split=3, b_k_off=g * gt,
                    deps=after if g == 0 else ())
                for g in range(len(DILATIONS))]
    return _dx_combine(du, dx_parts)


def _mesh_place():
    x, y, c = lax.axis_index("x"), lax.axis_index("y"), lax.axis_index("c")
    return x, y, c, 4 * x + 2 * y + c


def _peer(x, y, c, k):
    px = 1 - x if (k >> 2) & 1 else x
    py = 1 - y if (k >> 1) & 1 else y
    pc = 1 - c if k & 1 else c
    return (px, py, pc), 4 * px + 2 * py + pc


def _window(ref, axis, size, idx):
    if axis is None:
        return ref
    sl = [slice(None)] * len(ref.shape)
    sl[axis] = pl.ds(idx * size, size)
    return ref.at[tuple(sl)]


_HBM = pl.BlockSpec(memory_space=pltpu.HBM)
_SEM = pl.BlockSpec(memory_space=pltpu.SEMAPHORE)
_EFFECT = pltpu.SideEffectType.DATAFLOW_SIDE_EFFECTING


def _xchg_ends(src_ref, land_ref, axis, gather, me, other):
    if gather:
        size = src_ref.shape[axis]
        return src_ref, _window(land_ref, axis, size, me), _window(land_ref, axis, size, other)
    size = None if axis is None else src_ref.shape[axis] // N_DEV
    return _window(src_ref, axis, size, other), land_ref.at[me], land_ref.at[other]


def _xchg_start(srcs, lands, axes, *, gather, name, deps=()):
    n = len(srcs)
    nd = len(deps)

    def body(*refs):
        src_refs, land_refs = refs[:n], refs[n:2 * n]
        send, recv = refs[2 * n + nd:3 * n + nd], refs[3 * n + nd:4 * n + nd]
        token = refs[-1]
        x, y, c, me = _mesh_place()
        for k in range(1, N_DEV):
            peer, pidx = _peer(x, y, c, k)
            for i in range(n):
                src, dst, _ = _xchg_ends(src_refs[i], land_refs[i], axes[i], gather, me, pidx)
                pltpu.make_async_remote_copy(
                    src_ref=src, dst_ref=dst, send_sem=send[i].at[k - 1], recv_sem=recv[i].at[k - 1],
                    device_id=peer, device_id_type=pl.DeviceIdType.MESH).start()
        for i in range(n):
            src, dst, _ = _xchg_ends(src_refs[i], land_refs[i], axes[i], gather, me, me)
            pltpu.make_async_copy(src, dst, send[i].at[N_DEV - 1]).start()
        token[...] = jnp.zeros_like(token)

    bufs = list(srcs) + list(lands)
    outs = pl.pallas_call(
        body, name=name,
        out_shape=[pltpu.SemaphoreType.DMA((N_DEV,))] * (2 * n) + [pltpu.HBM(b.shape, b.dtype) for b in bufs]
        + [jax.ShapeDtypeStruct((8, LANES), F32)],
        in_specs=[_HBM] * (2 * n) + [pl.BlockSpec(memory_space=pl.ANY)] * nd,
        out_specs=[_SEM] * (2 * n) + [_HBM] * (2 * n) + [pl.BlockSpec(memory_space=pltpu.VMEM)],
        input_output_aliases={i: 2 * n + i for i in range(2 * n)},
        compiler_params=pltpu.CompilerParams(has_side_effects=_EFFECT),
    )(*[pltpu.with_memory_space_constraint(b, pltpu.HBM) for b in bufs], *deps)
    return dict(send=outs[:n], recv=outs[n:2 * n], srcs=outs[2 * n:3 * n], lands=outs[3 * n:4 * n], token=outs[-1],
                axes=list(axes), gather=gather)


def _xchg_wait(xc, items, after, *, name):
    m = len(items)
    gather = xc["gather"]
    axes = [xc["axes"][i] for i in items]

    def body(*refs):
        src_refs, land_refs = refs[:m], refs[m:2 * m]
        send, recv = refs[2 * m:3 * m], refs[3 * m:4 * m]
        x, y, c, me = _mesh_place()
        for k in range(1, N_DEV):
            peer, pidx = _peer(x, y, c, k)
            for j in range(m):
                src, dst, got = _xchg_ends(src_refs[j], land_refs[j], axes[j], gather, me, pidx)
                pltpu.make_async_remote_copy(
                    src_ref=src, dst_ref=dst, send_sem=send[j].at[k - 1], recv_sem=recv[j].at[k - 1],
                    device_id=peer, device_id_type=pl.DeviceIdType.MESH).wait_send()
                pltpu.make_async_remote_copy(
                    src_ref=src, dst_ref=got, send_sem=send[j].at[k - 1], recv_sem=recv[j].at[k - 1],
                    device_id=peer, device_id_type=pl.DeviceIdType.MESH).wait_recv()
        for j in range(m):
            src, dst, _ = _xchg_ends(src_refs[j], land_refs[j], axes[j], gather, me, me)
            pltpu.make_async_copy(src, dst, send[j].at[N_DEV - 1]).wait()

    bufs = [xc["srcs"][i] for i in items] + [xc["lands"][i] for i in items]
    sems = [xc["send"][i] for i in items] + [xc["recv"][i] for i in items]
    outs = pl.pallas_call(
        body, name=name, out_shape=[pltpu.HBM(b.shape, b.dtype) for b in bufs],
        in_specs=[_HBM] * (2 * m) + [_SEM] * (2 * m) + [pl.BlockSpec(memory_space=pl.ANY)],
        out_specs=[_HBM] * (2 * m), input_output_aliases={j: j for j in range(2 * m)},
        compiler_params=pltpu.CompilerParams(has_side_effects=_EFFECT),
    )(*bufs, *sems, after)
    return outs[m:]


def _cast_bf16(a, *, name):
    r, c = a.shape
    tr = min(r, 512)

    def body(a_ref, o_ref):
        o_ref[...] = a_ref[...].astype(BF16)

    spec = pl.BlockSpec((tr, c), lambda i: (i, 0))
    return pl.pallas_call(body, name=name, grid=(r // tr,), in_specs=[spec], out_specs=spec,
                          out_shape=jax.ShapeDtypeStruct((r, c), BF16), compiler_params=_params(("parallel",)))(a)


def _adamw(slabs, w, m, v, *, name):
    layers, r, c = w.shape
    tr = min(r, 256)

    def body(*refs):
        s_refs = refs[:layers]
        w_ref, m_ref, v_ref, g_ref, d_ref, mo_ref, vo_ref = refs[layers:]
        for l in range(layers):
            g = s_refs[l][0].astype(F32)
            for i in range(1, N_DEV):
                g = g + s_refs[l][i].astype(F32)
            m2 = ADAM_B1 * m_ref[l] + (1.0 - ADAM_B1) * g
            v2 = ADAM_B2 * v_ref[l] + (1.0 - ADAM_B2) * (g * g)
            m_hat = m2 / (1.0 - ADAM_B1 ** ADAM_STEP)
            v_hat = v2 / (1.0 - ADAM_B2 ** ADAM_STEP)
            g_ref[l] = g
            d_ref[l] = -ADAM_LR * (m_hat / (jnp.sqrt(v_hat) + ADAM_EPS) + ADAM_WD * w_ref[l])
            mo_ref[l] = m2
            vo_ref[l] = v2

    spec = pl.BlockSpec((layers, tr, c), lambda i: (0, i, 0))
    return pl.pallas_call(
        body, name=name, grid=(r // tr,),
        in_specs=[pl.BlockSpec((N_DEV, tr, c), lambda i: (0, i, 0))] * layers + [spec, spec, spec],
        out_specs=[spec] * 4, out_shape=[jax.ShapeDtypeStruct((layers, r, c), F32)] * 4,
        compiler_params=_params(("parallel",)),
    )(*slabs, w, m, v)


GATHER_AXIS = {"attn_w_in": 1, "attn_w_out": 0, "ffn_w_up0": 1, "ffn_w_down0": 0, "hgrn_w_in": 1, "hgrn_w_out": 0,
               "hgrn_norm_g": 1, "ffn_w_up1": 1, "ffn_w_down1": 0}
GATHER_STAGES = (("attn_w_in",), ("attn_w_out", "ffn_w_up0", "ffn_w_down0", "hgrn_w_in", "hgrn_w_out", "hgrn_norm_g"),
                 ("ffn_w_up1", "ffn_w_down1"))
GATHER_WAITS = ((("attn_w_in",), 0, 1), (("attn_w_out", "ffn_w_up0", "ffn_w_down0"), 1, 2),
                (("hgrn_w_in", "hgrn_w_out", "hgrn_norm_g"), 1, None), (("ffn_w_up1", "ffn_w_down1"), 2, None))
SCATTER_AXIS = dict(GATHER_AXIS, small=None)
BIG = ("attn_w_in", "attn_w_out", "hgrn_w_in", "hgrn_w_out", "ffn_w_up", "ffn_w_down")
SMALL = ("lb_logits", "ln_mix_g", "ln_mix_b", "ln_ffn_g", "ln_ffn_b")
SMALL_ROW = {"ln_mix_g": 0, "ln_mix_b": 2, "ln_ffn_g": 4, "ln_ffn_b": 6, "lb_logits": 8}
NORM_G_ROW = 10
LOSS_ROW = 11


def kernel(x, attn_w_in, attn_w_out, hgrn_w_in, hgrn_w_out, hgrn_norm_g, lb_logits, ln_mix_g, ln_mix_b, ln_ffn_g, ln_ffn_b, ffn_w_up, ffn_w_down, loss_target, m_attn_w_in, m_attn_w_out, m_hgrn_w_in, m_hgrn_w_out, m_hgrn_norm_g, m_lb_logits, m_ln_mix_g, m_ln_mix_b, m_ln_ffn_g, m_ln_ffn_b, m_ffn_w_up, m_ffn_w_down, v_attn_w_in, v_attn_w_out, v_hgrn_w_in, v_hgrn_w_out, v_hgrn_norm_g, v_lb_logits, v_ln_mix_g, v_ln_mix_b, v_ln_ffn_g, v_ln_ffn_b, v_ffn_w_up, v_ffn_w_down):
    wts = dict(attn_w_in=attn_w_in, attn_w_out=attn_w_out, hgrn_w_in=hgrn_w_in, hgrn_w_out=hgrn_w_out,
               hgrn_norm_g=hgrn_norm_g, lb_logits=lb_logits, ln_mix_g=ln_mix_g, ln_mix_b=ln_mix_b, ln_ffn_g=ln_ffn_g,
               ln_ffn_b=ln_ffn_b, ffn_w_up=ffn_w_up, ffn_w_down=ffn_w_down)
    mom = dict(attn_w_in=m_attn_w_in, attn_w_out=m_attn_w_out, hgrn_w_in=m_hgrn_w_in, hgrn_w_out=m_hgrn_w_out,
               hgrn_norm_g=m_hgrn_norm_g, lb_logits=m_lb_logits, ln_mix_g=m_ln_mix_g, ln_mix_b=m_ln_mix_b,
               ln_ffn_g=m_ln_ffn_g, ln_ffn_b=m_ln_ffn_b, ffn_w_up=m_ffn_w_up, ffn_w_down=m_ffn_w_down)
    vel = dict(attn_w_in=v_attn_w_in, attn_w_out=v_attn_w_out, hgrn_w_in=v_hgrn_w_in, hgrn_w_out=v_hgrn_w_out,
               hgrn_norm_g=v_hgrn_norm_g, lb_logits=v_lb_logits, ln_mix_g=v_ln_mix_g, ln_mix_b=v_ln_mix_b,
               ln_ffn_g=v_ln_ffn_g, ln_ffn_b=v_ln_ffn_b, ffn_w_up=v_ffn_w_up, ffn_w_down=v_ffn_w_down)
    me = 4 * lax.axis_index("x") + 2 * lax.axis_index("y") + lax.axis_index("c")

    src = {"attn_w_in": attn_w_in[0], "attn_w_out": attn_w_out[0], "hgrn_w_in": hgrn_w_in[0], "hgrn_w_out": hgrn_w_out[0],
           "ffn_w_up0": ffn_w_up[0], "ffn_w_down0": ffn_w_down[0], "ffn_w_up1": ffn_w_up[1], "ffn_w_down1": ffn_w_down[1]}
    gathers, got = {}, {}

    def start_gather(stage, deps):
        shards, lands = [], []
        for nm in GATHER_STAGES[stage]:
            sh = hgrn_norm_g if nm == "hgrn_norm_g" else _cast_bf16(src[nm], name=f"cast_{nm}")
            ax = GATHER_AXIS[nm]
            shape = list(sh.shape)
            shape[ax] *= N_DEV
            shards.append(sh)
            lands.append(lax.empty(tuple(shape), sh.dtype))
        gathers[stage] = _xchg_start(shards, lands, [GATHER_AXIS[nm] for nm in GATHER_STAGES[stage]], gather=True,
                                     name=f"gather_start_{stage}", deps=deps)
        return [gathers[stage]["token"]]

    def get_w(name, after):
        deps = []
        if name not in got:
            group, stage, then = [w for w in GATHER_WAITS if name in w[0]][0]
            xc = gathers[stage]
            res = _xchg_wait(xc, [GATHER_STAGES[stage].index(nm) for nm in group], xc["token"] if after is None else after,
                             name=f"gather_wait_{group[0]}")
            got.update(zip(group, res))
            if then is not None:
                deps = start_gather(then, [res[0]])
        return got[name], deps

    first = start_gather(0, [])

    scattered = {}

    def on_grads(tag, grads):
        gnames = list(grads)
        axes = [SCATTER_AXIS[nm] for nm in gnames]
        stacks = []
        for nm, ax in zip(gnames, axes):
            shape = list(grads[nm].shape)
            if ax is not None:
                shape[ax] //= N_DEV
            stacks.append(lax.empty((N_DEV, *shape), grads[nm].dtype))
        scattered[tag] = (gnames, _xchg_start([grads[nm] for nm in gnames], stacks, axes, gather=False,
                                              name=f"scatter_start_{tag}"))
        return [scattered[tag][1]["token"]]

    sm = jax.nn.softmax(lb_logits, axis=0)
    csum = jnp.cumsum(sm, axis=0)
    small = dict(lb=(csum - csum[0:1])[1:2], lb_logits=lb_logits, ln_mix_g=ln_mix_g, ln_mix_b=ln_mix_b,
                 ln_ffn_g=ln_ffn_g, ln_ffn_b=ln_ffn_b)
    grad_x = _local_step(x[0], loss_target[0], get_w, small, on_grads, deps=first)
    out = {}

    def stack_small(src_):
        rows = [None] * SMALL_ROWS
        for name in SMALL:
            rows[SMALL_ROW[name]], rows[SMALL_ROW[name] + 1] = src_[name][0:1], src_[name][1:2]
        zero = jnp.zeros((1, x.shape[-1]), F32)
        return jnp.concatenate([zero if r is None else r for r in rows], axis=0)[None]

    def update(name, slabs):
        shape = wts[name].shape
        out[name] = [r.reshape(shape) for r in _adamw(slabs, wts[name], mom[name], vel[name], name=f"adamw_{name}")]
        return out[name][0]

    slabs, after = {}, grad_x
    for tag, (gnames, xc) in scattered.items():
        slabs.update(zip(gnames, _xchg_wait(xc, list(range(len(gnames))), after, name=f"scatter_wait_{tag}")))
        if tag == "ffn1":
            continue
        if tag == "ffn0":
            update("ffn_w_down", [slabs["ffn_w_down0"], slabs["ffn_w_down1"]])
            after = update("ffn_w_up", [slabs["ffn_w_up0"], slabs["ffn_w_up1"]])
        elif tag == "hgrn":
            update("hgrn_w_out", [slabs["hgrn_w_out"]])
            after = update("hgrn_w_in", [slabs["hgrn_w_in"]])
        elif tag == "attn_out":
            after = update("attn_w_out", [slabs["attn_w_out"]])
        else:
            after = update("attn_w_in", [slabs["attn_w_in"]])
    res = _adamw([slabs["small"]], stack_small(wts), stack_small(mom), stack_small(vel), name="adamw_small")
    for name in SMALL:
        out[name] = [r[0, SMALL_ROW[name]:SMALL_ROW[name] + 2] for r in res]
    loss = res[0][0, LOSS_ROW, 0]
    ng = hgrn_norm_g.shape[-1]
    ng_slabs = lax.dynamic_slice(slabs["small"], (0, NORM_G_ROW, me * ng), (N_DEV, 1, ng))
    out["hgrn_norm_g"] = [r[0] for r in _adamw([ng_slabs], hgrn_norm_g[None], m_hgrn_norm_g[None],
                                                v_hgrn_norm_g[None], name="adamw_norm_g")]
    order =("attn_w_in", "attn_w_out", "hgrn_w_in", "hgrn_w_out", "hgrn_norm_g", "lb_logits", "ln_mix_g", "ln_mix_b",
             "ln_ffn_g", "ln_ffn_b", "ffn_w_up", "ffn_w_down")
    return (loss, grad_x[None], *[out[nm][0] for nm in order], *[out[nm][1] for nm in order],
            *[out[nm][2] for nm in order], *[out[nm][3] for nm in order])
```

```python
import jax
import jax.numpy as jnp
from jax import lax
from jax.experimental import pallas as pl
from jax.experimental.pallas import tpu as pltpu

F32 = jnp.float32
BF16 = jnp.bfloat16

N_DEV = 8
LANES = 128
D_MODEL = 1024
ATTN_HEAD_DIM = 64
ATTN_HEADS = 16
ATTN_BLK = 128
DILATIONS = (1, 4, 16)
ROPE_THETA = 10000.0
HGRN_HEADS = 8
HGRN_CHUNK = 64
D_FF = 4096
LN_EPS = 1e-5
RMS_EPS = 1e-6
DEPTH = 2
ALPHA = (2 * DEPTH) ** 0.25
ADAM_LR, ADAM_B1, ADAM_B2, ADAM_EPS, ADAM_WD, ADAM_STEP = 0.001, 0.9, 0.999, 1e-08, 0.01, 10
VMEM_LIMIT = 48 * 1024 * 1024

_NT = (((1,), (1,)), ((), ()))
_TN = (((0,), (0,)), ((), ()))


def _dot(a, b):
    return jnp.dot(a, b, preferred_element_type=F32)


def _dot_nt(a, b):
    return lax.dot_general(a, b, _NT, preferred_element_type=F32)


def _dot_tn(a, b):
    return lax.dot_general(a, b, _TN, preferred_element_type=F32)


def _split3(x):
    p1 = x.astype(BF16)
    r1 = x - p1.astype(F32)
    p2 = r1.astype(BF16)
    p3 = (r1 - p2.astype(F32)).astype(BF16)
    return p1, p2, p3


def _exact_dot(sel, x):
    p1, p2, p3 = _split3(x)
    return _dot(sel, p1) + _dot(sel, p2) + _dot(sel, p3)


def _exact_dot_r(x, sel):
    p1, p2, p3 = _split3(x)
    return _dot(p1, sel) + _dot(p2, sel) + _dot(p3, sel)


def _params(sem=None):
    return pltpu.CompilerParams(dimension_semantics=sem, vmem_limit_bytes=VMEM_LIMIT)


def _mm(a, b, mode, *, name, m, n, k, tm=1024, tn=1024, tk=1024, out_dtypes=(F32,), epi=None, a_pre=None,
        tile_extras=(), row_extras=(), vec_extras=(), row_outs=(), vec_outs=0, a_split=1, b_split=1, out_split=1,
        b_col_off=0, b_k_off=0, out_col_off=0, out_cols=None, alias=None, epi_wants_j=False, deps=(), t_out=False):
    tm, tn, tk = min(tm, m), min(tn, n), min(tk, k)
    assert m % tm == 0 and n % tn == 0 and k % tk == 0, (name, m, n, k, tm, tn, tk)
    gm, gn, gk = m // tm, n // tn, k // tk
    if mode in ("nn", "nt"):
        if a_split > 1:
            kc = (k // a_split) // tk
            a_spec = pl.BlockSpec((None, tm, tk), lambda i, j, kk: (kk // kc, i, kk % kc))
        else:
            a_spec = pl.BlockSpec((tm, tk), lambda i, j, kk: (i, kk))
    else:
        a_spec = pl.BlockSpec((tk, tm), lambda i, j, kk: (kk, i))
    if mode in ("nn", "tn"):
        if b_split > 1:
            nc = (n // b_split) // tn
            b_spec = pl.BlockSpec((None, tk, tn), lambda i, j, kk: (j // nc, kk, j % nc))
        else:
            b_spec = pl.BlockSpec((tk, tn), lambda i, j, kk: (kk + b_k_off, j + b_col_off))
    else:
        b_spec = pl.BlockSpec((tn, tk), lambda i, j, kk: (j + b_col_off, kk + b_k_off))
    if out_split > 1:
        nco = (n // out_split) // tn
        o_spec = pl.BlockSpec((None, tm, tn), lambda i, j, kk: (j // nco, i, j % nco))
        o_shape = (out_split, m, n // out_split)
    else:
        o_spec = pl.BlockSpec((tm, tn), lambda i, j, kk: (i, j + out_col_off))
        o_shape = (m, out_cols if out_cols is not None else n)
    n_ex = len(tile_extras) + len(row_extras) + len(vec_extras)
    n_out = len(out_dtypes)
    n_plain = n_out + len(row_outs)
    assert not vec_outs or gn == 1
    if epi is None:
        def epi(acc):
            return (acc,)
    dot = {"nn": _dot, "nt": _dot_nt, "tn": _dot_tn}[mode]

    def body(*refs):
        a_ref, b_ref = refs[0], refs[1]
        ex = refs[2:2 + n_ex]
        outs = refs[2 + n_ex + (1 if alias is not None else 0) + len(deps):][:n_plain + vec_outs + (1 if t_out else 0)]
        ii = pl.program_id(0)
        jj = pl.program_id(1)

        def product():
            av = a_ref[...]
            if a_pre is not None:
                av = a_pre(av)
            return dot(av.astype(BF16), b_ref[...].astype(BF16))

        def finish(total):
            lead = (jj,) if epi_wants_j else ()
            res = epi(*lead, total, *[e[...] for e in ex])
            for o, r in zip(outs[:n_plain], res):
                o[...] = r.astype(o.dtype)
            for o, r in zip(outs[n_plain:n_plain + vec_outs], res[n_plain:]):
                @pl.when(ii == 0)
                def _(o=o, r=r):
                    o[...] = r

                @pl.when(ii > 0)
                def _(o=o, r=r):
                    o[...] += r
            if t_out:
                outs[-1][...] = res[0].astype(outs[-1].dtype).T

        if gk == 1:
            finish(product())
        else:
            acc = refs[-1]
            kk = pl.program_id(2)

            @pl.when(kk == 0)
            def _():
                acc[...] = product()

            @pl.when(kk > 0)
            def _():
                acc[...] += product()

            @pl.when(kk == gk - 1)
            def _():
                finish(acc[...])

    in_specs = [a_spec, b_spec] + [o_spec] * len(tile_extras)
    in_specs += [pl.BlockSpec((tm, r.shape[1]), lambda i, j, kk: (i, 0)) for r in row_extras]
    in_specs += [pl.BlockSpec((1, tn), lambda i, j, kk: (0, j))] * len(vec_extras)
    args = [a, b] + list(tile_extras) + list(row_extras) + list(vec_extras)
    io_alias = {}
    if alias is not None:
        in_specs.append(pl.BlockSpec(memory_space=pl.ANY))
        args.append(alias)
        io_alias = {len(args) - 1: 0}
    in_specs += [pl.BlockSpec(memory_space=pl.ANY)] * len(deps)
    args += list(deps)
    out_specs = [o_spec] * n_out
    out_shape = [jax.ShapeDtypeStruct(o_shape, dt) for dt in out_dtypes]
    for dt, w in row_outs:
        out_specs.append(pl.BlockSpec((tm, w), lambda i, j, kk: (i, 0)))
        out_shape.append(jax.ShapeDtypeStruct((m, w), dt))
    out_specs += [pl.BlockSpec((1, tn), lambda i, j, kk: (0, j))] * vec_outs
    out_shape += [jax.ShapeDtypeStruct((1, n), F32)] * vec_outs
    if t_out:
        assert out_split > 1
        out_specs.append(pl.BlockSpec((None, tn, tm), lambda i, j, kk: (j // nco, j % nco, i)))
        out_shape.append(jax.ShapeDtypeStruct((out_split, n // out_split, m), out_dtypes[0]))
    out = pl.pallas_call(
        body, name=name, grid=(gm, gn, gk), in_specs=in_specs, out_specs=out_specs, out_shape=out_shape,
        scratch_shapes=[pltpu.VMEM((tm, tn), F32)] if gk > 1 else [],
        input_output_aliases=io_alias,
        compiler_params=_params(("arbitrary" if vec_outs else "parallel", "parallel", "arbitrary")),
    )(*args)
    return out[0] if len(out) == 1 else out


def _rope_tables(seq, dil):
    pos = jnp.arange(seq, dtype=jnp.int32).reshape(seq // dil, dil).T.reshape(seq)
    half = ATTN_HEAD_DIM // 2
    inv = ROPE_THETA ** (-jnp.arange(half, dtype=F32) * (2.0 / ATTN_HEAD_DIM))
    ang = pos.astype(F32)[:, None] * inv[None, :]
    cos, sin = jnp.cos(ang), jnp.sin(ang)
    reps = LANES // ATTN_HEAD_DIM
    return (jnp.tile(jnp.concatenate([cos, cos], axis=1), (1, reps)),
            jnp.tile(jnp.concatenate([-sin, sin], axis=1), (1, reps)))


def _rotate(x, c, ss, sign=1.0):
    w = x.shape[-1]
    half = ATTN_HEAD_DIM // 2
    lane = lax.broadcasted_iota(jnp.int32, x.shape, 1)
    first = (lane % ATTN_HEAD_DIM) < half
    partner = jnp.where(first, pltpu.roll(x, w - half, 1), pltpu.roll(x, half, 1))
    reps = w // LANES
    if reps > 1:
        c = jnp.concatenate([c] * reps, axis=1)
        ss = jnp.concatenate([ss] * reps, axis=1)
    return x * c + sign * (partner * ss)


ROW_TILE = 512


def _ln_epi(acc, x, g, b):
    u = ALPHA * x + acc
    mu = jnp.mean(u, axis=-1, keepdims=True)
    uc = u - mu
    var = jnp.mean(uc * uc, axis=-1, keepdims=True)
    rstd = lax.rsqrt(var + LN_EPS)
    xh = uc * rstd
    out = xh * g + b
    return out, out, xh, rstd


def _ln_grad(dy, xh, rstd, g):
    dxh = dy * g
    m1 = jnp.mean(dxh, axis=-1, keepdims=True)
    m2 = jnp.mean(dxh * xh, axis=-1, keepdims=True)
    du = rstd * (dxh - m1 - xh * m2)
    return du, du, jnp.sum(dy * xh, axis=0, keepdims=True), jnp.sum(dy, axis=0, keepdims=True)


def _ln_grad_epi(acc, du_next, xh, rstd, g):
    return _ln_grad(acc + ALPHA * du_next, xh, rstd, g)


def _loss_ln_bwd(y, target, xhat, rstd, g):
    s, d = y.shape

    def body(y_ref, t_ref, xh_ref, r_ref, g_ref, du_ref, dub_ref, dg_ref, db_ref, sq_ref):
        @pl.when(pl.program_id(0) == 0)
        def _():
            dg_ref[...] = jnp.zeros_like(dg_ref)
            db_ref[...] = jnp.zeros_like(db_ref)
            sq_ref[...] = jnp.zeros_like(sq_ref)

        e = y_ref[...] - t_ref[...]
        du, _, dg, db = _ln_grad(e * (1.0 / d), xh_ref[...], r_ref[...], g_ref[...])
        du_ref[...] = du
        dub_ref[...] = du.astype(BF16)
        dg_ref[...] += dg
        db_ref[...] += db
        sq_ref[...] += jnp.sum(e * e, axis=0, keepdims=True)

    row = pl.BlockSpec((ROW_TILE, d), lambda i: (i, 0))
    vec = pl.BlockSpec((1, d), lambda i: (0, 0))
    return pl.pallas_call(
        body, name="loss_ln_bwd", grid=(s // ROW_TILE,),
        in_specs=[row, row, row, pl.BlockSpec((ROW_TILE, 1), lambda i: (i, 0)), vec],
        out_specs=[row, row, vec, vec, vec],
        out_shape=[jax.ShapeDtypeStruct((s, d), F32), jax.ShapeDtypeStruct((s, d), BF16)]
        + [jax.ShapeDtypeStruct((1, d), F32)] * 3,
        compiler_params=_params(("arbitrary",)),
    )(y, target, xhat, rstd, g)


POS_BLK = 2048


def _class_rows(r, dil):
    return pl.ds(r, POS_BLK // dil, stride=dil) if dil > 1 else pl.ds(0, POS_BLK)


def _class_view(a, dil):
    s, w = a.shape
    return a.reshape(dil, s // dil, w)


def _class_spec(dil, all_tiles=True):
    if all_tiles:
        return pl.BlockSpec((dil, POS_BLK // dil, LANES), lambda i, t: (0, i, t))
    return pl.BlockSpec((dil, POS_BLK // dil, LANES), lambda i, t: (0, i, 0))


def _pos_spec(all_tiles=True):
    if all_tiles:
        return pl.BlockSpec((POS_BLK, LANES), lambda i, t: (i, t))
    return pl.BlockSpec((POS_BLK, LANES), lambda i, t: (i, 0))


def _prep_x(x, deps=()):
    s, d = x.shape

    def body(x_ref, *refs):
        outs = refs[len(deps):]
        for dil, o_ref in zip(DILATIONS, outs):
            for r in range(dil):
                o_ref[r] = x_ref[_class_rows(r, dil), :].astype(BF16)

    outs = pl.pallas_call(
        body, name="prep_x", grid=(s // POS_BLK, d // LANES),
        in_specs=[_pos_spec()] + [pl.BlockSpec(memory_space=pl.ANY)] * len(deps),
        out_specs=[_class_spec(dil) for dil in DILATIONS],
        out_shape=[jax.ShapeDtypeStruct((dil, s // dil, d), BF16) for dil in DILATIONS],
        compiler_params=_params(("parallel", "parallel")),
    )(x, *deps)
    return [o.reshape(s, d) for o in outs]


def _head_expand_matrix():
    h = lax.broadcasted_iota(jnp.int32, (LANES, D_MODEL), 0)
    l = lax.broadcasted_iota(jnp.int32, (LANES, D_MODEL), 1)
    return (l // ATTN_HEAD_DIM == h).astype(BF16)


def _attn_fwd(qkv, dil, *, name):
    _, s, d = qkv.shape
    nq = s // ATTN_BLK
    per = nq // dil
    scale = ATTN_HEAD_DIM ** -0.5
    tiles = d // LANES

    def body(q_ref, kc_ref, kp_ref, vc_ref, vp_ref, o_ref, lse_ref):
        qb = pl.program_id(0)
        first = (qb % per) == 0
        qi = lax.broadcasted_iota(jnp.int32, (ATTN_BLK, 2 * ATTN_BLK), 0)
        kj = lax.broadcasted_iota(jnp.int32, (ATTN_BLK, 2 * ATTN_BLK), 1)
        dist = qi + ATTN_BLK - kj
        valid = (dist >= 0) & (dist <= ATTN_BLK) & ((kj >= ATTN_BLK) | jnp.logical_not(first))
        lane = lax.broadcasted_iota(jnp.int32, (ATTN_BLK, LANES), 1)
        lse_tile = jnp.zeros((ATTN_BLK, LANES), F32)
        zero = jnp.zeros((), BF16)
        in_head = [(lane // ATTN_HEAD_DIM) == hh for hh in range(2)]
        for t0 in range(0, tiles, ATTN_TILE_GROUP):
            group = range(t0, t0 + ATTN_TILE_GROUP)
            heads = [(t, hh) for t in group for hh in range(2)]
            cols = {t: pl.ds(t * LANES, LANES) for t in group}
            k2 = {t: jnp.concatenate([kp_ref[:, cols[t]], kc_ref[:, cols[t]]], axis=0) for t in group}
            v2 = {t: jnp.concatenate([vp_ref[:, cols[t]], vc_ref[:, cols[t]]], axis=0) for t in group}
            sc = {(t, hh): jnp.where(valid, _dot_nt(jnp.where(in_head[hh], q_ref[:, cols[t]], zero), k2[t]) * scale,
                                     -jnp.inf) for t, hh in heads}
            mx = {i: jnp.max(sc[i], axis=-1, keepdims=True) for i in heads}
            p = {i: jnp.exp(sc[i] - mx[i]) for i in heads}
            l = {i: jnp.sum(p[i], axis=-1, keepdims=True) for i in heads}
            oh = {i: _dot(p[i].astype(BF16), v2[i[0]]) / l[i] for i in heads}
            for t in group:
                o_ref[:, cols[t]] = jnp.where(in_head[0], oh[t, 0], oh[t, 1])
                for hh in range(2):
                    lse_tile = jnp.where(lane == 2 * t + hh, mx[t, hh] + jnp.log(l[t, hh]), lse_tile)
        lse_ref[...] = lse_tile

    def blk(piece, prev):
        if prev:
            return pl.BlockSpec((None, ATTN_BLK, d), lambda i: (piece, jnp.maximum(i - 1, 0), 0))
        return pl.BlockSpec((None, ATTN_BLK, d), lambda i: (piece, i, 0))

    return pl.pallas_call(
        body, name=name, grid=(nq,),
        in_specs=[blk(0, False), blk(1, False), blk(1, True), blk(2, False), blk(2, True)],
        out_specs=[pl.BlockSpec((ATTN_BLK, d), lambda i: (i, 0)), pl.BlockSpec((ATTN_BLK, LANES), lambda i: (i, 0))],
        out_shape=[jax.ShapeDtypeStruct((s, d), F32), jax.ShapeDtypeStruct((s, LANES), F32)],
        compiler_params=_params(("parallel",)),
    )(qkv, qkv, qkv, qkv, qkv)


def _attn_combine(os_, lses):
    s, d = os_[0].shape
    sel = _head_expand_matrix()

    def body(o0, o1, o2, l0, l1, l2, sel_ref, of_ref, ob_ref, lt_ref, o_pos, l_pos):
        for g, (dil, o_ref, l_ref) in enumerate(zip(DILATIONS, (o0, o1, o2), (l0, l1, l2))):
            for r in range(dil):
                o_pos[g, _class_rows(r, dil), :] = o_ref[r]
                l_pos[g, _class_rows(r, dil), :] = l_ref[r]
        la, lb_, lc = l_pos[0], l_pos[1], l_pos[2]
        mx = jnp.maximum(jnp.maximum(la, lb_), lc)
        es = (jnp.exp(la - mx), jnp.exp(lb_ - mx), jnp.exp(lc - mx))
        z = es[0] + es[1] + es[2]
        lt_ref[...] = mx + jnp.log(z)
        acc = jnp.zeros((POS_BLK, LANES), F32)
        for g in range(3):
            acc += _exact_dot_r(es[g] / z, sel_ref[...]) * o_pos[g]
        of_ref[...] = acc
        ob_ref[...] = acc.astype(BF16)

    return pl.pallas_call(
        body, name="attn_combine", grid=(s // POS_BLK, d // LANES),
        in_specs=[_class_spec(dil) for dil in DILATIONS] + [_class_spec(dil, False) for dil in DILATIONS]
        + [pl.BlockSpec((LANES, LANES), lambda i, t: (0, t))],
        out_specs=[_pos_spec(), _pos_spec(), _pos_spec(False)],
        out_shape=[jax.ShapeDtypeStruct((s, d), F32), jax.ShapeDtypeStruct((s, d), BF16),
                   jax.ShapeDtypeStruct((s, LANES), F32)],
        scratch_shapes=[pltpu.VMEM((3, POS_BLK, LANES), F32), pltpu.VMEM((3, POS_BLK, LANES), F32)],
        compiler_params=_params(("parallel", "arbitrary")),
    )(*[_class_view(o, dil) for o, dil in zip(os_, DILATIONS)],
      *[_class_view(l, dil) for l, dil in zip(lses, DILATIONS)], sel)


ATTN_TILE_GROUP = 2
AUX_PER_TILE = 12


def _aux_placement():
    h = lax.broadcasted_iota(jnp.int32, (6, LANES, LANES), 1)
    l = lax.broadcasted_iota(jnp.int32, (6, LANES, LANES), 2)
    j = lax.broadcasted_iota(jnp.int32, (6, LANES, LANES), 0)
    target = AUX_PER_TILE * (h // 2) + 3 * (h % 2) + jnp.where(j < 3, j, 3 + j)
    return ((l == target) & (h < ATTN_HEADS)).astype(BF16)


def _attn_bwd_prep(do, o, lse):
    s, d = do.shape
    tiles = d // LANES
    sel_t = _head_expand_matrix().T

    def body(do_ref, o_ref, l_ref, sel_ref, place_ref, *refs):
        outs, delta, aux = refs[:9], refs[9], refs[10]
        t = pl.program_id(1)
        part = _exact_dot_r(do_ref[...] * o_ref[...], sel_ref[...])

        @pl.when(t == 0)
        def _():
            delta[...] = part

        @pl.when(t > 0)
        def _():
            delta[...] += part

        for g, dil in enumerate(DILATIONS):
            for r in range(dil):
                blk = do_ref[_class_rows(r, dil), :].astype(BF16)
                outs[g][r] = blk
                outs[3 + g][r] = blk.T

        @pl.when(t == tiles - 1)
        def _():
            pieces = _split3(l_ref[...]) + _split3(delta[...])
            acc = _dot(pieces[0], place_ref[0])
            for j in range(1, 6):
                acc += _dot(pieces[j], place_ref[j])
            aux[...] = acc
            for g, dil in enumerate(DILATIONS):
                for r in range(dil):
                    outs[6 + g][r] = aux[_class_rows(r, dil), :].astype(BF16)

    outs = pl.pallas_call(
        body, name="attn_bwd_prep", grid=(s // POS_BLK, tiles),
        in_specs=[_pos_spec(), _pos_spec(), _pos_spec(False), pl.BlockSpec((LANES, LANES), lambda i, t: (t, 0)),
                  pl.BlockSpec((6, LANES, LANES), lambda i, t: (0, 0, 0))],
        out_specs=[_class_spec(dil) for dil in DILATIONS]
        + [pl.BlockSpec((dil, LANES, POS_BLK // dil), lambda i, t: (0, t, i)) for dil in DILATIONS]
        + [_class_spec(dil, False) for dil in DILATIONS],
        out_shape=[jax.ShapeDtypeStruct((dil, s // dil, d), BF16) for dil in DILATIONS]
        + [jax.ShapeDtypeStruct((dil, d, s // dil), BF16) for dil in DILATIONS]
        + [jax.ShapeDtypeStruct((dil, s // dil, LANES), BF16) for dil in DILATIONS],
        scratch_shapes=[pltpu.VMEM((POS_BLK, LANES), F32), pltpu.VMEM((POS_BLK, LANES), F32)],
        compiler_params=_params(("parallel", "arbitrary")),
    )(do, o, lse, sel_t, _aux_placement())
    return ([a.reshape(s, d) for a in outs[0:3]], list(outs[3:6]), [a.reshape(s, LANES) for a in outs[6:9]])


def _attn_bwd(qkv, qkv_t, do, do_t, aux, tables, dil, *, name):
    _, s, d = qkv.shape
    nq = s // ATTN_BLK
    per = nq // dil
    scale = ATTN_HEAD_DIM ** -0.5
    assert scale == 0.125
    tiles = d // LANES
    half = ATTN_HEAD_DIM

    def body(qd_ref, qo_ref, k_ref, qtd_ref, qto_ref, kt_ref, vt_ref, dod_ref, doo_ref, dotd_ref, doto_ref,
             auxd_ref, auxo_ref, c_ref, ss_ref, out_ref, carry):
        kb = pl.program_id(0)

        @pl.when(kb == 0)
        def _():
            carry[...] = jnp.zeros_like(carry)

        has_next = (kb % per) != (per - 1)
        qi = lax.broadcasted_iota(jnp.int32, (ATTN_BLK, 2 * ATTN_BLK), 0)
        kj = lax.broadcasted_iota(jnp.int32, (ATTN_BLK, 2 * ATTN_BLK), 1) % ATTN_BLK
        valid_d = kj <= qi
        valid_o = (kj >= qi) & has_next
        lane = lax.broadcasted_iota(jnp.int32, (ATTN_BLK, LANES), 1)
        row = lax.broadcasted_iota(jnp.int32, (LANES, ATTN_BLK), 0)
        side = lax.broadcasted_iota(jnp.int32, (LANES, 2 * ATTN_BLK), 0)
        first = lax.broadcasted_iota(jnp.int32, (LANES, 2 * ATTN_BLK), 1) < ATTN_BLK
        c, ss = c_ref[...], ss_ref[...]
        zero = jnp.zeros((), BF16)
        sides = ((qd_ref, qtd_ref, dod_ref, dotd_ref, auxd_ref[...], valid_d),
                 (qo_ref, qto_ref, doo_ref, doto_ref, auxo_ref[...], valid_o))
        both = (0, 1)

        def head_halves(x, index):
            axis = 0 if index is lane else 1
            return jnp.concatenate([jnp.where(index < half, x, zero), jnp.where(index >= half, x, zero)], axis=axis)

        for t0 in range(0, tiles, ATTN_TILE_GROUP):
            group = range(t0, t0 + ATTN_TILE_GROUP)
            cols = {t: pl.ds(t * LANES, LANES) for t in group}
            kk, kk_t, vv_t = {}, {}, {}
            for t in group:
                base = AUX_PER_TILE * t
                hit = lambda lo: ((first & (side >= base + lo) & (side < base + lo + 3))
                                  | (jnp.logical_not(first) & (side >= base + lo + 3) & (side < base + lo + 6)))
                kk[t] = head_halves(k_ref[:, cols[t]], lane)
                kk_t[t] = jnp.concatenate([head_halves(kt_ref[cols[t], :], row),
                                           jnp.where(hit(0), -1.0 / scale, 0.0).astype(BF16)], axis=0)
                vv_t[t] = jnp.concatenate([head_halves(vt_ref[cols[t], :], row),
                                           jnp.where(hit(6), -1.0, 0.0).astype(BF16)], axis=0)
            sc = {(t, w): _dot(jnp.concatenate([sides[w][0][:, cols[t]], sides[w][4]], axis=1), kk_t[t])
                  for t in group for w in both}
            dpd = {(t, w): _dot(jnp.concatenate([sides[w][2][:, cols[t]], sides[w][4]], axis=1), vv_t[t])
                   for t in group for w in both}
            p = {i: jnp.where(sides[i[1]][5], jnp.exp(sc[i] * scale), 0.0) for i in sc}
            ds = {i: (p[i] * dpd[i] * scale).astype(BF16) for i in sc}
            pb = {i: p[i].astype(BF16) for i in sc}
            dv_t = {t: sum(_dot(sides[w][3][cols[t], :], pb[t, w]) for w in both) for t in group}
            dk_t = {t: sum(_dot(sides[w][1][cols[t], :], ds[t, w]) for w in both) for t in group}
            dq = {i: _dot(ds[i], kk[i[0]]) for i in sc}
            for t in group:
                dq_now = carry[:, cols[t]] + dq[t, 0]
                carry[:, cols[t]] = dq[t, 1]
                dk = jnp.where(row < half, dk_t[t][:, :ATTN_BLK], dk_t[t][:, ATTN_BLK:]).T
                dv = jnp.where(row < half, dv_t[t][:, :ATTN_BLK], dv_t[t][:, ATTN_BLK:]).T
                out_ref[0, :, cols[t]] = _rotate(dq_now, c, ss, -1.0).astype(BF16)
                out_ref[1, :, cols[t]] = _rotate(dk, c, ss, -1.0).astype(BF16)
                out_ref[2, :, cols[t]] = dv.astype(BF16)

    def nxt(i):
        return jnp.minimum(i + 1, nq - 1)

    def piece(p, shift):
        if shift:
            return pl.BlockSpec((None, ATTN_BLK, d), lambda i: (p, nxt(i), 0))
        return pl.BlockSpec((None, ATTN_BLK, d), lambda i: (p, i, 0))

    def piece_t(p, shift):
        if shift:
            return pl.BlockSpec((None, d, ATTN_BLK), lambda i: (p, 0, nxt(i)))
        return pl.BlockSpec((None, d, ATTN_BLK), lambda i: (p, 0, i))

    def rows(width, shift):
        if shift:
            return pl.BlockSpec((ATTN_BLK, width), lambda i: (nxt(i), 0))
        return pl.BlockSpec((ATTN_BLK, width), lambda i: (i, 0))

    def do_t_spec(shift):
        if shift:
            return pl.BlockSpec((None, d, ATTN_BLK), lambda i: (nxt(i) // per, 0, nxt(i) % per))
        return pl.BlockSpec((None, d, ATTN_BLK), lambda i: (i // per, 0, i % per))

    return pl.pallas_call(
        body, name=name, grid=(nq,),
        in_specs=[piece(0, False), piece(0, True), piece(1, False),
                  piece_t(0, False), piece_t(0, True), piece_t(1, False), piece_t(2, False),
                  rows(d, False), rows(d, True), do_t_spec(False), do_t_spec(True),
                  rows(LANES, False), rows(LANES, True), rows(LANES, False), rows(LANES, False)],
        out_specs=pl.BlockSpec((3, ATTN_BLK, d), lambda i: (0, i, 0)),
        out_shape=jax.ShapeDtypeStruct((3, s, d), BF16),
        scratch_shapes=[pltpu.VMEM((ATTN_BLK, d), F32)],
        compiler_params=_params(("arbitrary",)),
    )(qkv, qkv, qkv, qkv_t, qkv_t, qkv_t, qkv_t, do, do, do_t, do_t, aux, aux, *tables)


def _dx_combine(du, parts):
    s, d = du.shape

    def body(du_ref, p0, p1, p2, out_ref):
        out_ref[...] = ALPHA * du_ref[...] + p0[0]
        for dil, p_ref in zip(DILATIONS[1:], (p1, p2)):
            for r in range(dil):
                out_ref[_class_rows(r, dil), :] += p_ref[r]

    return pl.pallas_call(
        body, name="dx_combine", grid=(s // POS_BLK, d // LANES),
        in_specs=[_pos_spec()] + [_class_spec(dil) for dil in DILATIONS], out_specs=_pos_spec(),
        out_shape=jax.ShapeDtypeStruct((s, d), F32),
        compiler_params=_params(("parallel", "parallel")),
    )(du, *[_class_view(p, dil) for p, dil in zip(parts, DILATIONS)])


HGRN_ROWS = 1024


def _tri(lower):
    i = lax.broadcasted_iota(jnp.int32, (HGRN_CHUNK, HGRN_CHUNK), 0)
    j = lax.broadcasted_iota(jnp.int32, (HGRN_CHUNK, HGRN_CHUNK), 1)
    return (j <= i) if lower else (j >= i)


def _hgrn_gates(qr, z, lb):
    sq = jax.nn.sigmoid(qr)
    e = jnp.exp(-jnp.abs(z))
    big = 1.0 / (1.0 + e)
    small = e * big
    sg = jnp.where(z >= 0, big, small)
    sn = jnp.where(z >= 0, small, big)
    return sq, qr * sq, sg, sn, lb + (1.0 - lb) * sg, (1.0 - lb) * sn


def _hgrn_fwd(p2, lb, norm_g):
    _, s, d = p2.shape
    nblk = s // HGRN_ROWS
    cps = HGRN_ROWS // HGRN_CHUNK

    def body(p_ref, lb_ref, g_ref, o_ref, on_ref, st_ref, state):
        @pl.when(pl.program_id(1) == 0)
        def _():
            state[...] = jnp.zeros_like(state)

        chunks = [pl.ds(c * HGRN_CHUNK, HGRN_CHUNK) for c in range(cps)]
        ltri = _tri(True)
        lsel = ltri.astype(BF16)
        _, q, _, _, f, key = _hgrn_gates(p_ref[0], p_ref[1], lb_ref[...])
        lf = jnp.log(f)
        v = [p_ref[2, rows, :].astype(BF16) for rows in chunks]
        b = [_exact_dot(lsel, lf[c * HGRN_CHUNK:(c + 1) * HGRN_CHUNK]) for c in range(cps)]
        b_last = [bc[HGRN_CHUNK - 1:HGRN_CHUNK, :] for bc in b]
        q = [q[c * HGRN_CHUNK:(c + 1) * HGRN_CHUNK] for c in range(cps)]
        key = [key[c * HGRN_CHUNK:(c + 1) * HGRN_CHUNK] for c in range(cps)]
        qd = [(q[c] * jnp.exp(b[c])).astype(BF16) for c in range(cps)]
        kd = [(key[c] * jnp.exp(-b[c])).astype(BF16) for c in range(cps)]
        k2 = [(key[c] * jnp.exp(b_last[c] - b[c])).astype(BF16) for c in range(cps)]
        a = [jnp.where(ltri, _dot_nt(qd[c], kd[c]), 0.0).astype(BF16) for c in range(cps)]
        kv = [_dot_tn(v[c], k2[c]) for c in range(cps)]
        st, sts = state[...], []
        for c in range(cps):
            sts.append(st)
            st_ref[0, c] = st
            st = st * jnp.exp(b_last[c]) + kv[c]
        state[...] = st
        gv = g_ref[...]
        for c in range(cps):
            o = _dot(a[c], v[c]) + _dot_nt(qd[c], sts[c].astype(BF16))
            o_ref[chunks[c], :] = o
            r = lax.rsqrt(jnp.mean(o * o, axis=-1, keepdims=True) + RMS_EPS)
            on_ref[chunks[c], :] = (o * r * gv).astype(BF16)

    vec = pl.BlockSpec((1, LANES), lambda h, c: (0, h))
    col = pl.BlockSpec((HGRN_ROWS, LANES), lambda h, c: (c, h))
    return pl.pallas_call(
        body, name="hgrn_fwd", grid=(HGRN_HEADS, nblk),
        in_specs=[pl.BlockSpec((3, HGRN_ROWS, LANES), lambda h, c: (0, c, h)), vec, vec],
        out_specs=[col, col, pl.BlockSpec((1, cps, LANES, LANES), lambda h, c: (h, c, 0, 0))],
        out_shape=[jax.ShapeDtypeStruct((s, d), F32), jax.ShapeDtypeStruct((s, d), BF16),
                   jax.ShapeDtypeStruct((HGRN_HEADS, s // HGRN_CHUNK, LANES, LANES), F32)],
        scratch_shapes=[pltpu.VMEM((LANES, LANES), F32)],
        compiler_params=_params(("parallel", "arbitrary")),
    )(p2, lb, norm_g)


def _hgrn_bwd(p2, lb, norm_g, o_raw, states, dyn):
    _, s, d = p2.shape
    nblk = s // HGRN_ROWS
    cps = HGRN_ROWS // HGRN_CHUNK

    def body(p_ref, lb_ref, g_ref, o_ref, st_ref, dy_ref, dp_ref, dg_ref, dlb_ref, dstate):
        @pl.when(pl.program_id(1) == 0)
        def _():
            dstate[...] = jnp.zeros_like(dstate)
            dg_ref[...] = jnp.zeros_like(dg_ref)
            dlb_ref[...] = jnp.zeros_like(dlb_ref)

        n = cps
        cut = lambda t: [t[c * HGRN_CHUNK:(c + 1) * HGRN_CHUNK] for c in range(n)]
        chunks = [pl.ds(c * HGRN_CHUNK, HGRN_CHUNK) for c in range(n)]
        lbv = lb_ref[...]
        ltri = _tri(True)
        lsel = ltri.astype(BF16)
        usel = _tri(False).astype(BF16)
        last_row = lax.broadcasted_iota(jnp.int32, (HGRN_CHUNK, LANES), 0) == HGRN_CHUNK - 1
        qr, z = p_ref[0], p_ref[1]
        sq, q, sg, sn, f, key = _hgrn_gates(qr, z, lbv)
        lf = jnp.log(f)
        v = [p_ref[2, rows, :].astype(BF16) for rows in chunks]
        o, dyv = o_ref[...], dy_ref[...]
        r = lax.rsqrt(jnp.mean(o * o, axis=-1, keepdims=True) + RMS_EPS)
        oh = o * r
        doh = dyv * g_ref[...]
        do = cut((r * (doh - oh * jnp.mean(doh * oh, axis=-1, keepdims=True))).astype(BF16))
        dg_ref[...] += jnp.sum(dyv * oh, axis=0, keepdims=True)
        b = [_exact_dot(lsel, lfc) for lfc in cut(lf)]
        b_last = [bc[HGRN_CHUNK - 1:HGRN_CHUNK, :] for bc in b]
        q, key = cut(q), cut(key)
        eb = [jnp.exp(bc) for bc in b]
        enb = [jnp.exp(-bc) for bc in b]
        e2 = [jnp.exp(b_last[c] - b[c]) for c in range(n)]
        dec = [jnp.exp(bl) for bl in b_last]
        qd_f = [q[c] * eb[c] for c in range(n)]
        kd_f = [key[c] * enb[c] for c in range(n)]
        k2_f = [key[c] * e2[c] for c in range(n)]
        qd, kd, k2 = ([t.astype(BF16) for t in ts] for ts in (qd_f, kd_f, k2_f))
        a = [jnp.where(ltri, _dot_nt(qd[c], kd[c]), 0.0).astype(BF16) for c in range(n)]
        da = [jnp.where(ltri, _dot_nt(do[c], v[c]), 0.0).astype(BF16) for c in range(n)]
        grow = [_dot_tn(do[c], qd[c]) for c in range(n)]
        dst, dsts = dstate[...], [None] * n
        for c in reversed(range(n)):
            dsts[c] = dst
            dst = dst * dec[c] + grow[c]
        dstate[...] = dst
        st = [st_ref[0, c] for c in range(n)]
        dstb = [t.astype(BF16) for t in dsts]
        dv = [_dot_tn(a[c], do[c]) + _dot_nt(k2[c], dstb[c]) for c in range(n)]
        dqd = [_dot(da[c], kd[c]) + _dot(do[c], st[c].astype(BF16)) for c in range(n)]
        dkd = [_dot_tn(da[c], qd[c]) for c in range(n)]
        dk2 = [_dot(v[c], dstb[c]) for c in range(n)]
        db = []
        for c in range(n):
            ddec = jnp.sum(dsts[c] * st[c], axis=0, keepdims=True)
            db_last = jnp.sum(dk2[c] * k2_f[c], axis=0, keepdims=True) + ddec * dec[c]
            db.append(dqd[c] * qd_f[c] - dkd[c] * kd_f[c] - dk2[c] * k2_f[c] + jnp.where(last_row, db_last, 0.0))
        dlf = [_exact_dot(usel, dbc) for dbc in db]
        f, sg, sn, sq, qr = cut(f), cut(sg), cut(sn), cut(sq), cut(qr)
        dlb_acc = jnp.zeros((1, LANES), F32)
        for c in range(n):
            dkey = dkd[c] * enb[c] + dk2[c] * e2[c]
            common = dlf[c] / f[c] - dkey
            dlb_acc += jnp.sum(common * sn[c], axis=0, keepdims=True)
            dp_ref[0, chunks[c], :] = (dqd[c] * eb[c] * (sq[c] * (1.0 + qr[c] * (1.0 - sq[c])))).astype(BF16)
            dp_ref[1, chunks[c], :] = (common * ((1.0 - lbv) * sg[c] * sn[c])).astype(BF16)
            dp_ref[2, chunks[c], :] = dv[c].astype(BF16)
        dlb_ref[...] += dlb_acc

    def rev(c):
        return nblk - 1 - c

    vec = pl.BlockSpec((1, LANES), lambda h, c: (0, h))
    col = pl.BlockSpec((HGRN_ROWS, LANES), lambda h, c: (rev(c), h))
    p3 = pl.BlockSpec((3, HGRN_ROWS, LANES), lambda h, c: (0, rev(c), h))
    return pl.pallas_call(
        body, name="hgrn_bwd", grid=(HGRN_HEADS, nblk),
        in_specs=[p3, vec, vec, col, pl.BlockSpec((1, cps, LANES, LANES), lambda h, c: (h, rev(c), 0, 0)), col],
        out_specs=[p3, vec, vec],
        out_shape=[jax.ShapeDtypeStruct((3, s, d), BF16), jax.ShapeDtypeStruct((1, d), F32),
                   jax.ShapeDtypeStruct((1, d), F32)],
        scratch_shapes=[pltpu.VMEM((LANES, LANES), F32)],
        compiler_params=_params(("parallel", "arbitrary")),
    )(p2, lb, norm_g, o_raw, states, dyn)


SMALL_ROWS = 16
FUSED_TM = 512


def _relu2(h):
    r = jnp.maximum(h.astype(F32), 0.0)
    return r * r


def _dact_epi(acc, h):
    return (acc * (2.0 * jnp.maximum(h.astype(F32), 0.0)),)


def _rope_epi(j, acc, c, ss):
    return (jnp.where(j >= 2, acc, _rotate(acc, c, ss)),)


def _local_step(x, target, get_w, small, on_grads, deps=()):
    s, d = x.shape
    lb = small["lb"]
    gq = 3 * d
    gt = gq // 1024

    def ln_of(a, w, x_in, which, layer, *, name, k, a_pre=None, deps=()):
        g, b = small[f"ln_{which}_g"][layer:layer + 1], small[f"ln_{which}_b"][layer:layer + 1]
        return _mm(a, w, "nn", name=name, m=s, n=d, k=k, tm=FUSED_TM, a_pre=a_pre, out_dtypes=(F32, BF16, F32),
                   row_outs=((F32, 1),), epi=_ln_epi, tile_extras=(x_in,), vec_extras=(g, b), deps=deps)

    def ln_grad_of(a, w, du_next, xh, rstd, which, layer, *, name, k, a_split=1, deps=()):
        return _mm(a, w, "nt", name=name, m=s, n=d, k=k, tm=FUSED_TM, a_split=a_split, out_dtypes=(F32, BF16),
                   vec_outs=2, epi=_ln_grad_epi, tile_extras=(du_next, xh), row_extras=(rstd,),
                   vec_extras=(small[f"ln_{which}_g"][layer:layer + 1],), deps=deps)

    def ffn_fwd(xb, x_in, w_up, w_down, tag):
        h = _mm(xb, w_up, "nn", name=f"ffn_up_{tag}", m=s, n=D_FF, k=d, out_dtypes=(BF16,))
        return (h,) + tuple(ln_of(h, w_down, x_in, "ffn", tag, name=f"ffn_down_{tag}", k=D_FF, a_pre=_relu2))

    def ffn_bwd(du, dub, xb, h, w_up, w_down, xh, rstd, tag):
        dh = _mm(dub, w_down, "nt", name=f"ffn_dact_{tag}", m=s, n=D_FF, k=d, out_dtypes=(BF16,), epi=_dact_epi,
                 tile_extras=(h,))
        g_down = _mm(h, dub, "tn", name=f"ffn_gdown_{tag}", m=D_FF, n=d, k=s, a_pre=_relu2, out_dtypes=(BF16,))
        g_up = _mm(xb, dh, "tn", name=f"ffn_gup_{tag}", m=d, n=D_FF, k=s, out_dtypes=(BF16,))
        after = on_grads(f"ffn{tag}", {f"ffn_w_down{tag}": g_down, f"ffn_w_up{tag}": g_up})
        return ln_grad_of(dh, w_up, du, xh, rstd, "mix", tag, name=f"ffn_dx_{tag}", k=D_FF, deps=after)

    xs = _prep_x(x, deps)
    w_ain, after = get_w("attn_w_in", None)
    tabs, qkvs, qkv_ts, o_parts, lse_parts = [], [], [], [], []
    for g, dil in enumerate(DILATIONS):
        tabs.append(_rope_tables(s, dil))
        qkv, qkv_t = _mm(xs[g], w_ain, "nn", name=f"attn_in_{g}", m=s, n=gq, k=d, b_col_off=g * gt, out_split=3,
                         out_dtypes=(BF16,), epi=_rope_epi, epi_wants_j=True, row_extras=tabs[g], t_out=True,
                         deps=after if g == 0 else ())
        qkvs.append(qkv)
        qkv_ts.append(qkv_t)
        o_g, lse_g = _attn_fwd(qkv, dil, name=f"attn_fwd_{g}")
        o_parts.append(o_g)
        lse_parts.append(lse_g)
    o_f, o_b, lse_t = _attn_combine(o_parts, lse_parts)
    w_aout, after = get_w("attn_w_out", o_b)
    x1, x1b, xh1, r1 = ln_of(o_b, w_aout, x, "mix", 0, name="attn_out", k=d, deps=after)
    w_up0, w_down0 = get_w("ffn_w_up0", o_b)[0], get_w("ffn_w_down0", o_b)[0]
    h0, x2, x2b, xh2, r2 = ffn_fwd(x1b, x1, w_up0, w_down0, 0)
    w_hin, w_hout, norm_g = get_w("hgrn_w_in", x2b)[0], get_w("hgrn_w_out", x2b)[0], get_w("hgrn_norm_g", x2b)[0]
    p2 = _mm(x2b, w_hin, "nn", name="hgrn_in", m=s, n=3 * d, k=d, out_split=3)
    o_raw, o_n, states = _hgrn_fwd(p2, lb, norm_g)
    w_up1, w_down1 = get_w("ffn_w_up1", o_n)[0], get_w("ffn_w_down1", o_n)[0]
    x3, x3b, xh3, r3 = ln_of(o_n, w_hout, x2, "mix", 1, name="hgrn_out", k=d)
    h1, x4, _, xh4, r4 = ffn_fwd(x3b, x3, w_up1, w_down1, 1)
    du, dub, dg_ffn1, db_ffn1, sq = _loss_ln_bwd(x4, target, xh4, r4, small["ln_ffn_g"][1:2])
    du, dub, dg_mix1, db_mix1 = ffn_bwd(du, dub, x3b, h1, w_up1, w_down1, xh3, r3, 1)
    dyn = _mm(dub, w_hout, "nt", name="hgrn_dout", m=s, n=d, k=d)
    g_hout = _mm(o_n, dub, "tn", name="hgrn_gout", m=d, n=d, k=s, out_dtypes=(BF16,))
    dp2, d_norm_g, d_lb = _hgrn_bwd(p2, lb, norm_g, o_raw, states, dyn)
    g_hin = _mm(x2b, dp2, "tn", name="hgrn_gin", m=d, n=3 * d, k=s, b_split=3, out_dtypes=(BF16,))
    after = on_grads("hgrn", {"hgrn_w_out": g_hout, "hgrn_w_in": g_hin})
    du, dub, dg_ffn0, db_ffn0 = ln_grad_of(dp2, w_hin, du, xh2, r2, "ffn", 0, name="hgrn_dx", k=3 * d, a_split=3,
                                           deps=after)
    du, dub, dg_mix0, db_mix0 = ffn_bwd(du, dub, x1b, h0, w_up0, w_down0, xh1, r1, 0)
    g_aout = _mm(o_b, dub, "tn", name="attn_gout", m=d, n=d, k=s, out_dtypes=(BF16,))
    sm1 = jax.nn.softmax(small["lb_logits"], axis=0)
    d_l1 = d_lb * (sm1[0:1] * sm1[1:2])
    zeros = jnp.zeros((SMALL_ROWS - 12, d), F32)
    loss_row = jnp.broadcast_to(0.5 * jnp.sum(sq) / d, (1, d))
    small_grads = jnp.concatenate([dg_mix0, dg_mix1, db_mix0, db_mix1, dg_ffn0, dg_ffn1, db_ffn0, db_ffn1,
                                   -d_l1, d_l1, d_norm_g, loss_row, zeros], axis=0)
    after = on_grads("attn_out", {"attn_w_out": g_aout, "small": small_grads})
    do = _mm(dub, w_aout, "nt", name="attn_dout", m=s, n=d, k=d, deps=after)
    do_parts, do_ts, aux_parts = _attn_bwd_prep(do, o_f, lse_t)
    g_ain, dqkvs = None, []
    for g, dil in enumerate(DILATIONS):
        dqkvs.append(_attn_bwd(qkvs[g], qkv_ts[g], do_parts[g], do_ts[g], aux_parts[g], tabs[g], dil,
                               name=f"attn_bwd_{g}"))
        g_ain = _mm(xs[g], dqkvs[g], "tn", name=f"attn_gin_{g}", m=d, n=gq, k=s, b_split=3, out_dtypes=(BF16,),
                    out_col_off=g * gt, out_cols=3 * gq, alias=g_ain)
    after = on_grads("attn_in", {"attn_w_in": g_ain})
    dx_parts = [_mm(dqkvs[g], w_ain, "nt", name=f"attn_dx_{g}", m=s, n=d, k=gq, a_split=3, b_k_off=g * gt,
                    deps=after if g == 0 else ())
                for g in range(len(DILATIONS))]
    return _dx_combine(du, dx_parts)


def _mesh_place():
    x, y, c = lax.axis_index("x"), lax.axis_index("y"), lax.axis_index("c")
    return x, y, c, 4 * x + 2 * y + c


def _peer(x, y, c, k):
    px = 1 - x if (k >> 2) & 1 else x
    py = 1 - y if (k >> 1) & 1 else y
    pc = 1 - c if k & 1 else c
    return (px, py, pc), 4 * px + 2 * py + pc


def _window(ref, axis, size, idx):
    if axis is None:
        return ref
    sl = [slice(None)] * len(ref.shape)
    sl[axis] = pl.ds(idx * size, size)
    return ref.at[tuple(sl)]


_HBM = pl.BlockSpec(memory_space=pltpu.HBM)
_SEM = pl.BlockSpec(memory_space=pltpu.SEMAPHORE)
_EFFECT = pltpu.SideEffectType.DATAFLOW_SIDE_EFFECTING


def _xchg_ends(src_ref, land_ref, axis, gather, me, other):
    if gather:
        size = src_ref.shape[axis]
        return src_ref, _window(land_ref, axis, size, me), _window(land_ref, axis, size, other)
    size = None if axis is None else src_ref.shape[axis] // N_DEV
    return _window(src_ref, axis, size, other), land_ref.at[me], land_ref.at[other]


def _xchg_start(srcs, lands, axes, *, gather, name, deps=(), peers=tuple(range(1, N_DEV))):
    n = len(srcs)
    nd = len(deps)

    def body(*refs):
        src_refs, land_refs = refs[:n], refs[n:2 * n]
        send, recv = refs[2 * n + nd:3 * n + nd], refs[3 * n + nd:4 * n + nd]
        token = refs[-1]
        x, y, c, me = _mesh_place()
        for k in peers:
            peer, pidx = _peer(x, y, c, k)
            for i in range(n):
                src, dst, _ = _xchg_ends(src_refs[i], land_refs[i], axes[i], gather, me, pidx)
                pltpu.make_async_remote_copy(
                    src_ref=src, dst_ref=dst, send_sem=send[i].at[k - 1], recv_sem=recv[i].at[k - 1],
                    device_id=peer, device_id_type=pl.DeviceIdType.MESH).start()
        for i in range(n):
            src, dst, _ = _xchg_ends(src_refs[i], land_refs[i], axes[i], gather, me, me)
            pltpu.make_async_copy(src, dst, send[i].at[N_DEV - 1]).start()
        token[...] = jnp.zeros_like(token)

    bufs = list(srcs) + list(lands)
    outs = pl.pallas_call(
        body, name=name,
        out_shape=[pltpu.SemaphoreType.DMA((N_DEV,))] * (2 * n) + [pltpu.HBM(b.shape, b.dtype) for b in bufs]
        + [jax.ShapeDtypeStruct((8, LANES), F32)],
        in_specs=[_HBM] * (2 * n) + [pl.BlockSpec(memory_space=pl.ANY)] * nd,
        out_specs=[_SEM] * (2 * n) + [_HBM] * (2 * n) + [pl.BlockSpec(memory_space=pltpu.VMEM)],
        input_output_aliases={i: 2 * n + i for i in range(2 * n)},
        compiler_params=pltpu.CompilerParams(has_side_effects=_EFFECT),
    )(*[pltpu.with_memory_space_constraint(b, pltpu.HBM) for b in bufs], *deps)
    return dict(send=outs[:n], recv=outs[n:2 * n], srcs=outs[2 * n:3 * n], lands=outs[3 * n:4 * n], token=outs[-1],
                axes=list(axes), gather=gather)


SIBLING = 1
SAME_CORE = (2, 4, 6)


def _gather_relay(xc, after, *, name):
    axis = xc["axes"][0]

    def body(src_ref, land_ref, send1, recv1, after_ref, src_out, land_out, send2, recv2):
        x, y, c, me = _mesh_place()
        sibling, _ = _peer(x, y, c, SIBLING)
        for j, k in enumerate(SAME_CORE):
            peer, pidx = _peer(x, y, c, k)
            src, _, got = _xchg_ends(src_ref, land_ref, axis, True, me, pidx)
            pltpu.make_async_remote_copy(
                src_ref=src, dst_ref=got, send_sem=send1.at[k - 1], recv_sem=recv1.at[k - 1],
                device_id=peer, device_id_type=pl.DeviceIdType.MESH).wait_recv()
            pltpu.make_async_remote_copy(
                src_ref=got, dst_ref=got, send_sem=send2.at[j], recv_sem=recv2.at[j],
                device_id=sibling, device_id_type=pl.DeviceIdType.MESH).start()

    src, land = xc["srcs"][0], xc["lands"][0]
    outs = pl.pallas_call(
        body, name=name,
        out_shape=[pltpu.HBM(src.shape, src.dtype), pltpu.HBM(land.shape, land.dtype)]
        + [pltpu.SemaphoreType.DMA((len(SAME_CORE),))] * 2,
        in_specs=[_HBM, _HBM, _SEM, _SEM, pl.BlockSpec(memory_space=pl.ANY)],
        out_specs=[_HBM, _HBM, _SEM, _SEM], input_output_aliases={0: 0, 1: 1},
        compiler_params=pltpu.CompilerParams(has_side_effects=_EFFECT),
    )(src, land, xc["send"][0], xc["recv"][0], after)
    return dict(src=outs[0], land=outs[1], send=outs[2], recv=outs[3])


def _gather_relay_wait(xc, relay, after, *, name):
    axis = xc["axes"][0]

    def body(src_ref, land_ref, send1, recv1, send2, recv2, after_ref, src_out, land_out):
        x, y, c, me = _mesh_place()
        sibling, sidx = _peer(x, y, c, SIBLING)
        for k in (SIBLING,) + SAME_CORE:
            peer, pidx = _peer(x, y, c, k)
            src, dst, got = _xchg_ends(src_ref, land_ref, axis, True, me, pidx)
            pltpu.make_async_remote_copy(
                src_ref=src, dst_ref=dst, send_sem=send1.at[k - 1], recv_sem=recv1.at[k - 1],
                device_id=peer, device_id_type=pl.DeviceIdType.MESH).wait_send()
        src, dst, got = _xchg_ends(src_ref, land_ref, axis, True, me, sidx)
        pltpu.make_async_remote_copy(
            src_ref=src, dst_ref=got, send_sem=send1.at[SIBLING - 1], recv_sem=recv1.at[SIBLING - 1],
            device_id=sibling, device_id_type=pl.DeviceIdType.MESH).wait_recv()
        src, dst, _ = _xchg_ends(src_ref, land_ref, axis, True, me, me)
        pltpu.make_async_copy(src, dst, send1.at[N_DEV - 1]).wait()
        for j, k in enumerate(SAME_CORE):
            _, pidx = _peer(x, y, c, k)
            _, qidx = _peer(x, y, c, k ^ SIBLING)
            _, _, sent = _xchg_ends(src_ref, land_ref, axis, True, me, pidx)
            _, _, got = _xchg_ends(src_ref, land_ref, axis, True, me, qidx)
            pltpu.make_async_remote_copy(
                src_ref=sent, dst_ref=sent, send_sem=send2.at[j], recv_sem=recv2.at[j],
                device_id=sibling, device_id_type=pl.DeviceIdType.MESH).wait_send()
            pltpu.make_async_remote_copy(
                src_ref=got, dst_ref=got, send_sem=send2.at[j], recv_sem=recv2.at[j],
                device_id=sibling, device_id_type=pl.DeviceIdType.MESH).wait_recv()

    outs = pl.pallas_call(
        body, name=name, out_shape=[pltpu.HBM(relay["src"].shape, relay["src"].dtype),
                                    pltpu.HBM(relay["land"].shape, relay["land"].dtype)],
        in_specs=[_HBM, _HBM, _SEM, _SEM, _SEM, _SEM, pl.BlockSpec(memory_space=pl.ANY)],
        out_specs=[_HBM, _HBM], input_output_aliases={0: 0, 1: 1},
        compiler_params=pltpu.CompilerParams(has_side_effects=_EFFECT),
    )(relay["src"], relay["land"], xc["send"][0], xc["recv"][0], relay["send"], relay["recv"], after)
    return outs[1]


def _xchg_wait(xc, items, after, *, name):
    m = len(items)
    gather = xc["gather"]
    axes = [xc["axes"][i] for i in items]

    def body(*refs):
        src_refs, land_refs = refs[:m], refs[m:2 * m]
        send, recv = refs[2 * m:3 * m], refs[3 * m:4 * m]
        x, y, c, me = _mesh_place()
        for k in range(1, N_DEV):
            peer, pidx = _peer(x, y, c, k)
            for j in range(m):
                src, dst, got = _xchg_ends(src_refs[j], land_refs[j], axes[j], gather, me, pidx)
                pltpu.make_async_remote_copy(
                    src_ref=src, dst_ref=dst, send_sem=send[j].at[k - 1], recv_sem=recv[j].at[k - 1],
                    device_id=peer, device_id_type=pl.DeviceIdType.MESH).wait_send()
                pltpu.make_async_remote_copy(
                    src_ref=src, dst_ref=got, send_sem=send[j].at[k - 1], recv_sem=recv[j].at[k - 1],
                    device_id=peer, device_id_type=pl.DeviceIdType.MESH).wait_recv()
        for j in range(m):
            src, dst, _ = _xchg_ends(src_refs[j], land_refs[j], axes[j], gather, me, me)
            pltpu.make_async_copy(src, dst, send[j].at[N_DEV - 1]).wait()

    bufs = [xc["srcs"][i] for i in items] + [xc["lands"][i] for i in items]
    sems = [xc["send"][i] for i in items] + [xc["recv"][i] for i in items]
    outs = pl.pallas_call(
        body, name=name, out_shape=[pltpu.HBM(b.shape, b.dtype) for b in bufs],
        in_specs=[_HBM] * (2 * m) + [_SEM] * (2 * m) + [pl.BlockSpec(memory_space=pl.ANY)],
        out_specs=[_HBM] * (2 * m), input_output_aliases={j: j for j in range(2 * m)},
        compiler_params=pltpu.CompilerParams(has_side_effects=_EFFECT),
    )(*bufs, *sems, after)
    return outs[m:]


def _cast_bf16(a, *, name):
    r, c = a.shape
    tr = min(r, 512)

    def body(a_ref, o_ref):
        o_ref[...] = a_ref[...].astype(BF16)

    spec = pl.BlockSpec((tr, c), lambda i: (i, 0))
    return pl.pallas_call(body, name=name, grid=(r // tr,), in_specs=[spec], out_specs=spec,
                          out_shape=jax.ShapeDtypeStruct((r, c), BF16), compiler_params=_params(("parallel",)))(a)


def _adamw(slabs, w, m, v, *, name):
    layers, r, c = w.shape
    tr = min(r, 256)

    def body(*refs):
        s_refs = refs[:layers]
        w_ref, m_ref, v_ref, g_ref, d_ref, mo_ref, vo_ref = refs[layers:]
        for l in range(layers):
            g = s_refs[l][0].astype(F32)
            for i in range(1, N_DEV):
                g = g + s_refs[l][i].astype(F32)
            m2 = ADAM_B1 * m_ref[l] + (1.0 - ADAM_B1) * g
            v2 = ADAM_B2 * v_ref[l] + (1.0 - ADAM_B2) * (g * g)
            m_hat = m2 / (1.0 - ADAM_B1 ** ADAM_STEP)
            v_hat = v2 / (1.0 - ADAM_B2 ** ADAM_STEP)
            g_ref[l] = g
            d_ref[l] = -ADAM_LR * (m_hat / (jnp.sqrt(v_hat) + ADAM_EPS) + ADAM_WD * w_ref[l])
            mo_ref[l] = m2
            vo_ref[l] = v2

    spec = pl.BlockSpec((layers, tr, c), lambda i: (0, i, 0))
    return pl.pallas_call(
        body, name=name, grid=(r // tr,),
        in_specs=[pl.BlockSpec((N_DEV, tr, c), lambda i: (0, i, 0))] * layers + [spec, spec, spec],
        out_specs=[spec] * 4, out_shape=[jax.ShapeDtypeStruct((layers, r, c), F32)] * 4,
        compiler_params=_params(("parallel",)),
    )(*slabs, w, m, v)


GATHER_AXIS = {"attn_w_in": 1, "attn_w_out": 0, "ffn_w_up0": 1, "ffn_w_down0": 0, "hgrn_w_in": 1, "hgrn_w_out": 0,
               "hgrn_norm_g": 1, "ffn_w_up1": 1, "ffn_w_down1": 0}
GATHER_STAGES = (("attn_w_in",), ("attn_w_out", "ffn_w_up0", "ffn_w_down0", "hgrn_w_in", "hgrn_w_out", "hgrn_norm_g"),
                 ("ffn_w_up1", "ffn_w_down1"))
GATHER_WAITS = ((("attn_w_in",), 0, 1), (("attn_w_out", "ffn_w_up0", "ffn_w_down0"), 1, 2),
                (("hgrn_w_in", "hgrn_w_out", "hgrn_norm_g"), 1, None), (("ffn_w_up1", "ffn_w_down1"), 2, None))
SCATTER_AXIS = dict(GATHER_AXIS, small=None)
BIG = ("attn_w_in", "attn_w_out", "hgrn_w_in", "hgrn_w_out", "ffn_w_up", "ffn_w_down")
SMALL = ("lb_logits", "ln_mix_g", "ln_mix_b", "ln_ffn_g", "ln_ffn_b")
SMALL_ROW = {"ln_mix_g": 0, "ln_mix_b": 2, "ln_ffn_g": 4, "ln_ffn_b": 6, "lb_logits": 8}
NORM_G_ROW = 10
LOSS_ROW = 11


def kernel(x, attn_w_in, attn_w_out, hgrn_w_in, hgrn_w_out, hgrn_norm_g, lb_logits, ln_mix_g, ln_mix_b, ln_ffn_g, ln_ffn_b, ffn_w_up, ffn_w_down, loss_target, m_attn_w_in, m_attn_w_out, m_hgrn_w_in, m_hgrn_w_out, m_hgrn_norm_g, m_lb_logits, m_ln_mix_g, m_ln_mix_b, m_ln_ffn_g, m_ln_ffn_b, m_ffn_w_up, m_ffn_w_down, v_attn_w_in, v_attn_w_out, v_hgrn_w_in, v_hgrn_w_out, v_hgrn_norm_g, v_lb_logits, v_ln_mix_g, v_ln_mix_b, v_ln_ffn_g, v_ln_ffn_b, v_ffn_w_up, v_ffn_w_down):
    wts = dict(attn_w_in=attn_w_in, attn_w_out=attn_w_out, hgrn_w_in=hgrn_w_in, hgrn_w_out=hgrn_w_out,
               hgrn_norm_g=hgrn_norm_g, lb_logits=lb_logits, ln_mix_g=ln_mix_g, ln_mix_b=ln_mix_b, ln_ffn_g=ln_ffn_g,
               ln_ffn_b=ln_ffn_b, ffn_w_up=ffn_w_up, ffn_w_down=ffn_w_down)
    mom = dict(attn_w_in=m_attn_w_in, attn_w_out=m_attn_w_out, hgrn_w_in=m_hgrn_w_in, hgrn_w_out=m_hgrn_w_out,
               hgrn_norm_g=m_hgrn_norm_g, lb_logits=m_lb_logits, ln_mix_g=m_ln_mix_g, ln_mix_b=m_ln_mix_b,
               ln_ffn_g=m_ln_ffn_g, ln_ffn_b=m_ln_ffn_b, ffn_w_up=m_ffn_w_up, ffn_w_down=m_ffn_w_down)
    vel = dict(attn_w_in=v_attn_w_in, attn_w_out=v_attn_w_out, hgrn_w_in=v_hgrn_w_in, hgrn_w_out=v_hgrn_w_out,
               hgrn_norm_g=v_hgrn_norm_g, lb_logits=v_lb_logits, ln_mix_g=v_ln_mix_g, ln_mix_b=v_ln_mix_b,
               ln_ffn_g=v_ln_ffn_g, ln_ffn_b=v_ln_ffn_b, ffn_w_up=v_ffn_w_up, ffn_w_down=v_ffn_w_down)
    me = 4 * lax.axis_index("x") + 2 * lax.axis_index("y") + lax.axis_index("c")

    src = {"attn_w_in": attn_w_in[0], "attn_w_out": attn_w_out[0], "hgrn_w_in": hgrn_w_in[0], "hgrn_w_out": hgrn_w_out[0],
           "ffn_w_up0": ffn_w_up[0], "ffn_w_down0": ffn_w_down[0], "ffn_w_up1": ffn_w_up[1], "ffn_w_down1": ffn_w_down[1]}
    gathers, got = {}, {}

    def start_gather(stage, deps):
        shards, lands = [], []
        for nm in GATHER_STAGES[stage]:
            sh = hgrn_norm_g if nm == "hgrn_norm_g" else _cast_bf16(src[nm], name=f"cast_{nm}")
            ax = GATHER_AXIS[nm]
            shape = list(sh.shape)
            shape[ax] *= N_DEV
            shards.append(sh)
            lands.append(lax.empty(tuple(shape), sh.dtype))
        peers = (SIBLING,) + SAME_CORE if stage == 0 else tuple(range(1, N_DEV))
        gathers[stage] = _xchg_start(shards, lands, [GATHER_AXIS[nm] for nm in GATHER_STAGES[stage]], gather=True,
                                     name=f"gather_start_{stage}", deps=deps, peers=peers)
        return [gathers[stage]["token"]]

    def get_w(name, after):
        deps = []
        if name not in got:
            group, stage, then = [w for w in GATHER_WAITS if name in w[0]][0]
            xc = gathers[stage]
            if stage == 0:
                relay = _gather_relay(xc, xc["token"], name="gather_relay")
                res = [_gather_relay_wait(xc, relay, xc["token"], name=f"gather_wait_{group[0]}")]
            else:
                res = _xchg_wait(xc, [GATHER_STAGES[stage].index(nm) for nm in group], after,
                                 name=f"gather_wait_{group[0]}")
            got.update(zip(group, res))
            if then is not None:
                deps = start_gather(then, [res[0]])
        return got[name], deps

    first = start_gather(0, [])

    scattered = {}

    def on_grads(tag, grads):
        gnames = list(grads)
        axes = [SCATTER_AXIS[nm] for nm in gnames]
        stacks = []
        for nm, ax in zip(gnames, axes):
            shape = list(grads[nm].shape)
            if ax is not None:
                shape[ax] //= N_DEV
            stacks.append(lax.empty((N_DEV, *shape), grads[nm].dtype))
        scattered[tag] = (gnames, _xchg_start([grads[nm] for nm in gnames], stacks, axes, gather=False,
                                              name=f"scatter_start_{tag}"))
        return [scattered[tag][1]["token"]]

    sm = jax.nn.softmax(lb_logits, axis=0)
    csum = jnp.cumsum(sm, axis=0)
    small = dict(lb=(csum - csum[0:1])[1:2], lb_logits=lb_logits, ln_mix_g=ln_mix_g, ln_mix_b=ln_mix_b,
                 ln_ffn_g=ln_ffn_g, ln_ffn_b=ln_ffn_b)
    grad_x = _local_step(x[0], loss_target[0], get_w, small, on_grads, deps=first)
    out = {}

    def stack_small(src_):
        rows = [None] * SMALL_ROWS
        for name in SMALL:
            rows[SMALL_ROW[name]], rows[SMALL_ROW[name] + 1] = src_[name][0:1], src_[name][1:2]
        zero = jnp.zeros((1, x.shape[-1]), F32)
        return jnp.concatenate([zero if r is None else r for r in rows], axis=0)[None]

    def update(name, slabs):
        shape = wts[name].shape
        out[name] = [r.reshape(shape) for r in _adamw(slabs, wts[name], mom[name], vel[name], name=f"adamw_{name}")]
        return out[name][0]

    slabs, after = {}, grad_x
    for tag, (gnames, xc) in scattered.items():
        slabs.update(zip(gnames, _xchg_wait(xc, list(range(len(gnames))), after, name=f"scatter_wait_{tag}")))
        if tag == "ffn1":
            continue
        if tag == "ffn0":
            update("ffn_w_down", [slabs["ffn_w_down0"], slabs["ffn_w_down1"]])
            after = update("ffn_w_up", [slabs["ffn_w_up0"], slabs["ffn_w_up1"]])
        elif tag == "hgrn":
            update("hgrn_w_out", [slabs["hgrn_w_out"]])
            after = update("hgrn_w_in", [slabs["hgrn_w_in"]])
        elif tag == "attn_out":
            after = update("attn_w_out", [slabs["attn_w_out"]])
        else:
            after = update("attn_w_in", [slabs["attn_w_in"]])
    res = _adamw([slabs["small"]], stack_small(wts), stack_small(mom), stack_small(vel), name="adamw_small")
    for name in SMALL:
        out[name] = [r[0, SMALL_ROW[name]:SMALL_ROW[name] + 2] for r in res]
    loss = res[0][0, LOSS_ROW, 0]
    ng = hgrn_norm_g.shape[-1]
    ng_slabs = lax.dynamic_slice(slabs["small"], (0, NORM_G_ROW, me * ng), (N_DEV, 1, ng))
    out["hgrn_norm_g"] = [r[0] for r in _adamw([ng_slabs], hgrn_norm_g[None], m_hgrn_norm_g[None],
                                                v_hgrn_norm_g[None], name="adamw_norm_g")]
    order =("attn_w_in", "attn_w_out", "hgrn_w_in", "hgrn_w_out", "hgrn_norm_g", "lb_logits", "ln_mix_g", "ln_mix_b",
             "ln_ffn_g", "ln_ffn_b", "ffn_w_up", "ffn_w_down")
    return (loss, grad_x[None], *[out[nm][0] for nm in order], *[out[nm][1] for nm in order],
            *[out[nm][2] for nm in order], *[out[nm][3] for nm in order])
```

```python
import jax
import jax.numpy as jnp
from jax import lax
from jax.experimental import pallas as pl
from jax.experimental.pallas import tpu as pltpu

F32 = jnp.float32
BF16 = jnp.bfloat16

N_DEV = 8
LANES = 128
D_MODEL = 1024
ATTN_HEAD_DIM = 64
ATTN_HEADS = 16
ATTN_BLK = 128
DILATIONS = (1, 4, 16)
ROPE_THETA = 10000.0
HGRN_HEADS = 8
HGRN_CHUNK = 64
D_FF = 4096
LN_EPS = 1e-5
RMS_EPS = 1e-6
DEPTH = 2
ALPHA = (2 * DEPTH) ** 0.25
ADAM_LR, ADAM_B1, ADAM_B2, ADAM_EPS, ADAM_WD, ADAM_STEP = 0.001, 0.9, 0.999, 1e-08, 0.01, 10
VMEM_LIMIT = 48 * 1024 * 1024

_NT = (((1,), (1,)), ((), ()))
_TN = (((0,), (0,)), ((), ()))


def _dot(a, b):
    return jnp.dot(a, b, preferred_element_type=F32)


def _dot_nt(a, b):
    return lax.dot_general(a, b, _NT, preferred_element_type=F32)


def _dot_tn(a, b):
    return lax.dot_general(a, b, _TN, preferred_element_type=F32)


def _split3(x):
    p1 = x.astype(BF16)
    r1 = x - p1.astype(F32)
    p2 = r1.astype(BF16)
    p3 = (r1 - p2.astype(F32)).astype(BF16)
    return p1, p2, p3


def _exact_dot(sel3, x):
    return _dot(sel3, jnp.concatenate(_split3(x), axis=0))


def _exact_dot_r(x, sel3):
    return _dot(jnp.concatenate(_split3(x), axis=1), sel3)


def _params(sem=None):
    return pltpu.CompilerParams(dimension_semantics=sem, vmem_limit_bytes=VMEM_LIMIT)


def _mm(a, b, mode, *, name, m, n, k, tm=1024, tn=1024, tk=1024, out_dtypes=(F32,), epi=None, a_pre=None,
        tile_extras=(), row_extras=(), vec_extras=(), row_outs=(), vec_outs=0, a_split=1, b_split=1, out_split=1,
        b_col_off=0, b_k_off=0, out_col_off=0, out_cols=None, alias=None, epi_wants_j=False, deps=(), t_out=False):
    tm, tn, tk = min(tm, m), min(tn, n), min(tk, k)
    assert m % tm == 0 and n % tn == 0 and k % tk == 0, (name, m, n, k, tm, tn, tk)
    gm, gn, gk = m // tm, n // tn, k // tk
    if mode in ("nn", "nt"):
        if a_split > 1:
            kc = (k // a_split) // tk
            a_spec = pl.BlockSpec((None, tm, tk), lambda i, j, kk: (kk // kc, i, kk % kc))
        else:
            a_spec = pl.BlockSpec((tm, tk), lambda i, j, kk: (i, kk))
    else:
        a_spec = pl.BlockSpec((tk, tm), lambda i, j, kk: (kk, i))
    if mode in ("nn", "tn"):
        if b_split > 1:
            nc = (n // b_split) // tn
            b_spec = pl.BlockSpec((None, tk, tn), lambda i, j, kk: (j // nc, kk, j % nc))
        else:
            b_spec = pl.BlockSpec((tk, tn), lambda i, j, kk: (kk + b_k_off, j + b_col_off))
    else:
        b_spec = pl.BlockSpec((tn, tk), lambda i, j, kk: (j + b_col_off, kk + b_k_off))
    if out_split > 1:
        nco = (n // out_split) // tn
        o_spec = pl.BlockSpec((None, tm, tn), lambda i, j, kk: (j // nco, i, j % nco))
        o_shape = (out_split, m, n // out_split)
    else:
        o_spec = pl.BlockSpec((tm, tn), lambda i, j, kk: (i, j + out_col_off))
        o_shape = (m, out_cols if out_cols is not None else n)
    n_ex = len(tile_extras) + len(row_extras) + len(vec_extras)
    n_out = len(out_dtypes)
    n_plain = n_out + len(row_outs)
    assert not vec_outs or gn == 1
    if epi is None:
        def epi(acc):
            return (acc,)
    dot = {"nn": _dot, "nt": _dot_nt, "tn": _dot_tn}[mode]

    def body(*refs):
        a_ref, b_ref = refs[0], refs[1]
        ex = refs[2:2 + n_ex]
        outs = refs[2 + n_ex + (1 if alias is not None else 0) + len(deps):][:n_plain + vec_outs + (1 if t_out else 0)]
        ii = pl.program_id(0)
        jj = pl.program_id(1)

        def product():
            av = a_ref[...]
            if a_pre is not None:
                av = a_pre(av)
            return dot(av.astype(BF16), b_ref[...].astype(BF16))

        def finish(total):
            lead = (jj,) if epi_wants_j else ()
            res = epi(*lead, total, *[e[...] for e in ex])
            for o, r in zip(outs[:n_plain], res):
                o[...] = r.astype(o.dtype)
            for o, r in zip(outs[n_plain:n_plain + vec_outs], res[n_plain:]):
                @pl.when(ii == 0)
                def _(o=o, r=r):
                    o[...] = r

                @pl.when(ii > 0)
                def _(o=o, r=r):
                    o[...] += r
            if t_out:
                outs[-1][...] = res[0].astype(outs[-1].dtype).T

        if gk == 1:
            finish(product())
        else:
            acc = refs[-1]
            kk = pl.program_id(2)

            @pl.when(kk == 0)
            def _():
                acc[...] = product()

            @pl.when(kk > 0)
            def _():
                acc[...] += product()

            @pl.when(kk == gk - 1)
            def _():
                finish(acc[...])

    in_specs = [a_spec, b_spec] + [o_spec] * len(tile_extras)
    in_specs += [pl.BlockSpec((tm, r.shape[1]), lambda i, j, kk: (i, 0)) for r in row_extras]
    in_specs += [pl.BlockSpec((1, tn), lambda i, j, kk: (0, j))] * len(vec_extras)
    args = [a, b] + list(tile_extras) + list(row_extras) + list(vec_extras)
    io_alias = {}
    if alias is not None:
        in_specs.append(pl.BlockSpec(memory_space=pl.ANY))
        args.append(alias)
        io_alias = {len(args) - 1: 0}
    in_specs += [pl.BlockSpec(memory_space=pl.ANY)] * len(deps)
    args += list(deps)
    out_specs = [o_spec] * n_out
    out_shape = [jax.ShapeDtypeStruct(o_shape, dt) for dt in out_dtypes]
    for dt, w in row_outs:
        out_specs.append(pl.BlockSpec((tm, w), lambda i, j, kk: (i, 0)))
        out_shape.append(jax.ShapeDtypeStruct((m, w), dt))
    out_specs += [pl.BlockSpec((1, tn), lambda i, j, kk: (0, j))] * vec_outs
    out_shape += [jax.ShapeDtypeStruct((1, n), F32)] * vec_outs
    if t_out:
        assert out_split > 1
        out_specs.append(pl.BlockSpec((None, tn, tm), lambda i, j, kk: (j // nco, j % nco, i)))
        out_shape.append(jax.ShapeDtypeStruct((out_split, n // out_split, m), out_dtypes[0]))
    out = pl.pallas_call(
        body, name=name, grid=(gm, gn, gk), in_specs=in_specs, out_specs=out_specs, out_shape=out_shape,
        scratch_shapes=[pltpu.VMEM((tm, tn), F32)] if gk > 1 else [],
        input_output_aliases=io_alias,
        compiler_params=_params(("arbitrary" if vec_outs else "parallel", "parallel", "arbitrary")),
    )(*args)
    return out[0] if len(out) == 1 else out


def _rope_tables(seq, dil):
    pos = jnp.arange(seq, dtype=jnp.int32).reshape(seq // dil, dil).T.reshape(seq)
    half = ATTN_HEAD_DIM // 2
    inv = ROPE_THETA ** (-jnp.arange(half, dtype=F32) * (2.0 / ATTN_HEAD_DIM))
    ang = pos.astype(F32)[:, None] * inv[None, :]
    cos, sin = jnp.cos(ang), jnp.sin(ang)
    reps = LANES // ATTN_HEAD_DIM
    return (jnp.tile(jnp.concatenate([cos, cos], axis=1), (1, reps)),
            jnp.tile(jnp.concatenate([-sin, sin], axis=1), (1, reps)))


def _rotate(x, c, ss, sign=1.0):
    w = x.shape[-1]
    half = ATTN_HEAD_DIM // 2
    lane = lax.broadcasted_iota(jnp.int32, x.shape, 1)
    first = (lane % ATTN_HEAD_DIM) < half
    partner = jnp.where(first, pltpu.roll(x, w - half, 1), pltpu.roll(x, half, 1))
    reps = w // LANES
    if reps > 1:
        c = jnp.concatenate([c] * reps, axis=1)
        ss = jnp.concatenate([ss] * reps, axis=1)
    return x * c + sign * (partner * ss)


ROW_TILE = 512


def _ln_epi(acc, x, g, b):
    u = ALPHA * x + acc
    mu = jnp.mean(u, axis=-1, keepdims=True)
    uc = u - mu
    var = jnp.mean(uc * uc, axis=-1, keepdims=True)
    rstd = lax.rsqrt(var + LN_EPS)
    xh = uc * rstd
    out = xh * g + b
    return out, out, xh, rstd


def _ln_grad(dy, xh, rstd, g):
    dxh = dy * g
    m1 = jnp.mean(dxh, axis=-1, keepdims=True)
    m2 = jnp.mean(dxh * xh, axis=-1, keepdims=True)
    du = rstd * (dxh - m1 - xh * m2)
    return du, du, jnp.sum(dy * xh, axis=0, keepdims=True), jnp.sum(dy, axis=0, keepdims=True)


def _ln_grad_epi(acc, du_next, xh, rstd, g):
    return _ln_grad(acc + ALPHA * du_next, xh, rstd, g)


def _loss_ln_bwd(y, target, xhat, rstd, g):
    s, d = y.shape

    def body(y_ref, t_ref, xh_ref, r_ref, g_ref, du_ref, dub_ref, dg_ref, db_ref, sq_ref):
        @pl.when(pl.program_id(0) == 0)
        def _():
            dg_ref[...] = jnp.zeros_like(dg_ref)
            db_ref[...] = jnp.zeros_like(db_ref)
            sq_ref[...] = jnp.zeros_like(sq_ref)

        e = y_ref[...] - t_ref[...]
        du, _, dg, db = _ln_grad(e * (1.0 / d), xh_ref[...], r_ref[...], g_ref[...])
        du_ref[...] = du
        dub_ref[...] = du.astype(BF16)
        dg_ref[...] += dg
        db_ref[...] += db
        sq_ref[...] += jnp.sum(e * e, axis=0, keepdims=True)

    row = pl.BlockSpec((ROW_TILE, d), lambda i: (i, 0))
    vec = pl.BlockSpec((1, d), lambda i: (0, 0))
    return pl.pallas_call(
        body, name="loss_ln_bwd", grid=(s // ROW_TILE,),
        in_specs=[row, row, row, pl.BlockSpec((ROW_TILE, 1), lambda i: (i, 0)), vec],
        out_specs=[row, row, vec, vec, vec],
        out_shape=[jax.ShapeDtypeStruct((s, d), F32), jax.ShapeDtypeStruct((s, d), BF16)]
        + [jax.ShapeDtypeStruct((1, d), F32)] * 3,
        compiler_params=_params(("arbitrary",)),
    )(y, target, xhat, rstd, g)


POS_BLK = 2048


def _class_rows(r, dil):
    return pl.ds(r, POS_BLK // dil, stride=dil) if dil > 1 else pl.ds(0, POS_BLK)


def _class_view(a, dil):
    s, w = a.shape
    return a.reshape(dil, s // dil, w)


def _class_spec(dil, all_tiles=True):
    if all_tiles:
        return pl.BlockSpec((dil, POS_BLK // dil, LANES), lambda i, t: (0, i, t))
    return pl.BlockSpec((dil, POS_BLK // dil, LANES), lambda i, t: (0, i, 0))


def _pos_spec(all_tiles=True):
    if all_tiles:
        return pl.BlockSpec((POS_BLK, LANES), lambda i, t: (i, t))
    return pl.BlockSpec((POS_BLK, LANES), lambda i, t: (i, 0))


def _prep_x(x, deps=()):
    s, d = x.shape

    def body(x_ref, *refs):
        outs = refs[len(deps):]
        for dil, o_ref in zip(DILATIONS, outs):
            for r in range(dil):
                o_ref[r] = x_ref[_class_rows(r, dil), :].astype(BF16)

    outs = pl.pallas_call(
        body, name="prep_x", grid=(s // POS_BLK, d // LANES),
        in_specs=[_pos_spec()] + [pl.BlockSpec(memory_space=pl.ANY)] * len(deps),
        out_specs=[_class_spec(dil) for dil in DILATIONS],
        out_shape=[jax.ShapeDtypeStruct((dil, s // dil, d), BF16) for dil in DILATIONS],
        compiler_params=_params(("parallel", "parallel")),
    )(x, *deps)
    return [o.reshape(s, d) for o in outs]


def _head_expand_matrix():
    h = lax.broadcasted_iota(jnp.int32, (LANES, D_MODEL), 0)
    l = lax.broadcasted_iota(jnp.int32, (LANES, D_MODEL), 1)
    return (l // ATTN_HEAD_DIM == h).astype(BF16)


def _attn_fwd(qkv, dil, *, name):
    _, s, d = qkv.shape
    nq = s // ATTN_BLK
    per = nq // dil
    scale = ATTN_HEAD_DIM ** -0.5
    tiles = d // LANES

    def body(q_ref, kc_ref, kp_ref, vc_ref, vp_ref, o_ref, lse_ref):
        qb = pl.program_id(0)
        first = (qb % per) == 0
        qi = lax.broadcasted_iota(jnp.int32, (ATTN_BLK, 2 * ATTN_BLK), 0)
        kj = lax.broadcasted_iota(jnp.int32, (ATTN_BLK, 2 * ATTN_BLK), 1)
        dist = qi + ATTN_BLK - kj
        valid = (dist >= 0) & (dist <= ATTN_BLK) & ((kj >= ATTN_BLK) | jnp.logical_not(first))
        lane = lax.broadcasted_iota(jnp.int32, (ATTN_BLK, LANES), 1)
        lse_tile = jnp.zeros((ATTN_BLK, LANES), F32)
        zero = jnp.zeros((), BF16)
        in_head = [(lane // ATTN_HEAD_DIM) == hh for hh in range(2)]
        for t0 in range(0, tiles, ATTN_TILE_GROUP):
            group = range(t0, t0 + ATTN_TILE_GROUP)
            heads = [(t, hh) for t in group for hh in range(2)]
            cols = {t: pl.ds(t * LANES, LANES) for t in group}
            k2 = {t: jnp.concatenate([kp_ref[:, cols[t]], kc_ref[:, cols[t]]], axis=0) for t in group}
            v2 = {t: jnp.concatenate([vp_ref[:, cols[t]], vc_ref[:, cols[t]]], axis=0) for t in group}
            sc = {(t, hh): jnp.where(valid, _dot_nt(jnp.where(in_head[hh], q_ref[:, cols[t]], zero), k2[t]) * scale,
                                     -jnp.inf) for t, hh in heads}
            mx = {i: jnp.max(sc[i], axis=-1, keepdims=True) for i in heads}
            p = {i: jnp.exp(sc[i] - mx[i]) for i in heads}
            l = {i: jnp.sum(p[i], axis=-1, keepdims=True) for i in heads}
            oh = {i: _dot(p[i].astype(BF16), v2[i[0]]) / l[i] for i in heads}
            for t in group:
                o_ref[:, cols[t]] = jnp.where(in_head[0], oh[t, 0], oh[t, 1])
                for hh in range(2):
                    lse_tile = jnp.where(lane == 2 * t + hh, mx[t, hh] + jnp.log(l[t, hh]), lse_tile)
        lse_ref[...] = lse_tile

    def blk(piece, prev):
        if prev:
            return pl.BlockSpec((None, ATTN_BLK, d), lambda i: (piece, jnp.maximum(i - 1, 0), 0))
        return pl.BlockSpec((None, ATTN_BLK, d), lambda i: (piece, i, 0))

    return pl.pallas_call(
        body, name=name, grid=(nq,),
        in_specs=[blk(0, False), blk(1, False), blk(1, True), blk(2, False), blk(2, True)],
        out_specs=[pl.BlockSpec((ATTN_BLK, d), lambda i: (i, 0)), pl.BlockSpec((ATTN_BLK, LANES), lambda i: (i, 0))],
        out_shape=[jax.ShapeDtypeStruct((s, d), F32), jax.ShapeDtypeStruct((s, LANES), F32)],
        compiler_params=_params(("parallel",)),
    )(qkv, qkv, qkv, qkv, qkv)


def _attn_combine(os_, lses):
    s, d = os_[0].shape
    sel = jnp.concatenate([_head_expand_matrix()] * 3, axis=0)

    def body(o0, o1, o2, l0, l1, l2, sel_ref, of_ref, ob_ref, lt_ref, o_pos, l_pos):
        for g, (dil, o_ref, l_ref) in enumerate(zip(DILATIONS, (o0, o1, o2), (l0, l1, l2))):
            for r in range(dil):
                o_pos[g, _class_rows(r, dil), :] = o_ref[r]
                l_pos[g, _class_rows(r, dil), :] = l_ref[r]
        la, lb_, lc = l_pos[0], l_pos[1], l_pos[2]
        mx = jnp.maximum(jnp.maximum(la, lb_), lc)
        es = (jnp.exp(la - mx), jnp.exp(lb_ - mx), jnp.exp(lc - mx))
        z = es[0] + es[1] + es[2]
        lt_ref[...] = mx + jnp.log(z)
        acc = jnp.zeros((POS_BLK, LANES), F32)
        for g in range(3):
            acc += _exact_dot_r(es[g] / z, sel_ref[...]) * o_pos[g]
        of_ref[...] = acc
        ob_ref[...] = acc.astype(BF16)

    return pl.pallas_call(
        body, name="attn_combine", grid=(s // POS_BLK, d // LANES),
        in_specs=[_class_spec(dil) for dil in DILATIONS] + [_class_spec(dil, False) for dil in DILATIONS]
        + [pl.BlockSpec((3 * LANES, LANES), lambda i, t: (0, t))],
        out_specs=[_pos_spec(), _pos_spec(), _pos_spec(False)],
        out_shape=[jax.ShapeDtypeStruct((s, d), F32), jax.ShapeDtypeStruct((s, d), BF16),
                   jax.ShapeDtypeStruct((s, LANES), F32)],
        scratch_shapes=[pltpu.VMEM((3, POS_BLK, LANES), F32), pltpu.VMEM((3, POS_BLK, LANES), F32)],
        compiler_params=_params(("parallel", "arbitrary")),
    )(*[_class_view(o, dil) for o, dil in zip(os_, DILATIONS)],
      *[_class_view(l, dil) for l, dil in zip(lses, DILATIONS)], sel)


ATTN_TILE_GROUP = 2
AUX_PER_TILE = 12


def _aux_placement():
    h = lax.broadcasted_iota(jnp.int32, (6, LANES, LANES), 1)
    l = lax.broadcasted_iota(jnp.int32, (6, LANES, LANES), 2)
    j = lax.broadcasted_iota(jnp.int32, (6, LANES, LANES), 0)
    target = AUX_PER_TILE * (h // 2) + 3 * (h % 2) + jnp.where(j < 3, j, 3 + j)
    return ((l == target) & (h < ATTN_HEADS)).astype(BF16)


def _attn_bwd_prep(do, o, lse):
    s, d = do.shape
    tiles = d // LANES
    sel_t = jnp.tile(_head_expand_matrix().T.reshape(tiles, LANES, LANES), (1, 3, 1))

    def body(do_ref, o_ref, l_ref, sel_ref, place_ref, *refs):
        outs, delta, aux = refs[:9], refs[9], refs[10]
        t = pl.program_id(1)
        part = _exact_dot_r(do_ref[...] * o_ref[...], sel_ref[...])

        @pl.when(t == 0)
        def _():
            delta[...] = part

        @pl.when(t > 0)
        def _():
            delta[...] += part

        for g, dil in enumerate(DILATIONS):
            for r in range(dil):
                blk = do_ref[_class_rows(r, dil), :].astype(BF16)
                outs[g][r] = blk
                outs[3 + g][r] = blk.T

        @pl.when(t == tiles - 1)
        def _():
            pieces = _split3(l_ref[...]) + _split3(delta[...])
            aux[...] = _dot(jnp.concatenate(pieces, axis=1), place_ref[...])
            for g, dil in enumerate(DILATIONS):
                for r in range(dil):
                    outs[6 + g][r] = aux[_class_rows(r, dil), :].astype(BF16)

    outs = pl.pallas_call(
        body, name="attn_bwd_prep", grid=(s // POS_BLK, tiles),
        in_specs=[_pos_spec(), _pos_spec(), _pos_spec(False),
                  pl.BlockSpec((None, 3 * LANES, LANES), lambda i, t: (t, 0, 0)),
                  pl.BlockSpec((6 * LANES, LANES), lambda i, t: (0, 0))],
        out_specs=[_class_spec(dil) for dil in DILATIONS]
        + [pl.BlockSpec((dil, LANES, POS_BLK // dil), lambda i, t: (0, t, i)) for dil in DILATIONS]
        + [_class_spec(dil, False) for dil in DILATIONS],
        out_shape=[jax.ShapeDtypeStruct((dil, s // dil, d), BF16) for dil in DILATIONS]
        + [jax.ShapeDtypeStruct((dil, d, s // dil), BF16) for dil in DILATIONS]
        + [jax.ShapeDtypeStruct((dil, s // dil, LANES), BF16) for dil in DILATIONS],
        scratch_shapes=[pltpu.VMEM((POS_BLK, LANES), F32), pltpu.VMEM((POS_BLK, LANES), F32)],
        compiler_params=_params(("parallel", "arbitrary")),
    )(do, o, lse, sel_t, _aux_placement().reshape(6 * LANES, LANES))
    return ([a.reshape(s, d) for a in outs[0:3]], list(outs[3:6]), [a.reshape(s, LANES) for a in outs[6:9]])


def _attn_bwd(qkv, qkv_t, do, do_t, aux, tables, dil, *, name):
    _, s, d = qkv.shape
    nq = s // ATTN_BLK
    per = nq // dil
    scale = ATTN_HEAD_DIM ** -0.5
    assert scale == 0.125
    tiles = d // LANES
    half = ATTN_HEAD_DIM

    def body(qd_ref, qo_ref, k_ref, qtd_ref, qto_ref, kt_ref, vt_ref, dod_ref, doo_ref, dotd_ref, doto_ref,
             auxd_ref, auxo_ref, c_ref, ss_ref, out_ref, carry):
        kb = pl.program_id(0)

        @pl.when(kb == 0)
        def _():
            carry[...] = jnp.zeros_like(carry)

        has_next = (kb % per) != (per - 1)
        qi = lax.broadcasted_iota(jnp.int32, (ATTN_BLK, 2 * ATTN_BLK), 0)
        kj = lax.broadcasted_iota(jnp.int32, (ATTN_BLK, 2 * ATTN_BLK), 1) % ATTN_BLK
        valid_d = kj <= qi
        valid_o = (kj >= qi) & has_next
        lane = lax.broadcasted_iota(jnp.int32, (ATTN_BLK, LANES), 1)
        row = lax.broadcasted_iota(jnp.int32, (LANES, ATTN_BLK), 0)
        side = lax.broadcasted_iota(jnp.int32, (LANES, 2 * ATTN_BLK), 0)
        first = lax.broadcasted_iota(jnp.int32, (LANES, 2 * ATTN_BLK), 1) < ATTN_BLK
        c, ss = c_ref[...], ss_ref[...]
        zero = jnp.zeros((), BF16)
        sides = ((qd_ref, qtd_ref, dod_ref, dotd_ref, auxd_ref[...], valid_d),
                 (qo_ref, qto_ref, doo_ref, doto_ref, auxo_ref[...], valid_o))
        both = (0, 1)

        def head_halves(x, index):
            axis = 0 if index is lane else 1
            return jnp.concatenate([jnp.where(index < half, x, zero), jnp.where(index >= half, x, zero)], axis=axis)

        for t0 in range(0, tiles, ATTN_TILE_GROUP):
            group = range(t0, t0 + ATTN_TILE_GROUP)
            cols = {t: pl.ds(t * LANES, LANES) for t in group}
            kk, kk_t, vv_t = {}, {}, {}
            for t in group:
                base = AUX_PER_TILE * t
                hit = lambda lo: ((first & (side >= base + lo) & (side < base + lo + 3))
                                  | (jnp.logical_not(first) & (side >= base + lo + 3) & (side < base + lo + 6)))
                kk[t] = head_halves(k_ref[:, cols[t]], lane)
                kk_t[t] = jnp.concatenate([head_halves(kt_ref[cols[t], :], row),
                                           jnp.where(hit(0), -1.0 / scale, 0.0).astype(BF16)], axis=0)
                vv_t[t] = jnp.concatenate([head_halves(vt_ref[cols[t], :], row),
                                           jnp.where(hit(6), -1.0, 0.0).astype(BF16)], axis=0)
            sc = {(t, w): _dot(jnp.concatenate([sides[w][0][:, cols[t]], sides[w][4]], axis=1), kk_t[t])
                  for t in group for w in both}
            dpd = {(t, w): _dot(jnp.concatenate([sides[w][2][:, cols[t]], sides[w][4]], axis=1), vv_t[t])
                   for t in group for w in both}
            p = {i: jnp.where(sides[i[1]][5], jnp.exp(sc[i] * scale), 0.0) for i in sc}
            ds = {i: (p[i] * dpd[i] * scale).astype(BF16) for i in sc}
            pb = {i: p[i].astype(BF16) for i in sc}
            dv_t = {t: sum(_dot(sides[w][3][cols[t], :], pb[t, w]) for w in both) for t in group}
            dk_t = {t: sum(_dot(sides[w][1][cols[t], :], ds[t, w]) for w in both) for t in group}
            dq = {i: _dot(ds[i], kk[i[0]]) for i in sc}
            for t in group:
                dq_now = carry[:, cols[t]] + dq[t, 0]
                carry[:, cols[t]] = dq[t, 1]
                dk = jnp.where(row < half, dk_t[t][:, :ATTN_BLK], dk_t[t][:, ATTN_BLK:]).T
                dv = jnp.where(row < half, dv_t[t][:, :ATTN_BLK], dv_t[t][:, ATTN_BLK:]).T
                out_ref[0, :, cols[t]] = _rotate(dq_now, c, ss, -1.0).astype(BF16)
                out_ref[1, :, cols[t]] = _rotate(dk, c, ss, -1.0).astype(BF16)
                out_ref[2, :, cols[t]] = dv.astype(BF16)

    def nxt(i):
        return jnp.minimum(i + 1, nq - 1)

    def piece(p, shift):
        if shift:
            return pl.BlockSpec((None, ATTN_BLK, d), lambda i: (p, nxt(i), 0))
        return pl.BlockSpec((None, ATTN_BLK, d), lambda i: (p, i, 0))

    def piece_t(p, shift):
        if shift:
            return pl.BlockSpec((None, d, ATTN_BLK), lambda i: (p, 0, nxt(i)))
        return pl.BlockSpec((None, d, ATTN_BLK), lambda i: (p, 0, i))

    def rows(width, shift):
        if shift:
            return pl.BlockSpec((ATTN_BLK, width), lambda i: (nxt(i), 0))
        return pl.BlockSpec((ATTN_BLK, width), lambda i: (i, 0))

    def do_t_spec(shift):
        if shift:
            return pl.BlockSpec((None, d, ATTN_BLK), lambda i: (nxt(i) // per, 0, nxt(i) % per))
        return pl.BlockSpec((None, d, ATTN_BLK), lambda i: (i // per, 0, i % per))

    return pl.pallas_call(
        body, name=name, grid=(nq,),
        in_specs=[piece(0, False), piece(0, True), piece(1, False),
                  piece_t(0, False), piece_t(0, True), piece_t(1, False), piece_t(2, False),
                  rows(d, False), rows(d, True), do_t_spec(False), do_t_spec(True),
                  rows(LANES, False), rows(LANES, True), rows(LANES, False), rows(LANES, False)],
        out_specs=pl.BlockSpec((3, ATTN_BLK, d), lambda i: (0, i, 0)),
        out_shape=jax.ShapeDtypeStruct((3, s, d), BF16),
        scratch_shapes=[pltpu.VMEM((ATTN_BLK, d), F32)],
        compiler_params=_params(("arbitrary",)),
    )(qkv, qkv, qkv, qkv_t, qkv_t, qkv_t, qkv_t, do, do, do_t, do_t, aux, aux, *tables)


def _dx_combine(du, parts):
    s, d = du.shape

    def body(du_ref, p0, p1, p2, out_ref):
        out_ref[...] = ALPHA * du_ref[...] + p0[0]
        for dil, p_ref in zip(DILATIONS[1:], (p1, p2)):
            for r in range(dil):
                out_ref[_class_rows(r, dil), :] += p_ref[r]

    return pl.pallas_call(
        body, name="dx_combine", grid=(s // POS_BLK, d // LANES),
        in_specs=[_pos_spec()] + [_class_spec(dil) for dil in DILATIONS], out_specs=_pos_spec(),
        out_shape=jax.ShapeDtypeStruct((s, d), F32),
        compiler_params=_params(("parallel", "parallel")),
    )(du, *[_class_view(p, dil) for p, dil in zip(parts, DILATIONS)])


HGRN_ROWS = 1024


def _tri(lower, copies=1):
    i = lax.broadcasted_iota(jnp.int32, (HGRN_CHUNK, HGRN_CHUNK * copies), 0)
    j = lax.broadcasted_iota(jnp.int32, (HGRN_CHUNK, HGRN_CHUNK * copies), 1) % HGRN_CHUNK
    return (j <= i) if lower else (j >= i)


def _hgrn_gates(qr, z, lb):
    sq = jax.nn.sigmoid(qr)
    e = jnp.exp(-jnp.abs(z))
    big = 1.0 / (1.0 + e)
    small = e * big
    sg = jnp.where(z >= 0, big, small)
    sn = jnp.where(z >= 0, small, big)
    return sq, qr * sq, sg, sn, lb + (1.0 - lb) * sg, (1.0 - lb) * sn


def _hgrn_fwd(p2, lb, norm_g):
    _, s, d = p2.shape
    nblk = s // HGRN_ROWS
    cps = HGRN_ROWS // HGRN_CHUNK

    def body(p_ref, lb_ref, g_ref, o_ref, on_ref, st_ref, state):
        @pl.when(pl.program_id(1) == 0)
        def _():
            state[...] = jnp.zeros_like(state)

        chunks = [pl.ds(c * HGRN_CHUNK, HGRN_CHUNK) for c in range(cps)]
        ltri = _tri(True)
        lsel = _tri(True, 3).astype(BF16)
        _, q, _, _, f, key = _hgrn_gates(p_ref[0], p_ref[1], lb_ref[...])
        lf = jnp.log(f)
        v = [p_ref[2, rows, :].astype(BF16) for rows in chunks]
        b = [_exact_dot(lsel, lf[c * HGRN_CHUNK:(c + 1) * HGRN_CHUNK]) for c in range(cps)]
        b_last = [bc[HGRN_CHUNK - 1:HGRN_CHUNK, :] for bc in b]
        q = [q[c * HGRN_CHUNK:(c + 1) * HGRN_CHUNK] for c in range(cps)]
        key = [key[c * HGRN_CHUNK:(c + 1) * HGRN_CHUNK] for c in range(cps)]
        qd = [(q[c] * jnp.exp(b[c])).astype(BF16) for c in range(cps)]
        kd = [(key[c] * jnp.exp(-b[c])).astype(BF16) for c in range(cps)]
        k2 = [(key[c] * jnp.exp(b_last[c] - b[c])).astype(BF16) for c in range(cps)]
        a = [jnp.where(ltri, _dot_nt(qd[c], kd[c]), 0.0).astype(BF16) for c in range(cps)]
        kv = [_dot_tn(v[c], k2[c]) for c in range(cps)]
        st, sts = state[...], []
        for c in range(cps):
            sts.append(st)
            st_ref[0, c] = st
            st = st * jnp.exp(b_last[c]) + kv[c]
        state[...] = st
        gv = g_ref[...]
        for c in range(cps):
            o = _dot(a[c], v[c]) + _dot_nt(qd[c], sts[c].astype(BF16))
            o_ref[chunks[c], :] = o
            r = lax.rsqrt(jnp.mean(o * o, axis=-1, keepdims=True) + RMS_EPS)
            on_ref[chunks[c], :] = (o * r * gv).astype(BF16)

    vec = pl.BlockSpec((1, LANES), lambda h, c: (0, h))
    col = pl.BlockSpec((HGRN_ROWS, LANES), lambda h, c: (c, h))
    return pl.pallas_call(
        body, name="hgrn_fwd", grid=(HGRN_HEADS, nblk),
        in_specs=[pl.BlockSpec((3, HGRN_ROWS, LANES), lambda h, c: (0, c, h)), vec, vec],
        out_specs=[col, col, pl.BlockSpec((1, cps, LANES, LANES), lambda h, c: (h, c, 0, 0))],
        out_shape=[jax.ShapeDtypeStruct((s, d), F32), jax.ShapeDtypeStruct((s, d), BF16),
                   jax.ShapeDtypeStruct((HGRN_HEADS, s // HGRN_CHUNK, LANES, LANES), F32)],
        scratch_shapes=[pltpu.VMEM((LANES, LANES), F32)],
        compiler_params=_params(("parallel", "arbitrary")),
    )(p2, lb, norm_g)


def _hgrn_bwd(p2, lb, norm_g, o_raw, states, dyn):
    _, s, d = p2.shape
    nblk = s // HGRN_ROWS
    cps = HGRN_ROWS // HGRN_CHUNK

    def body(p_ref, lb_ref, g_ref, o_ref, st_ref, dy_ref, dp_ref, dg_ref, dlb_ref, dstate):
        @pl.when(pl.program_id(1) == 0)
        def _():
            dstate[...] = jnp.zeros_like(dstate)
            dg_ref[...] = jnp.zeros_like(dg_ref)
            dlb_ref[...] = jnp.zeros_like(dlb_ref)

        n = cps
        cut = lambda t: [t[c * HGRN_CHUNK:(c + 1) * HGRN_CHUNK] for c in range(n)]
        chunks = [pl.ds(c * HGRN_CHUNK, HGRN_CHUNK) for c in range(n)]
        lbv = lb_ref[...]
        ltri = _tri(True)
        lsel = _tri(True, 3).astype(BF16)
        usel = _tri(False, 3).astype(BF16)
        last_row = lax.broadcasted_iota(jnp.int32, (HGRN_CHUNK, LANES), 0) == HGRN_CHUNK - 1
        qr, z = p_ref[0], p_ref[1]
        sq, q, sg, sn, f, key = _hgrn_gates(qr, z, lbv)
        lf = jnp.log(f)
        v = [p_ref[2, rows, :].astype(BF16) for rows in chunks]
        o, dyv = o_ref[...], dy_ref[...]
        r = lax.rsqrt(jnp.mean(o * o, axis=-1, keepdims=True) + RMS_EPS)
        oh = o * r
        doh = dyv * g_ref[...]
        do = cut((r * (doh - oh * jnp.mean(doh * oh, axis=-1, keepdims=True))).astype(BF16))
        dg_ref[...] += jnp.sum(dyv * oh, axis=0, keepdims=True)
        b = [_exact_dot(lsel, lfc) for lfc in cut(lf)]
        b_last = [bc[HGRN_CHUNK - 1:HGRN_CHUNK, :] for bc in b]
        q, key = cut(q), cut(key)
        eb = [jnp.exp(bc) for bc in b]
        enb = [jnp.exp(-bc) for bc in b]
        e2 = [jnp.exp(b_last[c] - b[c]) for c in range(n)]
        dec = [jnp.exp(bl) for bl in b_last]
        qd_f = [q[c] * eb[c] for c in range(n)]
        kd_f = [key[c] * enb[c] for c in range(n)]
        k2_f = [key[c] * e2[c] for c in range(n)]
        qd, kd, k2 = ([t.astype(BF16) for t in ts] for ts in (qd_f, kd_f, k2_f))
        a = [jnp.where(ltri, _dot_nt(qd[c], kd[c]), 0.0).astype(BF16) for c in range(n)]
        da = [jnp.where(ltri, _dot_nt(do[c], v[c]), 0.0).astype(BF16) for c in range(n)]
        grow = [_dot_tn(do[c], qd[c]) for c in range(n)]
        dst, dsts = dstate[...], [None] * n
        for c in reversed(range(n)):
            dsts[c] = dst
            dst = dst * dec[c] + grow[c]
        dstate[...] = dst
        st = [st_ref[0, c] for c in range(n)]
        dstb = [t.astype(BF16) for t in dsts]
        dv = [_dot_tn(a[c], do[c]) + _dot_nt(k2[c], dstb[c]) for c in range(n)]
        dqd = [_dot(da[c], kd[c]) + _dot(do[c], st[c].astype(BF16)) for c in range(n)]
        dkd = [_dot_tn(da[c], qd[c]) for c in range(n)]
        dk2 = [_dot(v[c], dstb[c]) for c in range(n)]
        db = []
        for c in range(n):
            ddec = jnp.sum(dsts[c] * st[c], axis=0, keepdims=True)
            db_last = jnp.sum(dk2[c] * k2_f[c], axis=0, keepdims=True) + ddec * dec[c]
            db.append(dqd[c] * qd_f[c] - dkd[c] * kd_f[c] - dk2[c] * k2_f[c] + jnp.where(last_row, db_last, 0.0))
        dlf = [_exact_dot(usel, dbc) for dbc in db]
        f, sg, sn, sq, qr = cut(f), cut(sg), cut(sn), cut(sq), cut(qr)
        dlb_acc = jnp.zeros((1, LANES), F32)
        for c in range(n):
            dkey = dkd[c] * enb[c] + dk2[c] * e2[c]
            common = dlf[c] / f[c] - dkey
            dlb_acc += jnp.sum(common * sn[c], axis=0, keepdims=True)
            dp_ref[0, chunks[c], :] = (dqd[c] * eb[c] * (sq[c] * (1.0 + qr[c] * (1.0 - sq[c])))).astype(BF16)
            dp_ref[1, chunks[c], :] = (common * ((1.0 - lbv) * sg[c] * sn[c])).astype(BF16)
            dp_ref[2, chunks[c], :] = dv[c].astype(BF16)
        dlb_ref[...] += dlb_acc

    def rev(c):
        return nblk - 1 - c

    vec = pl.BlockSpec((1, LANES), lambda h, c: (0, h))
    col = pl.BlockSpec((HGRN_ROWS, LANES), lambda h, c: (rev(c), h))
    p3 = pl.BlockSpec((3, HGRN_ROWS, LANES), lambda h, c: (0, rev(c), h))
    return pl.pallas_call(
        body, name="hgrn_bwd", grid=(HGRN_HEADS, nblk),
        in_specs=[p3, vec, vec, col, pl.BlockSpec((1, cps, LANES, LANES), lambda h, c: (h, rev(c), 0, 0)), col],
        out_specs=[p3, vec, vec],
        out_shape=[jax.ShapeDtypeStruct((3, s, d), BF16), jax.ShapeDtypeStruct((1, d), F32),
                   jax.ShapeDtypeStruct((1, d), F32)],
        scratch_shapes=[pltpu.VMEM((LANES, LANES), F32)],
        compiler_params=_params(("parallel", "arbitrary")),
    )(p2, lb, norm_g, o_raw, states, dyn)


SMALL_ROWS = 16
FUSED_TM = 512


def _relu2(h):
    r = jnp.maximum(h.astype(F32), 0.0)
    return r * r


def _dact_epi(acc, h):
    return (acc * (2.0 * jnp.maximum(h.astype(F32), 0.0)),)


def _rope_epi(j, acc, c, ss):
    return (lax.cond(j >= 2, lambda: acc, lambda: _rotate(acc, c, ss)),)


def _local_step(x, target, get_w, small, on_grads, deps=()):
    s, d = x.shape
    lb = small["lb"]
    gq = 3 * d
    gt = gq // 1024

    def ln_of(a, w, x_in, which, layer, *, name, k, a_pre=None, deps=()):
        g, b = small[f"ln_{which}_g"][layer:layer + 1], small[f"ln_{which}_b"][layer:layer + 1]
        return _mm(a, w, "nn", name=name, m=s, n=d, k=k, tm=FUSED_TM, a_pre=a_pre, out_dtypes=(F32, BF16, F32),
                   row_outs=((F32, 1),), epi=_ln_epi, tile_extras=(x_in,), vec_extras=(g, b), deps=deps)

    def ln_grad_of(a, w, du_next, xh, rstd, which, layer, *, name, k, a_split=1, deps=()):
        return _mm(a, w, "nt", name=name, m=s, n=d, k=k, tm=FUSED_TM, a_split=a_split, out_dtypes=(F32, BF16),
                   vec_outs=2, epi=_ln_grad_epi, tile_extras=(du_next, xh), row_extras=(rstd,),
                   vec_extras=(small[f"ln_{which}_g"][layer:layer + 1],), deps=deps)

    def ffn_fwd(xb, x_in, w_up, w_down, tag):
        h = _mm(xb, w_up, "nn", name=f"ffn_up_{tag}", m=s, n=D_FF, k=d, out_dtypes=(BF16,))
        return (h,) + tuple(ln_of(h, w_down, x_in, "ffn", tag, name=f"ffn_down_{tag}", k=D_FF, a_pre=_relu2))

    def ffn_bwd(du, dub, xb, h, w_up, w_down, xh, rstd, tag):
        dh = _mm(dub, w_down, "nt", name=f"ffn_dact_{tag}", m=s, n=D_FF, k=d, out_dtypes=(BF16,), epi=_dact_epi,
                 tile_extras=(h,))
        g_down = _mm(h, dub, "tn", name=f"ffn_gdown_{tag}", m=D_FF, n=d, k=s, a_pre=_relu2, out_dtypes=(BF16,))
        g_up = _mm(xb, dh, "tn", name=f"ffn_gup_{tag}", m=d, n=D_FF, k=s, out_dtypes=(BF16,))
        after = on_grads(f"ffn{tag}", {f"ffn_w_down{tag}": g_down, f"ffn_w_up{tag}": g_up})
        return ln_grad_of(dh, w_up, du, xh, rstd, "mix", tag, name=f"ffn_dx_{tag}", k=D_FF, deps=after)

    xs = _prep_x(x, deps)
    w_ain, after = get_w("attn_w_in", None)
    tabs, qkvs, qkv_ts, o_parts, lse_parts = [], [], [], [], []
    for g, dil in enumerate(DILATIONS):
        tabs.append(_rope_tables(s, dil))
        qkv, qkv_t = _mm(xs[g], w_ain, "nn", name=f"attn_in_{g}", m=s, n=gq, k=d, b_col_off=g * gt, out_split=3,
                         out_dtypes=(BF16,), epi=_rope_epi, epi_wants_j=True, row_extras=tabs[g], t_out=True,
                         deps=after if g == 0 else ())
        qkvs.append(qkv)
        qkv_ts.append(qkv_t)
        o_g, lse_g = _attn_fwd(qkv, dil, name=f"attn_fwd_{g}")
        o_parts.append(o_g)
        lse_parts.append(lse_g)
    o_f, o_b, lse_t = _attn_combine(o_parts, lse_parts)
    w_aout, after = get_w("attn_w_out", o_b)
    x1, x1b, xh1, r1 = ln_of(o_b, w_aout, x, "mix", 0, name="attn_out", k=d, deps=after)
    w_up0, w_down0 = get_w("ffn_w_up0", o_b)[0], get_w("ffn_w_down0", o_b)[0]
    h0, x2, x2b, xh2, r2 = ffn_fwd(x1b, x1, w_up0, w_down0, 0)
    w_hin, w_hout, norm_g = get_w("hgrn_w_in", x2b)[0], get_w("hgrn_w_out", x2b)[0], get_w("hgrn_norm_g", x2b)[0]
    p2 = _mm(x2b, w_hin, "nn", name="hgrn_in", m=s, n=3 * d, k=d, out_split=3)
    o_raw, o_n, states = _hgrn_fwd(p2, lb, norm_g)
    w_up1, w_down1 = get_w("ffn_w_up1", o_n)[0], get_w("ffn_w_down1", o_n)[0]
    x3, x3b, xh3, r3 = ln_of(o_n, w_hout, x2, "mix", 1, name="hgrn_out", k=d)
    h1, x4, _, xh4, r4 = ffn_fwd(x3b, x3, w_up1, w_down1, 1)
    du, dub, dg_ffn1, db_ffn1, sq = _loss_ln_bwd(x4, target, xh4, r4, small["ln_ffn_g"][1:2])
    du, dub, dg_mix1, db_mix1 = ffn_bwd(du, dub, x3b, h1, w_up1, w_down1, xh3, r3, 1)
    dyn = _mm(dub, w_hout, "nt", name="hgrn_dout", m=s, n=d, k=d)
    g_hout = _mm(o_n, dub, "tn", name="hgrn_gout", m=d, n=d, k=s, out_dtypes=(BF16,))
    dp2, d_norm_g, d_lb = _hgrn_bwd(p2, lb, norm_g, o_raw, states, dyn)
    g_hin = _mm(x2b, dp2, "tn", name="hgrn_gin", m=d, n=3 * d, k=s, b_split=3, out_dtypes=(BF16,))
    after = on_grads("hgrn", {"hgrn_w_out": g_hout, "hgrn_w_in": g_hin})
    du, dub, dg_ffn0, db_ffn0 = ln_grad_of(dp2, w_hin, du, xh2, r2, "ffn", 0, name="hgrn_dx", k=3 * d, a_split=3,
                                           deps=after)
    du, dub, dg_mix0, db_mix0 = ffn_bwd(du, dub, x1b, h0, w_up0, w_down0, xh1, r1, 0)
    g_aout = _mm(o_b, dub, "tn", name="attn_gout", m=d, n=d, k=s, out_dtypes=(BF16,))
    sm1 = jax.nn.softmax(small["lb_logits"], axis=0)
    d_l1 = d_lb * (sm1[0:1] * sm1[1:2])
    zeros = jnp.zeros((SMALL_ROWS - 12, d), F32)
    loss_row = jnp.broadcast_to(0.5 * jnp.sum(sq) / d, (1, d))
    small_grads = jnp.concatenate([dg_mix0, dg_mix1, db_mix0, db_mix1, dg_ffn0, dg_ffn1, db_ffn0, db_ffn1,
                                   -d_l1, d_l1, d_norm_g, loss_row, zeros], axis=0)
    after = on_grads("attn_out", {"attn_w_out": g_aout, "small": small_grads})
    do = _mm(dub, w_aout, "nt", name="attn_dout", m=s, n=d, k=d, deps=after)
    do_parts, do_ts, aux_parts = _attn_bwd_prep(do, o_f, lse_t)
    g_ain, dqkvs = None, []
    for g, dil in enumerate(DILATIONS):
        dqkvs.append(_attn_bwd(qkvs[g], qkv_ts[g], do_parts[g], do_ts[g], aux_parts[g], tabs[g], dil,
                               name=f"attn_bwd_{g}"))
        g_ain = _mm(xs[g], dqkvs[g], "tn", name=f"attn_gin_{g}", m=d, n=gq, k=s, b_split=3, out_dtypes=(BF16,),
                    out_col_off=g * gt, out_cols=3 * gq, alias=g_ain)
    after = on_grads("attn_in", {"attn_w_in": g_ain})
    dx_parts = [_mm(dqkvs[g], w_ain, "nt", name=f"attn_dx_{g}", m=s, n=d, k=gq, a_split=3, b_k_off=g * gt,
                    deps=after if g == 0 else ())
                for g in range(len(DILATIONS))]
    return _dx_combine(du, dx_parts)


def _mesh_place():
    x, y, c = lax.axis_index("x"), lax.axis_index("y"), lax.axis_index("c")
    return x, y, c, 4 * x + 2 * y + c


def _peer(x, y, c, k):
    px = 1 - x if (k >> 2) & 1 else x
    py = 1 - y if (k >> 1) & 1 else y
    pc = 1 - c if k & 1 else c
    return (px, py, pc), 4 * px + 2 * py + pc


def _window(ref, axis, size, idx):
    if axis is None:
        return ref
    sl = [slice(None)] * len(ref.shape)
    sl[axis] = pl.ds(idx * size, size)
    return ref.at[tuple(sl)]


_HBM = pl.BlockSpec(memory_space=pltpu.HBM)
_SEM = pl.BlockSpec(memory_space=pltpu.SEMAPHORE)
_EFFECT = pltpu.SideEffectType.DATAFLOW_SIDE_EFFECTING


def _xchg_ends(src_ref, land_ref, axis, gather, me, other):
    if gather:
        size = src_ref.shape[axis]
        return src_ref, _window(land_ref, axis, size, me), _window(land_ref, axis, size, other)
    size = None if axis is None else src_ref.shape[axis] // N_DEV
    return _window(src_ref, axis, size, other), land_ref.at[me], land_ref.at[other]


def _xchg_start(srcs, lands, axes, *, gather, name, deps=(), peers=tuple(range(1, N_DEV))):
    n = len(srcs)
    nd = len(deps)

    def body(*refs):
        src_refs, land_refs = refs[:n], refs[n:2 * n]
        send, recv = refs[2 * n + nd:3 * n + nd], refs[3 * n + nd:4 * n + nd]
        token = refs[-1]
        x, y, c, me = _mesh_place()
        for k in peers:
            peer, pidx = _peer(x, y, c, k)
            for i in range(n):
                src, dst, _ = _xchg_ends(src_refs[i], land_refs[i], axes[i], gather, me, pidx)
                pltpu.make_async_remote_copy(
                    src_ref=src, dst_ref=dst, send_sem=send[i].at[k - 1], recv_sem=recv[i].at[k - 1],
                    device_id=peer, device_id_type=pl.DeviceIdType.MESH).start()
        for i in range(n):
            src, dst, _ = _xchg_ends(src_refs[i], land_refs[i], axes[i], gather, me, me)
            pltpu.make_async_copy(src, dst, send[i].at[N_DEV - 1]).start()
        token[...] = jnp.zeros_like(token)

    bufs = list(srcs) + list(lands)
    outs = pl.pallas_call(
        body, name=name,
        out_shape=[pltpu.SemaphoreType.DMA((N_DEV,))] * (2 * n) + [pltpu.HBM(b.shape, b.dtype) for b in bufs]
        + [jax.ShapeDtypeStruct((8, LANES), F32)],
        in_specs=[_HBM] * (2 * n) + [pl.BlockSpec(memory_space=pl.ANY)] * nd,
        out_specs=[_SEM] * (2 * n) + [_HBM] * (2 * n) + [pl.BlockSpec(memory_space=pltpu.VMEM)],
        input_output_aliases={i: 2 * n + i for i in range(2 * n)},
        compiler_params=pltpu.CompilerParams(has_side_effects=_EFFECT),
    )(*[pltpu.with_memory_space_constraint(b, pltpu.HBM) for b in bufs], *deps)
    return dict(send=outs[:n], recv=outs[n:2 * n], srcs=outs[2 * n:3 * n], lands=outs[3 * n:4 * n], token=outs[-1],
                axes=list(axes), gather=gather)


SIBLING = 1
SAME_CORE = (2, 4, 6)


def _gather_relay(xc, after, *, name):
    axis = xc["axes"][0]

    def body(src_ref, land_ref, send1, recv1, after_ref, src_out, land_out, send2, recv2):
        x, y, c, me = _mesh_place()
        sibling, _ = _peer(x, y, c, SIBLING)
        for j, k in enumerate(SAME_CORE):
            peer, pidx = _peer(x, y, c, k)
            src, _, got = _xchg_ends(src_ref, land_ref, axis, True, me, pidx)
            pltpu.make_async_remote_copy(
                src_ref=src, dst_ref=got, send_sem=send1.at[k - 1], recv_sem=recv1.at[k - 1],
                device_id=peer, device_id_type=pl.DeviceIdType.MESH).wait_recv()
            pltpu.make_async_remote_copy(
                src_ref=got, dst_ref=got, send_sem=send2.at[j], recv_sem=recv2.at[j],
                device_id=sibling, device_id_type=pl.DeviceIdType.MESH).start()

    src, land = xc["srcs"][0], xc["lands"][0]
    outs = pl.pallas_call(
        body, name=name,
        out_shape=[pltpu.HBM(src.shape, src.dtype), pltpu.HBM(land.shape, land.dtype)]
        + [pltpu.SemaphoreType.DMA((len(SAME_CORE),))] * 2,
        in_specs=[_HBM, _HBM, _SEM, _SEM, pl.BlockSpec(memory_space=pl.ANY)],
        out_specs=[_HBM, _HBM, _SEM, _SEM], input_output_aliases={0: 0, 1: 1},
        compiler_params=pltpu.CompilerParams(has_side_effects=_EFFECT),
    )(src, land, xc["send"][0], xc["recv"][0], after)
    return dict(src=outs[0], land=outs[1], send=outs[2], recv=outs[3])


def _gather_relay_wait(xc, relay, after, *, name):
    axis = xc["axes"][0]

    def body(src_ref, land_ref, send1, recv1, send2, recv2, after_ref, src_out, land_out):
        x, y, c, me = _mesh_place()
        sibling, sidx = _peer(x, y, c, SIBLING)
        for k in (SIBLING,) + SAME_CORE:
            peer, pidx = _peer(x, y, c, k)
            src, dst, got = _xchg_ends(src_ref, land_ref, axis, True, me, pidx)
            pltpu.make_async_remote_copy(
                src_ref=src, dst_ref=dst, send_sem=send1.at[k - 1], recv_sem=recv1.at[k - 1],
                device_id=peer, device_id_type=pl.DeviceIdType.MESH).wait_send()
        src, dst, got = _xchg_ends(src_ref, land_ref, axis, True, me, sidx)
        pltpu.make_async_remote_copy(
            src_ref=src, dst_ref=got, send_sem=send1.at[SIBLING - 1], recv_sem=recv1.at[SIBLING - 1],
            device_id=sibling, device_id_type=pl.DeviceIdType.MESH).wait_recv()
        src, dst, _ = _xchg_ends(src_ref, land_ref, axis, True, me, me)
        pltpu.make_async_copy(src, dst, send1.at[N_DEV - 1]).wait()
        for j, k in enumerate(SAME_CORE):
            _, pidx = _peer(x, y, c, k)
            _, qidx = _peer(x, y, c, k ^ SIBLING)
            _, _, sent = _xchg_ends(src_ref, land_ref, axis, True, me, pidx)
            _, _, got = _xchg_ends(src_ref, land_ref, axis, True, me, qidx)
            pltpu.make_async_remote_copy(
                src_ref=sent, dst_ref=sent, send_sem=send2.at[j], recv_sem=recv2.at[j],
                device_id=sibling, device_id_type=pl.DeviceIdType.MESH).wait_send()
            pltpu.make_async_remote_copy(
                src_ref=got, dst_ref=got, send_sem=send2.at[j], recv_sem=recv2.at[j],
                device_id=sibling, device_id_type=pl.DeviceIdType.MESH).wait_recv()

    outs = pl.pallas_call(
        body, name=name, out_shape=[pltpu.HBM(relay["src"].shape, relay["src"].dtype),
                                    pltpu.HBM(relay["land"].shape, relay["land"].dtype)],
        in_specs=[_HBM, _HBM, _SEM, _SEM, _SEM, _SEM, pl.BlockSpec(memory_space=pl.ANY)],
        out_specs=[_HBM, _HBM], input_output_aliases={0: 0, 1: 1},
        compiler_params=pltpu.CompilerParams(has_side_effects=_EFFECT),
    )(relay["src"], relay["land"], xc["send"][0], xc["recv"][0], relay["send"], relay["recv"], after)
    return outs[1]


def _xchg_wait(xc, items, after, *, name):
    m = len(items)
    gather = xc["gather"]
    axes = [xc["axes"][i] for i in items]

    def body(*refs):
        src_refs, land_refs = refs[:m], refs[m:2 * m]
        send, recv = refs[2 * m:3 * m], refs[3 * m:4 * m]
        x, y, c, me = _mesh_place()
        for k in range(1, N_DEV):
            peer, pidx = _peer(x, y, c, k)
            for j in range(m):
                src, dst, got = _xchg_ends(src_refs[j], land_refs[j], axes[j], gather, me, pidx)
                pltpu.make_async_remote_copy(
                    src_ref=src, dst_ref=dst, send_sem=send[j].at[k - 1], recv_sem=recv[j].at[k - 1],
                    device_id=peer, device_id_type=pl.DeviceIdType.MESH).wait_send()
                pltpu.make_async_remote_copy(
                    src_ref=src, dst_ref=got, send_sem=send[j].at[k - 1], recv_sem=recv[j].at[k - 1],
                    device_id=peer, device_id_type=pl.DeviceIdType.MESH).wait_recv()
        for j in range(m):
            src, dst, _ = _xchg_ends(src_refs[j], land_refs[j], axes[j], gather, me, me)
            pltpu.make_async_copy(src, dst, send[j].at[N_DEV - 1]).wait()

    bufs = [xc["srcs"][i] for i in items] + [xc["lands"][i] for i in items]
    sems = [xc["send"][i] for i in items] + [xc["recv"][i] for i in items]
    outs = pl.pallas_call(
        body, name=name, out_shape=[pltpu.HBM(b.shape, b.dtype) for b in bufs],
        in_specs=[_HBM] * (2 * m) + [_SEM] * (2 * m) + [pl.BlockSpec(memory_space=pl.ANY)],
        out_specs=[_HBM] * (2 * m), input_output_aliases={j: j for j in range(2 * m)},
        compiler_params=pltpu.CompilerParams(has_side_effects=_EFFECT),
    )(*bufs, *sems, after)
    return outs[m:]


def _cast_bf16(a, *, name):
    r, c = a.shape
    tr = min(r, 512)

    def body(a_ref, o_ref):
        o_ref[...] = a_ref[...].astype(BF16)

    spec = pl.BlockSpec((tr, c), lambda i: (i, 0))
    return pl.pallas_call(body, name=name, grid=(r // tr,), in_specs=[spec], out_specs=spec,
                          out_shape=jax.ShapeDtypeStruct((r, c), BF16), compiler_params=_params(("parallel",)))(a)


def _adamw(slabs, w, m, v, *, name):
    layers, r, c = w.shape
    tr = min(r, 256)

    def body(*refs):
        s_refs = refs[:layers]
        w_ref, m_ref, v_ref, g_ref, d_ref, mo_ref, vo_ref = refs[layers:]
        for l in range(layers):
            g = s_refs[l][0].astype(F32)
            for i in range(1, N_DEV):
                g = g + s_refs[l][i].astype(F32)
            m2 = ADAM_B1 * m_ref[l] + (1.0 - ADAM_B1) * g
            v2 = ADAM_B2 * v_ref[l] + (1.0 - ADAM_B2) * (g * g)
            m_hat = m2 / (1.0 - ADAM_B1 ** ADAM_STEP)
            v_hat = v2 / (1.0 - ADAM_B2 ** ADAM_STEP)
            g_ref[l] = g
            d_ref[l] = -ADAM_LR * (m_hat / (jnp.sqrt(v_hat) + ADAM_EPS) + ADAM_WD * w_ref[l])
            mo_ref[l] = m2
            vo_ref[l] = v2

    spec = pl.BlockSpec((layers, tr, c), lambda i: (0, i, 0))
    return pl.pallas_call(
        body, name=name, grid=(r // tr,),
        in_specs=[pl.BlockSpec((N_DEV, tr, c), lambda i: (0, i, 0))] * layers + [spec, spec, spec],
        out_specs=[spec] * 4, out_shape=[jax.ShapeDtypeStruct((layers, r, c), F32)] * 4,
        compiler_params=_params(("parallel",)),
    )(*slabs, w, m, v)


GATHER_AXIS = {"attn_w_in": 1, "attn_w_out": 0, "ffn_w_up0": 1, "ffn_w_down0": 0, "hgrn_w_in": 1, "hgrn_w_out": 0,
               "hgrn_norm_g": 1, "ffn_w_up1": 1, "ffn_w_down1": 0}
GATHER_STAGES = (("attn_w_in",), ("attn_w_out", "ffn_w_up0", "ffn_w_down0", "hgrn_w_in", "hgrn_w_out", "hgrn_norm_g"),
                 ("ffn_w_up1", "ffn_w_down1"))
GATHER_WAITS = ((("attn_w_in",), 0, 1), (("attn_w_out", "ffn_w_up0", "ffn_w_down0"), 1, 2),
                (("hgrn_w_in", "hgrn_w_out", "hgrn_norm_g"), 1, None), (("ffn_w_up1", "ffn_w_down1"), 2, None))
SCATTER_AXIS = dict(GATHER_AXIS, small=None)
BIG = ("attn_w_in", "attn_w_out", "hgrn_w_in", "hgrn_w_out", "ffn_w_up", "ffn_w_down")
SMALL = ("lb_logits", "ln_mix_g", "ln_mix_b", "ln_ffn_g", "ln_ffn_b")
SMALL_ROW = {"ln_mix_g": 0, "ln_mix_b": 2, "ln_ffn_g": 4, "ln_ffn_b": 6, "lb_logits": 8}
NORM_G_ROW = 10
LOSS_ROW = 11


def kernel(x, attn_w_in, attn_w_out, hgrn_w_in, hgrn_w_out, hgrn_norm_g, lb_logits, ln_mix_g, ln_mix_b, ln_ffn_g, ln_ffn_b, ffn_w_up, ffn_w_down, loss_target, m_attn_w_in, m_attn_w_out, m_hgrn_w_in, m_hgrn_w_out, m_hgrn_norm_g, m_lb_logits, m_ln_mix_g, m_ln_mix_b, m_ln_ffn_g, m_ln_ffn_b, m_ffn_w_up, m_ffn_w_down, v_attn_w_in, v_attn_w_out, v_hgrn_w_in, v_hgrn_w_out, v_hgrn_norm_g, v_lb_logits, v_ln_mix_g, v_ln_mix_b, v_ln_ffn_g, v_ln_ffn_b, v_ffn_w_up, v_ffn_w_down):
    wts = dict(attn_w_in=attn_w_in, attn_w_out=attn_w_out, hgrn_w_in=hgrn_w_in, hgrn_w_out=hgrn_w_out,
               hgrn_norm_g=hgrn_norm_g, lb_logits=lb_logits, ln_mix_g=ln_mix_g, ln_mix_b=ln_mix_b, ln_ffn_g=ln_ffn_g,
               ln_ffn_b=ln_ffn_b, ffn_w_up=ffn_w_up, ffn_w_down=ffn_w_down)
    mom = dict(attn_w_in=m_attn_w_in, attn_w_out=m_attn_w_out, hgrn_w_in=m_hgrn_w_in, hgrn_w_out=m_hgrn_w_out,
               hgrn_norm_g=m_hgrn_norm_g, lb_logits=m_lb_logits, ln_mix_g=m_ln_mix_g, ln_mix_b=m_ln_mix_b,
               ln_ffn_g=m_ln_ffn_g, ln_ffn_b=m_ln_ffn_b, ffn_w_up=m_ffn_w_up, ffn_w_down=m_ffn_w_down)
    vel = dict(attn_w_in=v_attn_w_in, attn_w_out=v_attn_w_out, hgrn_w_in=v_hgrn_w_in, hgrn_w_out=v_hgrn_w_out,
               hgrn_norm_g=v_hgrn_norm_g, lb_logits=v_lb_logits, ln_mix_g=v_ln_mix_g, ln_mix_b=v_ln_mix_b,
               ln_ffn_g=v_ln_ffn_g, ln_ffn_b=v_ln_ffn_b, ffn_w_up=v_ffn_w_up, ffn_w_down=v_ffn_w_down)
    me = 4 * lax.axis_index("x") + 2 * lax.axis_index("y") + lax.axis_index("c")

    src = {"attn_w_in": attn_w_in[0], "attn_w_out": attn_w_out[0], "hgrn_w_in": hgrn_w_in[0], "hgrn_w_out": hgrn_w_out[0],
           "ffn_w_up0": ffn_w_up[0], "ffn_w_down0": ffn_w_down[0], "ffn_w_up1": ffn_w_up[1], "ffn_w_down1": ffn_w_down[1]}
    gathers, got = {}, {}

    def start_gather(stage, deps):
        shards, lands = [], []
        for nm in GATHER_STAGES[stage]:
            sh = hgrn_norm_g if nm == "hgrn_norm_g" else _cast_bf16(src[nm], name=f"cast_{nm}")
            ax = GATHER_AXIS[nm]
            shape = list(sh.shape)
            shape[ax] *= N_DEV
            shards.append(sh)
            lands.append(lax.empty(tuple(shape), sh.dtype))
        peers = (SIBLING,) + SAME_CORE if stage == 0 else tuple(range(1, N_DEV))
        gathers[stage] = _xchg_start(shards, lands, [GATHER_AXIS[nm] for nm in GATHER_STAGES[stage]], gather=True,
                                     name=f"gather_start_{stage}", deps=deps, peers=peers)
        return [gathers[stage]["token"]]

    def get_w(name, after):
        deps = []
        if name not in got:
            group, stage, then = [w for w in GATHER_WAITS if name in w[0]][0]
            xc = gathers[stage]
            if stage == 0:
                relay = _gather_relay(xc, xc["token"], name="gather_relay")
                res = [_gather_relay_wait(xc, relay, xc["token"], name=f"gather_wait_{group[0]}")]
            else:
                res = _xchg_wait(xc, [GATHER_STAGES[stage].index(nm) for nm in group], after,
                                 name=f"gather_wait_{group[0]}")
            got.update(zip(group, res))
            if then is not None:
                deps = start_gather(then, [res[0]])
        return got[name], deps

    first = start_gather(0, [])

    scattered = {}

    def on_grads(tag, grads):
        gnames = list(grads)
        axes = [SCATTER_AXIS[nm] for nm in gnames]
        stacks = []
        for nm, ax in zip(gnames, axes):
            shape = list(grads[nm].shape)
            if ax is not None:
                shape[ax] //= N_DEV
            stacks.append(lax.empty((N_DEV, *shape), grads[nm].dtype))
        scattered[tag] = (gnames, _xchg_start([grads[nm] for nm in gnames], stacks, axes, gather=False,
                                              name=f"scatter_start_{tag}"))
        return [scattered[tag][1]["token"]]

    sm = jax.nn.softmax(lb_logits, axis=0)
    csum = jnp.cumsum(sm, axis=0)
    small = dict(lb=(csum - csum[0:1])[1:2], lb_logits=lb_logits, ln_mix_g=ln_mix_g, ln_mix_b=ln_mix_b,
                 ln_ffn_g=ln_ffn_g, ln_ffn_b=ln_ffn_b)
    grad_x = _local_step(x[0], loss_target[0], get_w, small, on_grads, deps=first)
    out = {}

    def stack_small(src_):
        rows = [None] * SMALL_ROWS
        for name in SMALL:
            rows[SMALL_ROW[name]], rows[SMALL_ROW[name] + 1] = src_[name][0:1], src_[name][1:2]
        zero = jnp.zeros((1, x.shape[-1]), F32)
        return jnp.concatenate([zero if r is None else r for r in rows], axis=0)[None]

    def update(name, slabs):
        shape = wts[name].shape
        out[name] = [r.reshape(shape) for r in _adamw(slabs, wts[name], mom[name], vel[name], name=f"adamw_{name}")]
        return out[name][0]

    slabs, after = {}, grad_x
    for tag, (gnames, xc) in scattered.items():
        slabs.update(zip(gnames, _xchg_wait(xc, list(range(len(gnames))), after, name=f"scatter_wait_{tag}")))
        if tag == "ffn1":
            continue
        if tag == "ffn0":
            update("ffn_w_down", [slabs["ffn_w_down0"], slabs["ffn_w_down1"]])
            after = update("ffn_w_up", [slabs["ffn_w_up0"], slabs["ffn_w_up1"]])
        elif tag == "hgrn":
            update("hgrn_w_out", [slabs["hgrn_w_out"]])
            after = update("hgrn_w_in", [slabs["hgrn_w_in"]])
        elif tag == "attn_out":
            after = update("attn_w_out", [slabs["attn_w_out"]])
        else:
            after = update("attn_w_in", [slabs["attn_w_in"]])
    res = _adamw([slabs["small"]], stack_small(wts), stack_small(mom), stack_small(vel), name="adamw_small")
    for name in SMALL:
        out[name] = [r[0, SMALL_ROW[name]:SMALL_ROW[name] + 2] for r in res]
    loss = res[0][0, LOSS_ROW, 0]
    ng = hgrn_norm_g.shape[-1]
    ng_slabs = lax.dynamic_slice(slabs["small"], (0, NORM_G_ROW, me * ng), (N_DEV, 1, ng))
    out["hgrn_norm_g"] = [r[0] for r in _adamw([ng_slabs], hgrn_norm_g[None], m_hgrn_norm_g[None],
                                                v_hgrn_norm_g[None], name="adamw_norm_g")]
    order =("attn_w_in", "attn_w_out", "hgrn_w_in", "hgrn_w_out", "hgrn_norm_g", "lb_logits", "ln_mix_g", "ln_mix_b",
             "ln_ffn_g", "ln_ffn_b", "ffn_w_up", "ffn_w_down")
    return (loss, grad_x[None], *[out[nm][0] for nm in order], *[out[nm][1] for nm in order],
            *[out[nm][2] for nm in order], *[out[nm][3] for nm in order])
```

```python
import jax
import jax.numpy as jnp
from jax import lax
from jax.experimental import pallas as pl
from jax.experimental.pallas import tpu as pltpu

F32 = jnp.float32
BF16 = jnp.bfloat16

N_DEV = 8
LANES = 128
D_MODEL = 1024
ATTN_HEAD_DIM = 64
ATTN_HEADS = 16
ATTN_BLK = 128
DILATIONS = (1, 4, 16)
ROPE_THETA = 10000.0
HGRN_HEADS = 8
HGRN_CHUNK = 64
D_FF = 4096
LN_EPS = 1e-5
RMS_EPS = 1e-6
DEPTH = 2
ALPHA = (2 * DEPTH) ** 0.25
ADAM_LR, ADAM_B1, ADAM_B2, ADAM_EPS, ADAM_WD, ADAM_STEP = 0.001, 0.9, 0.999, 1e-08, 0.01, 10
VMEM_LIMIT = 48 * 1024 * 1024

_NT = (((1,), (1,)), ((), ()))
_TN = (((0,), (0,)), ((), ()))


def _dot(a, b):
    return jnp.dot(a, b, preferred_element_type=F32)


def _dot_nt(a, b):
    return lax.dot_general(a, b, _NT, preferred_element_type=F32)


def _dot_tn(a, b):
    return lax.dot_general(a, b, _TN, preferred_element_type=F32)


def _split3(x):
    p1 = x.astype(BF16)
    r1 = x - p1.astype(F32)
    p2 = r1.astype(BF16)
    p3 = (r1 - p2.astype(F32)).astype(BF16)
    return p1, p2, p3


def _exact_dot(sel3, x):
    return _dot(sel3, jnp.concatenate(_split3(x), axis=0))


def _exact_dot_r(x, sel3):
    return _dot(jnp.concatenate(_split3(x), axis=1), sel3)


def _params(sem=None):
    return pltpu.CompilerParams(dimension_semantics=sem, vmem_limit_bytes=VMEM_LIMIT)


def _mm(a, b, mode, *, name, m, n, k, tm=1024, tn=1024, tk=1024, out_dtypes=(F32,), epi=None, a_pre=None,
        tile_extras=(), row_extras=(), vec_extras=(), row_outs=(), vec_outs=0, a_split=1, b_split=1, out_split=1,
        b_col_off=0, b_k_off=0, out_col_off=0, out_cols=None, alias=None, epi_wants_j=False, deps=(), t_out=False):
    tm, tn, tk = min(tm, m), min(tn, n), min(tk, k)
    assert m % tm == 0 and n % tn == 0 and k % tk == 0, (name, m, n, k, tm, tn, tk)
    gm, gn, gk = m // tm, n // tn, k // tk
    if mode in ("nn", "nt"):
        if a_split > 1:
            kc = (k // a_split) // tk
            a_spec = pl.BlockSpec((None, tm, tk), lambda i, j, kk: (kk // kc, i, kk % kc))
        else:
            a_spec = pl.BlockSpec((tm, tk), lambda i, j, kk: (i, kk))
    else:
        a_spec = pl.BlockSpec((tk, tm), lambda i, j, kk: (kk, i))
    if mode in ("nn", "tn"):
        if b_split > 1:
            nc = (n // b_split) // tn
            b_spec = pl.BlockSpec((None, tk, tn), lambda i, j, kk: (j // nc, kk, j % nc))
        else:
            b_spec = pl.BlockSpec((tk, tn), lambda i, j, kk: (kk + b_k_off, j + b_col_off))
    else:
        b_spec = pl.BlockSpec((tn, tk), lambda i, j, kk: (j + b_col_off, kk + b_k_off))
    if out_split > 1:
        nco = (n // out_split) // tn
        o_spec = pl.BlockSpec((None, tm, tn), lambda i, j, kk: (j // nco, i, j % nco))
        o_shape = (out_split, m, n // out_split)
    else:
        o_spec = pl.BlockSpec((tm, tn), lambda i, j, kk: (i, j + out_col_off))
        o_shape = (m, out_cols if out_cols is not None else n)
    n_ex = len(tile_extras) + len(row_extras) + len(vec_extras)
    n_out = len(out_dtypes)
    n_plain = n_out + len(row_outs)
    assert not vec_outs or gn == 1
    if epi is None:
        def epi(acc):
            return (acc,)
    dot = {"nn": _dot, "nt": _dot_nt, "tn": _dot_tn}[mode]

    def body(*refs):
        a_ref, b_ref = refs[0], refs[1]
        ex = refs[2:2 + n_ex]
        outs = refs[2 + n_ex + (1 if alias is not None else 0) + len(deps):][:n_plain + vec_outs + (1 if t_out else 0)]
        ii = pl.program_id(0)
        jj = pl.program_id(1)

        def product():
            av = a_ref[...]
            if a_pre is not None:
                av = a_pre(av)
            return dot(av.astype(BF16), b_ref[...].astype(BF16))

        def finish(total):
            lead = (jj,) if epi_wants_j else ()
            res = epi(*lead, total, *[e[...] for e in ex])
            for o, r in zip(outs[:n_plain], res):
                o[...] = r.astype(o.dtype)
            for o, r in zip(outs[n_plain:n_plain + vec_outs], res[n_plain:]):
                @pl.when(ii == 0)
                def _(o=o, r=r):
                    o[...] = r

                @pl.when(ii > 0)
                def _(o=o, r=r):
                    o[...] += r
            if t_out:
                outs[-1][...] = res[-1].astype(outs[-1].dtype)

        if gk == 1:
            finish(product())
        else:
            acc = refs[-1]
            kk = pl.program_id(2)

            @pl.when(kk == 0)
            def _():
                acc[...] = product()

            @pl.when(kk > 0)
            def _():
                acc[...] += product()

            @pl.when(kk == gk - 1)
            def _():
                finish(acc[...])

    in_specs = [a_spec, b_spec] + [o_spec] * len(tile_extras)
    in_specs += [pl.BlockSpec((tm, r.shape[1]), lambda i, j, kk: (i, 0)) if r.shape[0] == m else
                 pl.BlockSpec((r.shape[0], tm), lambda i, j, kk: (0, i)) for r in row_extras]
    in_specs += [pl.BlockSpec((1, tn), lambda i, j, kk: (0, j))] * len(vec_extras)
    args = [a, b] + list(tile_extras) + list(row_extras) + list(vec_extras)
    io_alias = {}
    if alias is not None:
        in_specs.append(pl.BlockSpec(memory_space=pl.ANY))
        args.append(alias)
        io_alias = {len(args) - 1: 0}
    in_specs += [pl.BlockSpec(memory_space=pl.ANY)] * len(deps)
    args += list(deps)
    out_specs = [o_spec] * n_out
    out_shape = [jax.ShapeDtypeStruct(o_shape, dt) for dt in out_dtypes]
    for dt, w in row_outs:
        out_specs.append(pl.BlockSpec((tm, w), lambda i, j, kk: (i, 0)))
        out_shape.append(jax.ShapeDtypeStruct((m, w), dt))
    out_specs += [pl.BlockSpec((1, tn), lambda i, j, kk: (0, j))] * vec_outs
    out_shape += [jax.ShapeDtypeStruct((1, n), F32)] * vec_outs
    if t_out:
        assert out_split > 1
        out_specs.append(pl.BlockSpec((None, tn, tm), lambda i, j, kk: (j // nco, j % nco, i)))
        out_shape.append(jax.ShapeDtypeStruct((out_split, n // out_split, m), out_dtypes[0]))
    out = pl.pallas_call(
        body, name=name, grid=(gm, gn, gk), in_specs=in_specs, out_specs=out_specs, out_shape=out_shape,
        scratch_shapes=[pltpu.VMEM((tm, tn), F32)] if gk > 1 else [],
        input_output_aliases=io_alias,
        compiler_params=_params(("arbitrary" if vec_outs else "parallel", "parallel", "arbitrary")),
    )(*args)
    return out[0] if len(out) == 1 else out


def _rope_tables(seq, dil):
    pos = jnp.arange(seq, dtype=jnp.int32).reshape(seq // dil, dil).T.reshape(seq)
    half = ATTN_HEAD_DIM // 2
    inv = ROPE_THETA ** (-jnp.arange(half, dtype=F32) * (2.0 / ATTN_HEAD_DIM))
    ang = pos.astype(F32)[:, None] * inv[None, :]
    cos, sin = jnp.cos(ang), jnp.sin(ang)
    reps = LANES // ATTN_HEAD_DIM
    return (jnp.tile(jnp.concatenate([cos, cos], axis=1), (1, reps)),
            jnp.tile(jnp.concatenate([-sin, sin], axis=1), (1, reps)))


def _rotate(x, c, ss, sign=1.0):
    w = x.shape[-1]
    half = ATTN_HEAD_DIM // 2
    lane = lax.broadcasted_iota(jnp.int32, x.shape, 1)
    first = (lane % ATTN_HEAD_DIM) < half
    partner = jnp.where(first, pltpu.roll(x, w - half, 1), pltpu.roll(x, half, 1))
    reps = w // LANES
    if reps > 1:
        c = jnp.concatenate([c] * reps, axis=1)
        ss = jnp.concatenate([ss] * reps, axis=1)
    return x * c + sign * (partner * ss)


def _rotate_t(xt, ct, sst):
    half = ATTN_HEAD_DIM // 2
    heads = xt.shape[0] // ATTN_HEAD_DIM
    parts = []
    for h in range(heads):
        lo = h * ATTN_HEAD_DIM
        parts += [xt[lo + half:lo + 2 * half], xt[lo:lo + half]]
    return (xt * jnp.concatenate([ct] * heads, axis=0)
            + jnp.concatenate(parts, axis=0) * jnp.concatenate([sst] * heads, axis=0))


ROW_TILE = 512


def _ln_epi(acc, x, g, b):
    u = ALPHA * x + acc
    mu = jnp.mean(u, axis=-1, keepdims=True)
    uc = u - mu
    var = jnp.mean(uc * uc, axis=-1, keepdims=True)
    rstd = lax.rsqrt(var + LN_EPS)
    xh = uc * rstd
    out = xh * g + b
    return out, out, xh, rstd


def _ln_grad(dy, xh, rstd, g):
    dxh = dy * g
    m1 = jnp.mean(dxh, axis=-1, keepdims=True)
    m2 = jnp.mean(dxh * xh, axis=-1, keepdims=True)
    du = rstd * (dxh - m1 - xh * m2)
    return du, du, jnp.sum(dy * xh, axis=0, keepdims=True), jnp.sum(dy, axis=0, keepdims=True)


def _ln_grad_epi(acc, du_next, xh, rstd, g):
    return _ln_grad(acc + ALPHA * du_next, xh, rstd, g)


def _loss_ln_bwd(y, target, xhat, rstd, g):
    s, d = y.shape

    def body(y_ref, t_ref, xh_ref, r_ref, g_ref, du_ref, dub_ref, dg_ref, db_ref, sq_ref):
        @pl.when(pl.program_id(0) == 0)
        def _():
            dg_ref[...] = jnp.zeros_like(dg_ref)
            db_ref[...] = jnp.zeros_like(db_ref)
            sq_ref[...] = jnp.zeros_like(sq_ref)

        e = y_ref[...] - t_ref[...]
        du, _, dg, db = _ln_grad(e * (1.0 / d), xh_ref[...], r_ref[...], g_ref[...])
        du_ref[...] = du
        dub_ref[...] = du.astype(BF16)
        dg_ref[...] += dg
        db_ref[...] += db
        sq_ref[...] += jnp.sum(e * e, axis=0, keepdims=True)

    row = pl.BlockSpec((ROW_TILE, d), lambda i: (i, 0))
    vec = pl.BlockSpec((1, d), lambda i: (0, 0))
    return pl.pallas_call(
        body, name="loss_ln_bwd", grid=(s // ROW_TILE,),
        in_specs=[row, row, row, pl.BlockSpec((ROW_TILE, 1), lambda i: (i, 0)), vec],
        out_specs=[row, row, vec, vec, vec],
        out_shape=[jax.ShapeDtypeStruct((s, d), F32), jax.ShapeDtypeStruct((s, d), BF16)]
        + [jax.ShapeDtypeStruct((1, d), F32)] * 3,
        compiler_params=_params(("arbitrary",)),
    )(y, target, xhat, rstd, g)


POS_BLK = 2048


def _class_rows(r, dil):
    return pl.ds(r, POS_BLK // dil, stride=dil) if dil > 1 else pl.ds(0, POS_BLK)


def _class_view(a, dil):
    s, w = a.shape
    return a.reshape(dil, s // dil, w)


def _class_spec(dil, all_tiles=True):
    if all_tiles:
        return pl.BlockSpec((dil, POS_BLK // dil, LANES), lambda i, t: (0, i, t))
    return pl.BlockSpec((dil, POS_BLK // dil, LANES), lambda i, t: (0, i, 0))


def _pos_spec(all_tiles=True):
    if all_tiles:
        return pl.BlockSpec((POS_BLK, LANES), lambda i, t: (i, t))
    return pl.BlockSpec((POS_BLK, LANES), lambda i, t: (i, 0))


def _prep_x(x, deps=()):
    s, d = x.shape

    def body(x_ref, *refs):
        outs = refs[len(deps):]
        for dil, o_ref in zip(DILATIONS, outs):
            for r in range(dil):
                o_ref[r] = x_ref[_class_rows(r, dil), :].astype(BF16)

    outs = pl.pallas_call(
        body, name="prep_x", grid=(s // POS_BLK, d // LANES),
        in_specs=[_pos_spec()] + [pl.BlockSpec(memory_space=pl.ANY)] * len(deps),
        out_specs=[_class_spec(dil) for dil in DILATIONS],
        out_shape=[jax.ShapeDtypeStruct((dil, s // dil, d), BF16) for dil in DILATIONS],
        compiler_params=_params(("parallel", "parallel")),
    )(x, *deps)
    return [o.reshape(s, d) for o in outs]


def _head_expand_matrix():
    h = lax.broadcasted_iota(jnp.int32, (LANES, D_MODEL), 0)
    l = lax.broadcasted_iota(jnp.int32, (LANES, D_MODEL), 1)
    return (l // ATTN_HEAD_DIM == h).astype(BF16)


def _attn_fwd(qkv, dil, *, name):
    _, s, d = qkv.shape
    nq = s // ATTN_BLK
    per = nq // dil
    scale = ATTN_HEAD_DIM ** -0.5
    tiles = d // LANES

    def body(q_ref, kc_ref, kp_ref, vc_ref, vp_ref, o_ref, lse_ref):
        qb = pl.program_id(0)
        first = (qb % per) == 0
        qi = lax.broadcasted_iota(jnp.int32, (ATTN_BLK, 2 * ATTN_BLK), 0)
        kj = lax.broadcasted_iota(jnp.int32, (ATTN_BLK, 2 * ATTN_BLK), 1)
        dist = qi + ATTN_BLK - kj
        valid = (dist >= 0) & (dist <= ATTN_BLK) & ((kj >= ATTN_BLK) | jnp.logical_not(first))
        lane = lax.broadcasted_iota(jnp.int32, (ATTN_BLK, LANES), 1)
        lse_tile = jnp.zeros((ATTN_BLK, LANES), F32)
        zero = jnp.zeros((), BF16)
        in_head = [(lane // ATTN_HEAD_DIM) == hh for hh in range(2)]
        for t0 in range(0, tiles, ATTN_TILE_GROUP):
            group = range(t0, t0 + ATTN_TILE_GROUP)
            heads = [(t, hh) for t in group for hh in range(2)]
            cols = {t: pl.ds(t * LANES, LANES) for t in group}
            k2 = {t: jnp.concatenate([kp_ref[:, cols[t]], kc_ref[:, cols[t]]], axis=0) for t in group}
            v2 = {t: jnp.concatenate([vp_ref[:, cols[t]], vc_ref[:, cols[t]]], axis=0) for t in group}
            sc = {(t, hh): jnp.where(valid, _dot_nt(jnp.where(in_head[hh], q_ref[:, cols[t]], zero), k2[t]) * scale,
                                     -jnp.inf) for t, hh in heads}
            mx = {i: jnp.max(sc[i], axis=-1, keepdims=True) for i in heads}
            p = {i: jnp.exp(sc[i] - mx[i]) for i in heads}
            l = {i: jnp.sum(p[i], axis=-1, keepdims=True) for i in heads}
            oh = {i: _dot(p[i].astype(BF16), v2[i[0]]) / l[i] for i in heads}
            for t in group:
                o_ref[:, cols[t]] = jnp.where(in_head[0], oh[t, 0], oh[t, 1])
                for hh in range(2):
                    lse_tile = jnp.where(lane == 2 * t + hh, mx[t, hh] + jnp.log(l[t, hh]), lse_tile)
        lse_ref[...] = lse_tile

    def blk(piece, prev):
        if prev:
            return pl.BlockSpec((None, ATTN_BLK, d), lambda i: (piece, jnp.maximum(i - 1, 0), 0))
        return pl.BlockSpec((None, ATTN_BLK, d), lambda i: (piece, i, 0))

    return pl.pallas_call(
        body, name=name, grid=(nq,),
        in_specs=[blk(0, False), blk(1, False), blk(1, True), blk(2, False), blk(2, True)],
        out_specs=[pl.BlockSpec((ATTN_BLK, d), lambda i: (i, 0)), pl.BlockSpec((ATTN_BLK, LANES), lambda i: (i, 0))],
        out_shape=[jax.ShapeDtypeStruct((s, d), F32), jax.ShapeDtypeStruct((s, LANES), F32)],
        compiler_params=_params(("parallel",)),
    )(qkv, qkv, qkv, qkv, qkv)


def _attn_combine(os_, lses):
    s, d = os_[0].shape
    sel = jnp.concatenate([_head_expand_matrix()] * 3, axis=0)

    def body(o0, o1, o2, l0, l1, l2, sel_ref, of_ref, ob_ref, lt_ref, o_pos, l_pos):
        for g, (dil, o_ref, l_ref) in enumerate(zip(DILATIONS, (o0, o1, o2), (l0, l1, l2))):
            for r in range(dil):
                o_pos[g, _class_rows(r, dil), :] = o_ref[r]
                l_pos[g, _class_rows(r, dil), :] = l_ref[r]
        la, lb_, lc = l_pos[0], l_pos[1], l_pos[2]
        mx = jnp.maximum(jnp.maximum(la, lb_), lc)
        es = (jnp.exp(la - mx), jnp.exp(lb_ - mx), jnp.exp(lc - mx))
        z = es[0] + es[1] + es[2]
        lt_ref[...] = mx + jnp.log(z)
        acc = jnp.zeros((POS_BLK, LANES), F32)
        for g in range(3):
            acc += _exact_dot_r(es[g] / z, sel_ref[...]) * o_pos[g]
        of_ref[...] = acc
        ob_ref[...] = acc.astype(BF16)

    return pl.pallas_call(
        body, name="attn_combine", grid=(s // POS_BLK, d // LANES),
        in_specs=[_class_spec(dil) for dil in DILATIONS] + [_class_spec(dil, False) for dil in DILATIONS]
        + [pl.BlockSpec((3 * LANES, LANES), lambda i, t: (0, t))],
        out_specs=[_pos_spec(), _pos_spec(), _pos_spec(False)],
        out_shape=[jax.ShapeDtypeStruct((s, d), F32), jax.ShapeDtypeStruct((s, d), BF16),
                   jax.ShapeDtypeStruct((s, LANES), F32)],
        scratch_shapes=[pltpu.VMEM((3, POS_BLK, LANES), F32), pltpu.VMEM((3, POS_BLK, LANES), F32)],
        compiler_params=_params(("parallel", "arbitrary")),
    )(*[_class_view(o, dil) for o, dil in zip(os_, DILATIONS)],
      *[_class_view(l, dil) for l, dil in zip(lses, DILATIONS)], sel)


ATTN_TILE_GROUP = 2
AUX_PER_TILE = 12


def _aux_placement():
    h = lax.broadcasted_iota(jnp.int32, (6, LANES, LANES), 1)
    l = lax.broadcasted_iota(jnp.int32, (6, LANES, LANES), 2)
    j = lax.broadcasted_iota(jnp.int32, (6, LANES, LANES), 0)
    target = AUX_PER_TILE * (h // 2) + 3 * (h % 2) + jnp.where(j < 3, j, 3 + j)
    return ((l == target) & (h < ATTN_HEADS)).astype(BF16)


def _attn_bwd_prep(do, o, lse):
    s, d = do.shape
    tiles = d // LANES
    sel_t = jnp.tile(_head_expand_matrix().T.reshape(tiles, LANES, LANES), (1, 3, 1))

    def body(do_ref, o_ref, l_ref, sel_ref, place_ref, *refs):
        outs, delta, aux = refs[:9], refs[9], refs[10]
        t = pl.program_id(1)
        part = _exact_dot_r(do_ref[...] * o_ref[...], sel_ref[...])

        @pl.when(t == 0)
        def _():
            delta[...] = part

        @pl.when(t > 0)
        def _():
            delta[...] += part

        for g, dil in enumerate(DILATIONS):
            for r in range(dil):
                blk = do_ref[_class_rows(r, dil), :].astype(BF16)
                outs[g][r] = blk
                outs[3 + g][r] = blk.T

        @pl.when(t == tiles - 1)
        def _():
            pieces = _split3(l_ref[...]) + _split3(delta[...])
            aux[...] = _dot(jnp.concatenate(pieces, axis=1), place_ref[...])
            for g, dil in enumerate(DILATIONS):
                for r in range(dil):
                    outs[6 + g][r] = aux[_class_rows(r, dil), :].astype(BF16)

    outs = pl.pallas_call(
        body, name="attn_bwd_prep", grid=(s // POS_BLK, tiles),
        in_specs=[_pos_spec(), _pos_spec(), _pos_spec(False),
                  pl.BlockSpec((None, 3 * LANES, LANES), lambda i, t: (t, 0, 0)),
                  pl.BlockSpec((6 * LANES, LANES), lambda i, t: (0, 0))],
        out_specs=[_class_spec(dil) for dil in DILATIONS]
        + [pl.BlockSpec((dil, LANES, POS_BLK // dil), lambda i, t: (0, t, i)) for dil in DILATIONS]
        + [_class_spec(dil, False) for dil in DILATIONS],
        out_shape=[jax.ShapeDtypeStruct((dil, s // dil, d), BF16) for dil in DILATIONS]
        + [jax.ShapeDtypeStruct((dil, d, s // dil), BF16) for dil in DILATIONS]
        + [jax.ShapeDtypeStruct((dil, s // dil, LANES), BF16) for dil in DILATIONS],
        scratch_shapes=[pltpu.VMEM((POS_BLK, LANES), F32), pltpu.VMEM((POS_BLK, LANES), F32)],
        compiler_params=_params(("parallel", "arbitrary")),
    )(do, o, lse, sel_t, _aux_placement().reshape(6 * LANES, LANES))
    return ([a.reshape(s, d) for a in outs[0:3]], list(outs[3:6]), [a.reshape(s, LANES) for a in outs[6:9]])


def _attn_bwd(qkv, qkv_t, do, do_t, aux, tables, dil, *, name):
    _, s, d = qkv.shape
    nq = s // ATTN_BLK
    per = nq // dil
    scale = ATTN_HEAD_DIM ** -0.5
    assert scale == 0.125
    tiles = d // LANES
    half = ATTN_HEAD_DIM

    def body(qd_ref, qo_ref, k_ref, qtd_ref, qto_ref, kt_ref, vt_ref, dod_ref, doo_ref, dotd_ref, doto_ref,
             auxd_ref, auxo_ref, c_ref, ss_ref, out_ref, carry):
        kb = pl.program_id(0)

        @pl.when(kb == 0)
        def _():
            carry[...] = jnp.zeros_like(carry)

        has_next = (kb % per) != (per - 1)
        qi = lax.broadcasted_iota(jnp.int32, (ATTN_BLK, 2 * ATTN_BLK), 0)
        kj = lax.broadcasted_iota(jnp.int32, (ATTN_BLK, 2 * ATTN_BLK), 1) % ATTN_BLK
        valid_d = kj <= qi
        valid_o = (kj >= qi) & has_next
        lane = lax.broadcasted_iota(jnp.int32, (ATTN_BLK, LANES), 1)
        row = lax.broadcasted_iota(jnp.int32, (LANES, ATTN_BLK), 0)
        side = lax.broadcasted_iota(jnp.int32, (LANES, 2 * ATTN_BLK), 0)
        first = lax.broadcasted_iota(jnp.int32, (LANES, 2 * ATTN_BLK), 1) < ATTN_BLK
        c, ss = c_ref[...], ss_ref[...]
        zero = jnp.zeros((), BF16)
        sides = ((qd_ref, qtd_ref, dod_ref, dotd_ref, auxd_ref[...], valid_d),
                 (qo_ref, qto_ref, doo_ref, doto_ref, auxo_ref[...], valid_o))
        both = (0, 1)

        def head_halves(x, index):
            axis = 0 if index is lane else 1
            return jnp.concatenate([jnp.where(index < half, x, zero), jnp.where(index >= half, x, zero)], axis=axis)

        for t0 in range(0, tiles, ATTN_TILE_GROUP):
            group = range(t0, t0 + ATTN_TILE_GROUP)
            cols = {t: pl.ds(t * LANES, LANES) for t in group}
            kk, kk_t, vv_t = {}, {}, {}
            for t in group:
                base = AUX_PER_TILE * t
                hit = lambda lo: ((first & (side >= base + lo) & (side < base + lo + 3))
                                  | (jnp.logical_not(first) & (side >= base + lo + 3) & (side < base + lo + 6)))
                kk[t] = head_halves(k_ref[:, cols[t]], lane)
                kk_t[t] = jnp.concatenate([head_halves(kt_ref[cols[t], :], row),
                                           jnp.where(hit(0), -1.0 / scale, 0.0).astype(BF16)], axis=0)
                vv_t[t] = jnp.concatenate([head_halves(vt_ref[cols[t], :], row),
                                           jnp.where(hit(6), -1.0, 0.0).astype(BF16)], axis=0)
            sc = {(t, w): _dot(jnp.concatenate([sides[w][0][:, cols[t]], sides[w][4]], axis=1), kk_t[t])
                  for t in group for w in both}
            dpd = {(t, w): _dot(jnp.concatenate([sides[w][2][:, cols[t]], sides[w][4]], axis=1), vv_t[t])
                   for t in group for w in both}
            p = {i: jnp.where(sides[i[1]][5], jnp.exp(sc[i] * scale), 0.0) for i in sc}
            ds = {i: (p[i] * dpd[i] * scale).astype(BF16) for i in sc}
            pb = {i: p[i].astype(BF16) for i in sc}
            dv_t = {t: sum(_dot(sides[w][3][cols[t], :], pb[t, w]) for w in both) for t in group}
            dk_t = {t: sum(_dot(sides[w][1][cols[t], :], ds[t, w]) for w in both) for t in group}
            dq = {i: _dot(ds[i], kk[i[0]]) for i in sc}
            for t in group:
                dq_now = carry[:, cols[t]] + dq[t, 0]
                carry[:, cols[t]] = dq[t, 1]
                dk = jnp.where(row < half, dk_t[t][:, :ATTN_BLK], dk_t[t][:, ATTN_BLK:]).T
                dv = jnp.where(row < half, dv_t[t][:, :ATTN_BLK], dv_t[t][:, ATTN_BLK:]).T
                out_ref[0, :, cols[t]] = _rotate(dq_now, c, ss, -1.0).astype(BF16)
                out_ref[1, :, cols[t]] = _rotate(dk, c, ss, -1.0).astype(BF16)
                out_ref[2, :, cols[t]] = dv.astype(BF16)

    def nxt(i):
        return jnp.minimum(i + 1, nq - 1)

    def piece(p, shift):
        if shift:
            return pl.BlockSpec((None, ATTN_BLK, d), lambda i: (p, nxt(i), 0))
        return pl.BlockSpec((None, ATTN_BLK, d), lambda i: (p, i, 0))

    def piece_t(p, shift):
        if shift:
            return pl.BlockSpec((None, d, ATTN_BLK), lambda i: (p, 0, nxt(i)))
        return pl.BlockSpec((None, d, ATTN_BLK), lambda i: (p, 0, i))

    def rows(width, shift):
        if shift:
            return pl.BlockSpec((ATTN_BLK, width), lambda i: (nxt(i), 0))
        return pl.BlockSpec((ATTN_BLK, width), lambda i: (i, 0))

    def do_t_spec(shift):
        if shift:
            return pl.BlockSpec((None, d, ATTN_BLK), lambda i: (nxt(i) // per, 0, nxt(i) % per))
        return pl.BlockSpec((None, d, ATTN_BLK), lambda i: (i // per, 0, i % per))

    return pl.pallas_call(
        body, name=name, grid=(nq,),
        in_specs=[piece(0, False), piece(0, True), piece(1, False),
                  piece_t(0, False), piece_t(0, True), piece_t(1, False), piece_t(2, False),
                  rows(d, False), rows(d, True), do_t_spec(False), do_t_spec(True),
                  rows(LANES, False), rows(LANES, True), rows(LANES, False), rows(LANES, False)],
        out_specs=pl.BlockSpec((3, ATTN_BLK, d), lambda i: (0, i, 0)),
        out_shape=jax.ShapeDtypeStruct((3, s, d), BF16),
        scratch_shapes=[pltpu.VMEM((ATTN_BLK, d), F32)],
        compiler_params=_params(("arbitrary",)),
    )(qkv, qkv, qkv, qkv_t, qkv_t, qkv_t, qkv_t, do, do, do_t, do_t, aux, aux, *tables)


def _dx_combine(du, parts):
    s, d = du.shape

    def body(du_ref, p0, p1, p2, out_ref):
        out_ref[...] = ALPHA * du_ref[...] + p0[0]
        for dil, p_ref in zip(DILATIONS[1:], (p1, p2)):
            for r in range(dil):
                out_ref[_class_rows(r, dil), :] += p_ref[r]

    return pl.pallas_call(
        body, name="dx_combine", grid=(s // POS_BLK, d // LANES),
        in_specs=[_pos_spec()] + [_class_spec(dil) for dil in DILATIONS], out_specs=_pos_spec(),
        out_shape=jax.ShapeDtypeStruct((s, d), F32),
        compiler_params=_params(("parallel", "parallel")),
    )(du, *[_class_view(p, dil) for p, dil in zip(parts, DILATIONS)])


HGRN_ROWS = 1024


def _tri(lower, copies=1):
    i = lax.broadcasted_iota(jnp.int32, (HGRN_CHUNK, HGRN_CHUNK * copies), 0)
    j = lax.broadcasted_iota(jnp.int32, (HGRN_CHUNK, HGRN_CHUNK * copies), 1) % HGRN_CHUNK
    return (j <= i) if lower else (j >= i)


def _hgrn_gates(qr, z, lb):
    sq = jax.nn.sigmoid(qr)
    e = jnp.exp(-jnp.abs(z))
    big = 1.0 / (1.0 + e)
    small = e * big
    sg = jnp.where(z >= 0, big, small)
    sn = jnp.where(z >= 0, small, big)
    return sq, qr * sq, sg, sn, lb + (1.0 - lb) * sg, (1.0 - lb) * sn


def _hgrn_fwd(p2, lb, norm_g):
    _, s, d = p2.shape
    nblk = s // HGRN_ROWS
    cps = HGRN_ROWS // HGRN_CHUNK

    def body(p_ref, lb_ref, g_ref, o_ref, on_ref, st_ref, state):
        @pl.when(pl.program_id(1) == 0)
        def _():
            state[...] = jnp.zeros_like(state)

        chunks = [pl.ds(c * HGRN_CHUNK, HGRN_CHUNK) for c in range(cps)]
        ltri = _tri(True)
        lsel = _tri(True, 3).astype(BF16)
        _, q, _, _, f, key = _hgrn_gates(p_ref[0], p_ref[1], lb_ref[...])
        lf = jnp.log(f)
        v = [p_ref[2, rows, :].astype(BF16) for rows in chunks]
        b = [_exact_dot(lsel, lf[c * HGRN_CHUNK:(c + 1) * HGRN_CHUNK]) for c in range(cps)]
        b_last = [bc[HGRN_CHUNK - 1:HGRN_CHUNK, :] for bc in b]
        q = [q[c * HGRN_CHUNK:(c + 1) * HGRN_CHUNK] for c in range(cps)]
        key = [key[c * HGRN_CHUNK:(c + 1) * HGRN_CHUNK] for c in range(cps)]
        qd = [(q[c] * jnp.exp(b[c])).astype(BF16) for c in range(cps)]
        kd = [(key[c] * jnp.exp(-b[c])).astype(BF16) for c in range(cps)]
        k2 = [(key[c] * jnp.exp(b_last[c] - b[c])).astype(BF16) for c in range(cps)]
        a = [jnp.where(ltri, _dot_nt(qd[c], kd[c]), 0.0).astype(BF16) for c in range(cps)]
        kv = [_dot_tn(v[c], k2[c]) for c in range(cps)]
        st, sts = state[...], []
        for c in range(cps):
            sts.append(st)
            st_ref[0, c] = st
            st = st * jnp.exp(b_last[c]) + kv[c]
        state[...] = st
        gv = g_ref[...]
        for c in range(cps):
            o = _dot(a[c], v[c]) + _dot_nt(qd[c], sts[c].astype(BF16))
            o_ref[chunks[c], :] = o
            r = lax.rsqrt(jnp.mean(o * o, axis=-1, keepdims=True) + RMS_EPS)
            on_ref[chunks[c], :] = (o * r * gv).astype(BF16)

    vec = pl.BlockSpec((1, LANES), lambda h, c: (0, h))
    col = pl.BlockSpec((HGRN_ROWS, LANES), lambda h, c: (c, h))
    return pl.pallas_call(
        body, name="hgrn_fwd", grid=(HGRN_HEADS, nblk),
        in_specs=[pl.BlockSpec((3, HGRN_ROWS, LANES), lambda h, c: (0, c, h)), vec, vec],
        out_specs=[col, col, pl.BlockSpec((1, cps, LANES, LANES), lambda h, c: (h, c, 0, 0))],
        out_shape=[jax.ShapeDtypeStruct((s, d), F32), jax.ShapeDtypeStruct((s, d), BF16),
                   jax.ShapeDtypeStruct((HGRN_HEADS, s // HGRN_CHUNK, LANES, LANES), F32)],
        scratch_shapes=[pltpu.VMEM((LANES, LANES), F32)],
        compiler_params=_params(("parallel", "arbitrary")),
    )(p2, lb, norm_g)


def _hgrn_bwd(p2, lb, norm_g, o_raw, states, dyn):
    _, s, d = p2.shape
    nblk = s // HGRN_ROWS
    cps = HGRN_ROWS // HGRN_CHUNK

    def body(p_ref, lb_ref, g_ref, o_ref, st_ref, dy_ref, dp_ref, dg_ref, dlb_ref, dstate):
        @pl.when(pl.program_id(1) == 0)
        def _():
            dstate[...] = jnp.zeros_like(dstate)
            dg_ref[...] = jnp.zeros_like(dg_ref)
            dlb_ref[...] = jnp.zeros_like(dlb_ref)

        n = cps
        cut = lambda t: [t[c * HGRN_CHUNK:(c + 1) * HGRN_CHUNK] for c in range(n)]
        chunks = [pl.ds(c * HGRN_CHUNK, HGRN_CHUNK) for c in range(n)]
        lbv = lb_ref[...]
        ltri = _tri(True)
        lsel = _tri(True, 3).astype(BF16)
        usel = _tri(False, 3).astype(BF16)
        last_row = lax.broadcasted_iota(jnp.int32, (HGRN_CHUNK, LANES), 0) == HGRN_CHUNK - 1
        qr, z = p_ref[0], p_ref[1]
        sq, q, sg, sn, f, key = _hgrn_gates(qr, z, lbv)
        lf = jnp.log(f)
        v = [p_ref[2, rows, :].astype(BF16) for rows in chunks]
        o, dyv = o_ref[...], dy_ref[...]
        r = lax.rsqrt(jnp.mean(o * o, axis=-1, keepdims=True) + RMS_EPS)
        oh = o * r
        doh = dyv * g_ref[...]
        do = cut((r * (doh - oh * jnp.mean(doh * oh, axis=-1, keepdims=True))).astype(BF16))
        dg_ref[...] += jnp.sum(dyv * oh, axis=0, keepdims=True)
        b = [_exact_dot(lsel, lfc) for lfc in cut(lf)]
        b_last = [bc[HGRN_CHUNK - 1:HGRN_CHUNK, :] for bc in b]
        q, key = cut(q), cut(key)
        eb = [jnp.exp(bc) for bc in b]
        enb = [jnp.exp(-bc) for bc in b]
        e2 = [jnp.exp(b_last[c] - b[c]) for c in range(n)]
        dec = [jnp.exp(bl) for bl in b_last]
        qd_f = [q[c] * eb[c] for c in range(n)]
        kd_f = [key[c] * enb[c] for c in range(n)]
        k2_f = [key[c] * e2[c] for c in range(n)]
        qd, kd, k2 = ([t.astype(BF16) for t in ts] for ts in (qd_f, kd_f, k2_f))
        a = [jnp.where(ltri, _dot_nt(qd[c], kd[c]), 0.0).astype(BF16) for c in range(n)]
        da = [jnp.where(ltri, _dot_nt(do[c], v[c]), 0.0).astype(BF16) for c in range(n)]
        grow = [_dot_tn(do[c], qd[c]) for c in range(n)]
        dst, dsts = dstate[...], [None] * n
        for c in reversed(range(n)):
            dsts[c] = dst
            dst = dst * dec[c] + grow[c]
        dstate[...] = dst
        st = [st_ref[0, c] for c in range(n)]
        dstb = [t.astype(BF16) for t in dsts]
        dv = [_dot_tn(a[c], do[c]) + _dot_nt(k2[c], dstb[c]) for c in range(n)]
        dqd = [_dot(da[c], kd[c]) + _dot(do[c], st[c].astype(BF16)) for c in range(n)]
        dkd = [_dot_tn(da[c], qd[c]) for c in range(n)]
        dk2 = [_dot(v[c], dstb[c]) for c in range(n)]
        db = []
        for c in range(n):
            ddec = jnp.sum(dsts[c] * st[c], axis=0, keepdims=True)
            db_last = jnp.sum(dk2[c] * k2_f[c], axis=0, keepdims=True) + ddec * dec[c]
            db.append(dqd[c] * qd_f[c] - dkd[c] * kd_f[c] - dk2[c] * k2_f[c] + jnp.where(last_row, db_last, 0.0))
        dlf = [_exact_dot(usel, dbc) for dbc in db]
        f, sg, sn, sq, qr = cut(f), cut(sg), cut(sn), cut(sq), cut(qr)
        dlb_acc = jnp.zeros((1, LANES), F32)
        for c in range(n):
            dkey = dkd[c] * enb[c] + dk2[c] * e2[c]
            common = dlf[c] / f[c] - dkey
            dlb_acc += jnp.sum(common * sn[c], axis=0, keepdims=True)
            dp_ref[0, chunks[c], :] = (dqd[c] * eb[c] * (sq[c] * (1.0 + qr[c] * (1.0 - sq[c])))).astype(BF16)
            dp_ref[1, chunks[c], :] = (common * ((1.0 - lbv) * sg[c] * sn[c])).astype(BF16)
            dp_ref[2, chunks[c], :] = dv[c].astype(BF16)
        dlb_ref[...] += dlb_acc

    def rev(c):
        return nblk - 1 - c

    vec = pl.BlockSpec((1, LANES), lambda h, c: (0, h))
    col = pl.BlockSpec((HGRN_ROWS, LANES), lambda h, c: (rev(c), h))
    p3 = pl.BlockSpec((3, HGRN_ROWS, LANES), lambda h, c: (0, rev(c), h))
    return pl.pallas_call(
        body, name="hgrn_bwd", grid=(HGRN_HEADS, nblk),
        in_specs=[p3, vec, vec, col, pl.BlockSpec((1, cps, LANES, LANES), lambda h, c: (h, rev(c), 0, 0)), col],
        out_specs=[p3, vec, vec],
        out_shape=[jax.ShapeDtypeStruct((3, s, d), BF16), jax.ShapeDtypeStruct((1, d), F32),
                   jax.ShapeDtypeStruct((1, d), F32)],
        scratch_shapes=[pltpu.VMEM((LANES, LANES), F32)],
        compiler_params=_params(("parallel", "arbitrary")),
    )(p2, lb, norm_g, o_raw, states, dyn)


SMALL_ROWS = 16
FUSED_TM = 512


def _relu2(h):
    r = jnp.maximum(h.astype(F32), 0.0)
    return r * r


def _dact_epi(acc, h):
    return (acc * (2.0 * jnp.maximum(h.astype(F32), 0.0)),)


def _rope_epi(j, acc, ct, sst):
    acc_t = acc.T
    out_t = jnp.where(j >= 2, acc_t, _rotate_t(acc_t, ct, sst)).astype(BF16)
    return out_t.T, out_t


def _local_step(x, target, get_w, small, on_grads, deps=()):
    s, d = x.shape
    lb = small["lb"]
    gq = 3 * d
    gt = gq // 1024

    def ln_of(a, w, x_in, which, layer, *, name, k, a_pre=None, deps=()):
        g, b = small[f"ln_{which}_g"][layer:layer + 1], small[f"ln_{which}_b"][layer:layer + 1]
        return _mm(a, w, "nn", name=name, m=s, n=d, k=k, tm=FUSED_TM, a_pre=a_pre, out_dtypes=(F32, BF16, F32),
                   row_outs=((F32, 1),), epi=_ln_epi, tile_extras=(x_in,), vec_extras=(g, b), deps=deps)

    def ln_grad_of(a, w, du_next, xh, rstd, which, layer, *, name, k, a_split=1, deps=()):
        return _mm(a, w, "nt", name=name, m=s, n=d, k=k, tm=FUSED_TM, a_split=a_split, out_dtypes=(F32, BF16),
                   vec_outs=2, epi=_ln_grad_epi, tile_extras=(du_next, xh), row_extras=(rstd,),
                   vec_extras=(small[f"ln_{which}_g"][layer:layer + 1],), deps=deps)

    def ffn_fwd(xb, x_in, w_up, w_down, tag):
        h = _mm(xb, w_up, "nn", name=f"ffn_up_{tag}", m=s, n=D_FF, k=d, out_dtypes=(BF16,))
        return (h,) + tuple(ln_of(h, w_down, x_in, "ffn", tag, name=f"ffn_down_{tag}", k=D_FF, a_pre=_relu2))

    def ffn_bwd(du, dub, xb, h, w_up, w_down, xh, rstd, tag):
        dh = _mm(dub, w_down, "nt", name=f"ffn_dact_{tag}", m=s, n=D_FF, k=d, out_dtypes=(BF16,), epi=_dact_epi,
                 tile_extras=(h,))
        g_down = _mm(h, dub, "tn", name=f"ffn_gdown_{tag}", m=D_FF, n=d, k=s, a_pre=_relu2, out_dtypes=(BF16,))
        g_up = _mm(xb, dh, "tn", name=f"ffn_gup_{tag}", m=d, n=D_FF, k=s, out_dtypes=(BF16,))
        after = on_grads(f"ffn{tag}", {f"ffn_w_down{tag}": g_down, f"ffn_w_up{tag}": g_up})
        return ln_grad_of(dh, w_up, du, xh, rstd, "mix", tag, name=f"ffn_dx_{tag}", k=D_FF, deps=after)

    xs = _prep_x(x, deps)
    w_ain, after = get_w("attn_w_in", None)
    tabs, qkvs, qkv_ts, o_parts, lse_parts = [], [], [], [], []
    for g, dil in enumerate(DILATIONS):
        tabs.append(_rope_tables(s, dil))
        tabs_t = [t[:, :ATTN_HEAD_DIM].T for t in tabs[g]]
        qkv, qkv_t = _mm(xs[g], w_ain, "nn", name=f"attn_in_{g}", m=s, n=gq, k=d, b_col_off=g * gt, out_split=3,
                         out_dtypes=(BF16,), epi=_rope_epi, epi_wants_j=True, row_extras=tabs_t, t_out=True,
                         deps=after if g == 0 else ())
        qkvs.append(qkv)
        qkv_ts.append(qkv_t)
        o_g, lse_g = _attn_fwd(qkv, dil, name=f"attn_fwd_{g}")
        o_parts.append(o_g)
        lse_parts.append(lse_g)
    o_f, o_b, lse_t = _attn_combine(o_parts, lse_parts)
    w_aout, after = get_w("attn_w_out", o_b)
    x1, x1b, xh1, r1 = ln_of(o_b, w_aout, x, "mix", 0, name="attn_out", k=d, deps=after)
    w_up0, w_down0 = get_w("ffn_w_up0", o_b)[0], get_w("ffn_w_down0", o_b)[0]
    h0, x2, x2b, xh2, r2 = ffn_fwd(x1b, x1, w_up0, w_down0, 0)
    w_hin, w_hout, norm_g = get_w("hgrn_w_in", x2b)[0], get_w("hgrn_w_out", x2b)[0], get_w("hgrn_norm_g", x2b)[0]
    p2 = _mm(x2b, w_hin, "nn", name="hgrn_in", m=s, n=3 * d, k=d, out_split=3)
    o_raw, o_n, states = _hgrn_fwd(p2, lb, norm_g)
    w_up1, w_down1 = get_w("ffn_w_up1", o_n)[0], get_w("ffn_w_down1", o_n)[0]
    x3, x3b, xh3, r3 = ln_of(o_n, w_hout, x2, "mix", 1, name="hgrn_out", k=d)
    h1, x4, _, xh4, r4 = ffn_fwd(x3b, x3, w_up1, w_down1, 1)
    du, dub, dg_ffn1, db_ffn1, sq = _loss_ln_bwd(x4, target, xh4, r4, small["ln_ffn_g"][1:2])
    du, dub, dg_mix1, db_mix1 = ffn_bwd(du, dub, x3b, h1, w_up1, w_down1, xh3, r3, 1)
    dyn = _mm(dub, w_hout, "nt", name="hgrn_dout", m=s, n=d, k=d)
    g_hout = _mm(o_n, dub, "tn", name="hgrn_gout", m=d, n=d, k=s, out_dtypes=(BF16,))
    dp2, d_norm_g, d_lb = _hgrn_bwd(p2, lb, norm_g, o_raw, states, dyn)
    g_hin = _mm(x2b, dp2, "tn", name="hgrn_gin", m=d, n=3 * d, k=s, b_split=3, out_dtypes=(BF16,))
    after = on_grads("hgrn", {"hgrn_w_out": g_hout, "hgrn_w_in": g_hin})
    du, dub, dg_ffn0, db_ffn0 = ln_grad_of(dp2, w_hin, du, xh2, r2, "ffn", 0, name="hgrn_dx", k=3 * d, a_split=3,
                                           deps=after)
    du, dub, dg_mix0, db_mix0 = ffn_bwd(du, dub, x1b, h0, w_up0, w_down0, xh1, r1, 0)
    g_aout = _mm(o_b, dub, "tn", name="attn_gout", m=d, n=d, k=s, out_dtypes=(BF16,))
    sm1 = jax.nn.softmax(small["lb_logits"], axis=0)
    d_l1 = d_lb * (sm1[0:1] * sm1[1:2])
    zeros = jnp.zeros((SMALL_ROWS - 12, d), F32)
    loss_row = jnp.broadcast_to(0.5 * jnp.sum(sq) / d, (1, d))
    small_grads = jnp.concatenate([dg_mix0, dg_mix1, db_mix0, db_mix1, dg_ffn0, dg_ffn1, db_ffn0, db_ffn1,
                                   -d_l1, d_l1, d_norm_g, loss_row, zeros], axis=0)
    after = on_grads("attn_out", {"attn_w_out": g_aout, "small": small_grads})
    do = _mm(dub, w_aout, "nt", name="attn_dout", m=s, n=d, k=d, deps=after)
    do_parts, do_ts, aux_parts = _attn_bwd_prep(do, o_f, lse_t)
    g_ain, dqkvs = None, []
    for g, dil in enumerate(DILATIONS):
        dqkvs.append(_attn_bwd(qkvs[g], qkv_ts[g], do_parts[g], do_ts[g], aux_parts[g], tabs[g], dil,
                               name=f"attn_bwd_{g}"))
        g_ain = _mm(xs[g], dqkvs[g], "tn", name=f"attn_gin_{g}", m=d, n=gq, k=s, b_split=3, out_dtypes=(BF16,),
                    out_col_off=g * gt, out_cols=3 * gq, alias=g_ain)
    after = on_grads("attn_in", {"attn_w_in": g_ain})
    dx_parts = [_mm(dqkvs[g], w_ain, "nt", name=f"attn_dx_{g}", m=s, n=d, k=gq, a_split=3, b_k_off=g * gt,
                    deps=after if g == 0 else ())
                for g in range(len(DILATIONS))]
    return _dx_combine(du, dx_parts)


def _mesh_place():
    x, y, c = lax.axis_index("x"), lax.axis_index("y"), lax.axis_index("c")
    return x, y, c, 4 * x + 2 * y + c


def _peer(x, y, c, k):
    px = 1 - x if (k >> 2) & 1 else x
    py = 1 - y if (k >> 1) & 1 else y
    pc = 1 - c if k & 1 else c
    return (px, py, pc), 4 * px + 2 * py + pc


def _window(ref, axis, size, idx):
    if axis is None:
        return ref
    sl = [slice(None)] * len(ref.shape)
    sl[axis] = pl.ds(idx * size, size)
    return ref.at[tuple(sl)]


_HBM = pl.BlockSpec(memory_space=pltpu.HBM)
_SEM = pl.BlockSpec(memory_space=pltpu.SEMAPHORE)
_EFFECT = pltpu.SideEffectType.DATAFLOW_SIDE_EFFECTING


def _xchg_ends(src_ref, land_ref, axis, gather, me, other):
    if gather:
        size = src_ref.shape[axis]
        return src_ref, _window(land_ref, axis, size, me), _window(land_ref, axis, size, other)
    size = None if axis is None else src_ref.shape[axis] // N_DEV
    return _window(src_ref, axis, size, other), land_ref.at[me], land_ref.at[other]


def _xchg_start(srcs, lands, axes, *, gather, name, deps=(), peers=tuple(range(1, N_DEV))):
    n = len(srcs)
    nd = len(deps)

    def body(*refs):
        src_refs, land_refs = refs[:n], refs[n:2 * n]
        send, recv = refs[2 * n + nd:3 * n + nd], refs[3 * n + nd:4 * n + nd]
        token = refs[-1]
        x, y, c, me = _mesh_place()
        for k in peers:
            peer, pidx = _peer(x, y, c, k)
            for i in range(n):
                src, dst, _ = _xchg_ends(src_refs[i], land_refs[i], axes[i], gather, me, pidx)
                pltpu.make_async_remote_copy(
                    src_ref=src, dst_ref=dst, send_sem=send[i].at[k - 1], recv_sem=recv[i].at[k - 1],
                    device_id=peer, device_id_type=pl.DeviceIdType.MESH).start()
        for i in range(n):
            src, dst, _ = _xchg_ends(src_refs[i], land_refs[i], axes[i], gather, me, me)
            pltpu.make_async_copy(src, dst, send[i].at[N_DEV - 1]).start()
        token[...] = jnp.zeros_like(token)

    bufs = list(srcs) + list(lands)
    outs = pl.pallas_call(
        body, name=name,
        out_shape=[pltpu.SemaphoreType.DMA((N_DEV,))] * (2 * n) + [pltpu.HBM(b.shape, b.dtype) for b in bufs]
        + [jax.ShapeDtypeStruct((8, LANES), F32)],
        in_specs=[_HBM] * (2 * n) + [pl.BlockSpec(memory_space=pl.ANY)] * nd,
        out_specs=[_SEM] * (2 * n) + [_HBM] * (2 * n) + [pl.BlockSpec(memory_space=pltpu.VMEM)],
        input_output_aliases={i: 2 * n + i for i in range(2 * n)},
        compiler_params=pltpu.CompilerParams(has_side_effects=_EFFECT),
    )(*[pltpu.with_memory_space_constraint(b, pltpu.HBM) for b in bufs], *deps)
    return dict(send=outs[:n], recv=outs[n:2 * n], srcs=outs[2 * n:3 * n], lands=outs[3 * n:4 * n], token=outs[-1],
                axes=list(axes), gather=gather)


SIBLING = 1
SAME_CORE = (2, 4, 6)


def _gather_relay(xc, after, *, name):
    axis = xc["axes"][0]

    def body(src_ref, land_ref, send1, recv1, after_ref, src_out, land_out, send2, recv2):
        x, y, c, me = _mesh_place()
        sibling, _ = _peer(x, y, c, SIBLING)
        for j, k in enumerate(SAME_CORE):
            peer, pidx = _peer(x, y, c, k)
            src, _, got = _xchg_ends(src_ref, land_ref, axis, True, me, pidx)
            pltpu.make_async_remote_copy(
                src_ref=src, dst_ref=got, send_sem=send1.at[k - 1], recv_sem=recv1.at[k - 1],
                device_id=peer, device_id_type=pl.DeviceIdType.MESH).wait_recv()
            pltpu.make_async_remote_copy(
                src_ref=got, dst_ref=got, send_sem=send2.at[j], recv_sem=recv2.at[j],
                device_id=sibling, device_id_type=pl.DeviceIdType.MESH).start()

    src, land = xc["srcs"][0], xc["lands"][0]
    outs = pl.pallas_call(
        body, name=name,
        out_shape=[pltpu.HBM(src.shape, src.dtype), pltpu.HBM(land.shape, land.dtype)]
        + [pltpu.SemaphoreType.DMA((len(SAME_CORE),))] * 2,
        in_specs=[_HBM, _HBM, _SEM, _SEM, pl.BlockSpec(memory_space=pl.ANY)],
        out_specs=[_HBM, _HBM, _SEM, _SEM], input_output_aliases={0: 0, 1: 1},
        compiler_params=pltpu.CompilerParams(has_side_effects=_EFFECT),
    )(src, land, xc["send"][0], xc["recv"][0], after)
    return dict(src=outs[0], land=outs[1], send=outs[2], recv=outs[3])


def _gather_relay_wait(xc, relay, after, *, name):
    axis = xc["axes"][0]

    def body(src_ref, land_ref, send1, recv1, send2, recv2, after_ref, src_out, land_out):
        x, y, c, me = _mesh_place()
        sibling, sidx = _peer(x, y, c, SIBLING)
        for k in (SIBLING,) + SAME_CORE:
            peer, pidx = _peer(x, y, c, k)
            src, dst, got = _xchg_ends(src_ref, land_ref, axis, True, me, pidx)
            pltpu.make_async_remote_copy(
                src_ref=src, dst_ref=dst, send_sem=send1.at[k - 1], recv_sem=recv1.at[k - 1],
                device_id=peer, device_id_type=pl.DeviceIdType.MESH).wait_send()
        src, dst, got = _xchg_ends(src_ref, land_ref, axis, True, me, sidx)
        pltpu.make_async_remote_copy(
            src_ref=src, dst_ref=got, send_sem=send1.at[SIBLING - 1], recv_sem=recv1.at[SIBLING - 1],
            device_id=sibling, device_id_type=pl.DeviceIdType.MESH).wait_recv()
        src, dst, _ = _xchg_ends(src_ref, land_ref, axis, True, me, me)
        pltpu.make_async_copy(src, dst, send1.at[N_DEV - 1]).wait()
        for j, k in enumerate(SAME_CORE):
            _, pidx = _peer(x, y, c, k)
            _, qidx = _peer(x, y, c, k ^ SIBLING)
            _, _, sent = _xchg_ends(src_ref, land_ref, axis, True, me, pidx)
            _, _, got = _xchg_ends(src_ref, land_ref, axis, True, me, qidx)
            pltpu.make_async_remote_copy(
                src_ref=sent, dst_ref=sent, send_sem=send2.at[j], recv_sem=recv2.at[j],
                device_id=sibling, device_id_type=pl.DeviceIdType.MESH).wait_send()
            pltpu.make_async_remote_copy(
                src_ref=got, dst_ref=got, send_sem=send2.at[j], recv_sem=recv2.at[j],
                device_id=sibling, device_id_type=pl.DeviceIdType.MESH).wait_recv()

    outs = pl.pallas_call(
        body, name=name, out_shape=[pltpu.HBM(relay["src"].shape, relay["src"].dtype),
                                    pltpu.HBM(relay["land"].shape, relay["land"].dtype)],
        in_specs=[_HBM, _HBM, _SEM, _SEM, _SEM, _SEM, pl.BlockSpec(memory_space=pl.ANY)],
        out_specs=[_HBM, _HBM], input_output_aliases={0: 0, 1: 1},
        compiler_params=pltpu.CompilerParams(has_side_effects=_EFFECT),
    )(relay["src"], relay["land"], xc["send"][0], xc["recv"][0], relay["send"], relay["recv"], after)
    return outs[1]


def _xchg_wait(xc, items, after, *, name):
    m = len(items)
    gather = xc["gather"]
    axes = [xc["axes"][i] for i in items]

    def body(*refs):
        src_refs, land_refs = refs[:m], refs[m:2 * m]
        send, recv = refs[2 * m:3 * m], refs[3 * m:4 * m]
        x, y, c, me = _mesh_place()
        for k in range(1, N_DEV):
            peer, pidx = _peer(x, y, c, k)
            for j in range(m):
                src, dst, got = _xchg_ends(src_refs[j], land_refs[j], axes[j], gather, me, pidx)
                pltpu.make_async_remote_copy(
                    src_ref=src, dst_ref=dst, send_sem=send[j].at[k - 1], recv_sem=recv[j].at[k - 1],
                    device_id=peer, device_id_type=pl.DeviceIdType.MESH).wait_send()
                pltpu.make_async_remote_copy(
                    src_ref=src, dst_ref=got, send_sem=send[j].at[k - 1], recv_sem=recv[j].at[k - 1],
                    device_id=peer, device_id_type=pl.DeviceIdType.MESH).wait_recv()
        for j in range(m):
            src, dst, _ = _xchg_ends(src_refs[j], land_refs[j], axes[j], gather, me, me)
            pltpu.make_async_copy(src, dst, send[j].at[N_DEV - 1]).wait()

    bufs = [xc["srcs"][i] for i in items] + [xc["lands"][i] for i in items]
    sems = [xc["send"][i] for i in items] + [xc["recv"][i] for i in items]
    outs = pl.pallas_call(
        body, name=name, out_shape=[pltpu.HBM(b.shape, b.dtype) for b in bufs],
        in_specs=[_HBM] * (2 * m) + [_SEM] * (2 * m) + [pl.BlockSpec(memory_space=pl.ANY)],
        out_specs=[_HBM] * (2 * m), input_output_aliases={j: j for j in range(2 * m)},
        compiler_params=pltpu.CompilerParams(has_side_effects=_EFFECT),
    )(*bufs, *sems, after)
    return outs[m:]


def _cast_bf16(a, *, name):
    r, c = a.shape
    tr = min(r, 512)

    def body(a_ref, o_ref):
        o_ref[...] = a_ref[...].astype(BF16)

    spec = pl.BlockSpec((tr, c), lambda i: (i, 0))
    return pl.pallas_call(body, name=name, grid=(r // tr,), in_specs=[spec], out_specs=spec,
                          out_shape=jax.ShapeDtypeStruct((r, c), BF16), compiler_params=_params(("parallel",)))(a)


def _adamw(slabs, w, m, v, *, name):
    layers, r, c = w.shape
    tr = min(r, 256)

    def body(*refs):
        s_refs = refs[:layers]
        w_ref, m_ref, v_ref, g_ref, d_ref, mo_ref, vo_ref = refs[layers:]
        for l in range(layers):
            g = s_refs[l][0].astype(F32)
            for i in range(1, N_DEV):
                g = g + s_refs[l][i].astype(F32)
            m2 = ADAM_B1 * m_ref[l] + (1.0 - ADAM_B1) * g
            v2 = ADAM_B2 * v_ref[l] + (1.0 - ADAM_B2) * (g * g)
            m_hat = m2 / (1.0 - ADAM_B1 ** ADAM_STEP)
            v_hat = v2 / (1.0 - ADAM_B2 ** ADAM_STEP)
            g_ref[l] = g
            d_ref[l] = -ADAM_LR * (m_hat / (jnp.sqrt(v_hat) + ADAM_EPS) + ADAM_WD * w_ref[l])
            mo_ref[l] = m2
            vo_ref[l] = v2

    spec = pl.BlockSpec((layers, tr, c), lambda i: (0, i, 0))
    return pl.pallas_call(
        body, name=name, grid=(r // tr,),
        in_specs=[pl.BlockSpec((N_DEV, tr, c), lambda i: (0, i, 0))] * layers + [spec, spec, spec],
        out_specs=[spec] * 4, out_shape=[jax.ShapeDtypeStruct((layers, r, c), F32)] * 4,
        compiler_params=_params(("parallel",)),
    )(*slabs, w, m, v)


GATHER_AXIS = {"attn_w_in": 1, "attn_w_out": 0, "ffn_w_up0": 1, "ffn_w_down0": 0, "hgrn_w_in": 1, "hgrn_w_out": 0,
               "hgrn_norm_g": 1, "ffn_w_up1": 1, "ffn_w_down1": 0}
GATHER_STAGES = (("attn_w_in",), ("attn_w_out", "ffn_w_up0", "ffn_w_down0", "hgrn_w_in", "hgrn_w_out", "hgrn_norm_g"),
                 ("ffn_w_up1", "ffn_w_down1"))
GATHER_WAITS = ((("attn_w_in",), 0, 1), (("attn_w_out", "ffn_w_up0", "ffn_w_down0"), 1, 2),
                (("hgrn_w_in", "hgrn_w_out", "hgrn_norm_g"), 1, None), (("ffn_w_up1", "ffn_w_down1"), 2, None))
SCATTER_AXIS = dict(GATHER_AXIS, small=None)
BIG = ("attn_w_in", "attn_w_out", "hgrn_w_in", "hgrn_w_out", "ffn_w_up", "ffn_w_down")
SMALL = ("lb_logits", "ln_mix_g", "ln_mix_b", "ln_ffn_g", "ln_ffn_b")
SMALL_ROW = {"ln_mix_g": 0, "ln_mix_b": 2, "ln_ffn_g": 4, "ln_ffn_b": 6, "lb_logits": 8}
NORM_G_ROW = 10
LOSS_ROW = 11


def kernel(x, attn_w_in, attn_w_out, hgrn_w_in, hgrn_w_out, hgrn_norm_g, lb_logits, ln_mix_g, ln_mix_b, ln_ffn_g, ln_ffn_b, ffn_w_up, ffn_w_down, loss_target, m_attn_w_in, m_attn_w_out, m_hgrn_w_in, m_hgrn_w_out, m_hgrn_norm_g, m_lb_logits, m_ln_mix_g, m_ln_mix_b, m_ln_ffn_g, m_ln_ffn_b, m_ffn_w_up, m_ffn_w_down, v_attn_w_in, v_attn_w_out, v_hgrn_w_in, v_hgrn_w_out, v_hgrn_norm_g, v_lb_logits, v_ln_mix_g, v_ln_mix_b, v_ln_ffn_g, v_ln_ffn_b, v_ffn_w_up, v_ffn_w_down):
    wts = dict(attn_w_in=attn_w_in, attn_w_out=attn_w_out, hgrn_w_in=hgrn_w_in, hgrn_w_out=hgrn_w_out,
               hgrn_norm_g=hgrn_norm_g, lb_logits=lb_logits, ln_mix_g=ln_mix_g, ln_mix_b=ln_mix_b, ln_ffn_g=ln_ffn_g,
               ln_ffn_b=ln_ffn_b, ffn_w_up=ffn_w_up, ffn_w_down=ffn_w_down)
    mom = dict(attn_w_in=m_attn_w_in, attn_w_out=m_attn_w_out, hgrn_w_in=m_hgrn_w_in, hgrn_w_out=m_hgrn_w_out,
               hgrn_norm_g=m_hgrn_norm_g, lb_logits=m_lb_logits, ln_mix_g=m_ln_mix_g, ln_mix_b=m_ln_mix_b,
               ln_ffn_g=m_ln_ffn_g, ln_ffn_b=m_ln_ffn_b, ffn_w_up=m_ffn_w_up, ffn_w_down=m_ffn_w_down)
    vel = dict(attn_w_in=v_attn_w_in, attn_w_out=v_attn_w_out, hgrn_w_in=v_hgrn_w_in, hgrn_w_out=v_hgrn_w_out,
               hgrn_norm_g=v_hgrn_norm_g, lb_logits=v_lb_logits, ln_mix_g=v_ln_mix_g, ln_mix_b=v_ln_mix_b,
               ln_ffn_g=v_ln_ffn_g, ln_ffn_b=v_ln_ffn_b, ffn_w_up=v_ffn_w_up, ffn_w_down=v_ffn_w_down)
    me = 4 * lax.axis_index("x") + 2 * lax.axis_index("y") + lax.axis_index("c")

    src = {"attn_w_in": attn_w_in[0], "attn_w_out": attn_w_out[0], "hgrn_w_in": hgrn_w_in[0], "hgrn_w_out": hgrn_w_out[0],
           "ffn_w_up0": ffn_w_up[0], "ffn_w_down0": ffn_w_down[0], "ffn_w_up1": ffn_w_up[1], "ffn_w_down1": ffn_w_down[1]}
    gathers, got = {}, {}

    def start_gather(stage, deps):
        shards, lands = [], []
        for nm in GATHER_STAGES[stage]:
            sh = hgrn_norm_g if nm == "hgrn_norm_g" else _cast_bf16(src[nm], name=f"cast_{nm}")
            ax = GATHER_AXIS[nm]
            shape = list(sh.shape)
            shape[ax] *= N_DEV
            shards.append(sh)
            lands.append(lax.empty(tuple(shape), sh.dtype))
        peers = (SIBLING,) + SAME_CORE if stage == 0 else tuple(range(1, N_DEV))
        gathers[stage] = _xchg_start(shards, lands, [GATHER_AXIS[nm] for nm in GATHER_STAGES[stage]], gather=True,
                                     name=f"gather_start_{stage}", deps=deps, peers=peers)
        return [gathers[stage]["token"]]

    def get_w(name, after):
        deps = []
        if name not in got:
            group, stage, then = [w for w in GATHER_WAITS if name in w[0]][0]
            xc = gathers[stage]
            if stage == 0:
                relay = _gather_relay(xc, xc["token"], name="gather_relay")
                res = [_gather_relay_wait(xc, relay, xc["token"], name=f"gather_wait_{group[0]}")]
            else:
                res = _xchg_wait(xc, [GATHER_STAGES[stage].index(nm) for nm in group], after,
                                 name=f"gather_wait_{group[0]}")
            got.update(zip(group, res))
            if then is not None:
                deps = start_gather(then, [res[0]])
        return got[name], deps

    first = start_gather(0, [])

    scattered = {}

    def on_grads(tag, grads):
        gnames = list(grads)
        axes = [SCATTER_AXIS[nm] for nm in gnames]
        stacks = []
        for nm, ax in zip(gnames, axes):
            shape = list(grads[nm].shape)
            if ax is not None:
                shape[ax] //= N_DEV
            stacks.append(lax.empty((N_DEV, *shape), grads[nm].dtype))
        scattered[tag] = (gnames, _xchg_start([grads[nm] for nm in gnames], stacks, axes, gather=False,
                                              name=f"scatter_start_{tag}"))
        return [scattered[tag][1]["token"]]

    sm = jax.nn.softmax(lb_logits, axis=0)
    csum = jnp.cumsum(sm, axis=0)
    small = dict(lb=(csum - csum[0:1])[1:2], lb_logits=lb_logits, ln_mix_g=ln_mix_g, ln_mix_b=ln_mix_b,
                 ln_ffn_g=ln_ffn_g, ln_ffn_b=ln_ffn_b)
    grad_x = _local_step(x[0], loss_target[0], get_w, small, on_grads, deps=first)
    out = {}

    def stack_small(src_):
        rows = [None] * SMALL_ROWS
        for name in SMALL:
            rows[SMALL_ROW[name]], rows[SMALL_ROW[name] + 1] = src_[name][0:1], src_[name][1:2]
        zero = jnp.zeros((1, x.shape[-1]), F32)
        return jnp.concatenate([zero if r is None else r for r in rows], axis=0)[None]

    def update(name, slabs):
        shape = wts[name].shape
        out[name] = [r.reshape(shape) for r in _adamw(slabs, wts[name], mom[name], vel[name], name=f"adamw_{name}")]
        return out[name][0]

    slabs, after = {}, grad_x
    for tag, (gnames, xc) in scattered.items():
        slabs.update(zip(gnames, _xchg_wait(xc, list(range(len(gnames))), after, name=f"scatter_wait_{tag}")))
        if tag == "ffn1":
            continue
        if tag == "ffn0":
            update("ffn_w_down", [slabs["ffn_w_down0"], slabs["ffn_w_down1"]])
            after = update("ffn_w_up", [slabs["ffn_w_up0"], slabs["ffn_w_up1"]])
        elif tag == "hgrn":
            update("hgrn_w_out", [slabs["hgrn_w_out"]])
            after = update("hgrn_w_in", [slabs["hgrn_w_in"]])
        elif tag == "attn_out":
            after = update("attn_w_out", [slabs["attn_w_out"]])
        else:
            after = update("attn_w_in", [slabs["attn_w_in"]])
    res = _adamw([slabs["small"]], stack_small(wts), stack_small(mom), stack_small(vel), name="adamw_small")
    for name in SMALL:
        out[name] = [r[0, SMALL_ROW[name]:SMALL_ROW[name] + 2] for r in res]
    loss = res[0][0, LOSS_ROW, 0]
    ng = hgrn_norm_g.shape[-1]
    ng_slabs = lax.dynamic_slice(slabs["small"], (0, NORM_G_ROW, me * ng), (N_DEV, 1, ng))
    out["hgrn_norm_g"] = [r[0] for r in _adamw([ng_slabs], hgrn_norm_g[None], m_hgrn_norm_g[None],
                                                v_hgrn_norm_g[None], name="adamw_norm_g")]
    order =("attn_w_in", "attn_w_out", "hgrn_w_in", "hgrn_w_out", "hgrn_norm_g", "lb_logits", "ln_mix_g", "ln_mix_b",
             "ln_ffn_g", "ln_ffn_b", "ffn_w_up", "ffn_w_down")
    return (loss, grad_x[None], *[out[nm][0] for nm in order], *[out[nm][1] for nm in order],
            *[out[nm][2] for nm in order], *[out[nm][3] for nm in order])
```

```python
import jax
import jax.numpy as jnp
from jax import lax
from jax.experimental import pallas as pl
from jax.experimental.pallas import tpu as pltpu

F32 = jnp.float32
BF16 = jnp.bfloat16

N_DEV = 8
LANES = 128
D_MODEL = 1024
ATTN_HEAD_DIM = 64
ATTN_HEADS = 16
ATTN_BLK = 128
DILATIONS = (1, 4, 16)
ROPE_THETA = 10000.0
HGRN_HEADS = 8
HGRN_CHUNK = 64
D_FF = 4096
LN_EPS = 1e-5
RMS_EPS = 1e-6
DEPTH = 2
ALPHA = (2 * DEPTH) ** 0.25
ADAM_LR, ADAM_B1, ADAM_B2, ADAM_EPS, ADAM_WD, ADAM_STEP = 0.001, 0.9, 0.999, 1e-08, 0.01, 10
VMEM_LIMIT = 48 * 1024 * 1024

_NT = (((1,), (1,)), ((), ()))
_TN = (((0,), (0,)), ((), ()))


def _dot(a, b):
    return jnp.dot(a, b, preferred_element_type=F32)


def _dot_nt(a, b):
    return lax.dot_general(a, b, _NT, preferred_element_type=F32)


def _dot_tn(a, b):
    return lax.dot_general(a, b, _TN, preferred_element_type=F32)


def _split3(x):
    p1 = x.astype(BF16)
    r1 = x - p1.astype(F32)
    p2 = r1.astype(BF16)
    p3 = (r1 - p2.astype(F32)).astype(BF16)
    return p1, p2, p3


def _exact_dot(sel3, x):
    return _dot(sel3, jnp.concatenate(_split3(x), axis=0))


def _exact_dot_r(x, sel3):
    return _dot(jnp.concatenate(_split3(x), axis=1), sel3)


def _params(sem=None):
    return pltpu.CompilerParams(dimension_semantics=sem, vmem_limit_bytes=VMEM_LIMIT)


def _mm(a, b, mode, *, name, m, n, k, tm=1024, tn=1024, tk=1024, out_dtypes=(F32,), epi=None, a_pre=None,
        tile_extras=(), row_extras=(), vec_extras=(), row_outs=(), vec_outs=0, a_split=1, b_split=1, out_split=1,
        b_col_off=0, b_k_off=0, out_col_off=0, out_cols=None, alias=None, epi_wants_j=False, deps=(), t_out=False):
    tm, tn, tk = min(tm, m), min(tn, n), min(tk, k)
    assert m % tm == 0 and n % tn == 0 and k % tk == 0, (name, m, n, k, tm, tn, tk)
    gm, gn, gk = m // tm, n // tn, k // tk
    if mode in ("nn", "nt"):
        if a_split > 1:
            kc = (k // a_split) // tk
            a_spec = pl.BlockSpec((None, tm, tk), lambda i, j, kk: (kk // kc, i, kk % kc))
        else:
            a_spec = pl.BlockSpec((tm, tk), lambda i, j, kk: (i, kk))
    else:
        a_spec = pl.BlockSpec((tk, tm), lambda i, j, kk: (kk, i))
    if mode in ("nn", "tn"):
        if b_split > 1:
            nc = (n // b_split) // tn
            b_spec = pl.BlockSpec((None, tk, tn), lambda i, j, kk: (j // nc, kk, j % nc))
        else:
            b_spec = pl.BlockSpec((tk, tn), lambda i, j, kk: (kk + b_k_off, j + b_col_off))
    else:
        b_spec = pl.BlockSpec((tn, tk), lambda i, j, kk: (j + b_col_off, kk + b_k_off))
    if out_split > 1:
        nco = (n // out_split) // tn
        o_spec = pl.BlockSpec((None, tm, tn), lambda i, j, kk: (j // nco, i, j % nco))
        o_shape = (out_split, m, n // out_split)
    else:
        o_spec = pl.BlockSpec((tm, tn), lambda i, j, kk: (i, j + out_col_off))
        o_shape = (m, out_cols if out_cols is not None else n)
    n_ex = len(tile_extras) + len(row_extras) + len(vec_extras)
    n_out = len(out_dtypes)
    n_plain = n_out + len(row_outs)
    assert not vec_outs or gn == 1
    if epi is None:
        def epi(acc):
            return (acc,)
    dot = {"nn": _dot, "nt": _dot_nt, "tn": _dot_tn}[mode]

    def body(*refs):
        a_ref, b_ref = refs[0], refs[1]
        ex = refs[2:2 + n_ex]
        outs = refs[2 + n_ex + (1 if alias is not None else 0) + len(deps):][:n_plain + vec_outs + (1 if t_out else 0)]
        ii = pl.program_id(0)
        jj = pl.program_id(1)

        def product():
            av = a_ref[...]
            if a_pre is not None:
                av = a_pre(av)
            return dot(av.astype(BF16), b_ref[...].astype(BF16))

        def finish(total):
            lead = (jj,) if epi_wants_j else ()
            res = epi(*lead, total, *[e[...] for e in ex])
            for o, r in zip(outs[:n_plain], res):
                o[...] = r.astype(o.dtype)
            for o, r in zip(outs[n_plain:n_plain + vec_outs], res[n_plain:]):
                @pl.when(ii == 0)
                def _(o=o, r=r):
                    o[...] = r

                @pl.when(ii > 0)
                def _(o=o, r=r):
                    o[...] += r
            if t_out:
                outs[-1][...] = res[-1].astype(outs[-1].dtype)

        if gk == 1:
            finish(product())
        else:
            acc = refs[-1]
            kk = pl.program_id(2)

            @pl.when(kk == 0)
            def _():
                acc[...] = product()

            @pl.when(kk > 0)
            def _():
                acc[...] += product()

            @pl.when(kk == gk - 1)
            def _():
                finish(acc[...])

    in_specs = [a_spec, b_spec] + [o_spec] * len(tile_extras)
    in_specs += [pl.BlockSpec((tm, r.shape[1]), lambda i, j, kk: (i, 0)) if r.shape[0] == m else
                 pl.BlockSpec((r.shape[0], tm), lambda i, j, kk: (0, i)) for r in row_extras]
    in_specs += [pl.BlockSpec((1, tn), lambda i, j, kk: (0, j))] * len(vec_extras)
    args = [a, b] + list(tile_extras) + list(row_extras) + list(vec_extras)
    io_alias = {}
    if alias is not None:
        in_specs.append(pl.BlockSpec(memory_space=pl.ANY))
        args.append(alias)
        io_alias = {len(args) - 1: 0}
    in_specs += [pl.BlockSpec(memory_space=pl.ANY)] * len(deps)
    args += list(deps)
    out_specs = [o_spec] * n_out
    out_shape = [jax.ShapeDtypeStruct(o_shape, dt) for dt in out_dtypes]
    for dt, w in row_outs:
        out_specs.append(pl.BlockSpec((tm, w), lambda i, j, kk: (i, 0)))
        out_shape.append(jax.ShapeDtypeStruct((m, w), dt))
    out_specs += [pl.BlockSpec((1, tn), lambda i, j, kk: (0, j))] * vec_outs
    out_shape += [jax.ShapeDtypeStruct((1, n), F32)] * vec_outs
    if t_out:
        assert out_split > 1
        out_specs.append(pl.BlockSpec((None, tn, tm), lambda i, j, kk: (j // nco, j % nco, i)))
        out_shape.append(jax.ShapeDtypeStruct((out_split, n // out_split, m), out_dtypes[0]))
    out = pl.pallas_call(
        body, name=name, grid=(gm, gn, gk), in_specs=in_specs, out_specs=out_specs, out_shape=out_shape,
        scratch_shapes=[pltpu.VMEM((tm, tn), F32)] if gk > 1 else [],
        input_output_aliases=io_alias,
        compiler_params=_params(("arbitrary" if vec_outs else "parallel", "parallel", "arbitrary")),
    )(*args)
    return out[0] if len(out) == 1 else out


def _rope_tables(seq, dil):
    pos = jnp.arange(seq, dtype=jnp.int32).reshape(seq // dil, dil).T.reshape(seq)
    half = ATTN_HEAD_DIM // 2
    inv = ROPE_THETA ** (-jnp.arange(half, dtype=F32) * (2.0 / ATTN_HEAD_DIM))
    ang = pos.astype(F32)[:, None] * inv[None, :]
    cos, sin = jnp.cos(ang), jnp.sin(ang)
    reps = LANES // ATTN_HEAD_DIM
    return (jnp.tile(jnp.concatenate([cos, cos], axis=1), (1, reps)),
            jnp.tile(jnp.concatenate([-sin, sin], axis=1), (1, reps)))


def _rotate(x, c, ss, sign=1.0):
    w = x.shape[-1]
    half = ATTN_HEAD_DIM // 2
    lane = lax.broadcasted_iota(jnp.int32, x.shape, 1)
    first = (lane % ATTN_HEAD_DIM) < half
    partner = jnp.where(first, pltpu.roll(x, w - half, 1), pltpu.roll(x, half, 1))
    reps = w // LANES
    if reps > 1:
        c = jnp.concatenate([c] * reps, axis=1)
        ss = jnp.concatenate([ss] * reps, axis=1)
    return x * c + sign * (partner * ss)


def _rotate_t(xt, ct, sst):
    half = ATTN_HEAD_DIM // 2
    heads = xt.shape[0] // ATTN_HEAD_DIM
    parts = []
    for h in range(heads):
        lo = h * ATTN_HEAD_DIM
        parts += [xt[lo + half:lo + 2 * half], xt[lo:lo + half]]
    return (xt * jnp.concatenate([ct] * heads, axis=0)
            + jnp.concatenate(parts, axis=0) * jnp.concatenate([sst] * heads, axis=0))


ROW_TILE = 512


def _ln_epi(acc, x, g, b):
    u = ALPHA * x + acc
    mu = jnp.mean(u, axis=-1, keepdims=True)
    uc = u - mu
    var = jnp.mean(uc * uc, axis=-1, keepdims=True)
    rstd = lax.rsqrt(var + LN_EPS)
    xh = uc * rstd
    out = xh * g + b
    return out, out, xh, rstd


def _ln_grad(dy, xh, rstd, g):
    dxh = dy * g
    m1 = jnp.mean(dxh, axis=-1, keepdims=True)
    m2 = jnp.mean(dxh * xh, axis=-1, keepdims=True)
    du = rstd * (dxh - m1 - xh * m2)
    return du, du, jnp.sum(dy * xh, axis=0, keepdims=True), jnp.sum(dy, axis=0, keepdims=True)


def _ln_grad_epi(acc, du_next, xh, rstd, g):
    return _ln_grad(acc + ALPHA * du_next, xh, rstd, g)


def _loss_ln_bwd(y, target, xhat, rstd, g):
    s, d = y.shape

    def body(y_ref, t_ref, xh_ref, r_ref, g_ref, du_ref, dub_ref, dg_ref, db_ref, sq_ref):
        @pl.when(pl.program_id(0) == 0)
        def _():
            dg_ref[...] = jnp.zeros_like(dg_ref)
            db_ref[...] = jnp.zeros_like(db_ref)
            sq_ref[...] = jnp.zeros_like(sq_ref)

        e = y_ref[...] - t_ref[...]
        du, _, dg, db = _ln_grad(e * (1.0 / d), xh_ref[...], r_ref[...], g_ref[...])
        du_ref[...] = du
        dub_ref[...] = du.astype(BF16)
        dg_ref[...] += dg
        db_ref[...] += db
        sq_ref[...] += jnp.sum(e * e, axis=0, keepdims=True)

    row = pl.BlockSpec((ROW_TILE, d), lambda i: (i, 0))
    vec = pl.BlockSpec((1, d), lambda i: (0, 0))
    return pl.pallas_call(
        body, name="loss_ln_bwd", grid=(s // ROW_TILE,),
        in_specs=[row, row, row, pl.BlockSpec((ROW_TILE, 1), lambda i: (i, 0)), vec],
        out_specs=[row, row, vec, vec, vec],
        out_shape=[jax.ShapeDtypeStruct((s, d), F32), jax.ShapeDtypeStruct((s, d), BF16)]
        + [jax.ShapeDtypeStruct((1, d), F32)] * 3,
        compiler_params=_params(("arbitrary",)),
    )(y, target, xhat, rstd, g)


POS_BLK = 2048


def _class_rows(r, dil):
    return pl.ds(r, POS_BLK // dil, stride=dil) if dil > 1 else pl.ds(0, POS_BLK)


def _class_view(a, dil):
    s, w = a.shape
    return a.reshape(dil, s // dil, w)


def _class_spec(dil, all_tiles=True):
    if all_tiles:
        return pl.BlockSpec((dil, POS_BLK // dil, LANES), lambda i, t: (0, i, t))
    return pl.BlockSpec((dil, POS_BLK // dil, LANES), lambda i, t: (0, i, 0))


def _pos_spec(all_tiles=True):
    if all_tiles:
        return pl.BlockSpec((POS_BLK, LANES), lambda i, t: (i, t))
    return pl.BlockSpec((POS_BLK, LANES), lambda i, t: (i, 0))


def _prep_x(x, deps=()):
    s, d = x.shape

    def body(x_ref, *refs):
        outs = refs[len(deps):]
        for dil, o_ref in zip(DILATIONS, outs):
            for r in range(dil):
                o_ref[r] = x_ref[_class_rows(r, dil), :].astype(BF16)

    outs = pl.pallas_call(
        body, name="prep_x", grid=(s // POS_BLK, d // LANES),
        in_specs=[_pos_spec()] + [pl.BlockSpec(memory_space=pl.ANY)] * len(deps),
        out_specs=[_class_spec(dil) for dil in DILATIONS],
        out_shape=[jax.ShapeDtypeStruct((dil, s // dil, d), BF16) for dil in DILATIONS],
        compiler_params=_params(("parallel", "parallel")),
    )(x, *deps)
    return [o.reshape(s, d) for o in outs]


def _head_expand_matrix():
    h = lax.broadcasted_iota(jnp.int32, (LANES, D_MODEL), 0)
    l = lax.broadcasted_iota(jnp.int32, (LANES, D_MODEL), 1)
    return (l // ATTN_HEAD_DIM == h).astype(BF16)


def _attn_fwd(qkv, dil, *, name):
    _, s, d = qkv.shape
    nq = s // ATTN_BLK
    per = nq // dil
    scale = ATTN_HEAD_DIM ** -0.5
    tiles = d // LANES

    def body(q_ref, kc_ref, kp_ref, vc_ref, vp_ref, o_ref, lse_ref):
        qb = pl.program_id(0)
        first = (qb % per) == 0
        qi = lax.broadcasted_iota(jnp.int32, (ATTN_BLK, 2 * ATTN_BLK), 0)
        kj = lax.broadcasted_iota(jnp.int32, (ATTN_BLK, 2 * ATTN_BLK), 1)
        dist = qi + ATTN_BLK - kj
        valid = (dist >= 0) & (dist <= ATTN_BLK) & ((kj >= ATTN_BLK) | jnp.logical_not(first))
        lane = lax.broadcasted_iota(jnp.int32, (ATTN_BLK, LANES), 1)
        lse_tile = jnp.zeros((ATTN_BLK, LANES), F32)
        zero = jnp.zeros((), BF16)
        in_head = [(lane // ATTN_HEAD_DIM) == hh for hh in range(2)]
        for t0 in range(0, tiles, ATTN_TILE_GROUP):
            group = range(t0, t0 + ATTN_TILE_GROUP)
            heads = [(t, hh) for t in group for hh in range(2)]
            cols = {t: pl.ds(t * LANES, LANES) for t in group}
            k2 = {t: jnp.concatenate([kp_ref[:, cols[t]], kc_ref[:, cols[t]]], axis=0) for t in group}
            v2 = {t: jnp.concatenate([vp_ref[:, cols[t]], vc_ref[:, cols[t]]], axis=0) for t in group}
            sc = {(t, hh): jnp.where(valid, _dot_nt(jnp.where(in_head[hh], q_ref[:, cols[t]], zero), k2[t]) * scale,
                                     -jnp.inf) for t, hh in heads}
            mx = {i: jnp.max(sc[i], axis=-1, keepdims=True) for i in heads}
            p = {i: jnp.exp(sc[i] - mx[i]) for i in heads}
            l = {i: jnp.sum(p[i], axis=-1, keepdims=True) for i in heads}
            oh = {i: _dot(p[i].astype(BF16), v2[i[0]]) / l[i] for i in heads}
            for t in group:
                o_ref[:, cols[t]] = jnp.where(in_head[0], oh[t, 0], oh[t, 1])
                for hh in range(2):
                    lse_tile = jnp.where(lane == 2 * t + hh, mx[t, hh] + jnp.log(l[t, hh]), lse_tile)
        lse_ref[...] = lse_tile

    def blk(piece, prev):
        if prev:
            return pl.BlockSpec((None, ATTN_BLK, d), lambda i: (piece, jnp.maximum(i - 1, 0), 0))
        return pl.BlockSpec((None, ATTN_BLK, d), lambda i: (piece, i, 0))

    return pl.pallas_call(
        body, name=name, grid=(nq,),
        in_specs=[blk(0, False), blk(1, False), blk(1, True), blk(2, False), blk(2, True)],
        out_specs=[pl.BlockSpec((ATTN_BLK, d), lambda i: (i, 0)), pl.BlockSpec((ATTN_BLK, LANES), lambda i: (i, 0))],
        out_shape=[jax.ShapeDtypeStruct((s, d), F32), jax.ShapeDtypeStruct((s, LANES), F32)],
        compiler_params=_params(("parallel",)),
    )(qkv, qkv, qkv, qkv, qkv)


def _attn_combine(os_, lses):
    s, d = os_[0].shape
    sel = jnp.concatenate([_head_expand_matrix()] * 3, axis=0)

    def body(o0, o1, o2, l0, l1, l2, sel_ref, of_ref, ob_ref, lt_ref, o_pos, l_pos):
        for g, (dil, o_ref, l_ref) in enumerate(zip(DILATIONS, (o0, o1, o2), (l0, l1, l2))):
            for r in range(dil):
                o_pos[g, _class_rows(r, dil), :] = o_ref[r]
                l_pos[g, _class_rows(r, dil), :] = l_ref[r]
        la, lb_, lc = l_pos[0], l_pos[1], l_pos[2]
        mx = jnp.maximum(jnp.maximum(la, lb_), lc)
        es = (jnp.exp(la - mx), jnp.exp(lb_ - mx), jnp.exp(lc - mx))
        z = es[0] + es[1] + es[2]
        lt_ref[...] = mx + jnp.log(z)
        acc = jnp.zeros((POS_BLK, LANES), F32)
        for g in range(3):
            acc += _exact_dot_r(es[g] / z, sel_ref[...]) * o_pos[g]
        of_ref[...] = acc
        ob_ref[...] = acc.astype(BF16)

    return pl.pallas_call(
        body, name="attn_combine", grid=(s // POS_BLK, d // LANES),
        in_specs=[_class_spec(dil) for dil in DILATIONS] + [_class_spec(dil, False) for dil in DILATIONS]
        + [pl.BlockSpec((3 * LANES, LANES), lambda i, t: (0, t))],
        out_specs=[_pos_spec(), _pos_spec(), _pos_spec(False)],
        out_shape=[jax.ShapeDtypeStruct((s, d), F32), jax.ShapeDtypeStruct((s, d), BF16),
                   jax.ShapeDtypeStruct((s, LANES), F32)],
        scratch_shapes=[pltpu.VMEM((3, POS_BLK, LANES), F32), pltpu.VMEM((3, POS_BLK, LANES), F32)],
        compiler_params=_params(("parallel", "arbitrary")),
    )(*[_class_view(o, dil) for o, dil in zip(os_, DILATIONS)],
      *[_class_view(l, dil) for l, dil in zip(lses, DILATIONS)], sel)


ATTN_TILE_GROUP = 2
AUX_PER_TILE = 12


def _aux_placement():
    h = lax.broadcasted_iota(jnp.int32, (6, LANES, LANES), 1)
    l = lax.broadcasted_iota(jnp.int32, (6, LANES, LANES), 2)
    j = lax.broadcasted_iota(jnp.int32, (6, LANES, LANES), 0)
    target = AUX_PER_TILE * (h // 2) + 3 * (h % 2) + jnp.where(j < 3, j, 3 + j)
    return ((l == target) & (h < ATTN_HEADS)).astype(BF16)


def _attn_bwd_prep(do, o, lse):
    s, d = do.shape
    tiles = d // LANES
    sel_t = jnp.tile(_head_expand_matrix().T.reshape(tiles, LANES, LANES), (1, 3, 1))

    def body(do_ref, o_ref, l_ref, sel_ref, place_ref, *refs):
        outs, delta, aux = refs[:9], refs[9], refs[10]
        t = pl.program_id(1)
        part = _exact_dot_r(do_ref[...] * o_ref[...], sel_ref[...])

        @pl.when(t == 0)
        def _():
            delta[...] = part

        @pl.when(t > 0)
        def _():
            delta[...] += part

        for g, dil in enumerate(DILATIONS):
            for r in range(dil):
                blk = do_ref[_class_rows(r, dil), :].astype(BF16)
                outs[g][r] = blk
                outs[3 + g][r] = blk.T

        @pl.when(t == tiles - 1)
        def _():
            pieces = _split3(l_ref[...]) + _split3(delta[...])
            aux[...] = _dot(jnp.concatenate(pieces, axis=1), place_ref[...])
            for g, dil in enumerate(DILATIONS):
                for r in range(dil):
                    outs[6 + g][r] = aux[_class_rows(r, dil), :].astype(BF16)

    outs = pl.pallas_call(
        body, name="attn_bwd_prep", grid=(s // POS_BLK, tiles),
        in_specs=[_pos_spec(), _pos_spec(), _pos_spec(False),
                  pl.BlockSpec((None, 3 * LANES, LANES), lambda i, t: (t, 0, 0)),
                  pl.BlockSpec((6 * LANES, LANES), lambda i, t: (0, 0))],
        out_specs=[_class_spec(dil) for dil in DILATIONS]
        + [pl.BlockSpec((dil, LANES, POS_BLK // dil), lambda i, t: (0, t, i)) for dil in DILATIONS]
        + [_class_spec(dil, False) for dil in DILATIONS],
        out_shape=[jax.ShapeDtypeStruct((dil, s // dil, d), BF16) for dil in DILATIONS]
        + [jax.ShapeDtypeStruct((dil, d, s // dil), BF16) for dil in DILATIONS]
        + [jax.ShapeDtypeStruct((dil, s // dil, LANES), BF16) for dil in DILATIONS],
        scratch_shapes=[pltpu.VMEM((POS_BLK, LANES), F32), pltpu.VMEM((POS_BLK, LANES), F32)],
        compiler_params=_params(("parallel", "arbitrary")),
    )(do, o, lse, sel_t, _aux_placement().reshape(6 * LANES, LANES))
    return ([a.reshape(s, d) for a in outs[0:3]], list(outs[3:6]), [a.reshape(s, LANES) for a in outs[6:9]])


def _attn_bwd(qkv, qkv_t, do, do_t, aux, tables, dil, *, name):
    _, s, d = qkv.shape
    nq = s // ATTN_BLK
    per = nq // dil
    scale = ATTN_HEAD_DIM ** -0.5
    assert scale == 0.125
    tiles = d // LANES
    half = ATTN_HEAD_DIM

    def body(qd_ref, qo_ref, k_ref, qtd_ref, qto_ref, kt_ref, vt_ref, dod_ref, doo_ref, dotd_ref, doto_ref,
             auxd_ref, auxo_ref, c_ref, ss_ref, out_ref, carry):
        kb = pl.program_id(0)

        @pl.when(kb == 0)
        def _():
            carry[...] = jnp.zeros_like(carry)

        has_next = (kb % per) != (per - 1)
        qi = lax.broadcasted_iota(jnp.int32, (ATTN_BLK, 2 * ATTN_BLK), 0)
        kj = lax.broadcasted_iota(jnp.int32, (ATTN_BLK, 2 * ATTN_BLK), 1) % ATTN_BLK
        valid_d = kj <= qi
        valid_o = (kj >= qi) & has_next
        lane = lax.broadcasted_iota(jnp.int32, (ATTN_BLK, LANES), 1)
        row = lax.broadcasted_iota(jnp.int32, (LANES, ATTN_BLK), 0)
        side = lax.broadcasted_iota(jnp.int32, (LANES, 2 * ATTN_BLK), 0)
        first = lax.broadcasted_iota(jnp.int32, (LANES, 2 * ATTN_BLK), 1) < ATTN_BLK
        c, ss = c_ref[...], ss_ref[...]
        zero = jnp.zeros((), BF16)
        sides = ((qd_ref, qtd_ref, dod_ref, dotd_ref, auxd_ref[...], valid_d),
                 (qo_ref, qto_ref, doo_ref, doto_ref, auxo_ref[...], valid_o))
        both = (0, 1)

        def head_halves(x, index):
            axis = 0 if index is lane else 1
            return jnp.concatenate([jnp.where(index < half, x, zero), jnp.where(index >= half, x, zero)], axis=axis)

        for t0 in range(0, tiles, ATTN_TILE_GROUP):
            group = range(t0, t0 + ATTN_TILE_GROUP)
            cols = {t: pl.ds(t * LANES, LANES) for t in group}
            kk, kk_t, vv_t = {}, {}, {}
            for t in group:
                base = AUX_PER_TILE * t
                hit = lambda lo: ((first & (side >= base + lo) & (side < base + lo + 3))
                                  | (jnp.logical_not(first) & (side >= base + lo + 3) & (side < base + lo + 6)))
                kk[t] = head_halves(k_ref[:, cols[t]], lane)
                kk_t[t] = jnp.concatenate([head_halves(kt_ref[cols[t], :], row),
                                           jnp.where(hit(0), -1.0 / scale, 0.0).astype(BF16)], axis=0)
                vv_t[t] = jnp.concatenate([head_halves(vt_ref[cols[t], :], row),
                                           jnp.where(hit(6), -1.0, 0.0).astype(BF16)], axis=0)
            sc = {(t, w): _dot(jnp.concatenate([sides[w][0][:, cols[t]], sides[w][4]], axis=1), kk_t[t])
                  for t in group for w in both}
            dpd = {(t, w): _dot(jnp.concatenate([sides[w][2][:, cols[t]], sides[w][4]], axis=1), vv_t[t])
                   for t in group for w in both}
            p = {i: jnp.where(sides[i[1]][5], jnp.exp(sc[i] * scale), 0.0) for i in sc}
            ds = {i: (p[i] * dpd[i] * scale).astype(BF16) for i in sc}
            pb = {i: p[i].astype(BF16) for i in sc}
            dv_t = {t: sum(_dot(sides[w][3][cols[t], :], pb[t, w]) for w in both) for t in group}
            dk_t = {t: sum(_dot(sides[w][1][cols[t], :], ds[t, w]) for w in both) for t in group}
            dq = {i: _dot(ds[i], kk[i[0]]) for i in sc}
            for t in group:
                dq_now = carry[:, cols[t]] + dq[t, 0]
                carry[:, cols[t]] = dq[t, 1]
                dk = jnp.where(row < half, dk_t[t][:, :ATTN_BLK], dk_t[t][:, ATTN_BLK:]).T
                dv = jnp.where(row < half, dv_t[t][:, :ATTN_BLK], dv_t[t][:, ATTN_BLK:]).T
                out_ref[0, :, cols[t]] = _rotate(dq_now, c, ss, -1.0).astype(BF16)
                out_ref[1, :, cols[t]] = _rotate(dk, c, ss, -1.0).astype(BF16)
                out_ref[2, :, cols[t]] = dv.astype(BF16)

    def nxt(i):
        return jnp.minimum(i + 1, nq - 1)

    def piece(p, shift):
        if shift:
            return pl.BlockSpec((None, ATTN_BLK, d), lambda i: (p, nxt(i), 0))
        return pl.BlockSpec((None, ATTN_BLK, d), lambda i: (p, i, 0))

    def piece_t(p, shift):
        if shift:
            return pl.BlockSpec((None, d, ATTN_BLK), lambda i: (p, 0, nxt(i)))
        return pl.BlockSpec((None, d, ATTN_BLK), lambda i: (p, 0, i))

    def rows(width, shift):
        if shift:
            return pl.BlockSpec((ATTN_BLK, width), lambda i: (nxt(i), 0))
        return pl.BlockSpec((ATTN_BLK, width), lambda i: (i, 0))

    def do_t_spec(shift):
        if shift:
            return pl.BlockSpec((None, d, ATTN_BLK), lambda i: (nxt(i) // per, 0, nxt(i) % per))
        return pl.BlockSpec((None, d, ATTN_BLK), lambda i: (i // per, 0, i % per))

    return pl.pallas_call(
        body, name=name, grid=(nq,),
        in_specs=[piece(0, False), piece(0, True), piece(1, False),
                  piece_t(0, False), piece_t(0, True), piece_t(1, False), piece_t(2, False),
                  rows(d, False), rows(d, True), do_t_spec(False), do_t_spec(True),
                  rows(LANES, False), rows(LANES, True), rows(LANES, False), rows(LANES, False)],
        out_specs=pl.BlockSpec((3, ATTN_BLK, d), lambda i: (0, i, 0)),
        out_shape=jax.ShapeDtypeStruct((3, s, d), BF16),
        scratch_shapes=[pltpu.VMEM((ATTN_BLK, d), F32)],
        compiler_params=_params(("arbitrary",)),
    )(qkv, qkv, qkv, qkv_t, qkv_t, qkv_t, qkv_t, do, do, do_t, do_t, aux, aux, *tables)


def _dx_combine(du, parts):
    s, d = du.shape

    def body(du_ref, p0, p1, p2, out_ref):
        out_ref[...] = ALPHA * du_ref[...] + p0[0]
        for dil, p_ref in zip(DILATIONS[1:], (p1, p2)):
            for r in range(dil):
                out_ref[_class_rows(r, dil), :] += p_ref[r]

    return pl.pallas_call(
        body, name="dx_combine", grid=(s // POS_BLK, d // LANES),
        in_specs=[_pos_spec()] + [_class_spec(dil) for dil in DILATIONS], out_specs=_pos_spec(),
        out_shape=jax.ShapeDtypeStruct((s, d), F32),
        compiler_params=_params(("parallel", "parallel")),
    )(du, *[_class_view(p, dil) for p, dil in zip(parts, DILATIONS)])


HGRN_ROWS = 1024


def _tri(lower, copies=1):
    i = lax.broadcasted_iota(jnp.int32, (HGRN_CHUNK, HGRN_CHUNK * copies), 0)
    j = lax.broadcasted_iota(jnp.int32, (HGRN_CHUNK, HGRN_CHUNK * copies), 1) % HGRN_CHUNK
    return (j <= i) if lower else (j >= i)


def _hgrn_gates(qr, z, lb):
    sq = jax.nn.sigmoid(qr)
    e = jnp.exp(-jnp.abs(z))
    big = 1.0 / (1.0 + e)
    small = e * big
    sg = jnp.where(z >= 0, big, small)
    sn = jnp.where(z >= 0, small, big)
    return sq, qr * sq, sg, sn, lb + (1.0 - lb) * sg, (1.0 - lb) * sn


def _hgrn_fwd(p2, lb, norm_g):
    _, s, d = p2.shape
    nblk = s // HGRN_ROWS
    cps = HGRN_ROWS // HGRN_CHUNK

    def body(p_ref, lb_ref, g_ref, o_ref, on_ref, st_ref, state):
        @pl.when(pl.program_id(1) == 0)
        def _():
            state[...] = jnp.zeros_like(state)

        chunks = [pl.ds(c * HGRN_CHUNK, HGRN_CHUNK) for c in range(cps)]
        ltri = _tri(True)
        lsel = _tri(True, 3).astype(BF16)
        _, q, _, _, f, key = _hgrn_gates(p_ref[0], p_ref[1], lb_ref[...])
        lf = jnp.log(f)
        v = [p_ref[2, rows, :].astype(BF16) for rows in chunks]
        b = [_exact_dot(lsel, lf[c * HGRN_CHUNK:(c + 1) * HGRN_CHUNK]) for c in range(cps)]
        b_last = [bc[HGRN_CHUNK - 1:HGRN_CHUNK, :] for bc in b]
        q = [q[c * HGRN_CHUNK:(c + 1) * HGRN_CHUNK] for c in range(cps)]
        key = [key[c * HGRN_CHUNK:(c + 1) * HGRN_CHUNK] for c in range(cps)]
        qd = [(q[c] * jnp.exp(b[c])).astype(BF16) for c in range(cps)]
        kd = [(key[c] * jnp.exp(-b[c])).astype(BF16) for c in range(cps)]
        k2 = [(key[c] * jnp.exp(b_last[c] - b[c])).astype(BF16) for c in range(cps)]
        a = [jnp.where(ltri, _dot_nt(qd[c], kd[c]), 0.0).astype(BF16) for c in range(cps)]
        kv = [_dot_tn(v[c], k2[c]) for c in range(cps)]
        st, sts = state[...], []
        for c in range(cps):
            sts.append(st)
            st_ref[0, c] = st
            st = st * jnp.exp(b_last[c]) + kv[c]
        state[...] = st
        gv = g_ref[...]
        for c in range(cps):
            o = _dot(a[c], v[c]) + _dot_nt(qd[c], sts[c].astype(BF16))
            o_ref[chunks[c], :] = o
            r = lax.rsqrt(jnp.mean(o * o, axis=-1, keepdims=True) + RMS_EPS)
            on_ref[chunks[c], :] = (o * r * gv).astype(BF16)

    vec = pl.BlockSpec((1, LANES), lambda h, c: (0, h))
    col = pl.BlockSpec((HGRN_ROWS, LANES), lambda h, c: (c, h))
    return pl.pallas_call(
        body, name="hgrn_fwd", grid=(HGRN_HEADS, nblk),
        in_specs=[pl.BlockSpec((3, HGRN_ROWS, LANES), lambda h, c: (0, c, h)), vec, vec],
        out_specs=[col, col, pl.BlockSpec((1, cps, LANES, LANES), lambda h, c: (h, c, 0, 0))],
        out_shape=[jax.ShapeDtypeStruct((s, d), F32), jax.ShapeDtypeStruct((s, d), BF16),
                   jax.ShapeDtypeStruct((HGRN_HEADS, s // HGRN_CHUNK, LANES, LANES), F32)],
        scratch_shapes=[pltpu.VMEM((LANES, LANES), F32)],
        compiler_params=_params(("parallel", "arbitrary")),
    )(p2, lb, norm_g)


def _hgrn_bwd(p2, lb, norm_g, o_raw, states, dyn):
    _, s, d = p2.shape
    nblk = s // HGRN_ROWS
    cps = HGRN_ROWS // HGRN_CHUNK

    def body(p_ref, lb_ref, g_ref, o_ref, st_ref, dy_ref, dp_ref, dg_ref, dlb_ref, dstate):
        @pl.when(pl.program_id(1) == 0)
        def _():
            dstate[...] = jnp.zeros_like(dstate)
            dg_ref[...] = jnp.zeros_like(dg_ref)
            dlb_ref[...] = jnp.zeros_like(dlb_ref)

        n = cps
        cut = lambda t: [t[c * HGRN_CHUNK:(c + 1) * HGRN_CHUNK] for c in range(n)]
        chunks = [pl.ds(c * HGRN_CHUNK, HGRN_CHUNK) for c in range(n)]
        lbv = lb_ref[...]
        ltri = _tri(True)
        lsel = _tri(True, 3).astype(BF16)
        usel = _tri(False, 3).astype(BF16)
        last_row = lax.broadcasted_iota(jnp.int32, (HGRN_CHUNK, LANES), 0) == HGRN_CHUNK - 1
        qr, z = p_ref[0], p_ref[1]
        sq, q, sg, sn, f, key = _hgrn_gates(qr, z, lbv)
        lf = jnp.log(f)
        v = [p_ref[2, rows, :].astype(BF16) for rows in chunks]
        o, dyv = o_ref[...], dy_ref[...]
        r = lax.rsqrt(jnp.mean(o * o, axis=-1, keepdims=True) + RMS_EPS)
        oh = o * r
        doh = dyv * g_ref[...]
        do = cut((r * (doh - oh * jnp.mean(doh * oh, axis=-1, keepdims=True))).astype(BF16))
        dg_ref[...] += jnp.sum(dyv * oh, axis=0, keepdims=True)
        b = [_exact_dot(lsel, lfc) for lfc in cut(lf)]
        b_last = [bc[HGRN_CHUNK - 1:HGRN_CHUNK, :] for bc in b]
        q, key = cut(q), cut(key)
        eb = [jnp.exp(bc) for bc in b]
        enb = [jnp.exp(-bc) for bc in b]
        e2 = [jnp.exp(b_last[c] - b[c]) for c in range(n)]
        dec = [jnp.exp(bl) for bl in b_last]
        qd_f = [q[c] * eb[c] for c in range(n)]
        kd_f = [key[c] * enb[c] for c in range(n)]
        k2_f = [key[c] * e2[c] for c in range(n)]
        qd, kd, k2 = ([t.astype(BF16) for t in ts] for ts in (qd_f, kd_f, k2_f))
        a = [jnp.where(ltri, _dot_nt(qd[c], kd[c]), 0.0).astype(BF16) for c in range(n)]
        da = [jnp.where(ltri, _dot_nt(do[c], v[c]), 0.0).astype(BF16) for c in range(n)]
        grow = [_dot_tn(do[c], qd[c]) for c in range(n)]
        dst, dsts = dstate[...], [None] * n
        for c in reversed(range(n)):
            dsts[c] = dst
            dst = dst * dec[c] + grow[c]
        dstate[...] = dst
        st = [st_ref[0, c] for c in range(n)]
        dstb = [t.astype(BF16) for t in dsts]
        dv = [_dot_tn(a[c], do[c]) + _dot_nt(k2[c], dstb[c]) for c in range(n)]
        dqd = [_dot(da[c], kd[c]) + _dot(do[c], st[c].astype(BF16)) for c in range(n)]
        dkd = [_dot_tn(da[c], qd[c]) for c in range(n)]
        dk2 = [_dot(v[c], dstb[c]) for c in range(n)]
        db = []
        for c in range(n):
            ddec = jnp.sum(dsts[c] * st[c], axis=0, keepdims=True)
            db_last = jnp.sum(dk2[c] * k2_f[c], axis=0, keepdims=True) + ddec * dec[c]
            db.append(dqd[c] * qd_f[c] - dkd[c] * kd_f[c] - dk2[c] * k2_f[c] + jnp.where(last_row, db_last, 0.0))
        dlf = [_exact_dot(usel, dbc) for dbc in db]
        f, sg, sn, sq, qr = cut(f), cut(sg), cut(sn), cut(sq), cut(qr)
        dlb_acc = jnp.zeros((1, LANES), F32)
        for c in range(n):
            dkey = dkd[c] * enb[c] + dk2[c] * e2[c]
            common = dlf[c] / f[c] - dkey
            dlb_acc += jnp.sum(common * sn[c], axis=0, keepdims=True)
            dp_ref[0, chunks[c], :] = (dqd[c] * eb[c] * (sq[c] * (1.0 + qr[c] * (1.0 - sq[c])))).astype(BF16)
            dp_ref[1, chunks[c], :] = (common * ((1.0 - lbv) * sg[c] * sn[c])).astype(BF16)
            dp_ref[2, chunks[c], :] = dv[c].astype(BF16)
        dlb_ref[...] += dlb_acc

    def rev(c):
        return nblk - 1 - c

    vec = pl.BlockSpec((1, LANES), lambda h, c: (0, h))
    col = pl.BlockSpec((HGRN_ROWS, LANES), lambda h, c: (rev(c), h))
    p3 = pl.BlockSpec((3, HGRN_ROWS, LANES), lambda h, c: (0, rev(c), h))
    return pl.pallas_call(
        body, name="hgrn_bwd", grid=(HGRN_HEADS, nblk),
        in_specs=[p3, vec, vec, col, pl.BlockSpec((1, cps, LANES, LANES), lambda h, c: (h, rev(c), 0, 0)), col],
        out_specs=[p3, vec, vec],
        out_shape=[jax.ShapeDtypeStruct((3, s, d), BF16), jax.ShapeDtypeStruct((1, d), F32),
                   jax.ShapeDtypeStruct((1, d), F32)],
        scratch_shapes=[pltpu.VMEM((LANES, LANES), F32)],
        compiler_params=_params(("parallel", "arbitrary")),
    )(p2, lb, norm_g, o_raw, states, dyn)


SMALL_ROWS = 16
FUSED_TM = 512


def _relu2(h):
    r = jnp.maximum(h.astype(F32), 0.0)
    return r * r


def _dact_epi(acc, h):
    return (acc * (2.0 * jnp.maximum(h.astype(F32), 0.0)),)


def _rope_epi(j, acc, ct, sst):
    acc_t = acc.T
    out_t = jnp.where(j >= 2, acc_t, _rotate_t(acc_t, ct, sst)).astype(BF16)
    return out_t.T, out_t


def _local_step(x, target, get_w, small, on_grads, deps=()):
    s, d = x.shape
    lb = small["lb"]
    gq = 3 * d
    gt = gq // 1024

    def ln_of(a, w, x_in, which, layer, *, name, k, a_pre=None, deps=()):
        g, b = small[f"ln_{which}_g"][layer:layer + 1], small[f"ln_{which}_b"][layer:layer + 1]
        return _mm(a, w, "nn", name=name, m=s, n=d, k=k, tm=FUSED_TM, tk=k, a_pre=a_pre, out_dtypes=(F32, BF16, F32),
                   row_outs=((F32, 1),), epi=_ln_epi, tile_extras=(x_in,), vec_extras=(g, b), deps=deps)

    def ln_grad_of(a, w, du_next, xh, rstd, which, layer, *, name, k, a_split=1, deps=()):
        return _mm(a, w, "nt", name=name, m=s, n=d, k=k, tm=FUSED_TM, tk=k // a_split, a_split=a_split,
                   out_dtypes=(F32, BF16), vec_outs=2, epi=_ln_grad_epi, tile_extras=(du_next, xh), row_extras=(rstd,),
                   vec_extras=(small[f"ln_{which}_g"][layer:layer + 1],), deps=deps)

    def ffn_fwd(xb, x_in, w_up, w_down, tag):
        h = _mm(xb, w_up, "nn", name=f"ffn_up_{tag}", m=s, n=D_FF, k=d, out_dtypes=(BF16,))
        return (h,) + tuple(ln_of(h, w_down, x_in, "ffn", tag, name=f"ffn_down_{tag}", k=D_FF, a_pre=_relu2))

    def ffn_bwd(du, dub, xb, h, w_up, w_down, xh, rstd, tag):
        dh = _mm(dub, w_down, "nt", name=f"ffn_dact_{tag}", m=s, n=D_FF, k=d, out_dtypes=(BF16,), epi=_dact_epi,
                 tile_extras=(h,))
        g_down = _mm(h, dub, "tn", name=f"ffn_gdown_{tag}", m=D_FF, n=d, k=s, a_pre=_relu2, out_dtypes=(BF16,))
        g_up = _mm(xb, dh, "tn", name=f"ffn_gup_{tag}", m=d, n=D_FF, k=s, out_dtypes=(BF16,))
        after = on_grads(f"ffn{tag}", {f"ffn_w_down{tag}": g_down, f"ffn_w_up{tag}": g_up})
        return ln_grad_of(dh, w_up, du, xh, rstd, "mix", tag, name=f"ffn_dx_{tag}", k=D_FF, deps=after)

    xs = _prep_x(x, deps)
    w_ain, after = get_w("attn_w_in", None)
    tabs, qkvs, qkv_ts, o_parts, lse_parts = [], [], [], [], []
    for g, dil in enumerate(DILATIONS):
        tabs.append(_rope_tables(s, dil))
        tabs_t = [t[:, :ATTN_HEAD_DIM].T for t in tabs[g]]
        qkv, qkv_t = _mm(xs[g], w_ain, "nn", name=f"attn_in_{g}", m=s, n=gq, k=d, b_col_off=g * gt, out_split=3,
                         out_dtypes=(BF16,), epi=_rope_epi, epi_wants_j=True, row_extras=tabs_t, t_out=True,
                         deps=after if g == 0 else ())
        qkvs.append(qkv)
        qkv_ts.append(qkv_t)
        o_g, lse_g = _attn_fwd(qkv, dil, name=f"attn_fwd_{g}")
        o_parts.append(o_g)
        lse_parts.append(lse_g)
    o_f, o_b, lse_t = _attn_combine(o_parts, lse_parts)
    w_aout, after = get_w("attn_w_out", o_b)
    x1, x1b, xh1, r1 = ln_of(o_b, w_aout, x, "mix", 0, name="attn_out", k=d, deps=after)
    w_up0, w_down0 = get_w("ffn_w_up0", o_b)[0], get_w("ffn_w_down0", o_b)[0]
    h0, x2, x2b, xh2, r2 = ffn_fwd(x1b, x1, w_up0, w_down0, 0)
    w_hin, w_hout, norm_g = get_w("hgrn_w_in", x2b)[0], get_w("hgrn_w_out", x2b)[0], get_w("hgrn_norm_g", x2b)[0]
    p2 = _mm(x2b, w_hin, "nn", name="hgrn_in", m=s, n=3 * d, k=d, out_split=3)
    o_raw, o_n, states = _hgrn_fwd(p2, lb, norm_g)
    w_up1, w_down1 = get_w("ffn_w_up1", o_n)[0], get_w("ffn_w_down1", o_n)[0]
    x3, x3b, xh3, r3 = ln_of(o_n, w_hout, x2, "mix", 1, name="hgrn_out", k=d)
    h1, x4, _, xh4, r4 = ffn_fwd(x3b, x3, w_up1, w_down1, 1)
    du, dub, dg_ffn1, db_ffn1, sq = _loss_ln_bwd(x4, target, xh4, r4, small["ln_ffn_g"][1:2])
    du, dub, dg_mix1, db_mix1 = ffn_bwd(du, dub, x3b, h1, w_up1, w_down1, xh3, r3, 1)
    dyn = _mm(dub, w_hout, "nt", name="hgrn_dout", m=s, n=d, k=d)
    g_hout = _mm(o_n, dub, "tn", name="hgrn_gout", m=d, n=d, k=s, out_dtypes=(BF16,))
    dp2, d_norm_g, d_lb = _hgrn_bwd(p2, lb, norm_g, o_raw, states, dyn)
    g_hin = _mm(x2b, dp2, "tn", name="hgrn_gin", m=d, n=3 * d, k=s, b_split=3, out_dtypes=(BF16,))
    after = on_grads("hgrn", {"hgrn_w_out": g_hout, "hgrn_w_in": g_hin})
    du, dub, dg_ffn0, db_ffn0 = ln_grad_of(dp2, w_hin, du, xh2, r2, "ffn", 0, name="hgrn_dx", k=3 * d, a_split=3,
                                           deps=after)
    du, dub, dg_mix0, db_mix0 = ffn_bwd(du, dub, x1b, h0, w_up0, w_down0, xh1, r1, 0)
    g_aout = _mm(o_b, dub, "tn", name="attn_gout", m=d, n=d, k=s, out_dtypes=(BF16,))
    sm1 = jax.nn.softmax(small["lb_logits"], axis=0)
    d_l1 = d_lb * (sm1[0:1] * sm1[1:2])
    zeros = jnp.zeros((SMALL_ROWS - 12, d), F32)
    loss_row = jnp.broadcast_to(0.5 * jnp.sum(sq) / d, (1, d))
    small_grads = jnp.concatenate([dg_mix0, dg_mix1, db_mix0, db_mix1, dg_ffn0, dg_ffn1, db_ffn0, db_ffn1,
                                   -d_l1, d_l1, d_norm_g, loss_row, zeros], axis=0)
    after = on_grads("attn_out", {"attn_w_out": g_aout, "small": small_grads})
    do = _mm(dub, w_aout, "nt", name="attn_dout", m=s, n=d, k=d, deps=after)
    do_parts, do_ts, aux_parts = _attn_bwd_prep(do, o_f, lse_t)
    g_ain, dqkvs = None, []
    for g, dil in enumerate(DILATIONS):
        dqkvs.append(_attn_bwd(qkvs[g], qkv_ts[g], do_parts[g], do_ts[g], aux_parts[g], tabs[g], dil,
                               name=f"attn_bwd_{g}"))
        g_ain = _mm(xs[g], dqkvs[g], "tn", name=f"attn_gin_{g}", m=d, n=gq, k=s, b_split=3, out_dtypes=(BF16,),
                    out_col_off=g * gt, out_cols=3 * gq, alias=g_ain)
    after = on_grads("attn_in", {"attn_w_in": g_ain})
    dx_parts = [_mm(dqkvs[g], w_ain, "nt", name=f"attn_dx_{g}", m=s, n=d, k=gq, a_split=3, b_k_off=g * gt,
                    deps=after if g == 0 else ())
                for g in range(len(DILATIONS))]
    return _dx_combine(du, dx_parts)


def _mesh_place():
    x, y, c = lax.axis_index("x"), lax.axis_index("y"), lax.axis_index("c")
    return x, y, c, 4 * x + 2 * y + c


def _peer(x, y, c, k):
    px = 1 - x if (k >> 2) & 1 else x
    py = 1 - y if (k >> 1) & 1 else y
    pc = 1 - c if k & 1 else c
    return (px, py, pc), 4 * px + 2 * py + pc


def _window(ref, axis, size, idx):
    if axis is None:
        return ref
    sl = [slice(None)] * len(ref.shape)
    sl[axis] = pl.ds(idx * size, size)
    return ref.at[tuple(sl)]


_HBM = pl.BlockSpec(memory_space=pltpu.HBM)
_SEM = pl.BlockSpec(memory_space=pltpu.SEMAPHORE)
_EFFECT = pltpu.SideEffectType.DATAFLOW_SIDE_EFFECTING


def _xchg_ends(src_ref, land_ref, axis, gather, me, other):
    if gather:
        size = src_ref.shape[axis]
        return src_ref, _window(land_ref, axis, size, me), _window(land_ref, axis, size, other)
    size = None if axis is None else src_ref.shape[axis] // N_DEV
    return _window(src_ref, axis, size, other), land_ref.at[me], land_ref.at[other]


def _xchg_start(srcs, lands, axes, *, gather, name, deps=(), peers=tuple(range(1, N_DEV))):
    n = len(srcs)
    nd = len(deps)

    def body(*refs):
        src_refs, land_refs = refs[:n], refs[n:2 * n]
        send, recv = refs[2 * n + nd:3 * n + nd], refs[3 * n + nd:4 * n + nd]
        token = refs[-1]
        x, y, c, me = _mesh_place()
        for k in peers:
            peer, pidx = _peer(x, y, c, k)
            for i in range(n):
                src, dst, _ = _xchg_ends(src_refs[i], land_refs[i], axes[i], gather, me, pidx)
                pltpu.make_async_remote_copy(
                    src_ref=src, dst_ref=dst, send_sem=send[i].at[k - 1], recv_sem=recv[i].at[k - 1],
                    device_id=peer, device_id_type=pl.DeviceIdType.MESH).start()
        for i in range(n):
            src, dst, _ = _xchg_ends(src_refs[i], land_refs[i], axes[i], gather, me, me)
            pltpu.make_async_copy(src, dst, send[i].at[N_DEV - 1]).start()
        token[...] = jnp.zeros_like(token)

    bufs = list(srcs) + list(lands)
    outs = pl.pallas_call(
        body, name=name,
        out_shape=[pltpu.SemaphoreType.DMA((N_DEV,))] * (2 * n) + [pltpu.HBM(b.shape, b.dtype) for b in bufs]
        + [jax.ShapeDtypeStruct((8, LANES), F32)],
        in_specs=[_HBM] * (2 * n) + [pl.BlockSpec(memory_space=pl.ANY)] * nd,
        out_specs=[_SEM] * (2 * n) + [_HBM] * (2 * n) + [pl.BlockSpec(memory_space=pltpu.VMEM)],
        input_output_aliases={i: 2 * n + i for i in range(2 * n)},
        compiler_params=pltpu.CompilerParams(has_side_effects=_EFFECT),
    )(*[pltpu.with_memory_space_constraint(b, pltpu.HBM) for b in bufs], *deps)
    return dict(send=outs[:n], recv=outs[n:2 * n], srcs=outs[2 * n:3 * n], lands=outs[3 * n:4 * n], token=outs[-1],
                axes=list(axes), gather=gather)


SIBLING = 1
SAME_CORE = (2, 4, 6)


def _gather_relay(xc, after, *, name):
    axis = xc["axes"][0]

    def body(src_ref, land_ref, send1, recv1, after_ref, src_out, land_out, send2, recv2):
        x, y, c, me = _mesh_place()
        sibling, _ = _peer(x, y, c, SIBLING)
        for j, k in enumerate(SAME_CORE):
            peer, pidx = _peer(x, y, c, k)
            src, _, got = _xchg_ends(src_ref, land_ref, axis, True, me, pidx)
            pltpu.make_async_remote_copy(
                src_ref=src, dst_ref=got, send_sem=send1.at[k - 1], recv_sem=recv1.at[k - 1],
                device_id=peer, device_id_type=pl.DeviceIdType.MESH).wait_recv()
            pltpu.make_async_remote_copy(
                src_ref=got, dst_ref=got, send_sem=send2.at[j], recv_sem=recv2.at[j],
                device_id=sibling, device_id_type=pl.DeviceIdType.MESH).start()

    src, land = xc["srcs"][0], xc["lands"][0]
    outs = pl.pallas_call(
        body, name=name,
        out_shape=[pltpu.HBM(src.shape, src.dtype), pltpu.HBM(land.shape, land.dtype)]
        + [pltpu.SemaphoreType.DMA((len(SAME_CORE),))] * 2,
        in_specs=[_HBM, _HBM, _SEM, _SEM, pl.BlockSpec(memory_space=pl.ANY)],
        out_specs=[_HBM, _HBM, _SEM, _SEM], input_output_aliases={0: 0, 1: 1},
        compiler_params=pltpu.CompilerParams(has_side_effects=_EFFECT),
    )(src, land, xc["send"][0], xc["recv"][0], after)
    return dict(src=outs[0], land=outs[1], send=outs[2], recv=outs[3])


def _gather_relay_wait(xc, relay, after, *, name):
    axis = xc["axes"][0]

    def body(src_ref, land_ref, send1, recv1, send2, recv2, after_ref, src_out, land_out):
        x, y, c, me = _mesh_place()
        sibling, sidx = _peer(x, y, c, SIBLING)
        for k in (SIBLING,) + SAME_CORE:
            peer, pidx = _peer(x, y, c, k)
            src, dst, got = _xchg_ends(src_ref, land_ref, axis, True, me, pidx)
            pltpu.make_async_remote_copy(
                src_ref=src, dst_ref=dst, send_sem=send1.at[k - 1], recv_sem=recv1.at[k - 1],
                device_id=peer, device_id_type=pl.DeviceIdType.MESH).wait_send()
        src, dst, got = _xchg_ends(src_ref, land_ref, axis, True, me, sidx)
        pltpu.make_async_remote_copy(
            src_ref=src, dst_ref=got, send_sem=send1.at[SIBLING - 1], recv_sem=recv1.at[SIBLING - 1],
            device_id=sibling, device_id_type=pl.DeviceIdType.MESH).wait_recv()
        src, dst, _ = _xchg_ends(src_ref, land_ref, axis, True, me, me)
        pltpu.make_async_copy(src, dst, send1.at[N_DEV - 1]).wait()
        for j, k in enumerate(SAME_CORE):
            _, pidx = _peer(x, y, c, k)
            _, qidx = _peer(x, y, c, k ^ SIBLING)
            _, _, sent = _xchg_ends(src_ref, land_ref, axis, True, me, pidx)
            _, _, got = _xchg_ends(src_ref, land_ref, axis, True, me, qidx)
            pltpu.make_async_remote_copy(
                src_ref=sent, dst_ref=sent, send_sem=send2.at[j], recv_sem=recv2.at[j],
                device_id=sibling, device_id_type=pl.DeviceIdType.MESH).wait_send()
            pltpu.make_async_remote_copy(
                src_ref=got, dst_ref=got, send_sem=send2.at[j], recv_sem=recv2.at[j],
                device_id=sibling, device_id_type=pl.DeviceIdType.MESH).wait_recv()

    outs = pl.pallas_call(
        body, name=name, out_shape=[pltpu.HBM(relay["src"].shape, relay["src"].dtype),
                                    pltpu.HBM(relay["land"].shape, relay["land"].dtype)],
        in_specs=[_HBM, _HBM, _SEM, _SEM, _SEM, _SEM, pl.BlockSpec(memory_space=pl.ANY)],
        out_specs=[_HBM, _HBM], input_output_aliases={0: 0, 1: 1},
        compiler_params=pltpu.CompilerParams(has_side_effects=_EFFECT),
    )(relay["src"], relay["land"], xc["send"][0], xc["recv"][0], relay["send"], relay["recv"], after)
    return outs[1]


def _xchg_wait(xc, items, after, *, name):
    m = len(items)
    gather = xc["gather"]
    axes = [xc["axes"][i] for i in items]

    def body(*refs):
        src_refs, land_refs = refs[:m], refs[m:2 * m]
        send, recv = refs[2 * m:3 * m], refs[3 * m:4 * m]
        x, y, c, me = _mesh_place()
        for k in range(1, N_DEV):
            peer, pidx = _peer(x, y, c, k)
            for j in range(m):
                src, dst, got = _xchg_ends(src_refs[j], land_refs[j], axes[j], gather, me, pidx)
                pltpu.make_async_remote_copy(
                    src_ref=src, dst_ref=dst, send_sem=send[j].at[k - 1], recv_sem=recv[j].at[k - 1],
                    device_id=peer, device_id_type=pl.DeviceIdType.MESH).wait_send()
                pltpu.make_async_remote_copy(
                    src_ref=src, dst_ref=got, send_sem=send[j].at[k - 1], recv_sem=recv[j].at[k - 1],
                    device_id=peer, device_id_type=pl.DeviceIdType.MESH).wait_recv()
        for j in range(m):
            src, dst, _ = _xchg_ends(src_refs[j], land_refs[j], axes[j], gather, me, me)
            pltpu.make_async_copy(src, dst, send[j].at[N_DEV - 1]).wait()

    bufs = [xc["srcs"][i] for i in items] + [xc["lands"][i] for i in items]
    sems = [xc["send"][i] for i in items] + [xc["recv"][i] for i in items]
    outs = pl.pallas_call(
        body, name=name, out_shape=[pltpu.HBM(b.shape, b.dtype) for b in bufs],
        in_specs=[_HBM] * (2 * m) + [_SEM] * (2 * m) + [pl.BlockSpec(memory_space=pl.ANY)],
        out_specs=[_HBM] * (2 * m), input_output_aliases={j: j for j in range(2 * m)},
        compiler_params=pltpu.CompilerParams(has_side_effects=_EFFECT),
    )(*bufs, *sems, after)
    return outs[m:]


def _cast_bf16(a, *, name):
    r, c = a.shape
    tr = min(r, 512)

    def body(a_ref, o_ref):
        o_ref[...] = a_ref[...].astype(BF16)

    spec = pl.BlockSpec((tr, c), lambda i: (i, 0))
    return pl.pallas_call(body, name=name, grid=(r // tr,), in_specs=[spec], out_specs=spec,
                          out_shape=jax.ShapeDtypeStruct((r, c), BF16), compiler_params=_params(("parallel",)))(a)


def _adamw(slabs, w, m, v, *, name):
    layers, r, c = w.shape
    tr = min(r, 256)

    def body(*refs):
        s_refs = refs[:layers]
        w_ref, m_ref, v_ref, g_ref, d_ref, mo_ref, vo_ref = refs[layers:]
        for l in range(layers):
            g = s_refs[l][0].astype(F32)
            for i in range(1, N_DEV):
                g = g + s_refs[l][i].astype(F32)
            m2 = ADAM_B1 * m_ref[l] + (1.0 - ADAM_B1) * g
            v2 = ADAM_B2 * v_ref[l] + (1.0 - ADAM_B2) * (g * g)
            m_hat = m2 / (1.0 - ADAM_B1 ** ADAM_STEP)
            v_hat = v2 / (1.0 - ADAM_B2 ** ADAM_STEP)
            g_ref[l] = g
            d_ref[l] = -ADAM_LR * (m_hat / (jnp.sqrt(v_hat) + ADAM_EPS) + ADAM_WD * w_ref[l])
            mo_ref[l] = m2
            vo_ref[l] = v2

    spec = pl.BlockSpec((layers, tr, c), lambda i: (0, i, 0))
    return pl.pallas_call(
        body, name=name, grid=(r // tr,),
        in_specs=[pl.BlockSpec((N_DEV, tr, c), lambda i: (0, i, 0))] * layers + [spec, spec, spec],
        out_specs=[spec] * 4, out_shape=[jax.ShapeDtypeStruct((layers, r, c), F32)] * 4,
        compiler_params=_params(("parallel",)),
    )(*slabs, w, m, v)


GATHER_AXIS = {"attn_w_in": 1, "attn_w_out": 0, "ffn_w_up0": 1, "ffn_w_down0": 0, "hgrn_w_in": 1, "hgrn_w_out": 0,
               "hgrn_norm_g": 1, "ffn_w_up1": 1, "ffn_w_down1": 0}
GATHER_STAGES = (("attn_w_in",), ("attn_w_out", "ffn_w_up0", "ffn_w_down0", "hgrn_w_in", "hgrn_w_out", "hgrn_norm_g"),
                 ("ffn_w_up1", "ffn_w_down1"))
GATHER_WAITS = ((("attn_w_in",), 0, 1), (("attn_w_out", "ffn_w_up0", "ffn_w_down0"), 1, 2),
                (("hgrn_w_in", "hgrn_w_out", "hgrn_norm_g"), 1, None), (("ffn_w_up1", "ffn_w_down1"), 2, None))
SCATTER_AXIS = dict(GATHER_AXIS, small=None)
BIG = ("attn_w_in", "attn_w_out", "hgrn_w_in", "hgrn_w_out", "ffn_w_up", "ffn_w_down")
SMALL = ("lb_logits", "ln_mix_g", "ln_mix_b", "ln_ffn_g", "ln_ffn_b")
SMALL_ROW = {"ln_mix_g": 0, "ln_mix_b": 2, "ln_ffn_g": 4, "ln_ffn_b": 6, "lb_logits": 8}
NORM_G_ROW = 10
LOSS_ROW = 11


def kernel(x, attn_w_in, attn_w_out, hgrn_w_in, hgrn_w_out, hgrn_norm_g, lb_logits, ln_mix_g, ln_mix_b, ln_ffn_g, ln_ffn_b, ffn_w_up, ffn_w_down, loss_target, m_attn_w_in, m_attn_w_out, m_hgrn_w_in, m_hgrn_w_out, m_hgrn_norm_g, m_lb_logits, m_ln_mix_g, m_ln_mix_b, m_ln_ffn_g, m_ln_ffn_b, m_ffn_w_up, m_ffn_w_down, v_attn_w_in, v_attn_w_out, v_hgrn_w_in, v_hgrn_w_out, v_hgrn_norm_g, v_lb_logits, v_ln_mix_g, v_ln_mix_b, v_ln_ffn_g, v_ln_ffn_b, v_ffn_w_up, v_ffn_w_down):
    wts = dict(attn_w_in=attn_w_in, attn_w_out=attn_w_out, hgrn_w_in=hgrn_w_in, hgrn_w_out=hgrn_w_out,
               hgrn_norm_g=hgrn_norm_g, lb_logits=lb_logits, ln_mix_g=ln_mix_g, ln_mix_b=ln_mix_b, ln_ffn_g=ln_ffn_g,
               ln_ffn_b=ln_ffn_b, ffn_w_up=ffn_w_up, ffn_w_down=ffn_w_down)
    mom = dict(attn_w_in=m_attn_w_in, attn_w_out=m_attn_w_out, hgrn_w_in=m_hgrn_w_in, hgrn_w_out=m_hgrn_w_out,
               hgrn_norm_g=m_hgrn_norm_g, lb_logits=m_lb_logits, ln_mix_g=m_ln_mix_g, ln_mix_b=m_ln_mix_b,
               ln_ffn_g=m_ln_ffn_g, ln_ffn_b=m_ln_ffn_b, ffn_w_up=m_ffn_w_up, ffn_w_down=m_ffn_w_down)
    vel = dict(attn_w_in=v_attn_w_in, attn_w_out=v_attn_w_out, hgrn_w_in=v_hgrn_w_in, hgrn_w_out=v_hgrn_w_out,
               hgrn_norm_g=v_hgrn_norm_g, lb_logits=v_lb_logits, ln_mix_g=v_ln_mix_g, ln_mix_b=v_ln_mix_b,
               ln_ffn_g=v_ln_ffn_g, ln_ffn_b=v_ln_ffn_b, ffn_w_up=v_ffn_w_up, ffn_w_down=v_ffn_w_down)
    me = 4 * lax.axis_index("x") + 2 * lax.axis_index("y") + lax.axis_index("c")

    src = {"attn_w_in": attn_w_in[0], "attn_w_out": attn_w_out[0], "hgrn_w_in": hgrn_w_in[0], "hgrn_w_out": hgrn_w_out[0],
           "ffn_w_up0": ffn_w_up[0], "ffn_w_down0": ffn_w_down[0], "ffn_w_up1": ffn_w_up[1], "ffn_w_down1": ffn_w_down[1]}
    gathers, got = {}, {}

    def start_gather(stage, deps):
        shards, lands = [], []
        for nm in GATHER_STAGES[stage]:
            sh = hgrn_norm_g if nm == "hgrn_norm_g" else _cast_bf16(src[nm], name=f"cast_{nm}")
            ax = GATHER_AXIS[nm]
            shape = list(sh.shape)
            shape[ax] *= N_DEV
            shards.append(sh)
            lands.append(lax.empty(tuple(shape), sh.dtype))
        peers = (SIBLING,) + SAME_CORE if stage == 0 else tuple(range(1, N_DEV))
        gathers[stage] = _xchg_start(shards, lands, [GATHER_AXIS[nm] for nm in GATHER_STAGES[stage]], gather=True,
                                     name=f"gather_start_{stage}", deps=deps, peers=peers)
        return [gathers[stage]["token"]]

    def get_w(name, after):
        deps = []
        if name not in got:
            group, stage, then = [w for w in GATHER_WAITS if name in w[0]][0]
            xc = gathers[stage]
            if stage == 0:
                relay = _gather_relay(xc, xc["token"], name="gather_relay")
                res = [_gather_relay_wait(xc, relay, xc["token"], name=f"gather_wait_{group[0]}")]
            else:
                res = _xchg_wait(xc, [GATHER_STAGES[stage].index(nm) for nm in group], after,
                                 name=f"gather_wait_{group[0]}")
            got.update(zip(group, res))
            if then is not None:
                deps = start_gather(then, [res[0]])
        return got[name], deps

    first = start_gather(0, [])

    scattered = {}

    def on_grads(tag, grads):
        gnames = list(grads)
        axes = [SCATTER_AXIS[nm] for nm in gnames]
        stacks = []
        for nm, ax in zip(gnames, axes):
            shape = list(grads[nm].shape)
            if ax is not None:
                shape[ax] //= N_DEV
            stacks.append(lax.empty((N_DEV, *shape), grads[nm].dtype))
        scattered[tag] = (gnames, _xchg_start([grads[nm] for nm in gnames], stacks, axes, gather=False,
                                              name=f"scatter_start_{tag}"))
        return [scattered[tag][1]["token"]]

    sm = jax.nn.softmax(lb_logits, axis=0)
    csum = jnp.cumsum(sm, axis=0)
    small = dict(lb=(csum - csum[0:1])[1:2], lb_logits=lb_logits, ln_mix_g=ln_mix_g, ln_mix_b=ln_mix_b,
                 ln_ffn_g=ln_ffn_g, ln_ffn_b=ln_ffn_b)
    grad_x = _local_step(x[0], loss_target[0], get_w, small, on_grads, deps=first)
    out = {}

    def stack_small(src_):
        rows = [None] * SMALL_ROWS
        for name in SMALL:
            rows[SMALL_ROW[name]], rows[SMALL_ROW[name] + 1] = src_[name][0:1], src_[name][1:2]
        zero = jnp.zeros((1, x.shape[-1]), F32)
        return jnp.concatenate([zero if r is None else r for r in rows], axis=0)[None]

    def update(name, slabs):
        shape = wts[name].shape
        out[name] = [r.reshape(shape) for r in _adamw(slabs, wts[name], mom[name], vel[name], name=f"adamw_{name}")]
        return out[name][0]

    slabs, after = {}, grad_x
    for tag, (gnames, xc) in scattered.items():
        slabs.update(zip(gnames, _xchg_wait(xc, list(range(len(gnames))), after, name=f"scatter_wait_{tag}")))
        if tag == "ffn1":
            continue
        if tag == "ffn0":
            update("ffn_w_down", [slabs["ffn_w_down0"], slabs["ffn_w_down1"]])
            after = update("ffn_w_up", [slabs["ffn_w_up0"], slabs["ffn_w_up1"]])
        elif tag == "hgrn":
            update("hgrn_w_out", [slabs["hgrn_w_out"]])
            after = update("hgrn_w_in", [slabs["hgrn_w_in"]])
        elif tag == "attn_out":
            after = update("attn_w_out", [slabs["attn_w_out"]])
        else:
            after = update("attn_w_in", [slabs["attn_w_in"]])
    res = _adamw([slabs["small"]], stack_small(wts), stack_small(mom), stack_small(vel), name="adamw_small")
    for name in SMALL:
        out[name] = [r[0, SMALL_ROW[name]:SMALL_ROW[name] + 2] for r in res]
    loss = res[0][0, LOSS_ROW, 0]
    ng = hgrn_norm_g.shape[-1]
    ng_slabs = lax.dynamic_slice(slabs["small"], (0, NORM_G_ROW, me * ng), (N_DEV, 1, ng))
    out["hgrn_norm_g"] = [r[0] for r in _adamw([ng_slabs], hgrn_norm_g[None], m_hgrn_norm_g[None],
                                                v_hgrn_norm_g[None], name="adamw_norm_g")]
    order =("attn_w_in", "attn_w_out", "hgrn_w_in", "hgrn_w_out", "hgrn_norm_g", "lb_logits", "ln_mix_g", "ln_mix_b",
             "ln_ffn_g", "ln_ffn_b", "ffn_w_up", "ffn_w_down")
    return (loss, grad_x[None], *[out[nm][0] for nm in order], *[out[nm][1] for nm in order],
            *[out[nm][2] for nm in order], *[out[nm][3] for nm in order])
```

```python
import jax
import jax.numpy as jnp
from jax import lax
from jax.experimental import pallas as pl
from jax.experimental.pallas import tpu as pltpu

F32 = jnp.float32
BF16 = jnp.bfloat16

N_DEV = 8
LANES = 128
D_MODEL = 1024
ATTN_HEAD_DIM = 64
ATTN_HEADS = 16
ATTN_BLK = 128
DILATIONS = (1, 4, 16)
ROPE_THETA = 10000.0
HGRN_HEADS = 8
HGRN_CHUNK = 64
D_FF = 4096
LN_EPS = 1e-5
RMS_EPS = 1e-6
DEPTH = 2
ALPHA = (2 * DEPTH) ** 0.25
ADAM_LR, ADAM_B1, ADAM_B2, ADAM_EPS, ADAM_WD, ADAM_STEP = 0.001, 0.9, 0.999, 1e-08, 0.01, 10
VMEM_LIMIT = 48 * 1024 * 1024

_NT = (((1,), (1,)), ((), ()))
_TN = (((0,), (0,)), ((), ()))


def _dot(a, b):
    return jnp.dot(a, b, preferred_element_type=F32)


def _dot_nt(a, b):
    return lax.dot_general(a, b, _NT, preferred_element_type=F32)


def _dot_tn(a, b):
    return lax.dot_general(a, b, _TN, preferred_element_type=F32)


def _split3(x):
    p1 = x.astype(BF16)
    r1 = x - p1.astype(F32)
    p2 = r1.astype(BF16)
    p3 = (r1 - p2.astype(F32)).astype(BF16)
    return p1, p2, p3


def _exact_dot(sel3, x):
    return _dot(sel3, jnp.concatenate(_split3(x), axis=0))


def _exact_dot_r(x, sel3):
    return _dot(jnp.concatenate(_split3(x), axis=1), sel3)


def _params(sem=None):
    return pltpu.CompilerParams(dimension_semantics=sem, vmem_limit_bytes=VMEM_LIMIT)


def _mm(a, b, mode, *, name, m, n, k, tm=1024, tn=1024, tk=1024, out_dtypes=(F32,), epi=None, a_pre=None,
        tile_extras=(), row_extras=(), vec_extras=(), row_outs=(), vec_outs=0, a_split=1, b_split=1, out_split=1,
        b_col_off=0, b_k_off=0, out_col_off=0, out_cols=None, alias=None, epi_wants_j=False, deps=(), t_out=False):
    tm, tn, tk = min(tm, m), min(tn, n), min(tk, k)
    assert m % tm == 0 and n % tn == 0 and k % tk == 0, (name, m, n, k, tm, tn, tk)
    gm, gn, gk = m // tm, n // tn, k // tk
    if mode in ("nn", "nt"):
        if a_split > 1 and tk == k:
            a_spec = pl.BlockSpec((a_split, tm, k // a_split), lambda i, j, kk: (0, i, 0))
        elif a_split > 1:
            kc = (k // a_split) // tk
            a_spec = pl.BlockSpec((None, tm, tk), lambda i, j, kk: (kk // kc, i, kk % kc))
        else:
            a_spec = pl.BlockSpec((tm, tk), lambda i, j, kk: (i, kk))
    else:
        a_spec = pl.BlockSpec((tk, tm), lambda i, j, kk: (kk, i))
    if mode in ("nn", "tn"):
        if b_split > 1:
            nc = (n // b_split) // tn
            b_spec = pl.BlockSpec((None, tk, tn), lambda i, j, kk: (j // nc, kk, j % nc))
        else:
            b_spec = pl.BlockSpec((tk, tn), lambda i, j, kk: (kk + b_k_off, j + b_col_off))
    else:
        b_spec = pl.BlockSpec((tn, tk), lambda i, j, kk: (j + b_col_off, kk + b_k_off))
    if out_split > 1:
        nco = (n // out_split) // tn
        o_spec = pl.BlockSpec((None, tm, tn), lambda i, j, kk: (j // nco, i, j % nco))
        o_shape = (out_split, m, n // out_split)
    else:
        o_spec = pl.BlockSpec((tm, tn), lambda i, j, kk: (i, j + out_col_off))
        o_shape = (m, out_cols if out_cols is not None else n)
    n_ex = len(tile_extras) + len(row_extras) + len(vec_extras)
    n_out = len(out_dtypes)
    n_plain = n_out + len(row_outs)
    assert not vec_outs or gn == 1
    if epi is None:
        def epi(acc):
            return (acc,)
    dot = {"nn": _dot, "nt": _dot_nt, "tn": _dot_tn}[mode]

    def body(*refs):
        a_ref, b_ref = refs[0], refs[1]
        ex = refs[2:2 + n_ex]
        outs = refs[2 + n_ex + (1 if alias is not None else 0) + len(deps):][:n_plain + vec_outs + (1 if t_out else 0)]
        ii = pl.program_id(0)
        jj = pl.program_id(1)

        def product():
            if a_split > 1 and tk == k:
                av = jnp.concatenate([a_ref[p] for p in range(a_split)], axis=1)
            else:
                av = a_ref[...]
            if a_pre is not None:
                av = a_pre(av)
            return dot(av.astype(BF16), b_ref[...].astype(BF16))

        def finish(total):
            lead = (jj,) if epi_wants_j else ()
            res = epi(*lead, total, *[e[...] for e in ex])
            for o, r in zip(outs[:n_plain], res):
                o[...] = r.astype(o.dtype)
            for o, r in zip(outs[n_plain:n_plain + vec_outs], res[n_plain:]):
                @pl.when(ii == 0)
                def _(o=o, r=r):
                    o[...] = r

                @pl.when(ii > 0)
                def _(o=o, r=r):
                    o[...] += r
            if t_out:
                outs[-1][...] = res[-1].astype(outs[-1].dtype)

        if gk == 1:
            finish(product())
        else:
            acc = refs[-1]
            kk = pl.program_id(2)

            @pl.when(kk == 0)
            def _():
                acc[...] = product()

            @pl.when(kk > 0)
            def _():
                acc[...] += product()

            @pl.when(kk == gk - 1)
            def _():
                finish(acc[...])

    in_specs = [a_spec, b_spec] + [o_spec] * len(tile_extras)
    in_specs += [pl.BlockSpec((tm, r.shape[1]), lambda i, j, kk: (i, 0)) if r.shape[0] == m else
                 pl.BlockSpec((r.shape[0], tm), lambda i, j, kk: (0, i)) for r in row_extras]
    in_specs += [pl.BlockSpec((1, tn), lambda i, j, kk: (0, j))] * len(vec_extras)
    args = [a, b] + list(tile_extras) + list(row_extras) + list(vec_extras)
    io_alias = {}
    if alias is not None:
        in_specs.append(pl.BlockSpec(memory_space=pl.ANY))
        args.append(alias)
        io_alias = {len(args) - 1: 0}
    in_specs += [pl.BlockSpec(memory_space=pl.ANY)] * len(deps)
    args += list(deps)
    out_specs = [o_spec] * n_out
    out_shape = [jax.ShapeDtypeStruct(o_shape, dt) for dt in out_dtypes]
    for dt, w in row_outs:
        out_specs.append(pl.BlockSpec((tm, w), lambda i, j, kk: (i, 0)))
        out_shape.append(jax.ShapeDtypeStruct((m, w), dt))
    out_specs += [pl.BlockSpec((1, tn), lambda i, j, kk: (0, j))] * vec_outs
    out_shape += [jax.ShapeDtypeStruct((1, n), F32)] * vec_outs
    if t_out:
        assert out_split > 1
        out_specs.append(pl.BlockSpec((None, tn, tm), lambda i, j, kk: (j // nco, j % nco, i)))
        out_shape.append(jax.ShapeDtypeStruct((out_split, n // out_split, m), out_dtypes[0]))
    out = pl.pallas_call(
        body, name=name, grid=(gm, gn, gk), in_specs=in_specs, out_specs=out_specs, out_shape=out_shape,
        scratch_shapes=[pltpu.VMEM((tm, tn), F32)] if gk > 1 else [],
        input_output_aliases=io_alias,
        compiler_params=_params(("arbitrary" if vec_outs else "parallel", "parallel", "arbitrary")),
    )(*args)
    return out[0] if len(out) == 1 else out


def _rope_tables(seq, dil):
    pos = jnp.arange(seq, dtype=jnp.int32).reshape(seq // dil, dil).T.reshape(seq)
    half = ATTN_HEAD_DIM // 2
    inv = ROPE_THETA ** (-jnp.arange(half, dtype=F32) * (2.0 / ATTN_HEAD_DIM))
    ang = pos.astype(F32)[:, None] * inv[None, :]
    cos, sin = jnp.cos(ang), jnp.sin(ang)
    reps = LANES // ATTN_HEAD_DIM
    return (jnp.tile(jnp.concatenate([cos, cos], axis=1), (1, reps)),
            jnp.tile(jnp.concatenate([-sin, sin], axis=1), (1, reps)))


def _rotate(x, c, ss, sign=1.0):
    w = x.shape[-1]
    half = ATTN_HEAD_DIM // 2
    lane = lax.broadcasted_iota(jnp.int32, x.shape, 1)
    first = (lane % ATTN_HEAD_DIM) < half
    partner = jnp.where(first, pltpu.roll(x, w - half, 1), pltpu.roll(x, half, 1))
    reps = w // LANES
    if reps > 1:
        c = jnp.concatenate([c] * reps, axis=1)
        ss = jnp.concatenate([ss] * reps, axis=1)
    return x * c + sign * (partner * ss)


def _rotate_t(xt, ct, sst):
    half = ATTN_HEAD_DIM // 2
    heads = xt.shape[0] // ATTN_HEAD_DIM
    parts = []
    for h in range(heads):
        lo = h * ATTN_HEAD_DIM
        parts += [xt[lo + half:lo + 2 * half], xt[lo:lo + half]]
    return (xt * jnp.concatenate([ct] * heads, axis=0)
            + jnp.concatenate(parts, axis=0) * jnp.concatenate([sst] * heads, axis=0))


def _ln_epi(acc, x, g, b):
    u = ALPHA * x + acc
    mu = jnp.mean(u, axis=-1, keepdims=True)
    uc = u - mu
    var = jnp.mean(uc * uc, axis=-1, keepdims=True)
    rstd = lax.rsqrt(var + LN_EPS)
    xh = uc * rstd
    out = xh * g + b
    return out, out, xh, rstd


def _ln_grad(dy, xh, rstd, g):
    dxh = dy * g
    m1 = jnp.mean(dxh, axis=-1, keepdims=True)
    m2 = jnp.mean(dxh * xh, axis=-1, keepdims=True)
    du = rstd * (dxh - m1 - xh * m2)
    return du, du, jnp.sum(dy * xh, axis=0, keepdims=True), jnp.sum(dy, axis=0, keepdims=True)


def _ln_grad_epi(acc, du_next, xh, rstd, g):
    return _ln_grad(acc + ALPHA * du_next, xh, rstd, g)


def _ln_loss_epi(acc, x_in, target, g, b):
    out, _, xh, rstd = _ln_epi(acc, x_in, g, b)
    e = out - target
    du, _, dg, db = _ln_grad(e * (1.0 / e.shape[-1]), xh, rstd, g)
    return du, du, dg, db, jnp.sum(e * e, axis=0, keepdims=True)


POS_BLK = 2048


def _class_rows(r, dil):
    return pl.ds(r, POS_BLK // dil, stride=dil) if dil > 1 else pl.ds(0, POS_BLK)


def _class_view(a, dil):
    s, w = a.shape
    return a.reshape(dil, s // dil, w)


def _class_spec(dil, all_tiles=True):
    if all_tiles:
        return pl.BlockSpec((dil, POS_BLK // dil, LANES), lambda i, t: (0, i, t))
    return pl.BlockSpec((dil, POS_BLK // dil, LANES), lambda i, t: (0, i, 0))


def _pos_spec(all_tiles=True):
    if all_tiles:
        return pl.BlockSpec((POS_BLK, LANES), lambda i, t: (i, t))
    return pl.BlockSpec((POS_BLK, LANES), lambda i, t: (i, 0))


def _prep_x(x, deps=()):
    s, d = x.shape

    def body(x_ref, *refs):
        outs = refs[len(deps):]
        for dil, o_ref in zip(DILATIONS, outs):
            for r in range(dil):
                o_ref[r] = x_ref[_class_rows(r, dil), :].astype(BF16)

    outs = pl.pallas_call(
        body, name="prep_x", grid=(s // POS_BLK, d // LANES),
        in_specs=[_pos_spec()] + [pl.BlockSpec(memory_space=pl.ANY)] * len(deps),
        out_specs=[_class_spec(dil) for dil in DILATIONS],
        out_shape=[jax.ShapeDtypeStruct((dil, s // dil, d), BF16) for dil in DILATIONS],
        compiler_params=_params(("parallel", "parallel")),
    )(x, *deps)
    return [o.reshape(s, d) for o in outs]


def _head_expand_matrix():
    h = lax.broadcasted_iota(jnp.int32, (LANES, D_MODEL), 0)
    l = lax.broadcasted_iota(jnp.int32, (LANES, D_MODEL), 1)
    return (l // ATTN_HEAD_DIM == h).astype(BF16)


def _attn_fwd(qkv, dil, *, name):
    _, s, d = qkv.shape
    nq = s // ATTN_BLK
    per = nq // dil
    scale = ATTN_HEAD_DIM ** -0.5
    tiles = d // LANES

    def body(q_ref, kc_ref, kp_ref, vc_ref, vp_ref, o_ref, lse_ref):
        qb = pl.program_id(0)
        first = (qb % per) == 0
        qi = lax.broadcasted_iota(jnp.int32, (ATTN_BLK, 2 * ATTN_BLK), 0)
        kj = lax.broadcasted_iota(jnp.int32, (ATTN_BLK, 2 * ATTN_BLK), 1)
        dist = qi + ATTN_BLK - kj
        valid = (dist >= 0) & (dist <= ATTN_BLK) & ((kj >= ATTN_BLK) | jnp.logical_not(first))
        lane = lax.broadcasted_iota(jnp.int32, (ATTN_BLK, LANES), 1)
        lse_tile = jnp.zeros((ATTN_BLK, LANES), F32)
        zero = jnp.zeros((), BF16)
        in_head = [(lane // ATTN_HEAD_DIM) == hh for hh in range(2)]
        for t0 in range(0, tiles, ATTN_TILE_GROUP):
            group = range(t0, t0 + ATTN_TILE_GROUP)
            heads = [(t, hh) for t in group for hh in range(2)]
            cols = {t: pl.ds(t * LANES, LANES) for t in group}
            k2 = {t: jnp.concatenate([kp_ref[:, cols[t]], kc_ref[:, cols[t]]], axis=0) for t in group}
            v2 = {t: jnp.concatenate([vp_ref[:, cols[t]], vc_ref[:, cols[t]]], axis=0) for t in group}
            sc = {(t, hh): jnp.where(valid, _dot_nt(jnp.where(in_head[hh], q_ref[:, cols[t]], zero), k2[t]) * scale,
                                     -jnp.inf) for t, hh in heads}
            mx = {i: jnp.max(sc[i], axis=-1, keepdims=True) for i in heads}
            p = {i: jnp.exp(sc[i] - mx[i]) for i in heads}
            l = {i: jnp.sum(p[i], axis=-1, keepdims=True) for i in heads}
            oh = {i: _dot(p[i].astype(BF16), v2[i[0]]) / l[i] for i in heads}
            for t in group:
                o_ref[:, cols[t]] = jnp.where(in_head[0], oh[t, 0], oh[t, 1])
                for hh in range(2):
                    lse_tile = jnp.where(lane == 2 * t + hh, mx[t, hh] + jnp.log(l[t, hh]), lse_tile)
        lse_ref[...] = lse_tile

    def blk(piece, prev):
        if prev:
            return pl.BlockSpec((None, ATTN_BLK, d), lambda i: (piece, jnp.maximum(i - 1, 0), 0))
        return pl.BlockSpec((None, ATTN_BLK, d), lambda i: (piece, i, 0))

    return pl.pallas_call(
        body, name=name, grid=(nq,),
        in_specs=[blk(0, False), blk(1, False), blk(1, True), blk(2, False), blk(2, True)],
        out_specs=[pl.BlockSpec((ATTN_BLK, d), lambda i: (i, 0)), pl.BlockSpec((ATTN_BLK, LANES), lambda i: (i, 0))],
        out_shape=[jax.ShapeDtypeStruct((s, d), F32), jax.ShapeDtypeStruct((s, LANES), F32)],
        compiler_params=_params(("parallel",)),
    )(qkv, qkv, qkv, qkv, qkv)


def _attn_combine(os_, lses):
    s, d = os_[0].shape
    sel = jnp.concatenate([_head_expand_matrix()] * 3, axis=0)

    def body(o0, o1, o2, l0, l1, l2, sel_ref, of_ref, ob_ref, lt_ref, o_pos, l_pos):
        for g, (dil, o_ref, l_ref) in enumerate(zip(DILATIONS, (o0, o1, o2), (l0, l1, l2))):
            for r in range(dil):
                o_pos[g, _class_rows(r, dil), :] = o_ref[r]
                l_pos[g, _class_rows(r, dil), :] = l_ref[r]
        la, lb_, lc = l_pos[0], l_pos[1], l_pos[2]
        mx = jnp.maximum(jnp.maximum(la, lb_), lc)
        es = (jnp.exp(la - mx), jnp.exp(lb_ - mx), jnp.exp(lc - mx))
        z = es[0] + es[1] + es[2]
        lt_ref[...] = mx + jnp.log(z)
        acc = jnp.zeros((POS_BLK, LANES), F32)
        for g in range(3):
            acc += _exact_dot_r(es[g] / z, sel_ref[...]) * o_pos[g]
        of_ref[...] = acc
        ob_ref[...] = acc.astype(BF16)

    return pl.pallas_call(
        body, name="attn_combine", grid=(s // POS_BLK, d // LANES),
        in_specs=[_class_spec(dil) for dil in DILATIONS] + [_class_spec(dil, False) for dil in DILATIONS]
        + [pl.BlockSpec((3 * LANES, LANES), lambda i, t: (0, t))],
        out_specs=[_pos_spec(), _pos_spec(), _pos_spec(False)],
        out_shape=[jax.ShapeDtypeStruct((s, d), F32), jax.ShapeDtypeStruct((s, d), BF16),
                   jax.ShapeDtypeStruct((s, LANES), F32)],
        scratch_shapes=[pltpu.VMEM((3, POS_BLK, LANES), F32), pltpu.VMEM((3, POS_BLK, LANES), F32)],
        compiler_params=_params(("parallel", "arbitrary")),
    )(*[_class_view(o, dil) for o, dil in zip(os_, DILATIONS)],
      *[_class_view(l, dil) for l, dil in zip(lses, DILATIONS)], sel)


ATTN_TILE_GROUP = 2
AUX_PER_TILE = 12


def _aux_placement():
    h = lax.broadcasted_iota(jnp.int32, (6, LANES, LANES), 1)
    l = lax.broadcasted_iota(jnp.int32, (6, LANES, LANES), 2)
    j = lax.broadcasted_iota(jnp.int32, (6, LANES, LANES), 0)
    target = AUX_PER_TILE * (h // 2) + 3 * (h % 2) + jnp.where(j < 3, j, 3 + j)
    return ((l == target) & (h < ATTN_HEADS)).astype(BF16)


def _attn_bwd_prep(do, o, lse):
    s, d = do.shape
    tiles = d // LANES
    sel_t = jnp.tile(_head_expand_matrix().T.reshape(tiles, LANES, LANES), (1, 3, 1))

    def body(do_ref, o_ref, l_ref, sel_ref, place_ref, *refs):
        outs, delta, aux = refs[:9], refs[9], refs[10]
        t = pl.program_id(1)
        part = _exact_dot_r(do_ref[...] * o_ref[...], sel_ref[...])

        @pl.when(t == 0)
        def _():
            delta[...] = part

        @pl.when(t > 0)
        def _():
            delta[...] += part

        for g, dil in enumerate(DILATIONS):
            for r in range(dil):
                blk = do_ref[_class_rows(r, dil), :].astype(BF16)
                outs[g][r] = blk
                outs[3 + g][r] = blk.T

        @pl.when(t == tiles - 1)
        def _():
            pieces = _split3(l_ref[...]) + _split3(delta[...])
            aux[...] = _dot(jnp.concatenate(pieces, axis=1), place_ref[...])
            for g, dil in enumerate(DILATIONS):
                for r in range(dil):
                    outs[6 + g][r] = aux[_class_rows(r, dil), :].astype(BF16)

    outs = pl.pallas_call(
        body, name="attn_bwd_prep", grid=(s // POS_BLK, tiles),
        in_specs=[_pos_spec(), _pos_spec(), _pos_spec(False),
                  pl.BlockSpec((None, 3 * LANES, LANES), lambda i, t: (t, 0, 0)),
                  pl.BlockSpec((6 * LANES, LANES), lambda i, t: (0, 0))],
        out_specs=[_class_spec(dil) for dil in DILATIONS]
        + [pl.BlockSpec((dil, LANES, POS_BLK // dil), lambda i, t: (0, t, i)) for dil in DILATIONS]
        + [_class_spec(dil, False) for dil in DILATIONS],
        out_shape=[jax.ShapeDtypeStruct((dil, s // dil, d), BF16) for dil in DILATIONS]
        + [jax.ShapeDtypeStruct((dil, d, s // dil), BF16) for dil in DILATIONS]
        + [jax.ShapeDtypeStruct((dil, s // dil, LANES), BF16) for dil in DILATIONS],
        scratch_shapes=[pltpu.VMEM((POS_BLK, LANES), F32), pltpu.VMEM((POS_BLK, LANES), F32)],
        compiler_params=_params(("parallel", "arbitrary")),
    )(do, o, lse, sel_t, _aux_placement().reshape(6 * LANES, LANES))
    return ([a.reshape(s, d) for a in outs[0:3]], list(outs[3:6]), [a.reshape(s, LANES) for a in outs[6:9]])


def _attn_bwd(qkv, qkv_t, do, do_t, aux, tables, dil, *, name):
    _, s, d = qkv.shape
    nq = s // ATTN_BLK
    per = nq // dil
    scale = ATTN_HEAD_DIM ** -0.5
    assert scale == 0.125
    tiles = d // LANES
    half = ATTN_HEAD_DIM

    def body(qd_ref, qo_ref, k_ref, qtd_ref, qto_ref, kt_ref, vt_ref, dod_ref, doo_ref, dotd_ref, doto_ref,
             auxd_ref, auxo_ref, c_ref, ss_ref, out_ref, carry):
        kb = pl.program_id(0)

        @pl.when(kb == 0)
        def _():
            carry[...] = jnp.zeros_like(carry)

        has_next = (kb % per) != (per - 1)
        qi = lax.broadcasted_iota(jnp.int32, (ATTN_BLK, 2 * ATTN_BLK), 0)
        kj = lax.broadcasted_iota(jnp.int32, (ATTN_BLK, 2 * ATTN_BLK), 1) % ATTN_BLK
        valid_d = kj <= qi
        valid_o = (kj >= qi) & has_next
        lane = lax.broadcasted_iota(jnp.int32, (ATTN_BLK, LANES), 1)
        row = lax.broadcasted_iota(jnp.int32, (LANES, ATTN_BLK), 0)
        side = lax.broadcasted_iota(jnp.int32, (LANES, 2 * ATTN_BLK), 0)
        first = lax.broadcasted_iota(jnp.int32, (LANES, 2 * ATTN_BLK), 1) < ATTN_BLK
        c, ss = c_ref[...], ss_ref[...]
        zero = jnp.zeros((), BF16)
        sides = ((qd_ref, qtd_ref, dod_ref, dotd_ref, auxd_ref[...], valid_d),
                 (qo_ref, qto_ref, doo_ref, doto_ref, auxo_ref[...], valid_o))
        both = (0, 1)

        def head_halves(x, index):
            axis = 0 if index is lane else 1
            return jnp.concatenate([jnp.where(index < half, x, zero), jnp.where(index >= half, x, zero)], axis=axis)

        for t0 in range(0, tiles, ATTN_TILE_GROUP):
            group = range(t0, t0 + ATTN_TILE_GROUP)
            cols = {t: pl.ds(t * LANES, LANES) for t in group}
            kk, kk_t, vv_t = {}, {}, {}
            for t in group:
                base = AUX_PER_TILE * t
                hit = lambda lo: ((first & (side >= base + lo) & (side < base + lo + 3))
                                  | (jnp.logical_not(first) & (side >= base + lo + 3) & (side < base + lo + 6)))
                kk[t] = head_halves(k_ref[:, cols[t]], lane)
                kk_t[t] = jnp.concatenate([head_halves(kt_ref[cols[t], :], row),
                                           jnp.where(hit(0), -1.0 / scale, 0.0).astype(BF16)], axis=0)
                vv_t[t] = jnp.concatenate([head_halves(vt_ref[cols[t], :], row),
                                           jnp.where(hit(6), -1.0, 0.0).astype(BF16)], axis=0)
            sc = {(t, w): _dot(jnp.concatenate([sides[w][0][:, cols[t]], sides[w][4]], axis=1), kk_t[t])
                  for t in group for w in both}
            dpd = {(t, w): _dot(jnp.concatenate([sides[w][2][:, cols[t]], sides[w][4]], axis=1), vv_t[t])
                   for t in group for w in both}
            p = {i: jnp.where(sides[i[1]][5], jnp.exp(sc[i] * scale), 0.0) for i in sc}
            ds = {i: (p[i] * dpd[i] * scale).astype(BF16) for i in sc}
            pb = {i: p[i].astype(BF16) for i in sc}
            dv_t = {t: sum(_dot(sides[w][3][cols[t], :], pb[t, w]) for w in both) for t in group}
            dk_t = {t: sum(_dot(sides[w][1][cols[t], :], ds[t, w]) for w in both) for t in group}
            dq = {i: _dot(ds[i], kk[i[0]]) for i in sc}
            for t in group:
                dq_now = carry[:, cols[t]] + dq[t, 0]
                carry[:, cols[t]] = dq[t, 1]
                dk = jnp.where(row < half, dk_t[t][:, :ATTN_BLK], dk_t[t][:, ATTN_BLK:]).T
                dv = jnp.where(row < half, dv_t[t][:, :ATTN_BLK], dv_t[t][:, ATTN_BLK:]).T
                out_ref[0, :, cols[t]] = _rotate(dq_now, c, ss, -1.0).astype(BF16)
                out_ref[1, :, cols[t]] = _rotate(dk, c, ss, -1.0).astype(BF16)
                out_ref[2, :, cols[t]] = dv.astype(BF16)

    def nxt(i):
        return jnp.minimum(i + 1, nq - 1)

    def piece(p, shift):
        if shift:
            return pl.BlockSpec((None, ATTN_BLK, d), lambda i: (p, nxt(i), 0))
        return pl.BlockSpec((None, ATTN_BLK, d), lambda i: (p, i, 0))

    def piece_t(p, shift):
        if shift:
            return pl.BlockSpec((None, d, ATTN_BLK), lambda i: (p, 0, nxt(i)))
        return pl.BlockSpec((None, d, ATTN_BLK), lambda i: (p, 0, i))

    def rows(width, shift):
        if shift:
            return pl.BlockSpec((ATTN_BLK, width), lambda i: (nxt(i), 0))
        return pl.BlockSpec((ATTN_BLK, width), lambda i: (i, 0))

    def do_t_spec(shift):
        if shift:
            return pl.BlockSpec((None, d, ATTN_BLK), lambda i: (nxt(i) // per, 0, nxt(i) % per))
        return pl.BlockSpec((None, d, ATTN_BLK), lambda i: (i // per, 0, i % per))

    return pl.pallas_call(
        body, name=name, grid=(nq,),
        in_specs=[piece(0, False), piece(0, True), piece(1, False),
                  piece_t(0, False), piece_t(0, True), piece_t(1, False), piece_t(2, False),
                  rows(d, False), rows(d, True), do_t_spec(False), do_t_spec(True),
                  rows(LANES, False), rows(LANES, True), rows(LANES, False), rows(LANES, False)],
        out_specs=pl.BlockSpec((3, ATTN_BLK, d), lambda i: (0, i, 0)),
        out_shape=jax.ShapeDtypeStruct((3, s, d), BF16),
        scratch_shapes=[pltpu.VMEM((ATTN_BLK, d), F32)],
        compiler_params=_params(("arbitrary",)),
    )(qkv, qkv, qkv, qkv_t, qkv_t, qkv_t, qkv_t, do, do, do_t, do_t, aux, aux, *tables)


def _dx_combine(du, parts):
    s, d = du.shape

    def body(du_ref, p0, p1, p2, out_ref):
        out_ref[...] = ALPHA * du_ref[...] + p0[0]
        for dil, p_ref in zip(DILATIONS[1:], (p1, p2)):
            for r in range(dil):
                out_ref[_class_rows(r, dil), :] += p_ref[r]

    return pl.pallas_call(
        body, name="dx_combine", grid=(s // POS_BLK, d // LANES),
        in_specs=[_pos_spec()] + [_class_spec(dil) for dil in DILATIONS], out_specs=_pos_spec(),
        out_shape=jax.ShapeDtypeStruct((s, d), F32),
        compiler_params=_params(("parallel", "parallel")),
    )(du, *[_class_view(p, dil) for p, dil in zip(parts, DILATIONS)])


HGRN_ROWS = 1024


def _tri(lower, copies=1):
    i = lax.broadcasted_iota(jnp.int32, (HGRN_CHUNK, HGRN_CHUNK * copies), 0)
    j = lax.broadcasted_iota(jnp.int32, (HGRN_CHUNK, HGRN_CHUNK * copies), 1) % HGRN_CHUNK
    return (j <= i) if lower else (j >= i)


def _hgrn_gates(qr, z, lb):
    sq = jax.nn.sigmoid(qr)
    e = jnp.exp(-jnp.abs(z))
    big = 1.0 / (1.0 + e)
    small = e * big
    sg = jnp.where(z >= 0, big, small)
    sn = jnp.where(z >= 0, small, big)
    return sq, qr * sq, sg, sn, lb + (1.0 - lb) * sg, (1.0 - lb) * sn


def _hgrn_fwd(p2, lb, norm_g):
    _, s, d = p2.shape
    nblk = s // HGRN_ROWS
    cps = HGRN_ROWS // HGRN_CHUNK

    def body(p_ref, lb_ref, g_ref, o_ref, on_ref, st_ref, state):
        @pl.when(pl.program_id(1) == 0)
        def _():
            state[...] = jnp.zeros_like(state)

        chunks = [pl.ds(c * HGRN_CHUNK, HGRN_CHUNK) for c in range(cps)]
        ltri = _tri(True)
        lsel = _tri(True, 3).astype(BF16)
        _, q, _, _, f, key = _hgrn_gates(p_ref[0], p_ref[1], lb_ref[...])
        lf = jnp.log(f)
        v = [p_ref[2, rows, :].astype(BF16) for rows in chunks]
        b = [_exact_dot(lsel, lf[c * HGRN_CHUNK:(c + 1) * HGRN_CHUNK]) for c in range(cps)]
        b_last = [bc[HGRN_CHUNK - 1:HGRN_CHUNK, :] for bc in b]
        q = [q[c * HGRN_CHUNK:(c + 1) * HGRN_CHUNK] for c in range(cps)]
        key = [key[c * HGRN_CHUNK:(c + 1) * HGRN_CHUNK] for c in range(cps)]
        qd = [(q[c] * jnp.exp(b[c])).astype(BF16) for c in range(cps)]
        kd = [(key[c] * jnp.exp(-b[c])).astype(BF16) for c in range(cps)]
        k2 = [(key[c] * jnp.exp(b_last[c] - b[c])).astype(BF16) for c in range(cps)]
        a = [jnp.where(ltri, _dot_nt(qd[c], kd[c]), 0.0).astype(BF16) for c in range(cps)]
        kv = [_dot_tn(v[c], k2[c]) for c in range(cps)]
        st, sts = state[...], []
        for c in range(cps):
            sts.append(st)
            st_ref[0, c] = st
            st = st * jnp.exp(b_last[c]) + kv[c]
        state[...] = st
        gv = g_ref[...]
        for c in range(cps):
            o = _dot(a[c], v[c]) + _dot_nt(qd[c], sts[c].astype(BF16))
            o_ref[chunks[c], :] = o
            r = lax.rsqrt(jnp.mean(o * o, axis=-1, keepdims=True) + RMS_EPS)
            on_ref[chunks[c], :] = (o * r * gv).astype(BF16)

    vec = pl.BlockSpec((1, LANES), lambda h, c: (0, h))
    col = pl.BlockSpec((HGRN_ROWS, LANES), lambda h, c: (c, h))
    return pl.pallas_call(
        body, name="hgrn_fwd", grid=(HGRN_HEADS, nblk),
        in_specs=[pl.BlockSpec((3, HGRN_ROWS, LANES), lambda h, c: (0, c, h)), vec, vec],
        out_specs=[col, col, pl.BlockSpec((1, cps, LANES, LANES), lambda h, c: (h, c, 0, 0))],
        out_shape=[jax.ShapeDtypeStruct((s, d), F32), jax.ShapeDtypeStruct((s, d), BF16),
                   jax.ShapeDtypeStruct((HGRN_HEADS, s // HGRN_CHUNK, LANES, LANES), F32)],
        scratch_shapes=[pltpu.VMEM((LANES, LANES), F32)],
        compiler_params=_params(("parallel", "arbitrary")),
    )(p2, lb, norm_g)


def _hgrn_bwd(p2, lb, norm_g, o_raw, states, dyn):
    _, s, d = p2.shape
    nblk = s // HGRN_ROWS
    cps = HGRN_ROWS // HGRN_CHUNK

    def body(p_ref, lb_ref, g_ref, o_ref, st_ref, dy_ref, dp_ref, dg_ref, dlb_ref, dstate):
        @pl.when(pl.program_id(1) == 0)
        def _():
            dstate[...] = jnp.zeros_like(dstate)
            dg_ref[...] = jnp.zeros_like(dg_ref)
            dlb_ref[...] = jnp.zeros_like(dlb_ref)

        n = cps
        cut = lambda t: [t[c * HGRN_CHUNK:(c + 1) * HGRN_CHUNK] for c in range(n)]
        chunks = [pl.ds(c * HGRN_CHUNK, HGRN_CHUNK) for c in range(n)]
        lbv = lb_ref[...]
        ltri = _tri(True)
        lsel = _tri(True, 3).astype(BF16)
        usel = _tri(False, 3).astype(BF16)
        last_row = lax.broadcasted_iota(jnp.int32, (HGRN_CHUNK, LANES), 0) == HGRN_CHUNK - 1
        qr, z = p_ref[0], p_ref[1]
        sq, q, sg, sn, f, key = _hgrn_gates(qr, z, lbv)
        lf = jnp.log(f)
        v = [p_ref[2, rows, :].astype(BF16) for rows in chunks]
        o, dyv = o_ref[...], dy_ref[...]
        r = lax.rsqrt(jnp.mean(o * o, axis=-1, keepdims=True) + RMS_EPS)
        oh = o * r
        doh = dyv * g_ref[...]
        do = cut((r * (doh - oh * jnp.mean(doh * oh, axis=-1, keepdims=True))).astype(BF16))
        dg_ref[...] += jnp.sum(dyv * oh, axis=0, keepdims=True)
        b = [_exact_dot(lsel, lfc) for lfc in cut(lf)]
        b_last = [bc[HGRN_CHUNK - 1:HGRN_CHUNK, :] for bc in b]
        q, key = cut(q), cut(key)
        eb = [jnp.exp(bc) for bc in b]
        enb = [jnp.exp(-bc) for bc in b]
        e2 = [jnp.exp(b_last[c] - b[c]) for c in range(n)]
        dec = [jnp.exp(bl) for bl in b_last]
        qd_f = [q[c] * eb[c] for c in range(n)]
        kd_f = [key[c] * enb[c] for c in range(n)]
        k2_f = [key[c] * e2[c] for c in range(n)]
        qd, kd, k2 = ([t.astype(BF16) for t in ts] for ts in (qd_f, kd_f, k2_f))
        a = [jnp.where(ltri, _dot_nt(qd[c], kd[c]), 0.0).astype(BF16) for c in range(n)]
        da = [jnp.where(ltri, _dot_nt(do[c], v[c]), 0.0).astype(BF16) for c in range(n)]
        grow = [_dot_tn(do[c], qd[c]) for c in range(n)]
        dst, dsts = dstate[...], [None] * n
        for c in reversed(range(n)):
            dsts[c] = dst
            dst = dst * dec[c] + grow[c]
        dstate[...] = dst
        st = [st_ref[0, c] for c in range(n)]
        dstb = [t.astype(BF16) for t in dsts]
        dv = [_dot_tn(a[c], do[c]) + _dot_nt(k2[c], dstb[c]) for c in range(n)]
        dqd = [_dot(da[c], kd[c]) + _dot(do[c], st[c].astype(BF16)) for c in range(n)]
        dkd = [_dot_tn(da[c], qd[c]) for c in range(n)]
        dk2 = [_dot(v[c], dstb[c]) for c in range(n)]
        db = []
        for c in range(n):
            ddec = jnp.sum(dsts[c] * st[c], axis=0, keepdims=True)
            db_last = jnp.sum(dk2[c] * k2_f[c], axis=0, keepdims=True) + ddec * dec[c]
            db.append(dqd[c] * qd_f[c] - dkd[c] * kd_f[c] - dk2[c] * k2_f[c] + jnp.where(last_row, db_last, 0.0))
        dlf = [_exact_dot(usel, dbc) for dbc in db]
        f, sg, sn, sq, qr = cut(f), cut(sg), cut(sn), cut(sq), cut(qr)
        dlb_acc = jnp.zeros((1, LANES), F32)
        for c in range(n):
            dkey = dkd[c] * enb[c] + dk2[c] * e2[c]
            common = dlf[c] / f[c] - dkey
            dlb_acc += jnp.sum(common * sn[c], axis=0, keepdims=True)
            dp_ref[0, chunks[c], :] = (dqd[c] * eb[c] * (sq[c] * (1.0 + qr[c] * (1.0 - sq[c])))).astype(BF16)
            dp_ref[1, chunks[c], :] = (common * ((1.0 - lbv) * sg[c] * sn[c])).astype(BF16)
            dp_ref[2, chunks[c], :] = dv[c].astype(BF16)
        dlb_ref[...] += dlb_acc

    def rev(c):
        return nblk - 1 - c

    vec = pl.BlockSpec((1, LANES), lambda h, c: (0, h))
    col = pl.BlockSpec((HGRN_ROWS, LANES), lambda h, c: (rev(c), h))
    p3 = pl.BlockSpec((3, HGRN_ROWS, LANES), lambda h, c: (0, rev(c), h))
    return pl.pallas_call(
        body, name="hgrn_bwd", grid=(HGRN_HEADS, nblk),
        in_specs=[p3, vec, vec, col, pl.BlockSpec((1, cps, LANES, LANES), lambda h, c: (h, rev(c), 0, 0)), col],
        out_specs=[p3, vec, vec],
        out_shape=[jax.ShapeDtypeStruct((3, s, d), BF16), jax.ShapeDtypeStruct((1, d), F32),
                   jax.ShapeDtypeStruct((1, d), F32)],
        scratch_shapes=[pltpu.VMEM((LANES, LANES), F32)],
        compiler_params=_params(("parallel", "arbitrary")),
    )(p2, lb, norm_g, o_raw, states, dyn)


SMALL_ROWS = 16
FUSED_TM = 512


def _relu2(h):
    r = jnp.maximum(h.astype(F32), 0.0)
    return r * r


def _dact_epi(acc, h):
    return (acc * (2.0 * jnp.maximum(h.astype(F32), 0.0)),)


def _rope_epi(j, acc, ct, sst):
    acc_t = acc.T
    out_t = jnp.where(j >= 2, acc_t, _rotate_t(acc_t, ct, sst)).astype(BF16)
    return out_t.T, out_t


def _local_step(x, target, get_w, small, on_grads, deps=()):
    s, d = x.shape
    lb = small["lb"]
    gq = 3 * d
    gt = gq // 1024

    def ln_of(a, w, x_in, which, layer, *, name, k, a_pre=None, deps=()):
        g, b = small[f"ln_{which}_g"][layer:layer + 1], small[f"ln_{which}_b"][layer:layer + 1]
        return _mm(a, w, "nn", name=name, m=s, n=d, k=k, tm=FUSED_TM, tk=k, a_pre=a_pre, out_dtypes=(F32, BF16, F32),
                   row_outs=((F32, 1),), epi=_ln_epi, tile_extras=(x_in,), vec_extras=(g, b), deps=deps)

    def ln_grad_of(a, w, du_next, xh, rstd, which, layer, *, name, k, a_split=1, deps=()):
        return _mm(a, w, "nt", name=name, m=s, n=d, k=k, tm=FUSED_TM, tk=k, a_split=a_split,
                   out_dtypes=(F32, BF16), vec_outs=2, epi=_ln_grad_epi, tile_extras=(du_next, xh), row_extras=(rstd,),
                   vec_extras=(small[f"ln_{which}_g"][layer:layer + 1],), deps=deps)

    def ffn_fwd(xb, x_in, w_up, w_down, tag):
        h = _mm(xb, w_up, "nn", name=f"ffn_up_{tag}", m=s, n=D_FF, k=d, out_dtypes=(BF16,))
        return (h,) + tuple(ln_of(h, w_down, x_in, "ffn", tag, name=f"ffn_down_{tag}", k=D_FF, a_pre=_relu2))

    def ffn_bwd(du, dub, xb, h, w_up, w_down, xh, rstd, tag):
        dh = _mm(dub, w_down, "nt", name=f"ffn_dact_{tag}", m=s, n=D_FF, k=d, out_dtypes=(BF16,), epi=_dact_epi,
                 tile_extras=(h,))
        g_down = _mm(h, dub, "tn", name=f"ffn_gdown_{tag}", m=D_FF, n=d, k=s, a_pre=_relu2, out_dtypes=(BF16,))
        g_up = _mm(xb, dh, "tn", name=f"ffn_gup_{tag}", m=d, n=D_FF, k=s, out_dtypes=(BF16,))
        after = on_grads(f"ffn{tag}", {f"ffn_w_down{tag}": g_down, f"ffn_w_up{tag}": g_up})
        return ln_grad_of(dh, w_up, du, xh, rstd, "mix", tag, name=f"ffn_dx_{tag}", k=D_FF, deps=after)

    xs = _prep_x(x, deps)
    w_ain, after = get_w("attn_w_in", None)
    tabs, qkvs, qkv_ts, o_parts, lse_parts = [], [], [], [], []
    for g, dil in enumerate(DILATIONS):
        tabs.append(_rope_tables(s, dil))
        tabs_t = [t[:, :ATTN_HEAD_DIM].T for t in tabs[g]]
        qkv, qkv_t = _mm(xs[g], w_ain, "nn", name=f"attn_in_{g}", m=s, n=gq, k=d, b_col_off=g * gt, out_split=3,
                         out_dtypes=(BF16,), epi=_rope_epi, epi_wants_j=True, row_extras=tabs_t, t_out=True,
                         deps=after if g == 0 else ())
        qkvs.append(qkv)
        qkv_ts.append(qkv_t)
        o_g, lse_g = _attn_fwd(qkv, dil, name=f"attn_fwd_{g}")
        o_parts.append(o_g)
        lse_parts.append(lse_g)
    o_f, o_b, lse_t = _attn_combine(o_parts, lse_parts)
    w_aout, after = get_w("attn_w_out", o_b)
    x1, x1b, xh1, r1 = ln_of(o_b, w_aout, x, "mix", 0, name="attn_out", k=d, deps=after)
    w_up0, w_down0 = get_w("ffn_w_up0", o_b)[0], get_w("ffn_w_down0", o_b)[0]
    h0, x2, x2b, xh2, r2 = ffn_fwd(x1b, x1, w_up0, w_down0, 0)
    w_hin, w_hout, norm_g = get_w("hgrn_w_in", x2b)[0], get_w("hgrn_w_out", x2b)[0], get_w("hgrn_norm_g", x2b)[0]
    p2 = _mm(x2b, w_hin, "nn", name="hgrn_in", m=s, n=3 * d, k=d, out_split=3)
    o_raw, o_n, states = _hgrn_fwd(p2, lb, norm_g)
    w_up1, w_down1 = get_w("ffn_w_up1", o_n)[0], get_w("ffn_w_down1", o_n)[0]
    x3, x3b, xh3, r3 = ln_of(o_n, w_hout, x2, "mix", 1, name="hgrn_out", k=d)
    h1 = _mm(x3b, w_up1, "nn", name="ffn_up_1", m=s, n=D_FF, k=d, out_dtypes=(BF16,))
    du, dub, dg_ffn1, db_ffn1, sq = _mm(
        h1, w_down1, "nn", name="ffn_down_1", m=s, n=d, k=D_FF, tm=FUSED_TM, tk=D_FF, a_pre=_relu2,
        out_dtypes=(F32, BF16), vec_outs=3, epi=_ln_loss_epi, tile_extras=(x3, target),
        vec_extras=(small["ln_ffn_g"][1:2], small["ln_ffn_b"][1:2]))
    du, dub, dg_mix1, db_mix1 = ffn_bwd(du, dub, x3b, h1, w_up1, w_down1, xh3, r3, 1)
    dyn = _mm(dub, w_hout, "nt", name="hgrn_dout", m=s, n=d, k=d)
    g_hout = _mm(o_n, dub, "tn", name="hgrn_gout", m=d, n=d, k=s, out_dtypes=(BF16,))
    dp2, d_norm_g, d_lb = _hgrn_bwd(p2, lb, norm_g, o_raw, states, dyn)
    g_hin = _mm(x2b, dp2, "tn", name="hgrn_gin", m=d, n=3 * d, k=s, b_split=3, out_dtypes=(BF16,))
    after = on_grads("hgrn", {"hgrn_w_out": g_hout, "hgrn_w_in": g_hin})
    du, dub, dg_ffn0, db_ffn0 = ln_grad_of(dp2, w_hin, du, xh2, r2, "ffn", 0, name="hgrn_dx", k=3 * d, a_split=3,
                                           deps=after)
    du, dub, dg_mix0, db_mix0 = ffn_bwd(du, dub, x1b, h0, w_up0, w_down0, xh1, r1, 0)
    g_aout = _mm(o_b, dub, "tn", name="attn_gout", m=d, n=d, k=s, out_dtypes=(BF16,))
    sm1 = jax.nn.softmax(small["lb_logits"], axis=0)
    d_l1 = d_lb * (sm1[0:1] * sm1[1:2])
    zeros = jnp.zeros((SMALL_ROWS - 12, d), F32)
    loss_row = jnp.broadcast_to(0.5 * jnp.sum(sq) / d, (1, d))
    small_grads = jnp.concatenate([dg_mix0, dg_mix1, db_mix0, db_mix1, dg_ffn0, dg_ffn1, db_ffn0, db_ffn1,
                                   -d_l1, d_l1, d_norm_g, loss_row, zeros], axis=0)
    after = on_grads("attn_out", {"attn_w_out": g_aout, "small": small_grads})
    do = _mm(dub, w_aout, "nt", name="attn_dout", m=s, n=d, k=d, deps=after)
    do_parts, do_ts, aux_parts = _attn_bwd_prep(do, o_f, lse_t)
    g_ain, dqkvs = None, []
    for g, dil in enumerate(DILATIONS):
        dqkvs.append(_attn_bwd(qkvs[g], qkv_ts[g], do_parts[g], do_ts[g], aux_parts[g], tabs[g], dil,
                               name=f"attn_bwd_{g}"))
        g_ain = _mm(xs[g], dqkvs[g], "tn", name=f"attn_gin_{g}", m=d, n=gq, k=s, b_split=3, out_dtypes=(BF16,),
                    out_col_off=g * gt, out_cols=3 * gq, alias=g_ain)
    after = on_grads("attn_in", {"attn_w_in": g_ain})
    dx_parts = [_mm(dqkvs[g], w_ain, "nt", name=f"attn_dx_{g}", m=s, n=d, k=gq, tm=FUSED_TM, tk=gq, a_split=3, b_k_off=g,
                    deps=after if g == 0 else ())
                for g in range(len(DILATIONS))]
    return _dx_combine(du, dx_parts)


def _mesh_place():
    x, y, c = lax.axis_index("x"), lax.axis_index("y"), lax.axis_index("c")
    return x, y, c, 4 * x + 2 * y + c


def _peer(x, y, c, k):
    px = 1 - x if (k >> 2) & 1 else x
    py = 1 - y if (k >> 1) & 1 else y
    pc = 1 - c if k & 1 else c
    return (px, py, pc), 4 * px + 2 * py + pc


def _window(ref, axis, size, idx):
    if axis is None:
        return ref
    sl = [slice(None)] * len(ref.shape)
    sl[axis] = pl.ds(idx * size, size)
    return ref.at[tuple(sl)]


_HBM = pl.BlockSpec(memory_space=pltpu.HBM)
_SEM = pl.BlockSpec(memory_space=pltpu.SEMAPHORE)
_EFFECT = pltpu.SideEffectType.DATAFLOW_SIDE_EFFECTING


def _xchg_ends(src_ref, land_ref, axis, gather, me, other):
    if gather:
        size = src_ref.shape[axis]
        return src_ref, _window(land_ref, axis, size, me), _window(land_ref, axis, size, other)
    size = None if axis is None else src_ref.shape[axis] // N_DEV
    return _window(src_ref, axis, size, other), land_ref.at[me], land_ref.at[other]


def _xchg_start(srcs, lands, axes, *, gather, name, deps=(), peers=tuple(range(1, N_DEV))):
    n = len(srcs)
    nd = len(deps)

    def body(*refs):
        src_refs, land_refs = refs[:n], refs[n:2 * n]
        send, recv = refs[2 * n + nd:3 * n + nd], refs[3 * n + nd:4 * n + nd]
        token = refs[-1]
        x, y, c, me = _mesh_place()
        for k in peers:
            peer, pidx = _peer(x, y, c, k)
            for i in range(n):
                src, dst, _ = _xchg_ends(src_refs[i], land_refs[i], axes[i], gather, me, pidx)
                pltpu.make_async_remote_copy(
                    src_ref=src, dst_ref=dst, send_sem=send[i].at[k - 1], recv_sem=recv[i].at[k - 1],
                    device_id=peer, device_id_type=pl.DeviceIdType.MESH).start()
        for i in range(n):
            src, dst, _ = _xchg_ends(src_refs[i], land_refs[i], axes[i], gather, me, me)
            pltpu.make_async_copy(src, dst, send[i].at[N_DEV - 1]).start()
        token[...] = jnp.zeros_like(token)

    bufs = list(srcs) + list(lands)
    outs = pl.pallas_call(
        body, name=name,
        out_shape=[pltpu.SemaphoreType.DMA((N_DEV,))] * (2 * n) + [pltpu.HBM(b.shape, b.dtype) for b in bufs]
        + [jax.ShapeDtypeStruct((8, LANES), F32)],
        in_specs=[_HBM] * (2 * n) + [pl.BlockSpec(memory_space=pl.ANY)] * nd,
        out_specs=[_SEM] * (2 * n) + [_HBM] * (2 * n) + [pl.BlockSpec(memory_space=pltpu.VMEM)],
        input_output_aliases={i: 2 * n + i for i in range(2 * n)},
        compiler_params=pltpu.CompilerParams(has_side_effects=_EFFECT),
    )(*[pltpu.with_memory_space_constraint(b, pltpu.HBM) for b in bufs], *deps)
    return dict(send=outs[:n], recv=outs[n:2 * n], srcs=outs[2 * n:3 * n], lands=outs[3 * n:4 * n], token=outs[-1],
                axes=list(axes), gather=gather)


SIBLING = 1
SAME_CORE = (2, 4, 6)


def _gather_relay(xc, after, *, name):
    axis = xc["axes"][0]

    def body(src_ref, land_ref, send1, recv1, after_ref, src_out, land_out, send2, recv2):
        x, y, c, me = _mesh_place()
        sibling, _ = _peer(x, y, c, SIBLING)
        for j, k in enumerate(SAME_CORE):
            peer, pidx = _peer(x, y, c, k)
            src, _, got = _xchg_ends(src_ref, land_ref, axis, True, me, pidx)
            pltpu.make_async_remote_copy(
                src_ref=src, dst_ref=got, send_sem=send1.at[k - 1], recv_sem=recv1.at[k - 1],
                device_id=peer, device_id_type=pl.DeviceIdType.MESH).wait_recv()
            pltpu.make_async_remote_copy(
                src_ref=got, dst_ref=got, send_sem=send2.at[j], recv_sem=recv2.at[j],
                device_id=sibling, device_id_type=pl.DeviceIdType.MESH).start()

    src, land = xc["srcs"][0], xc["lands"][0]
    outs = pl.pallas_call(
        body, name=name,
        out_shape=[pltpu.HBM(src.shape, src.dtype), pltpu.HBM(land.shape, land.dtype)]
        + [pltpu.SemaphoreType.DMA((len(SAME_CORE),))] * 2,
        in_specs=[_HBM, _HBM, _SEM, _SEM, pl.BlockSpec(memory_space=pl.ANY)],
        out_specs=[_HBM, _HBM, _SEM, _SEM], input_output_aliases={0: 0, 1: 1},
        compiler_params=pltpu.CompilerParams(has_side_effects=_EFFECT),
    )(src, land, xc["send"][0], xc["recv"][0], after)
    return dict(src=outs[0], land=outs[1], send=outs[2], recv=outs[3])


def _gather_relay_wait(xc, relay, after, *, name):
    axis = xc["axes"][0]

    def body(src_ref, land_ref, send1, recv1, send2, recv2, after_ref, src_out, land_out):
        x, y, c, me = _mesh_place()
        sibling, sidx = _peer(x, y, c, SIBLING)
        for k in (SIBLING,) + SAME_CORE:
            peer, pidx = _peer(x, y, c, k)
            src, dst, got = _xchg_ends(src_ref, land_ref, axis, True, me, pidx)
            pltpu.make_async_remote_copy(
                src_ref=src, dst_ref=dst, send_sem=send1.at[k - 1], recv_sem=recv1.at[k - 1],
                device_id=peer, device_id_type=pl.DeviceIdType.MESH).wait_send()
        src, dst, got = _xchg_ends(src_ref, land_ref, axis, True, me, sidx)
        pltpu.make_async_remote_copy(
            src_ref=src, dst_ref=got, send_sem=send1.at[SIBLING - 1], recv_sem=recv1.at[SIBLING - 1],
            device_id=sibling, device_id_type=pl.DeviceIdType.MESH).wait_recv()
        src, dst, _ = _xchg_ends(src_ref, land_ref, axis, True, me, me)
        pltpu.make_async_copy(src, dst, send1.at[N_DEV - 1]).wait()
        for j, k in enumerate(SAME_CORE):
            _, pidx = _peer(x, y, c, k)
            _, qidx = _peer(x, y, c, k ^ SIBLING)
            _, _, sent = _xchg_ends(src_ref, land_ref, axis, True, me, pidx)
            _, _, got = _xchg_ends(src_ref, land_ref, axis, True, me, qidx)
            pltpu.make_async_remote_copy(
                src_ref=sent, dst_ref=sent, send_sem=send2.at[j], recv_sem=recv2.at[j],
                device_id=sibling, device_id_type=pl.DeviceIdType.MESH).wait_send()
            pltpu.make_async_remote_copy(
                src_ref=got, dst_ref=got, send_sem=send2.at[j], recv_sem=recv2.at[j],
                device_id=sibling, device_id_type=pl.DeviceIdType.MESH).wait_recv()

    outs = pl.pallas_call(
        body, name=name, out_shape=[pltpu.HBM(relay["src"].shape, relay["src"].dtype),
                                    pltpu.HBM(relay["land"].shape, relay["land"].dtype)],
        in_specs=[_HBM, _HBM, _SEM, _SEM, _SEM, _SEM, pl.BlockSpec(memory_space=pl.ANY)],
        out_specs=[_HBM, _HBM], input_output_aliases={0: 0, 1: 1},
        compiler_params=pltpu.CompilerParams(has_side_effects=_EFFECT),
    )(relay["src"], relay["land"], xc["send"][0], xc["recv"][0], relay["send"], relay["recv"], after)
    return outs[1]


def _xchg_wait(xc, items, after, *, name):
    m = len(items)
    gather = xc["gather"]
    axes = [xc["axes"][i] for i in items]

    def body(*refs):
        src_refs, land_refs = refs[:m], refs[m:2 * m]
        send, recv = refs[2 * m:3 * m], refs[3 * m:4 * m]
        x, y, c, me = _mesh_place()
        for k in range(1, N_DEV):
            peer, pidx = _peer(x, y, c, k)
            for j in range(m):
                src, dst, got = _xchg_ends(src_refs[j], land_refs[j], axes[j], gather, me, pidx)
                pltpu.make_async_remote_copy(
                    src_ref=src, dst_ref=dst, send_sem=send[j].at[k - 1], recv_sem=recv[j].at[k - 1],
                    device_id=peer, device_id_type=pl.DeviceIdType.MESH).wait_send()
                pltpu.make_async_remote_copy(
                    src_ref=src, dst_ref=got, send_sem=send[j].at[k - 1], recv_sem=recv[j].at[k - 1],
                    device_id=peer, device_id_type=pl.DeviceIdType.MESH).wait_recv()
        for j in range(m):
            src, dst, _ = _xchg_ends(src_refs[j], land_refs[j], axes[j], gather, me, me)
            pltpu.make_async_copy(src, dst, send[j].at[N_DEV - 1]).wait()

    bufs = [xc["srcs"][i] for i in items] + [xc["lands"][i] for i in items]
    sems = [xc["send"][i] for i in items] + [xc["recv"][i] for i in items]
    outs = pl.pallas_call(
        body, name=name, out_shape=[pltpu.HBM(b.shape, b.dtype) for b in bufs],
        in_specs=[_HBM] * (2 * m) + [_SEM] * (2 * m) + [pl.BlockSpec(memory_space=pl.ANY)],
        out_specs=[_HBM] * (2 * m), input_output_aliases={j: j for j in range(2 * m)},
        compiler_params=pltpu.CompilerParams(has_side_effects=_EFFECT),
    )(*bufs, *sems, after)
    return outs[m:]


def _cast_bf16(a, *, name):
    r, c = a.shape
    tr = min(r, 512)

    def body(a_ref, o_ref):
        o_ref[...] = a_ref[...].astype(BF16)

    spec = pl.BlockSpec((tr, c), lambda i: (i, 0))
    return pl.pallas_call(body, name=name, grid=(r // tr,), in_specs=[spec], out_specs=spec,
                          out_shape=jax.ShapeDtypeStruct((r, c), BF16), compiler_params=_params(("parallel",)))(a)


def _adamw(slabs, w, m, v, *, name):
    layers, r, c = w.shape
    tr = min(r, 256)

    def body(*refs):
        s_refs = refs[:layers]
        w_ref, m_ref, v_ref, g_ref, d_ref, mo_ref, vo_ref = refs[layers:]
        for l in range(layers):
            g = s_refs[l][0].astype(F32)
            for i in range(1, N_DEV):
                g = g + s_refs[l][i].astype(F32)
            m2 = ADAM_B1 * m_ref[l] + (1.0 - ADAM_B1) * g
            v2 = ADAM_B2 * v_ref[l] + (1.0 - ADAM_B2) * (g * g)
            m_hat = m2 / (1.0 - ADAM_B1 ** ADAM_STEP)
            v_hat = v2 / (1.0 - ADAM_B2 ** ADAM_STEP)
            g_ref[l] = g
            d_ref[l] = -ADAM_LR * (m_hat / (jnp.sqrt(v_hat) + ADAM_EPS) + ADAM_WD * w_ref[l])
            mo_ref[l] = m2
            vo_ref[l] = v2

    spec = pl.BlockSpec((layers, tr, c), lambda i: (0, i, 0))
    return pl.pallas_call(
        body, name=name, grid=(r // tr,),
        in_specs=[pl.BlockSpec((N_DEV, tr, c), lambda i: (0, i, 0))] * layers + [spec, spec, spec],
        out_specs=[spec] * 4, out_shape=[jax.ShapeDtypeStruct((layers, r, c), F32)] * 4,
        compiler_params=_params(("parallel",)),
    )(*slabs, w, m, v)


GATHER_AXIS = {"attn_w_in": 1, "attn_w_out": 0, "ffn_w_up0": 1, "ffn_w_down0": 0, "hgrn_w_in": 1, "hgrn_w_out": 0,
               "hgrn_norm_g": 1, "ffn_w_up1": 1, "ffn_w_down1": 0}
GATHER_STAGES = (("attn_w_in",), ("attn_w_out", "ffn_w_up0", "ffn_w_down0", "hgrn_w_in", "hgrn_w_out", "hgrn_norm_g"),
                 ("ffn_w_up1", "ffn_w_down1"))
GATHER_WAITS = ((("attn_w_in",), 0, 1), (("attn_w_out", "ffn_w_up0", "ffn_w_down0"), 1, 2),
                (("hgrn_w_in", "hgrn_w_out", "hgrn_norm_g"), 1, None), (("ffn_w_up1", "ffn_w_down1"), 2, None))
SCATTER_AXIS = dict(GATHER_AXIS, small=None)
BIG = ("attn_w_in", "attn_w_out", "hgrn_w_in", "hgrn_w_out", "ffn_w_up", "ffn_w_down")
SMALL = ("lb_logits", "ln_mix_g", "ln_mix_b", "ln_ffn_g", "ln_ffn_b")
SMALL_ROW = {"ln_mix_g": 0, "ln_mix_b": 2, "ln_ffn_g": 4, "ln_ffn_b": 6, "lb_logits": 8}
NORM_G_ROW = 10
LOSS_ROW = 11


def kernel(x, attn_w_in, attn_w_out, hgrn_w_in, hgrn_w_out, hgrn_norm_g, lb_logits, ln_mix_g, ln_mix_b, ln_ffn_g, ln_ffn_b, ffn_w_up, ffn_w_down, loss_target, m_attn_w_in, m_attn_w_out, m_hgrn_w_in, m_hgrn_w_out, m_hgrn_norm_g, m_lb_logits, m_ln_mix_g, m_ln_mix_b, m_ln_ffn_g, m_ln_ffn_b, m_ffn_w_up, m_ffn_w_down, v_attn_w_in, v_attn_w_out, v_hgrn_w_in, v_hgrn_w_out, v_hgrn_norm_g, v_lb_logits, v_ln_mix_g, v_ln_mix_b, v_ln_ffn_g, v_ln_ffn_b, v_ffn_w_up, v_ffn_w_down):
    wts = dict(attn_w_in=attn_w_in, attn_w_out=attn_w_out, hgrn_w_in=hgrn_w_in, hgrn_w_out=hgrn_w_out,
               hgrn_norm_g=hgrn_norm_g, lb_logits=lb_logits, ln_mix_g=ln_mix_g, ln_mix_b=ln_mix_b, ln_ffn_g=ln_ffn_g,
               ln_ffn_b=ln_ffn_b, ffn_w_up=ffn_w_up, ffn_w_down=ffn_w_down)
    mom = dict(attn_w_in=m_attn_w_in, attn_w_out=m_attn_w_out, hgrn_w_in=m_hgrn_w_in, hgrn_w_out=m_hgrn_w_out,
               hgrn_norm_g=m_hgrn_norm_g, lb_logits=m_lb_logits, ln_mix_g=m_ln_mix_g, ln_mix_b=m_ln_mix_b,
               ln_ffn_g=m_ln_ffn_g, ln_ffn_b=m_ln_ffn_b, ffn_w_up=m_ffn_w_up, ffn_w_down=m_ffn_w_down)
    vel = dict(attn_w_in=v_attn_w_in, attn_w_out=v_attn_w_out, hgrn_w_in=v_hgrn_w_in, hgrn_w_out=v_hgrn_w_out,
               hgrn_norm_g=v_hgrn_norm_g, lb_logits=v_lb_logits, ln_mix_g=v_ln_mix_g, ln_mix_b=v_ln_mix_b,
               ln_ffn_g=v_ln_ffn_g, ln_ffn_b=v_ln_ffn_b, ffn_w_up=v_ffn_w_up, ffn_w_down=v_ffn_w_down)
    me = 4 * lax.axis_index("x") + 2 * lax.axis_index("y") + lax.axis_index("c")

    src = {"attn_w_in": attn_w_in[0], "attn_w_out": attn_w_out[0], "hgrn_w_in": hgrn_w_in[0], "hgrn_w_out": hgrn_w_out[0],
           "ffn_w_up0": ffn_w_up[0], "ffn_w_down0": ffn_w_down[0], "ffn_w_up1": ffn_w_up[1], "ffn_w_down1": ffn_w_down[1]}
    gathers, got = {}, {}

    def start_gather(stage, deps):
        shards, lands = [], []
        for nm in GATHER_STAGES[stage]:
            sh = hgrn_norm_g if nm == "hgrn_norm_g" else _cast_bf16(src[nm], name=f"cast_{nm}")
            ax = GATHER_AXIS[nm]
            shape = list(sh.shape)
            shape[ax] *= N_DEV
            shards.append(sh)
            lands.append(lax.empty(tuple(shape), sh.dtype))
        peers = (SIBLING,) + SAME_CORE if stage == 0 else tuple(range(1, N_DEV))
        gathers[stage] = _xchg_start(shards, lands, [GATHER_AXIS[nm] for nm in GATHER_STAGES[stage]], gather=True,
                                     name=f"gather_start_{stage}", deps=deps, peers=peers)
        return [gathers[stage]["token"]]

    def get_w(name, after):
        deps = []
        if name not in got:
            group, stage, then = [w for w in GATHER_WAITS if name in w[0]][0]
            xc = gathers[stage]
            if stage == 0:
                relay = _gather_relay(xc, xc["token"], name="gather_relay")
                res = [_gather_relay_wait(xc, relay, xc["token"], name=f"gather_wait_{group[0]}")]
            else:
                res = _xchg_wait(xc, [GATHER_STAGES[stage].index(nm) for nm in group], after,
                                 name=f"gather_wait_{group[0]}")
            got.update(zip(group, res))
            if then is not None:
                deps = start_gather(then, [res[0]])
        return got[name], deps

    first = start_gather(0, [])

    scattered = {}

    def on_grads(tag, grads):
        gnames = list(grads)
        axes = [SCATTER_AXIS[nm] for nm in gnames]
        stacks = []
        for nm, ax in zip(gnames, axes):
            shape = list(grads[nm].shape)
            if ax is not None:
                shape[ax] //= N_DEV
            stacks.append(lax.empty((N_DEV, *shape), grads[nm].dtype))
        scattered[tag] = (gnames, _xchg_start([grads[nm] for nm in gnames], stacks, axes, gather=False,
                                              name=f"scatter_start_{tag}"))
        return [scattered[tag][1]["token"]]

    sm = jax.nn.softmax(lb_logits, axis=0)
    csum = jnp.cumsum(sm, axis=0)
    small = dict(lb=(csum - csum[0:1])[1:2], lb_logits=lb_logits, ln_mix_g=ln_mix_g, ln_mix_b=ln_mix_b,
                 ln_ffn_g=ln_ffn_g, ln_ffn_b=ln_ffn_b)
    grad_x = _local_step(x[0], loss_target[0], get_w, small, on_grads, deps=first)
    out = {}

    def stack_small(src_):
        rows = [None] * SMALL_ROWS
        for name in SMALL:
            rows[SMALL_ROW[name]], rows[SMALL_ROW[name] + 1] = src_[name][0:1], src_[name][1:2]
        zero = jnp.zeros((1, x.shape[-1]), F32)
        return jnp.concatenate([zero if r is None else r for r in rows], axis=0)[None]

    def update(name, slabs):
        shape = wts[name].shape
        out[name] = [r.reshape(shape) for r in _adamw(slabs, wts[name], mom[name], vel[name], name=f"adamw_{name}")]
        return out[name][0]

    slabs, after = {}, grad_x
    for tag, (gnames, xc) in scattered.items():
        slabs.update(zip(gnames, _xchg_wait(xc, list(range(len(gnames))), after, name=f"scatter_wait_{tag}")))
        if tag == "ffn1":
            continue
        if tag == "ffn0":
            update("ffn_w_down", [slabs["ffn_w_down0"], slabs["ffn_w_down1"]])
            after = update("ffn_w_up", [slabs["ffn_w_up0"], slabs["ffn_w_up1"]])
        elif tag == "hgrn":
            update("hgrn_w_out", [slabs["hgrn_w_out"]])
            after = update("hgrn_w_in", [slabs["hgrn_w_in"]])
        elif tag == "attn_out":
            after = update("attn_w_out", [slabs["attn_w_out"]])
        else:
            after = update("attn_w_in", [slabs["attn_w_in"]])
    res = _adamw([slabs["small"]], stack_small(wts), stack_small(mom), stack_small(vel), name="adamw_small")
    for name in SMALL:
        out[name] = [r[0, SMALL_ROW[name]:SMALL_ROW[name] + 2] for r in res]
    loss = res[0][0, LOSS_ROW, 0]
    ng = hgrn_norm_g.shape[-1]
    ng_slabs = lax.dynamic_slice(slabs["small"], (0, NORM_G_ROW, me * ng), (N_DEV, 1, ng))
    out["hgrn_norm_g"] = [r[0] for r in _adamw([ng_slabs], hgrn_norm_g[None], m_hgrn_norm_g[None],
                                                v_hgrn_norm_g[None], name="adamw_norm_g")]
    order =("attn_w_in", "attn_w_out", "hgrn_w_in", "hgrn_w_out", "hgrn_norm_g", "lb_logits", "ln_mix_g", "ln_mix_b",
             "ln_ffn_g", "ln_ffn_b", "ffn_w_up", "ffn_w_down")
    return (loss, grad_x[None], *[out[nm][0] for nm in order], *[out[nm][1] for nm in order],
            *[out[nm][2] for nm in order], *[out[nm][3] for nm in order])
```

```python
import jax
import jax.numpy as jnp
from jax import lax
from jax.experimental import pallas as pl
from jax.experimental.pallas import tpu as pltpu

F32 = jnp.float32
BF16 = jnp.bfloat16

N_DEV = 8
LANES = 128
D_MODEL = 1024
ATTN_HEAD_DIM = 64
ATTN_HEADS = 16
ATTN_SCALE = ATTN_HEAD_DIM ** -0.5
ATTN_BLK = 128
DILATIONS = (1, 4, 16)
ROPE_THETA = 10000.0
HGRN_HEADS = 8
HGRN_CHUNK = 64
D_FF = 4096
LN_EPS = 1e-5
RMS_EPS = 1e-6
DEPTH = 2
ALPHA = (2 * DEPTH) ** 0.25
ADAM_LR, ADAM_B1, ADAM_B2, ADAM_EPS, ADAM_WD, ADAM_STEP = 0.001, 0.9, 0.999, 1e-08, 0.01, 10
VMEM_LIMIT = 48 * 1024 * 1024

_NT = (((1,), (1,)), ((), ()))
_TN = (((0,), (0,)), ((), ()))


def _dot(a, b):
    return jnp.dot(a, b, preferred_element_type=F32)


def _dot_nt(a, b):
    return lax.dot_general(a, b, _NT, preferred_element_type=F32)


def _dot_tn(a, b):
    return lax.dot_general(a, b, _TN, preferred_element_type=F32)


def _split3(x):
    p1 = x.astype(BF16)
    r1 = x - p1.astype(F32)
    p2 = r1.astype(BF16)
    p3 = (r1 - p2.astype(F32)).astype(BF16)
    return p1, p2, p3


def _exact_dot(sel3, x):
    return _dot(sel3, jnp.concatenate(_split3(x), axis=0))


def _exact_dot_r(x, sel3):
    return _dot(jnp.concatenate(_split3(x), axis=1), sel3)


def _params(sem=None):
    return pltpu.CompilerParams(dimension_semantics=sem, vmem_limit_bytes=VMEM_LIMIT)


def _mm(a, b, mode, *, name, m, n, k, tm=1024, tn=1024, tk=1024, out_dtypes=(F32,), epi=None, a_pre=None,
        tile_extras=(), row_extras=(), vec_extras=(), row_outs=(), vec_outs=0, a_split=1, b_split=1, out_split=1,
        b_col_off=0, b_k_off=0, out_col_off=0, out_cols=None, alias=None, epi_wants_j=False, deps=(), t_out=False):
    tm, tn, tk = min(tm, m), min(tn, n), min(tk, k)
    assert m % tm == 0 and n % tn == 0 and k % tk == 0, (name, m, n, k, tm, tn, tk)
    gm, gn, gk = m // tm, n // tn, k // tk
    if mode in ("nn", "nt"):
        if a_split > 1 and tk == k:
            a_spec = pl.BlockSpec((a_split, tm, k // a_split), lambda i, j, kk: (0, i, 0))
        elif a_split > 1:
            kc = (k // a_split) // tk
            a_spec = pl.BlockSpec((None, tm, tk), lambda i, j, kk: (kk // kc, i, kk % kc))
        else:
            a_spec = pl.BlockSpec((tm, tk), lambda i, j, kk: (i, kk))
    else:
        a_spec = pl.BlockSpec((tk, tm), lambda i, j, kk: (kk, i))
    if mode in ("nn", "tn"):
        if b_split > 1:
            nc = (n // b_split) // tn
            b_spec = pl.BlockSpec((None, tk, tn), lambda i, j, kk: (j // nc, kk, j % nc))
        else:
            b_spec = pl.BlockSpec((tk, tn), lambda i, j, kk: (kk + b_k_off, j + b_col_off))
    else:
        b_spec = pl.BlockSpec((tn, tk), lambda i, j, kk: (j + b_col_off, kk + b_k_off))
    if out_split > 1:
        nco = (n // out_split) // tn
        o_spec = pl.BlockSpec((None, tm, tn), lambda i, j, kk: (j // nco, i, j % nco))
        o_shape = (out_split, m, n // out_split)
    else:
        o_spec = pl.BlockSpec((tm, tn), lambda i, j, kk: (i, j + out_col_off))
        o_shape = (m, out_cols if out_cols is not None else n)
    n_ex = len(tile_extras) + len(row_extras) + len(vec_extras)
    n_out = len(out_dtypes)
    n_plain = n_out + len(row_outs)
    assert not vec_outs or gn == 1
    if epi is None:
        def epi(acc):
            return (acc,)
    dot = {"nn": _dot, "nt": _dot_nt, "tn": _dot_tn}[mode]

    def body(*refs):
        a_ref, b_ref = refs[0], refs[1]
        ex = refs[2:2 + n_ex]
        outs = refs[2 + n_ex + (1 if alias is not None else 0) + len(deps):][:n_plain + vec_outs + (1 if t_out else 0)]
        ii = pl.program_id(0)
        jj = pl.program_id(1)

        def product():
            if a_split > 1 and tk == k:
                av = jnp.concatenate([a_ref[p] for p in range(a_split)], axis=1)
            else:
                av = a_ref[...]
            if a_pre is not None:
                av = a_pre(av)
            return dot(av.astype(BF16), b_ref[...].astype(BF16))

        def finish(total):
            lead = (jj,) if epi_wants_j else ()
            res = epi(*lead, total, *[e[...] for e in ex])
            for o, r in zip(outs[:n_plain], res):
                o[...] = r.astype(o.dtype)
            for o, r in zip(outs[n_plain:n_plain + vec_outs], res[n_plain:]):
                @pl.when(ii == 0)
                def _(o=o, r=r):
                    o[...] = r

                @pl.when(ii > 0)
                def _(o=o, r=r):
                    o[...] += r
            if t_out:
                outs[-1][...] = res[-1].astype(outs[-1].dtype)

        if gk == 1:
            finish(product())
        else:
            acc = refs[-1]
            kk = pl.program_id(2)

            @pl.when(kk == 0)
            def _():
                acc[...] = product()

            @pl.when(kk > 0)
            def _():
                acc[...] += product()

            @pl.when(kk == gk - 1)
            def _():
                finish(acc[...])

    in_specs = [a_spec, b_spec] + [o_spec] * len(tile_extras)
    in_specs += [pl.BlockSpec((tm, r.shape[1]), lambda i, j, kk: (i, 0)) if r.shape[0] == m else
                 pl.BlockSpec((r.shape[0], tm), lambda i, j, kk: (0, i)) for r in row_extras]
    in_specs += [pl.BlockSpec((1, tn), lambda i, j, kk: (0, j))] * len(vec_extras)
    args = [a, b] + list(tile_extras) + list(row_extras) + list(vec_extras)
    io_alias = {}
    if alias is not None:
        in_specs.append(pl.BlockSpec(memory_space=pl.ANY))
        args.append(alias)
        io_alias = {len(args) - 1: 0}
    in_specs += [pl.BlockSpec(memory_space=pl.ANY)] * len(deps)
    args += list(deps)
    out_specs = [o_spec] * n_out
    out_shape = [jax.ShapeDtypeStruct(o_shape, dt) for dt in out_dtypes]
    for dt, w in row_outs:
        out_specs.append(pl.BlockSpec((tm, w), lambda i, j, kk: (i, 0)))
        out_shape.append(jax.ShapeDtypeStruct((m, w), dt))
    out_specs += [pl.BlockSpec((1, tn), lambda i, j, kk: (0, j))] * vec_outs
    out_shape += [jax.ShapeDtypeStruct((1, n), F32)] * vec_outs
    if t_out:
        assert out_split > 1
        out_specs.append(pl.BlockSpec((None, tn, tm), lambda i, j, kk: (j // nco, j % nco, i)))
        out_shape.append(jax.ShapeDtypeStruct((out_split, n // out_split, m), out_dtypes[0]))
    out = pl.pallas_call(
        body, name=name, grid=(gm, gn, gk), in_specs=in_specs, out_specs=out_specs, out_shape=out_shape,
        scratch_shapes=[pltpu.VMEM((tm, tn), F32)] if gk > 1 else [],
        input_output_aliases=io_alias,
        compiler_params=_params(("arbitrary" if vec_outs else "parallel", "parallel", "arbitrary")),
    )(*args)
    return out[0] if len(out) == 1 else out


def _rope_tables(seq, dil):
    pos = jnp.arange(seq, dtype=jnp.int32).reshape(seq // dil, dil).T.reshape(seq)
    half = ATTN_HEAD_DIM // 2
    inv = ROPE_THETA ** (-jnp.arange(half, dtype=F32) * (2.0 / ATTN_HEAD_DIM))
    ang = pos.astype(F32)[:, None] * inv[None, :]
    cos, sin = jnp.cos(ang), jnp.sin(ang)
    reps = LANES // ATTN_HEAD_DIM
    return (jnp.tile(jnp.concatenate([cos, cos], axis=1), (1, reps)),
            jnp.tile(jnp.concatenate([-sin, sin], axis=1), (1, reps)))


def _rotate(x, c, ss, sign=1.0):
    w = x.shape[-1]
    half = ATTN_HEAD_DIM // 2
    lane = lax.broadcasted_iota(jnp.int32, x.shape, 1)
    first = (lane % ATTN_HEAD_DIM) < half
    partner = jnp.where(first, pltpu.roll(x, w - half, 1), pltpu.roll(x, half, 1))
    reps = w // LANES
    if reps > 1:
        c = jnp.concatenate([c] * reps, axis=1)
        ss = jnp.concatenate([ss] * reps, axis=1)
    return x * c + sign * (partner * ss)


def _rotate_t(xt, ct, sst):
    half = ATTN_HEAD_DIM // 2
    heads = xt.shape[0] // ATTN_HEAD_DIM
    parts = []
    for h in range(heads):
        lo = h * ATTN_HEAD_DIM
        parts += [xt[lo + half:lo + 2 * half], xt[lo:lo + half]]
    return (xt * jnp.concatenate([ct] * heads, axis=0)
            + jnp.concatenate(parts, axis=0) * jnp.concatenate([sst] * heads, axis=0))


def _ln_epi(acc, x, g, b):
    u = ALPHA * x + acc
    mu = jnp.mean(u, axis=-1, keepdims=True)
    uc = u - mu
    var = jnp.mean(uc * uc, axis=-1, keepdims=True)
    rstd = lax.rsqrt(var + LN_EPS)
    xh = uc * rstd
    out = xh * g + b
    return out, out, xh, rstd


def _ln_grad(dy, xh, rstd, g):
    dxh = dy * g
    m1 = jnp.mean(dxh, axis=-1, keepdims=True)
    m2 = jnp.mean(dxh * xh, axis=-1, keepdims=True)
    du = rstd * (dxh - m1 - xh * m2)
    return du, du, jnp.sum(dy * xh, axis=0, keepdims=True), jnp.sum(dy, axis=0, keepdims=True)


def _ln_grad_epi(acc, du_next, xh, rstd, g):
    return _ln_grad(acc + ALPHA * du_next, xh, rstd, g)


def _ln_loss_epi(acc, x_in, target, g, b):
    out, _, xh, rstd = _ln_epi(acc, x_in, g, b)
    e = out - target
    du, _, dg, db = _ln_grad(e * (1.0 / e.shape[-1]), xh, rstd, g)
    return du, du, dg, db, jnp.sum(e * e, axis=0, keepdims=True)


POS_BLK = 2048


def _class_rows(r, dil):
    return pl.ds(r, POS_BLK // dil, stride=dil) if dil > 1 else pl.ds(0, POS_BLK)


def _class_view(a, dil):
    s, w = a.shape
    return a.reshape(dil, s // dil, w)


def _class_spec(dil, all_tiles=True):
    if all_tiles:
        return pl.BlockSpec((dil, POS_BLK // dil, LANES), lambda i, t: (0, i, t))
    return pl.BlockSpec((dil, POS_BLK // dil, LANES), lambda i, t: (0, i, 0))


def _pos_spec(all_tiles=True):
    if all_tiles:
        return pl.BlockSpec((POS_BLK, LANES), lambda i, t: (i, t))
    return pl.BlockSpec((POS_BLK, LANES), lambda i, t: (i, 0))


def _prep_x(x, deps=()):
    s, d = x.shape

    def body(x_ref, *refs):
        outs = refs[len(deps):]
        for dil, o_ref in zip(DILATIONS, outs):
            for r in range(dil):
                o_ref[r] = x_ref[_class_rows(r, dil), :].astype(BF16)

    outs = pl.pallas_call(
        body, name="prep_x", grid=(s // POS_BLK, d // LANES),
        in_specs=[_pos_spec()] + [pl.BlockSpec(memory_space=pl.ANY)] * len(deps),
        out_specs=[_class_spec(dil) for dil in DILATIONS],
        out_shape=[jax.ShapeDtypeStruct((dil, s // dil, d), BF16) for dil in DILATIONS],
        compiler_params=_params(("parallel", "parallel")),
    )(x, *deps)
    return [o.reshape(s, d) for o in outs]


def _head_expand_matrix():
    h = lax.broadcasted_iota(jnp.int32, (LANES, D_MODEL), 0)
    l = lax.broadcasted_iota(jnp.int32, (LANES, D_MODEL), 1)
    return (l // ATTN_HEAD_DIM == h).astype(BF16)


def _attn_fwd(qkv, dil, *, name):
    _, s, d = qkv.shape
    nq = s // ATTN_BLK
    per = nq // dil
    tiles = d // LANES

    def body(q_ref, kc_ref, kp_ref, vc_ref, vp_ref, o_ref, lse_ref):
        qb = pl.program_id(0)
        first = (qb % per) == 0
        qi = lax.broadcasted_iota(jnp.int32, (ATTN_BLK, 2 * ATTN_BLK), 0)
        kj = lax.broadcasted_iota(jnp.int32, (ATTN_BLK, 2 * ATTN_BLK), 1)
        dist = qi + ATTN_BLK - kj
        valid = (dist >= 0) & (dist <= ATTN_BLK) & ((kj >= ATTN_BLK) | jnp.logical_not(first))
        lane = lax.broadcasted_iota(jnp.int32, (ATTN_BLK, LANES), 1)
        lse_tile = jnp.zeros((ATTN_BLK, LANES), F32)
        zero = jnp.zeros((), BF16)
        in_head = [(lane // ATTN_HEAD_DIM) == hh for hh in range(2)]
        for t0 in range(0, tiles, ATTN_FWD_TILE_GROUP):
            group = range(t0, t0 + ATTN_FWD_TILE_GROUP)
            heads = [(t, hh) for t in group for hh in range(2)]
            cols = {t: pl.ds(t * LANES, LANES) for t in group}
            k2 = {t: jnp.concatenate([kp_ref[:, cols[t]], kc_ref[:, cols[t]]], axis=0) for t in group}
            v2 = {t: jnp.concatenate([vp_ref[:, cols[t]], vc_ref[:, cols[t]]], axis=0) for t in group}
            sc = {(t, hh): jnp.where(valid, _dot_nt(jnp.where(in_head[hh], q_ref[:, cols[t]], zero), k2[t]),
                                     -jnp.inf) for t, hh in heads}
            mx = {i: jnp.max(sc[i], axis=-1, keepdims=True) for i in heads}
            p = {i: jnp.exp(sc[i] - mx[i]) for i in heads}
            l = {i: jnp.sum(p[i], axis=-1, keepdims=True) for i in heads}
            oh = {i: _dot(p[i].astype(BF16), v2[i[0]]) / l[i] for i in heads}
            for t in group:
                o_ref[:, cols[t]] = jnp.where(in_head[0], oh[t, 0], oh[t, 1])
                for hh in range(2):
                    lse_tile = jnp.where(lane == 2 * t + hh, mx[t, hh] + jnp.log(l[t, hh]), lse_tile)
        lse_ref[...] = lse_tile

    def blk(piece, prev):
        if prev:
            return pl.BlockSpec((None, ATTN_BLK, d), lambda i: (piece, jnp.maximum(i - 1, 0), 0))
        return pl.BlockSpec((None, ATTN_BLK, d), lambda i: (piece, i, 0))

    return pl.pallas_call(
        body, name=name, grid=(nq,),
        in_specs=[blk(0, False), blk(1, False), blk(1, True), blk(2, False), blk(2, True)],
        out_specs=[pl.BlockSpec((ATTN_BLK, d), lambda i: (i, 0)), pl.BlockSpec((ATTN_BLK, LANES), lambda i: (i, 0))],
        out_shape=[jax.ShapeDtypeStruct((s, d), F32), jax.ShapeDtypeStruct((s, LANES), F32)],
        compiler_params=_params(("parallel",)),
    )(qkv, qkv, qkv, qkv, qkv)


def _attn_combine(os_, lses):
    s, d = os_[0].shape
    sel = jnp.concatenate([_head_expand_matrix()] * 3, axis=0)

    def body(o0, o1, o2, l0, l1, l2, sel_ref, of_ref, ob_ref, lt_ref, o_pos, l_pos):
        for g, (dil, o_ref, l_ref) in enumerate(zip(DILATIONS, (o0, o1, o2), (l0, l1, l2))):
            for r in range(dil):
                o_pos[g, _class_rows(r, dil), :] = o_ref[r]
                l_pos[g, _class_rows(r, dil), :] = l_ref[r]
        la, lb_, lc = l_pos[0], l_pos[1], l_pos[2]
        mx = jnp.maximum(jnp.maximum(la, lb_), lc)
        es = (jnp.exp(la - mx), jnp.exp(lb_ - mx), jnp.exp(lc - mx))
        z = es[0] + es[1] + es[2]
        lt_ref[...] = mx + jnp.log(z)
        acc = jnp.zeros((POS_BLK, LANES), F32)
        for g in range(3):
            acc += _exact_dot_r(es[g] / z, sel_ref[...]) * o_pos[g]
        of_ref[...] = acc
        ob_ref[...] = acc.astype(BF16)

    return pl.pallas_call(
        body, name="attn_combine", grid=(s // POS_BLK, d // LANES),
        in_specs=[_class_spec(dil) for dil in DILATIONS] + [_class_spec(dil, False) for dil in DILATIONS]
        + [pl.BlockSpec((3 * LANES, LANES), lambda i, t: (0, t))],
        out_specs=[_pos_spec(), _pos_spec(), _pos_spec(False)],
        out_shape=[jax.ShapeDtypeStruct((s, d), F32), jax.ShapeDtypeStruct((s, d), BF16),
                   jax.ShapeDtypeStruct((s, LANES), F32)],
        scratch_shapes=[pltpu.VMEM((3, POS_BLK, LANES), F32), pltpu.VMEM((3, POS_BLK, LANES), F32)],
        compiler_params=_params(("parallel", "arbitrary")),
    )(*[_class_view(o, dil) for o, dil in zip(os_, DILATIONS)],
      *[_class_view(l, dil) for l, dil in zip(lses, DILATIONS)], sel)


ATTN_TILE_GROUP = 2
ATTN_FWD_TILE_GROUP = 4
AUX_PER_TILE = 12


def _aux_placement():
    h = lax.broadcasted_iota(jnp.int32, (6, LANES, LANES), 1)
    l = lax.broadcasted_iota(jnp.int32, (6, LANES, LANES), 2)
    j = lax.broadcasted_iota(jnp.int32, (6, LANES, LANES), 0)
    target = AUX_PER_TILE * (h // 2) + 3 * (h % 2) + jnp.where(j < 3, j, 3 + j)
    return ((l == target) & (h < ATTN_HEADS)).astype(BF16)


def _attn_bwd_prep(do, o, lse):
    s, d = do.shape
    tiles = d // LANES
    sel_t = jnp.tile(_head_expand_matrix().T.reshape(tiles, LANES, LANES), (1, 3, 1))

    def body(do_ref, o_ref, l_ref, sel_ref, place_ref, *refs):
        outs, delta, aux = refs[:9], refs[9], refs[10]
        t = pl.program_id(1)
        part = _exact_dot_r(do_ref[...] * o_ref[...], sel_ref[...])

        @pl.when(t == 0)
        def _():
            delta[...] = part

        @pl.when(t > 0)
        def _():
            delta[...] += part

        for g, dil in enumerate(DILATIONS):
            for r in range(dil):
                blk = do_ref[_class_rows(r, dil), :].astype(BF16)
                outs[g][r] = blk
                outs[3 + g][r] = blk.T

        @pl.when(t == tiles - 1)
        def _():
            pieces = _split3(l_ref[...]) + _split3(delta[...])
            aux[...] = _dot(jnp.concatenate(pieces, axis=1), place_ref[...])
            for g, dil in enumerate(DILATIONS):
                for r in range(dil):
                    outs[6 + g][r] = aux[_class_rows(r, dil), :].astype(BF16)

    outs = pl.pallas_call(
        body, name="attn_bwd_prep", grid=(s // POS_BLK, tiles),
        in_specs=[_pos_spec(), _pos_spec(), _pos_spec(False),
                  pl.BlockSpec((None, 3 * LANES, LANES), lambda i, t: (t, 0, 0)),
                  pl.BlockSpec((6 * LANES, LANES), lambda i, t: (0, 0))],
        out_specs=[_class_spec(dil) for dil in DILATIONS]
        + [pl.BlockSpec((dil, LANES, POS_BLK // dil), lambda i, t: (0, t, i)) for dil in DILATIONS]
        + [_class_spec(dil, False) for dil in DILATIONS],
        out_shape=[jax.ShapeDtypeStruct((dil, s // dil, d), BF16) for dil in DILATIONS]
        + [jax.ShapeDtypeStruct((dil, d, s // dil), BF16) for dil in DILATIONS]
        + [jax.ShapeDtypeStruct((dil, s // dil, LANES), BF16) for dil in DILATIONS],
        scratch_shapes=[pltpu.VMEM((POS_BLK, LANES), F32), pltpu.VMEM((POS_BLK, LANES), F32)],
        compiler_params=_params(("parallel", "arbitrary")),
    )(do, o, lse, sel_t, _aux_placement().reshape(6 * LANES, LANES))
    return ([a.reshape(s, d) for a in outs[0:3]], list(outs[3:6]), [a.reshape(s, LANES) for a in outs[6:9]])


def _attn_bwd(qkv, qkv_t, do, do_t, aux, tables, dil, *, name):
    _, s, d = qkv.shape
    nq = s // ATTN_BLK
    per = nq // dil
    tiles = d // LANES
    half = ATTN_HEAD_DIM

    def body(qd_ref, qo_ref, k_ref, qtd_ref, qto_ref, kt_ref, vt_ref, dod_ref, doo_ref, dotd_ref, doto_ref,
             auxd_ref, auxo_ref, c_ref, ss_ref, out_ref, carry):
        kb = pl.program_id(0)

        @pl.when(kb == 0)
        def _():
            carry[...] = jnp.zeros_like(carry)

        has_next = (kb % per) != (per - 1)
        qi = lax.broadcasted_iota(jnp.int32, (ATTN_BLK, 2 * ATTN_BLK), 0)
        kj = lax.broadcasted_iota(jnp.int32, (ATTN_BLK, 2 * ATTN_BLK), 1) % ATTN_BLK
        valid_d = kj <= qi
        valid_o = (kj >= qi) & has_next
        lane = lax.broadcasted_iota(jnp.int32, (ATTN_BLK, LANES), 1)
        row = lax.broadcasted_iota(jnp.int32, (LANES, ATTN_BLK), 0)
        side = lax.broadcasted_iota(jnp.int32, (LANES, 2 * ATTN_BLK), 0)
        first = lax.broadcasted_iota(jnp.int32, (LANES, 2 * ATTN_BLK), 1) < ATTN_BLK
        c, ss = c_ref[...], ss_ref[...]
        zero = jnp.zeros((), BF16)
        sides = ((qd_ref, qtd_ref, dod_ref, dotd_ref, auxd_ref[...], valid_d),
                 (qo_ref, qto_ref, doo_ref, doto_ref, auxo_ref[...], valid_o))
        both = (0, 1)

        def head_halves(x, index):
            axis = 0 if index is lane else 1
            return jnp.concatenate([jnp.where(index < half, x, zero), jnp.where(index >= half, x, zero)], axis=axis)

        for t0 in range(0, tiles, ATTN_TILE_GROUP):
            group = range(t0, t0 + ATTN_TILE_GROUP)
            cols = {t: pl.ds(t * LANES, LANES) for t in group}
            kk, kk_t, vv_t = {}, {}, {}
            for t in group:
                base = AUX_PER_TILE * t
                hit = lambda lo: ((first & (side >= base + lo) & (side < base + lo + 3))
                                  | (jnp.logical_not(first) & (side >= base + lo + 3) & (side < base + lo + 6)))
                kk[t] = head_halves(k_ref[:, cols[t]], lane)
                kk_t[t] = jnp.concatenate([head_halves(kt_ref[cols[t], :], row),
                                           jnp.where(hit(0), -1.0, 0.0).astype(BF16)], axis=0)
                vv_t[t] = jnp.concatenate([head_halves(vt_ref[cols[t], :], row),
                                           jnp.where(hit(6), -1.0, 0.0).astype(BF16)], axis=0)
            sc = {(t, w): _dot(jnp.concatenate([sides[w][0][:, cols[t]], sides[w][4]], axis=1), kk_t[t])
                  for t in group for w in both}
            dpd = {(t, w): _dot(jnp.concatenate([sides[w][2][:, cols[t]], sides[w][4]], axis=1), vv_t[t])
                   for t in group for w in both}
            p = {i: jnp.where(sides[i[1]][5], jnp.exp(sc[i]), 0.0) for i in sc}
            ds = {i: (p[i] * dpd[i]).astype(BF16) for i in sc}
            pb = {i: p[i].astype(BF16) for i in sc}
            dv_t = {t: sum(_dot(sides[w][3][cols[t], :], pb[t, w]) for w in both) for t in group}
            dk_t = {t: sum(_dot(sides[w][1][cols[t], :], ds[t, w]) for w in both) for t in group}
            dq = {i: _dot(ds[i], kk[i[0]]) * ATTN_SCALE for i in sc}
            for t in group:
                dq_now = carry[:, cols[t]] + dq[t, 0]
                carry[:, cols[t]] = dq[t, 1]
                dk = jnp.where(row < half, dk_t[t][:, :ATTN_BLK], dk_t[t][:, ATTN_BLK:]).T
                dv = jnp.where(row < half, dv_t[t][:, :ATTN_BLK], dv_t[t][:, ATTN_BLK:]).T
                out_ref[0, :, cols[t]] = _rotate(dq_now, c, ss, -1.0).astype(BF16)
                out_ref[1, :, cols[t]] = _rotate(dk, c, ss, -1.0).astype(BF16)
                out_ref[2, :, cols[t]] = dv.astype(BF16)

    def nxt(i):
        return jnp.minimum(i + 1, nq - 1)

    def piece(p, shift):
        if shift:
            return pl.BlockSpec((None, ATTN_BLK, d), lambda i: (p, nxt(i), 0))
        return pl.BlockSpec((None, ATTN_BLK, d), lambda i: (p, i, 0))

    def piece_t(p, shift):
        if shift:
            return pl.BlockSpec((None, d, ATTN_BLK), lambda i: (p, 0, nxt(i)))
        return pl.BlockSpec((None, d, ATTN_BLK), lambda i: (p, 0, i))

    def rows(width, shift):
        if shift:
            return pl.BlockSpec((ATTN_BLK, width), lambda i: (nxt(i), 0))
        return pl.BlockSpec((ATTN_BLK, width), lambda i: (i, 0))

    def do_t_spec(shift):
        if shift:
            return pl.BlockSpec((None, d, ATTN_BLK), lambda i: (nxt(i) // per, 0, nxt(i) % per))
        return pl.BlockSpec((None, d, ATTN_BLK), lambda i: (i // per, 0, i % per))

    return pl.pallas_call(
        body, name=name, grid=(nq,),
        in_specs=[piece(0, False), piece(0, True), piece(1, False),
                  piece_t(0, False), piece_t(0, True), piece_t(1, False), piece_t(2, False),
                  rows(d, False), rows(d, True), do_t_spec(False), do_t_spec(True),
                  rows(LANES, False), rows(LANES, True), rows(LANES, False), rows(LANES, False)],
        out_specs=pl.BlockSpec((3, ATTN_BLK, d), lambda i: (0, i, 0)),
        out_shape=jax.ShapeDtypeStruct((3, s, d), BF16),
        scratch_shapes=[pltpu.VMEM((ATTN_BLK, d), F32)],
        compiler_params=_params(("arbitrary",)),
    )(qkv, qkv, qkv, qkv_t, qkv_t, qkv_t, qkv_t, do, do, do_t, do_t, aux, aux, *tables)


def _dx_combine(du, parts):
    s, d = du.shape

    def body(du_ref, p0, p1, p2, out_ref):
        out_ref[...] = ALPHA * du_ref[...] + p0[0]
        for dil, p_ref in zip(DILATIONS[1:], (p1, p2)):
            for r in range(dil):
                out_ref[_class_rows(r, dil), :] += p_ref[r]

    return pl.pallas_call(
        body, name="dx_combine", grid=(s // POS_BLK, d // LANES),
        in_specs=[_pos_spec()] + [_class_spec(dil) for dil in DILATIONS], out_specs=_pos_spec(),
        out_shape=jax.ShapeDtypeStruct((s, d), F32),
        compiler_params=_params(("parallel", "parallel")),
    )(du, *[_class_view(p, dil) for p, dil in zip(parts, DILATIONS)])


HGRN_ROWS = 1024


def _tri(lower, copies=1):
    i = lax.broadcasted_iota(jnp.int32, (HGRN_CHUNK, HGRN_CHUNK * copies), 0)
    j = lax.broadcasted_iota(jnp.int32, (HGRN_CHUNK, HGRN_CHUNK * copies), 1) % HGRN_CHUNK
    return (j <= i) if lower else (j >= i)


def _hgrn_gates(qr, z, lb):
    sq = jax.nn.sigmoid(qr)
    e = jnp.exp(-jnp.abs(z))
    big = 1.0 / (1.0 + e)
    small = e * big
    sg = jnp.where(z >= 0, big, small)
    sn = jnp.where(z >= 0, small, big)
    return sq, qr * sq, sg, sn, lb + (1.0 - lb) * sg, (1.0 - lb) * sn


def _hgrn_fwd(p2, lb, norm_g):
    _, s, d = p2.shape
    nblk = s // HGRN_ROWS
    cps = HGRN_ROWS // HGRN_CHUNK

    def body(p_ref, lb_ref, g_ref, o_ref, on_ref, st_ref, state):
        @pl.when(pl.program_id(1) == 0)
        def _():
            state[...] = jnp.zeros_like(state)

        chunks = [pl.ds(c * HGRN_CHUNK, HGRN_CHUNK) for c in range(cps)]
        ltri = _tri(True)
        lsel = _tri(True, 3).astype(BF16)
        _, q, _, _, f, key = _hgrn_gates(p_ref[0], p_ref[1], lb_ref[...])
        lf = jnp.log(f)
        v = [p_ref[2, rows, :].astype(BF16) for rows in chunks]
        b = [_exact_dot(lsel, lf[c * HGRN_CHUNK:(c + 1) * HGRN_CHUNK]) for c in range(cps)]
        b_last = [bc[HGRN_CHUNK - 1:HGRN_CHUNK, :] for bc in b]
        q = [q[c * HGRN_CHUNK:(c + 1) * HGRN_CHUNK] for c in range(cps)]
        key = [key[c * HGRN_CHUNK:(c + 1) * HGRN_CHUNK] for c in range(cps)]
        qd = [(q[c] * jnp.exp(b[c])).astype(BF16) for c in range(cps)]
        kd = [(key[c] * jnp.exp(-b[c])).astype(BF16) for c in range(cps)]
        k2 = [(key[c] * jnp.exp(b_last[c] - b[c])).astype(BF16) for c in range(cps)]
        a = [jnp.where(ltri, _dot_nt(qd[c], kd[c]), 0.0).astype(BF16) for c in range(cps)]
        kv = [_dot_tn(v[c], k2[c]) for c in range(cps)]
        st, sts = state[...], []
        for c in range(cps):
            sts.append(st)
            st_ref[0, c] = st
            st = st * jnp.exp(b_last[c]) + kv[c]
        state[...] = st
        gv = g_ref[...]
        for c in range(cps):
            o = _dot(a[c], v[c]) + _dot_nt(qd[c], sts[c].astype(BF16))
            o_ref[chunks[c], :] = o
            r = lax.rsqrt(jnp.mean(o * o, axis=-1, keepdims=True) + RMS_EPS)
            on_ref[chunks[c], :] = (o * r * gv).astype(BF16)

    vec = pl.BlockSpec((1, LANES), lambda h, c: (0, h))
    col = pl.BlockSpec((HGRN_ROWS, LANES), lambda h, c: (c, h))
    return pl.pallas_call(
        body, name="hgrn_fwd", grid=(HGRN_HEADS, nblk),
        in_specs=[pl.BlockSpec((3, HGRN_ROWS, LANES), lambda h, c: (0, c, h)), vec, vec],
        out_specs=[col, col, pl.BlockSpec((1, cps, LANES, LANES), lambda h, c: (h, c, 0, 0))],
        out_shape=[jax.ShapeDtypeStruct((s, d), F32), jax.ShapeDtypeStruct((s, d), BF16),
                   jax.ShapeDtypeStruct((HGRN_HEADS, s // HGRN_CHUNK, LANES, LANES), F32)],
        scratch_shapes=[pltpu.VMEM((LANES, LANES), F32)],
        compiler_params=_params(("parallel", "arbitrary")),
    )(p2, lb, norm_g)


def _hgrn_bwd(p2, lb, norm_g, o_raw, states, dyn):
    _, s, d = p2.shape
    nblk = s // HGRN_ROWS
    cps = HGRN_ROWS // HGRN_CHUNK

    def body(p_ref, lb_ref, g_ref, o_ref, st_ref, dy_ref, dp_ref, dg_ref, dlb_ref, dstate):
        @pl.when(pl.program_id(1) == 0)
        def _():
            dstate[...] = jnp.zeros_like(dstate)
            dg_ref[...] = jnp.zeros_like(dg_ref)
            dlb_ref[...] = jnp.zeros_like(dlb_ref)

        n = cps
        cut = lambda t: [t[c * HGRN_CHUNK:(c + 1) * HGRN_CHUNK] for c in range(n)]
        chunks = [pl.ds(c * HGRN_CHUNK, HGRN_CHUNK) for c in range(n)]
        lbv = lb_ref[...]
        ltri = _tri(True)
        lsel = _tri(True, 3).astype(BF16)
        usel = _tri(False, 3).astype(BF16)
        last_row = lax.broadcasted_iota(jnp.int32, (HGRN_CHUNK, LANES), 0) == HGRN_CHUNK - 1
        qr, z = p_ref[0], p_ref[1]
        sq, q, sg, sn, f, key = _hgrn_gates(qr, z, lbv)
        lf = jnp.log(f)
        v = [p_ref[2, rows, :].astype(BF16) for rows in chunks]
        o, dyv = o_ref[...], dy_ref[...]
        r = lax.rsqrt(jnp.mean(o * o, axis=-1, keepdims=True) + RMS_EPS)
        oh = o * r
        doh = dyv * g_ref[...]
        do = cut((r * (doh - oh * jnp.mean(doh * oh, axis=-1, keepdims=True))).astype(BF16))
        dg_ref[...] += jnp.sum(dyv * oh, axis=0, keepdims=True)
        b = [_exact_dot(lsel, lfc) for lfc in cut(lf)]
        b_last = [bc[HGRN_CHUNK - 1:HGRN_CHUNK, :] for bc in b]
        q, key = cut(q), cut(key)
        eb = [jnp.exp(bc) for bc in b]
        enb = [jnp.exp(-bc) for bc in b]
        e2 = [jnp.exp(b_last[c] - b[c]) for c in range(n)]
        dec = [jnp.exp(bl) for bl in b_last]
        qd_f = [q[c] * eb[c] for c in range(n)]
        kd_f = [key[c] * enb[c] for c in range(n)]
        k2_f = [key[c] * e2[c] for c in range(n)]
        qd, kd, k2 = ([t.astype(BF16) for t in ts] for ts in (qd_f, kd_f, k2_f))
        a = [jnp.where(ltri, _dot_nt(qd[c], kd[c]), 0.0).astype(BF16) for c in range(n)]
        da = [jnp.where(ltri, _dot_nt(do[c], v[c]), 0.0).astype(BF16) for c in range(n)]
        grow = [_dot_tn(do[c], qd[c]) for c in range(n)]
        dst, dsts = dstate[...], [None] * n
        for c in reversed(range(n)):
            dsts[c] = dst
            dst = dst * dec[c] + grow[c]
        dstate[...] = dst
        st = [st_ref[0, c] for c in range(n)]
        dstb = [t.astype(BF16) for t in dsts]
        dv = [_dot_tn(a[c], do[c]) + _dot_nt(k2[c], dstb[c]) for c in range(n)]
        dqd = [_dot(da[c], kd[c]) + _dot(do[c], st[c].astype(BF16)) for c in range(n)]
        dkd = [_dot_tn(da[c], qd[c]) for c in range(n)]
        dk2 = [_dot(v[c], dstb[c]) for c in range(n)]
        db = []
        for c in range(n):
            ddec = jnp.sum(dsts[c] * st[c], axis=0, keepdims=True)
            db_last = jnp.sum(dk2[c] * k2_f[c], axis=0, keepdims=True) + ddec * dec[c]
            db.append(dqd[c] * qd_f[c] - dkd[c] * kd_f[c] - dk2[c] * k2_f[c] + jnp.where(last_row, db_last, 0.0))
        dlf = [_exact_dot(usel, dbc) for dbc in db]
        f, sg, sn, sq, qr = cut(f), cut(sg), cut(sn), cut(sq), cut(qr)
        dlb_acc = jnp.zeros((1, LANES), F32)
        for c in range(n):
            dkey = dkd[c] * enb[c] + dk2[c] * e2[c]
            common = dlf[c] / f[c] - dkey
            dlb_acc += jnp.sum(common * sn[c], axis=0, keepdims=True)
            dp_ref[0, chunks[c], :] = (dqd[c] * eb[c] * (sq[c] * (1.0 + qr[c] * (1.0 - sq[c])))).astype(BF16)
            dp_ref[1, chunks[c], :] = (common * ((1.0 - lbv) * sg[c] * sn[c])).astype(BF16)
            dp_ref[2, chunks[c], :] = dv[c].astype(BF16)
        dlb_ref[...] += dlb_acc

    def rev(c):
        return nblk - 1 - c

    vec = pl.BlockSpec((1, LANES), lambda h, c: (0, h))
    col = pl.BlockSpec((HGRN_ROWS, LANES), lambda h, c: (rev(c), h))
    p3 = pl.BlockSpec((3, HGRN_ROWS, LANES), lambda h, c: (0, rev(c), h))
    return pl.pallas_call(
        body, name="hgrn_bwd", grid=(HGRN_HEADS, nblk),
        in_specs=[p3, vec, vec, col, pl.BlockSpec((1, cps, LANES, LANES), lambda h, c: (h, rev(c), 0, 0)), col],
        out_specs=[p3, vec, vec],
        out_shape=[jax.ShapeDtypeStruct((3, s, d), BF16), jax.ShapeDtypeStruct((1, d), F32),
                   jax.ShapeDtypeStruct((1, d), F32)],
        scratch_shapes=[pltpu.VMEM((LANES, LANES), F32)],
        compiler_params=_params(("parallel", "arbitrary")),
    )(p2, lb, norm_g, o_raw, states, dyn)


SMALL_ROWS = 16
FUSED_TM = 512


def _relu2(h):
    r = jnp.maximum(h.astype(F32), 0.0)
    return r * r


def _dact_epi(acc, h):
    return (acc * (2.0 * jnp.maximum(h.astype(F32), 0.0)),)


def _rope_epi(j, acc, ct, sst):
    acc_t = acc.T
    rot_t = _rotate_t(acc_t, ct, sst) * jnp.where(j == 0, ATTN_SCALE, 1.0)
    out_t = jnp.where(j >= 2, acc_t, rot_t).astype(BF16)
    return out_t.T, out_t


def _local_step(x, target, get_w, small, on_grads, deps=()):
    s, d = x.shape
    lb = small["lb"]
    gq = 3 * d
    gt = gq // 1024

    def ln_of(a, w, x_in, which, layer, *, name, k, a_pre=None, deps=()):
        g, b = small[f"ln_{which}_g"][layer:layer + 1], small[f"ln_{which}_b"][layer:layer + 1]
        return _mm(a, w, "nn", name=name, m=s, n=d, k=k, tm=FUSED_TM, tk=k, a_pre=a_pre, out_dtypes=(F32, BF16, F32),
                   row_outs=((F32, 1),), epi=_ln_epi, tile_extras=(x_in,), vec_extras=(g, b), deps=deps)

    def ln_grad_of(a, w, du_next, xh, rstd, which, layer, *, name, k, a_split=1, deps=()):
        return _mm(a, w, "nt", name=name, m=s, n=d, k=k, tm=FUSED_TM, tk=k, a_split=a_split,
                   out_dtypes=(F32, BF16), vec_outs=2, epi=_ln_grad_epi, tile_extras=(du_next, xh), row_extras=(rstd,),
                   vec_extras=(small[f"ln_{which}_g"][layer:layer + 1],), deps=deps)

    def ffn_fwd(xb, x_in, w_up, w_down, tag):
        h = _mm(xb, w_up, "nn", name=f"ffn_up_{tag}", m=s, n=D_FF, k=d, out_dtypes=(BF16,))
        return (h,) + tuple(ln_of(h, w_down, x_in, "ffn", tag, name=f"ffn_down_{tag}", k=D_FF, a_pre=_relu2))

    def ffn_bwd(du, dub, xb, h, w_up, w_down, xh, rstd, tag):
        dh = _mm(dub, w_down, "nt", name=f"ffn_dact_{tag}", m=s, n=D_FF, k=d, out_dtypes=(BF16,), epi=_dact_epi,
                 tile_extras=(h,))
        g_down = _mm(h, dub, "tn", name=f"ffn_gdown_{tag}", m=D_FF, n=d, k=s, a_pre=_relu2, out_dtypes=(BF16,))
        g_up = _mm(xb, dh, "tn", name=f"ffn_gup_{tag}", m=d, n=D_FF, k=s, out_dtypes=(BF16,))
        after = on_grads(f"ffn{tag}", {f"ffn_w_down{tag}": g_down, f"ffn_w_up{tag}": g_up})
        return ln_grad_of(dh, w_up, du, xh, rstd, "mix", tag, name=f"ffn_dx_{tag}", k=D_FF, deps=after)

    xs = _prep_x(x, deps)
    w_ain, after = get_w("attn_w_in", None)
    tabs, qkvs, qkv_ts, o_parts, lse_parts = [], [], [], [], []
    for g, dil in enumerate(DILATIONS):
        tabs.append(_rope_tables(s, dil))
        tabs_t = [t[:, :ATTN_HEAD_DIM].T for t in tabs[g]]
        qkv, qkv_t = _mm(xs[g], w_ain, "nn", name=f"attn_in_{g}", m=s, n=gq, k=d, b_col_off=g * gt, out_split=3,
                         out_dtypes=(BF16,), epi=_rope_epi, epi_wants_j=True, row_extras=tabs_t, t_out=True,
                         deps=after if g == 0 else ())
        qkvs.append(qkv)
        qkv_ts.append(qkv_t)
        o_g, lse_g = _attn_fwd(qkv, dil, name=f"attn_fwd_{g}")
        o_parts.append(o_g)
        lse_parts.append(lse_g)
    o_f, o_b, lse_t = _attn_combine(o_parts, lse_parts)
    w_aout, after = get_w("attn_w_out", o_b)
    x1, x1b, xh1, r1 = ln_of(o_b, w_aout, x, "mix", 0, name="attn_out", k=d, deps=after)
    w_up0, w_down0 = get_w("ffn_w_up0", o_b)[0], get_w("ffn_w_down0", o_b)[0]
    h0, x2, x2b, xh2, r2 = ffn_fwd(x1b, x1, w_up0, w_down0, 0)
    w_hin, w_hout, norm_g = get_w("hgrn_w_in", x2b)[0], get_w("hgrn_w_out", x2b)[0], get_w("hgrn_norm_g", x2b)[0]
    p2 = _mm(x2b, w_hin, "nn", name="hgrn_in", m=s, n=3 * d, k=d, out_split=3)
    o_raw, o_n, states = _hgrn_fwd(p2, lb, norm_g)
    w_up1, w_down1 = get_w("ffn_w_up1", o_n)[0], get_w("ffn_w_down1", o_n)[0]
    x3, x3b, xh3, r3 = ln_of(o_n, w_hout, x2, "mix", 1, name="hgrn_out", k=d)
    h1 = _mm(x3b, w_up1, "nn", name="ffn_up_1", m=s, n=D_FF, k=d, out_dtypes=(BF16,))
    du, dub, dg_ffn1, db_ffn1, sq = _mm(
        h1, w_down1, "nn", name="ffn_down_1", m=s, n=d, k=D_FF, tm=FUSED_TM, tk=D_FF, a_pre=_relu2,
        out_dtypes=(F32, BF16), vec_outs=3, epi=_ln_loss_epi, tile_extras=(x3, target),
        vec_extras=(small["ln_ffn_g"][1:2], small["ln_ffn_b"][1:2]))
    du, dub, dg_mix1, db_mix1 = ffn_bwd(du, dub, x3b, h1, w_up1, w_down1, xh3, r3, 1)
    dyn = _mm(dub, w_hout, "nt", name="hgrn_dout", m=s, n=d, k=d)
    g_hout = _mm(o_n, dub, "tn", name="hgrn_gout", m=d, n=d, k=s, out_dtypes=(BF16,))
    dp2, d_norm_g, d_lb = _hgrn_bwd(p2, lb, norm_g, o_raw, states, dyn)
    g_hin = _mm(x2b, dp2, "tn", name="hgrn_gin", m=d, n=3 * d, k=s, b_split=3, out_dtypes=(BF16,))
    after = on_grads("hgrn", {"hgrn_w_out": g_hout, "hgrn_w_in": g_hin})
    du, dub, dg_ffn0, db_ffn0 = ln_grad_of(dp2, w_hin, du, xh2, r2, "ffn", 0, name="hgrn_dx", k=3 * d, a_split=3,
                                           deps=after)
    du, dub, dg_mix0, db_mix0 = ffn_bwd(du, dub, x1b, h0, w_up0, w_down0, xh1, r1, 0)
    g_aout = _mm(o_b, dub, "tn", name="attn_gout", m=d, n=d, k=s, out_dtypes=(BF16,))
    sm1 = jax.nn.softmax(small["lb_logits"], axis=0)
    d_l1 = d_lb * (sm1[0:1] * sm1[1:2])
    zeros = jnp.zeros((SMALL_ROWS - 12, d), F32)
    loss_row = jnp.broadcast_to(0.5 * jnp.sum(sq) / d, (1, d))
    small_grads = jnp.concatenate([dg_mix0, dg_mix1, db_mix0, db_mix1, dg_ffn0, dg_ffn1, db_ffn0, db_ffn1,
                                   -d_l1, d_l1, d_norm_g, loss_row, zeros], axis=0)
    after = on_grads("attn_out", {"attn_w_out": g_aout, "small": small_grads})
    do = _mm(dub, w_aout, "nt", name="attn_dout", m=s, n=d, k=d, deps=after)
    do_parts, do_ts, aux_parts = _attn_bwd_prep(do, o_f, lse_t)
    g_ain, dqkvs = None, []
    for g, dil in enumerate(DILATIONS):
        dqkvs.append(_attn_bwd(qkvs[g], qkv_ts[g], do_parts[g], do_ts[g], aux_parts[g], tabs[g], dil,
                               name=f"attn_bwd_{g}"))
        g_ain = _mm(xs[g], dqkvs[g], "tn", name=f"attn_gin_{g}", m=d, n=gq, k=s, b_split=3, out_dtypes=(BF16,),
                    out_col_off=g * gt, out_cols=3 * gq, alias=g_ain)
    after = on_grads("attn_in", {"attn_w_in": g_ain})
    dx_parts = [_mm(dqkvs[g], w_ain, "nt", name=f"attn_dx_{g}", m=s, n=d, k=gq, tm=FUSED_TM, tk=gq, a_split=3, b_k_off=g,
                    deps=after if g == 0 else ())
                for g in range(len(DILATIONS))]
    return _dx_combine(du, dx_parts)


def _mesh_place():
    x, y, c = lax.axis_index("x"), lax.axis_index("y"), lax.axis_index("c")
    return x, y, c, 4 * x + 2 * y + c


def _peer(x, y, c, k):
    px = 1 - x if (k >> 2) & 1 else x
    py = 1 - y if (k >> 1) & 1 else y
    pc = 1 - c if k & 1 else c
    return (px, py, pc), 4 * px + 2 * py + pc


def _window(ref, axis, size, idx):
    if axis is None:
        return ref
    sl = [slice(None)] * len(ref.shape)
    sl[axis] = pl.ds(idx * size, size)
    return ref.at[tuple(sl)]


_HBM = pl.BlockSpec(memory_space=pltpu.HBM)
_SEM = pl.BlockSpec(memory_space=pltpu.SEMAPHORE)
_EFFECT = pltpu.SideEffectType.DATAFLOW_SIDE_EFFECTING


def _xchg_ends(src_ref, land_ref, axis, gather, me, other):
    if gather:
        size = src_ref.shape[axis]
        return src_ref, _window(land_ref, axis, size, me), _window(land_ref, axis, size, other)
    size = None if axis is None else src_ref.shape[axis] // N_DEV
    return _window(src_ref, axis, size, other), land_ref.at[me], land_ref.at[other]


def _xchg_start(srcs, lands, axes, *, gather, name, deps=(), peers=tuple(range(1, N_DEV))):
    n = len(srcs)
    nd = len(deps)

    def body(*refs):
        src_refs, land_refs = refs[:n], refs[n:2 * n]
        send, recv = refs[2 * n + nd:3 * n + nd], refs[3 * n + nd:4 * n + nd]
        token = refs[-1]
        x, y, c, me = _mesh_place()
        for k in peers:
            peer, pidx = _peer(x, y, c, k)
            for i in range(n):
                src, dst, _ = _xchg_ends(src_refs[i], land_refs[i], axes[i], gather, me, pidx)
                pltpu.make_async_remote_copy(
                    src_ref=src, dst_ref=dst, send_sem=send[i].at[k - 1], recv_sem=recv[i].at[k - 1],
                    device_id=peer, device_id_type=pl.DeviceIdType.MESH).start()
        for i in range(n):
            src, dst, _ = _xchg_ends(src_refs[i], land_refs[i], axes[i], gather, me, me)
            pltpu.make_async_copy(src, dst, send[i].at[N_DEV - 1]).start()
        token[...] = jnp.zeros_like(token)

    bufs = list(srcs) + list(lands)
    outs = pl.pallas_call(
        body, name=name,
        out_shape=[pltpu.SemaphoreType.DMA((N_DEV,))] * (2 * n) + [pltpu.HBM(b.shape, b.dtype) for b in bufs]
        + [jax.ShapeDtypeStruct((8, LANES), F32)],
        in_specs=[_HBM] * (2 * n) + [pl.BlockSpec(memory_space=pl.ANY)] * nd,
        out_specs=[_SEM] * (2 * n) + [_HBM] * (2 * n) + [pl.BlockSpec(memory_space=pltpu.VMEM)],
        input_output_aliases={i: 2 * n + i for i in range(2 * n)},
        compiler_params=pltpu.CompilerParams(has_side_effects=_EFFECT),
    )(*[pltpu.with_memory_space_constraint(b, pltpu.HBM) for b in bufs], *deps)
    return dict(send=outs[:n], recv=outs[n:2 * n], srcs=outs[2 * n:3 * n], lands=outs[3 * n:4 * n], token=outs[-1],
                axes=list(axes), gather=gather)


SIBLING = 1
SAME_CORE = (2, 4, 6)


def _gather_relay(xc, after, *, name):
    axis = xc["axes"][0]

    def body(src_ref, land_ref, send1, recv1, after_ref, src_out, land_out, send2, recv2):
        x, y, c, me = _mesh_place()
        sibling, _ = _peer(x, y, c, SIBLING)
        for j, k in enumerate(SAME_CORE):
            peer, pidx = _peer(x, y, c, k)
            src, _, got = _xchg_ends(src_ref, land_ref, axis, True, me, pidx)
            pltpu.make_async_remote_copy(
                src_ref=src, dst_ref=got, send_sem=send1.at[k - 1], recv_sem=recv1.at[k - 1],
                device_id=peer, device_id_type=pl.DeviceIdType.MESH).wait_recv()
            pltpu.make_async_remote_copy(
                src_ref=got, dst_ref=got, send_sem=send2.at[j], recv_sem=recv2.at[j],
                device_id=sibling, device_id_type=pl.DeviceIdType.MESH).start()

    src, land = xc["srcs"][0], xc["lands"][0]
    outs = pl.pallas_call(
        body, name=name,
        out_shape=[pltpu.HBM(src.shape, src.dtype), pltpu.HBM(land.shape, land.dtype)]
        + [pltpu.SemaphoreType.DMA((len(SAME_CORE),))] * 2,
        in_specs=[_HBM, _HBM, _SEM, _SEM, pl.BlockSpec(memory_space=pl.ANY)],
        out_specs=[_HBM, _HBM, _SEM, _SEM], input_output_aliases={0: 0, 1: 1},
        compiler_params=pltpu.CompilerParams(has_side_effects=_EFFECT),
    )(src, land, xc["send"][0], xc["recv"][0], after)
    return dict(src=outs[0], land=outs[1], send=outs[2], recv=outs[3])


def _gather_relay_wait(xc, relay, after, *, name):
    axis = xc["axes"][0]

    def body(src_ref, land_ref, send1, recv1, send2, recv2, after_ref, src_out, land_out):
        x, y, c, me = _mesh_place()
        sibling, sidx = _peer(x, y, c, SIBLING)
        for k in (SIBLING,) + SAME_CORE:
            peer, pidx = _peer(x, y, c, k)
            src, dst, got = _xchg_ends(src_ref, land_ref, axis, True, me, pidx)
            pltpu.make_async_remote_copy(
                src_ref=src, dst_ref=dst, send_sem=send1.at[k - 1], recv_sem=recv1.at[k - 1],
                device_id=peer, device_id_type=pl.DeviceIdType.MESH).wait_send()
        src, dst, got = _xchg_ends(src_ref, land_ref, axis, True, me, sidx)
        pltpu.make_async_remote_copy(
            src_ref=src, dst_ref=got, send_sem=send1.at[SIBLING - 1], recv_sem=recv1.at[SIBLING - 1],
            device_id=sibling, device_id_type=pl.DeviceIdType.MESH).wait_recv()
        src, dst, _ = _xchg_ends(src_ref, land_ref, axis, True, me, me)
        pltpu.make_async_copy(src, dst, send1.at[N_DEV - 1]).wait()
        for j, k in enumerate(SAME_CORE):
            _, pidx = _peer(x, y, c, k)
            _, qidx = _peer(x, y, c, k ^ SIBLING)
            _, _, sent = _xchg_ends(src_ref, land_ref, axis, True, me, pidx)
            _, _, got = _xchg_ends(src_ref, land_ref, axis, True, me, qidx)
            pltpu.make_async_remote_copy(
                src_ref=sent, dst_ref=sent, send_sem=send2.at[j], recv_sem=recv2.at[j],
                device_id=sibling, device_id_type=pl.DeviceIdType.MESH).wait_send()
            pltpu.make_async_remote_copy(
                src_ref=got, dst_ref=got, send_sem=send2.at[j], recv_sem=recv2.at[j],
                device_id=sibling, device_id_type=pl.DeviceIdType.MESH).wait_recv()

    outs = pl.pallas_call(
        body, name=name, out_shape=[pltpu.HBM(relay["src"].shape, relay["src"].dtype),
                                    pltpu.HBM(relay["land"].shape, relay["land"].dtype)],
        in_specs=[_HBM, _HBM, _SEM, _SEM, _SEM, _SEM, pl.BlockSpec(memory_space=pl.ANY)],
        out_specs=[_HBM, _HBM], input_output_aliases={0: 0, 1: 1},
        compiler_params=pltpu.CompilerParams(has_side_effects=_EFFECT),
    )(relay["src"], relay["land"], xc["send"][0], xc["recv"][0], relay["send"], relay["recv"], after)
    return outs[1]


def _xchg_wait(xc, items, after, *, name):
    m = len(items)
    gather = xc["gather"]
    axes = [xc["axes"][i] for i in items]

    def body(*refs):
        src_refs, land_refs = refs[:m], refs[m:2 * m]
        send, recv = refs[2 * m:3 * m], refs[3 * m:4 * m]
        x, y, c, me = _mesh_place()
        for k in range(1, N_DEV):
            peer, pidx = _peer(x, y, c, k)
            for j in range(m):
                src, dst, got = _xchg_ends(src_refs[j], land_refs[j], axes[j], gather, me, pidx)
                pltpu.make_async_remote_copy(
                    src_ref=src, dst_ref=dst, send_sem=send[j].at[k - 1], recv_sem=recv[j].at[k - 1],
                    device_id=peer, device_id_type=pl.DeviceIdType.MESH).wait_send()
                pltpu.make_async_remote_copy(
                    src_ref=src, dst_ref=got, send_sem=send[j].at[k - 1], recv_sem=recv[j].at[k - 1],
                    device_id=peer, device_id_type=pl.DeviceIdType.MESH).wait_recv()
        for j in range(m):
            src, dst, _ = _xchg_ends(src_refs[j], land_refs[j], axes[j], gather, me, me)
            pltpu.make_async_copy(src, dst, send[j].at[N_DEV - 1]).wait()

    bufs = [xc["srcs"][i] for i in items] + [xc["lands"][i] for i in items]
    sems = [xc["send"][i] for i in items] + [xc["recv"][i] for i in items]
    outs = pl.pallas_call(
        body, name=name, out_shape=[pltpu.HBM(b.shape, b.dtype) for b in bufs],
        in_specs=[_HBM] * (2 * m) + [_SEM] * (2 * m) + [pl.BlockSpec(memory_space=pl.ANY)],
        out_specs=[_HBM] * (2 * m), input_output_aliases={j: j for j in range(2 * m)},
        compiler_params=pltpu.CompilerParams(has_side_effects=_EFFECT),
    )(*bufs, *sems, after)
    return outs[m:]


def _cast_bf16(a, *, name):
    r, c = a.shape
    tr = min(r, 512)

    def body(a_ref, o_ref):
        o_ref[...] = a_ref[...].astype(BF16)

    spec = pl.BlockSpec((tr, c), lambda i: (i, 0))
    return pl.pallas_call(body, name=name, grid=(r // tr,), in_specs=[spec], out_specs=spec,
                          out_shape=jax.ShapeDtypeStruct((r, c), BF16), compiler_params=_params(("parallel",)))(a)


def _adamw(slabs, w, m, v, *, name):
    layers, r, c = w.shape
    tr = min(r, 256)

    def body(*refs):
        s_refs = refs[:layers]
        w_ref, m_ref, v_ref, g_ref, d_ref, mo_ref, vo_ref = refs[layers:]
        for l in range(layers):
            g = s_refs[l][0].astype(F32)
            for i in range(1, N_DEV):
                g = g + s_refs[l][i].astype(F32)
            m2 = ADAM_B1 * m_ref[l] + (1.0 - ADAM_B1) * g
            v2 = ADAM_B2 * v_ref[l] + (1.0 - ADAM_B2) * (g * g)
            m_hat = m2 / (1.0 - ADAM_B1 ** ADAM_STEP)
            v_hat = v2 / (1.0 - ADAM_B2 ** ADAM_STEP)
            g_ref[l] = g
            d_ref[l] = -ADAM_LR * (m_hat / (jnp.sqrt(v_hat) + ADAM_EPS) + ADAM_WD * w_ref[l])
            mo_ref[l] = m2
            vo_ref[l] = v2

    spec = pl.BlockSpec((layers, tr, c), lambda i: (0, i, 0))
    return pl.pallas_call(
        body, name=name, grid=(r // tr,),
        in_specs=[pl.BlockSpec((N_DEV, tr, c), lambda i: (0, i, 0))] * layers + [spec, spec, spec],
        out_specs=[spec] * 4, out_shape=[jax.ShapeDtypeStruct((layers, r, c), F32)] * 4,
        compiler_params=_params(("parallel",)),
    )(*slabs, w, m, v)


GATHER_AXIS = {"attn_w_in": 1, "attn_w_out": 0, "ffn_w_up0": 1, "ffn_w_down0": 0, "hgrn_w_in": 1, "hgrn_w_out": 0,
               "hgrn_norm_g": 1, "ffn_w_up1": 1, "ffn_w_down1": 0}
GATHER_STAGES = (("attn_w_in",), ("attn_w_out", "ffn_w_up0", "ffn_w_down0", "hgrn_w_in", "hgrn_w_out", "hgrn_norm_g"),
                 ("ffn_w_up1", "ffn_w_down1"))
GATHER_WAITS = ((("attn_w_in",), 0, 1), (("attn_w_out", "ffn_w_up0", "ffn_w_down0"), 1, 2),
                (("hgrn_w_in", "hgrn_w_out", "hgrn_norm_g"), 1, None), (("ffn_w_up1", "ffn_w_down1"), 2, None))
SCATTER_AXIS = dict(GATHER_AXIS, small=None)
BIG = ("attn_w_in", "attn_w_out", "hgrn_w_in", "hgrn_w_out", "ffn_w_up", "ffn_w_down")
SMALL = ("lb_logits", "ln_mix_g", "ln_mix_b", "ln_ffn_g", "ln_ffn_b")
SMALL_ROW = {"ln_mix_g": 0, "ln_mix_b": 2, "ln_ffn_g": 4, "ln_ffn_b": 6, "lb_logits": 8}
NORM_G_ROW = 10
LOSS_ROW = 11


def kernel(x, attn_w_in, attn_w_out, hgrn_w_in, hgrn_w_out, hgrn_norm_g, lb_logits, ln_mix_g, ln_mix_b, ln_ffn_g, ln_ffn_b, ffn_w_up, ffn_w_down, loss_target, m_attn_w_in, m_attn_w_out, m_hgrn_w_in, m_hgrn_w_out, m_hgrn_norm_g, m_lb_logits, m_ln_mix_g, m_ln_mix_b, m_ln_ffn_g, m_ln_ffn_b, m_ffn_w_up, m_ffn_w_down, v_attn_w_in, v_attn_w_out, v_hgrn_w_in, v_hgrn_w_out, v_hgrn_norm_g, v_lb_logits, v_ln_mix_g, v_ln_mix_b, v_ln_ffn_g, v_ln_ffn_b, v_ffn_w_up, v_ffn_w_down):
    wts = dict(attn_w_in=attn_w_in, attn_w_out=attn_w_out, hgrn_w_in=hgrn_w_in, hgrn_w_out=hgrn_w_out,
               hgrn_norm_g=hgrn_norm_g, lb_logits=lb_logits, ln_mix_g=ln_mix_g, ln_mix_b=ln_mix_b, ln_ffn_g=ln_ffn_g,
               ln_ffn_b=ln_ffn_b, ffn_w_up=ffn_w_up, ffn_w_down=ffn_w_down)
    mom = dict(attn_w_in=m_attn_w_in, attn_w_out=m_attn_w_out, hgrn_w_in=m_hgrn_w_in, hgrn_w_out=m_hgrn_w_out,
               hgrn_norm_g=m_hgrn_norm_g, lb_logits=m_lb_logits, ln_mix_g=m_ln_mix_g, ln_mix_b=m_ln_mix_b,
               ln_ffn_g=m_ln_ffn_g, ln_ffn_b=m_ln_ffn_b, ffn_w_up=m_ffn_w_up, ffn_w_down=m_ffn_w_down)
    vel = dict(attn_w_in=v_attn_w_in, attn_w_out=v_attn_w_out, hgrn_w_in=v_hgrn_w_in, hgrn_w_out=v_hgrn_w_out,
               hgrn_norm_g=v_hgrn_norm_g, lb_logits=v_lb_logits, ln_mix_g=v_ln_mix_g, ln_mix_b=v_ln_mix_b,
               ln_ffn_g=v_ln_ffn_g, ln_ffn_b=v_ln_ffn_b, ffn_w_up=v_ffn_w_up, ffn_w_down=v_ffn_w_down)
    me = 4 * lax.axis_index("x") + 2 * lax.axis_index("y") + lax.axis_index("c")

    src = {"attn_w_in": attn_w_in[0], "attn_w_out": attn_w_out[0], "hgrn_w_in": hgrn_w_in[0], "hgrn_w_out": hgrn_w_out[0],
           "ffn_w_up0": ffn_w_up[0], "ffn_w_down0": ffn_w_down[0], "ffn_w_up1": ffn_w_up[1], "ffn_w_down1": ffn_w_down[1]}
    gathers, got = {}, {}

    def start_gather(stage, deps):
        shards, lands = [], []
        for nm in GATHER_STAGES[stage]:
            sh = hgrn_norm_g if nm == "hgrn_norm_g" else _cast_bf16(src[nm], name=f"cast_{nm}")
            ax = GATHER_AXIS[nm]
            shape = list(sh.shape)
            shape[ax] *= N_DEV
            shards.append(sh)
            lands.append(lax.empty(tuple(shape), sh.dtype))
        peers = (SIBLING,) + SAME_CORE if stage == 0 else tuple(range(1, N_DEV))
        gathers[stage] = _xchg_start(shards, lands, [GATHER_AXIS[nm] for nm in GATHER_STAGES[stage]], gather=True,
                                     name=f"gather_start_{stage}", deps=deps, peers=peers)
        return [gathers[stage]["token"]]

    def get_w(name, after):
        deps = []
        if name not in got:
            group, stage, then = [w for w in GATHER_WAITS if name in w[0]][0]
            xc = gathers[stage]
            if stage == 0:
                relay = _gather_relay(xc, xc["token"], name="gather_relay")
                res = [_gather_relay_wait(xc, relay, xc["token"], name=f"gather_wait_{group[0]}")]
            else:
                res = _xchg_wait(xc, [GATHER_STAGES[stage].index(nm) for nm in group], after,
                                 name=f"gather_wait_{group[0]}")
            got.update(zip(group, res))
            if then is not None:
                deps = start_gather(then, [res[0]])
        return got[name], deps

    first = start_gather(0, [])

    scattered = {}

    def on_grads(tag, grads):
        gnames = list(grads)
        axes = [SCATTER_AXIS[nm] for nm in gnames]
        stacks = []
        for nm, ax in zip(gnames, axes):
            shape = list(grads[nm].shape)
            if ax is not None:
                shape[ax] //= N_DEV
            stacks.append(lax.empty((N_DEV, *shape), grads[nm].dtype))
        scattered[tag] = (gnames, _xchg_start([grads[nm] for nm in gnames], stacks, axes, gather=False,
                                              name=f"scatter_start_{tag}"))
        return [scattered[tag][1]["token"]]

    sm = jax.nn.softmax(lb_logits, axis=0)
    csum = jnp.cumsum(sm, axis=0)
    small = dict(lb=(csum - csum[0:1])[1:2], lb_logits=lb_logits, ln_mix_g=ln_mix_g, ln_mix_b=ln_mix_b,
                 ln_ffn_g=ln_ffn_g, ln_ffn_b=ln_ffn_b)
    grad_x = _local_step(x[0], loss_target[0], get_w, small, on_grads, deps=first)
    out = {}

    def stack_small(src_):
        rows = [None] * SMALL_ROWS
        for name in SMALL:
            rows[SMALL_ROW[name]], rows[SMALL_ROW[name] + 1] = src_[name][0:1], src_[name][1:2]
        zero = jnp.zeros((1, x.shape[-1]), F32)
        return jnp.concatenate([zero if r is None else r for r in rows], axis=0)[None]

    def update(name, slabs):
        shape = wts[name].shape
        out[name] = [r.reshape(shape) for r in _adamw(slabs, wts[name], mom[name], vel[name], name=f"adamw_{name}")]
        return out[name][0]

    slabs, after = {}, grad_x
    for tag, (gnames, xc) in scattered.items():
        slabs.update(zip(gnames, _xchg_wait(xc, list(range(len(gnames))), after, name=f"scatter_wait_{tag}")))
        if tag == "ffn1":
            continue
        if tag == "ffn0":
            update("ffn_w_down", [slabs["ffn_w_down0"], slabs["ffn_w_down1"]])
            after = update("ffn_w_up", [slabs["ffn_w_up0"], slabs["ffn_w_up1"]])
        elif tag == "hgrn":
            update("hgrn_w_out", [slabs["hgrn_w_out"]])
            after = update("hgrn_w_in", [slabs["hgrn_w_in"]])
        elif tag == "attn_out":
            after = update("attn_w_out", [slabs["attn_w_out"]])
        else:
            after = update("attn_w_in", [slabs["attn_w_in"]])
    res = _adamw([slabs["small"]], stack_small(wts), stack_small(mom), stack_small(vel), name="adamw_small")
    for name in SMALL:
        out[name] = [r[0, SMALL_ROW[name]:SMALL_ROW[name] + 2] for r in res]
    loss = res[0][0, LOSS_ROW, 0]
    ng = hgrn_norm_g.shape[-1]
    ng_slabs = lax.dynamic_slice(slabs["small"], (0, NORM_G_ROW, me * ng), (N_DEV, 1, ng))
    out["hgrn_norm_g"] = [r[0] for r in _adamw([ng_slabs], hgrn_norm_g[None], m_hgrn_norm_g[None],
                                                v_hgrn_norm_g[None], name="adamw_norm_g")]
    order =("attn_w_in", "attn_w_out", "hgrn_w_in", "hgrn_w_out", "hgrn_norm_g", "lb_logits", "ln_mix_g", "ln_mix_b",
             "ln_ffn_g", "ln_ffn_b", "ffn_w_up", "ffn_w_down")
    return (loss, grad_x[None], *[out[nm][0] for nm in order], *[out[nm][1] for nm in order],
            *[out[nm][2] for nm in order], *[out[nm][3] for nm in order])
```

```python
import jax
import jax.numpy as jnp
from jax import lax
from jax.experimental import pallas as pl
from jax.experimental.pallas import tpu as pltpu

F32 = jnp.float32
BF16 = jnp.bfloat16

N_DEV = 8
LANES = 128
D_MODEL = 1024
ATTN_HEAD_DIM = 64
ATTN_HEADS = 16
ATTN_SCALE = ATTN_HEAD_DIM ** -0.5
ATTN_BLK = 128
DILATIONS = (1, 4, 16)
ROPE_THETA = 10000.0
HGRN_HEADS = 8
HGRN_CHUNK = 64
D_FF = 4096
LN_EPS = 1e-5
RMS_EPS = 1e-6
DEPTH = 2
ALPHA = (2 * DEPTH) ** 0.25
ADAM_LR, ADAM_B1, ADAM_B2, ADAM_EPS, ADAM_WD, ADAM_STEP = 0.001, 0.9, 0.999, 1e-08, 0.01, 10
VMEM_LIMIT = 48 * 1024 * 1024

_NT = (((1,), (1,)), ((), ()))
_TN = (((0,), (0,)), ((), ()))


def _dot(a, b):
    return jnp.dot(a, b, preferred_element_type=F32)


def _dot_nt(a, b):
    return lax.dot_general(a, b, _NT, preferred_element_type=F32)


def _dot_tn(a, b):
    return lax.dot_general(a, b, _TN, preferred_element_type=F32)


def _split3(x):
    p1 = x.astype(BF16)
    r1 = x - p1.astype(F32)
    p2 = r1.astype(BF16)
    p3 = (r1 - p2.astype(F32)).astype(BF16)
    return p1, p2, p3


def _exact_dot(sel3, x):
    return _dot(sel3, jnp.concatenate(_split3(x), axis=0))


def _exact_dot_r(x, sel3):
    return _dot(jnp.concatenate(_split3(x), axis=1), sel3)


def _params(sem=None):
    return pltpu.CompilerParams(dimension_semantics=sem, vmem_limit_bytes=VMEM_LIMIT)


def _mm(a, b, mode, *, name, m, n, k, tm=1024, tn=1024, tk=1024, out_dtypes=(F32,), epi=None, a_pre=None,
        tile_extras=(), row_extras=(), vec_extras=(), row_outs=(), vec_outs=0, a_split=1, b_split=1, out_split=1,
        b_col_off=0, b_k_off=0, out_col_off=0, out_cols=None, alias=None, epi_wants_j=False, deps=(), t_out=False):
    tm, tn, tk = min(tm, m), min(tn, n), min(tk, k)
    assert m % tm == 0 and n % tn == 0 and k % tk == 0, (name, m, n, k, tm, tn, tk)
    gm, gn, gk = m // tm, n // tn, k // tk
    if mode in ("nn", "nt"):
        if a_split > 1 and tk == k:
            a_spec = pl.BlockSpec((a_split, tm, k // a_split), lambda i, j, kk: (0, i, 0))
        elif a_split > 1:
            kc = (k // a_split) // tk
            a_spec = pl.BlockSpec((None, tm, tk), lambda i, j, kk: (kk // kc, i, kk % kc))
        else:
            a_spec = pl.BlockSpec((tm, tk), lambda i, j, kk: (i, kk))
    else:
        a_spec = pl.BlockSpec((tk, tm), lambda i, j, kk: (kk, i))
    if mode in ("nn", "tn"):
        if b_split > 1:
            nc = (n // b_split) // tn
            b_spec = pl.BlockSpec((None, tk, tn), lambda i, j, kk: (j // nc, kk, j % nc))
        else:
            b_spec = pl.BlockSpec((tk, tn), lambda i, j, kk: (kk + b_k_off, j + b_col_off))
    else:
        b_spec = pl.BlockSpec((tn, tk), lambda i, j, kk: (j + b_col_off, kk + b_k_off))
    if out_split > 1:
        nco = (n // out_split) // tn
        o_spec = pl.BlockSpec((None, tm, tn), lambda i, j, kk: (j // nco, i, j % nco))
        o_shape = (out_split, m, n // out_split)
    else:
        o_spec = pl.BlockSpec((tm, tn), lambda i, j, kk: (i, j + out_col_off))
        o_shape = (m, out_cols if out_cols is not None else n)
    n_ex = len(tile_extras) + len(row_extras) + len(vec_extras)
    n_out = len(out_dtypes)
    n_plain = n_out + len(row_outs)
    assert not vec_outs or gn == 1
    if epi is None:
        def epi(acc):
            return (acc,)
    dot = {"nn": _dot, "nt": _dot_nt, "tn": _dot_tn}[mode]

    def body(*refs):
        a_ref, b_ref = refs[0], refs[1]
        ex = refs[2:2 + n_ex]
        outs = refs[2 + n_ex + (1 if alias is not None else 0) + len(deps):][:n_plain + vec_outs + (1 if t_out else 0)]
        ii = pl.program_id(0)
        jj = pl.program_id(1)

        def product():
            if a_split > 1 and tk == k:
                av = jnp.concatenate([a_ref[p] for p in range(a_split)], axis=1)
            else:
                av = a_ref[...]
            if a_pre is not None:
                av = a_pre(av)
            return dot(av.astype(BF16), b_ref[...].astype(BF16))

        def finish(total):
            lead = (jj,) if epi_wants_j else ()
            res = epi(*lead, total, *[e[...] for e in ex])
            for o, r in zip(outs[:n_plain], res):
                o[...] = r.astype(o.dtype)
            for o, r in zip(outs[n_plain:n_plain + vec_outs], res[n_plain:]):
                @pl.when(ii == 0)
                def _(o=o, r=r):
                    o[...] = r

                @pl.when(ii > 0)
                def _(o=o, r=r):
                    o[...] += r
            if t_out:
                outs[-1][...] = res[-1].astype(outs[-1].dtype)

        if gk == 1:
            finish(product())
        else:
            acc = refs[-1]
            kk = pl.program_id(2)

            @pl.when(kk == 0)
            def _():
                acc[...] = product()

            @pl.when(kk > 0)
            def _():
                acc[...] += product()

            @pl.when(kk == gk - 1)
            def _():
                finish(acc[...])

    in_specs = [a_spec, b_spec] + [o_spec] * len(tile_extras)
    in_specs += [pl.BlockSpec((tm, r.shape[1]), lambda i, j, kk: (i, 0)) if r.shape[0] == m else
                 pl.BlockSpec((r.shape[0], tm), lambda i, j, kk: (0, i)) for r in row_extras]
    in_specs += [pl.BlockSpec((1, tn), lambda i, j, kk: (0, j))] * len(vec_extras)
    args = [a, b] + list(tile_extras) + list(row_extras) + list(vec_extras)
    io_alias = {}
    if alias is not None:
        in_specs.append(pl.BlockSpec(memory_space=pl.ANY))
        args.append(alias)
        io_alias = {len(args) - 1: 0}
    in_specs += [pl.BlockSpec(memory_space=pl.ANY)] * len(deps)
    args += list(deps)
    out_specs = [o_spec] * n_out
    out_shape = [jax.ShapeDtypeStruct(o_shape, dt) for dt in out_dtypes]
    for dt, w in row_outs:
        out_specs.append(pl.BlockSpec((tm, w), lambda i, j, kk: (i, 0)))
        out_shape.append(jax.ShapeDtypeStruct((m, w), dt))
    out_specs += [pl.BlockSpec((1, tn), lambda i, j, kk: (0, j))] * vec_outs
    out_shape += [jax.ShapeDtypeStruct((1, n), F32)] * vec_outs
    if t_out:
        assert out_split > 1
        out_specs.append(pl.BlockSpec((None, tn, tm), lambda i, j, kk: (j // nco, j % nco, i)))
        out_shape.append(jax.ShapeDtypeStruct((out_split, n // out_split, m), out_dtypes[0]))
    out = pl.pallas_call(
        body, name=name, grid=(gm, gn, gk), in_specs=in_specs, out_specs=out_specs, out_shape=out_shape,
        scratch_shapes=[pltpu.VMEM((tm, tn), F32)] if gk > 1 else [],
        input_output_aliases=io_alias,
        compiler_params=_params(("arbitrary" if vec_outs else "parallel", "parallel", "arbitrary")),
    )(*args)
    return out[0] if len(out) == 1 else out


def _rope_tables(seq, dil):
    pos = jnp.arange(seq, dtype=jnp.int32).reshape(seq // dil, dil).T.reshape(seq)
    half = ATTN_HEAD_DIM // 2
    inv = ROPE_THETA ** (-jnp.arange(half, dtype=F32) * (2.0 / ATTN_HEAD_DIM))
    ang = pos.astype(F32)[:, None] * inv[None, :]
    cos, sin = jnp.cos(ang), jnp.sin(ang)
    reps = LANES // ATTN_HEAD_DIM
    return (jnp.tile(jnp.concatenate([cos, cos], axis=1), (1, reps)),
            jnp.tile(jnp.concatenate([-sin, sin], axis=1), (1, reps)))


def _rotate(x, c, ss, sign=1.0):
    w = x.shape[-1]
    half = ATTN_HEAD_DIM // 2
    lane = lax.broadcasted_iota(jnp.int32, x.shape, 1)
    first = (lane % ATTN_HEAD_DIM) < half
    partner = jnp.where(first, pltpu.roll(x, w - half, 1), pltpu.roll(x, half, 1))
    reps = w // LANES
    if reps > 1:
        c = jnp.concatenate([c] * reps, axis=1)
        ss = jnp.concatenate([ss] * reps, axis=1)
    return x * c + sign * (partner * ss)


def _rotate_t(xt, ct, sst):
    half = ATTN_HEAD_DIM // 2
    heads = xt.shape[0] // ATTN_HEAD_DIM
    parts = []
    for h in range(heads):
        lo = h * ATTN_HEAD_DIM
        parts += [xt[lo + half:lo + 2 * half], xt[lo:lo + half]]
    return (xt * jnp.concatenate([ct] * heads, axis=0)
            + jnp.concatenate(parts, axis=0) * jnp.concatenate([sst] * heads, axis=0))


def _ln_epi(acc, x, g, b):
    u = ALPHA * x + acc
    mu = jnp.mean(u, axis=-1, keepdims=True)
    uc = u - mu
    var = jnp.mean(uc * uc, axis=-1, keepdims=True)
    rstd = lax.rsqrt(var + LN_EPS)
    xh = uc * rstd
    out = xh * g + b
    return out, out, xh, rstd


def _ln_grad(dy, xh, rstd, g):
    dxh = dy * g
    m1 = jnp.mean(dxh, axis=-1, keepdims=True)
    m2 = jnp.mean(dxh * xh, axis=-1, keepdims=True)
    du = rstd * (dxh - m1 - xh * m2)
    return du, du, jnp.sum(dy * xh, axis=0, keepdims=True), jnp.sum(dy, axis=0, keepdims=True)


def _ln_grad_epi(acc, du_next, xh, rstd, g):
    return _ln_grad(acc + ALPHA * du_next, xh, rstd, g)


def _ln_loss_epi(acc, x_in, target, g, b):
    out, _, xh, rstd = _ln_epi(acc, x_in, g, b)
    e = out - target
    du, _, dg, db = _ln_grad(e * (1.0 / e.shape[-1]), xh, rstd, g)
    return du, du, dg, db, jnp.sum(e * e, axis=0, keepdims=True)


POS_BLK = 2048


def _class_rows(r, dil):
    return pl.ds(r, POS_BLK // dil, stride=dil) if dil > 1 else pl.ds(0, POS_BLK)


def _class_view(a, dil):
    s, w = a.shape
    return a.reshape(dil, s // dil, w)


def _class_spec(dil, all_tiles=True):
    if all_tiles:
        return pl.BlockSpec((dil, POS_BLK // dil, LANES), lambda i, t: (0, i, t))
    return pl.BlockSpec((dil, POS_BLK // dil, LANES), lambda i, t: (0, i, 0))


def _pos_spec(all_tiles=True):
    if all_tiles:
        return pl.BlockSpec((POS_BLK, LANES), lambda i, t: (i, t))
    return pl.BlockSpec((POS_BLK, LANES), lambda i, t: (i, 0))


def _prep_x(x, deps=()):
    s, d = x.shape

    def body(x_ref, *refs):
        outs = refs[len(deps):]
        for dil, o_ref in zip(DILATIONS, outs):
            for r in range(dil):
                o_ref[r] = x_ref[_class_rows(r, dil), :].astype(BF16)

    outs = pl.pallas_call(
        body, name="prep_x", grid=(s // POS_BLK, d // LANES),
        in_specs=[_pos_spec()] + [pl.BlockSpec(memory_space=pl.ANY)] * len(deps),
        out_specs=[_class_spec(dil) for dil in DILATIONS],
        out_shape=[jax.ShapeDtypeStruct((dil, s // dil, d), BF16) for dil in DILATIONS],
        compiler_params=_params(("parallel", "parallel")),
    )(x, *deps)
    return [o.reshape(s, d) for o in outs]


def _head_expand_matrix():
    h = lax.broadcasted_iota(jnp.int32, (LANES, D_MODEL), 0)
    l = lax.broadcasted_iota(jnp.int32, (LANES, D_MODEL), 1)
    return (l // ATTN_HEAD_DIM == h).astype(BF16)


def _attn_fwd(qkv, dil, *, name):
    _, s, d = qkv.shape
    nq = s // ATTN_BLK
    per = nq // dil
    tiles = d // LANES

    def body(q_ref, kc_ref, kp_ref, vc_ref, vp_ref, o_ref, lse_ref):
        qb = pl.program_id(0)
        first = (qb % per) == 0
        qi = lax.broadcasted_iota(jnp.int32, (ATTN_BLK, 2 * ATTN_BLK), 0)
        kj = lax.broadcasted_iota(jnp.int32, (ATTN_BLK, 2 * ATTN_BLK), 1)
        dist = qi + ATTN_BLK - kj
        valid = (dist >= 0) & (dist <= ATTN_BLK) & ((kj >= ATTN_BLK) | jnp.logical_not(first))
        lane = lax.broadcasted_iota(jnp.int32, (ATTN_BLK, LANES), 1)
        lse_tile = jnp.zeros((ATTN_BLK, LANES), F32)
        zero = jnp.zeros((), BF16)
        in_head = [(lane // ATTN_HEAD_DIM) == hh for hh in range(2)]
        for t0 in range(0, tiles, ATTN_FWD_TILE_GROUP):
            group = range(t0, t0 + ATTN_FWD_TILE_GROUP)
            heads = [(t, hh) for t in group for hh in range(2)]
            cols = {t: pl.ds(t * LANES, LANES) for t in group}
            k2 = {t: jnp.concatenate([kp_ref[:, cols[t]], kc_ref[:, cols[t]]], axis=0) for t in group}
            v2 = {t: jnp.concatenate([vp_ref[:, cols[t]], vc_ref[:, cols[t]]], axis=0) for t in group}
            sc = {(t, hh): jnp.where(valid, _dot_nt(jnp.where(in_head[hh], q_ref[:, cols[t]], zero), k2[t]),
                                     -jnp.inf) for t, hh in heads}
            mx = {i: jnp.max(sc[i], axis=-1, keepdims=True) for i in heads}
            p = {i: jnp.exp(sc[i] - mx[i]) for i in heads}
            l = {i: jnp.sum(p[i], axis=-1, keepdims=True) for i in heads}
            oh = {i: _dot(p[i].astype(BF16), v2[i[0]]) / l[i] for i in heads}
            for t in group:
                o_ref[:, cols[t]] = jnp.where(in_head[0], oh[t, 0], oh[t, 1])
                for hh in range(2):
                    lse_tile = jnp.where(lane == 2 * t + hh, mx[t, hh] + jnp.log(l[t, hh]), lse_tile)
        lse_ref[...] = lse_tile

    def blk(piece, prev):
        if prev:
            return pl.BlockSpec((None, ATTN_BLK, d), lambda i: (piece, jnp.maximum(i - 1, 0), 0))
        return pl.BlockSpec((None, ATTN_BLK, d), lambda i: (piece, i, 0))

    return pl.pallas_call(
        body, name=name, grid=(nq,),
        in_specs=[blk(0, False), blk(1, False), blk(1, True), blk(2, False), blk(2, True)],
        out_specs=[pl.BlockSpec((ATTN_BLK, d), lambda i: (i, 0)), pl.BlockSpec((ATTN_BLK, LANES), lambda i: (i, 0))],
        out_shape=[jax.ShapeDtypeStruct((s, d), F32), jax.ShapeDtypeStruct((s, LANES), F32)],
        compiler_params=_params(("parallel",)),
    )(qkv, qkv, qkv, qkv, qkv)


def _attn_combine(os_, lses):
    s, d = os_[0].shape
    sel = jnp.concatenate([_head_expand_matrix()] * 3, axis=0)

    def body(o0, o1, o2, l0, l1, l2, sel_ref, of_ref, ob_ref, lt_ref, o_pos, l_pos):
        for g, (dil, o_ref, l_ref) in enumerate(zip(DILATIONS, (o0, o1, o2), (l0, l1, l2))):
            for r in range(dil):
                o_pos[g, _class_rows(r, dil), :] = o_ref[r]
                l_pos[g, _class_rows(r, dil), :] = l_ref[r]
        la, lb_, lc = l_pos[0], l_pos[1], l_pos[2]
        mx = jnp.maximum(jnp.maximum(la, lb_), lc)
        es = (jnp.exp(la - mx), jnp.exp(lb_ - mx), jnp.exp(lc - mx))
        z = es[0] + es[1] + es[2]
        lt_ref[...] = mx + jnp.log(z)
        acc = jnp.zeros((POS_BLK, LANES), F32)
        for g in range(3):
            acc += _exact_dot_r(es[g] / z, sel_ref[...]) * o_pos[g]
        of_ref[...] = acc
        ob_ref[...] = acc.astype(BF16)

    return pl.pallas_call(
        body, name="attn_combine", grid=(s // POS_BLK, d // LANES),
        in_specs=[_class_spec(dil) for dil in DILATIONS] + [_class_spec(dil, False) for dil in DILATIONS]
        + [pl.BlockSpec((3 * LANES, LANES), lambda i, t: (0, t))],
        out_specs=[_pos_spec(), _pos_spec(), _pos_spec(False)],
        out_shape=[jax.ShapeDtypeStruct((s, d), F32), jax.ShapeDtypeStruct((s, d), BF16),
                   jax.ShapeDtypeStruct((s, LANES), F32)],
        scratch_shapes=[pltpu.VMEM((3, POS_BLK, LANES), F32), pltpu.VMEM((3, POS_BLK, LANES), F32)],
        compiler_params=_params(("parallel", "arbitrary")),
    )(*[_class_view(o, dil) for o, dil in zip(os_, DILATIONS)],
      *[_class_view(l, dil) for l, dil in zip(lses, DILATIONS)], sel)


ATTN_TILE_GROUP = 2
ATTN_FWD_TILE_GROUP = 4
AUX_PER_TILE = 12


def _aux_placement():
    h = lax.broadcasted_iota(jnp.int32, (6, LANES, LANES), 1)
    l = lax.broadcasted_iota(jnp.int32, (6, LANES, LANES), 2)
    j = lax.broadcasted_iota(jnp.int32, (6, LANES, LANES), 0)
    target = AUX_PER_TILE * (h // 2) + 3 * (h % 2) + jnp.where(j < 3, j, 3 + j)
    return ((l == target) & (h < ATTN_HEADS)).astype(BF16)


def _attn_bwd_prep(do, o, lse):
    s, d = do.shape
    tiles = d // LANES
    sel_t = jnp.tile(_head_expand_matrix().T.reshape(tiles, LANES, LANES), (1, 3, 1))

    def body(do_ref, o_ref, l_ref, sel_ref, place_ref, *refs):
        outs, delta, aux = refs[:9], refs[9], refs[10]
        t = pl.program_id(1)
        part = _exact_dot_r(do_ref[...] * o_ref[...], sel_ref[...])

        @pl.when(t == 0)
        def _():
            delta[...] = part

        @pl.when(t > 0)
        def _():
            delta[...] += part

        for g, dil in enumerate(DILATIONS):
            for r in range(dil):
                blk = do_ref[_class_rows(r, dil), :].astype(BF16)
                outs[g][r] = blk
                outs[3 + g][r] = blk.T

        @pl.when(t == tiles - 1)
        def _():
            pieces = _split3(l_ref[...]) + _split3(delta[...])
            aux[...] = _dot(jnp.concatenate(pieces, axis=1), place_ref[...])
            for g, dil in enumerate(DILATIONS):
                for r in range(dil):
                    outs[6 + g][r] = aux[_class_rows(r, dil), :].astype(BF16)

    outs = pl.pallas_call(
        body, name="attn_bwd_prep", grid=(s // POS_BLK, tiles),
        in_specs=[_pos_spec(), _pos_spec(), _pos_spec(False),
                  pl.BlockSpec((None, 3 * LANES, LANES), lambda i, t: (t, 0, 0)),
                  pl.BlockSpec((6 * LANES, LANES), lambda i, t: (0, 0))],
        out_specs=[_class_spec(dil) for dil in DILATIONS]
        + [pl.BlockSpec((dil, LANES, POS_BLK // dil), lambda i, t: (0, t, i)) for dil in DILATIONS]
        + [_class_spec(dil, False) for dil in DILATIONS],
        out_shape=[jax.ShapeDtypeStruct((dil, s // dil, d), BF16) for dil in DILATIONS]
        + [jax.ShapeDtypeStruct((dil, d, s // dil), BF16) for dil in DILATIONS]
        + [jax.ShapeDtypeStruct((dil, s // dil, LANES), BF16) for dil in DILATIONS],
        scratch_shapes=[pltpu.VMEM((POS_BLK, LANES), F32), pltpu.VMEM((POS_BLK, LANES), F32)],
        compiler_params=_params(("parallel", "arbitrary")),
    )(do, o, lse, sel_t, _aux_placement().reshape(6 * LANES, LANES))
    return ([a.reshape(s, d) for a in outs[0:3]], list(outs[3:6]), [a.reshape(s, LANES) for a in outs[6:9]])


def _attn_bwd(qkv, qkv_t, do, do_t, aux, tables, dil, *, name):
    _, s, d = qkv.shape
    nq = s // ATTN_BLK
    per = nq // dil
    tiles = d // LANES
    half = ATTN_HEAD_DIM

    def body(qd_ref, qo_ref, k_ref, qtd_ref, qto_ref, kt_ref, vt_ref, dod_ref, doo_ref, dotd_ref, doto_ref,
             auxd_ref, auxo_ref, c_ref, ss_ref, out_ref, carry):
        kb = pl.program_id(0)

        @pl.when(kb == 0)
        def _():
            carry[...] = jnp.zeros_like(carry)

        has_next = (kb % per) != (per - 1)
        qi = lax.broadcasted_iota(jnp.int32, (ATTN_BLK, 2 * ATTN_BLK), 0)
        kj = lax.broadcasted_iota(jnp.int32, (ATTN_BLK, 2 * ATTN_BLK), 1) % ATTN_BLK
        valid_d = kj <= qi
        valid_o = (kj >= qi) & has_next
        lane = lax.broadcasted_iota(jnp.int32, (ATTN_BLK, LANES), 1)
        row = lax.broadcasted_iota(jnp.int32, (LANES, ATTN_BLK), 0)
        side = lax.broadcasted_iota(jnp.int32, (LANES, 2 * ATTN_BLK), 0)
        first = lax.broadcasted_iota(jnp.int32, (LANES, 2 * ATTN_BLK), 1) < ATTN_BLK
        c, ss = c_ref[...], ss_ref[...]
        zero = jnp.zeros((), BF16)
        sides = ((qd_ref, qtd_ref, dod_ref, dotd_ref, auxd_ref[...], valid_d),
                 (qo_ref, qto_ref, doo_ref, doto_ref, auxo_ref[...], valid_o))
        both = (0, 1)

        def head_halves(x, index):
            axis = 0 if index is lane else 1
            return jnp.concatenate([jnp.where(index < half, x, zero), jnp.where(index >= half, x, zero)], axis=axis)

        for t0 in range(0, tiles, ATTN_TILE_GROUP):
            group = range(t0, t0 + ATTN_TILE_GROUP)
            cols = {t: pl.ds(t * LANES, LANES) for t in group}
            kk, kk_t, vv_t = {}, {}, {}
            for t in group:
                base = AUX_PER_TILE * t
                hit = lambda lo: ((first & (side >= base + lo) & (side < base + lo + 3))
                                  | (jnp.logical_not(first) & (side >= base + lo + 3) & (side < base + lo + 6)))
                kk[t] = head_halves(k_ref[:, cols[t]], lane)
                kk_t[t] = jnp.concatenate([head_halves(kt_ref[cols[t], :], row),
                                           jnp.where(hit(0), -1.0, 0.0).astype(BF16)], axis=0)
                vv_t[t] = jnp.concatenate([head_halves(vt_ref[cols[t], :], row),
                                           jnp.where(hit(6), -1.0, 0.0).astype(BF16)], axis=0)
            sc = {(t, w): _dot(jnp.concatenate([sides[w][0][:, cols[t]], sides[w][4]], axis=1), kk_t[t])
                  for t in group for w in both}
            dpd = {(t, w): _dot(jnp.concatenate([sides[w][2][:, cols[t]], sides[w][4]], axis=1), vv_t[t])
                   for t in group for w in both}
            p = {i: jnp.where(sides[i[1]][5], jnp.exp(sc[i]), 0.0) for i in sc}
            ds = {i: (p[i] * dpd[i]).astype(BF16) for i in sc}
            pb = {i: p[i].astype(BF16) for i in sc}
            dv_t = {t: sum(_dot(sides[w][3][cols[t], :], pb[t, w]) for w in both) for t in group}
            dk_t = {t: sum(_dot(sides[w][1][cols[t], :], ds[t, w]) for w in both) for t in group}
            dq = {i: _dot(ds[i], kk[i[0]]) * ATTN_SCALE for i in sc}
            for t in group:
                dq_now = carry[:, cols[t]] + dq[t, 0]
                carry[:, cols[t]] = dq[t, 1]
                dk = jnp.where(row < half, dk_t[t][:, :ATTN_BLK], dk_t[t][:, ATTN_BLK:]).T
                dv = jnp.where(row < half, dv_t[t][:, :ATTN_BLK], dv_t[t][:, ATTN_BLK:]).T
                out_ref[0, :, cols[t]] = _rotate(dq_now, c, ss, -1.0).astype(BF16)
                out_ref[1, :, cols[t]] = _rotate(dk, c, ss, -1.0).astype(BF16)
                out_ref[2, :, cols[t]] = dv.astype(BF16)

    def nxt(i):
        return jnp.minimum(i + 1, nq - 1)

    def piece(p, shift):
        if shift:
            return pl.BlockSpec((None, ATTN_BLK, d), lambda i: (p, nxt(i), 0))
        return pl.BlockSpec((None, ATTN_BLK, d), lambda i: (p, i, 0))

    def piece_t(p, shift):
        if shift:
            return pl.BlockSpec((None, d, ATTN_BLK), lambda i: (p, 0, nxt(i)))
        return pl.BlockSpec((None, d, ATTN_BLK), lambda i: (p, 0, i))

    def rows(width, shift):
        if shift:
            return pl.BlockSpec((ATTN_BLK, width), lambda i: (nxt(i), 0))
        return pl.BlockSpec((ATTN_BLK, width), lambda i: (i, 0))

    def do_t_spec(shift):
        if shift:
            return pl.BlockSpec((None, d, ATTN_BLK), lambda i: (nxt(i) // per, 0, nxt(i) % per))
        return pl.BlockSpec((None, d, ATTN_BLK), lambda i: (i // per, 0, i % per))

    return pl.pallas_call(
        body, name=name, grid=(nq,),
        in_specs=[piece(0, False), piece(0, True), piece(1, False),
                  piece_t(0, False), piece_t(0, True), piece_t(1, False), piece_t(2, False),
                  rows(d, False), rows(d, True), do_t_spec(False), do_t_spec(True),
                  rows(LANES, False), rows(LANES, True), rows(LANES, False), rows(LANES, False)],
        out_specs=pl.BlockSpec((3, ATTN_BLK, d), lambda i: (0, i, 0)),
        out_shape=jax.ShapeDtypeStruct((3, s, d), BF16),
        scratch_shapes=[pltpu.VMEM((ATTN_BLK, d), F32)],
        compiler_params=_params(("arbitrary",)),
    )(qkv, qkv, qkv, qkv_t, qkv_t, qkv_t, qkv_t, do, do, do_t, do_t, aux, aux, *tables)


def _dx_combine(du, parts):
    s, d = du.shape

    def body(du_ref, p0, p1, p2, out_ref):
        out_ref[...] = ALPHA * du_ref[...] + p0[0]
        for dil, p_ref in zip(DILATIONS[1:], (p1, p2)):
            for r in range(dil):
                out_ref[_class_rows(r, dil), :] += p_ref[r]

    return pl.pallas_call(
        body, name="dx_combine", grid=(s // POS_BLK, d // LANES),
        in_specs=[_pos_spec()] + [_class_spec(dil) for dil in DILATIONS], out_specs=_pos_spec(),
        out_shape=jax.ShapeDtypeStruct((s, d), F32),
        compiler_params=_params(("parallel", "parallel")),
    )(du, *[_class_view(p, dil) for p, dil in zip(parts, DILATIONS)])


HGRN_ROWS = 2048


def _tri(lower, copies=1):
    i = lax.broadcasted_iota(jnp.int32, (HGRN_CHUNK, HGRN_CHUNK * copies), 0)
    j = lax.broadcasted_iota(jnp.int32, (HGRN_CHUNK, HGRN_CHUNK * copies), 1) % HGRN_CHUNK
    return (j <= i) if lower else (j >= i)


def _hgrn_gates(qr, z, lb):
    sq = jax.nn.sigmoid(qr)
    e = jnp.exp(-jnp.abs(z))
    big = 1.0 / (1.0 + e)
    small = e * big
    sg = jnp.where(z >= 0, big, small)
    sn = jnp.where(z >= 0, small, big)
    return sq, qr * sq, sg, sn, lb + (1.0 - lb) * sg, (1.0 - lb) * sn


def _hgrn_fwd(p2, lb, norm_g):
    _, s, d = p2.shape
    nblk = s // HGRN_ROWS
    cps = HGRN_ROWS // HGRN_CHUNK

    def body(p_ref, lb_ref, g_ref, o_ref, on_ref, st_ref, state):
        @pl.when(pl.program_id(1) == 0)
        def _():
            state[...] = jnp.zeros_like(state)

        chunks = [pl.ds(c * HGRN_CHUNK, HGRN_CHUNK) for c in range(cps)]
        ltri = _tri(True)
        lsel = _tri(True, 3).astype(BF16)
        _, q, _, _, f, key = _hgrn_gates(p_ref[0], p_ref[1], lb_ref[...])
        lf = jnp.log(f)
        v = [p_ref[2, rows, :].astype(BF16) for rows in chunks]
        b = [_exact_dot(lsel, lf[c * HGRN_CHUNK:(c + 1) * HGRN_CHUNK]) for c in range(cps)]
        b_last = [bc[HGRN_CHUNK - 1:HGRN_CHUNK, :] for bc in b]
        q = [q[c * HGRN_CHUNK:(c + 1) * HGRN_CHUNK] for c in range(cps)]
        key = [key[c * HGRN_CHUNK:(c + 1) * HGRN_CHUNK] for c in range(cps)]
        qd = [(q[c] * jnp.exp(b[c])).astype(BF16) for c in range(cps)]
        kd = [(key[c] * jnp.exp(-b[c])).astype(BF16) for c in range(cps)]
        k2 = [(key[c] * jnp.exp(b_last[c] - b[c])).astype(BF16) for c in range(cps)]
        a = [jnp.where(ltri, _dot_nt(qd[c], kd[c]), 0.0).astype(BF16) for c in range(cps)]
        kv = [_dot_tn(v[c], k2[c]) for c in range(cps)]
        st, sts = state[...], []
        for c in range(cps):
            sts.append(st)
            st_ref[0, c] = st
            st = st * jnp.exp(b_last[c]) + kv[c]
        state[...] = st
        gv = g_ref[...]
        for c in range(cps):
            o = _dot(a[c], v[c]) + _dot_nt(qd[c], sts[c].astype(BF16))
            o_ref[chunks[c], :] = o
            r = lax.rsqrt(jnp.mean(o * o, axis=-1, keepdims=True) + RMS_EPS)
            on_ref[chunks[c], :] = (o * r * gv).astype(BF16)

    vec = pl.BlockSpec((1, LANES), lambda h, c: (0, h))
    col = pl.BlockSpec((HGRN_ROWS, LANES), lambda h, c: (c, h))
    return pl.pallas_call(
        body, name="hgrn_fwd", grid=(HGRN_HEADS, nblk),
        in_specs=[pl.BlockSpec((3, HGRN_ROWS, LANES), lambda h, c: (0, c, h)), vec, vec],
        out_specs=[col, col, pl.BlockSpec((1, cps, LANES, LANES), lambda h, c: (h, c, 0, 0))],
        out_shape=[jax.ShapeDtypeStruct((s, d), F32), jax.ShapeDtypeStruct((s, d), BF16),
                   jax.ShapeDtypeStruct((HGRN_HEADS, s // HGRN_CHUNK, LANES, LANES), F32)],
        scratch_shapes=[pltpu.VMEM((LANES, LANES), F32)],
        compiler_params=_params(("parallel", "arbitrary")),
    )(p2, lb, norm_g)


def _hgrn_bwd(p2, lb, norm_g, o_raw, states, dyn):
    _, s, d = p2.shape
    nblk = s // HGRN_ROWS
    cps = HGRN_ROWS // HGRN_CHUNK

    def body(p_ref, lb_ref, g_ref, o_ref, st_ref, dy_ref, dp_ref, dg_ref, dlb_ref, dstate):
        @pl.when(pl.program_id(1) == 0)
        def _():
            dstate[...] = jnp.zeros_like(dstate)
            dg_ref[...] = jnp.zeros_like(dg_ref)
            dlb_ref[...] = jnp.zeros_like(dlb_ref)

        n = cps
        cut = lambda t: [t[c * HGRN_CHUNK:(c + 1) * HGRN_CHUNK] for c in range(n)]
        chunks = [pl.ds(c * HGRN_CHUNK, HGRN_CHUNK) for c in range(n)]
        lbv = lb_ref[...]
        ltri = _tri(True)
        lsel = _tri(True, 3).astype(BF16)
        usel = _tri(False, 3).astype(BF16)
        last_row = lax.broadcasted_iota(jnp.int32, (HGRN_CHUNK, LANES), 0) == HGRN_CHUNK - 1
        qr, z = p_ref[0], p_ref[1]
        sq, q, sg, sn, f, key = _hgrn_gates(qr, z, lbv)
        lf = jnp.log(f)
        v = [p_ref[2, rows, :].astype(BF16) for rows in chunks]
        o, dyv = o_ref[...], dy_ref[...]
        r = lax.rsqrt(jnp.mean(o * o, axis=-1, keepdims=True) + RMS_EPS)
        oh = o * r
        doh = dyv * g_ref[...]
        do = cut((r * (doh - oh * jnp.mean(doh * oh, axis=-1, keepdims=True))).astype(BF16))
        dg_ref[...] += jnp.sum(dyv * oh, axis=0, keepdims=True)
        b = [_exact_dot(lsel, lfc) for lfc in cut(lf)]
        b_last = [bc[HGRN_CHUNK - 1:HGRN_CHUNK, :] for bc in b]
        q, key = cut(q), cut(key)
        eb = [jnp.exp(bc) for bc in b]
        enb = [jnp.exp(-bc) for bc in b]
        e2 = [jnp.exp(b_last[c] - b[c]) for c in range(n)]
        dec = [jnp.exp(bl) for bl in b_last]
        qd_f = [q[c] * eb[c] for c in range(n)]
        kd_f = [key[c] * enb[c] for c in range(n)]
        k2_f = [key[c] * e2[c] for c in range(n)]
        qd, kd, k2 = ([t.astype(BF16) for t in ts] for ts in (qd_f, kd_f, k2_f))
        a = [jnp.where(ltri, _dot_nt(qd[c], kd[c]), 0.0).astype(BF16) for c in range(n)]
        da = [jnp.where(ltri, _dot_nt(do[c], v[c]), 0.0).astype(BF16) for c in range(n)]
        grow = [_dot_tn(do[c], qd[c]) for c in range(n)]
        dst, dsts = dstate[...], [None] * n
        for c in reversed(range(n)):
            dsts[c] = dst
            dst = dst * dec[c] + grow[c]
        dstate[...] = dst
        st = [st_ref[0, c] for c in range(n)]
        dstb = [t.astype(BF16) for t in dsts]
        dv = [_dot_tn(a[c], do[c]) + _dot_nt(k2[c], dstb[c]) for c in range(n)]
        dqd = [_dot(da[c], kd[c]) + _dot(do[c], st[c].astype(BF16)) for c in range(n)]
        dkd = [_dot_tn(da[c], qd[c]) for c in range(n)]
        dk2 = [_dot(v[c], dstb[c]) for c in range(n)]
        db = []
        for c in range(n):
            ddec = jnp.sum(dsts[c] * st[c], axis=0, keepdims=True)
            db_last = jnp.sum(dk2[c] * k2_f[c], axis=0, keepdims=True) + ddec * dec[c]
            db.append(dqd[c] * qd_f[c] - dkd[c] * kd_f[c] - dk2[c] * k2_f[c] + jnp.where(last_row, db_last, 0.0))
        dlf = [_exact_dot(usel, dbc) for dbc in db]
        f, sg, sn, sq, qr = cut(f), cut(sg), cut(sn), cut(sq), cut(qr)
        dlb_acc = jnp.zeros((1, LANES), F32)
        for c in range(n):
            dkey = dkd[c] * enb[c] + dk2[c] * e2[c]
            common = dlf[c] / f[c] - dkey
            dlb_acc += jnp.sum(common * sn[c], axis=0, keepdims=True)
            dp_ref[0, chunks[c], :] = (dqd[c] * eb[c] * (sq[c] * (1.0 + qr[c] * (1.0 - sq[c])))).astype(BF16)
            dp_ref[1, chunks[c], :] = (common * ((1.0 - lbv) * sg[c] * sn[c])).astype(BF16)
            dp_ref[2, chunks[c], :] = dv[c].astype(BF16)
        dlb_ref[...] += dlb_acc

    def rev(c):
        return nblk - 1 - c

    vec = pl.BlockSpec((1, LANES), lambda h, c: (0, h))
    col = pl.BlockSpec((HGRN_ROWS, LANES), lambda h, c: (rev(c), h))
    p3 = pl.BlockSpec((3, HGRN_ROWS, LANES), lambda h, c: (0, rev(c), h))
    return pl.pallas_call(
        body, name="hgrn_bwd", grid=(HGRN_HEADS, nblk),
        in_specs=[p3, vec, vec, col, pl.BlockSpec((1, cps, LANES, LANES), lambda h, c: (h, rev(c), 0, 0)), col],
        out_specs=[p3, vec, vec],
        out_shape=[jax.ShapeDtypeStruct((3, s, d), BF16), jax.ShapeDtypeStruct((1, d), F32),
                   jax.ShapeDtypeStruct((1, d), F32)],
        scratch_shapes=[pltpu.VMEM((LANES, LANES), F32)],
        compiler_params=_params(("parallel", "arbitrary")),
    )(p2, lb, norm_g, o_raw, states, dyn)


SMALL_ROWS = 16
FUSED_TM = 512


def _relu2(h):
    r = jnp.maximum(h.astype(F32), 0.0)
    return r * r


def _dact_epi(acc, h):
    return (acc * (2.0 * jnp.maximum(h.astype(F32), 0.0)),)


def _rope_epi(j, acc, ct, sst):
    acc_t = acc.T
    rot_t = _rotate_t(acc_t, ct, sst) * jnp.where(j == 0, ATTN_SCALE, 1.0)
    out_t = jnp.where(j >= 2, acc_t, rot_t).astype(BF16)
    return out_t.T, out_t


def _local_step(x, target, get_w, small, on_grads, deps=()):
    s, d = x.shape
    lb = small["lb"]
    gq = 3 * d
    gt = gq // 1024

    def ln_of(a, w, x_in, which, layer, *, name, k, a_pre=None, deps=()):
        g, b = small[f"ln_{which}_g"][layer:layer + 1], small[f"ln_{which}_b"][layer:layer + 1]
        return _mm(a, w, "nn", name=name, m=s, n=d, k=k, tm=FUSED_TM, tk=k, a_pre=a_pre, out_dtypes=(F32, BF16, F32),
                   row_outs=((F32, 1),), epi=_ln_epi, tile_extras=(x_in,), vec_extras=(g, b), deps=deps)

    def ln_grad_of(a, w, du_next, xh, rstd, which, layer, *, name, k, a_split=1, deps=()):
        return _mm(a, w, "nt", name=name, m=s, n=d, k=k, tm=FUSED_TM, tk=k, a_split=a_split,
                   out_dtypes=(F32, BF16), vec_outs=2, epi=_ln_grad_epi, tile_extras=(du_next, xh), row_extras=(rstd,),
                   vec_extras=(small[f"ln_{which}_g"][layer:layer + 1],), deps=deps)

    def ffn_fwd(xb, x_in, w_up, w_down, tag):
        h = _mm(xb, w_up, "nn", name=f"ffn_up_{tag}", m=s, n=D_FF, k=d, tm=2048, out_dtypes=(BF16,))
        return (h,) + tuple(ln_of(h, w_down, x_in, "ffn", tag, name=f"ffn_down_{tag}", k=D_FF, a_pre=_relu2))

    def ffn_bwd(du, dub, xb, h, w_up, w_down, xh, rstd, tag):
        dh = _mm(dub, w_down, "nt", name=f"ffn_dact_{tag}", m=s, n=D_FF, k=d, tm=2048, out_dtypes=(BF16,), epi=_dact_epi,
                 tile_extras=(h,))
        g_down = _mm(h, dub, "tn", name=f"ffn_gdown_{tag}", m=D_FF, n=d, k=s, a_pre=_relu2, out_dtypes=(BF16,))
        g_up = _mm(xb, dh, "tn", name=f"ffn_gup_{tag}", m=d, n=D_FF, k=s, tk=s, out_dtypes=(BF16,))
        after = on_grads(f"ffn{tag}", {f"ffn_w_down{tag}": g_down, f"ffn_w_up{tag}": g_up})
        return ln_grad_of(dh, w_up, du, xh, rstd, "mix", tag, name=f"ffn_dx_{tag}", k=D_FF, deps=after)

    xs = _prep_x(x, deps)
    w_ain, after = get_w("attn_w_in", None)
    tabs, qkvs, qkv_ts, o_parts, lse_parts = [], [], [], [], []
    for g, dil in enumerate(DILATIONS):
        tabs.append(_rope_tables(s, dil))
        tabs_t = [t[:, :ATTN_HEAD_DIM].T for t in tabs[g]]
        qkv, qkv_t = _mm(xs[g], w_ain, "nn", name=f"attn_in_{g}", m=s, n=gq, k=d, b_col_off=g * gt, out_split=3,
                         out_dtypes=(BF16,), epi=_rope_epi, epi_wants_j=True, row_extras=tabs_t, t_out=True,
                         deps=after if g == 0 else ())
        qkvs.append(qkv)
        qkv_ts.append(qkv_t)
        o_g, lse_g = _attn_fwd(qkv, dil, name=f"attn_fwd_{g}")
        o_parts.append(o_g)
        lse_parts.append(lse_g)
    o_f, o_b, lse_t = _attn_combine(o_parts, lse_parts)
    w_aout, after = get_w("attn_w_out", o_b)
    x1, x1b, xh1, r1 = ln_of(o_b, w_aout, x, "mix", 0, name="attn_out", k=d, deps=after)
    w_up0, w_down0 = get_w("ffn_w_up0", o_b)[0], get_w("ffn_w_down0", o_b)[0]
    h0, x2, x2b, xh2, r2 = ffn_fwd(x1b, x1, w_up0, w_down0, 0)
    w_hin, w_hout, norm_g = get_w("hgrn_w_in", x2b)[0], get_w("hgrn_w_out", x2b)[0], get_w("hgrn_norm_g", x2b)[0]
    p2 = _mm(x2b, w_hin, "nn", name="hgrn_in", m=s, n=3 * d, k=d, out_split=3)
    o_raw, o_n, states = _hgrn_fwd(p2, lb, norm_g)
    w_up1, w_down1 = get_w("ffn_w_up1", o_n)[0], get_w("ffn_w_down1", o_n)[0]
    x3, x3b, xh3, r3 = ln_of(o_n, w_hout, x2, "mix", 1, name="hgrn_out", k=d)
    h1 = _mm(x3b, w_up1, "nn", name="ffn_up_1", m=s, n=D_FF, k=d, tm=2048, out_dtypes=(BF16,))
    du, dub, dg_ffn1, db_ffn1, sq = _mm(
        h1, w_down1, "nn", name="ffn_down_1", m=s, n=d, k=D_FF, tm=FUSED_TM, tk=D_FF, a_pre=_relu2,
        out_dtypes=(F32, BF16), vec_outs=3, epi=_ln_loss_epi, tile_extras=(x3, target),
        vec_extras=(small["ln_ffn_g"][1:2], small["ln_ffn_b"][1:2]))
    du, dub, dg_mix1, db_mix1 = ffn_bwd(du, dub, x3b, h1, w_up1, w_down1, xh3, r3, 1)
    dyn = _mm(dub, w_hout, "nt", name="hgrn_dout", m=s, n=d, k=d)
    g_hout = _mm(o_n, dub, "tn", name="hgrn_gout", m=d, n=d, k=s, out_dtypes=(BF16,))
    dp2, d_norm_g, d_lb = _hgrn_bwd(p2, lb, norm_g, o_raw, states, dyn)
    g_hin = _mm(x2b, dp2, "tn", name="hgrn_gin", m=d, n=3 * d, k=s, b_split=3, out_dtypes=(BF16,))
    after = on_grads("hgrn", {"hgrn_w_out": g_hout, "hgrn_w_in": g_hin})
    du, dub, dg_ffn0, db_ffn0 = ln_grad_of(dp2, w_hin, du, xh2, r2, "ffn", 0, name="hgrn_dx", k=3 * d, a_split=3,
                                           deps=after)
    du, dub, dg_mix0, db_mix0 = ffn_bwd(du, dub, x1b, h0, w_up0, w_down0, xh1, r1, 0)
    g_aout = _mm(o_b, dub, "tn", name="attn_gout", m=d, n=d, k=s, out_dtypes=(BF16,))
    sm1 = jax.nn.softmax(small["lb_logits"], axis=0)
    d_l1 = d_lb * (sm1[0:1] * sm1[1:2])
    zeros = jnp.zeros((SMALL_ROWS - 12, d), F32)
    loss_row = jnp.broadcast_to(0.5 * jnp.sum(sq) / d, (1, d))
    small_grads = jnp.concatenate([dg_mix0, dg_mix1, db_mix0, db_mix1, dg_ffn0, dg_ffn1, db_ffn0, db_ffn1,
                                   -d_l1, d_l1, d_norm_g, loss_row, zeros], axis=0)
    after = on_grads("attn_out", {"attn_w_out": g_aout, "small": small_grads})
    do = _mm(dub, w_aout, "nt", name="attn_dout", m=s, n=d, k=d, deps=after)
    do_parts, do_ts, aux_parts = _attn_bwd_prep(do, o_f, lse_t)
    g_ain, dqkvs = None, []
    for g, dil in enumerate(DILATIONS):
        dqkvs.append(_attn_bwd(qkvs[g], qkv_ts[g], do_parts[g], do_ts[g], aux_parts[g], tabs[g], dil,
                               name=f"attn_bwd_{g}"))
        g_ain = _mm(xs[g], dqkvs[g], "tn", name=f"attn_gin_{g}", m=d, n=gq, k=s, b_split=3, out_dtypes=(BF16,),
                    out_col_off=g * gt, out_cols=3 * gq, alias=g_ain)
    after = on_grads("attn_in", {"attn_w_in": g_ain})
    dx_parts = [_mm(dqkvs[g], w_ain, "nt", name=f"attn_dx_{g}", m=s, n=d, k=gq, tm=FUSED_TM, tk=gq, a_split=3, b_k_off=g,
                    deps=after if g == 0 else ())
                for g in range(len(DILATIONS))]
    return _dx_combine(du, dx_parts)


def _mesh_place():
    x, y, c = lax.axis_index("x"), lax.axis_index("y"), lax.axis_index("c")
    return x, y, c, 4 * x + 2 * y + c


def _peer(x, y, c, k):
    px = 1 - x if (k >> 2) & 1 else x
    py = 1 - y if (k >> 1) & 1 else y
    pc = 1 - c if k & 1 else c
    return (px, py, pc), 4 * px + 2 * py + pc


def _window(ref, axis, size, idx):
    if axis is None:
        return ref
    sl = [slice(None)] * len(ref.shape)
    sl[axis] = pl.ds(idx * size, size)
    return ref.at[tuple(sl)]


_HBM = pl.BlockSpec(memory_space=pltpu.HBM)
_SEM = pl.BlockSpec(memory_space=pltpu.SEMAPHORE)
_EFFECT = pltpu.SideEffectType.DATAFLOW_SIDE_EFFECTING


def _xchg_ends(src_ref, land_ref, axis, gather, me, other):
    if gather:
        size = src_ref.shape[axis]
        return src_ref, _window(land_ref, axis, size, me), _window(land_ref, axis, size, other)
    size = None if axis is None else src_ref.shape[axis] // N_DEV
    return _window(src_ref, axis, size, other), land_ref.at[me], land_ref.at[other]


def _xchg_start(srcs, lands, axes, *, gather, name, deps=(), peers=tuple(range(1, N_DEV))):
    n = len(srcs)
    nd = len(deps)

    def body(*refs):
        src_refs, land_refs = refs[:n], refs[n:2 * n]
        send, recv = refs[2 * n + nd:3 * n + nd], refs[3 * n + nd:4 * n + nd]
        token = refs[-1]
        x, y, c, me = _mesh_place()
        for k in peers:
            peer, pidx = _peer(x, y, c, k)
            for i in range(n):
                src, dst, _ = _xchg_ends(src_refs[i], land_refs[i], axes[i], gather, me, pidx)
                pltpu.make_async_remote_copy(
                    src_ref=src, dst_ref=dst, send_sem=send[i].at[k - 1], recv_sem=recv[i].at[k - 1],
                    device_id=peer, device_id_type=pl.DeviceIdType.MESH).start()
        for i in range(n):
            src, dst, _ = _xchg_ends(src_refs[i], land_refs[i], axes[i], gather, me, me)
            pltpu.make_async_copy(src, dst, send[i].at[N_DEV - 1]).start()
        token[...] = jnp.zeros_like(token)

    bufs = list(srcs) + list(lands)
    outs = pl.pallas_call(
        body, name=name,
        out_shape=[pltpu.SemaphoreType.DMA((N_DEV,))] * (2 * n) + [pltpu.HBM(b.shape, b.dtype) for b in bufs]
        + [jax.ShapeDtypeStruct((8, LANES), F32)],
        in_specs=[_HBM] * (2 * n) + [pl.BlockSpec(memory_space=pl.ANY)] * nd,
        out_specs=[_SEM] * (2 * n) + [_HBM] * (2 * n) + [pl.BlockSpec(memory_space=pltpu.VMEM)],
        input_output_aliases={i: 2 * n + i for i in range(2 * n)},
        compiler_params=pltpu.CompilerParams(has_side_effects=_EFFECT),
    )(*[pltpu.with_memory_space_constraint(b, pltpu.HBM) for b in bufs], *deps)
    return dict(send=outs[:n], recv=outs[n:2 * n], srcs=outs[2 * n:3 * n], lands=outs[3 * n:4 * n], token=outs[-1],
                axes=list(axes), gather=gather)


SIBLING = 1
SAME_CORE = (2, 4, 6)


def _gather_relay(xc, after, *, name):
    axis = xc["axes"][0]

    def body(src_ref, land_ref, send1, recv1, after_ref, src_out, land_out, send2, recv2):
        x, y, c, me = _mesh_place()
        sibling, _ = _peer(x, y, c, SIBLING)
        for j, k in enumerate(SAME_CORE):
            peer, pidx = _peer(x, y, c, k)
            src, _, got = _xchg_ends(src_ref, land_ref, axis, True, me, pidx)
            pltpu.make_async_remote_copy(
                src_ref=src, dst_ref=got, send_sem=send1.at[k - 1], recv_sem=recv1.at[k - 1],
                device_id=peer, device_id_type=pl.DeviceIdType.MESH).wait_recv()
            pltpu.make_async_remote_copy(
                src_ref=got, dst_ref=got, send_sem=send2.at[j], recv_sem=recv2.at[j],
                device_id=sibling, device_id_type=pl.DeviceIdType.MESH).start()

    src, land = xc["srcs"][0], xc["lands"][0]
    outs = pl.pallas_call(
        body, name=name,
        out_shape=[pltpu.HBM(src.shape, src.dtype), pltpu.HBM(land.shape, land.dtype)]
        + [pltpu.SemaphoreType.DMA((len(SAME_CORE),))] * 2,
        in_specs=[_HBM, _HBM, _SEM, _SEM, pl.BlockSpec(memory_space=pl.ANY)],
        out_specs=[_HBM, _HBM, _SEM, _SEM], input_output_aliases={0: 0, 1: 1},
        compiler_params=pltpu.CompilerParams(has_side_effects=_EFFECT),
    )(src, land, xc["send"][0], xc["recv"][0], after)
    return dict(src=outs[0], land=outs[1], send=outs[2], recv=outs[3])


def _gather_relay_wait(xc, relay, after, *, name):
    axis = xc["axes"][0]

    def body(src_ref, land_ref, send1, recv1, send2, recv2, after_ref, src_out, land_out):
        x, y, c, me = _mesh_place()
        sibling, sidx = _peer(x, y, c, SIBLING)
        for k in (SIBLING,) + SAME_CORE:
            peer, pidx = _peer(x, y, c, k)
            src, dst, got = _xchg_ends(src_ref, land_ref, axis, True, me, pidx)
            pltpu.make_async_remote_copy(
                src_ref=src, dst_ref=dst, send_sem=send1.at[k - 1], recv_sem=recv1.at[k - 1],
                device_id=peer, device_id_type=pl.DeviceIdType.MESH).wait_send()
        src, dst, got = _xchg_ends(src_ref, land_ref, axis, True, me, sidx)
        pltpu.make_async_remote_copy(
            src_ref=src, dst_ref=got, send_sem=send1.at[SIBLING - 1], recv_sem=recv1.at[SIBLING - 1],
            device_id=sibling, device_id_type=pl.DeviceIdType.MESH).wait_recv()
        src, dst, _ = _xchg_ends(src_ref, land_ref, axis, True, me, me)
        pltpu.make_async_copy(src, dst, send1.at[N_DEV - 1]).wait()
        for j, k in enumerate(SAME_CORE):
            _, pidx = _peer(x, y, c, k)
            _, qidx = _peer(x, y, c, k ^ SIBLING)
            _, _, sent = _xchg_ends(src_ref, land_ref, axis, True, me, pidx)
            _, _, got = _xchg_ends(src_ref, land_ref, axis, True, me, qidx)
            pltpu.make_async_remote_copy(
                src_ref=sent, dst_ref=sent, send_sem=send2.at[j], recv_sem=recv2.at[j],
                device_id=sibling, device_id_type=pl.DeviceIdType.MESH).wait_send()
            pltpu.make_async_remote_copy(
                src_ref=got, dst_ref=got, send_sem=send2.at[j], recv_sem=recv2.at[j],
                device_id=sibling, device_id_type=pl.DeviceIdType.MESH).wait_recv()

    outs = pl.pallas_call(
        body, name=name, out_shape=[pltpu.HBM(relay["src"].shape, relay["src"].dtype),
                                    pltpu.HBM(relay["land"].shape, relay["land"].dtype)],
        in_specs=[_HBM, _HBM, _SEM, _SEM, _SEM, _SEM, pl.BlockSpec(memory_space=pl.ANY)],
        out_specs=[_HBM, _HBM], input_output_aliases={0: 0, 1: 1},
        compiler_params=pltpu.CompilerParams(has_side_effects=_EFFECT),
    )(relay["src"], relay["land"], xc["send"][0], xc["recv"][0], relay["send"], relay["recv"], after)
    return outs[1]


def _xchg_wait(xc, items, after, *, name):
    m = len(items)
    gather = xc["gather"]
    axes = [xc["axes"][i] for i in items]

    def body(*refs):
        src_refs, land_refs = refs[:m], refs[m:2 * m]
        send, recv = refs[2 * m:3 * m], refs[3 * m:4 * m]
        x, y, c, me = _mesh_place()
        for k in range(1, N_DEV):
            peer, pidx = _peer(x, y, c, k)
            for j in range(m):
                src, dst, got = _xchg_ends(src_refs[j], land_refs[j], axes[j], gather, me, pidx)
                pltpu.make_async_remote_copy(
                    src_ref=src, dst_ref=dst, send_sem=send[j].at[k - 1], recv_sem=recv[j].at[k - 1],
                    device_id=peer, device_id_type=pl.DeviceIdType.MESH).wait_send()
                pltpu.make_async_remote_copy(
                    src_ref=src, dst_ref=got, send_sem=send[j].at[k - 1], recv_sem=recv[j].at[k - 1],
                    device_id=peer, device_id_type=pl.DeviceIdType.MESH).wait_recv()
        for j in range(m):
            src, dst, _ = _xchg_ends(src_refs[j], land_refs[j], axes[j], gather, me, me)
            pltpu.make_async_copy(src, dst, send[j].at[N_DEV - 1]).wait()

    bufs = [xc["srcs"][i] for i in items] + [xc["lands"][i] for i in items]
    sems = [xc["send"][i] for i in items] + [xc["recv"][i] for i in items]
    outs = pl.pallas_call(
        body, name=name, out_shape=[pltpu.HBM(b.shape, b.dtype) for b in bufs],
        in_specs=[_HBM] * (2 * m) + [_SEM] * (2 * m) + [pl.BlockSpec(memory_space=pl.ANY)],
        out_specs=[_HBM] * (2 * m), input_output_aliases={j: j for j in range(2 * m)},
        compiler_params=pltpu.CompilerParams(has_side_effects=_EFFECT),
    )(*bufs, *sems, after)
    return outs[m:]


def _cast_bf16(a, *, name):
    r, c = a.shape
    tr = min(r, 512)

    def body(a_ref, o_ref):
        o_ref[...] = a_ref[...].astype(BF16)

    spec = pl.BlockSpec((tr, c), lambda i: (i, 0))
    return pl.pallas_call(body, name=name, grid=(r // tr,), in_specs=[spec], out_specs=spec,
                          out_shape=jax.ShapeDtypeStruct((r, c), BF16), compiler_params=_params(("parallel",)))(a)


def _adamw(slabs, w, m, v, *, name):
    layers, r, c = w.shape
    tr = min(r, 256)

    def body(*refs):
        s_refs = refs[:layers]
        w_ref, m_ref, v_ref, g_ref, d_ref, mo_ref, vo_ref = refs[layers:]
        for l in range(layers):
            g = s_refs[l][0].astype(F32)
            for i in range(1, N_DEV):
                g = g + s_refs[l][i].astype(F32)
            m2 = ADAM_B1 * m_ref[l] + (1.0 - ADAM_B1) * g
            v2 = ADAM_B2 * v_ref[l] + (1.0 - ADAM_B2) * (g * g)
            m_hat = m2 / (1.0 - ADAM_B1 ** ADAM_STEP)
            v_hat = v2 / (1.0 - ADAM_B2 ** ADAM_STEP)
            g_ref[l] = g
            d_ref[l] = -ADAM_LR * (m_hat / (jnp.sqrt(v_hat) + ADAM_EPS) + ADAM_WD * w_ref[l])
            mo_ref[l] = m2
            vo_ref[l] = v2

    spec = pl.BlockSpec((layers, tr, c), lambda i: (0, i, 0))
    return pl.pallas_call(
        body, name=name, grid=(r // tr,),
        in_specs=[pl.BlockSpec((N_DEV, tr, c), lambda i: (0, i, 0))] * layers + [spec, spec, spec],
        out_specs=[spec] * 4, out_shape=[jax.ShapeDtypeStruct((layers, r, c), F32)] * 4,
        compiler_params=_params(("parallel",)),
    )(*slabs, w, m, v)


GATHER_AXIS = {"attn_w_in": 1, "attn_w_out": 0, "ffn_w_up0": 1, "ffn_w_down0": 0, "hgrn_w_in": 1, "hgrn_w_out": 0,
               "hgrn_norm_g": 1, "ffn_w_up1": 1, "ffn_w_down1": 0}
GATHER_STAGES = (("attn_w_in",), ("attn_w_out", "ffn_w_up0", "ffn_w_down0", "hgrn_w_in", "hgrn_w_out", "hgrn_norm_g"),
                 ("ffn_w_up1", "ffn_w_down1"))
GATHER_WAITS = ((("attn_w_in",), 0, 1), (("attn_w_out", "ffn_w_up0", "ffn_w_down0"), 1, 2),
                (("hgrn_w_in", "hgrn_w_out", "hgrn_norm_g"), 1, None), (("ffn_w_up1", "ffn_w_down1"), 2, None))
SCATTER_AXIS = dict(GATHER_AXIS, small=None)
BIG = ("attn_w_in", "attn_w_out", "hgrn_w_in", "hgrn_w_out", "ffn_w_up", "ffn_w_down")
SMALL = ("lb_logits", "ln_mix_g", "ln_mix_b", "ln_ffn_g", "ln_ffn_b")
SMALL_ROW = {"ln_mix_g": 0, "ln_mix_b": 2, "ln_ffn_g": 4, "ln_ffn_b": 6, "lb_logits": 8}
NORM_G_ROW = 10
LOSS_ROW = 11


def kernel(x, attn_w_in, attn_w_out, hgrn_w_in, hgrn_w_out, hgrn_norm_g, lb_logits, ln_mix_g, ln_mix_b, ln_ffn_g, ln_ffn_b, ffn_w_up, ffn_w_down, loss_target, m_attn_w_in, m_attn_w_out, m_hgrn_w_in, m_hgrn_w_out, m_hgrn_norm_g, m_lb_logits, m_ln_mix_g, m_ln_mix_b, m_ln_ffn_g, m_ln_ffn_b, m_ffn_w_up, m_ffn_w_down, v_attn_w_in, v_attn_w_out, v_hgrn_w_in, v_hgrn_w_out, v_hgrn_norm_g, v_lb_logits, v_ln_mix_g, v_ln_mix_b, v_ln_ffn_g, v_ln_ffn_b, v_ffn_w_up, v_ffn_w_down):
    wts = dict(attn_w_in=attn_w_in, attn_w_out=attn_w_out, hgrn_w_in=hgrn_w_in, hgrn_w_out=hgrn_w_out,
               hgrn_norm_g=hgrn_norm_g, lb_logits=lb_logits, ln_mix_g=ln_mix_g, ln_mix_b=ln_mix_b, ln_ffn_g=ln_ffn_g,
               ln_ffn_b=ln_ffn_b, ffn_w_up=ffn_w_up, ffn_w_down=ffn_w_down)
    mom = dict(attn_w_in=m_attn_w_in, attn_w_out=m_attn_w_out, hgrn_w_in=m_hgrn_w_in, hgrn_w_out=m_hgrn_w_out,
               hgrn_norm_g=m_hgrn_norm_g, lb_logits=m_lb_logits, ln_mix_g=m_ln_mix_g, ln_mix_b=m_ln_mix_b,
               ln_ffn_g=m_ln_ffn_g, ln_ffn_b=m_ln_ffn_b, ffn_w_up=m_ffn_w_up, ffn_w_down=m_ffn_w_down)
    vel = dict(attn_w_in=v_attn_w_in, attn_w_out=v_attn_w_out, hgrn_w_in=v_hgrn_w_in, hgrn_w_out=v_hgrn_w_out,
               hgrn_norm_g=v_hgrn_norm_g, lb_logits=v_lb_logits, ln_mix_g=v_ln_mix_g, ln_mix_b=v_ln_mix_b,
               ln_ffn_g=v_ln_ffn_g, ln_ffn_b=v_ln_ffn_b, ffn_w_up=v_ffn_w_up, ffn_w_down=v_ffn_w_down)
    me = 4 * lax.axis_index("x") + 2 * lax.axis_index("y") + lax.axis_index("c")

    src = {"attn_w_in": attn_w_in[0], "attn_w_out": attn_w_out[0], "hgrn_w_in": hgrn_w_in[0], "hgrn_w_out": hgrn_w_out[0],
           "ffn_w_up0": ffn_w_up[0], "ffn_w_down0": ffn_w_down[0], "ffn_w_up1": ffn_w_up[1], "ffn_w_down1": ffn_w_down[1]}
    gathers, got = {}, {}

    def start_gather(stage, deps):
        shards, lands = [], []
        for nm in GATHER_STAGES[stage]:
            sh = hgrn_norm_g if nm == "hgrn_norm_g" else _cast_bf16(src[nm], name=f"cast_{nm}")
            ax = GATHER_AXIS[nm]
            shape = list(sh.shape)
            shape[ax] *= N_DEV
            shards.append(sh)
            lands.append(lax.empty(tuple(shape), sh.dtype))
        peers = (SIBLING,) + SAME_CORE if stage == 0 else tuple(range(1, N_DEV))
        gathers[stage] = _xchg_start(shards, lands, [GATHER_AXIS[nm] for nm in GATHER_STAGES[stage]], gather=True,
                                     name=f"gather_start_{stage}", deps=deps, peers=peers)
        return [gathers[stage]["token"]]

    def get_w(name, after):
        deps = []
        if name not in got:
            group, stage, then = [w for w in GATHER_WAITS if name in w[0]][0]
            xc = gathers[stage]
            if stage == 0:
                relay = _gather_relay(xc, xc["token"], name="gather_relay")
                res = [_gather_relay_wait(xc, relay, xc["token"], name=f"gather_wait_{group[0]}")]
            else:
                res = _xchg_wait(xc, [GATHER_STAGES[stage].index(nm) for nm in group], after,
                                 name=f"gather_wait_{group[0]}")
            got.update(zip(group, res))
            if then is not None:
                deps = start_gather(then, [res[0]])
        return got[name], deps

    first = start_gather(0, [])

    scattered = {}

    def on_grads(tag, grads):
        gnames = list(grads)
        axes = [SCATTER_AXIS[nm] for nm in gnames]
        stacks = []
        for nm, ax in zip(gnames, axes):
            shape = list(grads[nm].shape)
            if ax is not None:
                shape[ax] //= N_DEV
            stacks.append(lax.empty((N_DEV, *shape), grads[nm].dtype))
        scattered[tag] = (gnames, _xchg_start([grads[nm] for nm in gnames], stacks, axes, gather=False,
                                              name=f"scatter_start_{tag}"))
        return [scattered[tag][1]["token"]]

    sm = jax.nn.softmax(lb_logits, axis=0)
    csum = jnp.cumsum(sm, axis=0)
    small = dict(lb=(csum - csum[0:1])[1:2], lb_logits=lb_logits, ln_mix_g=ln_mix_g, ln_mix_b=ln_mix_b,
                 ln_ffn_g=ln_ffn_g, ln_ffn_b=ln_ffn_b)
    grad_x = _local_step(x[0], loss_target[0], get_w, small, on_grads, deps=first)
    out = {}

    def stack_small(src_):
        rows = [None] * SMALL_ROWS
        for name in SMALL:
            rows[SMALL_ROW[name]], rows[SMALL_ROW[name] + 1] = src_[name][0:1], src_[name][1:2]
        zero = jnp.zeros((1, x.shape[-1]), F32)
        return jnp.concatenate([zero if r is None else r for r in rows], axis=0)[None]

    def update(name, slabs):
        shape = wts[name].shape
        out[name] = [r.reshape(shape) for r in _adamw(slabs, wts[name], mom[name], vel[name], name=f"adamw_{name}")]
        return out[name][0]

    slabs, after = {}, grad_x
    for tag, (gnames, xc) in scattered.items():
        slabs.update(zip(gnames, _xchg_wait(xc, list(range(len(gnames))), after, name=f"scatter_wait_{tag}")))
        if tag == "ffn1":
            continue
        if tag == "ffn0":
            update("ffn_w_down", [slabs["ffn_w_down0"], slabs["ffn_w_down1"]])
            after = update("ffn_w_up", [slabs["ffn_w_up0"], slabs["ffn_w_up1"]])
        elif tag == "hgrn":
            update("hgrn_w_out", [slabs["hgrn_w_out"]])
            after = update("hgrn_w_in", [slabs["hgrn_w_in"]])
        elif tag == "attn_out":
            after = update("attn_w_out", [slabs["attn_w_out"]])
        else:
            after = update("attn_w_in", [slabs["attn_w_in"]])
    res = _adamw([slabs["small"]], stack_small(wts), stack_small(mom), stack_small(vel), name="adamw_small")
    for name in SMALL:
        out[name] = [r[0, SMALL_ROW[name]:SMALL_ROW[name] + 2] for r in res]
    loss = res[0][0, LOSS_ROW, 0]
    ng = hgrn_norm_g.shape[-1]
    ng_slabs = lax.dynamic_slice(slabs["small"], (0, NORM_G_ROW, me * ng), (N_DEV, 1, ng))
    out["hgrn_norm_g"] = [r[0] for r in _adamw([ng_slabs], hgrn_norm_g[None], m_hgrn_norm_g[None],
                                                v_hgrn_norm_g[None], name="adamw_norm_g")]
    order =("attn_w_in", "attn_w_out", "hgrn_w_in", "hgrn_w_out", "hgrn_norm_g", "lb_logits", "ln_mix_g", "ln_mix_b",
             "ln_ffn_g", "ln_ffn_b", "ffn_w_up", "ffn_w_down")
    return (loss, grad_x[None], *[out[nm][0] for nm in order], *[out[nm][1] for nm in order],
            *[out[nm][2] for nm in order], *[out[nm][3] for nm in order])
```

```python
import jax
import jax.numpy as jnp
from jax import lax
from jax.experimental import pallas as pl
from jax.experimental.pallas import tpu as pltpu

F32 = jnp.float32
BF16 = jnp.bfloat16

N_DEV = 8
LANES = 128
D_MODEL = 1024
ATTN_HEAD_DIM = 64
ATTN_HEADS = 16
ATTN_SCALE = ATTN_HEAD_DIM ** -0.5
ATTN_BLK = 128
DILATIONS = (1, 4, 16)
ROPE_THETA = 10000.0
HGRN_HEADS = 8
HGRN_CHUNK = 64
D_FF = 4096
LN_EPS = 1e-5
RMS_EPS = 1e-6
DEPTH = 2
ALPHA = (2 * DEPTH) ** 0.25
ADAM_LR, ADAM_B1, ADAM_B2, ADAM_EPS, ADAM_WD, ADAM_STEP = 0.001, 0.9, 0.999, 1e-08, 0.01, 10
VMEM_LIMIT = 48 * 1024 * 1024

_NT = (((1,), (1,)), ((), ()))
_TN = (((0,), (0,)), ((), ()))


def _dot(a, b):
    return jnp.dot(a, b, preferred_element_type=F32)


def _dot_nt(a, b):
    return lax.dot_general(a, b, _NT, preferred_element_type=F32)


def _dot_tn(a, b):
    return lax.dot_general(a, b, _TN, preferred_element_type=F32)


def _split3(x):
    p1 = x.astype(BF16)
    r1 = x - p1.astype(F32)
    p2 = r1.astype(BF16)
    p3 = (r1 - p2.astype(F32)).astype(BF16)
    return p1, p2, p3


def _exact_dot(sel3, x):
    return _dot(sel3, jnp.concatenate(_split3(x), axis=0))


def _exact_dot_r(x, sel3):
    return _dot(jnp.concatenate(_split3(x), axis=1), sel3)


def _params(sem=None):
    return pltpu.CompilerParams(dimension_semantics=sem, vmem_limit_bytes=VMEM_LIMIT)


def _mm(a, b, mode, *, name, m, n, k, tm=1024, tn=1024, tk=1024, out_dtypes=(F32,), epi=None, a_pre=None,
        tile_extras=(), row_extras=(), vec_extras=(), row_outs=(), vec_outs=0, a_split=1, b_split=1, out_split=1,
        b_col_off=0, b_k_off=0, out_col_off=0, out_cols=None, alias=None, epi_wants_j=False, deps=(), t_out=False):
    tm, tn, tk = min(tm, m), min(tn, n), min(tk, k)
    assert m % tm == 0 and n % tn == 0 and k % tk == 0, (name, m, n, k, tm, tn, tk)
    gm, gn, gk = m // tm, n // tn, k // tk
    if mode in ("nn", "nt"):
        if a_split > 1 and tk == k:
            a_spec = pl.BlockSpec((a_split, tm, k // a_split), lambda i, j, kk: (0, i, 0))
        elif a_split > 1:
            kc = (k // a_split) // tk
            a_spec = pl.BlockSpec((None, tm, tk), lambda i, j, kk: (kk // kc, i, kk % kc))
        else:
            a_spec = pl.BlockSpec((tm, tk), lambda i, j, kk: (i, kk))
    else:
        a_spec = pl.BlockSpec((tk, tm), lambda i, j, kk: (kk, i))
    if mode in ("nn", "tn"):
        if b_split > 1:
            nc = (n // b_split) // tn
            b_spec = pl.BlockSpec((None, tk, tn), lambda i, j, kk: (j // nc, kk, j % nc))
        else:
            b_spec = pl.BlockSpec((tk, tn), lambda i, j, kk: (kk + b_k_off, j + b_col_off))
    else:
        b_spec = pl.BlockSpec((tn, tk), lambda i, j, kk: (j + b_col_off, kk + b_k_off))
    if out_split > 1:
        nco = (n // out_split) // tn
        o_spec = pl.BlockSpec((None, tm, tn), lambda i, j, kk: (j // nco, i, j % nco))
        o_shape = (out_split, m, n // out_split)
    else:
        o_spec = pl.BlockSpec((tm, tn), lambda i, j, kk: (i, j + out_col_off))
        o_shape = (m, out_cols if out_cols is not None else n)
    n_ex = len(tile_extras) + len(row_extras) + len(vec_extras)
    n_out = len(out_dtypes)
    n_plain = n_out + len(row_outs)
    assert not vec_outs or gn == 1
    if epi is None:
        def epi(acc):
            return (acc,)
    dot = {"nn": _dot, "nt": _dot_nt, "tn": _dot_tn}[mode]

    def body(*refs):
        a_ref, b_ref = refs[0], refs[1]
        ex = refs[2:2 + n_ex]
        outs = refs[2 + n_ex + (1 if alias is not None else 0) + len(deps):][:n_plain + vec_outs + (1 if t_out else 0)]
        ii = pl.program_id(0)
        jj = pl.program_id(1)

        def product():
            if a_split > 1 and tk == k:
                av = jnp.concatenate([a_ref[p] for p in range(a_split)], axis=1)
            else:
                av = a_ref[...]
            if a_pre is not None:
                av = a_pre(av)
            return dot(av.astype(BF16), b_ref[...].astype(BF16))

        def finish(total):
            lead = (jj,) if epi_wants_j else ()
            res = epi(*lead, total, *[e[...] for e in ex])
            for o, r in zip(outs[:n_plain], res):
                o[...] = r.astype(o.dtype)
            for o, r in zip(outs[n_plain:n_plain + vec_outs], res[n_plain:]):
                @pl.when(ii == 0)
                def _(o=o, r=r):
                    o[...] = r

                @pl.when(ii > 0)
                def _(o=o, r=r):
                    o[...] += r
            if t_out:
                outs[-1][...] = res[-1].astype(outs[-1].dtype)

        if gk == 1:
            finish(product())
        else:
            acc = refs[-1]
            kk = pl.program_id(2)

            @pl.when(kk == 0)
            def _():
                acc[...] = product()

            @pl.when(kk > 0)
            def _():
                acc[...] += product()

            @pl.when(kk == gk - 1)
            def _():
                finish(acc[...])

    in_specs = [a_spec, b_spec] + [o_spec] * len(tile_extras)
    in_specs += [pl.BlockSpec((tm, r.shape[1]), lambda i, j, kk: (i, 0)) if r.shape[0] == m else
                 pl.BlockSpec((r.shape[0], tm), lambda i, j, kk: (0, i)) for r in row_extras]
    in_specs += [pl.BlockSpec((1, tn), lambda i, j, kk: (0, j))] * len(vec_extras)
    args = [a, b] + list(tile_extras) + list(row_extras) + list(vec_extras)
    io_alias = {}
    if alias is not None:
        in_specs.append(pl.BlockSpec(memory_space=pl.ANY))
        args.append(alias)
        io_alias = {len(args) - 1: 0}
    in_specs += [pl.BlockSpec(memory_space=pl.ANY)] * len(deps)
    args += list(deps)
    out_specs = [o_spec] * n_out
    out_shape = [jax.ShapeDtypeStruct(o_shape, dt) for dt in out_dtypes]
    for dt, w in row_outs:
        out_specs.append(pl.BlockSpec((tm, w), lambda i, j, kk: (i, 0)))
        out_shape.append(jax.ShapeDtypeStruct((m, w), dt))
    out_specs += [pl.BlockSpec((1, tn), lambda i, j, kk: (0, j))] * vec_outs
    out_shape += [jax.ShapeDtypeStruct((1, n), F32)] * vec_outs
    if t_out:
        assert out_split > 1
        out_specs.append(pl.BlockSpec((None, tn, tm), lambda i, j, kk: (j // nco, j % nco, i)))
        out_shape.append(jax.ShapeDtypeStruct((out_split, n // out_split, m), out_dtypes[0]))
    out = pl.pallas_call(
        body, name=name, grid=(gm, gn, gk), in_specs=in_specs, out_specs=out_specs, out_shape=out_shape,
        scratch_shapes=[pltpu.VMEM((tm, tn), F32)] if gk > 1 else [],
        input_output_aliases=io_alias,
        compiler_params=_params(("arbitrary" if vec_outs else "parallel", "parallel", "arbitrary")),
    )(*args)
    return out[0] if len(out) == 1 else out


def _rope_tables(seq, dil):
    pos = jnp.arange(seq, dtype=jnp.int32).reshape(seq // dil, dil).T.reshape(seq)
    half = ATTN_HEAD_DIM // 2
    inv = ROPE_THETA ** (-jnp.arange(half, dtype=F32) * (2.0 / ATTN_HEAD_DIM))
    ang = pos.astype(F32)[:, None] * inv[None, :]
    cos, sin = jnp.cos(ang), jnp.sin(ang)
    reps = LANES // ATTN_HEAD_DIM
    return (jnp.tile(jnp.concatenate([cos, cos], axis=1), (1, reps)),
            jnp.tile(jnp.concatenate([-sin, sin], axis=1), (1, reps)))


def _rotate(x, c, ss, sign=1.0):
    w = x.shape[-1]
    half = ATTN_HEAD_DIM // 2
    lane = lax.broadcasted_iota(jnp.int32, x.shape, 1)
    first = (lane % ATTN_HEAD_DIM) < half
    partner = jnp.where(first, pltpu.roll(x, w - half, 1), pltpu.roll(x, half, 1))
    reps = w // LANES
    if reps > 1:
        c = jnp.concatenate([c] * reps, axis=1)
        ss = jnp.concatenate([ss] * reps, axis=1)
    return x * c + sign * (partner * ss)


def _rotate_t(xt, ct, sst):
    half = ATTN_HEAD_DIM // 2
    heads = xt.shape[0] // ATTN_HEAD_DIM
    parts = []
    for h in range(heads):
        lo = h * ATTN_HEAD_DIM
        parts += [xt[lo + half:lo + 2 * half], xt[lo:lo + half]]
    return (xt * jnp.concatenate([ct] * heads, axis=0)
            + jnp.concatenate(parts, axis=0) * jnp.concatenate([sst] * heads, axis=0))


def _ln_epi(acc, x, g, b):
    u = ALPHA * x + acc
    mu = jnp.mean(u, axis=-1, keepdims=True)
    uc = u - mu
    var = jnp.mean(uc * uc, axis=-1, keepdims=True)
    rstd = lax.rsqrt(var + LN_EPS)
    xh = uc * rstd
    out = xh * g + b
    return out, out, xh, rstd


def _ln_grad(dy, xh, rstd, g):
    dxh = dy * g
    m1 = jnp.mean(dxh, axis=-1, keepdims=True)
    m2 = jnp.mean(dxh * xh, axis=-1, keepdims=True)
    du = rstd * (dxh - m1 - xh * m2)
    return du, du, jnp.sum(dy * xh, axis=0, keepdims=True), jnp.sum(dy, axis=0, keepdims=True)


def _ln_grad_epi(acc, du_next, xh, rstd, g):
    return _ln_grad(acc + ALPHA * du_next, xh, rstd, g)


def _ln_loss_epi(acc, x_in, target, g, b):
    out, _, xh, rstd = _ln_epi(acc, x_in, g, b)
    e = out - target
    du, _, dg, db = _ln_grad(e * (1.0 / e.shape[-1]), xh, rstd, g)
    return du, du, dg, db, jnp.sum(e * e, axis=0, keepdims=True)


POS_BLK = 2048


def _class_rows(r, dil):
    return pl.ds(r, POS_BLK // dil, stride=dil) if dil > 1 else pl.ds(0, POS_BLK)


def _class_view(a, dil):
    s, w = a.shape
    return a.reshape(dil, s // dil, w)


def _class_spec(dil, all_tiles=True):
    if all_tiles:
        return pl.BlockSpec((dil, POS_BLK // dil, LANES), lambda i, t: (0, i, t))
    return pl.BlockSpec((dil, POS_BLK // dil, LANES), lambda i, t: (0, i, 0))


def _pos_spec(all_tiles=True):
    if all_tiles:
        return pl.BlockSpec((POS_BLK, LANES), lambda i, t: (i, t))
    return pl.BlockSpec((POS_BLK, LANES), lambda i, t: (i, 0))


def _prep_x(x, deps=()):
    s, d = x.shape

    def body(x_ref, *refs):
        outs = refs[len(deps):]
        for dil, o_ref in zip(DILATIONS, outs):
            for r in range(dil):
                o_ref[r] = x_ref[_class_rows(r, dil), :].astype(BF16)

    outs = pl.pallas_call(
        body, name="prep_x", grid=(s // POS_BLK, d // LANES),
        in_specs=[_pos_spec()] + [pl.BlockSpec(memory_space=pl.ANY)] * len(deps),
        out_specs=[_class_spec(dil) for dil in DILATIONS],
        out_shape=[jax.ShapeDtypeStruct((dil, s // dil, d), BF16) for dil in DILATIONS],
        compiler_params=_params(("parallel", "parallel")),
    )(x, *deps)
    return [o.reshape(s, d) for o in outs]


def _head_expand_matrix():
    h = lax.broadcasted_iota(jnp.int32, (LANES, D_MODEL), 0)
    l = lax.broadcasted_iota(jnp.int32, (LANES, D_MODEL), 1)
    return (l // ATTN_HEAD_DIM == h).astype(BF16)


def _attn_fwd(qkv, dil, *, name):
    _, s, d = qkv.shape
    nq = s // ATTN_BLK
    per = nq // dil
    tiles = d // LANES

    def body(q_ref, kc_ref, kp_ref, vc_ref, vp_ref, o_ref, lse_ref):
        qb = pl.program_id(0)
        first = (qb % per) == 0
        qi = lax.broadcasted_iota(jnp.int32, (ATTN_BLK, 2 * ATTN_BLK), 0)
        kj = lax.broadcasted_iota(jnp.int32, (ATTN_BLK, 2 * ATTN_BLK), 1)
        dist = qi + ATTN_BLK - kj
        valid = (dist >= 0) & (dist <= ATTN_BLK) & ((kj >= ATTN_BLK) | jnp.logical_not(first))
        lane = lax.broadcasted_iota(jnp.int32, (ATTN_BLK, LANES), 1)
        lse_tile = jnp.zeros((ATTN_BLK, LANES), F32)
        zero = jnp.zeros((), BF16)
        in_head = [(lane // ATTN_HEAD_DIM) == hh for hh in range(2)]
        for t0 in range(0, tiles, ATTN_FWD_TILE_GROUP):
            group = range(t0, t0 + ATTN_FWD_TILE_GROUP)
            heads = [(t, hh) for t in group for hh in range(2)]
            cols = {t: pl.ds(t * LANES, LANES) for t in group}
            k2 = {t: jnp.concatenate([kp_ref[:, cols[t]], kc_ref[:, cols[t]]], axis=0) for t in group}
            v2 = {t: jnp.concatenate([vp_ref[:, cols[t]], vc_ref[:, cols[t]]], axis=0) for t in group}
            sc = {(t, hh): jnp.where(valid, _dot_nt(jnp.where(in_head[hh], q_ref[:, cols[t]], zero), k2[t]),
                                     -jnp.inf) for t, hh in heads}
            mx = {i: jnp.max(sc[i], axis=-1, keepdims=True) for i in heads}
            p = {i: jnp.exp(sc[i] - mx[i]) for i in heads}
            l = {i: jnp.sum(p[i], axis=-1, keepdims=True) for i in heads}
            oh = {i: _dot(p[i].astype(BF16), v2[i[0]]) / l[i] for i in heads}
            for t in group:
                o_ref[:, cols[t]] = jnp.where(in_head[0], oh[t, 0], oh[t, 1])
                for hh in range(2):
                    lse_tile = jnp.where(lane == 2 * t + hh, mx[t, hh] + jnp.log(l[t, hh]), lse_tile)
        lse_ref[...] = lse_tile

    def blk(piece, prev):
        if prev:
            return pl.BlockSpec((None, ATTN_BLK, d), lambda i: (piece, jnp.maximum(i - 1, 0), 0))
        return pl.BlockSpec((None, ATTN_BLK, d), lambda i: (piece, i, 0))

    return pl.pallas_call(
        body, name=name, grid=(nq,),
        in_specs=[blk(0, False), blk(1, False), blk(1, True), blk(2, False), blk(2, True)],
        out_specs=[pl.BlockSpec((ATTN_BLK, d), lambda i: (i, 0)), pl.BlockSpec((ATTN_BLK, LANES), lambda i: (i, 0))],
        out_shape=[jax.ShapeDtypeStruct((s, d), F32), jax.ShapeDtypeStruct((s, LANES), F32)],
        compiler_params=_params(("parallel",)),
    )(qkv, qkv, qkv, qkv, qkv)


def _attn_combine(os_, lses):
    s, d = os_[0].shape
    sel = jnp.concatenate([_head_expand_matrix()] * 3, axis=0)

    def body(o0, o1, o2, l0, l1, l2, sel_ref, of_ref, ob_ref, lt_ref, o_pos, l_pos):
        for g, (dil, o_ref, l_ref) in enumerate(zip(DILATIONS, (o0, o1, o2), (l0, l1, l2))):
            for r in range(dil):
                o_pos[g, _class_rows(r, dil), :] = o_ref[r]
                l_pos[g, _class_rows(r, dil), :] = l_ref[r]
        la, lb_, lc = l_pos[0], l_pos[1], l_pos[2]
        mx = jnp.maximum(jnp.maximum(la, lb_), lc)
        es = (jnp.exp(la - mx), jnp.exp(lb_ - mx), jnp.exp(lc - mx))
        z = es[0] + es[1] + es[2]
        lt_ref[...] = mx + jnp.log(z)
        acc = jnp.zeros((POS_BLK, LANES), F32)
        for g in range(3):
            acc += _exact_dot_r(es[g] / z, sel_ref[...]) * o_pos[g]
        of_ref[...] = acc
        ob_ref[...] = acc.astype(BF16)

    return pl.pallas_call(
        body, name="attn_combine", grid=(s // POS_BLK, d // LANES),
        in_specs=[_class_spec(dil) for dil in DILATIONS] + [_class_spec(dil, False) for dil in DILATIONS]
        + [pl.BlockSpec((3 * LANES, LANES), lambda i, t: (0, t))],
        out_specs=[_pos_spec(), _pos_spec(), _pos_spec(False)],
        out_shape=[jax.ShapeDtypeStruct((s, d), F32), jax.ShapeDtypeStruct((s, d), BF16),
                   jax.ShapeDtypeStruct((s, LANES), F32)],
        scratch_shapes=[pltpu.VMEM((3, POS_BLK, LANES), F32), pltpu.VMEM((3, POS_BLK, LANES), F32)],
        compiler_params=_params(("parallel", "arbitrary")),
    )(*[_class_view(o, dil) for o, dil in zip(os_, DILATIONS)],
      *[_class_view(l, dil) for l, dil in zip(lses, DILATIONS)], sel)


ATTN_TILE_GROUP = 2
ATTN_FWD_TILE_GROUP = 4
AUX_PER_TILE = 12


def _aux_placement():
    h = lax.broadcasted_iota(jnp.int32, (6, LANES, LANES), 1)
    l = lax.broadcasted_iota(jnp.int32, (6, LANES, LANES), 2)
    j = lax.broadcasted_iota(jnp.int32, (6, LANES, LANES), 0)
    target = AUX_PER_TILE * (h // 2) + 3 * (h % 2) + jnp.where(j < 3, j, 3 + j)
    return ((l == target) & (h < ATTN_HEADS)).astype(BF16)


def _attn_bwd_prep(do, o, lse):
    s, d = do.shape
    tiles = d // LANES
    sel_t = jnp.tile(_head_expand_matrix().T.reshape(tiles, LANES, LANES), (1, 3, 1))

    def body(do_ref, o_ref, l_ref, sel_ref, place_ref, *refs):
        outs, delta, aux = refs[:9], refs[9], refs[10]
        t = pl.program_id(1)
        part = _exact_dot_r(do_ref[...] * o_ref[...], sel_ref[...])

        @pl.when(t == 0)
        def _():
            delta[...] = part

        @pl.when(t > 0)
        def _():
            delta[...] += part

        for g, dil in enumerate(DILATIONS):
            for r in range(dil):
                blk = do_ref[_class_rows(r, dil), :].astype(BF16)
                outs[g][r] = blk
                outs[3 + g][r] = blk.T

        @pl.when(t == tiles - 1)
        def _():
            pieces = _split3(l_ref[...]) + _split3(delta[...])
            aux[...] = _dot(jnp.concatenate(pieces, axis=1), place_ref[...])
            for g, dil in enumerate(DILATIONS):
                for r in range(dil):
                    outs[6 + g][r] = aux[_class_rows(r, dil), :].astype(BF16)

    outs = pl.pallas_call(
        body, name="attn_bwd_prep", grid=(s // POS_BLK, tiles),
        in_specs=[_pos_spec(), _pos_spec(), _pos_spec(False),
                  pl.BlockSpec((None, 3 * LANES, LANES), lambda i, t: (t, 0, 0)),
                  pl.BlockSpec((6 * LANES, LANES), lambda i, t: (0, 0))],
        out_specs=[_class_spec(dil) for dil in DILATIONS]
        + [pl.BlockSpec((dil, LANES, POS_BLK // dil), lambda i, t: (0, t, i)) for dil in DILATIONS]
        + [_class_spec(dil, False) for dil in DILATIONS],
        out_shape=[jax.ShapeDtypeStruct((dil, s // dil, d), BF16) for dil in DILATIONS]
        + [jax.ShapeDtypeStruct((dil, d, s // dil), BF16) for dil in DILATIONS]
        + [jax.ShapeDtypeStruct((dil, s // dil, LANES), BF16) for dil in DILATIONS],
        scratch_shapes=[pltpu.VMEM((POS_BLK, LANES), F32), pltpu.VMEM((POS_BLK, LANES), F32)],
        compiler_params=_params(("parallel", "arbitrary")),
    )(do, o, lse, sel_t, _aux_placement().reshape(6 * LANES, LANES))
    return ([a.reshape(s, d) for a in outs[0:3]], list(outs[3:6]), [a.reshape(s, LANES) for a in outs[6:9]])


def _attn_bwd(qkv, qkv_t, do, do_t, aux, tables, dil, *, name):
    _, s, d = qkv.shape
    nq = s // ATTN_BLK
    per = nq // dil
    tiles = d // LANES
    half = ATTN_HEAD_DIM

    def body(qd_ref, qo_ref, k_ref, qtd_ref, qto_ref, kt_ref, vt_ref, dod_ref, doo_ref, dotd_ref, doto_ref,
             auxd_ref, auxo_ref, c_ref, ss_ref, out_ref, carry):
        kb = pl.program_id(0)

        @pl.when(kb == 0)
        def _():
            carry[...] = jnp.zeros_like(carry)

        has_next = (kb % per) != (per - 1)
        qi = lax.broadcasted_iota(jnp.int32, (ATTN_BLK, 2 * ATTN_BLK), 0)
        kj = lax.broadcasted_iota(jnp.int32, (ATTN_BLK, 2 * ATTN_BLK), 1) % ATTN_BLK
        valid_d = kj <= qi
        valid_o = (kj >= qi) & has_next
        lane = lax.broadcasted_iota(jnp.int32, (ATTN_BLK, LANES), 1)
        row = lax.broadcasted_iota(jnp.int32, (LANES, ATTN_BLK), 0)
        side = lax.broadcasted_iota(jnp.int32, (LANES, 2 * ATTN_BLK), 0)
        first = lax.broadcasted_iota(jnp.int32, (LANES, 2 * ATTN_BLK), 1) < ATTN_BLK
        c, ss = c_ref[...], ss_ref[...]
        zero = jnp.zeros((), BF16)
        sides = ((qd_ref, qtd_ref, dod_ref, dotd_ref, auxd_ref[...], valid_d),
                 (qo_ref, qto_ref, doo_ref, doto_ref, auxo_ref[...], valid_o))
        both = (0, 1)

        def head_halves(x, index):
            axis = 0 if index is lane else 1
            return jnp.concatenate([jnp.where(index < half, x, zero), jnp.where(index >= half, x, zero)], axis=axis)

        for t0 in range(0, tiles, ATTN_TILE_GROUP):
            group = range(t0, t0 + ATTN_TILE_GROUP)
            cols = {t: pl.ds(t * LANES, LANES) for t in group}
            kk, kk_t, vv_t = {}, {}, {}
            for t in group:
                base = AUX_PER_TILE * t
                hit = lambda lo: ((first & (side >= base + lo) & (side < base + lo + 3))
                                  | (jnp.logical_not(first) & (side >= base + lo + 3) & (side < base + lo + 6)))
                kk[t] = head_halves(k_ref[:, cols[t]], lane)
                kk_t[t] = jnp.concatenate([head_halves(kt_ref[cols[t], :], row),
                                           jnp.where(hit(0), -1.0, 0.0).astype(BF16)], axis=0)
                vv_t[t] = jnp.concatenate([head_halves(vt_ref[cols[t], :], row),
                                           jnp.where(hit(6), -1.0, 0.0).astype(BF16)], axis=0)
            sc = {(t, w): _dot(jnp.concatenate([sides[w][0][:, cols[t]], sides[w][4]], axis=1), kk_t[t])
                  for t in group for w in both}
            dpd = {(t, w): _dot(jnp.concatenate([sides[w][2][:, cols[t]], sides[w][4]], axis=1), vv_t[t])
                   for t in group for w in both}
            p = {i: jnp.where(sides[i[1]][5], jnp.exp(sc[i]), 0.0) for i in sc}
            ds = {i: (p[i] * dpd[i]).astype(BF16) for i in sc}
            pb = {i: p[i].astype(BF16) for i in sc}
            dv_t = {t: sum(_dot(sides[w][3][cols[t], :], pb[t, w]) for w in both) for t in group}
            dk_t = {t: sum(_dot(sides[w][1][cols[t], :], ds[t, w]) for w in both) for t in group}
            dq = {i: _dot(ds[i], kk[i[0]]) * ATTN_SCALE for i in sc}
            for t in group:
                dq_now = carry[:, cols[t]] + dq[t, 0]
                carry[:, cols[t]] = dq[t, 1]
                dk = jnp.where(row < half, dk_t[t][:, :ATTN_BLK], dk_t[t][:, ATTN_BLK:]).T
                dv = jnp.where(row < half, dv_t[t][:, :ATTN_BLK], dv_t[t][:, ATTN_BLK:]).T
                out_ref[0, :, cols[t]] = _rotate(dq_now, c, ss, -1.0).astype(BF16)
                out_ref[1, :, cols[t]] = _rotate(dk, c, ss, -1.0).astype(BF16)
                out_ref[2, :, cols[t]] = dv.astype(BF16)

    def nxt(i):
        return jnp.minimum(i + 1, nq - 1)

    def piece(p, shift):
        if shift:
            return pl.BlockSpec((None, ATTN_BLK, d), lambda i: (p, nxt(i), 0))
        return pl.BlockSpec((None, ATTN_BLK, d), lambda i: (p, i, 0))

    def piece_t(p, shift):
        if shift:
            return pl.BlockSpec((None, d, ATTN_BLK), lambda i: (p, 0, nxt(i)))
        return pl.BlockSpec((None, d, ATTN_BLK), lambda i: (p, 0, i))

    def rows(width, shift):
        if shift:
            return pl.BlockSpec((ATTN_BLK, width), lambda i: (nxt(i), 0))
        return pl.BlockSpec((ATTN_BLK, width), lambda i: (i, 0))

    def do_t_spec(shift):
        if shift:
            return pl.BlockSpec((None, d, ATTN_BLK), lambda i: (nxt(i) // per, 0, nxt(i) % per))
        return pl.BlockSpec((None, d, ATTN_BLK), lambda i: (i // per, 0, i % per))

    return pl.pallas_call(
        body, name=name, grid=(nq,),
        in_specs=[piece(0, False), piece(0, True), piece(1, False),
                  piece_t(0, False), piece_t(0, True), piece_t(1, False), piece_t(2, False),
                  rows(d, False), rows(d, True), do_t_spec(False), do_t_spec(True),
                  rows(LANES, False), rows(LANES, True), rows(LANES, False), rows(LANES, False)],
        out_specs=pl.BlockSpec((3, ATTN_BLK, d), lambda i: (0, i, 0)),
        out_shape=jax.ShapeDtypeStruct((3, s, d), BF16),
        scratch_shapes=[pltpu.VMEM((ATTN_BLK, d), F32)],
        compiler_params=_params(("arbitrary",)),
    )(qkv, qkv, qkv, qkv_t, qkv_t, qkv_t, qkv_t, do, do, do_t, do_t, aux, aux, *tables)


def _dx_combine(du, parts):
    s, d = du.shape

    def body(du_ref, p0, p1, p2, out_ref):
        out_ref[...] = ALPHA * du_ref[...] + p0[0]
        for dil, p_ref in zip(DILATIONS[1:], (p1, p2)):
            for r in range(dil):
                out_ref[_class_rows(r, dil), :] += p_ref[r]

    return pl.pallas_call(
        body, name="dx_combine", grid=(s // POS_BLK, d // LANES),
        in_specs=[_pos_spec()] + [_class_spec(dil) for dil in DILATIONS], out_specs=_pos_spec(),
        out_shape=jax.ShapeDtypeStruct((s, d), F32),
        compiler_params=_params(("parallel", "parallel")),
    )(du, *[_class_view(p, dil) for p, dil in zip(parts, DILATIONS)])


HGRN_ROWS = 2048


def _tri(lower, copies=1):
    i = lax.broadcasted_iota(jnp.int32, (HGRN_CHUNK, HGRN_CHUNK * copies), 0)
    j = lax.broadcasted_iota(jnp.int32, (HGRN_CHUNK, HGRN_CHUNK * copies), 1) % HGRN_CHUNK
    return (j <= i) if lower else (j >= i)


def _hgrn_gates(qr, z, lb):
    sq = jax.nn.sigmoid(qr)
    e = jnp.exp(-jnp.abs(z))
    big = 1.0 / (1.0 + e)
    small = e * big
    sg = jnp.where(z >= 0, big, small)
    sn = jnp.where(z >= 0, small, big)
    return sq, qr * sq, sg, sn, lb + (1.0 - lb) * sg, (1.0 - lb) * sn


def _hgrn_fwd(p2, lb, norm_g):
    _, s, d = p2.shape
    nblk = s // HGRN_ROWS
    cps = HGRN_ROWS // HGRN_CHUNK

    def body(p_ref, lb_ref, g_ref, o_ref, on_ref, st_ref, state):
        @pl.when(pl.program_id(1) == 0)
        def _():
            state[...] = jnp.zeros_like(state)

        chunks = [pl.ds(c * HGRN_CHUNK, HGRN_CHUNK) for c in range(cps)]
        ltri = _tri(True)
        lsel = _tri(True, 3).astype(BF16)
        _, q, _, _, f, key = _hgrn_gates(p_ref[0], p_ref[1], lb_ref[...])
        lf = jnp.log(f)
        v = [p_ref[2, rows, :].astype(BF16) for rows in chunks]
        b = [_exact_dot(lsel, lf[c * HGRN_CHUNK:(c + 1) * HGRN_CHUNK]) for c in range(cps)]
        b_last = [bc[HGRN_CHUNK - 1:HGRN_CHUNK, :] for bc in b]
        q = [q[c * HGRN_CHUNK:(c + 1) * HGRN_CHUNK] for c in range(cps)]
        key = [key[c * HGRN_CHUNK:(c + 1) * HGRN_CHUNK] for c in range(cps)]
        qd = [(q[c] * jnp.exp(b[c])).astype(BF16) for c in range(cps)]
        kd = [(key[c] * jnp.exp(-b[c])).astype(BF16) for c in range(cps)]
        k2 = [(key[c] * jnp.exp(b_last[c] - b[c])).astype(BF16) for c in range(cps)]
        a = [jnp.where(ltri, _dot_nt(qd[c], kd[c]), 0.0).astype(BF16) for c in range(cps)]
        kv = [_dot_tn(v[c], k2[c]) for c in range(cps)]
        st, sts = state[...], []
        for c in range(cps):
            sts.append(st)
            st_ref[0, c] = st
            st = st * jnp.exp(b_last[c]) + kv[c]
        state[...] = st
        gv = g_ref[...]
        for c in range(cps):
            o = _dot(a[c], v[c]) + _dot_nt(qd[c], sts[c].astype(BF16))
            o_ref[chunks[c], :] = o
            r = lax.rsqrt(jnp.mean(o * o, axis=-1, keepdims=True) + RMS_EPS)
            on_ref[chunks[c], :] = (o * r * gv).astype(BF16)

    vec = pl.BlockSpec((1, LANES), lambda h, c: (0, h))
    col = pl.BlockSpec((HGRN_ROWS, LANES), lambda h, c: (c, h))
    return pl.pallas_call(
        body, name="hgrn_fwd", grid=(HGRN_HEADS, nblk),
        in_specs=[pl.BlockSpec((3, HGRN_ROWS, LANES), lambda h, c: (0, c, h)), vec, vec],
        out_specs=[col, col, pl.BlockSpec((1, cps, LANES, LANES), lambda h, c: (h, c, 0, 0))],
        out_shape=[jax.ShapeDtypeStruct((s, d), F32), jax.ShapeDtypeStruct((s, d), BF16),
                   jax.ShapeDtypeStruct((HGRN_HEADS, s // HGRN_CHUNK, LANES, LANES), F32)],
        scratch_shapes=[pltpu.VMEM((LANES, LANES), F32)],
        compiler_params=_params(("parallel", "arbitrary")),
    )(p2, lb, norm_g)


def _hgrn_bwd(p2, lb, norm_g, o_raw, states, dyn):
    _, s, d = p2.shape
    nblk = s // HGRN_ROWS
    cps = HGRN_ROWS // HGRN_CHUNK

    def body(p_ref, lb_ref, g_ref, o_ref, st_ref, dy_ref, dp_ref, dg_ref, dlb_ref, dstate):
        @pl.when(pl.program_id(1) == 0)
        def _():
            dstate[...] = jnp.zeros_like(dstate)
            dg_ref[...] = jnp.zeros_like(dg_ref)
            dlb_ref[...] = jnp.zeros_like(dlb_ref)

        n = cps
        cut = lambda t: [t[c * HGRN_CHUNK:(c + 1) * HGRN_CHUNK] for c in range(n)]
        chunks = [pl.ds(c * HGRN_CHUNK, HGRN_CHUNK) for c in range(n)]
        lbv = lb_ref[...]
        ltri = _tri(True)
        lsel = _tri(True, 3).astype(BF16)
        usel = _tri(False, 3).astype(BF16)
        last_row = lax.broadcasted_iota(jnp.int32, (HGRN_CHUNK, LANES), 0) == HGRN_CHUNK - 1
        qr, z = p_ref[0], p_ref[1]
        sq, q, sg, sn, f, key = _hgrn_gates(qr, z, lbv)
        lf = jnp.log(f)
        v = [p_ref[2, rows, :].astype(BF16) for rows in chunks]
        o, dyv = o_ref[...], dy_ref[...]
        r = lax.rsqrt(jnp.mean(o * o, axis=-1, keepdims=True) + RMS_EPS)
        oh = o * r
        doh = dyv * g_ref[...]
        do = cut((r * (doh - oh * jnp.mean(doh * oh, axis=-1, keepdims=True))).astype(BF16))
        dg_ref[...] += jnp.sum(dyv * oh, axis=0, keepdims=True)
        b = [_exact_dot(lsel, lfc) for lfc in cut(lf)]
        b_last = [bc[HGRN_CHUNK - 1:HGRN_CHUNK, :] for bc in b]
        q, key = cut(q), cut(key)
        eb = [jnp.exp(bc) for bc in b]
        enb = [jnp.exp(-bc) for bc in b]
        e2 = [jnp.exp(b_last[c] - b[c]) for c in range(n)]
        dec = [jnp.exp(bl) for bl in b_last]
        qd_f = [q[c] * eb[c] for c in range(n)]
        kd_f = [key[c] * enb[c] for c in range(n)]
        k2_f = [key[c] * e2[c] for c in range(n)]
        qd, kd, k2 = ([t.astype(BF16) for t in ts] for ts in (qd_f, kd_f, k2_f))
        a = [jnp.where(ltri, _dot_nt(qd[c], kd[c]), 0.0).astype(BF16) for c in range(n)]
        da = [jnp.where(ltri, _dot_nt(do[c], v[c]), 0.0).astype(BF16) for c in range(n)]
        grow = [_dot_tn(do[c], qd[c]) for c in range(n)]
        dst, dsts = dstate[...], [None] * n
        for c in reversed(range(n)):
            dsts[c] = dst
            dst = dst * dec[c] + grow[c]
        dstate[...] = dst
        st = [st_ref[0, c] for c in range(n)]
        dstb = [t.astype(BF16) for t in dsts]
        dv = [_dot_tn(a[c], do[c]) + _dot_nt(k2[c], dstb[c]) for c in range(n)]
        dqd = [_dot(da[c], kd[c]) + _dot(do[c], st[c].astype(BF16)) for c in range(n)]
        dkd = [_dot_tn(da[c], qd[c]) for c in range(n)]
        dk2 = [_dot(v[c], dstb[c]) for c in range(n)]
        db = []
        for c in range(n):
            ddec = jnp.sum(dsts[c] * st[c], axis=0, keepdims=True)
            db_last = jnp.sum(dk2[c] * k2_f[c], axis=0, keepdims=True) + ddec * dec[c]
            db.append(dqd[c] * qd_f[c] - dkd[c] * kd_f[c] - dk2[c] * k2_f[c] + jnp.where(last_row, db_last, 0.0))
        dlf = [_exact_dot(usel, dbc) for dbc in db]
        f, sg, sn, sq, qr = cut(f), cut(sg), cut(sn), cut(sq), cut(qr)
        dlb_acc = jnp.zeros((1, LANES), F32)
        for c in range(n):
            dkey = dkd[c] * enb[c] + dk2[c] * e2[c]
            common = dlf[c] / f[c] - dkey
            dlb_acc += jnp.sum(common * sn[c], axis=0, keepdims=True)
            dp_ref[0, chunks[c], :] = (dqd[c] * eb[c] * (sq[c] * (1.0 + qr[c] * (1.0 - sq[c])))).astype(BF16)
            dp_ref[1, chunks[c], :] = (common * ((1.0 - lbv) * sg[c] * sn[c])).astype(BF16)
            dp_ref[2, chunks[c], :] = dv[c].astype(BF16)
        dlb_ref[...] += dlb_acc

    def rev(c):
        return nblk - 1 - c

    vec = pl.BlockSpec((1, LANES), lambda h, c: (0, h))
    col = pl.BlockSpec((HGRN_ROWS, LANES), lambda h, c: (rev(c), h))
    p3 = pl.BlockSpec((3, HGRN_ROWS, LANES), lambda h, c: (0, rev(c), h))
    return pl.pallas_call(
        body, name="hgrn_bwd", grid=(HGRN_HEADS, nblk),
        in_specs=[p3, vec, vec, col, pl.BlockSpec((1, cps, LANES, LANES), lambda h, c: (h, rev(c), 0, 0)), col],
        out_specs=[p3, vec, vec],
        out_shape=[jax.ShapeDtypeStruct((3, s, d), BF16), jax.ShapeDtypeStruct((1, d), F32),
                   jax.ShapeDtypeStruct((1, d), F32)],
        scratch_shapes=[pltpu.VMEM((LANES, LANES), F32)],
        compiler_params=_params(("parallel", "arbitrary")),
    )(p2, lb, norm_g, o_raw, states, dyn)


SMALL_ROWS = 16
FUSED_TM = 512


def _relu2(h):
    r = jnp.maximum(h.astype(F32), 0.0)
    return r * r


def _dact_epi(acc, h):
    return (acc * (2.0 * jnp.maximum(h.astype(F32), 0.0)),)


def _rope_epi(j, acc, ct, sst):
    acc_t = acc.T
    rot_t = _rotate_t(acc_t, ct, sst) * jnp.where(j == 0, ATTN_SCALE, 1.0)
    out_t = jnp.where(j >= 2, acc_t, rot_t).astype(BF16)
    return out_t.T, out_t


def _local_step(x, target, get_w, small, on_grads, deps=()):
    s, d = x.shape
    lb = small["lb"]
    gq = 3 * d
    gt = gq // 1024

    def ln_of(a, w, x_in, which, layer, *, name, k, a_pre=None, deps=()):
        g, b = small[f"ln_{which}_g"][layer:layer + 1], small[f"ln_{which}_b"][layer:layer + 1]
        return _mm(a, w, "nn", name=name, m=s, n=d, k=k, tm=FUSED_TM, tk=k, a_pre=a_pre, out_dtypes=(F32, BF16, F32),
                   row_outs=((F32, 1),), epi=_ln_epi, tile_extras=(x_in,), vec_extras=(g, b), deps=deps)

    def ln_grad_of(a, w, du_next, xh, rstd, which, layer, *, name, k, a_split=1, deps=()):
        return _mm(a, w, "nt", name=name, m=s, n=d, k=k, tm=FUSED_TM, tk=k, a_split=a_split,
                   out_dtypes=(F32, BF16), vec_outs=2, epi=_ln_grad_epi, tile_extras=(du_next, xh), row_extras=(rstd,),
                   vec_extras=(small[f"ln_{which}_g"][layer:layer + 1],), deps=deps)

    def ffn_fwd(xb, x_in, w_up, w_down, tag):
        h = _mm(xb, w_up, "nn", name=f"ffn_up_{tag}", m=s, n=D_FF, k=d, tm=2048, out_dtypes=(BF16,))
        return (h,) + tuple(ln_of(h, w_down, x_in, "ffn", tag, name=f"ffn_down_{tag}", k=D_FF, a_pre=_relu2))

    def ffn_bwd(du, dub, xb, h, w_up, w_down, xh, rstd, tag):
        dh = _mm(dub, w_down, "nt", name=f"ffn_dact_{tag}", m=s, n=D_FF, k=d, tm=2048, out_dtypes=(BF16,), epi=_dact_epi,
                 tile_extras=(h,))
        g_down = _mm(h, dub, "tn", name=f"ffn_gdown_{tag}", m=D_FF, n=d, k=s, tm=512, tk=s, a_pre=_relu2,
                     out_dtypes=(BF16,))
        g_up = _mm(xb, dh, "tn", name=f"ffn_gup_{tag}", m=d, n=D_FF, k=s, tk=s, out_dtypes=(BF16,))
        after = on_grads(f"ffn{tag}", {f"ffn_w_down{tag}": g_down, f"ffn_w_up{tag}": g_up})
        return ln_grad_of(dh, w_up, du, xh, rstd, "mix", tag, name=f"ffn_dx_{tag}", k=D_FF, deps=after)

    xs = _prep_x(x, deps)
    w_ain, after = get_w("attn_w_in", None)
    tabs, qkvs, qkv_ts, o_parts, lse_parts = [], [], [], [], []
    for g, dil in enumerate(DILATIONS):
        tabs.append(_rope_tables(s, dil))
        tabs_t = [t[:, :ATTN_HEAD_DIM].T for t in tabs[g]]
        qkv, qkv_t = _mm(xs[g], w_ain, "nn", name=f"attn_in_{g}", m=s, n=gq, k=d, b_col_off=g * gt, out_split=3,
                         out_dtypes=(BF16,), epi=_rope_epi, epi_wants_j=True, row_extras=tabs_t, t_out=True,
                         deps=after if g == 0 else ())
        qkvs.append(qkv)
        qkv_ts.append(qkv_t)
        o_g, lse_g = _attn_fwd(qkv, dil, name=f"attn_fwd_{g}")
        o_parts.append(o_g)
        lse_parts.append(lse_g)
    o_f, o_b, lse_t = _attn_combine(o_parts, lse_parts)
    w_aout, after = get_w("attn_w_out", o_b)
    x1, x1b, xh1, r1 = ln_of(o_b, w_aout, x, "mix", 0, name="attn_out", k=d, deps=after)
    w_up0, w_down0 = get_w("ffn_w_up0", o_b)[0], get_w("ffn_w_down0", o_b)[0]
    h0, x2, x2b, xh2, r2 = ffn_fwd(x1b, x1, w_up0, w_down0, 0)
    w_hin, w_hout, norm_g = get_w("hgrn_w_in", x2b)[0], get_w("hgrn_w_out", x2b)[0], get_w("hgrn_norm_g", x2b)[0]
    p2 = _mm(x2b, w_hin, "nn", name="hgrn_in", m=s, n=3 * d, k=d, tm=2048, out_split=3)
    o_raw, o_n, states = _hgrn_fwd(p2, lb, norm_g)
    w_up1, w_down1 = get_w("ffn_w_up1", o_n)[0], get_w("ffn_w_down1", o_n)[0]
    x3, x3b, xh3, r3 = ln_of(o_n, w_hout, x2, "mix", 1, name="hgrn_out", k=d)
    h1 = _mm(x3b, w_up1, "nn", name="ffn_up_1", m=s, n=D_FF, k=d, tm=2048, out_dtypes=(BF16,))
    du, dub, dg_ffn1, db_ffn1, sq = _mm(
        h1, w_down1, "nn", name="ffn_down_1", m=s, n=d, k=D_FF, tm=FUSED_TM, tk=D_FF, a_pre=_relu2,
        out_dtypes=(F32, BF16), vec_outs=3, epi=_ln_loss_epi, tile_extras=(x3, target),
        vec_extras=(small["ln_ffn_g"][1:2], small["ln_ffn_b"][1:2]))
    du, dub, dg_mix1, db_mix1 = ffn_bwd(du, dub, x3b, h1, w_up1, w_down1, xh3, r3, 1)
    dyn = _mm(dub, w_hout, "nt", name="hgrn_dout", m=s, n=d, k=d)
    g_hout = _mm(o_n, dub, "tn", name="hgrn_gout", m=d, n=d, k=s, out_dtypes=(BF16,))
    dp2, d_norm_g, d_lb = _hgrn_bwd(p2, lb, norm_g, o_raw, states, dyn)
    g_hin = _mm(x2b, dp2, "tn", name="hgrn_gin", m=d, n=3 * d, k=s, tk=s, b_split=3, out_dtypes=(BF16,))
    after = on_grads("hgrn", {"hgrn_w_out": g_hout, "hgrn_w_in": g_hin})
    du, dub, dg_ffn0, db_ffn0 = ln_grad_of(dp2, w_hin, du, xh2, r2, "ffn", 0, name="hgrn_dx", k=3 * d, a_split=3,
                                           deps=after)
    du, dub, dg_mix0, db_mix0 = ffn_bwd(du, dub, x1b, h0, w_up0, w_down0, xh1, r1, 0)
    g_aout = _mm(o_b, dub, "tn", name="attn_gout", m=d, n=d, k=s, out_dtypes=(BF16,))
    sm1 = jax.nn.softmax(small["lb_logits"], axis=0)
    d_l1 = d_lb * (sm1[0:1] * sm1[1:2])
    zeros = jnp.zeros((SMALL_ROWS - 12, d), F32)
    loss_row = jnp.broadcast_to(0.5 * jnp.sum(sq) / d, (1, d))
    small_grads = jnp.concatenate([dg_mix0, dg_mix1, db_mix0, db_mix1, dg_ffn0, dg_ffn1, db_ffn0, db_ffn1,
                                   -d_l1, d_l1, d_norm_g, loss_row, zeros], axis=0)
    after = on_grads("attn_out", {"attn_w_out": g_aout, "small": small_grads})
    do = _mm(dub, w_aout, "nt", name="attn_dout", m=s, n=d, k=d, deps=after)
    do_parts, do_ts, aux_parts = _attn_bwd_prep(do, o_f, lse_t)
    g_ain, dqkvs = None, []
    for g, dil in enumerate(DILATIONS):
        dqkvs.append(_attn_bwd(qkvs[g], qkv_ts[g], do_parts[g], do_ts[g], aux_parts[g], tabs[g], dil,
                               name=f"attn_bwd_{g}"))
        g_ain = _mm(xs[g], dqkvs[g], "tn", name=f"attn_gin_{g}", m=d, n=gq, k=s, tk=s, b_split=3, out_dtypes=(BF16,),
                    out_col_off=g * gt, out_cols=3 * gq, alias=g_ain)
    after = on_grads("attn_in", {"attn_w_in": g_ain})
    dx_parts = [_mm(dqkvs[g], w_ain, "nt", name=f"attn_dx_{g}", m=s, n=d, k=gq, tm=FUSED_TM, tk=gq, a_split=3, b_k_off=g,
                    deps=after if g == 0 else ())
                for g in range(len(DILATIONS))]
    return _dx_combine(du, dx_parts)


def _mesh_place():
    x, y, c = lax.axis_index("x"), lax.axis_index("y"), lax.axis_index("c")
    return x, y, c, 4 * x + 2 * y + c


def _peer(x, y, c, k):
    px = 1 - x if (k >> 2) & 1 else x
    py = 1 - y if (k >> 1) & 1 else y
    pc = 1 - c if k & 1 else c
    return (px, py, pc), 4 * px + 2 * py + pc


def _window(ref, axis, size, idx):
    if axis is None:
        return ref
    sl = [slice(None)] * len(ref.shape)
    sl[axis] = pl.ds(idx * size, size)
    return ref.at[tuple(sl)]


_HBM = pl.BlockSpec(memory_space=pltpu.HBM)
_SEM = pl.BlockSpec(memory_space=pltpu.SEMAPHORE)
_EFFECT = pltpu.SideEffectType.DATAFLOW_SIDE_EFFECTING


def _xchg_ends(src_ref, land_ref, axis, gather, me, other):
    if gather:
        size = src_ref.shape[axis]
        return src_ref, _window(land_ref, axis, size, me), _window(land_ref, axis, size, other)
    size = None if axis is None else src_ref.shape[axis] // N_DEV
    return _window(src_ref, axis, size, other), land_ref.at[me], land_ref.at[other]


def _xchg_start(srcs, lands, axes, *, gather, name, deps=(), peers=tuple(range(1, N_DEV))):
    n = len(srcs)
    nd = len(deps)

    def body(*refs):
        src_refs, land_refs = refs[:n], refs[n:2 * n]
        send, recv = refs[2 * n + nd:3 * n + nd], refs[3 * n + nd:4 * n + nd]
        token = refs[-1]
        x, y, c, me = _mesh_place()
        for k in peers:
            peer, pidx = _peer(x, y, c, k)
            for i in range(n):
                src, dst, _ = _xchg_ends(src_refs[i], land_refs[i], axes[i], gather, me, pidx)
                pltpu.make_async_remote_copy(
                    src_ref=src, dst_ref=dst, send_sem=send[i].at[k - 1], recv_sem=recv[i].at[k - 1],
                    device_id=peer, device_id_type=pl.DeviceIdType.MESH).start()
        for i in range(n):
            src, dst, _ = _xchg_ends(src_refs[i], land_refs[i], axes[i], gather, me, me)
            pltpu.make_async_copy(src, dst, send[i].at[N_DEV - 1]).start()
        token[...] = jnp.zeros_like(token)

    bufs = list(srcs) + list(lands)
    outs = pl.pallas_call(
        body, name=name,
        out_shape=[pltpu.SemaphoreType.DMA((N_DEV,))] * (2 * n) + [pltpu.HBM(b.shape, b.dtype) for b in bufs]
        + [jax.ShapeDtypeStruct((8, LANES), F32)],
        in_specs=[_HBM] * (2 * n) + [pl.BlockSpec(memory_space=pl.ANY)] * nd,
        out_specs=[_SEM] * (2 * n) + [_HBM] * (2 * n) + [pl.BlockSpec(memory_space=pltpu.VMEM)],
        input_output_aliases={i: 2 * n + i for i in range(2 * n)},
        compiler_params=pltpu.CompilerParams(has_side_effects=_EFFECT),
    )(*[pltpu.with_memory_space_constraint(b, pltpu.HBM) for b in bufs], *deps)
    return dict(send=outs[:n], recv=outs[n:2 * n], srcs=outs[2 * n:3 * n], lands=outs[3 * n:4 * n], token=outs[-1],
                axes=list(axes), gather=gather)


SIBLING = 1
SAME_CORE = (2, 4, 6)


def _gather_relay(xc, after, *, name):
    axis = xc["axes"][0]

    def body(src_ref, land_ref, send1, recv1, after_ref, src_out, land_out, send2, recv2):
        x, y, c, me = _mesh_place()
        sibling, _ = _peer(x, y, c, SIBLING)
        for j, k in enumerate(SAME_CORE):
            peer, pidx = _peer(x, y, c, k)
            src, _, got = _xchg_ends(src_ref, land_ref, axis, True, me, pidx)
            pltpu.make_async_remote_copy(
                src_ref=src, dst_ref=got, send_sem=send1.at[k - 1], recv_sem=recv1.at[k - 1],
                device_id=peer, device_id_type=pl.DeviceIdType.MESH).wait_recv()
            pltpu.make_async_remote_copy(
                src_ref=got, dst_ref=got, send_sem=send2.at[j], recv_sem=recv2.at[j],
                device_id=sibling, device_id_type=pl.DeviceIdType.MESH).start()

    src, land = xc["srcs"][0], xc["lands"][0]
    outs = pl.pallas_call(
        body, name=name,
        out_shape=[pltpu.HBM(src.shape, src.dtype), pltpu.HBM(land.shape, land.dtype)]
        + [pltpu.SemaphoreType.DMA((len(SAME_CORE),))] * 2,
        in_specs=[_HBM, _HBM, _SEM, _SEM, pl.BlockSpec(memory_space=pl.ANY)],
        out_specs=[_HBM, _HBM, _SEM, _SEM], input_output_aliases={0: 0, 1: 1},
        compiler_params=pltpu.CompilerParams(has_side_effects=_EFFECT),
    )(src, land, xc["send"][0], xc["recv"][0], after)
    return dict(src=outs[0], land=outs[1], send=outs[2], recv=outs[3])


def _gather_relay_wait(xc, relay, after, *, name):
    axis = xc["axes"][0]

    def body(src_ref, land_ref, send1, recv1, send2, recv2, after_ref, src_out, land_out):
        x, y, c, me = _mesh_place()
        sibling, sidx = _peer(x, y, c, SIBLING)
        for k in (SIBLING,) + SAME_CORE:
            peer, pidx = _peer(x, y, c, k)
            src, dst, got = _xchg_ends(src_ref, land_ref, axis, True, me, pidx)
            pltpu.make_async_remote_copy(
                src_ref=src, dst_ref=dst, send_sem=send1.at[k - 1], recv_sem=recv1.at[k - 1],
                device_id=peer, device_id_type=pl.DeviceIdType.MESH).wait_send()
        src, dst, got = _xchg_ends(src_ref, land_ref, axis, True, me, sidx)
        pltpu.make_async_remote_copy(
            src_ref=src, dst_ref=got, send_sem=send1.at[SIBLING - 1], recv_sem=recv1.at[SIBLING - 1],
            device_id=sibling, device_id_type=pl.DeviceIdType.MESH).wait_recv()
        src, dst, _ = _xchg_ends(src_ref, land_ref, axis, True, me, me)
        pltpu.make_async_copy(src, dst, send1.at[N_DEV - 1]).wait()
        for j, k in enumerate(SAME_CORE):
            _, pidx = _peer(x, y, c, k)
            _, qidx = _peer(x, y, c, k ^ SIBLING)
            _, _, sent = _xchg_ends(src_ref, land_ref, axis, True, me, pidx)
            _, _, got = _xchg_ends(src_ref, land_ref, axis, True, me, qidx)
            pltpu.make_async_remote_copy(
                src_ref=sent, dst_ref=sent, send_sem=send2.at[j], recv_sem=recv2.at[j],
                device_id=sibling, device_id_type=pl.DeviceIdType.MESH).wait_send()
            pltpu.make_async_remote_copy(
                src_ref=got, dst_ref=got, send_sem=send2.at[j], recv_sem=recv2.at[j],
                device_id=sibling, device_id_type=pl.DeviceIdType.MESH).wait_recv()

    outs = pl.pallas_call(
        body, name=name, out_shape=[pltpu.HBM(relay["src"].shape, relay["src"].dtype),
                                    pltpu.HBM(relay["land"].shape, relay["land"].dtype)],
        in_specs=[_HBM, _HBM, _SEM, _SEM, _SEM, _SEM, pl.BlockSpec(memory_space=pl.ANY)],
        out_specs=[_HBM, _HBM], input_output_aliases={0: 0, 1: 1},
        compiler_params=pltpu.CompilerParams(has_side_effects=_EFFECT),
    )(relay["src"], relay["land"], xc["send"][0], xc["recv"][0], relay["send"], relay["recv"], after)
    return outs[1]


def _xchg_wait(xc, items, after, *, name):
    m = len(items)
    gather = xc["gather"]
    axes = [xc["axes"][i] for i in items]

    def body(*refs):
        src_refs, land_refs = refs[:m], refs[m:2 * m]
        send, recv = refs[2 * m:3 * m], refs[3 * m:4 * m]
        x, y, c, me = _mesh_place()
        for k in range(1, N_DEV):
            peer, pidx = _peer(x, y, c, k)
            for j in range(m):
                src, dst, got = _xchg_ends(src_refs[j], land_refs[j], axes[j], gather, me, pidx)
                pltpu.make_async_remote_copy(
                    src_ref=src, dst_ref=dst, send_sem=send[j].at[k - 1], recv_sem=recv[j].at[k - 1],
                    device_id=peer, device_id_type=pl.DeviceIdType.MESH).wait_send()
                pltpu.make_async_remote_copy(
                    src_ref=src, dst_ref=got, send_sem=send[j].at[k - 1], recv_sem=recv[j].at[k - 1],
                    device_id=peer, device_id_type=pl.DeviceIdType.MESH).wait_recv()
        for j in range(m):
            src, dst, _ = _xchg_ends(src_refs[j], land_refs[j], axes[j], gather, me, me)
            pltpu.make_async_copy(src, dst, send[j].at[N_DEV - 1]).wait()

    bufs = [xc["srcs"][i] for i in items] + [xc["lands"][i] for i in items]
    sems = [xc["send"][i] for i in items] + [xc["recv"][i] for i in items]
    outs = pl.pallas_call(
        body, name=name, out_shape=[pltpu.HBM(b.shape, b.dtype) for b in bufs],
        in_specs=[_HBM] * (2 * m) + [_SEM] * (2 * m) + [pl.BlockSpec(memory_space=pl.ANY)],
        out_specs=[_HBM] * (2 * m), input_output_aliases={j: j for j in range(2 * m)},
        compiler_params=pltpu.CompilerParams(has_side_effects=_EFFECT),
    )(*bufs, *sems, after)
    return outs[m:]


def _cast_bf16(a, *, name):
    r, c = a.shape
    tr = min(r, 512)

    def body(a_ref, o_ref):
        o_ref[...] = a_ref[...].astype(BF16)

    spec = pl.BlockSpec((tr, c), lambda i: (i, 0))
    return pl.pallas_call(body, name=name, grid=(r // tr,), in_specs=[spec], out_specs=spec,
                          out_shape=jax.ShapeDtypeStruct((r, c), BF16), compiler_params=_params(("parallel",)))(a)


def _adamw(slabs, w, m, v, *, name):
    layers, r, c = w.shape
    tr = min(r, 256)

    def body(*refs):
        s_refs = refs[:layers]
        w_ref, m_ref, v_ref, g_ref, d_ref, mo_ref, vo_ref = refs[layers:]
        for l in range(layers):
            g = s_refs[l][0].astype(F32)
            for i in range(1, N_DEV):
                g = g + s_refs[l][i].astype(F32)
            m2 = ADAM_B1 * m_ref[l] + (1.0 - ADAM_B1) * g
            v2 = ADAM_B2 * v_ref[l] + (1.0 - ADAM_B2) * (g * g)
            m_hat = m2 / (1.0 - ADAM_B1 ** ADAM_STEP)
            v_hat = v2 / (1.0 - ADAM_B2 ** ADAM_STEP)
            g_ref[l] = g
            d_ref[l] = -ADAM_LR * (m_hat / (jnp.sqrt(v_hat) + ADAM_EPS) + ADAM_WD * w_ref[l])
            mo_ref[l] = m2
            vo_ref[l] = v2

    spec = pl.BlockSpec((layers, tr, c), lambda i: (0, i, 0))
    return pl.pallas_call(
        body, name=name, grid=(r // tr,),
        in_specs=[pl.BlockSpec((N_DEV, tr, c), lambda i: (0, i, 0))] * layers + [spec, spec, spec],
        out_specs=[spec] * 4, out_shape=[jax.ShapeDtypeStruct((layers, r, c), F32)] * 4,
        compiler_params=_params(("parallel",)),
    )(*slabs, w, m, v)


GATHER_AXIS = {"attn_w_in": 1, "attn_w_out": 0, "ffn_w_up0": 1, "ffn_w_down0": 0, "hgrn_w_in": 1, "hgrn_w_out": 0,
               "hgrn_norm_g": 1, "ffn_w_up1": 1, "ffn_w_down1": 0}
GATHER_STAGES = (("attn_w_in",), ("attn_w_out", "ffn_w_up0", "ffn_w_down0", "hgrn_w_in", "hgrn_w_out", "hgrn_norm_g"),
                 ("ffn_w_up1", "ffn_w_down1"))
GATHER_WAITS = ((("attn_w_in",), 0, 1), (("attn_w_out", "ffn_w_up0", "ffn_w_down0"), 1, 2),
                (("hgrn_w_in", "hgrn_w_out", "hgrn_norm_g"), 1, None), (("ffn_w_up1", "ffn_w_down1"), 2, None))
SCATTER_AXIS = dict(GATHER_AXIS, small=None)
BIG = ("attn_w_in", "attn_w_out", "hgrn_w_in", "hgrn_w_out", "ffn_w_up", "ffn_w_down")
SMALL = ("lb_logits", "ln_mix_g", "ln_mix_b", "ln_ffn_g", "ln_ffn_b")
SMALL_ROW = {"ln_mix_g": 0, "ln_mix_b": 2, "ln_ffn_g": 4, "ln_ffn_b": 6, "lb_logits": 8}
NORM_G_ROW = 10
LOSS_ROW = 11


def kernel(x, attn_w_in, attn_w_out, hgrn_w_in, hgrn_w_out, hgrn_norm_g, lb_logits, ln_mix_g, ln_mix_b, ln_ffn_g, ln_ffn_b, ffn_w_up, ffn_w_down, loss_target, m_attn_w_in, m_attn_w_out, m_hgrn_w_in, m_hgrn_w_out, m_hgrn_norm_g, m_lb_logits, m_ln_mix_g, m_ln_mix_b, m_ln_ffn_g, m_ln_ffn_b, m_ffn_w_up, m_ffn_w_down, v_attn_w_in, v_attn_w_out, v_hgrn_w_in, v_hgrn_w_out, v_hgrn_norm_g, v_lb_logits, v_ln_mix_g, v_ln_mix_b, v_ln_ffn_g, v_ln_ffn_b, v_ffn_w_up, v_ffn_w_down):
    wts = dict(attn_w_in=attn_w_in, attn_w_out=attn_w_out, hgrn_w_in=hgrn_w_in, hgrn_w_out=hgrn_w_out,
               hgrn_norm_g=hgrn_norm_g, lb_logits=lb_logits, ln_mix_g=ln_mix_g, ln_mix_b=ln_mix_b, ln_ffn_g=ln_ffn_g,
               ln_ffn_b=ln_ffn_b, ffn_w_up=ffn_w_up, ffn_w_down=ffn_w_down)
    mom = dict(attn_w_in=m_attn_w_in, attn_w_out=m_attn_w_out, hgrn_w_in=m_hgrn_w_in, hgrn_w_out=m_hgrn_w_out,
               hgrn_norm_g=m_hgrn_norm_g, lb_logits=m_lb_logits, ln_mix_g=m_ln_mix_g, ln_mix_b=m_ln_mix_b,
               ln_ffn_g=m_ln_ffn_g, ln_ffn_b=m_ln_ffn_b, ffn_w_up=m_ffn_w_up, ffn_w_down=m_ffn_w_down)
    vel = dict(attn_w_in=v_attn_w_in, attn_w_out=v_attn_w_out, hgrn_w_in=v_hgrn_w_in, hgrn_w_out=v_hgrn_w_out,
               hgrn_norm_g=v_hgrn_norm_g, lb_logits=v_lb_logits, ln_mix_g=v_ln_mix_g, ln_mix_b=v_ln_mix_b,
               ln_ffn_g=v_ln_ffn_g, ln_ffn_b=v_ln_ffn_b, ffn_w_up=v_ffn_w_up, ffn_w_down=v_ffn_w_down)
    me = 4 * lax.axis_index("x") + 2 * lax.axis_index("y") + lax.axis_index("c")

    src = {"attn_w_in": attn_w_in[0], "attn_w_out": attn_w_out[0], "hgrn_w_in": hgrn_w_in[0], "hgrn_w_out": hgrn_w_out[0],
           "ffn_w_up0": ffn_w_up[0], "ffn_w_down0": ffn_w_down[0], "ffn_w_up1": ffn_w_up[1], "ffn_w_down1": ffn_w_down[1]}
    gathers, got = {}, {}

    def start_gather(stage, deps):
        shards, lands = [], []
        for nm in GATHER_STAGES[stage]:
            sh = hgrn_norm_g if nm == "hgrn_norm_g" else _cast_bf16(src[nm], name=f"cast_{nm}")
            ax = GATHER_AXIS[nm]
            shape = list(sh.shape)
            shape[ax] *= N_DEV
            shards.append(sh)
            lands.append(lax.empty(tuple(shape), sh.dtype))
        peers = (SIBLING,) + SAME_CORE if stage == 0 else tuple(range(1, N_DEV))
        gathers[stage] = _xchg_start(shards, lands, [GATHER_AXIS[nm] for nm in GATHER_STAGES[stage]], gather=True,
                                     name=f"gather_start_{stage}", deps=deps, peers=peers)
        return [gathers[stage]["token"]]

    def get_w(name, after):
        deps = []
        if name not in got:
            group, stage, then = [w for w in GATHER_WAITS if name in w[0]][0]
            xc = gathers[stage]
            if stage == 0:
                relay = _gather_relay(xc, xc["token"], name="gather_relay")
                res = [_gather_relay_wait(xc, relay, xc["token"], name=f"gather_wait_{group[0]}")]
            else:
                res = _xchg_wait(xc, [GATHER_STAGES[stage].index(nm) for nm in group], after,
                                 name=f"gather_wait_{group[0]}")
            got.update(zip(group, res))
            if then is not None:
                deps = start_gather(then, [res[0]])
        return got[name], deps

    first = start_gather(0, [])

    scattered = {}

    def on_grads(tag, grads):
        gnames = list(grads)
        axes = [SCATTER_AXIS[nm] for nm in gnames]
        stacks = []
        for nm, ax in zip(gnames, axes):
            shape = list(grads[nm].shape)
            if ax is not None:
                shape[ax] //= N_DEV
            stacks.append(lax.empty((N_DEV, *shape), grads[nm].dtype))
        scattered[tag] = (gnames, _xchg_start([grads[nm] for nm in gnames], stacks, axes, gather=False,
                                              name=f"scatter_start_{tag}"))
        return [scattered[tag][1]["token"]]

    sm = jax.nn.softmax(lb_logits, axis=0)
    csum = jnp.cumsum(sm, axis=0)
    small = dict(lb=(csum - csum[0:1])[1:2], lb_logits=lb_logits, ln_mix_g=ln_mix_g, ln_mix_b=ln_mix_b,
                 ln_ffn_g=ln_ffn_g, ln_ffn_b=ln_ffn_b)
    grad_x = _local_step(x[0], loss_target[0], get_w, small, on_grads, deps=first)
    out = {}

    def stack_small(src_):
        rows = [None] * SMALL_ROWS
        for name in SMALL:
            rows[SMALL_ROW[name]], rows[SMALL_ROW[name] + 1] = src_[name][0:1], src_[name][1:2]
        zero = jnp.zeros((1, x.shape[-1]), F32)
        return jnp.concatenate([zero if r is None else r for r in rows], axis=0)[None]

    def update(name, slabs):
        shape = wts[name].shape
        out[name] = [r.reshape(shape) for r in _adamw(slabs, wts[name], mom[name], vel[name], name=f"adamw_{name}")]
        return out[name][0]

    slabs, after = {}, grad_x
    for tag, (gnames, xc) in scattered.items():
        slabs.update(zip(gnames, _xchg_wait(xc, list(range(len(gnames))), after, name=f"scatter_wait_{tag}")))
        if tag == "ffn1":
            continue
        if tag == "ffn0":
            update("ffn_w_down", [slabs["ffn_w_down0"], slabs["ffn_w_down1"]])
            after = update("ffn_w_up", [slabs["ffn_w_up0"], slabs["ffn_w_up1"]])
        elif tag == "hgrn":
            update("hgrn_w_out", [slabs["hgrn_w_out"]])
            after = update("hgrn_w_in", [slabs["hgrn_w_in"]])
        elif tag == "attn_out":
            after = update("attn_w_out", [slabs["attn_w_out"]])
        else:
            after = update("attn_w_in", [slabs["attn_w_in"]])
    res = _adamw([slabs["small"]], stack_small(wts), stack_small(mom), stack_small(vel), name="adamw_small")
    for name in SMALL:
        out[name] = [r[0, SMALL_ROW[name]:SMALL_ROW[name] + 2] for r in res]
    loss = res[0][0, LOSS_ROW, 0]
    ng = hgrn_norm_g.shape[-1]
    ng_slabs = lax.dynamic_slice(slabs["small"], (0, NORM_G_ROW, me * ng), (N_DEV, 1, ng))
    out["hgrn_norm_g"] = [r[0] for r in _adamw([ng_slabs], hgrn_norm_g[None], m_hgrn_norm_g[None],
                                                v_hgrn_norm_g[None], name="adamw_norm_g")]
    order =("attn_w_in", "attn_w_out", "hgrn_w_in", "hgrn_w_out", "hgrn_norm_g", "lb_logits", "ln_mix_g", "ln_mix_b",
             "ln_ffn_g", "ln_ffn_b", "ffn_w_up", "ffn_w_down")
    return (loss, grad_x[None], *[out[nm][0] for nm in order], *[out[nm][1] for nm in order],
            *[out[nm][2] for nm in order], *[out[nm][3] for nm in order])
```

```python
import jax
import jax.numpy as jnp
from jax import lax
from jax.experimental import pallas as pl
from jax.experimental.pallas import tpu as pltpu

F32 = jnp.float32
BF16 = jnp.bfloat16

N_DEV = 8
LANES = 128
D_MODEL = 1024
ATTN_HEAD_DIM = 64
ATTN_HEADS = 16
ATTN_SCALE = ATTN_HEAD_DIM ** -0.5
ATTN_BLK = 128
DILATIONS = (1, 4, 16)
ROPE_THETA = 10000.0
HGRN_HEADS = 8
HGRN_CHUNK = 64
D_FF = 4096
LN_EPS = 1e-5
RMS_EPS = 1e-6
DEPTH = 2
ALPHA = (2 * DEPTH) ** 0.25
ADAM_LR, ADAM_B1, ADAM_B2, ADAM_EPS, ADAM_WD, ADAM_STEP = 0.001, 0.9, 0.999, 1e-08, 0.01, 10
VMEM_LIMIT = 48 * 1024 * 1024

_NT = (((1,), (1,)), ((), ()))
_TN = (((0,), (0,)), ((), ()))


def _dot(a, b):
    return jnp.dot(a, b, preferred_element_type=F32)


def _dot_nt(a, b):
    return lax.dot_general(a, b, _NT, preferred_element_type=F32)


def _dot_tn(a, b):
    return lax.dot_general(a, b, _TN, preferred_element_type=F32)


def _split3(x):
    p1 = x.astype(BF16)
    r1 = x - p1.astype(F32)
    p2 = r1.astype(BF16)
    p3 = (r1 - p2.astype(F32)).astype(BF16)
    return p1, p2, p3


def _exact_dot(sel3, x):
    return _dot(sel3, jnp.concatenate(_split3(x), axis=0))


def _exact_dot_r(x, sel3):
    return _dot(jnp.concatenate(_split3(x), axis=1), sel3)


def _params(sem=None):
    return pltpu.CompilerParams(dimension_semantics=sem, vmem_limit_bytes=VMEM_LIMIT)


def _mm(a, b, mode, *, name, m, n, k, tm=1024, tn=1024, tk=1024, out_dtypes=(F32,), epi=None, a_pre=None,
        tile_extras=(), row_extras=(), vec_extras=(), row_outs=(), vec_outs=0, a_split=1, b_split=1, out_split=1,
        b_col_off=0, b_k_off=0, out_col_off=0, out_cols=None, alias=None, epi_wants_j=False, deps=(), t_out=False):
    tm, tn, tk = min(tm, m), min(tn, n), min(tk, k)
    assert m % tm == 0 and n % tn == 0 and k % tk == 0, (name, m, n, k, tm, tn, tk)
    gm, gn, gk = m // tm, n // tn, k // tk
    if mode in ("nn", "nt"):
        if a_split > 1 and tk == k:
            a_spec = pl.BlockSpec((a_split, tm, k // a_split), lambda i, j, kk: (0, i, 0))
        elif a_split > 1:
            kc = (k // a_split) // tk
            a_spec = pl.BlockSpec((None, tm, tk), lambda i, j, kk: (kk // kc, i, kk % kc))
        else:
            a_spec = pl.BlockSpec((tm, tk), lambda i, j, kk: (i, kk))
    else:
        a_spec = pl.BlockSpec((tk, tm), lambda i, j, kk: (kk, i))
    if mode in ("nn", "tn"):
        if b_split > 1:
            nc = (n // b_split) // tn
            b_spec = pl.BlockSpec((None, tk, tn), lambda i, j, kk: (j // nc, kk, j % nc))
        else:
            b_spec = pl.BlockSpec((tk, tn), lambda i, j, kk: (kk + b_k_off, j + b_col_off))
    else:
        b_spec = pl.BlockSpec((tn, tk), lambda i, j, kk: (j + b_col_off, kk + b_k_off))
    if out_split > 1:
        nco = (n // out_split) // tn
        o_spec = pl.BlockSpec((None, tm, tn), lambda i, j, kk: (j // nco, i, j % nco))
        o_shape = (out_split, m, n // out_split)
    else:
        o_spec = pl.BlockSpec((tm, tn), lambda i, j, kk: (i, j + out_col_off))
        o_shape = (m, out_cols if out_cols is not None else n)
    n_ex = len(tile_extras) + len(row_extras) + len(vec_extras)
    n_out = len(out_dtypes)
    n_plain = n_out + len(row_outs)
    assert not vec_outs or gn == 1
    if epi is None:
        def epi(acc):
            return (acc,)
    dot = {"nn": _dot, "nt": _dot_nt, "tn": _dot_tn}[mode]

    def body(*refs):
        a_ref, b_ref = refs[0], refs[1]
        ex = refs[2:2 + n_ex]
        outs = refs[2 + n_ex + (1 if alias is not None else 0) + len(deps):][:n_plain + vec_outs + (1 if t_out else 0)]
        ii = pl.program_id(0)
        jj = pl.program_id(1)

        def product():
            if a_split > 1 and tk == k:
                av = jnp.concatenate([a_ref[p] for p in range(a_split)], axis=1)
            else:
                av = a_ref[...]
            if a_pre is not None:
                av = a_pre(av)
            return dot(av.astype(BF16), b_ref[...].astype(BF16))

        def finish(total):
            lead = (jj,) if epi_wants_j else ()
            res = epi(*lead, total, *[e[...] for e in ex])
            for o, r in zip(outs[:n_plain], res):
                o[...] = r.astype(o.dtype)
            for o, r in zip(outs[n_plain:n_plain + vec_outs], res[n_plain:]):
                @pl.when(ii == 0)
                def _(o=o, r=r):
                    o[...] = r

                @pl.when(ii > 0)
                def _(o=o, r=r):
                    o[...] += r
            if t_out:
                outs[-1][...] = res[-1].astype(outs[-1].dtype)

        if gk == 1:
            finish(product())
        else:
            acc = refs[-1]
            kk = pl.program_id(2)

            @pl.when(kk == 0)
            def _():
                acc[...] = product()

            @pl.when(kk > 0)
            def _():
                acc[...] += product()

            @pl.when(kk == gk - 1)
            def _():
                finish(acc[...])

    in_specs = [a_spec, b_spec] + [o_spec] * len(tile_extras)
    in_specs += [pl.BlockSpec((tm, r.shape[1]), lambda i, j, kk: (i, 0)) if r.shape[0] == m else
                 pl.BlockSpec((r.shape[0], tm), lambda i, j, kk: (0, i)) for r in row_extras]
    in_specs += [pl.BlockSpec((1, tn), lambda i, j, kk: (0, j))] * len(vec_extras)
    args = [a, b] + list(tile_extras) + list(row_extras) + list(vec_extras)
    io_alias = {}
    if alias is not None:
        in_specs.append(pl.BlockSpec(memory_space=pl.ANY))
        args.append(alias)
        io_alias = {len(args) - 1: 0}
    in_specs += [pl.BlockSpec(memory_space=pl.ANY)] * len(deps)
    args += list(deps)
    out_specs = [o_spec] * n_out
    out_shape = [jax.ShapeDtypeStruct(o_shape, dt) for dt in out_dtypes]
    for dt, w in row_outs:
        out_specs.append(pl.BlockSpec((tm, w), lambda i, j, kk: (i, 0)))
        out_shape.append(jax.ShapeDtypeStruct((m, w), dt))
    out_specs += [pl.BlockSpec((1, tn), lambda i, j, kk: (0, j))] * vec_outs
    out_shape += [jax.ShapeDtypeStruct((1, n), F32)] * vec_outs
    if t_out:
        assert out_split > 1
        out_specs.append(pl.BlockSpec((None, tn, tm), lambda i, j, kk: (j // nco, j % nco, i)))
        out_shape.append(jax.ShapeDtypeStruct((out_split, n // out_split, m), out_dtypes[0]))
    out = pl.pallas_call(
        body, name=name, grid=(gm, gn, gk), in_specs=in_specs, out_specs=out_specs, out_shape=out_shape,
        scratch_shapes=[pltpu.VMEM((tm, tn), F32)] if gk > 1 else [],
        input_output_aliases=io_alias,
        compiler_params=_params(("arbitrary" if vec_outs else "parallel", "parallel", "arbitrary")),
    )(*args)
    return out[0] if len(out) == 1 else out


def _rope_tables(seq, dil):
    pos = jnp.arange(seq, dtype=jnp.int32).reshape(seq // dil, dil).T.reshape(seq)
    half = ATTN_HEAD_DIM // 2
    inv = ROPE_THETA ** (-jnp.arange(half, dtype=F32) * (2.0 / ATTN_HEAD_DIM))
    ang = pos.astype(F32)[:, None] * inv[None, :]
    cos, sin = jnp.cos(ang), jnp.sin(ang)
    reps = LANES // ATTN_HEAD_DIM
    return (jnp.tile(jnp.concatenate([cos, cos], axis=1), (1, reps)),
            jnp.tile(jnp.concatenate([-sin, sin], axis=1), (1, reps)))


def _rotate(x, c, ss, sign=1.0):
    w = x.shape[-1]
    half = ATTN_HEAD_DIM // 2
    lane = lax.broadcasted_iota(jnp.int32, x.shape, 1)
    first = (lane % ATTN_HEAD_DIM) < half
    partner = jnp.where(first, pltpu.roll(x, w - half, 1), pltpu.roll(x, half, 1))
    reps = w // LANES
    if reps > 1:
        c = jnp.concatenate([c] * reps, axis=1)
        ss = jnp.concatenate([ss] * reps, axis=1)
    return x * c + sign * (partner * ss)


def _rotate_t(xt, ct, sst):
    half = ATTN_HEAD_DIM // 2
    heads = xt.shape[0] // ATTN_HEAD_DIM
    parts = []
    for h in range(heads):
        lo = h * ATTN_HEAD_DIM
        parts += [xt[lo + half:lo + 2 * half], xt[lo:lo + half]]
    return (xt * jnp.concatenate([ct] * heads, axis=0)
            + jnp.concatenate(parts, axis=0) * jnp.concatenate([sst] * heads, axis=0))


def _ln_epi(acc, x, g, b):
    u = ALPHA * x + acc
    mu = jnp.mean(u, axis=-1, keepdims=True)
    uc = u - mu
    var = jnp.mean(uc * uc, axis=-1, keepdims=True)
    rstd = lax.rsqrt(var + LN_EPS)
    xh = uc * rstd
    out = xh * g + b
    return out, out, xh, rstd


def _ln_grad(dy, xh, rstd, g):
    dxh = dy * g
    m1 = jnp.mean(dxh, axis=-1, keepdims=True)
    m2 = jnp.mean(dxh * xh, axis=-1, keepdims=True)
    du = rstd * (dxh - m1 - xh * m2)
    return du, du, jnp.sum(dy * xh, axis=0, keepdims=True), jnp.sum(dy, axis=0, keepdims=True)


def _ln_grad_epi(acc, du_next, xh, rstd, g):
    return _ln_grad(acc + ALPHA * du_next, xh, rstd, g)


def _ln_loss_epi(acc, x_in, target, g, b):
    out, _, xh, rstd = _ln_epi(acc, x_in, g, b)
    e = out - target
    du, _, dg, db = _ln_grad(e * (1.0 / e.shape[-1]), xh, rstd, g)
    return du, du, dg, db, jnp.sum(e * e, axis=0, keepdims=True)


POS_BLK = 2048


def _class_rows(r, dil):
    return pl.ds(r, POS_BLK // dil, stride=dil) if dil > 1 else pl.ds(0, POS_BLK)


def _class_view(a, dil):
    s, w = a.shape
    return a.reshape(dil, s // dil, w)


def _class_spec(dil, all_tiles=True):
    if all_tiles:
        return pl.BlockSpec((dil, POS_BLK // dil, LANES), lambda i, t: (0, i, t))
    return pl.BlockSpec((dil, POS_BLK // dil, LANES), lambda i, t: (0, i, 0))


def _pos_spec(all_tiles=True):
    if all_tiles:
        return pl.BlockSpec((POS_BLK, LANES), lambda i, t: (i, t))
    return pl.BlockSpec((POS_BLK, LANES), lambda i, t: (i, 0))


def _prep_x(x, deps=()):
    s, d = x.shape

    def body(x_ref, *refs):
        outs = refs[len(deps):]
        for dil, o_ref in zip(DILATIONS, outs):
            for r in range(dil):
                o_ref[r] = x_ref[_class_rows(r, dil), :].astype(BF16)

    outs = pl.pallas_call(
        body, name="prep_x", grid=(s // POS_BLK, d // LANES),
        in_specs=[_pos_spec()] + [pl.BlockSpec(memory_space=pl.ANY)] * len(deps),
        out_specs=[_class_spec(dil) for dil in DILATIONS],
        out_shape=[jax.ShapeDtypeStruct((dil, s // dil, d), BF16) for dil in DILATIONS],
        compiler_params=_params(("parallel", "parallel")),
    )(x, *deps)
    return [o.reshape(s, d) for o in outs]


def _head_expand_matrix():
    h = lax.broadcasted_iota(jnp.int32, (LANES, D_MODEL), 0)
    l = lax.broadcasted_iota(jnp.int32, (LANES, D_MODEL), 1)
    return (l // ATTN_HEAD_DIM == h).astype(BF16)


def _attn_fwd(qkv, dil, *, name):
    _, s, d = qkv.shape
    nq = s // ATTN_BLK
    per = nq // dil
    tiles = d // LANES

    def body(q_ref, kc_ref, kp_ref, vc_ref, vp_ref, o_ref, lse_ref):
        qb = pl.program_id(0)
        first = (qb % per) == 0
        qi = lax.broadcasted_iota(jnp.int32, (ATTN_BLK, 2 * ATTN_BLK), 0)
        kj = lax.broadcasted_iota(jnp.int32, (ATTN_BLK, 2 * ATTN_BLK), 1)
        dist = qi + ATTN_BLK - kj
        valid = (dist >= 0) & (dist <= ATTN_BLK) & ((kj >= ATTN_BLK) | jnp.logical_not(first))
        lane = lax.broadcasted_iota(jnp.int32, (ATTN_BLK, LANES), 1)
        lse_tile = jnp.zeros((ATTN_BLK, LANES), F32)
        zero = jnp.zeros((), BF16)
        in_head = [(lane // ATTN_HEAD_DIM) == hh for hh in range(2)]
        for t0 in range(0, tiles, ATTN_FWD_TILE_GROUP):
            group = range(t0, t0 + ATTN_FWD_TILE_GROUP)
            heads = [(t, hh) for t in group for hh in range(2)]
            cols = {t: pl.ds(t * LANES, LANES) for t in group}
            k2 = {t: jnp.concatenate([kp_ref[:, cols[t]], kc_ref[:, cols[t]]], axis=0) for t in group}
            v2 = {t: jnp.concatenate([vp_ref[:, cols[t]], vc_ref[:, cols[t]]], axis=0) for t in group}
            sc = {(t, hh): jnp.where(valid, _dot_nt(jnp.where(in_head[hh], q_ref[:, cols[t]], zero), k2[t]),
                                     -jnp.inf) for t, hh in heads}
            mx = {i: jnp.max(sc[i], axis=-1, keepdims=True) for i in heads}
            p = {i: jnp.exp(sc[i] - mx[i]) for i in heads}
            l = {i: jnp.sum(p[i], axis=-1, keepdims=True) for i in heads}
            oh = {i: _dot(p[i].astype(BF16), v2[i[0]]) / l[i] for i in heads}
            for t in group:
                o_ref[:, cols[t]] = jnp.where(in_head[0], oh[t, 0], oh[t, 1]).astype(BF16)
                for hh in range(2):
                    lse_tile = jnp.where(lane == 2 * t + hh, mx[t, hh] + jnp.log(l[t, hh]), lse_tile)
        lse_ref[...] = lse_tile

    def blk(piece, prev):
        if prev:
            return pl.BlockSpec((None, ATTN_BLK, d), lambda i: (piece, jnp.maximum(i - 1, 0), 0))
        return pl.BlockSpec((None, ATTN_BLK, d), lambda i: (piece, i, 0))

    return pl.pallas_call(
        body, name=name, grid=(nq,),
        in_specs=[blk(0, False), blk(1, False), blk(1, True), blk(2, False), blk(2, True)],
        out_specs=[pl.BlockSpec((ATTN_BLK, d), lambda i: (i, 0)), pl.BlockSpec((ATTN_BLK, LANES), lambda i: (i, 0))],
        out_shape=[jax.ShapeDtypeStruct((s, d), BF16), jax.ShapeDtypeStruct((s, LANES), F32)],
        compiler_params=_params(("parallel",)),
    )(qkv, qkv, qkv, qkv, qkv)


def _attn_combine(os_, lses):
    s, d = os_[0].shape
    sel = jnp.concatenate([_head_expand_matrix()] * 3, axis=0)

    def body(o0, o1, o2, l0, l1, l2, sel_ref, of_ref, ob_ref, lt_ref, o_pos, l_pos):
        for g, (dil, o_ref, l_ref) in enumerate(zip(DILATIONS, (o0, o1, o2), (l0, l1, l2))):
            for r in range(dil):
                o_pos[g, _class_rows(r, dil), :] = o_ref[r].astype(F32)
                l_pos[g, _class_rows(r, dil), :] = l_ref[r]
        la, lb_, lc = l_pos[0], l_pos[1], l_pos[2]
        mx = jnp.maximum(jnp.maximum(la, lb_), lc)
        es = (jnp.exp(la - mx), jnp.exp(lb_ - mx), jnp.exp(lc - mx))
        z = es[0] + es[1] + es[2]
        lt_ref[...] = mx + jnp.log(z)
        acc = jnp.zeros((POS_BLK, LANES), F32)
        for g in range(3):
            acc += _exact_dot_r(es[g] / z, sel_ref[...]) * o_pos[g]
        of_ref[...] = acc
        ob_ref[...] = acc.astype(BF16)

    return pl.pallas_call(
        body, name="attn_combine", grid=(s // POS_BLK, d // LANES),
        in_specs=[_class_spec(dil) for dil in DILATIONS] + [_class_spec(dil, False) for dil in DILATIONS]
        + [pl.BlockSpec((3 * LANES, LANES), lambda i, t: (0, t))],
        out_specs=[_pos_spec(), _pos_spec(), _pos_spec(False)],
        out_shape=[jax.ShapeDtypeStruct((s, d), F32), jax.ShapeDtypeStruct((s, d), BF16),
                   jax.ShapeDtypeStruct((s, LANES), F32)],
        scratch_shapes=[pltpu.VMEM((3, POS_BLK, LANES), F32), pltpu.VMEM((3, POS_BLK, LANES), F32)],
        compiler_params=_params(("parallel", "arbitrary")),
    )(*[_class_view(o, dil) for o, dil in zip(os_, DILATIONS)],
      *[_class_view(l, dil) for l, dil in zip(lses, DILATIONS)], sel)


ATTN_TILE_GROUP = 2
ATTN_FWD_TILE_GROUP = 4
AUX_PER_TILE = 12


def _aux_placement():
    h = lax.broadcasted_iota(jnp.int32, (6, LANES, LANES), 1)
    l = lax.broadcasted_iota(jnp.int32, (6, LANES, LANES), 2)
    j = lax.broadcasted_iota(jnp.int32, (6, LANES, LANES), 0)
    target = AUX_PER_TILE * (h // 2) + 3 * (h % 2) + jnp.where(j < 3, j, 3 + j)
    return ((l == target) & (h < ATTN_HEADS)).astype(BF16)


def _attn_bwd_prep(do, o, lse):
    s, d = do.shape
    tiles = d // LANES
    sel_t = jnp.tile(_head_expand_matrix().T.reshape(tiles, LANES, LANES), (1, 3, 1))

    def body(do_ref, o_ref, l_ref, sel_ref, place_ref, *refs):
        outs, delta, aux = refs[:9], refs[9], refs[10]
        t = pl.program_id(1)
        part = _exact_dot_r(do_ref[...] * o_ref[...], sel_ref[...])

        @pl.when(t == 0)
        def _():
            delta[...] = part

        @pl.when(t > 0)
        def _():
            delta[...] += part

        for g, dil in enumerate(DILATIONS):
            for r in range(dil):
                blk = do_ref[_class_rows(r, dil), :].astype(BF16)
                outs[g][r] = blk
                outs[3 + g][r] = blk.T

        @pl.when(t == tiles - 1)
        def _():
            pieces = _split3(l_ref[...]) + _split3(delta[...])
            aux[...] = _dot(jnp.concatenate(pieces, axis=1), place_ref[...])
            for g, dil in enumerate(DILATIONS):
                for r in range(dil):
                    outs[6 + g][r] = aux[_class_rows(r, dil), :].astype(BF16)

    outs = pl.pallas_call(
        body, name="attn_bwd_prep", grid=(s // POS_BLK, tiles),
        in_specs=[_pos_spec(), _pos_spec(), _pos_spec(False),
                  pl.BlockSpec((None, 3 * LANES, LANES), lambda i, t: (t, 0, 0)),
                  pl.BlockSpec((6 * LANES, LANES), lambda i, t: (0, 0))],
        out_specs=[_class_spec(dil) for dil in DILATIONS]
        + [pl.BlockSpec((dil, LANES, POS_BLK // dil), lambda i, t: (0, t, i)) for dil in DILATIONS]
        + [_class_spec(dil, False) for dil in DILATIONS],
        out_shape=[jax.ShapeDtypeStruct((dil, s // dil, d), BF16) for dil in DILATIONS]
        + [jax.ShapeDtypeStruct((dil, d, s // dil), BF16) for dil in DILATIONS]
        + [jax.ShapeDtypeStruct((dil, s // dil, LANES), BF16) for dil in DILATIONS],
        scratch_shapes=[pltpu.VMEM((POS_BLK, LANES), F32), pltpu.VMEM((POS_BLK, LANES), F32)],
        compiler_params=_params(("parallel", "arbitrary")),
    )(do, o, lse, sel_t, _aux_placement().reshape(6 * LANES, LANES))
    return ([a.reshape(s, d) for a in outs[0:3]], list(outs[3:6]), [a.reshape(s, LANES) for a in outs[6:9]])


def _attn_bwd(qkv, qkv_t, do, do_t, aux, tables, dil, *, name):
    _, s, d = qkv.shape
    nq = s // ATTN_BLK
    per = nq // dil
    tiles = d // LANES
    half = ATTN_HEAD_DIM

    def body(qd_ref, qo_ref, k_ref, qtd_ref, qto_ref, kt_ref, vt_ref, dod_ref, doo_ref, dotd_ref, doto_ref,
             auxd_ref, auxo_ref, c_ref, ss_ref, out_ref, carry):
        kb = pl.program_id(0)

        @pl.when(kb == 0)
        def _():
            carry[...] = jnp.zeros_like(carry)

        has_next = (kb % per) != (per - 1)
        qi = lax.broadcasted_iota(jnp.int32, (ATTN_BLK, 2 * ATTN_BLK), 0)
        kj = lax.broadcasted_iota(jnp.int32, (ATTN_BLK, 2 * ATTN_BLK), 1) % ATTN_BLK
        valid_d = kj <= qi
        valid_o = (kj >= qi) & has_next
        lane = lax.broadcasted_iota(jnp.int32, (ATTN_BLK, LANES), 1)
        row = lax.broadcasted_iota(jnp.int32, (LANES, ATTN_BLK), 0)
        side = lax.broadcasted_iota(jnp.int32, (LANES, 2 * ATTN_BLK), 0)
        first = lax.broadcasted_iota(jnp.int32, (LANES, 2 * ATTN_BLK), 1) < ATTN_BLK
        c, ss = c_ref[...], ss_ref[...]
        zero = jnp.zeros((), BF16)
        sides = ((qd_ref, qtd_ref, dod_ref, dotd_ref, auxd_ref[...], valid_d),
                 (qo_ref, qto_ref, doo_ref, doto_ref, auxo_ref[...], valid_o))
        both = (0, 1)

        def head_halves(x, index):
            axis = 0 if index is lane else 1
            return jnp.concatenate([jnp.where(index < half, x, zero), jnp.where(index >= half, x, zero)], axis=axis)

        for t0 in range(0, tiles, ATTN_TILE_GROUP):
            group = range(t0, t0 + ATTN_TILE_GROUP)
            cols = {t: pl.ds(t * LANES, LANES) for t in group}
            kk, kk_t, vv_t = {}, {}, {}
            for t in group:
                base = AUX_PER_TILE * t
                hit = lambda lo: ((first & (side >= base + lo) & (side < base + lo + 3))
                                  | (jnp.logical_not(first) & (side >= base + lo + 3) & (side < base + lo + 6)))
                kk[t] = head_halves(k_ref[:, cols[t]], lane)
                kk_t[t] = jnp.concatenate([head_halves(kt_ref[cols[t], :], row),
                                           jnp.where(hit(0), -1.0, 0.0).astype(BF16)], axis=0)
                vv_t[t] = jnp.concatenate([head_halves(vt_ref[cols[t], :], row),
                                           jnp.where(hit(6), -1.0, 0.0).astype(BF16)], axis=0)
            sc = {(t, w): _dot(jnp.concatenate([sides[w][0][:, cols[t]], sides[w][4]], axis=1), kk_t[t])
                  for t in group for w in both}
            dpd = {(t, w): _dot(jnp.concatenate([sides[w][2][:, cols[t]], sides[w][4]], axis=1), vv_t[t])
                   for t in group for w in both}
            p = {i: jnp.where(sides[i[1]][5], jnp.exp(sc[i]), 0.0) for i in sc}
            ds = {i: (p[i] * dpd[i]).astype(BF16) for i in sc}
            pb = {i: p[i].astype(BF16) for i in sc}
            dv_t = {t: sum(_dot(sides[w][3][cols[t], :], pb[t, w]) for w in both) for t in group}
            dk_t = {t: sum(_dot(sides[w][1][cols[t], :], ds[t, w]) for w in both) for t in group}
            dq = {i: _dot(ds[i], kk[i[0]]) * ATTN_SCALE for i in sc}
            for t in group:
                dq_now = carry[:, cols[t]] + dq[t, 0]
                carry[:, cols[t]] = dq[t, 1]
                dk = jnp.where(row < half, dk_t[t][:, :ATTN_BLK], dk_t[t][:, ATTN_BLK:]).T
                dv = jnp.where(row < half, dv_t[t][:, :ATTN_BLK], dv_t[t][:, ATTN_BLK:]).T
                out_ref[0, :, cols[t]] = _rotate(dq_now, c, ss, -1.0).astype(BF16)
                out_ref[1, :, cols[t]] = _rotate(dk, c, ss, -1.0).astype(BF16)
                out_ref[2, :, cols[t]] = dv.astype(BF16)

    def nxt(i):
        return jnp.minimum(i + 1, nq - 1)

    def piece(p, shift):
        if shift:
            return pl.BlockSpec((None, ATTN_BLK, d), lambda i: (p, nxt(i), 0))
        return pl.BlockSpec((None, ATTN_BLK, d), lambda i: (p, i, 0))

    def piece_t(p, shift):
        if shift:
            return pl.BlockSpec((None, d, ATTN_BLK), lambda i: (p, 0, nxt(i)))
        return pl.BlockSpec((None, d, ATTN_BLK), lambda i: (p, 0, i))

    def rows(width, shift):
        if shift:
            return pl.BlockSpec((ATTN_BLK, width), lambda i: (nxt(i), 0))
        return pl.BlockSpec((ATTN_BLK, width), lambda i: (i, 0))

    def do_t_spec(shift):
        if shift:
            return pl.BlockSpec((None, d, ATTN_BLK), lambda i: (nxt(i) // per, 0, nxt(i) % per))
        return pl.BlockSpec((None, d, ATTN_BLK), lambda i: (i // per, 0, i % per))

    return pl.pallas_call(
        body, name=name, grid=(nq,),
        in_specs=[piece(0, False), piece(0, True), piece(1, False),
                  piece_t(0, False), piece_t(0, True), piece_t(1, False), piece_t(2, False),
                  rows(d, False), rows(d, True), do_t_spec(False), do_t_spec(True),
                  rows(LANES, False), rows(LANES, True), rows(LANES, False), rows(LANES, False)],
        out_specs=pl.BlockSpec((3, ATTN_BLK, d), lambda i: (0, i, 0)),
        out_shape=jax.ShapeDtypeStruct((3, s, d), BF16),
        scratch_shapes=[pltpu.VMEM((ATTN_BLK, d), F32)],
        compiler_params=_params(("arbitrary",)),
    )(qkv, qkv, qkv, qkv_t, qkv_t, qkv_t, qkv_t, do, do, do_t, do_t, aux, aux, *tables)


def _dx_combine(du, parts):
    s, d = du.shape

    def body(du_ref, p0, p1, p2, out_ref):
        out_ref[...] = ALPHA * du_ref[...] + p0[0].astype(F32)
        for dil, p_ref in zip(DILATIONS[1:], (p1, p2)):
            for r in range(dil):
                out_ref[_class_rows(r, dil), :] += p_ref[r].astype(F32)

    return pl.pallas_call(
        body, name="dx_combine", grid=(s // POS_BLK, d // LANES),
        in_specs=[_pos_spec()] + [_class_spec(dil) for dil in DILATIONS], out_specs=_pos_spec(),
        out_shape=jax.ShapeDtypeStruct((s, d), F32),
        compiler_params=_params(("parallel", "parallel")),
    )(du, *[_class_view(p, dil) for p, dil in zip(parts, DILATIONS)])


HGRN_ROWS = 2048


def _tri(lower, copies=1):
    i = lax.broadcasted_iota(jnp.int32, (HGRN_CHUNK, HGRN_CHUNK * copies), 0)
    j = lax.broadcasted_iota(jnp.int32, (HGRN_CHUNK, HGRN_CHUNK * copies), 1) % HGRN_CHUNK
    return (j <= i) if lower else (j >= i)


def _hgrn_gates(qr, z, lb):
    sq = jax.nn.sigmoid(qr)
    e = jnp.exp(-jnp.abs(z))
    big = 1.0 / (1.0 + e)
    small = e * big
    sg = jnp.where(z >= 0, big, small)
    sn = jnp.where(z >= 0, small, big)
    return sq, qr * sq, sg, sn, lb + (1.0 - lb) * sg, (1.0 - lb) * sn


def _hgrn_fwd(p2, lb, norm_g):
    _, s, d = p2.shape
    nblk = s // HGRN_ROWS
    cps = HGRN_ROWS // HGRN_CHUNK

    def body(p_ref, lb_ref, g_ref, o_ref, on_ref, st_ref, state):
        @pl.when(pl.program_id(1) == 0)
        def _():
            state[...] = jnp.zeros_like(state)

        chunks = [pl.ds(c * HGRN_CHUNK, HGRN_CHUNK) for c in range(cps)]
        ltri = _tri(True)
        lsel = _tri(True, 3).astype(BF16)
        _, q, _, _, f, key = _hgrn_gates(p_ref[0], p_ref[1], lb_ref[...])
        lf = jnp.log(f)
        v = [p_ref[2, rows, :].astype(BF16) for rows in chunks]
        b = [_exact_dot(lsel, lf[c * HGRN_CHUNK:(c + 1) * HGRN_CHUNK]) for c in range(cps)]
        b_last = [bc[HGRN_CHUNK - 1:HGRN_CHUNK, :] for bc in b]
        q = [q[c * HGRN_CHUNK:(c + 1) * HGRN_CHUNK] for c in range(cps)]
        key = [key[c * HGRN_CHUNK:(c + 1) * HGRN_CHUNK] for c in range(cps)]
        qd = [(q[c] * jnp.exp(b[c])).astype(BF16) for c in range(cps)]
        kd = [(key[c] * jnp.exp(-b[c])).astype(BF16) for c in range(cps)]
        k2 = [(key[c] * jnp.exp(b_last[c] - b[c])).astype(BF16) for c in range(cps)]
        a = [jnp.where(ltri, _dot_nt(qd[c], kd[c]), 0.0).astype(BF16) for c in range(cps)]
        kv = [_dot_tn(v[c], k2[c]) for c in range(cps)]
        st, sts = state[...], []
        for c in range(cps):
            sts.append(st)
            st_ref[0, c] = st
            st = st * jnp.exp(b_last[c]) + kv[c]
        state[...] = st
        gv = g_ref[...]
        for c in range(cps):
            o = _dot(a[c], v[c]) + _dot_nt(qd[c], sts[c].astype(BF16))
            o_ref[chunks[c], :] = o
            r = lax.rsqrt(jnp.mean(o * o, axis=-1, keepdims=True) + RMS_EPS)
            on_ref[chunks[c], :] = (o * r * gv).astype(BF16)

    vec = pl.BlockSpec((1, LANES), lambda h, c: (0, h))
    col = pl.BlockSpec((HGRN_ROWS, LANES), lambda h, c: (c, h))
    return pl.pallas_call(
        body, name="hgrn_fwd", grid=(HGRN_HEADS, nblk),
        in_specs=[pl.BlockSpec((3, HGRN_ROWS, LANES), lambda h, c: (0, c, h)), vec, vec],
        out_specs=[col, col, pl.BlockSpec((1, cps, LANES, LANES), lambda h, c: (h, c, 0, 0))],
        out_shape=[jax.ShapeDtypeStruct((s, d), F32), jax.ShapeDtypeStruct((s, d), BF16),
                   jax.ShapeDtypeStruct((HGRN_HEADS, s // HGRN_CHUNK, LANES, LANES), F32)],
        scratch_shapes=[pltpu.VMEM((LANES, LANES), F32)],
        compiler_params=_params(("parallel", "arbitrary")),
    )(p2, lb, norm_g)


def _hgrn_bwd(p2, lb, norm_g, o_raw, states, dyn):
    _, s, d = p2.shape
    nblk = s // HGRN_ROWS
    cps = HGRN_ROWS // HGRN_CHUNK

    def body(p_ref, lb_ref, g_ref, o_ref, st_ref, dy_ref, dp_ref, dg_ref, dlb_ref, dstate):
        @pl.when(pl.program_id(1) == 0)
        def _():
            dstate[...] = jnp.zeros_like(dstate)
            dg_ref[...] = jnp.zeros_like(dg_ref)
            dlb_ref[...] = jnp.zeros_like(dlb_ref)

        n = cps
        cut = lambda t: [t[c * HGRN_CHUNK:(c + 1) * HGRN_CHUNK] for c in range(n)]
        chunks = [pl.ds(c * HGRN_CHUNK, HGRN_CHUNK) for c in range(n)]
        lbv = lb_ref[...]
        ltri = _tri(True)
        lsel = _tri(True, 3).astype(BF16)
        usel = _tri(False, 3).astype(BF16)
        last_row = lax.broadcasted_iota(jnp.int32, (HGRN_CHUNK, LANES), 0) == HGRN_CHUNK - 1
        qr, z = p_ref[0], p_ref[1]
        sq, q, sg, sn, f, key = _hgrn_gates(qr, z, lbv)
        lf = jnp.log(f)
        v = [p_ref[2, rows, :].astype(BF16) for rows in chunks]
        o, dyv = o_ref[...], dy_ref[...]
        r = lax.rsqrt(jnp.mean(o * o, axis=-1, keepdims=True) + RMS_EPS)
        oh = o * r
        doh = dyv * g_ref[...]
        do = cut((r * (doh - oh * jnp.mean(doh * oh, axis=-1, keepdims=True))).astype(BF16))
        dg_ref[...] += jnp.sum(dyv * oh, axis=0, keepdims=True)
        b = [_exact_dot(lsel, lfc) for lfc in cut(lf)]
        b_last = [bc[HGRN_CHUNK - 1:HGRN_CHUNK, :] for bc in b]
        q, key = cut(q), cut(key)
        eb = [jnp.exp(bc) for bc in b]
        enb = [jnp.exp(-bc) for bc in b]
        e2 = [jnp.exp(b_last[c] - b[c]) for c in range(n)]
        dec = [jnp.exp(bl) for bl in b_last]
        qd_f = [q[c] * eb[c] for c in range(n)]
        kd_f = [key[c] * enb[c] for c in range(n)]
        k2_f = [key[c] * e2[c] for c in range(n)]
        qd, kd, k2 = ([t.astype(BF16) for t in ts] for ts in (qd_f, kd_f, k2_f))
        a = [jnp.where(ltri, _dot_nt(qd[c], kd[c]), 0.0).astype(BF16) for c in range(n)]
        da = [jnp.where(ltri, _dot_nt(do[c], v[c]), 0.0).astype(BF16) for c in range(n)]
        grow = [_dot_tn(do[c], qd[c]) for c in range(n)]
        dst, dsts = dstate[...], [None] * n
        for c in reversed(range(n)):
            dsts[c] = dst
            dst = dst * dec[c] + grow[c]
        dstate[...] = dst
        st = [st_ref[0, c] for c in range(n)]
        dstb = [t.astype(BF16) for t in dsts]
        dv = [_dot_tn(a[c], do[c]) + _dot_nt(k2[c], dstb[c]) for c in range(n)]
        dqd = [_dot(da[c], kd[c]) + _dot(do[c], st[c].astype(BF16)) for c in range(n)]
        dkd = [_dot_tn(da[c], qd[c]) for c in range(n)]
        dk2 = [_dot(v[c], dstb[c]) for c in range(n)]
        db = []
        for c in range(n):
            ddec = jnp.sum(dsts[c] * st[c], axis=0, keepdims=True)
            db_last = jnp.sum(dk2[c] * k2_f[c], axis=0, keepdims=True) + ddec * dec[c]
            db.append(dqd[c] * qd_f[c] - dkd[c] * kd_f[c] - dk2[c] * k2_f[c] + jnp.where(last_row, db_last, 0.0))
        dlf = [_exact_dot(usel, dbc) for dbc in db]
        f, sg, sn, sq, qr = cut(f), cut(sg), cut(sn), cut(sq), cut(qr)
        dlb_acc = jnp.zeros((1, LANES), F32)
        for c in range(n):
            dkey = dkd[c] * enb[c] + dk2[c] * e2[c]
            common = dlf[c] / f[c] - dkey
            dlb_acc += jnp.sum(common * sn[c], axis=0, keepdims=True)
            dp_ref[0, chunks[c], :] = (dqd[c] * eb[c] * (sq[c] * (1.0 + qr[c] * (1.0 - sq[c])))).astype(BF16)
            dp_ref[1, chunks[c], :] = (common * ((1.0 - lbv) * sg[c] * sn[c])).astype(BF16)
            dp_ref[2, chunks[c], :] = dv[c].astype(BF16)
        dlb_ref[...] += dlb_acc

    def rev(c):
        return nblk - 1 - c

    vec = pl.BlockSpec((1, LANES), lambda h, c: (0, h))
    col = pl.BlockSpec((HGRN_ROWS, LANES), lambda h, c: (rev(c), h))
    p3 = pl.BlockSpec((3, HGRN_ROWS, LANES), lambda h, c: (0, rev(c), h))
    return pl.pallas_call(
        body, name="hgrn_bwd", grid=(HGRN_HEADS, nblk),
        in_specs=[p3, vec, vec, col, pl.BlockSpec((1, cps, LANES, LANES), lambda h, c: (h, rev(c), 0, 0)), col],
        out_specs=[p3, vec, vec],
        out_shape=[jax.ShapeDtypeStruct((3, s, d), BF16), jax.ShapeDtypeStruct((1, d), F32),
                   jax.ShapeDtypeStruct((1, d), F32)],
        scratch_shapes=[pltpu.VMEM((LANES, LANES), F32)],
        compiler_params=_params(("parallel", "arbitrary")),
    )(p2, lb, norm_g, o_raw, states, dyn)


SMALL_ROWS = 16
FUSED_TM = 512


def _relu2(h):
    r = jnp.maximum(h.astype(F32), 0.0)
    return r * r


def _dact_epi(acc, h):
    return (acc * (2.0 * jnp.maximum(h.astype(F32), 0.0)),)


def _rope_epi(j, acc, ct, sst):
    acc_t = acc.T
    rot_t = _rotate_t(acc_t, ct, sst) * jnp.where(j == 0, ATTN_SCALE, 1.0)
    out_t = jnp.where(j >= 2, acc_t, rot_t).astype(BF16)
    return out_t.T, out_t


def _local_step(x, target, get_w, small, on_grads, deps=()):
    s, d = x.shape
    lb = small["lb"]
    gq = 3 * d
    gt = gq // 1024

    def ln_of(a, w, x_in, which, layer, *, name, k, a_pre=None, deps=()):
        g, b = small[f"ln_{which}_g"][layer:layer + 1], small[f"ln_{which}_b"][layer:layer + 1]
        return _mm(a, w, "nn", name=name, m=s, n=d, k=k, tm=FUSED_TM, tk=k, a_pre=a_pre, out_dtypes=(F32, BF16, F32),
                   row_outs=((F32, 1),), epi=_ln_epi, tile_extras=(x_in,), vec_extras=(g, b), deps=deps)

    def ln_grad_of(a, w, du_next, xh, rstd, which, layer, *, name, k, a_split=1, deps=()):
        return _mm(a, w, "nt", name=name, m=s, n=d, k=k, tm=FUSED_TM, tk=k, a_split=a_split,
                   out_dtypes=(F32, BF16), vec_outs=2, epi=_ln_grad_epi, tile_extras=(du_next, xh), row_extras=(rstd,),
                   vec_extras=(small[f"ln_{which}_g"][layer:layer + 1],), deps=deps)

    def ffn_fwd(xb, x_in, w_up, w_down, tag):
        h = _mm(xb, w_up, "nn", name=f"ffn_up_{tag}", m=s, n=D_FF, k=d, tm=2048, out_dtypes=(BF16,))
        return (h,) + tuple(ln_of(h, w_down, x_in, "ffn", tag, name=f"ffn_down_{tag}", k=D_FF, a_pre=_relu2))

    def ffn_bwd(du, dub, xb, h, w_up, w_down, xh, rstd, tag):
        dh = _mm(dub, w_down, "nt", name=f"ffn_dact_{tag}", m=s, n=D_FF, k=d, tm=2048, out_dtypes=(BF16,), epi=_dact_epi,
                 tile_extras=(h,))
        g_down = _mm(h, dub, "tn", name=f"ffn_gdown_{tag}", m=D_FF, n=d, k=s, tm=512, tk=s, a_pre=_relu2,
                     out_dtypes=(BF16,))
        g_up = _mm(xb, dh, "tn", name=f"ffn_gup_{tag}", m=d, n=D_FF, k=s, tk=s, out_dtypes=(BF16,))
        after = on_grads(f"ffn{tag}", {f"ffn_w_down{tag}": g_down, f"ffn_w_up{tag}": g_up})
        return ln_grad_of(dh, w_up, du, xh, rstd, "mix", tag, name=f"ffn_dx_{tag}", k=D_FF, deps=after)

    xs = _prep_x(x, deps)
    w_ain, after = get_w("attn_w_in", None)
    tabs, qkvs, qkv_ts, o_parts, lse_parts = [], [], [], [], []
    for g, dil in enumerate(DILATIONS):
        tabs.append(_rope_tables(s, dil))
        tabs_t = [t[:, :ATTN_HEAD_DIM].T for t in tabs[g]]
        qkv, qkv_t = _mm(xs[g], w_ain, "nn", name=f"attn_in_{g}", m=s, n=gq, k=d, b_col_off=g * gt, out_split=3,
                         out_dtypes=(BF16,), epi=_rope_epi, epi_wants_j=True, row_extras=tabs_t, t_out=True,
                         deps=after if g == 0 else ())
        qkvs.append(qkv)
        qkv_ts.append(qkv_t)
        o_g, lse_g = _attn_fwd(qkv, dil, name=f"attn_fwd_{g}")
        o_parts.append(o_g)
        lse_parts.append(lse_g)
    o_f, o_b, lse_t = _attn_combine(o_parts, lse_parts)
    w_aout, after = get_w("attn_w_out", o_b)
    x1, x1b, xh1, r1 = ln_of(o_b, w_aout, x, "mix", 0, name="attn_out", k=d, deps=after)
    w_up0, w_down0 = get_w("ffn_w_up0", o_b)[0], get_w("ffn_w_down0", o_b)[0]
    h0, x2, x2b, xh2, r2 = ffn_fwd(x1b, x1, w_up0, w_down0, 0)
    w_hin, w_hout, norm_g = get_w("hgrn_w_in", x2b)[0], get_w("hgrn_w_out", x2b)[0], get_w("hgrn_norm_g", x2b)[0]
    p2 = _mm(x2b, w_hin, "nn", name="hgrn_in", m=s, n=3 * d, k=d, tm=2048, out_split=3)
    o_raw, o_n, states = _hgrn_fwd(p2, lb, norm_g)
    w_up1, w_down1 = get_w("ffn_w_up1", o_n)[0], get_w("ffn_w_down1", o_n)[0]
    x3, x3b, xh3, r3 = ln_of(o_n, w_hout, x2, "mix", 1, name="hgrn_out", k=d)
    h1 = _mm(x3b, w_up1, "nn", name="ffn_up_1", m=s, n=D_FF, k=d, tm=2048, out_dtypes=(BF16,))
    du, dub, dg_ffn1, db_ffn1, sq = _mm(
        h1, w_down1, "nn", name="ffn_down_1", m=s, n=d, k=D_FF, tm=FUSED_TM, tk=D_FF, a_pre=_relu2,
        out_dtypes=(F32, BF16), vec_outs=3, epi=_ln_loss_epi, tile_extras=(x3, target),
        vec_extras=(small["ln_ffn_g"][1:2], small["ln_ffn_b"][1:2]))
    du, dub, dg_mix1, db_mix1 = ffn_bwd(du, dub, x3b, h1, w_up1, w_down1, xh3, r3, 1)
    dyn = _mm(dub, w_hout, "nt", name="hgrn_dout", m=s, n=d, k=d)
    g_hout = _mm(o_n, dub, "tn", name="hgrn_gout", m=d, n=d, k=s, out_dtypes=(BF16,))
    dp2, d_norm_g, d_lb = _hgrn_bwd(p2, lb, norm_g, o_raw, states, dyn)
    g_hin = _mm(x2b, dp2, "tn", name="hgrn_gin", m=d, n=3 * d, k=s, tk=s, b_split=3, out_dtypes=(BF16,))
    after = on_grads("hgrn", {"hgrn_w_out": g_hout, "hgrn_w_in": g_hin})
    du, dub, dg_ffn0, db_ffn0 = ln_grad_of(dp2, w_hin, du, xh2, r2, "ffn", 0, name="hgrn_dx", k=3 * d, a_split=3,
                                           deps=after)
    du, dub, dg_mix0, db_mix0 = ffn_bwd(du, dub, x1b, h0, w_up0, w_down0, xh1, r1, 0)
    g_aout = _mm(o_b, dub, "tn", name="attn_gout", m=d, n=d, k=s, out_dtypes=(BF16,))
    sm1 = jax.nn.softmax(small["lb_logits"], axis=0)
    d_l1 = d_lb * (sm1[0:1] * sm1[1:2])
    zeros = jnp.zeros((SMALL_ROWS - 12, d), F32)
    loss_row = jnp.broadcast_to(0.5 * jnp.sum(sq) / d, (1, d))
    small_grads = jnp.concatenate([dg_mix0, dg_mix1, db_mix0, db_mix1, dg_ffn0, dg_ffn1, db_ffn0, db_ffn1,
                                   -d_l1, d_l1, d_norm_g, loss_row, zeros], axis=0)
    after = on_grads("attn_out", {"attn_w_out": g_aout, "small": small_grads})
    do = _mm(dub, w_aout, "nt", name="attn_dout", m=s, n=d, k=d, deps=after)
    do_parts, do_ts, aux_parts = _attn_bwd_prep(do, o_f, lse_t)
    g_ain, dqkvs = None, []
    for g, dil in enumerate(DILATIONS):
        dqkvs.append(_attn_bwd(qkvs[g], qkv_ts[g], do_parts[g], do_ts[g], aux_parts[g], tabs[g], dil,
                               name=f"attn_bwd_{g}"))
        g_ain = _mm(xs[g], dqkvs[g], "tn", name=f"attn_gin_{g}", m=d, n=gq, k=s, tk=s, b_split=3, out_dtypes=(BF16,),
                    out_col_off=g * gt, out_cols=3 * gq, alias=g_ain)
    after = on_grads("attn_in", {"attn_w_in": g_ain})
    dx_parts = [_mm(dqkvs[g], w_ain, "nt", name=f"attn_dx_{g}", m=s, n=d, k=gq, tm=FUSED_TM, tk=gq, a_split=3, b_k_off=g,
                    out_dtypes=(BF16,), deps=after if g == 0 else ())
                for g in range(len(DILATIONS))]
    return _dx_combine(du, dx_parts)


def _mesh_place():
    x, y, c = lax.axis_index("x"), lax.axis_index("y"), lax.axis_index("c")
    return x, y, c, 4 * x + 2 * y + c


def _peer(x, y, c, k):
    px = 1 - x if (k >> 2) & 1 else x
    py = 1 - y if (k >> 1) & 1 else y
    pc = 1 - c if k & 1 else c
    return (px, py, pc), 4 * px + 2 * py + pc


def _window(ref, axis, size, idx):
    if axis is None:
        return ref
    sl = [slice(None)] * len(ref.shape)
    sl[axis] = pl.ds(idx * size, size)
    return ref.at[tuple(sl)]


_HBM = pl.BlockSpec(memory_space=pltpu.HBM)
_SEM = pl.BlockSpec(memory_space=pltpu.SEMAPHORE)
_EFFECT = pltpu.SideEffectType.DATAFLOW_SIDE_EFFECTING


def _xchg_ends(src_ref, land_ref, axis, gather, me, other):
    if gather:
        size = src_ref.shape[axis]
        return src_ref, _window(land_ref, axis, size, me), _window(land_ref, axis, size, other)
    size = None if axis is None else src_ref.shape[axis] // N_DEV
    return _window(src_ref, axis, size, other), land_ref.at[me], land_ref.at[other]


def _xchg_start(srcs, lands, axes, *, gather, name, deps=(), peers=tuple(range(1, N_DEV))):
    n = len(srcs)
    nd = len(deps)

    def body(*refs):
        src_refs, land_refs = refs[:n], refs[n:2 * n]
        send, recv = refs[2 * n + nd:3 * n + nd], refs[3 * n + nd:4 * n + nd]
        token = refs[-1]
        x, y, c, me = _mesh_place()
        for k in peers:
            peer, pidx = _peer(x, y, c, k)
            for i in range(n):
                src, dst, _ = _xchg_ends(src_refs[i], land_refs[i], axes[i], gather, me, pidx)
                pltpu.make_async_remote_copy(
                    src_ref=src, dst_ref=dst, send_sem=send[i].at[k - 1], recv_sem=recv[i].at[k - 1],
                    device_id=peer, device_id_type=pl.DeviceIdType.MESH).start()
        for i in range(n):
            src, dst, _ = _xchg_ends(src_refs[i], land_refs[i], axes[i], gather, me, me)
            pltpu.make_async_copy(src, dst, send[i].at[N_DEV - 1]).start()
        token[...] = jnp.zeros_like(token)

    bufs = list(srcs) + list(lands)
    outs = pl.pallas_call(
        body, name=name,
        out_shape=[pltpu.SemaphoreType.DMA((N_DEV,))] * (2 * n) + [pltpu.HBM(b.shape, b.dtype) for b in bufs]
        + [jax.ShapeDtypeStruct((8, LANES), F32)],
        in_specs=[_HBM] * (2 * n) + [pl.BlockSpec(memory_space=pl.ANY)] * nd,
        out_specs=[_SEM] * (2 * n) + [_HBM] * (2 * n) + [pl.BlockSpec(memory_space=pltpu.VMEM)],
        input_output_aliases={i: 2 * n + i for i in range(2 * n)},
        compiler_params=pltpu.CompilerParams(has_side_effects=_EFFECT),
    )(*[pltpu.with_memory_space_constraint(b, pltpu.HBM) for b in bufs], *deps)
    return dict(send=outs[:n], recv=outs[n:2 * n], srcs=outs[2 * n:3 * n], lands=outs[3 * n:4 * n], token=outs[-1],
                axes=list(axes), gather=gather)


SIBLING = 1
SAME_CORE = (2, 4, 6)


def _gather_relay(xc, after, *, name):
    axis = xc["axes"][0]

    def body(src_ref, land_ref, send1, recv1, after_ref, src_out, land_out, send2, recv2):
        x, y, c, me = _mesh_place()
        sibling, _ = _peer(x, y, c, SIBLING)
        for j, k in enumerate(SAME_CORE):
            peer, pidx = _peer(x, y, c, k)
            src, _, got = _xchg_ends(src_ref, land_ref, axis, True, me, pidx)
            pltpu.make_async_remote_copy(
                src_ref=src, dst_ref=got, send_sem=send1.at[k - 1], recv_sem=recv1.at[k - 1],
                device_id=peer, device_id_type=pl.DeviceIdType.MESH).wait_recv()
            pltpu.make_async_remote_copy(
                src_ref=got, dst_ref=got, send_sem=send2.at[j], recv_sem=recv2.at[j],
                device_id=sibling, device_id_type=pl.DeviceIdType.MESH).start()

    src, land = xc["srcs"][0], xc["lands"][0]
    outs = pl.pallas_call(
        body, name=name,
        out_shape=[pltpu.HBM(src.shape, src.dtype), pltpu.HBM(land.shape, land.dtype)]
        + [pltpu.SemaphoreType.DMA((len(SAME_CORE),))] * 2,
        in_specs=[_HBM, _HBM, _SEM, _SEM, pl.BlockSpec(memory_space=pl.ANY)],
        out_specs=[_HBM, _HBM, _SEM, _SEM], input_output_aliases={0: 0, 1: 1},
        compiler_params=pltpu.CompilerParams(has_side_effects=_EFFECT),
    )(src, land, xc["send"][0], xc["recv"][0], after)
    return dict(src=outs[0], land=outs[1], send=outs[2], recv=outs[3])


def _gather_relay_wait(xc, relay, after, *, name):
    axis = xc["axes"][0]

    def body(src_ref, land_ref, send1, recv1, send2, recv2, after_ref, src_out, land_out):
        x, y, c, me = _mesh_place()
        sibling, sidx = _peer(x, y, c, SIBLING)
        for k in (SIBLING,) + SAME_CORE:
            peer, pidx = _peer(x, y, c, k)
            src, dst, got = _xchg_ends(src_ref, land_ref, axis, True, me, pidx)
            pltpu.make_async_remote_copy(
                src_ref=src, dst_ref=dst, send_sem=send1.at[k - 1], recv_sem=recv1.at[k - 1],
                device_id=peer, device_id_type=pl.DeviceIdType.MESH).wait_send()
        src, dst, got = _xchg_ends(src_ref, land_ref, axis, True, me, sidx)
        pltpu.make_async_remote_copy(
            src_ref=src, dst_ref=got, send_sem=send1.at[SIBLING - 1], recv_sem=recv1.at[SIBLING - 1],
            device_id=sibling, device_id_type=pl.DeviceIdType.MESH).wait_recv()
        src, dst, _ = _xchg_ends(src_ref, land_ref, axis, True, me, me)
        pltpu.make_async_copy(src, dst, send1.at[N_DEV - 1]).wait()
        for j, k in enumerate(SAME_CORE):
            _, pidx = _peer(x, y, c, k)
            _, qidx = _peer(x, y, c, k ^ SIBLING)
            _, _, sent = _xchg_ends(src_ref, land_ref, axis, True, me, pidx)
            _, _, got = _xchg_ends(src_ref, land_ref, axis, True, me, qidx)
            pltpu.make_async_remote_copy(
                src_ref=sent, dst_ref=sent, send_sem=send2.at[j], recv_sem=recv2.at[j],
                device_id=sibling, device_id_type=pl.DeviceIdType.MESH).wait_send()
            pltpu.make_async_remote_copy(
                src_ref=got, dst_ref=got, send_sem=send2.at[j], recv_sem=recv2.at[j],
                device_id=sibling, device_id_type=pl.DeviceIdType.MESH).wait_recv()

    outs = pl.pallas_call(
        body, name=name, out_shape=[pltpu.HBM(relay["src"].shape, relay["src"].dtype),
                                    pltpu.HBM(relay["land"].shape, relay["land"].dtype)],
        in_specs=[_HBM, _HBM, _SEM, _SEM, _SEM, _SEM, pl.BlockSpec(memory_space=pl.ANY)],
        out_specs=[_HBM, _HBM], input_output_aliases={0: 0, 1: 1},
        compiler_params=pltpu.CompilerParams(has_side_effects=_EFFECT),
    )(relay["src"], relay["land"], xc["send"][0], xc["recv"][0], relay["send"], relay["recv"], after)
    return outs[1]


def _xchg_wait(xc, items, after, *, name):
    m = len(items)
    gather = xc["gather"]
    axes = [xc["axes"][i] for i in items]

    def body(*refs):
        src_refs, land_refs = refs[:m], refs[m:2 * m]
        send, recv = refs[2 * m:3 * m], refs[3 * m:4 * m]
        x, y, c, me = _mesh_place()
        for k in range(1, N_DEV):
            peer, pidx = _peer(x, y, c, k)
            for j in range(m):
                src, dst, got = _xchg_ends(src_refs[j], land_refs[j], axes[j], gather, me, pidx)
                pltpu.make_async_remote_copy(
                    src_ref=src, dst_ref=dst, send_sem=send[j].at[k - 1], recv_sem=recv[j].at[k - 1],
                    device_id=peer, device_id_type=pl.DeviceIdType.MESH).wait_send()
                pltpu.make_async_remote_copy(
                    src_ref=src, dst_ref=got, send_sem=send[j].at[k - 1], recv_sem=recv[j].at[k - 1],
                    device_id=peer, device_id_type=pl.DeviceIdType.MESH).wait_recv()
        for j in range(m):
            src, dst, _ = _xchg_ends(src_refs[j], land_refs[j], axes[j], gather, me, me)
            pltpu.make_async_copy(src, dst, send[j].at[N_DEV - 1]).wait()

    bufs = [xc["srcs"][i] for i in items] + [xc["lands"][i] for i in items]
    sems = [xc["send"][i] for i in items] + [xc["recv"][i] for i in items]
    outs = pl.pallas_call(
        body, name=name, out_shape=[pltpu.HBM(b.shape, b.dtype) for b in bufs],
        in_specs=[_HBM] * (2 * m) + [_SEM] * (2 * m) + [pl.BlockSpec(memory_space=pl.ANY)],
        out_specs=[_HBM] * (2 * m), input_output_aliases={j: j for j in range(2 * m)},
        compiler_params=pltpu.CompilerParams(has_side_effects=_EFFECT),
    )(*bufs, *sems, after)
    return outs[m:]


def _cast_bf16(a, *, name):
    r, c = a.shape
    tr = min(r, 512)

    def body(a_ref, o_ref):
        o_ref[...] = a_ref[...].astype(BF16)

    spec = pl.BlockSpec((tr, c), lambda i: (i, 0))
    return pl.pallas_call(body, name=name, grid=(r // tr,), in_specs=[spec], out_specs=spec,
                          out_shape=jax.ShapeDtypeStruct((r, c), BF16), compiler_params=_params(("parallel",)))(a)


def _adamw(slabs, w, m, v, *, name):
    layers, r, c = w.shape
    tr = min(r, 256)

    def body(*refs):
        s_refs = refs[:layers]
        w_ref, m_ref, v_ref, g_ref, d_ref, mo_ref, vo_ref = refs[layers:]
        for l in range(layers):
            g = s_refs[l][0].astype(F32)
            for i in range(1, N_DEV):
                g = g + s_refs[l][i].astype(F32)
            m2 = ADAM_B1 * m_ref[l] + (1.0 - ADAM_B1) * g
            v2 = ADAM_B2 * v_ref[l] + (1.0 - ADAM_B2) * (g * g)
            m_hat = m2 / (1.0 - ADAM_B1 ** ADAM_STEP)
            v_hat = v2 / (1.0 - ADAM_B2 ** ADAM_STEP)
            g_ref[l] = g
            d_ref[l] = -ADAM_LR * (m_hat / (jnp.sqrt(v_hat) + ADAM_EPS) + ADAM_WD * w_ref[l])
            mo_ref[l] = m2
            vo_ref[l] = v2

    spec = pl.BlockSpec((layers, tr, c), lambda i: (0, i, 0))
    return pl.pallas_call(
        body, name=name, grid=(r // tr,),
        in_specs=[pl.BlockSpec((N_DEV, tr, c), lambda i: (0, i, 0))] * layers + [spec, spec, spec],
        out_specs=[spec] * 4, out_shape=[jax.ShapeDtypeStruct((layers, r, c), F32)] * 4,
        compiler_params=_params(("parallel",)),
    )(*slabs, w, m, v)


GATHER_AXIS = {"attn_w_in": 1, "attn_w_out": 0, "ffn_w_up0": 1, "ffn_w_down0": 0, "hgrn_w_in": 1, "hgrn_w_out": 0,
               "hgrn_norm_g": 1, "ffn_w_up1": 1, "ffn_w_down1": 0}
GATHER_STAGES = (("attn_w_in",), ("attn_w_out", "ffn_w_up0", "ffn_w_down0", "hgrn_w_in", "hgrn_w_out", "hgrn_norm_g"),
                 ("ffn_w_up1", "ffn_w_down1"))
GATHER_WAITS = ((("attn_w_in",), 0, 1), (("attn_w_out", "ffn_w_up0", "ffn_w_down0"), 1, 2),
                (("hgrn_w_in", "hgrn_w_out", "hgrn_norm_g"), 1, None), (("ffn_w_up1", "ffn_w_down1"), 2, None))
SCATTER_AXIS = dict(GATHER_AXIS, small=None)
BIG = ("attn_w_in", "attn_w_out", "hgrn_w_in", "hgrn_w_out", "ffn_w_up", "ffn_w_down")
SMALL = ("lb_logits", "ln_mix_g", "ln_mix_b", "ln_ffn_g", "ln_ffn_b")
SMALL_ROW = {"ln_mix_g": 0, "ln_mix_b": 2, "ln_ffn_g": 4, "ln_ffn_b": 6, "lb_logits": 8}
NORM_G_ROW = 10
LOSS_ROW = 11


def kernel(x, attn_w_in, attn_w_out, hgrn_w_in, hgrn_w_out, hgrn_norm_g, lb_logits, ln_mix_g, ln_mix_b, ln_ffn_g, ln_ffn_b, ffn_w_up, ffn_w_down, loss_target, m_attn_w_in, m_attn_w_out, m_hgrn_w_in, m_hgrn_w_out, m_hgrn_norm_g, m_lb_logits, m_ln_mix_g, m_ln_mix_b, m_ln_ffn_g, m_ln_ffn_b, m_ffn_w_up, m_ffn_w_down, v_attn_w_in, v_attn_w_out, v_hgrn_w_in, v_hgrn_w_out, v_hgrn_norm_g, v_lb_logits, v_ln_mix_g, v_ln_mix_b, v_ln_ffn_g, v_ln_ffn_b, v_ffn_w_up, v_ffn_w_down):
    wts = dict(attn_w_in=attn_w_in, attn_w_out=attn_w_out, hgrn_w_in=hgrn_w_in, hgrn_w_out=hgrn_w_out,
               hgrn_norm_g=hgrn_norm_g, lb_logits=lb_logits, ln_mix_g=ln_mix_g, ln_mix_b=ln_mix_b, ln_ffn_g=ln_ffn_g,
               ln_ffn_b=ln_ffn_b, ffn_w_up=ffn_w_up, ffn_w_down=ffn_w_down)
    mom = dict(attn_w_in=m_attn_w_in, attn_w_out=m_attn_w_out, hgrn_w_in=m_hgrn_w_in, hgrn_w_out=m_hgrn_w_out,
               hgrn_norm_g=m_hgrn_norm_g, lb_logits=m_lb_logits, ln_mix_g=m_ln_mix_g, ln_mix_b=m_ln_mix_b,
               ln_ffn_g=m_ln_ffn_g, ln_ffn_b=m_ln_ffn_b, ffn_w_up=m_ffn_w_up, ffn_w_down=m_ffn_w_down)
    vel = dict(attn_w_in=v_attn_w_in, attn_w_out=v_attn_w_out, hgrn_w_in=v_hgrn_w_in, hgrn_w_out=v_hgrn_w_out,
               hgrn_norm_g=v_hgrn_norm_g, lb_logits=v_lb_logits, ln_mix_g=v_ln_mix_g, ln_mix_b=v_ln_mix_b,
               ln_ffn_g=v_ln_ffn_g, ln_ffn_b=v_ln_ffn_b, ffn_w_up=v_ffn_w_up, ffn_w_down=v_ffn_w_down)
    me = 4 * lax.axis_index("x") + 2 * lax.axis_index("y") + lax.axis_index("c")

    src = {"attn_w_in": attn_w_in[0], "attn_w_out": attn_w_out[0], "hgrn_w_in": hgrn_w_in[0], "hgrn_w_out": hgrn_w_out[0],
           "ffn_w_up0": ffn_w_up[0], "ffn_w_down0": ffn_w_down[0], "ffn_w_up1": ffn_w_up[1], "ffn_w_down1": ffn_w_down[1]}
    gathers, got = {}, {}

    def start_gather(stage, deps):
        shards, lands = [], []
        for nm in GATHER_STAGES[stage]:
            sh = hgrn_norm_g if nm == "hgrn_norm_g" else _cast_bf16(src[nm], name=f"cast_{nm}")
            ax = GATHER_AXIS[nm]
            shape = list(sh.shape)
            shape[ax] *= N_DEV
            shards.append(sh)
            lands.append(lax.empty(tuple(shape), sh.dtype))
        peers = (SIBLING,) + SAME_CORE if stage == 0 else tuple(range(1, N_DEV))
        gathers[stage] = _xchg_start(shards, lands, [GATHER_AXIS[nm] for nm in GATHER_STAGES[stage]], gather=True,
                                     name=f"gather_start_{stage}", deps=deps, peers=peers)
        return [gathers[stage]["token"]]

    def get_w(name, after):
        deps = []
        if name not in got:
            group, stage, then = [w for w in GATHER_WAITS if name in w[0]][0]
            xc = gathers[stage]
            if stage == 0:
                relay = _gather_relay(xc, xc["token"], name="gather_relay")
                res = [_gather_relay_wait(xc, relay, xc["token"], name=f"gather_wait_{group[0]}")]
            else:
                res = _xchg_wait(xc, [GATHER_STAGES[stage].index(nm) for nm in group], after,
                                 name=f"gather_wait_{group[0]}")
            got.update(zip(group, res))
            if then is not None:
                deps = start_gather(then, [res[0]])
        return got[name], deps

    first = start_gather(0, [])

    scattered = {}

    def on_grads(tag, grads):
        gnames = list(grads)
        axes = [SCATTER_AXIS[nm] for nm in gnames]
        stacks = []
        for nm, ax in zip(gnames, axes):
            shape = list(grads[nm].shape)
            if ax is not None:
                shape[ax] //= N_DEV
            stacks.append(lax.empty((N_DEV, *shape), grads[nm].dtype))
        scattered[tag] = (gnames, _xchg_start([grads[nm] for nm in gnames], stacks, axes, gather=False,
                                              name=f"scatter_start_{tag}"))
        return [scattered[tag][1]["token"]]

    sm = jax.nn.softmax(lb_logits, axis=0)
    csum = jnp.cumsum(sm, axis=0)
    small = dict(lb=(csum - csum[0:1])[1:2], lb_logits=lb_logits, ln_mix_g=ln_mix_g, ln_mix_b=ln_mix_b,
                 ln_ffn_g=ln_ffn_g, ln_ffn_b=ln_ffn_b)
    grad_x = _local_step(x[0], loss_target[0], get_w, small, on_grads, deps=first)
    out = {}

    def stack_small(src_):
        rows = [None] * SMALL_ROWS
        for name in SMALL:
            rows[SMALL_ROW[name]], rows[SMALL_ROW[name] + 1] = src_[name][0:1], src_[name][1:2]
        zero = jnp.zeros((1, x.shape[-1]), F32)
        return jnp.concatenate([zero if r is None else r for r in rows], axis=0)[None]

    def update(name, slabs):
        shape = wts[name].shape
        out[name] = [r.reshape(shape) for r in _adamw(slabs, wts[name], mom[name], vel[name], name=f"adamw_{name}")]
        return out[name][0]

    slabs, after = {}, grad_x
    for tag, (gnames, xc) in scattered.items():
        slabs.update(zip(gnames, _xchg_wait(xc, list(range(len(gnames))), after, name=f"scatter_wait_{tag}")))
        if tag == "ffn1":
            continue
        if tag == "ffn0":
            update("ffn_w_down", [slabs["ffn_w_down0"], slabs["ffn_w_down1"]])
            after = update("ffn_w_up", [slabs["ffn_w_up0"], slabs["ffn_w_up1"]])
        elif tag == "hgrn":
            update("hgrn_w_out", [slabs["hgrn_w_out"]])
            after = update("hgrn_w_in", [slabs["hgrn_w_in"]])
        elif tag == "attn_out":
            after = update("attn_w_out", [slabs["attn_w_out"]])
        else:
            after = update("attn_w_in", [slabs["attn_w_in"]])
    res = _adamw([slabs["small"]], stack_small(wts), stack_small(mom), stack_small(vel), name="adamw_small")
    for name in SMALL:
        out[name] = [r[0, SMALL_ROW[name]:SMALL_ROW[name] + 2] for r in res]
    loss = res[0][0, LOSS_ROW, 0]
    ng = hgrn_norm_g.shape[-1]
    ng_slabs = lax.dynamic_slice(slabs["small"], (0, NORM_G_ROW, me * ng), (N_DEV, 1, ng))
    out["hgrn_norm_g"] = [r[0] for r in _adamw([ng_slabs], hgrn_norm_g[None], m_hgrn_norm_g[None],
                                                v_hgrn_norm_g[None], name="adamw_norm_g")]
    order =("attn_w_in", "attn_w_out", "hgrn_w_in", "hgrn_w_out", "hgrn_norm_g", "lb_logits", "ln_mix_g", "ln_mix_b",
             "ln_ffn_g", "ln_ffn_b", "ffn_w_up", "ffn_w_down")
    return (loss, grad_x[None], *[out[nm][0] for nm in order], *[out[nm][1] for nm in order],
            *[out[nm][2] for nm in order], *[out[nm][3] for nm in order])
```

```python
import jax
import jax.numpy as jnp
from jax import lax
from jax.experimental import pallas as pl
from jax.experimental.pallas import tpu as pltpu

F32 = jnp.float32
BF16 = jnp.bfloat16

N_DEV = 8
LANES = 128
D_MODEL = 1024
ATTN_HEAD_DIM = 64
ATTN_HEADS = 16
ATTN_SCALE = ATTN_HEAD_DIM ** -0.5
ATTN_BLK = 128
DILATIONS = (1, 4, 16)
ROPE_THETA = 10000.0
HGRN_HEADS = 8
HGRN_CHUNK = 64
D_FF = 4096
LN_EPS = 1e-5
RMS_EPS = 1e-6
DEPTH = 2
ALPHA = (2 * DEPTH) ** 0.25
ADAM_LR, ADAM_B1, ADAM_B2, ADAM_EPS, ADAM_WD, ADAM_STEP = 0.001, 0.9, 0.999, 1e-08, 0.01, 10
VMEM_LIMIT = 48 * 1024 * 1024

_NT = (((1,), (1,)), ((), ()))
_TN = (((0,), (0,)), ((), ()))


def _dot(a, b):
    return jnp.dot(a, b, preferred_element_type=F32)


def _dot_nt(a, b):
    return lax.dot_general(a, b, _NT, preferred_element_type=F32)


def _dot_tn(a, b):
    return lax.dot_general(a, b, _TN, preferred_element_type=F32)


def _split3(x):
    p1 = x.astype(BF16)
    r1 = x - p1.astype(F32)
    p2 = r1.astype(BF16)
    p3 = (r1 - p2.astype(F32)).astype(BF16)
    return p1, p2, p3


def _exact_dot(sel3, x):
    return _dot(sel3, jnp.concatenate(_split3(x), axis=0))


def _exact_dot_r(x, sel3):
    return _dot(jnp.concatenate(_split3(x), axis=1), sel3)


def _params(sem=None):
    return pltpu.CompilerParams(dimension_semantics=sem, vmem_limit_bytes=VMEM_LIMIT)


def _mm(a, b, mode, *, name, m, n, k, tm=1024, tn=1024, tk=1024, out_dtypes=(F32,), epi=None, a_pre=None,
        tile_extras=(), row_extras=(), vec_extras=(), row_outs=(), vec_outs=0, a_split=1, b_split=1, out_split=1,
        b_col_off=0, b_k_off=0, out_col_off=0, out_cols=None, alias=None, deps=(), row_slabs=1):
    tm, tn, tk = min(tm, m), min(tn, n), min(tk, k)
    assert m % tm == 0 and n % tn == 0 and k % tk == 0, (name, m, n, k, tm, tn, tk)
    gm, gn, gk = m // tm, n // tn, k // tk
    if mode in ("nn", "nt"):
        if a_split > 1 and tk == k:
            a_spec = pl.BlockSpec((a_split, tm, k // a_split), lambda i, j, kk: (0, i, 0))
        elif a_split > 1:
            kc = (k // a_split) // tk
            a_spec = pl.BlockSpec((None, tm, tk), lambda i, j, kk: (kk // kc, i, kk % kc))
        else:
            a_spec = pl.BlockSpec((tm, tk), lambda i, j, kk: (i, kk))
    else:
        a_spec = pl.BlockSpec((tk, tm), lambda i, j, kk: (kk, i))
    if mode in ("nn", "tn"):
        if b_split > 1:
            nc = (n // b_split) // tn
            b_spec = pl.BlockSpec((None, tk, tn), lambda i, j, kk: (j // nc, kk, j % nc))
        else:
            b_spec = pl.BlockSpec((tk, tn), lambda i, j, kk: (kk + b_k_off, j + b_col_off))
    else:
        b_spec = pl.BlockSpec((tn, tk), lambda i, j, kk: (j + b_col_off, kk + b_k_off))
    if out_split > 1:
        nco = (n // out_split) // tn
        o_spec = pl.BlockSpec((None, tm, tn), lambda i, j, kk: (j // nco, i, j % nco))
        o_shape = (out_split, m, n // out_split)
    else:
        o_spec = pl.BlockSpec((tm, tn), lambda i, j, kk: (i, j + out_col_off))
        o_shape = (m, out_cols if out_cols is not None else n)
    n_ex = len(tile_extras) + len(row_extras) + len(vec_extras)
    n_out = len(out_dtypes)
    n_plain = n_out + len(row_outs)
    assert not vec_outs or gn == 1
    assert row_slabs == 1 or (gk == 1 and mode != "tn" and out_split == 1)
    if epi is None:
        def epi(acc):
            return (acc,)
    dot = {"nn": _dot, "nt": _dot_nt, "tn": _dot_tn}[mode]

    def body(*refs):
        a_ref, b_ref = refs[0], refs[1]
        ex = refs[2:2 + n_ex]
        outs = refs[2 + n_ex + (1 if alias is not None else 0) + len(deps):][:n_plain + vec_outs]
        ii = pl.program_id(0)
        n_rowwise = len(tile_extras) + len(row_extras)
        everything = slice(None)

        def product(rows=everything):
            if a_split > 1 and tk == k:
                av = jnp.concatenate([a_ref[p, rows, :] for p in range(a_split)], axis=1)
            else:
                av = a_ref[...] if rows is everything else a_ref[rows, :]
            if a_pre is not None:
                av = a_pre(av)
            return dot(av.astype(BF16), b_ref[...].astype(BF16))

        def finish(total, rows=everything):
            res = epi(total, *[e[rows, :] for e in ex[:n_rowwise]], *[e[...] for e in ex[n_rowwise:]])
            for o, r in zip(outs[:n_plain], res):
                o[rows, :] = r.astype(o.dtype)
            return res[n_plain:]

        def add_up(parts):
            for o, r in zip(outs[n_plain:], parts):
                @pl.when(ii == 0)
                def _(o=o, r=r):
                    o[...] = r

                @pl.when(ii > 0)
                def _(o=o, r=r):
                    o[...] += r

        if row_slabs > 1:
            slabs = [pl.ds(c * (tm // row_slabs), tm // row_slabs) for c in range(row_slabs)]
            pending, parts = None, []
            for rows in slabs:
                total = product(rows)
                if pending is not None:
                    parts.append(finish(*pending))
                pending = (total, rows)
            parts.append(finish(*pending))
            add_up([sum(p[v] for p in parts) for v in range(vec_outs)])
        elif gk == 1:
            add_up(finish(product()))
        else:
            acc = refs[-1]
            kk = pl.program_id(2)

            @pl.when(kk == 0)
            def _():
                acc[...] = product()

            @pl.when(kk > 0)
            def _():
                acc[...] += product()

            @pl.when(kk == gk - 1)
            def _():
                add_up(finish(acc[...]))

    in_specs = [a_spec, b_spec] + [o_spec] * len(tile_extras)
    in_specs += [pl.BlockSpec((tm, r.shape[1]), lambda i, j, kk: (i, 0)) for r in row_extras]
    in_specs += [pl.BlockSpec((1, tn), lambda i, j, kk: (0, j))] * len(vec_extras)
    args = [a, b] + list(tile_extras) + list(row_extras) + list(vec_extras)
    io_alias = {}
    if alias is not None:
        in_specs.append(pl.BlockSpec(memory_space=pl.ANY))
        args.append(alias)
        io_alias = {len(args) - 1: 0}
    in_specs += [pl.BlockSpec(memory_space=pl.ANY)] * len(deps)
    args += list(deps)
    out_specs = [o_spec] * n_out
    out_shape = [jax.ShapeDtypeStruct(o_shape, dt) for dt in out_dtypes]
    for dt, w in row_outs:
        out_specs.append(pl.BlockSpec((tm, w), lambda i, j, kk: (i, 0)))
        out_shape.append(jax.ShapeDtypeStruct((m, w), dt))
    out_specs += [pl.BlockSpec((1, tn), lambda i, j, kk: (0, j))] * vec_outs
    out_shape += [jax.ShapeDtypeStruct((1, n), F32)] * vec_outs
    out = pl.pallas_call(
        body, name=name, grid=(gm, gn, gk), in_specs=in_specs, out_specs=out_specs, out_shape=out_shape,
        scratch_shapes=[pltpu.VMEM((tm, tn), F32)] if gk > 1 else [],
        input_output_aliases=io_alias,
        compiler_params=_params(("arbitrary" if vec_outs else "parallel", "parallel", "arbitrary")),
    )(*args)
    return out[0] if len(out) == 1 else out


def _rope_tables(seq, dil):
    pos = jnp.arange(seq, dtype=jnp.int32).reshape(seq // dil, dil).T.reshape(seq)
    half = ATTN_HEAD_DIM // 2
    inv = ROPE_THETA ** (-jnp.arange(half, dtype=F32) * (2.0 / ATTN_HEAD_DIM))
    ang = pos.astype(F32)[:, None] * inv[None, :]
    cos, sin = jnp.cos(ang), jnp.sin(ang)
    reps = LANES // ATTN_HEAD_DIM
    return (jnp.tile(jnp.concatenate([cos, cos], axis=1), (1, reps)),
            jnp.tile(jnp.concatenate([-sin, sin], axis=1), (1, reps)))


def _rotate(x, c, ss, sign=1.0):
    w = x.shape[-1]
    half = ATTN_HEAD_DIM // 2
    lane = lax.broadcasted_iota(jnp.int32, x.shape, 1)
    first = (lane % ATTN_HEAD_DIM) < half
    partner = jnp.where(first, pltpu.roll(x, w - half, 1), pltpu.roll(x, half, 1))
    reps = w // LANES
    if reps > 1:
        c = jnp.concatenate([c] * reps, axis=1)
        ss = jnp.concatenate([ss] * reps, axis=1)
    return x * c + sign * (partner * ss)


def _rotate_t(xt, ct, sst):
    half = ATTN_HEAD_DIM // 2
    heads = xt.shape[0] // ATTN_HEAD_DIM
    parts = []
    for h in range(heads):
        lo = h * ATTN_HEAD_DIM
        parts += [xt[lo + half:lo + 2 * half], xt[lo:lo + half]]
    return (xt * jnp.concatenate([ct] * heads, axis=0)
            + jnp.concatenate(parts, axis=0) * jnp.concatenate([sst] * heads, axis=0))


ATTN_IN_TOKENS = 1024
ATTN_IN_SLAB = 128


def _attn_in(xs_t, w, tables_t, group, *, name, deps=()):
    dil, d, n = xs_t.shape
    s = dil * n
    tt = min(ATTN_IN_TOKENS, s)
    per_tile = max(1, tt // n)
    nd = len(deps)

    def body(w_ref, x_ref, ct_ref, sst_ref, *refs):
        q_ref, qt_ref, wt = refs[nd:]
        j, i = pl.program_id(0), pl.program_id(1)

        @pl.when(i == 0)
        def _():
            wt[...] = w_ref[...].T

        xt = x_ref[...] if per_tile == 1 else jnp.concatenate([x_ref[c] for c in range(per_tile)], axis=1)
        ct, sst = ct_ref[...], sst_ref[...]
        q_scale = jnp.where(j == 0, ATTN_SCALE, 1.0)

        def finish(rows, acc_t):
            rot_t = _rotate_t(acc_t, ct, sst) * q_scale
            out_t = jnp.where(j >= 2, acc_t, rot_t).astype(BF16)
            qt_ref[rows, :] = out_t
            q_ref[:, rows] = out_t.T

        slabs = [pl.ds(c * ATTN_IN_SLAB, ATTN_IN_SLAB) for c in range(d // ATTN_IN_SLAB)]
        pending = None
        for rows in slabs:
            acc_t = _dot(wt[rows, :], xt)
            if pending is not None:
                finish(*pending)
            pending = (rows, acc_t)
        finish(*pending)

    if per_tile == 1:
        tiles_per_class = n // tt
        x_spec = pl.BlockSpec((None, d, tt), lambda j, i: (i // tiles_per_class, 0, i % tiles_per_class))
    else:
        x_spec = pl.BlockSpec((per_tile, d, n), lambda j, i: (i, 0, 0))
    tab = pl.BlockSpec((ATTN_HEAD_DIM, tt), lambda j, i: (0, i))
    return pl.pallas_call(
        body, name=name, grid=(3, s // tt),
        in_specs=[pl.BlockSpec((d, d), lambda j, i: (0, 3 * group + j)), x_spec, tab, tab]
        + [pl.BlockSpec(memory_space=pl.ANY)] * nd,
        out_specs=[pl.BlockSpec((None, tt, d), lambda j, i: (j, i, 0)),
                   pl.BlockSpec((None, d, tt), lambda j, i: (j, 0, i))],
        out_shape=[jax.ShapeDtypeStruct((3, s, d), BF16), jax.ShapeDtypeStruct((3, d, s), BF16)],
        scratch_shapes=[pltpu.VMEM((d, d), BF16)],
        compiler_params=_params(("arbitrary", "arbitrary")),
    )(w, xs_t, *tables_t, *deps)


def _ln_epi(acc, x, g, b):
    u = ALPHA * x + acc
    mu = jnp.mean(u, axis=-1, keepdims=True)
    uc = u - mu
    var = jnp.mean(uc * uc, axis=-1, keepdims=True)
    rstd = lax.rsqrt(var + LN_EPS)
    xh = uc * rstd
    out = xh * g + b
    return out, out, xh, rstd


def _ln_grad(dy, xh, rstd, g):
    dxh = dy * g
    m1 = jnp.mean(dxh, axis=-1, keepdims=True)
    m2 = jnp.mean(dxh * xh, axis=-1, keepdims=True)
    du = rstd * (dxh - m1 - xh * m2)
    return du, du, jnp.sum(dy * xh, axis=0, keepdims=True), jnp.sum(dy, axis=0, keepdims=True)


def _ln_grad_epi(acc, du_next, xh, rstd, g):
    return _ln_grad(acc + ALPHA * du_next, xh, rstd, g)


def _ln_loss_epi(acc, x_in, target, g, b):
    out, _, xh, rstd = _ln_epi(acc, x_in, g, b)
    e = out - target
    du, _, dg, db = _ln_grad(e * (1.0 / e.shape[-1]), xh, rstd, g)
    return du, du, dg, db, jnp.sum(e * e, axis=0, keepdims=True)


POS_BLK = 2048


def _class_rows(r, dil):
    return pl.ds(r, POS_BLK // dil, stride=dil) if dil > 1 else pl.ds(0, POS_BLK)


def _class_view(a, dil):
    s, w = a.shape
    return a.reshape(dil, s // dil, w)


def _class_spec(dil, all_tiles=True):
    if all_tiles:
        return pl.BlockSpec((dil, POS_BLK // dil, LANES), lambda i, t: (0, i, t))
    return pl.BlockSpec((dil, POS_BLK // dil, LANES), lambda i, t: (0, i, 0))


def _pos_spec(all_tiles=True):
    if all_tiles:
        return pl.BlockSpec((POS_BLK, LANES), lambda i, t: (i, t))
    return pl.BlockSpec((POS_BLK, LANES), lambda i, t: (i, 0))


def _prep_x(x, deps=()):
    s, d = x.shape
    n_dil = len(DILATIONS)

    def body(x_ref, *refs):
        outs = refs[len(deps):]
        for g, dil in enumerate(DILATIONS):
            for r in range(dil):
                blk = x_ref[_class_rows(r, dil), :].astype(BF16)
                outs[g][r] = blk
                outs[n_dil + g][r] = blk.T

    outs = pl.pallas_call(
        body, name="prep_x", grid=(s // POS_BLK, d // LANES),
        in_specs=[_pos_spec()] + [pl.BlockSpec(memory_space=pl.ANY)] * len(deps),
        out_specs=[_class_spec(dil) for dil in DILATIONS]
        + [pl.BlockSpec((dil, LANES, POS_BLK // dil), lambda i, t: (0, t, i)) for dil in DILATIONS],
        out_shape=[jax.ShapeDtypeStruct((dil, s // dil, d), BF16) for dil in DILATIONS]
        + [jax.ShapeDtypeStruct((dil, d, s // dil), BF16) for dil in DILATIONS],
        compiler_params=_params(("parallel", "parallel")),
    )(x, *deps)
    return [o.reshape(s, d) for o in outs[:n_dil]], list(outs[n_dil:])


def _head_expand_matrix():
    h = lax.broadcasted_iota(jnp.int32, (LANES, D_MODEL), 0)
    l = lax.broadcasted_iota(jnp.int32, (LANES, D_MODEL), 1)
    return (l // ATTN_HEAD_DIM == h).astype(BF16)


def _attn_fwd(qkv, dil, *, name):
    _, s, d = qkv.shape
    nq = s // ATTN_BLK
    per = nq // dil
    tiles = d // LANES

    def body(q_ref, kc_ref, kp_ref, vc_ref, vp_ref, o_ref, lse_ref):
        qb = pl.program_id(0)
        first = (qb % per) == 0
        qi = lax.broadcasted_iota(jnp.int32, (ATTN_BLK, 2 * ATTN_BLK), 0)
        kj = lax.broadcasted_iota(jnp.int32, (ATTN_BLK, 2 * ATTN_BLK), 1)
        dist = qi + ATTN_BLK - kj
        valid = (dist >= 0) & (dist <= ATTN_BLK) & ((kj >= ATTN_BLK) | jnp.logical_not(first))
        lane = lax.broadcasted_iota(jnp.int32, (ATTN_BLK, LANES), 1)
        lse_tile = jnp.zeros((ATTN_BLK, LANES), F32)
        zero = jnp.zeros((), BF16)
        in_head = [(lane // ATTN_HEAD_DIM) == hh for hh in range(2)]
        for t0 in range(0, tiles, ATTN_FWD_TILE_GROUP):
            group = range(t0, t0 + ATTN_FWD_TILE_GROUP)
            heads = [(t, hh) for t in group for hh in range(2)]
            cols = {t: pl.ds(t * LANES, LANES) for t in group}
            k2 = {t: jnp.concatenate([kp_ref[:, cols[t]], kc_ref[:, cols[t]]], axis=0) for t in group}
            v2 = {t: jnp.concatenate([vp_ref[:, cols[t]], vc_ref[:, cols[t]]], axis=0) for t in group}
            sc = {(t, hh): jnp.where(valid, _dot_nt(jnp.where(in_head[hh], q_ref[:, cols[t]], zero), k2[t]),
                                     -jnp.inf) for t, hh in heads}
            mx = {i: jnp.max(sc[i], axis=-1, keepdims=True) for i in heads}
            p = {i: jnp.exp(sc[i] - mx[i]) for i in heads}
            l = {i: jnp.sum(p[i], axis=-1, keepdims=True) for i in heads}
            oh = {i: _dot(p[i].astype(BF16), v2[i[0]]) / l[i] for i in heads}
            for t in group:
                o_ref[:, cols[t]] = jnp.where(in_head[0], oh[t, 0], oh[t, 1])
                for hh in range(2):
                    lse_tile = jnp.where(lane == 2 * t + hh, mx[t, hh] + jnp.log(l[t, hh]), lse_tile)
        lse_ref[...] = lse_tile

    def blk(piece, prev):
        if prev:
            return pl.BlockSpec((None, ATTN_BLK, d), lambda i: (piece, jnp.maximum(i - 1, 0), 0))
        return pl.BlockSpec((None, ATTN_BLK, d), lambda i: (piece, i, 0))

    return pl.pallas_call(
        body, name=name, grid=(nq,),
        in_specs=[blk(0, False), blk(1, False), blk(1, True), blk(2, False), blk(2, True)],
        out_specs=[pl.BlockSpec((ATTN_BLK, d), lambda i: (i, 0)), pl.BlockSpec((ATTN_BLK, LANES), lambda i: (i, 0))],
        out_shape=[jax.ShapeDtypeStruct((s, d), F32), jax.ShapeDtypeStruct((s, LANES), F32)],
        compiler_params=_params(("parallel",)),
    )(qkv, qkv, qkv, qkv, qkv)


def _attn_combine(os_, lses):
    s, d = os_[0].shape
    sel = jnp.concatenate([_head_expand_matrix()] * 3, axis=0)

    def body(o0, o1, o2, l0, l1, l2, sel_ref, of_ref, ob_ref, lt_ref, o_pos, l_pos):
        for g, (dil, o_ref, l_ref) in enumerate(zip(DILATIONS, (o0, o1, o2), (l0, l1, l2))):
            for r in range(dil):
                o_pos[g, _class_rows(r, dil), :] = o_ref[r]
                l_pos[g, _class_rows(r, dil), :] = l_ref[r]
        la, lb_, lc = l_pos[0], l_pos[1], l_pos[2]
        mx = jnp.maximum(jnp.maximum(la, lb_), lc)
        es = (jnp.exp(la - mx), jnp.exp(lb_ - mx), jnp.exp(lc - mx))
        z = es[0] + es[1] + es[2]
        lt_ref[...] = mx + jnp.log(z)
        acc = jnp.zeros((POS_BLK, LANES), F32)
        for g in range(3):
            acc += _exact_dot_r(es[g] / z, sel_ref[...]) * o_pos[g]
        of_ref[...] = acc
        ob_ref[...] = acc.astype(BF16)

    return pl.pallas_call(
        body, name="attn_combine", grid=(s // POS_BLK, d // LANES),
        in_specs=[_class_spec(dil) for dil in DILATIONS] + [_class_spec(dil, False) for dil in DILATIONS]
        + [pl.BlockSpec((3 * LANES, LANES), lambda i, t: (0, t))],
        out_specs=[_pos_spec(), _pos_spec(), _pos_spec(False)],
        out_shape=[jax.ShapeDtypeStruct((s, d), F32), jax.ShapeDtypeStruct((s, d), BF16),
                   jax.ShapeDtypeStruct((s, LANES), F32)],
        scratch_shapes=[pltpu.VMEM((3, POS_BLK, LANES), F32), pltpu.VMEM((3, POS_BLK, LANES), F32)],
        compiler_params=_params(("parallel", "arbitrary")),
    )(*[_class_view(o, dil) for o, dil in zip(os_, DILATIONS)],
      *[_class_view(l, dil) for l, dil in zip(lses, DILATIONS)], sel)


ATTN_TILE_GROUP = 2
ATTN_FWD_TILE_GROUP = 4
AUX_PER_TILE = 12


def _aux_placement():
    h = lax.broadcasted_iota(jnp.int32, (6, LANES, LANES), 1)
    l = lax.broadcasted_iota(jnp.int32, (6, LANES, LANES), 2)
    j = lax.broadcasted_iota(jnp.int32, (6, LANES, LANES), 0)
    target = AUX_PER_TILE * (h // 2) + 3 * (h % 2) + jnp.where(j < 3, j, 3 + j)
    return ((l == target) & (h < ATTN_HEADS)).astype(BF16)


def _attn_bwd_prep(do, o, lse):
    s, d = do.shape
    tiles = d // LANES
    sel_t = jnp.tile(_head_expand_matrix().T.reshape(tiles, LANES, LANES), (1, 3, 1))

    def body(do_ref, o_ref, l_ref, sel_ref, place_ref, *refs):
        outs, delta, aux = refs[:9], refs[9], refs[10]
        t = pl.program_id(1)
        part = _exact_dot_r(do_ref[...] * o_ref[...], sel_ref[...])

        @pl.when(t == 0)
        def _():
            delta[...] = part

        @pl.when(t > 0)
        def _():
            delta[...] += part

        for g, dil in enumerate(DILATIONS):
            for r in range(dil):
                blk = do_ref[_class_rows(r, dil), :].astype(BF16)
                outs[g][r] = blk
                outs[3 + g][r] = blk.T

        @pl.when(t == tiles - 1)
        def _():
            pieces = _split3(l_ref[...]) + _split3(delta[...])
            aux[...] = _dot(jnp.concatenate(pieces, axis=1), place_ref[...])
            for g, dil in enumerate(DILATIONS):
                for r in range(dil):
                    outs[6 + g][r] = aux[_class_rows(r, dil), :].astype(BF16)

    outs = pl.pallas_call(
        body, name="attn_bwd_prep", grid=(s // POS_BLK, tiles),
        in_specs=[_pos_spec(), _pos_spec(), _pos_spec(False),
                  pl.BlockSpec((None, 3 * LANES, LANES), lambda i, t: (t, 0, 0)),
                  pl.BlockSpec((6 * LANES, LANES), lambda i, t: (0, 0))],
        out_specs=[_class_spec(dil) for dil in DILATIONS]
        + [pl.BlockSpec((dil, LANES, POS_BLK // dil), lambda i, t: (0, t, i)) for dil in DILATIONS]
        + [_class_spec(dil, False) for dil in DILATIONS],
        out_shape=[jax.ShapeDtypeStruct((dil, s // dil, d), BF16) for dil in DILATIONS]
        + [jax.ShapeDtypeStruct((dil, d, s // dil), BF16) for dil in DILATIONS]
        + [jax.ShapeDtypeStruct((dil, s // dil, LANES), BF16) for dil in DILATIONS],
        scratch_shapes=[pltpu.VMEM((POS_BLK, LANES), F32), pltpu.VMEM((POS_BLK, LANES), F32)],
        compiler_params=_params(("parallel", "arbitrary")),
    )(do, o, lse, sel_t, _aux_placement().reshape(6 * LANES, LANES))
    return ([a.reshape(s, d) for a in outs[0:3]], list(outs[3:6]), [a.reshape(s, LANES) for a in outs[6:9]])


def _attn_bwd(qkv, qkv_t, do, do_t, aux, tables, dil, *, name):
    _, s, d = qkv.shape
    nq = s // ATTN_BLK
    per = nq // dil
    tiles = d // LANES
    half = ATTN_HEAD_DIM

    def body(qd_ref, qo_ref, k_ref, qtd_ref, qto_ref, kt_ref, vt_ref, dod_ref, doo_ref, dotd_ref, doto_ref,
             auxd_ref, auxo_ref, c_ref, ss_ref, out_ref, carry):
        kb = pl.program_id(0)

        @pl.when(kb == 0)
        def _():
            carry[...] = jnp.zeros_like(carry)

        has_next = (kb % per) != (per - 1)
        qi = lax.broadcasted_iota(jnp.int32, (ATTN_BLK, 2 * ATTN_BLK), 0)
        kj = lax.broadcasted_iota(jnp.int32, (ATTN_BLK, 2 * ATTN_BLK), 1) % ATTN_BLK
        valid_d = kj <= qi
        valid_o = (kj >= qi) & has_next
        lane = lax.broadcasted_iota(jnp.int32, (ATTN_BLK, LANES), 1)
        row = lax.broadcasted_iota(jnp.int32, (LANES, ATTN_BLK), 0)
        side = lax.broadcasted_iota(jnp.int32, (LANES, 2 * ATTN_BLK), 0)
        first = lax.broadcasted_iota(jnp.int32, (LANES, 2 * ATTN_BLK), 1) < ATTN_BLK
        c, ss = c_ref[...], ss_ref[...]
        zero = jnp.zeros((), BF16)
        sides = ((qd_ref, qtd_ref, dod_ref, dotd_ref, auxd_ref[...], valid_d),
                 (qo_ref, qto_ref, doo_ref, doto_ref, auxo_ref[...], valid_o))
        both = (0, 1)

        def head_halves(x, index):
            axis = 0 if index is lane else 1
            return jnp.concatenate([jnp.where(index < half, x, zero), jnp.where(index >= half, x, zero)], axis=axis)

        for t0 in range(0, tiles, ATTN_TILE_GROUP):
            group = range(t0, t0 + ATTN_TILE_GROUP)
            cols = {t: pl.ds(t * LANES, LANES) for t in group}
            kk, kk_t, vv_t = {}, {}, {}
            for t in group:
                base = AUX_PER_TILE * t
                hit = lambda lo: ((first & (side >= base + lo) & (side < base + lo + 3))
                                  | (jnp.logical_not(first) & (side >= base + lo + 3) & (side < base + lo + 6)))
                kk[t] = head_halves(k_ref[:, cols[t]], lane)
                kk_t[t] = jnp.concatenate([head_halves(kt_ref[cols[t], :], row),
                                           jnp.where(hit(0), -1.0, 0.0).astype(BF16)], axis=0)
                vv_t[t] = jnp.concatenate([head_halves(vt_ref[cols[t], :], row),
                                           jnp.where(hit(6), -1.0, 0.0).astype(BF16)], axis=0)
            sc = {(t, w): _dot(jnp.concatenate([sides[w][0][:, cols[t]], sides[w][4]], axis=1), kk_t[t])
                  for t in group for w in both}
            dpd = {(t, w): _dot(jnp.concatenate([sides[w][2][:, cols[t]], sides[w][4]], axis=1), vv_t[t])
                   for t in group for w in both}
            p = {i: jnp.where(sides[i[1]][5], jnp.exp(sc[i]), 0.0) for i in sc}
            ds = {i: (p[i] * dpd[i]).astype(BF16) for i in sc}
            pb = {i: p[i].astype(BF16) for i in sc}
            dv_t = {t: sum(_dot(sides[w][3][cols[t], :], pb[t, w]) for w in both) for t in group}
            dk_t = {t: sum(_dot(sides[w][1][cols[t], :], ds[t, w]) for w in both) for t in group}
            dq = {i: _dot(ds[i], kk[i[0]]) * ATTN_SCALE for i in sc}
            for t in group:
                dq_now = carry[:, cols[t]] + dq[t, 0]
                carry[:, cols[t]] = dq[t, 1]
                dk = jnp.where(row < half, dk_t[t][:, :ATTN_BLK], dk_t[t][:, ATTN_BLK:]).T
                dv = jnp.where(row < half, dv_t[t][:, :ATTN_BLK], dv_t[t][:, ATTN_BLK:]).T
                out_ref[0, :, cols[t]] = _rotate(dq_now, c, ss, -1.0).astype(BF16)
                out_ref[1, :, cols[t]] = _rotate(dk, c, ss, -1.0).astype(BF16)
                out_ref[2, :, cols[t]] = dv.astype(BF16)

    def nxt(i):
        return jnp.minimum(i + 1, nq - 1)

    def piece(p, shift):
        if shift:
            return pl.BlockSpec((None, ATTN_BLK, d), lambda i: (p, nxt(i), 0))
        return pl.BlockSpec((None, ATTN_BLK, d), lambda i: (p, i, 0))

    def piece_t(p, shift):
        if shift:
            return pl.BlockSpec((None, d, ATTN_BLK), lambda i: (p, 0, nxt(i)))
        return pl.BlockSpec((None, d, ATTN_BLK), lambda i: (p, 0, i))

    def rows(width, shift):
        if shift:
            return pl.BlockSpec((ATTN_BLK, width), lambda i: (nxt(i), 0))
        return pl.BlockSpec((ATTN_BLK, width), lambda i: (i, 0))

    def do_t_spec(shift):
        if shift:
            return pl.BlockSpec((None, d, ATTN_BLK), lambda i: (nxt(i) // per, 0, nxt(i) % per))
        return pl.BlockSpec((None, d, ATTN_BLK), lambda i: (i // per, 0, i % per))

    return pl.pallas_call(
        body, name=name, grid=(nq,),
        in_specs=[piece(0, False), piece(0, True), piece(1, False),
                  piece_t(0, False), piece_t(0, True), piece_t(1, False), piece_t(2, False),
                  rows(d, False), rows(d, True), do_t_spec(False), do_t_spec(True),
                  rows(LANES, False), rows(LANES, True), rows(LANES, False), rows(LANES, False)],
        out_specs=pl.BlockSpec((3, ATTN_BLK, d), lambda i: (0, i, 0)),
        out_shape=jax.ShapeDtypeStruct((3, s, d), BF16),
        scratch_shapes=[pltpu.VMEM((ATTN_BLK, d), F32)],
        compiler_params=_params(("arbitrary",)),
    )(qkv, qkv, qkv, qkv_t, qkv_t, qkv_t, qkv_t, do, do, do_t, do_t, aux, aux, *tables)


def _dx_combine(du, parts):
    s, d = du.shape

    def body(du_ref, p0, p1, p2, out_ref):
        out_ref[...] = ALPHA * du_ref[...] + p0[0]
        for dil, p_ref in zip(DILATIONS[1:], (p1, p2)):
            for r in range(dil):
                out_ref[_class_rows(r, dil), :] += p_ref[r]

    return pl.pallas_call(
        body, name="dx_combine", grid=(s // POS_BLK, d // LANES),
        in_specs=[_pos_spec()] + [_class_spec(dil) for dil in DILATIONS], out_specs=_pos_spec(),
        out_shape=jax.ShapeDtypeStruct((s, d), F32),
        compiler_params=_params(("parallel", "parallel")),
    )(du, *[_class_view(p, dil) for p, dil in zip(parts, DILATIONS)])


HGRN_ROWS = 2048


def _tri(lower, copies=1):
    i = lax.broadcasted_iota(jnp.int32, (HGRN_CHUNK, HGRN_CHUNK * copies), 0)
    j = lax.broadcasted_iota(jnp.int32, (HGRN_CHUNK, HGRN_CHUNK * copies), 1) % HGRN_CHUNK
    return (j <= i) if lower else (j >= i)


def _hgrn_gates(qr, z, lb):
    sq = jax.nn.sigmoid(qr)
    e = jnp.exp(-jnp.abs(z))
    big = 1.0 / (1.0 + e)
    small = e * big
    sg = jnp.where(z >= 0, big, small)
    sn = jnp.where(z >= 0, small, big)
    return sq, qr * sq, sg, sn, lb + (1.0 - lb) * sg, (1.0 - lb) * sn


def _hgrn_fwd(p2, lb, norm_g):
    _, s, d = p2.shape
    nblk = s // HGRN_ROWS
    cps = HGRN_ROWS // HGRN_CHUNK

    def body(p_ref, lb_ref, g_ref, o_ref, on_ref, st_ref, state):
        @pl.when(pl.program_id(1) == 0)
        def _():
            state[...] = jnp.zeros_like(state)

        chunks = [pl.ds(c * HGRN_CHUNK, HGRN_CHUNK) for c in range(cps)]
        ltri = _tri(True)
        lsel = _tri(True, 3).astype(BF16)
        _, q, _, _, f, key = _hgrn_gates(p_ref[0], p_ref[1], lb_ref[...])
        lf = jnp.log(f)
        v = [p_ref[2, rows, :].astype(BF16) for rows in chunks]
        b = [_exact_dot(lsel, lf[c * HGRN_CHUNK:(c + 1) * HGRN_CHUNK]) for c in range(cps)]
        b_last = [bc[HGRN_CHUNK - 1:HGRN_CHUNK, :] for bc in b]
        q = [q[c * HGRN_CHUNK:(c + 1) * HGRN_CHUNK] for c in range(cps)]
        key = [key[c * HGRN_CHUNK:(c + 1) * HGRN_CHUNK] for c in range(cps)]
        qd = [(q[c] * jnp.exp(b[c])).astype(BF16) for c in range(cps)]
        kd = [(key[c] * jnp.exp(-b[c])).astype(BF16) for c in range(cps)]
        k2 = [(key[c] * jnp.exp(b_last[c] - b[c])).astype(BF16) for c in range(cps)]
        a = [jnp.where(ltri, _dot_nt(qd[c], kd[c]), 0.0).astype(BF16) for c in range(cps)]
        kv = [_dot_tn(v[c], k2[c]) for c in range(cps)]
        st, sts = state[...], []
        for c in range(cps):
            sts.append(st)
            st_ref[0, c] = st
            st = st * jnp.exp(b_last[c]) + kv[c]
        state[...] = st
        gv = g_ref[...]
        for c in range(cps):
            o = _dot(a[c], v[c]) + _dot_nt(qd[c], sts[c].astype(BF16))
            o_ref[chunks[c], :] = o
            r = lax.rsqrt(jnp.mean(o * o, axis=-1, keepdims=True) + RMS_EPS)
            on_ref[chunks[c], :] = (o * r * gv).astype(BF16)

    vec = pl.BlockSpec((1, LANES), lambda h, c: (0, h))
    col = pl.BlockSpec((HGRN_ROWS, LANES), lambda h, c: (c, h))
    return pl.pallas_call(
        body, name="hgrn_fwd", grid=(HGRN_HEADS, nblk),
        in_specs=[pl.BlockSpec((3, HGRN_ROWS, LANES), lambda h, c: (0, c, h)), vec, vec],
        out_specs=[col, col, pl.BlockSpec((1, cps, LANES, LANES), lambda h, c: (h, c, 0, 0))],
        out_shape=[jax.ShapeDtypeStruct((s, d), F32), jax.ShapeDtypeStruct((s, d), BF16),
                   jax.ShapeDtypeStruct((HGRN_HEADS, s // HGRN_CHUNK, LANES, LANES), F32)],
        scratch_shapes=[pltpu.VMEM((LANES, LANES), F32)],
        compiler_params=_params(("parallel", "arbitrary")),
    )(p2, lb, norm_g)


def _hgrn_bwd(p2, lb, norm_g, o_raw, states, dyn):
    _, s, d = p2.shape
    nblk = s // HGRN_ROWS
    cps = HGRN_ROWS // HGRN_CHUNK

    def body(p_ref, lb_ref, g_ref, o_ref, st_ref, dy_ref, dp_ref, dg_ref, dlb_ref, dstate):
        @pl.when(pl.program_id(1) == 0)
        def _():
            dstate[...] = jnp.zeros_like(dstate)
            dg_ref[...] = jnp.zeros_like(dg_ref)
            dlb_ref[...] = jnp.zeros_like(dlb_ref)

        n = cps
        cut = lambda t: [t[c * HGRN_CHUNK:(c + 1) * HGRN_CHUNK] for c in range(n)]
        chunks = [pl.ds(c * HGRN_CHUNK, HGRN_CHUNK) for c in range(n)]
        lbv = lb_ref[...]
        ltri = _tri(True)
        lsel = _tri(True, 3).astype(BF16)
        usel = _tri(False, 3).astype(BF16)
        last_row = lax.broadcasted_iota(jnp.int32, (HGRN_CHUNK, LANES), 0) == HGRN_CHUNK - 1
        qr, z = p_ref[0], p_ref[1]
        sq, q, sg, sn, f, key = _hgrn_gates(qr, z, lbv)
        lf = jnp.log(f)
        v = [p_ref[2, rows, :].astype(BF16) for rows in chunks]
        o, dyv = o_ref[...], dy_ref[...]
        r = lax.rsqrt(jnp.mean(o * o, axis=-1, keepdims=True) + RMS_EPS)
        oh = o * r
        doh = dyv * g_ref[...]
        do = cut((r * (doh - oh * jnp.mean(doh * oh, axis=-1, keepdims=True))).astype(BF16))
        dg_ref[...] += jnp.sum(dyv * oh, axis=0, keepdims=True)
        b = [_exact_dot(lsel, lfc) for lfc in cut(lf)]
        b_last = [bc[HGRN_CHUNK - 1:HGRN_CHUNK, :] for bc in b]
        q, key = cut(q), cut(key)
        eb = [jnp.exp(bc) for bc in b]
        enb = [jnp.exp(-bc) for bc in b]
        e2 = [jnp.exp(b_last[c] - b[c]) for c in range(n)]
        dec = [jnp.exp(bl) for bl in b_last]
        qd_f = [q[c] * eb[c] for c in range(n)]
        kd_f = [key[c] * enb[c] for c in range(n)]
        k2_f = [key[c] * e2[c] for c in range(n)]
        qd, kd, k2 = ([t.astype(BF16) for t in ts] for ts in (qd_f, kd_f, k2_f))
        a = [jnp.where(ltri, _dot_nt(qd[c], kd[c]), 0.0).astype(BF16) for c in range(n)]
        da = [jnp.where(ltri, _dot_nt(do[c], v[c]), 0.0).astype(BF16) for c in range(n)]
        grow = [_dot_tn(do[c], qd[c]) for c in range(n)]
        dst, dsts = dstate[...], [None] * n
        for c in reversed(range(n)):
            dsts[c] = dst
            dst = dst * dec[c] + grow[c]
        dstate[...] = dst
        st = [st_ref[0, c] for c in range(n)]
        dstb = [t.astype(BF16) for t in dsts]
        dv = [_dot_tn(a[c], do[c]) + _dot_nt(k2[c], dstb[c]) for c in range(n)]
        dqd = [_dot(da[c], kd[c]) + _dot(do[c], st[c].astype(BF16)) for c in range(n)]
        dkd = [_dot_tn(da[c], qd[c]) for c in range(n)]
        dk2 = [_dot(v[c], dstb[c]) for c in range(n)]
        db = []
        for c in range(n):
            ddec = jnp.sum(dsts[c] * st[c], axis=0, keepdims=True)
            db_last = jnp.sum(dk2[c] * k2_f[c], axis=0, keepdims=True) + ddec * dec[c]
            db.append(dqd[c] * qd_f[c] - dkd[c] * kd_f[c] - dk2[c] * k2_f[c] + jnp.where(last_row, db_last, 0.0))
        dlf = [_exact_dot(usel, dbc) for dbc in db]
        f, sg, sn, sq, qr = cut(f), cut(sg), cut(sn), cut(sq), cut(qr)
        dlb_acc = jnp.zeros((1, LANES), F32)
        for c in range(n):
            dkey = dkd[c] * enb[c] + dk2[c] * e2[c]
            common = dlf[c] / f[c] - dkey
            dlb_acc += jnp.sum(common * sn[c], axis=0, keepdims=True)
            dp_ref[0, chunks[c], :] = (dqd[c] * eb[c] * (sq[c] * (1.0 + qr[c] * (1.0 - sq[c])))).astype(BF16)
            dp_ref[1, chunks[c], :] = (common * ((1.0 - lbv) * sg[c] * sn[c])).astype(BF16)
            dp_ref[2, chunks[c], :] = dv[c].astype(BF16)
        dlb_ref[...] += dlb_acc

    def rev(c):
        return nblk - 1 - c

    vec = pl.BlockSpec((1, LANES), lambda h, c: (0, h))
    col = pl.BlockSpec((HGRN_ROWS, LANES), lambda h, c: (rev(c), h))
    p3 = pl.BlockSpec((3, HGRN_ROWS, LANES), lambda h, c: (0, rev(c), h))
    return pl.pallas_call(
        body, name="hgrn_bwd", grid=(HGRN_HEADS, nblk),
        in_specs=[p3, vec, vec, col, pl.BlockSpec((1, cps, LANES, LANES), lambda h, c: (h, rev(c), 0, 0)), col],
        out_specs=[p3, vec, vec],
        out_shape=[jax.ShapeDtypeStruct((3, s, d), BF16), jax.ShapeDtypeStruct((1, d), F32),
                   jax.ShapeDtypeStruct((1, d), F32)],
        scratch_shapes=[pltpu.VMEM((LANES, LANES), F32)],
        compiler_params=_params(("parallel", "arbitrary")),
    )(p2, lb, norm_g, o_raw, states, dyn)


SMALL_ROWS = 16
FUSED_TM = 512
FUSED_SLABS = 4


def _relu2(h):
    r = jnp.maximum(h.astype(F32), 0.0)
    return r * r


def _dact_epi(acc, h):
    return (acc * (2.0 * jnp.maximum(h.astype(F32), 0.0)),)


def _local_step(x, target, get_w, small, on_grads, deps=()):
    s, d = x.shape
    lb = small["lb"]
    gq = 3 * d
    gt = gq // 1024

    def ln_of(a, w, x_in, which, layer, *, name, k, a_pre=None, deps=()):
        g, b = small[f"ln_{which}_g"][layer:layer + 1], small[f"ln_{which}_b"][layer:layer + 1]
        return _mm(a, w, "nn", name=name, m=s, n=d, k=k, tm=FUSED_TM, tk=k, a_pre=a_pre, out_dtypes=(F32, BF16, F32),
                   row_outs=((F32, 1),), epi=_ln_epi, tile_extras=(x_in,), vec_extras=(g, b), deps=deps,
                   row_slabs=FUSED_SLABS)

    def ln_grad_of(a, w, du_next, xh, rstd, which, layer, *, name, k, a_split=1, deps=()):
        return _mm(a, w, "nt", name=name, m=s, n=d, k=k, tm=FUSED_TM, tk=k, a_split=a_split,
                   out_dtypes=(F32, BF16), vec_outs=2, epi=_ln_grad_epi, tile_extras=(du_next, xh), row_extras=(rstd,),
                   vec_extras=(small[f"ln_{which}_g"][layer:layer + 1],), deps=deps)

    def ffn_fwd(xb, x_in, w_up, w_down, tag):
        h = _mm(xb, w_up, "nn", name=f"ffn_up_{tag}", m=s, n=D_FF, k=d, tm=2048, out_dtypes=(BF16,))
        return (h,) + tuple(ln_of(h, w_down, x_in, "ffn", tag, name=f"ffn_down_{tag}", k=D_FF, a_pre=_relu2))

    def ffn_bwd(du, dub, xb, h, w_up, w_down, xh, rstd, tag):
        dh = _mm(dub, w_down, "nt", name=f"ffn_dact_{tag}", m=s, n=D_FF, k=d, tm=2048, out_dtypes=(BF16,), epi=_dact_epi,
                 tile_extras=(h,))
        g_down = _mm(h, dub, "tn", name=f"ffn_gdown_{tag}", m=D_FF, n=d, k=s, tm=512, tk=s, a_pre=_relu2,
                     out_dtypes=(BF16,))
        g_up = _mm(xb, dh, "tn", name=f"ffn_gup_{tag}", m=d, n=D_FF, k=s, tk=s, out_dtypes=(BF16,))
        after = on_grads(f"ffn{tag}", {f"ffn_w_down{tag}": g_down, f"ffn_w_up{tag}": g_up})
        return ln_grad_of(dh, w_up, du, xh, rstd, "mix", tag, name=f"ffn_dx_{tag}", k=D_FF, deps=after)

    xs, xs_t = _prep_x(x, deps)
    w_ain, after = get_w("attn_w_in", None)
    tabs, qkvs, qkv_ts, o_parts, lse_parts = [], [], [], [], []
    for g, dil in enumerate(DILATIONS):
        tabs.append(_rope_tables(s, dil))
        tabs_t = [t[:, :ATTN_HEAD_DIM].T for t in tabs[g]]
        qkv, qkv_t = _attn_in(xs_t[g], w_ain, tabs_t, g, name=f"attn_in_{g}", deps=after if g == 0 else ())
        qkvs.append(qkv)
        qkv_ts.append(qkv_t)
        o_g, lse_g = _attn_fwd(qkv, dil, name=f"attn_fwd_{g}")
        o_parts.append(o_g)
        lse_parts.append(lse_g)
    o_f, o_b, lse_t = _attn_combine(o_parts, lse_parts)
    w_aout, after = get_w("attn_w_out", o_b)
    x1, x1b, xh1, r1 = ln_of(o_b, w_aout, x, "mix", 0, name="attn_out", k=d, deps=after)
    w_up0, w_down0 = get_w("ffn_w_up0", o_b)[0], get_w("ffn_w_down0", o_b)[0]
    h0, x2, x2b, xh2, r2 = ffn_fwd(x1b, x1, w_up0, w_down0, 0)
    w_hin, w_hout, norm_g = get_w("hgrn_w_in", x2b)[0], get_w("hgrn_w_out", x2b)[0], get_w("hgrn_norm_g", x2b)[0]
    p2 = _mm(x2b, w_hin, "nn", name="hgrn_in", m=s, n=3 * d, k=d, tm=2048, out_split=3)
    o_raw, o_n, states = _hgrn_fwd(p2, lb, norm_g)
    w_up1, w_down1 = get_w("ffn_w_up1", o_n)[0], get_w("ffn_w_down1", o_n)[0]
    x3, x3b, xh3, r3 = ln_of(o_n, w_hout, x2, "mix", 1, name="hgrn_out", k=d)
    h1 = _mm(x3b, w_up1, "nn", name="ffn_up_1", m=s, n=D_FF, k=d, tm=2048, out_dtypes=(BF16,))
    du, dub, dg_ffn1, db_ffn1, sq = _mm(
        h1, w_down1, "nn", name="ffn_down_1", m=s, n=d, k=D_FF, tm=FUSED_TM, tk=D_FF, a_pre=_relu2,
        out_dtypes=(F32, BF16), vec_outs=3, epi=_ln_loss_epi, tile_extras=(x3, target),
        vec_extras=(small["ln_ffn_g"][1:2], small["ln_ffn_b"][1:2]), row_slabs=FUSED_SLABS)
    du, dub, dg_mix1, db_mix1 = ffn_bwd(du, dub, x3b, h1, w_up1, w_down1, xh3, r3, 1)
    dyn = _mm(dub, w_hout, "nt", name="hgrn_dout", m=s, n=d, k=d)
    g_hout = _mm(o_n, dub, "tn", name="hgrn_gout", m=d, n=d, k=s, out_dtypes=(BF16,))
    dp2, d_norm_g, d_lb = _hgrn_bwd(p2, lb, norm_g, o_raw, states, dyn)
    g_hin = _mm(x2b, dp2, "tn", name="hgrn_gin", m=d, n=3 * d, k=s, tk=s, b_split=3, out_dtypes=(BF16,))
    after = on_grads("hgrn", {"hgrn_w_out": g_hout, "hgrn_w_in": g_hin})
    du, dub, dg_ffn0, db_ffn0 = ln_grad_of(dp2, w_hin, du, xh2, r2, "ffn", 0, name="hgrn_dx", k=3 * d, a_split=3,
                                           deps=after)
    du, dub, dg_mix0, db_mix0 = ffn_bwd(du, dub, x1b, h0, w_up0, w_down0, xh1, r1, 0)
    g_aout = _mm(o_b, dub, "tn", name="attn_gout", m=d, n=d, k=s, out_dtypes=(BF16,))
    sm1 = jax.nn.softmax(small["lb_logits"], axis=0)
    d_l1 = d_lb * (sm1[0:1] * sm1[1:2])
    zeros = jnp.zeros((SMALL_ROWS - 12, d), F32)
    loss_row = jnp.broadcast_to(0.5 * jnp.sum(sq) / d, (1, d))
    small_grads = jnp.concatenate([dg_mix0, dg_mix1, db_mix0, db_mix1, dg_ffn0, dg_ffn1, db_ffn0, db_ffn1,
                                   -d_l1, d_l1, d_norm_g, loss_row, zeros], axis=0)
    after = on_grads("attn_out", {"attn_w_out": g_aout, "small": small_grads})
    do = _mm(dub, w_aout, "nt", name="attn_dout", m=s, n=d, k=d, deps=after)
    do_parts, do_ts, aux_parts = _attn_bwd_prep(do, o_f, lse_t)
    g_ain, dqkvs = None, []
    for g, dil in enumerate(DILATIONS):
        dqkvs.append(_attn_bwd(qkvs[g], qkv_ts[g], do_parts[g], do_ts[g], aux_parts[g], tabs[g], dil,
                               name=f"attn_bwd_{g}"))
        g_ain = _mm(xs[g], dqkvs[g], "tn", name=f"attn_gin_{g}", m=d, n=gq, k=s, tk=s, b_split=3, out_dtypes=(BF16,),
                    out_col_off=g * gt, out_cols=3 * gq, alias=g_ain)
    after = on_grads("attn_in", {"attn_w_in": g_ain})
    dx_parts = [_mm(dqkvs[g], w_ain, "nt", name=f"attn_dx_{g}", m=s, n=d, k=gq, tm=FUSED_TM, tk=gq, a_split=3, b_k_off=g,
                    deps=after if g == 0 else ())
                for g in range(len(DILATIONS))]
    return _dx_combine(du, dx_parts)


def _mesh_place():
    x, y, c = lax.axis_index("x"), lax.axis_index("y"), lax.axis_index("c")
    return x, y, c, 4 * x + 2 * y + c


def _peer(x, y, c, k):
    px = 1 - x if (k >> 2) & 1 else x
    py = 1 - y if (k >> 1) & 1 else y
    pc = 1 - c if k & 1 else c
    return (px, py, pc), 4 * px + 2 * py + pc


def _window(ref, axis, size, idx):
    if axis is None:
        return ref
    sl = [slice(None)] * len(ref.shape)
    sl[axis] = pl.ds(idx * size, size)
    return ref.at[tuple(sl)]


_HBM = pl.BlockSpec(memory_space=pltpu.HBM)
_SEM = pl.BlockSpec(memory_space=pltpu.SEMAPHORE)
_EFFECT = pltpu.SideEffectType.DATAFLOW_SIDE_EFFECTING


def _xchg_ends(src_ref, land_ref, axis, gather, me, other):
    if gather:
        size = src_ref.shape[axis]
        return src_ref, _window(land_ref, axis, size, me), _window(land_ref, axis, size, other)
    size = None if axis is None else src_ref.shape[axis] // N_DEV
    return _window(src_ref, axis, size, other), land_ref.at[me], land_ref.at[other]


def _xchg_start(srcs, lands, axes, *, gather, name, deps=(), peers=tuple(range(1, N_DEV))):
    n = len(srcs)
    nd = len(deps)

    def body(*refs):
        src_refs, land_refs = refs[:n], refs[n:2 * n]
        send, recv = refs[2 * n + nd:3 * n + nd], refs[3 * n + nd:4 * n + nd]
        token = refs[-1]
        x, y, c, me = _mesh_place()
        for k in peers:
            peer, pidx = _peer(x, y, c, k)
            for i in range(n):
                src, dst, _ = _xchg_ends(src_refs[i], land_refs[i], axes[i], gather, me, pidx)
                pltpu.make_async_remote_copy(
                    src_ref=src, dst_ref=dst, send_sem=send[i].at[k - 1], recv_sem=recv[i].at[k - 1],
                    device_id=peer, device_id_type=pl.DeviceIdType.MESH).start()
        for i in range(n):
            src, dst, _ = _xchg_ends(src_refs[i], land_refs[i], axes[i], gather, me, me)
            pltpu.make_async_copy(src, dst, send[i].at[N_DEV - 1]).start()
        token[...] = jnp.zeros_like(token)

    bufs = list(srcs) + list(lands)
    outs = pl.pallas_call(
        body, name=name,
        out_shape=[pltpu.SemaphoreType.DMA((N_DEV,))] * (2 * n) + [pltpu.HBM(b.shape, b.dtype) for b in bufs]
        + [jax.ShapeDtypeStruct((8, LANES), F32)],
        in_specs=[_HBM] * (2 * n) + [pl.BlockSpec(memory_space=pl.ANY)] * nd,
        out_specs=[_SEM] * (2 * n) + [_HBM] * (2 * n) + [pl.BlockSpec(memory_space=pltpu.VMEM)],
        input_output_aliases={i: 2 * n + i for i in range(2 * n)},
        compiler_params=pltpu.CompilerParams(has_side_effects=_EFFECT),
    )(*[pltpu.with_memory_space_constraint(b, pltpu.HBM) for b in bufs], *deps)
    return dict(send=outs[:n], recv=outs[n:2 * n], srcs=outs[2 * n:3 * n], lands=outs[3 * n:4 * n], token=outs[-1],
                axes=list(axes), gather=gather)


SIBLING = 1
SAME_CORE = (2, 4, 6)


def _gather_relay(xc, after, *, name):
    axis = xc["axes"][0]

    def body(src_ref, land_ref, send1, recv1, after_ref, src_out, land_out, send2, recv2):
        x, y, c, me = _mesh_place()
        sibling, _ = _peer(x, y, c, SIBLING)
        for j, k in enumerate(SAME_CORE):
            peer, pidx = _peer(x, y, c, k)
            src, _, got = _xchg_ends(src_ref, land_ref, axis, True, me, pidx)
            pltpu.make_async_remote_copy(
                src_ref=src, dst_ref=got, send_sem=send1.at[k - 1], recv_sem=recv1.at[k - 1],
                device_id=peer, device_id_type=pl.DeviceIdType.MESH).wait_recv()
            pltpu.make_async_remote_copy(
                src_ref=got, dst_ref=got, send_sem=send2.at[j], recv_sem=recv2.at[j],
                device_id=sibling, device_id_type=pl.DeviceIdType.MESH).start()

    src, land = xc["srcs"][0], xc["lands"][0]
    outs = pl.pallas_call(
        body, name=name,
        out_shape=[pltpu.HBM(src.shape, src.dtype), pltpu.HBM(land.shape, land.dtype)]
        + [pltpu.SemaphoreType.DMA((len(SAME_CORE),))] * 2,
        in_specs=[_HBM, _HBM, _SEM, _SEM, pl.BlockSpec(memory_space=pl.ANY)],
        out_specs=[_HBM, _HBM, _SEM, _SEM], input_output_aliases={0: 0, 1: 1},
        compiler_params=pltpu.CompilerParams(has_side_effects=_EFFECT),
    )(src, land, xc["send"][0], xc["recv"][0], after)
    return dict(src=outs[0], land=outs[1], send=outs[2], recv=outs[3])


def _gather_relay_wait(xc, relay, after, *, name):
    axis = xc["axes"][0]

    def body(src_ref, land_ref, send1, recv1, send2, recv2, after_ref, src_out, land_out):
        x, y, c, me = _mesh_place()
        sibling, sidx = _peer(x, y, c, SIBLING)
        for k in (SIBLING,) + SAME_CORE:
            peer, pidx = _peer(x, y, c, k)
            src, dst, got = _xchg_ends(src_ref, land_ref, axis, True, me, pidx)
            pltpu.make_async_remote_copy(
                src_ref=src, dst_ref=dst, send_sem=send1.at[k - 1], recv_sem=recv1.at[k - 1],
                device_id=peer, device_id_type=pl.DeviceIdType.MESH).wait_send()
        src, dst, got = _xchg_ends(src_ref, land_ref, axis, True, me, sidx)
        pltpu.make_async_remote_copy(
            src_ref=src, dst_ref=got, send_sem=send1.at[SIBLING - 1], recv_sem=recv1.at[SIBLING - 1],
            device_id=sibling, device_id_type=pl.DeviceIdType.MESH).wait_recv()
        src, dst, _ = _xchg_ends(src_ref, land_ref, axis, True, me, me)
        pltpu.make_async_copy(src, dst, send1.at[N_DEV - 1]).wait()
        for j, k in enumerate(SAME_CORE):
            _, pidx = _peer(x, y, c, k)
            _, qidx = _peer(x, y, c, k ^ SIBLING)
            _, _, sent = _xchg_ends(src_ref, land_ref, axis, True, me, pidx)
            _, _, got = _xchg_ends(src_ref, land_ref, axis, True, me, qidx)
            pltpu.make_async_remote_copy(
                src_ref=sent, dst_ref=sent, send_sem=send2.at[j], recv_sem=recv2.at[j],
                device_id=sibling, device_id_type=pl.DeviceIdType.MESH).wait_send()
            pltpu.make_async_remote_copy(
                src_ref=got, dst_ref=got, send_sem=send2.at[j], recv_sem=recv2.at[j],
                device_id=sibling, device_id_type=pl.DeviceIdType.MESH).wait_recv()

    outs = pl.pallas_call(
        body, name=name, out_shape=[pltpu.HBM(relay["src"].shape, relay["src"].dtype),
                                    pltpu.HBM(relay["land"].shape, relay["land"].dtype)],
        in_specs=[_HBM, _HBM, _SEM, _SEM, _SEM, _SEM, pl.BlockSpec(memory_space=pl.ANY)],
        out_specs=[_HBM, _HBM], input_output_aliases={0: 0, 1: 1},
        compiler_params=pltpu.CompilerParams(has_side_effects=_EFFECT),
    )(relay["src"], relay["land"], xc["send"][0], xc["recv"][0], relay["send"], relay["recv"], after)
    return outs[1]


def _xchg_wait(xc, items, after, *, name):
    m = len(items)
    gather = xc["gather"]
    axes = [xc["axes"][i] for i in items]

    def body(*refs):
        src_refs, land_refs = refs[:m], refs[m:2 * m]
        send, recv = refs[2 * m:3 * m], refs[3 * m:4 * m]
        x, y, c, me = _mesh_place()
        for k in range(1, N_DEV):
            peer, pidx = _peer(x, y, c, k)
            for j in range(m):
                src, dst, got = _xchg_ends(src_refs[j], land_refs[j], axes[j], gather, me, pidx)
                pltpu.make_async_remote_copy(
                    src_ref=src, dst_ref=dst, send_sem=send[j].at[k - 1], recv_sem=recv[j].at[k - 1],
                    device_id=peer, device_id_type=pl.DeviceIdType.MESH).wait_send()
                pltpu.make_async_remote_copy(
                    src_ref=src, dst_ref=got, send_sem=send[j].at[k - 1], recv_sem=recv[j].at[k - 1],
                    device_id=peer, device_id_type=pl.DeviceIdType.MESH).wait_recv()
        for j in range(m):
            src, dst, _ = _xchg_ends(src_refs[j], land_refs[j], axes[j], gather, me, me)
            pltpu.make_async_copy(src, dst, send[j].at[N_DEV - 1]).wait()

    bufs = [xc["srcs"][i] for i in items] + [xc["lands"][i] for i in items]
    sems = [xc["send"][i] for i in items] + [xc["recv"][i] for i in items]
    outs = pl.pallas_call(
        body, name=name, out_shape=[pltpu.HBM(b.shape, b.dtype) for b in bufs],
        in_specs=[_HBM] * (2 * m) + [_SEM] * (2 * m) + [pl.BlockSpec(memory_space=pl.ANY)],
        out_specs=[_HBM] * (2 * m), input_output_aliases={j: j for j in range(2 * m)},
        compiler_params=pltpu.CompilerParams(has_side_effects=_EFFECT),
    )(*bufs, *sems, after)
    return outs[m:]


def _cast_bf16(a, *, name):
    r, c = a.shape
    tr = min(r, 512)

    def body(a_ref, o_ref):
        o_ref[...] = a_ref[...].astype(BF16)

    spec = pl.BlockSpec((tr, c), lambda i: (i, 0))
    return pl.pallas_call(body, name=name, grid=(r // tr,), in_specs=[spec], out_specs=spec,
                          out_shape=jax.ShapeDtypeStruct((r, c), BF16), compiler_params=_params(("parallel",)))(a)


def _adamw(slabs, w, m, v, *, name):
    layers, r, c = w.shape
    tr = min(r, 256)

    def body(*refs):
        s_refs = refs[:layers]
        w_ref, m_ref, v_ref, g_ref, d_ref, mo_ref, vo_ref = refs[layers:]
        for l in range(layers):
            g = s_refs[l][0].astype(F32)
            for i in range(1, N_DEV):
                g = g + s_refs[l][i].astype(F32)
            m2 = ADAM_B1 * m_ref[l] + (1.0 - ADAM_B1) * g
            v2 = ADAM_B2 * v_ref[l] + (1.0 - ADAM_B2) * (g * g)
            m_hat = m2 / (1.0 - ADAM_B1 ** ADAM_STEP)
            v_hat = v2 / (1.0 - ADAM_B2 ** ADAM_STEP)
            g_ref[l] = g
            d_ref[l] = -ADAM_LR * (m_hat / (jnp.sqrt(v_hat) + ADAM_EPS) + ADAM_WD * w_ref[l])
            mo_ref[l] = m2
            vo_ref[l] = v2

    spec = pl.BlockSpec((layers, tr, c), lambda i: (0, i, 0))
    return pl.pallas_call(
        body, name=name, grid=(r // tr,),
        in_specs=[pl.BlockSpec((N_DEV, tr, c), lambda i: (0, i, 0))] * layers + [spec, spec, spec],
        out_specs=[spec] * 4, out_shape=[jax.ShapeDtypeStruct((layers, r, c), F32)] * 4,
        compiler_params=_params(("parallel",)),
    )(*slabs, w, m, v)


GATHER_AXIS = {"attn_w_in": 1, "attn_w_out": 0, "ffn_w_up0": 1, "ffn_w_down0": 0, "hgrn_w_in": 1, "hgrn_w_out": 0,
               "hgrn_norm_g": 1, "ffn_w_up1": 1, "ffn_w_down1": 0}
GATHER_STAGES = (("attn_w_in",), ("attn_w_out", "ffn_w_up0", "ffn_w_down0", "hgrn_w_in", "hgrn_w_out", "hgrn_norm_g"),
                 ("ffn_w_up1", "ffn_w_down1"))
GATHER_WAITS = ((("attn_w_in",), 0, 1), (("attn_w_out", "ffn_w_up0", "ffn_w_down0"), 1, 2),
                (("hgrn_w_in", "hgrn_w_out", "hgrn_norm_g"), 1, None), (("ffn_w_up1", "ffn_w_down1"), 2, None))
SCATTER_AXIS = dict(GATHER_AXIS, small=None)
BIG = ("attn_w_in", "attn_w_out", "hgrn_w_in", "hgrn_w_out", "ffn_w_up", "ffn_w_down")
SMALL = ("lb_logits", "ln_mix_g", "ln_mix_b", "ln_ffn_g", "ln_ffn_b")
SMALL_ROW = {"ln_mix_g": 0, "ln_mix_b": 2, "ln_ffn_g": 4, "ln_ffn_b": 6, "lb_logits": 8}
NORM_G_ROW = 10
LOSS_ROW = 11


def kernel(x, attn_w_in, attn_w_out, hgrn_w_in, hgrn_w_out, hgrn_norm_g, lb_logits, ln_mix_g, ln_mix_b, ln_ffn_g, ln_ffn_b, ffn_w_up, ffn_w_down, loss_target, m_attn_w_in, m_attn_w_out, m_hgrn_w_in, m_hgrn_w_out, m_hgrn_norm_g, m_lb_logits, m_ln_mix_g, m_ln_mix_b, m_ln_ffn_g, m_ln_ffn_b, m_ffn_w_up, m_ffn_w_down, v_attn_w_in, v_attn_w_out, v_hgrn_w_in, v_hgrn_w_out, v_hgrn_norm_g, v_lb_logits, v_ln_mix_g, v_ln_mix_b, v_ln_ffn_g, v_ln_ffn_b, v_ffn_w_up, v_ffn_w_down):
    wts = dict(attn_w_in=attn_w_in, attn_w_out=attn_w_out, hgrn_w_in=hgrn_w_in, hgrn_w_out=hgrn_w_out,
               hgrn_norm_g=hgrn_norm_g, lb_logits=lb_logits, ln_mix_g=ln_mix_g, ln_mix_b=ln_mix_b, ln_ffn_g=ln_ffn_g,
               ln_ffn_b=ln_ffn_b, ffn_w_up=ffn_w_up, ffn_w_down=ffn_w_down)
    mom = dict(attn_w_in=m_attn_w_in, attn_w_out=m_attn_w_out, hgrn_w_in=m_hgrn_w_in, hgrn_w_out=m_hgrn_w_out,
               hgrn_norm_g=m_hgrn_norm_g, lb_logits=m_lb_logits, ln_mix_g=m_ln_mix_g, ln_mix_b=m_ln_mix_b,
               ln_ffn_g=m_ln_ffn_g, ln_ffn_b=m_ln_ffn_b, ffn_w_up=m_ffn_w_up, ffn_w_down=m_ffn_w_down)
    vel = dict(attn_w_in=v_attn_w_in, attn_w_out=v_attn_w_out, hgrn_w_in=v_hgrn_w_in, hgrn_w_out=v_hgrn_w_out,
               hgrn_norm_g=v_hgrn_norm_g, lb_logits=v_lb_logits, ln_mix_g=v_ln_mix_g, ln_mix_b=v_ln_mix_b,
               ln_ffn_g=v_ln_ffn_g, ln_ffn_b=v_ln_ffn_b, ffn_w_up=v_ffn_w_up, ffn_w_down=v_ffn_w_down)
    me = 4 * lax.axis_index("x") + 2 * lax.axis_index("y") + lax.axis_index("c")

    src = {"attn_w_in": attn_w_in[0], "attn_w_out": attn_w_out[0], "hgrn_w_in": hgrn_w_in[0], "hgrn_w_out": hgrn_w_out[0],
           "ffn_w_up0": ffn_w_up[0], "ffn_w_down0": ffn_w_down[0], "ffn_w_up1": ffn_w_up[1], "ffn_w_down1": ffn_w_down[1]}
    gathers, got = {}, {}

    def start_gather(stage, deps):
        shards, lands = [], []
        for nm in GATHER_STAGES[stage]:
            sh = hgrn_norm_g if nm == "hgrn_norm_g" else _cast_bf16(src[nm], name=f"cast_{nm}")
            ax = GATHER_AXIS[nm]
            shape = list(sh.shape)
            shape[ax] *= N_DEV
            shards.append(sh)
            lands.append(lax.empty(tuple(shape), sh.dtype))
        peers = (SIBLING,) + SAME_CORE if stage == 0 else tuple(range(1, N_DEV))
        gathers[stage] = _xchg_start(shards, lands, [GATHER_AXIS[nm] for nm in GATHER_STAGES[stage]], gather=True,
                                     name=f"gather_start_{stage}", deps=deps, peers=peers)
        return [gathers[stage]["token"]]

    def get_w(name, after):
        deps = []
        if name not in got:
            group, stage, then = [w for w in GATHER_WAITS if name in w[0]][0]
            xc = gathers[stage]
            if stage == 0:
                relay = _gather_relay(xc, xc["token"], name="gather_relay")
                res = [_gather_relay_wait(xc, relay, xc["token"], name=f"gather_wait_{group[0]}")]
            else:
                res = _xchg_wait(xc, [GATHER_STAGES[stage].index(nm) for nm in group], after,
                                 name=f"gather_wait_{group[0]}")
            got.update(zip(group, res))
            if then is not None:
                deps = start_gather(then, [res[0]])
        return got[name], deps

    first = start_gather(0, [])

    scattered = {}

    def on_grads(tag, grads):
        gnames = list(grads)
        axes = [SCATTER_AXIS[nm] for nm in gnames]
        stacks = []
        for nm, ax in zip(gnames, axes):
            shape = list(grads[nm].shape)
            if ax is not None:
                shape[ax] //= N_DEV
            stacks.append(lax.empty((N_DEV, *shape), grads[nm].dtype))
        scattered[tag] = (gnames, _xchg_start([grads[nm] for nm in gnames], stacks, axes, gather=False,
                                              name=f"scatter_start_{tag}"))
        return [scattered[tag][1]["token"]]

    sm = jax.nn.softmax(lb_logits, axis=0)
    csum = jnp.cumsum(sm, axis=0)
    small = dict(lb=(csum - csum[0:1])[1:2], lb_logits=lb_logits, ln_mix_g=ln_mix_g, ln_mix_b=ln_mix_b,
                 ln_ffn_g=ln_ffn_g, ln_ffn_b=ln_ffn_b)
    grad_x = _local_step(x[0], loss_target[0], get_w, small, on_grads, deps=first)
    out = {}

    def stack_small(src_):
        rows = [None] * SMALL_ROWS
        for name in SMALL:
            rows[SMALL_ROW[name]], rows[SMALL_ROW[name] + 1] = src_[name][0:1], src_[name][1:2]
        zero = jnp.zeros((1, x.shape[-1]), F32)
        return jnp.concatenate([zero if r is None else r for r in rows], axis=0)[None]

    def update(name, slabs):
        shape = wts[name].shape
        out[name] = [r.reshape(shape) for r in _adamw(slabs, wts[name], mom[name], vel[name], name=f"adamw_{name}")]
        return out[name][0]

    slabs, after = {}, grad_x
    for tag, (gnames, xc) in scattered.items():
        slabs.update(zip(gnames, _xchg_wait(xc, list(range(len(gnames))), after, name=f"scatter_wait_{tag}")))
        if tag == "ffn1":
            continue
        if tag == "ffn0":
            update("ffn_w_down", [slabs["ffn_w_down0"], slabs["ffn_w_down1"]])
            after = update("ffn_w_up", [slabs["ffn_w_up0"], slabs["ffn_w_up1"]])
        elif tag == "hgrn":
            update("hgrn_w_out", [slabs["hgrn_w_out"]])
            after = update("hgrn_w_in", [slabs["hgrn_w_in"]])
        elif tag == "attn_out":
            after = update("attn_w_out", [slabs["attn_w_out"]])
        else:
            after = update("attn_w_in", [slabs["attn_w_in"]])
    res = _adamw([slabs["small"]], stack_small(wts), stack_small(mom), stack_small(vel), name="adamw_small")
    for name in SMALL:
        out[name] = [r[0, SMALL_ROW[name]:SMALL_ROW[name] + 2] for r in res]
    loss = res[0][0, LOSS_ROW, 0]
    ng = hgrn_norm_g.shape[-1]
    ng_slabs = lax.dynamic_slice(slabs["small"], (0, NORM_G_ROW, me * ng), (N_DEV, 1, ng))
    out["hgrn_norm_g"] = [r[0] for r in _adamw([ng_slabs], hgrn_norm_g[None], m_hgrn_norm_g[None],
                                                v_hgrn_norm_g[None], name="adamw_norm_g")]
    order =("attn_w_in", "attn_w_out", "hgrn_w_in", "hgrn_w_out", "hgrn_norm_g", "lb_logits", "ln_mix_g", "ln_mix_b",
             "ln_ffn_g", "ln_ffn_b", "ffn_w_up", "ffn_w_down")
    return (loss, grad_x[None], *[out[nm][0] for nm in order], *[out[nm][1] for nm in order],
            *[out[nm][2] for nm in order], *[out[nm][3] for nm in order])
```

```python
import jax
import jax.numpy as jnp
from jax import lax
from jax.experimental import pallas as pl
from jax.experimental.pallas import tpu as pltpu

F32 = jnp.float32
BF16 = jnp.bfloat16

N_DEV = 8
LANES = 128
D_MODEL = 1024
ATTN_HEAD_DIM = 64
ATTN_HEADS = 16
ATTN_SCALE = ATTN_HEAD_DIM ** -0.5
ATTN_BLK = 128
DILATIONS = (1, 4, 16)
ROPE_THETA = 10000.0
HGRN_HEADS = 8
HGRN_CHUNK = 64
D_FF = 4096
LN_EPS = 1e-5
RMS_EPS = 1e-6
DEPTH = 2
ALPHA = (2 * DEPTH) ** 0.25
ADAM_LR, ADAM_B1, ADAM_B2, ADAM_EPS, ADAM_WD, ADAM_STEP = 0.001, 0.9, 0.999, 1e-08, 0.01, 10
VMEM_LIMIT = 48 * 1024 * 1024

_NT = (((1,), (1,)), ((), ()))
_TN = (((0,), (0,)), ((), ()))


def _dot(a, b):
    return jnp.dot(a, b, preferred_element_type=F32)


def _dot_nt(a, b):
    return lax.dot_general(a, b, _NT, preferred_element_type=F32)


def _dot_tn(a, b):
    return lax.dot_general(a, b, _TN, preferred_element_type=F32)


def _split3(x):
    p1 = x.astype(BF16)
    r1 = x - p1.astype(F32)
    p2 = r1.astype(BF16)
    p3 = (r1 - p2.astype(F32)).astype(BF16)
    return p1, p2, p3


def _exact_dot(sel3, x):
    return _dot(sel3, jnp.concatenate(_split3(x), axis=0))


def _exact_dot_r(x, sel3):
    return _dot(jnp.concatenate(_split3(x), axis=1), sel3)


def _params(sem=None):
    return pltpu.CompilerParams(dimension_semantics=sem, vmem_limit_bytes=VMEM_LIMIT)


def _mm(a, b, mode, *, name, m, n, k, tm=1024, tn=1024, tk=1024, out_dtypes=(F32,), epi=None, a_pre=None,
        tile_extras=(), row_extras=(), vec_extras=(), row_outs=(), vec_outs=0, a_split=1, b_split=1, out_split=1,
        b_col_off=0, b_k_off=0, out_col_off=0, out_cols=None, alias=None, deps=(), row_slabs=1):
    tm, tn, tk = min(tm, m), min(tn, n), min(tk, k)
    assert m % tm == 0 and n % tn == 0 and k % tk == 0, (name, m, n, k, tm, tn, tk)
    gm, gn, gk = m // tm, n // tn, k // tk
    if mode in ("nn", "nt"):
        if a_split > 1 and tk == k:
            a_spec = pl.BlockSpec((a_split, tm, k // a_split), lambda i, j, kk: (0, i, 0))
        elif a_split > 1:
            kc = (k // a_split) // tk
            a_spec = pl.BlockSpec((None, tm, tk), lambda i, j, kk: (kk // kc, i, kk % kc))
        else:
            a_spec = pl.BlockSpec((tm, tk), lambda i, j, kk: (i, kk))
    else:
        a_spec = pl.BlockSpec((tk, tm), lambda i, j, kk: (kk, i))
    if mode in ("nn", "tn"):
        if b_split > 1:
            nc = (n // b_split) // tn
            b_spec = pl.BlockSpec((None, tk, tn), lambda i, j, kk: (j // nc, kk, j % nc))
        else:
            b_spec = pl.BlockSpec((tk, tn), lambda i, j, kk: (kk + b_k_off, j + b_col_off))
    else:
        b_spec = pl.BlockSpec((tn, tk), lambda i, j, kk: (j + b_col_off, kk + b_k_off))
    if out_split > 1:
        nco = (n // out_split) // tn
        o_spec = pl.BlockSpec((None, tm, tn), lambda i, j, kk: (j // nco, i, j % nco))
        o_shape = (out_split, m, n // out_split)
    else:
        o_spec = pl.BlockSpec((tm, tn), lambda i, j, kk: (i, j + out_col_off))
        o_shape = (m, out_cols if out_cols is not None else n)
    n_ex = len(tile_extras) + len(row_extras) + len(vec_extras)
    n_out = len(out_dtypes)
    n_plain = n_out + len(row_outs)
    assert not vec_outs or gn == 1
    assert row_slabs == 1 or (gk == 1 and mode != "tn" and out_split == 1)
    if epi is None:
        def epi(acc):
            return (acc,)
    dot = {"nn": _dot, "nt": _dot_nt, "tn": _dot_tn}[mode]

    def body(*refs):
        a_ref, b_ref = refs[0], refs[1]
        ex = refs[2:2 + n_ex]
        outs = refs[2 + n_ex + (1 if alias is not None else 0) + len(deps):][:n_plain + vec_outs]
        ii = pl.program_id(0)
        n_rowwise = len(tile_extras) + len(row_extras)
        everything = slice(None)

        def product(rows=everything):
            if a_split > 1 and tk == k:
                av = jnp.concatenate([a_ref[p, rows, :] for p in range(a_split)], axis=1)
            else:
                av = a_ref[...] if rows is everything else a_ref[rows, :]
            if a_pre is not None:
                av = a_pre(av)
            return dot(av.astype(BF16), b_ref[...].astype(BF16))

        def finish(total, rows=everything):
            res = epi(total, *[e[rows, :] for e in ex[:n_rowwise]], *[e[...] for e in ex[n_rowwise:]])
            for o, r in zip(outs[:n_plain], res):
                o[rows, :] = r.astype(o.dtype)
            return res[n_plain:]

        def add_up(parts):
            for o, r in zip(outs[n_plain:], parts):
                @pl.when(ii == 0)
                def _(o=o, r=r):
                    o[...] = r

                @pl.when(ii > 0)
                def _(o=o, r=r):
                    o[...] += r

        if row_slabs > 1:
            slabs = [pl.ds(c * (tm // row_slabs), tm // row_slabs) for c in range(row_slabs)]
            pending, parts = None, []
            for rows in slabs:
                total = product(rows)
                if pending is not None:
                    parts.append(finish(*pending))
                pending = (total, rows)
            parts.append(finish(*pending))
            add_up([sum(p[v] for p in parts) for v in range(vec_outs)])
        elif gk == 1:
            add_up(finish(product()))
        else:
            acc = refs[-1]
            kk = pl.program_id(2)

            @pl.when(kk == 0)
            def _():
                acc[...] = product()

            @pl.when(kk > 0)
            def _():
                acc[...] += product()

            @pl.when(kk == gk - 1)
            def _():
                add_up(finish(acc[...]))

    in_specs = [a_spec, b_spec] + [o_spec] * len(tile_extras)
    in_specs += [pl.BlockSpec((tm, r.shape[1]), lambda i, j, kk: (i, 0)) for r in row_extras]
    in_specs += [pl.BlockSpec((1, tn), lambda i, j, kk: (0, j))] * len(vec_extras)
    args = [a, b] + list(tile_extras) + list(row_extras) + list(vec_extras)
    io_alias = {}
    if alias is not None:
        in_specs.append(pl.BlockSpec(memory_space=pl.ANY))
        args.append(alias)
        io_alias = {len(args) - 1: 0}
    in_specs += [pl.BlockSpec(memory_space=pl.ANY)] * len(deps)
    args += list(deps)
    out_specs = [o_spec] * n_out
    out_shape = [jax.ShapeDtypeStruct(o_shape, dt) for dt in out_dtypes]
    for dt, w in row_outs:
        out_specs.append(pl.BlockSpec((tm, w), lambda i, j, kk: (i, 0)))
        out_shape.append(jax.ShapeDtypeStruct((m, w), dt))
    out_specs += [pl.BlockSpec((1, tn), lambda i, j, kk: (0, j))] * vec_outs
    out_shape += [jax.ShapeDtypeStruct((1, n), F32)] * vec_outs
    out = pl.pallas_call(
        body, name=name, grid=(gm, gn, gk), in_specs=in_specs, out_specs=out_specs, out_shape=out_shape,
        scratch_shapes=[pltpu.VMEM((tm, tn), F32)] if gk > 1 else [],
        input_output_aliases=io_alias,
        compiler_params=_params(("arbitrary" if vec_outs else "parallel", "parallel", "arbitrary")),
    )(*args)
    return out[0] if len(out) == 1 else out


def _rope_tables(seq, dil):
    pos = jnp.arange(seq, dtype=jnp.int32).reshape(seq // dil, dil).T.reshape(seq)
    half = ATTN_HEAD_DIM // 2
    inv = ROPE_THETA ** (-jnp.arange(half, dtype=F32) * (2.0 / ATTN_HEAD_DIM))
    ang = pos.astype(F32)[:, None] * inv[None, :]
    cos, sin = jnp.cos(ang), jnp.sin(ang)
    reps = LANES // ATTN_HEAD_DIM
    return (jnp.tile(jnp.concatenate([cos, cos], axis=1), (1, reps)),
            jnp.tile(jnp.concatenate([-sin, sin], axis=1), (1, reps)))


def _rotate(x, c, ss, sign=1.0):
    w = x.shape[-1]
    half = ATTN_HEAD_DIM // 2
    lane = lax.broadcasted_iota(jnp.int32, x.shape, 1)
    first = (lane % ATTN_HEAD_DIM) < half
    partner = jnp.where(first, pltpu.roll(x, w - half, 1), pltpu.roll(x, half, 1))
    reps = w // LANES
    if reps > 1:
        c = jnp.concatenate([c] * reps, axis=1)
        ss = jnp.concatenate([ss] * reps, axis=1)
    return x * c + sign * (partner * ss)


def _rotate_t(xt, ct, sst):
    half = ATTN_HEAD_DIM // 2
    heads = xt.shape[0] // ATTN_HEAD_DIM
    parts = []
    for h in range(heads):
        lo = h * ATTN_HEAD_DIM
        parts += [xt[lo + half:lo + 2 * half], xt[lo:lo + half]]
    return (xt * jnp.concatenate([ct] * heads, axis=0)
            + jnp.concatenate(parts, axis=0) * jnp.concatenate([sst] * heads, axis=0))


ATTN_IN_TOKENS = 1024
ATTN_IN_SLAB = 128


def _attn_in(xs_t, w, tables_t, group, *, name, deps=()):
    dil, d, n = xs_t.shape
    s = dil * n
    tt = min(ATTN_IN_TOKENS, s)
    per_tile = max(1, tt // n)
    nd = len(deps)

    def body(w_ref, x_ref, ct_ref, sst_ref, *refs):
        q_ref, qt_ref, wt = refs[nd:]
        j, i = pl.program_id(0), pl.program_id(1)

        @pl.when(i == 0)
        def _():
            wt[...] = w_ref[...].T

        xt = x_ref[...] if per_tile == 1 else jnp.concatenate([x_ref[c] for c in range(per_tile)], axis=1)
        ct, sst = ct_ref[...], sst_ref[...]
        q_scale = jnp.where(j == 0, ATTN_SCALE, 1.0)

        def finish(rows, acc_t):
            rot_t = _rotate_t(acc_t, ct, sst) * q_scale
            out_t = jnp.where(j >= 2, acc_t, rot_t).astype(BF16)
            qt_ref[rows, :] = out_t
            q_ref[:, rows] = out_t.T

        slabs = [pl.ds(c * ATTN_IN_SLAB, ATTN_IN_SLAB) for c in range(d // ATTN_IN_SLAB)]
        pending = None
        for rows in slabs:
            acc_t = _dot(wt[rows, :], xt)
            if pending is not None:
                finish(*pending)
            pending = (rows, acc_t)
        finish(*pending)

    if per_tile == 1:
        tiles_per_class = n // tt
        x_spec = pl.BlockSpec((None, d, tt), lambda j, i: (i // tiles_per_class, 0, i % tiles_per_class))
    else:
        x_spec = pl.BlockSpec((per_tile, d, n), lambda j, i: (i, 0, 0))
    tab = pl.BlockSpec((ATTN_HEAD_DIM, tt), lambda j, i: (0, i))
    return pl.pallas_call(
        body, name=name, grid=(3, s // tt),
        in_specs=[pl.BlockSpec((d, d), lambda j, i: (0, 3 * group + j)), x_spec, tab, tab]
        + [pl.BlockSpec(memory_space=pl.ANY)] * nd,
        out_specs=[pl.BlockSpec((None, tt, d), lambda j, i: (j, i, 0)),
                   pl.BlockSpec((None, d, tt), lambda j, i: (j, 0, i))],
        out_shape=[jax.ShapeDtypeStruct((3, s, d), BF16), jax.ShapeDtypeStruct((3, d, s), BF16)],
        scratch_shapes=[pltpu.VMEM((d, d), BF16)],
        compiler_params=_params(("arbitrary", "arbitrary")),
    )(w, xs_t, *tables_t, *deps)


def _ln_epi(acc, x, g, b):
    u = ALPHA * x + acc
    mu = jnp.mean(u, axis=-1, keepdims=True)
    uc = u - mu
    var = jnp.mean(uc * uc, axis=-1, keepdims=True)
    rstd = lax.rsqrt(var + LN_EPS)
    xh = uc * rstd
    out = xh * g + b
    return out, out, xh, rstd


def _ln_grad(dy, xh, rstd, g):
    dxh = dy * g
    m1 = jnp.mean(dxh, axis=-1, keepdims=True)
    m2 = jnp.mean(dxh * xh, axis=-1, keepdims=True)
    du = rstd * (dxh - m1 - xh * m2)
    return du, du, jnp.sum(dy * xh, axis=0, keepdims=True), jnp.sum(dy, axis=0, keepdims=True)


def _ln_grad_epi(acc, du_next, xh, rstd, g):
    return _ln_grad(acc + ALPHA * du_next, xh, rstd, g)


def _ln_loss_epi(acc, x_in, target, g, b):
    out, _, xh, rstd = _ln_epi(acc, x_in, g, b)
    e = out - target
    du, _, dg, db = _ln_grad(e * (1.0 / e.shape[-1]), xh, rstd, g)
    return du, du, dg, db, jnp.sum(e * e, axis=0, keepdims=True)


POS_BLK = 2048


def _class_rows(r, dil):
    return pl.ds(r, POS_BLK // dil, stride=dil) if dil > 1 else pl.ds(0, POS_BLK)


def _class_view(a, dil):
    s, w = a.shape
    return a.reshape(dil, s // dil, w)


def _class_spec(dil, all_tiles=True):
    if all_tiles:
        return pl.BlockSpec((dil, POS_BLK // dil, LANES), lambda i, t: (0, i, t))
    return pl.BlockSpec((dil, POS_BLK // dil, LANES), lambda i, t: (0, i, 0))


def _pos_spec(all_tiles=True):
    if all_tiles:
        return pl.BlockSpec((POS_BLK, LANES), lambda i, t: (i, t))
    return pl.BlockSpec((POS_BLK, LANES), lambda i, t: (i, 0))


def _prep_x(x, deps=()):
    s, d = x.shape
    n_dil = len(DILATIONS)

    def body(x_ref, *refs):
        outs = refs[len(deps):]
        for g, dil in enumerate(DILATIONS):
            for r in range(dil):
                blk = x_ref[_class_rows(r, dil), :].astype(BF16)
                outs[g][r] = blk
                outs[n_dil + g][r] = blk.T

    outs = pl.pallas_call(
        body, name="prep_x", grid=(s // POS_BLK, d // LANES),
        in_specs=[_pos_spec()] + [pl.BlockSpec(memory_space=pl.ANY)] * len(deps),
        out_specs=[_class_spec(dil) for dil in DILATIONS]
        + [pl.BlockSpec((dil, LANES, POS_BLK // dil), lambda i, t: (0, t, i)) for dil in DILATIONS],
        out_shape=[jax.ShapeDtypeStruct((dil, s // dil, d), BF16) for dil in DILATIONS]
        + [jax.ShapeDtypeStruct((dil, d, s // dil), BF16) for dil in DILATIONS],
        compiler_params=_params(("parallel", "parallel")),
    )(x, *deps)
    return [o.reshape(s, d) for o in outs[:n_dil]], list(outs[n_dil:])


def _head_expand_matrix():
    h = lax.broadcasted_iota(jnp.int32, (LANES, D_MODEL), 0)
    l = lax.broadcasted_iota(jnp.int32, (LANES, D_MODEL), 1)
    return (l // ATTN_HEAD_DIM == h).astype(BF16)


def _attn_fwd(qkv, dil, *, name):
    _, s, d = qkv.shape
    nq = s // ATTN_BLK
    per = nq // dil
    tiles = d // LANES

    def body(q_ref, kc_ref, kp_ref, vc_ref, vp_ref, o_ref, lse_ref):
        qb = pl.program_id(0)
        first = (qb % per) == 0
        qi = lax.broadcasted_iota(jnp.int32, (ATTN_BLK, 2 * ATTN_BLK), 0)
        kj = lax.broadcasted_iota(jnp.int32, (ATTN_BLK, 2 * ATTN_BLK), 1)
        dist = qi + ATTN_BLK - kj
        valid = (dist >= 0) & (dist <= ATTN_BLK) & ((kj >= ATTN_BLK) | jnp.logical_not(first))
        lane = lax.broadcasted_iota(jnp.int32, (ATTN_BLK, LANES), 1)
        lse_tile = jnp.zeros((ATTN_BLK, LANES), F32)
        zero = jnp.zeros((), BF16)
        in_head = [(lane // ATTN_HEAD_DIM) == hh for hh in range(2)]
        for t0 in range(0, tiles, ATTN_FWD_TILE_GROUP):
            group = range(t0, t0 + ATTN_FWD_TILE_GROUP)
            heads = [(t, hh) for t in group for hh in range(2)]
            cols = {t: pl.ds(t * LANES, LANES) for t in group}
            k2 = {t: jnp.concatenate([kp_ref[:, cols[t]], kc_ref[:, cols[t]]], axis=0) for t in group}
            v2 = {t: jnp.concatenate([vp_ref[:, cols[t]], vc_ref[:, cols[t]]], axis=0) for t in group}
            sc = {(t, hh): jnp.where(valid, _dot_nt(jnp.where(in_head[hh], q_ref[:, cols[t]], zero), k2[t]),
                                     -jnp.inf) for t, hh in heads}
            mx = {i: jnp.max(sc[i], axis=-1, keepdims=True) for i in heads}
            p = {i: jnp.exp(sc[i] - mx[i]) for i in heads}
            l = {i: jnp.sum(p[i], axis=-1, keepdims=True) for i in heads}
            oh = {i: _dot(p[i].astype(BF16), v2[i[0]]) / l[i] for i in heads}
            for t in group:
                o_ref[:, cols[t]] = jnp.where(in_head[0], oh[t, 0], oh[t, 1])
                for hh in range(2):
                    lse_tile = jnp.where(lane == 2 * t + hh, mx[t, hh] + jnp.log(l[t, hh]), lse_tile)
        lse_ref[...] = lse_tile

    def blk(piece, prev):
        if prev:
            return pl.BlockSpec((None, ATTN_BLK, d), lambda i: (piece, jnp.maximum(i - 1, 0), 0))
        return pl.BlockSpec((None, ATTN_BLK, d), lambda i: (piece, i, 0))

    return pl.pallas_call(
        body, name=name, grid=(nq,),
        in_specs=[blk(0, False), blk(1, False), blk(1, True), blk(2, False), blk(2, True)],
        out_specs=[pl.BlockSpec((ATTN_BLK, d), lambda i: (i, 0)), pl.BlockSpec((ATTN_BLK, LANES), lambda i: (i, 0))],
        out_shape=[jax.ShapeDtypeStruct((s, d), F32), jax.ShapeDtypeStruct((s, LANES), F32)],
        compiler_params=_params(("parallel",)),
    )(qkv, qkv, qkv, qkv, qkv)


def _attn_combine(os_, lses):
    s, d = os_[0].shape
    sel = jnp.concatenate([_head_expand_matrix()] * 3, axis=0)

    def body(o0, o1, o2, l0, l1, l2, sel_ref, of_ref, ob_ref, lt_ref, o_pos, l_pos):
        for g, (dil, o_ref, l_ref) in enumerate(zip(DILATIONS, (o0, o1, o2), (l0, l1, l2))):
            for r in range(dil):
                o_pos[g, _class_rows(r, dil), :] = o_ref[r]
                l_pos[g, _class_rows(r, dil), :] = l_ref[r]
        la, lb_, lc = l_pos[0], l_pos[1], l_pos[2]
        mx = jnp.maximum(jnp.maximum(la, lb_), lc)
        es = (jnp.exp(la - mx), jnp.exp(lb_ - mx), jnp.exp(lc - mx))
        z = es[0] + es[1] + es[2]
        lt_ref[...] = mx + jnp.log(z)
        acc = jnp.zeros((POS_BLK, LANES), F32)
        for g in range(3):
            acc += _exact_dot_r(es[g] / z, sel_ref[...]) * o_pos[g]
        of_ref[...] = acc
        ob_ref[...] = acc.astype(BF16)

    return pl.pallas_call(
        body, name="attn_combine", grid=(s // POS_BLK, d // LANES),
        in_specs=[_class_spec(dil) for dil in DILATIONS] + [_class_spec(dil, False) for dil in DILATIONS]
        + [pl.BlockSpec((3 * LANES, LANES), lambda i, t: (0, t))],
        out_specs=[_pos_spec(), _pos_spec(), _pos_spec(False)],
        out_shape=[jax.ShapeDtypeStruct((s, d), F32), jax.ShapeDtypeStruct((s, d), BF16),
                   jax.ShapeDtypeStruct((s, LANES), F32)],
        scratch_shapes=[pltpu.VMEM((3, POS_BLK, LANES), F32), pltpu.VMEM((3, POS_BLK, LANES), F32)],
        compiler_params=_params(("parallel", "arbitrary")),
    )(*[_class_view(o, dil) for o, dil in zip(os_, DILATIONS)],
      *[_class_view(l, dil) for l, dil in zip(lses, DILATIONS)], sel)


ATTN_TILE_GROUP = 2
ATTN_FWD_TILE_GROUP = 4
AUX_PER_TILE = 12


def _aux_placement():
    h = lax.broadcasted_iota(jnp.int32, (6, LANES, LANES), 1)
    l = lax.broadcasted_iota(jnp.int32, (6, LANES, LANES), 2)
    j = lax.broadcasted_iota(jnp.int32, (6, LANES, LANES), 0)
    target = AUX_PER_TILE * (h // 2) + 3 * (h % 2) + jnp.where(j < 3, j, 3 + j)
    return ((l == target) & (h < ATTN_HEADS)).astype(BF16)


def _attn_bwd_prep(do, o, lse):
    s, d = do.shape
    tiles = d // LANES
    sel_t = jnp.tile(_head_expand_matrix().T.reshape(tiles, LANES, LANES), (1, 3, 1))

    def body(do_ref, o_ref, l_ref, sel_ref, place_ref, *refs):
        outs, delta, aux = refs[:9], refs[9], refs[10]
        t = pl.program_id(1)
        part = _exact_dot_r(do_ref[...] * o_ref[...], sel_ref[...])

        @pl.when(t == 0)
        def _():
            delta[...] = part

        @pl.when(t > 0)
        def _():
            delta[...] += part

        for g, dil in enumerate(DILATIONS):
            for r in range(dil):
                blk = do_ref[_class_rows(r, dil), :].astype(BF16)
                outs[g][r] = blk
                outs[3 + g][r] = blk.T

        @pl.when(t == tiles - 1)
        def _():
            pieces = _split3(l_ref[...]) + _split3(delta[...])
            aux[...] = _dot(jnp.concatenate(pieces, axis=1), place_ref[...])
            for g, dil in enumerate(DILATIONS):
                for r in range(dil):
                    outs[6 + g][r] = aux[_class_rows(r, dil), :].astype(BF16)

    outs = pl.pallas_call(
        body, name="attn_bwd_prep", grid=(s // POS_BLK, tiles),
        in_specs=[_pos_spec(), _pos_spec(), _pos_spec(False),
                  pl.BlockSpec((None, 3 * LANES, LANES), lambda i, t: (t, 0, 0)),
                  pl.BlockSpec((6 * LANES, LANES), lambda i, t: (0, 0))],
        out_specs=[_class_spec(dil) for dil in DILATIONS]
        + [pl.BlockSpec((dil, LANES, POS_BLK // dil), lambda i, t: (0, t, i)) for dil in DILATIONS]
        + [_class_spec(dil, False) for dil in DILATIONS],
        out_shape=[jax.ShapeDtypeStruct((dil, s // dil, d), BF16) for dil in DILATIONS]
        + [jax.ShapeDtypeStruct((dil, d, s // dil), BF16) for dil in DILATIONS]
        + [jax.ShapeDtypeStruct((dil, s // dil, LANES), BF16) for dil in DILATIONS],
        scratch_shapes=[pltpu.VMEM((POS_BLK, LANES), F32), pltpu.VMEM((POS_BLK, LANES), F32)],
        compiler_params=_params(("parallel", "arbitrary")),
    )(do, o, lse, sel_t, _aux_placement().reshape(6 * LANES, LANES))
    return ([a.reshape(s, d) for a in outs[0:3]], list(outs[3:6]), [a.reshape(s, LANES) for a in outs[6:9]])


def _attn_bwd(qkv, qkv_t, do, do_t, aux, tables, dil, *, name, deps=()):
    _, s, d = qkv.shape
    nq = s // ATTN_BLK
    per = nq // dil
    tiles = d // LANES
    half = ATTN_HEAD_DIM

    def body(qd_ref, qo_ref, k_ref, qtd_ref, qto_ref, kt_ref, vt_ref, dod_ref, doo_ref, dotd_ref, doto_ref,
             auxd_ref, auxo_ref, c_ref, ss_ref, *refs):
        out_ref, carry = refs[len(deps):]
        kb = pl.program_id(0)

        @pl.when(kb == 0)
        def _():
            carry[...] = jnp.zeros_like(carry)

        has_next = (kb % per) != (per - 1)
        qi = lax.broadcasted_iota(jnp.int32, (ATTN_BLK, 2 * ATTN_BLK), 0)
        kj = lax.broadcasted_iota(jnp.int32, (ATTN_BLK, 2 * ATTN_BLK), 1) % ATTN_BLK
        valid_d = kj <= qi
        valid_o = (kj >= qi) & has_next
        lane = lax.broadcasted_iota(jnp.int32, (ATTN_BLK, LANES), 1)
        row = lax.broadcasted_iota(jnp.int32, (LANES, ATTN_BLK), 0)
        side = lax.broadcasted_iota(jnp.int32, (LANES, 2 * ATTN_BLK), 0)
        first = lax.broadcasted_iota(jnp.int32, (LANES, 2 * ATTN_BLK), 1) < ATTN_BLK
        c, ss = c_ref[...], ss_ref[...]
        zero = jnp.zeros((), BF16)
        sides = ((qd_ref, qtd_ref, dod_ref, dotd_ref, auxd_ref[...], valid_d),
                 (qo_ref, qto_ref, doo_ref, doto_ref, auxo_ref[...], valid_o))
        both = (0, 1)

        def head_halves(x, index):
            axis = 0 if index is lane else 1
            return jnp.concatenate([jnp.where(index < half, x, zero), jnp.where(index >= half, x, zero)], axis=axis)

        for t0 in range(0, tiles, ATTN_TILE_GROUP):
            group = range(t0, t0 + ATTN_TILE_GROUP)
            cols = {t: pl.ds(t * LANES, LANES) for t in group}
            kk, kk_t, vv_t = {}, {}, {}
            for t in group:
                base = AUX_PER_TILE * t
                hit = lambda lo: ((first & (side >= base + lo) & (side < base + lo + 3))
                                  | (jnp.logical_not(first) & (side >= base + lo + 3) & (side < base + lo + 6)))
                kk[t] = head_halves(k_ref[:, cols[t]], lane)
                kk_t[t] = jnp.concatenate([head_halves(kt_ref[cols[t], :], row),
                                           jnp.where(hit(0), -1.0, 0.0).astype(BF16)], axis=0)
                vv_t[t] = jnp.concatenate([head_halves(vt_ref[cols[t], :], row),
                                           jnp.where(hit(6), -1.0, 0.0).astype(BF16)], axis=0)
            sc = {(t, w): _dot(jnp.concatenate([sides[w][0][:, cols[t]], sides[w][4]], axis=1), kk_t[t])
                  for t in group for w in both}
            dpd = {(t, w): _dot(jnp.concatenate([sides[w][2][:, cols[t]], sides[w][4]], axis=1), vv_t[t])
                   for t in group for w in both}
            p = {i: jnp.where(sides[i[1]][5], jnp.exp(sc[i]), 0.0) for i in sc}
            ds = {i: (p[i] * dpd[i]).astype(BF16) for i in sc}
            pb = {i: p[i].astype(BF16) for i in sc}
            dv_t = {t: sum(_dot(sides[w][3][cols[t], :], pb[t, w]) for w in both) for t in group}
            dk_t = {t: sum(_dot(sides[w][1][cols[t], :], ds[t, w]) for w in both) for t in group}
            dq = {i: _dot(ds[i], kk[i[0]]) * ATTN_SCALE for i in sc}
            for t in group:
                dq_now = carry[:, cols[t]] + dq[t, 0]
                carry[:, cols[t]] = dq[t, 1]
                dk = jnp.where(row < half, dk_t[t][:, :ATTN_BLK], dk_t[t][:, ATTN_BLK:]).T
                dv = jnp.where(row < half, dv_t[t][:, :ATTN_BLK], dv_t[t][:, ATTN_BLK:]).T
                out_ref[0, :, cols[t]] = _rotate(dq_now, c, ss, -1.0).astype(BF16)
                out_ref[1, :, cols[t]] = _rotate(dk, c, ss, -1.0).astype(BF16)
                out_ref[2, :, cols[t]] = dv.astype(BF16)

    def nxt(i):
        return jnp.minimum(i + 1, nq - 1)

    def piece(p, shift):
        if shift:
            return pl.BlockSpec((None, ATTN_BLK, d), lambda i: (p, nxt(i), 0))
        return pl.BlockSpec((None, ATTN_BLK, d), lambda i: (p, i, 0))

    def piece_t(p, shift):
        if shift:
            return pl.BlockSpec((None, d, ATTN_BLK), lambda i: (p, 0, nxt(i)))
        return pl.BlockSpec((None, d, ATTN_BLK), lambda i: (p, 0, i))

    def rows(width, shift):
        if shift:
            return pl.BlockSpec((ATTN_BLK, width), lambda i: (nxt(i), 0))
        return pl.BlockSpec((ATTN_BLK, width), lambda i: (i, 0))

    def do_t_spec(shift):
        if shift:
            return pl.BlockSpec((None, d, ATTN_BLK), lambda i: (nxt(i) // per, 0, nxt(i) % per))
        return pl.BlockSpec((None, d, ATTN_BLK), lambda i: (i // per, 0, i % per))

    return pl.pallas_call(
        body, name=name, grid=(nq,),
        in_specs=[piece(0, False), piece(0, True), piece(1, False),
                  piece_t(0, False), piece_t(0, True), piece_t(1, False), piece_t(2, False),
                  rows(d, False), rows(d, True), do_t_spec(False), do_t_spec(True),
                  rows(LANES, False), rows(LANES, True), rows(LANES, False), rows(LANES, False)]
        + [pl.BlockSpec(memory_space=pl.ANY)] * len(deps),
        out_specs=pl.BlockSpec((3, ATTN_BLK, d), lambda i: (0, i, 0)),
        out_shape=jax.ShapeDtypeStruct((3, s, d), BF16),
        scratch_shapes=[pltpu.VMEM((ATTN_BLK, d), F32)],
        compiler_params=_params(("arbitrary",)),
    )(qkv, qkv, qkv, qkv_t, qkv_t, qkv_t, qkv_t, do, do, do_t, do_t, aux, aux, *tables, *deps)


def _dx_combine(du, parts):
    s, d = du.shape

    def body(du_ref, p0, p1, p2, out_ref):
        out_ref[...] = ALPHA * du_ref[...] + p0[0]
        for dil, p_ref in zip(DILATIONS[1:], (p1, p2)):
            for r in range(dil):
                out_ref[_class_rows(r, dil), :] += p_ref[r]

    return pl.pallas_call(
        body, name="dx_combine", grid=(s // POS_BLK, d // LANES),
        in_specs=[_pos_spec()] + [_class_spec(dil) for dil in DILATIONS], out_specs=_pos_spec(),
        out_shape=jax.ShapeDtypeStruct((s, d), F32),
        compiler_params=_params(("parallel", "parallel")),
    )(du, *[_class_view(p, dil) for p, dil in zip(parts, DILATIONS)])


HGRN_ROWS = 2048


def _tri(lower, copies=1):
    i = lax.broadcasted_iota(jnp.int32, (HGRN_CHUNK, HGRN_CHUNK * copies), 0)
    j = lax.broadcasted_iota(jnp.int32, (HGRN_CHUNK, HGRN_CHUNK * copies), 1) % HGRN_CHUNK
    return (j <= i) if lower else (j >= i)


def _hgrn_gates(qr, z, lb):
    sq = jax.nn.sigmoid(qr)
    e = jnp.exp(-jnp.abs(z))
    big = 1.0 / (1.0 + e)
    small = e * big
    sg = jnp.where(z >= 0, big, small)
    sn = jnp.where(z >= 0, small, big)
    return sq, qr * sq, sg, sn, lb + (1.0 - lb) * sg, (1.0 - lb) * sn


def _hgrn_fwd(p2, lb, norm_g):
    _, s, d = p2.shape
    nblk = s // HGRN_ROWS
    cps = HGRN_ROWS // HGRN_CHUNK

    def body(p_ref, lb_ref, g_ref, o_ref, on_ref, st_ref, state):
        @pl.when(pl.program_id(1) == 0)
        def _():
            state[...] = jnp.zeros_like(state)

        chunks = [pl.ds(c * HGRN_CHUNK, HGRN_CHUNK) for c in range(cps)]
        ltri = _tri(True)
        lsel = _tri(True, 3).astype(BF16)
        _, q, _, _, f, key = _hgrn_gates(p_ref[0], p_ref[1], lb_ref[...])
        lf = jnp.log(f)
        v = [p_ref[2, rows, :].astype(BF16) for rows in chunks]
        b = [_exact_dot(lsel, lf[c * HGRN_CHUNK:(c + 1) * HGRN_CHUNK]) for c in range(cps)]
        b_last = [bc[HGRN_CHUNK - 1:HGRN_CHUNK, :] for bc in b]
        q = [q[c * HGRN_CHUNK:(c + 1) * HGRN_CHUNK] for c in range(cps)]
        key = [key[c * HGRN_CHUNK:(c + 1) * HGRN_CHUNK] for c in range(cps)]
        qd = [(q[c] * jnp.exp(b[c])).astype(BF16) for c in range(cps)]
        kd = [(key[c] * jnp.exp(-b[c])).astype(BF16) for c in range(cps)]
        k2 = [(key[c] * jnp.exp(b_last[c] - b[c])).astype(BF16) for c in range(cps)]
        a = [jnp.where(ltri, _dot_nt(qd[c], kd[c]), 0.0).astype(BF16) for c in range(cps)]
        kv = [_dot_tn(v[c], k2[c]) for c in range(cps)]
        st, sts = state[...], []
        for c in range(cps):
            sts.append(st)
            st_ref[0, c] = st
            st = st * jnp.exp(b_last[c]) + kv[c]
        state[...] = st
        gv = g_ref[...]
        for c in range(cps):
            o = _dot(a[c], v[c]) + _dot_nt(qd[c], sts[c].astype(BF16))
            o_ref[chunks[c], :] = o
            r = lax.rsqrt(jnp.mean(o * o, axis=-1, keepdims=True) + RMS_EPS)
            on_ref[chunks[c], :] = (o * r * gv).astype(BF16)

    vec = pl.BlockSpec((1, LANES), lambda h, c: (0, h))
    col = pl.BlockSpec((HGRN_ROWS, LANES), lambda h, c: (c, h))
    return pl.pallas_call(
        body, name="hgrn_fwd", grid=(HGRN_HEADS, nblk),
        in_specs=[pl.BlockSpec((3, HGRN_ROWS, LANES), lambda h, c: (0, c, h)), vec, vec],
        out_specs=[col, col, pl.BlockSpec((1, cps, LANES, LANES), lambda h, c: (h, c, 0, 0))],
        out_shape=[jax.ShapeDtypeStruct((s, d), F32), jax.ShapeDtypeStruct((s, d), BF16),
                   jax.ShapeDtypeStruct((HGRN_HEADS, s // HGRN_CHUNK, LANES, LANES), F32)],
        scratch_shapes=[pltpu.VMEM((LANES, LANES), F32)],
        compiler_params=_params(("parallel", "arbitrary")),
    )(p2, lb, norm_g)


def _hgrn_bwd(p2, lb, norm_g, o_raw, states, dyn):
    _, s, d = p2.shape
    nblk = s // HGRN_ROWS
    cps = HGRN_ROWS // HGRN_CHUNK

    def body(p_ref, lb_ref, g_ref, o_ref, st_ref, dy_ref, dp_ref, dg_ref, dlb_ref, dstate):
        @pl.when(pl.program_id(1) == 0)
        def _():
            dstate[...] = jnp.zeros_like(dstate)
            dg_ref[...] = jnp.zeros_like(dg_ref)
            dlb_ref[...] = jnp.zeros_like(dlb_ref)

        n = cps
        cut = lambda t: [t[c * HGRN_CHUNK:(c + 1) * HGRN_CHUNK] for c in range(n)]
        chunks = [pl.ds(c * HGRN_CHUNK, HGRN_CHUNK) for c in range(n)]
        lbv = lb_ref[...]
        ltri = _tri(True)
        lsel = _tri(True, 3).astype(BF16)
        usel = _tri(False, 3).astype(BF16)
        last_row = lax.broadcasted_iota(jnp.int32, (HGRN_CHUNK, LANES), 0) == HGRN_CHUNK - 1
        qr, z = p_ref[0], p_ref[1]
        sq, q, sg, sn, f, key = _hgrn_gates(qr, z, lbv)
        lf = jnp.log(f)
        v = [p_ref[2, rows, :].astype(BF16) for rows in chunks]
        o, dyv = o_ref[...], dy_ref[...]
        r = lax.rsqrt(jnp.mean(o * o, axis=-1, keepdims=True) + RMS_EPS)
        oh = o * r
        doh = dyv * g_ref[...]
        do = cut((r * (doh - oh * jnp.mean(doh * oh, axis=-1, keepdims=True))).astype(BF16))
        dg_ref[...] += jnp.sum(dyv * oh, axis=0, keepdims=True)
        b = [_exact_dot(lsel, lfc) for lfc in cut(lf)]
        b_last = [bc[HGRN_CHUNK - 1:HGRN_CHUNK, :] for bc in b]
        q, key = cut(q), cut(key)
        eb = [jnp.exp(bc) for bc in b]
        enb = [jnp.exp(-bc) for bc in b]
        e2 = [jnp.exp(b_last[c] - b[c]) for c in range(n)]
        dec = [jnp.exp(bl) for bl in b_last]
        qd_f = [q[c] * eb[c] for c in range(n)]
        kd_f = [key[c] * enb[c] for c in range(n)]
        k2_f = [key[c] * e2[c] for c in range(n)]
        qd, kd, k2 = ([t.astype(BF16) for t in ts] for ts in (qd_f, kd_f, k2_f))
        a = [jnp.where(ltri, _dot_nt(qd[c], kd[c]), 0.0).astype(BF16) for c in range(n)]
        da = [jnp.where(ltri, _dot_nt(do[c], v[c]), 0.0).astype(BF16) for c in range(n)]
        grow = [_dot_tn(do[c], qd[c]) for c in range(n)]
        dst, dsts = dstate[...], [None] * n
        for c in reversed(range(n)):
            dsts[c] = dst
            dst = dst * dec[c] + grow[c]
        dstate[...] = dst
        st = [st_ref[0, c] for c in range(n)]
        dstb = [t.astype(BF16) for t in dsts]
        dv = [_dot_tn(a[c], do[c]) + _dot_nt(k2[c], dstb[c]) for c in range(n)]
        dqd = [_dot(da[c], kd[c]) + _dot(do[c], st[c].astype(BF16)) for c in range(n)]
        dkd = [_dot_tn(da[c], qd[c]) for c in range(n)]
        dk2 = [_dot(v[c], dstb[c]) for c in range(n)]
        db = []
        for c in range(n):
            ddec = jnp.sum(dsts[c] * st[c], axis=0, keepdims=True)
            db_last = jnp.sum(dk2[c] * k2_f[c], axis=0, keepdims=True) + ddec * dec[c]
            db.append(dqd[c] * qd_f[c] - dkd[c] * kd_f[c] - dk2[c] * k2_f[c] + jnp.where(last_row, db_last, 0.0))
        dlf = [_exact_dot(usel, dbc) for dbc in db]
        f, sg, sn, sq, qr = cut(f), cut(sg), cut(sn), cut(sq), cut(qr)
        dlb_acc = jnp.zeros((1, LANES), F32)
        for c in range(n):
            dkey = dkd[c] * enb[c] + dk2[c] * e2[c]
            common = dlf[c] / f[c] - dkey
            dlb_acc += jnp.sum(common * sn[c], axis=0, keepdims=True)
            dp_ref[0, chunks[c], :] = (dqd[c] * eb[c] * (sq[c] * (1.0 + qr[c] * (1.0 - sq[c])))).astype(BF16)
            dp_ref[1, chunks[c], :] = (common * ((1.0 - lbv) * sg[c] * sn[c])).astype(BF16)
            dp_ref[2, chunks[c], :] = dv[c].astype(BF16)
        dlb_ref[...] += dlb_acc

    def rev(c):
        return nblk - 1 - c

    vec = pl.BlockSpec((1, LANES), lambda h, c: (0, h))
    col = pl.BlockSpec((HGRN_ROWS, LANES), lambda h, c: (rev(c), h))
    p3 = pl.BlockSpec((3, HGRN_ROWS, LANES), lambda h, c: (0, rev(c), h))
    return pl.pallas_call(
        body, name="hgrn_bwd", grid=(HGRN_HEADS, nblk),
        in_specs=[p3, vec, vec, col, pl.BlockSpec((1, cps, LANES, LANES), lambda h, c: (h, rev(c), 0, 0)), col],
        out_specs=[p3, vec, vec],
        out_shape=[jax.ShapeDtypeStruct((3, s, d), BF16), jax.ShapeDtypeStruct((1, d), F32),
                   jax.ShapeDtypeStruct((1, d), F32)],
        scratch_shapes=[pltpu.VMEM((LANES, LANES), F32)],
        compiler_params=_params(("parallel", "arbitrary")),
    )(p2, lb, norm_g, o_raw, states, dyn)


SMALL_ROWS = 16
FUSED_TM = 512
FUSED_SLABS = 4


def _relu2(h):
    r = jnp.maximum(h.astype(F32), 0.0)
    return r * r


def _dact_epi(acc, h):
    return (acc * (2.0 * jnp.maximum(h.astype(F32), 0.0)),)


def _local_step(x, target, get_w, small, on_grads, deps=()):
    s, d = x.shape
    lb = small["lb"]
    gq = 3 * d
    gt = gq // 1024

    def ln_of(a, w, x_in, which, layer, *, name, k, a_pre=None, deps=()):
        g, b = small[f"ln_{which}_g"][layer:layer + 1], small[f"ln_{which}_b"][layer:layer + 1]
        return _mm(a, w, "nn", name=name, m=s, n=d, k=k, tm=FUSED_TM, tk=k, a_pre=a_pre, out_dtypes=(F32, BF16, F32),
                   row_outs=((F32, 1),), epi=_ln_epi, tile_extras=(x_in,), vec_extras=(g, b), deps=deps,
                   row_slabs=FUSED_SLABS)

    def ln_grad_of(a, w, du_next, xh, rstd, which, layer, *, name, k, a_split=1, deps=()):
        return _mm(a, w, "nt", name=name, m=s, n=d, k=k, tm=FUSED_TM, tk=k, a_split=a_split,
                   out_dtypes=(F32, BF16), vec_outs=2, epi=_ln_grad_epi, tile_extras=(du_next, xh), row_extras=(rstd,),
                   vec_extras=(small[f"ln_{which}_g"][layer:layer + 1],), deps=deps)

    def ffn_fwd(xb, x_in, w_up, w_down, tag):
        h = _mm(xb, w_up, "nn", name=f"ffn_up_{tag}", m=s, n=D_FF, k=d, tm=2048, out_dtypes=(BF16,))
        return (h,) + tuple(ln_of(h, w_down, x_in, "ffn", tag, name=f"ffn_down_{tag}", k=D_FF, a_pre=_relu2))

    def ffn_bwd(du, dub, xb, h, w_up, w_down, xh, rstd, tag):
        dh = _mm(dub, w_down, "nt", name=f"ffn_dact_{tag}", m=s, n=D_FF, k=d, tm=2048, out_dtypes=(BF16,), epi=_dact_epi,
                 tile_extras=(h,))
        g_down = _mm(h, dub, "tn", name=f"ffn_gdown_{tag}", m=D_FF, n=d, k=s, tm=512, tk=s, a_pre=_relu2,
                     out_dtypes=(BF16,))
        g_up = _mm(xb, dh, "tn", name=f"ffn_gup_{tag}", m=d, n=D_FF, k=s, tk=s, out_dtypes=(BF16,))
        after = on_grads(f"ffn{tag}", {f"ffn_w_down{tag}": g_down, f"ffn_w_up{tag}": g_up})
        return ln_grad_of(dh, w_up, du, xh, rstd, "mix", tag, name=f"ffn_dx_{tag}", k=D_FF, deps=after)

    xs, xs_t = _prep_x(x, deps)
    w_ain, after = get_w("attn_w_in", xs_t[-1])
    tabs, qkvs, qkv_ts, o_parts, lse_parts = [], [], [], [], []
    for g, dil in enumerate(DILATIONS):
        tabs.append(_rope_tables(s, dil))
        tabs_t = [t[:, :ATTN_HEAD_DIM].T for t in tabs[g]]
        qkv, qkv_t = _attn_in(xs_t[g], w_ain, tabs_t, g, name=f"attn_in_{g}", deps=after if g == 0 else ())
        qkvs.append(qkv)
        qkv_ts.append(qkv_t)
        o_g, lse_g = _attn_fwd(qkv, dil, name=f"attn_fwd_{g}")
        o_parts.append(o_g)
        lse_parts.append(lse_g)
    o_f, o_b, lse_t = _attn_combine(o_parts, lse_parts)
    w_aout, after = get_w("attn_w_out", o_b)
    x1, x1b, xh1, r1 = ln_of(o_b, w_aout, x, "mix", 0, name="attn_out", k=d, deps=after)
    w_up0, w_down0 = get_w("ffn_w_up0", o_b)[0], get_w("ffn_w_down0", o_b)[0]
    h0, x2, x2b, xh2, r2 = ffn_fwd(x1b, x1, w_up0, w_down0, 0)
    w_hin, w_hout, norm_g = get_w("hgrn_w_in", x2b)[0], get_w("hgrn_w_out", x2b)[0], get_w("hgrn_norm_g", x2b)[0]
    p2 = _mm(x2b, w_hin, "nn", name="hgrn_in", m=s, n=3 * d, k=d, tm=2048, out_split=3)
    o_raw, o_n, states = _hgrn_fwd(p2, lb, norm_g)
    w_up1, w_down1 = get_w("ffn_w_up1", o_n)[0], get_w("ffn_w_down1", o_n)[0]
    x3, x3b, xh3, r3 = ln_of(o_n, w_hout, x2, "mix", 1, name="hgrn_out", k=d)
    h1 = _mm(x3b, w_up1, "nn", name="ffn_up_1", m=s, n=D_FF, k=d, tm=2048, out_dtypes=(BF16,))
    du, dub, dg_ffn1, db_ffn1, sq = _mm(
        h1, w_down1, "nn", name="ffn_down_1", m=s, n=d, k=D_FF, tm=FUSED_TM, tk=D_FF, a_pre=_relu2,
        out_dtypes=(F32, BF16), vec_outs=3, epi=_ln_loss_epi, tile_extras=(x3, target),
        vec_extras=(small["ln_ffn_g"][1:2], small["ln_ffn_b"][1:2]), row_slabs=FUSED_SLABS)
    du, dub, dg_mix1, db_mix1 = ffn_bwd(du, dub, x3b, h1, w_up1, w_down1, xh3, r3, 1)
    dyn = _mm(dub, w_hout, "nt", name="hgrn_dout", m=s, n=d, k=d)
    g_hout = _mm(o_n, dub, "tn", name="hgrn_gout", m=d, n=d, k=s, out_dtypes=(BF16,))
    dp2, d_norm_g, d_lb = _hgrn_bwd(p2, lb, norm_g, o_raw, states, dyn)
    g_hin = _mm(x2b, dp2, "tn", name="hgrn_gin", m=d, n=3 * d, k=s, tk=s, b_split=3, out_dtypes=(BF16,))
    after = on_grads("hgrn", {"hgrn_w_out": g_hout, "hgrn_w_in": g_hin})
    du, dub, dg_ffn0, db_ffn0 = ln_grad_of(dp2, w_hin, du, xh2, r2, "ffn", 0, name="hgrn_dx", k=3 * d, a_split=3,
                                           deps=after)
    du, dub, dg_mix0, db_mix0 = ffn_bwd(du, dub, x1b, h0, w_up0, w_down0, xh1, r1, 0)
    g_aout = _mm(o_b, dub, "tn", name="attn_gout", m=d, n=d, k=s, out_dtypes=(BF16,))
    sm1 = jax.nn.softmax(small["lb_logits"], axis=0)
    d_l1 = d_lb * (sm1[0:1] * sm1[1:2])
    zeros = jnp.zeros((SMALL_ROWS - 12, d), F32)
    loss_row = jnp.broadcast_to(0.5 * jnp.sum(sq) / d, (1, d))
    small_grads = jnp.concatenate([dg_mix0, dg_mix1, db_mix0, db_mix1, dg_ffn0, dg_ffn1, db_ffn0, db_ffn1,
                                   -d_l1, d_l1, d_norm_g, loss_row, zeros], axis=0)
    after = on_grads("attn_out", {"attn_w_out": g_aout, "small": small_grads})
    do = _mm(dub, w_aout, "nt", name="attn_dout", m=s, n=d, k=d, deps=after)
    do_parts, do_ts, aux_parts = _attn_bwd_prep(do, o_f, lse_t)
    dqkvs, after = [], []
    for g, dil in enumerate(DILATIONS):
        dqkvs.append(_attn_bwd(qkvs[g], qkv_ts[g], do_parts[g], do_ts[g], aux_parts[g], tabs[g], dil,
                               name=f"attn_bwd_{g}", deps=after))
        g_ain = _mm(xs[g], dqkvs[g], "tn", name=f"attn_gin_{g}", m=d, n=gq, k=s, tk=s, b_split=3, out_dtypes=(BF16,))
        after = on_grads(f"attn_in{g}", {f"attn_w_in@{g}": g_ain})
    dx_parts = [_mm(dqkvs[g], w_ain, "nt", name=f"attn_dx_{g}", m=s, n=d, k=gq, tm=FUSED_TM, tk=gq, a_split=3, b_k_off=g,
                    deps=after if g == 0 else ())
                for g in range(len(DILATIONS))]
    return _dx_combine(du, dx_parts)


def _mesh_place():
    x, y, c = lax.axis_index("x"), lax.axis_index("y"), lax.axis_index("c")
    return x, y, c, 4 * x + 2 * y + c


def _peer(x, y, c, k):
    px = 1 - x if (k >> 2) & 1 else x
    py = 1 - y if (k >> 1) & 1 else y
    pc = 1 - c if k & 1 else c
    return (px, py, pc), 4 * px + 2 * py + pc


def _window(ref, axis, size, idx):
    if axis is None:
        return ref
    sl = [slice(None)] * len(ref.shape)
    sl[axis] = pl.ds(idx * size, size)
    return ref.at[tuple(sl)]


_HBM = pl.BlockSpec(memory_space=pltpu.HBM)
_SEM = pl.BlockSpec(memory_space=pltpu.SEMAPHORE)
_EFFECT = pltpu.SideEffectType.DATAFLOW_SIDE_EFFECTING


def _xchg_ends(src_ref, land_ref, axis, gather, me, other):
    if gather:
        size = src_ref.shape[axis]
        return src_ref, _window(land_ref, axis, size, me), _window(land_ref, axis, size, other)
    size = None if axis is None else src_ref.shape[axis] // N_DEV
    return _window(src_ref, axis, size, other), land_ref.at[me], land_ref.at[other]


def _xchg_start(srcs, lands, axes, *, gather, name, deps=(), peers=tuple(range(1, N_DEV))):
    n = len(srcs)
    nd = len(deps)

    def body(*refs):
        src_refs, land_refs = refs[:n], refs[n:2 * n]
        send, recv = refs[2 * n + nd:3 * n + nd], refs[3 * n + nd:4 * n + nd]
        token = refs[-1]
        x, y, c, me = _mesh_place()
        for k in peers:
            peer, pidx = _peer(x, y, c, k)
            for i in range(n):
                src, dst, _ = _xchg_ends(src_refs[i], land_refs[i], axes[i], gather, me, pidx)
                pltpu.make_async_remote_copy(
                    src_ref=src, dst_ref=dst, send_sem=send[i].at[k - 1], recv_sem=recv[i].at[k - 1],
                    device_id=peer, device_id_type=pl.DeviceIdType.MESH).start()
        for i in range(n):
            src, dst, _ = _xchg_ends(src_refs[i], land_refs[i], axes[i], gather, me, me)
            pltpu.make_async_copy(src, dst, send[i].at[N_DEV - 1]).start()
        token[...] = jnp.zeros_like(token)

    bufs = list(srcs) + list(lands)
    outs = pl.pallas_call(
        body, name=name,
        out_shape=[pltpu.SemaphoreType.DMA((N_DEV,))] * (2 * n) + [pltpu.HBM(b.shape, b.dtype) for b in bufs]
        + [jax.ShapeDtypeStruct((8, LANES), F32)],
        in_specs=[_HBM] * (2 * n) + [pl.BlockSpec(memory_space=pl.ANY)] * nd,
        out_specs=[_SEM] * (2 * n) + [_HBM] * (2 * n) + [pl.BlockSpec(memory_space=pltpu.VMEM)],
        input_output_aliases={i: 2 * n + i for i in range(2 * n)},
        compiler_params=pltpu.CompilerParams(has_side_effects=_EFFECT),
    )(*[pltpu.with_memory_space_constraint(b, pltpu.HBM) for b in bufs], *deps)
    return dict(send=outs[:n], recv=outs[n:2 * n], srcs=outs[2 * n:3 * n], lands=outs[3 * n:4 * n], token=outs[-1],
                axes=list(axes), gather=gather)


SIBLING = 1
SAME_CORE = (2, 4, 6)


def _gather_relay(xc, after, *, name):
    axis = xc["axes"][0]

    def body(src_ref, land_ref, send1, recv1, after_ref, src_out, land_out, send2, recv2):
        x, y, c, me = _mesh_place()
        sibling, _ = _peer(x, y, c, SIBLING)
        for j, k in enumerate(SAME_CORE):
            peer, pidx = _peer(x, y, c, k)
            src, _, got = _xchg_ends(src_ref, land_ref, axis, True, me, pidx)
            pltpu.make_async_remote_copy(
                src_ref=src, dst_ref=got, send_sem=send1.at[k - 1], recv_sem=recv1.at[k - 1],
                device_id=peer, device_id_type=pl.DeviceIdType.MESH).wait_recv()
            pltpu.make_async_remote_copy(
                src_ref=got, dst_ref=got, send_sem=send2.at[j], recv_sem=recv2.at[j],
                device_id=sibling, device_id_type=pl.DeviceIdType.MESH).start()

    src, land = xc["srcs"][0], xc["lands"][0]
    outs = pl.pallas_call(
        body, name=name,
        out_shape=[pltpu.HBM(src.shape, src.dtype), pltpu.HBM(land.shape, land.dtype)]
        + [pltpu.SemaphoreType.DMA((len(SAME_CORE),))] * 2,
        in_specs=[_HBM, _HBM, _SEM, _SEM, pl.BlockSpec(memory_space=pl.ANY)],
        out_specs=[_HBM, _HBM, _SEM, _SEM], input_output_aliases={0: 0, 1: 1},
        compiler_params=pltpu.CompilerParams(has_side_effects=_EFFECT),
    )(src, land, xc["send"][0], xc["recv"][0], after)
    return dict(src=outs[0], land=outs[1], send=outs[2], recv=outs[3])


def _gather_relay_wait(xc, relay, after, *, name):
    axis = xc["axes"][0]

    def body(src_ref, land_ref, send1, recv1, send2, recv2, after_ref, src_out, land_out):
        x, y, c, me = _mesh_place()
        sibling, sidx = _peer(x, y, c, SIBLING)
        for k in (SIBLING,) + SAME_CORE:
            peer, pidx = _peer(x, y, c, k)
            src, dst, got = _xchg_ends(src_ref, land_ref, axis, True, me, pidx)
            pltpu.make_async_remote_copy(
                src_ref=src, dst_ref=dst, send_sem=send1.at[k - 1], recv_sem=recv1.at[k - 1],
                device_id=peer, device_id_type=pl.DeviceIdType.MESH).wait_send()
        src, dst, got = _xchg_ends(src_ref, land_ref, axis, True, me, sidx)
        pltpu.make_async_remote_copy(
            src_ref=src, dst_ref=got, send_sem=send1.at[SIBLING - 1], recv_sem=recv1.at[SIBLING - 1],
            device_id=sibling, device_id_type=pl.DeviceIdType.MESH).wait_recv()
        src, dst, _ = _xchg_ends(src_ref, land_ref, axis, True, me, me)
        pltpu.make_async_copy(src, dst, send1.at[N_DEV - 1]).wait()
        for j, k in enumerate(SAME_CORE):
            _, pidx = _peer(x, y, c, k)
            _, qidx = _peer(x, y, c, k ^ SIBLING)
            _, _, sent = _xchg_ends(src_ref, land_ref, axis, True, me, pidx)
            _, _, got = _xchg_ends(src_ref, land_ref, axis, True, me, qidx)
            pltpu.make_async_remote_copy(
                src_ref=sent, dst_ref=sent, send_sem=send2.at[j], recv_sem=recv2.at[j],
                device_id=sibling, device_id_type=pl.DeviceIdType.MESH).wait_send()
            pltpu.make_async_remote_copy(
                src_ref=got, dst_ref=got, send_sem=send2.at[j], recv_sem=recv2.at[j],
                device_id=sibling, device_id_type=pl.DeviceIdType.MESH).wait_recv()

    outs = pl.pallas_call(
        body, name=name, out_shape=[pltpu.HBM(relay["src"].shape, relay["src"].dtype),
                                    pltpu.HBM(relay["land"].shape, relay["land"].dtype)],
        in_specs=[_HBM, _HBM, _SEM, _SEM, _SEM, _SEM, pl.BlockSpec(memory_space=pl.ANY)],
        out_specs=[_HBM, _HBM], input_output_aliases={0: 0, 1: 1},
        compiler_params=pltpu.CompilerParams(has_side_effects=_EFFECT),
    )(relay["src"], relay["land"], xc["send"][0], xc["recv"][0], relay["send"], relay["recv"], after)
    return outs[1]


def _xchg_wait(xc, items, after, *, name):
    m = len(items)
    gather = xc["gather"]
    axes = [xc["axes"][i] for i in items]

    def body(*refs):
        src_refs, land_refs = refs[:m], refs[m:2 * m]
        send, recv = refs[2 * m:3 * m], refs[3 * m:4 * m]
        x, y, c, me = _mesh_place()
        for k in range(1, N_DEV):
            peer, pidx = _peer(x, y, c, k)
            for j in range(m):
                src, dst, got = _xchg_ends(src_refs[j], land_refs[j], axes[j], gather, me, pidx)
                pltpu.make_async_remote_copy(
                    src_ref=src, dst_ref=dst, send_sem=send[j].at[k - 1], recv_sem=recv[j].at[k - 1],
                    device_id=peer, device_id_type=pl.DeviceIdType.MESH).wait_send()
                pltpu.make_async_remote_copy(
                    src_ref=src, dst_ref=got, send_sem=send[j].at[k - 1], recv_sem=recv[j].at[k - 1],
                    device_id=peer, device_id_type=pl.DeviceIdType.MESH).wait_recv()
        for j in range(m):
            src, dst, _ = _xchg_ends(src_refs[j], land_refs[j], axes[j], gather, me, me)
            pltpu.make_async_copy(src, dst, send[j].at[N_DEV - 1]).wait()

    bufs = [xc["srcs"][i] for i in items] + [xc["lands"][i] for i in items]
    sems = [xc["send"][i] for i in items] + [xc["recv"][i] for i in items]
    outs = pl.pallas_call(
        body, name=name, out_shape=[pltpu.HBM(b.shape, b.dtype) for b in bufs],
        in_specs=[_HBM] * (2 * m) + [_SEM] * (2 * m) + [pl.BlockSpec(memory_space=pl.ANY)],
        out_specs=[_HBM] * (2 * m), input_output_aliases={j: j for j in range(2 * m)},
        compiler_params=pltpu.CompilerParams(has_side_effects=_EFFECT),
    )(*bufs, *sems, after)
    return outs[m:]


def _column_pieces(group, group_cols, shard_cols):
    lo, hi = group * group_cols, (group + 1) * group_cols
    pieces = []
    for dev in range(N_DEV):
        a, b = max(lo, dev * shard_cols), min(hi, (dev + 1) * shard_cols)
        if a < b:
            pieces.append((dev, a - lo, b - a, a - dev * shard_cols))
    return pieces


def _device_place(dev):
    return (dev >> 2) & 1, (dev >> 1) & 1, dev & 1


def _scatter_cols_start(src, land, pieces, *, name):
    def body(src_ref, land_ref, send, recv, src_out, land_out, token):
        _, _, _, me = _mesh_place()
        for j, (dev, s_lo, width, d_lo) in enumerate(pieces):
            piece = src_ref.at[:, pl.ds(s_lo, width)]
            dst = land_ref.at[me, :, pl.ds(d_lo, width)]

            @pl.when(me != dev)
            def _(piece=piece, dst=dst, j=j, dev=dev):
                pltpu.make_async_remote_copy(
                    src_ref=piece, dst_ref=dst, send_sem=send.at[j], recv_sem=recv.at[me],
                    device_id=_device_place(dev), device_id_type=pl.DeviceIdType.MESH).start()

            @pl.when(me == dev)
            def _(piece=piece, dst=dst, j=j):
                pltpu.make_async_copy(piece, dst, send.at[j]).start()
        token[...] = jnp.zeros_like(token)

    outs = pl.pallas_call(
        body, name=name,
        out_shape=[pltpu.SemaphoreType.DMA((len(pieces),)), pltpu.SemaphoreType.DMA((N_DEV,)),
                   pltpu.HBM(src.shape, src.dtype), pltpu.HBM(land.shape, land.dtype),
                   jax.ShapeDtypeStruct((8, LANES), F32)],
        in_specs=[_HBM, _HBM], out_specs=[_SEM, _SEM, _HBM, _HBM, pl.BlockSpec(memory_space=pltpu.VMEM)],
        input_output_aliases={0: 2, 1: 3},
        compiler_params=pltpu.CompilerParams(has_side_effects=_EFFECT),
    )(pltpu.with_memory_space_constraint(src, pltpu.HBM), pltpu.with_memory_space_constraint(land, pltpu.HBM))
    return dict(send=outs[0], recv=outs[1], src=outs[2], land=outs[3], token=outs[4], pieces=pieces)


def _scatter_cols_wait(stages, land, after, *, name):
    n = len(stages)

    def body(*refs):
        srcs, land_ref = refs[:n], refs[n]
        sends, recvs = refs[n + 1:2 * n + 1], refs[2 * n + 1:3 * n + 1]
        _, _, _, me = _mesh_place()
        for g, stage in enumerate(stages):
            for j, (dev, s_lo, width, d_lo) in enumerate(stage["pieces"]):
                piece = srcs[g].at[:, pl.ds(s_lo, width)]
                dst = land_ref.at[me, :, pl.ds(d_lo, width)]

                @pl.when(me != dev)
                def _(piece=piece, dst=dst, g=g, j=j, dev=dev):
                    pltpu.make_async_remote_copy(
                        src_ref=piece, dst_ref=dst, send_sem=sends[g].at[j], recv_sem=recvs[g].at[me],
                        device_id=_device_place(dev), device_id_type=pl.DeviceIdType.MESH).wait_send()

                @pl.when(me == dev)
                def _(piece=piece, dst=dst, g=g, j=j, dev=dev, d_lo=d_lo, width=width):
                    pltpu.make_async_copy(piece, dst, sends[g].at[j]).wait()
                    for sender in range(N_DEV):
                        if sender != dev:
                            pltpu.make_async_remote_copy(
                                src_ref=piece, dst_ref=land_ref.at[sender, :, pl.ds(d_lo, width)],
                                send_sem=sends[g].at[j], recv_sem=recvs[g].at[sender],
                                device_id=_device_place(sender), device_id_type=pl.DeviceIdType.MESH).wait_recv()

    bufs = [st["src"] for st in stages] + [land]
    outs = pl.pallas_call(
        body, name=name, out_shape=[pltpu.HBM(b.shape, b.dtype) for b in bufs],
        in_specs=[_HBM] * (n + 1) + [_SEM] * (2 * n) + [pl.BlockSpec(memory_space=pl.ANY)],
        out_specs=[_HBM] * (n + 1), input_output_aliases={i: i for i in range(n + 1)},
        compiler_params=pltpu.CompilerParams(has_side_effects=_EFFECT),
    )(*bufs, *[st["send"] for st in stages], *[st["recv"] for st in stages], after)
    return outs[n]


def _cast_bf16(a, *, name):
    r, c = a.shape
    tr = min(r, 512)

    def body(a_ref, o_ref):
        o_ref[...] = a_ref[...].astype(BF16)

    spec = pl.BlockSpec((tr, c), lambda i: (i, 0))
    return pl.pallas_call(body, name=name, grid=(r // tr,), in_specs=[spec], out_specs=spec,
                          out_shape=jax.ShapeDtypeStruct((r, c), BF16), compiler_params=_params(("parallel",)))(a)


def _adamw(slabs, w, m, v, *, name):
    layers, r, c = w.shape
    tr = min(r, 256)

    def body(*refs):
        s_refs = refs[:layers]
        w_ref, m_ref, v_ref, g_ref, d_ref, mo_ref, vo_ref = refs[layers:]
        for l in range(layers):
            g = s_refs[l][0].astype(F32)
            for i in range(1, N_DEV):
                g = g + s_refs[l][i].astype(F32)
            m2 = ADAM_B1 * m_ref[l] + (1.0 - ADAM_B1) * g
            v2 = ADAM_B2 * v_ref[l] + (1.0 - ADAM_B2) * (g * g)
            m_hat = m2 / (1.0 - ADAM_B1 ** ADAM_STEP)
            v_hat = v2 / (1.0 - ADAM_B2 ** ADAM_STEP)
            g_ref[l] = g
            d_ref[l] = -ADAM_LR * (m_hat / (jnp.sqrt(v_hat) + ADAM_EPS) + ADAM_WD * w_ref[l])
            mo_ref[l] = m2
            vo_ref[l] = v2

    spec = pl.BlockSpec((layers, tr, c), lambda i: (0, i, 0))
    return pl.pallas_call(
        body, name=name, grid=(r // tr,),
        in_specs=[pl.BlockSpec((N_DEV, tr, c), lambda i: (0, i, 0))] * layers + [spec, spec, spec],
        out_specs=[spec] * 4, out_shape=[jax.ShapeDtypeStruct((layers, r, c), F32)] * 4,
        compiler_params=_params(("parallel",)),
    )(*slabs, w, m, v)


GATHER_AXIS = {"attn_w_in": 1, "attn_w_out": 0, "ffn_w_up0": 1, "ffn_w_down0": 0, "hgrn_w_in": 1, "hgrn_w_out": 0,
               "hgrn_norm_g": 1, "ffn_w_up1": 1, "ffn_w_down1": 0}
GATHER_STAGES = (("attn_w_in",), ("attn_w_out", "ffn_w_up0", "ffn_w_down0", "hgrn_w_in", "hgrn_w_out", "hgrn_norm_g"),
                 ("ffn_w_up1", "ffn_w_down1"))
GATHER_WAITS = ((("attn_w_in",), 0, 1), (("attn_w_out", "ffn_w_up0", "ffn_w_down0"), 1, 2),
                (("hgrn_w_in", "hgrn_w_out", "hgrn_norm_g"), 1, None), (("ffn_w_up1", "ffn_w_down1"), 2, None))
SCATTER_AXIS = dict(GATHER_AXIS, small=None)
BIG = ("attn_w_in", "attn_w_out", "hgrn_w_in", "hgrn_w_out", "ffn_w_up", "ffn_w_down")
SMALL = ("lb_logits", "ln_mix_g", "ln_mix_b", "ln_ffn_g", "ln_ffn_b")
SMALL_ROW = {"ln_mix_g": 0, "ln_mix_b": 2, "ln_ffn_g": 4, "ln_ffn_b": 6, "lb_logits": 8}
NORM_G_ROW = 10
LOSS_ROW = 11


def kernel(x, attn_w_in, attn_w_out, hgrn_w_in, hgrn_w_out, hgrn_norm_g, lb_logits, ln_mix_g, ln_mix_b, ln_ffn_g, ln_ffn_b, ffn_w_up, ffn_w_down, loss_target, m_attn_w_in, m_attn_w_out, m_hgrn_w_in, m_hgrn_w_out, m_hgrn_norm_g, m_lb_logits, m_ln_mix_g, m_ln_mix_b, m_ln_ffn_g, m_ln_ffn_b, m_ffn_w_up, m_ffn_w_down, v_attn_w_in, v_attn_w_out, v_hgrn_w_in, v_hgrn_w_out, v_hgrn_norm_g, v_lb_logits, v_ln_mix_g, v_ln_mix_b, v_ln_ffn_g, v_ln_ffn_b, v_ffn_w_up, v_ffn_w_down):
    wts = dict(attn_w_in=attn_w_in, attn_w_out=attn_w_out, hgrn_w_in=hgrn_w_in, hgrn_w_out=hgrn_w_out,
               hgrn_norm_g=hgrn_norm_g, lb_logits=lb_logits, ln_mix_g=ln_mix_g, ln_mix_b=ln_mix_b, ln_ffn_g=ln_ffn_g,
               ln_ffn_b=ln_ffn_b, ffn_w_up=ffn_w_up, ffn_w_down=ffn_w_down)
    mom = dict(attn_w_in=m_attn_w_in, attn_w_out=m_attn_w_out, hgrn_w_in=m_hgrn_w_in, hgrn_w_out=m_hgrn_w_out,
               hgrn_norm_g=m_hgrn_norm_g, lb_logits=m_lb_logits, ln_mix_g=m_ln_mix_g, ln_mix_b=m_ln_mix_b,
               ln_ffn_g=m_ln_ffn_g, ln_ffn_b=m_ln_ffn_b, ffn_w_up=m_ffn_w_up, ffn_w_down=m_ffn_w_down)
    vel = dict(attn_w_in=v_attn_w_in, attn_w_out=v_attn_w_out, hgrn_w_in=v_hgrn_w_in, hgrn_w_out=v_hgrn_w_out,
               hgrn_norm_g=v_hgrn_norm_g, lb_logits=v_lb_logits, ln_mix_g=v_ln_mix_g, ln_mix_b=v_ln_mix_b,
               ln_ffn_g=v_ln_ffn_g, ln_ffn_b=v_ln_ffn_b, ffn_w_up=v_ffn_w_up, ffn_w_down=v_ffn_w_down)
    me = 4 * lax.axis_index("x") + 2 * lax.axis_index("y") + lax.axis_index("c")

    src = {"attn_w_in": attn_w_in[0], "attn_w_out": attn_w_out[0], "hgrn_w_in": hgrn_w_in[0], "hgrn_w_out": hgrn_w_out[0],
           "ffn_w_up0": ffn_w_up[0], "ffn_w_down0": ffn_w_down[0], "ffn_w_up1": ffn_w_up[1], "ffn_w_down1": ffn_w_down[1]}
    gathers, got = {}, {}

    def start_gather(stage, deps):
        shards, lands = [], []
        for nm in GATHER_STAGES[stage]:
            sh = hgrn_norm_g if nm == "hgrn_norm_g" else _cast_bf16(src[nm], name=f"cast_{nm}")
            ax = GATHER_AXIS[nm]
            shape = list(sh.shape)
            shape[ax] *= N_DEV
            shards.append(sh)
            lands.append(lax.empty(tuple(shape), sh.dtype))
        peers = (SIBLING,) + SAME_CORE if stage == 0 else tuple(range(1, N_DEV))
        gathers[stage] = _xchg_start(shards, lands, [GATHER_AXIS[nm] for nm in GATHER_STAGES[stage]], gather=True,
                                     name=f"gather_start_{stage}", deps=deps, peers=peers)
        return [gathers[stage]["token"]]

    def get_w(name, after):
        deps = []
        if name not in got:
            group, stage, then = [w for w in GATHER_WAITS if name in w[0]][0]
            xc = gathers[stage]
            if stage == 0:
                relay = _gather_relay(xc, xc["token"] if after is None else after, name="gather_relay")
                res = [_gather_relay_wait(xc, relay, xc["token"], name=f"gather_wait_{group[0]}")]
            else:
                res = _xchg_wait(xc, [GATHER_STAGES[stage].index(nm) for nm in group], after,
                                 name=f"gather_wait_{group[0]}")
            got.update(zip(group, res))
            if then is not None:
                deps = start_gather(then, [res[0]])
        return got[name], deps

    first = start_gather(0, [])

    scattered = {}

    col_stages = []

    def on_grads(tag, grads):
        gnames = list(grads)
        if gnames[0].startswith("attn_w_in@"):
            group = int(gnames[0].split("@")[1])
            part = grads[gnames[0]]
            shard_cols = attn_w_in.shape[-1]
            land = col_stages[-1]["land"] if col_stages else lax.empty((N_DEV, part.shape[0], shard_cols), part.dtype)
            col_stages.append(_scatter_cols_start(part, land, _column_pieces(group, part.shape[1], shard_cols),
                                                  name=f"scatter_start_{tag}"))
            return [col_stages[-1]["token"]]
        axes = [SCATTER_AXIS[nm] for nm in gnames]
        stacks = []
        for nm, ax in zip(gnames, axes):
            shape = list(grads[nm].shape)
            if ax is not None:
                shape[ax] //= N_DEV
            stacks.append(lax.empty((N_DEV, *shape), grads[nm].dtype))
        scattered[tag] = (gnames, _xchg_start([grads[nm] for nm in gnames], stacks, axes, gather=False,
                                              name=f"scatter_start_{tag}"))
        return [scattered[tag][1]["token"]]

    sm = jax.nn.softmax(lb_logits, axis=0)
    csum = jnp.cumsum(sm, axis=0)
    small = dict(lb=(csum - csum[0:1])[1:2], lb_logits=lb_logits, ln_mix_g=ln_mix_g, ln_mix_b=ln_mix_b,
                 ln_ffn_g=ln_ffn_g, ln_ffn_b=ln_ffn_b)
    grad_x = _local_step(x[0], loss_target[0], get_w, small, on_grads, deps=first)
    out = {}

    def stack_small(src_):
        rows = [None] * SMALL_ROWS
        for name in SMALL:
            rows[SMALL_ROW[name]], rows[SMALL_ROW[name] + 1] = src_[name][0:1], src_[name][1:2]
        zero = jnp.zeros((1, x.shape[-1]), F32)
        return jnp.concatenate([zero if r is None else r for r in rows], axis=0)[None]

    def update(name, slabs):
        shape = wts[name].shape
        out[name] = [r.reshape(shape) for r in _adamw(slabs, wts[name], mom[name], vel[name], name=f"adamw_{name}")]
        return out[name][0]

    slabs, after = {}, grad_x
    for tag, (gnames, xc) in scattered.items():
        slabs.update(zip(gnames, _xchg_wait(xc, list(range(len(gnames))), after, name=f"scatter_wait_{tag}")))
        if tag == "ffn1":
            continue
        if tag == "ffn0":
            update("ffn_w_down", [slabs["ffn_w_down0"], slabs["ffn_w_down1"]])
            after = update("ffn_w_up", [slabs["ffn_w_up0"], slabs["ffn_w_up1"]])
        elif tag == "hgrn":
            update("hgrn_w_out", [slabs["hgrn_w_out"]])
            after = update("hgrn_w_in", [slabs["hgrn_w_in"]])
        else:
            after = update("attn_w_out", [slabs["attn_w_out"]])
    update("attn_w_in", [_scatter_cols_wait(col_stages, col_stages[-1]["land"], after, name="scatter_wait_attn_in")])
    res = _adamw([slabs["small"]], stack_small(wts), stack_small(mom), stack_small(vel), name="adamw_small")
    for name in SMALL:
        out[name] = [r[0, SMALL_ROW[name]:SMALL_ROW[name] + 2] for r in res]
    loss = res[0][0, LOSS_ROW, 0]
    ng = hgrn_norm_g.shape[-1]
    ng_slabs = lax.dynamic_slice(slabs["small"], (0, NORM_G_ROW, me * ng), (N_DEV, 1, ng))
    out["hgrn_norm_g"] = [r[0] for r in _adamw([ng_slabs], hgrn_norm_g[None], m_hgrn_norm_g[None],
                                                v_hgrn_norm_g[None], name="adamw_norm_g")]
    order =("attn_w_in", "attn_w_out", "hgrn_w_in", "hgrn_w_out", "hgrn_norm_g", "lb_logits", "ln_mix_g", "ln_mix_b",
             "ln_ffn_g", "ln_ffn_b", "ffn_w_up", "ffn_w_down")
    return (loss, grad_x[None], *[out[nm][0] for nm in order], *[out[nm][1] for nm in order],
            *[out[nm][2] for nm in order], *[out[nm][3] for nm in order])
```

```python
import jax
import jax.numpy as jnp
from jax import lax
from jax.experimental import pallas as pl
from jax.experimental.pallas import tpu as pltpu

F32 = jnp.float32
BF16 = jnp.bfloat16

N_DEV = 8
LANES = 128
D_MODEL = 1024
ATTN_HEAD_DIM = 64
ATTN_HEADS = 16
ATTN_SCALE = ATTN_HEAD_DIM ** -0.5
ATTN_BLK = 128
DILATIONS = (1, 4, 16)
ROPE_THETA = 10000.0
HGRN_HEADS = 8
HGRN_CHUNK = 64
D_FF = 4096
LN_EPS = 1e-5
RMS_EPS = 1e-6
DEPTH = 2
ALPHA = (2 * DEPTH) ** 0.25
ADAM_LR, ADAM_B1, ADAM_B2, ADAM_EPS, ADAM_WD, ADAM_STEP = 0.001, 0.9, 0.999, 1e-08, 0.01, 10
VMEM_LIMIT = 48 * 1024 * 1024

_NT = (((1,), (1,)), ((), ()))
_TN = (((0,), (0,)), ((), ()))


def _dot(a, b):
    return jnp.dot(a, b, preferred_element_type=F32)


def _dot_nt(a, b):
    return lax.dot_general(a, b, _NT, preferred_element_type=F32)


def _dot_tn(a, b):
    return lax.dot_general(a, b, _TN, preferred_element_type=F32)


def _split3(x):
    p1 = x.astype(BF16)
    r1 = x - p1.astype(F32)
    p2 = r1.astype(BF16)
    p3 = (r1 - p2.astype(F32)).astype(BF16)
    return p1, p2, p3


def _exact_dot(sel3, x):
    return _dot(sel3, jnp.concatenate(_split3(x), axis=0))


def _exact_dot_r(x, sel3):
    return _dot(jnp.concatenate(_split3(x), axis=1), sel3)


def _params(sem=None):
    return pltpu.CompilerParams(dimension_semantics=sem, vmem_limit_bytes=VMEM_LIMIT)


def _mm(a, b, mode, *, name, m, n, k, tm=1024, tn=1024, tk=1024, out_dtypes=(F32,), epi=None, a_pre=None,
        tile_extras=(), row_extras=(), vec_extras=(), row_outs=(), vec_outs=0, a_split=1, b_split=1, out_split=1,
        b_col_off=0, b_k_off=0, out_col_off=0, out_cols=None, alias=None, epi_wants_j=False, deps=(), t_out=False):
    tm, tn, tk = min(tm, m), min(tn, n), min(tk, k)
    assert m % tm == 0 and n % tn == 0 and k % tk == 0, (name, m, n, k, tm, tn, tk)
    gm, gn, gk = m // tm, n // tn, k // tk
    if mode in ("nn", "nt"):
        if a_split > 1 and tk == k:
            a_spec = pl.BlockSpec((a_split, tm, k // a_split), lambda i, j, kk: (0, i, 0))
        elif a_split > 1:
            kc = (k // a_split) // tk
            a_spec = pl.BlockSpec((None, tm, tk), lambda i, j, kk: (kk // kc, i, kk % kc))
        else:
            a_spec = pl.BlockSpec((tm, tk), lambda i, j, kk: (i, kk))
    else:
        a_spec = pl.BlockSpec((tk, tm), lambda i, j, kk: (kk, i))
    if mode in ("nn", "tn"):
        if b_split > 1:
            nc = (n // b_split) // tn
            b_spec = pl.BlockSpec((None, tk, tn), lambda i, j, kk: (j // nc, kk, j % nc))
        else:
            b_spec = pl.BlockSpec((tk, tn), lambda i, j, kk: (kk + b_k_off, j + b_col_off))
    else:
        b_spec = pl.BlockSpec((tn, tk), lambda i, j, kk: (j + b_col_off, kk + b_k_off))
    if out_split > 1:
        nco = (n // out_split) // tn
        o_spec = pl.BlockSpec((None, tm, tn), lambda i, j, kk: (j // nco, i, j % nco))
        o_shape = (out_split, m, n // out_split)
    else:
        o_spec = pl.BlockSpec((tm, tn), lambda i, j, kk: (i, j + out_col_off))
        o_shape = (m, out_cols if out_cols is not None else n)
    n_ex = len(tile_extras) + len(row_extras) + len(vec_extras)
    n_out = len(out_dtypes)
    n_plain = n_out + len(row_outs)
    assert not vec_outs or gn == 1
    if epi is None:
        def epi(acc):
            return (acc,)
    dot = {"nn": _dot, "nt": _dot_nt, "tn": _dot_tn}[mode]

    def body(*refs):
        a_ref, b_ref = refs[0], refs[1]
        ex = refs[2:2 + n_ex]
        outs = refs[2 + n_ex + (1 if alias is not None else 0) + len(deps):][:n_plain + vec_outs + (1 if t_out else 0)]
        ii = pl.program_id(0)
        jj = pl.program_id(1)

        def product():
            if a_split > 1 and tk == k:
                av = jnp.concatenate([a_ref[p] for p in range(a_split)], axis=1)
            else:
                av = a_ref[...]
            if a_pre is not None:
                av = a_pre(av)
            return dot(av.astype(BF16), b_ref[...].astype(BF16))

        def finish(total):
            lead = (jj,) if epi_wants_j else ()
            res = epi(*lead, total, *[e[...] for e in ex])
            for o, r in zip(outs[:n_plain], res):
                o[...] = r.astype(o.dtype)
            for o, r in zip(outs[n_plain:n_plain + vec_outs], res[n_plain:]):
                @pl.when(ii == 0)
                def _(o=o, r=r):
                    o[...] = r

                @pl.when(ii > 0)
                def _(o=o, r=r):
                    o[...] += r
            if t_out:
                outs[-1][...] = res[-1].astype(outs[-1].dtype)

        if gk == 1:
            finish(product())
        else:
            acc = refs[-1]
            kk = pl.program_id(2)

            @pl.when(kk == 0)
            def _():
                acc[...] = product()

            @pl.when(kk > 0)
            def _():
                acc[...] += product()

            @pl.when(kk == gk - 1)
            def _():
                finish(acc[...])

    in_specs = [a_spec, b_spec] + [o_spec] * len(tile_extras)
    in_specs += [pl.BlockSpec((tm, r.shape[1]), lambda i, j, kk: (i, 0)) if r.shape[0] == m else
                 pl.BlockSpec((r.shape[0], tm), lambda i, j, kk: (0, i)) for r in row_extras]
    in_specs += [pl.BlockSpec((1, tn), lambda i, j, kk: (0, j))] * len(vec_extras)
    args = [a, b] + list(tile_extras) + list(row_extras) + list(vec_extras)
    io_alias = {}
    if alias is not None:
        in_specs.append(pl.BlockSpec(memory_space=pl.ANY))
        args.append(alias)
        io_alias = {len(args) - 1: 0}
    in_specs += [pl.BlockSpec(memory_space=pl.ANY)] * len(deps)
    args += list(deps)
    out_specs = [o_spec] * n_out
    out_shape = [jax.ShapeDtypeStruct(o_shape, dt) for dt in out_dtypes]
    for dt, w in row_outs:
        out_specs.append(pl.BlockSpec((tm, w), lambda i, j, kk: (i, 0)))
        out_shape.append(jax.ShapeDtypeStruct((m, w), dt))
    out_specs += [pl.BlockSpec((1, tn), lambda i, j, kk: (0, j))] * vec_outs
    out_shape += [jax.ShapeDtypeStruct((1, n), F32)] * vec_outs
    if t_out:
        assert out_split > 1
        out_specs.append(pl.BlockSpec((None, tn, tm), lambda i, j, kk: (j // nco, j % nco, i)))
        out_shape.append(jax.ShapeDtypeStruct((out_split, n // out_split, m), out_dtypes[0]))
    out = pl.pallas_call(
        body, name=name, grid=(gm, gn, gk), in_specs=in_specs, out_specs=out_specs, out_shape=out_shape,
        scratch_shapes=[pltpu.VMEM((tm, tn), F32)] if gk > 1 else [],
        input_output_aliases=io_alias,
        compiler_params=_params(("arbitrary" if vec_outs else "parallel", "parallel", "arbitrary")),
    )(*args)
    return out[0] if len(out) == 1 else out


def _rope_tables(seq, dil):
    pos = jnp.arange(seq, dtype=jnp.int32).reshape(seq // dil, dil).T.reshape(seq)
    half = ATTN_HEAD_DIM // 2
    inv = ROPE_THETA ** (-jnp.arange(half, dtype=F32) * (2.0 / ATTN_HEAD_DIM))
    ang = pos.astype(F32)[:, None] * inv[None, :]
    cos, sin = jnp.cos(ang), jnp.sin(ang)
    reps = LANES // ATTN_HEAD_DIM
    return (jnp.tile(jnp.concatenate([cos, cos], axis=1), (1, reps)),
            jnp.tile(jnp.concatenate([-sin, sin], axis=1), (1, reps)))


def _rotate(x, c, ss, sign=1.0):
    w = x.shape[-1]
    half = ATTN_HEAD_DIM // 2
    lane = lax.broadcasted_iota(jnp.int32, x.shape, 1)
    first = (lane % ATTN_HEAD_DIM) < half
    partner = jnp.where(first, pltpu.roll(x, w - half, 1), pltpu.roll(x, half, 1))
    reps = w // LANES
    if reps > 1:
        c = jnp.concatenate([c] * reps, axis=1)
        ss = jnp.concatenate([ss] * reps, axis=1)
    return x * c + sign * (partner * ss)


def _rotate_t(xt, ct, sst):
    half = ATTN_HEAD_DIM // 2
    heads = xt.shape[0] // ATTN_HEAD_DIM
    parts = []
    for h in range(heads):
        lo = h * ATTN_HEAD_DIM
        parts += [xt[lo + half:lo + 2 * half], xt[lo:lo + half]]
    return (xt * jnp.concatenate([ct] * heads, axis=0)
            + jnp.concatenate(parts, axis=0) * jnp.concatenate([sst] * heads, axis=0))


def _ln_epi(acc, x, g, b, *prev):
    if prev:
        x = x * prev[0] + prev[1]
    u = ALPHA * x + acc
    mu = jnp.mean(u, axis=-1, keepdims=True)
    uc = u - mu
    var = jnp.mean(uc * uc, axis=-1, keepdims=True)
    rstd = lax.rsqrt(var + LN_EPS)
    xh = uc * rstd
    return xh * g + b, xh, rstd


def _ln_grad(dy, xh, rstd, g):
    dxh = dy * g
    m1 = jnp.mean(dxh, axis=-1, keepdims=True)
    m2 = jnp.mean(dxh * xh, axis=-1, keepdims=True)
    du = rstd * (dxh - m1 - xh * m2)
    return du, du, jnp.sum(dy * xh, axis=0, keepdims=True), jnp.sum(dy, axis=0, keepdims=True)


def _ln_grad_epi(acc, du_next, xh, rstd, g):
    return _ln_grad(acc + ALPHA * du_next, xh, rstd, g)


def _ln_loss_epi(acc, x_in, target, g, b, *prev):
    out, xh, rstd = _ln_epi(acc, x_in, g, b, *prev)
    e = out - target
    du, _, dg, db = _ln_grad(e * (1.0 / e.shape[-1]), xh, rstd, g)
    return du, du, dg, db, jnp.sum(e * e, axis=0, keepdims=True)


POS_BLK = 2048


def _class_rows(r, dil):
    return pl.ds(r, POS_BLK // dil, stride=dil) if dil > 1 else pl.ds(0, POS_BLK)


def _class_view(a, dil):
    s, w = a.shape
    return a.reshape(dil, s // dil, w)


def _class_spec(dil, all_tiles=True):
    if all_tiles:
        return pl.BlockSpec((dil, POS_BLK // dil, LANES), lambda i, t: (0, i, t))
    return pl.BlockSpec((dil, POS_BLK // dil, LANES), lambda i, t: (0, i, 0))


def _pos_spec(all_tiles=True):
    if all_tiles:
        return pl.BlockSpec((POS_BLK, LANES), lambda i, t: (i, t))
    return pl.BlockSpec((POS_BLK, LANES), lambda i, t: (i, 0))


def _prep_x(x, deps=()):
    s, d = x.shape

    def body(x_ref, *refs):
        outs = refs[len(deps):]
        for dil, o_ref in zip(DILATIONS, outs):
            for r in range(dil):
                o_ref[r] = x_ref[_class_rows(r, dil), :].astype(BF16)

    outs = pl.pallas_call(
        body, name="prep_x", grid=(s // POS_BLK, d // LANES),
        in_specs=[_pos_spec()] + [pl.BlockSpec(memory_space=pl.ANY)] * len(deps),
        out_specs=[_class_spec(dil) for dil in DILATIONS],
        out_shape=[jax.ShapeDtypeStruct((dil, s // dil, d), BF16) for dil in DILATIONS],
        compiler_params=_params(("parallel", "parallel")),
    )(x, *deps)
    return [o.reshape(s, d) for o in outs]


def _head_expand_matrix():
    h = lax.broadcasted_iota(jnp.int32, (LANES, D_MODEL), 0)
    l = lax.broadcasted_iota(jnp.int32, (LANES, D_MODEL), 1)
    return (l // ATTN_HEAD_DIM == h).astype(BF16)


def _attn_fwd(qkv, dil, *, name):
    _, s, d = qkv.shape
    nq = s // ATTN_BLK
    per = nq // dil
    tiles = d // LANES

    def body(q_ref, kc_ref, kp_ref, vc_ref, vp_ref, o_ref, lse_ref):
        qb = pl.program_id(0)
        first = (qb % per) == 0
        qi = lax.broadcasted_iota(jnp.int32, (ATTN_BLK, 2 * ATTN_BLK), 0)
        kj = lax.broadcasted_iota(jnp.int32, (ATTN_BLK, 2 * ATTN_BLK), 1)
        dist = qi + ATTN_BLK - kj
        valid = (dist >= 0) & (dist <= ATTN_BLK) & ((kj >= ATTN_BLK) | jnp.logical_not(first))
        lane = lax.broadcasted_iota(jnp.int32, (ATTN_BLK, LANES), 1)
        lse_tile = jnp.zeros((ATTN_BLK, LANES), F32)
        zero = jnp.zeros((), BF16)
        in_head = [(lane // ATTN_HEAD_DIM) == hh for hh in range(2)]
        for t0 in range(0, tiles, ATTN_FWD_TILE_GROUP):
            group = range(t0, t0 + ATTN_FWD_TILE_GROUP)
            heads = [(t, hh) for t in group for hh in range(2)]
            cols = {t: pl.ds(t * LANES, LANES) for t in group}
            k2 = {t: jnp.concatenate([kp_ref[:, cols[t]], kc_ref[:, cols[t]]], axis=0) for t in group}
            v2 = {t: jnp.concatenate([vp_ref[:, cols[t]], vc_ref[:, cols[t]]], axis=0) for t in group}
            sc = {(t, hh): jnp.where(valid, _dot_nt(jnp.where(in_head[hh], q_ref[:, cols[t]], zero), k2[t]),
                                     -jnp.inf) for t, hh in heads}
            mx = {i: jnp.max(sc[i], axis=-1, keepdims=True) for i in heads}
            p = {i: jnp.exp(sc[i] - mx[i]) for i in heads}
            l = {i: jnp.sum(p[i], axis=-1, keepdims=True) for i in heads}
            oh = {i: _dot(p[i].astype(BF16), v2[i[0]]) / l[i] for i in heads}
            for t in group:
                o_ref[:, cols[t]] = jnp.where(in_head[0], oh[t, 0], oh[t, 1])
                for hh in range(2):
                    lse_tile = jnp.where(lane == 2 * t + hh, mx[t, hh] + jnp.log(l[t, hh]), lse_tile)
        lse_ref[...] = lse_tile

    def blk(piece, prev):
        if prev:
            return pl.BlockSpec((None, ATTN_BLK, d), lambda i: (piece, jnp.maximum(i - 1, 0), 0))
        return pl.BlockSpec((None, ATTN_BLK, d), lambda i: (piece, i, 0))

    return pl.pallas_call(
        body, name=name, grid=(nq,),
        in_specs=[blk(0, False), blk(1, False), blk(1, True), blk(2, False), blk(2, True)],
        out_specs=[pl.BlockSpec((ATTN_BLK, d), lambda i: (i, 0)), pl.BlockSpec((ATTN_BLK, LANES), lambda i: (i, 0))],
        out_shape=[jax.ShapeDtypeStruct((s, d), F32), jax.ShapeDtypeStruct((s, LANES), F32)],
        compiler_params=_params(("parallel",)),
    )(qkv, qkv, qkv, qkv, qkv)


def _attn_combine(os_, lses):
    s, d = os_[0].shape
    sel = jnp.concatenate([_head_expand_matrix()] * 3, axis=0)

    def body(o0, o1, o2, l0, l1, l2, sel_ref, of_ref, ob_ref, lt_ref, o_pos, l_pos):
        for g, (dil, o_ref, l_ref) in enumerate(zip(DILATIONS, (o0, o1, o2), (l0, l1, l2))):
            for r in range(dil):
                o_pos[g, _class_rows(r, dil), :] = o_ref[r]
                l_pos[g, _class_rows(r, dil), :] = l_ref[r]
        la, lb_, lc = l_pos[0], l_pos[1], l_pos[2]
        mx = jnp.maximum(jnp.maximum(la, lb_), lc)
        es = (jnp.exp(la - mx), jnp.exp(lb_ - mx), jnp.exp(lc - mx))
        z = es[0] + es[1] + es[2]
        lt_ref[...] = mx + jnp.log(z)
        acc = jnp.zeros((POS_BLK, LANES), F32)
        for g in range(3):
            acc += _exact_dot_r(es[g] / z, sel_ref[...]) * o_pos[g]
        of_ref[...] = acc
        ob_ref[...] = acc.astype(BF16)

    return pl.pallas_call(
        body, name="attn_combine", grid=(s // POS_BLK, d // LANES),
        in_specs=[_class_spec(dil) for dil in DILATIONS] + [_class_spec(dil, False) for dil in DILATIONS]
        + [pl.BlockSpec((3 * LANES, LANES), lambda i, t: (0, t))],
        out_specs=[_pos_spec(), _pos_spec(), _pos_spec(False)],
        out_shape=[jax.ShapeDtypeStruct((s, d), F32), jax.ShapeDtypeStruct((s, d), BF16),
                   jax.ShapeDtypeStruct((s, LANES), F32)],
        scratch_shapes=[pltpu.VMEM((3, POS_BLK, LANES), F32), pltpu.VMEM((3, POS_BLK, LANES), F32)],
        compiler_params=_params(("parallel", "arbitrary")),
    )(*[_class_view(o, dil) for o, dil in zip(os_, DILATIONS)],
      *[_class_view(l, dil) for l, dil in zip(lses, DILATIONS)], sel)


ATTN_TILE_GROUP = 2
ATTN_FWD_TILE_GROUP = 4
AUX_PER_TILE = 12


def _aux_placement():
    h = lax.broadcasted_iota(jnp.int32, (6, LANES, LANES), 1)
    l = lax.broadcasted_iota(jnp.int32, (6, LANES, LANES), 2)
    j = lax.broadcasted_iota(jnp.int32, (6, LANES, LANES), 0)
    target = AUX_PER_TILE * (h // 2) + 3 * (h % 2) + jnp.where(j < 3, j, 3 + j)
    return ((l == target) & (h < ATTN_HEADS)).astype(BF16)


def _attn_bwd_prep(do, o, lse):
    s, d = do.shape
    tiles = d // LANES
    sel_t = jnp.tile(_head_expand_matrix().T.reshape(tiles, LANES, LANES), (1, 3, 1))

    def body(do_ref, o_ref, l_ref, sel_ref, place_ref, *refs):
        outs, delta, aux = refs[:9], refs[9], refs[10]
        t = pl.program_id(1)
        part = _exact_dot_r(do_ref[...] * o_ref[...], sel_ref[...])

        @pl.when(t == 0)
        def _():
            delta[...] = part

        @pl.when(t > 0)
        def _():
            delta[...] += part

        for g, dil in enumerate(DILATIONS):
            for r in range(dil):
                blk = do_ref[_class_rows(r, dil), :].astype(BF16)
                outs[g][r] = blk
                outs[3 + g][r] = blk.T

        @pl.when(t == tiles - 1)
        def _():
            pieces = _split3(l_ref[...]) + _split3(delta[...])
            aux[...] = _dot(jnp.concatenate(pieces, axis=1), place_ref[...])
            for g, dil in enumerate(DILATIONS):
                for r in range(dil):
                    outs[6 + g][r] = aux[_class_rows(r, dil), :].astype(BF16)

    outs = pl.pallas_call(
        body, name="attn_bwd_prep", grid=(s // POS_BLK, tiles),
        in_specs=[_pos_spec(), _pos_spec(), _pos_spec(False),
                  pl.BlockSpec((None, 3 * LANES, LANES), lambda i, t: (t, 0, 0)),
                  pl.BlockSpec((6 * LANES, LANES), lambda i, t: (0, 0))],
        out_specs=[_class_spec(dil) for dil in DILATIONS]
        + [pl.BlockSpec((dil, LANES, POS_BLK // dil), lambda i, t: (0, t, i)) for dil in DILATIONS]
        + [_class_spec(dil, False) for dil in DILATIONS],
        out_shape=[jax.ShapeDtypeStruct((dil, s // dil, d), BF16) for dil in DILATIONS]
        + [jax.ShapeDtypeStruct((dil, d, s // dil), BF16) for dil in DILATIONS]
        + [jax.ShapeDtypeStruct((dil, s // dil, LANES), BF16) for dil in DILATIONS],
        scratch_shapes=[pltpu.VMEM((POS_BLK, LANES), F32), pltpu.VMEM((POS_BLK, LANES), F32)],
        compiler_params=_params(("parallel", "arbitrary")),
    )(do, o, lse, sel_t, _aux_placement().reshape(6 * LANES, LANES))
    return ([a.reshape(s, d) for a in outs[0:3]], list(outs[3:6]), [a.reshape(s, LANES) for a in outs[6:9]])


def _attn_bwd(qkv, qkv_t, do, do_t, aux, tables, dil, *, name):
    _, s, d = qkv.shape
    nq = s // ATTN_BLK
    per = nq // dil
    tiles = d // LANES
    half = ATTN_HEAD_DIM

    def body(qd_ref, qo_ref, k_ref, qtd_ref, qto_ref, kt_ref, vt_ref, dod_ref, doo_ref, dotd_ref, doto_ref,
             auxd_ref, auxo_ref, c_ref, ss_ref, out_ref, carry):
        kb = pl.program_id(0)

        @pl.when(kb == 0)
        def _():
            carry[...] = jnp.zeros_like(carry)

        has_next = (kb % per) != (per - 1)
        qi = lax.broadcasted_iota(jnp.int32, (ATTN_BLK, 2 * ATTN_BLK), 0)
        kj = lax.broadcasted_iota(jnp.int32, (ATTN_BLK, 2 * ATTN_BLK), 1) % ATTN_BLK
        valid_d = kj <= qi
        valid_o = (kj >= qi) & has_next
        lane = lax.broadcasted_iota(jnp.int32, (ATTN_BLK, LANES), 1)
        row = lax.broadcasted_iota(jnp.int32, (LANES, ATTN_BLK), 0)
        side = lax.broadcasted_iota(jnp.int32, (LANES, 2 * ATTN_BLK), 0)
        first = lax.broadcasted_iota(jnp.int32, (LANES, 2 * ATTN_BLK), 1) < ATTN_BLK
        c, ss = c_ref[...], ss_ref[...]
        zero = jnp.zeros((), BF16)
        sides = ((qd_ref, qtd_ref, dod_ref, dotd_ref, auxd_ref[...], valid_d),
                 (qo_ref, qto_ref, doo_ref, doto_ref, auxo_ref[...], valid_o))
        both = (0, 1)

        def head_halves(x, index):
            axis = 0 if index is lane else 1
            return jnp.concatenate([jnp.where(index < half, x, zero), jnp.where(index >= half, x, zero)], axis=axis)

        for t0 in range(0, tiles, ATTN_TILE_GROUP):
            group = range(t0, t0 + ATTN_TILE_GROUP)
            cols = {t: pl.ds(t * LANES, LANES) for t in group}
            kk, kk_t, vv_t = {}, {}, {}
            for t in group:
                base = AUX_PER_TILE * t
                hit = lambda lo: ((first & (side >= base + lo) & (side < base + lo + 3))
                                  | (jnp.logical_not(first) & (side >= base + lo + 3) & (side < base + lo + 6)))
                kk[t] = head_halves(k_ref[:, cols[t]], lane)
                kk_t[t] = jnp.concatenate([head_halves(kt_ref[cols[t], :], row),
                                           jnp.where(hit(0), -1.0, 0.0).astype(BF16)], axis=0)
                vv_t[t] = jnp.concatenate([head_halves(vt_ref[cols[t], :], row),
                                           jnp.where(hit(6), -1.0, 0.0).astype(BF16)], axis=0)
            sc = {(t, w): _dot(jnp.concatenate([sides[w][0][:, cols[t]], sides[w][4]], axis=1), kk_t[t])
                  for t in group for w in both}
            dpd = {(t, w): _dot(jnp.concatenate([sides[w][2][:, cols[t]], sides[w][4]], axis=1), vv_t[t])
                   for t in group for w in both}
            p = {i: jnp.where(sides[i[1]][5], jnp.exp(sc[i]), 0.0) for i in sc}
            ds = {i: (p[i] * dpd[i]).astype(BF16) for i in sc}
            pb = {i: p[i].astype(BF16) for i in sc}
            dv_t = {t: sum(_dot(sides[w][3][cols[t], :], pb[t, w]) for w in both) for t in group}
            dk_t = {t: sum(_dot(sides[w][1][cols[t], :], ds[t, w]) for w in both) for t in group}
            dq = {i: _dot(ds[i], kk[i[0]]) * ATTN_SCALE for i in sc}
            for t in group:
                dq_now = carry[:, cols[t]] + dq[t, 0]
                carry[:, cols[t]] = dq[t, 1]
                dk = jnp.where(row < half, dk_t[t][:, :ATTN_BLK], dk_t[t][:, ATTN_BLK:]).T
                dv = jnp.where(row < half, dv_t[t][:, :ATTN_BLK], dv_t[t][:, ATTN_BLK:]).T
                out_ref[0, :, cols[t]] = _rotate(dq_now, c, ss, -1.0).astype(BF16)
                out_ref[1, :, cols[t]] = _rotate(dk, c, ss, -1.0).astype(BF16)
                out_ref[2, :, cols[t]] = dv.astype(BF16)

    def nxt(i):
        return jnp.minimum(i + 1, nq - 1)

    def piece(p, shift):
        if shift:
            return pl.BlockSpec((None, ATTN_BLK, d), lambda i: (p, nxt(i), 0))
        return pl.BlockSpec((None, ATTN_BLK, d), lambda i: (p, i, 0))

    def piece_t(p, shift):
        if shift:
            return pl.BlockSpec((None, d, ATTN_BLK), lambda i: (p, 0, nxt(i)))
        return pl.BlockSpec((None, d, ATTN_BLK), lambda i: (p, 0, i))

    def rows(width, shift):
        if shift:
            return pl.BlockSpec((ATTN_BLK, width), lambda i: (nxt(i), 0))
        return pl.BlockSpec((ATTN_BLK, width), lambda i: (i, 0))

    def do_t_spec(shift):
        if shift:
            return pl.BlockSpec((None, d, ATTN_BLK), lambda i: (nxt(i) // per, 0, nxt(i) % per))
        return pl.BlockSpec((None, d, ATTN_BLK), lambda i: (i // per, 0, i % per))

    return pl.pallas_call(
        body, name=name, grid=(nq,),
        in_specs=[piece(0, False), piece(0, True), piece(1, False),
                  piece_t(0, False), piece_t(0, True), piece_t(1, False), piece_t(2, False),
                  rows(d, False), rows(d, True), do_t_spec(False), do_t_spec(True),
                  rows(LANES, False), rows(LANES, True), rows(LANES, False), rows(LANES, False)],
        out_specs=pl.BlockSpec((3, ATTN_BLK, d), lambda i: (0, i, 0)),
        out_shape=jax.ShapeDtypeStruct((3, s, d), BF16),
        scratch_shapes=[pltpu.VMEM((ATTN_BLK, d), F32)],
        compiler_params=_params(("arbitrary",)),
    )(qkv, qkv, qkv, qkv_t, qkv_t, qkv_t, qkv_t, do, do, do_t, do_t, aux, aux, *tables)


def _dx_combine(du, parts):
    s, d = du.shape

    def body(du_ref, p0, p1, p2, out_ref):
        out_ref[...] = ALPHA * du_ref[...] + p0[0]
        for dil, p_ref in zip(DILATIONS[1:], (p1, p2)):
            for r in range(dil):
                out_ref[_class_rows(r, dil), :] += p_ref[r]

    return pl.pallas_call(
        body, name="dx_combine", grid=(s // POS_BLK, d // LANES),
        in_specs=[_pos_spec()] + [_class_spec(dil) for dil in DILATIONS], out_specs=_pos_spec(),
        out_shape=jax.ShapeDtypeStruct((s, d), F32),
        compiler_params=_params(("parallel", "parallel")),
    )(du, *[_class_view(p, dil) for p, dil in zip(parts, DILATIONS)])


HGRN_ROWS = 2048


def _tri(lower, copies=1):
    i = lax.broadcasted_iota(jnp.int32, (HGRN_CHUNK, HGRN_CHUNK * copies), 0)
    j = lax.broadcasted_iota(jnp.int32, (HGRN_CHUNK, HGRN_CHUNK * copies), 1) % HGRN_CHUNK
    return (j <= i) if lower else (j >= i)


def _hgrn_gates(qr, z, lb):
    sq = jax.nn.sigmoid(qr)
    e = jnp.exp(-jnp.abs(z))
    big = 1.0 / (1.0 + e)
    small = e * big
    sg = jnp.where(z >= 0, big, small)
    sn = jnp.where(z >= 0, small, big)
    return sq, qr * sq, sg, sn, lb + (1.0 - lb) * sg, (1.0 - lb) * sn


def _hgrn_fwd(p2, lb, norm_g):
    _, s, d = p2.shape
    nblk = s // HGRN_ROWS
    cps = HGRN_ROWS // HGRN_CHUNK

    def body(p_ref, lb_ref, g_ref, o_ref, on_ref, st_ref, state):
        @pl.when(pl.program_id(1) == 0)
        def _():
            state[...] = jnp.zeros_like(state)

        chunks = [pl.ds(c * HGRN_CHUNK, HGRN_CHUNK) for c in range(cps)]
        ltri = _tri(True)
        lsel = _tri(True, 3).astype(BF16)
        _, q, _, _, f, key = _hgrn_gates(p_ref[0], p_ref[1], lb_ref[...])
        lf = jnp.log(f)
        v = [p_ref[2, rows, :].astype(BF16) for rows in chunks]
        b = [_exact_dot(lsel, lf[c * HGRN_CHUNK:(c + 1) * HGRN_CHUNK]) for c in range(cps)]
        b_last = [bc[HGRN_CHUNK - 1:HGRN_CHUNK, :] for bc in b]
        q = [q[c * HGRN_CHUNK:(c + 1) * HGRN_CHUNK] for c in range(cps)]
        key = [key[c * HGRN_CHUNK:(c + 1) * HGRN_CHUNK] for c in range(cps)]
        qd = [(q[c] * jnp.exp(b[c])).astype(BF16) for c in range(cps)]
        kd = [(key[c] * jnp.exp(-b[c])).astype(BF16) for c in range(cps)]
        k2 = [(key[c] * jnp.exp(b_last[c] - b[c])).astype(BF16) for c in range(cps)]
        a = [jnp.where(ltri, _dot_nt(qd[c], kd[c]), 0.0).astype(BF16) for c in range(cps)]
        kv = [_dot_tn(v[c], k2[c]) for c in range(cps)]
        st, sts = state[...], []
        for c in range(cps):
            sts.append(st)
            st_ref[0, c] = st
            st = st * jnp.exp(b_last[c]) + kv[c]
        state[...] = st
        gv = g_ref[...]
        for c in range(cps):
            o = _dot(a[c], v[c]) + _dot_nt(qd[c], sts[c].astype(BF16))
            o_ref[chunks[c], :] = o
            r = lax.rsqrt(jnp.mean(o * o, axis=-1, keepdims=True) + RMS_EPS)
            on_ref[chunks[c], :] = (o * r * gv).astype(BF16)

    vec = pl.BlockSpec((1, LANES), lambda h, c: (0, h))
    col = pl.BlockSpec((HGRN_ROWS, LANES), lambda h, c: (c, h))
    return pl.pallas_call(
        body, name="hgrn_fwd", grid=(HGRN_HEADS, nblk),
        in_specs=[pl.BlockSpec((3, HGRN_ROWS, LANES), lambda h, c: (0, c, h)), vec, vec],
        out_specs=[col, col, pl.BlockSpec((1, cps, LANES, LANES), lambda h, c: (h, c, 0, 0))],
        out_shape=[jax.ShapeDtypeStruct((s, d), F32), jax.ShapeDtypeStruct((s, d), BF16),
                   jax.ShapeDtypeStruct((HGRN_HEADS, s // HGRN_CHUNK, LANES, LANES), F32)],
        scratch_shapes=[pltpu.VMEM((LANES, LANES), F32)],
        compiler_params=_params(("parallel", "arbitrary")),
    )(p2, lb, norm_g)


def _hgrn_bwd(p2, lb, norm_g, o_raw, states, dyn):
    _, s, d = p2.shape
    nblk = s // HGRN_ROWS
    cps = HGRN_ROWS // HGRN_CHUNK

    def body(p_ref, lb_ref, g_ref, o_ref, st_ref, dy_ref, dp_ref, dg_ref, dlb_ref, dstate):
        @pl.when(pl.program_id(1) == 0)
        def _():
            dstate[...] = jnp.zeros_like(dstate)
            dg_ref[...] = jnp.zeros_like(dg_ref)
            dlb_ref[...] = jnp.zeros_like(dlb_ref)

        n = cps
        cut = lambda t: [t[c * HGRN_CHUNK:(c + 1) * HGRN_CHUNK] for c in range(n)]
        chunks = [pl.ds(c * HGRN_CHUNK, HGRN_CHUNK) for c in range(n)]
        lbv = lb_ref[...]
        ltri = _tri(True)
        lsel = _tri(True, 3).astype(BF16)
        usel = _tri(False, 3).astype(BF16)
        last_row = lax.broadcasted_iota(jnp.int32, (HGRN_CHUNK, LANES), 0) == HGRN_CHUNK - 1
        qr, z = p_ref[0], p_ref[1]
        sq, q, sg, sn, f, key = _hgrn_gates(qr, z, lbv)
        lf = jnp.log(f)
        v = [p_ref[2, rows, :].astype(BF16) for rows in chunks]
        o, dyv = o_ref[...], dy_ref[...]
        r = lax.rsqrt(jnp.mean(o * o, axis=-1, keepdims=True) + RMS_EPS)
        oh = o * r
        doh = dyv * g_ref[...]
        do = cut((r * (doh - oh * jnp.mean(doh * oh, axis=-1, keepdims=True))).astype(BF16))
        dg_ref[...] += jnp.sum(dyv * oh, axis=0, keepdims=True)
        b = [_exact_dot(lsel, lfc) for lfc in cut(lf)]
        b_last = [bc[HGRN_CHUNK - 1:HGRN_CHUNK, :] for bc in b]
        q, key = cut(q), cut(key)
        eb = [jnp.exp(bc) for bc in b]
        enb = [jnp.exp(-bc) for bc in b]
        e2 = [jnp.exp(b_last[c] - b[c]) for c in range(n)]
        dec = [jnp.exp(bl) for bl in b_last]
        qd_f = [q[c] * eb[c] for c in range(n)]
        kd_f = [key[c] * enb[c] for c in range(n)]
        k2_f = [key[c] * e2[c] for c in range(n)]
        qd, kd, k2 = ([t.astype(BF16) for t in ts] for ts in (qd_f, kd_f, k2_f))
        a = [jnp.where(ltri, _dot_nt(qd[c], kd[c]), 0.0).astype(BF16) for c in range(n)]
        da = [jnp.where(ltri, _dot_nt(do[c], v[c]), 0.0).astype(BF16) for c in range(n)]
        grow = [_dot_tn(do[c], qd[c]) for c in range(n)]
        dst, dsts = dstate[...], [None] * n
        for c in reversed(range(n)):
            dsts[c] = dst
            dst = dst * dec[c] + grow[c]
        dstate[...] = dst
        st = [st_ref[0, c] for c in range(n)]
        dstb = [t.astype(BF16) for t in dsts]
        dv = [_dot_tn(a[c], do[c]) + _dot_nt(k2[c], dstb[c]) for c in range(n)]
        dqd = [_dot(da[c], kd[c]) + _dot(do[c], st[c].astype(BF16)) for c in range(n)]
        dkd = [_dot_tn(da[c], qd[c]) for c in range(n)]
        dk2 = [_dot(v[c], dstb[c]) for c in range(n)]
        db = []
        for c in range(n):
            ddec = jnp.sum(dsts[c] * st[c], axis=0, keepdims=True)
            db_last = jnp.sum(dk2[c] * k2_f[c], axis=0, keepdims=True) + ddec * dec[c]
            db.append(dqd[c] * qd_f[c] - dkd[c] * kd_f[c] - dk2[c] * k2_f[c] + jnp.where(last_row, db_last, 0.0))
        dlf = [_exact_dot(usel, dbc) for dbc in db]
        f, sg, sn, sq, qr = cut(f), cut(sg), cut(sn), cut(sq), cut(qr)
        dlb_acc = jnp.zeros((1, LANES), F32)
        for c in range(n):
            dkey = dkd[c] * enb[c] + dk2[c] * e2[c]
            common = dlf[c] / f[c] - dkey
            dlb_acc += jnp.sum(common * sn[c], axis=0, keepdims=True)
            dp_ref[0, chunks[c], :] = (dqd[c] * eb[c] * (sq[c] * (1.0 + qr[c] * (1.0 - sq[c])))).astype(BF16)
            dp_ref[1, chunks[c], :] = (common * ((1.0 - lbv) * sg[c] * sn[c])).astype(BF16)
            dp_ref[2, chunks[c], :] = dv[c].astype(BF16)
        dlb_ref[...] += dlb_acc

    def rev(c):
        return nblk - 1 - c

    vec = pl.BlockSpec((1, LANES), lambda h, c: (0, h))
    col = pl.BlockSpec((HGRN_ROWS, LANES), lambda h, c: (rev(c), h))
    p3 = pl.BlockSpec((3, HGRN_ROWS, LANES), lambda h, c: (0, rev(c), h))
    return pl.pallas_call(
        body, name="hgrn_bwd", grid=(HGRN_HEADS, nblk),
        in_specs=[p3, vec, vec, col, pl.BlockSpec((1, cps, LANES, LANES), lambda h, c: (h, rev(c), 0, 0)), col],
        out_specs=[p3, vec, vec],
        out_shape=[jax.ShapeDtypeStruct((3, s, d), BF16), jax.ShapeDtypeStruct((1, d), F32),
                   jax.ShapeDtypeStruct((1, d), F32)],
        scratch_shapes=[pltpu.VMEM((LANES, LANES), F32)],
        compiler_params=_params(("parallel", "arbitrary")),
    )(p2, lb, norm_g, o_raw, states, dyn)


SMALL_ROWS = 16
FUSED_TM = 512


def _relu2(h):
    r = jnp.maximum(h.astype(F32), 0.0)
    return r * r


def _dact_epi(acc, h):
    return (acc * (2.0 * jnp.maximum(h.astype(F32), 0.0)),)


def _rope_epi(j, acc, ct, sst):
    acc_t = acc.T
    rot_t = _rotate_t(acc_t, ct, sst) * jnp.where(j == 0, ATTN_SCALE, 1.0)
    out_t = jnp.where(j >= 2, acc_t, rot_t).astype(BF16)
    return out_t.T, out_t


def _local_step(x, target, get_w, small, on_grads, deps=()):
    s, d = x.shape
    lb = small["lb"]
    gq = 3 * d
    gt = gq // 1024

    def ln_vecs(which, layer):
        return small[f"ln_{which}_g"][layer:layer + 1], small[f"ln_{which}_b"][layer:layer + 1]

    def ln_of(a, w, x_in, which, layer, *, name, k, a_pre=None, deps=(), prev=None):
        vecs = ln_vecs(which, layer) + (ln_vecs(*prev) if prev else ())
        return _mm(a, w, "nn", name=name, m=s, n=d, k=k, tm=FUSED_TM, tk=k, a_pre=a_pre, out_dtypes=(BF16, F32),
                   row_outs=((F32, 1),), epi=_ln_epi, tile_extras=(x_in,), vec_extras=vecs, deps=deps)

    def ln_grad_of(a, w, du_next, xh, rstd, which, layer, *, name, k, a_split=1, deps=()):
        return _mm(a, w, "nt", name=name, m=s, n=d, k=k, tm=FUSED_TM, tk=k, a_split=a_split,
                   out_dtypes=(F32, BF16), vec_outs=2, epi=_ln_grad_epi, tile_extras=(du_next, xh), row_extras=(rstd,),
                   vec_extras=(small[f"ln_{which}_g"][layer:layer + 1],), deps=deps)

    def ffn_fwd(xb, xh_in, w_up, w_down, tag):
        h = _mm(xb, w_up, "nn", name=f"ffn_up_{tag}", m=s, n=D_FF, k=d, tm=2048, out_dtypes=(BF16,))
        return (h,) + tuple(ln_of(h, w_down, xh_in, "ffn", tag, name=f"ffn_down_{tag}", k=D_FF, a_pre=_relu2,
                                  prev=("mix", tag)))

    def ffn_bwd(du, dub, xb, h, w_up, w_down, xh, rstd, tag):
        dh = _mm(dub, w_down, "nt", name=f"ffn_dact_{tag}", m=s, n=D_FF, k=d, tm=2048, out_dtypes=(BF16,), epi=_dact_epi,
                 tile_extras=(h,))
        g_down = _mm(h, dub, "tn", name=f"ffn_gdown_{tag}", m=D_FF, n=d, k=s, tm=512, tk=s, a_pre=_relu2,
                     out_dtypes=(BF16,))
        g_up = _mm(xb, dh, "tn", name=f"ffn_gup_{tag}", m=d, n=D_FF, k=s, tk=s, out_dtypes=(BF16,))
        after = on_grads(f"ffn{tag}", {f"ffn_w_down{tag}": g_down, f"ffn_w_up{tag}": g_up})
        return ln_grad_of(dh, w_up, du, xh, rstd, "mix", tag, name=f"ffn_dx_{tag}", k=D_FF, deps=after)

    xs = _prep_x(x, deps)
    w_ain, after = get_w("attn_w_in", None)
    tabs, qkvs, qkv_ts, o_parts, lse_parts = [], [], [], [], []
    for g, dil in enumerate(DILATIONS):
        tabs.append(_rope_tables(s, dil))
        tabs_t = [t[:, :ATTN_HEAD_DIM].T for t in tabs[g]]
        qkv, qkv_t = _mm(xs[g], w_ain, "nn", name=f"attn_in_{g}", m=s, n=gq, k=d, b_col_off=g * gt, out_split=3,
                         out_dtypes=(BF16,), epi=_rope_epi, epi_wants_j=True, row_extras=tabs_t, t_out=True,
                         deps=after if g == 0 else ())
        qkvs.append(qkv)
        qkv_ts.append(qkv_t)
        o_g, lse_g = _attn_fwd(qkv, dil, name=f"attn_fwd_{g}")
        o_parts.append(o_g)
        lse_parts.append(lse_g)
    o_f, o_b, lse_t = _attn_combine(o_parts, lse_parts)
    w_aout, after = get_w("attn_w_out", o_b)
    x1b, xh1, r1 = ln_of(o_b, w_aout, x, "mix", 0, name="attn_out", k=d, deps=after)
    w_up0, w_down0 = get_w("ffn_w_up0", o_b)[0], get_w("ffn_w_down0", o_b)[0]
    h0, x2b, xh2, r2 = ffn_fwd(x1b, xh1, w_up0, w_down0, 0)
    w_hin, w_hout, norm_g = get_w("hgrn_w_in", x2b)[0], get_w("hgrn_w_out", x2b)[0], get_w("hgrn_norm_g", x2b)[0]
    p2 = _mm(x2b, w_hin, "nn", name="hgrn_in", m=s, n=3 * d, k=d, tm=2048, out_split=3)
    o_raw, o_n, states = _hgrn_fwd(p2, lb, norm_g)
    w_up1, w_down1 = get_w("ffn_w_up1", o_n)[0], get_w("ffn_w_down1", o_n)[0]
    x3b, xh3, r3 = ln_of(o_n, w_hout, xh2, "mix", 1, name="hgrn_out", k=d, prev=("ffn", 0))
    h1 = _mm(x3b, w_up1, "nn", name="ffn_up_1", m=s, n=D_FF, k=d, tm=2048, out_dtypes=(BF16,))
    du, dub, dg_ffn1, db_ffn1, sq = _mm(
        h1, w_down1, "nn", name="ffn_down_1", m=s, n=d, k=D_FF, tm=FUSED_TM, tk=D_FF, a_pre=_relu2,
        out_dtypes=(F32, BF16), vec_outs=3, epi=_ln_loss_epi, tile_extras=(xh3, target),
        vec_extras=ln_vecs("ffn", 1) + ln_vecs("mix", 1))
    du, dub, dg_mix1, db_mix1 = ffn_bwd(du, dub, x3b, h1, w_up1, w_down1, xh3, r3, 1)
    dyn = _mm(dub, w_hout, "nt", name="hgrn_dout", m=s, n=d, k=d)
    g_hout = _mm(o_n, dub, "tn", name="hgrn_gout", m=d, n=d, k=s, out_dtypes=(BF16,))
    dp2, d_norm_g, d_lb = _hgrn_bwd(p2, lb, norm_g, o_raw, states, dyn)
    g_hin = _mm(x2b, dp2, "tn", name="hgrn_gin", m=d, n=3 * d, k=s, tk=s, b_split=3, out_dtypes=(BF16,))
    after = on_grads("hgrn", {"hgrn_w_out": g_hout, "hgrn_w_in": g_hin})
    du, dub, dg_ffn0, db_ffn0 = ln_grad_of(dp2, w_hin, du, xh2, r2, "ffn", 0, name="hgrn_dx", k=3 * d, a_split=3,
                                           deps=after)
    du, dub, dg_mix0, db_mix0 = ffn_bwd(du, dub, x1b, h0, w_up0, w_down0, xh1, r1, 0)
    g_aout = _mm(o_b, dub, "tn", name="attn_gout", m=d, n=d, k=s, out_dtypes=(BF16,))
    sm1 = jax.nn.softmax(small["lb_logits"], axis=0)
    d_l1 = d_lb * (sm1[0:1] * sm1[1:2])
    zeros = jnp.zeros((SMALL_ROWS - 12, d), F32)
    loss_row = jnp.broadcast_to(0.5 * jnp.sum(sq) / d, (1, d))
    small_grads = jnp.concatenate([dg_mix0, dg_mix1, db_mix0, db_mix1, dg_ffn0, dg_ffn1, db_ffn0, db_ffn1,
                                   -d_l1, d_l1, d_norm_g, loss_row, zeros], axis=0)
    after = on_grads("attn_out", {"attn_w_out": g_aout, "small": small_grads})
    do = _mm(dub, w_aout, "nt", name="attn_dout", m=s, n=d, k=d, deps=after)
    do_parts, do_ts, aux_parts = _attn_bwd_prep(do, o_f, lse_t)
    g_ain, dqkvs = None, []
    for g, dil in enumerate(DILATIONS):
        dqkvs.append(_attn_bwd(qkvs[g], qkv_ts[g], do_parts[g], do_ts[g], aux_parts[g], tabs[g], dil,
                               name=f"attn_bwd_{g}"))
        g_ain = _mm(xs[g], dqkvs[g], "tn", name=f"attn_gin_{g}", m=d, n=gq, k=s, tk=s, b_split=3, out_dtypes=(BF16,),
                    out_col_off=g * gt, out_cols=3 * gq, alias=g_ain)
    after = on_grads("attn_in", {"attn_w_in": g_ain})
    dx_parts = [_mm(dqkvs[g], w_ain, "nt", name=f"attn_dx_{g}", m=s, n=d, k=gq, tm=FUSED_TM, tk=gq, a_split=3, b_k_off=g,
                    deps=after if g == 0 else ())
                for g in range(len(DILATIONS))]
    return _dx_combine(du, dx_parts)


def _mesh_place():
    x, y, c = lax.axis_index("x"), lax.axis_index("y"), lax.axis_index("c")
    return x, y, c, 4 * x + 2 * y + c


def _peer(x, y, c, k):
    px = 1 - x if (k >> 2) & 1 else x
    py = 1 - y if (k >> 1) & 1 else y
    pc = 1 - c if k & 1 else c
    return (px, py, pc), 4 * px + 2 * py + pc


def _window(ref, axis, size, idx):
    if axis is None:
        return ref
    sl = [slice(None)] * len(ref.shape)
    sl[axis] = pl.ds(idx * size, size)
    return ref.at[tuple(sl)]


_HBM = pl.BlockSpec(memory_space=pltpu.HBM)
_SEM = pl.BlockSpec(memory_space=pltpu.SEMAPHORE)
_EFFECT = pltpu.SideEffectType.DATAFLOW_SIDE_EFFECTING


def _xchg_ends(src_ref, land_ref, axis, gather, me, other):
    if gather:
        size = src_ref.shape[axis]
        return src_ref, _window(land_ref, axis, size, me), _window(land_ref, axis, size, other)
    size = None if axis is None else src_ref.shape[axis] // N_DEV
    return _window(src_ref, axis, size, other), land_ref.at[me], land_ref.at[other]


def _xchg_start(srcs, lands, axes, *, gather, name, deps=(), peers=tuple(range(1, N_DEV))):
    n = len(srcs)
    nd = len(deps)

    def body(*refs):
        src_refs, land_refs = refs[:n], refs[n:2 * n]
        send, recv = refs[2 * n + nd:3 * n + nd], refs[3 * n + nd:4 * n + nd]
        token = refs[-1]
        x, y, c, me = _mesh_place()
        for k in peers:
            peer, pidx = _peer(x, y, c, k)
            for i in range(n):
                src, dst, _ = _xchg_ends(src_refs[i], land_refs[i], axes[i], gather, me, pidx)
                pltpu.make_async_remote_copy(
                    src_ref=src, dst_ref=dst, send_sem=send[i].at[k - 1], recv_sem=recv[i].at[k - 1],
                    device_id=peer, device_id_type=pl.DeviceIdType.MESH).start()
        for i in range(n):
            src, dst, _ = _xchg_ends(src_refs[i], land_refs[i], axes[i], gather, me, me)
            pltpu.make_async_copy(src, dst, send[i].at[N_DEV - 1]).start()
        token[...] = jnp.zeros_like(token)

    bufs = list(srcs) + list(lands)
    outs = pl.pallas_call(
        body, name=name,
        out_shape=[pltpu.SemaphoreType.DMA((N_DEV,))] * (2 * n) + [pltpu.HBM(b.shape, b.dtype) for b in bufs]
        + [jax.ShapeDtypeStruct((8, LANES), F32)],
        in_specs=[_HBM] * (2 * n) + [pl.BlockSpec(memory_space=pl.ANY)] * nd,
        out_specs=[_SEM] * (2 * n) + [_HBM] * (2 * n) + [pl.BlockSpec(memory_space=pltpu.VMEM)],
        input_output_aliases={i: 2 * n + i for i in range(2 * n)},
        compiler_params=pltpu.CompilerParams(has_side_effects=_EFFECT),
    )(*[pltpu.with_memory_space_constraint(b, pltpu.HBM) for b in bufs], *deps)
    return dict(send=outs[:n], recv=outs[n:2 * n], srcs=outs[2 * n:3 * n], lands=outs[3 * n:4 * n], token=outs[-1],
                axes=list(axes), gather=gather)


SIBLING = 1
SAME_CORE = (2, 4, 6)


def _gather_relay(xc, after, *, name):
    axis = xc["axes"][0]

    def body(src_ref, land_ref, send1, recv1, after_ref, src_out, land_out, send2, recv2):
        x, y, c, me = _mesh_place()
        sibling, _ = _peer(x, y, c, SIBLING)
        for j, k in enumerate(SAME_CORE):
            peer, pidx = _peer(x, y, c, k)
            src, _, got = _xchg_ends(src_ref, land_ref, axis, True, me, pidx)
            pltpu.make_async_remote_copy(
                src_ref=src, dst_ref=got, send_sem=send1.at[k - 1], recv_sem=recv1.at[k - 1],
                device_id=peer, device_id_type=pl.DeviceIdType.MESH).wait_recv()
            pltpu.make_async_remote_copy(
                src_ref=got, dst_ref=got, send_sem=send2.at[j], recv_sem=recv2.at[j],
                device_id=sibling, device_id_type=pl.DeviceIdType.MESH).start()

    src, land = xc["srcs"][0], xc["lands"][0]
    outs = pl.pallas_call(
        body, name=name,
        out_shape=[pltpu.HBM(src.shape, src.dtype), pltpu.HBM(land.shape, land.dtype)]
        + [pltpu.SemaphoreType.DMA((len(SAME_CORE),))] * 2,
        in_specs=[_HBM, _HBM, _SEM, _SEM, pl.BlockSpec(memory_space=pl.ANY)],
        out_specs=[_HBM, _HBM, _SEM, _SEM], input_output_aliases={0: 0, 1: 1},
        compiler_params=pltpu.CompilerParams(has_side_effects=_EFFECT),
    )(src, land, xc["send"][0], xc["recv"][0], after)
    return dict(src=outs[0], land=outs[1], send=outs[2], recv=outs[3])


def _gather_relay_wait(xc, relay, after, *, name):
    axis = xc["axes"][0]

    def body(src_ref, land_ref, send1, recv1, send2, recv2, after_ref, src_out, land_out):
        x, y, c, me = _mesh_place()
        sibling, sidx = _peer(x, y, c, SIBLING)
        for k in (SIBLING,) + SAME_CORE:
            peer, pidx = _peer(x, y, c, k)
            src, dst, got = _xchg_ends(src_ref, land_ref, axis, True, me, pidx)
            pltpu.make_async_remote_copy(
                src_ref=src, dst_ref=dst, send_sem=send1.at[k - 1], recv_sem=recv1.at[k - 1],
                device_id=peer, device_id_type=pl.DeviceIdType.MESH).wait_send()
        src, dst, got = _xchg_ends(src_ref, land_ref, axis, True, me, sidx)
        pltpu.make_async_remote_copy(
            src_ref=src, dst_ref=got, send_sem=send1.at[SIBLING - 1], recv_sem=recv1.at[SIBLING - 1],
            device_id=sibling, device_id_type=pl.DeviceIdType.MESH).wait_recv()
        src, dst, _ = _xchg_ends(src_ref, land_ref, axis, True, me, me)
        pltpu.make_async_copy(src, dst, send1.at[N_DEV - 1]).wait()
        for j, k in enumerate(SAME_CORE):
            _, pidx = _peer(x, y, c, k)
            _, qidx = _peer(x, y, c, k ^ SIBLING)
            _, _, sent = _xchg_ends(src_ref, land_ref, axis, True, me, pidx)
            _, _, got = _xchg_ends(src_ref, land_ref, axis, True, me, qidx)
            pltpu.make_async_remote_copy(
                src_ref=sent, dst_ref=sent, send_sem=send2.at[j], recv_sem=recv2.at[j],
                device_id=sibling, device_id_type=pl.DeviceIdType.MESH).wait_send()
            pltpu.make_async_remote_copy(
                src_ref=got, dst_ref=got, send_sem=send2.at[j], recv_sem=recv2.at[j],
                device_id=sibling, device_id_type=pl.DeviceIdType.MESH).wait_recv()

    outs = pl.pallas_call(
        body, name=name, out_shape=[pltpu.HBM(relay["src"].shape, relay["src"].dtype),
                                    pltpu.HBM(relay["land"].shape, relay["land"].dtype)],
        in_specs=[_HBM, _HBM, _SEM, _SEM, _SEM, _SEM, pl.BlockSpec(memory_space=pl.ANY)],
        out_specs=[_HBM, _HBM], input_output_aliases={0: 0, 1: 1},
        compiler_params=pltpu.CompilerParams(has_side_effects=_EFFECT),
    )(relay["src"], relay["land"], xc["send"][0], xc["recv"][0], relay["send"], relay["recv"], after)
    return outs[1]


def _xchg_wait(xc, items, after, *, name):
    m = len(items)
    gather = xc["gather"]
    axes = [xc["axes"][i] for i in items]

    def body(*refs):
        src_refs, land_refs = refs[:m], refs[m:2 * m]
        send, recv = refs[2 * m:3 * m], refs[3 * m:4 * m]
        x, y, c, me = _mesh_place()
        for k in range(1, N_DEV):
            peer, pidx = _peer(x, y, c, k)
            for j in range(m):
                src, dst, got = _xchg_ends(src_refs[j], land_refs[j], axes[j], gather, me, pidx)
                pltpu.make_async_remote_copy(
                    src_ref=src, dst_ref=dst, send_sem=send[j].at[k - 1], recv_sem=recv[j].at[k - 1],
                    device_id=peer, device_id_type=pl.DeviceIdType.MESH).wait_send()
                pltpu.make_async_remote_copy(
                    src_ref=src, dst_ref=got, send_sem=send[j].at[k - 1], recv_sem=recv[j].at[k - 1],
                    device_id=peer, device_id_type=pl.DeviceIdType.MESH).wait_recv()
        for j in range(m):
            src, dst, _ = _xchg_ends(src_refs[j], land_refs[j], axes[j], gather, me, me)
            pltpu.make_async_copy(src, dst, send[j].at[N_DEV - 1]).wait()

    bufs = [xc["srcs"][i] for i in items] + [xc["lands"][i] for i in items]
    sems = [xc["send"][i] for i in items] + [xc["recv"][i] for i in items]
    outs = pl.pallas_call(
        body, name=name, out_shape=[pltpu.HBM(b.shape, b.dtype) for b in bufs],
        in_specs=[_HBM] * (2 * m) + [_SEM] * (2 * m) + [pl.BlockSpec(memory_space=pl.ANY)],
        out_specs=[_HBM] * (2 * m), input_output_aliases={j: j for j in range(2 * m)},
        compiler_params=pltpu.CompilerParams(has_side_effects=_EFFECT),
    )(*bufs, *sems, after)
    return outs[m:]


def _cast_bf16(a, *, name):
    r, c = a.shape
    tr = min(r, 512)

    def body(a_ref, o_ref):
        o_ref[...] = a_ref[...].astype(BF16)

    spec = pl.BlockSpec((tr, c), lambda i: (i, 0))
    return pl.pallas_call(body, name=name, grid=(r // tr,), in_specs=[spec], out_specs=spec,
                          out_shape=jax.ShapeDtypeStruct((r, c), BF16), compiler_params=_params(("parallel",)))(a)


def _adamw(slabs, w, m, v, *, name):
    layers, r, c = w.shape
    tr = min(r, 256)

    def body(*refs):
        s_refs = refs[:layers]
        w_ref, m_ref, v_ref, g_ref, d_ref, mo_ref, vo_ref = refs[layers:]
        for l in range(layers):
            g = s_refs[l][0].astype(F32)
            for i in range(1, N_DEV):
                g = g + s_refs[l][i].astype(F32)
            m2 = ADAM_B1 * m_ref[l] + (1.0 - ADAM_B1) * g
            v2 = ADAM_B2 * v_ref[l] + (1.0 - ADAM_B2) * (g * g)
            m_hat = m2 / (1.0 - ADAM_B1 ** ADAM_STEP)
            v_hat = v2 / (1.0 - ADAM_B2 ** ADAM_STEP)
            g_ref[l] = g
            d_ref[l] = -ADAM_LR * (m_hat / (jnp.sqrt(v_hat) + ADAM_EPS) + ADAM_WD * w_ref[l])
            mo_ref[l] = m2
            vo_ref[l] = v2

    spec = pl.BlockSpec((layers, tr, c), lambda i: (0, i, 0))
    return pl.pallas_call(
        body, name=name, grid=(r // tr,),
        in_specs=[pl.BlockSpec((N_DEV, tr, c), lambda i: (0, i, 0))] * layers + [spec, spec, spec],
        out_specs=[spec] * 4, out_shape=[jax.ShapeDtypeStruct((layers, r, c), F32)] * 4,
        compiler_params=_params(("parallel",)),
    )(*slabs, w, m, v)


GATHER_AXIS = {"attn_w_in": 1, "attn_w_out": 0, "ffn_w_up0": 1, "ffn_w_down0": 0, "hgrn_w_in": 1, "hgrn_w_out": 0,
               "hgrn_norm_g": 1, "ffn_w_up1": 1, "ffn_w_down1": 0}
GATHER_STAGES = (("attn_w_in",), ("attn_w_out", "ffn_w_up0", "ffn_w_down0", "hgrn_w_in", "hgrn_w_out", "hgrn_norm_g"),
                 ("ffn_w_up1", "ffn_w_down1"))
GATHER_WAITS = ((("attn_w_in",), 0, 1), (("attn_w_out", "ffn_w_up0", "ffn_w_down0"), 1, 2),
                (("hgrn_w_in", "hgrn_w_out", "hgrn_norm_g"), 1, None), (("ffn_w_up1", "ffn_w_down1"), 2, None))
SCATTER_AXIS = dict(GATHER_AXIS, small=None)
BIG = ("attn_w_in", "attn_w_out", "hgrn_w_in", "hgrn_w_out", "ffn_w_up", "ffn_w_down")
SMALL = ("lb_logits", "ln_mix_g", "ln_mix_b", "ln_ffn_g", "ln_ffn_b")
SMALL_ROW = {"ln_mix_g": 0, "ln_mix_b": 2, "ln_ffn_g": 4, "ln_ffn_b": 6, "lb_logits": 8}
NORM_G_ROW = 10
LOSS_ROW = 11


def kernel(x, attn_w_in, attn_w_out, hgrn_w_in, hgrn_w_out, hgrn_norm_g, lb_logits, ln_mix_g, ln_mix_b, ln_ffn_g, ln_ffn_b, ffn_w_up, ffn_w_down, loss_target, m_attn_w_in, m_attn_w_out, m_hgrn_w_in, m_hgrn_w_out, m_hgrn_norm_g, m_lb_logits, m_ln_mix_g, m_ln_mix_b, m_ln_ffn_g, m_ln_ffn_b, m_ffn_w_up, m_ffn_w_down, v_attn_w_in, v_attn_w_out, v_hgrn_w_in, v_hgrn_w_out, v_hgrn_norm_g, v_lb_logits, v_ln_mix_g, v_ln_mix_b, v_ln_ffn_g, v_ln_ffn_b, v_ffn_w_up, v_ffn_w_down):
    wts = dict(attn_w_in=attn_w_in, attn_w_out=attn_w_out, hgrn_w_in=hgrn_w_in, hgrn_w_out=hgrn_w_out,
               hgrn_norm_g=hgrn_norm_g, lb_logits=lb_logits, ln_mix_g=ln_mix_g, ln_mix_b=ln_mix_b, ln_ffn_g=ln_ffn_g,
               ln_ffn_b=ln_ffn_b, ffn_w_up=ffn_w_up, ffn_w_down=ffn_w_down)
    mom = dict(attn_w_in=m_attn_w_in, attn_w_out=m_attn_w_out, hgrn_w_in=m_hgrn_w_in, hgrn_w_out=m_hgrn_w_out,
               hgrn_norm_g=m_hgrn_norm_g, lb_logits=m_lb_logits, ln_mix_g=m_ln_mix_g, ln_mix_b=m_ln_mix_b,
               ln_ffn_g=m_ln_ffn_g, ln_ffn_b=m_ln_ffn_b, ffn_w_up=m_ffn_w_up, ffn_w_down=m_ffn_w_down)
    vel = dict(attn_w_in=v_attn_w_in, attn_w_out=v_attn_w_out, hgrn_w_in=v_hgrn_w_in, hgrn_w_out=v_hgrn_w_out,
               hgrn_norm_g=v_hgrn_norm_g, lb_logits=v_lb_logits, ln_mix_g=v_ln_mix_g, ln_mix_b=v_ln_mix_b,
               ln_ffn_g=v_ln_ffn_g, ln_ffn_b=v_ln_ffn_b, ffn_w_up=v_ffn_w_up, ffn_w_down=v_ffn_w_down)
    me = 4 * lax.axis_index("x") + 2 * lax.axis_index("y") + lax.axis_index("c")

    src = {"attn_w_in": attn_w_in[0], "attn_w_out": attn_w_out[0], "hgrn_w_in": hgrn_w_in[0], "hgrn_w_out": hgrn_w_out[0],
           "ffn_w_up0": ffn_w_up[0], "ffn_w_down0": ffn_w_down[0], "ffn_w_up1": ffn_w_up[1], "ffn_w_down1": ffn_w_down[1]}
    gathers, got = {}, {}

    def start_gather(stage, deps):
        shards, lands = [], []
        for nm in GATHER_STAGES[stage]:
            sh = hgrn_norm_g if nm == "hgrn_norm_g" else _cast_bf16(src[nm], name=f"cast_{nm}")
            ax = GATHER_AXIS[nm]
            shape = list(sh.shape)
            shape[ax] *= N_DEV
            shards.append(sh)
            lands.append(lax.empty(tuple(shape), sh.dtype))
        peers = (SIBLING,) + SAME_CORE if stage == 0 else tuple(range(1, N_DEV))
        gathers[stage] = _xchg_start(shards, lands, [GATHER_AXIS[nm] for nm in GATHER_STAGES[stage]], gather=True,
                                     name=f"gather_start_{stage}", deps=deps, peers=peers)
        return [gathers[stage]["token"]]

    def get_w(name, after):
        deps = []
        if name not in got:
            group, stage, then = [w for w in GATHER_WAITS if name in w[0]][0]
            xc = gathers[stage]
            if stage == 0:
                relay = _gather_relay(xc, xc["token"], name="gather_relay")
                res = [_gather_relay_wait(xc, relay, xc["token"], name=f"gather_wait_{group[0]}")]
            else:
                res = _xchg_wait(xc, [GATHER_STAGES[stage].index(nm) for nm in group], after,
                                 name=f"gather_wait_{group[0]}")
            got.update(zip(group, res))
            if then is not None:
                deps = start_gather(then, [res[0]])
        return got[name], deps

    first = start_gather(0, [])

    scattered = {}

    def on_grads(tag, grads):
        gnames = list(grads)
        axes = [SCATTER_AXIS[nm] for nm in gnames]
        stacks = []
        for nm, ax in zip(gnames, axes):
            shape = list(grads[nm].shape)
            if ax is not None:
                shape[ax] //= N_DEV
            stacks.append(lax.empty((N_DEV, *shape), grads[nm].dtype))
        scattered[tag] = (gnames, _xchg_start([grads[nm] for nm in gnames], stacks, axes, gather=False,
                                              name=f"scatter_start_{tag}"))
        return [scattered[tag][1]["token"]]

    sm = jax.nn.softmax(lb_logits, axis=0)
    csum = jnp.cumsum(sm, axis=0)
    small = dict(lb=(csum - csum[0:1])[1:2], lb_logits=lb_logits, ln_mix_g=ln_mix_g, ln_mix_b=ln_mix_b,
                 ln_ffn_g=ln_ffn_g, ln_ffn_b=ln_ffn_b)
    grad_x = _local_step(x[0], loss_target[0], get_w, small, on_grads, deps=first)
    out = {}

    def stack_small(src_):
        rows = [None] * SMALL_ROWS
        for name in SMALL:
            rows[SMALL_ROW[name]], rows[SMALL_ROW[name] + 1] = src_[name][0:1], src_[name][1:2]
        zero = jnp.zeros((1, x.shape[-1]), F32)
        return jnp.concatenate([zero if r is None else r for r in rows], axis=0)[None]

    def update(name, slabs):
        shape = wts[name].shape
        out[name] = [r.reshape(shape) for r in _adamw(slabs, wts[name], mom[name], vel[name], name=f"adamw_{name}")]
        return out[name][0]

    slabs, after = {}, grad_x
    for tag, (gnames, xc) in scattered.items():
        slabs.update(zip(gnames, _xchg_wait(xc, list(range(len(gnames))), after, name=f"scatter_wait_{tag}")))
        if tag == "ffn1":
            continue
        if tag == "ffn0":
            update("ffn_w_down", [slabs["ffn_w_down0"], slabs["ffn_w_down1"]])
            after = update("ffn_w_up", [slabs["ffn_w_up0"], slabs["ffn_w_up1"]])
        elif tag == "hgrn":
            update("hgrn_w_out", [slabs["hgrn_w_out"]])
            after = update("hgrn_w_in", [slabs["hgrn_w_in"]])
        elif tag == "attn_out":
            after = update("attn_w_out", [slabs["attn_w_out"]])
        else:
            after = update("attn_w_in", [slabs["attn_w_in"]])
    res = _adamw([slabs["small"]], stack_small(wts), stack_small(mom), stack_small(vel), name="adamw_small")
    for name in SMALL:
        out[name] = [r[0, SMALL_ROW[name]:SMALL_ROW[name] + 2] for r in res]
    loss = res[0][0, LOSS_ROW, 0]
    ng = hgrn_norm_g.shape[-1]
    ng_slabs = lax.dynamic_slice(slabs["small"], (0, NORM_G_ROW, me * ng), (N_DEV, 1, ng))
    out["hgrn_norm_g"] = [r[0] for r in _adamw([ng_slabs], hgrn_norm_g[None], m_hgrn_norm_g[None],
                                                v_hgrn_norm_g[None], name="adamw_norm_g")]
    order =("attn_w_in", "attn_w_out", "hgrn_w_in", "hgrn_w_out", "hgrn_norm_g", "lb_logits", "ln_mix_g", "ln_mix_b",
             "ln_ffn_g", "ln_ffn_b", "ffn_w_up", "ffn_w_down")
    return (loss, grad_x[None], *[out[nm][0] for nm in order], *[out[nm][1] for nm in order],
            *[out[nm][2] for nm in order], *[out[nm][3] for nm in order])
```

```python
import jax
import jax.numpy as jnp
from jax import lax
from jax.experimental import pallas as pl
from jax.experimental.pallas import tpu as pltpu

F32 = jnp.float32
BF16 = jnp.bfloat16

N_DEV = 8
LANES = 128
D_MODEL = 1024
ATTN_HEAD_DIM = 64
ATTN_HEADS = 16
ATTN_SCALE = ATTN_HEAD_DIM ** -0.5
ATTN_BLK = 128
DILATIONS = (1, 4, 16)
ROPE_THETA = 10000.0
HGRN_HEADS = 8
HGRN_CHUNK = 64
D_FF = 4096
LN_EPS = 1e-5
RMS_EPS = 1e-6
DEPTH = 2
ALPHA = (2 * DEPTH) ** 0.25
ADAM_LR, ADAM_B1, ADAM_B2, ADAM_EPS, ADAM_WD, ADAM_STEP = 0.001, 0.9, 0.999, 1e-08, 0.01, 10
VMEM_LIMIT = 48 * 1024 * 1024

_NT = (((1,), (1,)), ((), ()))
_TN = (((0,), (0,)), ((), ()))


def _dot(a, b):
    return jnp.dot(a, b, preferred_element_type=F32)


def _dot_nt(a, b):
    return lax.dot_general(a, b, _NT, preferred_element_type=F32)


def _dot_tn(a, b):
    return lax.dot_general(a, b, _TN, preferred_element_type=F32)


def _split3(x):
    p1 = x.astype(BF16)
    r1 = x - p1.astype(F32)
    p2 = r1.astype(BF16)
    p3 = (r1 - p2.astype(F32)).astype(BF16)
    return p1, p2, p3


def _exact_dot(sel3, x):
    return _dot(sel3, jnp.concatenate(_split3(x), axis=0))


def _exact_dot_r(x, sel3):
    return _dot(jnp.concatenate(_split3(x), axis=1), sel3)


def _params(sem=None):
    return pltpu.CompilerParams(dimension_semantics=sem, vmem_limit_bytes=VMEM_LIMIT)


def _mm(a, b, mode, *, name, m, n, k, tm=1024, tn=1024, tk=1024, out_dtypes=(F32,), epi=None, a_pre=None,
        tile_extras=(), row_extras=(), vec_extras=(), row_outs=(), vec_outs=0, a_split=1, b_split=1, out_split=1,
        b_col_off=0, b_k_off=0, out_col_off=0, out_cols=None, alias=None, epi_wants_j=False, deps=(), t_out=False):
    tm, tn, tk = min(tm, m), min(tn, n), min(tk, k)
    assert m % tm == 0 and n % tn == 0 and k % tk == 0, (name, m, n, k, tm, tn, tk)
    gm, gn, gk = m // tm, n // tn, k // tk
    if mode in ("nn", "nt"):
        if a_split > 1 and tk == k:
            a_spec = pl.BlockSpec((a_split, tm, k // a_split), lambda i, j, kk: (0, i, 0))
        elif a_split > 1:
            kc = (k // a_split) // tk
            a_spec = pl.BlockSpec((None, tm, tk), lambda i, j, kk: (kk // kc, i, kk % kc))
        else:
            a_spec = pl.BlockSpec((tm, tk), lambda i, j, kk: (i, kk))
    else:
        a_spec = pl.BlockSpec((tk, tm), lambda i, j, kk: (kk, i))
    if mode in ("nn", "tn"):
        if b_split > 1:
            nc = (n // b_split) // tn
            b_spec = pl.BlockSpec((None, tk, tn), lambda i, j, kk: (j // nc, kk, j % nc))
        else:
            b_spec = pl.BlockSpec((tk, tn), lambda i, j, kk: (kk + b_k_off, j + b_col_off))
    else:
        b_spec = pl.BlockSpec((tn, tk), lambda i, j, kk: (j + b_col_off, kk + b_k_off))
    if out_split > 1:
        nco = (n // out_split) // tn
        o_spec = pl.BlockSpec((None, tm, tn), lambda i, j, kk: (j // nco, i, j % nco))
        o_shape = (out_split, m, n // out_split)
    else:
        o_spec = pl.BlockSpec((tm, tn), lambda i, j, kk: (i, j + out_col_off))
        o_shape = (m, out_cols if out_cols is not None else n)
    n_ex = len(tile_extras) + len(row_extras) + len(vec_extras)
    n_out = len(out_dtypes)
    n_plain = n_out + len(row_outs)
    assert not vec_outs or gn == 1
    if epi is None:
        def epi(acc):
            return (acc,)
    dot = {"nn": _dot, "nt": _dot_nt, "tn": _dot_tn}[mode]

    def body(*refs):
        a_ref, b_ref = refs[0], refs[1]
        ex = refs[2:2 + n_ex]
        outs = refs[2 + n_ex + (1 if alias is not None else 0) + len(deps):][:n_plain + vec_outs + (1 if t_out else 0)]
        ii = pl.program_id(0)
        jj = pl.program_id(1)

        def product():
            if a_split > 1 and tk == k:
                av = jnp.concatenate([a_ref[p] for p in range(a_split)], axis=1)
            else:
                av = a_ref[...]
            if a_pre is not None:
                av = a_pre(av)
            return dot(av.astype(BF16), b_ref[...].astype(BF16))

        def finish(total):
            lead = (jj,) if epi_wants_j else ()
            res = epi(*lead, total, *[e[...] for e in ex])
            for o, r in zip(outs[:n_plain], res):
                o[...] = r.astype(o.dtype)
            for o, r in zip(outs[n_plain:n_plain + vec_outs], res[n_plain:]):
                @pl.when(ii == 0)
                def _(o=o, r=r):
                    o[...] = r

                @pl.when(ii > 0)
                def _(o=o, r=r):
                    o[...] += r
            if t_out:
                outs[-1][...] = res[-1].astype(outs[-1].dtype)

        if gk == 1:
            finish(product())
        else:
            acc = refs[-1]
            kk = pl.program_id(2)

            @pl.when(kk == 0)
            def _():
                acc[...] = product()

            @pl.when(kk > 0)
            def _():
                acc[...] += product()

            @pl.when(kk == gk - 1)
            def _():
                finish(acc[...])

    in_specs = [a_spec, b_spec] + [o_spec] * len(tile_extras)
    in_specs += [pl.BlockSpec((tm, r.shape[1]), lambda i, j, kk: (i, 0)) if r.shape[0] == m else
                 pl.BlockSpec((r.shape[0], tm), lambda i, j, kk: (0, i)) for r in row_extras]
    in_specs += [pl.BlockSpec((1, tn), lambda i, j, kk: (0, j))] * len(vec_extras)
    args = [a, b] + list(tile_extras) + list(row_extras) + list(vec_extras)
    io_alias = {}
    if alias is not None:
        in_specs.append(pl.BlockSpec(memory_space=pl.ANY))
        args.append(alias)
        io_alias = {len(args) - 1: 0}
    in_specs += [pl.BlockSpec(memory_space=pl.ANY)] * len(deps)
    args += list(deps)
    out_specs = [o_spec] * n_out
    out_shape = [jax.ShapeDtypeStruct(o_shape, dt) for dt in out_dtypes]
    for dt, w in row_outs:
        out_specs.append(pl.BlockSpec((tm, w), lambda i, j, kk: (i, 0)))
        out_shape.append(jax.ShapeDtypeStruct((m, w), dt))
    out_specs += [pl.BlockSpec((1, tn), lambda i, j, kk: (0, j))] * vec_outs
    out_shape += [jax.ShapeDtypeStruct((1, n), F32)] * vec_outs
    if t_out:
        assert out_split > 1
        out_specs.append(pl.BlockSpec((None, tn, tm), lambda i, j, kk: (j // nco, j % nco, i)))
        out_shape.append(jax.ShapeDtypeStruct((out_split, n // out_split, m), out_dtypes[0]))
    out = pl.pallas_call(
        body, name=name, grid=(gm, gn, gk), in_specs=in_specs, out_specs=out_specs, out_shape=out_shape,
        scratch_shapes=[pltpu.VMEM((tm, tn), F32)] if gk > 1 else [],
        input_output_aliases=io_alias,
        compiler_params=_params(("arbitrary" if vec_outs else "parallel", "parallel", "arbitrary")),
    )(*args)
    return out[0] if len(out) == 1 else out


def _rope_tables(seq, dil):
    half = ATTN_HEAD_DIM // 2
    row = jnp.arange(seq, dtype=jnp.int32)
    pos = (row % (seq // dil)) * dil + row // (seq // dil)
    lane = jnp.arange(LANES, dtype=jnp.int32)
    inv = ROPE_THETA ** (-(lane % half).astype(F32) * (2.0 / ATTN_HEAD_DIM))
    ang = pos.astype(F32)[:, None] * inv[None, :]
    sign = jnp.where(lane % ATTN_HEAD_DIM < half, -1.0, 1.0).astype(F32)
    return jnp.cos(ang), jnp.sin(ang) * sign[None, :]


def _rotate(x, c, ss, sign=1.0):
    w = x.shape[-1]
    half = ATTN_HEAD_DIM // 2
    lane = lax.broadcasted_iota(jnp.int32, x.shape, 1)
    first = (lane % ATTN_HEAD_DIM) < half
    partner = jnp.where(first, pltpu.roll(x, w - half, 1), pltpu.roll(x, half, 1))
    reps = w // LANES
    if reps > 1:
        c = jnp.concatenate([c] * reps, axis=1)
        ss = jnp.concatenate([ss] * reps, axis=1)
    return x * c + sign * (partner * ss)


def _rotate_t(xt, ct, sst):
    half = ATTN_HEAD_DIM // 2
    heads = xt.shape[0] // ATTN_HEAD_DIM
    parts = []
    for h in range(heads):
        lo = h * ATTN_HEAD_DIM
        parts += [xt[lo + half:lo + 2 * half], xt[lo:lo + half]]
    return (xt * jnp.concatenate([ct] * heads, axis=0)
            + jnp.concatenate(parts, axis=0) * jnp.concatenate([sst] * heads, axis=0))


def _ln_epi(acc, x, g, b, *prev):
    if prev:
        x = x * prev[0] + prev[1]
    u = ALPHA * x + acc
    mu = jnp.mean(u, axis=-1, keepdims=True)
    uc = u - mu
    var = jnp.mean(uc * uc, axis=-1, keepdims=True)
    rstd = lax.rsqrt(var + LN_EPS)
    xh = uc * rstd
    return xh * g + b, xh, rstd


def _ln_grad(dy, xh, rstd, g):
    dxh = dy * g
    m1 = jnp.mean(dxh, axis=-1, keepdims=True)
    m2 = jnp.mean(dxh * xh, axis=-1, keepdims=True)
    du = rstd * (dxh - m1 - xh * m2)
    return du, du, jnp.sum(dy * xh, axis=0, keepdims=True), jnp.sum(dy, axis=0, keepdims=True)


def _ln_grad_epi(acc, du_next, xh, rstd, g):
    return _ln_grad(acc + ALPHA * du_next, xh, rstd, g)


def _ln_loss_epi(acc, x_in, target, g, b, *prev):
    out, xh, rstd = _ln_epi(acc, x_in, g, b, *prev)
    e = out - target
    du, _, dg, db = _ln_grad(e * (1.0 / e.shape[-1]), xh, rstd, g)
    return du, du, dg, db, jnp.sum(e * e, axis=0, keepdims=True)


POS_BLK = 2048


def _class_rows(r, dil):
    return pl.ds(r, POS_BLK // dil, stride=dil) if dil > 1 else pl.ds(0, POS_BLK)


def _class_view(a, dil):
    s, w = a.shape
    return a.reshape(dil, s // dil, w)


def _class_spec(dil, all_tiles=True):
    if all_tiles:
        return pl.BlockSpec((dil, POS_BLK // dil, LANES), lambda i, t: (0, i, t))
    return pl.BlockSpec((dil, POS_BLK // dil, LANES), lambda i, t: (0, i, 0))


def _pos_spec(all_tiles=True):
    if all_tiles:
        return pl.BlockSpec((POS_BLK, LANES), lambda i, t: (i, t))
    return pl.BlockSpec((POS_BLK, LANES), lambda i, t: (i, 0))


def _prep_x(x, deps=()):
    s, d = x.shape

    def body(x_ref, *refs):
        outs = refs[len(deps):]
        for dil, o_ref in zip(DILATIONS, outs):
            for r in range(dil):
                o_ref[r] = x_ref[_class_rows(r, dil), :].astype(BF16)

    outs = pl.pallas_call(
        body, name="prep_x", grid=(s // POS_BLK, d // LANES),
        in_specs=[_pos_spec()] + [pl.BlockSpec(memory_space=pl.ANY)] * len(deps),
        out_specs=[_class_spec(dil) for dil in DILATIONS],
        out_shape=[jax.ShapeDtypeStruct((dil, s // dil, d), BF16) for dil in DILATIONS],
        compiler_params=_params(("parallel", "parallel")),
    )(x, *deps)
    return [o.reshape(s, d) for o in outs]


def _head_expand_matrix():
    h = lax.broadcasted_iota(jnp.int32, (LANES, D_MODEL), 0)
    l = lax.broadcasted_iota(jnp.int32, (LANES, D_MODEL), 1)
    return (l // ATTN_HEAD_DIM == h).astype(BF16)


def _attn_fwd(qkv, dil, *, name):
    _, s, d = qkv.shape
    nq = s // ATTN_BLK
    per = nq // dil
    tiles = d // LANES

    def body(q_ref, kc_ref, kp_ref, vc_ref, vp_ref, o_ref, lse_ref):
        qb = pl.program_id(0)
        first = (qb % per) == 0
        qi = lax.broadcasted_iota(jnp.int32, (ATTN_BLK, 2 * ATTN_BLK), 0)
        kj = lax.broadcasted_iota(jnp.int32, (ATTN_BLK, 2 * ATTN_BLK), 1)
        dist = qi + ATTN_BLK - kj
        valid = (dist >= 0) & (dist <= ATTN_BLK) & ((kj >= ATTN_BLK) | jnp.logical_not(first))
        lane = lax.broadcasted_iota(jnp.int32, (ATTN_BLK, LANES), 1)
        lse_tile = jnp.zeros((ATTN_BLK, LANES), F32)
        zero = jnp.zeros((), BF16)
        in_head = [(lane // ATTN_HEAD_DIM) == hh for hh in range(2)]
        for t0 in range(0, tiles, ATTN_FWD_TILE_GROUP):
            group = range(t0, t0 + ATTN_FWD_TILE_GROUP)
            heads = [(t, hh) for t in group for hh in range(2)]
            cols = {t: pl.ds(t * LANES, LANES) for t in group}
            k2 = {t: jnp.concatenate([kp_ref[:, cols[t]], kc_ref[:, cols[t]]], axis=0) for t in group}
            v2 = {t: jnp.concatenate([vp_ref[:, cols[t]], vc_ref[:, cols[t]]], axis=0) for t in group}
            sc = {(t, hh): jnp.where(valid, _dot_nt(jnp.where(in_head[hh], q_ref[:, cols[t]], zero), k2[t]),
                                     -jnp.inf) for t, hh in heads}
            mx = {i: jnp.max(sc[i], axis=-1, keepdims=True) for i in heads}
            p = {i: jnp.exp(sc[i] - mx[i]) for i in heads}
            l = {i: jnp.sum(p[i], axis=-1, keepdims=True) for i in heads}
            oh = {i: _dot(p[i].astype(BF16), v2[i[0]]) / l[i] for i in heads}
            for t in group:
                o_ref[:, cols[t]] = jnp.where(in_head[0], oh[t, 0], oh[t, 1])
                for hh in range(2):
                    lse_tile = jnp.where(lane == 2 * t + hh, mx[t, hh] + jnp.log(l[t, hh]), lse_tile)
        lse_ref[...] = lse_tile

    def blk(piece, prev):
        if prev:
            return pl.BlockSpec((None, ATTN_BLK, d), lambda i: (piece, jnp.maximum(i - 1, 0), 0))
        return pl.BlockSpec((None, ATTN_BLK, d), lambda i: (piece, i, 0))

    return pl.pallas_call(
        body, name=name, grid=(nq,),
        in_specs=[blk(0, False), blk(1, False), blk(1, True), blk(2, False), blk(2, True)],
        out_specs=[pl.BlockSpec((ATTN_BLK, d), lambda i: (i, 0)), pl.BlockSpec((ATTN_BLK, LANES), lambda i: (i, 0))],
        out_shape=[jax.ShapeDtypeStruct((s, d), F32), jax.ShapeDtypeStruct((s, LANES), F32)],
        compiler_params=_params(("parallel",)),
    )(qkv, qkv, qkv, qkv, qkv)


def _attn_combine(os_, lses):
    s, d = os_[0].shape
    sel = jnp.concatenate([_head_expand_matrix()] * 3, axis=0)

    def body(o0, o1, o2, l0, l1, l2, sel_ref, of_ref, ob_ref, lt_ref, o_pos, l_pos):
        for g, (dil, o_ref, l_ref) in enumerate(zip(DILATIONS, (o0, o1, o2), (l0, l1, l2))):
            for r in range(dil):
                o_pos[g, _class_rows(r, dil), :] = o_ref[r]
                l_pos[g, _class_rows(r, dil), :] = l_ref[r]
        la, lb_, lc = l_pos[0], l_pos[1], l_pos[2]
        mx = jnp.maximum(jnp.maximum(la, lb_), lc)
        es = (jnp.exp(la - mx), jnp.exp(lb_ - mx), jnp.exp(lc - mx))
        z = es[0] + es[1] + es[2]
        lt_ref[...] = mx + jnp.log(z)
        acc = jnp.zeros((POS_BLK, LANES), F32)
        for g in range(3):
            acc += _exact_dot_r(es[g] / z, sel_ref[...]) * o_pos[g]
        of_ref[...] = acc
        ob_ref[...] = acc.astype(BF16)

    return pl.pallas_call(
        body, name="attn_combine", grid=(s // POS_BLK, d // LANES),
        in_specs=[_class_spec(dil) for dil in DILATIONS] + [_class_spec(dil, False) for dil in DILATIONS]
        + [pl.BlockSpec((3 * LANES, LANES), lambda i, t: (0, t))],
        out_specs=[_pos_spec(), _pos_spec(), _pos_spec(False)],
        out_shape=[jax.ShapeDtypeStruct((s, d), F32), jax.ShapeDtypeStruct((s, d), BF16),
                   jax.ShapeDtypeStruct((s, LANES), F32)],
        scratch_shapes=[pltpu.VMEM((3, POS_BLK, LANES), F32), pltpu.VMEM((3, POS_BLK, LANES), F32)],
        compiler_params=_params(("parallel", "arbitrary")),
    )(*[_class_view(o, dil) for o, dil in zip(os_, DILATIONS)],
      *[_class_view(l, dil) for l, dil in zip(lses, DILATIONS)], sel)


ATTN_TILE_GROUP = 2
ATTN_FWD_TILE_GROUP = 4
AUX_PER_TILE = 12


def _aux_placement():
    h = lax.broadcasted_iota(jnp.int32, (6, LANES, LANES), 1)
    l = lax.broadcasted_iota(jnp.int32, (6, LANES, LANES), 2)
    j = lax.broadcasted_iota(jnp.int32, (6, LANES, LANES), 0)
    target = AUX_PER_TILE * (h // 2) + 3 * (h % 2) + jnp.where(j < 3, j, 3 + j)
    return ((l == target) & (h < ATTN_HEADS)).astype(BF16)


def _attn_bwd_prep(do, o, lse):
    s, d = do.shape
    tiles = d // LANES
    sel_t = jnp.tile(_head_expand_matrix().T.reshape(tiles, LANES, LANES), (1, 3, 1))

    def body(do_ref, o_ref, l_ref, sel_ref, place_ref, *refs):
        outs, delta, aux = refs[:9], refs[9], refs[10]
        t = pl.program_id(1)
        part = _exact_dot_r(do_ref[...] * o_ref[...], sel_ref[...])

        @pl.when(t == 0)
        def _():
            delta[...] = part

        @pl.when(t > 0)
        def _():
            delta[...] += part

        for g, dil in enumerate(DILATIONS):
            for r in range(dil):
                blk = do_ref[_class_rows(r, dil), :].astype(BF16)
                outs[g][r] = blk
                outs[3 + g][r] = blk.T

        @pl.when(t == tiles - 1)
        def _():
            pieces = _split3(l_ref[...]) + _split3(delta[...])
            aux[...] = _dot(jnp.concatenate(pieces, axis=1), place_ref[...])
            for g, dil in enumerate(DILATIONS):
                for r in range(dil):
                    outs[6 + g][r] = aux[_class_rows(r, dil), :].astype(BF16)

    outs = pl.pallas_call(
        body, name="attn_bwd_prep", grid=(s // POS_BLK, tiles),
        in_specs=[_pos_spec(), _pos_spec(), _pos_spec(False),
                  pl.BlockSpec((None, 3 * LANES, LANES), lambda i, t: (t, 0, 0)),
                  pl.BlockSpec((6 * LANES, LANES), lambda i, t: (0, 0))],
        out_specs=[_class_spec(dil) for dil in DILATIONS]
        + [pl.BlockSpec((dil, LANES, POS_BLK // dil), lambda i, t: (0, t, i)) for dil in DILATIONS]
        + [_class_spec(dil, False) for dil in DILATIONS],
        out_shape=[jax.ShapeDtypeStruct((dil, s // dil, d), BF16) for dil in DILATIONS]
        + [jax.ShapeDtypeStruct((dil, d, s // dil), BF16) for dil in DILATIONS]
        + [jax.ShapeDtypeStruct((dil, s // dil, LANES), BF16) for dil in DILATIONS],
        scratch_shapes=[pltpu.VMEM((POS_BLK, LANES), F32), pltpu.VMEM((POS_BLK, LANES), F32)],
        compiler_params=_params(("parallel", "arbitrary")),
    )(do, o, lse, sel_t, _aux_placement().reshape(6 * LANES, LANES))
    return ([a.reshape(s, d) for a in outs[0:3]], list(outs[3:6]), [a.reshape(s, LANES) for a in outs[6:9]])


def _attn_bwd(qkv, qkv_t, do, do_t, aux, tables, dil, *, name):
    _, s, d = qkv.shape
    nq = s // ATTN_BLK
    per = nq // dil
    tiles = d // LANES
    half = ATTN_HEAD_DIM

    def body(qd_ref, qo_ref, k_ref, qtd_ref, qto_ref, kt_ref, vt_ref, dod_ref, doo_ref, dotd_ref, doto_ref,
             auxd_ref, auxo_ref, c_ref, ss_ref, out_ref, carry):
        kb = pl.program_id(0)

        @pl.when(kb == 0)
        def _():
            carry[...] = jnp.zeros_like(carry)

        has_next = (kb % per) != (per - 1)
        qi = lax.broadcasted_iota(jnp.int32, (ATTN_BLK, 2 * ATTN_BLK), 0)
        kj = lax.broadcasted_iota(jnp.int32, (ATTN_BLK, 2 * ATTN_BLK), 1) % ATTN_BLK
        valid_d = kj <= qi
        valid_o = (kj >= qi) & has_next
        lane = lax.broadcasted_iota(jnp.int32, (ATTN_BLK, LANES), 1)
        row = lax.broadcasted_iota(jnp.int32, (LANES, ATTN_BLK), 0)
        side = lax.broadcasted_iota(jnp.int32, (LANES, 2 * ATTN_BLK), 0)
        first = lax.broadcasted_iota(jnp.int32, (LANES, 2 * ATTN_BLK), 1) < ATTN_BLK
        c, ss = c_ref[...], ss_ref[...]
        zero = jnp.zeros((), BF16)
        sides = ((qd_ref, qtd_ref, dod_ref, dotd_ref, auxd_ref[...], valid_d),
                 (qo_ref, qto_ref, doo_ref, doto_ref, auxo_ref[...], valid_o))
        both = (0, 1)

        def head_halves(x, index):
            axis = 0 if index is lane else 1
            return jnp.concatenate([jnp.where(index < half, x, zero), jnp.where(index >= half, x, zero)], axis=axis)

        for t0 in range(0, tiles, ATTN_TILE_GROUP):
            group = range(t0, t0 + ATTN_TILE_GROUP)
            cols = {t: pl.ds(t * LANES, LANES) for t in group}
            kk, kk_t, vv_t = {}, {}, {}
            for t in group:
                base = AUX_PER_TILE * t
                hit = lambda lo: ((first & (side >= base + lo) & (side < base + lo + 3))
                                  | (jnp.logical_not(first) & (side >= base + lo + 3) & (side < base + lo + 6)))
                kk[t] = head_halves(k_ref[:, cols[t]], lane)
                kk_t[t] = jnp.concatenate([head_halves(kt_ref[cols[t], :], row),
                                           jnp.where(hit(0), -1.0, 0.0).astype(BF16)], axis=0)
                vv_t[t] = jnp.concatenate([head_halves(vt_ref[cols[t], :], row),
                                           jnp.where(hit(6), -1.0, 0.0).astype(BF16)], axis=0)
            sc = {(t, w): _dot(jnp.concatenate([sides[w][0][:, cols[t]], sides[w][4]], axis=1), kk_t[t])
                  for t in group for w in both}
            dpd = {(t, w): _dot(jnp.concatenate([sides[w][2][:, cols[t]], sides[w][4]], axis=1), vv_t[t])
                   for t in group for w in both}
            p = {i: jnp.where(sides[i[1]][5], jnp.exp(sc[i]), 0.0) for i in sc}
            ds = {i: (p[i] * dpd[i]).astype(BF16) for i in sc}
            pb = {i: p[i].astype(BF16) for i in sc}
            dv_t = {t: sum(_dot(sides[w][3][cols[t], :], pb[t, w]) for w in both) for t in group}
            dk_t = {t: sum(_dot(sides[w][1][cols[t], :], ds[t, w]) for w in both) for t in group}
            dq = {i: _dot(ds[i], kk[i[0]]) * ATTN_SCALE for i in sc}
            for t in group:
                dq_now = carry[:, cols[t]] + dq[t, 0]
                carry[:, cols[t]] = dq[t, 1]
                dk = jnp.where(row < half, dk_t[t][:, :ATTN_BLK], dk_t[t][:, ATTN_BLK:]).T
                dv = jnp.where(row < half, dv_t[t][:, :ATTN_BLK], dv_t[t][:, ATTN_BLK:]).T
                out_ref[0, :, cols[t]] = _rotate(dq_now, c, ss, -1.0).astype(BF16)
                out_ref[1, :, cols[t]] = _rotate(dk, c, ss, -1.0).astype(BF16)
                out_ref[2, :, cols[t]] = dv.astype(BF16)

    def nxt(i):
        return jnp.minimum(i + 1, nq - 1)

    def piece(p, shift):
        if shift:
            return pl.BlockSpec((None, ATTN_BLK, d), lambda i: (p, nxt(i), 0))
        return pl.BlockSpec((None, ATTN_BLK, d), lambda i: (p, i, 0))

    def piece_t(p, shift):
        if shift:
            return pl.BlockSpec((None, d, ATTN_BLK), lambda i: (p, 0, nxt(i)))
        return pl.BlockSpec((None, d, ATTN_BLK), lambda i: (p, 0, i))

    def rows(width, shift):
        if shift:
            return pl.BlockSpec((ATTN_BLK, width), lambda i: (nxt(i), 0))
        return pl.BlockSpec((ATTN_BLK, width), lambda i: (i, 0))

    def do_t_spec(shift):
        if shift:
            return pl.BlockSpec((None, d, ATTN_BLK), lambda i: (nxt(i) // per, 0, nxt(i) % per))
        return pl.BlockSpec((None, d, ATTN_BLK), lambda i: (i // per, 0, i % per))

    return pl.pallas_call(
        body, name=name, grid=(nq,),
        in_specs=[piece(0, False), piece(0, True), piece(1, False),
                  piece_t(0, False), piece_t(0, True), piece_t(1, False), piece_t(2, False),
                  rows(d, False), rows(d, True), do_t_spec(False), do_t_spec(True),
                  rows(LANES, False), rows(LANES, True), rows(LANES, False), rows(LANES, False)],
        out_specs=pl.BlockSpec((3, ATTN_BLK, d), lambda i: (0, i, 0)),
        out_shape=jax.ShapeDtypeStruct((3, s, d), BF16),
        scratch_shapes=[pltpu.VMEM((ATTN_BLK, d), F32)],
        compiler_params=_params(("arbitrary",)),
    )(qkv, qkv, qkv, qkv_t, qkv_t, qkv_t, qkv_t, do, do, do_t, do_t, aux, aux, *tables)


def _dx_combine(du, parts):
    s, d = du.shape

    def body(du_ref, p0, p1, p2, out_ref):
        out_ref[...] = ALPHA * du_ref[...] + p0[0]
        for dil, p_ref in zip(DILATIONS[1:], (p1, p2)):
            for r in range(dil):
                out_ref[_class_rows(r, dil), :] += p_ref[r]

    return pl.pallas_call(
        body, name="dx_combine", grid=(s // POS_BLK, d // LANES),
        in_specs=[_pos_spec()] + [_class_spec(dil) for dil in DILATIONS], out_specs=_pos_spec(),
        out_shape=jax.ShapeDtypeStruct((s, d), F32),
        compiler_params=_params(("parallel", "parallel")),
    )(du, *[_class_view(p, dil) for p, dil in zip(parts, DILATIONS)])


HGRN_ROWS = 2048


def _tri(lower, copies=1):
    i = lax.broadcasted_iota(jnp.int32, (HGRN_CHUNK, HGRN_CHUNK * copies), 0)
    j = lax.broadcasted_iota(jnp.int32, (HGRN_CHUNK, HGRN_CHUNK * copies), 1) % HGRN_CHUNK
    return (j <= i) if lower else (j >= i)


def _hgrn_gates(qr, z, lb):
    sq = jax.nn.sigmoid(qr)
    e = jnp.exp(-jnp.abs(z))
    big = 1.0 / (1.0 + e)
    small = e * big
    sg = jnp.where(z >= 0, big, small)
    sn = jnp.where(z >= 0, small, big)
    return sq, qr * sq, sg, sn, lb + (1.0 - lb) * sg, (1.0 - lb) * sn


def _hgrn_fwd(p2, lb, norm_g):
    _, s, d = p2.shape
    nblk = s // HGRN_ROWS
    cps = HGRN_ROWS // HGRN_CHUNK

    def body(p_ref, lb_ref, g_ref, o_ref, on_ref, st_ref, state):
        @pl.when(pl.program_id(1) == 0)
        def _():
            state[...] = jnp.zeros_like(state)

        chunks = [pl.ds(c * HGRN_CHUNK, HGRN_CHUNK) for c in range(cps)]
        ltri = _tri(True)
        lsel = _tri(True, 3).astype(BF16)
        _, q, _, _, f, key = _hgrn_gates(p_ref[0], p_ref[1], lb_ref[...])
        lf = jnp.log(f)
        v = [p_ref[2, rows, :].astype(BF16) for rows in chunks]
        b = [_exact_dot(lsel, lf[c * HGRN_CHUNK:(c + 1) * HGRN_CHUNK]) for c in range(cps)]
        b_last = [bc[HGRN_CHUNK - 1:HGRN_CHUNK, :] for bc in b]
        q = [q[c * HGRN_CHUNK:(c + 1) * HGRN_CHUNK] for c in range(cps)]
        key = [key[c * HGRN_CHUNK:(c + 1) * HGRN_CHUNK] for c in range(cps)]
        qd = [(q[c] * jnp.exp(b[c])).astype(BF16) for c in range(cps)]
        kd = [(key[c] * jnp.exp(-b[c])).astype(BF16) for c in range(cps)]
        k2 = [(key[c] * jnp.exp(b_last[c] - b[c])).astype(BF16) for c in range(cps)]
        a = [jnp.where(ltri, _dot_nt(qd[c], kd[c]), 0.0).astype(BF16) for c in range(cps)]
        kv = [_dot_tn(v[c], k2[c]) for c in range(cps)]
        st, sts = state[...], []
        for c in range(cps):
            sts.append(st)
            st_ref[0, c] = st
            st = st * jnp.exp(b_last[c]) + kv[c]
        state[...] = st
        gv = g_ref[...]
        for c in range(cps):
            o = _dot(a[c], v[c]) + _dot_nt(qd[c], sts[c].astype(BF16))
            o_ref[chunks[c], :] = o
            r = lax.rsqrt(jnp.mean(o * o, axis=-1, keepdims=True) + RMS_EPS)
            on_ref[chunks[c], :] = (o * r * gv).astype(BF16)

    vec = pl.BlockSpec((1, LANES), lambda h, c: (0, h))
    col = pl.BlockSpec((HGRN_ROWS, LANES), lambda h, c: (c, h))
    return pl.pallas_call(
        body, name="hgrn_fwd", grid=(HGRN_HEADS, nblk),
        in_specs=[pl.BlockSpec((3, HGRN_ROWS, LANES), lambda h, c: (0, c, h)), vec, vec],
        out_specs=[col, col, pl.BlockSpec((1, cps, LANES, LANES), lambda h, c: (h, c, 0, 0))],
        out_shape=[jax.ShapeDtypeStruct((s, d), F32), jax.ShapeDtypeStruct((s, d), BF16),
                   jax.ShapeDtypeStruct((HGRN_HEADS, s // HGRN_CHUNK, LANES, LANES), F32)],
        scratch_shapes=[pltpu.VMEM((LANES, LANES), F32)],
        compiler_params=_params(("parallel", "arbitrary")),
    )(p2, lb, norm_g)


def _hgrn_bwd(p2, lb, norm_g, o_raw, states, dyn):
    _, s, d = p2.shape
    nblk = s // HGRN_ROWS
    cps = HGRN_ROWS // HGRN_CHUNK

    def body(p_ref, lb_ref, g_ref, o_ref, st_ref, dy_ref, dp_ref, dg_ref, dlb_ref, dstate):
        @pl.when(pl.program_id(1) == 0)
        def _():
            dstate[...] = jnp.zeros_like(dstate)
            dg_ref[...] = jnp.zeros_like(dg_ref)
            dlb_ref[...] = jnp.zeros_like(dlb_ref)

        n = cps
        cut = lambda t: [t[c * HGRN_CHUNK:(c + 1) * HGRN_CHUNK] for c in range(n)]
        chunks = [pl.ds(c * HGRN_CHUNK, HGRN_CHUNK) for c in range(n)]
        lbv = lb_ref[...]
        ltri = _tri(True)
        lsel = _tri(True, 3).astype(BF16)
        usel = _tri(False, 3).astype(BF16)
        last_row = lax.broadcasted_iota(jnp.int32, (HGRN_CHUNK, LANES), 0) == HGRN_CHUNK - 1
        qr, z = p_ref[0], p_ref[1]
        sq, q, sg, sn, f, key = _hgrn_gates(qr, z, lbv)
        lf = jnp.log(f)
        v = [p_ref[2, rows, :].astype(BF16) for rows in chunks]
        o, dyv = o_ref[...], dy_ref[...]
        r = lax.rsqrt(jnp.mean(o * o, axis=-1, keepdims=True) + RMS_EPS)
        oh = o * r
        doh = dyv * g_ref[...]
        do = cut((r * (doh - oh * jnp.mean(doh * oh, axis=-1, keepdims=True))).astype(BF16))
        dg_ref[...] += jnp.sum(dyv * oh, axis=0, keepdims=True)
        b = [_exact_dot(lsel, lfc) for lfc in cut(lf)]
        b_last = [bc[HGRN_CHUNK - 1:HGRN_CHUNK, :] for bc in b]
        q, key = cut(q), cut(key)
        eb = [jnp.exp(bc) for bc in b]
        enb = [jnp.exp(-bc) for bc in b]
        e2 = [jnp.exp(b_last[c] - b[c]) for c in range(n)]
        dec = [jnp.exp(bl) for bl in b_last]
        qd_f = [q[c] * eb[c] for c in range(n)]
        kd_f = [key[c] * enb[c] for c in range(n)]
        k2_f = [key[c] * e2[c] for c in range(n)]
        qd, kd, k2 = ([t.astype(BF16) for t in ts] for ts in (qd_f, kd_f, k2_f))
        a = [jnp.where(ltri, _dot_nt(qd[c], kd[c]), 0.0).astype(BF16) for c in range(n)]
        da = [jnp.where(ltri, _dot_nt(do[c], v[c]), 0.0).astype(BF16) for c in range(n)]
        grow = [_dot_tn(do[c], qd[c]) for c in range(n)]
        dst, dsts = dstate[...], [None] * n
        for c in reversed(range(n)):
            dsts[c] = dst
            dst = dst * dec[c] + grow[c]
        dstate[...] = dst
        st = [st_ref[0, c] for c in range(n)]
        dstb = [t.astype(BF16) for t in dsts]
        dv = [_dot_tn(a[c], do[c]) + _dot_nt(k2[c], dstb[c]) for c in range(n)]
        dqd = [_dot(da[c], kd[c]) + _dot(do[c], st[c].astype(BF16)) for c in range(n)]
        dkd = [_dot_tn(da[c], qd[c]) for c in range(n)]
        dk2 = [_dot(v[c], dstb[c]) for c in range(n)]
        db = []
        for c in range(n):
            ddec = jnp.sum(dsts[c] * st[c], axis=0, keepdims=True)
            db_last = jnp.sum(dk2[c] * k2_f[c], axis=0, keepdims=True) + ddec * dec[c]
            db.append(dqd[c] * qd_f[c] - dkd[c] * kd_f[c] - dk2[c] * k2_f[c] + jnp.where(last_row, db_last, 0.0))
        dlf = [_exact_dot(usel, dbc) for dbc in db]
        f, sg, sn, sq, qr = cut(f), cut(sg), cut(sn), cut(sq), cut(qr)
        dlb_acc = jnp.zeros((1, LANES), F32)
        for c in range(n):
            dkey = dkd[c] * enb[c] + dk2[c] * e2[c]
            common = dlf[c] / f[c] - dkey
            dlb_acc += jnp.sum(common * sn[c], axis=0, keepdims=True)
            dp_ref[0, chunks[c], :] = (dqd[c] * eb[c] * (sq[c] * (1.0 + qr[c] * (1.0 - sq[c])))).astype(BF16)
            dp_ref[1, chunks[c], :] = (common * ((1.0 - lbv) * sg[c] * sn[c])).astype(BF16)
            dp_ref[2, chunks[c], :] = dv[c].astype(BF16)
        dlb_ref[...] += dlb_acc

    def rev(c):
        return nblk - 1 - c

    vec = pl.BlockSpec((1, LANES), lambda h, c: (0, h))
    col = pl.BlockSpec((HGRN_ROWS, LANES), lambda h, c: (rev(c), h))
    p3 = pl.BlockSpec((3, HGRN_ROWS, LANES), lambda h, c: (0, rev(c), h))
    return pl.pallas_call(
        body, name="hgrn_bwd", grid=(HGRN_HEADS, nblk),
        in_specs=[p3, vec, vec, col, pl.BlockSpec((1, cps, LANES, LANES), lambda h, c: (h, rev(c), 0, 0)), col],
        out_specs=[p3, vec, vec],
        out_shape=[jax.ShapeDtypeStruct((3, s, d), BF16), jax.ShapeDtypeStruct((1, d), F32),
                   jax.ShapeDtypeStruct((1, d), F32)],
        scratch_shapes=[pltpu.VMEM((LANES, LANES), F32)],
        compiler_params=_params(("parallel", "arbitrary")),
    )(p2, lb, norm_g, o_raw, states, dyn)


SMALL_ROWS = 16
FUSED_TM = 512


def _relu2(h):
    r = jnp.maximum(h.astype(F32), 0.0)
    return r * r


def _dact_epi(acc, h):
    return (acc * (2.0 * jnp.maximum(h.astype(F32), 0.0)),)


def _rope_epi(j, acc, ct, sst):
    acc_t = acc.T
    rot_t = _rotate_t(acc_t, ct, sst) * jnp.where(j == 0, ATTN_SCALE, 1.0)
    out_t = jnp.where(j >= 2, acc_t, rot_t).astype(BF16)
    return out_t.T, out_t


def _local_step(x, target, get_w, small, on_grads, deps=()):
    s, d = x.shape
    lb = small["lb"]
    gq = 3 * d
    gt = gq // 1024

    def ln_vecs(which, layer):
        return small[f"ln_{which}_g"][layer:layer + 1], small[f"ln_{which}_b"][layer:layer + 1]

    def ln_of(a, w, x_in, which, layer, *, name, k, a_pre=None, deps=(), prev=None):
        vecs = ln_vecs(which, layer) + (ln_vecs(*prev) if prev else ())
        return _mm(a, w, "nn", name=name, m=s, n=d, k=k, tm=FUSED_TM, tk=k, a_pre=a_pre, out_dtypes=(BF16, F32),
                   row_outs=((F32, 1),), epi=_ln_epi, tile_extras=(x_in,), vec_extras=vecs, deps=deps)

    def ln_grad_of(a, w, du_next, xh, rstd, which, layer, *, name, k, a_split=1, deps=()):
        return _mm(a, w, "nt", name=name, m=s, n=d, k=k, tm=FUSED_TM, tk=k, a_split=a_split,
                   out_dtypes=(F32, BF16), vec_outs=2, epi=_ln_grad_epi, tile_extras=(du_next, xh), row_extras=(rstd,),
                   vec_extras=(small[f"ln_{which}_g"][layer:layer + 1],), deps=deps)

    def ffn_fwd(xb, xh_in, w_up, w_down, tag):
        h = _mm(xb, w_up, "nn", name=f"ffn_up_{tag}", m=s, n=D_FF, k=d, tm=2048, out_dtypes=(BF16,))
        return (h,) + tuple(ln_of(h, w_down, xh_in, "ffn", tag, name=f"ffn_down_{tag}", k=D_FF, a_pre=_relu2,
                                  prev=("mix", tag)))

    def ffn_bwd(du, dub, xb, h, w_up, w_down, xh, rstd, tag):
        dh = _mm(dub, w_down, "nt", name=f"ffn_dact_{tag}", m=s, n=D_FF, k=d, tm=2048, out_dtypes=(BF16,), epi=_dact_epi,
                 tile_extras=(h,))
        g_down = _mm(h, dub, "tn", name=f"ffn_gdown_{tag}", m=D_FF, n=d, k=s, tm=512, tk=s, a_pre=_relu2,
                     out_dtypes=(BF16,))
        g_up = _mm(xb, dh, "tn", name=f"ffn_gup_{tag}", m=d, n=D_FF, k=s, tk=s, out_dtypes=(BF16,))
        after = on_grads(f"ffn{tag}", {f"ffn_w_down{tag}": g_down, f"ffn_w_up{tag}": g_up})
        return ln_grad_of(dh, w_up, du, xh, rstd, "mix", tag, name=f"ffn_dx_{tag}", k=D_FF, deps=after)

    xs = _prep_x(x, deps)
    tabs = [_rope_tables(s, dil) for dil in DILATIONS]
    tabs_t = [[t[:, :ATTN_HEAD_DIM].T for t in tab] for tab in tabs]
    w_ain, after = get_w("attn_w_in", xs + [t for tab in tabs + tabs_t for t in tab])
    qkvs, qkv_ts, o_parts, lse_parts = [], [], [], []
    for g, dil in enumerate(DILATIONS):
        qkv, qkv_t = _mm(xs[g], w_ain, "nn", name=f"attn_in_{g}", m=s, n=gq, k=d, b_col_off=g * gt, out_split=3,
                         out_dtypes=(BF16,), epi=_rope_epi, epi_wants_j=True, row_extras=tabs_t[g], t_out=True,
                         deps=after if g == 0 else ())
        qkvs.append(qkv)
        qkv_ts.append(qkv_t)
        o_g, lse_g = _attn_fwd(qkv, dil, name=f"attn_fwd_{g}")
        o_parts.append(o_g)
        lse_parts.append(lse_g)
    o_f, o_b, lse_t = _attn_combine(o_parts, lse_parts)
    w_aout, after = get_w("attn_w_out", o_b)
    x1b, xh1, r1 = ln_of(o_b, w_aout, x, "mix", 0, name="attn_out", k=d, deps=after)
    w_up0, w_down0 = get_w("ffn_w_up0", o_b)[0], get_w("ffn_w_down0", o_b)[0]
    h0, x2b, xh2, r2 = ffn_fwd(x1b, xh1, w_up0, w_down0, 0)
    w_hin, w_hout, norm_g = get_w("hgrn_w_in", x2b)[0], get_w("hgrn_w_out", x2b)[0], get_w("hgrn_norm_g", x2b)[0]
    p2 = _mm(x2b, w_hin, "nn", name="hgrn_in", m=s, n=3 * d, k=d, tm=2048, out_split=3)
    o_raw, o_n, states = _hgrn_fwd(p2, lb, norm_g)
    w_up1, w_down1 = get_w("ffn_w_up1", o_n)[0], get_w("ffn_w_down1", o_n)[0]
    x3b, xh3, r3 = ln_of(o_n, w_hout, xh2, "mix", 1, name="hgrn_out", k=d, prev=("ffn", 0))
    h1 = _mm(x3b, w_up1, "nn", name="ffn_up_1", m=s, n=D_FF, k=d, tm=2048, out_dtypes=(BF16,))
    du, dub, dg_ffn1, db_ffn1, sq = _mm(
        h1, w_down1, "nn", name="ffn_down_1", m=s, n=d, k=D_FF, tm=FUSED_TM, tk=D_FF, a_pre=_relu2,
        out_dtypes=(F32, BF16), vec_outs=3, epi=_ln_loss_epi, tile_extras=(xh3, target),
        vec_extras=ln_vecs("ffn", 1) + ln_vecs("mix", 1))
    du, dub, dg_mix1, db_mix1 = ffn_bwd(du, dub, x3b, h1, w_up1, w_down1, xh3, r3, 1)
    dyn = _mm(dub, w_hout, "nt", name="hgrn_dout", m=s, n=d, k=d)
    g_hout = _mm(o_n, dub, "tn", name="hgrn_gout", m=d, n=d, k=s, out_dtypes=(BF16,))
    dp2, d_norm_g, d_lb = _hgrn_bwd(p2, lb, norm_g, o_raw, states, dyn)
    g_hin = _mm(x2b, dp2, "tn", name="hgrn_gin", m=d, n=3 * d, k=s, tk=s, b_split=3, out_dtypes=(BF16,))
    after = on_grads("hgrn", {"hgrn_w_out": g_hout, "hgrn_w_in": g_hin})
    du, dub, dg_ffn0, db_ffn0 = ln_grad_of(dp2, w_hin, du, xh2, r2, "ffn", 0, name="hgrn_dx", k=3 * d, a_split=3,
                                           deps=after)
    du, dub, dg_mix0, db_mix0 = ffn_bwd(du, dub, x1b, h0, w_up0, w_down0, xh1, r1, 0)
    g_aout = _mm(o_b, dub, "tn", name="attn_gout", m=d, n=d, k=s, out_dtypes=(BF16,))
    sm1 = jax.nn.softmax(small["lb_logits"], axis=0)
    d_l1 = d_lb * (sm1[0:1] * sm1[1:2])
    zeros = jnp.zeros((SMALL_ROWS - 12, d), F32)
    loss_row = jnp.broadcast_to(0.5 * jnp.sum(sq) / d, (1, d))
    small_grads = jnp.concatenate([dg_mix0, dg_mix1, db_mix0, db_mix1, dg_ffn0, dg_ffn1, db_ffn0, db_ffn1,
                                   -d_l1, d_l1, d_norm_g, loss_row, zeros], axis=0)
    after = on_grads("attn_out", {"attn_w_out": g_aout, "small": small_grads})
    do = _mm(dub, w_aout, "nt", name="attn_dout", m=s, n=d, k=d, deps=after)
    do_parts, do_ts, aux_parts = _attn_bwd_prep(do, o_f, lse_t)
    g_ain, dqkvs = None, []
    for g, dil in enumerate(DILATIONS):
        dqkvs.append(_attn_bwd(qkvs[g], qkv_ts[g], do_parts[g], do_ts[g], aux_parts[g], tabs[g], dil,
                               name=f"attn_bwd_{g}"))
        g_ain = _mm(xs[g], dqkvs[g], "tn", name=f"attn_gin_{g}", m=d, n=gq, k=s, tk=s, b_split=3, out_dtypes=(BF16,),
                    out_col_off=g * gt, out_cols=3 * gq, alias=g_ain)
    after = on_grads("attn_in", {"attn_w_in": g_ain})
    dx_parts = [_mm(dqkvs[g], w_ain, "nt", name=f"attn_dx_{g}", m=s, n=d, k=gq, tm=FUSED_TM, tk=gq, a_split=3, b_k_off=g,
                    deps=after if g == 0 else ())
                for g in range(len(DILATIONS))]
    return _dx_combine(du, dx_parts)


def _mesh_place():
    x, y, c = lax.axis_index("x"), lax.axis_index("y"), lax.axis_index("c")
    return x, y, c, 4 * x + 2 * y + c


def _peer(x, y, c, k):
    px = 1 - x if (k >> 2) & 1 else x
    py = 1 - y if (k >> 1) & 1 else y
    pc = 1 - c if k & 1 else c
    return (px, py, pc), 4 * px + 2 * py + pc


def _window(ref, axis, size, idx):
    if axis is None:
        return ref
    sl = [slice(None)] * len(ref.shape)
    sl[axis] = pl.ds(idx * size, size)
    return ref.at[tuple(sl)]


_HBM = pl.BlockSpec(memory_space=pltpu.HBM)
_SEM = pl.BlockSpec(memory_space=pltpu.SEMAPHORE)
_EFFECT = pltpu.SideEffectType.DATAFLOW_SIDE_EFFECTING


def _xchg_ends(src_ref, land_ref, axis, gather, me, other):
    if gather:
        size = src_ref.shape[axis]
        return src_ref, _window(land_ref, axis, size, me), _window(land_ref, axis, size, other)
    size = None if axis is None else src_ref.shape[axis] // N_DEV
    return _window(src_ref, axis, size, other), land_ref.at[me], land_ref.at[other]


def _xchg_start(srcs, lands, axes, *, gather, name, deps=(), peers=tuple(range(1, N_DEV))):
    n = len(srcs)
    nd = len(deps)

    def body(*refs):
        src_refs, land_refs = refs[:n], refs[n:2 * n]
        send, recv = refs[2 * n + nd:3 * n + nd], refs[3 * n + nd:4 * n + nd]
        token = refs[-1]
        x, y, c, me = _mesh_place()
        for k in peers:
            peer, pidx = _peer(x, y, c, k)
            for i in range(n):
                src, dst, _ = _xchg_ends(src_refs[i], land_refs[i], axes[i], gather, me, pidx)
                pltpu.make_async_remote_copy(
                    src_ref=src, dst_ref=dst, send_sem=send[i].at[k - 1], recv_sem=recv[i].at[k - 1],
                    device_id=peer, device_id_type=pl.DeviceIdType.MESH).start()
        for i in range(n):
            src, dst, _ = _xchg_ends(src_refs[i], land_refs[i], axes[i], gather, me, me)
            pltpu.make_async_copy(src, dst, send[i].at[N_DEV - 1]).start()
        token[...] = jnp.zeros_like(token)

    bufs = list(srcs) + list(lands)
    outs = pl.pallas_call(
        body, name=name,
        out_shape=[pltpu.SemaphoreType.DMA((N_DEV,))] * (2 * n) + [pltpu.HBM(b.shape, b.dtype) for b in bufs]
        + [jax.ShapeDtypeStruct((8, LANES), F32)],
        in_specs=[_HBM] * (2 * n) + [pl.BlockSpec(memory_space=pl.ANY)] * nd,
        out_specs=[_SEM] * (2 * n) + [_HBM] * (2 * n) + [pl.BlockSpec(memory_space=pltpu.VMEM)],
        input_output_aliases={i: 2 * n + i for i in range(2 * n)},
        compiler_params=pltpu.CompilerParams(has_side_effects=_EFFECT),
    )(*[pltpu.with_memory_space_constraint(b, pltpu.HBM) for b in bufs], *deps)
    return dict(send=outs[:n], recv=outs[n:2 * n], srcs=outs[2 * n:3 * n], lands=outs[3 * n:4 * n], token=outs[-1],
                axes=list(axes), gather=gather)


SIBLING = 1
SAME_CORE = (2, 4, 6)


def _gather_relay(xc, after, *, name):
    axis = xc["axes"][0]

    def body(src_ref, land_ref, send1, recv1, *rest):
        send2, recv2 = rest[-2:]
        x, y, c, me = _mesh_place()
        sibling, _ = _peer(x, y, c, SIBLING)
        for j, k in enumerate(SAME_CORE):
            peer, pidx = _peer(x, y, c, k)
            src, _, got = _xchg_ends(src_ref, land_ref, axis, True, me, pidx)
            pltpu.make_async_remote_copy(
                src_ref=src, dst_ref=got, send_sem=send1.at[k - 1], recv_sem=recv1.at[k - 1],
                device_id=peer, device_id_type=pl.DeviceIdType.MESH).wait_recv()
            pltpu.make_async_remote_copy(
                src_ref=got, dst_ref=got, send_sem=send2.at[j], recv_sem=recv2.at[j],
                device_id=sibling, device_id_type=pl.DeviceIdType.MESH).start()

    src, land = xc["srcs"][0], xc["lands"][0]
    outs = pl.pallas_call(
        body, name=name,
        out_shape=[pltpu.HBM(src.shape, src.dtype), pltpu.HBM(land.shape, land.dtype)]
        + [pltpu.SemaphoreType.DMA((len(SAME_CORE),))] * 2,
        in_specs=[_HBM, _HBM, _SEM, _SEM] + [pl.BlockSpec(memory_space=pl.ANY)] * len(after),
        out_specs=[_HBM, _HBM, _SEM, _SEM], input_output_aliases={0: 0, 1: 1},
        compiler_params=pltpu.CompilerParams(has_side_effects=_EFFECT),
    )(src, land, xc["send"][0], xc["recv"][0], *after)
    return dict(src=outs[0], land=outs[1], send=outs[2], recv=outs[3])


def _gather_relay_wait(xc, relay, after, *, name):
    axis = xc["axes"][0]

    def body(src_ref, land_ref, send1, recv1, send2, recv2, after_ref, src_out, land_out):
        x, y, c, me = _mesh_place()
        sibling, sidx = _peer(x, y, c, SIBLING)
        for k in (SIBLING,) + SAME_CORE:
            peer, pidx = _peer(x, y, c, k)
            src, dst, got = _xchg_ends(src_ref, land_ref, axis, True, me, pidx)
            pltpu.make_async_remote_copy(
                src_ref=src, dst_ref=dst, send_sem=send1.at[k - 1], recv_sem=recv1.at[k - 1],
                device_id=peer, device_id_type=pl.DeviceIdType.MESH).wait_send()
        src, dst, got = _xchg_ends(src_ref, land_ref, axis, True, me, sidx)
        pltpu.make_async_remote_copy(
            src_ref=src, dst_ref=got, send_sem=send1.at[SIBLING - 1], recv_sem=recv1.at[SIBLING - 1],
            device_id=sibling, device_id_type=pl.DeviceIdType.MESH).wait_recv()
        src, dst, _ = _xchg_ends(src_ref, land_ref, axis, True, me, me)
        pltpu.make_async_copy(src, dst, send1.at[N_DEV - 1]).wait()
        for j, k in enumerate(SAME_CORE):
            _, pidx = _peer(x, y, c, k)
            _, qidx = _peer(x, y, c, k ^ SIBLING)
            _, _, sent = _xchg_ends(src_ref, land_ref, axis, True, me, pidx)
            _, _, got = _xchg_ends(src_ref, land_ref, axis, True, me, qidx)
            pltpu.make_async_remote_copy(
                src_ref=sent, dst_ref=sent, send_sem=send2.at[j], recv_sem=recv2.at[j],
                device_id=sibling, device_id_type=pl.DeviceIdType.MESH).wait_send()
            pltpu.make_async_remote_copy(
                src_ref=got, dst_ref=got, send_sem=send2.at[j], recv_sem=recv2.at[j],
                device_id=sibling, device_id_type=pl.DeviceIdType.MESH).wait_recv()

    outs = pl.pallas_call(
        body, name=name, out_shape=[pltpu.HBM(relay["src"].shape, relay["src"].dtype),
                                    pltpu.HBM(relay["land"].shape, relay["land"].dtype)],
        in_specs=[_HBM, _HBM, _SEM, _SEM, _SEM, _SEM, pl.BlockSpec(memory_space=pl.ANY)],
        out_specs=[_HBM, _HBM], input_output_aliases={0: 0, 1: 1},
        compiler_params=pltpu.CompilerParams(has_side_effects=_EFFECT),
    )(relay["src"], relay["land"], xc["send"][0], xc["recv"][0], relay["send"], relay["recv"], after)
    return outs[1]


def _xchg_wait(xc, items, after, *, name):
    m = len(items)
    gather = xc["gather"]
    axes = [xc["axes"][i] for i in items]

    def body(*refs):
        src_refs, land_refs = refs[:m], refs[m:2 * m]
        send, recv = refs[2 * m:3 * m], refs[3 * m:4 * m]
        x, y, c, me = _mesh_place()
        for k in range(1, N_DEV):
            peer, pidx = _peer(x, y, c, k)
            for j in range(m):
                src, dst, got = _xchg_ends(src_refs[j], land_refs[j], axes[j], gather, me, pidx)
                pltpu.make_async_remote_copy(
                    src_ref=src, dst_ref=dst, send_sem=send[j].at[k - 1], recv_sem=recv[j].at[k - 1],
                    device_id=peer, device_id_type=pl.DeviceIdType.MESH).wait_send()
                pltpu.make_async_remote_copy(
                    src_ref=src, dst_ref=got, send_sem=send[j].at[k - 1], recv_sem=recv[j].at[k - 1],
                    device_id=peer, device_id_type=pl.DeviceIdType.MESH).wait_recv()
        for j in range(m):
            src, dst, _ = _xchg_ends(src_refs[j], land_refs[j], axes[j], gather, me, me)
            pltpu.make_async_copy(src, dst, send[j].at[N_DEV - 1]).wait()

    bufs = [xc["srcs"][i] for i in items] + [xc["lands"][i] for i in items]
    sems = [xc["send"][i] for i in items] + [xc["recv"][i] for i in items]
    outs = pl.pallas_call(
        body, name=name, out_shape=[pltpu.HBM(b.shape, b.dtype) for b in bufs],
        in_specs=[_HBM] * (2 * m) + [_SEM] * (2 * m) + [pl.BlockSpec(memory_space=pl.ANY)] * len(after),
        out_specs=[_HBM] * (2 * m), input_output_aliases={j: j for j in range(2 * m)},
        compiler_params=pltpu.CompilerParams(has_side_effects=_EFFECT),
    )(*bufs, *sems, *after)
    return outs[m:]


def _cast_bf16(a, *, name):
    r, c = a.shape
    tr = min(r, 512)

    def body(a_ref, o_ref):
        o_ref[...] = a_ref[...].astype(BF16)

    spec = pl.BlockSpec((tr, c), lambda i: (i, 0))
    return pl.pallas_call(body, name=name, grid=(r // tr,), in_specs=[spec], out_specs=spec,
                          out_shape=jax.ShapeDtypeStruct((r, c), BF16), compiler_params=_params(("parallel",)))(a)


def _adamw(slabs, w, m, v, *, name):
    layers, r, c = w.shape
    tr = min(r, 256)

    def body(*refs):
        s_refs = refs[:layers]
        w_ref, m_ref, v_ref, g_ref, d_ref, mo_ref, vo_ref = refs[layers:]
        for l in range(layers):
            g = s_refs[l][0].astype(F32)
            for i in range(1, N_DEV):
                g = g + s_refs[l][i].astype(F32)
            m2 = ADAM_B1 * m_ref[l] + (1.0 - ADAM_B1) * g
            v2 = ADAM_B2 * v_ref[l] + (1.0 - ADAM_B2) * (g * g)
            m_hat = m2 / (1.0 - ADAM_B1 ** ADAM_STEP)
            v_hat = v2 / (1.0 - ADAM_B2 ** ADAM_STEP)
            g_ref[l] = g
            d_ref[l] = -ADAM_LR * (m_hat / (jnp.sqrt(v_hat) + ADAM_EPS) + ADAM_WD * w_ref[l])
            mo_ref[l] = m2
            vo_ref[l] = v2

    spec = pl.BlockSpec((layers, tr, c), lambda i: (0, i, 0))
    return pl.pallas_call(
        body, name=name, grid=(r // tr,),
        in_specs=[pl.BlockSpec((N_DEV, tr, c), lambda i: (0, i, 0))] * layers + [spec, spec, spec],
        out_specs=[spec] * 4, out_shape=[jax.ShapeDtypeStruct((layers, r, c), F32)] * 4,
        compiler_params=_params(("parallel",)),
    )(*slabs, w, m, v)


GATHER_AXIS = {"attn_w_in": 1, "attn_w_out": 0, "ffn_w_up0": 1, "ffn_w_down0": 0, "hgrn_w_in": 1, "hgrn_w_out": 0,
               "hgrn_norm_g": 1, "ffn_w_up1": 1, "ffn_w_down1": 0}
GATHER_STAGES = (("attn_w_in",), ("attn_w_out", "ffn_w_up0", "ffn_w_down0", "hgrn_w_in", "hgrn_w_out", "hgrn_norm_g"),
                 ("ffn_w_up1", "ffn_w_down1"))
GATHER_WAITS = ((("attn_w_in",), 0, 1), (("attn_w_out", "ffn_w_up0", "ffn_w_down0"), 1, 2),
                (("hgrn_w_in", "hgrn_w_out", "hgrn_norm_g"), 1, None), (("ffn_w_up1", "ffn_w_down1"), 2, None))
SCATTER_AXIS = dict(GATHER_AXIS, small=None)
BIG = ("attn_w_in", "attn_w_out", "hgrn_w_in", "hgrn_w_out", "ffn_w_up", "ffn_w_down")
SMALL = ("lb_logits", "ln_mix_g", "ln_mix_b", "ln_ffn_g", "ln_ffn_b")
SMALL_ROW = {"ln_mix_g": 0, "ln_mix_b": 2, "ln_ffn_g": 4, "ln_ffn_b": 6, "lb_logits": 8}
NORM_G_ROW = 10
LOSS_ROW = 11


def kernel(x, attn_w_in, attn_w_out, hgrn_w_in, hgrn_w_out, hgrn_norm_g, lb_logits, ln_mix_g, ln_mix_b, ln_ffn_g, ln_ffn_b, ffn_w_up, ffn_w_down, loss_target, m_attn_w_in, m_attn_w_out, m_hgrn_w_in, m_hgrn_w_out, m_hgrn_norm_g, m_lb_logits, m_ln_mix_g, m_ln_mix_b, m_ln_ffn_g, m_ln_ffn_b, m_ffn_w_up, m_ffn_w_down, v_attn_w_in, v_attn_w_out, v_hgrn_w_in, v_hgrn_w_out, v_hgrn_norm_g, v_lb_logits, v_ln_mix_g, v_ln_mix_b, v_ln_ffn_g, v_ln_ffn_b, v_ffn_w_up, v_ffn_w_down):
    wts = dict(attn_w_in=attn_w_in, attn_w_out=attn_w_out, hgrn_w_in=hgrn_w_in, hgrn_w_out=hgrn_w_out,
               hgrn_norm_g=hgrn_norm_g, lb_logits=lb_logits, ln_mix_g=ln_mix_g, ln_mix_b=ln_mix_b, ln_ffn_g=ln_ffn_g,
               ln_ffn_b=ln_ffn_b, ffn_w_up=ffn_w_up, ffn_w_down=ffn_w_down)
    mom = dict(attn_w_in=m_attn_w_in, attn_w_out=m_attn_w_out, hgrn_w_in=m_hgrn_w_in, hgrn_w_out=m_hgrn_w_out,
               hgrn_norm_g=m_hgrn_norm_g, lb_logits=m_lb_logits, ln_mix_g=m_ln_mix_g, ln_mix_b=m_ln_mix_b,
               ln_ffn_g=m_ln_ffn_g, ln_ffn_b=m_ln_ffn_b, ffn_w_up=m_ffn_w_up, ffn_w_down=m_ffn_w_down)
    vel = dict(attn_w_in=v_attn_w_in, attn_w_out=v_attn_w_out, hgrn_w_in=v_hgrn_w_in, hgrn_w_out=v_hgrn_w_out,
               hgrn_norm_g=v_hgrn_norm_g, lb_logits=v_lb_logits, ln_mix_g=v_ln_mix_g, ln_mix_b=v_ln_mix_b,
               ln_ffn_g=v_ln_ffn_g, ln_ffn_b=v_ln_ffn_b, ffn_w_up=v_ffn_w_up, ffn_w_down=v_ffn_w_down)
    me = 4 * lax.axis_index("x") + 2 * lax.axis_index("y") + lax.axis_index("c")

    src = {"attn_w_in": attn_w_in[0], "attn_w_out": attn_w_out[0], "hgrn_w_in": hgrn_w_in[0], "hgrn_w_out": hgrn_w_out[0],
           "ffn_w_up0": ffn_w_up[0], "ffn_w_down0": ffn_w_down[0], "ffn_w_up1": ffn_w_up[1], "ffn_w_down1": ffn_w_down[1]}
    gathers, got = {}, {}
    casts = {nm: _cast_bf16(a, name=f"cast_{nm}") for nm, a in src.items()}
    casts["hgrn_norm_g"] = hgrn_norm_g

    def start_gather(stage, deps):
        shards, lands = [], []
        for nm in GATHER_STAGES[stage]:
            sh = casts[nm]
            ax = GATHER_AXIS[nm]
            shape = list(sh.shape)
            shape[ax] *= N_DEV
            shards.append(sh)
            lands.append(lax.empty(tuple(shape), sh.dtype))
        peers = (SIBLING,) + SAME_CORE if stage == 0 else tuple(range(1, N_DEV))
        gathers[stage] = _xchg_start(shards, lands, [GATHER_AXIS[nm] for nm in GATHER_STAGES[stage]], gather=True,
                                     name=f"gather_start_{stage}", deps=deps, peers=peers)
        return [gathers[stage]["token"]]

    def get_w(name, after):
        deps = []
        if name not in got:
            group, stage, then = [w for w in GATHER_WAITS if name in w[0]][0]
            xc = gathers[stage]
            if stage == 0:
                later = [casts[nm] for st in GATHER_STAGES[1:] for nm in st if nm != "hgrn_norm_g"]
                relay = _gather_relay(xc, [xc["token"], *after, *later], name="gather_relay")
                res = [_gather_relay_wait(xc, relay, xc["token"], name=f"gather_wait_{group[0]}")]
            else:
                res = _xchg_wait(xc, [GATHER_STAGES[stage].index(nm) for nm in group], [after],
                                 name=f"gather_wait_{group[0]}")
            got.update(zip(group, res))
            if then is not None:
                deps = start_gather(then, [res[0]])
        return got[name], deps

    first = start_gather(0, [])

    scattered = {}

    def on_grads(tag, grads):
        gnames = list(grads)
        axes = [SCATTER_AXIS[nm] for nm in gnames]
        stacks = []
        for nm, ax in zip(gnames, axes):
            shape = list(grads[nm].shape)
            if ax is not None:
                shape[ax] //= N_DEV
            stacks.append(lax.empty((N_DEV, *shape), grads[nm].dtype))
        scattered[tag] = (gnames, _xchg_start([grads[nm] for nm in gnames], stacks, axes, gather=False,
                                              name=f"scatter_start_{tag}"))
        return [scattered[tag][1]["token"]]

    sm = jax.nn.softmax(lb_logits, axis=0)
    csum = jnp.cumsum(sm, axis=0)
    small = dict(lb=(csum - csum[0:1])[1:2], lb_logits=lb_logits, ln_mix_g=ln_mix_g, ln_mix_b=ln_mix_b,
                 ln_ffn_g=ln_ffn_g, ln_ffn_b=ln_ffn_b)
    grad_x = _local_step(x[0], loss_target[0], get_w, small, on_grads, deps=first)
    out = {}

    def stack_small(src_):
        rows = [None] * SMALL_ROWS
        for name in SMALL:
            rows[SMALL_ROW[name]], rows[SMALL_ROW[name] + 1] = src_[name][0:1], src_[name][1:2]
        zero = jnp.zeros((1, x.shape[-1]), F32)
        return jnp.concatenate([zero if r is None else r for r in rows], axis=0)[None]

    def update(name, slabs):
        shape = wts[name].shape
        out[name] = [r.reshape(shape) for r in _adamw(slabs, wts[name], mom[name], vel[name], name=f"adamw_{name}")]
        return out[name][0]

    slabs, after = {}, [grad_x]
    for tag, (gnames, xc) in scattered.items():
        slabs.update(zip(gnames, _xchg_wait(xc, list(range(len(gnames))), after, name=f"scatter_wait_{tag}")))
        if tag == "ffn1":
            continue
        if tag == "ffn0":
            after = [update("ffn_w_down", [slabs["ffn_w_down0"], slabs["ffn_w_down1"]]),
                     update("ffn_w_up", [slabs["ffn_w_up0"], slabs["ffn_w_up1"]])]
        elif tag == "hgrn":
            after = [update("hgrn_w_out", [slabs["hgrn_w_out"]]), update("hgrn_w_in", [slabs["hgrn_w_in"]])]
        elif tag == "attn_out":
            after = [update("attn_w_out", [slabs["attn_w_out"]])]
        else:
            update("attn_w_in", [slabs["attn_w_in"]])
    res = _adamw([slabs["small"]], stack_small(wts), stack_small(mom), stack_small(vel), name="adamw_small")
    for name in SMALL:
        out[name] = [r[0, SMALL_ROW[name]:SMALL_ROW[name] + 2] for r in res]
    loss = res[0][0, LOSS_ROW, 0]
    ng = hgrn_norm_g.shape[-1]
    ng_slabs = lax.dynamic_slice(slabs["small"], (0, NORM_G_ROW, me * ng), (N_DEV, 1, ng))
    out["hgrn_norm_g"] = [r[0] for r in _adamw([ng_slabs], hgrn_norm_g[None], m_hgrn_norm_g[None],
                                                v_hgrn_norm_g[None], name="adamw_norm_g")]
    order =("attn_w_in", "attn_w_out", "hgrn_w_in", "hgrn_w_out", "hgrn_norm_g", "lb_logits", "ln_mix_g", "ln_mix_b",
             "ln_ffn_g", "ln_ffn_b", "ffn_w_up", "ffn_w_down")
    return (loss, grad_x[None], *[out[nm][0] for nm in order], *[out[nm][1] for nm in order],
            *[out[nm][2] for nm in order], *[out[nm][3] for nm in order])
```

```python
import jax
import jax.numpy as jnp
from jax import lax
from jax.experimental import pallas as pl
from jax.experimental.pallas import tpu as pltpu

F32 = jnp.float32
BF16 = jnp.bfloat16

N_DEV = 8
LANES = 128
D_MODEL = 1024
ATTN_HEAD_DIM = 64
ATTN_HEADS = 16
ATTN_SCALE = ATTN_HEAD_DIM ** -0.5
ATTN_BLK = 128
DILATIONS = (1, 4, 16)
ROPE_THETA = 10000.0
HGRN_HEADS = 8
HGRN_CHUNK = 64
D_FF = 4096
LN_EPS = 1e-5
RMS_EPS = 1e-6
DEPTH = 2
ALPHA = (2 * DEPTH) ** 0.25
ADAM_LR, ADAM_B1, ADAM_B2, ADAM_EPS, ADAM_WD, ADAM_STEP = 0.001, 0.9, 0.999, 1e-08, 0.01, 10
VMEM_LIMIT = 48 * 1024 * 1024

_NT = (((1,), (1,)), ((), ()))
_TN = (((0,), (0,)), ((), ()))


def _dot(a, b):
    return jnp.dot(a, b, preferred_element_type=F32)


def _dot_nt(a, b):
    return lax.dot_general(a, b, _NT, preferred_element_type=F32)


def _dot_tn(a, b):
    return lax.dot_general(a, b, _TN, preferred_element_type=F32)


def _split3(x):
    p1 = x.astype(BF16)
    r1 = x - p1.astype(F32)
    p2 = r1.astype(BF16)
    p3 = (r1 - p2.astype(F32)).astype(BF16)
    return p1, p2, p3


def _exact_dot(sel3, x):
    return _dot(sel3, jnp.concatenate(_split3(x), axis=0))


def _exact_dot_r(x, sel3):
    return _dot(jnp.concatenate(_split3(x), axis=1), sel3)


def _params(sem=None):
    return pltpu.CompilerParams(dimension_semantics=sem, vmem_limit_bytes=VMEM_LIMIT)


def _mm(a, b, mode, *, name, m, n, k, tm=1024, tn=1024, tk=1024, out_dtypes=(F32,), epi=None, a_pre=None,
        tile_extras=(), row_extras=(), vec_extras=(), row_outs=(), vec_outs=0, a_split=1, b_split=1, out_split=1,
        b_col_off=0, b_k_off=0, out_col_off=0, out_cols=None, alias=None, epi_wants_j=False, deps=(), t_out=False):
    tm, tn, tk = min(tm, m), min(tn, n), min(tk, k)
    assert m % tm == 0 and n % tn == 0 and k % tk == 0, (name, m, n, k, tm, tn, tk)
    gm, gn, gk = m // tm, n // tn, k // tk
    if mode in ("nn", "nt"):
        if a_split > 1 and tk == k:
            a_spec = pl.BlockSpec((a_split, tm, k // a_split), lambda i, j, kk: (0, i, 0))
        elif a_split > 1:
            kc = (k // a_split) // tk
            a_spec = pl.BlockSpec((None, tm, tk), lambda i, j, kk: (kk // kc, i, kk % kc))
        else:
            a_spec = pl.BlockSpec((tm, tk), lambda i, j, kk: (i, kk))
    else:
        a_spec = pl.BlockSpec((tk, tm), lambda i, j, kk: (kk, i))
    if mode in ("nn", "tn"):
        if b_split > 1:
            nc = (n // b_split) // tn
            b_spec = pl.BlockSpec((None, tk, tn), lambda i, j, kk: (j // nc, kk, j % nc))
        else:
            b_spec = pl.BlockSpec((tk, tn), lambda i, j, kk: (kk + b_k_off, j + b_col_off))
    else:
        b_spec = pl.BlockSpec((tn, tk), lambda i, j, kk: (j + b_col_off, kk + b_k_off))
    if out_split > 1:
        nco = (n // out_split) // tn
        o_spec = pl.BlockSpec((None, tm, tn), lambda i, j, kk: (j // nco, i, j % nco))
        o_shape = (out_split, m, n // out_split)
    else:
        o_spec = pl.BlockSpec((tm, tn), lambda i, j, kk: (i, j + out_col_off))
        o_shape = (m, out_cols if out_cols is not None else n)
    n_ex = len(tile_extras) + len(row_extras) + len(vec_extras)
    n_out = len(out_dtypes)
    n_plain = n_out + len(row_outs)
    assert not vec_outs or gn == 1
    if epi is None:
        def epi(acc):
            return (acc,)
    dot = {"nn": _dot, "nt": _dot_nt, "tn": _dot_tn}[mode]

    def body(*refs):
        a_ref, b_ref = refs[0], refs[1]
        ex = refs[2:2 + n_ex]
        outs = refs[2 + n_ex + (1 if alias is not None else 0) + len(deps):][:n_plain + vec_outs + (1 if t_out else 0)]
        ii = pl.program_id(0)
        jj = pl.program_id(1)

        def product():
            if a_split > 1 and tk == k:
                av = jnp.concatenate([a_ref[p] for p in range(a_split)], axis=1)
            else:
                av = a_ref[...]
            if a_pre is not None:
                av = a_pre(av)
            return dot(av.astype(BF16), b_ref[...].astype(BF16))

        def finish(total):
            lead = (jj,) if epi_wants_j else ()
            res = epi(*lead, total, *[e[...] for e in ex])
            for o, r in zip(outs[:n_plain], res):
                o[...] = r.astype(o.dtype)
            for o, r in zip(outs[n_plain:n_plain + vec_outs], res[n_plain:]):
                @pl.when(ii == 0)
                def _(o=o, r=r):
                    o[...] = r

                @pl.when(ii > 0)
                def _(o=o, r=r):
                    o[...] += r
            if t_out:
                outs[-1][...] = res[-1].astype(outs[-1].dtype)

        if gk == 1:
            finish(product())
        else:
            acc = refs[-1]
            kk = pl.program_id(2)

            @pl.when(kk == 0)
            def _():
                acc[...] = product()

            @pl.when(kk > 0)
            def _():
                acc[...] += product()

            @pl.when(kk == gk - 1)
            def _():
                finish(acc[...])

    in_specs = [a_spec, b_spec] + [o_spec] * len(tile_extras)
    in_specs += [pl.BlockSpec((tm, r.shape[1]), lambda i, j, kk: (i, 0)) if r.shape[0] == m else
                 pl.BlockSpec((r.shape[0], tm), lambda i, j, kk: (0, i)) for r in row_extras]
    in_specs += [pl.BlockSpec((1, tn), lambda i, j, kk: (0, j))] * len(vec_extras)
    args = [a, b] + list(tile_extras) + list(row_extras) + list(vec_extras)
    io_alias = {}
    if alias is not None:
        in_specs.append(pl.BlockSpec(memory_space=pl.ANY))
        args.append(alias)
        io_alias = {len(args) - 1: 0}
    in_specs += [pl.BlockSpec(memory_space=pl.ANY)] * len(deps)
    args += list(deps)
    out_specs = [o_spec] * n_out
    out_shape = [jax.ShapeDtypeStruct(o_shape, dt) for dt in out_dtypes]
    for dt, w in row_outs:
        out_specs.append(pl.BlockSpec((tm, w), lambda i, j, kk: (i, 0)))
        out_shape.append(jax.ShapeDtypeStruct((m, w), dt))
    out_specs += [pl.BlockSpec((1, tn), lambda i, j, kk: (0, j))] * vec_outs
    out_shape += [jax.ShapeDtypeStruct((1, n), F32)] * vec_outs
    if t_out:
        assert out_split > 1
        out_specs.append(pl.BlockSpec((None, tn, tm), lambda i, j, kk: (j // nco, j % nco, i)))
        out_shape.append(jax.ShapeDtypeStruct((out_split, n // out_split, m), out_dtypes[0]))
    out = pl.pallas_call(
        body, name=name, grid=(gm, gn, gk), in_specs=in_specs, out_specs=out_specs, out_shape=out_shape,
        scratch_shapes=[pltpu.VMEM((tm, tn), F32)] if gk > 1 else [],
        input_output_aliases=io_alias,
        compiler_params=_params(("arbitrary" if vec_outs else "parallel", "parallel", "arbitrary")),
    )(*args)
    return out[0] if len(out) == 1 else out


def _rope_tables(seq, dil):
    half = ATTN_HEAD_DIM // 2
    row = jnp.arange(seq, dtype=jnp.int32)
    pos = ((row % (seq // dil)) * dil + row // (seq // dil)).astype(F32)

    def tables(features, axis):
        f = jnp.arange(features, dtype=jnp.int32)
        inv = ROPE_THETA ** (-(f % half).astype(F32) * (2.0 / ATTN_HEAD_DIM))
        sign = jnp.where(f % ATTN_HEAD_DIM < half, -1.0, 1.0).astype(F32)
        ang = jnp.expand_dims(pos, axis) * jnp.expand_dims(inv, 1 - axis)
        return jnp.cos(ang), jnp.sin(ang) * jnp.expand_dims(sign, 1 - axis)

    return tables(LANES, 1), tables(ATTN_HEAD_DIM, 0)


def _rotate(x, c, ss, sign=1.0):
    w = x.shape[-1]
    half = ATTN_HEAD_DIM // 2
    lane = lax.broadcasted_iota(jnp.int32, x.shape, 1)
    first = (lane % ATTN_HEAD_DIM) < half
    partner = jnp.where(first, pltpu.roll(x, w - half, 1), pltpu.roll(x, half, 1))
    reps = w // LANES
    if reps > 1:
        c = jnp.concatenate([c] * reps, axis=1)
        ss = jnp.concatenate([ss] * reps, axis=1)
    return x * c + sign * (partner * ss)


def _rotate_t(xt, ct, sst):
    half = ATTN_HEAD_DIM // 2
    heads = xt.shape[0] // ATTN_HEAD_DIM
    parts = []
    for h in range(heads):
        lo = h * ATTN_HEAD_DIM
        parts += [xt[lo + half:lo + 2 * half], xt[lo:lo + half]]
    return (xt * jnp.concatenate([ct] * heads, axis=0)
            + jnp.concatenate(parts, axis=0) * jnp.concatenate([sst] * heads, axis=0))


def _ln_epi(acc, x, g, b, *prev):
    if prev:
        x = x * prev[0] + prev[1]
    u = ALPHA * x + acc
    mu = jnp.mean(u, axis=-1, keepdims=True)
    uc = u - mu
    var = jnp.mean(uc * uc, axis=-1, keepdims=True)
    rstd = lax.rsqrt(var + LN_EPS)
    xh = uc * rstd
    return xh * g + b, xh, rstd


def _ln_grad(dy, xh, rstd, g):
    dxh = dy * g
    m1 = jnp.mean(dxh, axis=-1, keepdims=True)
    m2 = jnp.mean(dxh * xh, axis=-1, keepdims=True)
    du = rstd * (dxh - m1 - xh * m2)
    return du, du, jnp.sum(dy * xh, axis=0, keepdims=True), jnp.sum(dy, axis=0, keepdims=True)


def _ln_grad_epi(acc, du_next, xh, rstd, g):
    return _ln_grad(acc + ALPHA * du_next, xh, rstd, g)


def _ln_loss_epi(acc, x_in, target, g, b, *prev):
    out, xh, rstd = _ln_epi(acc, x_in, g, b, *prev)
    e = out - target
    du, _, dg, db = _ln_grad(e * (1.0 / e.shape[-1]), xh, rstd, g)
    return du, du, dg, db, jnp.sum(e * e, axis=0, keepdims=True)


POS_BLK = 2048


def _class_rows(r, dil):
    return pl.ds(r, POS_BLK // dil, stride=dil) if dil > 1 else pl.ds(0, POS_BLK)


def _class_view(a, dil):
    s, w = a.shape
    return a.reshape(dil, s // dil, w)


def _class_spec(dil, all_tiles=True):
    if all_tiles:
        return pl.BlockSpec((dil, POS_BLK // dil, LANES), lambda i, t: (0, i, t))
    return pl.BlockSpec((dil, POS_BLK // dil, LANES), lambda i, t: (0, i, 0))


def _pos_spec(all_tiles=True):
    if all_tiles:
        return pl.BlockSpec((POS_BLK, LANES), lambda i, t: (i, t))
    return pl.BlockSpec((POS_BLK, LANES), lambda i, t: (i, 0))


def _prep_x(x, deps=()):
    s, d = x.shape

    def body(x_ref, *refs):
        outs = refs[len(deps):]
        for dil, o_ref in zip(DILATIONS, outs):
            for r in range(dil):
                o_ref[r] = x_ref[_class_rows(r, dil), :].astype(BF16)

    outs = pl.pallas_call(
        body, name="prep_x", grid=(s // POS_BLK, d // LANES),
        in_specs=[_pos_spec()] + [pl.BlockSpec(memory_space=pl.ANY)] * len(deps),
        out_specs=[_class_spec(dil) for dil in DILATIONS],
        out_shape=[jax.ShapeDtypeStruct((dil, s // dil, d), BF16) for dil in DILATIONS],
        compiler_params=_params(("parallel", "parallel")),
    )(x, *deps)
    return [o.reshape(s, d) for o in outs]


def _head_expand_matrix():
    h = lax.broadcasted_iota(jnp.int32, (LANES, D_MODEL), 0)
    l = lax.broadcasted_iota(jnp.int32, (LANES, D_MODEL), 1)
    return (l // ATTN_HEAD_DIM == h).astype(BF16)


def _attn_fwd(qkv, dil, *, name):
    _, s, d = qkv.shape
    nq = s // ATTN_BLK
    per = nq // dil
    tiles = d // LANES

    def body(q_ref, kc_ref, kp_ref, vc_ref, vp_ref, o_ref, lse_ref):
        qb = pl.program_id(0)
        first = (qb % per) == 0
        qi = lax.broadcasted_iota(jnp.int32, (ATTN_BLK, 2 * ATTN_BLK), 0)
        kj = lax.broadcasted_iota(jnp.int32, (ATTN_BLK, 2 * ATTN_BLK), 1)
        dist = qi + ATTN_BLK - kj
        valid = (dist >= 0) & (dist <= ATTN_BLK) & ((kj >= ATTN_BLK) | jnp.logical_not(first))
        lane = lax.broadcasted_iota(jnp.int32, (ATTN_BLK, LANES), 1)
        lse_tile = jnp.zeros((ATTN_BLK, LANES), F32)
        zero = jnp.zeros((), BF16)
        in_head = [(lane // ATTN_HEAD_DIM) == hh for hh in range(2)]
        for t0 in range(0, tiles, ATTN_FWD_TILE_GROUP):
            group = range(t0, t0 + ATTN_FWD_TILE_GROUP)
            heads = [(t, hh) for t in group for hh in range(2)]
            cols = {t: pl.ds(t * LANES, LANES) for t in group}
            k2 = {t: jnp.concatenate([kp_ref[:, cols[t]], kc_ref[:, cols[t]]], axis=0) for t in group}
            v2 = {t: jnp.concatenate([vp_ref[:, cols[t]], vc_ref[:, cols[t]]], axis=0) for t in group}
            sc = {(t, hh): jnp.where(valid, _dot_nt(jnp.where(in_head[hh], q_ref[:, cols[t]], zero), k2[t]),
                                     -jnp.inf) for t, hh in heads}
            mx = {i: jnp.max(sc[i], axis=-1, keepdims=True) for i in heads}
            p = {i: jnp.exp(sc[i] - mx[i]) for i in heads}
            l = {i: jnp.sum(p[i], axis=-1, keepdims=True) for i in heads}
            oh = {i: _dot(p[i].astype(BF16), v2[i[0]]) / l[i] for i in heads}
            for t in group:
                o_ref[:, cols[t]] = jnp.where(in_head[0], oh[t, 0], oh[t, 1])
                for hh in range(2):
                    lse_tile = jnp.where(lane == 2 * t + hh, mx[t, hh] + jnp.log(l[t, hh]), lse_tile)
        lse_ref[...] = lse_tile

    def blk(piece, prev):
        if prev:
            return pl.BlockSpec((None, ATTN_BLK, d), lambda i: (piece, jnp.maximum(i - 1, 0), 0))
        return pl.BlockSpec((None, ATTN_BLK, d), lambda i: (piece, i, 0))

    return pl.pallas_call(
        body, name=name, grid=(nq,),
        in_specs=[blk(0, False), blk(1, False), blk(1, True), blk(2, False), blk(2, True)],
        out_specs=[pl.BlockSpec((ATTN_BLK, d), lambda i: (i, 0)), pl.BlockSpec((ATTN_BLK, LANES), lambda i: (i, 0))],
        out_shape=[jax.ShapeDtypeStruct((s, d), F32), jax.ShapeDtypeStruct((s, LANES), F32)],
        compiler_params=_params(("parallel",)),
    )(qkv, qkv, qkv, qkv, qkv)


def _attn_combine(os_, lses):
    s, d = os_[0].shape
    sel = jnp.concatenate([_head_expand_matrix()] * 3, axis=0)

    def body(o0, o1, o2, l0, l1, l2, sel_ref, of_ref, ob_ref, lt_ref, o_pos, l_pos):
        for g, (dil, o_ref, l_ref) in enumerate(zip(DILATIONS, (o0, o1, o2), (l0, l1, l2))):
            for r in range(dil):
                o_pos[g, _class_rows(r, dil), :] = o_ref[r]
                l_pos[g, _class_rows(r, dil), :] = l_ref[r]
        la, lb_, lc = l_pos[0], l_pos[1], l_pos[2]
        mx = jnp.maximum(jnp.maximum(la, lb_), lc)
        es = (jnp.exp(la - mx), jnp.exp(lb_ - mx), jnp.exp(lc - mx))
        z = es[0] + es[1] + es[2]
        lt_ref[...] = mx + jnp.log(z)
        acc = jnp.zeros((POS_BLK, LANES), F32)
        for g in range(3):
            acc += _exact_dot_r(es[g] / z, sel_ref[...]) * o_pos[g]
        of_ref[...] = acc
        ob_ref[...] = acc.astype(BF16)

    return pl.pallas_call(
        body, name="attn_combine", grid=(s // POS_BLK, d // LANES),
        in_specs=[_class_spec(dil) for dil in DILATIONS] + [_class_spec(dil, False) for dil in DILATIONS]
        + [pl.BlockSpec((3 * LANES, LANES), lambda i, t: (0, t))],
        out_specs=[_pos_spec(), _pos_spec(), _pos_spec(False)],
        out_shape=[jax.ShapeDtypeStruct((s, d), F32), jax.ShapeDtypeStruct((s, d), BF16),
                   jax.ShapeDtypeStruct((s, LANES), F32)],
        scratch_shapes=[pltpu.VMEM((3, POS_BLK, LANES), F32), pltpu.VMEM((3, POS_BLK, LANES), F32)],
        compiler_params=_params(("parallel", "arbitrary")),
    )(*[_class_view(o, dil) for o, dil in zip(os_, DILATIONS)],
      *[_class_view(l, dil) for l, dil in zip(lses, DILATIONS)], sel)


ATTN_TILE_GROUP = 2
ATTN_FWD_TILE_GROUP = 4
AUX_PER_TILE = 12


def _aux_placement():
    h = lax.broadcasted_iota(jnp.int32, (6, LANES, LANES), 1)
    l = lax.broadcasted_iota(jnp.int32, (6, LANES, LANES), 2)
    j = lax.broadcasted_iota(jnp.int32, (6, LANES, LANES), 0)
    target = AUX_PER_TILE * (h // 2) + 3 * (h % 2) + jnp.where(j < 3, j, 3 + j)
    return ((l == target) & (h < ATTN_HEADS)).astype(BF16)


def _attn_bwd_prep(do, o, lse):
    s, d = do.shape
    tiles = d // LANES
    sel_t = jnp.tile(_head_expand_matrix().T.reshape(tiles, LANES, LANES), (1, 3, 1))

    def body(do_ref, o_ref, l_ref, sel_ref, place_ref, *refs):
        outs, delta, aux = refs[:9], refs[9], refs[10]
        t = pl.program_id(1)
        part = _exact_dot_r(do_ref[...] * o_ref[...], sel_ref[...])

        @pl.when(t == 0)
        def _():
            delta[...] = part

        @pl.when(t > 0)
        def _():
            delta[...] += part

        for g, dil in enumerate(DILATIONS):
            for r in range(dil):
                blk = do_ref[_class_rows(r, dil), :].astype(BF16)
                outs[g][r] = blk
                outs[3 + g][r] = blk.T

        @pl.when(t == tiles - 1)
        def _():
            pieces = _split3(l_ref[...]) + _split3(delta[...])
            aux[...] = _dot(jnp.concatenate(pieces, axis=1), place_ref[...])
            for g, dil in enumerate(DILATIONS):
                for r in range(dil):
                    outs[6 + g][r] = aux[_class_rows(r, dil), :].astype(BF16)

    outs = pl.pallas_call(
        body, name="attn_bwd_prep", grid=(s // POS_BLK, tiles),
        in_specs=[_pos_spec(), _pos_spec(), _pos_spec(False),
                  pl.BlockSpec((None, 3 * LANES, LANES), lambda i, t: (t, 0, 0)),
                  pl.BlockSpec((6 * LANES, LANES), lambda i, t: (0, 0))],
        out_specs=[_class_spec(dil) for dil in DILATIONS]
        + [pl.BlockSpec((dil, LANES, POS_BLK // dil), lambda i, t: (0, t, i)) for dil in DILATIONS]
        + [_class_spec(dil, False) for dil in DILATIONS],
        out_shape=[jax.ShapeDtypeStruct((dil, s // dil, d), BF16) for dil in DILATIONS]
        + [jax.ShapeDtypeStruct((dil, d, s // dil), BF16) for dil in DILATIONS]
        + [jax.ShapeDtypeStruct((dil, s // dil, LANES), BF16) for dil in DILATIONS],
        scratch_shapes=[pltpu.VMEM((POS_BLK, LANES), F32), pltpu.VMEM((POS_BLK, LANES), F32)],
        compiler_params=_params(("parallel", "arbitrary")),
    )(do, o, lse, sel_t, _aux_placement().reshape(6 * LANES, LANES))
    return ([a.reshape(s, d) for a in outs[0:3]], list(outs[3:6]), [a.reshape(s, LANES) for a in outs[6:9]])


def _attn_bwd(qkv, qkv_t, do, do_t, aux, tables, dil, *, name):
    _, s, d = qkv.shape
    nq = s // ATTN_BLK
    per = nq // dil
    tiles = d // LANES
    half = ATTN_HEAD_DIM

    def body(qd_ref, qo_ref, k_ref, qtd_ref, qto_ref, kt_ref, vt_ref, dod_ref, doo_ref, dotd_ref, doto_ref,
             auxd_ref, auxo_ref, c_ref, ss_ref, out_ref, carry):
        kb = pl.program_id(0)

        @pl.when(kb == 0)
        def _():
            carry[...] = jnp.zeros_like(carry)

        has_next = (kb % per) != (per - 1)
        qi = lax.broadcasted_iota(jnp.int32, (ATTN_BLK, 2 * ATTN_BLK), 0)
        kj = lax.broadcasted_iota(jnp.int32, (ATTN_BLK, 2 * ATTN_BLK), 1) % ATTN_BLK
        valid_d = kj <= qi
        valid_o = (kj >= qi) & has_next
        lane = lax.broadcasted_iota(jnp.int32, (ATTN_BLK, LANES), 1)
        row = lax.broadcasted_iota(jnp.int32, (LANES, ATTN_BLK), 0)
        side = lax.broadcasted_iota(jnp.int32, (LANES, 2 * ATTN_BLK), 0)
        first = lax.broadcasted_iota(jnp.int32, (LANES, 2 * ATTN_BLK), 1) < ATTN_BLK
        c, ss = c_ref[...], ss_ref[...]
        zero = jnp.zeros((), BF16)
        sides = ((qd_ref, qtd_ref, dod_ref, dotd_ref, auxd_ref[...], valid_d),
                 (qo_ref, qto_ref, doo_ref, doto_ref, auxo_ref[...], valid_o))
        both = (0, 1)

        def head_halves(x, index):
            axis = 0 if index is lane else 1
            return jnp.concatenate([jnp.where(index < half, x, zero), jnp.where(index >= half, x, zero)], axis=axis)

        for t0 in range(0, tiles, ATTN_TILE_GROUP):
            group = range(t0, t0 + ATTN_TILE_GROUP)
            cols = {t: pl.ds(t * LANES, LANES) for t in group}
            kk, kk_t, vv_t = {}, {}, {}
            for t in group:
                base = AUX_PER_TILE * t
                hit = lambda lo: ((first & (side >= base + lo) & (side < base + lo + 3))
                                  | (jnp.logical_not(first) & (side >= base + lo + 3) & (side < base + lo + 6)))
                kk[t] = head_halves(k_ref[:, cols[t]], lane)
                kk_t[t] = jnp.concatenate([head_halves(kt_ref[cols[t], :], row),
                                           jnp.where(hit(0), -1.0, 0.0).astype(BF16)], axis=0)
                vv_t[t] = jnp.concatenate([head_halves(vt_ref[cols[t], :], row),
                                           jnp.where(hit(6), -1.0, 0.0).astype(BF16)], axis=0)
            sc = {(t, w): _dot(jnp.concatenate([sides[w][0][:, cols[t]], sides[w][4]], axis=1), kk_t[t])
                  for t in group for w in both}
            dpd = {(t, w): _dot(jnp.concatenate([sides[w][2][:, cols[t]], sides[w][4]], axis=1), vv_t[t])
                   for t in group for w in both}
            p = {i: jnp.where(sides[i[1]][5], jnp.exp(sc[i]), 0.0) for i in sc}
            ds = {i: (p[i] * dpd[i]).astype(BF16) for i in sc}
            pb = {i: p[i].astype(BF16) for i in sc}
            dv_t = {t: sum(_dot(sides[w][3][cols[t], :], pb[t, w]) for w in both) for t in group}
            dk_t = {t: sum(_dot(sides[w][1][cols[t], :], ds[t, w]) for w in both) for t in group}
            dq = {i: _dot(ds[i], kk[i[0]]) * ATTN_SCALE for i in sc}
            for t in group:
                dq_now = carry[:, cols[t]] + dq[t, 0]
                carry[:, cols[t]] = dq[t, 1]
                dk = jnp.where(row < half, dk_t[t][:, :ATTN_BLK], dk_t[t][:, ATTN_BLK:]).T
                dv = jnp.where(row < half, dv_t[t][:, :ATTN_BLK], dv_t[t][:, ATTN_BLK:]).T
                out_ref[0, :, cols[t]] = _rotate(dq_now, c, ss, -1.0).astype(BF16)
                out_ref[1, :, cols[t]] = _rotate(dk, c, ss, -1.0).astype(BF16)
                out_ref[2, :, cols[t]] = dv.astype(BF16)

    def nxt(i):
        return jnp.minimum(i + 1, nq - 1)

    def piece(p, shift):
        if shift:
            return pl.BlockSpec((None, ATTN_BLK, d), lambda i: (p, nxt(i), 0))
        return pl.BlockSpec((None, ATTN_BLK, d), lambda i: (p, i, 0))

    def piece_t(p, shift):
        if shift:
            return pl.BlockSpec((None, d, ATTN_BLK), lambda i: (p, 0, nxt(i)))
        return pl.BlockSpec((None, d, ATTN_BLK), lambda i: (p, 0, i))

    def rows(width, shift):
        if shift:
            return pl.BlockSpec((ATTN_BLK, width), lambda i: (nxt(i), 0))
        return pl.BlockSpec((ATTN_BLK, width), lambda i: (i, 0))

    def do_t_spec(shift):
        if shift:
            return pl.BlockSpec((None, d, ATTN_BLK), lambda i: (nxt(i) // per, 0, nxt(i) % per))
        return pl.BlockSpec((None, d, ATTN_BLK), lambda i: (i // per, 0, i % per))

    return pl.pallas_call(
        body, name=name, grid=(nq,),
        in_specs=[piece(0, False), piece(0, True), piece(1, False),
                  piece_t(0, False), piece_t(0, True), piece_t(1, False), piece_t(2, False),
                  rows(d, False), rows(d, True), do_t_spec(False), do_t_spec(True),
                  rows(LANES, False), rows(LANES, True), rows(LANES, False), rows(LANES, False)],
        out_specs=pl.BlockSpec((3, ATTN_BLK, d), lambda i: (0, i, 0)),
        out_shape=jax.ShapeDtypeStruct((3, s, d), BF16),
        scratch_shapes=[pltpu.VMEM((ATTN_BLK, d), F32)],
        compiler_params=_params(("arbitrary",)),
    )(qkv, qkv, qkv, qkv_t, qkv_t, qkv_t, qkv_t, do, do, do_t, do_t, aux, aux, *tables)


def _dx_combine(du, parts):
    s, d = du.shape

    def body(du_ref, p0, p1, p2, out_ref):
        out_ref[...] = ALPHA * du_ref[...] + p0[0]
        for dil, p_ref in zip(DILATIONS[1:], (p1, p2)):
            for r in range(dil):
                out_ref[_class_rows(r, dil), :] += p_ref[r]

    return pl.pallas_call(
        body, name="dx_combine", grid=(s // POS_BLK, d // LANES),
        in_specs=[_pos_spec()] + [_class_spec(dil) for dil in DILATIONS], out_specs=_pos_spec(),
        out_shape=jax.ShapeDtypeStruct((s, d), F32),
        compiler_params=_params(("parallel", "parallel")),
    )(du, *[_class_view(p, dil) for p, dil in zip(parts, DILATIONS)])


HGRN_ROWS = 2048


def _tri(lower, copies=1):
    i = lax.broadcasted_iota(jnp.int32, (HGRN_CHUNK, HGRN_CHUNK * copies), 0)
    j = lax.broadcasted_iota(jnp.int32, (HGRN_CHUNK, HGRN_CHUNK * copies), 1) % HGRN_CHUNK
    return (j <= i) if lower else (j >= i)


def _hgrn_gates(qr, z, lb):
    sq = jax.nn.sigmoid(qr)
    e = jnp.exp(-jnp.abs(z))
    big = 1.0 / (1.0 + e)
    small = e * big
    sg = jnp.where(z >= 0, big, small)
    sn = jnp.where(z >= 0, small, big)
    return sq, qr * sq, sg, sn, lb + (1.0 - lb) * sg, (1.0 - lb) * sn


def _hgrn_fwd(p2, lb, norm_g):
    _, s, d = p2.shape
    nblk = s // HGRN_ROWS
    cps = HGRN_ROWS // HGRN_CHUNK

    def body(p_ref, lb_ref, g_ref, o_ref, on_ref, st_ref, state):
        @pl.when(pl.program_id(1) == 0)
        def _():
            state[...] = jnp.zeros_like(state)

        chunks = [pl.ds(c * HGRN_CHUNK, HGRN_CHUNK) for c in range(cps)]
        ltri = _tri(True)
        lsel = _tri(True, 3).astype(BF16)
        _, q, _, _, f, key = _hgrn_gates(p_ref[0], p_ref[1], lb_ref[...])
        lf = jnp.log(f)
        v = [p_ref[2, rows, :].astype(BF16) for rows in chunks]
        b = [_exact_dot(lsel, lf[c * HGRN_CHUNK:(c + 1) * HGRN_CHUNK]) for c in range(cps)]
        b_last = [bc[HGRN_CHUNK - 1:HGRN_CHUNK, :] for bc in b]
        q = [q[c * HGRN_CHUNK:(c + 1) * HGRN_CHUNK] for c in range(cps)]
        key = [key[c * HGRN_CHUNK:(c + 1) * HGRN_CHUNK] for c in range(cps)]
        qd = [(q[c] * jnp.exp(b[c])).astype(BF16) for c in range(cps)]
        kd = [(key[c] * jnp.exp(-b[c])).astype(BF16) for c in range(cps)]
        k2 = [(key[c] * jnp.exp(b_last[c] - b[c])).astype(BF16) for c in range(cps)]
        a = [jnp.where(ltri, _dot_nt(qd[c], kd[c]), 0.0).astype(BF16) for c in range(cps)]
        kv = [_dot_tn(v[c], k2[c]) for c in range(cps)]
        st, sts = state[...], []
        for c in range(cps):
            sts.append(st)
            st_ref[0, c] = st
            st = st * jnp.exp(b_last[c]) + kv[c]
        state[...] = st
        gv = g_ref[...]
        for c in range(cps):
            o = _dot(a[c], v[c]) + _dot_nt(qd[c], sts[c].astype(BF16))
            o_ref[chunks[c], :] = o
            r = lax.rsqrt(jnp.mean(o * o, axis=-1, keepdims=True) + RMS_EPS)
            on_ref[chunks[c], :] = (o * r * gv).astype(BF16)

    vec = pl.BlockSpec((1, LANES), lambda h, c: (0, h))
    col = pl.BlockSpec((HGRN_ROWS, LANES), lambda h, c: (c, h))
    return pl.pallas_call(
        body, name="hgrn_fwd", grid=(HGRN_HEADS, nblk),
        in_specs=[pl.BlockSpec((3, HGRN_ROWS, LANES), lambda h, c: (0, c, h)), vec, vec],
        out_specs=[col, col, pl.BlockSpec((1, cps, LANES, LANES), lambda h, c: (h, c, 0, 0))],
        out_shape=[jax.ShapeDtypeStruct((s, d), F32), jax.ShapeDtypeStruct((s, d), BF16),
                   jax.ShapeDtypeStruct((HGRN_HEADS, s // HGRN_CHUNK, LANES, LANES), F32)],
        scratch_shapes=[pltpu.VMEM((LANES, LANES), F32)],
        compiler_params=_params(("parallel", "arbitrary")),
    )(p2, lb, norm_g)


def _hgrn_bwd(p2, lb, norm_g, o_raw, states, dyn):
    _, s, d = p2.shape
    nblk = s // HGRN_ROWS
    cps = HGRN_ROWS // HGRN_CHUNK

    def body(p_ref, lb_ref, g_ref, o_ref, st_ref, dy_ref, dp_ref, dg_ref, dlb_ref, dstate):
        @pl.when(pl.program_id(1) == 0)
        def _():
            dstate[...] = jnp.zeros_like(dstate)
            dg_ref[...] = jnp.zeros_like(dg_ref)
            dlb_ref[...] = jnp.zeros_like(dlb_ref)

        n = cps
        cut = lambda t: [t[c * HGRN_CHUNK:(c + 1) * HGRN_CHUNK] for c in range(n)]
        chunks = [pl.ds(c * HGRN_CHUNK, HGRN_CHUNK) for c in range(n)]
        lbv = lb_ref[...]
        ltri = _tri(True)
        lsel = _tri(True, 3).astype(BF16)
        usel = _tri(False, 3).astype(BF16)
        last_row = lax.broadcasted_iota(jnp.int32, (HGRN_CHUNK, LANES), 0) == HGRN_CHUNK - 1
        qr, z = p_ref[0], p_ref[1]
        sq, q, sg, sn, f, key = _hgrn_gates(qr, z, lbv)
        lf = jnp.log(f)
        v = [p_ref[2, rows, :].astype(BF16) for rows in chunks]
        o, dyv = o_ref[...], dy_ref[...]
        r = lax.rsqrt(jnp.mean(o * o, axis=-1, keepdims=True) + RMS_EPS)
        oh = o * r
        doh = dyv * g_ref[...]
        do = cut((r * (doh - oh * jnp.mean(doh * oh, axis=-1, keepdims=True))).astype(BF16))
        dg_ref[...] += jnp.sum(dyv * oh, axis=0, keepdims=True)
        b = [_exact_dot(lsel, lfc) for lfc in cut(lf)]
        b_last = [bc[HGRN_CHUNK - 1:HGRN_CHUNK, :] for bc in b]
        q, key = cut(q), cut(key)
        eb = [jnp.exp(bc) for bc in b]
        enb = [jnp.exp(-bc) for bc in b]
        e2 = [jnp.exp(b_last[c] - b[c]) for c in range(n)]
        dec = [jnp.exp(bl) for bl in b_last]
        qd_f = [q[c] * eb[c] for c in range(n)]
        kd_f = [key[c] * enb[c] for c in range(n)]
        k2_f = [key[c] * e2[c] for c in range(n)]
        qd, kd, k2 = ([t.astype(BF16) for t in ts] for ts in (qd_f, kd_f, k2_f))
        a = [jnp.where(ltri, _dot_nt(qd[c], kd[c]), 0.0).astype(BF16) for c in range(n)]
        da = [jnp.where(ltri, _dot_nt(do[c], v[c]), 0.0).astype(BF16) for c in range(n)]
        grow = [_dot_tn(do[c], qd[c]) for c in range(n)]
        dst, dsts = dstate[...], [None] * n
        for c in reversed(range(n)):
            dsts[c] = dst
            dst = dst * dec[c] + grow[c]
        dstate[...] = dst
        st = [st_ref[0, c] for c in range(n)]
        dstb = [t.astype(BF16) for t in dsts]
        dv = [_dot_tn(a[c], do[c]) + _dot_nt(k2[c], dstb[c]) for c in range(n)]
        dqd = [_dot(da[c], kd[c]) + _dot(do[c], st[c].astype(BF16)) for c in range(n)]
        dkd = [_dot_tn(da[c], qd[c]) for c in range(n)]
        dk2 = [_dot(v[c], dstb[c]) for c in range(n)]
        db = []
        for c in range(n):
            ddec = jnp.sum(dsts[c] * st[c], axis=0, keepdims=True)
            db_last = jnp.sum(dk2[c] * k2_f[c], axis=0, keepdims=True) + ddec * dec[c]
            db.append(dqd[c] * qd_f[c] - dkd[c] * kd_f[c] - dk2[c] * k2_f[c] + jnp.where(last_row, db_last, 0.0))
        dlf = [_exact_dot(usel, dbc) for dbc in db]
        f, sg, sn, sq, qr = cut(f), cut(sg), cut(sn), cut(sq), cut(qr)
        dlb_acc = jnp.zeros((1, LANES), F32)
        for c in range(n):
            dkey = dkd[c] * enb[c] + dk2[c] * e2[c]
            common = dlf[c] / f[c] - dkey
            dlb_acc += jnp.sum(common * sn[c], axis=0, keepdims=True)
            dp_ref[0, chunks[c], :] = (dqd[c] * eb[c] * (sq[c] * (1.0 + qr[c] * (1.0 - sq[c])))).astype(BF16)
            dp_ref[1, chunks[c], :] = (common * ((1.0 - lbv) * sg[c] * sn[c])).astype(BF16)
            dp_ref[2, chunks[c], :] = dv[c].astype(BF16)
        dlb_ref[...] += dlb_acc

    def rev(c):
        return nblk - 1 - c

    vec = pl.BlockSpec((1, LANES), lambda h, c: (0, h))
    col = pl.BlockSpec((HGRN_ROWS, LANES), lambda h, c: (rev(c), h))
    p3 = pl.BlockSpec((3, HGRN_ROWS, LANES), lambda h, c: (0, rev(c), h))
    return pl.pallas_call(
        body, name="hgrn_bwd", grid=(HGRN_HEADS, nblk),
        in_specs=[p3, vec, vec, col, pl.BlockSpec((1, cps, LANES, LANES), lambda h, c: (h, rev(c), 0, 0)), col],
        out_specs=[p3, vec, vec],
        out_shape=[jax.ShapeDtypeStruct((3, s, d), BF16), jax.ShapeDtypeStruct((1, d), F32),
                   jax.ShapeDtypeStruct((1, d), F32)],
        scratch_shapes=[pltpu.VMEM((LANES, LANES), F32)],
        compiler_params=_params(("parallel", "arbitrary")),
    )(p2, lb, norm_g, o_raw, states, dyn)


SMALL_ROWS = 16
FUSED_TM = 512


def _relu2(h):
    r = jnp.maximum(h.astype(F32), 0.0)
    return r * r


def _dact_epi(acc, h):
    return (acc * (2.0 * jnp.maximum(h.astype(F32), 0.0)),)


def _rope_epi(j, acc, ct, sst):
    acc_t = acc.T
    rot_t = _rotate_t(acc_t, ct, sst) * jnp.where(j == 0, ATTN_SCALE, 1.0)
    out_t = jnp.where(j >= 2, acc_t, rot_t).astype(BF16)
    return out_t.T, out_t


def _local_step(x, target, get_w, small, on_grads, deps=()):
    s, d = x.shape
    lb = small["lb"]
    gq = 3 * d
    gt = gq // 1024

    def ln_vecs(which, layer):
        return small[f"ln_{which}_g"][layer:layer + 1], small[f"ln_{which}_b"][layer:layer + 1]

    def ln_of(a, w, x_in, which, layer, *, name, k, a_pre=None, deps=(), prev=None):
        vecs = ln_vecs(which, layer) + (ln_vecs(*prev) if prev else ())
        return _mm(a, w, "nn", name=name, m=s, n=d, k=k, tm=FUSED_TM, tk=k, a_pre=a_pre, out_dtypes=(BF16, F32),
                   row_outs=((F32, 1),), epi=_ln_epi, tile_extras=(x_in,), vec_extras=vecs, deps=deps)

    def ln_grad_of(a, w, du_next, xh, rstd, which, layer, *, name, k, a_split=1, deps=()):
        return _mm(a, w, "nt", name=name, m=s, n=d, k=k, tm=FUSED_TM, tk=k, a_split=a_split,
                   out_dtypes=(F32, BF16), vec_outs=2, epi=_ln_grad_epi, tile_extras=(du_next, xh), row_extras=(rstd,),
                   vec_extras=(small[f"ln_{which}_g"][layer:layer + 1],), deps=deps)

    def ffn_fwd(xb, xh_in, w_up, w_down, tag):
        h = _mm(xb, w_up, "nn", name=f"ffn_up_{tag}", m=s, n=D_FF, k=d, tm=2048, out_dtypes=(BF16,))
        return (h,) + tuple(ln_of(h, w_down, xh_in, "ffn", tag, name=f"ffn_down_{tag}", k=D_FF, a_pre=_relu2,
                                  prev=("mix", tag)))

    def ffn_bwd(du, dub, xb, h, w_up, w_down, xh, rstd, tag):
        dh = _mm(dub, w_down, "nt", name=f"ffn_dact_{tag}", m=s, n=D_FF, k=d, tm=2048, out_dtypes=(BF16,), epi=_dact_epi,
                 tile_extras=(h,))
        g_down = _mm(h, dub, "tn", name=f"ffn_gdown_{tag}", m=D_FF, n=d, k=s, tm=512, tk=s, a_pre=_relu2,
                     out_dtypes=(BF16,))
        g_up = _mm(xb, dh, "tn", name=f"ffn_gup_{tag}", m=d, n=D_FF, k=s, tk=s, out_dtypes=(BF16,))
        after = on_grads(f"ffn{tag}", {f"ffn_w_down{tag}": g_down, f"ffn_w_up{tag}": g_up})
        return ln_grad_of(dh, w_up, du, xh, rstd, "mix", tag, name=f"ffn_dx_{tag}", k=D_FF, deps=after)

    xs = _prep_x(x, deps)
    tabs, tabs_t = zip(*[_rope_tables(s, dil) for dil in DILATIONS])
    w_ain, after = get_w("attn_w_in", xs + [t for tab in tabs + tabs_t for t in tab])
    qkvs, qkv_ts, o_parts, lse_parts = [], [], [], []
    for g, dil in enumerate(DILATIONS):
        qkv, qkv_t = _mm(xs[g], w_ain, "nn", name=f"attn_in_{g}", m=s, n=gq, k=d, b_col_off=g * gt, out_split=3,
                         out_dtypes=(BF16,), epi=_rope_epi, epi_wants_j=True, row_extras=tabs_t[g], t_out=True,
                         deps=after if g == 0 else ())
        qkvs.append(qkv)
        qkv_ts.append(qkv_t)
        o_g, lse_g = _attn_fwd(qkv, dil, name=f"attn_fwd_{g}")
        o_parts.append(o_g)
        lse_parts.append(lse_g)
    o_f, o_b, lse_t = _attn_combine(o_parts, lse_parts)
    w_aout, after = get_w("attn_w_out", o_b)
    x1b, xh1, r1 = ln_of(o_b, w_aout, x, "mix", 0, name="attn_out", k=d, deps=after)
    w_up0, w_down0 = get_w("ffn_w_up0", o_b)[0], get_w("ffn_w_down0", o_b)[0]
    h0, x2b, xh2, r2 = ffn_fwd(x1b, xh1, w_up0, w_down0, 0)
    w_hin, w_hout, norm_g = get_w("hgrn_w_in", x2b)[0], get_w("hgrn_w_out", x2b)[0], get_w("hgrn_norm_g", x2b)[0]
    p2 = _mm(x2b, w_hin, "nn", name="hgrn_in", m=s, n=3 * d, k=d, tm=2048, out_split=3)
    o_raw, o_n, states = _hgrn_fwd(p2, lb, norm_g)
    w_up1, w_down1 = get_w("ffn_w_up1", o_n)[0], get_w("ffn_w_down1", o_n)[0]
    x3b, xh3, r3 = ln_of(o_n, w_hout, xh2, "mix", 1, name="hgrn_out", k=d, prev=("ffn", 0))
    h1 = _mm(x3b, w_up1, "nn", name="ffn_up_1", m=s, n=D_FF, k=d, tm=2048, out_dtypes=(BF16,))
    du, dub, dg_ffn1, db_ffn1, sq = _mm(
        h1, w_down1, "nn", name="ffn_down_1", m=s, n=d, k=D_FF, tm=FUSED_TM, tk=D_FF, a_pre=_relu2,
        out_dtypes=(F32, BF16), vec_outs=3, epi=_ln_loss_epi, tile_extras=(xh3, target),
        vec_extras=ln_vecs("ffn", 1) + ln_vecs("mix", 1))
    du, dub, dg_mix1, db_mix1 = ffn_bwd(du, dub, x3b, h1, w_up1, w_down1, xh3, r3, 1)
    dyn = _mm(dub, w_hout, "nt", name="hgrn_dout", m=s, n=d, k=d)
    g_hout = _mm(o_n, dub, "tn", name="hgrn_gout", m=d, n=d, k=s, out_dtypes=(BF16,))
    dp2, d_norm_g, d_lb = _hgrn_bwd(p2, lb, norm_g, o_raw, states, dyn)
    g_hin = _mm(x2b, dp2, "tn", name="hgrn_gin", m=d, n=3 * d, k=s, tk=s, b_split=3, out_dtypes=(BF16,))
    after = on_grads("hgrn", {"hgrn_w_out": g_hout, "hgrn_w_in": g_hin})
    du, dub, dg_ffn0, db_ffn0 = ln_grad_of(dp2, w_hin, du, xh2, r2, "ffn", 0, name="hgrn_dx", k=3 * d, a_split=3,
                                           deps=after)
    du, dub, dg_mix0, db_mix0 = ffn_bwd(du, dub, x1b, h0, w_up0, w_down0, xh1, r1, 0)
    g_aout = _mm(o_b, dub, "tn", name="attn_gout", m=d, n=d, k=s, out_dtypes=(BF16,))
    sm1 = jax.nn.softmax(small["lb_logits"], axis=0)
    d_l1 = d_lb * (sm1[0:1] * sm1[1:2])
    zeros = jnp.zeros((SMALL_ROWS - 12, d), F32)
    loss_row = jnp.broadcast_to(0.5 * jnp.sum(sq) / d, (1, d))
    small_grads = jnp.concatenate([dg_mix0, dg_mix1, db_mix0, db_mix1, dg_ffn0, dg_ffn1, db_ffn0, db_ffn1,
                                   -d_l1, d_l1, d_norm_g, loss_row, zeros], axis=0)
    after = on_grads("attn_out", {"attn_w_out": g_aout, "small": small_grads})
    do = _mm(dub, w_aout, "nt", name="attn_dout", m=s, n=d, k=d, deps=after)
    do_parts, do_ts, aux_parts = _attn_bwd_prep(do, o_f, lse_t)
    g_ain, dqkvs = None, []
    for g, dil in enumerate(DILATIONS):
        dqkvs.append(_attn_bwd(qkvs[g], qkv_ts[g], do_parts[g], do_ts[g], aux_parts[g], tabs[g], dil,
                               name=f"attn_bwd_{g}"))
        g_ain = _mm(xs[g], dqkvs[g], "tn", name=f"attn_gin_{g}", m=d, n=gq, k=s, tk=s, b_split=3, out_dtypes=(BF16,),
                    out_col_off=g * gt, out_cols=3 * gq, alias=g_ain)
    after = on_grads("attn_in", {"attn_w_in": g_ain})
    dx_parts = [_mm(dqkvs[g], w_ain, "nt", name=f"attn_dx_{g}", m=s, n=d, k=gq, tm=FUSED_TM, tk=gq, a_split=3, b_k_off=g,
                    deps=after if g == 0 else ())
                for g in range(len(DILATIONS))]
    return _dx_combine(du, dx_parts)


def _mesh_place():
    x, y, c = lax.axis_index("x"), lax.axis_index("y"), lax.axis_index("c")
    return x, y, c, 4 * x + 2 * y + c


def _peer(x, y, c, k):
    px = 1 - x if (k >> 2) & 1 else x
    py = 1 - y if (k >> 1) & 1 else y
    pc = 1 - c if k & 1 else c
    return (px, py, pc), 4 * px + 2 * py + pc


def _window(ref, axis, size, idx):
    if axis is None:
        return ref
    sl = [slice(None)] * len(ref.shape)
    sl[axis] = pl.ds(idx * size, size)
    return ref.at[tuple(sl)]


_HBM = pl.BlockSpec(memory_space=pltpu.HBM)
_SEM = pl.BlockSpec(memory_space=pltpu.SEMAPHORE)
_EFFECT = pltpu.SideEffectType.DATAFLOW_SIDE_EFFECTING


def _xchg_ends(src_ref, land_ref, axis, gather, me, other):
    if gather:
        size = src_ref.shape[axis]
        return src_ref, _window(land_ref, axis, size, me), _window(land_ref, axis, size, other)
    size = None if axis is None else src_ref.shape[axis] // N_DEV
    return _window(src_ref, axis, size, other), land_ref.at[me], land_ref.at[other]


def _xchg_start(srcs, lands, axes, *, gather, name, deps=(), peers=tuple(range(1, N_DEV))):
    n = len(srcs)
    nd = len(deps)

    def body(*refs):
        src_refs, land_refs = refs[:n], refs[n:2 * n]
        send, recv = refs[2 * n + nd:3 * n + nd], refs[3 * n + nd:4 * n + nd]
        token = refs[-1]
        x, y, c, me = _mesh_place()
        for k in peers:
            peer, pidx = _peer(x, y, c, k)
            for i in range(n):
                src, dst, _ = _xchg_ends(src_refs[i], land_refs[i], axes[i], gather, me, pidx)
                pltpu.make_async_remote_copy(
                    src_ref=src, dst_ref=dst, send_sem=send[i].at[k - 1], recv_sem=recv[i].at[k - 1],
                    device_id=peer, device_id_type=pl.DeviceIdType.MESH).start()
        for i in range(n):
            src, dst, _ = _xchg_ends(src_refs[i], land_refs[i], axes[i], gather, me, me)
            pltpu.make_async_copy(src, dst, send[i].at[N_DEV - 1]).start()
        token[...] = jnp.zeros_like(token)

    bufs = list(srcs) + list(lands)
    outs = pl.pallas_call(
        body, name=name,
        out_shape=[pltpu.SemaphoreType.DMA((N_DEV,))] * (2 * n) + [pltpu.HBM(b.shape, b.dtype) for b in bufs]
        + [jax.ShapeDtypeStruct((8, LANES), F32)],
        in_specs=[_HBM] * (2 * n) + [pl.BlockSpec(memory_space=pl.ANY)] * nd,
        out_specs=[_SEM] * (2 * n) + [_HBM] * (2 * n) + [pl.BlockSpec(memory_space=pltpu.VMEM)],
        input_output_aliases={i: 2 * n + i for i in range(2 * n)},
        compiler_params=pltpu.CompilerParams(has_side_effects=_EFFECT),
    )(*[pltpu.with_memory_space_constraint(b, pltpu.HBM) for b in bufs], *deps)
    return dict(send=outs[:n], recv=outs[n:2 * n], srcs=outs[2 * n:3 * n], lands=outs[3 * n:4 * n], token=outs[-1],
                axes=list(axes), gather=gather)


SIBLING = 1
SAME_CORE = (2, 4, 6)


def _gather_relay(xc, after, *, name):
    axis = xc["axes"][0]

    def body(src_ref, land_ref, send1, recv1, *rest):
        send2, recv2 = rest[-2:]
        x, y, c, me = _mesh_place()
        sibling, _ = _peer(x, y, c, SIBLING)
        for j, k in enumerate(SAME_CORE):
            peer, pidx = _peer(x, y, c, k)
            src, _, got = _xchg_ends(src_ref, land_ref, axis, True, me, pidx)
            pltpu.make_async_remote_copy(
                src_ref=src, dst_ref=got, send_sem=send1.at[k - 1], recv_sem=recv1.at[k - 1],
                device_id=peer, device_id_type=pl.DeviceIdType.MESH).wait_recv()
            pltpu.make_async_remote_copy(
                src_ref=got, dst_ref=got, send_sem=send2.at[j], recv_sem=recv2.at[j],
                device_id=sibling, device_id_type=pl.DeviceIdType.MESH).start()

    src, land = xc["srcs"][0], xc["lands"][0]
    outs = pl.pallas_call(
        body, name=name,
        out_shape=[pltpu.HBM(src.shape, src.dtype), pltpu.HBM(land.shape, land.dtype)]
        + [pltpu.SemaphoreType.DMA((len(SAME_CORE),))] * 2,
        in_specs=[_HBM, _HBM, _SEM, _SEM] + [pl.BlockSpec(memory_space=pl.ANY)] * len(after),
        out_specs=[_HBM, _HBM, _SEM, _SEM], input_output_aliases={0: 0, 1: 1},
        compiler_params=pltpu.CompilerParams(has_side_effects=_EFFECT),
    )(src, land, xc["send"][0], xc["recv"][0], *after)
    return dict(src=outs[0], land=outs[1], send=outs[2], recv=outs[3])


def _gather_relay_wait(xc, relay, after, *, name):
    axis = xc["axes"][0]

    def body(src_ref, land_ref, send1, recv1, send2, recv2, after_ref, src_out, land_out):
        x, y, c, me = _mesh_place()
        sibling, sidx = _peer(x, y, c, SIBLING)
        for k in (SIBLING,) + SAME_CORE:
            peer, pidx = _peer(x, y, c, k)
            src, dst, got = _xchg_ends(src_ref, land_ref, axis, True, me, pidx)
            pltpu.make_async_remote_copy(
                src_ref=src, dst_ref=dst, send_sem=send1.at[k - 1], recv_sem=recv1.at[k - 1],
                device_id=peer, device_id_type=pl.DeviceIdType.MESH).wait_send()
        src, dst, got = _xchg_ends(src_ref, land_ref, axis, True, me, sidx)
        pltpu.make_async_remote_copy(
            src_ref=src, dst_ref=got, send_sem=send1.at[SIBLING - 1], recv_sem=recv1.at[SIBLING - 1],
            device_id=sibling, device_id_type=pl.DeviceIdType.MESH).wait_recv()
        src, dst, _ = _xchg_ends(src_ref, land_ref, axis, True, me, me)
        pltpu.make_async_copy(src, dst, send1.at[N_DEV - 1]).wait()
        for j, k in enumerate(SAME_CORE):
            _, pidx = _peer(x, y, c, k)
            _, qidx = _peer(x, y, c, k ^ SIBLING)
            _, _, sent = _xchg_ends(src_ref, land_ref, axis, True, me, pidx)
            _, _, got = _xchg_ends(src_ref, land_ref, axis, True, me, qidx)
            pltpu.make_async_remote_copy(
                src_ref=sent, dst_ref=sent, send_sem=send2.at[j], recv_sem=recv2.at[j],
                device_id=sibling, device_id_type=pl.DeviceIdType.MESH).wait_send()
            pltpu.make_async_remote_copy(
                src_ref=got, dst_ref=got, send_sem=send2.at[j], recv_sem=recv2.at[j],
                device_id=sibling, device_id_type=pl.DeviceIdType.MESH).wait_recv()

    outs = pl.pallas_call(
        body, name=name, out_shape=[pltpu.HBM(relay["src"].shape, relay["src"].dtype),
                                    pltpu.HBM(relay["land"].shape, relay["land"].dtype)],
        in_specs=[_HBM, _HBM, _SEM, _SEM, _SEM, _SEM, pl.BlockSpec(memory_space=pl.ANY)],
        out_specs=[_HBM, _HBM], input_output_aliases={0: 0, 1: 1},
        compiler_params=pltpu.CompilerParams(has_side_effects=_EFFECT),
    )(relay["src"], relay["land"], xc["send"][0], xc["recv"][0], relay["send"], relay["recv"], after)
    return outs[1]


def _xchg_wait(xc, items, after, *, name):
    m = len(items)
    gather = xc["gather"]
    axes = [xc["axes"][i] for i in items]

    def body(*refs):
        src_refs, land_refs = refs[:m], refs[m:2 * m]
        send, recv = refs[2 * m:3 * m], refs[3 * m:4 * m]
        x, y, c, me = _mesh_place()
        for k in range(1, N_DEV):
            peer, pidx = _peer(x, y, c, k)
            for j in range(m):
                src, dst, got = _xchg_ends(src_refs[j], land_refs[j], axes[j], gather, me, pidx)
                pltpu.make_async_remote_copy(
                    src_ref=src, dst_ref=dst, send_sem=send[j].at[k - 1], recv_sem=recv[j].at[k - 1],
                    device_id=peer, device_id_type=pl.DeviceIdType.MESH).wait_send()
                pltpu.make_async_remote_copy(
                    src_ref=src, dst_ref=got, send_sem=send[j].at[k - 1], recv_sem=recv[j].at[k - 1],
                    device_id=peer, device_id_type=pl.DeviceIdType.MESH).wait_recv()
        for j in range(m):
            src, dst, _ = _xchg_ends(src_refs[j], land_refs[j], axes[j], gather, me, me)
            pltpu.make_async_copy(src, dst, send[j].at[N_DEV - 1]).wait()

    bufs = [xc["srcs"][i] for i in items] + [xc["lands"][i] for i in items]
    sems = [xc["send"][i] for i in items] + [xc["recv"][i] for i in items]
    outs = pl.pallas_call(
        body, name=name, out_shape=[pltpu.HBM(b.shape, b.dtype) for b in bufs],
        in_specs=[_HBM] * (2 * m) + [_SEM] * (2 * m) + [pl.BlockSpec(memory_space=pl.ANY)] * len(after),
        out_specs=[_HBM] * (2 * m), input_output_aliases={j: j for j in range(2 * m)},
        compiler_params=pltpu.CompilerParams(has_side_effects=_EFFECT),
    )(*bufs, *sems, *after)
    return outs[m:]


def _cast_bf16(a, *, name):
    r, c = a.shape
    tr = min(r, 512)

    def body(a_ref, o_ref):
        o_ref[...] = a_ref[...].astype(BF16)

    spec = pl.BlockSpec((tr, c), lambda i: (i, 0))
    return pl.pallas_call(body, name=name, grid=(r // tr,), in_specs=[spec], out_specs=spec,
                          out_shape=jax.ShapeDtypeStruct((r, c), BF16), compiler_params=_params(("parallel",)))(a)


def _adamw(slabs, w, m, v, *, name):
    layers, r, c = w.shape
    tr = min(r, 256)

    def body(*refs):
        s_refs = refs[:layers]
        w_ref, m_ref, v_ref, g_ref, d_ref, mo_ref, vo_ref = refs[layers:]
        for l in range(layers):
            g = s_refs[l][0].astype(F32)
            for i in range(1, N_DEV):
                g = g + s_refs[l][i].astype(F32)
            m2 = ADAM_B1 * m_ref[l] + (1.0 - ADAM_B1) * g
            v2 = ADAM_B2 * v_ref[l] + (1.0 - ADAM_B2) * (g * g)
            m_hat = m2 / (1.0 - ADAM_B1 ** ADAM_STEP)
            v_hat = v2 / (1.0 - ADAM_B2 ** ADAM_STEP)
            g_ref[l] = g
            d_ref[l] = -ADAM_LR * (m_hat / (jnp.sqrt(v_hat) + ADAM_EPS) + ADAM_WD * w_ref[l])
            mo_ref[l] = m2
            vo_ref[l] = v2

    spec = pl.BlockSpec((layers, tr, c), lambda i: (0, i, 0))
    return pl.pallas_call(
        body, name=name, grid=(r // tr,),
        in_specs=[pl.BlockSpec((N_DEV, tr, c), lambda i: (0, i, 0))] * layers + [spec, spec, spec],
        out_specs=[spec] * 4, out_shape=[jax.ShapeDtypeStruct((layers, r, c), F32)] * 4,
        compiler_params=_params(("parallel",)),
    )(*slabs, w, m, v)


GATHER_AXIS = {"attn_w_in": 1, "attn_w_out": 0, "ffn_w_up0": 1, "ffn_w_down0": 0, "hgrn_w_in": 1, "hgrn_w_out": 0,
               "hgrn_norm_g": 1, "ffn_w_up1": 1, "ffn_w_down1": 0}
GATHER_STAGES = (("attn_w_in",), ("attn_w_out", "ffn_w_up0", "ffn_w_down0", "hgrn_w_in", "hgrn_w_out", "hgrn_norm_g"),
                 ("ffn_w_up1", "ffn_w_down1"))
GATHER_WAITS = ((("attn_w_in",), 0, 1), (("attn_w_out", "ffn_w_up0", "ffn_w_down0"), 1, 2),
                (("hgrn_w_in", "hgrn_w_out", "hgrn_norm_g"), 1, None), (("ffn_w_up1", "ffn_w_down1"), 2, None))
SCATTER_AXIS = dict(GATHER_AXIS, small=None)
BIG = ("attn_w_in", "attn_w_out", "hgrn_w_in", "hgrn_w_out", "ffn_w_up", "ffn_w_down")
SMALL = ("lb_logits", "ln_mix_g", "ln_mix_b", "ln_ffn_g", "ln_ffn_b")
SMALL_ROW = {"ln_mix_g": 0, "ln_mix_b": 2, "ln_ffn_g": 4, "ln_ffn_b": 6, "lb_logits": 8}
NORM_G_ROW = 10
LOSS_ROW = 11


def kernel(x, attn_w_in, attn_w_out, hgrn_w_in, hgrn_w_out, hgrn_norm_g, lb_logits, ln_mix_g, ln_mix_b, ln_ffn_g, ln_ffn_b, ffn_w_up, ffn_w_down, loss_target, m_attn_w_in, m_attn_w_out, m_hgrn_w_in, m_hgrn_w_out, m_hgrn_norm_g, m_lb_logits, m_ln_mix_g, m_ln_mix_b, m_ln_ffn_g, m_ln_ffn_b, m_ffn_w_up, m_ffn_w_down, v_attn_w_in, v_attn_w_out, v_hgrn_w_in, v_hgrn_w_out, v_hgrn_norm_g, v_lb_logits, v_ln_mix_g, v_ln_mix_b, v_ln_ffn_g, v_ln_ffn_b, v_ffn_w_up, v_ffn_w_down):
    wts = dict(attn_w_in=attn_w_in, attn_w_out=attn_w_out, hgrn_w_in=hgrn_w_in, hgrn_w_out=hgrn_w_out,
               hgrn_norm_g=hgrn_norm_g, lb_logits=lb_logits, ln_mix_g=ln_mix_g, ln_mix_b=ln_mix_b, ln_ffn_g=ln_ffn_g,
               ln_ffn_b=ln_ffn_b, ffn_w_up=ffn_w_up, ffn_w_down=ffn_w_down)
    mom = dict(attn_w_in=m_attn_w_in, attn_w_out=m_attn_w_out, hgrn_w_in=m_hgrn_w_in, hgrn_w_out=m_hgrn_w_out,
               hgrn_norm_g=m_hgrn_norm_g, lb_logits=m_lb_logits, ln_mix_g=m_ln_mix_g, ln_mix_b=m_ln_mix_b,
               ln_ffn_g=m_ln_ffn_g, ln_ffn_b=m_ln_ffn_b, ffn_w_up=m_ffn_w_up, ffn_w_down=m_ffn_w_down)
    vel = dict(attn_w_in=v_attn_w_in, attn_w_out=v_attn_w_out, hgrn_w_in=v_hgrn_w_in, hgrn_w_out=v_hgrn_w_out,
               hgrn_norm_g=v_hgrn_norm_g, lb_logits=v_lb_logits, ln_mix_g=v_ln_mix_g, ln_mix_b=v_ln_mix_b,
               ln_ffn_g=v_ln_ffn_g, ln_ffn_b=v_ln_ffn_b, ffn_w_up=v_ffn_w_up, ffn_w_down=v_ffn_w_down)
    me = 4 * lax.axis_index("x") + 2 * lax.axis_index("y") + lax.axis_index("c")

    src = {"attn_w_in": attn_w_in[0], "attn_w_out": attn_w_out[0], "hgrn_w_in": hgrn_w_in[0], "hgrn_w_out": hgrn_w_out[0],
           "ffn_w_up0": ffn_w_up[0], "ffn_w_down0": ffn_w_down[0], "ffn_w_up1": ffn_w_up[1], "ffn_w_down1": ffn_w_down[1]}
    gathers, got = {}, {}
    casts = {nm: _cast_bf16(a, name=f"cast_{nm}") for nm, a in src.items()}
    casts["hgrn_norm_g"] = hgrn_norm_g

    def start_gather(stage, deps):
        shards, lands = [], []
        for nm in GATHER_STAGES[stage]:
            sh = casts[nm]
            ax = GATHER_AXIS[nm]
            shape = list(sh.shape)
            shape[ax] *= N_DEV
            shards.append(sh)
            lands.append(lax.empty(tuple(shape), sh.dtype))
        peers = (SIBLING,) + SAME_CORE if stage == 0 else tuple(range(1, N_DEV))
        gathers[stage] = _xchg_start(shards, lands, [GATHER_AXIS[nm] for nm in GATHER_STAGES[stage]], gather=True,
                                     name=f"gather_start_{stage}", deps=deps, peers=peers)
        return [gathers[stage]["token"]]

    def get_w(name, after):
        deps = []
        if name not in got:
            group, stage, then = [w for w in GATHER_WAITS if name in w[0]][0]
            xc = gathers[stage]
            if stage == 0:
                later = [casts[nm] for st in GATHER_STAGES[1:] for nm in st if nm != "hgrn_norm_g"]
                relay = _gather_relay(xc, [xc["token"], *after, *later], name="gather_relay")
                res = [_gather_relay_wait(xc, relay, xc["token"], name=f"gather_wait_{group[0]}")]
            else:
                res = _xchg_wait(xc, [GATHER_STAGES[stage].index(nm) for nm in group], [after],
                                 name=f"gather_wait_{group[0]}")
            got.update(zip(group, res))
            if then is not None:
                deps = start_gather(then, [res[0]])
        return got[name], deps

    first = start_gather(0, [])

    scattered = {}

    def on_grads(tag, grads):
        gnames = list(grads)
        axes = [SCATTER_AXIS[nm] for nm in gnames]
        stacks = []
        for nm, ax in zip(gnames, axes):
            shape = list(grads[nm].shape)
            if ax is not None:
                shape[ax] //= N_DEV
            stacks.append(lax.empty((N_DEV, *shape), grads[nm].dtype))
        scattered[tag] = (gnames, _xchg_start([grads[nm] for nm in gnames], stacks, axes, gather=False,
                                              name=f"scatter_start_{tag}"))
        return [scattered[tag][1]["token"]]

    sm = jax.nn.softmax(lb_logits, axis=0)
    csum = jnp.cumsum(sm, axis=0)
    small = dict(lb=(csum - csum[0:1])[1:2], lb_logits=lb_logits, ln_mix_g=ln_mix_g, ln_mix_b=ln_mix_b,
                 ln_ffn_g=ln_ffn_g, ln_ffn_b=ln_ffn_b)
    grad_x = _local_step(x[0], loss_target[0], get_w, small, on_grads, deps=first)
    out = {}

    def stack_small(src_):
        rows = [None] * SMALL_ROWS
        for name in SMALL:
            rows[SMALL_ROW[name]], rows[SMALL_ROW[name] + 1] = src_[name][0:1], src_[name][1:2]
        zero = jnp.zeros((1, x.shape[-1]), F32)
        return jnp.concatenate([zero if r is None else r for r in rows], axis=0)[None]

    def update(name, slabs):
        shape = wts[name].shape
        out[name] = [r.reshape(shape) for r in _adamw(slabs, wts[name], mom[name], vel[name], name=f"adamw_{name}")]
        return out[name][0]

    slabs, after = {}, [grad_x]
    for tag, (gnames, xc) in scattered.items():
        slabs.update(zip(gnames, _xchg_wait(xc, list(range(len(gnames))), after, name=f"scatter_wait_{tag}")))
        if tag == "ffn1":
            continue
        if tag == "ffn0":
            after = [update("ffn_w_down", [slabs["ffn_w_down0"], slabs["ffn_w_down1"]]),
                     update("ffn_w_up", [slabs["ffn_w_up0"], slabs["ffn_w_up1"]])]
        elif tag == "hgrn":
            after = [update("hgrn_w_out", [slabs["hgrn_w_out"]]), update("hgrn_w_in", [slabs["hgrn_w_in"]])]
        elif tag == "attn_out":
            after = [update("attn_w_out", [slabs["attn_w_out"]])]
        else:
            update("attn_w_in", [slabs["attn_w_in"]])
    res = _adamw([slabs["small"]], stack_small(wts), stack_small(mom), stack_small(vel), name="adamw_small")
    for name in SMALL:
        out[name] = [r[0, SMALL_ROW[name]:SMALL_ROW[name] + 2] for r in res]
    loss = res[0][0, LOSS_ROW, 0]
    ng = hgrn_norm_g.shape[-1]
    ng_slabs = lax.dynamic_slice(slabs["small"], (0, NORM_G_ROW, me * ng), (N_DEV, 1, ng))
    out["hgrn_norm_g"] = [r[0] for r in _adamw([ng_slabs], hgrn_norm_g[None], m_hgrn_norm_g[None],
                                                v_hgrn_norm_g[None], name="adamw_norm_g")]
    order =("attn_w_in", "attn_w_out", "hgrn_w_in", "hgrn_w_out", "hgrn_norm_g", "lb_logits", "ln_mix_g", "ln_mix_b",
             "ln_ffn_g", "ln_ffn_b", "ffn_w_up", "ffn_w_down")
    return (loss, grad_x[None], *[out[nm][0] for nm in order], *[out[nm][1] for nm in order],
            *[out[nm][2] for nm in order], *[out[nm][3] for nm in order])
```

```python
import jax
import jax.numpy as jnp
from jax import lax
from jax.experimental import pallas as pl
from jax.experimental.pallas import tpu as pltpu

F32 = jnp.float32
BF16 = jnp.bfloat16

N_DEV = 8
LANES = 128
D_MODEL = 1024
ATTN_HEAD_DIM = 64
ATTN_HEADS = 16
ATTN_SCALE = ATTN_HEAD_DIM ** -0.5
ATTN_BLK = 128
DILATIONS = (1, 4, 16)
ROPE_THETA = 10000.0
HGRN_HEADS = 8
HGRN_CHUNK = 64
D_FF = 4096
LN_EPS = 1e-5
RMS_EPS = 1e-6
DEPTH = 2
ALPHA = (2 * DEPTH) ** 0.25
ADAM_LR, ADAM_B1, ADAM_B2, ADAM_EPS, ADAM_WD, ADAM_STEP = 0.001, 0.9, 0.999, 1e-08, 0.01, 10
VMEM_LIMIT = 48 * 1024 * 1024

_NT = (((1,), (1,)), ((), ()))
_TN = (((0,), (0,)), ((), ()))


def _dot(a, b):
    return jnp.dot(a, b, preferred_element_type=F32)


def _dot_nt(a, b):
    return lax.dot_general(a, b, _NT, preferred_element_type=F32)


def _dot_tn(a, b):
    return lax.dot_general(a, b, _TN, preferred_element_type=F32)


def _split3(x):
    p1 = x.astype(BF16)
    r1 = x - p1.astype(F32)
    p2 = r1.astype(BF16)
    p3 = (r1 - p2.astype(F32)).astype(BF16)
    return p1, p2, p3


def _exact_dot(sel3, x):
    return _dot(sel3, jnp.concatenate(_split3(x), axis=0))


def _exact_dot_r(x, sel3):
    return _dot(jnp.concatenate(_split3(x), axis=1), sel3)


def _params(sem=None):
    return pltpu.CompilerParams(dimension_semantics=sem, vmem_limit_bytes=VMEM_LIMIT)


def _mm(a, b, mode, *, name, m, n, k, tm=1024, tn=1024, tk=1024, out_dtypes=(F32,), epi=None, a_pre=None,
        tile_extras=(), row_extras=(), vec_extras=(), row_outs=(), vec_outs=0, a_split=1, b_split=1, out_split=1,
        b_col_off=0, b_k_off=0, out_col_off=0, out_cols=None, alias=None, epi_wants_j=False, deps=(), t_out=False):
    tm, tn, tk = min(tm, m), min(tn, n), min(tk, k)
    assert m % tm == 0 and n % tn == 0 and k % tk == 0, (name, m, n, k, tm, tn, tk)
    gm, gn, gk = m // tm, n // tn, k // tk
    if mode in ("nn", "nt"):
        if a_split > 1 and tk == k:
            a_spec = pl.BlockSpec((a_split, tm, k // a_split), lambda i, j, kk: (0, i, 0))
        elif a_split > 1:
            kc = (k // a_split) // tk
            a_spec = pl.BlockSpec((None, tm, tk), lambda i, j, kk: (kk // kc, i, kk % kc))
        else:
            a_spec = pl.BlockSpec((tm, tk), lambda i, j, kk: (i, kk))
    else:
        a_spec = pl.BlockSpec((tk, tm), lambda i, j, kk: (kk, i))
    if mode in ("nn", "tn"):
        if b_split > 1:
            nc = (n // b_split) // tn
            b_spec = pl.BlockSpec((None, tk, tn), lambda i, j, kk: (j // nc, kk, j % nc))
        else:
            b_spec = pl.BlockSpec((tk, tn), lambda i, j, kk: (kk + b_k_off, j + b_col_off))
    else:
        b_spec = pl.BlockSpec((tn, tk), lambda i, j, kk: (j + b_col_off, kk + b_k_off))
    if out_split > 1:
        nco = (n // out_split) // tn
        o_spec = pl.BlockSpec((None, tm, tn), lambda i, j, kk: (j // nco, i, j % nco))
        o_shape = (out_split, m, n // out_split)
    else:
        o_spec = pl.BlockSpec((tm, tn), lambda i, j, kk: (i, j + out_col_off))
        o_shape = (m, out_cols if out_cols is not None else n)
    n_ex = len(tile_extras) + len(row_extras) + len(vec_extras)
    n_out = len(out_dtypes)
    n_plain = n_out + len(row_outs)
    assert not vec_outs or gn == 1
    if epi is None:
        def epi(acc):
            return (acc,)
    dot = {"nn": _dot, "nt": _dot_nt, "tn": _dot_tn}[mode]

    def body(*refs):
        a_ref, b_ref = refs[0], refs[1]
        ex = refs[2:2 + n_ex]
        outs = refs[2 + n_ex + (1 if alias is not None else 0) + len(deps):][:n_plain + vec_outs + (1 if t_out else 0)]
        ii = pl.program_id(0)
        jj = pl.program_id(1)

        def product():
            if a_split > 1 and tk == k:
                av = jnp.concatenate([a_ref[p] for p in range(a_split)], axis=1)
            else:
                av = a_ref[...]
            if a_pre is not None:
                av = a_pre(av)
            return dot(av.astype(BF16), b_ref[...].astype(BF16))

        def finish(total):
            lead = (jj,) if epi_wants_j else ()
            res = epi(*lead, total, *[e[...] for e in ex])
            for o, r in zip(outs[:n_plain], res):
                o[...] = r.astype(o.dtype)
            for o, r in zip(outs[n_plain:n_plain + vec_outs], res[n_plain:]):
                @pl.when(ii == 0)
                def _(o=o, r=r):
                    o[...] = r

                @pl.when(ii > 0)
                def _(o=o, r=r):
                    o[...] += r
            if t_out:
                outs[-1][...] = res[-1].astype(outs[-1].dtype)

        if gk == 1:
            finish(product())
        else:
            acc = refs[-1]
            kk = pl.program_id(2)

            @pl.when(kk == 0)
            def _():
                acc[...] = product()

            @pl.when(kk > 0)
            def _():
                acc[...] += product()

            @pl.when(kk == gk - 1)
            def _():
                finish(acc[...])

    in_specs = [a_spec, b_spec] + [o_spec] * len(tile_extras)
    in_specs += [pl.BlockSpec((tm, r.shape[1]), lambda i, j, kk: (i, 0)) if r.shape[0] == m else
                 pl.BlockSpec((r.shape[0], tm), lambda i, j, kk: (0, i)) for r in row_extras]
    in_specs += [pl.BlockSpec((1, tn), lambda i, j, kk: (0, j))] * len(vec_extras)
    args = [a, b] + list(tile_extras) + list(row_extras) + list(vec_extras)
    io_alias = {}
    if alias is not None:
        in_specs.append(pl.BlockSpec(memory_space=pl.ANY))
        args.append(alias)
        io_alias = {len(args) - 1: 0}
    in_specs += [pl.BlockSpec(memory_space=pl.ANY)] * len(deps)
    args += list(deps)
    out_specs = [o_spec] * n_out
    out_shape = [jax.ShapeDtypeStruct(o_shape, dt) for dt in out_dtypes]
    for dt, w in row_outs:
        out_specs.append(pl.BlockSpec((tm, w), lambda i, j, kk: (i, 0)))
        out_shape.append(jax.ShapeDtypeStruct((m, w), dt))
    out_specs += [pl.BlockSpec((1, tn), lambda i, j, kk: (0, j))] * vec_outs
    out_shape += [jax.ShapeDtypeStruct((1, n), F32)] * vec_outs
    if t_out:
        assert out_split > 1
        out_specs.append(pl.BlockSpec((None, tn, tm), lambda i, j, kk: (j // nco, j % nco, i)))
        out_shape.append(jax.ShapeDtypeStruct((out_split, n // out_split, m), out_dtypes[0]))
    out = pl.pallas_call(
        body, name=name, grid=(gm, gn, gk), in_specs=in_specs, out_specs=out_specs, out_shape=out_shape,
        scratch_shapes=[pltpu.VMEM((tm, tn), F32)] if gk > 1 else [],
        input_output_aliases=io_alias,
        compiler_params=_params(("arbitrary" if vec_outs else "parallel", "parallel", "arbitrary")),
    )(*args)
    return out[0] if len(out) == 1 else out


def _rope_tables(seq, dil):
    half = ATTN_HEAD_DIM // 2
    row = jnp.arange(seq, dtype=jnp.int32)
    pos = ((row % (seq // dil)) * dil + row // (seq // dil)).astype(F32)

    def tables(features, axis):
        f = jnp.arange(features, dtype=jnp.int32)
        inv = ROPE_THETA ** (-(f % half).astype(F32) * (2.0 / ATTN_HEAD_DIM))
        sign = jnp.where(f % ATTN_HEAD_DIM < half, -1.0, 1.0).astype(F32)
        ang = jnp.expand_dims(pos, axis) * jnp.expand_dims(inv, 1 - axis)
        return jnp.cos(ang), jnp.sin(ang) * jnp.expand_dims(sign, 1 - axis)

    return tables(LANES, 1), tables(ATTN_HEAD_DIM, 0)


def _rotate(x, c, ss, sign=1.0):
    w = x.shape[-1]
    half = ATTN_HEAD_DIM // 2
    lane = lax.broadcasted_iota(jnp.int32, x.shape, 1)
    first = (lane % ATTN_HEAD_DIM) < half
    partner = jnp.where(first, pltpu.roll(x, w - half, 1), pltpu.roll(x, half, 1))
    reps = w // LANES
    if reps > 1:
        c = jnp.concatenate([c] * reps, axis=1)
        ss = jnp.concatenate([ss] * reps, axis=1)
    return x * c + sign * (partner * ss)


def _rotate_t(xt, ct, sst):
    half = ATTN_HEAD_DIM // 2
    heads = xt.shape[0] // ATTN_HEAD_DIM
    parts = []
    for h in range(heads):
        lo = h * ATTN_HEAD_DIM
        parts += [xt[lo + half:lo + 2 * half], xt[lo:lo + half]]
    return (xt * jnp.concatenate([ct] * heads, axis=0)
            + jnp.concatenate(parts, axis=0) * jnp.concatenate([sst] * heads, axis=0))


def _ln_epi(acc, x, g, b, *prev):
    if prev:
        x = x * prev[0] + prev[1]
    u = ALPHA * x + acc
    mu = jnp.mean(u, axis=-1, keepdims=True)
    uc = u - mu
    var = jnp.mean(uc * uc, axis=-1, keepdims=True)
    rstd = lax.rsqrt(var + LN_EPS)
    xh = uc * rstd
    return xh * g + b, xh, rstd


def _ln_grad(dy, xh, rstd, g):
    dxh = dy * g
    m1 = jnp.mean(dxh, axis=-1, keepdims=True)
    m2 = jnp.mean(dxh * xh, axis=-1, keepdims=True)
    du = rstd * (dxh - m1 - xh * m2)
    return du, du, jnp.sum(dy * xh, axis=0, keepdims=True), jnp.sum(dy, axis=0, keepdims=True)


def _ln_grad_epi(acc, du_next, xh, rstd, g):
    return _ln_grad(acc + ALPHA * du_next, xh, rstd, g)


def _ln_loss_epi(acc, x_in, target, g, b, *prev):
    out, xh, rstd = _ln_epi(acc, x_in, g, b, *prev)
    e = out - target
    du, _, dg, db = _ln_grad(e * (1.0 / e.shape[-1]), xh, rstd, g)
    return du, du, dg, db, jnp.sum(e * e, axis=0, keepdims=True)


POS_BLK = 2048


def _class_rows(r, dil):
    return pl.ds(r, POS_BLK // dil, stride=dil) if dil > 1 else pl.ds(0, POS_BLK)


def _class_view(a, dil):
    s, w = a.shape
    return a.reshape(dil, s // dil, w)


def _class_spec(dil, all_tiles=True):
    if all_tiles:
        return pl.BlockSpec((dil, POS_BLK // dil, LANES), lambda i, t: (0, i, t))
    return pl.BlockSpec((dil, POS_BLK // dil, LANES), lambda i, t: (0, i, 0))


def _pos_spec(all_tiles=True):
    if all_tiles:
        return pl.BlockSpec((POS_BLK, LANES), lambda i, t: (i, t))
    return pl.BlockSpec((POS_BLK, LANES), lambda i, t: (i, 0))


def _prep_x(x, deps=()):
    s, d = x.shape

    def body(x_ref, *refs):
        outs = refs[len(deps):]
        for dil, o_ref in zip(DILATIONS, outs):
            for r in range(dil):
                o_ref[r] = x_ref[_class_rows(r, dil), :].astype(BF16)

    outs = pl.pallas_call(
        body, name="prep_x", grid=(s // POS_BLK, d // LANES),
        in_specs=[_pos_spec()] + [pl.BlockSpec(memory_space=pl.ANY)] * len(deps),
        out_specs=[_class_spec(dil) for dil in DILATIONS],
        out_shape=[jax.ShapeDtypeStruct((dil, s // dil, d), BF16) for dil in DILATIONS],
        compiler_params=_params(("parallel", "parallel")),
    )(x, *deps)
    return [o.reshape(s, d) for o in outs]


def _head_expand_matrix():
    h = lax.broadcasted_iota(jnp.int32, (LANES, D_MODEL), 0)
    l = lax.broadcasted_iota(jnp.int32, (LANES, D_MODEL), 1)
    return (l // ATTN_HEAD_DIM == h).astype(BF16)


def _attn_fwd(qkv, dil, *, name):
    _, s, d = qkv.shape
    nq = s // ATTN_BLK
    per = nq // dil
    tiles = d // LANES

    def body(q_ref, kc_ref, kp_ref, vc_ref, vp_ref, o_ref, lse_ref):
        qb = pl.program_id(0)
        first = (qb % per) == 0
        qi = lax.broadcasted_iota(jnp.int32, (ATTN_BLK, 2 * ATTN_BLK), 0)
        kj = lax.broadcasted_iota(jnp.int32, (ATTN_BLK, 2 * ATTN_BLK), 1)
        dist = qi + ATTN_BLK - kj
        valid = (dist >= 0) & (dist <= ATTN_BLK) & ((kj >= ATTN_BLK) | jnp.logical_not(first))
        lane = lax.broadcasted_iota(jnp.int32, (ATTN_BLK, LANES), 1)
        lse_tile = jnp.zeros((ATTN_BLK, LANES), F32)
        zero = jnp.zeros((), BF16)
        in_head = [(lane // ATTN_HEAD_DIM) == hh for hh in range(2)]
        for t0 in range(0, tiles, ATTN_FWD_TILE_GROUP):
            group = range(t0, t0 + ATTN_FWD_TILE_GROUP)
            heads = [(t, hh) for t in group for hh in range(2)]
            cols = {t: pl.ds(t * LANES, LANES) for t in group}
            k2 = {t: jnp.concatenate([kp_ref[:, cols[t]], kc_ref[:, cols[t]]], axis=0) for t in group}
            v2 = {t: jnp.concatenate([vp_ref[:, cols[t]], vc_ref[:, cols[t]]], axis=0) for t in group}
            sc = {(t, hh): jnp.where(valid, _dot_nt(jnp.where(in_head[hh], q_ref[:, cols[t]], zero), k2[t]),
                                     -jnp.inf) for t, hh in heads}
            mx = {i: jnp.max(sc[i], axis=-1, keepdims=True) for i in heads}
            p = {i: jnp.exp(sc[i] - mx[i]) for i in heads}
            l = {i: jnp.sum(p[i], axis=-1, keepdims=True) for i in heads}
            oh = {i: _dot(p[i].astype(BF16), v2[i[0]]) / l[i] for i in heads}
            for t in group:
                o_ref[:, cols[t]] = jnp.where(in_head[0], oh[t, 0], oh[t, 1])
                for hh in range(2):
                    lse_tile = jnp.where(lane == 2 * t + hh, mx[t, hh] + jnp.log(l[t, hh]), lse_tile)
        lse_ref[...] = lse_tile

    def blk(piece, prev):
        if prev:
            return pl.BlockSpec((None, ATTN_BLK, d), lambda i: (piece, jnp.maximum(i - 1, 0), 0))
        return pl.BlockSpec((None, ATTN_BLK, d), lambda i: (piece, i, 0))

    return pl.pallas_call(
        body, name=name, grid=(nq,),
        in_specs=[blk(0, False), blk(1, False), blk(1, True), blk(2, False), blk(2, True)],
        out_specs=[pl.BlockSpec((ATTN_BLK, d), lambda i: (i, 0)), pl.BlockSpec((ATTN_BLK, LANES), lambda i: (i, 0))],
        out_shape=[jax.ShapeDtypeStruct((s, d), F32), jax.ShapeDtypeStruct((s, LANES), F32)],
        compiler_params=_params(("parallel",)),
    )(qkv, qkv, qkv, qkv, qkv)


def _attn_combine(os_, lses):
    s, d = os_[0].shape
    sel = jnp.concatenate([_head_expand_matrix()] * 3, axis=0)

    def body(o0, o1, o2, l0, l1, l2, sel_ref, of_ref, ob_ref, lt_ref, o_pos, l_pos, mix):
        @pl.when(pl.program_id(1) == 0)
        def _():
            for g, (dil, l_ref) in enumerate(zip(DILATIONS, (l0, l1, l2))):
                for r in range(dil):
                    l_pos[g, _class_rows(r, dil), :] = l_ref[r]
            la, lb_, lc = l_pos[0], l_pos[1], l_pos[2]
            mx = jnp.maximum(jnp.maximum(la, lb_), lc)
            es = (jnp.exp(la - mx), jnp.exp(lb_ - mx), jnp.exp(lc - mx))
            z = es[0] + es[1] + es[2]
            lt_ref[...] = mx + jnp.log(z)
            for g in range(3):
                mix[g] = jnp.concatenate(_split3(es[g] / z), axis=1)

        for g, (dil, o_ref) in enumerate(zip(DILATIONS, (o0, o1, o2))):
            for r in range(dil):
                o_pos[g, _class_rows(r, dil), :] = o_ref[r]
        acc = jnp.zeros((POS_BLK, LANES), F32)
        for g in range(3):
            acc += _dot(mix[g], sel_ref[...]) * o_pos[g]
        of_ref[...] = acc
        ob_ref[...] = acc.astype(BF16)

    return pl.pallas_call(
        body, name="attn_combine", grid=(s // POS_BLK, d // LANES),
        in_specs=[_class_spec(dil) for dil in DILATIONS] + [_class_spec(dil, False) for dil in DILATIONS]
        + [pl.BlockSpec((3 * LANES, LANES), lambda i, t: (0, t))],
        out_specs=[_pos_spec(), _pos_spec(), _pos_spec(False)],
        out_shape=[jax.ShapeDtypeStruct((s, d), F32), jax.ShapeDtypeStruct((s, d), BF16),
                   jax.ShapeDtypeStruct((s, LANES), F32)],
        scratch_shapes=[pltpu.VMEM((3, POS_BLK, LANES), F32), pltpu.VMEM((3, POS_BLK, LANES), F32),
                        pltpu.VMEM((3, POS_BLK, 3 * LANES), BF16)],
        compiler_params=_params(("parallel", "arbitrary")),
    )(*[_class_view(o, dil) for o, dil in zip(os_, DILATIONS)],
      *[_class_view(l, dil) for l, dil in zip(lses, DILATIONS)], sel)


ATTN_TILE_GROUP = 2
ATTN_FWD_TILE_GROUP = 4
AUX_PER_TILE = 12


def _aux_placement():
    h = lax.broadcasted_iota(jnp.int32, (6, LANES, LANES), 1)
    l = lax.broadcasted_iota(jnp.int32, (6, LANES, LANES), 2)
    j = lax.broadcasted_iota(jnp.int32, (6, LANES, LANES), 0)
    target = AUX_PER_TILE * (h // 2) + 3 * (h % 2) + jnp.where(j < 3, j, 3 + j)
    return ((l == target) & (h < ATTN_HEADS)).astype(BF16)


def _attn_bwd_prep(do, o, lse):
    s, d = do.shape
    tiles = d // LANES
    sel_t = jnp.tile(_head_expand_matrix().T.reshape(tiles, LANES, LANES), (1, 3, 1))

    def body(do_ref, o_ref, l_ref, sel_ref, place_ref, *refs):
        outs, delta, aux = refs[:9], refs[9], refs[10]
        t = pl.program_id(1)
        part = _exact_dot_r(do_ref[...] * o_ref[...], sel_ref[...])

        @pl.when(t == 0)
        def _():
            delta[...] = part

        @pl.when(t > 0)
        def _():
            delta[...] += part

        for g, dil in enumerate(DILATIONS):
            for r in range(dil):
                blk = do_ref[_class_rows(r, dil), :].astype(BF16)
                outs[g][r] = blk
                outs[3 + g][r] = blk.T

        @pl.when(t == tiles - 1)
        def _():
            pieces = _split3(l_ref[...]) + _split3(delta[...])
            aux[...] = _dot(jnp.concatenate(pieces, axis=1), place_ref[...])
            for g, dil in enumerate(DILATIONS):
                for r in range(dil):
                    outs[6 + g][r] = aux[_class_rows(r, dil), :].astype(BF16)

    outs = pl.pallas_call(
        body, name="attn_bwd_prep", grid=(s // POS_BLK, tiles),
        in_specs=[_pos_spec(), _pos_spec(), _pos_spec(False),
                  pl.BlockSpec((None, 3 * LANES, LANES), lambda i, t: (t, 0, 0)),
                  pl.BlockSpec((6 * LANES, LANES), lambda i, t: (0, 0))],
        out_specs=[_class_spec(dil) for dil in DILATIONS]
        + [pl.BlockSpec((dil, LANES, POS_BLK // dil), lambda i, t: (0, t, i)) for dil in DILATIONS]
        + [_class_spec(dil, False) for dil in DILATIONS],
        out_shape=[jax.ShapeDtypeStruct((dil, s // dil, d), BF16) for dil in DILATIONS]
        + [jax.ShapeDtypeStruct((dil, d, s // dil), BF16) for dil in DILATIONS]
        + [jax.ShapeDtypeStruct((dil, s // dil, LANES), BF16) for dil in DILATIONS],
        scratch_shapes=[pltpu.VMEM((POS_BLK, LANES), F32), pltpu.VMEM((POS_BLK, LANES), F32)],
        compiler_params=_params(("parallel", "arbitrary")),
    )(do, o, lse, sel_t, _aux_placement().reshape(6 * LANES, LANES))
    return ([a.reshape(s, d) for a in outs[0:3]], list(outs[3:6]), [a.reshape(s, LANES) for a in outs[6:9]])


def _attn_bwd(qkv, qkv_t, do, do_t, aux, tables, dil, *, name):
    _, s, d = qkv.shape
    nq = s // ATTN_BLK
    per = nq // dil
    tiles = d // LANES
    half = ATTN_HEAD_DIM

    def body(qd_ref, qo_ref, k_ref, qtd_ref, qto_ref, kt_ref, vt_ref, dod_ref, doo_ref, dotd_ref, doto_ref,
             auxd_ref, auxo_ref, c_ref, ss_ref, out_ref, carry):
        kb = pl.program_id(0)

        @pl.when(kb == 0)
        def _():
            carry[...] = jnp.zeros_like(carry)

        has_next = (kb % per) != (per - 1)
        qi = lax.broadcasted_iota(jnp.int32, (ATTN_BLK, 2 * ATTN_BLK), 0)
        kj = lax.broadcasted_iota(jnp.int32, (ATTN_BLK, 2 * ATTN_BLK), 1) % ATTN_BLK
        valid_d = kj <= qi
        valid_o = (kj >= qi) & has_next
        lane = lax.broadcasted_iota(jnp.int32, (ATTN_BLK, LANES), 1)
        row = lax.broadcasted_iota(jnp.int32, (LANES, ATTN_BLK), 0)
        side = lax.broadcasted_iota(jnp.int32, (LANES, 2 * ATTN_BLK), 0)
        first = lax.broadcasted_iota(jnp.int32, (LANES, 2 * ATTN_BLK), 1) < ATTN_BLK
        c, ss = c_ref[...], ss_ref[...]
        zero = jnp.zeros((), BF16)
        sides = ((qd_ref, qtd_ref, dod_ref, dotd_ref, auxd_ref[...], valid_d),
                 (qo_ref, qto_ref, doo_ref, doto_ref, auxo_ref[...], valid_o))
        both = (0, 1)

        def head_halves(x, index):
            axis = 0 if index is lane else 1
            return jnp.concatenate([jnp.where(index < half, x, zero), jnp.where(index >= half, x, zero)], axis=axis)

        for t0 in range(0, tiles, ATTN_TILE_GROUP):
            group = range(t0, t0 + ATTN_TILE_GROUP)
            cols = {t: pl.ds(t * LANES, LANES) for t in group}
            kk, kk_t, vv_t = {}, {}, {}
            for t in group:
                base = AUX_PER_TILE * t
                hit = lambda lo: ((first & (side >= base + lo) & (side < base + lo + 3))
                                  | (jnp.logical_not(first) & (side >= base + lo + 3) & (side < base + lo + 6)))
                kk[t] = head_halves(k_ref[:, cols[t]], lane)
                kk_t[t] = jnp.concatenate([head_halves(kt_ref[cols[t], :], row),
                                           jnp.where(hit(0), -1.0, 0.0).astype(BF16)], axis=0)
                vv_t[t] = jnp.concatenate([head_halves(vt_ref[cols[t], :], row),
                                           jnp.where(hit(6), -1.0, 0.0).astype(BF16)], axis=0)
            sc = {(t, w): _dot(jnp.concatenate([sides[w][0][:, cols[t]], sides[w][4]], axis=1), kk_t[t])
                  for t in group for w in both}
            dpd = {(t, w): _dot(jnp.concatenate([sides[w][2][:, cols[t]], sides[w][4]], axis=1), vv_t[t])
                   for t in group for w in both}
            p = {i: jnp.where(sides[i[1]][5], jnp.exp(sc[i]), 0.0) for i in sc}
            ds = {i: (p[i] * dpd[i]).astype(BF16) for i in sc}
            pb = {i: p[i].astype(BF16) for i in sc}
            dv_t = {t: sum(_dot(sides[w][3][cols[t], :], pb[t, w]) for w in both) for t in group}
            dk_t = {t: sum(_dot(sides[w][1][cols[t], :], ds[t, w]) for w in both) for t in group}
            dq = {i: _dot(ds[i], kk[i[0]]) * ATTN_SCALE for i in sc}
            for t in group:
                dq_now = carry[:, cols[t]] + dq[t, 0]
                carry[:, cols[t]] = dq[t, 1]
                dk = jnp.where(row < half, dk_t[t][:, :ATTN_BLK], dk_t[t][:, ATTN_BLK:]).T
                dv = jnp.where(row < half, dv_t[t][:, :ATTN_BLK], dv_t[t][:, ATTN_BLK:]).T
                out_ref[0, :, cols[t]] = _rotate(dq_now, c, ss, -1.0).astype(BF16)
                out_ref[1, :, cols[t]] = _rotate(dk, c, ss, -1.0).astype(BF16)
                out_ref[2, :, cols[t]] = dv.astype(BF16)

    def nxt(i):
        return jnp.minimum(i + 1, nq - 1)

    def piece(p, shift):
        if shift:
            return pl.BlockSpec((None, ATTN_BLK, d), lambda i: (p, nxt(i), 0))
        return pl.BlockSpec((None, ATTN_BLK, d), lambda i: (p, i, 0))

    def piece_t(p, shift):
        if shift:
            return pl.BlockSpec((None, d, ATTN_BLK), lambda i: (p, 0, nxt(i)))
        return pl.BlockSpec((None, d, ATTN_BLK), lambda i: (p, 0, i))

    def rows(width, shift):
        if shift:
            return pl.BlockSpec((ATTN_BLK, width), lambda i: (nxt(i), 0))
        return pl.BlockSpec((ATTN_BLK, width), lambda i: (i, 0))

    def do_t_spec(shift):
        if shift:
            return pl.BlockSpec((None, d, ATTN_BLK), lambda i: (nxt(i) // per, 0, nxt(i) % per))
        return pl.BlockSpec((None, d, ATTN_BLK), lambda i: (i // per, 0, i % per))

    return pl.pallas_call(
        body, name=name, grid=(nq,),
        in_specs=[piece(0, False), piece(0, True), piece(1, False),
                  piece_t(0, False), piece_t(0, True), piece_t(1, False), piece_t(2, False),
                  rows(d, False), rows(d, True), do_t_spec(False), do_t_spec(True),
                  rows(LANES, False), rows(LANES, True), rows(LANES, False), rows(LANES, False)],
        out_specs=pl.BlockSpec((3, ATTN_BLK, d), lambda i: (0, i, 0)),
        out_shape=jax.ShapeDtypeStruct((3, s, d), BF16),
        scratch_shapes=[pltpu.VMEM((ATTN_BLK, d), F32)],
        compiler_params=_params(("arbitrary",)),
    )(qkv, qkv, qkv, qkv_t, qkv_t, qkv_t, qkv_t, do, do, do_t, do_t, aux, aux, *tables)


def _dx_combine(du, parts):
    s, d = du.shape

    def body(du_ref, p0, p1, p2, out_ref):
        out_ref[...] = ALPHA * du_ref[...] + p0[0]
        for dil, p_ref in zip(DILATIONS[1:], (p1, p2)):
            for r in range(dil):
                out_ref[_class_rows(r, dil), :] += p_ref[r]

    return pl.pallas_call(
        body, name="dx_combine", grid=(s // POS_BLK, d // LANES),
        in_specs=[_pos_spec()] + [_class_spec(dil) for dil in DILATIONS], out_specs=_pos_spec(),
        out_shape=jax.ShapeDtypeStruct((s, d), F32),
        compiler_params=_params(("parallel", "parallel")),
    )(du, *[_class_view(p, dil) for p, dil in zip(parts, DILATIONS)])


HGRN_ROWS = 2048


def _tri(lower, copies=1):
    i = lax.broadcasted_iota(jnp.int32, (HGRN_CHUNK, HGRN_CHUNK * copies), 0)
    j = lax.broadcasted_iota(jnp.int32, (HGRN_CHUNK, HGRN_CHUNK * copies), 1) % HGRN_CHUNK
    return (j <= i) if lower else (j >= i)


def _hgrn_gates(qr, z, lb):
    sq = jax.nn.sigmoid(qr)
    e = jnp.exp(-jnp.abs(z))
    big = 1.0 / (1.0 + e)
    small = e * big
    sg = jnp.where(z >= 0, big, small)
    sn = jnp.where(z >= 0, small, big)
    return sq, qr * sq, sg, sn, lb + (1.0 - lb) * sg, (1.0 - lb) * sn


def _hgrn_fwd(p2, lb, norm_g):
    _, s, d = p2.shape
    nblk = s // HGRN_ROWS
    cps = HGRN_ROWS // HGRN_CHUNK

    def body(p_ref, lb_ref, g_ref, o_ref, on_ref, st_ref, state):
        @pl.when(pl.program_id(1) == 0)
        def _():
            state[...] = jnp.zeros_like(state)

        chunks = [pl.ds(c * HGRN_CHUNK, HGRN_CHUNK) for c in range(cps)]
        ltri = _tri(True)
        lsel = _tri(True, 3).astype(BF16)
        _, q, _, _, f, key = _hgrn_gates(p_ref[0], p_ref[1], lb_ref[...])
        lf = jnp.log(f)
        v = [p_ref[2, rows, :].astype(BF16) for rows in chunks]
        b = [_exact_dot(lsel, lf[c * HGRN_CHUNK:(c + 1) * HGRN_CHUNK]) for c in range(cps)]
        b_last = [bc[HGRN_CHUNK - 1:HGRN_CHUNK, :] for bc in b]
        q = [q[c * HGRN_CHUNK:(c + 1) * HGRN_CHUNK] for c in range(cps)]
        key = [key[c * HGRN_CHUNK:(c + 1) * HGRN_CHUNK] for c in range(cps)]
        qd = [(q[c] * jnp.exp(b[c])).astype(BF16) for c in range(cps)]
        kd = [(key[c] * jnp.exp(-b[c])).astype(BF16) for c in range(cps)]
        k2 = [(key[c] * jnp.exp(b_last[c] - b[c])).astype(BF16) for c in range(cps)]
        a = [jnp.where(ltri, _dot_nt(qd[c], kd[c]), 0.0).astype(BF16) for c in range(cps)]
        kv = [_dot_tn(v[c], k2[c]) for c in range(cps)]
        st, sts = state[...], []
        for c in range(cps):
            sts.append(st)
            st_ref[0, c] = st
            st = st * jnp.exp(b_last[c]) + kv[c]
        state[...] = st
        gv = g_ref[...]
        for c in range(cps):
            o = _dot(a[c], v[c]) + _dot_nt(qd[c], sts[c].astype(BF16))
            o_ref[chunks[c], :] = o
            r = lax.rsqrt(jnp.mean(o * o, axis=-1, keepdims=True) + RMS_EPS)
            on_ref[chunks[c], :] = (o * r * gv).astype(BF16)

    vec = pl.BlockSpec((1, LANES), lambda h, c: (0, h))
    col = pl.BlockSpec((HGRN_ROWS, LANES), lambda h, c: (c, h))
    return pl.pallas_call(
        body, name="hgrn_fwd", grid=(HGRN_HEADS, nblk),
        in_specs=[pl.BlockSpec((3, HGRN_ROWS, LANES), lambda h, c: (0, c, h)), vec, vec],
        out_specs=[col, col, pl.BlockSpec((1, cps, LANES, LANES), lambda h, c: (h, c, 0, 0))],
        out_shape=[jax.ShapeDtypeStruct((s, d), F32), jax.ShapeDtypeStruct((s, d), BF16),
                   jax.ShapeDtypeStruct((HGRN_HEADS, s // HGRN_CHUNK, LANES, LANES), F32)],
        scratch_shapes=[pltpu.VMEM((LANES, LANES), F32)],
        compiler_params=_params(("parallel", "arbitrary")),
    )(p2, lb, norm_g)


def _hgrn_bwd(p2, lb, norm_g, o_raw, states, dyn):
    _, s, d = p2.shape
    nblk = s // HGRN_ROWS
    cps = HGRN_ROWS // HGRN_CHUNK

    def body(p_ref, lb_ref, g_ref, o_ref, st_ref, dy_ref, dp_ref, dg_ref, dlb_ref, dstate):
        @pl.when(pl.program_id(1) == 0)
        def _():
            dstate[...] = jnp.zeros_like(dstate)
            dg_ref[...] = jnp.zeros_like(dg_ref)
            dlb_ref[...] = jnp.zeros_like(dlb_ref)

        n = cps
        cut = lambda t: [t[c * HGRN_CHUNK:(c + 1) * HGRN_CHUNK] for c in range(n)]
        chunks = [pl.ds(c * HGRN_CHUNK, HGRN_CHUNK) for c in range(n)]
        lbv = lb_ref[...]
        ltri = _tri(True)
        lsel = _tri(True, 3).astype(BF16)
        usel = _tri(False, 3).astype(BF16)
        last_row = lax.broadcasted_iota(jnp.int32, (HGRN_CHUNK, LANES), 0) == HGRN_CHUNK - 1
        qr, z = p_ref[0], p_ref[1]
        sq, q, sg, sn, f, key = _hgrn_gates(qr, z, lbv)
        lf = jnp.log(f)
        v = [p_ref[2, rows, :].astype(BF16) for rows in chunks]
        o, dyv = o_ref[...], dy_ref[...]
        r = lax.rsqrt(jnp.mean(o * o, axis=-1, keepdims=True) + RMS_EPS)
        oh = o * r
        doh = dyv * g_ref[...]
        do = cut((r * (doh - oh * jnp.mean(doh * oh, axis=-1, keepdims=True))).astype(BF16))
        dg_ref[...] += jnp.sum(dyv * oh, axis=0, keepdims=True)
        b = [_exact_dot(lsel, lfc) for lfc in cut(lf)]
        b_last = [bc[HGRN_CHUNK - 1:HGRN_CHUNK, :] for bc in b]
        q, key = cut(q), cut(key)
        eb = [jnp.exp(bc) for bc in b]
        enb = [jnp.exp(-bc) for bc in b]
        e2 = [jnp.exp(b_last[c] - b[c]) for c in range(n)]
        dec = [jnp.exp(bl) for bl in b_last]
        qd_f = [q[c] * eb[c] for c in range(n)]
        kd_f = [key[c] * enb[c] for c in range(n)]
        k2_f = [key[c] * e2[c] for c in range(n)]
        qd, kd, k2 = ([t.astype(BF16) for t in ts] for ts in (qd_f, kd_f, k2_f))
        a = [jnp.where(ltri, _dot_nt(qd[c], kd[c]), 0.0).astype(BF16) for c in range(n)]
        da = [jnp.where(ltri, _dot_nt(do[c], v[c]), 0.0).astype(BF16) for c in range(n)]
        grow = [_dot_tn(do[c], qd[c]) for c in range(n)]
        dst, dsts = dstate[...], [None] * n
        for c in reversed(range(n)):
            dsts[c] = dst
            dst = dst * dec[c] + grow[c]
        dstate[...] = dst
        st = [st_ref[0, c] for c in range(n)]
        dstb = [t.astype(BF16) for t in dsts]
        dv = [_dot_tn(a[c], do[c]) + _dot_nt(k2[c], dstb[c]) for c in range(n)]
        dqd = [_dot(da[c], kd[c]) + _dot(do[c], st[c].astype(BF16)) for c in range(n)]
        dkd = [_dot_tn(da[c], qd[c]) for c in range(n)]
        dk2 = [_dot(v[c], dstb[c]) for c in range(n)]
        db = []
        for c in range(n):
            ddec = jnp.sum(dsts[c] * st[c], axis=0, keepdims=True)
            db_last = jnp.sum(dk2[c] * k2_f[c], axis=0, keepdims=True) + ddec * dec[c]
            db.append(dqd[c] * qd_f[c] - dkd[c] * kd_f[c] - dk2[c] * k2_f[c] + jnp.where(last_row, db_last, 0.0))
        dlf = [_exact_dot(usel, dbc) for dbc in db]
        f, sg, sn, sq, qr = cut(f), cut(sg), cut(sn), cut(sq), cut(qr)
        dlb_acc = jnp.zeros((1, LANES), F32)
        for c in range(n):
            dkey = dkd[c] * enb[c] + dk2[c] * e2[c]
            common = dlf[c] / f[c] - dkey
            dlb_acc += jnp.sum(common * sn[c], axis=0, keepdims=True)
            dp_ref[0, chunks[c], :] = (dqd[c] * eb[c] * (sq[c] * (1.0 + qr[c] * (1.0 - sq[c])))).astype(BF16)
            dp_ref[1, chunks[c], :] = (common * ((1.0 - lbv) * sg[c] * sn[c])).astype(BF16)
            dp_ref[2, chunks[c], :] = dv[c].astype(BF16)
        dlb_ref[...] += dlb_acc

    def rev(c):
        return nblk - 1 - c

    vec = pl.BlockSpec((1, LANES), lambda h, c: (0, h))
    col = pl.BlockSpec((HGRN_ROWS, LANES), lambda h, c: (rev(c), h))
    p3 = pl.BlockSpec((3, HGRN_ROWS, LANES), lambda h, c: (0, rev(c), h))
    return pl.pallas_call(
        body, name="hgrn_bwd", grid=(HGRN_HEADS, nblk),
        in_specs=[p3, vec, vec, col, pl.BlockSpec((1, cps, LANES, LANES), lambda h, c: (h, rev(c), 0, 0)), col],
        out_specs=[p3, vec, vec],
        out_shape=[jax.ShapeDtypeStruct((3, s, d), BF16), jax.ShapeDtypeStruct((1, d), F32),
                   jax.ShapeDtypeStruct((1, d), F32)],
        scratch_shapes=[pltpu.VMEM((LANES, LANES), F32)],
        compiler_params=_params(("parallel", "arbitrary")),
    )(p2, lb, norm_g, o_raw, states, dyn)


SMALL_ROWS = 16
FUSED_TM = 512


def _relu2(h):
    r = jnp.maximum(h.astype(F32), 0.0)
    return r * r


def _dact_epi(acc, h):
    return (acc * (2.0 * jnp.maximum(h.astype(F32), 0.0)),)


def _rope_epi(j, acc, ct, sst):
    acc_t = acc.T
    rot_t = _rotate_t(acc_t, ct, sst) * jnp.where(j == 0, ATTN_SCALE, 1.0)
    out_t = jnp.where(j >= 2, acc_t, rot_t).astype(BF16)
    return out_t.T, out_t


def _local_step(x, target, get_w, small, on_grads, deps=()):
    s, d = x.shape
    lb = small["lb"]
    gq = 3 * d
    gt = gq // 1024

    def ln_vecs(which, layer):
        return small[f"ln_{which}_g"][layer:layer + 1], small[f"ln_{which}_b"][layer:layer + 1]

    def ln_of(a, w, x_in, which, layer, *, name, k, a_pre=None, deps=(), prev=None):
        vecs = ln_vecs(which, layer) + (ln_vecs(*prev) if prev else ())
        return _mm(a, w, "nn", name=name, m=s, n=d, k=k, tm=FUSED_TM, tk=k, a_pre=a_pre, out_dtypes=(BF16, F32),
                   row_outs=((F32, 1),), epi=_ln_epi, tile_extras=(x_in,), vec_extras=vecs, deps=deps)

    def ln_grad_of(a, w, du_next, xh, rstd, which, layer, *, name, k, a_split=1, deps=()):
        return _mm(a, w, "nt", name=name, m=s, n=d, k=k, tm=FUSED_TM, tk=k, a_split=a_split,
                   out_dtypes=(F32, BF16), vec_outs=2, epi=_ln_grad_epi, tile_extras=(du_next, xh), row_extras=(rstd,),
                   vec_extras=(small[f"ln_{which}_g"][layer:layer + 1],), deps=deps)

    def ffn_fwd(xb, xh_in, w_up, w_down, tag):
        h = _mm(xb, w_up, "nn", name=f"ffn_up_{tag}", m=s, n=D_FF, k=d, tm=2048, out_dtypes=(BF16,))
        return (h,) + tuple(ln_of(h, w_down, xh_in, "ffn", tag, name=f"ffn_down_{tag}", k=D_FF, a_pre=_relu2,
                                  prev=("mix", tag)))

    def ffn_bwd(du, dub, xb, h, w_up, w_down, xh, rstd, tag):
        dh = _mm(dub, w_down, "nt", name=f"ffn_dact_{tag}", m=s, n=D_FF, k=d, tm=2048, out_dtypes=(BF16,), epi=_dact_epi,
                 tile_extras=(h,))
        g_down = _mm(h, dub, "tn", name=f"ffn_gdown_{tag}", m=D_FF, n=d, k=s, tm=512, tk=s, a_pre=_relu2,
                     out_dtypes=(BF16,))
        g_up = _mm(xb, dh, "tn", name=f"ffn_gup_{tag}", m=d, n=D_FF, k=s, tk=s, out_dtypes=(BF16,))
        after = on_grads(f"ffn{tag}", {f"ffn_w_down{tag}": g_down, f"ffn_w_up{tag}": g_up})
        return ln_grad_of(dh, w_up, du, xh, rstd, "mix", tag, name=f"ffn_dx_{tag}", k=D_FF, deps=after)

    xs = _prep_x(x, deps)
    tabs, tabs_t = zip(*[_rope_tables(s, dil) for dil in DILATIONS])
    w_ain, after = get_w("attn_w_in", xs + [t for tab in tabs + tabs_t for t in tab])
    qkvs, qkv_ts, o_parts, lse_parts = [], [], [], []
    for g, dil in enumerate(DILATIONS):
        qkv, qkv_t = _mm(xs[g], w_ain, "nn", name=f"attn_in_{g}", m=s, n=gq, k=d, b_col_off=g * gt, out_split=3,
                         out_dtypes=(BF16,), epi=_rope_epi, epi_wants_j=True, row_extras=tabs_t[g], t_out=True,
                         deps=after if g == 0 else ())
        qkvs.append(qkv)
        qkv_ts.append(qkv_t)
        o_g, lse_g = _attn_fwd(qkv, dil, name=f"attn_fwd_{g}")
        o_parts.append(o_g)
        lse_parts.append(lse_g)
    o_f, o_b, lse_t = _attn_combine(o_parts, lse_parts)
    w_aout, after = get_w("attn_w_out", o_b)
    x1b, xh1, r1 = ln_of(o_b, w_aout, x, "mix", 0, name="attn_out", k=d, deps=after)
    w_up0, w_down0 = get_w("ffn_w_up0", o_b)[0], get_w("ffn_w_down0", o_b)[0]
    h0, x2b, xh2, r2 = ffn_fwd(x1b, xh1, w_up0, w_down0, 0)
    w_hin, w_hout, norm_g = get_w("hgrn_w_in", x2b)[0], get_w("hgrn_w_out", x2b)[0], get_w("hgrn_norm_g", x2b)[0]
    p2 = _mm(x2b, w_hin, "nn", name="hgrn_in", m=s, n=3 * d, k=d, tm=2048, out_split=3)
    o_raw, o_n, states = _hgrn_fwd(p2, lb, norm_g)
    w_up1, w_down1 = get_w("ffn_w_up1", o_n)[0], get_w("ffn_w_down1", o_n)[0]
    x3b, xh3, r3 = ln_of(o_n, w_hout, xh2, "mix", 1, name="hgrn_out", k=d, prev=("ffn", 0))
    h1 = _mm(x3b, w_up1, "nn", name="ffn_up_1", m=s, n=D_FF, k=d, tm=2048, out_dtypes=(BF16,))
    du, dub, dg_ffn1, db_ffn1, sq = _mm(
        h1, w_down1, "nn", name="ffn_down_1", m=s, n=d, k=D_FF, tm=FUSED_TM, tk=D_FF, a_pre=_relu2,
        out_dtypes=(F32, BF16), vec_outs=3, epi=_ln_loss_epi, tile_extras=(xh3, target),
        vec_extras=ln_vecs("ffn", 1) + ln_vecs("mix", 1))
    du, dub, dg_mix1, db_mix1 = ffn_bwd(du, dub, x3b, h1, w_up1, w_down1, xh3, r3, 1)
    dyn = _mm(dub, w_hout, "nt", name="hgrn_dout", m=s, n=d, k=d)
    g_hout = _mm(o_n, dub, "tn", name="hgrn_gout", m=d, n=d, k=s, out_dtypes=(BF16,))
    dp2, d_norm_g, d_lb = _hgrn_bwd(p2, lb, norm_g, o_raw, states, dyn)
    g_hin = _mm(x2b, dp2, "tn", name="hgrn_gin", m=d, n=3 * d, k=s, tk=s, b_split=3, out_dtypes=(BF16,))
    after = on_grads("hgrn", {"hgrn_w_out": g_hout, "hgrn_w_in": g_hin})
    du, dub, dg_ffn0, db_ffn0 = ln_grad_of(dp2, w_hin, du, xh2, r2, "ffn", 0, name="hgrn_dx", k=3 * d, a_split=3,
                                           deps=after)
    du, dub, dg_mix0, db_mix0 = ffn_bwd(du, dub, x1b, h0, w_up0, w_down0, xh1, r1, 0)
    g_aout = _mm(o_b, dub, "tn", name="attn_gout", m=d, n=d, k=s, out_dtypes=(BF16,))
    sm1 = jax.nn.softmax(small["lb_logits"], axis=0)
    d_l1 = d_lb * (sm1[0:1] * sm1[1:2])
    zeros = jnp.zeros((SMALL_ROWS - 12, d), F32)
    loss_row = jnp.broadcast_to(0.5 * jnp.sum(sq) / d, (1, d))
    small_grads = jnp.concatenate([dg_mix0, dg_mix1, db_mix0, db_mix1, dg_ffn0, dg_ffn1, db_ffn0, db_ffn1,
                                   -d_l1, d_l1, d_norm_g, loss_row, zeros], axis=0)
    after = on_grads("attn_out", {"attn_w_out": g_aout, "small": small_grads})
    do = _mm(dub, w_aout, "nt", name="attn_dout", m=s, n=d, k=d, deps=after)
    do_parts, do_ts, aux_parts = _attn_bwd_prep(do, o_f, lse_t)
    g_ain, dqkvs = None, []
    for g, dil in enumerate(DILATIONS):
        dqkvs.append(_attn_bwd(qkvs[g], qkv_ts[g], do_parts[g], do_ts[g], aux_parts[g], tabs[g], dil,
                               name=f"attn_bwd_{g}"))
        g_ain = _mm(xs[g], dqkvs[g], "tn", name=f"attn_gin_{g}", m=d, n=gq, k=s, tk=s, b_split=3, out_dtypes=(BF16,),
                    out_col_off=g * gt, out_cols=3 * gq, alias=g_ain)
    after = on_grads("attn_in", {"attn_w_in": g_ain})
    dx_parts = [_mm(dqkvs[g], w_ain, "nt", name=f"attn_dx_{g}", m=s, n=d, k=gq, tm=FUSED_TM, tk=gq, a_split=3, b_k_off=g,
                    deps=after if g == 0 else ())
                for g in range(len(DILATIONS))]
    return _dx_combine(du, dx_parts)


def _mesh_place():
    x, y, c = lax.axis_index("x"), lax.axis_index("y"), lax.axis_index("c")
    return x, y, c, 4 * x + 2 * y + c


def _peer(x, y, c, k):
    px = 1 - x if (k >> 2) & 1 else x
    py = 1 - y if (k >> 1) & 1 else y
    pc = 1 - c if k & 1 else c
    return (px, py, pc), 4 * px + 2 * py + pc


def _window(ref, axis, size, idx):
    if axis is None:
        return ref
    sl = [slice(None)] * len(ref.shape)
    sl[axis] = pl.ds(idx * size, size)
    return ref.at[tuple(sl)]


_HBM = pl.BlockSpec(memory_space=pltpu.HBM)
_SEM = pl.BlockSpec(memory_space=pltpu.SEMAPHORE)
_EFFECT = pltpu.SideEffectType.DATAFLOW_SIDE_EFFECTING


def _xchg_ends(src_ref, land_ref, axis, gather, me, other):
    if gather:
        size = src_ref.shape[axis]
        return src_ref, _window(land_ref, axis, size, me), _window(land_ref, axis, size, other)
    size = None if axis is None else src_ref.shape[axis] // N_DEV
    return _window(src_ref, axis, size, other), land_ref.at[me], land_ref.at[other]


def _xchg_start(srcs, lands, axes, *, gather, name, deps=(), peers=tuple(range(1, N_DEV))):
    n = len(srcs)
    nd = len(deps)

    def body(*refs):
        src_refs, land_refs = refs[:n], refs[n:2 * n]
        send, recv = refs[2 * n + nd:3 * n + nd], refs[3 * n + nd:4 * n + nd]
        token = refs[-1]
        x, y, c, me = _mesh_place()
        for k in peers:
            peer, pidx = _peer(x, y, c, k)
            for i in range(n):
                src, dst, _ = _xchg_ends(src_refs[i], land_refs[i], axes[i], gather, me, pidx)
                pltpu.make_async_remote_copy(
                    src_ref=src, dst_ref=dst, send_sem=send[i].at[k - 1], recv_sem=recv[i].at[k - 1],
                    device_id=peer, device_id_type=pl.DeviceIdType.MESH).start()
        for i in range(n):
            src, dst, _ = _xchg_ends(src_refs[i], land_refs[i], axes[i], gather, me, me)
            pltpu.make_async_copy(src, dst, send[i].at[N_DEV - 1]).start()
        token[...] = jnp.zeros_like(token)

    bufs = list(srcs) + list(lands)
    outs = pl.pallas_call(
        body, name=name,
        out_shape=[pltpu.SemaphoreType.DMA((N_DEV,))] * (2 * n) + [pltpu.HBM(b.shape, b.dtype) for b in bufs]
        + [jax.ShapeDtypeStruct((8, LANES), F32)],
        in_specs=[_HBM] * (2 * n) + [pl.BlockSpec(memory_space=pl.ANY)] * nd,
        out_specs=[_SEM] * (2 * n) + [_HBM] * (2 * n) + [pl.BlockSpec(memory_space=pltpu.VMEM)],
        input_output_aliases={i: 2 * n + i for i in range(2 * n)},
        compiler_params=pltpu.CompilerParams(has_side_effects=_EFFECT),
    )(*[pltpu.with_memory_space_constraint(b, pltpu.HBM) for b in bufs], *deps)
    return dict(send=outs[:n], recv=outs[n:2 * n], srcs=outs[2 * n:3 * n], lands=outs[3 * n:4 * n], token=outs[-1],
                axes=list(axes), gather=gather)


SIBLING = 1
SAME_CORE = (2, 4, 6)


def _gather_relay(xc, after, *, name):
    axis = xc["axes"][0]

    def body(src_ref, land_ref, send1, recv1, *rest):
        send2, recv2 = rest[-2:]
        x, y, c, me = _mesh_place()
        sibling, _ = _peer(x, y, c, SIBLING)
        for j, k in enumerate(SAME_CORE):
            peer, pidx = _peer(x, y, c, k)
            src, _, got = _xchg_ends(src_ref, land_ref, axis, True, me, pidx)
            pltpu.make_async_remote_copy(
                src_ref=src, dst_ref=got, send_sem=send1.at[k - 1], recv_sem=recv1.at[k - 1],
                device_id=peer, device_id_type=pl.DeviceIdType.MESH).wait_recv()
            pltpu.make_async_remote_copy(
                src_ref=got, dst_ref=got, send_sem=send2.at[j], recv_sem=recv2.at[j],
                device_id=sibling, device_id_type=pl.DeviceIdType.MESH).start()

    src, land = xc["srcs"][0], xc["lands"][0]
    outs = pl.pallas_call(
        body, name=name,
        out_shape=[pltpu.HBM(src.shape, src.dtype), pltpu.HBM(land.shape, land.dtype)]
        + [pltpu.SemaphoreType.DMA((len(SAME_CORE),))] * 2,
        in_specs=[_HBM, _HBM, _SEM, _SEM] + [pl.BlockSpec(memory_space=pl.ANY)] * len(after),
        out_specs=[_HBM, _HBM, _SEM, _SEM], input_output_aliases={0: 0, 1: 1},
        compiler_params=pltpu.CompilerParams(has_side_effects=_EFFECT),
    )(src, land, xc["send"][0], xc["recv"][0], *after)
    return dict(src=outs[0], land=outs[1], send=outs[2], recv=outs[3])


def _gather_relay_wait(xc, relay, after, *, name):
    axis = xc["axes"][0]

    def body(src_ref, land_ref, send1, recv1, send2, recv2, after_ref, src_out, land_out):
        x, y, c, me = _mesh_place()
        sibling, sidx = _peer(x, y, c, SIBLING)
        for k in (SIBLING,) + SAME_CORE:
            peer, pidx = _peer(x, y, c, k)
            src, dst, got = _xchg_ends(src_ref, land_ref, axis, True, me, pidx)
            pltpu.make_async_remote_copy(
                src_ref=src, dst_ref=dst, send_sem=send1.at[k - 1], recv_sem=recv1.at[k - 1],
                device_id=peer, device_id_type=pl.DeviceIdType.MESH).wait_send()
        src, dst, got = _xchg_ends(src_ref, land_ref, axis, True, me, sidx)
        pltpu.make_async_remote_copy(
            src_ref=src, dst_ref=got, send_sem=send1.at[SIBLING - 1], recv_sem=recv1.at[SIBLING - 1],
            device_id=sibling, device_id_type=pl.DeviceIdType.MESH).wait_recv()
        src, dst, _ = _xchg_ends(src_ref, land_ref, axis, True, me, me)
        pltpu.make_async_copy(src, dst, send1.at[N_DEV - 1]).wait()
        for j, k in enumerate(SAME_CORE):
            _, pidx = _peer(x, y, c, k)
            _, qidx = _peer(x, y, c, k ^ SIBLING)
            _, _, sent = _xchg_ends(src_ref, land_ref, axis, True, me, pidx)
            _, _, got = _xchg_ends(src_ref, land_ref, axis, True, me, qidx)
            pltpu.make_async_remote_copy(
                src_ref=sent, dst_ref=sent, send_sem=send2.at[j], recv_sem=recv2.at[j],
                device_id=sibling, device_id_type=pl.DeviceIdType.MESH).wait_send()
            pltpu.make_async_remote_copy(
                src_ref=got, dst_ref=got, send_sem=send2.at[j], recv_sem=recv2.at[j],
                device_id=sibling, device_id_type=pl.DeviceIdType.MESH).wait_recv()

    outs = pl.pallas_call(
        body, name=name, out_shape=[pltpu.HBM(relay["src"].shape, relay["src"].dtype),
                                    pltpu.HBM(relay["land"].shape, relay["land"].dtype)],
        in_specs=[_HBM, _HBM, _SEM, _SEM, _SEM, _SEM, pl.BlockSpec(memory_space=pl.ANY)],
        out_specs=[_HBM, _HBM], input_output_aliases={0: 0, 1: 1},
        compiler_params=pltpu.CompilerParams(has_side_effects=_EFFECT),
    )(relay["src"], relay["land"], xc["send"][0], xc["recv"][0], relay["send"], relay["recv"], after)
    return outs[1]


def _xchg_wait(xc, items, after, *, name):
    m = len(items)
    gather = xc["gather"]
    axes = [xc["axes"][i] for i in items]

    def body(*refs):
        src_refs, land_refs = refs[:m], refs[m:2 * m]
        send, recv = refs[2 * m:3 * m], refs[3 * m:4 * m]
        x, y, c, me = _mesh_place()
        for k in range(1, N_DEV):
            peer, pidx = _peer(x, y, c, k)
            for j in range(m):
                src, dst, got = _xchg_ends(src_refs[j], land_refs[j], axes[j], gather, me, pidx)
                pltpu.make_async_remote_copy(
                    src_ref=src, dst_ref=dst, send_sem=send[j].at[k - 1], recv_sem=recv[j].at[k - 1],
                    device_id=peer, device_id_type=pl.DeviceIdType.MESH).wait_send()
                pltpu.make_async_remote_copy(
                    src_ref=src, dst_ref=got, send_sem=send[j].at[k - 1], recv_sem=recv[j].at[k - 1],
                    device_id=peer, device_id_type=pl.DeviceIdType.MESH).wait_recv()
        for j in range(m):
            src, dst, _ = _xchg_ends(src_refs[j], land_refs[j], axes[j], gather, me, me)
            pltpu.make_async_copy(src, dst, send[j].at[N_DEV - 1]).wait()

    bufs = [xc["srcs"][i] for i in items] + [xc["lands"][i] for i in items]
    sems = [xc["send"][i] for i in items] + [xc["recv"][i] for i in items]
    outs = pl.pallas_call(
        body, name=name, out_shape=[pltpu.HBM(b.shape, b.dtype) for b in bufs],
        in_specs=[_HBM] * (2 * m) + [_SEM] * (2 * m) + [pl.BlockSpec(memory_space=pl.ANY)] * len(after),
        out_specs=[_HBM] * (2 * m), input_output_aliases={j: j for j in range(2 * m)},
        compiler_params=pltpu.CompilerParams(has_side_effects=_EFFECT),
    )(*bufs, *sems, *after)
    return outs[m:]


def _cast_bf16(a, *, name):
    r, c = a.shape
    tr = min(r, 512)

    def body(a_ref, o_ref):
        o_ref[...] = a_ref[...].astype(BF16)

    spec = pl.BlockSpec((tr, c), lambda i: (i, 0))
    return pl.pallas_call(body, name=name, grid=(r // tr,), in_specs=[spec], out_specs=spec,
                          out_shape=jax.ShapeDtypeStruct((r, c), BF16), compiler_params=_params(("parallel",)))(a)


def _adamw(slabs, w, m, v, *, name):
    layers, r, c = w.shape
    tr = min(r, 256)

    def body(*refs):
        s_refs = refs[:layers]
        w_ref, m_ref, v_ref, g_ref, d_ref, mo_ref, vo_ref = refs[layers:]
        for l in range(layers):
            g = s_refs[l][0].astype(F32)
            for i in range(1, N_DEV):
                g = g + s_refs[l][i].astype(F32)
            m2 = ADAM_B1 * m_ref[l] + (1.0 - ADAM_B1) * g
            v2 = ADAM_B2 * v_ref[l] + (1.0 - ADAM_B2) * (g * g)
            m_hat = m2 / (1.0 - ADAM_B1 ** ADAM_STEP)
            v_hat = v2 / (1.0 - ADAM_B2 ** ADAM_STEP)
            g_ref[l] = g
            d_ref[l] = -ADAM_LR * (m_hat / (jnp.sqrt(v_hat) + ADAM_EPS) + ADAM_WD * w_ref[l])
            mo_ref[l] = m2
            vo_ref[l] = v2

    spec = pl.BlockSpec((layers, tr, c), lambda i: (0, i, 0))
    return pl.pallas_call(
        body, name=name, grid=(r // tr,),
        in_specs=[pl.BlockSpec((N_DEV, tr, c), lambda i: (0, i, 0))] * layers + [spec, spec, spec],
        out_specs=[spec] * 4, out_shape=[jax.ShapeDtypeStruct((layers, r, c), F32)] * 4,
        compiler_params=_params(("parallel",)),
    )(*slabs, w, m, v)


GATHER_AXIS = {"attn_w_in": 1, "attn_w_out": 0, "ffn_w_up0": 1, "ffn_w_down0": 0, "hgrn_w_in": 1, "hgrn_w_out": 0,
               "hgrn_norm_g": 1, "ffn_w_up1": 1, "ffn_w_down1": 0}
GATHER_STAGES = (("attn_w_in",), ("attn_w_out", "ffn_w_up0", "ffn_w_down0", "hgrn_w_in", "hgrn_w_out", "hgrn_norm_g"),
                 ("ffn_w_up1", "ffn_w_down1"))
GATHER_WAITS = ((("attn_w_in",), 0, 1), (("attn_w_out", "ffn_w_up0", "ffn_w_down0"), 1, 2),
                (("hgrn_w_in", "hgrn_w_out", "hgrn_norm_g"), 1, None), (("ffn_w_up1", "ffn_w_down1"), 2, None))
SCATTER_AXIS = dict(GATHER_AXIS, small=None)
BIG = ("attn_w_in", "attn_w_out", "hgrn_w_in", "hgrn_w_out", "ffn_w_up", "ffn_w_down")
SMALL = ("lb_logits", "ln_mix_g", "ln_mix_b", "ln_ffn_g", "ln_ffn_b")
SMALL_ROW = {"ln_mix_g": 0, "ln_mix_b": 2, "ln_ffn_g": 4, "ln_ffn_b": 6, "lb_logits": 8}
NORM_G_ROW = 10
LOSS_ROW = 11


def kernel(x, attn_w_in, attn_w_out, hgrn_w_in, hgrn_w_out, hgrn_norm_g, lb_logits, ln_mix_g, ln_mix_b, ln_ffn_g, ln_ffn_b, ffn_w_up, ffn_w_down, loss_target, m_attn_w_in, m_attn_w_out, m_hgrn_w_in, m_hgrn_w_out, m_hgrn_norm_g, m_lb_logits, m_ln_mix_g, m_ln_mix_b, m_ln_ffn_g, m_ln_ffn_b, m_ffn_w_up, m_ffn_w_down, v_attn_w_in, v_attn_w_out, v_hgrn_w_in, v_hgrn_w_out, v_hgrn_norm_g, v_lb_logits, v_ln_mix_g, v_ln_mix_b, v_ln_ffn_g, v_ln_ffn_b, v_ffn_w_up, v_ffn_w_down):
    wts = dict(attn_w_in=attn_w_in, attn_w_out=attn_w_out, hgrn_w_in=hgrn_w_in, hgrn_w_out=hgrn_w_out,
               hgrn_norm_g=hgrn_norm_g, lb_logits=lb_logits, ln_mix_g=ln_mix_g, ln_mix_b=ln_mix_b, ln_ffn_g=ln_ffn_g,
               ln_ffn_b=ln_ffn_b, ffn_w_up=ffn_w_up, ffn_w_down=ffn_w_down)
    mom = dict(attn_w_in=m_attn_w_in, attn_w_out=m_attn_w_out, hgrn_w_in=m_hgrn_w_in, hgrn_w_out=m_hgrn_w_out,
               hgrn_norm_g=m_hgrn_norm_g, lb_logits=m_lb_logits, ln_mix_g=m_ln_mix_g, ln_mix_b=m_ln_mix_b,
               ln_ffn_g=m_ln_ffn_g, ln_ffn_b=m_ln_ffn_b, ffn_w_up=m_ffn_w_up, ffn_w_down=m_ffn_w_down)
    vel = dict(attn_w_in=v_attn_w_in, attn_w_out=v_attn_w_out, hgrn_w_in=v_hgrn_w_in, hgrn_w_out=v_hgrn_w_out,
               hgrn_norm_g=v_hgrn_norm_g, lb_logits=v_lb_logits, ln_mix_g=v_ln_mix_g, ln_mix_b=v_ln_mix_b,
               ln_ffn_g=v_ln_ffn_g, ln_ffn_b=v_ln_ffn_b, ffn_w_up=v_ffn_w_up, ffn_w_down=v_ffn_w_down)
    me = 4 * lax.axis_index("x") + 2 * lax.axis_index("y") + lax.axis_index("c")

    src = {"attn_w_in": attn_w_in[0], "attn_w_out": attn_w_out[0], "hgrn_w_in": hgrn_w_in[0], "hgrn_w_out": hgrn_w_out[0],
           "ffn_w_up0": ffn_w_up[0], "ffn_w_down0": ffn_w_down[0], "ffn_w_up1": ffn_w_up[1], "ffn_w_down1": ffn_w_down[1]}
    gathers, got = {}, {}
    casts = {nm: _cast_bf16(a, name=f"cast_{nm}") for nm, a in src.items()}
    casts["hgrn_norm_g"] = hgrn_norm_g

    def start_gather(stage, deps):
        shards, lands = [], []
        for nm in GATHER_STAGES[stage]:
            sh = casts[nm]
            ax = GATHER_AXIS[nm]
            shape = list(sh.shape)
            shape[ax] *= N_DEV
            shards.append(sh)
            lands.append(lax.empty(tuple(shape), sh.dtype))
        peers = (SIBLING,) + SAME_CORE if stage == 0 else tuple(range(1, N_DEV))
        gathers[stage] = _xchg_start(shards, lands, [GATHER_AXIS[nm] for nm in GATHER_STAGES[stage]], gather=True,
                                     name=f"gather_start_{stage}", deps=deps, peers=peers)
        return [gathers[stage]["token"]]

    def get_w(name, after):
        deps = []
        if name not in got:
            group, stage, then = [w for w in GATHER_WAITS if name in w[0]][0]
            xc = gathers[stage]
            if stage == 0:
                later = [casts[nm] for st in GATHER_STAGES[1:] for nm in st if nm != "hgrn_norm_g"]
                relay = _gather_relay(xc, [xc["token"], *after, *later], name="gather_relay")
                res = [_gather_relay_wait(xc, relay, xc["token"], name=f"gather_wait_{group[0]}")]
            else:
                res = _xchg_wait(xc, [GATHER_STAGES[stage].index(nm) for nm in group], [after],
                                 name=f"gather_wait_{group[0]}")
            got.update(zip(group, res))
            if then is not None:
                deps = start_gather(then, [res[0]])
        return got[name], deps

    first = start_gather(0, [])

    scattered = {}

    def on_grads(tag, grads):
        gnames = list(grads)
        axes = [SCATTER_AXIS[nm] for nm in gnames]
        stacks = []
        for nm, ax in zip(gnames, axes):
            shape = list(grads[nm].shape)
            if ax is not None:
                shape[ax] //= N_DEV
            stacks.append(lax.empty((N_DEV, *shape), grads[nm].dtype))
        scattered[tag] = (gnames, _xchg_start([grads[nm] for nm in gnames], stacks, axes, gather=False,
                                              name=f"scatter_start_{tag}"))
        return [scattered[tag][1]["token"]]

    sm = jax.nn.softmax(lb_logits, axis=0)
    csum = jnp.cumsum(sm, axis=0)
    small = dict(lb=(csum - csum[0:1])[1:2], lb_logits=lb_logits, ln_mix_g=ln_mix_g, ln_mix_b=ln_mix_b,
                 ln_ffn_g=ln_ffn_g, ln_ffn_b=ln_ffn_b)
    grad_x = _local_step(x[0], loss_target[0], get_w, small, on_grads, deps=first)
    out = {}

    def stack_small(src_):
        rows = [None] * SMALL_ROWS
        for name in SMALL:
            rows[SMALL_ROW[name]], rows[SMALL_ROW[name] + 1] = src_[name][0:1], src_[name][1:2]
        zero = jnp.zeros((1, x.shape[-1]), F32)
        return jnp.concatenate([zero if r is None else r for r in rows], axis=0)[None]

    def update(name, slabs):
        shape = wts[name].shape
        out[name] = [r.reshape(shape) for r in _adamw(slabs, wts[name], mom[name], vel[name], name=f"adamw_{name}")]
        return out[name][0]

    slabs, after = {}, [grad_x]
    for tag, (gnames, xc) in scattered.items():
        slabs.update(zip(gnames, _xchg_wait(xc, list(range(len(gnames))), after, name=f"scatter_wait_{tag}")))
        if tag == "ffn1":
            continue
        if tag == "ffn0":
            after = [update("ffn_w_down", [slabs["ffn_w_down0"], slabs["ffn_w_down1"]]),
                     update("ffn_w_up", [slabs["ffn_w_up0"], slabs["ffn_w_up1"]])]
        elif tag == "hgrn":
            after = [update("hgrn_w_out", [slabs["hgrn_w_out"]]), update("hgrn_w_in", [slabs["hgrn_w_in"]])]
        elif tag == "attn_out":
            after = [update("attn_w_out", [slabs["attn_w_out"]])]
        else:
            update("attn_w_in", [slabs["attn_w_in"]])
    res = _adamw([slabs["small"]], stack_small(wts), stack_small(mom), stack_small(vel), name="adamw_small")
    for name in SMALL:
        out[name] = [r[0, SMALL_ROW[name]:SMALL_ROW[name] + 2] for r in res]
    loss = res[0][0, LOSS_ROW, 0]
    ng = hgrn_norm_g.shape[-1]
    ng_slabs = lax.dynamic_slice(slabs["small"], (0, NORM_G_ROW, me * ng), (N_DEV, 1, ng))
    out["hgrn_norm_g"] = [r[0] for r in _adamw([ng_slabs], hgrn_norm_g[None], m_hgrn_norm_g[None],
                                                v_hgrn_norm_g[None], name="adamw_norm_g")]
    order =("attn_w_in", "attn_w_out", "hgrn_w_in", "hgrn_w_out", "hgrn_norm_g", "lb_logits", "ln_mix_g", "ln_mix_b",
             "ln_ffn_g", "ln_ffn_b", "ffn_w_up", "ffn_w_down")
    return (loss, grad_x[None], *[out[nm][0] for nm in order], *[out[nm][1] for nm in order],
            *[out[nm][2] for nm in order], *[out[nm][3] for nm in order])
```

```python
import jax
import jax.numpy as jnp
from jax import lax
from jax.experimental import pallas as pl
from jax.experimental.pallas import tpu as pltpu

F32 = jnp.float32
BF16 = jnp.bfloat16

N_DEV = 8
LANES = 128
D_MODEL = 1024
ATTN_HEAD_DIM = 64
ATTN_HEADS = 16
ATTN_SCALE = ATTN_HEAD_DIM ** -0.5
ATTN_BLK = 128
DILATIONS = (1, 4, 16)
ROPE_THETA = 10000.0
HGRN_HEADS = 8
HGRN_CHUNK = 64
D_FF = 4096
LN_EPS = 1e-5
RMS_EPS = 1e-6
DEPTH = 2
ALPHA = (2 * DEPTH) ** 0.25
ADAM_LR, ADAM_B1, ADAM_B2, ADAM_EPS, ADAM_WD, ADAM_STEP = 0.001, 0.9, 0.999, 1e-08, 0.01, 10
VMEM_LIMIT = 48 * 1024 * 1024

_NT = (((1,), (1,)), ((), ()))
_TN = (((0,), (0,)), ((), ()))


def _dot(a, b):
    return jnp.dot(a, b, preferred_element_type=F32)


def _dot_nt(a, b):
    return lax.dot_general(a, b, _NT, preferred_element_type=F32)


def _dot_tn(a, b):
    return lax.dot_general(a, b, _TN, preferred_element_type=F32)


def _split3(x):
    p1 = x.astype(BF16)
    r1 = x - p1.astype(F32)
    p2 = r1.astype(BF16)
    p3 = (r1 - p2.astype(F32)).astype(BF16)
    return p1, p2, p3


def _exact_dot(sel3, x):
    return _dot(sel3, jnp.concatenate(_split3(x), axis=0))


def _exact_dot_r(x, sel3):
    return _dot(jnp.concatenate(_split3(x), axis=1), sel3)


def _params(sem=None):
    return pltpu.CompilerParams(dimension_semantics=sem, vmem_limit_bytes=VMEM_LIMIT)


def _mm(a, b, mode, *, name, m, n, k, tm=1024, tn=1024, tk=1024, out_dtypes=(F32,), epi=None, a_pre=None,
        tile_extras=(), row_extras=(), vec_extras=(), row_outs=(), vec_outs=0, a_split=1, b_split=1, out_split=1,
        b_col_off=0, b_k_off=0, out_col_off=0, out_cols=None, alias=None, epi_wants_j=False, deps=(), t_out=False):
    tm, tn, tk = min(tm, m), min(tn, n), min(tk, k)
    assert m % tm == 0 and n % tn == 0 and k % tk == 0, (name, m, n, k, tm, tn, tk)
    gm, gn, gk = m // tm, n // tn, k // tk
    if mode in ("nn", "nt"):
        if a_split > 1 and tk == k:
            a_spec = pl.BlockSpec((a_split, tm, k // a_split), lambda i, j, kk: (0, i, 0))
        elif a_split > 1:
            kc = (k // a_split) // tk
            a_spec = pl.BlockSpec((None, tm, tk), lambda i, j, kk: (kk // kc, i, kk % kc))
        else:
            a_spec = pl.BlockSpec((tm, tk), lambda i, j, kk: (i, kk))
    else:
        a_spec = pl.BlockSpec((tk, tm), lambda i, j, kk: (kk, i))
    if mode in ("nn", "tn"):
        if b_split > 1:
            nc = (n // b_split) // tn
            b_spec = pl.BlockSpec((None, tk, tn), lambda i, j, kk: (j // nc, kk, j % nc))
        else:
            b_spec = pl.BlockSpec((tk, tn), lambda i, j, kk: (kk + b_k_off, j + b_col_off))
    else:
        b_spec = pl.BlockSpec((tn, tk), lambda i, j, kk: (j + b_col_off, kk + b_k_off))
    if out_split > 1:
        nco = (n // out_split) // tn
        o_spec = pl.BlockSpec((None, tm, tn), lambda i, j, kk: (j // nco, i, j % nco))
        o_shape = (out_split, m, n // out_split)
    else:
        o_spec = pl.BlockSpec((tm, tn), lambda i, j, kk: (i, j + out_col_off))
        o_shape = (m, out_cols if out_cols is not None else n)
    n_ex = len(tile_extras) + len(row_extras) + len(vec_extras)
    n_out = len(out_dtypes)
    n_plain = n_out + len(row_outs)
    assert not vec_outs or gn == 1
    if epi is None:
        def epi(acc):
            return (acc,)
    dot = {"nn": _dot, "nt": _dot_nt, "tn": _dot_tn}[mode]

    def body(*refs):
        a_ref, b_ref = refs[0], refs[1]
        ex = refs[2:2 + n_ex]
        outs = refs[2 + n_ex + (1 if alias is not None else 0) + len(deps):][:n_plain + vec_outs + (1 if t_out else 0)]
        ii = pl.program_id(0)
        jj = pl.program_id(1)

        def product():
            if a_split > 1 and tk == k:
                av = jnp.concatenate([a_ref[p] for p in range(a_split)], axis=1)
            else:
                av = a_ref[...]
            if a_pre is not None:
                av = a_pre(av)
            return dot(av.astype(BF16), b_ref[...].astype(BF16))

        def finish(total):
            lead = (jj,) if epi_wants_j else ()
            res = epi(*lead, total, *[e[...] for e in ex])
            for o, r in zip(outs[:n_plain], res):
                o[...] = r.astype(o.dtype)
            for o, r in zip(outs[n_plain:n_plain + vec_outs], res[n_plain:]):
                @pl.when(ii == 0)
                def _(o=o, r=r):
                    o[...] = r

                @pl.when(ii > 0)
                def _(o=o, r=r):
                    o[...] += r
            if t_out:
                outs[-1][...] = res[-1].astype(outs[-1].dtype)

        if gk == 1:
            finish(product())
        else:
            acc = refs[-1]
            kk = pl.program_id(2)

            @pl.when(kk == 0)
            def _():
                acc[...] = product()

            @pl.when(kk > 0)
            def _():
                acc[...] += product()

            @pl.when(kk == gk - 1)
            def _():
                finish(acc[...])

    in_specs = [a_spec, b_spec] + [o_spec] * len(tile_extras)
    in_specs += [pl.BlockSpec((tm, r.shape[1]), lambda i, j, kk: (i, 0)) if r.shape[0] == m else
                 pl.BlockSpec((r.shape[0], tm), lambda i, j, kk: (0, i)) for r in row_extras]
    in_specs += [pl.BlockSpec((1, tn), lambda i, j, kk: (0, j))] * len(vec_extras)
    args = [a, b] + list(tile_extras) + list(row_extras) + list(vec_extras)
    io_alias = {}
    if alias is not None:
        in_specs.append(pl.BlockSpec(memory_space=pl.ANY))
        args.append(alias)
        io_alias = {len(args) - 1: 0}
    in_specs += [pl.BlockSpec(memory_space=pl.ANY)] * len(deps)
    args += list(deps)
    out_specs = [o_spec] * n_out
    out_shape = [jax.ShapeDtypeStruct(o_shape, dt) for dt in out_dtypes]
    for dt, w in row_outs:
        out_specs.append(pl.BlockSpec((tm, w), lambda i, j, kk: (i, 0)))
        out_shape.append(jax.ShapeDtypeStruct((m, w), dt))
    out_specs += [pl.BlockSpec((1, tn), lambda i, j, kk: (0, j))] * vec_outs
    out_shape += [jax.ShapeDtypeStruct((1, n), F32)] * vec_outs
    if t_out:
        assert out_split > 1
        out_specs.append(pl.BlockSpec((None, tn, tm), lambda i, j, kk: (j // nco, j % nco, i)))
        out_shape.append(jax.ShapeDtypeStruct((out_split, n // out_split, m), out_dtypes[0]))
    out = pl.pallas_call(
        body, name=name, grid=(gm, gn, gk), in_specs=in_specs, out_specs=out_specs, out_shape=out_shape,
        scratch_shapes=[pltpu.VMEM((tm, tn), F32)] if gk > 1 else [],
        input_output_aliases=io_alias,
        compiler_params=_params(("arbitrary" if vec_outs else "parallel", "parallel", "arbitrary")),
    )(*args)
    return out[0] if len(out) == 1 else out


def _rope_tables(seq, dil):
    half = ATTN_HEAD_DIM // 2
    row = jnp.arange(seq, dtype=jnp.int32)
    pos = ((row % (seq // dil)) * dil + row // (seq // dil)).astype(F32)

    def tables(features, axis):
        f = jnp.arange(features, dtype=jnp.int32)
        inv = ROPE_THETA ** (-(f % half).astype(F32) * (2.0 / ATTN_HEAD_DIM))
        sign = jnp.where(f % ATTN_HEAD_DIM < half, -1.0, 1.0).astype(F32)
        ang = jnp.expand_dims(pos, axis) * jnp.expand_dims(inv, 1 - axis)
        return jnp.cos(ang), jnp.sin(ang) * jnp.expand_dims(sign, 1 - axis)

    return tables(LANES, 1), tables(ATTN_HEAD_DIM, 0)


def _rotate(x, c, ss, sign=1.0):
    w = x.shape[-1]
    half = ATTN_HEAD_DIM // 2
    lane = lax.broadcasted_iota(jnp.int32, x.shape, 1)
    first = (lane % ATTN_HEAD_DIM) < half
    partner = jnp.where(first, pltpu.roll(x, w - half, 1), pltpu.roll(x, half, 1))
    reps = w // LANES
    if reps > 1:
        c = jnp.concatenate([c] * reps, axis=1)
        ss = jnp.concatenate([ss] * reps, axis=1)
    return x * c + sign * (partner * ss)


def _rotate_t(xt, ct, sst):
    half = ATTN_HEAD_DIM // 2
    heads = xt.shape[0] // ATTN_HEAD_DIM
    parts = []
    for h in range(heads):
        lo = h * ATTN_HEAD_DIM
        parts += [xt[lo + half:lo + 2 * half], xt[lo:lo + half]]
    return (xt * jnp.concatenate([ct] * heads, axis=0)
            + jnp.concatenate(parts, axis=0) * jnp.concatenate([sst] * heads, axis=0))


def _ln_epi(acc, x, g, b, *prev):
    if prev:
        x = x * prev[0] + prev[1]
    u = ALPHA * x + acc
    mu = jnp.mean(u, axis=-1, keepdims=True)
    uc = u - mu
    var = jnp.mean(uc * uc, axis=-1, keepdims=True)
    rstd = lax.rsqrt(var + LN_EPS)
    xh = uc * rstd
    return xh * g + b, xh, rstd


def _ln_grad(dy, xh, rstd, g):
    dxh = dy * g
    m1 = jnp.mean(dxh, axis=-1, keepdims=True)
    m2 = jnp.mean(dxh * xh, axis=-1, keepdims=True)
    du = rstd * (dxh - m1 - xh * m2)
    return du, du, jnp.sum(dy * xh, axis=0, keepdims=True), jnp.sum(dy, axis=0, keepdims=True)


def _ln_grad_epi(acc, du_next, xh, rstd, g):
    return _ln_grad(acc + ALPHA * du_next, xh, rstd, g)


def _ln_loss_epi(acc, x_in, target, g, b, *prev):
    out, xh, rstd = _ln_epi(acc, x_in, g, b, *prev)
    e = out - target
    du, _, dg, db = _ln_grad(e * (1.0 / e.shape[-1]), xh, rstd, g)
    return du, du, dg, db, jnp.sum(e * e, axis=0, keepdims=True)


POS_BLK = 2048


def _class_rows(r, dil):
    return pl.ds(r, POS_BLK // dil, stride=dil) if dil > 1 else pl.ds(0, POS_BLK)


def _class_view(a, dil):
    s, w = a.shape
    return a.reshape(dil, s // dil, w)


def _class_spec(dil, all_tiles=True):
    if all_tiles:
        return pl.BlockSpec((dil, POS_BLK // dil, LANES), lambda i, t: (0, i, t))
    return pl.BlockSpec((dil, POS_BLK // dil, LANES), lambda i, t: (0, i, 0))


def _pos_spec(all_tiles=True):
    if all_tiles:
        return pl.BlockSpec((POS_BLK, LANES), lambda i, t: (i, t))
    return pl.BlockSpec((POS_BLK, LANES), lambda i, t: (i, 0))


def _prep_x(x, deps=()):
    s, d = x.shape

    def body(x_ref, *refs):
        outs = refs[len(deps):]
        for dil, o_ref in zip(DILATIONS, outs):
            for r in range(dil):
                o_ref[r] = x_ref[_class_rows(r, dil), :].astype(BF16)

    outs = pl.pallas_call(
        body, name="prep_x", grid=(s // POS_BLK, d // LANES),
        in_specs=[_pos_spec()] + [pl.BlockSpec(memory_space=pl.ANY)] * len(deps),
        out_specs=[_class_spec(dil) for dil in DILATIONS],
        out_shape=[jax.ShapeDtypeStruct((dil, s // dil, d), BF16) for dil in DILATIONS],
        compiler_params=_params(("parallel", "parallel")),
    )(x, *deps)
    return [o.reshape(s, d) for o in outs]


def _head_expand_matrix():
    h = lax.broadcasted_iota(jnp.int32, (LANES, D_MODEL), 0)
    l = lax.broadcasted_iota(jnp.int32, (LANES, D_MODEL), 1)
    return (l // ATTN_HEAD_DIM == h).astype(BF16)


def _attn_fwd(qkv, dil, *, name):
    _, s, d = qkv.shape
    nq = s // ATTN_BLK
    per = nq // dil
    tiles = d // LANES

    def body(q_ref, kc_ref, kp_ref, vc_ref, vp_ref, o_ref, lse_ref):
        qb = pl.program_id(0)
        first = (qb % per) == 0
        qi = lax.broadcasted_iota(jnp.int32, (ATTN_BLK, 2 * ATTN_BLK), 0)
        kj = lax.broadcasted_iota(jnp.int32, (ATTN_BLK, 2 * ATTN_BLK), 1)
        dist = qi + ATTN_BLK - kj
        valid = (dist >= 0) & (dist <= ATTN_BLK) & ((kj >= ATTN_BLK) | jnp.logical_not(first))
        lane = lax.broadcasted_iota(jnp.int32, (ATTN_BLK, LANES), 1)
        lse_tile = jnp.zeros((ATTN_BLK, LANES), F32)
        zero = jnp.zeros((), BF16)
        in_head = [(lane // ATTN_HEAD_DIM) == hh for hh in range(2)]
        for t0 in range(0, tiles, ATTN_FWD_TILE_GROUP):
            group = range(t0, t0 + ATTN_FWD_TILE_GROUP)
            heads = [(t, hh) for t in group for hh in range(2)]
            cols = {t: pl.ds(t * LANES, LANES) for t in group}
            k2 = {t: jnp.concatenate([kp_ref[:, cols[t]], kc_ref[:, cols[t]]], axis=0) for t in group}
            v2 = {t: jnp.concatenate([vp_ref[:, cols[t]], vc_ref[:, cols[t]]], axis=0) for t in group}
            sc = {(t, hh): jnp.where(valid, _dot_nt(jnp.where(in_head[hh], q_ref[:, cols[t]], zero), k2[t]),
                                     -jnp.inf) for t, hh in heads}
            mx = {i: jnp.max(sc[i], axis=-1, keepdims=True) for i in heads}
            p = {i: jnp.exp(sc[i] - mx[i]) for i in heads}
            l = {i: jnp.sum(p[i], axis=-1, keepdims=True) for i in heads}
            oh = {i: _dot(p[i].astype(BF16), v2[i[0]]) / l[i] for i in heads}
            for t in group:
                o_ref[:, cols[t]] = jnp.where(in_head[0], oh[t, 0], oh[t, 1])
                for hh in range(2):
                    lse_tile = jnp.where(lane == 2 * t + hh, mx[t, hh] + jnp.log(l[t, hh]), lse_tile)
        lse_ref[...] = lse_tile

    def blk(piece, prev):
        if prev:
            return pl.BlockSpec((None, ATTN_BLK, d), lambda i: (piece, jnp.maximum(i - 1, 0), 0))
        return pl.BlockSpec((None, ATTN_BLK, d), lambda i: (piece, i, 0))

    return pl.pallas_call(
        body, name=name, grid=(nq,),
        in_specs=[blk(0, False), blk(1, False), blk(1, True), blk(2, False), blk(2, True)],
        out_specs=[pl.BlockSpec((ATTN_BLK, d), lambda i: (i, 0)), pl.BlockSpec((ATTN_BLK, LANES), lambda i: (i, 0))],
        out_shape=[jax.ShapeDtypeStruct((s, d), F32), jax.ShapeDtypeStruct((s, LANES), F32)],
        compiler_params=_params(("parallel",)),
    )(qkv, qkv, qkv, qkv, qkv)


def _attn_combine(os_, lses):
    s, d = os_[0].shape
    sel = jnp.concatenate([_head_expand_matrix()] * 3, axis=0)

    def body(o0, o1, o2, l0, l1, l2, sel_ref, of_ref, ob_ref, lt_ref, o_pos, l_pos):
        for g, (dil, o_ref, l_ref) in enumerate(zip(DILATIONS, (o0, o1, o2), (l0, l1, l2))):
            for r in range(dil):
                o_pos[g, _class_rows(r, dil), :] = o_ref[r]
                l_pos[g, _class_rows(r, dil), :] = l_ref[r]
        la, lb_, lc = l_pos[0], l_pos[1], l_pos[2]
        mx = jnp.maximum(jnp.maximum(la, lb_), lc)
        es = (jnp.exp(la - mx), jnp.exp(lb_ - mx), jnp.exp(lc - mx))
        z = es[0] + es[1] + es[2]
        lt_ref[...] = mx + jnp.log(z)
        acc = jnp.zeros((POS_BLK, LANES), F32)
        for g in range(3):
            acc += _exact_dot_r(es[g] / z, sel_ref[...]) * o_pos[g]
        of_ref[...] = acc
        ob_ref[...] = acc.astype(BF16)

    return pl.pallas_call(
        body, name="attn_combine", grid=(s // POS_BLK, d // LANES),
        in_specs=[_class_spec(dil) for dil in DILATIONS] + [_class_spec(dil, False) for dil in DILATIONS]
        + [pl.BlockSpec((3 * LANES, LANES), lambda i, t: (0, t))],
        out_specs=[_pos_spec(), _pos_spec(), _pos_spec(False)],
        out_shape=[jax.ShapeDtypeStruct((s, d), F32), jax.ShapeDtypeStruct((s, d), BF16),
                   jax.ShapeDtypeStruct((s, LANES), F32)],
        scratch_shapes=[pltpu.VMEM((3, POS_BLK, LANES), F32), pltpu.VMEM((3, POS_BLK, LANES), F32)],
        compiler_params=_params(("parallel", "arbitrary")),
    )(*[_class_view(o, dil) for o, dil in zip(os_, DILATIONS)],
      *[_class_view(l, dil) for l, dil in zip(lses, DILATIONS)], sel)


ATTN_TILE_GROUP = 2
ATTN_FWD_TILE_GROUP = 4
AUX_PER_TILE = 12


def _aux_placement():
    h = lax.broadcasted_iota(jnp.int32, (6, LANES, LANES), 1)
    l = lax.broadcasted_iota(jnp.int32, (6, LANES, LANES), 2)
    j = lax.broadcasted_iota(jnp.int32, (6, LANES, LANES), 0)
    target = AUX_PER_TILE * (h // 2) + 3 * (h % 2) + jnp.where(j < 3, j, 3 + j)
    return ((l == target) & (h < ATTN_HEADS)).astype(BF16)


def _attn_bwd_prep(do, o, lse):
    s, d = do.shape
    tiles = d // LANES
    sel_t = jnp.tile(_head_expand_matrix().T.reshape(tiles, LANES, LANES), (1, 3, 1))

    def body(do_ref, o_ref, l_ref, sel_ref, place_ref, *refs):
        outs, delta, aux = refs[:9], refs[9], refs[10]
        t = pl.program_id(1)
        part = _exact_dot_r(do_ref[...] * o_ref[...], sel_ref[...])

        @pl.when(t == 0)
        def _():
            delta[...] = part

        @pl.when(t > 0)
        def _():
            delta[...] += part

        for g, dil in enumerate(DILATIONS):
            for r in range(dil):
                blk = do_ref[_class_rows(r, dil), :].astype(BF16)
                outs[g][r] = blk
                outs[3 + g][r] = blk.T

        @pl.when(t == tiles - 1)
        def _():
            pieces = _split3(l_ref[...]) + _split3(delta[...])
            aux[...] = _dot(jnp.concatenate(pieces, axis=1), place_ref[...])
            for g, dil in enumerate(DILATIONS):
                for r in range(dil):
                    outs[6 + g][r] = aux[_class_rows(r, dil), :].astype(BF16)

    outs = pl.pallas_call(
        body, name="attn_bwd_prep", grid=(s // POS_BLK, tiles),
        in_specs=[_pos_spec(), _pos_spec(), _pos_spec(False),
                  pl.BlockSpec((None, 3 * LANES, LANES), lambda i, t: (t, 0, 0)),
                  pl.BlockSpec((6 * LANES, LANES), lambda i, t: (0, 0))],
        out_specs=[_class_spec(dil) for dil in DILATIONS]
        + [pl.BlockSpec((dil, LANES, POS_BLK // dil), lambda i, t: (0, t, i)) for dil in DILATIONS]
        + [_class_spec(dil, False) for dil in DILATIONS],
        out_shape=[jax.ShapeDtypeStruct((dil, s // dil, d), BF16) for dil in DILATIONS]
        + [jax.ShapeDtypeStruct((dil, d, s // dil), BF16) for dil in DILATIONS]
        + [jax.ShapeDtypeStruct((dil, s // dil, LANES), BF16) for dil in DILATIONS],
        scratch_shapes=[pltpu.VMEM((POS_BLK, LANES), F32), pltpu.VMEM((POS_BLK, LANES), F32)],
        compiler_params=_params(("parallel", "arbitrary")),
    )(do, o, lse, sel_t, _aux_placement().reshape(6 * LANES, LANES))
    return ([a.reshape(s, d) for a in outs[0:3]], list(outs[3:6]), [a.reshape(s, LANES) for a in outs[6:9]])


def _attn_bwd(qkv, qkv_t, do, do_t, aux, tables, dil, *, name):
    _, s, d = qkv.shape
    nq = s // ATTN_BLK
    per = nq // dil
    tiles = d // LANES
    half = ATTN_HEAD_DIM

    def body(qd_ref, qo_ref, k_ref, qtd_ref, qto_ref, kt_ref, vt_ref, dod_ref, doo_ref, dotd_ref, doto_ref,
             auxd_ref, auxo_ref, c_ref, ss_ref, out_ref, carry):
        kb = pl.program_id(0)

        @pl.when(kb == 0)
        def _():
            carry[...] = jnp.zeros_like(carry)

        has_next = (kb % per) != (per - 1)
        qi = lax.broadcasted_iota(jnp.int32, (ATTN_BLK, 2 * ATTN_BLK), 0)
        kj = lax.broadcasted_iota(jnp.int32, (ATTN_BLK, 2 * ATTN_BLK), 1) % ATTN_BLK
        valid_d = kj <= qi
        valid_o = (kj >= qi) & has_next
        lane = lax.broadcasted_iota(jnp.int32, (ATTN_BLK, LANES), 1)
        row = lax.broadcasted_iota(jnp.int32, (LANES, ATTN_BLK), 0)
        side = lax.broadcasted_iota(jnp.int32, (LANES, 2 * ATTN_BLK), 0)
        first = lax.broadcasted_iota(jnp.int32, (LANES, 2 * ATTN_BLK), 1) < ATTN_BLK
        c, ss = c_ref[...], ss_ref[...]
        zero = jnp.zeros((), BF16)
        sides = ((qd_ref, qtd_ref, dod_ref, dotd_ref, auxd_ref[...], valid_d),
                 (qo_ref, qto_ref, doo_ref, doto_ref, auxo_ref[...], valid_o))
        both = (0, 1)

        def head_halves(x, index):
            axis = 0 if index is lane else 1
            return jnp.concatenate([jnp.where(index < half, x, zero), jnp.where(index >= half, x, zero)], axis=axis)

        for t0 in range(0, tiles, ATTN_TILE_GROUP):
            group = range(t0, t0 + ATTN_TILE_GROUP)
            cols = {t: pl.ds(t * LANES, LANES) for t in group}
            kk, kk_t, vv_t = {}, {}, {}
            for t in group:
                base = AUX_PER_TILE * t
                hit = lambda lo: ((first & (side >= base + lo) & (side < base + lo + 3))
                                  | (jnp.logical_not(first) & (side >= base + lo + 3) & (side < base + lo + 6)))
                kk[t] = head_halves(k_ref[:, cols[t]], lane)
                kk_t[t] = jnp.concatenate([head_halves(kt_ref[cols[t], :], row),
                                           jnp.where(hit(0), -1.0, 0.0).astype(BF16)], axis=0)
                vv_t[t] = jnp.concatenate([head_halves(vt_ref[cols[t], :], row),
                                           jnp.where(hit(6), -1.0, 0.0).astype(BF16)], axis=0)
            sc = {(t, w): _dot(jnp.concatenate([sides[w][0][:, cols[t]], sides[w][4]], axis=1), kk_t[t])
                  for t in group for w in both}
            dpd = {(t, w): _dot(jnp.concatenate([sides[w][2][:, cols[t]], sides[w][4]], axis=1), vv_t[t])
                   for t in group for w in both}
            p = {i: jnp.where(sides[i[1]][5], jnp.exp(sc[i]), 0.0) for i in sc}
            ds = {i: (p[i] * dpd[i]).astype(BF16) for i in sc}
            pb = {i: p[i].astype(BF16) for i in sc}
            dv_t = {t: sum(_dot(sides[w][3][cols[t], :], pb[t, w]) for w in both) for t in group}
            dk_t = {t: sum(_dot(sides[w][1][cols[t], :], ds[t, w]) for w in both) for t in group}
            dq = {i: _dot(ds[i], kk[i[0]]) * ATTN_SCALE for i in sc}
            for t in group:
                dq_now = carry[:, cols[t]] + dq[t, 0]
                carry[:, cols[t]] = dq[t, 1]
                dk = jnp.where(row < half, dk_t[t][:, :ATTN_BLK], dk_t[t][:, ATTN_BLK:]).T
                dv = jnp.where(row < half, dv_t[t][:, :ATTN_BLK], dv_t[t][:, ATTN_BLK:]).T
                out_ref[0, :, cols[t]] = _rotate(dq_now, c, ss, -1.0).astype(BF16)
                out_ref[1, :, cols[t]] = _rotate(dk, c, ss, -1.0).astype(BF16)
                out_ref[2, :, cols[t]] = dv.astype(BF16)

    def nxt(i):
        return jnp.minimum(i + 1, nq - 1)

    def piece(p, shift):
        if shift:
            return pl.BlockSpec((None, ATTN_BLK, d), lambda i: (p, nxt(i), 0))
        return pl.BlockSpec((None, ATTN_BLK, d), lambda i: (p, i, 0))

    def piece_t(p, shift):
        if shift:
            return pl.BlockSpec((None, d, ATTN_BLK), lambda i: (p, 0, nxt(i)))
        return pl.BlockSpec((None, d, ATTN_BLK), lambda i: (p, 0, i))

    def rows(width, shift):
        if shift:
            return pl.BlockSpec((ATTN_BLK, width), lambda i: (nxt(i), 0))
        return pl.BlockSpec((ATTN_BLK, width), lambda i: (i, 0))

    def do_t_spec(shift):
        if shift:
            return pl.BlockSpec((None, d, ATTN_BLK), lambda i: (nxt(i) // per, 0, nxt(i) % per))
        return pl.BlockSpec((None, d, ATTN_BLK), lambda i: (i // per, 0, i % per))

    return pl.pallas_call(
        body, name=name, grid=(nq,),
        in_specs=[piece(0, False), piece(0, True), piece(1, False),
                  piece_t(0, False), piece_t(0, True), piece_t(1, False), piece_t(2, False),
                  rows(d, False), rows(d, True), do_t_spec(False), do_t_spec(True),
                  rows(LANES, False), rows(LANES, True), rows(LANES, False), rows(LANES, False)],
        out_specs=pl.BlockSpec((3, ATTN_BLK, d), lambda i: (0, i, 0)),
        out_shape=jax.ShapeDtypeStruct((3, s, d), BF16),
        scratch_shapes=[pltpu.VMEM((ATTN_BLK, d), F32)],
        compiler_params=_params(("arbitrary",)),
    )(qkv, qkv, qkv, qkv_t, qkv_t, qkv_t, qkv_t, do, do, do_t, do_t, aux, aux, *tables)


def _dx_combine(du, parts):
    s, d = du.shape

    def body(du_ref, p0, p1, p2, out_ref):
        out_ref[...] = ALPHA * du_ref[...] + p0[0]
        for dil, p_ref in zip(DILATIONS[1:], (p1, p2)):
            for r in range(dil):
                out_ref[_class_rows(r, dil), :] += p_ref[r]

    return pl.pallas_call(
        body, name="dx_combine", grid=(s // POS_BLK, d // LANES),
        in_specs=[_pos_spec()] + [_class_spec(dil) for dil in DILATIONS], out_specs=_pos_spec(),
        out_shape=jax.ShapeDtypeStruct((s, d), F32),
        compiler_params=_params(("parallel", "parallel")),
    )(du, *[_class_view(p, dil) for p, dil in zip(parts, DILATIONS)])


HGRN_ROWS = 2048


def _tri(lower, copies=1):
    i = lax.broadcasted_iota(jnp.int32, (HGRN_CHUNK, HGRN_CHUNK * copies), 0)
    j = lax.broadcasted_iota(jnp.int32, (HGRN_CHUNK, HGRN_CHUNK * copies), 1) % HGRN_CHUNK
    return (j <= i) if lower else (j >= i)


def _hgrn_gates(qr, z, lb):
    sq = jax.nn.sigmoid(qr)
    e = jnp.exp(-jnp.abs(z))
    big = 1.0 / (1.0 + e)
    small = e * big
    sg = jnp.where(z >= 0, big, small)
    sn = jnp.where(z >= 0, small, big)
    return sq, qr * sq, sg, sn, lb + (1.0 - lb) * sg, (1.0 - lb) * sn


def _hgrn_fwd(p2, lb, norm_g):
    _, s, d = p2.shape
    nblk = s // HGRN_ROWS
    cps = HGRN_ROWS // HGRN_CHUNK

    def body(p_ref, lb_ref, g_ref, o_ref, on_ref, st_ref, state):
        @pl.when(pl.program_id(1) == 0)
        def _():
            state[...] = jnp.zeros_like(state)

        chunks = [pl.ds(c * HGRN_CHUNK, HGRN_CHUNK) for c in range(cps)]
        ltri = _tri(True)
        lsel = _tri(True, 3).astype(BF16)
        _, q, _, _, f, key = _hgrn_gates(p_ref[0], p_ref[1], lb_ref[...])
        lf = jnp.log(f)
        v = [p_ref[2, rows, :].astype(BF16) for rows in chunks]
        b = [_exact_dot(lsel, lf[c * HGRN_CHUNK:(c + 1) * HGRN_CHUNK]) for c in range(cps)]
        b_last = [bc[HGRN_CHUNK - 1:HGRN_CHUNK, :] for bc in b]
        q = [q[c * HGRN_CHUNK:(c + 1) * HGRN_CHUNK] for c in range(cps)]
        key = [key[c * HGRN_CHUNK:(c + 1) * HGRN_CHUNK] for c in range(cps)]
        qd = [(q[c] * jnp.exp(b[c])).astype(BF16) for c in range(cps)]
        kd = [(key[c] * jnp.exp(-b[c])).astype(BF16) for c in range(cps)]
        k2 = [(key[c] * jnp.exp(b_last[c] - b[c])).astype(BF16) for c in range(cps)]
        a = [jnp.where(ltri, _dot_nt(qd[c], kd[c]), 0.0).astype(BF16) for c in range(cps)]
        kv = [_dot_tn(v[c], k2[c]) for c in range(cps)]
        st, sts = state[...], []
        for c in range(cps):
            sts.append(st)
            st_ref[0, c] = st
            st = st * jnp.exp(b_last[c]) + kv[c]
        state[...] = st
        gv = g_ref[...]
        for c in range(cps):
            o = _dot(a[c], v[c]) + _dot_nt(qd[c], sts[c].astype(BF16))
            o_ref[chunks[c], :] = o
            r = lax.rsqrt(jnp.mean(o * o, axis=-1, keepdims=True) + RMS_EPS)
            on_ref[chunks[c], :] = (o * r * gv).astype(BF16)

    vec = pl.BlockSpec((1, LANES), lambda h, c: (0, h))
    col = pl.BlockSpec((HGRN_ROWS, LANES), lambda h, c: (c, h))
    return pl.pallas_call(
        body, name="hgrn_fwd", grid=(HGRN_HEADS, nblk),
        in_specs=[pl.BlockSpec((3, HGRN_ROWS, LANES), lambda h, c: (0, c, h)), vec, vec],
        out_specs=[col, col, pl.BlockSpec((1, cps, LANES, LANES), lambda h, c: (h, c, 0, 0))],
        out_shape=[jax.ShapeDtypeStruct((s, d), F32), jax.ShapeDtypeStruct((s, d), BF16),
                   jax.ShapeDtypeStruct((HGRN_HEADS, s // HGRN_CHUNK, LANES, LANES), F32)],
        scratch_shapes=[pltpu.VMEM((LANES, LANES), F32)],
        compiler_params=_params(("parallel", "arbitrary")),
    )(p2, lb, norm_g)


def _hgrn_bwd(p2, lb, norm_g, o_raw, states, dyn):
    _, s, d = p2.shape
    nblk = s // HGRN_ROWS
    cps = HGRN_ROWS // HGRN_CHUNK

    def body(p_ref, lb_ref, g_ref, o_ref, st_ref, dy_ref, dp_ref, dg_ref, dlb_ref, dstate):
        @pl.when(pl.program_id(1) == 0)
        def _():
            dstate[...] = jnp.zeros_like(dstate)
            dg_ref[...] = jnp.zeros_like(dg_ref)
            dlb_ref[...] = jnp.zeros_like(dlb_ref)

        n = cps
        cut = lambda t: [t[c * HGRN_CHUNK:(c + 1) * HGRN_CHUNK] for c in range(n)]
        chunks = [pl.ds(c * HGRN_CHUNK, HGRN_CHUNK) for c in range(n)]
        lbv = lb_ref[...]
        ltri = _tri(True)
        lsel = _tri(True, 3).astype(BF16)
        usel = _tri(False, 3).astype(BF16)
        last_row = lax.broadcasted_iota(jnp.int32, (HGRN_CHUNK, LANES), 0) == HGRN_CHUNK - 1
        qr, z = p_ref[0], p_ref[1]
        sq, q, sg, sn, f, key = _hgrn_gates(qr, z, lbv)
        lf = jnp.log(f)
        v = [p_ref[2, rows, :].astype(BF16) for rows in chunks]
        o, dyv = o_ref[...], dy_ref[...]
        r = lax.rsqrt(jnp.mean(o * o, axis=-1, keepdims=True) + RMS_EPS)
        oh = o * r
        doh = dyv * g_ref[...]
        do = cut((r * (doh - oh * jnp.mean(doh * oh, axis=-1, keepdims=True))).astype(BF16))
        dg_ref[...] += jnp.sum(dyv * oh, axis=0, keepdims=True)
        b = [_exact_dot(lsel, lfc) for lfc in cut(lf)]
        b_last = [bc[HGRN_CHUNK - 1:HGRN_CHUNK, :] for bc in b]
        q, key = cut(q), cut(key)
        eb = [jnp.exp(bc) for bc in b]
        enb = [jnp.exp(-bc) for bc in b]
        e2 = [jnp.exp(b_last[c] - b[c]) for c in range(n)]
        dec = [jnp.exp(bl) for bl in b_last]
        qd_f = [q[c] * eb[c] for c in range(n)]
        kd_f = [key[c] * enb[c] for c in range(n)]
        k2_f = [key[c] * e2[c] for c in range(n)]
        qd, kd, k2 = ([t.astype(BF16) for t in ts] for ts in (qd_f, kd_f, k2_f))
        a = [jnp.where(ltri, _dot_nt(qd[c], kd[c]), 0.0).astype(BF16) for c in range(n)]
        da = [jnp.where(ltri, _dot_nt(do[c], v[c]), 0.0).astype(BF16) for c in range(n)]
        grow = [_dot_tn(do[c], qd[c]) for c in range(n)]
        dst, dsts = dstate[...], [None] * n
        for c in reversed(range(n)):
            dsts[c] = dst
            dst = dst * dec[c] + grow[c]
        dstate[...] = dst
        st = [st_ref[0, c] for c in range(n)]
        dstb = [t.astype(BF16) for t in dsts]
        dv = [_dot_tn(a[c], do[c]) + _dot_nt(k2[c], dstb[c]) for c in range(n)]
        dqd = [_dot(da[c], kd[c]) + _dot(do[c], st[c].astype(BF16)) for c in range(n)]
        dkd = [_dot_tn(da[c], qd[c]) for c in range(n)]
        dk2 = [_dot(v[c], dstb[c]) for c in range(n)]
        db = []
        for c in range(n):
            ddec = jnp.sum(dsts[c] * st[c], axis=0, keepdims=True)
            db_last = jnp.sum(dk2[c] * k2_f[c], axis=0, keepdims=True) + ddec * dec[c]
            db.append(dqd[c] * qd_f[c] - dkd[c] * kd_f[c] - dk2[c] * k2_f[c] + jnp.where(last_row, db_last, 0.0))
        dlf = [_exact_dot(usel, dbc) for dbc in db]
        f, sg, sn, sq, qr = cut(f), cut(sg), cut(sn), cut(sq), cut(qr)
        dlb_acc = jnp.zeros((1, LANES), F32)
        for c in range(n):
            dkey = dkd[c] * enb[c] + dk2[c] * e2[c]
            common = dlf[c] / f[c] - dkey
            dlb_acc += jnp.sum(common * sn[c], axis=0, keepdims=True)
            dp_ref[0, chunks[c], :] = (dqd[c] * eb[c] * (sq[c] * (1.0 + qr[c] * (1.0 - sq[c])))).astype(BF16)
            dp_ref[1, chunks[c], :] = (common * ((1.0 - lbv) * sg[c] * sn[c])).astype(BF16)
            dp_ref[2, chunks[c], :] = dv[c].astype(BF16)
        dlb_ref[...] += dlb_acc

    def rev(c):
        return nblk - 1 - c

    vec = pl.BlockSpec((1, LANES), lambda h, c: (0, h))
    col = pl.BlockSpec((HGRN_ROWS, LANES), lambda h, c: (rev(c), h))
    p3 = pl.BlockSpec((3, HGRN_ROWS, LANES), lambda h, c: (0, rev(c), h))
    return pl.pallas_call(
        body, name="hgrn_bwd", grid=(HGRN_HEADS, nblk),
        in_specs=[p3, vec, vec, col, pl.BlockSpec((1, cps, LANES, LANES), lambda h, c: (h, rev(c), 0, 0)), col],
        out_specs=[p3, vec, vec],
        out_shape=[jax.ShapeDtypeStruct((3, s, d), BF16), jax.ShapeDtypeStruct((1, d), F32),
                   jax.ShapeDtypeStruct((1, d), F32)],
        scratch_shapes=[pltpu.VMEM((LANES, LANES), F32)],
        compiler_params=_params(("parallel", "arbitrary")),
    )(p2, lb, norm_g, o_raw, states, dyn)


SMALL_ROWS = 16
FUSED_TM = 512


def _relu2(h):
    r = jnp.maximum(h.astype(F32), 0.0)
    return r * r


def _dact_epi(acc, h):
    return (acc * (2.0 * jnp.maximum(h.astype(F32), 0.0)),)


def _rope_epi(j, acc, ct, sst):
    acc_t = acc.T
    rot_t = _rotate_t(acc_t, ct, sst) * jnp.where(j == 0, ATTN_SCALE, 1.0)
    out_t = jnp.where(j >= 2, acc_t, rot_t).astype(BF16)
    return out_t.T, out_t


def _local_step(x, target, get_w, small, on_grads, deps=()):
    s, d = x.shape
    lb = small["lb"]
    gq = 3 * d
    gt = gq // 1024

    def ln_vecs(which, layer):
        return small[f"ln_{which}_g"][layer:layer + 1], small[f"ln_{which}_b"][layer:layer + 1]

    def ln_of(a, w, x_in, which, layer, *, name, k, a_pre=None, deps=(), prev=None):
        vecs = ln_vecs(which, layer) + (ln_vecs(*prev) if prev else ())
        return _mm(a, w, "nn", name=name, m=s, n=d, k=k, tm=FUSED_TM, tk=k, a_pre=a_pre, out_dtypes=(BF16, F32),
                   row_outs=((F32, 1),), epi=_ln_epi, tile_extras=(x_in,), vec_extras=vecs, deps=deps)

    def ln_grad_of(a, w, du_next, xh, rstd, which, layer, *, name, k, a_split=1, deps=()):
        return _mm(a, w, "nt", name=name, m=s, n=d, k=k, tm=FUSED_TM, tk=k, a_split=a_split,
                   out_dtypes=(F32, BF16), vec_outs=2, epi=_ln_grad_epi, tile_extras=(du_next, xh), row_extras=(rstd,),
                   vec_extras=(small[f"ln_{which}_g"][layer:layer + 1],), deps=deps)

    def ffn_fwd(xb, xh_in, w_up, w_down, tag):
        h = _mm(xb, w_up, "nn", name=f"ffn_up_{tag}", m=s, n=D_FF, k=d, tm=2048, out_dtypes=(BF16,))
        return (h,) + tuple(ln_of(h, w_down, xh_in, "ffn", tag, name=f"ffn_down_{tag}", k=D_FF, a_pre=_relu2,
                                  prev=("mix", tag)))

    def ffn_bwd(du, dub, xb, h, w_up, w_down, xh, rstd, tag):
        dh = _mm(dub, w_down, "nt", name=f"ffn_dact_{tag}", m=s, n=D_FF, k=d, tm=2048, out_dtypes=(BF16,), epi=_dact_epi,
                 tile_extras=(h,))
        g_down = _mm(h, dub, "tn", name=f"ffn_gdown_{tag}", m=D_FF, n=d, k=s, tm=512, tk=s, a_pre=_relu2,
                     out_dtypes=(BF16,))
        g_up = _mm(xb, dh, "tn", name=f"ffn_gup_{tag}", m=d, n=D_FF, k=s, tk=s, out_dtypes=(BF16,))
        after = on_grads(f"ffn{tag}", {f"ffn_w_down{tag}": g_down, f"ffn_w_up{tag}": g_up})
        return ln_grad_of(dh, w_up, du, xh, rstd, "mix", tag, name=f"ffn_dx_{tag}", k=D_FF, deps=after)

    xs = _prep_x(x, deps)
    tabs, tabs_t = zip(*[_rope_tables(s, dil) for dil in DILATIONS])
    w_ain, after = get_w("attn_w_in", xs + [t for tab in tabs + tabs_t for t in tab])
    qkvs, qkv_ts, o_parts, lse_parts = [], [], [], []
    for g, dil in enumerate(DILATIONS):
        qkv, qkv_t = _mm(xs[g], w_ain, "nn", name=f"attn_in_{g}", m=s, n=gq, k=d, b_col_off=g * gt, out_split=3,
                         out_dtypes=(BF16,), epi=_rope_epi, epi_wants_j=True, row_extras=tabs_t[g], t_out=True,
                         deps=after if g == 0 else ())
        qkvs.append(qkv)
        qkv_ts.append(qkv_t)
        o_g, lse_g = _attn_fwd(qkv, dil, name=f"attn_fwd_{g}")
        o_parts.append(o_g)
        lse_parts.append(lse_g)
    o_f, o_b, lse_t = _attn_combine(o_parts, lse_parts)
    w_aout, after = get_w("attn_w_out", o_b)
    x1b, xh1, r1 = ln_of(o_b, w_aout, x, "mix", 0, name="attn_out", k=d, deps=after)
    w_up0, w_down0 = get_w("ffn_w_up0", o_b)[0], get_w("ffn_w_down0", o_b)[0]
    h0, x2b, xh2, r2 = ffn_fwd(x1b, xh1, w_up0, w_down0, 0)
    w_hin, w_hout, norm_g = get_w("hgrn_w_in", x2b)[0], get_w("hgrn_w_out", x2b)[0], get_w("hgrn_norm_g", x2b)[0]
    p2 = _mm(x2b, w_hin, "nn", name="hgrn_in", m=s, n=3 * d, k=d, tm=2048, out_split=3)
    o_raw, o_n, states = _hgrn_fwd(p2, lb, norm_g)
    w_up1, w_down1 = get_w("ffn_w_up1", o_n)[0], get_w("ffn_w_down1", o_n)[0]
    x3b, xh3, r3 = ln_of(o_n, w_hout, xh2, "mix", 1, name="hgrn_out", k=d, prev=("ffn", 0))
    h1 = _mm(x3b, w_up1, "nn", name="ffn_up_1", m=s, n=D_FF, k=d, tm=2048, out_dtypes=(BF16,))
    du, dub, dg_ffn1, db_ffn1, sq = _mm(
        h1, w_down1, "nn", name="ffn_down_1", m=s, n=d, k=D_FF, tm=FUSED_TM, tk=D_FF, a_pre=_relu2,
        out_dtypes=(F32, BF16), vec_outs=3, epi=_ln_loss_epi, tile_extras=(xh3, target),
        vec_extras=ln_vecs("ffn", 1) + ln_vecs("mix", 1))
    du, dub, dg_mix1, db_mix1 = ffn_bwd(du, dub, x3b, h1, w_up1, w_down1, xh3, r3, 1)
    dyn = _mm(dub, w_hout, "nt", name="hgrn_dout", m=s, n=d, k=d)
    g_hout = _mm(o_n, dub, "tn", name="hgrn_gout", m=d, n=d, k=s, out_dtypes=(BF16,))
    dp2, d_norm_g, d_lb = _hgrn_bwd(p2, lb, norm_g, o_raw, states, dyn)
    g_hin = _mm(x2b, dp2, "tn", name="hgrn_gin", m=d, n=3 * d, k=s, tk=s, b_split=3, out_dtypes=(BF16,))
    after = on_grads("hgrn", {"hgrn_w_out": g_hout, "hgrn_w_in": g_hin})
    du, dub, dg_ffn0, db_ffn0 = ln_grad_of(dp2, w_hin, du, xh2, r2, "ffn", 0, name="hgrn_dx", k=3 * d, a_split=3,
                                           deps=after)
    du, dub, dg_mix0, db_mix0 = ffn_bwd(du, dub, x1b, h0, w_up0, w_down0, xh1, r1, 0)
    g_aout = _mm(o_b, dub, "tn", name="attn_gout", m=d, n=d, k=s, out_dtypes=(BF16,))
    sm1 = jax.nn.softmax(small["lb_logits"], axis=0)
    d_l1 = d_lb * (sm1[0:1] * sm1[1:2])
    zeros = jnp.zeros((SMALL_ROWS - 12, d), F32)
    loss_row = jnp.broadcast_to(0.5 * jnp.sum(sq) / d, (1, d))
    small_grads = jnp.concatenate([dg_mix0, dg_mix1, db_mix0, db_mix1, dg_ffn0, dg_ffn1, db_ffn0, db_ffn1,
                                   -d_l1, d_l1, d_norm_g, loss_row, zeros], axis=0)
    after = on_grads("attn_out", {"attn_w_out": g_aout, "small": small_grads})
    do = _mm(dub, w_aout, "nt", name="attn_dout", m=s, n=d, k=d, deps=after)
    do_parts, do_ts, aux_parts = _attn_bwd_prep(do, o_f, lse_t)
    g_ain, dqkvs = None, []
    for g, dil in enumerate(DILATIONS):
        dqkvs.append(_attn_bwd(qkvs[g], qkv_ts[g], do_parts[g], do_ts[g], aux_parts[g], tabs[g], dil,
                               name=f"attn_bwd_{g}"))
        g_ain = _mm(xs[g], dqkvs[g], "tn", name=f"attn_gin_{g}", m=d, n=gq, k=s, tk=s, b_split=3, out_dtypes=(BF16,),
                    out_col_off=g * gt, out_cols=3 * gq, alias=g_ain)
    after = on_grads("attn_in", {"attn_w_in": g_ain})
    dx_parts = [_mm(dqkvs[g], w_ain, "nt", name=f"attn_dx_{g}", m=s, n=d, k=gq, tm=FUSED_TM, tk=gq, a_split=3, b_k_off=g,
                    deps=after if g == 0 else ())
                for g in range(len(DILATIONS))]
    return _dx_combine(du, dx_parts)


def _mesh_place():
    x, y, c = lax.axis_index("x"), lax.axis_index("y"), lax.axis_index("c")
    return x, y, c, 4 * x + 2 * y + c


def _peer(x, y, c, k):
    px = 1 - x if (k >> 2) & 1 else x
    py = 1 - y if (k >> 1) & 1 else y
    pc = 1 - c if k & 1 else c
    return (px, py, pc), 4 * px + 2 * py + pc


def _window(ref, axis, size, idx):
    if axis is None:
        return ref
    sl = [slice(None)] * len(ref.shape)
    sl[axis] = pl.ds(idx * size, size)
    return ref.at[tuple(sl)]


_HBM = pl.BlockSpec(memory_space=pltpu.HBM)
_SEM = pl.BlockSpec(memory_space=pltpu.SEMAPHORE)
_EFFECT = pltpu.SideEffectType.DATAFLOW_SIDE_EFFECTING


def _xchg_ends(src_ref, land_ref, axis, gather, me, other):
    if gather:
        size = src_ref.shape[axis]
        return src_ref, _window(land_ref, axis, size, me), _window(land_ref, axis, size, other)
    size = None if axis is None else src_ref.shape[axis] // N_DEV
    return _window(src_ref, axis, size, other), land_ref.at[me], land_ref.at[other]


def _xchg_start(srcs, lands, axes, *, gather, name, deps=(), peers=tuple(range(1, N_DEV))):
    n = len(srcs)
    nd = len(deps)

    def body(*refs):
        src_refs, land_refs = refs[:n], refs[n:2 * n]
        send, recv = refs[2 * n + nd:3 * n + nd], refs[3 * n + nd:4 * n + nd]
        token = refs[-1]
        x, y, c, me = _mesh_place()
        for k in peers:
            peer, pidx = _peer(x, y, c, k)
            for i in range(n):
                src, dst, _ = _xchg_ends(src_refs[i], land_refs[i], axes[i], gather, me, pidx)
                pltpu.make_async_remote_copy(
                    src_ref=src, dst_ref=dst, send_sem=send[i].at[k - 1], recv_sem=recv[i].at[k - 1],
                    device_id=peer, device_id_type=pl.DeviceIdType.MESH).start()
        for i in range(n):
            src, dst, _ = _xchg_ends(src_refs[i], land_refs[i], axes[i], gather, me, me)
            pltpu.make_async_copy(src, dst, send[i].at[N_DEV - 1]).start()
        token[...] = jnp.zeros_like(token)

    bufs = list(srcs) + list(lands)
    outs = pl.pallas_call(
        body, name=name,
        out_shape=[pltpu.SemaphoreType.DMA((N_DEV,))] * (2 * n) + [pltpu.HBM(b.shape, b.dtype) for b in bufs]
        + [jax.ShapeDtypeStruct((8, LANES), F32)],
        in_specs=[_HBM] * (2 * n) + [pl.BlockSpec(memory_space=pl.ANY)] * nd,
        out_specs=[_SEM] * (2 * n) + [_HBM] * (2 * n) + [pl.BlockSpec(memory_space=pltpu.VMEM)],
        input_output_aliases={i: 2 * n + i for i in range(2 * n)},
        compiler_params=pltpu.CompilerParams(has_side_effects=_EFFECT),
    )(*[pltpu.with_memory_space_constraint(b, pltpu.HBM) for b in bufs], *deps)
    return dict(send=outs[:n], recv=outs[n:2 * n], srcs=outs[2 * n:3 * n], lands=outs[3 * n:4 * n], token=outs[-1],
                axes=list(axes), gather=gather)


SIBLING = 1
SAME_CORE = (2, 4, 6)


def _gather_relay(xc, after, *, name):
    axis = xc["axes"][0]

    def body(src_ref, land_ref, send1, recv1, *rest):
        send2, recv2 = rest[-2:]
        x, y, c, me = _mesh_place()
        sibling, _ = _peer(x, y, c, SIBLING)
        for j, k in enumerate(SAME_CORE):
            peer, pidx = _peer(x, y, c, k)
            src, _, got = _xchg_ends(src_ref, land_ref, axis, True, me, pidx)
            pltpu.make_async_remote_copy(
                src_ref=src, dst_ref=got, send_sem=send1.at[k - 1], recv_sem=recv1.at[k - 1],
                device_id=peer, device_id_type=pl.DeviceIdType.MESH).wait_recv()
            pltpu.make_async_remote_copy(
                src_ref=got, dst_ref=got, send_sem=send2.at[j], recv_sem=recv2.at[j],
                device_id=sibling, device_id_type=pl.DeviceIdType.MESH).start()

    src, land = xc["srcs"][0], xc["lands"][0]
    outs = pl.pallas_call(
        body, name=name,
        out_shape=[pltpu.HBM(src.shape, src.dtype), pltpu.HBM(land.shape, land.dtype)]
        + [pltpu.SemaphoreType.DMA((len(SAME_CORE),))] * 2,
        in_specs=[_HBM, _HBM, _SEM, _SEM] + [pl.BlockSpec(memory_space=pl.ANY)] * len(after),
        out_specs=[_HBM, _HBM, _SEM, _SEM], input_output_aliases={0: 0, 1: 1},
        compiler_params=pltpu.CompilerParams(has_side_effects=_EFFECT),
    )(src, land, xc["send"][0], xc["recv"][0], *after)
    return dict(src=outs[0], land=outs[1], send=outs[2], recv=outs[3])


def _gather_relay_wait(xc, relay, after, *, name):
    axis = xc["axes"][0]

    def body(src_ref, land_ref, send1, recv1, send2, recv2, after_ref, src_out, land_out):
        x, y, c, me = _mesh_place()
        sibling, sidx = _peer(x, y, c, SIBLING)
        for k in (SIBLING,) + SAME_CORE:
            peer, pidx = _peer(x, y, c, k)
            src, dst, got = _xchg_ends(src_ref, land_ref, axis, True, me, pidx)
            pltpu.make_async_remote_copy(
                src_ref=src, dst_ref=dst, send_sem=send1.at[k - 1], recv_sem=recv1.at[k - 1],
                device_id=peer, device_id_type=pl.DeviceIdType.MESH).wait_send()
        src, dst, got = _xchg_ends(src_ref, land_ref, axis, True, me, sidx)
        pltpu.make_async_remote_copy(
            src_ref=src, dst_ref=got, send_sem=send1.at[SIBLING - 1], recv_sem=recv1.at[SIBLING - 1],
            device_id=sibling, device_id_type=pl.DeviceIdType.MESH).wait_recv()
        src, dst, _ = _xchg_ends(src_ref, land_ref, axis, True, me, me)
        pltpu.make_async_copy(src, dst, send1.at[N_DEV - 1]).wait()
        for j, k in enumerate(SAME_CORE):
            _, pidx = _peer(x, y, c, k)
            _, qidx = _peer(x, y, c, k ^ SIBLING)
            _, _, sent = _xchg_ends(src_ref, land_ref, axis, True, me, pidx)
            _, _, got = _xchg_ends(src_ref, land_ref, axis, True, me, qidx)
            pltpu.make_async_remote_copy(
                src_ref=sent, dst_ref=sent, send_sem=send2.at[j], recv_sem=recv2.at[j],
                device_id=sibling, device_id_type=pl.DeviceIdType.MESH).wait_send()
            pltpu.make_async_remote_copy(
                src_ref=got, dst_ref=got, send_sem=send2.at[j], recv_sem=recv2.at[j],
                device_id=sibling, device_id_type=pl.DeviceIdType.MESH).wait_recv()

    outs = pl.pallas_call(
        body, name=name, out_shape=[pltpu.HBM(relay["src"].shape, relay["src"].dtype),
                                    pltpu.HBM(relay["land"].shape, relay["land"].dtype)],
        in_specs=[_HBM, _HBM, _SEM, _SEM, _SEM, _SEM, pl.BlockSpec(memory_space=pl.ANY)],
        out_specs=[_HBM, _HBM], input_output_aliases={0: 0, 1: 1},
        compiler_params=pltpu.CompilerParams(has_side_effects=_EFFECT),
    )(relay["src"], relay["land"], xc["send"][0], xc["recv"][0], relay["send"], relay["recv"], after)
    return outs[1]


def _xchg_wait(xc, items, after, *, name):
    m = len(items)
    gather = xc["gather"]
    axes = [xc["axes"][i] for i in items]

    def body(*refs):
        src_refs, land_refs = refs[:m], refs[m:2 * m]
        send, recv = refs[2 * m:3 * m], refs[3 * m:4 * m]
        x, y, c, me = _mesh_place()
        for k in range(1, N_DEV):
            peer, pidx = _peer(x, y, c, k)
            for j in range(m):
                src, dst, got = _xchg_ends(src_refs[j], land_refs[j], axes[j], gather, me, pidx)
                pltpu.make_async_remote_copy(
                    src_ref=src, dst_ref=dst, send_sem=send[j].at[k - 1], recv_sem=recv[j].at[k - 1],
                    device_id=peer, device_id_type=pl.DeviceIdType.MESH).wait_send()
                pltpu.make_async_remote_copy(
                    src_ref=src, dst_ref=got, send_sem=send[j].at[k - 1], recv_sem=recv[j].at[k - 1],
                    device_id=peer, device_id_type=pl.DeviceIdType.MESH).wait_recv()
        for j in range(m):
            src, dst, _ = _xchg_ends(src_refs[j], land_refs[j], axes[j], gather, me, me)
            pltpu.make_async_copy(src, dst, send[j].at[N_DEV - 1]).wait()

    bufs = [xc["srcs"][i] for i in items] + [xc["lands"][i] for i in items]
    sems = [xc["send"][i] for i in items] + [xc["recv"][i] for i in items]
    outs = pl.pallas_call(
        body, name=name, out_shape=[pltpu.HBM(b.shape, b.dtype) for b in bufs],
        in_specs=[_HBM] * (2 * m) + [_SEM] * (2 * m) + [pl.BlockSpec(memory_space=pl.ANY)] * len(after),
        out_specs=[_HBM] * (2 * m), input_output_aliases={j: j for j in range(2 * m)},
        compiler_params=pltpu.CompilerParams(has_side_effects=_EFFECT),
    )(*bufs, *sems, *after)
    return outs[m:]


def _cast_bf16(a, *, name):
    r, c = a.shape
    tr = min(r, 512)

    def body(a_ref, o_ref):
        o_ref[...] = a_ref[...].astype(BF16)

    spec = pl.BlockSpec((tr, c), lambda i: (i, 0))
    return pl.pallas_call(body, name=name, grid=(r // tr,), in_specs=[spec], out_specs=spec,
                          out_shape=jax.ShapeDtypeStruct((r, c), BF16), compiler_params=_params(("parallel",)))(a)


def _adamw(slabs, w, m, v, *, name):
    layers, r, c = w.shape
    tr = min(r, 256)

    def body(*refs):
        s_refs = refs[:layers]
        w_ref, m_ref, v_ref, g_ref, d_ref, mo_ref, vo_ref = refs[layers:]
        for l in range(layers):
            g = s_refs[l][0].astype(F32)
            for i in range(1, N_DEV):
                g = g + s_refs[l][i].astype(F32)
            m2 = ADAM_B1 * m_ref[l] + (1.0 - ADAM_B1) * g
            v2 = ADAM_B2 * v_ref[l] + (1.0 - ADAM_B2) * (g * g)
            m_hat = m2 / (1.0 - ADAM_B1 ** ADAM_STEP)
            v_hat = v2 / (1.0 - ADAM_B2 ** ADAM_STEP)
            g_ref[l] = g
            d_ref[l] = -ADAM_LR * (m_hat / (jnp.sqrt(v_hat) + ADAM_EPS) + ADAM_WD * w_ref[l])
            mo_ref[l] = m2
            vo_ref[l] = v2

    spec = pl.BlockSpec((layers, tr, c), lambda i: (0, i, 0))
    return pl.pallas_call(
        body, name=name, grid=(r // tr,),
        in_specs=[pl.BlockSpec((N_DEV, tr, c), lambda i: (0, i, 0))] * layers + [spec, spec, spec],
        out_specs=[spec] * 4, out_shape=[jax.ShapeDtypeStruct((layers, r, c), F32)] * 4,
        compiler_params=_params(("parallel",)),
    )(*slabs, w, m, v)


GATHER_AXIS = {"attn_w_in": 1, "attn_w_out": 0, "ffn_w_up0": 1, "ffn_w_down0": 0, "hgrn_w_in": 1, "hgrn_w_out": 0,
               "hgrn_norm_g": 1, "ffn_w_up1": 1, "ffn_w_down1": 0}
GATHER_STAGES = (("attn_w_in",), ("attn_w_out", "ffn_w_up0", "ffn_w_down0", "hgrn_w_in", "hgrn_w_out", "hgrn_norm_g"),
                 ("ffn_w_up1", "ffn_w_down1"))
GATHER_WAITS = ((("attn_w_in",), 0, 1), (("attn_w_out", "ffn_w_up0", "ffn_w_down0"), 1, 2),
                (("hgrn_w_in", "hgrn_w_out", "hgrn_norm_g"), 1, None), (("ffn_w_up1", "ffn_w_down1"), 2, None))
SCATTER_AXIS = dict(GATHER_AXIS, small=None)
BIG = ("attn_w_in", "attn_w_out", "hgrn_w_in", "hgrn_w_out", "ffn_w_up", "ffn_w_down")
SMALL = ("lb_logits", "ln_mix_g", "ln_mix_b", "ln_ffn_g", "ln_ffn_b")
SMALL_ROW = {"ln_mix_g": 0, "ln_mix_b": 2, "ln_ffn_g": 4, "ln_ffn_b": 6, "lb_logits": 8}
NORM_G_ROW = 10
LOSS_ROW = 11


def kernel(x, attn_w_in, attn_w_out, hgrn_w_in, hgrn_w_out, hgrn_norm_g, lb_logits, ln_mix_g, ln_mix_b, ln_ffn_g, ln_ffn_b, ffn_w_up, ffn_w_down, loss_target, m_attn_w_in, m_attn_w_out, m_hgrn_w_in, m_hgrn_w_out, m_hgrn_norm_g, m_lb_logits, m_ln_mix_g, m_ln_mix_b, m_ln_ffn_g, m_ln_ffn_b, m_ffn_w_up, m_ffn_w_down, v_attn_w_in, v_attn_w_out, v_hgrn_w_in, v_hgrn_w_out, v_hgrn_norm_g, v_lb_logits, v_ln_mix_g, v_ln_mix_b, v_ln_ffn_g, v_ln_ffn_b, v_ffn_w_up, v_ffn_w_down):
    wts = dict(attn_w_in=attn_w_in, attn_w_out=attn_w_out, hgrn_w_in=hgrn_w_in, hgrn_w_out=hgrn_w_out,
               hgrn_norm_g=hgrn_norm_g, lb_logits=lb_logits, ln_mix_g=ln_mix_g, ln_mix_b=ln_mix_b, ln_ffn_g=ln_ffn_g,
               ln_ffn_b=ln_ffn_b, ffn_w_up=ffn_w_up, ffn_w_down=ffn_w_down)
    mom = dict(attn_w_in=m_attn_w_in, attn_w_out=m_attn_w_out, hgrn_w_in=m_hgrn_w_in, hgrn_w_out=m_hgrn_w_out,
               hgrn_norm_g=m_hgrn_norm_g, lb_logits=m_lb_logits, ln_mix_g=m_ln_mix_g, ln_mix_b=m_ln_mix_b,
               ln_ffn_g=m_ln_ffn_g, ln_ffn_b=m_ln_ffn_b, ffn_w_up=m_ffn_w_up, ffn_w_down=m_ffn_w_down)
    vel = dict(attn_w_in=v_attn_w_in, attn_w_out=v_attn_w_out, hgrn_w_in=v_hgrn_w_in, hgrn_w_out=v_hgrn_w_out,
               hgrn_norm_g=v_hgrn_norm_g, lb_logits=v_lb_logits, ln_mix_g=v_ln_mix_g, ln_mix_b=v_ln_mix_b,
               ln_ffn_g=v_ln_ffn_g, ln_ffn_b=v_ln_ffn_b, ffn_w_up=v_ffn_w_up, ffn_w_down=v_ffn_w_down)
    me = 4 * lax.axis_index("x") + 2 * lax.axis_index("y") + lax.axis_index("c")

    src = {"attn_w_in": attn_w_in[0], "attn_w_out": attn_w_out[0], "hgrn_w_in": hgrn_w_in[0], "hgrn_w_out": hgrn_w_out[0],
           "ffn_w_up0": ffn_w_up[0], "ffn_w_down0": ffn_w_down[0], "ffn_w_up1": ffn_w_up[1], "ffn_w_down1": ffn_w_down[1]}
    gathers, got = {}, {}
    casts = {nm: _cast_bf16(a, name=f"cast_{nm}") for nm, a in src.items()}
    casts["hgrn_norm_g"] = hgrn_norm_g

    def start_gather(stage, deps):
        shards, lands = [], []
        for nm in GATHER_STAGES[stage]:
            sh = casts[nm]
            ax = GATHER_AXIS[nm]
            shape = list(sh.shape)
            shape[ax] *= N_DEV
            shards.append(sh)
            lands.append(lax.empty(tuple(shape), sh.dtype))
        peers = (SIBLING,) + SAME_CORE if stage == 0 else tuple(range(1, N_DEV))
        gathers[stage] = _xchg_start(shards, lands, [GATHER_AXIS[nm] for nm in GATHER_STAGES[stage]], gather=True,
                                     name=f"gather_start_{stage}", deps=deps, peers=peers)
        return [gathers[stage]["token"]]

    def get_w(name, after):
        deps = []
        if name not in got:
            group, stage, then = [w for w in GATHER_WAITS if name in w[0]][0]
            xc = gathers[stage]
            if stage == 0:
                later = [casts[nm] for st in GATHER_STAGES[1:] for nm in st if nm != "hgrn_norm_g"]
                relay = _gather_relay(xc, [xc["token"], *after, *later], name="gather_relay")
                res = [_gather_relay_wait(xc, relay, xc["token"], name=f"gather_wait_{group[0]}")]
            else:
                res = _xchg_wait(xc, [GATHER_STAGES[stage].index(nm) for nm in group], [after],
                                 name=f"gather_wait_{group[0]}")
            got.update(zip(group, res))
            if then is not None:
                deps = start_gather(then, [res[0]])
        return got[name], deps

    first = start_gather(0, [])

    scattered, held = {}, {}

    def on_grads(tag, grads):
        if tag in ("ffn1", "ffn0"):
            held.update(grads)
            return []
        grads = {**held, **grads}
        held.clear()
        gnames = list(grads)
        axes = [SCATTER_AXIS[nm] for nm in gnames]
        stacks = []
        for nm, ax in zip(gnames, axes):
            shape = list(grads[nm].shape)
            if ax is not None:
                shape[ax] //= N_DEV
            stacks.append(lax.empty((N_DEV, *shape), grads[nm].dtype))
        scattered[tag] = (gnames, _xchg_start([grads[nm] for nm in gnames], stacks, axes, gather=False,
                                              name=f"scatter_start_{tag}"))
        return [scattered[tag][1]["token"]]

    sm = jax.nn.softmax(lb_logits, axis=0)
    csum = jnp.cumsum(sm, axis=0)
    small = dict(lb=(csum - csum[0:1])[1:2], lb_logits=lb_logits, ln_mix_g=ln_mix_g, ln_mix_b=ln_mix_b,
                 ln_ffn_g=ln_ffn_g, ln_ffn_b=ln_ffn_b)
    grad_x = _local_step(x[0], loss_target[0], get_w, small, on_grads, deps=first)
    out = {}

    def stack_small(src_):
        rows = [None] * SMALL_ROWS
        for name in SMALL:
            rows[SMALL_ROW[name]], rows[SMALL_ROW[name] + 1] = src_[name][0:1], src_[name][1:2]
        zero = jnp.zeros((1, x.shape[-1]), F32)
        return jnp.concatenate([zero if r is None else r for r in rows], axis=0)[None]

    def update(name, slabs):
        shape = wts[name].shape
        out[name] = [r.reshape(shape) for r in _adamw(slabs, wts[name], mom[name], vel[name], name=f"adamw_{name}")]
        return out[name][0]

    slabs, after = {}, [grad_x]
    for tag, (gnames, xc) in scattered.items():
        slabs.update(zip(gnames, _xchg_wait(xc, list(range(len(gnames))), after, name=f"scatter_wait_{tag}")))
        if tag == "hgrn":
            after = [update("hgrn_w_out", [slabs["hgrn_w_out"]]), update("hgrn_w_in", [slabs["hgrn_w_in"]])]
        elif tag == "attn_out":
            after = [update("ffn_w_down", [slabs["ffn_w_down0"], slabs["ffn_w_down1"]]),
                     update("ffn_w_up", [slabs["ffn_w_up0"], slabs["ffn_w_up1"]]),
                     update("attn_w_out", [slabs["attn_w_out"]])]
        else:
            update("attn_w_in", [slabs["attn_w_in"]])
    res = _adamw([slabs["small"]], stack_small(wts), stack_small(mom), stack_small(vel), name="adamw_small")
    for name in SMALL:
        out[name] = [r[0, SMALL_ROW[name]:SMALL_ROW[name] + 2] for r in res]
    loss = res[0][0, LOSS_ROW, 0]
    ng = hgrn_norm_g.shape[-1]
    ng_slabs = lax.dynamic_slice(slabs["small"], (0, NORM_G_ROW, me * ng), (N_DEV, 1, ng))
    out["hgrn_norm_g"] = [r[0] for r in _adamw([ng_slabs], hgrn_norm_g[None], m_hgrn_norm_g[None],
                                                v_hgrn_norm_g[None], name="adamw_norm_g")]
    order =("attn_w_in", "attn_w_out", "hgrn_w_in", "hgrn_w_out", "hgrn_norm_g", "lb_logits", "ln_mix_g", "ln_mix_b",
             "ln_ffn_g", "ln_ffn_b", "ffn_w_up", "ffn_w_down")
    return (loss, grad_x[None], *[out[nm][0] for nm in order], *[out[nm][1] for nm in order],
            *[out[nm][2] for nm in order], *[out[nm][3] for nm in order])
```

```python
import jax
import jax.numpy as jnp
from jax import lax
from jax.experimental import pallas as pl
from jax.experimental.pallas import tpu as pltpu

F32 = jnp.float32
BF16 = jnp.bfloat16

N_DEV = 8
LANES = 128
D_MODEL = 1024
ATTN_HEAD_DIM = 64
ATTN_HEADS = 16
ATTN_SCALE = ATTN_HEAD_DIM ** -0.5
ATTN_BLK = 128
DILATIONS = (1, 4, 16)
ROPE_THETA = 10000.0
HGRN_HEADS = 8
HGRN_CHUNK = 64
D_FF = 4096
LN_EPS = 1e-5
RMS_EPS = 1e-6
DEPTH = 2
ALPHA = (2 * DEPTH) ** 0.25
ADAM_LR, ADAM_B1, ADAM_B2, ADAM_EPS, ADAM_WD, ADAM_STEP = 0.001, 0.9, 0.999, 1e-08, 0.01, 10
VMEM_LIMIT = 48 * 1024 * 1024

_NT = (((1,), (1,)), ((), ()))
_TN = (((0,), (0,)), ((), ()))


def _dot(a, b):
    return jnp.dot(a, b, preferred_element_type=F32)


def _dot_nt(a, b):
    return lax.dot_general(a, b, _NT, preferred_element_type=F32)


def _dot_tn(a, b):
    return lax.dot_general(a, b, _TN, preferred_element_type=F32)


def _split3(x):
    p1 = x.astype(BF16)
    r1 = x - p1.astype(F32)
    p2 = r1.astype(BF16)
    p3 = (r1 - p2.astype(F32)).astype(BF16)
    return p1, p2, p3


def _exact_dot(sel3, x):
    return _dot(sel3, jnp.concatenate(_split3(x), axis=0))


def _exact_dot_r(x, sel3):
    return _dot(jnp.concatenate(_split3(x), axis=1), sel3)


def _params(sem=None):
    return pltpu.CompilerParams(dimension_semantics=sem, vmem_limit_bytes=VMEM_LIMIT)


def _in_hbm(*arrays):
    return [pltpu.with_memory_space_constraint(a, pltpu.HBM) for a in arrays]


def _mm(a, b, mode, *, name, m, n, k, tm=1024, tn=1024, tk=1024, out_dtypes=(F32,), epi=None, a_pre=None,
        tile_extras=(), row_extras=(), vec_extras=(), row_outs=(), vec_outs=0, a_split=1, b_split=1, out_split=1,
        b_col_off=0, b_k_off=0, out_col_off=0, out_cols=None, alias=None, epi_wants_j=False, deps=(), t_out=False):
    tm, tn, tk = min(tm, m), min(tn, n), min(tk, k)
    assert m % tm == 0 and n % tn == 0 and k % tk == 0, (name, m, n, k, tm, tn, tk)
    gm, gn, gk = m // tm, n // tn, k // tk
    if mode in ("nn", "nt"):
        if a_split > 1 and tk == k:
            a_spec = pl.BlockSpec((a_split, tm, k // a_split), lambda i, j, kk: (0, i, 0))
        elif a_split > 1:
            kc = (k // a_split) // tk
            a_spec = pl.BlockSpec((None, tm, tk), lambda i, j, kk: (kk // kc, i, kk % kc))
        else:
            a_spec = pl.BlockSpec((tm, tk), lambda i, j, kk: (i, kk))
    else:
        a_spec = pl.BlockSpec((tk, tm), lambda i, j, kk: (kk, i))
    if mode in ("nn", "tn"):
        if b_split > 1:
            nc = (n // b_split) // tn
            b_spec = pl.BlockSpec((None, tk, tn), lambda i, j, kk: (j // nc, kk, j % nc))
        else:
            b_spec = pl.BlockSpec((tk, tn), lambda i, j, kk: (kk + b_k_off, j + b_col_off))
    else:
        b_spec = pl.BlockSpec((tn, tk), lambda i, j, kk: (j + b_col_off, kk + b_k_off))
    if out_split > 1:
        nco = (n // out_split) // tn
        o_spec = pl.BlockSpec((None, tm, tn), lambda i, j, kk: (j // nco, i, j % nco))
        o_shape = (out_split, m, n // out_split)
    else:
        o_spec = pl.BlockSpec((tm, tn), lambda i, j, kk: (i, j + out_col_off))
        o_shape = (m, out_cols if out_cols is not None else n)
    n_ex = len(tile_extras) + len(row_extras) + len(vec_extras)
    n_out = len(out_dtypes)
    n_plain = n_out + len(row_outs)
    assert not vec_outs or gn == 1
    if epi is None:
        def epi(acc):
            return (acc,)
    dot = {"nn": _dot, "nt": _dot_nt, "tn": _dot_tn}[mode]

    def body(*refs):
        a_ref, b_ref = refs[0], refs[1]
        ex = refs[2:2 + n_ex]
        outs = refs[2 + n_ex + (1 if alias is not None else 0) + len(deps):][:n_plain + vec_outs + (1 if t_out else 0)]
        ii = pl.program_id(0)
        jj = pl.program_id(1)

        def product():
            if a_split > 1 and tk == k:
                av = jnp.concatenate([a_ref[p] for p in range(a_split)], axis=1)
            else:
                av = a_ref[...]
            if a_pre is not None:
                av = a_pre(av)
            return dot(av.astype(BF16), b_ref[...].astype(BF16))

        def finish(total):
            lead = (jj,) if epi_wants_j else ()
            res = epi(*lead, total, *[e[...] for e in ex])
            for o, r in zip(outs[:n_plain], res):
                o[...] = r.astype(o.dtype)
            for o, r in zip(outs[n_plain:n_plain + vec_outs], res[n_plain:]):
                @pl.when(ii == 0)
                def _(o=o, r=r):
                    o[...] = r

                @pl.when(ii > 0)
                def _(o=o, r=r):
                    o[...] += r
            if t_out:
                outs[-1][...] = res[-1].astype(outs[-1].dtype)

        if gk == 1:
            finish(product())
        else:
            acc = refs[-1]
            kk = pl.program_id(2)

            @pl.when(kk == 0)
            def _():
                acc[...] = product()

            @pl.when(kk > 0)
            def _():
                acc[...] += product()

            @pl.when(kk == gk - 1)
            def _():
                finish(acc[...])

    in_specs = [a_spec, b_spec] + [o_spec] * len(tile_extras)
    in_specs += [pl.BlockSpec((tm, r.shape[1]), lambda i, j, kk: (i, 0)) if r.shape[0] == m else
                 pl.BlockSpec((r.shape[0], tm), lambda i, j, kk: (0, i)) for r in row_extras]
    in_specs += [pl.BlockSpec((1, tn), lambda i, j, kk: (0, j))] * len(vec_extras)
    args = [a, b] + list(tile_extras) + list(row_extras) + list(vec_extras)
    io_alias = {}
    if alias is not None:
        in_specs.append(pl.BlockSpec(memory_space=pl.ANY))
        args.append(alias)
        io_alias = {len(args) - 1: 0}
    in_specs += [pl.BlockSpec(memory_space=pl.ANY)] * len(deps)
    args += list(deps)
    out_specs = [o_spec] * n_out
    out_shape = [jax.ShapeDtypeStruct(o_shape, dt) for dt in out_dtypes]
    for dt, w in row_outs:
        out_specs.append(pl.BlockSpec((tm, w), lambda i, j, kk: (i, 0)))
        out_shape.append(jax.ShapeDtypeStruct((m, w), dt))
    out_specs += [pl.BlockSpec((1, tn), lambda i, j, kk: (0, j))] * vec_outs
    out_shape += [jax.ShapeDtypeStruct((1, n), F32)] * vec_outs
    if t_out:
        assert out_split > 1
        out_specs.append(pl.BlockSpec((None, tn, tm), lambda i, j, kk: (j // nco, j % nco, i)))
        out_shape.append(jax.ShapeDtypeStruct((out_split, n // out_split, m), out_dtypes[0]))
    out = pl.pallas_call(
        body, name=name, grid=(gm, gn, gk), in_specs=in_specs, out_specs=out_specs, out_shape=out_shape,
        scratch_shapes=[pltpu.VMEM((tm, tn), F32)] if gk > 1 else [],
        input_output_aliases=io_alias,
        compiler_params=_params(("arbitrary" if vec_outs else "parallel", "parallel", "arbitrary")),
    )(*_in_hbm(*args[:len(args) - len(deps)]), *deps)
    return out[0] if len(out) == 1 else out


def _rope_tables(seq, dil):
    half = ATTN_HEAD_DIM // 2
    row = jnp.arange(seq, dtype=jnp.int32)
    pos = ((row % (seq // dil)) * dil + row // (seq // dil)).astype(F32)

    def tables(features, axis):
        f = jnp.arange(features, dtype=jnp.int32)
        inv = ROPE_THETA ** (-(f % half).astype(F32) * (2.0 / ATTN_HEAD_DIM))
        sign = jnp.where(f % ATTN_HEAD_DIM < half, -1.0, 1.0).astype(F32)
        ang = jnp.expand_dims(pos, axis) * jnp.expand_dims(inv, 1 - axis)
        return jnp.cos(ang), jnp.sin(ang) * jnp.expand_dims(sign, 1 - axis)

    return tables(LANES, 1), tables(ATTN_HEAD_DIM, 0)


def _rotate(x, c, ss, sign=1.0):
    w = x.shape[-1]
    half = ATTN_HEAD_DIM // 2
    lane = lax.broadcasted_iota(jnp.int32, x.shape, 1)
    first = (lane % ATTN_HEAD_DIM) < half
    partner = jnp.where(first, pltpu.roll(x, w - half, 1), pltpu.roll(x, half, 1))
    reps = w // LANES
    if reps > 1:
        c = jnp.concatenate([c] * reps, axis=1)
        ss = jnp.concatenate([ss] * reps, axis=1)
    return x * c + sign * (partner * ss)


def _rotate_t(xt, ct, sst):
    half = ATTN_HEAD_DIM // 2
    heads = xt.shape[0] // ATTN_HEAD_DIM
    parts = []
    for h in range(heads):
        lo = h * ATTN_HEAD_DIM
        parts += [xt[lo + half:lo + 2 * half], xt[lo:lo + half]]
    return (xt * jnp.concatenate([ct] * heads, axis=0)
            + jnp.concatenate(parts, axis=0) * jnp.concatenate([sst] * heads, axis=0))


def _ln_epi(acc, x, g, b, *prev):
    if prev:
        x = x * prev[0] + prev[1]
    u = ALPHA * x + acc
    mu = jnp.mean(u, axis=-1, keepdims=True)
    uc = u - mu
    var = jnp.mean(uc * uc, axis=-1, keepdims=True)
    rstd = lax.rsqrt(var + LN_EPS)
    xh = uc * rstd
    return xh * g + b, xh, rstd


def _ln_grad(dy, xh, rstd, g):
    dxh = dy * g
    m1 = jnp.mean(dxh, axis=-1, keepdims=True)
    m2 = jnp.mean(dxh * xh, axis=-1, keepdims=True)
    du = rstd * (dxh - m1 - xh * m2)
    return du, du, jnp.sum(dy * xh, axis=0, keepdims=True), jnp.sum(dy, axis=0, keepdims=True)


def _ln_grad_epi(acc, du_next, xh, rstd, g):
    return _ln_grad(acc + ALPHA * du_next, xh, rstd, g)


def _ln_loss_epi(acc, x_in, target, g, b, *prev):
    out, xh, rstd = _ln_epi(acc, x_in, g, b, *prev)
    e = out - target
    du, _, dg, db = _ln_grad(e * (1.0 / e.shape[-1]), xh, rstd, g)
    return du, du, dg, db, jnp.sum(e * e, axis=0, keepdims=True)


POS_BLK = 2048


def _class_rows(r, dil):
    return pl.ds(r, POS_BLK // dil, stride=dil) if dil > 1 else pl.ds(0, POS_BLK)


def _class_view(a, dil):
    s, w = a.shape
    return a.reshape(dil, s // dil, w)


def _class_spec(dil, all_tiles=True):
    if all_tiles:
        return pl.BlockSpec((dil, POS_BLK // dil, LANES), lambda i, t: (0, i, t))
    return pl.BlockSpec((dil, POS_BLK // dil, LANES), lambda i, t: (0, i, 0))


def _pos_spec(all_tiles=True):
    if all_tiles:
        return pl.BlockSpec((POS_BLK, LANES), lambda i, t: (i, t))
    return pl.BlockSpec((POS_BLK, LANES), lambda i, t: (i, 0))


def _prep_x(x, deps=()):
    s, d = x.shape

    def body(x_ref, *refs):
        outs = refs[len(deps):]
        for dil, o_ref in zip(DILATIONS, outs):
            for r in range(dil):
                o_ref[r] = x_ref[_class_rows(r, dil), :].astype(BF16)

    outs = pl.pallas_call(
        body, name="prep_x", grid=(s // POS_BLK, d // LANES),
        in_specs=[_pos_spec()] + [pl.BlockSpec(memory_space=pl.ANY)] * len(deps),
        out_specs=[_class_spec(dil) for dil in DILATIONS],
        out_shape=[jax.ShapeDtypeStruct((dil, s // dil, d), BF16) for dil in DILATIONS],
        compiler_params=_params(("parallel", "parallel")),
    )(x, *deps)
    return [o.reshape(s, d) for o in outs]


def _head_expand_matrix():
    h = lax.broadcasted_iota(jnp.int32, (LANES, D_MODEL), 0)
    l = lax.broadcasted_iota(jnp.int32, (LANES, D_MODEL), 1)
    return (l // ATTN_HEAD_DIM == h).astype(BF16)


def _attn_fwd(qkv, dil, *, name):
    _, s, d = qkv.shape
    nq = s // ATTN_BLK
    per = nq // dil
    tiles = d // LANES

    def body(q_ref, kc_ref, kp_ref, vc_ref, vp_ref, o_ref, lse_ref):
        qb = pl.program_id(0)
        first = (qb % per) == 0
        qi = lax.broadcasted_iota(jnp.int32, (ATTN_BLK, 2 * ATTN_BLK), 0)
        kj = lax.broadcasted_iota(jnp.int32, (ATTN_BLK, 2 * ATTN_BLK), 1)
        dist = qi + ATTN_BLK - kj
        valid = (dist >= 0) & (dist <= ATTN_BLK) & ((kj >= ATTN_BLK) | jnp.logical_not(first))
        lane = lax.broadcasted_iota(jnp.int32, (ATTN_BLK, LANES), 1)
        lse_tile = jnp.zeros((ATTN_BLK, LANES), F32)
        zero = jnp.zeros((), BF16)
        in_head = [(lane // ATTN_HEAD_DIM) == hh for hh in range(2)]
        for t0 in range(0, tiles, ATTN_FWD_TILE_GROUP):
            group = range(t0, t0 + ATTN_FWD_TILE_GROUP)
            heads = [(t, hh) for t in group for hh in range(2)]
            cols = {t: pl.ds(t * LANES, LANES) for t in group}
            k2 = {t: jnp.concatenate([kp_ref[:, cols[t]], kc_ref[:, cols[t]]], axis=0) for t in group}
            v2 = {t: jnp.concatenate([vp_ref[:, cols[t]], vc_ref[:, cols[t]]], axis=0) for t in group}
            sc = {(t, hh): jnp.where(valid, _dot_nt(jnp.where(in_head[hh], q_ref[:, cols[t]], zero), k2[t]),
                                     -jnp.inf) for t, hh in heads}
            mx = {i: jnp.max(sc[i], axis=-1, keepdims=True) for i in heads}
            p = {i: jnp.exp(sc[i] - mx[i]) for i in heads}
            l = {i: jnp.sum(p[i], axis=-1, keepdims=True) for i in heads}
            oh = {i: _dot(p[i].astype(BF16), v2[i[0]]) / l[i] for i in heads}
            for t in group:
                o_ref[:, cols[t]] = jnp.where(in_head[0], oh[t, 0], oh[t, 1])
                for hh in range(2):
                    lse_tile = jnp.where(lane == 2 * t + hh, mx[t, hh] + jnp.log(l[t, hh]), lse_tile)
        lse_ref[...] = lse_tile

    def blk(piece, prev):
        if prev:
            return pl.BlockSpec((None, ATTN_BLK, d), lambda i: (piece, jnp.maximum(i - 1, 0), 0))
        return pl.BlockSpec((None, ATTN_BLK, d), lambda i: (piece, i, 0))

    return pl.pallas_call(
        body, name=name, grid=(nq,),
        in_specs=[blk(0, False), blk(1, False), blk(1, True), blk(2, False), blk(2, True)],
        out_specs=[pl.BlockSpec((ATTN_BLK, d), lambda i: (i, 0)), pl.BlockSpec((ATTN_BLK, LANES), lambda i: (i, 0))],
        out_shape=[jax.ShapeDtypeStruct((s, d), F32), jax.ShapeDtypeStruct((s, LANES), F32)],
        compiler_params=_params(("parallel",)),
    )(qkv, qkv, qkv, qkv, qkv)


def _attn_combine(os_, lses):
    s, d = os_[0].shape
    sel = jnp.concatenate([_head_expand_matrix()] * 3, axis=0)

    def body(o0, o1, o2, l0, l1, l2, sel_ref, of_ref, ob_ref, lt_ref, o_pos, l_pos):
        for g, (dil, o_ref, l_ref) in enumerate(zip(DILATIONS, (o0, o1, o2), (l0, l1, l2))):
            for r in range(dil):
                o_pos[g, _class_rows(r, dil), :] = o_ref[r]
                l_pos[g, _class_rows(r, dil), :] = l_ref[r]
        la, lb_, lc = l_pos[0], l_pos[1], l_pos[2]
        mx = jnp.maximum(jnp.maximum(la, lb_), lc)
        es = (jnp.exp(la - mx), jnp.exp(lb_ - mx), jnp.exp(lc - mx))
        z = es[0] + es[1] + es[2]
        lt_ref[...] = mx + jnp.log(z)
        acc = jnp.zeros((POS_BLK, LANES), F32)
        for g in range(3):
            acc += _exact_dot_r(es[g] / z, sel_ref[...]) * o_pos[g]
        of_ref[...] = acc
        ob_ref[...] = acc.astype(BF16)

    return pl.pallas_call(
        body, name="attn_combine", grid=(s // POS_BLK, d // LANES),
        in_specs=[_class_spec(dil) for dil in DILATIONS] + [_class_spec(dil, False) for dil in DILATIONS]
        + [pl.BlockSpec((3 * LANES, LANES), lambda i, t: (0, t))],
        out_specs=[_pos_spec(), _pos_spec(), _pos_spec(False)],
        out_shape=[jax.ShapeDtypeStruct((s, d), F32), jax.ShapeDtypeStruct((s, d), BF16),
                   jax.ShapeDtypeStruct((s, LANES), F32)],
        scratch_shapes=[pltpu.VMEM((3, POS_BLK, LANES), F32), pltpu.VMEM((3, POS_BLK, LANES), F32)],
        compiler_params=_params(("parallel", "arbitrary")),
    )(*_in_hbm(*[_class_view(o, dil) for o, dil in zip(os_, DILATIONS)],
               *[_class_view(l, dil) for l, dil in zip(lses, DILATIONS)], sel))


ATTN_TILE_GROUP = 2
ATTN_FWD_TILE_GROUP = 4
AUX_PER_TILE = 12


def _aux_placement():
    h = lax.broadcasted_iota(jnp.int32, (6, LANES, LANES), 1)
    l = lax.broadcasted_iota(jnp.int32, (6, LANES, LANES), 2)
    j = lax.broadcasted_iota(jnp.int32, (6, LANES, LANES), 0)
    target = AUX_PER_TILE * (h // 2) + 3 * (h % 2) + jnp.where(j < 3, j, 3 + j)
    return ((l == target) & (h < ATTN_HEADS)).astype(BF16)


def _attn_bwd_prep(do, o, lse):
    s, d = do.shape
    tiles = d // LANES
    sel_t = jnp.tile(_head_expand_matrix().T.reshape(tiles, LANES, LANES), (1, 3, 1))

    def body(do_ref, o_ref, l_ref, sel_ref, place_ref, *refs):
        outs, delta, aux = refs[:9], refs[9], refs[10]
        t = pl.program_id(1)
        part = _exact_dot_r(do_ref[...] * o_ref[...], sel_ref[...])

        @pl.when(t == 0)
        def _():
            delta[...] = part

        @pl.when(t > 0)
        def _():
            delta[...] += part

        for g, dil in enumerate(DILATIONS):
            for r in range(dil):
                blk = do_ref[_class_rows(r, dil), :].astype(BF16)
                outs[g][r] = blk
                outs[3 + g][r] = blk.T

        @pl.when(t == tiles - 1)
        def _():
            pieces = _split3(l_ref[...]) + _split3(delta[...])
            aux[...] = _dot(jnp.concatenate(pieces, axis=1), place_ref[...])
            for g, dil in enumerate(DILATIONS):
                for r in range(dil):
                    outs[6 + g][r] = aux[_class_rows(r, dil), :].astype(BF16)

    outs = pl.pallas_call(
        body, name="attn_bwd_prep", grid=(s // POS_BLK, tiles),
        in_specs=[_pos_spec(), _pos_spec(), _pos_spec(False),
                  pl.BlockSpec((None, 3 * LANES, LANES), lambda i, t: (t, 0, 0)),
                  pl.BlockSpec((6 * LANES, LANES), lambda i, t: (0, 0))],
        out_specs=[_class_spec(dil) for dil in DILATIONS]
        + [pl.BlockSpec((dil, LANES, POS_BLK // dil), lambda i, t: (0, t, i)) for dil in DILATIONS]
        + [_class_spec(dil, False) for dil in DILATIONS],
        out_shape=[jax.ShapeDtypeStruct((dil, s // dil, d), BF16) for dil in DILATIONS]
        + [jax.ShapeDtypeStruct((dil, d, s // dil), BF16) for dil in DILATIONS]
        + [jax.ShapeDtypeStruct((dil, s // dil, LANES), BF16) for dil in DILATIONS],
        scratch_shapes=[pltpu.VMEM((POS_BLK, LANES), F32), pltpu.VMEM((POS_BLK, LANES), F32)],
        compiler_params=_params(("parallel", "arbitrary")),
    )(*_in_hbm(do, o, lse, sel_t, _aux_placement().reshape(6 * LANES, LANES)))
    return ([a.reshape(s, d) for a in outs[0:3]], list(outs[3:6]), [a.reshape(s, LANES) for a in outs[6:9]])


def _attn_bwd(qkv, qkv_t, do, do_t, aux, tables, dil, *, name):
    _, s, d = qkv.shape
    nq = s // ATTN_BLK
    per = nq // dil
    tiles = d // LANES
    half = ATTN_HEAD_DIM

    def body(qd_ref, qo_ref, k_ref, qtd_ref, qto_ref, kt_ref, vt_ref, dod_ref, doo_ref, dotd_ref, doto_ref,
             auxd_ref, auxo_ref, c_ref, ss_ref, out_ref, carry):
        kb = pl.program_id(0)

        @pl.when(kb == 0)
        def _():
            carry[...] = jnp.zeros_like(carry)

        has_next = (kb % per) != (per - 1)
        qi = lax.broadcasted_iota(jnp.int32, (ATTN_BLK, 2 * ATTN_BLK), 0)
        kj = lax.broadcasted_iota(jnp.int32, (ATTN_BLK, 2 * ATTN_BLK), 1) % ATTN_BLK
        valid_d = kj <= qi
        valid_o = (kj >= qi) & has_next
        lane = lax.broadcasted_iota(jnp.int32, (ATTN_BLK, LANES), 1)
        row = lax.broadcasted_iota(jnp.int32, (LANES, ATTN_BLK), 0)
        side = lax.broadcasted_iota(jnp.int32, (LANES, 2 * ATTN_BLK), 0)
        first = lax.broadcasted_iota(jnp.int32, (LANES, 2 * ATTN_BLK), 1) < ATTN_BLK
        c, ss = c_ref[...], ss_ref[...]
        zero = jnp.zeros((), BF16)
        sides = ((qd_ref, qtd_ref, dod_ref, dotd_ref, auxd_ref[...], valid_d),
                 (qo_ref, qto_ref, doo_ref, doto_ref, auxo_ref[...], valid_o))
        both = (0, 1)

        def head_halves(x, index):
            axis = 0 if index is lane else 1
            return jnp.concatenate([jnp.where(index < half, x, zero), jnp.where(index >= half, x, zero)], axis=axis)

        for t0 in range(0, tiles, ATTN_TILE_GROUP):
            group = range(t0, t0 + ATTN_TILE_GROUP)
            cols = {t: pl.ds(t * LANES, LANES) for t in group}
            kk, kk_t, vv_t = {}, {}, {}
            for t in group:
                base = AUX_PER_TILE * t
                hit = lambda lo: ((first & (side >= base + lo) & (side < base + lo + 3))
                                  | (jnp.logical_not(first) & (side >= base + lo + 3) & (side < base + lo + 6)))
                kk[t] = head_halves(k_ref[:, cols[t]], lane)
                kk_t[t] = jnp.concatenate([head_halves(kt_ref[cols[t], :], row),
                                           jnp.where(hit(0), -1.0, 0.0).astype(BF16)], axis=0)
                vv_t[t] = jnp.concatenate([head_halves(vt_ref[cols[t], :], row),
                                           jnp.where(hit(6), -1.0, 0.0).astype(BF16)], axis=0)
            sc = {(t, w): _dot(jnp.concatenate([sides[w][0][:, cols[t]], sides[w][4]], axis=1), kk_t[t])
                  for t in group for w in both}
            dpd = {(t, w): _dot(jnp.concatenate([sides[w][2][:, cols[t]], sides[w][4]], axis=1), vv_t[t])
                   for t in group for w in both}
            p = {i: jnp.where(sides[i[1]][5], jnp.exp(sc[i]), 0.0) for i in sc}
            ds = {i: (p[i] * dpd[i]).astype(BF16) for i in sc}
            pb = {i: p[i].astype(BF16) for i in sc}
            dv_t = {t: sum(_dot(sides[w][3][cols[t], :], pb[t, w]) for w in both) for t in group}
            dk_t = {t: sum(_dot(sides[w][1][cols[t], :], ds[t, w]) for w in both) for t in group}
            dq = {i: _dot(ds[i], kk[i[0]]) * ATTN_SCALE for i in sc}
            for t in group:
                dq_now = carry[:, cols[t]] + dq[t, 0]
                carry[:, cols[t]] = dq[t, 1]
                dk = jnp.where(row < half, dk_t[t][:, :ATTN_BLK], dk_t[t][:, ATTN_BLK:]).T
                dv = jnp.where(row < half, dv_t[t][:, :ATTN_BLK], dv_t[t][:, ATTN_BLK:]).T
                out_ref[0, :, cols[t]] = _rotate(dq_now, c, ss, -1.0).astype(BF16)
                out_ref[1, :, cols[t]] = _rotate(dk, c, ss, -1.0).astype(BF16)
                out_ref[2, :, cols[t]] = dv.astype(BF16)

    def nxt(i):
        return jnp.minimum(i + 1, nq - 1)

    def piece(p, shift):
        if shift:
            return pl.BlockSpec((None, ATTN_BLK, d), lambda i: (p, nxt(i), 0))
        return pl.BlockSpec((None, ATTN_BLK, d), lambda i: (p, i, 0))

    def piece_t(p, shift):
        if shift:
            return pl.BlockSpec((None, d, ATTN_BLK), lambda i: (p, 0, nxt(i)))
        return pl.BlockSpec((None, d, ATTN_BLK), lambda i: (p, 0, i))

    def rows(width, shift):
        if shift:
            return pl.BlockSpec((ATTN_BLK, width), lambda i: (nxt(i), 0))
        return pl.BlockSpec((ATTN_BLK, width), lambda i: (i, 0))

    def do_t_spec(shift):
        if shift:
            return pl.BlockSpec((None, d, ATTN_BLK), lambda i: (nxt(i) // per, 0, nxt(i) % per))
        return pl.BlockSpec((None, d, ATTN_BLK), lambda i: (i // per, 0, i % per))

    return pl.pallas_call(
        body, name=name, grid=(nq,),
        in_specs=[piece(0, False), piece(0, True), piece(1, False),
                  piece_t(0, False), piece_t(0, True), piece_t(1, False), piece_t(2, False),
                  rows(d, False), rows(d, True), do_t_spec(False), do_t_spec(True),
                  rows(LANES, False), rows(LANES, True), rows(LANES, False), rows(LANES, False)],
        out_specs=pl.BlockSpec((3, ATTN_BLK, d), lambda i: (0, i, 0)),
        out_shape=jax.ShapeDtypeStruct((3, s, d), BF16),
        scratch_shapes=[pltpu.VMEM((ATTN_BLK, d), F32)],
        compiler_params=_params(("arbitrary",)),
    )(*_in_hbm(qkv, qkv, qkv, qkv_t, qkv_t, qkv_t, qkv_t, do, do, do_t, do_t, aux, aux, *tables))


def _dx_combine(du, parts):
    s, d = du.shape

    def body(du_ref, p0, p1, p2, out_ref):
        out_ref[...] = ALPHA * du_ref[...] + p0[0]
        for dil, p_ref in zip(DILATIONS[1:], (p1, p2)):
            for r in range(dil):
                out_ref[_class_rows(r, dil), :] += p_ref[r]

    return pl.pallas_call(
        body, name="dx_combine", grid=(s // POS_BLK, d // LANES),
        in_specs=[_pos_spec()] + [_class_spec(dil) for dil in DILATIONS], out_specs=_pos_spec(),
        out_shape=jax.ShapeDtypeStruct((s, d), F32),
        compiler_params=_params(("parallel", "parallel")),
    )(du, *[_class_view(p, dil) for p, dil in zip(parts, DILATIONS)])


HGRN_ROWS = 2048


def _tri(lower, copies=1):
    i = lax.broadcasted_iota(jnp.int32, (HGRN_CHUNK, HGRN_CHUNK * copies), 0)
    j = lax.broadcasted_iota(jnp.int32, (HGRN_CHUNK, HGRN_CHUNK * copies), 1) % HGRN_CHUNK
    return (j <= i) if lower else (j >= i)


def _hgrn_gates(qr, z, lb):
    sq = jax.nn.sigmoid(qr)
    e = jnp.exp(-jnp.abs(z))
    big = 1.0 / (1.0 + e)
    small = e * big
    sg = jnp.where(z >= 0, big, small)
    sn = jnp.where(z >= 0, small, big)
    return sq, qr * sq, sg, sn, lb + (1.0 - lb) * sg, (1.0 - lb) * sn


def _hgrn_fwd(p2, lb, norm_g):
    _, s, d = p2.shape
    nblk = s // HGRN_ROWS
    cps = HGRN_ROWS // HGRN_CHUNK

    def body(p_ref, lb_ref, g_ref, o_ref, on_ref, st_ref, state):
        @pl.when(pl.program_id(1) == 0)
        def _():
            state[...] = jnp.zeros_like(state)

        chunks = [pl.ds(c * HGRN_CHUNK, HGRN_CHUNK) for c in range(cps)]
        ltri = _tri(True)
        lsel = _tri(True, 3).astype(BF16)
        _, q, _, _, f, key = _hgrn_gates(p_ref[0], p_ref[1], lb_ref[...])
        lf = jnp.log(f)
        v = [p_ref[2, rows, :].astype(BF16) for rows in chunks]
        b = [_exact_dot(lsel, lf[c * HGRN_CHUNK:(c + 1) * HGRN_CHUNK]) for c in range(cps)]
        b_last = [bc[HGRN_CHUNK - 1:HGRN_CHUNK, :] for bc in b]
        q = [q[c * HGRN_CHUNK:(c + 1) * HGRN_CHUNK] for c in range(cps)]
        key = [key[c * HGRN_CHUNK:(c + 1) * HGRN_CHUNK] for c in range(cps)]
        qd = [(q[c] * jnp.exp(b[c])).astype(BF16) for c in range(cps)]
        kd = [(key[c] * jnp.exp(-b[c])).astype(BF16) for c in range(cps)]
        k2 = [(key[c] * jnp.exp(b_last[c] - b[c])).astype(BF16) for c in range(cps)]
        a = [jnp.where(ltri, _dot_nt(qd[c], kd[c]), 0.0).astype(BF16) for c in range(cps)]
        kv = [_dot_tn(v[c], k2[c]) for c in range(cps)]
        st, sts = state[...], []
        for c in range(cps):
            sts.append(st)
            st_ref[0, c] = st
            st = st * jnp.exp(b_last[c]) + kv[c]
        state[...] = st
        gv = g_ref[...]
        for c in range(cps):
            o = _dot(a[c], v[c]) + _dot_nt(qd[c], sts[c].astype(BF16))
            o_ref[chunks[c], :] = o
            r = lax.rsqrt(jnp.mean(o * o, axis=-1, keepdims=True) + RMS_EPS)
            on_ref[chunks[c], :] = (o * r * gv).astype(BF16)

    vec = pl.BlockSpec((1, LANES), lambda h, c: (0, h))
    col = pl.BlockSpec((HGRN_ROWS, LANES), lambda h, c: (c, h))
    return pl.pallas_call(
        body, name="hgrn_fwd", grid=(HGRN_HEADS, nblk),
        in_specs=[pl.BlockSpec((3, HGRN_ROWS, LANES), lambda h, c: (0, c, h)), vec, vec],
        out_specs=[col, col, pl.BlockSpec((1, cps, LANES, LANES), lambda h, c: (h, c, 0, 0))],
        out_shape=[jax.ShapeDtypeStruct((s, d), F32), jax.ShapeDtypeStruct((s, d), BF16),
                   jax.ShapeDtypeStruct((HGRN_HEADS, s // HGRN_CHUNK, LANES, LANES), F32)],
        scratch_shapes=[pltpu.VMEM((LANES, LANES), F32)],
        compiler_params=_params(("parallel", "arbitrary")),
    )(*_in_hbm(p2, lb, norm_g))


def _hgrn_bwd(p2, lb, norm_g, o_raw, states, dyn):
    _, s, d = p2.shape
    nblk = s // HGRN_ROWS
    cps = HGRN_ROWS // HGRN_CHUNK

    def body(p_ref, lb_ref, g_ref, o_ref, st_ref, dy_ref, dp_ref, dg_ref, dlb_ref, dstate):
        @pl.when(pl.program_id(1) == 0)
        def _():
            dstate[...] = jnp.zeros_like(dstate)
            dg_ref[...] = jnp.zeros_like(dg_ref)
            dlb_ref[...] = jnp.zeros_like(dlb_ref)

        n = cps
        cut = lambda t: [t[c * HGRN_CHUNK:(c + 1) * HGRN_CHUNK] for c in range(n)]
        chunks = [pl.ds(c * HGRN_CHUNK, HGRN_CHUNK) for c in range(n)]
        lbv = lb_ref[...]
        ltri = _tri(True)
        lsel = _tri(True, 3).astype(BF16)
        usel = _tri(False, 3).astype(BF16)
        last_row = lax.broadcasted_iota(jnp.int32, (HGRN_CHUNK, LANES), 0) == HGRN_CHUNK - 1
        qr, z = p_ref[0], p_ref[1]
        sq, q, sg, sn, f, key = _hgrn_gates(qr, z, lbv)
        lf = jnp.log(f)
        v = [p_ref[2, rows, :].astype(BF16) for rows in chunks]
        o, dyv = o_ref[...], dy_ref[...]
        r = lax.rsqrt(jnp.mean(o * o, axis=-1, keepdims=True) + RMS_EPS)
        oh = o * r
        doh = dyv * g_ref[...]
        do = cut((r * (doh - oh * jnp.mean(doh * oh, axis=-1, keepdims=True))).astype(BF16))
        dg_ref[...] += jnp.sum(dyv * oh, axis=0, keepdims=True)
        b = [_exact_dot(lsel, lfc) for lfc in cut(lf)]
        b_last = [bc[HGRN_CHUNK - 1:HGRN_CHUNK, :] for bc in b]
        q, key = cut(q), cut(key)
        eb = [jnp.exp(bc) for bc in b]
        enb = [jnp.exp(-bc) for bc in b]
        e2 = [jnp.exp(b_last[c] - b[c]) for c in range(n)]
        dec = [jnp.exp(bl) for bl in b_last]
        qd_f = [q[c] * eb[c] for c in range(n)]
        kd_f = [key[c] * enb[c] for c in range(n)]
        k2_f = [key[c] * e2[c] for c in range(n)]
        qd, kd, k2 = ([t.astype(BF16) for t in ts] for ts in (qd_f, kd_f, k2_f))
        a = [jnp.where(ltri, _dot_nt(qd[c], kd[c]), 0.0).astype(BF16) for c in range(n)]
        da = [jnp.where(ltri, _dot_nt(do[c], v[c]), 0.0).astype(BF16) for c in range(n)]
        grow = [_dot_tn(do[c], qd[c]) for c in range(n)]
        dst, dsts = dstate[...], [None] * n
        for c in reversed(range(n)):
            dsts[c] = dst
            dst = dst * dec[c] + grow[c]
        dstate[...] = dst
        st = [st_ref[0, c] for c in range(n)]
        dstb = [t.astype(BF16) for t in dsts]
        dv = [_dot_tn(a[c], do[c]) + _dot_nt(k2[c], dstb[c]) for c in range(n)]
        dqd = [_dot(da[c], kd[c]) + _dot(do[c], st[c].astype(BF16)) for c in range(n)]
        dkd = [_dot_tn(da[c], qd[c]) for c in range(n)]
        dk2 = [_dot(v[c], dstb[c]) for c in range(n)]
        db = []
        for c in range(n):
            ddec = jnp.sum(dsts[c] * st[c], axis=0, keepdims=True)
            db_last = jnp.sum(dk2[c] * k2_f[c], axis=0, keepdims=True) + ddec * dec[c]
            db.append(dqd[c] * qd_f[c] - dkd[c] * kd_f[c] - dk2[c] * k2_f[c] + jnp.where(last_row, db_last, 0.0))
        dlf = [_exact_dot(usel, dbc) for dbc in db]
        f, sg, sn, sq, qr = cut(f), cut(sg), cut(sn), cut(sq), cut(qr)
        dlb_acc = jnp.zeros((1, LANES), F32)
        for c in range(n):
            dkey = dkd[c] * enb[c] + dk2[c] * e2[c]
            common = dlf[c] / f[c] - dkey
            dlb_acc += jnp.sum(common * sn[c], axis=0, keepdims=True)
            dp_ref[0, chunks[c], :] = (dqd[c] * eb[c] * (sq[c] * (1.0 + qr[c] * (1.0 - sq[c])))).astype(BF16)
            dp_ref[1, chunks[c], :] = (common * ((1.0 - lbv) * sg[c] * sn[c])).astype(BF16)
            dp_ref[2, chunks[c], :] = dv[c].astype(BF16)
        dlb_ref[...] += dlb_acc

    def rev(c):
        return nblk - 1 - c

    vec = pl.BlockSpec((1, LANES), lambda h, c: (0, h))
    col = pl.BlockSpec((HGRN_ROWS, LANES), lambda h, c: (rev(c), h))
    p3 = pl.BlockSpec((3, HGRN_ROWS, LANES), lambda h, c: (0, rev(c), h))
    return pl.pallas_call(
        body, name="hgrn_bwd", grid=(HGRN_HEADS, nblk),
        in_specs=[p3, vec, vec, col, pl.BlockSpec((1, cps, LANES, LANES), lambda h, c: (h, rev(c), 0, 0)), col],
        out_specs=[p3, vec, vec],
        out_shape=[jax.ShapeDtypeStruct((3, s, d), BF16), jax.ShapeDtypeStruct((1, d), F32),
                   jax.ShapeDtypeStruct((1, d), F32)],
        scratch_shapes=[pltpu.VMEM((LANES, LANES), F32)],
        compiler_params=_params(("parallel", "arbitrary")),
    )(*_in_hbm(p2, lb, norm_g, o_raw, states, dyn))


SMALL_ROWS = 16
FUSED_TM = 512


def _relu2(h):
    r = jnp.maximum(h.astype(F32), 0.0)
    return r * r


def _dact_epi(acc, h):
    return (acc * (2.0 * jnp.maximum(h.astype(F32), 0.0)),)


def _rope_epi(j, acc, ct, sst):
    acc_t = acc.T
    rot_t = _rotate_t(acc_t, ct, sst) * jnp.where(j == 0, ATTN_SCALE, 1.0)
    out_t = jnp.where(j >= 2, acc_t, rot_t).astype(BF16)
    return out_t.T, out_t


def _local_step(x, target, get_w, small, on_grads, deps=()):
    s, d = x.shape
    lb = small["lb"]
    gq = 3 * d
    gt = gq // 1024

    def ln_vecs(which, layer):
        return small[f"ln_{which}_g"][layer:layer + 1], small[f"ln_{which}_b"][layer:layer + 1]

    def ln_of(a, w, x_in, which, layer, *, name, k, a_pre=None, deps=(), prev=None):
        vecs = ln_vecs(which, layer) + (ln_vecs(*prev) if prev else ())
        return _mm(a, w, "nn", name=name, m=s, n=d, k=k, tm=FUSED_TM, tk=k, a_pre=a_pre, out_dtypes=(BF16, F32),
                   row_outs=((F32, 1),), epi=_ln_epi, tile_extras=(x_in,), vec_extras=vecs, deps=deps)

    def ln_grad_of(a, w, du_next, xh, rstd, which, layer, *, name, k, a_split=1, deps=()):
        return _mm(a, w, "nt", name=name, m=s, n=d, k=k, tm=FUSED_TM, tk=k, a_split=a_split,
                   out_dtypes=(F32, BF16), vec_outs=2, epi=_ln_grad_epi, tile_extras=(du_next, xh), row_extras=(rstd,),
                   vec_extras=(small[f"ln_{which}_g"][layer:layer + 1],), deps=deps)

    def ffn_fwd(xb, xh_in, w_up, w_down, tag):
        h = _mm(xb, w_up, "nn", name=f"ffn_up_{tag}", m=s, n=D_FF, k=d, tm=2048, out_dtypes=(BF16,))
        return (h,) + tuple(ln_of(h, w_down, xh_in, "ffn", tag, name=f"ffn_down_{tag}", k=D_FF, a_pre=_relu2,
                                  prev=("mix", tag)))

    def ffn_bwd(du, dub, xb, h, w_up, w_down, xh, rstd, tag):
        dh = _mm(dub, w_down, "nt", name=f"ffn_dact_{tag}", m=s, n=D_FF, k=d, tm=2048, out_dtypes=(BF16,), epi=_dact_epi,
                 tile_extras=(h,))
        g_down = _mm(h, dub, "tn", name=f"ffn_gdown_{tag}", m=D_FF, n=d, k=s, tm=512, tk=s, a_pre=_relu2,
                     out_dtypes=(BF16,))
        g_up = _mm(xb, dh, "tn", name=f"ffn_gup_{tag}", m=d, n=D_FF, k=s, tk=s, out_dtypes=(BF16,))
        after = on_grads(f"ffn{tag}", {f"ffn_w_down{tag}": g_down, f"ffn_w_up{tag}": g_up})
        return ln_grad_of(dh, w_up, du, xh, rstd, "mix", tag, name=f"ffn_dx_{tag}", k=D_FF, deps=after)

    xs = _prep_x(x, deps)
    tabs, tabs_t = zip(*[_rope_tables(s, dil) for dil in DILATIONS])
    w_ain, after = get_w("attn_w_in", xs + [t for tab in tabs + tabs_t for t in tab])
    qkvs, qkv_ts, o_parts, lse_parts = [], [], [], []
    for g, dil in enumerate(DILATIONS):
        qkv, qkv_t = _mm(xs[g], w_ain, "nn", name=f"attn_in_{g}", m=s, n=gq, k=d, b_col_off=g * gt, out_split=3,
                         out_dtypes=(BF16,), epi=_rope_epi, epi_wants_j=True, row_extras=tabs_t[g], t_out=True,
                         deps=after if g == 0 else ())
        qkvs.append(qkv)
        qkv_ts.append(qkv_t)
        o_g, lse_g = _attn_fwd(qkv, dil, name=f"attn_fwd_{g}")
        o_parts.append(o_g)
        lse_parts.append(lse_g)
    o_f, o_b, lse_t = _attn_combine(o_parts, lse_parts)
    w_aout, after = get_w("attn_w_out", o_b)
    x1b, xh1, r1 = ln_of(o_b, w_aout, x, "mix", 0, name="attn_out", k=d, deps=after)
    w_up0, w_down0 = get_w("ffn_w_up0", o_b)[0], get_w("ffn_w_down0", o_b)[0]
    h0, x2b, xh2, r2 = ffn_fwd(x1b, xh1, w_up0, w_down0, 0)
    w_hin, w_hout, norm_g = get_w("hgrn_w_in", x2b)[0], get_w("hgrn_w_out", x2b)[0], get_w("hgrn_norm_g", x2b)[0]
    p2 = _mm(x2b, w_hin, "nn", name="hgrn_in", m=s, n=3 * d, k=d, tm=2048, out_split=3)
    o_raw, o_n, states = _hgrn_fwd(p2, lb, norm_g)
    w_up1, w_down1 = get_w("ffn_w_up1", o_n)[0], get_w("ffn_w_down1", o_n)[0]
    x3b, xh3, r3 = ln_of(o_n, w_hout, xh2, "mix", 1, name="hgrn_out", k=d, prev=("ffn", 0))
    h1 = _mm(x3b, w_up1, "nn", name="ffn_up_1", m=s, n=D_FF, k=d, tm=2048, out_dtypes=(BF16,))
    du, dub, dg_ffn1, db_ffn1, sq = _mm(
        h1, w_down1, "nn", name="ffn_down_1", m=s, n=d, k=D_FF, tm=FUSED_TM, tk=D_FF, a_pre=_relu2,
        out_dtypes=(F32, BF16), vec_outs=3, epi=_ln_loss_epi, tile_extras=(xh3, target),
        vec_extras=ln_vecs("ffn", 1) + ln_vecs("mix", 1))
    du, dub, dg_mix1, db_mix1 = ffn_bwd(du, dub, x3b, h1, w_up1, w_down1, xh3, r3, 1)
    dyn = _mm(dub, w_hout, "nt", name="hgrn_dout", m=s, n=d, k=d)
    g_hout = _mm(o_n, dub, "tn", name="hgrn_gout", m=d, n=d, k=s, out_dtypes=(BF16,))
    dp2, d_norm_g, d_lb = _hgrn_bwd(p2, lb, norm_g, o_raw, states, dyn)
    g_hin = _mm(x2b, dp2, "tn", name="hgrn_gin", m=d, n=3 * d, k=s, tk=s, b_split=3, out_dtypes=(BF16,))
    after = on_grads("hgrn", {"hgrn_w_out": g_hout, "hgrn_w_in": g_hin})
    du, dub, dg_ffn0, db_ffn0 = ln_grad_of(dp2, w_hin, du, xh2, r2, "ffn", 0, name="hgrn_dx", k=3 * d, a_split=3,
                                           deps=after)
    du, dub, dg_mix0, db_mix0 = ffn_bwd(du, dub, x1b, h0, w_up0, w_down0, xh1, r1, 0)
    g_aout = _mm(o_b, dub, "tn", name="attn_gout", m=d, n=d, k=s, out_dtypes=(BF16,))
    sm1 = jax.nn.softmax(small["lb_logits"], axis=0)
    d_l1 = d_lb * (sm1[0:1] * sm1[1:2])
    zeros = jnp.zeros((SMALL_ROWS - 12, d), F32)
    loss_row = jnp.broadcast_to(0.5 * jnp.sum(sq) / d, (1, d))
    small_grads = jnp.concatenate([dg_mix0, dg_mix1, db_mix0, db_mix1, dg_ffn0, dg_ffn1, db_ffn0, db_ffn1,
                                   -d_l1, d_l1, d_norm_g, loss_row, zeros], axis=0)
    after = on_grads("attn_out", {"attn_w_out": g_aout, "small": small_grads})
    do = _mm(dub, w_aout, "nt", name="attn_dout", m=s, n=d, k=d, deps=after)
    do_parts, do_ts, aux_parts = _attn_bwd_prep(do, o_f, lse_t)
    g_ain, dqkvs = None, []
    for g, dil in enumerate(DILATIONS):
        dqkvs.append(_attn_bwd(qkvs[g], qkv_ts[g], do_parts[g], do_ts[g], aux_parts[g], tabs[g], dil,
                               name=f"attn_bwd_{g}"))
        g_ain = _mm(xs[g], dqkvs[g], "tn", name=f"attn_gin_{g}", m=d, n=gq, k=s, tk=s, b_split=3, out_dtypes=(BF16,),
                    out_col_off=g * gt, out_cols=3 * gq, alias=g_ain)
    after = on_grads("attn_in", {"attn_w_in": g_ain})
    dx_parts = [_mm(dqkvs[g], w_ain, "nt", name=f"attn_dx_{g}", m=s, n=d, k=gq, tm=FUSED_TM, tk=gq, a_split=3, b_k_off=g,
                    deps=after if g == 0 else ())
                for g in range(len(DILATIONS))]
    return _dx_combine(du, dx_parts)


def _mesh_place():
    x, y, c = lax.axis_index("x"), lax.axis_index("y"), lax.axis_index("c")
    return x, y, c, 4 * x + 2 * y + c


def _peer(x, y, c, k):
    px = 1 - x if (k >> 2) & 1 else x
    py = 1 - y if (k >> 1) & 1 else y
    pc = 1 - c if k & 1 else c
    return (px, py, pc), 4 * px + 2 * py + pc


def _window(ref, axis, size, idx):
    if axis is None:
        return ref
    sl = [slice(None)] * len(ref.shape)
    sl[axis] = pl.ds(idx * size, size)
    return ref.at[tuple(sl)]


_HBM = pl.BlockSpec(memory_space=pltpu.HBM)
_SEM = pl.BlockSpec(memory_space=pltpu.SEMAPHORE)
_EFFECT = pltpu.SideEffectType.DATAFLOW_SIDE_EFFECTING


def _xchg_ends(src_ref, land_ref, axis, gather, me, other):
    if gather:
        size = src_ref.shape[axis]
        return src_ref, _window(land_ref, axis, size, me), _window(land_ref, axis, size, other)
    size = None if axis is None else src_ref.shape[axis] // N_DEV
    return _window(src_ref, axis, size, other), land_ref.at[me], land_ref.at[other]


def _xchg_start(srcs, lands, axes, *, gather, name, deps=(), peers=tuple(range(1, N_DEV))):
    n = len(srcs)
    nd = len(deps)

    def body(*refs):
        src_refs, land_refs = refs[:n], refs[n:2 * n]
        send, recv = refs[2 * n + nd:3 * n + nd], refs[3 * n + nd:4 * n + nd]
        token = refs[-1]
        x, y, c, me = _mesh_place()
        for k in peers:
            peer, pidx = _peer(x, y, c, k)
            for i in range(n):
                src, dst, _ = _xchg_ends(src_refs[i], land_refs[i], axes[i], gather, me, pidx)
                pltpu.make_async_remote_copy(
                    src_ref=src, dst_ref=dst, send_sem=send[i].at[k - 1], recv_sem=recv[i].at[k - 1],
                    device_id=peer, device_id_type=pl.DeviceIdType.MESH).start()
        for i in range(n):
            src, dst, _ = _xchg_ends(src_refs[i], land_refs[i], axes[i], gather, me, me)
            pltpu.make_async_copy(src, dst, send[i].at[N_DEV - 1]).start()
        token[...] = jnp.zeros_like(token)

    bufs = list(srcs) + list(lands)
    outs = pl.pallas_call(
        body, name=name,
        out_shape=[pltpu.SemaphoreType.DMA((N_DEV,))] * (2 * n) + [pltpu.HBM(b.shape, b.dtype) for b in bufs]
        + [jax.ShapeDtypeStruct((8, LANES), F32)],
        in_specs=[_HBM] * (2 * n) + [pl.BlockSpec(memory_space=pl.ANY)] * nd,
        out_specs=[_SEM] * (2 * n) + [_HBM] * (2 * n) + [pl.BlockSpec(memory_space=pltpu.VMEM)],
        input_output_aliases={i: 2 * n + i for i in range(2 * n)},
        compiler_params=pltpu.CompilerParams(has_side_effects=_EFFECT),
    )(*[pltpu.with_memory_space_constraint(b, pltpu.HBM) for b in bufs], *deps)
    return dict(send=outs[:n], recv=outs[n:2 * n], srcs=outs[2 * n:3 * n], lands=outs[3 * n:4 * n], token=outs[-1],
                axes=list(axes), gather=gather)


SIBLING = 1
SAME_CORE = (2, 4, 6)


def _gather_relay(xc, after, *, name):
    axis = xc["axes"][0]

    def body(src_ref, land_ref, send1, recv1, *rest):
        send2, recv2 = rest[-2:]
        x, y, c, me = _mesh_place()
        sibling, _ = _peer(x, y, c, SIBLING)
        for j, k in enumerate(SAME_CORE):
            peer, pidx = _peer(x, y, c, k)
            src, _, got = _xchg_ends(src_ref, land_ref, axis, True, me, pidx)
            pltpu.make_async_remote_copy(
                src_ref=src, dst_ref=got, send_sem=send1.at[k - 1], recv_sem=recv1.at[k - 1],
                device_id=peer, device_id_type=pl.DeviceIdType.MESH).wait_recv()
            pltpu.make_async_remote_copy(
                src_ref=got, dst_ref=got, send_sem=send2.at[j], recv_sem=recv2.at[j],
                device_id=sibling, device_id_type=pl.DeviceIdType.MESH).start()

    src, land = xc["srcs"][0], xc["lands"][0]
    outs = pl.pallas_call(
        body, name=name,
        out_shape=[pltpu.HBM(src.shape, src.dtype), pltpu.HBM(land.shape, land.dtype)]
        + [pltpu.SemaphoreType.DMA((len(SAME_CORE),))] * 2,
        in_specs=[_HBM, _HBM, _SEM, _SEM] + [pl.BlockSpec(memory_space=pl.ANY)] * len(after),
        out_specs=[_HBM, _HBM, _SEM, _SEM], input_output_aliases={0: 0, 1: 1},
        compiler_params=pltpu.CompilerParams(has_side_effects=_EFFECT),
    )(src, land, xc["send"][0], xc["recv"][0], *after)
    return dict(src=outs[0], land=outs[1], send=outs[2], recv=outs[3])


def _gather_relay_wait(xc, relay, after, *, name):
    axis = xc["axes"][0]

    def body(src_ref, land_ref, send1, recv1, send2, recv2, after_ref, src_out, land_out):
        x, y, c, me = _mesh_place()
        sibling, sidx = _peer(x, y, c, SIBLING)
        for k in (SIBLING,) + SAME_CORE:
            peer, pidx = _peer(x, y, c, k)
            src, dst, got = _xchg_ends(src_ref, land_ref, axis, True, me, pidx)
            pltpu.make_async_remote_copy(
                src_ref=src, dst_ref=dst, send_sem=send1.at[k - 1], recv_sem=recv1.at[k - 1],
                device_id=peer, device_id_type=pl.DeviceIdType.MESH).wait_send()
        src, dst, got = _xchg_ends(src_ref, land_ref, axis, True, me, sidx)
        pltpu.make_async_remote_copy(
            src_ref=src, dst_ref=got, send_sem=send1.at[SIBLING - 1], recv_sem=recv1.at[SIBLING - 1],
            device_id=sibling, device_id_type=pl.DeviceIdType.MESH).wait_recv()
        src, dst, _ = _xchg_ends(src_ref, land_ref, axis, True, me, me)
        pltpu.make_async_copy(src, dst, send1.at[N_DEV - 1]).wait()
        for j, k in enumerate(SAME_CORE):
            _, pidx = _peer(x, y, c, k)
            _, qidx = _peer(x, y, c, k ^ SIBLING)
            _, _, sent = _xchg_ends(src_ref, land_ref, axis, True, me, pidx)
            _, _, got = _xchg_ends(src_ref, land_ref, axis, True, me, qidx)
            pltpu.make_async_remote_copy(
                src_ref=sent, dst_ref=sent, send_sem=send2.at[j], recv_sem=recv2.at[j],
                device_id=sibling, device_id_type=pl.DeviceIdType.MESH).wait_send()
            pltpu.make_async_remote_copy(
                src_ref=got, dst_ref=got, send_sem=send2.at[j], recv_sem=recv2.at[j],
                device_id=sibling, device_id_type=pl.DeviceIdType.MESH).wait_recv()

    outs = pl.pallas_call(
        body, name=name, out_shape=[pltpu.HBM(relay["src"].shape, relay["src"].dtype),
                                    pltpu.HBM(relay["land"].shape, relay["land"].dtype)],
        in_specs=[_HBM, _HBM, _SEM, _SEM, _SEM, _SEM, pl.BlockSpec(memory_space=pl.ANY)],
        out_specs=[_HBM, _HBM], input_output_aliases={0: 0, 1: 1},
        compiler_params=pltpu.CompilerParams(has_side_effects=_EFFECT),
    )(relay["src"], relay["land"], xc["send"][0], xc["recv"][0], relay["send"], relay["recv"], after)
    return outs[1]


def _xchg_wait(xc, items, after, *, name):
    m = len(items)
    gather = xc["gather"]
    axes = [xc["axes"][i] for i in items]

    def body(*refs):
        src_refs, land_refs = refs[:m], refs[m:2 * m]
        send, recv = refs[2 * m:3 * m], refs[3 * m:4 * m]
        x, y, c, me = _mesh_place()
        for k in range(1, N_DEV):
            peer, pidx = _peer(x, y, c, k)
            for j in range(m):
                src, dst, got = _xchg_ends(src_refs[j], land_refs[j], axes[j], gather, me, pidx)
                pltpu.make_async_remote_copy(
                    src_ref=src, dst_ref=dst, send_sem=send[j].at[k - 1], recv_sem=recv[j].at[k - 1],
                    device_id=peer, device_id_type=pl.DeviceIdType.MESH).wait_send()
                pltpu.make_async_remote_copy(
                    src_ref=src, dst_ref=got, send_sem=send[j].at[k - 1], recv_sem=recv[j].at[k - 1],
                    device_id=peer, device_id_type=pl.DeviceIdType.MESH).wait_recv()
        for j in range(m):
            src, dst, _ = _xchg_ends(src_refs[j], land_refs[j], axes[j], gather, me, me)
            pltpu.make_async_copy(src, dst, send[j].at[N_DEV - 1]).wait()

    bufs = [xc["srcs"][i] for i in items] + [xc["lands"][i] for i in items]
    sems = [xc["send"][i] for i in items] + [xc["recv"][i] for i in items]
    outs = pl.pallas_call(
        body, name=name, out_shape=[pltpu.HBM(b.shape, b.dtype) for b in bufs],
        in_specs=[_HBM] * (2 * m) + [_SEM] * (2 * m) + [pl.BlockSpec(memory_space=pl.ANY)] * len(after),
        out_specs=[_HBM] * (2 * m), input_output_aliases={j: j for j in range(2 * m)},
        compiler_params=pltpu.CompilerParams(has_side_effects=_EFFECT),
    )(*bufs, *sems, *after)
    return outs[m:]


def _cast_bf16(a, *, name):
    r, c = a.shape
    tr = min(r, 512)

    def body(a_ref, o_ref):
        o_ref[...] = a_ref[...].astype(BF16)

    spec = pl.BlockSpec((tr, c), lambda i: (i, 0))
    return pl.pallas_call(body, name=name, grid=(r // tr,), in_specs=[spec], out_specs=spec,
                          out_shape=jax.ShapeDtypeStruct((r, c), BF16), compiler_params=_params(("parallel",)))(a)


def _adamw(slabs, w, m, v, *, name):
    layers, r, c = w.shape
    tr = min(r, 256)

    def body(*refs):
        s_refs = refs[:layers]
        w_ref, m_ref, v_ref, g_ref, d_ref, mo_ref, vo_ref = refs[layers:]
        for l in range(layers):
            g = s_refs[l][0].astype(F32)
            for i in range(1, N_DEV):
                g = g + s_refs[l][i].astype(F32)
            m2 = ADAM_B1 * m_ref[l] + (1.0 - ADAM_B1) * g
            v2 = ADAM_B2 * v_ref[l] + (1.0 - ADAM_B2) * (g * g)
            m_hat = m2 / (1.0 - ADAM_B1 ** ADAM_STEP)
            v_hat = v2 / (1.0 - ADAM_B2 ** ADAM_STEP)
            g_ref[l] = g
            d_ref[l] = -ADAM_LR * (m_hat / (jnp.sqrt(v_hat) + ADAM_EPS) + ADAM_WD * w_ref[l])
            mo_ref[l] = m2
            vo_ref[l] = v2

    spec = pl.BlockSpec((layers, tr, c), lambda i: (0, i, 0))
    return pl.pallas_call(
        body, name=name, grid=(r // tr,),
        in_specs=[pl.BlockSpec((N_DEV, tr, c), lambda i: (0, i, 0))] * layers + [spec, spec, spec],
        out_specs=[spec] * 4, out_shape=[jax.ShapeDtypeStruct((layers, r, c), F32)] * 4,
        compiler_params=_params(("parallel",)),
    )(*_in_hbm(*slabs, w, m, v))


GATHER_AXIS = {"attn_w_in": 1, "attn_w_out": 0, "ffn_w_up0": 1, "ffn_w_down0": 0, "hgrn_w_in": 1, "hgrn_w_out": 0,
               "hgrn_norm_g": 1, "ffn_w_up1": 1, "ffn_w_down1": 0}
GATHER_STAGES = (("attn_w_in",), ("attn_w_out", "ffn_w_up0", "ffn_w_down0", "hgrn_w_in", "hgrn_w_out", "hgrn_norm_g"),
                 ("ffn_w_up1", "ffn_w_down1"))
GATHER_WAITS = ((("attn_w_in",), 0, 1), (("attn_w_out", "ffn_w_up0", "ffn_w_down0"), 1, 2),
                (("hgrn_w_in", "hgrn_w_out", "hgrn_norm_g"), 1, None), (("ffn_w_up1", "ffn_w_down1"), 2, None))
SCATTER_AXIS = dict(GATHER_AXIS, small=None)
BIG = ("attn_w_in", "attn_w_out", "hgrn_w_in", "hgrn_w_out", "ffn_w_up", "ffn_w_down")
SMALL = ("lb_logits", "ln_mix_g", "ln_mix_b", "ln_ffn_g", "ln_ffn_b")
SMALL_ROW = {"ln_mix_g": 0, "ln_mix_b": 2, "ln_ffn_g": 4, "ln_ffn_b": 6, "lb_logits": 8}
NORM_G_ROW = 10
LOSS_ROW = 11


def kernel(x, attn_w_in, attn_w_out, hgrn_w_in, hgrn_w_out, hgrn_norm_g, lb_logits, ln_mix_g, ln_mix_b, ln_ffn_g, ln_ffn_b, ffn_w_up, ffn_w_down, loss_target, m_attn_w_in, m_attn_w_out, m_hgrn_w_in, m_hgrn_w_out, m_hgrn_norm_g, m_lb_logits, m_ln_mix_g, m_ln_mix_b, m_ln_ffn_g, m_ln_ffn_b, m_ffn_w_up, m_ffn_w_down, v_attn_w_in, v_attn_w_out, v_hgrn_w_in, v_hgrn_w_out, v_hgrn_norm_g, v_lb_logits, v_ln_mix_g, v_ln_mix_b, v_ln_ffn_g, v_ln_ffn_b, v_ffn_w_up, v_ffn_w_down):
    wts = dict(attn_w_in=attn_w_in, attn_w_out=attn_w_out, hgrn_w_in=hgrn_w_in, hgrn_w_out=hgrn_w_out,
               hgrn_norm_g=hgrn_norm_g, lb_logits=lb_logits, ln_mix_g=ln_mix_g, ln_mix_b=ln_mix_b, ln_ffn_g=ln_ffn_g,
               ln_ffn_b=ln_ffn_b, ffn_w_up=ffn_w_up, ffn_w_down=ffn_w_down)
    mom = dict(attn_w_in=m_attn_w_in, attn_w_out=m_attn_w_out, hgrn_w_in=m_hgrn_w_in, hgrn_w_out=m_hgrn_w_out,
               hgrn_norm_g=m_hgrn_norm_g, lb_logits=m_lb_logits, ln_mix_g=m_ln_mix_g, ln_mix_b=m_ln_mix_b,
               ln_ffn_g=m_ln_ffn_g, ln_ffn_b=m_ln_ffn_b, ffn_w_up=m_ffn_w_up, ffn_w_down=m_ffn_w_down)
    vel = dict(attn_w_in=v_attn_w_in, attn_w_out=v_attn_w_out, hgrn_w_in=v_hgrn_w_in, hgrn_w_out=v_hgrn_w_out,
               hgrn_norm_g=v_hgrn_norm_g, lb_logits=v_lb_logits, ln_mix_g=v_ln_mix_g, ln_mix_b=v_ln_mix_b,
               ln_ffn_g=v_ln_ffn_g, ln_ffn_b=v_ln_ffn_b, ffn_w_up=v_ffn_w_up, ffn_w_down=v_ffn_w_down)
    me = 4 * lax.axis_index("x") + 2 * lax.axis_index("y") + lax.axis_index("c")

    src = {"attn_w_in": attn_w_in[0], "attn_w_out": attn_w_out[0], "hgrn_w_in": hgrn_w_in[0], "hgrn_w_out": hgrn_w_out[0],
           "ffn_w_up0": ffn_w_up[0], "ffn_w_down0": ffn_w_down[0], "ffn_w_up1": ffn_w_up[1], "ffn_w_down1": ffn_w_down[1]}
    gathers, got = {}, {}
    casts = {nm: _cast_bf16(a, name=f"cast_{nm}") for nm, a in src.items()}
    casts["hgrn_norm_g"] = hgrn_norm_g

    def start_gather(stage, deps):
        shards, lands = [], []
        for nm in GATHER_STAGES[stage]:
            sh = casts[nm]
            ax = GATHER_AXIS[nm]
            shape = list(sh.shape)
            shape[ax] *= N_DEV
            shards.append(sh)
            lands.append(lax.empty(tuple(shape), sh.dtype))
        peers = (SIBLING,) + SAME_CORE if stage == 0 else tuple(range(1, N_DEV))
        gathers[stage] = _xchg_start(shards, lands, [GATHER_AXIS[nm] for nm in GATHER_STAGES[stage]], gather=True,
                                     name=f"gather_start_{stage}", deps=deps, peers=peers)
        return [gathers[stage]["token"]]

    def get_w(name, after):
        deps = []
        if name not in got:
            group, stage, then = [w for w in GATHER_WAITS if name in w[0]][0]
            xc = gathers[stage]
            if stage == 0:
                later = [casts[nm] for st in GATHER_STAGES[1:] for nm in st if nm != "hgrn_norm_g"]
                relay = _gather_relay(xc, [xc["token"], *after, *later], name="gather_relay")
                res = [_gather_relay_wait(xc, relay, xc["token"], name=f"gather_wait_{group[0]}")]
            else:
                res = _xchg_wait(xc, [GATHER_STAGES[stage].index(nm) for nm in group], [after],
                                 name=f"gather_wait_{group[0]}")
            got.update(zip(group, res))
            if then is not None:
                deps = start_gather(then, [res[0]])
        return got[name], deps

    first = start_gather(0, [])

    scattered, held = {}, {}

    def on_grads(tag, grads):
        if tag in ("ffn1", "ffn0"):
            held.update(grads)
            return []
        grads = {**held, **grads}
        held.clear()
        gnames = list(grads)
        axes = [SCATTER_AXIS[nm] for nm in gnames]
        stacks = []
        for nm, ax in zip(gnames, axes):
            shape = list(grads[nm].shape)
            if ax is not None:
                shape[ax] //= N_DEV
            stacks.append(lax.empty((N_DEV, *shape), grads[nm].dtype))
        scattered[tag] = (gnames, _xchg_start([grads[nm] for nm in gnames], stacks, axes, gather=False,
                                              name=f"scatter_start_{tag}"))
        return [scattered[tag][1]["token"]]

    sm = jax.nn.softmax(lb_logits, axis=0)
    csum = jnp.cumsum(sm, axis=0)
    small = dict(lb=(csum - csum[0:1])[1:2], lb_logits=lb_logits, ln_mix_g=ln_mix_g, ln_mix_b=ln_mix_b,
                 ln_ffn_g=ln_ffn_g, ln_ffn_b=ln_ffn_b)
    grad_x = _local_step(x[0], loss_target[0], get_w, small, on_grads, deps=first)
    out = {}

    def stack_small(src_):
        rows = [None] * SMALL_ROWS
        for name in SMALL:
            rows[SMALL_ROW[name]], rows[SMALL_ROW[name] + 1] = src_[name][0:1], src_[name][1:2]
        zero = jnp.zeros((1, x.shape[-1]), F32)
        return jnp.concatenate([zero if r is None else r for r in rows], axis=0)[None]

    def update(name, slabs):
        shape = wts[name].shape
        out[name] = [r.reshape(shape) for r in _adamw(slabs, wts[name], mom[name], vel[name], name=f"adamw_{name}")]
        return out[name][0]

    slabs, after = {}, [grad_x]
    for tag, (gnames, xc) in scattered.items():
        slabs.update(zip(gnames, _xchg_wait(xc, list(range(len(gnames))), after, name=f"scatter_wait_{tag}")))
        if tag == "hgrn":
            after = [update("hgrn_w_out", [slabs["hgrn_w_out"]]), update("hgrn_w_in", [slabs["hgrn_w_in"]])]
        elif tag == "attn_out":
            after = [update("ffn_w_down", [slabs["ffn_w_down0"], slabs["ffn_w_down1"]]),
                     update("ffn_w_up", [slabs["ffn_w_up0"], slabs["ffn_w_up1"]]),
                     update("attn_w_out", [slabs["attn_w_out"]])]
        else:
            update("attn_w_in", [slabs["attn_w_in"]])
    res = _adamw([slabs["small"]], stack_small(wts), stack_small(mom), stack_small(vel), name="adamw_small")
    for name in SMALL:
        out[name] = [r[0, SMALL_ROW[name]:SMALL_ROW[name] + 2] for r in res]
    loss = res[0][0, LOSS_ROW, 0]
    ng = hgrn_norm_g.shape[-1]
    ng_slabs = lax.dynamic_slice(slabs["small"], (0, NORM_G_ROW, me * ng), (N_DEV, 1, ng))
    out["hgrn_norm_g"] = [r[0] for r in _adamw([ng_slabs], hgrn_norm_g[None], m_hgrn_norm_g[None],
                                                v_hgrn_norm_g[None], name="adamw_norm_g")]
    order =("attn_w_in", "attn_w_out", "hgrn_w_in", "hgrn_w_out", "hgrn_norm_g", "lb_logits", "ln_mix_g", "ln_mix_b",
             "ln_ffn_g", "ln_ffn_b", "ffn_w_up", "ffn_w_down")
    return (loss, grad_x[None], *[out[nm][0] for nm in order], *[out[nm][1] for nm in order],
            *[out[nm][2] for nm in order], *[out[nm][3] for nm in order])
```

```python
import jax
import jax.numpy as jnp
from jax import lax
from jax.experimental import pallas as pl
from jax.experimental.pallas import tpu as pltpu

F32 = jnp.float32
BF16 = jnp.bfloat16

N_DEV = 8
LANES = 128
D_MODEL = 1024
ATTN_HEAD_DIM = 64
ATTN_HEADS = 16
ATTN_SCALE = ATTN_HEAD_DIM ** -0.5
ATTN_BLK = 128
DILATIONS = (1, 4, 16)
ROPE_THETA = 10000.0
HGRN_HEADS = 8
HGRN_CHUNK = 64
D_FF = 4096
LN_EPS = 1e-5
RMS_EPS = 1e-6
DEPTH = 2
ALPHA = (2 * DEPTH) ** 0.25
ADAM_LR, ADAM_B1, ADAM_B2, ADAM_EPS, ADAM_WD, ADAM_STEP = 0.001, 0.9, 0.999, 1e-08, 0.01, 10
VMEM_LIMIT = 48 * 1024 * 1024

_NT = (((1,), (1,)), ((), ()))
_TN = (((0,), (0,)), ((), ()))


def _dot(a, b):
    return jnp.dot(a, b, preferred_element_type=F32)


def _dot_nt(a, b):
    return lax.dot_general(a, b, _NT, preferred_element_type=F32)


def _dot_tn(a, b):
    return lax.dot_general(a, b, _TN, preferred_element_type=F32)


def _split3(x):
    p1 = x.astype(BF16)
    r1 = x - p1.astype(F32)
    p2 = r1.astype(BF16)
    p3 = (r1 - p2.astype(F32)).astype(BF16)
    return p1, p2, p3


def _exact_dot(sel3, x):
    return _dot(sel3, jnp.concatenate(_split3(x), axis=0))


def _exact_dot_r(x, sel3):
    return _dot(jnp.concatenate(_split3(x), axis=1), sel3)


def _params(sem=None):
    return pltpu.CompilerParams(dimension_semantics=sem, vmem_limit_bytes=VMEM_LIMIT)


def _mm(a, b, mode, *, name, m, n, k, tm=1024, tn=1024, tk=1024, out_dtypes=(F32,), epi=None, a_pre=None,
        tile_extras=(), row_extras=(), vec_extras=(), row_outs=(), vec_outs=0, a_split=1, b_split=1, out_split=1,
        b_col_off=0, b_k_off=0, out_col_off=0, out_cols=None, alias=None, epi_wants_j=False, deps=(), t_out=False,
        epi_rows=0):
    tm, tn, tk = min(tm, m), min(tn, n), min(tk, k)
    assert m % tm == 0 and n % tn == 0 and k % tk == 0, (name, m, n, k, tm, tn, tk)
    gm, gn, gk = m // tm, n // tn, k // tk
    if mode in ("nn", "nt"):
        if a_split > 1 and tk == k:
            a_spec = pl.BlockSpec((a_split, tm, k // a_split), lambda i, j, kk: (0, i, 0))
        elif a_split > 1:
            kc = (k // a_split) // tk
            a_spec = pl.BlockSpec((None, tm, tk), lambda i, j, kk: (kk // kc, i, kk % kc))
        else:
            a_spec = pl.BlockSpec((tm, tk), lambda i, j, kk: (i, kk))
    else:
        a_spec = pl.BlockSpec((tk, tm), lambda i, j, kk: (kk, i))
    if mode in ("nn", "tn"):
        if b_split > 1:
            nc = (n // b_split) // tn
            b_spec = pl.BlockSpec((None, tk, tn), lambda i, j, kk: (j // nc, kk, j % nc))
        else:
            b_spec = pl.BlockSpec((tk, tn), lambda i, j, kk: (kk + b_k_off, j + b_col_off))
    else:
        b_spec = pl.BlockSpec((tn, tk), lambda i, j, kk: (j + b_col_off, kk + b_k_off))
    if out_split > 1:
        nco = (n // out_split) // tn
        o_spec = pl.BlockSpec((None, tm, tn), lambda i, j, kk: (j // nco, i, j % nco))
        o_shape = (out_split, m, n // out_split)
    else:
        o_spec = pl.BlockSpec((tm, tn), lambda i, j, kk: (i, j + out_col_off))
        o_shape = (m, out_cols if out_cols is not None else n)
    n_ex = len(tile_extras) + len(row_extras) + len(vec_extras)
    n_out = len(out_dtypes)
    n_plain = n_out + len(row_outs)
    assert not vec_outs or gn == 1
    if epi is None:
        def epi(acc):
            return (acc,)
    dot = {"nn": _dot, "nt": _dot_nt, "tn": _dot_tn}[mode]

    def body(*refs):
        a_ref, b_ref = refs[0], refs[1]
        ex = refs[2:2 + n_ex]
        outs = refs[2 + n_ex + (1 if alias is not None else 0) + len(deps):][:n_plain + vec_outs + (1 if t_out else 0)]
        ii = pl.program_id(0)
        jj = pl.program_id(1)

        def product():
            if a_split > 1 and tk == k:
                av = jnp.concatenate([a_ref[p] for p in range(a_split)], axis=1)
            else:
                av = a_ref[...]
            if a_pre is not None:
                av = a_pre(av)
            return dot(av.astype(BF16), b_ref[...].astype(BF16))

        def finish(total):
            lead = (jj,) if epi_wants_j else ()
            res = epi(*lead, total, *[e[...] for e in ex])
            for o, r in zip(outs[:n_plain], res):
                o[...] = r.astype(o.dtype)
            for o, r in zip(outs[n_plain:n_plain + vec_outs], res[n_plain:]):
                @pl.when(ii == 0)
                def _(o=o, r=r):
                    o[...] = r

                @pl.when(ii > 0)
                def _(o=o, r=r):
                    o[...] += r
            if t_out:
                outs[-1][...] = res[-1].astype(outs[-1].dtype)

        if epi_rows:
            acc = refs[-1]
            acc[...] = product()
            n_rowwise = len(tile_extras) + len(row_extras)
            sums = None
            for c in range(tm // epi_rows):
                rows = pl.ds(c * epi_rows, epi_rows)
                res = epi(acc[rows, :], *[e[rows, :] for e in ex[:n_rowwise]], *[e[...] for e in ex[n_rowwise:]])
                for o, r in zip(outs[:n_plain], res):
                    o[rows, :] = r.astype(o.dtype)
                part = res[n_plain:]
                sums = part if sums is None else [s_ + p_ for s_, p_ in zip(sums, part)]
            for o, r in zip(outs[n_plain:n_plain + vec_outs], sums):
                @pl.when(ii == 0)
                def _(o=o, r=r):
                    o[...] = r

                @pl.when(ii > 0)
                def _(o=o, r=r):
                    o[...] += r
        elif gk == 1:
            finish(product())
        else:
            acc = refs[-1]
            kk = pl.program_id(2)

            @pl.when(kk == 0)
            def _():
                acc[...] = product()

            @pl.when(kk > 0)
            def _():
                acc[...] += product()

            @pl.when(kk == gk - 1)
            def _():
                finish(acc[...])

    in_specs = [a_spec, b_spec] + [o_spec] * len(tile_extras)
    in_specs += [pl.BlockSpec((tm, r.shape[1]), lambda i, j, kk: (i, 0)) if r.shape[0] == m else
                 pl.BlockSpec((r.shape[0], tm), lambda i, j, kk: (0, i)) for r in row_extras]
    in_specs += [pl.BlockSpec((1, tn), lambda i, j, kk: (0, j))] * len(vec_extras)
    args = [a, b] + list(tile_extras) + list(row_extras) + list(vec_extras)
    io_alias = {}
    if alias is not None:
        in_specs.append(pl.BlockSpec(memory_space=pl.ANY))
        args.append(alias)
        io_alias = {len(args) - 1: 0}
    in_specs += [pl.BlockSpec(memory_space=pl.ANY)] * len(deps)
    args += list(deps)
    out_specs = [o_spec] * n_out
    out_shape = [jax.ShapeDtypeStruct(o_shape, dt) for dt in out_dtypes]
    for dt, w in row_outs:
        out_specs.append(pl.BlockSpec((tm, w), lambda i, j, kk: (i, 0)))
        out_shape.append(jax.ShapeDtypeStruct((m, w), dt))
    out_specs += [pl.BlockSpec((1, tn), lambda i, j, kk: (0, j))] * vec_outs
    out_shape += [jax.ShapeDtypeStruct((1, n), F32)] * vec_outs
    if t_out:
        assert out_split > 1
        out_specs.append(pl.BlockSpec((None, tn, tm), lambda i, j, kk: (j // nco, j % nco, i)))
        out_shape.append(jax.ShapeDtypeStruct((out_split, n // out_split, m), out_dtypes[0]))
    out = pl.pallas_call(
        body, name=name, grid=(gm, gn, gk), in_specs=in_specs, out_specs=out_specs, out_shape=out_shape,
        scratch_shapes=[pltpu.VMEM((tm, tn), F32)] if gk > 1 or epi_rows else [],
        input_output_aliases=io_alias,
        compiler_params=_params(("arbitrary" if vec_outs else "parallel", "parallel", "arbitrary")),
    )(*args)
    return out[0] if len(out) == 1 else out


def _rope_tables(seq, dil):
    half = ATTN_HEAD_DIM // 2
    row = jnp.arange(seq, dtype=jnp.int32)
    pos = ((row % (seq // dil)) * dil + row // (seq // dil)).astype(F32)

    def tables(features, axis):
        f = jnp.arange(features, dtype=jnp.int32)
        inv = ROPE_THETA ** (-(f % half).astype(F32) * (2.0 / ATTN_HEAD_DIM))
        sign = jnp.where(f % ATTN_HEAD_DIM < half, -1.0, 1.0).astype(F32)
        ang = jnp.expand_dims(pos, axis) * jnp.expand_dims(inv, 1 - axis)
        return jnp.cos(ang), jnp.sin(ang) * jnp.expand_dims(sign, 1 - axis)

    return tables(LANES, 1), tables(ATTN_HEAD_DIM, 0)


def _rotate(x, c, ss, sign=1.0):
    w = x.shape[-1]
    half = ATTN_HEAD_DIM // 2
    lane = lax.broadcasted_iota(jnp.int32, x.shape, 1)
    first = (lane % ATTN_HEAD_DIM) < half
    partner = jnp.where(first, pltpu.roll(x, w - half, 1), pltpu.roll(x, half, 1))
    reps = w // LANES
    if reps > 1:
        c = jnp.concatenate([c] * reps, axis=1)
        ss = jnp.concatenate([ss] * reps, axis=1)
    return x * c + sign * (partner * ss)


def _rotate_t(xt, ct, sst):
    half = ATTN_HEAD_DIM // 2
    heads = xt.shape[0] // ATTN_HEAD_DIM
    parts = []
    for h in range(heads):
        lo = h * ATTN_HEAD_DIM
        parts += [xt[lo + half:lo + 2 * half], xt[lo:lo + half]]
    return (xt * jnp.concatenate([ct] * heads, axis=0)
            + jnp.concatenate(parts, axis=0) * jnp.concatenate([sst] * heads, axis=0))


def _ln_epi(acc, x, g, b, *prev):
    if prev:
        x = x * prev[0] + prev[1]
    u = ALPHA * x + acc
    mu = jnp.mean(u, axis=-1, keepdims=True)
    uc = u - mu
    var = jnp.mean(uc * uc, axis=-1, keepdims=True)
    rstd = lax.rsqrt(var + LN_EPS)
    xh = uc * rstd
    return xh * g + b, xh, rstd


def _ln_grad(dy, xh, rstd, g):
    dxh = dy * g
    m1 = jnp.mean(dxh, axis=-1, keepdims=True)
    m2 = jnp.mean(dxh * xh, axis=-1, keepdims=True)
    du = rstd * (dxh - m1 - xh * m2)
    return du, du, jnp.sum(dy * xh, axis=0, keepdims=True), jnp.sum(dy, axis=0, keepdims=True)


def _ln_grad_epi(acc, du_next, xh, rstd, g):
    return _ln_grad(acc + ALPHA * du_next, xh, rstd, g)


def _ln_loss_epi(acc, x_in, target, g, b, *prev):
    out, xh, rstd = _ln_epi(acc, x_in, g, b, *prev)
    e = out - target
    du, _, dg, db = _ln_grad(e * (1.0 / e.shape[-1]), xh, rstd, g)
    return du, du, dg, db, jnp.sum(e * e, axis=0, keepdims=True)


POS_BLK = 2048


def _class_rows(r, dil):
    return pl.ds(r, POS_BLK // dil, stride=dil) if dil > 1 else pl.ds(0, POS_BLK)


def _class_view(a, dil):
    s, w = a.shape
    return a.reshape(dil, s // dil, w)


def _class_spec(dil, all_tiles=True):
    if all_tiles:
        return pl.BlockSpec((dil, POS_BLK // dil, LANES), lambda i, t: (0, i, t))
    return pl.BlockSpec((dil, POS_BLK // dil, LANES), lambda i, t: (0, i, 0))


def _pos_spec(all_tiles=True):
    if all_tiles:
        return pl.BlockSpec((POS_BLK, LANES), lambda i, t: (i, t))
    return pl.BlockSpec((POS_BLK, LANES), lambda i, t: (i, 0))


def _prep_x(x, deps=()):
    s, d = x.shape

    def body(x_ref, *refs):
        outs = refs[len(deps):]
        for dil, o_ref in zip(DILATIONS, outs):
            for r in range(dil):
                o_ref[r] = x_ref[_class_rows(r, dil), :].astype(BF16)

    outs = pl.pallas_call(
        body, name="prep_x", grid=(s // POS_BLK, d // LANES),
        in_specs=[_pos_spec()] + [pl.BlockSpec(memory_space=pl.ANY)] * len(deps),
        out_specs=[_class_spec(dil) for dil in DILATIONS],
        out_shape=[jax.ShapeDtypeStruct((dil, s // dil, d), BF16) for dil in DILATIONS],
        compiler_params=_params(("parallel", "parallel")),
    )(x, *deps)
    return [o.reshape(s, d) for o in outs]


def _head_expand_matrix():
    h = lax.broadcasted_iota(jnp.int32, (LANES, D_MODEL), 0)
    l = lax.broadcasted_iota(jnp.int32, (LANES, D_MODEL), 1)
    return (l // ATTN_HEAD_DIM == h).astype(BF16)


def _attn_fwd(qkv, dil, *, name):
    _, s, d = qkv.shape
    nq = s // ATTN_BLK
    per = nq // dil
    tiles = d // LANES

    def body(q_ref, kc_ref, kp_ref, vc_ref, vp_ref, o_ref, lse_ref):
        qb = pl.program_id(0)
        first = (qb % per) == 0
        qi = lax.broadcasted_iota(jnp.int32, (ATTN_BLK, 2 * ATTN_BLK), 0)
        kj = lax.broadcasted_iota(jnp.int32, (ATTN_BLK, 2 * ATTN_BLK), 1)
        dist = qi + ATTN_BLK - kj
        valid = (dist >= 0) & (dist <= ATTN_BLK) & ((kj >= ATTN_BLK) | jnp.logical_not(first))
        lane = lax.broadcasted_iota(jnp.int32, (ATTN_BLK, LANES), 1)
        lse_tile = jnp.zeros((ATTN_BLK, LANES), F32)
        zero = jnp.zeros((), BF16)
        in_head = [(lane // ATTN_HEAD_DIM) == hh for hh in range(2)]
        for t0 in range(0, tiles, ATTN_FWD_TILE_GROUP):
            group = range(t0, t0 + ATTN_FWD_TILE_GROUP)
            heads = [(t, hh) for t in group for hh in range(2)]
            cols = {t: pl.ds(t * LANES, LANES) for t in group}
            k2 = {t: jnp.concatenate([kp_ref[:, cols[t]], kc_ref[:, cols[t]]], axis=0) for t in group}
            v2 = {t: jnp.concatenate([vp_ref[:, cols[t]], vc_ref[:, cols[t]]], axis=0) for t in group}
            sc = {(t, hh): jnp.where(valid, _dot_nt(jnp.where(in_head[hh], q_ref[:, cols[t]], zero), k2[t]),
                                     -jnp.inf) for t, hh in heads}
            mx = {i: jnp.max(sc[i], axis=-1, keepdims=True) for i in heads}
            p = {i: jnp.exp(sc[i] - mx[i]) for i in heads}
            l = {i: jnp.sum(p[i], axis=-1, keepdims=True) for i in heads}
            oh = {i: _dot(p[i].astype(BF16), v2[i[0]]) / l[i] for i in heads}
            for t in group:
                o_ref[:, cols[t]] = jnp.where(in_head[0], oh[t, 0], oh[t, 1])
                for hh in range(2):
                    lse_tile = jnp.where(lane == 2 * t + hh, mx[t, hh] + jnp.log(l[t, hh]), lse_tile)
        lse_ref[...] = lse_tile

    def blk(piece, prev):
        if prev:
            return pl.BlockSpec((None, ATTN_BLK, d), lambda i: (piece, jnp.maximum(i - 1, 0), 0))
        return pl.BlockSpec((None, ATTN_BLK, d), lambda i: (piece, i, 0))

    return pl.pallas_call(
        body, name=name, grid=(nq,),
        in_specs=[blk(0, False), blk(1, False), blk(1, True), blk(2, False), blk(2, True)],
        out_specs=[pl.BlockSpec((ATTN_BLK, d), lambda i: (i, 0)), pl.BlockSpec((ATTN_BLK, LANES), lambda i: (i, 0))],
        out_shape=[jax.ShapeDtypeStruct((s, d), F32), jax.ShapeDtypeStruct((s, LANES), F32)],
        compiler_params=_params(("parallel",)),
    )(qkv, qkv, qkv, qkv, qkv)


def _attn_combine(os_, lses):
    s, d = os_[0].shape
    sel = jnp.concatenate([_head_expand_matrix()] * 3, axis=0)

    def body(o0, o1, o2, l0, l1, l2, sel_ref, of_ref, ob_ref, lt_ref, o_pos, l_pos):
        for g, (dil, o_ref, l_ref) in enumerate(zip(DILATIONS, (o0, o1, o2), (l0, l1, l2))):
            for r in range(dil):
                o_pos[g, _class_rows(r, dil), :] = o_ref[r]
                l_pos[g, _class_rows(r, dil), :] = l_ref[r]
        la, lb_, lc = l_pos[0], l_pos[1], l_pos[2]
        mx = jnp.maximum(jnp.maximum(la, lb_), lc)
        es = (jnp.exp(la - mx), jnp.exp(lb_ - mx), jnp.exp(lc - mx))
        z = es[0] + es[1] + es[2]
        lt_ref[...] = mx + jnp.log(z)
        acc = jnp.zeros((POS_BLK, LANES), F32)
        for g in range(3):
            acc += _exact_dot_r(es[g] / z, sel_ref[...]) * o_pos[g]
        of_ref[...] = acc
        ob_ref[...] = acc.astype(BF16)

    return pl.pallas_call(
        body, name="attn_combine", grid=(s // POS_BLK, d // LANES),
        in_specs=[_class_spec(dil) for dil in DILATIONS] + [_class_spec(dil, False) for dil in DILATIONS]
        + [pl.BlockSpec((3 * LANES, LANES), lambda i, t: (0, t))],
        out_specs=[_pos_spec(), _pos_spec(), _pos_spec(False)],
        out_shape=[jax.ShapeDtypeStruct((s, d), F32), jax.ShapeDtypeStruct((s, d), BF16),
                   jax.ShapeDtypeStruct((s, LANES), F32)],
        scratch_shapes=[pltpu.VMEM((3, POS_BLK, LANES), F32), pltpu.VMEM((3, POS_BLK, LANES), F32)],
        compiler_params=_params(("parallel", "arbitrary")),
    )(*[_class_view(o, dil) for o, dil in zip(os_, DILATIONS)],
      *[_class_view(l, dil) for l, dil in zip(lses, DILATIONS)], sel)


ATTN_TILE_GROUP = 2
ATTN_FWD_TILE_GROUP = 4
AUX_PER_TILE = 12


def _aux_placement():
    h = lax.broadcasted_iota(jnp.int32, (6, LANES, LANES), 1)
    l = lax.broadcasted_iota(jnp.int32, (6, LANES, LANES), 2)
    j = lax.broadcasted_iota(jnp.int32, (6, LANES, LANES), 0)
    target = AUX_PER_TILE * (h // 2) + 3 * (h % 2) + jnp.where(j < 3, j, 3 + j)
    return ((l == target) & (h < ATTN_HEADS)).astype(BF16)


def _attn_bwd_prep(do, o, lse):
    s, d = do.shape
    tiles = d // LANES
    sel_t = jnp.tile(_head_expand_matrix().T.reshape(tiles, LANES, LANES), (1, 3, 1))

    def body(do_ref, o_ref, l_ref, sel_ref, place_ref, *refs):
        outs, delta, aux = refs[:9], refs[9], refs[10]
        t = pl.program_id(1)
        part = _exact_dot_r(do_ref[...] * o_ref[...], sel_ref[...])

        @pl.when(t == 0)
        def _():
            delta[...] = part

        @pl.when(t > 0)
        def _():
            delta[...] += part

        for g, dil in enumerate(DILATIONS):
            for r in range(dil):
                blk = do_ref[_class_rows(r, dil), :].astype(BF16)
                outs[g][r] = blk
                outs[3 + g][r] = blk.T

        @pl.when(t == tiles - 1)
        def _():
            pieces = _split3(l_ref[...]) + _split3(delta[...])
            aux[...] = _dot(jnp.concatenate(pieces, axis=1), place_ref[...])
            for g, dil in enumerate(DILATIONS):
                for r in range(dil):
                    outs[6 + g][r] = aux[_class_rows(r, dil), :].astype(BF16)

    outs = pl.pallas_call(
        body, name="attn_bwd_prep", grid=(s // POS_BLK, tiles),
        in_specs=[_pos_spec(), _pos_spec(), _pos_spec(False),
                  pl.BlockSpec((None, 3 * LANES, LANES), lambda i, t: (t, 0, 0)),
                  pl.BlockSpec((6 * LANES, LANES), lambda i, t: (0, 0))],
        out_specs=[_class_spec(dil) for dil in DILATIONS]
        + [pl.BlockSpec((dil, LANES, POS_BLK // dil), lambda i, t: (0, t, i)) for dil in DILATIONS]
        + [_class_spec(dil, False) for dil in DILATIONS],
        out_shape=[jax.ShapeDtypeStruct((dil, s // dil, d), BF16) for dil in DILATIONS]
        + [jax.ShapeDtypeStruct((dil, d, s // dil), BF16) for dil in DILATIONS]
        + [jax.ShapeDtypeStruct((dil, s // dil, LANES), BF16) for dil in DILATIONS],
        scratch_shapes=[pltpu.VMEM((POS_BLK, LANES), F32), pltpu.VMEM((POS_BLK, LANES), F32)],
        compiler_params=_params(("parallel", "arbitrary")),
    )(do, o, lse, sel_t, _aux_placement().reshape(6 * LANES, LANES))
    return ([a.reshape(s, d) for a in outs[0:3]], list(outs[3:6]), [a.reshape(s, LANES) for a in outs[6:9]])


def _attn_bwd(qkv, qkv_t, do, do_t, aux, tables, dil, *, name):
    _, s, d = qkv.shape
    nq = s // ATTN_BLK
    per = nq // dil
    tiles = d // LANES
    half = ATTN_HEAD_DIM

    def body(qd_ref, qo_ref, k_ref, qtd_ref, qto_ref, kt_ref, vt_ref, dod_ref, doo_ref, dotd_ref, doto_ref,
             auxd_ref, auxo_ref, c_ref, ss_ref, out_ref, carry):
        kb = pl.program_id(0)

        @pl.when(kb == 0)
        def _():
            carry[...] = jnp.zeros_like(carry)

        has_next = (kb % per) != (per - 1)
        qi = lax.broadcasted_iota(jnp.int32, (ATTN_BLK, 2 * ATTN_BLK), 0)
        kj = lax.broadcasted_iota(jnp.int32, (ATTN_BLK, 2 * ATTN_BLK), 1) % ATTN_BLK
        valid_d = kj <= qi
        valid_o = (kj >= qi) & has_next
        lane = lax.broadcasted_iota(jnp.int32, (ATTN_BLK, LANES), 1)
        row = lax.broadcasted_iota(jnp.int32, (LANES, ATTN_BLK), 0)
        side = lax.broadcasted_iota(jnp.int32, (LANES, 2 * ATTN_BLK), 0)
        first = lax.broadcasted_iota(jnp.int32, (LANES, 2 * ATTN_BLK), 1) < ATTN_BLK
        c, ss = c_ref[...], ss_ref[...]
        zero = jnp.zeros((), BF16)
        sides = ((qd_ref, qtd_ref, dod_ref, dotd_ref, auxd_ref[...], valid_d),
                 (qo_ref, qto_ref, doo_ref, doto_ref, auxo_ref[...], valid_o))
        both = (0, 1)

        def head_halves(x, index):
            axis = 0 if index is lane else 1
            return jnp.concatenate([jnp.where(index < half, x, zero), jnp.where(index >= half, x, zero)], axis=axis)

        for t0 in range(0, tiles, ATTN_TILE_GROUP):
            group = range(t0, t0 + ATTN_TILE_GROUP)
            cols = {t: pl.ds(t * LANES, LANES) for t in group}
            kk, kk_t, vv_t = {}, {}, {}
            for t in group:
                base = AUX_PER_TILE * t
                hit = lambda lo: ((first & (side >= base + lo) & (side < base + lo + 3))
                                  | (jnp.logical_not(first) & (side >= base + lo + 3) & (side < base + lo + 6)))
                kk[t] = head_halves(k_ref[:, cols[t]], lane)
                kk_t[t] = jnp.concatenate([head_halves(kt_ref[cols[t], :], row),
                                           jnp.where(hit(0), -1.0, 0.0).astype(BF16)], axis=0)
                vv_t[t] = jnp.concatenate([head_halves(vt_ref[cols[t], :], row),
                                           jnp.where(hit(6), -1.0, 0.0).astype(BF16)], axis=0)
            sc = {(t, w): _dot(jnp.concatenate([sides[w][0][:, cols[t]], sides[w][4]], axis=1), kk_t[t])
                  for t in group for w in both}
            dpd = {(t, w): _dot(jnp.concatenate([sides[w][2][:, cols[t]], sides[w][4]], axis=1), vv_t[t])
                   for t in group for w in both}
            p = {i: jnp.where(sides[i[1]][5], jnp.exp(sc[i]), 0.0) for i in sc}
            ds = {i: (p[i] * dpd[i]).astype(BF16) for i in sc}
            pb = {i: p[i].astype(BF16) for i in sc}
            dv_t = {t: sum(_dot(sides[w][3][cols[t], :], pb[t, w]) for w in both) for t in group}
            dk_t = {t: sum(_dot(sides[w][1][cols[t], :], ds[t, w]) for w in both) for t in group}
            dq = {i: _dot(ds[i], kk[i[0]]) * ATTN_SCALE for i in sc}
            for t in group:
                dq_now = carry[:, cols[t]] + dq[t, 0]
                carry[:, cols[t]] = dq[t, 1]
                dk = jnp.where(row < half, dk_t[t][:, :ATTN_BLK], dk_t[t][:, ATTN_BLK:]).T
                dv = jnp.where(row < half, dv_t[t][:, :ATTN_BLK], dv_t[t][:, ATTN_BLK:]).T
                out_ref[0, :, cols[t]] = _rotate(dq_now, c, ss, -1.0).astype(BF16)
                out_ref[1, :, cols[t]] = _rotate(dk, c, ss, -1.0).astype(BF16)
                out_ref[2, :, cols[t]] = dv.astype(BF16)

    def nxt(i):
        return jnp.minimum(i + 1, nq - 1)

    def piece(p, shift):
        if shift:
            return pl.BlockSpec((None, ATTN_BLK, d), lambda i: (p, nxt(i), 0))
        return pl.BlockSpec((None, ATTN_BLK, d), lambda i: (p, i, 0))

    def piece_t(p, shift):
        if shift:
            return pl.BlockSpec((None, d, ATTN_BLK), lambda i: (p, 0, nxt(i)))
        return pl.BlockSpec((None, d, ATTN_BLK), lambda i: (p, 0, i))

    def rows(width, shift):
        if shift:
            return pl.BlockSpec((ATTN_BLK, width), lambda i: (nxt(i), 0))
        return pl.BlockSpec((ATTN_BLK, width), lambda i: (i, 0))

    def do_t_spec(shift):
        if shift:
            return pl.BlockSpec((None, d, ATTN_BLK), lambda i: (nxt(i) // per, 0, nxt(i) % per))
        return pl.BlockSpec((None, d, ATTN_BLK), lambda i: (i // per, 0, i % per))

    return pl.pallas_call(
        body, name=name, grid=(nq,),
        in_specs=[piece(0, False), piece(0, True), piece(1, False),
                  piece_t(0, False), piece_t(0, True), piece_t(1, False), piece_t(2, False),
                  rows(d, False), rows(d, True), do_t_spec(False), do_t_spec(True),
                  rows(LANES, False), rows(LANES, True), rows(LANES, False), rows(LANES, False)],
        out_specs=pl.BlockSpec((3, ATTN_BLK, d), lambda i: (0, i, 0)),
        out_shape=jax.ShapeDtypeStruct((3, s, d), BF16),
        scratch_shapes=[pltpu.VMEM((ATTN_BLK, d), F32)],
        compiler_params=_params(("arbitrary",)),
    )(qkv, qkv, qkv, qkv_t, qkv_t, qkv_t, qkv_t, do, do, do_t, do_t, aux, aux, *tables)


def _dx_combine(du, parts):
    s, d = du.shape

    def body(du_ref, p0, p1, p2, out_ref):
        out_ref[...] = ALPHA * du_ref[...] + p0[0]
        for dil, p_ref in zip(DILATIONS[1:], (p1, p2)):
            for r in range(dil):
                out_ref[_class_rows(r, dil), :] += p_ref[r]

    return pl.pallas_call(
        body, name="dx_combine", grid=(s // POS_BLK, d // LANES),
        in_specs=[_pos_spec()] + [_class_spec(dil) for dil in DILATIONS], out_specs=_pos_spec(),
        out_shape=jax.ShapeDtypeStruct((s, d), F32),
        compiler_params=_params(("parallel", "parallel")),
    )(du, *[_class_view(p, dil) for p, dil in zip(parts, DILATIONS)])


HGRN_ROWS = 2048


def _tri(lower, copies=1):
    i = lax.broadcasted_iota(jnp.int32, (HGRN_CHUNK, HGRN_CHUNK * copies), 0)
    j = lax.broadcasted_iota(jnp.int32, (HGRN_CHUNK, HGRN_CHUNK * copies), 1) % HGRN_CHUNK
    return (j <= i) if lower else (j >= i)


def _hgrn_gates(qr, z, lb):
    sq = jax.nn.sigmoid(qr)
    e = jnp.exp(-jnp.abs(z))
    big = 1.0 / (1.0 + e)
    small = e * big
    sg = jnp.where(z >= 0, big, small)
    sn = jnp.where(z >= 0, small, big)
    return sq, qr * sq, sg, sn, lb + (1.0 - lb) * sg, (1.0 - lb) * sn


def _hgrn_fwd(p2, lb, norm_g):
    _, s, d = p2.shape
    nblk = s // HGRN_ROWS
    cps = HGRN_ROWS // HGRN_CHUNK

    def body(p_ref, lb_ref, g_ref, o_ref, on_ref, st_ref, state):
        @pl.when(pl.program_id(1) == 0)
        def _():
            state[...] = jnp.zeros_like(state)

        chunks = [pl.ds(c * HGRN_CHUNK, HGRN_CHUNK) for c in range(cps)]
        ltri = _tri(True)
        lsel = _tri(True, 3).astype(BF16)
        _, q, _, _, f, key = _hgrn_gates(p_ref[0], p_ref[1], lb_ref[...])
        lf = jnp.log(f)
        v = [p_ref[2, rows, :].astype(BF16) for rows in chunks]
        b = [_exact_dot(lsel, lf[c * HGRN_CHUNK:(c + 1) * HGRN_CHUNK]) for c in range(cps)]
        b_last = [bc[HGRN_CHUNK - 1:HGRN_CHUNK, :] for bc in b]
        q = [q[c * HGRN_CHUNK:(c + 1) * HGRN_CHUNK] for c in range(cps)]
        key = [key[c * HGRN_CHUNK:(c + 1) * HGRN_CHUNK] for c in range(cps)]
        qd = [(q[c] * jnp.exp(b[c])).astype(BF16) for c in range(cps)]
        kd = [(key[c] * jnp.exp(-b[c])).astype(BF16) for c in range(cps)]
        k2 = [(key[c] * jnp.exp(b_last[c] - b[c])).astype(BF16) for c in range(cps)]
        a = [jnp.where(ltri, _dot_nt(qd[c], kd[c]), 0.0).astype(BF16) for c in range(cps)]
        kv = [_dot_tn(v[c], k2[c]) for c in range(cps)]
        st, sts = state[...], []
        for c in range(cps):
            sts.append(st)
            st_ref[0, c] = st
            st = st * jnp.exp(b_last[c]) + kv[c]
        state[...] = st
        gv = g_ref[...]
        for c in range(cps):
            o = _dot(a[c], v[c]) + _dot_nt(qd[c], sts[c].astype(BF16))
            o_ref[chunks[c], :] = o
            r = lax.rsqrt(jnp.mean(o * o, axis=-1, keepdims=True) + RMS_EPS)
            on_ref[chunks[c], :] = (o * r * gv).astype(BF16)

    vec = pl.BlockSpec((1, LANES), lambda h, c: (0, h))
    col = pl.BlockSpec((HGRN_ROWS, LANES), lambda h, c: (c, h))
    return pl.pallas_call(
        body, name="hgrn_fwd", grid=(HGRN_HEADS, nblk),
        in_specs=[pl.BlockSpec((3, HGRN_ROWS, LANES), lambda h, c: (0, c, h)), vec, vec],
        out_specs=[col, col, pl.BlockSpec((1, cps, LANES, LANES), lambda h, c: (h, c, 0, 0))],
        out_shape=[jax.ShapeDtypeStruct((s, d), F32), jax.ShapeDtypeStruct((s, d), BF16),
                   jax.ShapeDtypeStruct((HGRN_HEADS, s // HGRN_CHUNK, LANES, LANES), F32)],
        scratch_shapes=[pltpu.VMEM((LANES, LANES), F32)],
        compiler_params=_params(("parallel", "arbitrary")),
    )(p2, lb, norm_g)


def _hgrn_bwd(p2, lb, norm_g, o_raw, states, dyn):
    _, s, d = p2.shape
    nblk = s // HGRN_ROWS
    cps = HGRN_ROWS // HGRN_CHUNK

    def body(p_ref, lb_ref, g_ref, o_ref, st_ref, dy_ref, dp_ref, dg_ref, dlb_ref, dstate):
        @pl.when(pl.program_id(1) == 0)
        def _():
            dstate[...] = jnp.zeros_like(dstate)
            dg_ref[...] = jnp.zeros_like(dg_ref)
            dlb_ref[...] = jnp.zeros_like(dlb_ref)

        n = cps
        cut = lambda t: [t[c * HGRN_CHUNK:(c + 1) * HGRN_CHUNK] for c in range(n)]
        chunks = [pl.ds(c * HGRN_CHUNK, HGRN_CHUNK) for c in range(n)]
        lbv = lb_ref[...]
        ltri = _tri(True)
        lsel = _tri(True, 3).astype(BF16)
        usel = _tri(False, 3).astype(BF16)
        last_row = lax.broadcasted_iota(jnp.int32, (HGRN_CHUNK, LANES), 0) == HGRN_CHUNK - 1
        qr, z = p_ref[0], p_ref[1]
        sq, q, sg, sn, f, key = _hgrn_gates(qr, z, lbv)
        lf = jnp.log(f)
        v = [p_ref[2, rows, :].astype(BF16) for rows in chunks]
        o, dyv = o_ref[...], dy_ref[...]
        r = lax.rsqrt(jnp.mean(o * o, axis=-1, keepdims=True) + RMS_EPS)
        oh = o * r
        doh = dyv * g_ref[...]
        do = cut((r * (doh - oh * jnp.mean(doh * oh, axis=-1, keepdims=True))).astype(BF16))
        dg_ref[...] += jnp.sum(dyv * oh, axis=0, keepdims=True)
        b = [_exact_dot(lsel, lfc) for lfc in cut(lf)]
        b_last = [bc[HGRN_CHUNK - 1:HGRN_CHUNK, :] for bc in b]
        q, key = cut(q), cut(key)
        eb = [jnp.exp(bc) for bc in b]
        enb = [jnp.exp(-bc) for bc in b]
        e2 = [jnp.exp(b_last[c] - b[c]) for c in range(n)]
        dec = [jnp.exp(bl) for bl in b_last]
        qd_f = [q[c] * eb[c] for c in range(n)]
        kd_f = [key[c] * enb[c] for c in range(n)]
        k2_f = [key[c] * e2[c] for c in range(n)]
        qd, kd, k2 = ([t.astype(BF16) for t in ts] for ts in (qd_f, kd_f, k2_f))
        a = [jnp.where(ltri, _dot_nt(qd[c], kd[c]), 0.0).astype(BF16) for c in range(n)]
        da = [jnp.where(ltri, _dot_nt(do[c], v[c]), 0.0).astype(BF16) for c in range(n)]
        grow = [_dot_tn(do[c], qd[c]) for c in range(n)]
        dst, dsts = dstate[...], [None] * n
        for c in reversed(range(n)):
            dsts[c] = dst
            dst = dst * dec[c] + grow[c]
        dstate[...] = dst
        st = [st_ref[0, c] for c in range(n)]
        dstb = [t.astype(BF16) for t in dsts]
        dv = [_dot_tn(a[c], do[c]) + _dot_nt(k2[c], dstb[c]) for c in range(n)]
        dqd = [_dot(da[c], kd[c]) + _dot(do[c], st[c].astype(BF16)) for c in range(n)]
        dkd = [_dot_tn(da[c], qd[c]) for c in range(n)]
        dk2 = [_dot(v[c], dstb[c]) for c in range(n)]
        db = []
        for c in range(n):
            ddec = jnp.sum(dsts[c] * st[c], axis=0, keepdims=True)
            db_last = jnp.sum(dk2[c] * k2_f[c], axis=0, keepdims=True) + ddec * dec[c]
            db.append(dqd[c] * qd_f[c] - dkd[c] * kd_f[c] - dk2[c] * k2_f[c] + jnp.where(last_row, db_last, 0.0))
        dlf = [_exact_dot(usel, dbc) for dbc in db]
        f, sg, sn, sq, qr = cut(f), cut(sg), cut(sn), cut(sq), cut(qr)
        dlb_acc = jnp.zeros((1, LANES), F32)
        for c in range(n):
            dkey = dkd[c] * enb[c] + dk2[c] * e2[c]
            common = dlf[c] / f[c] - dkey
            dlb_acc += jnp.sum(common * sn[c], axis=0, keepdims=True)
            dp_ref[0, chunks[c], :] = (dqd[c] * eb[c] * (sq[c] * (1.0 + qr[c] * (1.0 - sq[c])))).astype(BF16)
            dp_ref[1, chunks[c], :] = (common * ((1.0 - lbv) * sg[c] * sn[c])).astype(BF16)
            dp_ref[2, chunks[c], :] = dv[c].astype(BF16)
        dlb_ref[...] += dlb_acc

    def rev(c):
        return nblk - 1 - c

    vec = pl.BlockSpec((1, LANES), lambda h, c: (0, h))
    col = pl.BlockSpec((HGRN_ROWS, LANES), lambda h, c: (rev(c), h))
    p3 = pl.BlockSpec((3, HGRN_ROWS, LANES), lambda h, c: (0, rev(c), h))
    return pl.pallas_call(
        body, name="hgrn_bwd", grid=(HGRN_HEADS, nblk),
        in_specs=[p3, vec, vec, col, pl.BlockSpec((1, cps, LANES, LANES), lambda h, c: (h, rev(c), 0, 0)), col],
        out_specs=[p3, vec, vec],
        out_shape=[jax.ShapeDtypeStruct((3, s, d), BF16), jax.ShapeDtypeStruct((1, d), F32),
                   jax.ShapeDtypeStruct((1, d), F32)],
        scratch_shapes=[pltpu.VMEM((LANES, LANES), F32)],
        compiler_params=_params(("parallel", "arbitrary")),
    )(p2, lb, norm_g, o_raw, states, dyn)


SMALL_ROWS = 16
FUSED_TM = 512
FUSED_ROWS = 16


def _relu2(h):
    r = jnp.maximum(h.astype(F32), 0.0)
    return r * r


def _dact_epi(acc, h):
    return (acc * (2.0 * jnp.maximum(h.astype(F32), 0.0)),)


def _rope_epi(j, acc, ct, sst):
    acc_t = acc.T
    rot_t = _rotate_t(acc_t, ct, sst) * jnp.where(j == 0, ATTN_SCALE, 1.0)
    out_t = jnp.where(j >= 2, acc_t, rot_t).astype(BF16)
    return out_t.T, out_t


def _local_step(x, target, get_w, small, on_grads, deps=()):
    s, d = x.shape
    lb = small["lb"]
    gq = 3 * d
    gt = gq // 1024

    def ln_vecs(which, layer):
        return small[f"ln_{which}_g"][layer:layer + 1], small[f"ln_{which}_b"][layer:layer + 1]

    def ln_of(a, w, x_in, which, layer, *, name, k, a_pre=None, deps=(), prev=None):
        vecs = ln_vecs(which, layer) + (ln_vecs(*prev) if prev else ())
        return _mm(a, w, "nn", name=name, m=s, n=d, k=k, tm=FUSED_TM, tk=k, a_pre=a_pre, out_dtypes=(BF16, F32),
                   row_outs=((F32, 1),), epi=_ln_epi, tile_extras=(x_in,), vec_extras=vecs, deps=deps,
                   epi_rows=FUSED_ROWS)

    def ln_grad_of(a, w, du_next, xh, rstd, which, layer, *, name, k, a_split=1, deps=()):
        return _mm(a, w, "nt", name=name, m=s, n=d, k=k, tm=FUSED_TM, tk=k, a_split=a_split,
                   out_dtypes=(F32, BF16), vec_outs=2, epi=_ln_grad_epi, tile_extras=(du_next, xh), row_extras=(rstd,),
                   vec_extras=(small[f"ln_{which}_g"][layer:layer + 1],), deps=deps, epi_rows=FUSED_ROWS)

    def ffn_fwd(xb, xh_in, w_up, w_down, tag):
        h = _mm(xb, w_up, "nn", name=f"ffn_up_{tag}", m=s, n=D_FF, k=d, tm=2048, out_dtypes=(BF16,))
        return (h,) + tuple(ln_of(h, w_down, xh_in, "ffn", tag, name=f"ffn_down_{tag}", k=D_FF, a_pre=_relu2,
                                  prev=("mix", tag)))

    def ffn_bwd(du, dub, xb, h, w_up, w_down, xh, rstd, tag):
        dh = _mm(dub, w_down, "nt", name=f"ffn_dact_{tag}", m=s, n=D_FF, k=d, tm=2048, out_dtypes=(BF16,), epi=_dact_epi,
                 tile_extras=(h,))
        g_down = _mm(h, dub, "tn", name=f"ffn_gdown_{tag}", m=D_FF, n=d, k=s, tm=512, tk=s, a_pre=_relu2,
                     out_dtypes=(BF16,))
        g_up = _mm(xb, dh, "tn", name=f"ffn_gup_{tag}", m=d, n=D_FF, k=s, tk=s, out_dtypes=(BF16,))
        after = on_grads(f"ffn{tag}", {f"ffn_w_down{tag}": g_down, f"ffn_w_up{tag}": g_up})
        return ln_grad_of(dh, w_up, du, xh, rstd, "mix", tag, name=f"ffn_dx_{tag}", k=D_FF, deps=after)

    xs = _prep_x(x, deps)
    tabs, tabs_t = zip(*[_rope_tables(s, dil) for dil in DILATIONS])
    w_ain, after = get_w("attn_w_in", xs + [t for tab in tabs + tabs_t for t in tab])
    qkvs, qkv_ts, o_parts, lse_parts = [], [], [], []
    for g, dil in enumerate(DILATIONS):
        qkv, qkv_t = _mm(xs[g], w_ain, "nn", name=f"attn_in_{g}", m=s, n=gq, k=d, b_col_off=g * gt, out_split=3,
                         out_dtypes=(BF16,), epi=_rope_epi, epi_wants_j=True, row_extras=tabs_t[g], t_out=True,
                         deps=after if g == 0 else ())
        qkvs.append(qkv)
        qkv_ts.append(qkv_t)
        o_g, lse_g = _attn_fwd(qkv, dil, name=f"attn_fwd_{g}")
        o_parts.append(o_g)
        lse_parts.append(lse_g)
    o_f, o_b, lse_t = _attn_combine(o_parts, lse_parts)
    w_aout, after = get_w("attn_w_out", o_b)
    x1b, xh1, r1 = ln_of(o_b, w_aout, x, "mix", 0, name="attn_out", k=d, deps=after)
    w_up0, w_down0 = get_w("ffn_w_up0", o_b)[0], get_w("ffn_w_down0", o_b)[0]
    h0, x2b, xh2, r2 = ffn_fwd(x1b, xh1, w_up0, w_down0, 0)
    w_hin, w_hout, norm_g = get_w("hgrn_w_in", x2b)[0], get_w("hgrn_w_out", x2b)[0], get_w("hgrn_norm_g", x2b)[0]
    p2 = _mm(x2b, w_hin, "nn", name="hgrn_in", m=s, n=3 * d, k=d, tm=2048, out_split=3)
    o_raw, o_n, states = _hgrn_fwd(p2, lb, norm_g)
    w_up1, w_down1 = get_w("ffn_w_up1", o_n)[0], get_w("ffn_w_down1", o_n)[0]
    x3b, xh3, r3 = ln_of(o_n, w_hout, xh2, "mix", 1, name="hgrn_out", k=d, prev=("ffn", 0))
    h1 = _mm(x3b, w_up1, "nn", name="ffn_up_1", m=s, n=D_FF, k=d, tm=2048, out_dtypes=(BF16,))
    du, dub, dg_ffn1, db_ffn1, sq = _mm(
        h1, w_down1, "nn", name="ffn_down_1", m=s, n=d, k=D_FF, tm=FUSED_TM, tk=D_FF, a_pre=_relu2,
        out_dtypes=(F32, BF16), vec_outs=3, epi=_ln_loss_epi, tile_extras=(xh3, target),
        vec_extras=ln_vecs("ffn", 1) + ln_vecs("mix", 1), epi_rows=FUSED_ROWS)
    du, dub, dg_mix1, db_mix1 = ffn_bwd(du, dub, x3b, h1, w_up1, w_down1, xh3, r3, 1)
    dyn = _mm(dub, w_hout, "nt", name="hgrn_dout", m=s, n=d, k=d)
    g_hout = _mm(o_n, dub, "tn", name="hgrn_gout", m=d, n=d, k=s, out_dtypes=(BF16,))
    dp2, d_norm_g, d_lb = _hgrn_bwd(p2, lb, norm_g, o_raw, states, dyn)
    g_hin = _mm(x2b, dp2, "tn", name="hgrn_gin", m=d, n=3 * d, k=s, tk=s, b_split=3, out_dtypes=(BF16,))
    after = on_grads("hgrn", {"hgrn_w_out": g_hout, "hgrn_w_in": g_hin})
    du, dub, dg_ffn0, db_ffn0 = ln_grad_of(dp2, w_hin, du, xh2, r2, "ffn", 0, name="hgrn_dx", k=3 * d, a_split=3,
                                           deps=after)
    du, dub, dg_mix0, db_mix0 = ffn_bwd(du, dub, x1b, h0, w_up0, w_down0, xh1, r1, 0)
    g_aout = _mm(o_b, dub, "tn", name="attn_gout", m=d, n=d, k=s, out_dtypes=(BF16,))
    sm1 = jax.nn.softmax(small["lb_logits"], axis=0)
    d_l1 = d_lb * (sm1[0:1] * sm1[1:2])
    zeros = jnp.zeros((SMALL_ROWS - 12, d), F32)
    loss_row = jnp.broadcast_to(0.5 * jnp.sum(sq) / d, (1, d))
    small_grads = jnp.concatenate([dg_mix0, dg_mix1, db_mix0, db_mix1, dg_ffn0, dg_ffn1, db_ffn0, db_ffn1,
                                   -d_l1, d_l1, d_norm_g, loss_row, zeros], axis=0)
    after = on_grads("attn_out", {"attn_w_out": g_aout, "small": small_grads})
    do = _mm(dub, w_aout, "nt", name="attn_dout", m=s, n=d, k=d, deps=after)
    do_parts, do_ts, aux_parts = _attn_bwd_prep(do, o_f, lse_t)
    g_ain, dqkvs = None, []
    for g, dil in enumerate(DILATIONS):
        dqkvs.append(_attn_bwd(qkvs[g], qkv_ts[g], do_parts[g], do_ts[g], aux_parts[g], tabs[g], dil,
                               name=f"attn_bwd_{g}"))
        g_ain = _mm(xs[g], dqkvs[g], "tn", name=f"attn_gin_{g}", m=d, n=gq, k=s, tk=s, b_split=3, out_dtypes=(BF16,),
                    out_col_off=g * gt, out_cols=3 * gq, alias=g_ain)
    after = on_grads("attn_in", {"attn_w_in": g_ain})
    dx_parts = [_mm(dqkvs[g], w_ain, "nt", name=f"attn_dx_{g}", m=s, n=d, k=gq, tm=FUSED_TM, tk=gq, a_split=3, b_k_off=g,
                    deps=after if g == 0 else ())
                for g in range(len(DILATIONS))]
    return _dx_combine(du, dx_parts)


def _mesh_place():
    x, y, c = lax.axis_index("x"), lax.axis_index("y"), lax.axis_index("c")
    return x, y, c, 4 * x + 2 * y + c


def _peer(x, y, c, k):
    px = 1 - x if (k >> 2) & 1 else x
    py = 1 - y if (k >> 1) & 1 else y
    pc = 1 - c if k & 1 else c
    return (px, py, pc), 4 * px + 2 * py + pc


def _window(ref, axis, size, idx):
    if axis is None:
        return ref
    sl = [slice(None)] * len(ref.shape)
    sl[axis] = pl.ds(idx * size, size)
    return ref.at[tuple(sl)]


_HBM = pl.BlockSpec(memory_space=pltpu.HBM)
_SEM = pl.BlockSpec(memory_space=pltpu.SEMAPHORE)
_EFFECT = pltpu.SideEffectType.DATAFLOW_SIDE_EFFECTING


def _xchg_ends(src_ref, land_ref, axis, gather, me, other):
    if gather:
        size = src_ref.shape[axis]
        return src_ref, _window(land_ref, axis, size, me), _window(land_ref, axis, size, other)
    size = None if axis is None else src_ref.shape[axis] // N_DEV
    return _window(src_ref, axis, size, other), land_ref.at[me], land_ref.at[other]


def _xchg_start(srcs, lands, axes, *, gather, name, deps=(), peers=tuple(range(1, N_DEV))):
    n = len(srcs)
    nd = len(deps)

    def body(*refs):
        src_refs, land_refs = refs[:n], refs[n:2 * n]
        send, recv = refs[2 * n + nd:3 * n + nd], refs[3 * n + nd:4 * n + nd]
        token = refs[-1]
        x, y, c, me = _mesh_place()
        for k in peers:
            peer, pidx = _peer(x, y, c, k)
            for i in range(n):
                src, dst, _ = _xchg_ends(src_refs[i], land_refs[i], axes[i], gather, me, pidx)
                pltpu.make_async_remote_copy(
                    src_ref=src, dst_ref=dst, send_sem=send[i].at[k - 1], recv_sem=recv[i].at[k - 1],
                    device_id=peer, device_id_type=pl.DeviceIdType.MESH).start()
        for i in range(n):
            src, dst, _ = _xchg_ends(src_refs[i], land_refs[i], axes[i], gather, me, me)
            pltpu.make_async_copy(src, dst, send[i].at[N_DEV - 1]).start()
        token[...] = jnp.zeros_like(token)

    bufs = list(srcs) + list(lands)
    outs = pl.pallas_call(
        body, name=name,
        out_shape=[pltpu.SemaphoreType.DMA((N_DEV,))] * (2 * n) + [pltpu.HBM(b.shape, b.dtype) for b in bufs]
        + [jax.ShapeDtypeStruct((8, LANES), F32)],
        in_specs=[_HBM] * (2 * n) + [pl.BlockSpec(memory_space=pl.ANY)] * nd,
        out_specs=[_SEM] * (2 * n) + [_HBM] * (2 * n) + [pl.BlockSpec(memory_space=pltpu.VMEM)],
        input_output_aliases={i: 2 * n + i for i in range(2 * n)},
        compiler_params=pltpu.CompilerParams(has_side_effects=_EFFECT),
    )(*[pltpu.with_memory_space_constraint(b, pltpu.HBM) for b in bufs], *deps)
    return dict(send=outs[:n], recv=outs[n:2 * n], srcs=outs[2 * n:3 * n], lands=outs[3 * n:4 * n], token=outs[-1],
                axes=list(axes), gather=gather)


SIBLING = 1
SAME_CORE = (2, 4, 6)


def _gather_relay(xc, after, *, name):
    axis = xc["axes"][0]

    def body(src_ref, land_ref, send1, recv1, *rest):
        send2, recv2 = rest[-2:]
        x, y, c, me = _mesh_place()
        sibling, _ = _peer(x, y, c, SIBLING)
        for j, k in enumerate(SAME_CORE):
            peer, pidx = _peer(x, y, c, k)
            src, _, got = _xchg_ends(src_ref, land_ref, axis, True, me, pidx)
            pltpu.make_async_remote_copy(
                src_ref=src, dst_ref=got, send_sem=send1.at[k - 1], recv_sem=recv1.at[k - 1],
                device_id=peer, device_id_type=pl.DeviceIdType.MESH).wait_recv()
            pltpu.make_async_remote_copy(
                src_ref=got, dst_ref=got, send_sem=send2.at[j], recv_sem=recv2.at[j],
                device_id=sibling, device_id_type=pl.DeviceIdType.MESH).start()

    src, land = xc["srcs"][0], xc["lands"][0]
    outs = pl.pallas_call(
        body, name=name,
        out_shape=[pltpu.HBM(src.shape, src.dtype), pltpu.HBM(land.shape, land.dtype)]
        + [pltpu.SemaphoreType.DMA((len(SAME_CORE),))] * 2,
        in_specs=[_HBM, _HBM, _SEM, _SEM] + [pl.BlockSpec(memory_space=pl.ANY)] * len(after),
        out_specs=[_HBM, _HBM, _SEM, _SEM], input_output_aliases={0: 0, 1: 1},
        compiler_params=pltpu.CompilerParams(has_side_effects=_EFFECT),
    )(src, land, xc["send"][0], xc["recv"][0], *after)
    return dict(src=outs[0], land=outs[1], send=outs[2], recv=outs[3])


def _gather_relay_wait(xc, relay, after, *, name):
    axis = xc["axes"][0]

    def body(src_ref, land_ref, send1, recv1, send2, recv2, after_ref, src_out, land_out):
        x, y, c, me = _mesh_place()
        sibling, sidx = _peer(x, y, c, SIBLING)
        for k in (SIBLING,) + SAME_CORE:
            peer, pidx = _peer(x, y, c, k)
            src, dst, got = _xchg_ends(src_ref, land_ref, axis, True, me, pidx)
            pltpu.make_async_remote_copy(
                src_ref=src, dst_ref=dst, send_sem=send1.at[k - 1], recv_sem=recv1.at[k - 1],
                device_id=peer, device_id_type=pl.DeviceIdType.MESH).wait_send()
        src, dst, got = _xchg_ends(src_ref, land_ref, axis, True, me, sidx)
        pltpu.make_async_remote_copy(
            src_ref=src, dst_ref=got, send_sem=send1.at[SIBLING - 1], recv_sem=recv1.at[SIBLING - 1],
            device_id=sibling, device_id_type=pl.DeviceIdType.MESH).wait_recv()
        src, dst, _ = _xchg_ends(src_ref, land_ref, axis, True, me, me)
        pltpu.make_async_copy(src, dst, send1.at[N_DEV - 1]).wait()
        for j, k in enumerate(SAME_CORE):
            _, pidx = _peer(x, y, c, k)
            _, qidx = _peer(x, y, c, k ^ SIBLING)
            _, _, sent = _xchg_ends(src_ref, land_ref, axis, True, me, pidx)
            _, _, got = _xchg_ends(src_ref, land_ref, axis, True, me, qidx)
            pltpu.make_async_remote_copy(
                src_ref=sent, dst_ref=sent, send_sem=send2.at[j], recv_sem=recv2.at[j],
                device_id=sibling, device_id_type=pl.DeviceIdType.MESH).wait_send()
            pltpu.make_async_remote_copy(
                src_ref=got, dst_ref=got, send_sem=send2.at[j], recv_sem=recv2.at[j],
                device_id=sibling, device_id_type=pl.DeviceIdType.MESH).wait_recv()

    outs = pl.pallas_call(
        body, name=name, out_shape=[pltpu.HBM(relay["src"].shape, relay["src"].dtype),
                                    pltpu.HBM(relay["land"].shape, relay["land"].dtype)],
        in_specs=[_HBM, _HBM, _SEM, _SEM, _SEM, _SEM, pl.BlockSpec(memory_space=pl.ANY)],
        out_specs=[_HBM, _HBM], input_output_aliases={0: 0, 1: 1},
        compiler_params=pltpu.CompilerParams(has_side_effects=_EFFECT),
    )(relay["src"], relay["land"], xc["send"][0], xc["recv"][0], relay["send"], relay["recv"], after)
    return outs[1]


def _xchg_wait(xc, items, after, *, name):
    m = len(items)
    gather = xc["gather"]
    axes = [xc["axes"][i] for i in items]

    def body(*refs):
        src_refs, land_refs = refs[:m], refs[m:2 * m]
        send, recv = refs[2 * m:3 * m], refs[3 * m:4 * m]
        x, y, c, me = _mesh_place()
        for k in range(1, N_DEV):
            peer, pidx = _peer(x, y, c, k)
            for j in range(m):
                src, dst, got = _xchg_ends(src_refs[j], land_refs[j], axes[j], gather, me, pidx)
                pltpu.make_async_remote_copy(
                    src_ref=src, dst_ref=dst, send_sem=send[j].at[k - 1], recv_sem=recv[j].at[k - 1],
                    device_id=peer, device_id_type=pl.DeviceIdType.MESH).wait_send()
                pltpu.make_async_remote_copy(
                    src_ref=src, dst_ref=got, send_sem=send[j].at[k - 1], recv_sem=recv[j].at[k - 1],
                    device_id=peer, device_id_type=pl.DeviceIdType.MESH).wait_recv()
        for j in range(m):
            src, dst, _ = _xchg_ends(src_refs[j], land_refs[j], axes[j], gather, me, me)
            pltpu.make_async_copy(src, dst, send[j].at[N_DEV - 1]).wait()

    bufs = [xc["srcs"][i] for i in items] + [xc["lands"][i] for i in items]
    sems = [xc["send"][i] for i in items] + [xc["recv"][i] for i in items]
    outs = pl.pallas_call(
        body, name=name, out_shape=[pltpu.HBM(b.shape, b.dtype) for b in bufs],
        in_specs=[_HBM] * (2 * m) + [_SEM] * (2 * m) + [pl.BlockSpec(memory_space=pl.ANY)] * len(after),
        out_specs=[_HBM] * (2 * m), input_output_aliases={j: j for j in range(2 * m)},
        compiler_params=pltpu.CompilerParams(has_side_effects=_EFFECT),
    )(*bufs, *sems, *after)
    return outs[m:]


def _cast_bf16(a, *, name):
    r, c = a.shape
    tr = min(r, 512)

    def body(a_ref, o_ref):
        o_ref[...] = a_ref[...].astype(BF16)

    spec = pl.BlockSpec((tr, c), lambda i: (i, 0))
    return pl.pallas_call(body, name=name, grid=(r // tr,), in_specs=[spec], out_specs=spec,
                          out_shape=jax.ShapeDtypeStruct((r, c), BF16), compiler_params=_params(("parallel",)))(a)


def _adamw(slabs, w, m, v, *, name):
    layers, r, c = w.shape
    tr = min(r, 256)

    def body(*refs):
        s_refs = refs[:layers]
        w_ref, m_ref, v_ref, g_ref, d_ref, mo_ref, vo_ref = refs[layers:]
        for l in range(layers):
            g = s_refs[l][0].astype(F32)
            for i in range(1, N_DEV):
                g = g + s_refs[l][i].astype(F32)
            m2 = ADAM_B1 * m_ref[l] + (1.0 - ADAM_B1) * g
            v2 = ADAM_B2 * v_ref[l] + (1.0 - ADAM_B2) * (g * g)
            m_hat = m2 / (1.0 - ADAM_B1 ** ADAM_STEP)
            v_hat = v2 / (1.0 - ADAM_B2 ** ADAM_STEP)
            g_ref[l] = g
            d_ref[l] = -ADAM_LR * (m_hat / (jnp.sqrt(v_hat) + ADAM_EPS) + ADAM_WD * w_ref[l])
            mo_ref[l] = m2
            vo_ref[l] = v2

    spec = pl.BlockSpec((layers, tr, c), lambda i: (0, i, 0))
    return pl.pallas_call(
        body, name=name, grid=(r // tr,),
        in_specs=[pl.BlockSpec((N_DEV, tr, c), lambda i: (0, i, 0))] * layers + [spec, spec, spec],
        out_specs=[spec] * 4, out_shape=[jax.ShapeDtypeStruct((layers, r, c), F32)] * 4,
        compiler_params=_params(("parallel",)),
    )(*slabs, w, m, v)


GATHER_AXIS = {"attn_w_in": 1, "attn_w_out": 0, "ffn_w_up0": 1, "ffn_w_down0": 0, "hgrn_w_in": 1, "hgrn_w_out": 0,
               "hgrn_norm_g": 1, "ffn_w_up1": 1, "ffn_w_down1": 0}
GATHER_STAGES = (("attn_w_in",), ("attn_w_out", "ffn_w_up0", "ffn_w_down0", "hgrn_w_in", "hgrn_w_out", "hgrn_norm_g"),
                 ("ffn_w_up1", "ffn_w_down1"))
GATHER_WAITS = ((("attn_w_in",), 0, 1), (("attn_w_out", "ffn_w_up0", "ffn_w_down0"), 1, 2),
                (("hgrn_w_in", "hgrn_w_out", "hgrn_norm_g"), 1, None), (("ffn_w_up1", "ffn_w_down1"), 2, None))
SCATTER_AXIS = dict(GATHER_AXIS, small=None)
BIG = ("attn_w_in", "attn_w_out", "hgrn_w_in", "hgrn_w_out", "ffn_w_up", "ffn_w_down")
SMALL = ("lb_logits", "ln_mix_g", "ln_mix_b", "ln_ffn_g", "ln_ffn_b")
SMALL_ROW = {"ln_mix_g": 0, "ln_mix_b": 2, "ln_ffn_g": 4, "ln_ffn_b": 6, "lb_logits": 8}
NORM_G_ROW = 10
LOSS_ROW = 11


def kernel(x, attn_w_in, attn_w_out, hgrn_w_in, hgrn_w_out, hgrn_norm_g, lb_logits, ln_mix_g, ln_mix_b, ln_ffn_g, ln_ffn_b, ffn_w_up, ffn_w_down, loss_target, m_attn_w_in, m_attn_w_out, m_hgrn_w_in, m_hgrn_w_out, m_hgrn_norm_g, m_lb_logits, m_ln_mix_g, m_ln_mix_b, m_ln_ffn_g, m_ln_ffn_b, m_ffn_w_up, m_ffn_w_down, v_attn_w_in, v_attn_w_out, v_hgrn_w_in, v_hgrn_w_out, v_hgrn_norm_g, v_lb_logits, v_ln_mix_g, v_ln_mix_b, v_ln_ffn_g, v_ln_ffn_b, v_ffn_w_up, v_ffn_w_down):
    wts = dict(attn_w_in=attn_w_in, attn_w_out=attn_w_out, hgrn_w_in=hgrn_w_in, hgrn_w_out=hgrn_w_out,
               hgrn_norm_g=hgrn_norm_g, lb_logits=lb_logits, ln_mix_g=ln_mix_g, ln_mix_b=ln_mix_b, ln_ffn_g=ln_ffn_g,
               ln_ffn_b=ln_ffn_b, ffn_w_up=ffn_w_up, ffn_w_down=ffn_w_down)
    mom = dict(attn_w_in=m_attn_w_in, attn_w_out=m_attn_w_out, hgrn_w_in=m_hgrn_w_in, hgrn_w_out=m_hgrn_w_out,
               hgrn_norm_g=m_hgrn_norm_g, lb_logits=m_lb_logits, ln_mix_g=m_ln_mix_g, ln_mix_b=m_ln_mix_b,
               ln_ffn_g=m_ln_ffn_g, ln_ffn_b=m_ln_ffn_b, ffn_w_up=m_ffn_w_up, ffn_w_down=m_ffn_w_down)
    vel = dict(attn_w_in=v_attn_w_in, attn_w_out=v_attn_w_out, hgrn_w_in=v_hgrn_w_in, hgrn_w_out=v_hgrn_w_out,
               hgrn_norm_g=v_hgrn_norm_g, lb_logits=v_lb_logits, ln_mix_g=v_ln_mix_g, ln_mix_b=v_ln_mix_b,
               ln_ffn_g=v_ln_ffn_g, ln_ffn_b=v_ln_ffn_b, ffn_w_up=v_ffn_w_up, ffn_w_down=v_ffn_w_down)
    me = 4 * lax.axis_index("x") + 2 * lax.axis_index("y") + lax.axis_index("c")

    src = {"attn_w_in": attn_w_in[0], "attn_w_out": attn_w_out[0], "hgrn_w_in": hgrn_w_in[0], "hgrn_w_out": hgrn_w_out[0],
           "ffn_w_up0": ffn_w_up[0], "ffn_w_down0": ffn_w_down[0], "ffn_w_up1": ffn_w_up[1], "ffn_w_down1": ffn_w_down[1]}
    gathers, got = {}, {}
    casts = {nm: _cast_bf16(a, name=f"cast_{nm}") for nm, a in src.items()}
    casts["hgrn_norm_g"] = hgrn_norm_g

    def start_gather(stage, deps):
        shards, lands = [], []
        for nm in GATHER_STAGES[stage]:
            sh = casts[nm]
            ax = GATHER_AXIS[nm]
            shape = list(sh.shape)
            shape[ax] *= N_DEV
            shards.append(sh)
            lands.append(lax.empty(tuple(shape), sh.dtype))
        peers = (SIBLING,) + SAME_CORE if stage == 0 else tuple(range(1, N_DEV))
        gathers[stage] = _xchg_start(shards, lands, [GATHER_AXIS[nm] for nm in GATHER_STAGES[stage]], gather=True,
                                     name=f"gather_start_{stage}", deps=deps, peers=peers)
        return [gathers[stage]["token"]]

    def get_w(name, after):
        deps = []
        if name not in got:
            group, stage, then = [w for w in GATHER_WAITS if name in w[0]][0]
            xc = gathers[stage]
            if stage == 0:
                later = [casts[nm] for st in GATHER_STAGES[1:] for nm in st if nm != "hgrn_norm_g"]
                relay = _gather_relay(xc, [xc["token"], *after, *later], name="gather_relay")
                res = [_gather_relay_wait(xc, relay, xc["token"], name=f"gather_wait_{group[0]}")]
            else:
                res = _xchg_wait(xc, [GATHER_STAGES[stage].index(nm) for nm in group], [after],
                                 name=f"gather_wait_{group[0]}")
            got.update(zip(group, res))
            if then is not None:
                deps = start_gather(then, [res[0]])
        return got[name], deps

    first = start_gather(0, [])

    scattered, held = {}, {}

    def on_grads(tag, grads):
        if tag in ("ffn1", "ffn0"):
            held.update(grads)
            return []
        grads = {**held, **grads}
        held.clear()
        gnames = list(grads)
        axes = [SCATTER_AXIS[nm] for nm in gnames]
        stacks = []
        for nm, ax in zip(gnames, axes):
            shape = list(grads[nm].shape)
            if ax is not None:
                shape[ax] //= N_DEV
            stacks.append(lax.empty((N_DEV, *shape), grads[nm].dtype))
        scattered[tag] = (gnames, _xchg_start([grads[nm] for nm in gnames], stacks, axes, gather=False,
                                              name=f"scatter_start_{tag}"))
        return [scattered[tag][1]["token"]]

    sm = jax.nn.softmax(lb_logits, axis=0)
    csum = jnp.cumsum(sm, axis=0)
    small = dict(lb=(csum - csum[0:1])[1:2], lb_logits=lb_logits, ln_mix_g=ln_mix_g, ln_mix_b=ln_mix_b,
                 ln_ffn_g=ln_ffn_g, ln_ffn_b=ln_ffn_b)
    grad_x = _local_step(x[0], loss_target[0], get_w, small, on_grads, deps=first)
    out = {}

    def stack_small(src_):
        rows = [None] * SMALL_ROWS
        for name in SMALL:
            rows[SMALL_ROW[name]], rows[SMALL_ROW[name] + 1] = src_[name][0:1], src_[name][1:2]
        zero = jnp.zeros((1, x.shape[-1]), F32)
        return jnp.concatenate([zero if r is None else r for r in rows], axis=0)[None]

    def update(name, slabs):
        shape = wts[name].shape
        out[name] = [r.reshape(shape) for r in _adamw(slabs, wts[name], mom[name], vel[name], name=f"adamw_{name}")]
        return out[name][0]

    slabs, after = {}, [grad_x]
    for tag, (gnames, xc) in scattered.items():
        slabs.update(zip(gnames, _xchg_wait(xc, list(range(len(gnames))), after, name=f"scatter_wait_{tag}")))
        if tag == "hgrn":
            after = [update("hgrn_w_out", [slabs["hgrn_w_out"]]), update("hgrn_w_in", [slabs["hgrn_w_in"]])]
        elif tag == "attn_out":
            after = [update("ffn_w_down", [slabs["ffn_w_down0"], slabs["ffn_w_down1"]]),
                     update("ffn_w_up", [slabs["ffn_w_up0"], slabs["ffn_w_up1"]]),
                     update("attn_w_out", [slabs["attn_w_out"]])]
        else:
            update("attn_w_in", [slabs["attn_w_in"]])
    res = _adamw([slabs["small"]], stack_small(wts), stack_small(mom), stack_small(vel), name="adamw_small")
    for name in SMALL:
        out[name] = [r[0, SMALL_ROW[name]:SMALL_ROW[name] + 2] for r in res]
    loss = res[0][0, LOSS_ROW, 0]
    ng = hgrn_norm_g.shape[-1]
    ng_slabs = lax.dynamic_slice(slabs["small"], (0, NORM_G_ROW, me * ng), (N_DEV, 1, ng))
    out["hgrn_norm_g"] = [r[0] for r in _adamw([ng_slabs], hgrn_norm_g[None], m_hgrn_norm_g[None],
                                                v_hgrn_norm_g[None], name="adamw_norm_g")]
    order =("attn_w_in", "attn_w_out", "hgrn_w_in", "hgrn_w_out", "hgrn_norm_g", "lb_logits", "ln_mix_g", "ln_mix_b",
             "ln_ffn_g", "ln_ffn_b", "ffn_w_up", "ffn_w_down")
    return (loss, grad_x[None], *[out[nm][0] for nm in order], *[out[nm][1] for nm in order],
            *[out[nm][2] for nm in order], *[out[nm][3] for nm in order])
```
